```python
import jax, jax.numpy as jnp
from jax import lax
import numpy as np

D_MODEL = 2048
BATCH = 8
SEQ = 2048
DEPTH = 1

HEAD_DIM = 64
FOX_HEADS = D_MODEL // (2 * HEAD_DIM)
SWA_Q_HEADS = D_MODEL // (2 * HEAD_DIM)
SWA_KV_HEADS = SWA_Q_HEADS // 4
D_MIX = (FOX_HEADS + SWA_Q_HEADS) * HEAD_DIM
WINDOW = 128
SWA_BLOCK = WINDOW
Q_BLOCK = 128
D_FF = 5632
ROPE_THETA = 10000.0
EPS = 1e-6

SPLIT_SIZES = (
    FOX_HEADS * HEAD_DIM,
    FOX_HEADS * HEAD_DIM,
    FOX_HEADS * HEAD_DIM,
    FOX_HEADS,
    SWA_Q_HEADS * HEAD_DIM,
    SWA_KV_HEADS * HEAD_DIM,
    SWA_KV_HEADS * HEAD_DIM,
)
D_IN_PROJ = sum(SPLIT_SIZES)
SPLIT_POINTS = tuple(int(v) for v in np.cumsum(SPLIT_SIZES)[:-1])

kernel_name = "hybrid_fox_swa_sink_macaron"


def rms_norm(x, g):
    xf = x.astype(jnp.float32)
    y = xf * lax.rsqrt(jnp.mean(xf * xf, axis=-1, keepdims=True) + EPS)
    return (y * g.astype(jnp.float32)).astype(x.dtype)


def swiglu(x, w_gate, w_up, w_down):
    return (jax.nn.silu(x @ w_gate) * (x @ w_up)) @ w_down


def rope(x, positions):
    d = x.shape[-1]
    inv_freq = ROPE_THETA ** (-jnp.arange(0, d, 2, dtype=jnp.float32) / d)
    ang = positions.astype(jnp.float32)[..., None] * inv_freq
    cos = jnp.cos(ang)[:, :, None, :]
    sin = jnp.sin(ang)[:, :, None, :]
    x1, x2 = jnp.split(x.astype(jnp.float32), 2, axis=-1)
    out = jnp.concatenate([x1 * cos - x2 * sin, x2 * cos + x1 * sin], axis=-1)
    return out.astype(x.dtype)


def forgetting_attention(q, k, v, log_f):
    B, S, H, d = q.shape
    nb = S // Q_BLOCK
    c = jnp.cumsum(log_f, axis=1).transpose(0, 2, 1)
    qh = q.transpose(0, 2, 1, 3)
    kh = k.transpose(0, 2, 1, 3)
    vh = v.transpose(0, 2, 1, 3)
    qb = qh.reshape(B, H, nb, Q_BLOCK, d).transpose(2, 0, 1, 3, 4)
    cb = c.reshape(B, H, nb, Q_BLOCK).transpose(2, 0, 1, 3)
    q_pos = jnp.arange(S).reshape(nb, Q_BLOCK)
    k_pos = jnp.arange(S)
    scale = d ** -0.5

    def block(args):
        q_i, c_i, p_i = args
        s = jnp.einsum('bhqd,bhkd->bhqk', q_i, kh).astype(jnp.float32) * scale
        s = s + c_i[..., None] - c[:, :, None, :]
        causal = p_i[:, None] >= k_pos[None, :]
        s = jnp.where(causal, s, -jnp.inf)
        p = jax.nn.softmax(s, axis=-1).astype(vh.dtype)
        return jnp.einsum('bhqk,bhkd->bhqd', p, vh)

    o = lax.map(block, (qb, cb, q_pos))
    return o.transpose(1, 0, 3, 2, 4).reshape(B, S, H * d)


def sliding_window_sink_attention(q, k, v, sinks):
    B, S, Hq, d = q.shape
    Hk = k.shape[2]
    G = Hq // Hk
    W = SWA_BLOCK
    nb = S // W
    qb = q.reshape(B, nb, W, Hk, G, d)

    def with_prev(t):
        t = t.reshape(B, nb, W, Hk, d)
        prev = jnp.pad(t, ((0, 0), (1, 0), (0, 0), (0, 0), (0, 0)))[:, :-1]
        return jnp.concatenate([prev, t], axis=2)

    kw = with_prev(k)
    vw = with_prev(v)
    s = jnp.einsum('bnqhgd,bnkhd->bnhgqk', qb, kw).astype(jnp.float32) * (d ** -0.5)
    blk = jnp.arange(nb)[:, None]
    q_abs = blk * W + jnp.arange(W)[None, :]
    k_abs = blk * W - W + jnp.arange(2 * W)[None, :]
    rel = q_abs[:, :, None] - k_abs[:, None, :]
    band = (rel >= 0) & (rel < WINDOW) & (k_abs[:, None, :] >= 0)
    s = jnp.where(band[None, :, None, None], s, -jnp.inf)
    sink = jnp.broadcast_to(sinks.astype(jnp.float32).reshape(1, 1, Hk, G, 1, 1), s.shape[:-1] + (1,))
    p = jax.nn.softmax(jnp.concatenate([s, sink], axis=-1), axis=-1)[..., :-1].astype(v.dtype)
    o = jnp.einsum('bnhgqk,bnkhd->bnqhgd', p, vw)
    return o.reshape(B, S, Hq * d)


def _fwd_setup_inputs(seed: int = 0) -> dict:
    key = jax.random.key(seed)
    ks = jax.random.split(key, 24)
    f32 = jnp.float32

    def w(k, shape, fan_in):
        return jax.random.normal(k, shape, f32) * (fan_in ** -0.5)

    def gain(k, shape):
        return 1.0 + 0.02 * jax.random.normal(k, shape, f32)

    x = jax.random.normal(ks[0], (BATCH, SEQ, D_MODEL), f32)
    positions = jnp.broadcast_to(jnp.arange(SEQ, dtype=jnp.int32), (BATCH, SEQ))
    return {
        "x": x,
        "positions": positions,
        "norm_ffn1_g": gain(ks[1], (DEPTH, D_MODEL)),
        "ffn1_w_gate": w(ks[2], (DEPTH, D_MODEL, D_FF), D_MODEL),
        "ffn1_w_up": w(ks[3], (DEPTH, D_MODEL, D_FF), D_MODEL),
        "ffn1_w_down": w(ks[4], (DEPTH, D_FF, D_MODEL), D_FF),
        "norm_mix_g": gain(ks[5], (DEPTH, D_MODEL)),
        "w_in": w(ks[6], (DEPTH, D_MODEL, D_IN_PROJ), D_MODEL),
        "b_forget": 0.1 * jax.random.normal(ks[7], (DEPTH, FOX_HEADS), f32),
        "fox_q_norm_g": gain(ks[8], (DEPTH, HEAD_DIM)),
        "fox_k_norm_g": gain(ks[9], (DEPTH, HEAD_DIM)),
        "swa_q_norm_g": gain(ks[10], (DEPTH, HEAD_DIM)),
        "swa_k_norm_g": gain(ks[11], (DEPTH, HEAD_DIM)),
        "swa_sinks": 0.5 * jax.random.normal(ks[12], (DEPTH, SWA_Q_HEADS), f32),
        "out_norm_fox_g": gain(ks[13], (DEPTH, FOX_HEADS * HEAD_DIM)),
        "out_norm_swa_g": gain(ks[14], (DEPTH, SWA_Q_HEADS * HEAD_DIM)),
        "w_out": w(ks[15], (DEPTH, D_MIX, D_MODEL), D_MIX),
        "norm_ffn2_g": gain(ks[16], (DEPTH, D_MODEL)),
        "ffn2_w_gate": w(ks[17], (DEPTH, D_MODEL, D_FF), D_MODEL),
        "ffn2_w_up": w(ks[18], (DEPTH, D_MODEL, D_FF), D_MODEL),
        "ffn2_w_down": w(ks[19], (DEPTH, D_FF, D_MODEL), D_FF),
    }


def _fwd_reference(x, positions, norm_ffn1_g, ffn1_w_gate, ffn1_w_up, ffn1_w_down, norm_mix_g, w_in,
              b_forget, fox_q_norm_g, fox_k_norm_g, swa_q_norm_g, swa_k_norm_g, swa_sinks,
              out_norm_fox_g, out_norm_swa_g, w_out, norm_ffn2_g, ffn2_w_gate, ffn2_w_up, ffn2_w_down):
    B, S, _ = x.shape
    h = x
    for l in range(DEPTH):
        h = h + 0.5 * swiglu(rms_norm(h, norm_ffn1_g[l]), ffn1_w_gate[l], ffn1_w_up[l], ffn1_w_down[l])

        u = rms_norm(h, norm_mix_g[l])
        proj = u @ w_in[l]
        q_f, k_f, v_f, f_logit, q_s, k_s, v_s = jnp.split(proj, SPLIT_POINTS, axis=-1)

        q_f = rms_norm(q_f.reshape(B, S, FOX_HEADS, HEAD_DIM), fox_q_norm_g[l])
        k_f = rms_norm(k_f.reshape(B, S, FOX_HEADS, HEAD_DIM), fox_k_norm_g[l])
        v_f = v_f.reshape(B, S, FOX_HEADS, HEAD_DIM)
        log_f = jax.nn.log_sigmoid((f_logit + b_forget[l]).astype(jnp.float32))
        o_fox = forgetting_attention(q_f, k_f, v_f, log_f)

        q_s = rope(rms_norm(q_s.reshape(B, S, SWA_Q_HEADS, HEAD_DIM), swa_q_norm_g[l]), positions)
        k_s = rope(rms_norm(k_s.reshape(B, S, SWA_KV_HEADS, HEAD_DIM), swa_k_norm_g[l]), positions)
        v_s = v_s.reshape(B, S, SWA_KV_HEADS, HEAD_DIM)
        o_swa = sliding_window_sink_attention(q_s, k_s, v_s, swa_sinks[l])

        o = jnp.concatenate([rms_norm(o_fox, out_norm_fox_g[l]), rms_norm(o_swa, out_norm_swa_g[l])], axis=-1)
        h = h + o @ w_out[l]

        h = h + 0.5 * swiglu(rms_norm(h, norm_ffn2_g[l]), ffn2_w_gate[l], ffn2_w_up[l], ffn2_w_down[l])
    return h


import jax as _jax
import jax.numpy as _jnp

TWIN_FORMAT = 'train_step'
FWD_PARAMS = ['x', 'positions', 'norm_ffn1_g', 'ffn1_w_gate', 'ffn1_w_up', 'ffn1_w_down', 'norm_mix_g', 'w_in', 'b_forget', 'fox_q_norm_g', 'fox_k_norm_g', 'swa_q_norm_g', 'swa_k_norm_g', 'swa_sinks', 'out_norm_fox_g', 'out_norm_swa_g', 'w_out', 'norm_ffn2_g', 'ffn2_w_gate', 'ffn2_w_up', 'ffn2_w_down']
TWIN_WEIGHTS = ['norm_ffn1_g', 'ffn1_w_gate', 'ffn1_w_up', 'ffn1_w_down', 'norm_mix_g', 'w_in', 'b_forget', 'fox_q_norm_g', 'fox_k_norm_g', 'swa_q_norm_g', 'swa_k_norm_g', 'swa_sinks', 'out_norm_fox_g', 'out_norm_swa_g', 'w_out', 'norm_ffn2_g', 'ffn2_w_gate', 'ffn2_w_up', 'ffn2_w_down']
TWIN_DIFF_INPUT = 'x'
TWIN_INPUTS = ['x', 'positions', 'norm_ffn1_g', 'ffn1_w_gate', 'ffn1_w_up', 'ffn1_w_down', 'norm_mix_g', 'w_in', 'b_forget', 'fox_q_norm_g', 'fox_k_norm_g', 'swa_q_norm_g', 'swa_k_norm_g', 'swa_sinks', 'out_norm_fox_g', 'out_norm_swa_g', 'w_out', 'norm_ffn2_g', 'ffn2_w_gate', 'ffn2_w_up', 'ffn2_w_down', 'loss_target', 'm_norm_ffn1_g', 'm_ffn1_w_gate', 'm_ffn1_w_up', 'm_ffn1_w_down', 'm_norm_mix_g', 'm_w_in', 'm_b_forget', 'm_fox_q_norm_g', 'm_fox_k_norm_g', 'm_swa_q_norm_g', 'm_swa_k_norm_g', 'm_swa_sinks', 'm_out_norm_fox_g', 'm_out_norm_swa_g', 'm_w_out', 'm_norm_ffn2_g', 'm_ffn2_w_gate', 'm_ffn2_w_up', 'm_ffn2_w_down', 'v_norm_ffn1_g', 'v_ffn1_w_gate', 'v_ffn1_w_up', 'v_ffn1_w_down', 'v_norm_mix_g', 'v_w_in', 'v_b_forget', 'v_fox_q_norm_g', 'v_fox_k_norm_g', 'v_swa_q_norm_g', 'v_swa_k_norm_g', 'v_swa_sinks', 'v_out_norm_fox_g', 'v_out_norm_swa_g', 'v_w_out', 'v_norm_ffn2_g', 'v_ffn2_w_gate', 'v_ffn2_w_up', 'v_ffn2_w_down']
TWIN_OUTPUTS = ['loss', 'grad_x', 'grad_norm_ffn1_g', 'grad_ffn1_w_gate', 'grad_ffn1_w_up', 'grad_ffn1_w_down', 'grad_norm_mix_g', 'grad_w_in', 'grad_b_forget', 'grad_fox_q_norm_g', 'grad_fox_k_norm_g', 'grad_swa_q_norm_g', 'grad_swa_k_norm_g', 'grad_swa_sinks', 'grad_out_norm_fox_g', 'grad_out_norm_swa_g', 'grad_w_out', 'grad_norm_ffn2_g', 'grad_ffn2_w_gate', 'grad_ffn2_w_up', 'grad_ffn2_w_down', 'delta_norm_ffn1_g', 'delta_ffn1_w_gate', 'delta_ffn1_w_up', 'delta_ffn1_w_down', 'delta_norm_mix_g', 'delta_w_in', 'delta_b_forget', 'delta_fox_q_norm_g', 'delta_fox_k_norm_g', 'delta_swa_q_norm_g', 'delta_swa_k_norm_g', 'delta_swa_sinks', 'delta_out_norm_fox_g', 'delta_out_norm_swa_g', 'delta_w_out', 'delta_norm_ffn2_g', 'delta_ffn2_w_gate', 'delta_ffn2_w_up', 'delta_ffn2_w_down', 'new_m_norm_ffn1_g', 'new_m_ffn1_w_gate', 'new_m_ffn1_w_up', 'new_m_ffn1_w_down', 'new_m_norm_mix_g', 'new_m_w_in', 'new_m_b_forget', 'new_m_fox_q_norm_g', 'new_m_fox_k_norm_g', 'new_m_swa_q_norm_g', 'new_m_swa_k_norm_g', 'new_m_swa_sinks', 'new_m_out_norm_fox_g', 'new_m_out_norm_swa_g', 'new_m_w_out', 'new_m_norm_ffn2_g', 'new_m_ffn2_w_gate', 'new_m_ffn2_w_up', 'new_m_ffn2_w_down', 'new_v_norm_ffn1_g', 'new_v_ffn1_w_gate', 'new_v_ffn1_w_up', 'new_v_ffn1_w_down', 'new_v_norm_mix_g', 'new_v_w_in', 'new_v_b_forget', 'new_v_fox_q_norm_g', 'new_v_fox_k_norm_g', 'new_v_swa_q_norm_g', 'new_v_swa_k_norm_g', 'new_v_swa_sinks', 'new_v_out_norm_fox_g', 'new_v_out_norm_swa_g', 'new_v_w_out', 'new_v_norm_ffn2_g', 'new_v_ffn2_w_gate', 'new_v_ffn2_w_up', 'new_v_ffn2_w_down']
TWIN_LEAF_KINDS = {'loss': 'loss', 'grad_x': 'grad_x', 'grad_norm_ffn1_g': 'grad_w', 'grad_ffn1_w_gate': 'grad_w', 'grad_ffn1_w_up': 'grad_w', 'grad_ffn1_w_down': 'grad_w', 'grad_norm_mix_g': 'grad_w', 'grad_w_in': 'grad_w', 'grad_b_forget': 'grad_w', 'grad_fox_q_norm_g': 'grad_w', 'grad_fox_k_norm_g': 'grad_w', 'grad_swa_q_norm_g': 'grad_w', 'grad_swa_k_norm_g': 'grad_w', 'grad_swa_sinks': 'grad_w', 'grad_out_norm_fox_g': 'grad_w', 'grad_out_norm_swa_g': 'grad_w', 'grad_w_out': 'grad_w', 'grad_norm_ffn2_g': 'grad_w', 'grad_ffn2_w_gate': 'grad_w', 'grad_ffn2_w_up': 'grad_w', 'grad_ffn2_w_down': 'grad_w', 'delta_norm_ffn1_g': 'delta_w', 'delta_ffn1_w_gate': 'delta_w', 'delta_ffn1_w_up': 'delta_w', 'delta_ffn1_w_down': 'delta_w', 'delta_norm_mix_g': 'delta_w', 'delta_w_in': 'delta_w', 'delta_b_forget': 'delta_w', 'delta_fox_q_norm_g': 'delta_w', 'delta_fox_k_norm_g': 'delta_w', 'delta_swa_q_norm_g': 'delta_w', 'delta_swa_k_norm_g': 'delta_w', 'delta_swa_sinks': 'delta_w', 'delta_out_norm_fox_g': 'delta_w', 'delta_out_norm_swa_g': 'delta_w', 'delta_w_out': 'delta_w', 'delta_norm_ffn2_g': 'delta_w', 'delta_ffn2_w_gate': 'delta_w', 'delta_ffn2_w_up': 'delta_w', 'delta_ffn2_w_down': 'delta_w', 'new_m_norm_ffn1_g': 'new_m', 'new_m_ffn1_w_gate': 'new_m', 'new_m_ffn1_w_up': 'new_m', 'new_m_ffn1_w_down': 'new_m', 'new_m_norm_mix_g': 'new_m', 'new_m_w_in': 'new_m', 'new_m_b_forget': 'new_m', 'new_m_fox_q_norm_g': 'new_m', 'new_m_fox_k_norm_g': 'new_m', 'new_m_swa_q_norm_g': 'new_m', 'new_m_swa_k_norm_g': 'new_m', 'new_m_swa_sinks': 'new_m', 'new_m_out_norm_fox_g': 'new_m', 'new_m_out_norm_swa_g': 'new_m', 'new_m_w_out': 'new_m', 'new_m_norm_ffn2_g': 'new_m', 'new_m_ffn2_w_gate': 'new_m', 'new_m_ffn2_w_up': 'new_m', 'new_m_ffn2_w_down': 'new_m', 'new_v_norm_ffn1_g': 'new_v', 'new_v_ffn1_w_gate': 'new_v', 'new_v_ffn1_w_up': 'new_v', 'new_v_ffn1_w_down': 'new_v', 'new_v_norm_mix_g': 'new_v', 'new_v_w_in': 'new_v', 'new_v_b_forget': 'new_v', 'new_v_fox_q_norm_g': 'new_v', 'new_v_fox_k_norm_g': 'new_v', 'new_v_swa_q_norm_g': 'new_v', 'new_v_swa_k_norm_g': 'new_v', 'new_v_swa_sinks': 'new_v', 'new_v_out_norm_fox_g': 'new_v', 'new_v_out_norm_swa_g': 'new_v', 'new_v_w_out': 'new_v', 'new_v_norm_ffn2_g': 'new_v', 'new_v_ffn2_w_gate': 'new_v', 'new_v_ffn2_w_up': 'new_v', 'new_v_ffn2_w_down': 'new_v'}


def _forward(args):
    return _fwd_reference(*[args[k] for k in FWD_PARAMS])


def _output_shape():
    out = _jax.eval_shape(lambda: _forward(_fwd_setup_inputs(0)))
    return out.shape, out.dtype

N_MICROBATCH = 1
ADAM_LR = 0.001
ADAM_B1 = 0.9
ADAM_B2 = 0.999
ADAM_EPS = 1e-08
ADAM_WD = 0.01
ADAM_STEP = 10
PER_EXAMPLE_BATCH_AXIS = {'x': 0, 'positions': 0, 'loss_target': 0}
SHARED_INPUTS = []
_WEIGHT_DTYPES = {'norm_ffn1_g': _jnp.float32, 'ffn1_w_gate': _jnp.float32, 'ffn1_w_up': _jnp.float32, 'ffn1_w_down': _jnp.float32, 'norm_mix_g': _jnp.float32, 'w_in': _jnp.float32, 'b_forget': _jnp.float32, 'fox_q_norm_g': _jnp.float32, 'fox_k_norm_g': _jnp.float32, 'swa_q_norm_g': _jnp.float32, 'swa_k_norm_g': _jnp.float32, 'swa_sinks': _jnp.float32, 'out_norm_fox_g': _jnp.float32, 'out_norm_swa_g': _jnp.float32, 'w_out': _jnp.float32, 'norm_ffn2_g': _jnp.float32, 'ffn2_w_gate': _jnp.float32, 'ffn2_w_up': _jnp.float32, 'ffn2_w_down': _jnp.float32}
MOMENT_SCALE = {'norm_ffn1_g': 1.528108e+00, 'ffn1_w_gate': 4.472882e-02, 'ffn1_w_up': 4.423877e-02, 'ffn1_w_down': 7.280558e-02, 'norm_mix_g': 2.493457e-01, 'w_in': 1.518286e-01, 'b_forget': 8.127808e-01, 'fox_q_norm_g': 2.591389e-01, 'fox_k_norm_g': 2.569179e-01, 'swa_q_norm_g': 4.914613e-01, 'swa_k_norm_g': 5.174118e-01, 'swa_sinks': 5.653165e-02, 'out_norm_fox_g': 8.017167e+00, 'out_norm_swa_g': 7.968308e+00, 'w_out': 2.391063e-01, 'norm_ffn2_g': 1.548393e+00, 'ffn2_w_gate': 2.374844e-02, 'ffn2_w_up': 2.623667e-02, 'ffn2_w_down': 4.223901e-02}


def _to_microbatches(a, axis):
    t = _jnp.moveaxis(a, axis, 0)
    t = t.reshape((N_MICROBATCH, t.shape[0] // N_MICROBATCH) + t.shape[1:])
    return _jnp.moveaxis(t, 1, axis + 1)


def setup_inputs(seed: int = 0) -> dict:
    inp = _fwd_setup_inputs(seed)
    key = _jax.random.fold_in(_jax.random.key(seed), 7919)
    shape, _ = _output_shape()
    out = dict(inp)
    out["loss_target"] = _jax.random.normal(_jax.random.fold_in(key, 0), shape, _jnp.float32)
    for i, name in enumerate(TWIN_WEIGHTS):
        w = inp[name].astype(_jnp.float32)
        if MOMENT_SCALE is None:
            s = _jnp.sqrt(_jnp.mean(_jnp.square(w)) + 1e-30)
        else:
            s = MOMENT_SCALE[name]
        km, kv = _jax.random.split(_jax.random.fold_in(key, i + 1))
        out[name] = w
        out["m_" + name] = s * _jax.random.normal(km, w.shape, _jnp.float32)
        out["v_" + name] = (s * s) * _jax.random.uniform(kv, w.shape, _jnp.float32, 0.5, 1.5)
    if N_MICROBATCH > 1:
        for name, axis in PER_EXAMPLE_BATCH_AXIS.items():
            out[name] = _to_microbatches(out[name], axis)
    return {'x': out['x'], 'positions': out['positions'], 'norm_ffn1_g': out['norm_ffn1_g'], 'ffn1_w_gate': out['ffn1_w_gate'], 'ffn1_w_up': out['ffn1_w_up'], 'ffn1_w_down': out['ffn1_w_down'], 'norm_mix_g': out['norm_mix_g'], 'w_in': out['w_in'], 'b_forget': out['b_forget'], 'fox_q_norm_g': out['fox_q_norm_g'], 'fox_k_norm_g': out['fox_k_norm_g'], 'swa_q_norm_g': out['swa_q_norm_g'], 'swa_k_norm_g': out['swa_k_norm_g'], 'swa_sinks': out['swa_sinks'], 'out_norm_fox_g': out['out_norm_fox_g'], 'out_norm_swa_g': out['out_norm_swa_g'], 'w_out': out['w_out'], 'norm_ffn2_g': out['norm_ffn2_g'], 'ffn2_w_gate': out['ffn2_w_gate'], 'ffn2_w_up': out['ffn2_w_up'], 'ffn2_w_down': out['ffn2_w_down'], 'loss_target': out['loss_target'], 'm_norm_ffn1_g': out['m_norm_ffn1_g'], 'm_ffn1_w_gate': out['m_ffn1_w_gate'], 'm_ffn1_w_up': out['m_ffn1_w_up'], 'm_ffn1_w_down': out['m_ffn1_w_down'], 'm_norm_mix_g': out['m_norm_mix_g'], 'm_w_in': out['m_w_in'], 'm_b_forget': out['m_b_forget'], 'm_fox_q_norm_g': out['m_fox_q_norm_g'], 'm_fox_k_norm_g': out['m_fox_k_norm_g'], 'm_swa_q_norm_g': out['m_swa_q_norm_g'], 'm_swa_k_norm_g': out['m_swa_k_norm_g'], 'm_swa_sinks': out['m_swa_sinks'], 'm_out_norm_fox_g': out['m_out_norm_fox_g'], 'm_out_norm_swa_g': out['m_out_norm_swa_g'], 'm_w_out': out['m_w_out'], 'm_norm_ffn2_g': out['m_norm_ffn2_g'], 'm_ffn2_w_gate': out['m_ffn2_w_gate'], 'm_ffn2_w_up': out['m_ffn2_w_up'], 'm_ffn2_w_down': out['m_ffn2_w_down'], 'v_norm_ffn1_g': out['v_norm_ffn1_g'], 'v_ffn1_w_gate': out['v_ffn1_w_gate'], 'v_ffn1_w_up': out['v_ffn1_w_up'], 'v_ffn1_w_down': out['v_ffn1_w_down'], 'v_norm_mix_g': out['v_norm_mix_g'], 'v_w_in': out['v_w_in'], 'v_b_forget': out['v_b_forget'], 'v_fox_q_norm_g': out['v_fox_q_norm_g'], 'v_fox_k_norm_g': out['v_fox_k_norm_g'], 'v_swa_q_norm_g': out['v_swa_q_norm_g'], 'v_swa_k_norm_g': out['v_swa_k_norm_g'], 'v_swa_sinks': out['v_swa_sinks'], 'v_out_norm_fox_g': out['v_out_norm_fox_g'], 'v_out_norm_swa_g': out['v_out_norm_swa_g'], 'v_w_out': out['v_w_out'], 'v_norm_ffn2_g': out['v_norm_ffn2_g'], 'v_ffn2_w_gate': out['v_ffn2_w_gate'], 'v_ffn2_w_up': out['v_ffn2_w_up'], 'v_ffn2_w_down': out['v_ffn2_w_down']}


def _loss(weights, diff, rest, loss_target):
    with _jax.named_scope("forward"):
        args = {**rest, TWIN_DIFF_INPUT: diff, **{k: w.astype(_WEIGHT_DTYPES[k]) for k, w in weights.items()}}
        y = _forward(args)
    with _jax.named_scope("loss_head"):
        err = _jnp.square(y.astype(_jnp.float32) - loss_target)
        return 0.5 * _jnp.sum(_jnp.mean(err, axis=-1)) if err.ndim else 0.5 * err


def _adamw(w, g, m, v):
    m = ADAM_B1 * m + (1.0 - ADAM_B1) * g
    v = ADAM_B2 * v + (1.0 - ADAM_B2) * _jnp.square(g)
    m_hat = m / (1.0 - ADAM_B1 ** ADAM_STEP)
    v_hat = v / (1.0 - ADAM_B2 ** ADAM_STEP)
    delta = -ADAM_LR * (m_hat / (_jnp.sqrt(v_hat) + ADAM_EPS) + ADAM_WD * w)
    return delta, m, v


def reference(x, positions, norm_ffn1_g, ffn1_w_gate, ffn1_w_up, ffn1_w_down, norm_mix_g, w_in, b_forget, fox_q_norm_g, fox_k_norm_g, swa_q_norm_g, swa_k_norm_g, swa_sinks, out_norm_fox_g, out_norm_swa_g, w_out, norm_ffn2_g, ffn2_w_gate, ffn2_w_up, ffn2_w_down, loss_target, m_norm_ffn1_g, m_ffn1_w_gate, m_ffn1_w_up, m_ffn1_w_down, m_norm_mix_g, m_w_in, m_b_forget, m_fox_q_norm_g, m_fox_k_norm_g, m_swa_q_norm_g, m_swa_k_norm_g, m_swa_sinks, m_out_norm_fox_g, m_out_norm_swa_g, m_w_out, m_norm_ffn2_g, m_ffn2_w_gate, m_ffn2_w_up, m_ffn2_w_down, v_norm_ffn1_g, v_ffn1_w_gate, v_ffn1_w_up, v_ffn1_w_down, v_norm_mix_g, v_w_in, v_b_forget, v_fox_q_norm_g, v_fox_k_norm_g, v_swa_q_norm_g, v_swa_k_norm_g, v_swa_sinks, v_out_norm_fox_g, v_out_norm_swa_g, v_w_out, v_norm_ffn2_g, v_ffn2_w_gate, v_ffn2_w_up, v_ffn2_w_down):
    given = dict(x=x, positions=positions, norm_ffn1_g=norm_ffn1_g, ffn1_w_gate=ffn1_w_gate, ffn1_w_up=ffn1_w_up, ffn1_w_down=ffn1_w_down, norm_mix_g=norm_mix_g, w_in=w_in, b_forget=b_forget, fox_q_norm_g=fox_q_norm_g, fox_k_norm_g=fox_k_norm_g, swa_q_norm_g=swa_q_norm_g, swa_k_norm_g=swa_k_norm_g, swa_sinks=swa_sinks, out_norm_fox_g=out_norm_fox_g, out_norm_swa_g=out_norm_swa_g, w_out=w_out, norm_ffn2_g=norm_ffn2_g, ffn2_w_gate=ffn2_w_gate, ffn2_w_up=ffn2_w_up, ffn2_w_down=ffn2_w_down, loss_target=loss_target, m_norm_ffn1_g=m_norm_ffn1_g, m_ffn1_w_gate=m_ffn1_w_gate, m_ffn1_w_up=m_ffn1_w_up, m_ffn1_w_down=m_ffn1_w_down, m_norm_mix_g=m_norm_mix_g, m_w_in=m_w_in, m_b_forget=m_b_forget, m_fox_q_norm_g=m_fox_q_norm_g, m_fox_k_norm_g=m_fox_k_norm_g, m_swa_q_norm_g=m_swa_q_norm_g, m_swa_k_norm_g=m_swa_k_norm_g, m_swa_sinks=m_swa_sinks, m_out_norm_fox_g=m_out_norm_fox_g, m_out_norm_swa_g=m_out_norm_swa_g, m_w_out=m_w_out, m_norm_ffn2_g=m_norm_ffn2_g, m_ffn2_w_gate=m_ffn2_w_gate, m_ffn2_w_up=m_ffn2_w_up, m_ffn2_w_down=m_ffn2_w_down, v_norm_ffn1_g=v_norm_ffn1_g, v_ffn1_w_gate=v_ffn1_w_gate, v_ffn1_w_up=v_ffn1_w_up, v_ffn1_w_down=v_ffn1_w_down, v_norm_mix_g=v_norm_mix_g, v_w_in=v_w_in, v_b_forget=v_b_forget, v_fox_q_norm_g=v_fox_q_norm_g, v_fox_k_norm_g=v_fox_k_norm_g, v_swa_q_norm_g=v_swa_q_norm_g, v_swa_k_norm_g=v_swa_k_norm_g, v_swa_sinks=v_swa_sinks, v_out_norm_fox_g=v_out_norm_fox_g, v_out_norm_swa_g=v_out_norm_swa_g, v_w_out=v_w_out, v_norm_ffn2_g=v_norm_ffn2_g, v_ffn2_w_gate=v_ffn2_w_gate, v_ffn2_w_up=v_ffn2_w_up, v_ffn2_w_down=v_ffn2_w_down)
    weights = {n: given[n] for n in TWIN_WEIGHTS}
    shared = {n: given[n] for n in SHARED_INPUTS}
    per_example = {n: given[n] for n in ['x', 'positions']}
    grad_fn = _jax.value_and_grad(_loss, argnums=(0, 1))

    def one_microbatch(ex, loss_target):
        ex = dict(ex)
        diff = ex.pop(TWIN_DIFF_INPUT)
        return grad_fn(weights, diff, {**shared, **ex}, loss_target)

    if N_MICROBATCH == 1:
        loss, (grad_w, grad_x) = one_microbatch(per_example, given["loss_target"])
    else:
        def body(carry, xs):
            loss_sum, grad_sum = carry
            l_k, (gw_k, gx_k) = one_microbatch(xs[0], xs[1])
            with _jax.named_scope("update"):
                return (loss_sum + l_k, _jax.tree.map(_jnp.add, grad_sum, gw_k)), gx_k

        init = (_jnp.zeros((), _jnp.float32), _jax.tree.map(_jnp.zeros_like, weights))
        (loss, grad_w), grad_x = _jax.lax.scan(body, init, (per_example, given["loss_target"]))
    with _jax.named_scope("update"):
        delta_w, new_m, new_v = {}, {}, {}
        for n in TWIN_WEIGHTS:
            delta_w[n], new_m[n], new_v[n] = _adamw(weights[n], grad_w[n], given["m_" + n], given["v_" + n])
    return (loss, grad_x, *[grad_w[n] for n in TWIN_WEIGHTS], *[delta_w[n] for n in TWIN_WEIGHTS],
            *[new_m[n] for n in TWIN_WEIGHTS], *[new_v[n] for n in TWIN_WEIGHTS])
```

```python
import functools

import jax
import jax.numpy as jnp
from jax import lax
from jax.experimental import pallas as pl
from jax.experimental.pallas import tpu as pltpu

F32 = jnp.float32
BF16 = jnp.bfloat16

HEAD_DIM = 64
LANES = 128
WINDOW = 128
GQA_GROUP = 4
EPS = 1e-6
ROPE_THETA = 10000.0
ADAM_LR = 0.001
ADAM_B1 = 0.9
ADAM_B2 = 0.999
ADAM_EPS = 1e-08
ADAM_WD = 0.01
ADAM_STEP = 10
N_DEV = 8
NEG = -1e30
VMEM_LIMIT_V7X = 48 * 1024 * 1024
ROW_TILE_CAP = 512
MESH = pl.DeviceIdType.MESH

NN = (((1,), (0,)), ((), ()))
NT = (((1,), (1,)), ((), ()))
TN = (((0,), (0,)), ((), ()))


def _dot(a, b, dims):
    return lax.dot_general(a, b, dims, preferred_element_type=F32)


def _params(n_axes):
    return pltpu.CompilerParams(dimension_semantics=("arbitrary",) * n_axes, vmem_limit_bytes=VMEM_LIMIT_V7X)


def _row_tile(rows, cap=ROW_TILE_CAP):
    best = None
    for t in range(16, min(rows, cap) + 1, 16):
        if rows % t == 0:
            best = t
    return best or rows


def _lane_lo(shape):
    return lax.broadcasted_iota(jnp.int32, shape, len(shape) - 1) < HEAD_DIM


def _keep(sel, x):
    return jnp.where(sel, x.astype(F32), 0.0).astype(BF16)


def _rms_bwd(dn, x, g):
    r = lax.rsqrt(jnp.mean(x * x, axis=-1, keepdims=True) + EPS)
    xh = x * r
    dxh = dn * g
    dx = r * (dxh - xh * jnp.mean(dxh * xh, axis=-1, keepdims=True))
    return dx, jnp.sum(dn * xh, axis=0, keepdims=True)


def _rmsnorm_fwd(name, x, g, tm):
    T, D = x.shape

    def body(x_ref, g_ref, o_ref):
        xf = x_ref[...]
        r = lax.rsqrt(jnp.mean(xf * xf, axis=-1, keepdims=True) + EPS)
        o_ref[...] = (xf * r * g_ref[...]).astype(BF16)

    return pl.pallas_call(
        body, out_shape=jax.ShapeDtypeStruct((T, D), BF16), grid=(T // tm,),
        in_specs=[pl.BlockSpec((tm, D), lambda i: (i, 0)), pl.BlockSpec((1, D), lambda i: (0, 0))],
        out_specs=pl.BlockSpec((tm, D), lambda i: (i, 0)), name=name, compiler_params=_params(1))(x, g)


def _outnorm_fwd(name, o_fox, o_swa, g_fox, g_swa, tm):
    T, Dh = o_fox.shape

    def body(a_ref, b_ref, ga_ref, gb_ref, o_ref):
        for ref, g_ref, lo in ((a_ref, ga_ref, 0), (b_ref, gb_ref, Dh)):
            xf = ref[...]
            r = lax.rsqrt(jnp.mean(xf * xf, axis=-1, keepdims=True) + EPS)
            o_ref[:, lo:lo + Dh] = (xf * r * g_ref[...]).astype(BF16)

    row = pl.BlockSpec((tm, Dh), lambda i: (i, 0))
    gain = pl.BlockSpec((1, Dh), lambda i: (0, 0))
    return pl.pallas_call(
        body, out_shape=jax.ShapeDtypeStruct((T, 2 * Dh), BF16), grid=(T // tm,),
        in_specs=[row, row, gain, gain], out_specs=pl.BlockSpec((tm, 2 * Dh), lambda i: (i, 0)),
        name=name, compiler_params=_params(1))(o_fox, o_swa, g_fox, g_swa)


def _outnorm_bwd(name, dhb, wout, half, o, g, tm):
    T, D = dhb.shape
    Dh = o.shape[1]

    def body(a_ref, w_ref, o_ref, g_ref, do_ref, dg_ref):
        don = _dot(a_ref[...], w_ref[...], NT)
        dx, dg = _rms_bwd(don, o_ref[...], g_ref[...])
        do_ref[...] = dx.astype(BF16)

        @pl.when(pl.program_id(0) == 0)
        def _():
            dg_ref[...] = dg

        @pl.when(pl.program_id(0) > 0)
        def _():
            dg_ref[...] += dg

    return pl.pallas_call(
        body, out_shape=(jax.ShapeDtypeStruct((T, Dh), BF16), jax.ShapeDtypeStruct((1, Dh), F32)), grid=(T // tm,),
        in_specs=[pl.BlockSpec((tm, D), lambda i: (i, 0)), pl.BlockSpec((Dh, D), lambda i: (half, 0)),
                  pl.BlockSpec((tm, Dh), lambda i: (i, 0)), pl.BlockSpec((1, Dh), lambda i: (0, 0))],
        out_specs=(pl.BlockSpec((tm, Dh), lambda i: (i, 0)), pl.BlockSpec((1, Dh), lambda i: (0, 0))),
        name=name, compiler_params=_params(1))(dhb, wout, o, g)


def _mm_nn(name, a, b, tm, tn, resid=None):
    M, K = a.shape
    N = b.shape[1]

    def body(*refs):
        if resid is None:
            a_ref, b_ref, o_ref = refs
            o_ref[...] = _dot(a_ref[...], b_ref[...], NN)
        else:
            a_ref, b_ref, r_ref, o_ref = refs
            o_ref[...] = r_ref[...] + _dot(a_ref[...], b_ref[...], NN)

    ospec = pl.BlockSpec((tm, tn), lambda n, i: (i, n))
    in_specs = [pl.BlockSpec((tm, K), lambda n, i: (i, 0)), pl.BlockSpec((K, tn), lambda n, i: (0, n))]
    args = [a, b]
    if resid is not None:
        in_specs.append(ospec)
        args.append(resid)
    return pl.pallas_call(
        body, out_shape=jax.ShapeDtypeStruct((M, N), F32), grid=(N // tn, M // tm),
        in_specs=in_specs, out_specs=ospec, name=name, compiler_params=_params(2))(*args)


def _wgrad_2d(name, a, b, tmm, tn):
    T, M = a.shape
    N = b.shape[1]

    def body(a_ref, b_ref, o_ref):
        o_ref[...] = _dot(a_ref[...], b_ref[...], TN).astype(BF16)

    return pl.pallas_call(
        body, out_shape=jax.ShapeDtypeStruct((M, N), BF16), grid=(M // tmm, N // tn),
        in_specs=[pl.BlockSpec((T, tmm), lambda m, n: (0, m)), pl.BlockSpec((T, tn), lambda m, n: (0, n))],
        out_specs=pl.BlockSpec((tmm, tn), lambda m, n: (m, n)), name=name, compiler_params=_params(2))(a, b)


def _wgrad_down(name, hm, df, tn):
    J, T, Fs = hm.shape
    D = df.shape[1]

    def body(a_ref, b_ref, o_ref):
        o_ref[...] = _dot(a_ref[...], b_ref[...], TN).astype(BF16)

    return pl.pallas_call(
        body, out_shape=jax.ShapeDtypeStruct((J, Fs, D), BF16), grid=(J, D // tn),
        in_specs=[pl.BlockSpec((None, T, Fs), lambda j, n: (j, 0, 0)), pl.BlockSpec((T, tn), lambda j, n: (0, n))],
        out_specs=pl.BlockSpec((None, Fs, tn), lambda j, n: (j, 0, n)), name=name, compiler_params=_params(2))(hm, df)


def _wgrad_up(name, n, da, db, tmm):
    T, D = n.shape
    J, _, Fs = da.shape

    def body(n_ref, da_ref, db_ref, og_ref, ou_ref):
        nv = n_ref[...]
        og_ref[...] = _dot(nv, da_ref[...], TN).astype(BF16)
        ou_ref[...] = _dot(nv, db_ref[...], TN).astype(BF16)

    act = pl.BlockSpec((None, T, Fs), lambda j, m: (j, 0, 0))
    out = pl.BlockSpec((None, tmm, Fs), lambda j, m: (j, m, 0))
    shape = jax.ShapeDtypeStruct((J, D, Fs), BF16)
    return pl.pallas_call(
        body, out_shape=(shape, shape), grid=(J, D // tmm),
        in_specs=[pl.BlockSpec((T, tmm), lambda j, m: (0, m)), act, act], out_specs=(out, out),
        name=name, compiler_params=_params(2))(n, da, db)


def _nt_reduce(name, pairs, once, T, D, tm, steps):
    n_pairs = len(pairs)
    n_once = len(once)

    def body(*refs):
        pr = refs[:2 * n_pairs]
        on = refs[2 * n_pairs:2 * (n_pairs + n_once)]
        o_ref, acc = refs[-2:]
        r = pl.program_id(1)
        part = _dot(pr[0][...], pr[1][...], NT)
        for p in range(1, n_pairs):
            part = part + _dot(pr[2 * p][...], pr[2 * p + 1][...], NT)

        @pl.when(r == 0)
        def _():
            acc[...] = part

        @pl.when(r > 0)
        def _():
            acc[...] += part

        @pl.when(r == steps - 1)
        def _():
            dn = acc[...]
            for p in range(n_once):
                dn = dn + _dot(on[2 * p][...], on[2 * p + 1][...], NT)
            o_ref[...] = dn

    in_specs, args = [], []
    for a, a_spec, w, w_spec in list(pairs) + list(once):
        in_specs += [a_spec, w_spec]
        args += [a, w]
    row = pl.BlockSpec((tm, D), lambda i, r: (i, 0))
    return pl.pallas_call(
        body, out_shape=jax.ShapeDtypeStruct((T, D), F32), grid=(T // tm, steps), in_specs=in_specs, out_specs=row,
        scratch_shapes=[pltpu.VMEM((tm, D), F32)], name=name, compiler_params=_params(2))(*args)


def _rmsnorm_bwd(name, dn, x, g, dh, tm, bf16_scale):
    T, D = x.shape
    emit_bf16 = bf16_scale is not None

    def body(dn_ref, x_ref, g_ref, dh_ref, *outs):
        dxn, dg = _rms_bwd(dn_ref[...], x_ref[...], g_ref[...])
        dx = dh_ref[...] + dxn
        outs[0][...] = dx
        if emit_bf16:
            outs[2][...] = (bf16_scale * dx).astype(BF16)

        @pl.when(pl.program_id(0) == 0)
        def _():
            outs[1][...] = dg

        @pl.when(pl.program_id(0) > 0)
        def _():
            outs[1][...] += dg

    row = pl.BlockSpec((tm, D), lambda i: (i, 0))
    gain = pl.BlockSpec((1, D), lambda i: (0, 0))
    out_shape = [jax.ShapeDtypeStruct((T, D), F32), jax.ShapeDtypeStruct((1, D), F32)]
    out_specs = [row, gain]
    if emit_bf16:
        out_shape.append(jax.ShapeDtypeStruct((T, D), BF16))
        out_specs.append(row)
    return pl.pallas_call(
        body, out_shape=tuple(out_shape), grid=(T // tm,), in_specs=[row, row, gain, row], out_specs=tuple(out_specs),
        name=name, compiler_params=_params(1))(dn, x, g, dh)


def _loss_grad(name, y, target, tm):
    T, D = y.shape

    def body(y_ref, t_ref, dy_ref, dyh_ref, sq_ref):
        diff = y_ref[...] - t_ref[...]
        sq = jnp.sum(jnp.sum(diff * diff, axis=1, keepdims=True), axis=0, keepdims=True)
        dy = diff * (1.0 / D)
        dy_ref[...] = dy
        dyh_ref[...] = (0.5 * dy).astype(BF16)

        @pl.when(pl.program_id(0) == 0)
        def _():
            sq_ref[...] = sq

        @pl.when(pl.program_id(0) > 0)
        def _():
            sq_ref[...] += sq

    row = pl.BlockSpec((tm, D), lambda i: (i, 0))
    return pl.pallas_call(
        body, out_shape=(jax.ShapeDtypeStruct((T, D), F32), jax.ShapeDtypeStruct((T, D), BF16), jax.ShapeDtypeStruct((1, 1), F32)),
        grid=(T // tm,), in_specs=[row, row], out_specs=(row, row, pl.BlockSpec((1, 1), lambda i: (0, 0))),
        name=name, compiler_params=_params(1))(y, target)


def _ffn_up(name, n, wg, wu, tm):
    T, D = n.shape
    J, _, Fs = wg.shape

    def body(n_ref, wg_ref, wu_ref, a_ref, b_ref, h_ref):
        xv = n_ref[...]
        a = _dot(xv, wg_ref[...], NN)
        b = _dot(xv, wu_ref[...], NN)
        a_ref[...] = a.astype(BF16)
        b_ref[...] = b.astype(BF16)
        h_ref[...] = (a * jax.nn.sigmoid(a) * b).astype(BF16)

    act = jax.ShapeDtypeStruct((J, T, Fs), BF16)
    wspec = pl.BlockSpec((None, D, Fs), lambda j, i: (j, 0, 0))
    aspec = pl.BlockSpec((None, tm, Fs), lambda j, i: (j, i, 0))
    return pl.pallas_call(
        body, out_shape=(act, act, act), grid=(J, T // tm),
        in_specs=[pl.BlockSpec((tm, D), lambda j, i: (i, 0)), wspec, wspec], out_specs=(aspec, aspec, aspec),
        name=name, compiler_params=_params(2))(n, wg, wu)


def _ffn_down(name, hm, wd, resid, tm):
    J, T, Fs = hm.shape
    D = wd.shape[2]

    def body(h_ref, w_ref, r_ref, o_ref, acc):
        j = pl.program_id(1)
        part = _dot(h_ref[...], w_ref[...], NN)

        @pl.when(j == 0)
        def _():
            acc[...] = part

        @pl.when(j > 0)
        def _():
            acc[...] += part

        @pl.when(j == J - 1)
        def _():
            o_ref[...] = r_ref[...] + 0.5 * acc[...]

    row = pl.BlockSpec((tm, D), lambda i, j: (i, 0))
    return pl.pallas_call(
        body, out_shape=jax.ShapeDtypeStruct((T, D), F32), grid=(T // tm, J),
        in_specs=[pl.BlockSpec((None, tm, Fs), lambda i, j: (j, i, 0)), pl.BlockSpec((None, Fs, D), lambda i, j: (j, 0, 0)), row],
        out_specs=row, scratch_shapes=[pltpu.VMEM((tm, D), F32)], name=name, compiler_params=_params(2))(hm, wd, resid)


def _ffn_bwd_mid(name, dfh, wd, a, b, tm):
    T, D = dfh.shape
    J, Fs, _ = wd.shape

    def body(df_ref, w_ref, a_ref, b_ref, da_ref, db_ref):
        dhm = _dot(df_ref[...], w_ref[...], NT)
        av = a_ref[...].astype(F32)
        bv = b_ref[...].astype(F32)
        sg = jax.nn.sigmoid(av)
        da_ref[...] = (dhm * bv * (sg * (1.0 + av * (1.0 - sg)))).astype(BF16)
        db_ref[...] = (dhm * (av * sg)).astype(BF16)

    act = jax.ShapeDtypeStruct((J, T, Fs), BF16)
    aspec = pl.BlockSpec((None, tm, Fs), lambda j, i: (j, i, 0))
    return pl.pallas_call(
        body, out_shape=(act, act), grid=(J, T // tm),
        in_specs=[pl.BlockSpec((tm, D), lambda j, i: (i, 0)), pl.BlockSpec((None, Fs, D), lambda j, i: (j, 0, 0)), aspec, aspec],
        out_specs=(aspec, aspec), name=name, compiler_params=_params(2))(dfh, wd, a, b)


def _ffn_backward(tag, dh, dhh, n, a, b, hm, wg, wu, wd, x, g, tm, bf16_scale):
    T, D = x.shape
    J, _, Fs = wg.shape
    da, db = _ffn_bwd_mid(tag + "_bwd_mid", dhh, wd, a, b, tm)
    dwd = _wgrad_down(tag + "_wgrad_down", hm, dhh, min(1024, D))
    dwg, dwu = _wgrad_up(tag + "_wgrad_up", n, da, db, min(512, D))
    aspec = pl.BlockSpec((None, tm, Fs), lambda i, j: (j, i, 0))
    wspec = pl.BlockSpec((None, D, Fs), lambda i, j: (j, 0, 0))
    dn = _nt_reduce(tag + "_bwd_in", [(da, aspec, wg, wspec), (db, aspec, wu, wspec)], [], T, D, tm, J)
    outs = _rmsnorm_bwd(tag + "_norm_bwd", dn, x, g, dh, min(256, T), bf16_scale)
    return outs, dwg, dwu, dwd


def _rot_half(y, lane):
    first = (lane & (HEAD_DIM // 2)) == 0
    return jnp.where(first, pltpu.roll(y, LANES - HEAD_DIM // 2, 1), pltpu.roll(y, HEAD_DIM // 2, 1))


def _head_rstd(x, lo):
    sq = x * x
    ss_a = jnp.sum(jnp.where(lo, sq, 0.0), axis=-1, keepdims=True)
    ss_b = jnp.sum(jnp.where(lo, 0.0, sq), axis=-1, keepdims=True)
    return lax.rsqrt(jnp.where(lo, ss_a, ss_b) * (1.0 / HEAD_DIM) + EPS)


def _headnorm_fwd(name, proj, col_off, ncb, gains, tm, scale, rope=None, dup=False):
    T = proj.shape[0]
    with_rope = rope is not None
    width = 2 * LANES if dup else LANES

    def body(*refs):
        if with_rope:
            x_ref, g_ref, cos_ref, sin_ref, o_ref = refs
        else:
            x_ref, g_ref, o_ref = refs
        xv = x_ref[...]
        lane = lax.broadcasted_iota(jnp.int32, xv.shape, 1)
        lo = lane < HEAD_DIM
        y = xv * _head_rstd(xv, lo) * g_ref[...]
        if with_rope:
            y = y * cos_ref[...] + _rot_half(y, lane) * sin_ref[...]
        y = y * scale
        if dup:
            sw = pltpu.roll(y, HEAD_DIM, 1)
            o_ref[:, :LANES] = jnp.where(lo, y, sw).astype(BF16)
            o_ref[:, LANES:] = jnp.where(lo, sw, y).astype(BF16)
        else:
            o_ref[...] = y.astype(BF16)

    in_specs = [pl.BlockSpec((tm, LANES), lambda c, i: (i, col_off + c)), pl.BlockSpec((None, 1, LANES), lambda c, i: (c, 0, 0))]
    args = [proj, gains]
    if with_rope:
        tab = pl.BlockSpec((tm, LANES), lambda c, i: (i, 0))
        in_specs += [tab, tab]
        args += list(rope)
    return pl.pallas_call(
        body, out_shape=jax.ShapeDtypeStruct((T, ncb * width), BF16), grid=(ncb, T // tm),
        in_specs=in_specs, out_specs=pl.BlockSpec((tm, width), lambda c, i: (i, c)),
        name=name, compiler_params=_params(2))(*args)


def _headnorm_bwd(name, dy, proj, col_off, ncb, gains, group, tm, scale, rope=None, fold=False, norm=True):
    T = dy.shape[0]
    with_rope = rope is not None
    n_groups = ncb // group
    dy_width = 4 * LANES if fold else LANES

    def body(*refs):
        refs = list(refs)
        dy_ref = refs.pop(0)
        x_ref = refs.pop(0) if norm else None
        g_ref = refs.pop(0) if norm else None
        cos_ref = refs.pop(0) if with_rope else None
        sin_ref = refs.pop(0) if with_rope else None
        dx_ref = refs.pop(0)
        dg_ref = refs.pop(0) if norm else None
        c = pl.program_id(0)
        i = pl.program_id(1)
        d = dy_ref[...]
        lane = lax.broadcasted_iota(jnp.int32, (d.shape[0], LANES), 1)
        lo = lane < HEAD_DIM
        if fold:
            t0 = d[:, 0:LANES] + d[:, LANES:2 * LANES]
            t1 = d[:, 2 * LANES:3 * LANES] + d[:, 3 * LANES:4 * LANES]
            d = jnp.where(lo, t0 + pltpu.roll(t0, HEAD_DIM, 1), t1 + pltpu.roll(t1, HEAD_DIM, 1))
        d = d * scale
        if with_rope:
            d = d * cos_ref[...] + _rot_half(d * sin_ref[...], lane)
        if not norm:
            dx_ref[...] = d.astype(BF16)
            return
        xv = x_ref[...]
        gv = g_ref[...]
        r = _head_rstd(xv, lo)
        xh = xv * r
        dxh = d * gv
        pr = dxh * xh
        m_a = jnp.sum(jnp.where(lo, pr, 0.0), axis=-1, keepdims=True)
        m_b = jnp.sum(jnp.where(lo, 0.0, pr), axis=-1, keepdims=True)
        mean = jnp.where(lo, m_a, m_b) * (1.0 / HEAD_DIM)
        dx_ref[...] = (r * (dxh - xh * mean)).astype(BF16)
        dgp = jnp.sum(d * xh, axis=0, keepdims=True)
        dgp = dgp + pltpu.roll(dgp, HEAD_DIM, 1)
        first = jnp.logical_and(c % group == 0, i == 0)

        @pl.when(first)
        def _():
            dg_ref[...] = dgp

        @pl.when(jnp.logical_not(first))
        def _():
            dg_ref[...] += dgp

    in_specs = [pl.BlockSpec((tm, dy_width), lambda c, i: (i, c))]
    args = [dy]
    if norm:
        in_specs += [pl.BlockSpec((tm, LANES), lambda c, i: (i, col_off + c)), pl.BlockSpec((None, 1, LANES), lambda c, i: (c, 0, 0))]
        args += [proj, gains]
    if with_rope:
        tab = pl.BlockSpec((tm, LANES), lambda c, i: (i, 0))
        in_specs += [tab, tab]
        args += list(rope)
    out_shape = [jax.ShapeDtypeStruct((T, ncb * LANES), BF16)]
    out_specs = [pl.BlockSpec((tm, LANES), lambda c, i: (i, c))]
    if norm:
        out_shape.append(jax.ShapeDtypeStruct((n_groups, 1, LANES), F32))
        out_specs.append(pl.BlockSpec((None, 1, LANES), lambda c, i: (c // group, 0, 0)))
    res = pl.pallas_call(
        body, out_shape=tuple(out_shape), grid=(ncb, T // tm), in_specs=in_specs, out_specs=tuple(out_specs),
        name=name, compiler_params=_params(2))(*args)
    return res if norm else (res[0], None)


def _dot_exact(x, tri):
    hi = x.astype(BF16)
    r1 = x - hi.astype(F32)
    mid = r1.astype(BF16)
    lo = (r1 - mid.astype(F32)).astype(BF16)
    return _dot(hi, tri, NN) + _dot(mid, tri, NN) + _dot(lo, tri, NN)


def _forget_fwd(name, zt, bias):
    H, T = zt.shape
    blk = min(256, T)

    def body(z_ref, b_ref, c_ref, s_ref):
        z = z_ref[...] + b_ref[...]
        s_ref[...] = jax.nn.sigmoid(-z)
        lf = jnp.minimum(z, 0.0) - jnp.log(1.0 + jnp.exp(-jnp.abs(z)))
        tri = (lax.broadcasted_iota(jnp.int32, (blk, blk), 0) <= lax.broadcasted_iota(jnp.int32, (blk, blk), 1)).astype(BF16)
        carry = jnp.zeros((H, 1), F32)
        for bi in range(T // blk):
            xb = lf[:, bi * blk:(bi + 1) * blk]
            c_ref[:, bi * blk:(bi + 1) * blk] = _dot_exact(xb, tri) + carry
            carry = carry + jnp.sum(xb, axis=-1, keepdims=True)

    shape = jax.ShapeDtypeStruct((H, T), F32)
    full = pl.BlockSpec((H, T), lambda i: (0, 0))
    return pl.pallas_call(
        body, out_shape=(shape, shape), grid=(1,), in_specs=[full, pl.BlockSpec((H, 1), lambda i: (0, 0))],
        out_specs=(full, full), name=name, compiler_params=_params(1))(zt, bias)


def _forget_bwd(name, dct, drt, sgt):
    H, T = dct.shape
    blk = min(256, T)

    def body(dc_ref, dr_ref, s_ref, dz_ref, db_ref):
        dc = dc_ref[...] + dr_ref[...]
        tri = (lax.broadcasted_iota(jnp.int32, (blk, blk), 0) >= lax.broadcasted_iota(jnp.int32, (blk, blk), 1)).astype(BF16)
        carry = jnp.zeros((H, 1), F32)
        db = jnp.zeros((H, 1), F32)
        for bi in reversed(range(T // blk)):
            xb = dc[:, bi * blk:(bi + 1) * blk]
            dz = (_dot_exact(xb, tri) + carry) * s_ref[:, bi * blk:(bi + 1) * blk]
            dz_ref[:, bi * blk:(bi + 1) * blk] = dz
            db = db + jnp.sum(dz, axis=-1, keepdims=True)
            carry = carry + jnp.sum(xb, axis=-1, keepdims=True)
        db_ref[...] = db

    full = pl.BlockSpec((H, T), lambda i: (0, 0))
    return pl.pallas_call(
        body, out_shape=(jax.ShapeDtypeStruct((H, T), F32), jax.ShapeDtypeStruct((H, 1), F32)), grid=(1,),
        in_specs=[full, full, full], out_specs=(full, pl.BlockSpec((H, 1), lambda i: (0, 0))),
        name=name, compiler_params=_params(1))(dct, drt, sgt)


def _fox_fwd(name, qk, v, ccol, crow, tq, tk):
    T, Dh = v.shape
    HP = Dh // LANES
    nk = T // tk

    def body(q_ref, k_ref, v_ref, ca_ref, cb_ref, ra_ref, rb_ref, o_ref, la_ref, lb_ref):
        i = pl.program_id(1)
        q2 = q_ref[...]
        lo = _lane_lo((tq, LANES))
        rows = i * tq + lax.broadcasted_iota(jnp.int32, (tq, tk), 0)
        col0 = lax.broadcasted_iota(jnp.int32, (tq, tk), 1)
        n_chunks = ((i + 1) * tq + tk - 1) // tk
        res = []
        for sel, c_ref, r_ref in ((lo, ca_ref, ra_ref), (jnp.logical_not(lo), cb_ref, rb_ref)):
            qm = _keep(sel, q2)
            ct = c_ref[...]

            def step(kc, carry, qm=qm, ct=ct, r_ref=r_ref):
                m, l, acc = carry
                start = pl.multiple_of(kc * tk, tk)
                kb = k_ref[pl.ds(start, tk), :]
                vb = v_ref[pl.ds(start, tk), :]
                s = _dot(qm, kb, NT) + ct - r_ref[kc]
                s = jnp.where(rows >= col0 + start, s, NEG)
                mn = jnp.maximum(m, jnp.max(s, axis=-1, keepdims=True))
                p = jnp.exp(s - mn)
                alpha = jnp.exp(m - mn)
                l = alpha * l + jnp.sum(p, axis=-1, keepdims=True)
                acc = alpha * acc + _dot(p.astype(BF16), vb, NN)
                return mn, l, acc

            m, l, acc = lax.fori_loop(0, n_chunks, step, (jnp.full((tq, 1), NEG, F32), jnp.zeros((tq, 1), F32), jnp.zeros((tq, LANES), F32)))
            res.append((acc / l, m + jnp.log(l)))
        o_ref[...] = jnp.where(lo, res[0][0], res[1][0])
        la_ref[...] = res[0][1]
        lb_ref[...] = res[1][1]

    col = lambda off: pl.BlockSpec((None, tq, 1), lambda h, i: (2 * h + off, i, 0))
    row = lambda off: pl.BlockSpec((None, nk, 1, tk), lambda h, i: (2 * h + off, 0, 0, 0))
    lse = jax.ShapeDtypeStruct((HP, T, 1), F32)
    lspec = pl.BlockSpec((None, tq, 1), lambda h, i: (h, i, 0))
    return pl.pallas_call(
        body, out_shape=(jax.ShapeDtypeStruct((T, Dh), F32), lse, lse), grid=(HP, T // tq),
        in_specs=[pl.BlockSpec((tq, LANES), lambda h, i: (i, h)), pl.BlockSpec((T, LANES), lambda h, i: (0, HP + h)),
                  pl.BlockSpec((T, LANES), lambda h, i: (0, h)), col(0), col(1), row(0), row(1)],
        out_specs=(pl.BlockSpec((tq, LANES), lambda h, i: (i, h)), lspec, lspec),
        name=name, compiler_params=_params(2))(qk, qk, v, ccol, ccol, crow, crow)


def _fox_bwd(name, qk, v, o, do, ccol, crow, lse_a, lse_b, tq, tk):
    T, Dh = v.shape
    HP = Dh // LANES
    nk = T // tk
    scale = HEAD_DIM ** -0.5

    def body(q_ref, k_ref, v_ref, o_ref, do_ref, ca_ref, cb_ref, ra_ref, rb_ref, la_ref, lb_ref,
             dq_ref, dk_ref, dv_ref, dca_ref, dcb_ref, dra_ref, drb_ref):
        i = pl.program_id(1)

        @pl.when(i == 0)
        def _():
            dk_ref[...] = jnp.zeros_like(dk_ref)
            dv_ref[...] = jnp.zeros_like(dv_ref)
            dca_ref[...] = jnp.zeros_like(dca_ref)
            dcb_ref[...] = jnp.zeros_like(dcb_ref)

        q2 = q_ref[...]
        do2 = do_ref[...]
        lo = _lane_lo((tq, LANES))
        hi = jnp.logical_not(lo)
        prod = do2.astype(F32) * o_ref[...]
        rows = i * tq + lax.broadcasted_iota(jnp.int32, (tq, tk), 0)
        col0 = lax.broadcasted_iota(jnp.int32, (tq, tk), 1)
        n_chunks = ((i + 1) * tq + tk - 1) // tk
        dqs = []
        for sel, c_ref, r_ref, l_ref, dc_ref, dr_ref in ((lo, ca_ref, ra_ref, la_ref, dca_ref, dra_ref),
                                                         (hi, cb_ref, rb_ref, lb_ref, dcb_ref, drb_ref)):
            qm = _keep(sel, q2)
            dom = _keep(sel, do2)
            dsum = jnp.sum(jnp.where(sel, prod, 0.0), axis=-1, keepdims=True)
            ct = c_ref[...]
            lse = l_ref[...]

            def step(kc, carry, qm=qm, dom=dom, dsum=dsum, ct=ct, lse=lse, r_ref=r_ref, dc_ref=dc_ref):
                dq, dr = carry
                start = pl.multiple_of(kc * tk, tk)
                kb = k_ref[pl.ds(start, tk), :]
                vb = v_ref[pl.ds(start, tk), :]
                s = _dot(qm, kb, NT) + ct - r_ref[kc]
                s = jnp.where(rows >= col0 + start, s, NEG)
                p = jnp.exp(s - lse)
                ds = p * (_dot(dom, vb, NT) - dsum)
                dsb = ds.astype(BF16)
                dk_ref[pl.ds(start, tk), :] += _dot(dsb, qm, TN)
                dv_ref[pl.ds(start, tk), :] += _dot(p.astype(BF16), dom, TN)
                dc_ref[kc] = dc_ref[kc] - jnp.sum(ds, axis=0, keepdims=True)
                return dq + _dot(dsb, kb, NN), dr + jnp.sum(ds, axis=-1, keepdims=True)

            dq, dr = lax.fori_loop(0, n_chunks, step, (jnp.zeros((tq, LANES), F32), jnp.zeros((tq, 1), F32)))
            dqs.append(dq)
            dr_ref[...] = dr
        dq_ref[...] = jnp.where(lo, dqs[0], dqs[1]) * scale

    col = lambda off: pl.BlockSpec((None, tq, 1), lambda h, i: (2 * h + off, i, 0))
    row = lambda off: pl.BlockSpec((None, nk, 1, tk), lambda h, i: (2 * h + off, 0, 0, 0))
    lspec = pl.BlockSpec((None, tq, 1), lambda h, i: (h, i, 0))
    qspec = pl.BlockSpec((tq, LANES), lambda h, i: (i, h))
    full = pl.BlockSpec((T, LANES), lambda h, i: (0, h))
    dcspec = pl.BlockSpec((None, nk, 1, tk), lambda h, i: (h, 0, 0, 0))
    grad = jax.ShapeDtypeStruct((T, Dh), F32)
    dc = jax.ShapeDtypeStruct((HP, nk, 1, tk), F32)
    dr = jax.ShapeDtypeStruct((HP, T, 1), F32)
    return pl.pallas_call(
        body, out_shape=(grad, grad, grad, dc, dc, dr, dr), grid=(HP, T // tq),
        in_specs=[qspec, pl.BlockSpec((T, LANES), lambda h, i: (0, HP + h)), full, qspec, qspec,
                  col(0), col(1), row(0), row(1), lspec, lspec],
        out_specs=(qspec, full, full, dcspec, dcspec, lspec, lspec),
        name=name, compiler_params=_params(2))(qk, qk, v, o, do, ccol, ccol, crow, crow, lse_a, lse_b)


def _swa_block(n, q_ref, k_ref):
    qs = pl.multiple_of(n * WINDOW, WINDOW)
    ks = pl.multiple_of(jnp.maximum(n - 1, 0) * WINDOW, WINDOW)
    rel = (qs + lax.broadcasted_iota(jnp.int32, (WINDOW, 2 * WINDOW), 0)) - (ks + lax.broadcasted_iota(jnp.int32, (WINDOW, 2 * WINDOW), 1))
    valid = jnp.logical_and(rel >= 0, rel < WINDOW)
    return qs, ks, valid


def _swa_fwd(name, q, kd, vd, sinks):
    T, Dh = q.shape
    HP = Dh // LANES

    def body(q_ref, k_ref, v_ref, sa_ref, sb_ref, o_ref, la_ref, lb_ref):
        lo = _lane_lo((WINDOW, LANES))

        def block(n, _):
            qs, ks, valid = _swa_block(n, q_ref, k_ref)
            q2 = q_ref[pl.ds(qs, WINDOW), :]
            kb = k_ref[pl.ds(ks, 2 * WINDOW), :]
            vb = v_ref[pl.ds(ks, 2 * WINDOW), :]
            res = []
            for sel, s_ref in ((lo, sa_ref), (jnp.logical_not(lo), sb_ref)):
                qm = _keep(sel, q2)
                sink = s_ref[...]
                s = jnp.where(valid, _dot(qm, kb, NT), NEG)
                m = jnp.maximum(jnp.max(s, axis=-1, keepdims=True), sink)
                p = jnp.exp(s - m)
                l = jnp.sum(p, axis=-1, keepdims=True) + jnp.exp(sink - m)
                res.append((_dot(p.astype(BF16), vb, NN) / l, m + jnp.log(l)))
            o_ref[pl.ds(qs, WINDOW), :] = jnp.where(lo, res[0][0], res[1][0])
            la_ref[pl.ds(qs, WINDOW), :] = res[0][1]
            lb_ref[pl.ds(qs, WINDOW), :] = res[1][1]
            return 0

        lax.fori_loop(0, T // WINDOW, block, 0)

    full = pl.BlockSpec((T, LANES), lambda h: (0, h))
    kv = pl.BlockSpec((T, LANES), lambda h: (0, h // 2))
    sink = lambda off: pl.BlockSpec((None, 1, 1), lambda h: (2 * h + off, 0, 0))
    lse = jax.ShapeDtypeStruct((HP, T, 1), F32)
    lspec = pl.BlockSpec((None, T, 1), lambda h: (h, 0, 0))
    return pl.pallas_call(
        body, out_shape=(jax.ShapeDtypeStruct((T, Dh), F32), lse, lse), grid=(HP,),
        in_specs=[full, kv, kv, sink(0), sink(1)], out_specs=(full, lspec, lspec),
        name=name, compiler_params=_params(1))(q, kd, vd, sinks, sinks)


def _swa_bwd(name, q, kd, vd, sinks, o, do, lse_a, lse_b):
    T, Dh = q.shape
    HP = Dh // LANES
    scale = HEAD_DIM ** -0.5

    def body(q_ref, k_ref, v_ref, sa_ref, sb_ref, o_ref, do_ref, la_ref, lb_ref, dq_ref, dk_ref, dv_ref, dsa_ref, dsb_ref):
        lo = _lane_lo((WINDOW, LANES))
        hi = jnp.logical_not(lo)
        dk_ref[...] = jnp.zeros_like(dk_ref)
        dv_ref[...] = jnp.zeros_like(dv_ref)

        def block(n, dsinks):
            qs, ks, valid = _swa_block(n, q_ref, k_ref)
            q2 = q_ref[pl.ds(qs, WINDOW), :]
            do2 = do_ref[pl.ds(qs, WINDOW), :]
            kb = k_ref[pl.ds(ks, 2 * WINDOW), :]
            vb = v_ref[pl.ds(ks, 2 * WINDOW), :]
            prod = do2.astype(F32) * o_ref[pl.ds(qs, WINDOW), :]
            dqs, new = [], []
            dk = jnp.zeros((2 * WINDOW, LANES), F32)
            dv = jnp.zeros((2 * WINDOW, LANES), F32)
            for sel, s_ref, l_ref, dsink in ((lo, sa_ref, la_ref, dsinks[0]), (hi, sb_ref, lb_ref, dsinks[1])):
                qm = _keep(sel, q2)
                dom = _keep(sel, do2)
                dsum = jnp.sum(jnp.where(sel, prod, 0.0), axis=-1, keepdims=True)
                lse = l_ref[pl.ds(qs, WINDOW), :]
                s = jnp.where(valid, _dot(qm, kb, NT), NEG)
                p = jnp.exp(s - lse)
                ds = p * (_dot(dom, vb, NT) - dsum)
                dsb = ds.astype(BF16)
                dqs.append(_dot(dsb, kb, NN))
                dk = dk + _dot(dsb, qm, TN)
                dv = dv + _dot(p.astype(BF16), dom, TN)
                new.append(dsink - jnp.sum(jnp.exp(s_ref[...] - lse) * dsum, axis=0, keepdims=True))
            dq_ref[pl.ds(qs, WINDOW), :] = jnp.where(lo, dqs[0], dqs[1]) * scale
            dk_ref[pl.ds(ks, 2 * WINDOW), :] += dk
            dv_ref[pl.ds(ks, 2 * WINDOW), :] += dv
            return tuple(new)

        dsa, dsb_ = lax.fori_loop(0, T // WINDOW, block, (jnp.zeros((1, 1), F32), jnp.zeros((1, 1), F32)))
        dsa_ref[...] = dsa
        dsb_ref[...] = dsb_

    full = pl.BlockSpec((T, LANES), lambda h: (0, h))
    kv = pl.BlockSpec((T, LANES), lambda h: (0, h // 2))
    sink = lambda off: pl.BlockSpec((None, 1, 1), lambda h: (2 * h + off, 0, 0))
    lspec = pl.BlockSpec((None, T, 1), lambda h: (h, 0, 0))
    dsink = pl.BlockSpec((None, 1, 1), lambda h: (h, 0, 0))
    grad = jax.ShapeDtypeStruct((T, Dh), F32)
    ds_shape = jax.ShapeDtypeStruct((HP, 1, 1), F32)
    return pl.pallas_call(
        body, out_shape=(grad, grad, grad, ds_shape, ds_shape), grid=(HP,),
        in_specs=[full, kv, kv, sink(0), sink(1), full, full, lspec, lspec],
        out_specs=(full, full, full, dsink, dsink),
        name=name, compiler_params=_params(1))(q, kd, vd, sinks, sinks, o, do, lse_a, lse_b)


_HBM = pl.BlockSpec(memory_space=pltpu.HBM)


def _place():
    return lax.axis_index("x"), lax.axis_index("y"), lax.axis_index("c")


def _all_gather(shards):
    n = len(shards)

    def body(*refs):
        ins, outs = refs[:n], refs[n:2 * n]
        send, recv, local = refs[2 * n:]
        x, y, c = _place()
        me, sibling = (x, y, c), (x, y, 1 - c)
        chips = [(1 - x, y), (x, 1 - y), (1 - x, 1 - y)]

        def copy(w, k, block, to, src=None):
            slot = 4 * block[0] + 2 * block[1] + block[2]
            return pltpu.make_async_remote_copy(
                src_ref=outs[w].at[slot] if src is None else src, dst_ref=outs[w].at[slot],
                send_sem=send.at[w, k], recv_sem=recv.at[w, k], device_id=to, device_id_type=MESH)

        started = []
        for w in range(n):
            mine = pltpu.make_async_copy(ins[w], outs[w].at[4 * x + 2 * y + c], local.at[w])
            mine.start()
            started.append(mine)
        sends = []
        for w in range(n):
            sends.append(copy(w, 0, me, sibling, src=ins[w]))
            sends += [copy(w, 1 + j, me, (*chip, c), src=ins[w]) for j, chip in enumerate(chips)]
        for cp in sends:
            cp.start()
        for w in range(n):
            for j, chip in enumerate(chips):
                copy(w, 1 + j, (*chip, c), me).wait_recv()
                fwd = copy(w, 4 + j, (*chip, c), sibling)
                fwd.start()
                sends.append(fwd)
        for w in range(n):
            copy(w, 0, sibling, me).wait_recv()
            for j, chip in enumerate(chips):
                copy(w, 4 + j, (*chip, 1 - c), me).wait_recv()
        for cp in sends:
            cp.wait_send()
        for mine in started:
            mine.wait()

    return pl.pallas_call(
        body, out_shape=tuple(jax.ShapeDtypeStruct((N_DEV,) + s.shape, s.dtype) for s in shards),
        in_specs=[_HBM] * n, out_specs=tuple([_HBM] * n),
        scratch_shapes=[pltpu.SemaphoreType.DMA((n, 7)), pltpu.SemaphoreType.DMA((n, 7)), pltpu.SemaphoreType.DMA((n,))],
        name="weights_all_gather")(*shards)


def _exchange_sibling(grads):
    n = len(grads)

    def body(*refs):
        ins, outs = refs[:n], refs[n:2 * n]
        send, recv = refs[2 * n:]
        x, y, c = _place()
        copies = []
        for w in range(n):
            for q in range(4):
                copies.append(pltpu.make_async_remote_copy(
                    src_ref=ins[w].at[2 * q + (1 - c)], dst_ref=outs[w].at[q], send_sem=send.at[w, q], recv_sem=recv.at[w, q],
                    device_id=(x, y, 1 - c), device_id_type=MESH))
        for cp in copies:
            cp.start()
        for cp in copies:
            cp.wait()

    return pl.pallas_call(
        body, out_shape=tuple(jax.ShapeDtypeStruct((4,) + g.shape[1:], g.dtype) for g in grads),
        in_specs=[_HBM] * n, out_specs=tuple([_HBM] * n),
        scratch_shapes=[pltpu.SemaphoreType.DMA((n, 4)), pltpu.SemaphoreType.DMA((n, 4))],
        name="grads_exchange_sibling")(*grads)


def _exchange_chips(sums):
    n = len(sums)

    def body(*refs):
        ins, outs = refs[:n], refs[n:2 * n]
        send, recv = refs[2 * n:]
        x, y, c = _place()
        chips = [(1 - x, y), (x, 1 - y), (1 - x, 1 - y)]
        copies = []
        for w in range(n):
            for k, chip in enumerate(chips):
                copies.append(pltpu.make_async_remote_copy(
                    src_ref=ins[w].at[2 * chip[0] + chip[1]], dst_ref=outs[w].at[k], send_sem=send.at[w, k], recv_sem=recv.at[w, k],
                    device_id=(*chip, c), device_id_type=MESH))
        for cp in copies:
            cp.start()
        for cp in copies:
            cp.wait()

    return pl.pallas_call(
        body, out_shape=tuple(jax.ShapeDtypeStruct((3,) + s.shape[1:], s.dtype) for s in sums),
        in_specs=[_HBM] * n, out_specs=tuple([_HBM] * n),
        scratch_shapes=[pltpu.SemaphoreType.DMA((n, 3)), pltpu.SemaphoreType.DMA((n, 3))],
        name="grads_exchange_chips")(*sums)


def _gather_small(packed):
    R, C = packed.shape

    def body(in_ref, out_ref, send, recv):
        x, y, c = _place()
        mine = 4 * x + 2 * y + c
        out_ref[mine] = in_ref[...]
        copies = []
        for k in range(1, N_DEV):
            peer = (x ^ (k >> 2), y ^ ((k >> 1) & 1), c ^ (k & 1))
            copies.append(pltpu.make_async_remote_copy(
                src_ref=in_ref, dst_ref=out_ref.at[mine], send_sem=send.at[k - 1], recv_sem=recv.at[k - 1],
                device_id=peer, device_id_type=MESH))
        for cp in copies:
            cp.start()
        for cp in copies:
            cp.wait()

    vmem = pl.BlockSpec(memory_space=pltpu.VMEM)
    return pl.pallas_call(
        body, out_shape=jax.ShapeDtypeStruct((N_DEV, R, C), F32), in_specs=[vmem], out_specs=vmem,
        scratch_shapes=[pltpu.SemaphoreType.DMA((N_DEV - 1,)), pltpu.SemaphoreType.DMA((N_DEV - 1,))],
        name="small_grads_all_gather")(packed)


def _adamw(w, g, m, v):
    m = ADAM_B1 * m + (1.0 - ADAM_B1) * g
    v = ADAM_B2 * v + (1.0 - ADAM_B2) * (g * g)
    m_hat = m / (1.0 - ADAM_B1 ** ADAM_STEP)
    v_hat = v / (1.0 - ADAM_B2 ** ADAM_STEP)
    delta = -ADAM_LR * (m_hat / (jnp.sqrt(v_hat) + ADAM_EPS) + ADAM_WD * w)
    return delta, m, v


def _pair_add(name, grads, received, c_idx):
    _, R, C = grads.shape
    tr = _row_tile(R)

    def body(c_ref, g_ref, r_ref, o_ref):
        o_ref[...] = (g_ref[...].astype(F32) + r_ref[...].astype(F32)).astype(BF16)

    blk = pl.BlockSpec((None, tr, C), lambda q, i, c: (q, i, 0))
    return pl.pallas_call(
        body, out_shape=jax.ShapeDtypeStruct((4, R, C), BF16),
        grid_spec=pltpu.PrefetchScalarGridSpec(
            num_scalar_prefetch=1, grid=(4, R // tr),
            in_specs=[pl.BlockSpec((None, tr, C), lambda q, i, c: (2 * q + c[0], i, 0)), blk], out_specs=blk),
        name=name, compiler_params=_params(2))(c_idx, grads, received)


def _adam_shard(name, sums, received, w, m, v, chip_idx):
    R, C = w.shape
    tr = _row_tile(R, 128)

    def body(q_ref, s_ref, r_ref, w_ref, m_ref, v_ref, g_out, d_out, m_out, v_out):
        g = s_ref[...].astype(F32) + r_ref[0].astype(F32) + r_ref[1].astype(F32) + r_ref[2].astype(F32)
        delta, mn, vn = _adamw(w_ref[...], g, m_ref[...], v_ref[...])
        g_out[...] = g
        d_out[...] = delta
        m_out[...] = mn
        v_out[...] = vn

    blk = pl.BlockSpec((tr, C), lambda i, q: (i, 0))
    shape = jax.ShapeDtypeStruct((R, C), F32)
    return pl.pallas_call(
        body, out_shape=(shape,) * 4,
        grid_spec=pltpu.PrefetchScalarGridSpec(
            num_scalar_prefetch=1, grid=(R // tr,),
            in_specs=[pl.BlockSpec((None, tr, C), lambda i, q: (q[0], i, 0)), pl.BlockSpec((3, tr, C), lambda i, q: (0, i, 0)),
                      blk, blk, blk],
            out_specs=(blk,) * 4),
        name=name, compiler_params=_params(1))(chip_idx, sums, received, w, m, v)


def _adam_small(name, gathered, w, m, v):
    R, C = w.shape

    def body(ga_ref, w_ref, m_ref, v_ref, g_out, d_out, m_out, v_out):
        g = ga_ref[0]
        for d in range(1, N_DEV):
            g = g + ga_ref[d]
        delta, mn, vn = _adamw(w_ref[...], g, m_ref[...], v_ref[...])
        g_out[...] = g
        d_out[...] = delta
        m_out[...] = mn
        v_out[...] = vn

    full = pl.BlockSpec((R, C), lambda i: (0, 0))
    shape = jax.ShapeDtypeStruct((R, C), F32)
    return pl.pallas_call(
        body, out_shape=(shape,) * 4, grid=(1,),
        in_specs=[pl.BlockSpec((N_DEV, R, C), lambda i: (0, 0, 0)), full, full, full], out_specs=(full,) * 4,
        name=name, compiler_params=_params(1))(gathered, w, m, v)


def _pack_small(parts, D):
    g1, gmix, g2, gof, gos, bf, gqf, gkf, gqs, gks, sinks = [p.reshape(-1).astype(F32) for p in parts]
    row3 = jnp.concatenate([gof, gos])
    row4 = jnp.zeros((D,), F32)
    for slot, vec in enumerate((bf, gqf, gkf, gqs, gks, sinks)):
        row4 = lax.dynamic_update_slice(row4, vec, (slot * LANES,))
    zero = jnp.zeros((D,), F32)
    return jnp.stack([g1, gmix, g2, row3, row4, zero, zero, zero])


def _unpack_small(packed, D, H):
    Dh = D // 2
    row4 = packed[4]
    short = [row4[s * LANES:s * LANES + n] for s, n in enumerate((H, HEAD_DIM, HEAD_DIM, HEAD_DIM, HEAD_DIM, H))]
    vecs = [packed[0], packed[1], packed[2], packed[3, :Dh], packed[3, Dh:]] + short
    return [v[None, :] for v in vecs]


def kernel(x, positions, norm_ffn1_g, ffn1_w_gate, ffn1_w_up, ffn1_w_down, norm_mix_g, w_in, b_forget, fox_q_norm_g, fox_k_norm_g, swa_q_norm_g, swa_k_norm_g, swa_sinks, out_norm_fox_g, out_norm_swa_g, w_out, norm_ffn2_g, ffn2_w_gate, ffn2_w_up, ffn2_w_down, loss_target, m_norm_ffn1_g, m_ffn1_w_gate, m_ffn1_w_up, m_ffn1_w_down, m_norm_mix_g, m_w_in, m_b_forget, m_fox_q_norm_g, m_fox_k_norm_g, m_swa_q_norm_g, m_swa_k_norm_g, m_swa_sinks, m_out_norm_fox_g, m_out_norm_swa_g, m_w_out, m_norm_ffn2_g, m_ffn2_w_gate, m_ffn2_w_up, m_ffn2_w_down, v_norm_ffn1_g, v_ffn1_w_gate, v_ffn1_w_up, v_ffn1_w_down, v_norm_mix_g, v_w_in, v_b_forget, v_fox_q_norm_g, v_fox_k_norm_g, v_swa_q_norm_g, v_swa_k_norm_g, v_swa_sinks, v_out_norm_fox_g, v_out_norm_swa_g, v_w_out, v_norm_ffn2_g, v_ffn2_w_gate, v_ffn2_w_up, v_ffn2_w_down):
    xs = x[0]
    target = loss_target[0]
    T, D = xs.shape
    Dh = D // 2
    H = Dh // HEAD_DIM
    HP = H // 2
    KVW = (H // GQA_GROUP) * HEAD_DIM
    KVB = KVW // LANES
    MAIN = 4 * Dh + 2 * KVW
    F_OFF = 3 * Dh
    tm = min(ROW_TILE_CAP, T)
    tq = min(256, T)
    tk = min(512, T)
    nk = T // tk
    cx, cy, cc = _place()
    c_idx = jnp.reshape(cc, (1,)).astype(jnp.int32)
    chip_idx = jnp.reshape(2 * cx + cy, (1,)).astype(jnp.int32)

    big_w = [ffn1_w_gate[0], ffn1_w_up[0], ffn1_w_down[0], w_in[0], w_out[0], ffn2_w_gate[0], ffn2_w_up[0], ffn2_w_down[0]]
    big_m = [m_ffn1_w_gate[0], m_ffn1_w_up[0], m_ffn1_w_down[0], m_w_in[0], m_w_out[0], m_ffn2_w_gate[0], m_ffn2_w_up[0], m_ffn2_w_down[0]]
    big_v = [v_ffn1_w_gate[0], v_ffn1_w_up[0], v_ffn1_w_down[0], v_w_in[0], v_w_out[0], v_ffn2_w_gate[0], v_ffn2_w_up[0], v_ffn2_w_down[0]]
    wg1, wu1, wd1, win_g, wout_g, wg2, wu2, wd2 = _all_gather([w.astype(BF16) for w in big_w])
    n_in = win_g.shape[2]
    win_full = jnp.transpose(win_g, (1, 0, 2)).reshape(D, N_DEV * n_in)
    win_main = jnp.concatenate([win_full[:, :F_OFF], win_full[:, F_OFF + H:]], axis=1)
    win_f = jnp.pad(win_full[:, F_OFF:F_OFF + H], ((0, 0), (0, LANES - H)))
    wout = wout_g.reshape(D, D)

    lane = jnp.arange(LANES)
    inv_freq = ROPE_THETA ** (-(2.0 * (lane % (HEAD_DIM // 2))).astype(F32) / HEAD_DIM)
    ang = positions[0].astype(F32)[:, None] * inv_freq[None, :]
    cos_t = jnp.cos(ang)
    sin_t = jnp.where((lane & (HEAD_DIM // 2)) == 0, -1.0, 1.0)[None, :] * jnp.sin(ang)
    rope = (cos_t, sin_t)

    def pair_gain(g, blocks):
        return jnp.tile(jnp.concatenate([g[0], g[0]])[None, None, :], (blocks, 1, 1))

    n1 = _rmsnorm_fwd("ffn1_norm", xs, norm_ffn1_g, tm)
    a1, b1, hm1 = _ffn_up("ffn1_up", n1, wg1, wu1, tm)
    h1 = _ffn_down("ffn1_down", hm1, wd1, xs, tm)

    u = _rmsnorm_fwd("mix_norm", h1, norm_mix_g, tm)
    proj = _mm_nn("mix_proj", u, win_main, tm, MAIN // 9)
    proj_f = _mm_nn("mix_proj_forget", u, win_f, tm, LANES)
    scale = HEAD_DIM ** -0.5
    fox_gains = jnp.concatenate([pair_gain(fox_q_norm_g, HP), pair_gain(fox_k_norm_g, HP)])
    qk_f = _headnorm_fwd_scaled("fox_qk_norm", proj, 0, 2 * HP, fox_gains, tm, scale, HP)
    v_f = proj[:, 2 * Dh:3 * Dh].astype(BF16)
    c_t, sg_t = _forget_fwd("forget_gates", proj_f[:, :H].T, b_forget.reshape(H, 1))
    ccol = c_t[:, :, None]
    crow = c_t.reshape(H, nk, 1, tk)
    o_fox, lse_fa, lse_fb = _fox_fwd("fox_attention", qk_f, v_f, ccol, crow, tq, tk)

    swa_q_gains = pair_gain(swa_q_norm_g, HP)
    swa_k_gains = pair_gain(swa_k_norm_g, KVB)
    q_s = _headnorm_fwd("swa_q_norm", proj, 3 * HP, HP, swa_q_gains, tm, scale, rope=rope)
    k_d = _headnorm_fwd("swa_k_norm", proj, 4 * HP, KVB, swa_k_gains, tm, 1.0, rope=rope, dup=True)
    v_s = proj[:, 4 * Dh + KVW:].astype(BF16).reshape(T, H // GQA_GROUP, 1, HEAD_DIM)
    v_d = jnp.broadcast_to(v_s, (T, H // GQA_GROUP, 2, HEAD_DIM)).reshape(T, 2 * KVW)
    sinks3 = swa_sinks.reshape(H, 1, 1)
    o_swa, lse_sa, lse_sb = _swa_fwd("swa_attention", q_s, k_d, v_d, sinks3)

    on = _outnorm_fwd("out_norm", o_fox, o_swa, out_norm_fox_g, out_norm_swa_g, tm)
    h2 = _mm_nn("mix_out", on, wout, tm, min(512, D), resid=h1)

    n2 = _rmsnorm_fwd("ffn2_norm", h2, norm_ffn2_g, tm)
    a2, b2, hm2 = _ffn_up("ffn2_up", n2, wg2, wu2, tm)
    y = _ffn_down("ffn2_down", hm2, wd2, h2, tm)
    dy, dyh, sq = _loss_grad("loss_grad", y, target, min(256, T))
    loss = lax.psum(0.5 * sq[0, 0] / D, ("x", "y", "c"))

    (dh2, dg_ffn2, dh2b), dwg2, dwu2, dwd2 = _ffn_backward("ffn2", dy, dyh, n2, a2, b2, hm2, wg2, wu2, wd2, h2, norm_ffn2_g, tm, 1.0)

    dwout = _wgrad_2d("mix_out_wgrad", on, dh2b, min(512, D), min(1024, D))
    do_fox, dg_of = _outnorm_bwd("out_norm_bwd_fox", dh2b, wout, 0, o_fox, out_norm_fox_g, tm)
    do_swa, dg_os = _outnorm_bwd("out_norm_bwd_swa", dh2b, wout, 1, o_swa, out_norm_swa_g, tm)

    dq_f, dk_f, dv_f, dc_a, dc_b, dr_a, dr_b = _fox_bwd("fox_attention_bwd", qk_f, v_f, o_fox, do_fox, ccol, crow, lse_fa, lse_fb, tq, tk)
    dqk_f = jnp.concatenate([dq_f, dk_f], axis=1)
    dqk_raw, dg_fox = _headnorm_bwd("fox_qk_norm_bwd", dqk_f, proj, 0, 2 * HP, fox_gains, HP, tm, 1.0)
    dct = jnp.stack([dc_a.reshape(HP, T), dc_b.reshape(HP, T)], axis=1).reshape(H, T)
    drt = jnp.stack([dr_a.reshape(HP, T), dr_b.reshape(HP, T)], axis=1).reshape(H, T)
    dz_t, db_f = _forget_bwd("forget_gates_bwd", dct, drt, sg_t)

    dq_s, dk_p, dv_p, dsink_a, dsink_b = _swa_bwd("swa_attention_bwd", q_s, k_d, v_d, sinks3, o_swa, do_swa, lse_sa, lse_sb)
    dqs_raw, dg_sq = _headnorm_bwd("swa_q_norm_bwd", dq_s, proj, 3 * HP, HP, swa_q_gains, HP, tm, 1.0, rope=rope)
    dks_raw, dg_sk = _headnorm_bwd("swa_k_norm_bwd", dk_p, proj, 4 * HP, KVB, swa_k_gains, KVB, tm, 1.0, rope=rope, fold=True)
    dvs_raw, _ = _headnorm_bwd("swa_v_fold", dv_p, None, 0, KVB, None, KVB, tm, 1.0, fold=True, norm=False)

    dproj = jnp.concatenate([dqk_raw, dv_f.astype(BF16), dqs_raw, dks_raw, dvs_raw], axis=1)
    dproj_f = jnp.pad(dz_t.T, ((0, 0), (0, LANES - H))).astype(BF16)
    dwin_main = _wgrad_2d("mix_proj_wgrad", u, dproj, min(1024, D), MAIN // 9)
    dwin_f = _wgrad_2d("mix_proj_forget_wgrad", u, dproj_f, min(1024, D), LANES)
    tkb = MAIN // 9
    du = _nt_reduce(
        "mix_bwd_in",
        [(dproj, pl.BlockSpec((tm, tkb), lambda i, r: (i, r)), win_main, pl.BlockSpec((D, tkb), lambda i, r: (0, r)))],
        [(dproj_f, pl.BlockSpec((tm, LANES), lambda i, r: (i, 0)), win_f, pl.BlockSpec((D, LANES), lambda i, r: (0, 0)))],
        T, D, tm, 9)
    dh1, dg_mix, dh1h = _rmsnorm_bwd("mix_norm_bwd", du, h1, norm_mix_g, dh2, min(256, T), 0.5)

    (dx, dg_ffn1), dwg1, dwu1, dwd1 = _ffn_backward("ffn1", dh1, dh1h, n1, a1, b1, hm1, wg1, wu1, wd1, xs, norm_ffn1_g, tm, None)

    dwin_full = jnp.concatenate([dwin_main[:, :F_OFF], dwin_f[:, :H], dwin_main[:, F_OFF:]], axis=1)
    dwin_g = jnp.transpose(dwin_full.reshape(D, N_DEV, n_in), (1, 0, 2))
    big_g = [dwg1, dwu1, dwd1, dwin_g, dwout.reshape(N_DEV, D // N_DEV, D), dwg2, dwu2, dwd2]
    from_sibling = _exchange_sibling(big_g)
    names = ["ffn1_w_gate", "ffn1_w_up", "ffn1_w_down", "w_in", "w_out", "ffn2_w_gate", "ffn2_w_up", "ffn2_w_down"]
    chip_sums = [_pair_add("sum_" + nm, g, r, c_idx) for nm, g, r in zip(names, big_g, from_sibling)]
    from_chips = _exchange_chips(chip_sums)
    big_out = [_adam_shard("adam_" + nm, s, r, w, m, v, chip_idx)
               for nm, s, r, w, m, v in zip(names, chip_sums, from_chips, big_w, big_m, big_v)]

    dsinks = jnp.stack([dsink_a.reshape(HP), dsink_b.reshape(HP)], axis=1).reshape(H)
    small_g = [dg_ffn1, dg_mix, dg_ffn2, dg_of, dg_os, db_f, dg_fox[0, 0, :HEAD_DIM], dg_fox[1, 0, :HEAD_DIM],
               dg_sq[0, 0, :HEAD_DIM], dg_sk[0, 0, :HEAD_DIM], dsinks]
    small_w = [norm_ffn1_g, norm_mix_g, norm_ffn2_g, out_norm_fox_g, out_norm_swa_g, b_forget, fox_q_norm_g, fox_k_norm_g,
               swa_q_norm_g, swa_k_norm_g, swa_sinks]
    small_m = [m_norm_ffn1_g, m_norm_mix_g, m_norm_ffn2_g, m_out_norm_fox_g, m_out_norm_swa_g, m_b_forget, m_fox_q_norm_g,
               m_fox_k_norm_g, m_swa_q_norm_g, m_swa_k_norm_g, m_swa_sinks]
    small_v = [v_norm_ffn1_g, v_norm_mix_g, v_norm_ffn2_g, v_out_norm_fox_g, v_out_norm_swa_g, v_b_forget, v_fox_q_norm_g,
               v_fox_k_norm_g, v_swa_q_norm_g, v_swa_k_norm_g, v_swa_sinks]
    gathered = _gather_small(_pack_small(small_g, D))
    small_out = _adam_small("adam_small", gathered, _pack_small(small_w, D), _pack_small(small_m, D), _pack_small(small_v, D))
    small_out = [_unpack_small(p, D, H) for p in small_out]

    order = ["norm_ffn1_g", "ffn1_w_gate", "ffn1_w_up", "ffn1_w_down", "norm_mix_g", "w_in", "b_forget", "fox_q_norm_g", "fox_k_norm_g",
             "swa_q_norm_g", "swa_k_norm_g", "swa_sinks", "out_norm_fox_g", "out_norm_swa_g", "w_out", "norm_ffn2_g",
             "ffn2_w_gate", "ffn2_w_up", "ffn2_w_down"]
    small_names = ["norm_ffn1_g", "norm_mix_g", "norm_ffn2_g", "out_norm_fox_g", "out_norm_swa_g", "b_forget", "fox_q_norm_g",
                   "fox_k_norm_g", "swa_q_norm_g", "swa_k_norm_g", "swa_sinks"]
    result = [loss, dx[None]]
    for kind in range(4):
        for nm in order:
            if nm in names:
                result.append(big_out[names.index(nm)][kind][None])
            else:
                result.append(small_out[kind][small_names.index(nm)])
    return tuple(result)


def _headnorm_fwd_scaled(name, proj, col_off, ncb, gains, tm, scale, n_scaled):
    T = proj.shape[0]

    def body(x_ref, g_ref, o_ref):
        xv = x_ref[...]
        lo = _lane_lo(xv.shape)
        y = xv * _head_rstd(xv, lo) * g_ref[...]
        y = y * jnp.where(pl.program_id(0) < n_scaled, scale, 1.0)
        o_ref[...] = y.astype(BF16)

    return pl.pallas_call(
        body, out_shape=jax.ShapeDtypeStruct((T, ncb * LANES), BF16), grid=(ncb, T // tm),
        in_specs=[pl.BlockSpec((tm, LANES), lambda c, i: (i, col_off + c)), pl.BlockSpec((None, 1, LANES), lambda c, i: (c, 0, 0))],
        out_specs=pl.BlockSpec((tm, LANES), lambda c, i: (i, c)), name=name, compiler_params=_params(2))(proj, gains)
```

```python
import functools

import jax
import jax.numpy as jnp
from jax import lax
from jax.experimental import pallas as pl
from jax.experimental.pallas import tpu as pltpu

F32 = jnp.float32
BF16 = jnp.bfloat16

HEAD_DIM = 64
LANES = 128
WINDOW = 128
GQA_GROUP = 4
EPS = 1e-6
ROPE_THETA = 10000.0
ADAM_LR = 0.001
ADAM_B1 = 0.9
ADAM_B2 = 0.999
ADAM_EPS = 1e-08
ADAM_WD = 0.01
ADAM_STEP = 10
N_DEV = 8
NEG = -1e30
VMEM_LIMIT_V7X = 48 * 1024 * 1024
ROW_TILE_CAP = 512
MESH = pl.DeviceIdType.MESH

NN = (((1,), (0,)), ((), ()))
NT = (((1,), (1,)), ((), ()))
TN = (((0,), (0,)), ((), ()))


def _dot(a, b, dims):
    return lax.dot_general(a, b, dims, preferred_element_type=F32)


def _params(n_axes):
    return pltpu.CompilerParams(dimension_semantics=("arbitrary",) * n_axes, vmem_limit_bytes=VMEM_LIMIT_V7X)


def _row_tile(rows, cap=ROW_TILE_CAP):
    best = None
    for t in range(16, min(rows, cap) + 1, 16):
        if rows % t == 0:
            best = t
    return best or rows


def _lane_lo(shape):
    return lax.broadcasted_iota(jnp.int32, shape, len(shape) - 1) < HEAD_DIM


def _keep(sel, x):
    return jnp.where(sel, x.astype(F32), 0.0).astype(BF16)


_HBM = pl.BlockSpec(memory_space=pltpu.HBM)


class _Carry:
    def __init__(self, inputs, out_shapes, scratch, start, finish):
        self.inputs, self.out_shapes, self.scratch, self.start, self.finish = list(inputs), list(out_shapes), list(scratch), start, finish


def _call(body, *, name, grid, in_specs, out_specs, out_shape, args, scratch_shapes=(), carry=None):
    params = _params(len(grid))
    if carry is None:
        return pl.pallas_call(body, out_shape=out_shape, grid=grid, in_specs=list(in_specs), out_specs=out_specs,
                              scratch_shapes=list(scratch_shapes), name=name, compiler_params=params)(*args)
    single = not isinstance(out_shape, (tuple, list))
    shapes = (out_shape,) if single else tuple(out_shape)
    specs = (out_specs,) if single else tuple(out_specs)
    n_in, n_out, n_scr = len(args), len(shapes), len(scratch_shapes)
    c_in, c_out = len(carry.inputs), len(carry.out_shapes)

    def wrapped(*refs):
        ins, c_ins = refs[:n_in], refs[n_in:n_in + c_in]
        o0 = n_in + c_in
        outs, c_outs = refs[o0:o0 + n_out], refs[o0 + n_out:o0 + n_out + c_out]
        s0 = o0 + n_out + c_out
        scr, c_scr = refs[s0:s0 + n_scr], refs[s0 + n_scr:]
        first = pl.program_id(0) == 0
        last = pl.program_id(0) == grid[0] - 1
        for ax in range(1, len(grid)):
            first = jnp.logical_and(first, pl.program_id(ax) == 0)
            last = jnp.logical_and(last, pl.program_id(ax) == grid[ax] - 1)

        @pl.when(first)
        def _():
            carry.start(c_ins, c_outs, c_scr)

        body(*ins, *outs, *scr)

        @pl.when(last)
        def _():
            carry.finish(c_ins, c_outs, c_scr)

    res = pl.pallas_call(
        wrapped, out_shape=shapes + tuple(carry.out_shapes), grid=grid, in_specs=list(in_specs) + [_HBM] * c_in,
        out_specs=specs + (_HBM,) * c_out, scratch_shapes=list(scratch_shapes) + carry.scratch, name=name,
        compiler_params=params)(*args, *carry.inputs)
    main = res[:n_out]
    return (main[0] if single else tuple(main)), tuple(res[n_out:])


def _rms_bwd(dn, x, g):
    r = lax.rsqrt(jnp.mean(x * x, axis=-1, keepdims=True) + EPS)
    xh = x * r
    dxh = dn * g
    dx = r * (dxh - xh * jnp.mean(dxh * xh, axis=-1, keepdims=True))
    return dx, jnp.sum(dn * xh, axis=0, keepdims=True)


def _rmsnorm_fwd(name, x, g, tm):
    T, D = x.shape

    def body(x_ref, g_ref, o_ref):
        xf = x_ref[...]
        r = lax.rsqrt(jnp.mean(xf * xf, axis=-1, keepdims=True) + EPS)
        o_ref[...] = (xf * r * g_ref[...]).astype(BF16)

    return pl.pallas_call(
        body, out_shape=jax.ShapeDtypeStruct((T, D), BF16), grid=(T // tm,),
        in_specs=[pl.BlockSpec((tm, D), lambda i: (i, 0)), pl.BlockSpec((1, D), lambda i: (0, 0))],
        out_specs=pl.BlockSpec((tm, D), lambda i: (i, 0)), name=name, compiler_params=_params(1))(x, g)


def _outnorm_fwd(name, o_fox, o_swa, g_fox, g_swa, tm):
    T, Dh = o_fox.shape

    def body(a_ref, b_ref, ga_ref, gb_ref, o_ref):
        for ref, g_ref, lo in ((a_ref, ga_ref, 0), (b_ref, gb_ref, Dh)):
            xf = ref[...]
            r = lax.rsqrt(jnp.mean(xf * xf, axis=-1, keepdims=True) + EPS)
            o_ref[:, lo:lo + Dh] = (xf * r * g_ref[...]).astype(BF16)

    row = pl.BlockSpec((tm, Dh), lambda i: (i, 0))
    gain = pl.BlockSpec((1, Dh), lambda i: (0, 0))
    return pl.pallas_call(
        body, out_shape=jax.ShapeDtypeStruct((T, 2 * Dh), BF16), grid=(T // tm,),
        in_specs=[row, row, gain, gain], out_specs=pl.BlockSpec((tm, 2 * Dh), lambda i: (i, 0)),
        name=name, compiler_params=_params(1))(o_fox, o_swa, g_fox, g_swa)


def _outnorm_bwd(name, dhb, wout, half, o, g, tm):
    T, D = dhb.shape
    Dh = o.shape[1]

    def body(a_ref, w_ref, o_ref, g_ref, do_ref, dg_ref):
        don = _dot(a_ref[...], w_ref[...], NT)
        dx, dg = _rms_bwd(don, o_ref[...], g_ref[...])
        do_ref[...] = dx.astype(BF16)

        @pl.when(pl.program_id(0) == 0)
        def _():
            dg_ref[...] = dg

        @pl.when(pl.program_id(0) > 0)
        def _():
            dg_ref[...] += dg

    return pl.pallas_call(
        body, out_shape=(jax.ShapeDtypeStruct((T, Dh), BF16), jax.ShapeDtypeStruct((1, Dh), F32)), grid=(T // tm,),
        in_specs=[pl.BlockSpec((tm, D), lambda i: (i, 0)), pl.BlockSpec((Dh, D), lambda i: (half, 0)),
                  pl.BlockSpec((tm, Dh), lambda i: (i, 0)), pl.BlockSpec((1, Dh), lambda i: (0, 0))],
        out_specs=(pl.BlockSpec((tm, Dh), lambda i: (i, 0)), pl.BlockSpec((1, Dh), lambda i: (0, 0))),
        name=name, compiler_params=_params(1))(dhb, wout, o, g)


def _mm_nn(name, a, b, tm, tn, resid=None, carry=None):
    M, K = a.shape
    N = b.shape[1]

    def body(*refs):
        if resid is None:
            a_ref, b_ref, o_ref = refs
            o_ref[...] = _dot(a_ref[...], b_ref[...], NN)
        else:
            a_ref, b_ref, r_ref, o_ref = refs
            o_ref[...] = r_ref[...] + _dot(a_ref[...], b_ref[...], NN)

    ospec = pl.BlockSpec((tm, tn), lambda n, i: (i, n))
    in_specs = [pl.BlockSpec((tm, K), lambda n, i: (i, 0)), pl.BlockSpec((K, tn), lambda n, i: (0, n))]
    args = [a, b]
    if resid is not None:
        in_specs.append(ospec)
        args.append(resid)
    return _call(body, name=name, grid=(N // tn, M // tm), in_specs=in_specs, out_specs=ospec,
                 out_shape=jax.ShapeDtypeStruct((M, N), F32), args=args, carry=carry)


def _wgrad_2d(name, a, b, tmm, tn):
    T, M = a.shape
    N = b.shape[1]

    def body(a_ref, b_ref, o_ref):
        o_ref[...] = _dot(a_ref[...], b_ref[...], TN).astype(BF16)

    return pl.pallas_call(
        body, out_shape=jax.ShapeDtypeStruct((M, N), BF16), grid=(M // tmm, N // tn),
        in_specs=[pl.BlockSpec((T, tmm), lambda m, n: (0, m)), pl.BlockSpec((T, tn), lambda m, n: (0, n))],
        out_specs=pl.BlockSpec((tmm, tn), lambda m, n: (m, n)), name=name, compiler_params=_params(2))(a, b)


def _wgrad_down(name, hm, df, tn, carry=None):
    J, T, Fs = hm.shape
    D = df.shape[1]

    def body(a_ref, b_ref, o_ref):
        o_ref[...] = _dot(a_ref[...], b_ref[...], TN).astype(BF16)

    return _call(
        body, name=name, grid=(J, D // tn), out_shape=jax.ShapeDtypeStruct((J, Fs, D), BF16),
        in_specs=[pl.BlockSpec((None, T, Fs), lambda j, n: (j, 0, 0)), pl.BlockSpec((T, tn), lambda j, n: (0, n))],
        out_specs=pl.BlockSpec((None, Fs, tn), lambda j, n: (j, 0, n)), args=[hm, df], carry=carry)


def _wgrad_up(name, n, da, db, tmm, carry=None):
    T, D = n.shape
    J, _, Fs = da.shape

    def body(n_ref, da_ref, db_ref, og_ref, ou_ref):
        nv = n_ref[...]
        og_ref[...] = _dot(nv, da_ref[...], TN).astype(BF16)
        ou_ref[...] = _dot(nv, db_ref[...], TN).astype(BF16)

    act = pl.BlockSpec((None, T, Fs), lambda j, m: (j, 0, 0))
    out = pl.BlockSpec((None, tmm, Fs), lambda j, m: (j, m, 0))
    shape = jax.ShapeDtypeStruct((J, D, Fs), BF16)
    return _call(
        body, name=name, grid=(J, D // tmm), out_shape=(shape, shape),
        in_specs=[pl.BlockSpec((T, tmm), lambda j, m: (0, m)), act, act], out_specs=(out, out),
        args=[n, da, db], carry=carry)


def _nt_reduce(name, pairs, once, T, D, tm, steps, carry=None):
    n_pairs = len(pairs)
    n_once = len(once)

    def body(*refs):
        pr = refs[:2 * n_pairs]
        on = refs[2 * n_pairs:2 * (n_pairs + n_once)]
        o_ref, acc = refs[-2:]
        r = pl.program_id(1)
        part = _dot(pr[0][...], pr[1][...], NT)
        for p in range(1, n_pairs):
            part = part + _dot(pr[2 * p][...], pr[2 * p + 1][...], NT)

        @pl.when(r == 0)
        def _():
            acc[...] = part

        @pl.when(r > 0)
        def _():
            acc[...] += part

        @pl.when(r == steps - 1)
        def _():
            dn = acc[...]
            for p in range(n_once):
                dn = dn + _dot(on[2 * p][...], on[2 * p + 1][...], NT)
            o_ref[...] = dn

    in_specs, args = [], []
    for a, a_spec, w, w_spec in list(pairs) + list(once):
        in_specs += [a_spec, w_spec]
        args += [a, w]
    row = pl.BlockSpec((tm, D), lambda i, r: (i, 0))
    return _call(body, name=name, grid=(T // tm, steps), in_specs=in_specs, out_specs=row, out_shape=jax.ShapeDtypeStruct((T, D), F32),
                 args=args, scratch_shapes=[pltpu.VMEM((tm, D), F32)], carry=carry)


def _rmsnorm_bwd(name, dn, x, g, dh, tm, bf16_scale):
    T, D = x.shape
    emit_bf16 = bf16_scale is not None

    def body(dn_ref, x_ref, g_ref, dh_ref, *outs):
        dxn, dg = _rms_bwd(dn_ref[...], x_ref[...], g_ref[...])
        dx = dh_ref[...] + dxn
        outs[0][...] = dx
        if emit_bf16:
            outs[2][...] = (bf16_scale * dx).astype(BF16)

        @pl.when(pl.program_id(0) == 0)
        def _():
            outs[1][...] = dg

        @pl.when(pl.program_id(0) > 0)
        def _():
            outs[1][...] += dg

    row = pl.BlockSpec((tm, D), lambda i: (i, 0))
    gain = pl.BlockSpec((1, D), lambda i: (0, 0))
    out_shape = [jax.ShapeDtypeStruct((T, D), F32), jax.ShapeDtypeStruct((1, D), F32)]
    out_specs = [row, gain]
    if emit_bf16:
        out_shape.append(jax.ShapeDtypeStruct((T, D), BF16))
        out_specs.append(row)
    return pl.pallas_call(
        body, out_shape=tuple(out_shape), grid=(T // tm,), in_specs=[row, row, gain, row], out_specs=tuple(out_specs),
        name=name, compiler_params=_params(1))(dn, x, g, dh)


def _loss_grad(name, y, target, tm):
    T, D = y.shape

    def body(y_ref, t_ref, dy_ref, dyh_ref, sq_ref):
        diff = y_ref[...] - t_ref[...]
        sq = jnp.sum(jnp.sum(diff * diff, axis=1, keepdims=True), axis=0, keepdims=True)
        dy = diff * (1.0 / D)
        dy_ref[...] = dy
        dyh_ref[...] = (0.5 * dy).astype(BF16)

        @pl.when(pl.program_id(0) == 0)
        def _():
            sq_ref[...] = sq

        @pl.when(pl.program_id(0) > 0)
        def _():
            sq_ref[...] += sq

    row = pl.BlockSpec((tm, D), lambda i: (i, 0))
    return pl.pallas_call(
        body, out_shape=(jax.ShapeDtypeStruct((T, D), F32), jax.ShapeDtypeStruct((T, D), BF16), jax.ShapeDtypeStruct((1, 1), F32)),
        grid=(T // tm,), in_specs=[row, row], out_specs=(row, row, pl.BlockSpec((1, 1), lambda i: (0, 0))),
        name=name, compiler_params=_params(1))(y, target)


def _ffn_up(name, n, wg, wu, tm, carry=None):
    T, D = n.shape
    J, _, Fs = wg.shape

    def body(n_ref, wg_ref, wu_ref, a_ref, b_ref, h_ref):
        xv = n_ref[...]
        a = _dot(xv, wg_ref[...], NN)
        b = _dot(xv, wu_ref[...], NN)
        a_ref[...] = a.astype(BF16)
        b_ref[...] = b.astype(BF16)
        h_ref[...] = (a * jax.nn.sigmoid(a) * b).astype(BF16)

    act = jax.ShapeDtypeStruct((J, T, Fs), BF16)
    wspec = pl.BlockSpec((None, D, Fs), lambda j, i: (j, 0, 0))
    aspec = pl.BlockSpec((None, tm, Fs), lambda j, i: (j, i, 0))
    return _call(
        body, name=name, grid=(J, T // tm), out_shape=(act, act, act),
        in_specs=[pl.BlockSpec((tm, D), lambda j, i: (i, 0)), wspec, wspec], out_specs=(aspec, aspec, aspec),
        args=[n, wg, wu], carry=carry)


def _ffn_down(name, hm, wd, resid, tm, carry=None):
    J, T, Fs = hm.shape
    D = wd.shape[2]

    def body(h_ref, w_ref, r_ref, o_ref, acc):
        j = pl.program_id(1)
        part = _dot(h_ref[...], w_ref[...], NN)

        @pl.when(j == 0)
        def _():
            acc[...] = part

        @pl.when(j > 0)
        def _():
            acc[...] += part

        @pl.when(j == J - 1)
        def _():
            o_ref[...] = r_ref[...] + 0.5 * acc[...]

    row = pl.BlockSpec((tm, D), lambda i, j: (i, 0))
    return _call(
        body, name=name, grid=(T // tm, J), out_shape=jax.ShapeDtypeStruct((T, D), F32),
        in_specs=[pl.BlockSpec((None, tm, Fs), lambda i, j: (j, i, 0)), pl.BlockSpec((None, Fs, D), lambda i, j: (j, 0, 0)), row],
        out_specs=row, scratch_shapes=[pltpu.VMEM((tm, D), F32)], args=[hm, wd, resid], carry=carry)


def _ffn_bwd_mid(name, dfh, wd, a, b, tm, carry=None):
    T, D = dfh.shape
    J, Fs, _ = wd.shape

    def body(df_ref, w_ref, a_ref, b_ref, da_ref, db_ref):
        dhm = _dot(df_ref[...], w_ref[...], NT)
        av = a_ref[...].astype(F32)
        bv = b_ref[...].astype(F32)
        sg = jax.nn.sigmoid(av)
        da_ref[...] = (dhm * bv * (sg * (1.0 + av * (1.0 - sg)))).astype(BF16)
        db_ref[...] = (dhm * (av * sg)).astype(BF16)

    act = jax.ShapeDtypeStruct((J, T, Fs), BF16)
    aspec = pl.BlockSpec((None, tm, Fs), lambda j, i: (j, i, 0))
    return _call(
        body, name=name, grid=(J, T // tm), out_shape=(act, act),
        in_specs=[pl.BlockSpec((tm, D), lambda j, i: (i, 0)), pl.BlockSpec((None, Fs, D), lambda j, i: (j, 0, 0)), aspec, aspec],
        out_specs=(aspec, aspec), args=[dfh, wd, a, b], carry=carry)


def _rot_half(y, lane):
    first = (lane & (HEAD_DIM // 2)) == 0
    return jnp.where(first, pltpu.roll(y, LANES - HEAD_DIM // 2, 1), pltpu.roll(y, HEAD_DIM // 2, 1))


def _head_rstd(x, lo):
    sq = x * x
    ss_a = jnp.sum(jnp.where(lo, sq, 0.0), axis=-1, keepdims=True)
    ss_b = jnp.sum(jnp.where(lo, 0.0, sq), axis=-1, keepdims=True)
    return lax.rsqrt(jnp.where(lo, ss_a, ss_b) * (1.0 / HEAD_DIM) + EPS)


def _headnorm_fwd(name, proj, col_off, ncb, gains, tm, scale, rope=None, dup=False):
    T = proj.shape[0]
    with_rope = rope is not None
    width = 2 * LANES if dup else LANES

    def body(*refs):
        if with_rope:
            x_ref, g_ref, cos_ref, sin_ref, o_ref = refs
        else:
            x_ref, g_ref, o_ref = refs
        xv = x_ref[...]
        lane = lax.broadcasted_iota(jnp.int32, xv.shape, 1)
        lo = lane < HEAD_DIM
        y = xv * _head_rstd(xv, lo) * g_ref[...]
        if with_rope:
            y = y * cos_ref[...] + _rot_half(y, lane) * sin_ref[...]
        y = y * scale
        if dup:
            sw = pltpu.roll(y, HEAD_DIM, 1)
            o_ref[:, :LANES] = jnp.where(lo, y, sw).astype(BF16)
            o_ref[:, LANES:] = jnp.where(lo, sw, y).astype(BF16)
        else:
            o_ref[...] = y.astype(BF16)

    in_specs = [pl.BlockSpec((tm, LANES), lambda c, i: (i, col_off + c)), pl.BlockSpec((None, 1, LANES), lambda c, i: (c, 0, 0))]
    args = [proj, gains]
    if with_rope:
        tab = pl.BlockSpec((tm, LANES), lambda c, i: (i, 0))
        in_specs += [tab, tab]
        args += list(rope)
    return pl.pallas_call(
        body, out_shape=jax.ShapeDtypeStruct((T, ncb * width), BF16), grid=(ncb, T // tm),
        in_specs=in_specs, out_specs=pl.BlockSpec((tm, width), lambda c, i: (i, c)),
        name=name, compiler_params=_params(2))(*args)


def _headnorm_bwd(name, dy, proj, col_off, ncb, gains, group, tm, scale, rope=None, fold=False, norm=True):
    T = dy.shape[0]
    with_rope = rope is not None
    n_groups = ncb // group
    dy_width = 4 * LANES if fold else LANES

    def body(*refs):
        refs = list(refs)
        dy_ref = refs.pop(0)
        x_ref = refs.pop(0) if norm else None
        g_ref = refs.pop(0) if norm else None
        cos_ref = refs.pop(0) if with_rope else None
        sin_ref = refs.pop(0) if with_rope else None
        dx_ref = refs.pop(0)
        dg_ref = refs.pop(0) if norm else None
        c = pl.program_id(0)
        i = pl.program_id(1)
        d = dy_ref[...]
        lane = lax.broadcasted_iota(jnp.int32, (d.shape[0], LANES), 1)
        lo = lane < HEAD_DIM
        if fold:
            t0 = d[:, 0:LANES] + d[:, LANES:2 * LANES]
            t1 = d[:, 2 * LANES:3 * LANES] + d[:, 3 * LANES:4 * LANES]
            d = jnp.where(lo, t0 + pltpu.roll(t0, HEAD_DIM, 1), t1 + pltpu.roll(t1, HEAD_DIM, 1))
        d = d * scale
        if with_rope:
            d = d * cos_ref[...] + _rot_half(d * sin_ref[...], lane)
        if not norm:
            dx_ref[...] = d.astype(BF16)
            return
        xv = x_ref[...]
        gv = g_ref[...]
        r = _head_rstd(xv, lo)
        xh = xv * r
        dxh = d * gv
        pr = dxh * xh
        m_a = jnp.sum(jnp.where(lo, pr, 0.0), axis=-1, keepdims=True)
        m_b = jnp.sum(jnp.where(lo, 0.0, pr), axis=-1, keepdims=True)
        mean = jnp.where(lo, m_a, m_b) * (1.0 / HEAD_DIM)
        dx_ref[...] = (r * (dxh - xh * mean)).astype(BF16)
        dgp = jnp.sum(d * xh, axis=0, keepdims=True)
        dgp = dgp + pltpu.roll(dgp, HEAD_DIM, 1)
        first = jnp.logical_and(c % group == 0, i == 0)

        @pl.when(first)
        def _():
            dg_ref[...] = dgp

        @pl.when(jnp.logical_not(first))
        def _():
            dg_ref[...] += dgp

    in_specs = [pl.BlockSpec((tm, dy_width), lambda c, i: (i, c))]
    args = [dy]
    if norm:
        in_specs += [pl.BlockSpec((tm, LANES), lambda c, i: (i, col_off + c)), pl.BlockSpec((None, 1, LANES), lambda c, i: (c, 0, 0))]
        args += [proj, gains]
    if with_rope:
        tab = pl.BlockSpec((tm, LANES), lambda c, i: (i, 0))
        in_specs += [tab, tab]
        args += list(rope)
    out_shape = [jax.ShapeDtypeStruct((T, ncb * LANES), BF16)]
    out_specs = [pl.BlockSpec((tm, LANES), lambda c, i: (i, c))]
    if norm:
        out_shape.append(jax.ShapeDtypeStruct((n_groups, 1, LANES), F32))
        out_specs.append(pl.BlockSpec((None, 1, LANES), lambda c, i: (c // group, 0, 0)))
    res = pl.pallas_call(
        body, out_shape=tuple(out_shape), grid=(ncb, T // tm), in_specs=in_specs, out_specs=tuple(out_specs),
        name=name, compiler_params=_params(2))(*args)
    return res if norm else (res[0], None)


def _dot_exact(x, tri):
    hi = x.astype(BF16)
    r1 = x - hi.astype(F32)
    mid = r1.astype(BF16)
    lo = (r1 - mid.astype(F32)).astype(BF16)
    return _dot(hi, tri, NN) + _dot(mid, tri, NN) + _dot(lo, tri, NN)


def _forget_fwd(name, zt, bias):
    H, T = zt.shape
    blk = min(256, T)

    def body(z_ref, b_ref, c_ref, s_ref):
        z = z_ref[...] + b_ref[...]
        s_ref[...] = jax.nn.sigmoid(-z)
        lf = jnp.minimum(z, 0.0) - jnp.log(1.0 + jnp.exp(-jnp.abs(z)))
        tri = (lax.broadcasted_iota(jnp.int32, (blk, blk), 0) <= lax.broadcasted_iota(jnp.int32, (blk, blk), 1)).astype(BF16)
        carry = jnp.zeros((H, 1), F32)
        for bi in range(T // blk):
            xb = lf[:, bi * blk:(bi + 1) * blk]
            c_ref[:, bi * blk:(bi + 1) * blk] = _dot_exact(xb, tri) + carry
            carry = carry + jnp.sum(xb, axis=-1, keepdims=True)

    shape = jax.ShapeDtypeStruct((H, T), F32)
    full = pl.BlockSpec((H, T), lambda i: (0, 0))
    return pl.pallas_call(
        body, out_shape=(shape, shape), grid=(1,), in_specs=[full, pl.BlockSpec((H, 1), lambda i: (0, 0))],
        out_specs=(full, full), name=name, compiler_params=_params(1))(zt, bias)


def _forget_bwd(name, dct, drt, sgt):
    H, T = dct.shape
    blk = min(256, T)

    def body(dc_ref, dr_ref, s_ref, dz_ref, db_ref):
        dc = dc_ref[...] + dr_ref[...]
        tri = (lax.broadcasted_iota(jnp.int32, (blk, blk), 0) >= lax.broadcasted_iota(jnp.int32, (blk, blk), 1)).astype(BF16)
        carry = jnp.zeros((H, 1), F32)
        db = jnp.zeros((H, 1), F32)
        for bi in reversed(range(T // blk)):
            xb = dc[:, bi * blk:(bi + 1) * blk]
            dz = (_dot_exact(xb, tri) + carry) * s_ref[:, bi * blk:(bi + 1) * blk]
            dz_ref[:, bi * blk:(bi + 1) * blk] = dz
            db = db + jnp.sum(dz, axis=-1, keepdims=True)
            carry = carry + jnp.sum(xb, axis=-1, keepdims=True)
        db_ref[...] = db

    full = pl.BlockSpec((H, T), lambda i: (0, 0))
    return pl.pallas_call(
        body, out_shape=(jax.ShapeDtypeStruct((H, T), F32), jax.ShapeDtypeStruct((H, 1), F32)), grid=(1,),
        in_specs=[full, full, full], out_specs=(full, pl.BlockSpec((H, 1), lambda i: (0, 0))),
        name=name, compiler_params=_params(1))(dct, drt, sgt)


def _fox_fwd(name, qk, v, ccol, crow, tq, tk, carry=None):
    T, Dh = v.shape
    HP = Dh // LANES
    nk = T // tk

    def body(q_ref, k_ref, v_ref, ca_ref, cb_ref, ra_ref, rb_ref, o_ref, la_ref, lb_ref):
        i = pl.program_id(1)
        q2 = q_ref[...]
        lo = _lane_lo((tq, LANES))
        rows = i * tq + lax.broadcasted_iota(jnp.int32, (tq, tk), 0)
        col0 = lax.broadcasted_iota(jnp.int32, (tq, tk), 1)
        n_chunks = ((i + 1) * tq + tk - 1) // tk
        res = []
        for sel, c_ref, r_ref in ((lo, ca_ref, ra_ref), (jnp.logical_not(lo), cb_ref, rb_ref)):
            qm = _keep(sel, q2)
            ct = c_ref[...]

            def step(kc, carry, qm=qm, ct=ct, r_ref=r_ref):
                m, l, acc = carry
                start = pl.multiple_of(kc * tk, tk)
                kb = k_ref[pl.ds(start, tk), :]
                vb = v_ref[pl.ds(start, tk), :]
                s = _dot(qm, kb, NT) + ct - r_ref[kc]
                s = jnp.where(rows >= col0 + start, s, NEG)
                mn = jnp.maximum(m, jnp.max(s, axis=-1, keepdims=True))
                p = jnp.exp(s - mn)
                alpha = jnp.exp(m - mn)
                l = alpha * l + jnp.sum(p, axis=-1, keepdims=True)
                acc = alpha * acc + _dot(p.astype(BF16), vb, NN)
                return mn, l, acc

            m, l, acc = lax.fori_loop(0, n_chunks, step, (jnp.full((tq, 1), NEG, F32), jnp.zeros((tq, 1), F32), jnp.zeros((tq, LANES), F32)))
            res.append((acc / l, m + jnp.log(l)))
        o_ref[...] = jnp.where(lo, res[0][0], res[1][0])
        la_ref[...] = res[0][1]
        lb_ref[...] = res[1][1]

    col = lambda off: pl.BlockSpec((None, tq, 1), lambda h, i: (2 * h + off, i, 0))
    row = lambda off: pl.BlockSpec((None, nk, 1, tk), lambda h, i: (2 * h + off, 0, 0, 0))
    lse = jax.ShapeDtypeStruct((HP, T, 1), F32)
    lspec = pl.BlockSpec((None, tq, 1), lambda h, i: (h, i, 0))
    return _call(
        body, name=name, grid=(HP, T // tq), out_shape=(jax.ShapeDtypeStruct((T, Dh), F32), lse, lse),
        in_specs=[pl.BlockSpec((tq, LANES), lambda h, i: (i, h)), pl.BlockSpec((T, LANES), lambda h, i: (0, HP + h)),
                  pl.BlockSpec((T, LANES), lambda h, i: (0, h)), col(0), col(1), row(0), row(1)],
        out_specs=(pl.BlockSpec((tq, LANES), lambda h, i: (i, h)), lspec, lspec),
        args=[qk, qk, v, ccol, ccol, crow, crow], carry=carry)


def _fox_bwd(name, qk, v, o, do, ccol, crow, lse_a, lse_b, tq, tk, carry=None):
    T, Dh = v.shape
    HP = Dh // LANES
    nk = T // tk
    scale = HEAD_DIM ** -0.5

    def body(q_ref, k_ref, v_ref, o_ref, do_ref, ca_ref, cb_ref, ra_ref, rb_ref, la_ref, lb_ref,
             dq_ref, dk_ref, dv_ref, dca_ref, dcb_ref, dra_ref, drb_ref):
        i = pl.program_id(1)

        @pl.when(i == 0)
        def _():
            dk_ref[...] = jnp.zeros_like(dk_ref)
            dv_ref[...] = jnp.zeros_like(dv_ref)
            dca_ref[...] = jnp.zeros_like(dca_ref)
            dcb_ref[...] = jnp.zeros_like(dcb_ref)

        q2 = q_ref[...]
        do2 = do_ref[...]
        lo = _lane_lo((tq, LANES))
        hi = jnp.logical_not(lo)
        prod = do2.astype(F32) * o_ref[...]
        rows = i * tq + lax.broadcasted_iota(jnp.int32, (tq, tk), 0)
        col0 = lax.broadcasted_iota(jnp.int32, (tq, tk), 1)
        n_chunks = ((i + 1) * tq + tk - 1) // tk
        dqs = []
        for sel, c_ref, r_ref, l_ref, dc_ref, dr_ref in ((lo, ca_ref, ra_ref, la_ref, dca_ref, dra_ref),
                                                         (hi, cb_ref, rb_ref, lb_ref, dcb_ref, drb_ref)):
            qm = _keep(sel, q2)
            dom = _keep(sel, do2)
            dsum = jnp.sum(jnp.where(sel, prod, 0.0), axis=-1, keepdims=True)
            ct = c_ref[...]
            lse = l_ref[...]

            def step(kc, carry, qm=qm, dom=dom, dsum=dsum, ct=ct, lse=lse, r_ref=r_ref, dc_ref=dc_ref):
                dq, dr = carry
                start = pl.multiple_of(kc * tk, tk)
                kb = k_ref[pl.ds(start, tk), :]
                vb = v_ref[pl.ds(start, tk), :]
                s = _dot(qm, kb, NT) + ct - r_ref[kc]
                s = jnp.where(rows >= col0 + start, s, NEG)
                p = jnp.exp(s - lse)
                ds = p * (_dot(dom, vb, NT) - dsum)
                dsb = ds.astype(BF16)
                dk_ref[pl.ds(start, tk), :] += _dot(dsb, qm, TN)
                dv_ref[pl.ds(start, tk), :] += _dot(p.astype(BF16), dom, TN)
                dc_ref[kc] = dc_ref[kc] - jnp.sum(ds, axis=0, keepdims=True)
                return dq + _dot(dsb, kb, NN), dr + jnp.sum(ds, axis=-1, keepdims=True)

            dq, dr = lax.fori_loop(0, n_chunks, step, (jnp.zeros((tq, LANES), F32), jnp.zeros((tq, 1), F32)))
            dqs.append(dq)
            dr_ref[...] = dr
        dq_ref[...] = jnp.where(lo, dqs[0], dqs[1]) * scale

    col = lambda off: pl.BlockSpec((None, tq, 1), lambda h, i: (2 * h + off, i, 0))
    row = lambda off: pl.BlockSpec((None, nk, 1, tk), lambda h, i: (2 * h + off, 0, 0, 0))
    lspec = pl.BlockSpec((None, tq, 1), lambda h, i: (h, i, 0))
    qspec = pl.BlockSpec((tq, LANES), lambda h, i: (i, h))
    full = pl.BlockSpec((T, LANES), lambda h, i: (0, h))
    dcspec = pl.BlockSpec((None, nk, 1, tk), lambda h, i: (h, 0, 0, 0))
    grad = jax.ShapeDtypeStruct((T, Dh), F32)
    dc = jax.ShapeDtypeStruct((HP, nk, 1, tk), F32)
    dr = jax.ShapeDtypeStruct((HP, T, 1), F32)
    return _call(
        body, name=name, grid=(HP, T // tq), out_shape=(grad, grad, grad, dc, dc, dr, dr),
        in_specs=[qspec, pl.BlockSpec((T, LANES), lambda h, i: (0, HP + h)), full, qspec, qspec,
                  col(0), col(1), row(0), row(1), lspec, lspec],
        out_specs=(qspec, full, full, dcspec, dcspec, lspec, lspec),
        args=[qk, qk, v, o, do, ccol, ccol, crow, crow, lse_a, lse_b], carry=carry)


def _swa_block(n, q_ref, k_ref):
    qs = pl.multiple_of(n * WINDOW, WINDOW)
    ks = pl.multiple_of(jnp.maximum(n - 1, 0) * WINDOW, WINDOW)
    rel = (qs + lax.broadcasted_iota(jnp.int32, (WINDOW, 2 * WINDOW), 0)) - (ks + lax.broadcasted_iota(jnp.int32, (WINDOW, 2 * WINDOW), 1))
    valid = jnp.logical_and(rel >= 0, rel < WINDOW)
    return qs, ks, valid


def _swa_fwd(name, q, kd, vd, sinks, carry=None):
    T, Dh = q.shape
    HP = Dh // LANES

    def body(q_ref, k_ref, v_ref, sa_ref, sb_ref, o_ref, la_ref, lb_ref):
        lo = _lane_lo((WINDOW, LANES))

        def block(n, _):
            qs, ks, valid = _swa_block(n, q_ref, k_ref)
            q2 = q_ref[pl.ds(qs, WINDOW), :]
            kb = k_ref[pl.ds(ks, 2 * WINDOW), :]
            vb = v_ref[pl.ds(ks, 2 * WINDOW), :]
            res = []
            for sel, s_ref in ((lo, sa_ref), (jnp.logical_not(lo), sb_ref)):
                qm = _keep(sel, q2)
                sink = s_ref[...]
                s = jnp.where(valid, _dot(qm, kb, NT), NEG)
                m = jnp.maximum(jnp.max(s, axis=-1, keepdims=True), sink)
                p = jnp.exp(s - m)
                l = jnp.sum(p, axis=-1, keepdims=True) + jnp.exp(sink - m)
                res.append((_dot(p.astype(BF16), vb, NN) / l, m + jnp.log(l)))
            o_ref[pl.ds(qs, WINDOW), :] = jnp.where(lo, res[0][0], res[1][0])
            la_ref[pl.ds(qs, WINDOW), :] = res[0][1]
            lb_ref[pl.ds(qs, WINDOW), :] = res[1][1]
            return 0

        lax.fori_loop(0, T // WINDOW, block, 0)

    full = pl.BlockSpec((T, LANES), lambda h: (0, h))
    kv = pl.BlockSpec((T, LANES), lambda h: (0, h // 2))
    sink = lambda off: pl.BlockSpec((None, 1, 1), lambda h: (2 * h + off, 0, 0))
    lse = jax.ShapeDtypeStruct((HP, T, 1), F32)
    lspec = pl.BlockSpec((None, T, 1), lambda h: (h, 0, 0))
    return _call(
        body, name=name, grid=(HP,), out_shape=(jax.ShapeDtypeStruct((T, Dh), F32), lse, lse),
        in_specs=[full, kv, kv, sink(0), sink(1)], out_specs=(full, lspec, lspec),
        args=[q, kd, vd, sinks, sinks], carry=carry)


def _swa_bwd(name, q, kd, vd, sinks, o, do, lse_a, lse_b, carry=None):
    T, Dh = q.shape
    HP = Dh // LANES
    scale = HEAD_DIM ** -0.5

    def body(q_ref, k_ref, v_ref, sa_ref, sb_ref, o_ref, do_ref, la_ref, lb_ref, dq_ref, dk_ref, dv_ref, dsa_ref, dsb_ref):
        lo = _lane_lo((WINDOW, LANES))
        hi = jnp.logical_not(lo)
        dk_ref[...] = jnp.zeros_like(dk_ref)
        dv_ref[...] = jnp.zeros_like(dv_ref)

        def block(n, dsinks):
            qs, ks, valid = _swa_block(n, q_ref, k_ref)
            q2 = q_ref[pl.ds(qs, WINDOW), :]
            do2 = do_ref[pl.ds(qs, WINDOW), :]
            kb = k_ref[pl.ds(ks, 2 * WINDOW), :]
            vb = v_ref[pl.ds(ks, 2 * WINDOW), :]
            prod = do2.astype(F32) * o_ref[pl.ds(qs, WINDOW), :]
            dqs, new = [], []
            dk = jnp.zeros((2 * WINDOW, LANES), F32)
            dv = jnp.zeros((2 * WINDOW, LANES), F32)
            for sel, s_ref, l_ref, dsink in ((lo, sa_ref, la_ref, dsinks[0]), (hi, sb_ref, lb_ref, dsinks[1])):
                qm = _keep(sel, q2)
                dom = _keep(sel, do2)
                dsum = jnp.sum(jnp.where(sel, prod, 0.0), axis=-1, keepdims=True)
                lse = l_ref[pl.ds(qs, WINDOW), :]
                s = jnp.where(valid, _dot(qm, kb, NT), NEG)
                p = jnp.exp(s - lse)
                ds = p * (_dot(dom, vb, NT) - dsum)
                dsb = ds.astype(BF16)
                dqs.append(_dot(dsb, kb, NN))
                dk = dk + _dot(dsb, qm, TN)
                dv = dv + _dot(p.astype(BF16), dom, TN)
                new.append(dsink - jnp.sum(jnp.exp(s_ref[...] - lse) * dsum, axis=0, keepdims=True))
            dq_ref[pl.ds(qs, WINDOW), :] = jnp.where(lo, dqs[0], dqs[1]) * scale
            dk_ref[pl.ds(ks, 2 * WINDOW), :] += dk
            dv_ref[pl.ds(ks, 2 * WINDOW), :] += dv
            return tuple(new)

        dsa, dsb_ = lax.fori_loop(0, T // WINDOW, block, (jnp.zeros((1, 1), F32), jnp.zeros((1, 1), F32)))
        dsa_ref[...] = dsa
        dsb_ref[...] = dsb_

    full = pl.BlockSpec((T, LANES), lambda h: (0, h))
    kv = pl.BlockSpec((T, LANES), lambda h: (0, h // 2))
    sink = lambda off: pl.BlockSpec((None, 1, 1), lambda h: (2 * h + off, 0, 0))
    lspec = pl.BlockSpec((None, T, 1), lambda h: (h, 0, 0))
    dsink = pl.BlockSpec((None, 1, 1), lambda h: (h, 0, 0))
    grad = jax.ShapeDtypeStruct((T, Dh), F32)
    ds_shape = jax.ShapeDtypeStruct((HP, 1, 1), F32)
    return _call(
        body, name=name, grid=(HP,), out_shape=(grad, grad, grad, ds_shape, ds_shape),
        in_specs=[full, kv, kv, sink(0), sink(1), full, full, lspec, lspec],
        out_specs=(full, full, full, dsink, dsink),
        args=[q, kd, vd, sinks, sinks, o, do, lse_a, lse_b], carry=carry)


def _place():
    return lax.axis_index("x"), lax.axis_index("y"), lax.axis_index("c")


def _run_carry(name, carry):
    c_in, c_out = len(carry.inputs), len(carry.out_shapes)

    def body(*refs):
        ins, outs, scr = refs[:c_in], refs[c_in:c_in + c_out], refs[c_in + c_out:]
        carry.start(ins, outs, scr)
        carry.finish(ins, outs, scr)

    return pl.pallas_call(
        body, out_shape=tuple(carry.out_shapes), in_specs=[_HBM] * c_in, out_specs=tuple([_HBM] * c_out),
        scratch_shapes=carry.scratch, name=name)(*carry.inputs)


def _gather_carry(shards):
    n = len(shards)

    def plan(ins, outs, scr):
        send, recv, local = scr
        x, y, c = _place()
        me, sibling = (x, y, c), (x, y, 1 - c)
        chips = [(1 - x, y), (x, 1 - y), (1 - x, 1 - y)]

        def copy(w, k, block, to, src=None):
            slot = 4 * block[0] + 2 * block[1] + block[2]
            return pltpu.make_async_remote_copy(
                src_ref=outs[w].at[slot] if src is None else src, dst_ref=outs[w].at[slot],
                send_sem=send.at[w, k], recv_sem=recv.at[w, k], device_id=to, device_id_type=MESH)

        own = [pltpu.make_async_copy(ins[w], outs[w].at[4 * x + 2 * y + c], local.at[w]) for w in range(n)]
        first = []
        for w in range(n):
            first.append(copy(w, 0, me, sibling, src=ins[w]))
            first += [copy(w, 1 + j, me, (*chip, c), src=ins[w]) for j, chip in enumerate(chips)]
        return copy, own, first, me, sibling, chips, c

    def start(ins, outs, scr):
        _, own, first, _, _, _, _ = plan(ins, outs, scr)
        for cp in own + first:
            cp.start()

    def finish(ins, outs, scr):
        copy, own, first, me, sibling, chips, c = plan(ins, outs, scr)
        passed = []
        for w in range(n):
            for j, chip in enumerate(chips):
                copy(w, 1 + j, (*chip, c), me).wait_recv()
                fwd = copy(w, 4 + j, (*chip, c), sibling)
                fwd.start()
                passed.append(fwd)
        for w in range(n):
            copy(w, 0, sibling, me).wait_recv()
            for j, chip in enumerate(chips):
                copy(w, 4 + j, (*chip, 1 - c), me).wait_recv()
        for cp in first + passed:
            cp.wait_send()
        for cp in own:
            cp.wait()

    return _Carry(shards, [jax.ShapeDtypeStruct((N_DEV,) + s.shape, s.dtype) for s in shards],
                  [pltpu.SemaphoreType.DMA((n, 7)), pltpu.SemaphoreType.DMA((n, 7)), pltpu.SemaphoreType.DMA((n,))], start, finish)


def _sibling_carry(grads):
    n = len(grads)

    def copies(ins, outs, scr):
        send, recv = scr
        x, y, c = _place()
        return [pltpu.make_async_remote_copy(
            src_ref=ins[w].at[2 * q + (1 - c)], dst_ref=outs[w].at[q], send_sem=send.at[w, q], recv_sem=recv.at[w, q],
            device_id=(x, y, 1 - c), device_id_type=MESH) for w in range(n) for q in range(4)]

    def start(ins, outs, scr):
        for cp in copies(ins, outs, scr):
            cp.start()

    def finish(ins, outs, scr):
        for cp in copies(ins, outs, scr):
            cp.wait()

    return _Carry(grads, [jax.ShapeDtypeStruct((4,) + g.shape[1:], g.dtype) for g in grads],
                  [pltpu.SemaphoreType.DMA((n, 4)), pltpu.SemaphoreType.DMA((n, 4))], start, finish)


def _chips_carry(sums):
    n = len(sums)

    def copies(ins, outs, scr):
        send, recv = scr
        x, y, c = _place()
        chips = [(1 - x, y), (x, 1 - y), (1 - x, 1 - y)]
        return [pltpu.make_async_remote_copy(
            src_ref=ins[w].at[2 * chip[0] + chip[1]], dst_ref=outs[w].at[k], send_sem=send.at[w, k], recv_sem=recv.at[w, k],
            device_id=(*chip, c), device_id_type=MESH) for w in range(n) for k, chip in enumerate(chips)]

    def start(ins, outs, scr):
        for cp in copies(ins, outs, scr):
            cp.start()

    def finish(ins, outs, scr):
        for cp in copies(ins, outs, scr):
            cp.wait()

    return _Carry(sums, [jax.ShapeDtypeStruct((3,) + s.shape[1:], s.dtype) for s in sums],
                  [pltpu.SemaphoreType.DMA((n, 3)), pltpu.SemaphoreType.DMA((n, 3))], start, finish)


def _gather_small(packed):
    R, C = packed.shape

    def body(in_ref, out_ref, send, recv):
        x, y, c = _place()
        mine = 4 * x + 2 * y + c
        out_ref[mine] = in_ref[...]
        copies = []
        for k in range(1, N_DEV):
            peer = (x ^ (k >> 2), y ^ ((k >> 1) & 1), c ^ (k & 1))
            copies.append(pltpu.make_async_remote_copy(
                src_ref=in_ref, dst_ref=out_ref.at[mine], send_sem=send.at[k - 1], recv_sem=recv.at[k - 1],
                device_id=peer, device_id_type=MESH))
        for cp in copies:
            cp.start()
        for cp in copies:
            cp.wait()

    vmem = pl.BlockSpec(memory_space=pltpu.VMEM)
    return pl.pallas_call(
        body, out_shape=jax.ShapeDtypeStruct((N_DEV, R, C), F32), in_specs=[vmem], out_specs=vmem,
        scratch_shapes=[pltpu.SemaphoreType.DMA((N_DEV - 1,)), pltpu.SemaphoreType.DMA((N_DEV - 1,))],
        name="small_grads_all_gather")(packed)


def _adamw(w, g, m, v):
    m = ADAM_B1 * m + (1.0 - ADAM_B1) * g
    v = ADAM_B2 * v + (1.0 - ADAM_B2) * (g * g)
    m_hat = m / (1.0 - ADAM_B1 ** ADAM_STEP)
    v_hat = v / (1.0 - ADAM_B2 ** ADAM_STEP)
    delta = -ADAM_LR * (m_hat / (jnp.sqrt(v_hat) + ADAM_EPS) + ADAM_WD * w)
    return delta, m, v


def _pair_add(name, grads, received, c_idx):
    _, R, C = grads.shape
    tr = _row_tile(R)

    def body(c_ref, g_ref, r_ref, o_ref):
        o_ref[...] = (g_ref[...].astype(F32) + r_ref[...].astype(F32)).astype(BF16)

    blk = pl.BlockSpec((None, tr, C), lambda q, i, c: (q, i, 0))
    return pl.pallas_call(
        body, out_shape=jax.ShapeDtypeStruct((4, R, C), BF16),
        grid_spec=pltpu.PrefetchScalarGridSpec(
            num_scalar_prefetch=1, grid=(4, R // tr),
            in_specs=[pl.BlockSpec((None, tr, C), lambda q, i, c: (2 * q + c[0], i, 0)), blk], out_specs=blk),
        name=name, compiler_params=_params(2))(c_idx, grads, received)


def _adam_shard(name, sums, received, w, m, v, chip_idx):
    R, C = w.shape
    tr = _row_tile(R, 128)

    def body(q_ref, s_ref, r_ref, w_ref, m_ref, v_ref, g_out, d_out, m_out, v_out):
        g = s_ref[...].astype(F32) + r_ref[0].astype(F32) + r_ref[1].astype(F32) + r_ref[2].astype(F32)
        delta, mn, vn = _adamw(w_ref[...], g, m_ref[...], v_ref[...])
        g_out[...] = g
        d_out[...] = delta
        m_out[...] = mn
        v_out[...] = vn

    blk = pl.BlockSpec((tr, C), lambda i, q: (i, 0))
    shape = jax.ShapeDtypeStruct((R, C), F32)
    return pl.pallas_call(
        body, out_shape=(shape,) * 4,
        grid_spec=pltpu.PrefetchScalarGridSpec(
            num_scalar_prefetch=1, grid=(R // tr,),
            in_specs=[pl.BlockSpec((None, tr, C), lambda i, q: (q[0], i, 0)), pl.BlockSpec((3, tr, C), lambda i, q: (0, i, 0)),
                      blk, blk, blk],
            out_specs=(blk,) * 4),
        name=name, compiler_params=_params(1))(chip_idx, sums, received, w, m, v)


def _adam_small(name, gathered, w, m, v):
    R, C = w.shape

    def body(ga_ref, w_ref, m_ref, v_ref, g_out, d_out, m_out, v_out):
        g = ga_ref[0]
        for d in range(1, N_DEV):
            g = g + ga_ref[d]
        delta, mn, vn = _adamw(w_ref[...], g, m_ref[...], v_ref[...])
        g_out[...] = g
        d_out[...] = delta
        m_out[...] = mn
        v_out[...] = vn

    full = pl.BlockSpec((R, C), lambda i: (0, 0))
    shape = jax.ShapeDtypeStruct((R, C), F32)
    return pl.pallas_call(
        body, out_shape=(shape,) * 4, grid=(1,),
        in_specs=[pl.BlockSpec((N_DEV, R, C), lambda i: (0, 0, 0)), full, full, full], out_specs=(full,) * 4,
        name=name, compiler_params=_params(1))(gathered, w, m, v)


def _pack_small(parts, D):
    g1, gmix, g2, gof, gos, bf, gqf, gkf, gqs, gks, sinks = [p.reshape(-1).astype(F32) for p in parts]
    row3 = jnp.concatenate([gof, gos])
    row4 = jnp.zeros((D,), F32)
    for slot, vec in enumerate((bf, gqf, gkf, gqs, gks, sinks)):
        row4 = lax.dynamic_update_slice(row4, vec, (slot * LANES,))
    zero = jnp.zeros((D,), F32)
    return jnp.stack([g1, gmix, g2, row3, row4, zero, zero, zero])


def _unpack_small(packed, D, H):
    Dh = D // 2
    row4 = packed[4]
    short = [row4[s * LANES:s * LANES + n] for s, n in enumerate((H, HEAD_DIM, HEAD_DIM, HEAD_DIM, HEAD_DIM, H))]
    vecs = [packed[0], packed[1], packed[2], packed[3, :Dh], packed[3, Dh:]] + short
    return [v[None, :] for v in vecs]


def kernel(x, positions, norm_ffn1_g, ffn1_w_gate, ffn1_w_up, ffn1_w_down, norm_mix_g, w_in, b_forget, fox_q_norm_g, fox_k_norm_g, swa_q_norm_g, swa_k_norm_g, swa_sinks, out_norm_fox_g, out_norm_swa_g, w_out, norm_ffn2_g, ffn2_w_gate, ffn2_w_up, ffn2_w_down, loss_target, m_norm_ffn1_g, m_ffn1_w_gate, m_ffn1_w_up, m_ffn1_w_down, m_norm_mix_g, m_w_in, m_b_forget, m_fox_q_norm_g, m_fox_k_norm_g, m_swa_q_norm_g, m_swa_k_norm_g, m_swa_sinks, m_out_norm_fox_g, m_out_norm_swa_g, m_w_out, m_norm_ffn2_g, m_ffn2_w_gate, m_ffn2_w_up, m_ffn2_w_down, v_norm_ffn1_g, v_ffn1_w_gate, v_ffn1_w_up, v_ffn1_w_down, v_norm_mix_g, v_w_in, v_b_forget, v_fox_q_norm_g, v_fox_k_norm_g, v_swa_q_norm_g, v_swa_k_norm_g, v_swa_sinks, v_out_norm_fox_g, v_out_norm_swa_g, v_w_out, v_norm_ffn2_g, v_ffn2_w_gate, v_ffn2_w_up, v_ffn2_w_down):
    xs = x[0]
    target = loss_target[0]
    T, D = xs.shape
    Dh = D // 2
    H = Dh // HEAD_DIM
    HP = H // 2
    KVW = (H // GQA_GROUP) * HEAD_DIM
    KVB = KVW // LANES
    MAIN = 4 * Dh + 2 * KVW
    F_OFF = 3 * Dh
    tm = min(ROW_TILE_CAP, T)
    tq = min(256, T)
    tk = min(512, T)
    nk = T // tk
    cx, cy, cc = _place()
    c_idx = jnp.reshape(cc, (1,)).astype(jnp.int32)
    chip_idx = jnp.reshape(2 * cx + cy, (1,)).astype(jnp.int32)

    big_w = [ffn1_w_gate[0], ffn1_w_up[0], ffn1_w_down[0], w_in[0], w_out[0], ffn2_w_gate[0], ffn2_w_up[0], ffn2_w_down[0]]
    big_m = [m_ffn1_w_gate[0], m_ffn1_w_up[0], m_ffn1_w_down[0], m_w_in[0], m_w_out[0], m_ffn2_w_gate[0], m_ffn2_w_up[0], m_ffn2_w_down[0]]
    big_v = [v_ffn1_w_gate[0], v_ffn1_w_up[0], v_ffn1_w_down[0], v_w_in[0], v_w_out[0], v_ffn2_w_gate[0], v_ffn2_w_up[0], v_ffn2_w_down[0]]
    names = ["ffn1_w_gate", "ffn1_w_up", "ffn1_w_down", "w_in", "w_out", "ffn2_w_gate", "ffn2_w_up", "ffn2_w_down"]
    sh = dict(zip(names, [w.astype(BF16) for w in big_w]))
    wg1, wu1 = _run_carry("weights_all_gather", _gather_carry([sh["ffn1_w_gate"], sh["ffn1_w_up"]]))

    lane = jnp.arange(LANES)
    inv_freq = ROPE_THETA ** (-(2.0 * (lane % (HEAD_DIM // 2))).astype(F32) / HEAD_DIM)
    ang = positions[0].astype(F32)[:, None] * inv_freq[None, :]
    cos_t = jnp.cos(ang)
    sin_t = jnp.where((lane & (HEAD_DIM // 2)) == 0, -1.0, 1.0)[None, :] * jnp.sin(ang)
    rope = (cos_t, sin_t)

    def pair_gain(g, blocks):
        return jnp.tile(jnp.concatenate([g[0], g[0]])[None, None, :], (blocks, 1, 1))

    n1 = _rmsnorm_fwd("ffn1_norm", xs, norm_ffn1_g, tm)
    (a1, b1, hm1), (wd1,) = _ffn_up("ffn1_up", n1, wg1, wu1, tm, carry=_gather_carry([sh["ffn1_w_down"]]))
    h1, (win_g,) = _ffn_down("ffn1_down", hm1, wd1, xs, tm, carry=_gather_carry([sh["w_in"]]))
    n_in = win_g.shape[2]
    win_full = jnp.transpose(win_g, (1, 0, 2)).reshape(D, N_DEV * n_in)
    win_main = jnp.concatenate([win_full[:, :F_OFF], win_full[:, F_OFF + H:]], axis=1)
    win_f = jnp.pad(win_full[:, F_OFF:F_OFF + H], ((0, 0), (0, LANES - H)))

    u = _rmsnorm_fwd("mix_norm", h1, norm_mix_g, tm)
    proj, (wout_g,) = _mm_nn("mix_proj", u, win_main, tm, MAIN // 9, carry=_gather_carry([sh["w_out"]]))
    wout = wout_g.reshape(D, D)
    proj_f = _mm_nn("mix_proj_forget", u, win_f, tm, LANES)
    scale = HEAD_DIM ** -0.5
    fox_gains = jnp.concatenate([pair_gain(fox_q_norm_g, HP), pair_gain(fox_k_norm_g, HP)])
    qk_f = _headnorm_fwd_scaled("fox_qk_norm", proj, 0, 2 * HP, fox_gains, tm, scale, HP)
    v_f = proj[:, 2 * Dh:3 * Dh].astype(BF16)
    c_t, sg_t = _forget_fwd("forget_gates", proj_f[:, :H].T, b_forget.reshape(H, 1))
    ccol = c_t[:, :, None]
    crow = c_t.reshape(H, nk, 1, tk)
    (o_fox, lse_fa, lse_fb), (wg2, wu2) = _fox_fwd("fox_attention", qk_f, v_f, ccol, crow, tq, tk,
                                                   carry=_gather_carry([sh["ffn2_w_gate"], sh["ffn2_w_up"]]))

    swa_q_gains = pair_gain(swa_q_norm_g, HP)
    swa_k_gains = pair_gain(swa_k_norm_g, KVB)
    q_s = _headnorm_fwd("swa_q_norm", proj, 3 * HP, HP, swa_q_gains, tm, scale, rope=rope)
    k_d = _headnorm_fwd("swa_k_norm", proj, 4 * HP, KVB, swa_k_gains, tm, 1.0, rope=rope, dup=True)
    v_s = proj[:, 4 * Dh + KVW:].astype(BF16).reshape(T, H // GQA_GROUP, 1, HEAD_DIM)
    v_d = jnp.broadcast_to(v_s, (T, H // GQA_GROUP, 2, HEAD_DIM)).reshape(T, 2 * KVW)
    sinks3 = swa_sinks.reshape(H, 1, 1)
    (o_swa, lse_sa, lse_sb), (wd2,) = _swa_fwd("swa_attention", q_s, k_d, v_d, sinks3, carry=_gather_carry([sh["ffn2_w_down"]]))

    on = _outnorm_fwd("out_norm", o_fox, o_swa, out_norm_fox_g, out_norm_swa_g, tm)
    h2 = _mm_nn("mix_out", on, wout, tm, min(512, D), resid=h1)

    n2 = _rmsnorm_fwd("ffn2_norm", h2, norm_ffn2_g, tm)
    a2, b2, hm2 = _ffn_up("ffn2_up", n2, wg2, wu2, tm)
    y = _ffn_down("ffn2_down", hm2, wd2, h2, tm)
    dy, dyh, sq = _loss_grad("loss_grad", y, target, min(256, T))
    loss = lax.psum(0.5 * sq[0, 0] / D, ("x", "y", "c"))

    J, _, Fs = wg2.shape
    aspec = pl.BlockSpec((None, tm, Fs), lambda i, j: (j, i, 0))
    wspec = pl.BlockSpec((None, D, Fs), lambda i, j: (j, 0, 0))

    def pair_sums(keys, grads, received):
        return [_pair_add("sum_" + nm, g, r, c_idx) for nm, g, r in zip(keys, grads, received)]

    da2, db2 = _ffn_bwd_mid("ffn2_bwd_mid", dyh, wd2, a2, b2, tm)
    dwd2 = _wgrad_down("ffn2_wgrad_down", hm2, dyh, min(1024, D))
    dwg2, dwu2 = _wgrad_up("ffn2_wgrad_up", n2, da2, db2, min(512, D))
    dn2, sib2 = _nt_reduce("ffn2_bwd_in", [(da2, aspec, wg2, wspec), (db2, aspec, wu2, wspec)], [], T, D, tm, J,
                           carry=_sibling_carry([dwg2, dwu2, dwd2]))
    dh2, dg_ffn2, dh2b = _rmsnorm_bwd("ffn2_norm_bwd", dn2, h2, norm_ffn2_g, dy, min(256, T), 1.0)
    sum_wg2, sum_wu2, sum_wd2 = pair_sums(names[5:8], [dwg2, dwu2, dwd2], sib2)

    dwout = _wgrad_2d("mix_out_wgrad", on, dh2b, min(512, D), min(1024, D))
    do_fox, dg_of = _outnorm_bwd("out_norm_bwd_fox", dh2b, wout, 0, o_fox, out_norm_fox_g, tm)
    do_swa, dg_os = _outnorm_bwd("out_norm_bwd_swa", dh2b, wout, 1, o_swa, out_norm_swa_g, tm)

    (dq_f, dk_f, dv_f, dc_a, dc_b, dr_a, dr_b), (rc_wg2, rc_wu2) = _fox_bwd(
        "fox_attention_bwd", qk_f, v_f, o_fox, do_fox, ccol, crow, lse_fa, lse_fb, tq, tk, carry=_chips_carry([sum_wg2, sum_wu2]))
    dqk_f = jnp.concatenate([dq_f, dk_f], axis=1)
    dqk_raw, dg_fox = _headnorm_bwd("fox_qk_norm_bwd", dqk_f, proj, 0, 2 * HP, fox_gains, HP, tm, 1.0)
    dct = jnp.stack([dc_a.reshape(HP, T), dc_b.reshape(HP, T)], axis=1).reshape(H, T)
    drt = jnp.stack([dr_a.reshape(HP, T), dr_b.reshape(HP, T)], axis=1).reshape(H, T)
    dz_t, db_f = _forget_bwd("forget_gates_bwd", dct, drt, sg_t)

    (dq_s, dk_p, dv_p, dsink_a, dsink_b), (rc_wd2,) = _swa_bwd(
        "swa_attention_bwd", q_s, k_d, v_d, sinks3, o_swa, do_swa, lse_sa, lse_sb, carry=_chips_carry([sum_wd2]))
    dqs_raw, dg_sq = _headnorm_bwd("swa_q_norm_bwd", dq_s, proj, 3 * HP, HP, swa_q_gains, HP, tm, 1.0, rope=rope)
    dks_raw, dg_sk = _headnorm_bwd("swa_k_norm_bwd", dk_p, proj, 4 * HP, KVB, swa_k_gains, KVB, tm, 1.0, rope=rope, fold=True)
    dvs_raw, _ = _headnorm_bwd("swa_v_fold", dv_p, None, 0, KVB, None, KVB, tm, 1.0, fold=True, norm=False)

    dproj = jnp.concatenate([dqk_raw, dv_f.astype(BF16), dqs_raw, dks_raw, dvs_raw], axis=1)
    dproj_f = jnp.pad(dz_t.T, ((0, 0), (0, LANES - H))).astype(BF16)
    dwin_main = _wgrad_2d("mix_proj_wgrad", u, dproj, min(1024, D), MAIN // 9)
    dwin_f = _wgrad_2d("mix_proj_forget_wgrad", u, dproj_f, min(1024, D), LANES)
    dwin_full = jnp.concatenate([dwin_main[:, :F_OFF], dwin_f[:, :H], dwin_main[:, F_OFF:]], axis=1)
    dwin_g = jnp.transpose(dwin_full.reshape(D, N_DEV, n_in), (1, 0, 2))
    dwout_g = dwout.reshape(N_DEV, D // N_DEV, D)
    tkb = MAIN // 9
    du, sib_mix = _nt_reduce(
        "mix_bwd_in",
        [(dproj, pl.BlockSpec((tm, tkb), lambda i, r: (i, r)), win_main, pl.BlockSpec((D, tkb), lambda i, r: (0, r)))],
        [(dproj_f, pl.BlockSpec((tm, LANES), lambda i, r: (i, 0)), win_f, pl.BlockSpec((D, LANES), lambda i, r: (0, 0)))],
        T, D, tm, 9, carry=_sibling_carry([dwin_g, dwout_g]))
    dh1, dg_mix, dh1h = _rmsnorm_bwd("mix_norm_bwd", du, h1, norm_mix_g, dh2, min(256, T), 0.5)
    sum_win, sum_wout = pair_sums(names[3:5], [dwin_g, dwout_g], sib_mix)

    (da1, db1), (rc_win,) = _ffn_bwd_mid("ffn1_bwd_mid", dh1h, wd1, a1, b1, tm, carry=_chips_carry([sum_win]))
    dwd1, (rc_wout,) = _wgrad_down("ffn1_wgrad_down", hm1, dh1h, min(1024, D), carry=_chips_carry([sum_wout]))
    dwg1, dwu1 = _wgrad_up("ffn1_wgrad_up", n1, da1, db1, min(512, D))
    dn1, sib1 = _nt_reduce("ffn1_bwd_in", [(da1, aspec, wg1, wspec), (db1, aspec, wu1, wspec)], [], T, D, tm, J,
                           carry=_sibling_carry([dwg1, dwu1, dwd1]))
    dx, dg_ffn1 = _rmsnorm_bwd("ffn1_norm_bwd", dn1, xs, norm_ffn1_g, dh1, min(256, T), None)
    sums1 = pair_sums(names[0:3], [dwg1, dwu1, dwd1], sib1)
    rc1 = _run_carry("grads_exchange_chips", _chips_carry(sums1))

    chip_sums = list(sums1) + [sum_win, sum_wout, sum_wg2, sum_wu2, sum_wd2]
    from_chips = list(rc1) + [rc_win, rc_wout, rc_wg2, rc_wu2, rc_wd2]
    big_out = [_adam_shard("adam_" + nm, s, r, w, m, v, chip_idx)
               for nm, s, r, w, m, v in zip(names, chip_sums, from_chips, big_w, big_m, big_v)]

    dsinks = jnp.stack([dsink_a.reshape(HP), dsink_b.reshape(HP)], axis=1).reshape(H)
    small_g = [dg_ffn1, dg_mix, dg_ffn2, dg_of, dg_os, db_f, dg_fox[0, 0, :HEAD_DIM], dg_fox[1, 0, :HEAD_DIM],
               dg_sq[0, 0, :HEAD_DIM], dg_sk[0, 0, :HEAD_DIM], dsinks]
    small_w = [norm_ffn1_g, norm_mix_g, norm_ffn2_g, out_norm_fox_g, out_norm_swa_g, b_forget, fox_q_norm_g, fox_k_norm_g,
               swa_q_norm_g, swa_k_norm_g, swa_sinks]
    small_m = [m_norm_ffn1_g, m_norm_mix_g, m_norm_ffn2_g, m_out_norm_fox_g, m_out_norm_swa_g, m_b_forget, m_fox_q_norm_g,
               m_fox_k_norm_g, m_swa_q_norm_g, m_swa_k_norm_g, m_swa_sinks]
    small_v = [v_norm_ffn1_g, v_norm_mix_g, v_norm_ffn2_g, v_out_norm_fox_g, v_out_norm_swa_g, v_b_forget, v_fox_q_norm_g,
               v_fox_k_norm_g, v_swa_q_norm_g, v_swa_k_norm_g, v_swa_sinks]
    gathered = _gather_small(_pack_small(small_g, D))
    small_out = _adam_small("adam_small", gathered, _pack_small(small_w, D), _pack_small(small_m, D), _pack_small(small_v, D))
    small_out = [_unpack_small(p, D, H) for p in small_out]

    order = ["norm_ffn1_g", "ffn1_w_gate", "ffn1_w_up", "ffn1_w_down", "norm_mix_g", "w_in", "b_forget", "fox_q_norm_g", "fox_k_norm_g",
             "swa_q_norm_g", "swa_k_norm_g", "swa_sinks", "out_norm_fox_g", "out_norm_swa_g", "w_out", "norm_ffn2_g",
             "ffn2_w_gate", "ffn2_w_up", "ffn2_w_down"]
    small_names = ["norm_ffn1_g", "norm_mix_g", "norm_ffn2_g", "out_norm_fox_g", "out_norm_swa_g", "b_forget", "fox_q_norm_g",
                   "fox_k_norm_g", "swa_q_norm_g", "swa_k_norm_g", "swa_sinks"]
    result = [loss, dx[None]]
    for kind in range(4):
        for nm in order:
            if nm in names:
                result.append(big_out[names.index(nm)][kind][None])
            else:
                result.append(small_out[kind][small_names.index(nm)])
    return tuple(result)


def _headnorm_fwd_scaled(name, proj, col_off, ncb, gains, tm, scale, n_scaled):
    T = proj.shape[0]

    def body(x_ref, g_ref, o_ref):
        xv = x_ref[...]
        lo = _lane_lo(xv.shape)
        y = xv * _head_rstd(xv, lo) * g_ref[...]
        y = y * jnp.where(pl.program_id(0) < n_scaled, scale, 1.0)
        o_ref[...] = y.astype(BF16)

    return pl.pallas_call(
        body, out_shape=jax.ShapeDtypeStruct((T, ncb * LANES), BF16), grid=(ncb, T // tm),
        in_specs=[pl.BlockSpec((tm, LANES), lambda c, i: (i, col_off + c)), pl.BlockSpec((None, 1, LANES), lambda c, i: (c, 0, 0))],
        out_specs=pl.BlockSpec((tm, LANES), lambda c, i: (i, c)), name=name, compiler_params=_params(2))(proj, gains)
```

```python
import functools

import jax
import jax.numpy as jnp
from jax import lax
from jax.experimental import pallas as pl
from jax.experimental.pallas import tpu as pltpu

F32 = jnp.float32
BF16 = jnp.bfloat16

HEAD_DIM = 64
LANES = 128
WINDOW = 128
GQA_GROUP = 4
EPS = 1e-6
ROPE_THETA = 10000.0
ADAM_LR = 0.001
ADAM_B1 = 0.9
ADAM_B2 = 0.999
ADAM_EPS = 1e-08
ADAM_WD = 0.01
ADAM_STEP = 10
N_DEV = 8
NEG = -1e30
VMEM_LIMIT_V7X = 48 * 1024 * 1024
ROW_TILE_CAP = 512
MESH = pl.DeviceIdType.MESH

NN = (((1,), (0,)), ((), ()))
NT = (((1,), (1,)), ((), ()))
TN = (((0,), (0,)), ((), ()))


def _dot(a, b, dims):
    return lax.dot_general(a, b, dims, preferred_element_type=F32)


def _params(n_axes):
    return pltpu.CompilerParams(dimension_semantics=("arbitrary",) * n_axes, vmem_limit_bytes=VMEM_LIMIT_V7X)


def _row_tile(rows, cap=ROW_TILE_CAP):
    best = None
    for t in range(16, min(rows, cap) + 1, 16):
        if rows % t == 0:
            best = t
    return best or rows


def _lane_lo(shape):
    return lax.broadcasted_iota(jnp.int32, shape, len(shape) - 1) < HEAD_DIM


def _keep(sel, x):
    return jnp.where(sel, x.astype(F32), 0.0).astype(BF16)


_HBM = pl.BlockSpec(memory_space=pltpu.HBM)


class _Carry:
    def __init__(self, inputs, out_shapes, scratch, start, finish):
        self.inputs, self.out_shapes, self.scratch, self.start, self.finish = list(inputs), list(out_shapes), list(scratch), start, finish


def _join(*carries):
    def hook(which):
        def run(ins, outs, scr):
            i = o = s = 0
            for c in carries:
                ni, no, ns = len(c.inputs), len(c.out_shapes), len(c.scratch)
                getattr(c, which)(ins[i:i + ni], outs[o:o + no], scr[s:s + ns])
                i, o, s = i + ni, o + no, s + ns
        return run

    return _Carry([a for c in carries for a in c.inputs], [a for c in carries for a in c.out_shapes],
                  [a for c in carries for a in c.scratch], hook("start"), hook("finish"))


def _call(body, *, name, grid, in_specs, out_specs, out_shape, args, scratch_shapes=(), carry=None):
    params = _params(len(grid))
    if carry is None:
        return pl.pallas_call(body, out_shape=out_shape, grid=grid, in_specs=list(in_specs), out_specs=out_specs,
                              scratch_shapes=list(scratch_shapes), name=name, compiler_params=params)(*args)
    single = not isinstance(out_shape, (tuple, list))
    shapes = (out_shape,) if single else tuple(out_shape)
    specs = (out_specs,) if single else tuple(out_specs)
    n_in, n_out, n_scr = len(args), len(shapes), len(scratch_shapes)
    c_in, c_out = len(carry.inputs), len(carry.out_shapes)

    def wrapped(*refs):
        ins, c_ins = refs[:n_in], refs[n_in:n_in + c_in]
        o0 = n_in + c_in
        outs, c_outs = refs[o0:o0 + n_out], refs[o0 + n_out:o0 + n_out + c_out]
        s0 = o0 + n_out + c_out
        scr, c_scr = refs[s0:s0 + n_scr], refs[s0 + n_scr:]
        first = pl.program_id(0) == 0
        last = pl.program_id(0) == grid[0] - 1
        for ax in range(1, len(grid)):
            first = jnp.logical_and(first, pl.program_id(ax) == 0)
            last = jnp.logical_and(last, pl.program_id(ax) == grid[ax] - 1)

        @pl.when(first)
        def _():
            carry.start(c_ins, c_outs, c_scr)

        body(*ins, *outs, *scr)

        @pl.when(last)
        def _():
            carry.finish(c_ins, c_outs, c_scr)

    res = pl.pallas_call(
        wrapped, out_shape=shapes + tuple(carry.out_shapes), grid=grid, in_specs=list(in_specs) + [_HBM] * c_in,
        out_specs=specs + (_HBM,) * c_out, scratch_shapes=list(scratch_shapes) + carry.scratch, name=name,
        compiler_params=params)(*args, *carry.inputs)
    main = res[:n_out]
    return (main[0] if single else tuple(main)), tuple(res[n_out:])


def _rms_bwd(dn, x, g):
    r = lax.rsqrt(jnp.mean(x * x, axis=-1, keepdims=True) + EPS)
    xh = x * r
    dxh = dn * g
    dx = r * (dxh - xh * jnp.mean(dxh * xh, axis=-1, keepdims=True))
    return dx, jnp.sum(dn * xh, axis=0, keepdims=True)


def _rmsnorm_fwd(name, x, g, tm):
    T, D = x.shape

    def body(x_ref, g_ref, o_ref):
        xf = x_ref[...]
        r = lax.rsqrt(jnp.mean(xf * xf, axis=-1, keepdims=True) + EPS)
        o_ref[...] = (xf * r * g_ref[...]).astype(BF16)

    return pl.pallas_call(
        body, out_shape=jax.ShapeDtypeStruct((T, D), BF16), grid=(T // tm,),
        in_specs=[pl.BlockSpec((tm, D), lambda i: (i, 0)), pl.BlockSpec((1, D), lambda i: (0, 0))],
        out_specs=pl.BlockSpec((tm, D), lambda i: (i, 0)), name=name, compiler_params=_params(1))(x, g)


def _outnorm_fwd(name, o_fox, o_swa, g_fox, g_swa, tm):
    T, Dh = o_fox.shape

    def body(a_ref, b_ref, ga_ref, gb_ref, o_ref):
        for ref, g_ref, lo in ((a_ref, ga_ref, 0), (b_ref, gb_ref, Dh)):
            xf = ref[...]
            r = lax.rsqrt(jnp.mean(xf * xf, axis=-1, keepdims=True) + EPS)
            o_ref[:, lo:lo + Dh] = (xf * r * g_ref[...]).astype(BF16)

    row = pl.BlockSpec((tm, Dh), lambda i: (i, 0))
    gain = pl.BlockSpec((1, Dh), lambda i: (0, 0))
    return pl.pallas_call(
        body, out_shape=jax.ShapeDtypeStruct((T, 2 * Dh), BF16), grid=(T // tm,),
        in_specs=[row, row, gain, gain], out_specs=pl.BlockSpec((tm, 2 * Dh), lambda i: (i, 0)),
        name=name, compiler_params=_params(1))(o_fox, o_swa, g_fox, g_swa)


def _outnorm_bwd(name, dhb, wout, half, o, g, tm):
    T, D = dhb.shape
    Dh = o.shape[1]

    def body(a_ref, w_ref, o_ref, g_ref, do_ref, dg_ref):
        don = _dot(a_ref[...], w_ref[...], NT)
        dx, dg = _rms_bwd(don, o_ref[...], g_ref[...])
        do_ref[...] = dx.astype(BF16)

        @pl.when(pl.program_id(0) == 0)
        def _():
            dg_ref[...] = dg

        @pl.when(pl.program_id(0) > 0)
        def _():
            dg_ref[...] += dg

    return pl.pallas_call(
        body, out_shape=(jax.ShapeDtypeStruct((T, Dh), BF16), jax.ShapeDtypeStruct((1, Dh), F32)), grid=(T // tm,),
        in_specs=[pl.BlockSpec((tm, D), lambda i: (i, 0)), pl.BlockSpec((Dh, D), lambda i: (half, 0)),
                  pl.BlockSpec((tm, Dh), lambda i: (i, 0)), pl.BlockSpec((1, Dh), lambda i: (0, 0))],
        out_specs=(pl.BlockSpec((tm, Dh), lambda i: (i, 0)), pl.BlockSpec((1, Dh), lambda i: (0, 0))),
        name=name, compiler_params=_params(1))(dhb, wout, o, g)


def _mm_nn(name, a, b, tm, tn, resid=None, carry=None):
    M, K = a.shape
    N = b.shape[1]

    def body(*refs):
        if resid is None:
            a_ref, b_ref, o_ref = refs
            o_ref[...] = _dot(a_ref[...], b_ref[...], NN)
        else:
            a_ref, b_ref, r_ref, o_ref = refs
            o_ref[...] = r_ref[...] + _dot(a_ref[...], b_ref[...], NN)

    ospec = pl.BlockSpec((tm, tn), lambda n, i: (i, n))
    in_specs = [pl.BlockSpec((tm, K), lambda n, i: (i, 0)), pl.BlockSpec((K, tn), lambda n, i: (0, n))]
    args = [a, b]
    if resid is not None:
        in_specs.append(ospec)
        args.append(resid)
    return _call(body, name=name, grid=(N // tn, M // tm), in_specs=in_specs, out_specs=ospec,
                 out_shape=jax.ShapeDtypeStruct((M, N), F32), args=args, carry=carry)


def _wgrad_2d(name, a, b, tmm, tn):
    T, M = a.shape
    N = b.shape[1]

    def body(a_ref, b_ref, o_ref):
        o_ref[...] = _dot(a_ref[...], b_ref[...], TN).astype(BF16)

    return pl.pallas_call(
        body, out_shape=jax.ShapeDtypeStruct((M, N), BF16), grid=(M // tmm, N // tn),
        in_specs=[pl.BlockSpec((T, tmm), lambda m, n: (0, m)), pl.BlockSpec((T, tn), lambda m, n: (0, n))],
        out_specs=pl.BlockSpec((tmm, tn), lambda m, n: (m, n)), name=name, compiler_params=_params(2))(a, b)


def _wgrad_down(name, hm, df, tn, carry=None):
    J, T, Fs = hm.shape
    D = df.shape[1]

    def body(a_ref, b_ref, o_ref):
        o_ref[...] = _dot(a_ref[...], b_ref[...], TN).astype(BF16)

    return _call(
        body, name=name, grid=(J, D // tn), out_shape=jax.ShapeDtypeStruct((J, Fs, D), BF16),
        in_specs=[pl.BlockSpec((None, T, Fs), lambda j, n: (j, 0, 0)), pl.BlockSpec((T, tn), lambda j, n: (0, n))],
        out_specs=pl.BlockSpec((None, Fs, tn), lambda j, n: (j, 0, n)), args=[hm, df], carry=carry)


def _wgrad_up(name, n, da, db, tn, carry=None):
    T, D = n.shape
    J, _, Fs = da.shape

    def body(n_ref, da_ref, db_ref, og_ref, ou_ref):
        nv = n_ref[...]
        og_ref[...] = _dot(da_ref[...], nv, TN).astype(BF16)
        ou_ref[...] = _dot(db_ref[...], nv, TN).astype(BF16)

    act = pl.BlockSpec((None, T, Fs), lambda j, m: (j, 0, 0))
    out = pl.BlockSpec((None, Fs, tn), lambda j, m: (j, 0, m))
    shape = jax.ShapeDtypeStruct((J, Fs, D), BF16)
    return _call(
        body, name=name, grid=(J, D // tn), out_shape=(shape, shape),
        in_specs=[pl.BlockSpec((T, tn), lambda j, m: (0, m)), act, act], out_specs=(out, out),
        args=[n, da, db], carry=carry)


def _reduce_mm(name, pairs, once, dims, T, D, tm, steps, carry=None):
    n_pairs = len(pairs)
    n_once = len(once)

    def body(*refs):
        pr = refs[:2 * n_pairs]
        on = refs[2 * n_pairs:2 * (n_pairs + n_once)]
        o_ref, acc = refs[-2:]
        r = pl.program_id(1)
        part = _dot(pr[0][...], pr[1][...], dims)
        for p in range(1, n_pairs):
            part = part + _dot(pr[2 * p][...], pr[2 * p + 1][...], dims)

        @pl.when(r == 0)
        def _():
            acc[...] = part

        @pl.when(r > 0)
        def _():
            acc[...] += part

        @pl.when(r == steps - 1)
        def _():
            dn = acc[...]
            for p in range(n_once):
                dn = dn + _dot(on[2 * p][...], on[2 * p + 1][...], dims)
            o_ref[...] = dn

    in_specs, args = [], []
    for a, a_spec, w, w_spec in list(pairs) + list(once):
        in_specs += [a_spec, w_spec]
        args += [a, w]
    row = pl.BlockSpec((tm, D), lambda i, r: (i, 0))
    return _call(body, name=name, grid=(T // tm, steps), in_specs=in_specs, out_specs=row, out_shape=jax.ShapeDtypeStruct((T, D), F32),
                 args=args, scratch_shapes=[pltpu.VMEM((tm, D), F32)], carry=carry)


def _rmsnorm_bwd(name, dn, x, g, dh, tm, bf16_scale, carry=None):
    T, D = x.shape
    emit_bf16 = bf16_scale is not None

    def body(dn_ref, x_ref, g_ref, dh_ref, *outs):
        dxn, dg = _rms_bwd(dn_ref[...], x_ref[...], g_ref[...])
        dx = dh_ref[...] + dxn
        outs[0][...] = dx
        if emit_bf16:
            outs[2][...] = (bf16_scale * dx).astype(BF16)

        @pl.when(pl.program_id(0) == 0)
        def _():
            outs[1][...] = dg

        @pl.when(pl.program_id(0) > 0)
        def _():
            outs[1][...] += dg

    row = pl.BlockSpec((tm, D), lambda i: (i, 0))
    gain = pl.BlockSpec((1, D), lambda i: (0, 0))
    out_shape = [jax.ShapeDtypeStruct((T, D), F32), jax.ShapeDtypeStruct((1, D), F32)]
    out_specs = [row, gain]
    if emit_bf16:
        out_shape.append(jax.ShapeDtypeStruct((T, D), BF16))
        out_specs.append(row)
    return _call(body, name=name, grid=(T // tm,), in_specs=[row, row, gain, row], out_specs=tuple(out_specs),
                 out_shape=tuple(out_shape), args=[dn, x, g, dh], carry=carry)


def _loss_grad(name, y, target, tm):
    T, D = y.shape

    def body(y_ref, t_ref, dy_ref, dyh_ref, sq_ref):
        diff = y_ref[...] - t_ref[...]
        sq = jnp.sum(jnp.sum(diff * diff, axis=1, keepdims=True), axis=0, keepdims=True)
        dy = diff * (1.0 / D)
        dy_ref[...] = dy
        dyh_ref[...] = (0.5 * dy).astype(BF16)

        @pl.when(pl.program_id(0) == 0)
        def _():
            sq_ref[...] = sq

        @pl.when(pl.program_id(0) > 0)
        def _():
            sq_ref[...] += sq

    row = pl.BlockSpec((tm, D), lambda i: (i, 0))
    return pl.pallas_call(
        body, out_shape=(jax.ShapeDtypeStruct((T, D), F32), jax.ShapeDtypeStruct((T, D), BF16), jax.ShapeDtypeStruct((1, 1), F32)),
        grid=(T // tm,), in_specs=[row, row], out_specs=(row, row, pl.BlockSpec((1, 1), lambda i: (0, 0))),
        name=name, compiler_params=_params(1))(y, target)


def _ffn_up(name, n, wg, wu, tm, carry=None):
    T, D = n.shape
    J, Fs, _ = wg.shape

    def body(n_ref, wg_ref, wu_ref, a_ref, b_ref, h_ref):
        xv = n_ref[...]
        a = _dot(xv, wg_ref[...], NT)
        b = _dot(xv, wu_ref[...], NT)
        a_ref[...] = a.astype(BF16)
        b_ref[...] = b.astype(BF16)
        h_ref[...] = (a * jax.nn.sigmoid(a) * b).astype(BF16)

    act = jax.ShapeDtypeStruct((J, T, Fs), BF16)
    wspec = pl.BlockSpec((None, Fs, D), lambda j, i: (j, 0, 0))
    aspec = pl.BlockSpec((None, tm, Fs), lambda j, i: (j, i, 0))
    return _call(
        body, name=name, grid=(J, T // tm), out_shape=(act, act, act),
        in_specs=[pl.BlockSpec((tm, D), lambda j, i: (i, 0)), wspec, wspec], out_specs=(aspec, aspec, aspec),
        args=[n, wg, wu], carry=carry)


def _ffn_down(name, hm, wd, resid, tm, carry=None):
    J, T, Fs = hm.shape
    D = wd.shape[2]

    def body(h_ref, w_ref, r_ref, o_ref, acc):
        j = pl.program_id(1)
        part = _dot(h_ref[...], w_ref[...], NN)

        @pl.when(j == 0)
        def _():
            acc[...] = part

        @pl.when(j > 0)
        def _():
            acc[...] += part

        @pl.when(j == J - 1)
        def _():
            o_ref[...] = r_ref[...] + 0.5 * acc[...]

    row = pl.BlockSpec((tm, D), lambda i, j: (i, 0))
    return _call(
        body, name=name, grid=(T // tm, J), out_shape=jax.ShapeDtypeStruct((T, D), F32),
        in_specs=[pl.BlockSpec((None, tm, Fs), lambda i, j: (j, i, 0)), pl.BlockSpec((None, Fs, D), lambda i, j: (j, 0, 0)), row],
        out_specs=row, scratch_shapes=[pltpu.VMEM((tm, D), F32)], args=[hm, wd, resid], carry=carry)


def _ffn_bwd_mid(name, dfh, wd, a, b, tm, carry=None):
    T, D = dfh.shape
    J, Fs, _ = wd.shape

    def body(df_ref, w_ref, a_ref, b_ref, da_ref, db_ref):
        dhm = _dot(df_ref[...], w_ref[...], NT)
        av = a_ref[...].astype(F32)
        bv = b_ref[...].astype(F32)
        sg = jax.nn.sigmoid(av)
        da_ref[...] = (dhm * bv * (sg * (1.0 + av * (1.0 - sg)))).astype(BF16)
        db_ref[...] = (dhm * (av * sg)).astype(BF16)

    act = jax.ShapeDtypeStruct((J, T, Fs), BF16)
    aspec = pl.BlockSpec((None, tm, Fs), lambda j, i: (j, i, 0))
    return _call(
        body, name=name, grid=(J, T // tm), out_shape=(act, act),
        in_specs=[pl.BlockSpec((tm, D), lambda j, i: (i, 0)), pl.BlockSpec((None, Fs, D), lambda j, i: (j, 0, 0)), aspec, aspec],
        out_specs=(aspec, aspec), args=[dfh, wd, a, b], carry=carry)


def _rot_half(y, lane):
    first = (lane & (HEAD_DIM // 2)) == 0
    return jnp.where(first, pltpu.roll(y, LANES - HEAD_DIM // 2, 1), pltpu.roll(y, HEAD_DIM // 2, 1))


def _head_rstd(x, lo):
    sq = x * x
    ss_a = jnp.sum(jnp.where(lo, sq, 0.0), axis=-1, keepdims=True)
    ss_b = jnp.sum(jnp.where(lo, 0.0, sq), axis=-1, keepdims=True)
    return lax.rsqrt(jnp.where(lo, ss_a, ss_b) * (1.0 / HEAD_DIM) + EPS)


def _headnorm_fwd(name, proj, col_off, ncb, gains, tm, scale, rope=None, dup=False):
    T = proj.shape[0]
    with_rope = rope is not None
    width = 2 * LANES if dup else LANES

    def body(*refs):
        if with_rope:
            x_ref, g_ref, cos_ref, sin_ref, o_ref = refs
        else:
            x_ref, g_ref, o_ref = refs
        xv = x_ref[...]
        lane = lax.broadcasted_iota(jnp.int32, xv.shape, 1)
        lo = lane < HEAD_DIM
        y = xv * _head_rstd(xv, lo) * g_ref[...]
        if with_rope:
            y = y * cos_ref[...] + _rot_half(y, lane) * sin_ref[...]
        y = y * scale
        if dup:
            sw = pltpu.roll(y, HEAD_DIM, 1)
            o_ref[:, :LANES] = jnp.where(lo, y, sw).astype(BF16)
            o_ref[:, LANES:] = jnp.where(lo, sw, y).astype(BF16)
        else:
            o_ref[...] = y.astype(BF16)

    in_specs = [pl.BlockSpec((tm, LANES), lambda c, i: (i, col_off + c)), pl.BlockSpec((None, 1, LANES), lambda c, i: (c, 0, 0))]
    args = [proj, gains]
    if with_rope:
        tab = pl.BlockSpec((tm, LANES), lambda c, i: (i, 0))
        in_specs += [tab, tab]
        args += list(rope)
    return pl.pallas_call(
        body, out_shape=jax.ShapeDtypeStruct((T, ncb * width), BF16), grid=(ncb, T // tm),
        in_specs=in_specs, out_specs=pl.BlockSpec((tm, width), lambda c, i: (i, c)),
        name=name, compiler_params=_params(2))(*args)


def _headnorm_bwd(name, dy, proj, col_off, ncb, gains, group, tm, scale, rope=None, fold=False, norm=True):
    T = dy.shape[0]
    with_rope = rope is not None
    n_groups = ncb // group
    dy_width = 4 * LANES if fold else LANES

    def body(*refs):
        refs = list(refs)
        dy_ref = refs.pop(0)
        x_ref = refs.pop(0) if norm else None
        g_ref = refs.pop(0) if norm else None
        cos_ref = refs.pop(0) if with_rope else None
        sin_ref = refs.pop(0) if with_rope else None
        dx_ref = refs.pop(0)
        dg_ref = refs.pop(0) if norm else None
        c = pl.program_id(0)
        i = pl.program_id(1)
        d = dy_ref[...]
        lane = lax.broadcasted_iota(jnp.int32, (d.shape[0], LANES), 1)
        lo = lane < HEAD_DIM
        if fold:
            t0 = d[:, 0:LANES] + d[:, LANES:2 * LANES]
            t1 = d[:, 2 * LANES:3 * LANES] + d[:, 3 * LANES:4 * LANES]
            d = jnp.where(lo, t0 + pltpu.roll(t0, HEAD_DIM, 1), t1 + pltpu.roll(t1, HEAD_DIM, 1))
        d = d * scale
        if with_rope:
            d = d * cos_ref[...] + _rot_half(d * sin_ref[...], lane)
        if not norm:
            dx_ref[...] = d.astype(BF16)
            return
        xv = x_ref[...]
        gv = g_ref[...]
        r = _head_rstd(xv, lo)
        xh = xv * r
        dxh = d * gv
        pr = dxh * xh
        m_a = jnp.sum(jnp.where(lo, pr, 0.0), axis=-1, keepdims=True)
        m_b = jnp.sum(jnp.where(lo, 0.0, pr), axis=-1, keepdims=True)
        mean = jnp.where(lo, m_a, m_b) * (1.0 / HEAD_DIM)
        dx_ref[...] = (r * (dxh - xh * mean)).astype(BF16)
        dgp = jnp.sum(d * xh, axis=0, keepdims=True)
        dgp = dgp + pltpu.roll(dgp, HEAD_DIM, 1)
        first = jnp.logical_and(c % group == 0, i == 0)

        @pl.when(first)
        def _():
            dg_ref[...] = dgp

        @pl.when(jnp.logical_not(first))
        def _():
            dg_ref[...] += dgp

    in_specs = [pl.BlockSpec((tm, dy_width), lambda c, i: (i, c))]
    args = [dy]
    if norm:
        in_specs += [pl.BlockSpec((tm, LANES), lambda c, i: (i, col_off + c)), pl.BlockSpec((None, 1, LANES), lambda c, i: (c, 0, 0))]
        args += [proj, gains]
    if with_rope:
        tab = pl.BlockSpec((tm, LANES), lambda c, i: (i, 0))
        in_specs += [tab, tab]
        args += list(rope)
    out_shape = [jax.ShapeDtypeStruct((T, ncb * LANES), BF16)]
    out_specs = [pl.BlockSpec((tm, LANES), lambda c, i: (i, c))]
    if norm:
        out_shape.append(jax.ShapeDtypeStruct((n_groups, 1, LANES), F32))
        out_specs.append(pl.BlockSpec((None, 1, LANES), lambda c, i: (c // group, 0, 0)))
    res = pl.pallas_call(
        body, out_shape=tuple(out_shape), grid=(ncb, T // tm), in_specs=in_specs, out_specs=tuple(out_specs),
        name=name, compiler_params=_params(2))(*args)
    return res if norm else (res[0], None)


def _dot_exact(x, tri):
    hi = x.astype(BF16)
    r1 = x - hi.astype(F32)
    mid = r1.astype(BF16)
    lo = (r1 - mid.astype(F32)).astype(BF16)
    return _dot(hi, tri, NN) + _dot(mid, tri, NN) + _dot(lo, tri, NN)


def _forget_fwd(name, zt, bias):
    H, T = zt.shape
    blk = min(256, T)

    def body(z_ref, b_ref, c_ref, s_ref):
        z = z_ref[...] + b_ref[...]
        s_ref[...] = jax.nn.sigmoid(-z)
        lf = jnp.minimum(z, 0.0) - jnp.log(1.0 + jnp.exp(-jnp.abs(z)))
        tri = (lax.broadcasted_iota(jnp.int32, (blk, blk), 0) <= lax.broadcasted_iota(jnp.int32, (blk, blk), 1)).astype(BF16)
        carry = jnp.zeros((H, 1), F32)
        for bi in range(T // blk):
            xb = lf[:, bi * blk:(bi + 1) * blk]
            c_ref[:, bi * blk:(bi + 1) * blk] = _dot_exact(xb, tri) + carry
            carry = carry + jnp.sum(xb, axis=-1, keepdims=True)

    shape = jax.ShapeDtypeStruct((H, T), F32)
    full = pl.BlockSpec((H, T), lambda i: (0, 0))
    return pl.pallas_call(
        body, out_shape=(shape, shape), grid=(1,), in_specs=[full, pl.BlockSpec((H, 1), lambda i: (0, 0))],
        out_specs=(full, full), name=name, compiler_params=_params(1))(zt, bias)


def _forget_bwd(name, dct, drt, sgt):
    H, T = dct.shape
    blk = min(256, T)

    def body(dc_ref, dr_ref, s_ref, dz_ref, db_ref):
        dc = dc_ref[...] + dr_ref[...]
        tri = (lax.broadcasted_iota(jnp.int32, (blk, blk), 0) >= lax.broadcasted_iota(jnp.int32, (blk, blk), 1)).astype(BF16)
        carry = jnp.zeros((H, 1), F32)
        db = jnp.zeros((H, 1), F32)
        for bi in reversed(range(T // blk)):
            xb = dc[:, bi * blk:(bi + 1) * blk]
            dz = (_dot_exact(xb, tri) + carry) * s_ref[:, bi * blk:(bi + 1) * blk]
            dz_ref[:, bi * blk:(bi + 1) * blk] = dz
            db = db + jnp.sum(dz, axis=-1, keepdims=True)
            carry = carry + jnp.sum(xb, axis=-1, keepdims=True)
        db_ref[...] = db

    full = pl.BlockSpec((H, T), lambda i: (0, 0))
    return pl.pallas_call(
        body, out_shape=(jax.ShapeDtypeStruct((H, T), F32), jax.ShapeDtypeStruct((H, 1), F32)), grid=(1,),
        in_specs=[full, full, full], out_specs=(full, pl.BlockSpec((H, 1), lambda i: (0, 0))),
        name=name, compiler_params=_params(1))(dct, drt, sgt)


def _fox_fwd(name, qk, v, ccol, crow, tq, tk, carry=None):
    T, Dh = v.shape
    HP = Dh // LANES
    nk = T // tk

    def body(q_ref, k_ref, v_ref, ca_ref, cb_ref, ra_ref, rb_ref, o_ref, la_ref, lb_ref):
        i = pl.program_id(1)
        q2 = q_ref[...]
        lo = _lane_lo((tq, LANES))
        rows = i * tq + lax.broadcasted_iota(jnp.int32, (tq, tk), 0)
        col0 = lax.broadcasted_iota(jnp.int32, (tq, tk), 1)
        n_chunks = ((i + 1) * tq + tk - 1) // tk
        res = []
        for sel, c_ref, r_ref in ((lo, ca_ref, ra_ref), (jnp.logical_not(lo), cb_ref, rb_ref)):
            qm = _keep(sel, q2)
            ct = c_ref[...]

            def step(kc, carry, qm=qm, ct=ct, r_ref=r_ref):
                m, l, acc = carry
                start = pl.multiple_of(kc * tk, tk)
                kb = k_ref[pl.ds(start, tk), :]
                vb = v_ref[pl.ds(start, tk), :]
                s = _dot(qm, kb, NT) + ct - r_ref[kc]
                s = jnp.where(rows >= col0 + start, s, NEG)
                mn = jnp.maximum(m, jnp.max(s, axis=-1, keepdims=True))
                p = jnp.exp(s - mn)
                alpha = jnp.exp(m - mn)
                l = alpha * l + jnp.sum(p, axis=-1, keepdims=True)
                acc = alpha * acc + _dot(p.astype(BF16), vb, NN)
                return mn, l, acc

            m, l, acc = lax.fori_loop(0, n_chunks, step, (jnp.full((tq, 1), NEG, F32), jnp.zeros((tq, 1), F32), jnp.zeros((tq, LANES), F32)))
            res.append((acc / l, m + jnp.log(l)))
        o_ref[...] = jnp.where(lo, res[0][0], res[1][0])
        la_ref[...] = res[0][1]
        lb_ref[...] = res[1][1]

    col = lambda off: pl.BlockSpec((None, tq, 1), lambda h, i: (2 * h + off, i, 0))
    row = lambda off: pl.BlockSpec((None, nk, 1, tk), lambda h, i: (2 * h + off, 0, 0, 0))
    lse = jax.ShapeDtypeStruct((HP, T, 1), F32)
    lspec = pl.BlockSpec((None, tq, 1), lambda h, i: (h, i, 0))
    return _call(
        body, name=name, grid=(HP, T // tq), out_shape=(jax.ShapeDtypeStruct((T, Dh), F32), lse, lse),
        in_specs=[pl.BlockSpec((tq, LANES), lambda h, i: (i, h)), pl.BlockSpec((T, LANES), lambda h, i: (0, HP + h)),
                  pl.BlockSpec((T, LANES), lambda h, i: (0, h)), col(0), col(1), row(0), row(1)],
        out_specs=(pl.BlockSpec((tq, LANES), lambda h, i: (i, h)), lspec, lspec),
        args=[qk, qk, v, ccol, ccol, crow, crow], carry=carry)


def _fox_bwd(name, qk, v, o, do, ccol, crow, lse_a, lse_b, tq, tk, carry=None):
    T, Dh = v.shape
    HP = Dh // LANES
    nk = T // tk
    scale = HEAD_DIM ** -0.5

    def body(q_ref, k_ref, v_ref, o_ref, do_ref, ca_ref, cb_ref, ra_ref, rb_ref, la_ref, lb_ref,
             dq_ref, dk_ref, dv_ref, dca_ref, dcb_ref, dra_ref, drb_ref):
        i = pl.program_id(1)

        @pl.when(i == 0)
        def _():
            dk_ref[...] = jnp.zeros_like(dk_ref)
            dv_ref[...] = jnp.zeros_like(dv_ref)
            dca_ref[...] = jnp.zeros_like(dca_ref)
            dcb_ref[...] = jnp.zeros_like(dcb_ref)

        q2 = q_ref[...]
        do2 = do_ref[...]
        lo = _lane_lo((tq, LANES))
        hi = jnp.logical_not(lo)
        prod = do2.astype(F32) * o_ref[...]
        rows = i * tq + lax.broadcasted_iota(jnp.int32, (tq, tk), 0)
        col0 = lax.broadcasted_iota(jnp.int32, (tq, tk), 1)
        n_chunks = ((i + 1) * tq + tk - 1) // tk
        dqs = []
        for sel, c_ref, r_ref, l_ref, dc_ref, dr_ref in ((lo, ca_ref, ra_ref, la_ref, dca_ref, dra_ref),
                                                         (hi, cb_ref, rb_ref, lb_ref, dcb_ref, drb_ref)):
            qm = _keep(sel, q2)
            dom = _keep(sel, do2)
            dsum = jnp.sum(jnp.where(sel, prod, 0.0), axis=-1, keepdims=True)
            ct = c_ref[...]
            lse = l_ref[...]

            def step(kc, carry, qm=qm, dom=dom, dsum=dsum, ct=ct, lse=lse, r_ref=r_ref, dc_ref=dc_ref):
                dq, dr = carry
                start = pl.multiple_of(kc * tk, tk)
                kb = k_ref[pl.ds(start, tk), :]
                vb = v_ref[pl.ds(start, tk), :]
                s = _dot(qm, kb, NT) + ct - r_ref[kc]
                s = jnp.where(rows >= col0 + start, s, NEG)
                p = jnp.exp(s - lse)
                ds = p * (_dot(dom, vb, NT) - dsum)
                dsb = ds.astype(BF16)
                dk_ref[pl.ds(start, tk), :] += _dot(dsb, qm, TN)
                dv_ref[pl.ds(start, tk), :] += _dot(p.astype(BF16), dom, TN)
                dc_ref[kc] = dc_ref[kc] - jnp.sum(ds, axis=0, keepdims=True)
                return dq + _dot(dsb, kb, NN), dr + jnp.sum(ds, axis=-1, keepdims=True)

            dq, dr = lax.fori_loop(0, n_chunks, step, (jnp.zeros((tq, LANES), F32), jnp.zeros((tq, 1), F32)))
            dqs.append(dq)
            dr_ref[...] = dr
        dq_ref[...] = jnp.where(lo, dqs[0], dqs[1]) * scale

    col = lambda off: pl.BlockSpec((None, tq, 1), lambda h, i: (2 * h + off, i, 0))
    row = lambda off: pl.BlockSpec((None, nk, 1, tk), lambda h, i: (2 * h + off, 0, 0, 0))
    lspec = pl.BlockSpec((None, tq, 1), lambda h, i: (h, i, 0))
    qspec = pl.BlockSpec((tq, LANES), lambda h, i: (i, h))
    full = pl.BlockSpec((T, LANES), lambda h, i: (0, h))
    dcspec = pl.BlockSpec((None, nk, 1, tk), lambda h, i: (h, 0, 0, 0))
    grad = jax.ShapeDtypeStruct((T, Dh), F32)
    dc = jax.ShapeDtypeStruct((HP, nk, 1, tk), F32)
    dr = jax.ShapeDtypeStruct((HP, T, 1), F32)
    return _call(
        body, name=name, grid=(HP, T // tq), out_shape=(grad, grad, grad, dc, dc, dr, dr),
        in_specs=[qspec, pl.BlockSpec((T, LANES), lambda h, i: (0, HP + h)), full, qspec, qspec,
                  col(0), col(1), row(0), row(1), lspec, lspec],
        out_specs=(qspec, full, full, dcspec, dcspec, lspec, lspec),
        args=[qk, qk, v, o, do, ccol, ccol, crow, crow, lse_a, lse_b], carry=carry)


def _swa_block(n, q_ref, k_ref):
    qs = pl.multiple_of(n * WINDOW, WINDOW)
    ks = pl.multiple_of(jnp.maximum(n - 1, 0) * WINDOW, WINDOW)
    rel = (qs + lax.broadcasted_iota(jnp.int32, (WINDOW, 2 * WINDOW), 0)) - (ks + lax.broadcasted_iota(jnp.int32, (WINDOW, 2 * WINDOW), 1))
    valid = jnp.logical_and(rel >= 0, rel < WINDOW)
    return qs, ks, valid


def _swa_fwd(name, q, kd, vd, sinks, carry=None):
    T, Dh = q.shape
    HP = Dh // LANES

    def body(q_ref, k_ref, v_ref, sa_ref, sb_ref, o_ref, la_ref, lb_ref):
        lo = _lane_lo((WINDOW, LANES))

        def block(n, _):
            qs, ks, valid = _swa_block(n, q_ref, k_ref)
            q2 = q_ref[pl.ds(qs, WINDOW), :]
            kb = k_ref[pl.ds(ks, 2 * WINDOW), :]
            vb = v_ref[pl.ds(ks, 2 * WINDOW), :]
            res = []
            for sel, s_ref in ((lo, sa_ref), (jnp.logical_not(lo), sb_ref)):
                qm = _keep(sel, q2)
                sink = s_ref[...]
                s = jnp.where(valid, _dot(qm, kb, NT), NEG)
                m = jnp.maximum(jnp.max(s, axis=-1, keepdims=True), sink)
                p = jnp.exp(s - m)
                l = jnp.sum(p, axis=-1, keepdims=True) + jnp.exp(sink - m)
                res.append((_dot(p.astype(BF16), vb, NN) / l, m + jnp.log(l)))
            o_ref[pl.ds(qs, WINDOW), :] = jnp.where(lo, res[0][0], res[1][0])
            la_ref[pl.ds(qs, WINDOW), :] = res[0][1]
            lb_ref[pl.ds(qs, WINDOW), :] = res[1][1]
            return 0

        lax.fori_loop(0, T // WINDOW, block, 0)

    full = pl.BlockSpec((T, LANES), lambda h: (0, h))
    kv = pl.BlockSpec((T, LANES), lambda h: (0, h // 2))
    sink = lambda off: pl.BlockSpec((None, 1, 1), lambda h: (2 * h + off, 0, 0))
    lse = jax.ShapeDtypeStruct((HP, T, 1), F32)
    lspec = pl.BlockSpec((None, T, 1), lambda h: (h, 0, 0))
    return _call(
        body, name=name, grid=(HP,), out_shape=(jax.ShapeDtypeStruct((T, Dh), F32), lse, lse),
        in_specs=[full, kv, kv, sink(0), sink(1)], out_specs=(full, lspec, lspec),
        args=[q, kd, vd, sinks, sinks], carry=carry)


def _swa_bwd(name, q, kd, vd, sinks, o, do, lse_a, lse_b, carry=None):
    T, Dh = q.shape
    HP = Dh // LANES
    scale = HEAD_DIM ** -0.5

    def body(q_ref, k_ref, v_ref, sa_ref, sb_ref, o_ref, do_ref, la_ref, lb_ref, dq_ref, dk_ref, dv_ref, dsa_ref, dsb_ref):
        lo = _lane_lo((WINDOW, LANES))
        hi = jnp.logical_not(lo)
        dk_ref[...] = jnp.zeros_like(dk_ref)
        dv_ref[...] = jnp.zeros_like(dv_ref)

        def block(n, dsinks):
            qs, ks, valid = _swa_block(n, q_ref, k_ref)
            q2 = q_ref[pl.ds(qs, WINDOW), :]
            do2 = do_ref[pl.ds(qs, WINDOW), :]
            kb = k_ref[pl.ds(ks, 2 * WINDOW), :]
            vb = v_ref[pl.ds(ks, 2 * WINDOW), :]
            prod = do2.astype(F32) * o_ref[pl.ds(qs, WINDOW), :]
            dqs, new = [], []
            dk = jnp.zeros((2 * WINDOW, LANES), F32)
            dv = jnp.zeros((2 * WINDOW, LANES), F32)
            for sel, s_ref, l_ref, dsink in ((lo, sa_ref, la_ref, dsinks[0]), (hi, sb_ref, lb_ref, dsinks[1])):
                qm = _keep(sel, q2)
                dom = _keep(sel, do2)
                dsum = jnp.sum(jnp.where(sel, prod, 0.0), axis=-1, keepdims=True)
                lse = l_ref[pl.ds(qs, WINDOW), :]
                s = jnp.where(valid, _dot(qm, kb, NT), NEG)
                p = jnp.exp(s - lse)
                ds = p * (_dot(dom, vb, NT) - dsum)
                dsb = ds.astype(BF16)
                dqs.append(_dot(dsb, kb, NN))
                dk = dk + _dot(dsb, qm, TN)
                dv = dv + _dot(p.astype(BF16), dom, TN)
                new.append(dsink - jnp.sum(jnp.exp(s_ref[...] - lse) * dsum, axis=0, keepdims=True))
            dq_ref[pl.ds(qs, WINDOW), :] = jnp.where(lo, dqs[0], dqs[1]) * scale
            dk_ref[pl.ds(ks, 2 * WINDOW), :] += dk
            dv_ref[pl.ds(ks, 2 * WINDOW), :] += dv
            return tuple(new)

        dsa, dsb_ = lax.fori_loop(0, T // WINDOW, block, (jnp.zeros((1, 1), F32), jnp.zeros((1, 1), F32)))
        dsa_ref[...] = dsa
        dsb_ref[...] = dsb_

    full = pl.BlockSpec((T, LANES), lambda h: (0, h))
    kv = pl.BlockSpec((T, LANES), lambda h: (0, h // 2))
    sink = lambda off: pl.BlockSpec((None, 1, 1), lambda h: (2 * h + off, 0, 0))
    lspec = pl.BlockSpec((None, T, 1), lambda h: (h, 0, 0))
    dsink = pl.BlockSpec((None, 1, 1), lambda h: (h, 0, 0))
    grad = jax.ShapeDtypeStruct((T, Dh), F32)
    ds_shape = jax.ShapeDtypeStruct((HP, 1, 1), F32)
    return _call(
        body, name=name, grid=(HP,), out_shape=(grad, grad, grad, ds_shape, ds_shape),
        in_specs=[full, kv, kv, sink(0), sink(1), full, full, lspec, lspec],
        out_specs=(full, full, full, dsink, dsink),
        args=[q, kd, vd, sinks, sinks, o, do, lse_a, lse_b], carry=carry)


def _place():
    return lax.axis_index("x"), lax.axis_index("y"), lax.axis_index("c")


def _run_carry(name, carry):
    c_in, c_out = len(carry.inputs), len(carry.out_shapes)

    def body(*refs):
        ins, outs, scr = refs[:c_in], refs[c_in:c_in + c_out], refs[c_in + c_out:]
        carry.start(ins, outs, scr)
        carry.finish(ins, outs, scr)

    return pl.pallas_call(
        body, out_shape=tuple(carry.out_shapes), in_specs=[_HBM] * c_in, out_specs=tuple([_HBM] * c_out),
        scratch_shapes=carry.scratch, name=name)(*carry.inputs)


def _gather_carry(shards):
    n = len(shards)

    def plan(ins, outs, scr):
        send, recv, local = scr
        x, y, c = _place()
        me, sibling = (x, y, c), (x, y, 1 - c)
        chips = [(1 - x, y), (x, 1 - y), (1 - x, 1 - y)]

        def copy(w, k, block, to, src=None):
            slot = 4 * block[0] + 2 * block[1] + block[2]
            return pltpu.make_async_remote_copy(
                src_ref=outs[w].at[slot] if src is None else src, dst_ref=outs[w].at[slot],
                send_sem=send.at[w, k], recv_sem=recv.at[w, k], device_id=to, device_id_type=MESH)

        own = [pltpu.make_async_copy(ins[w], outs[w].at[4 * x + 2 * y + c], local.at[w]) for w in range(n)]
        first = []
        for w in range(n):
            first.append(copy(w, 0, me, sibling, src=ins[w]))
            first += [copy(w, 1 + j, me, (*chip, c), src=ins[w]) for j, chip in enumerate(chips)]
        return copy, own, first, me, sibling, chips, c

    def start(ins, outs, scr):
        _, own, first, _, _, _, _ = plan(ins, outs, scr)
        for cp in own + first:
            cp.start()

    def finish(ins, outs, scr):
        copy, own, first, me, sibling, chips, c = plan(ins, outs, scr)
        passed = []
        for w in range(n):
            for j, chip in enumerate(chips):
                copy(w, 1 + j, (*chip, c), me).wait_recv()
                fwd = copy(w, 4 + j, (*chip, c), sibling)
                fwd.start()
                passed.append(fwd)
        for w in range(n):
            copy(w, 0, sibling, me).wait_recv()
            for j, chip in enumerate(chips):
                copy(w, 4 + j, (*chip, 1 - c), me).wait_recv()
        for cp in first + passed:
            cp.wait_send()
        for cp in own:
            cp.wait()

    return _Carry(shards, [jax.ShapeDtypeStruct((N_DEV,) + s.shape, s.dtype) for s in shards],
                  [pltpu.SemaphoreType.DMA((n, 7)), pltpu.SemaphoreType.DMA((n, 7)), pltpu.SemaphoreType.DMA((n,))], start, finish)


def _sibling_carry(grads):
    n = len(grads)

    def copies(ins, outs, scr):
        send, recv = scr
        x, y, c = _place()
        return [pltpu.make_async_remote_copy(
            src_ref=ins[w].at[2 * q + (1 - c)], dst_ref=outs[w].at[q], send_sem=send.at[w, q], recv_sem=recv.at[w, q],
            device_id=(x, y, 1 - c), device_id_type=MESH) for w in range(n) for q in range(4)]

    def start(ins, outs, scr):
        for cp in copies(ins, outs, scr):
            cp.start()

    def finish(ins, outs, scr):
        for cp in copies(ins, outs, scr):
            cp.wait()

    return _Carry(grads, [jax.ShapeDtypeStruct((4,) + g.shape[1:], g.dtype) for g in grads],
                  [pltpu.SemaphoreType.DMA((n, 4)), pltpu.SemaphoreType.DMA((n, 4))], start, finish)


def _chips_carry(sums):
    n = len(sums)

    def copies(ins, outs, scr):
        send, recv = scr
        x, y, c = _place()
        chips = [(1 - x, y), (x, 1 - y), (1 - x, 1 - y)]
        return [pltpu.make_async_remote_copy(
            src_ref=ins[w].at[2 * chip[0] + chip[1]], dst_ref=outs[w].at[k], send_sem=send.at[w, k], recv_sem=recv.at[w, k],
            device_id=(*chip, c), device_id_type=MESH) for w in range(n) for k, chip in enumerate(chips)]

    def start(ins, outs, scr):
        for cp in copies(ins, outs, scr):
            cp.start()

    def finish(ins, outs, scr):
        for cp in copies(ins, outs, scr):
            cp.wait()

    return _Carry(sums, [jax.ShapeDtypeStruct((3,) + s.shape[1:], s.dtype) for s in sums],
                  [pltpu.SemaphoreType.DMA((n, 3)), pltpu.SemaphoreType.DMA((n, 3))], start, finish)


def _gather_small(packed):
    R, C = packed.shape

    def body(in_ref, out_ref, send, recv):
        x, y, c = _place()
        mine = 4 * x + 2 * y + c
        out_ref[mine] = in_ref[...]
        copies = []
        for k in range(1, N_DEV):
            peer = (x ^ (k >> 2), y ^ ((k >> 1) & 1), c ^ (k & 1))
            copies.append(pltpu.make_async_remote_copy(
                src_ref=in_ref, dst_ref=out_ref.at[mine], send_sem=send.at[k - 1], recv_sem=recv.at[k - 1],
                device_id=peer, device_id_type=MESH))
        for cp in copies:
            cp.start()
        for cp in copies:
            cp.wait()

    vmem = pl.BlockSpec(memory_space=pltpu.VMEM)
    return pl.pallas_call(
        body, out_shape=jax.ShapeDtypeStruct((N_DEV, R, C), F32), in_specs=[vmem], out_specs=vmem,
        scratch_shapes=[pltpu.SemaphoreType.DMA((N_DEV - 1,)), pltpu.SemaphoreType.DMA((N_DEV - 1,))],
        name="small_grads_all_gather")(packed)


def _adamw(w, g, m, v):
    m = ADAM_B1 * m + (1.0 - ADAM_B1) * g
    v = ADAM_B2 * v + (1.0 - ADAM_B2) * (g * g)
    m_hat = m / (1.0 - ADAM_B1 ** ADAM_STEP)
    v_hat = v / (1.0 - ADAM_B2 ** ADAM_STEP)
    delta = -ADAM_LR * (m_hat / (jnp.sqrt(v_hat) + ADAM_EPS) + ADAM_WD * w)
    return delta, m, v


def _pair_add(name, grads, received, c_idx):
    _, R, C = grads.shape
    tr = _row_tile(R)

    def body(c_ref, g_ref, r_ref, o_ref):
        o_ref[...] = (g_ref[...].astype(F32) + r_ref[...].astype(F32)).astype(BF16)

    blk = pl.BlockSpec((None, tr, C), lambda q, i, c: (q, i, 0))
    return pl.pallas_call(
        body, out_shape=jax.ShapeDtypeStruct((4, R, C), BF16),
        grid_spec=pltpu.PrefetchScalarGridSpec(
            num_scalar_prefetch=1, grid=(4, R // tr),
            in_specs=[pl.BlockSpec((None, tr, C), lambda q, i, c: (2 * q + c[0], i, 0)), blk], out_specs=blk),
        name=name, compiler_params=_params(2))(c_idx, grads, received)


def _adam_shard(name, sums, received, w, m, v, chip_idx):
    R, C = w.shape
    tr = _row_tile(R, 128)

    def body(q_ref, s_ref, r_ref, w_ref, m_ref, v_ref, g_out, d_out, m_out, v_out):
        g = s_ref[...].astype(F32) + r_ref[0].astype(F32) + r_ref[1].astype(F32) + r_ref[2].astype(F32)
        delta, mn, vn = _adamw(w_ref[...], g, m_ref[...], v_ref[...])
        g_out[...] = g
        d_out[...] = delta
        m_out[...] = mn
        v_out[...] = vn

    blk = pl.BlockSpec((tr, C), lambda i, q: (i, 0))
    shape = jax.ShapeDtypeStruct((R, C), F32)
    return pl.pallas_call(
        body, out_shape=(shape,) * 4,
        grid_spec=pltpu.PrefetchScalarGridSpec(
            num_scalar_prefetch=1, grid=(R // tr,),
            in_specs=[pl.BlockSpec((None, tr, C), lambda i, q: (q[0], i, 0)), pl.BlockSpec((3, tr, C), lambda i, q: (0, i, 0)),
                      blk, blk, blk],
            out_specs=(blk,) * 4),
        name=name, compiler_params=_params(1))(chip_idx, sums, received, w, m, v)


def _adam_small(name, gathered, w, m, v):
    R, C = w.shape

    def body(ga_ref, w_ref, m_ref, v_ref, g_out, d_out, m_out, v_out):
        g = ga_ref[0]
        for d in range(1, N_DEV):
            g = g + ga_ref[d]
        delta, mn, vn = _adamw(w_ref[...], g, m_ref[...], v_ref[...])
        g_out[...] = g
        d_out[...] = delta
        m_out[...] = mn
        v_out[...] = vn

    full = pl.BlockSpec((R, C), lambda i: (0, 0))
    shape = jax.ShapeDtypeStruct((R, C), F32)
    return pl.pallas_call(
        body, out_shape=(shape,) * 4, grid=(1,),
        in_specs=[pl.BlockSpec((N_DEV, R, C), lambda i: (0, 0, 0)), full, full, full], out_specs=(full,) * 4,
        name=name, compiler_params=_params(1))(gathered, w, m, v)


def _pack_small(parts, D):
    g1, gmix, g2, gof, gos, bf, gqf, gkf, gqs, gks, sinks = [p.reshape(-1).astype(F32) for p in parts]
    row3 = jnp.concatenate([gof, gos])
    row4 = jnp.zeros((D,), F32)
    for slot, vec in enumerate((bf, gqf, gkf, gqs, gks, sinks)):
        row4 = lax.dynamic_update_slice(row4, vec, (slot * LANES,))
    zero = jnp.zeros((D,), F32)
    return jnp.stack([g1, gmix, g2, row3, row4, zero, zero, zero])


def _unpack_small(packed, D, H):
    Dh = D // 2
    row4 = packed[4]
    short = [row4[s * LANES:s * LANES + n] for s, n in enumerate((H, HEAD_DIM, HEAD_DIM, HEAD_DIM, HEAD_DIM, H))]
    vecs = [packed[0], packed[1], packed[2], packed[3, :Dh], packed[3, Dh:]] + short
    return [v[None, :] for v in vecs]


def kernel(x, positions, norm_ffn1_g, ffn1_w_gate, ffn1_w_up, ffn1_w_down, norm_mix_g, w_in, b_forget, fox_q_norm_g, fox_k_norm_g, swa_q_norm_g, swa_k_norm_g, swa_sinks, out_norm_fox_g, out_norm_swa_g, w_out, norm_ffn2_g, ffn2_w_gate, ffn2_w_up, ffn2_w_down, loss_target, m_norm_ffn1_g, m_ffn1_w_gate, m_ffn1_w_up, m_ffn1_w_down, m_norm_mix_g, m_w_in, m_b_forget, m_fox_q_norm_g, m_fox_k_norm_g, m_swa_q_norm_g, m_swa_k_norm_g, m_swa_sinks, m_out_norm_fox_g, m_out_norm_swa_g, m_w_out, m_norm_ffn2_g, m_ffn2_w_gate, m_ffn2_w_up, m_ffn2_w_down, v_norm_ffn1_g, v_ffn1_w_gate, v_ffn1_w_up, v_ffn1_w_down, v_norm_mix_g, v_w_in, v_b_forget, v_fox_q_norm_g, v_fox_k_norm_g, v_swa_q_norm_g, v_swa_k_norm_g, v_swa_sinks, v_out_norm_fox_g, v_out_norm_swa_g, v_w_out, v_norm_ffn2_g, v_ffn2_w_gate, v_ffn2_w_up, v_ffn2_w_down):
    xs = x[0]
    target = loss_target[0]
    T, D = xs.shape
    Dh = D // 2
    H = Dh // HEAD_DIM
    HP = H // 2
    KVW = (H // GQA_GROUP) * HEAD_DIM
    KVB = KVW // LANES
    MAIN = 4 * Dh + 2 * KVW
    F_OFF = 3 * Dh
    tm = min(ROW_TILE_CAP, T)
    tq = min(256, T)
    tk = min(512, T)
    nk = T // tk
    cx, cy, cc = _place()
    c_idx = jnp.reshape(cc, (1,)).astype(jnp.int32)
    chip_idx = jnp.reshape(2 * cx + cy, (1,)).astype(jnp.int32)

    tr = jnp.transpose
    big_w = [tr(ffn1_w_gate[0]), tr(ffn1_w_up[0]), ffn1_w_down[0], w_in[0], w_out[0], tr(ffn2_w_gate[0]), tr(ffn2_w_up[0]), ffn2_w_down[0]]
    big_m = [tr(m_ffn1_w_gate[0]), tr(m_ffn1_w_up[0]), m_ffn1_w_down[0], m_w_in[0], m_w_out[0], tr(m_ffn2_w_gate[0]), tr(m_ffn2_w_up[0]),
             m_ffn2_w_down[0]]
    big_v = [tr(v_ffn1_w_gate[0]), tr(v_ffn1_w_up[0]), v_ffn1_w_down[0], v_w_in[0], v_w_out[0], tr(v_ffn2_w_gate[0]), tr(v_ffn2_w_up[0]),
             v_ffn2_w_down[0]]
    transposed = {"ffn1_w_gate", "ffn1_w_up", "ffn2_w_gate", "ffn2_w_up"}
    names = ["ffn1_w_gate", "ffn1_w_up", "ffn1_w_down", "w_in", "w_out", "ffn2_w_gate", "ffn2_w_up", "ffn2_w_down"]
    sh = dict(zip(names, [w.astype(BF16) for w in big_w]))
    wg1, wu1 = _run_carry("weights_all_gather", _gather_carry([sh["ffn1_w_gate"], sh["ffn1_w_up"]]))

    lane = jnp.arange(LANES)
    inv_freq = ROPE_THETA ** (-(2.0 * (lane % (HEAD_DIM // 2))).astype(F32) / HEAD_DIM)
    ang = positions[0].astype(F32)[:, None] * inv_freq[None, :]
    cos_t = jnp.cos(ang)
    sin_t = jnp.where((lane & (HEAD_DIM // 2)) == 0, -1.0, 1.0)[None, :] * jnp.sin(ang)
    rope = (cos_t, sin_t)

    def pair_gain(g, blocks):
        return jnp.tile(jnp.concatenate([g[0], g[0]])[None, None, :], (blocks, 1, 1))

    n1 = _rmsnorm_fwd("ffn1_norm", xs, norm_ffn1_g, tm)
    (a1, b1, hm1), (wd1,) = _ffn_up("ffn1_up", n1, wg1, wu1, tm, carry=_gather_carry([sh["ffn1_w_down"]]))
    h1, (win_g,) = _ffn_down("ffn1_down", hm1, wd1, xs, tm, carry=_gather_carry([sh["w_in"]]))
    n_in = win_g.shape[2]
    win_full = jnp.transpose(win_g, (1, 0, 2)).reshape(D, N_DEV * n_in)
    win_main = jnp.concatenate([win_full[:, :F_OFF], win_full[:, F_OFF + H:]], axis=1)
    win_f = jnp.pad(win_full[:, F_OFF:F_OFF + H], ((0, 0), (0, LANES - H)))

    u = _rmsnorm_fwd("mix_norm", h1, norm_mix_g, tm)
    proj, (wout_g,) = _mm_nn("mix_proj", u, win_main, tm, MAIN // 9, carry=_gather_carry([sh["w_out"]]))
    wout = wout_g.reshape(D, D)
    proj_f = _mm_nn("mix_proj_forget", u, win_f, tm, LANES)
    scale = HEAD_DIM ** -0.5
    fox_gains = jnp.concatenate([pair_gain(fox_q_norm_g, HP), pair_gain(fox_k_norm_g, HP)])
    qk_f = _headnorm_fwd_scaled("fox_qk_norm", proj, 0, 2 * HP, fox_gains, tm, scale, HP)
    v_f = proj[:, 2 * Dh:3 * Dh].astype(BF16)
    c_t, sg_t = _forget_fwd("forget_gates", proj_f[:, :H].T, b_forget.reshape(H, 1))
    ccol = c_t[:, :, None]
    crow = c_t.reshape(H, nk, 1, tk)
    (o_fox, lse_fa, lse_fb), (wg2, wu2) = _fox_fwd("fox_attention", qk_f, v_f, ccol, crow, tq, tk,
                                                   carry=_gather_carry([sh["ffn2_w_gate"], sh["ffn2_w_up"]]))

    swa_q_gains = pair_gain(swa_q_norm_g, HP)
    swa_k_gains = pair_gain(swa_k_norm_g, KVB)
    q_s = _headnorm_fwd("swa_q_norm", proj, 3 * HP, HP, swa_q_gains, tm, scale, rope=rope)
    k_d = _headnorm_fwd("swa_k_norm", proj, 4 * HP, KVB, swa_k_gains, tm, 1.0, rope=rope, dup=True)
    v_s = proj[:, 4 * Dh + KVW:].astype(BF16).reshape(T, H // GQA_GROUP, 1, HEAD_DIM)
    v_d = jnp.broadcast_to(v_s, (T, H // GQA_GROUP, 2, HEAD_DIM)).reshape(T, 2 * KVW)
    sinks3 = swa_sinks.reshape(H, 1, 1)
    (o_swa, lse_sa, lse_sb), (wd2,) = _swa_fwd("swa_attention", q_s, k_d, v_d, sinks3, carry=_gather_carry([sh["ffn2_w_down"]]))

    on = _outnorm_fwd("out_norm", o_fox, o_swa, out_norm_fox_g, out_norm_swa_g, tm)
    h2 = _mm_nn("mix_out", on, wout, tm, min(512, D), resid=h1)

    n2 = _rmsnorm_fwd("ffn2_norm", h2, norm_ffn2_g, tm)
    a2, b2, hm2 = _ffn_up("ffn2_up", n2, wg2, wu2, tm)
    y = _ffn_down("ffn2_down", hm2, wd2, h2, tm)
    dy, dyh, sq = _loss_grad("loss_grad", y, target, min(256, T))
    loss = lax.psum(0.5 * sq[0, 0] / D, ("x", "y", "c"))

    J, Fs, _ = wg2.shape
    aspec = pl.BlockSpec((None, tm, Fs), lambda i, j: (j, i, 0))
    wspec = pl.BlockSpec((None, Fs, D), lambda i, j: (j, 0, 0))

    def pair_sums(keys, grads, received):
        return [_pair_add("sum_" + nm, g, r, c_idx) for nm, g, r in zip(keys, grads, received)]

    da2, db2 = _ffn_bwd_mid("ffn2_bwd_mid", dyh, wd2, a2, b2, tm)
    dwd2 = _wgrad_down("ffn2_wgrad_down", hm2, dyh, min(1024, D))
    dwg2, dwu2 = _wgrad_up("ffn2_wgrad_up", n2, da2, db2, min(1024, D))
    dn2, sib2 = _reduce_mm("ffn2_bwd_in", [(da2, aspec, wg2, wspec), (db2, aspec, wu2, wspec)], [], NN, T, D, tm, J,
                           carry=_sibling_carry([dwg2, dwu2, dwd2]))
    dh2, dg_ffn2, dh2b = _rmsnorm_bwd("ffn2_norm_bwd", dn2, h2, norm_ffn2_g, dy, min(256, T), 1.0)
    sum_wg2, sum_wu2, sum_wd2 = pair_sums(names[5:8], [dwg2, dwu2, dwd2], sib2)

    dwout = _wgrad_2d("mix_out_wgrad", on, dh2b, min(512, D), min(1024, D))
    do_fox, dg_of = _outnorm_bwd("out_norm_bwd_fox", dh2b, wout, 0, o_fox, out_norm_fox_g, tm)
    do_swa, dg_os = _outnorm_bwd("out_norm_bwd_swa", dh2b, wout, 1, o_swa, out_norm_swa_g, tm)

    (dq_f, dk_f, dv_f, dc_a, dc_b, dr_a, dr_b), (rc_wg2, rc_wu2) = _fox_bwd(
        "fox_attention_bwd", qk_f, v_f, o_fox, do_fox, ccol, crow, lse_fa, lse_fb, tq, tk, carry=_chips_carry([sum_wg2, sum_wu2]))
    dqk_f = jnp.concatenate([dq_f, dk_f], axis=1)
    dqk_raw, dg_fox = _headnorm_bwd("fox_qk_norm_bwd", dqk_f, proj, 0, 2 * HP, fox_gains, HP, tm, 1.0)
    dct = jnp.stack([dc_a.reshape(HP, T), dc_b.reshape(HP, T)], axis=1).reshape(H, T)
    drt = jnp.stack([dr_a.reshape(HP, T), dr_b.reshape(HP, T)], axis=1).reshape(H, T)
    dz_t, db_f = _forget_bwd("forget_gates_bwd", dct, drt, sg_t)

    (dq_s, dk_p, dv_p, dsink_a, dsink_b), (rc_wd2,) = _swa_bwd(
        "swa_attention_bwd", q_s, k_d, v_d, sinks3, o_swa, do_swa, lse_sa, lse_sb, carry=_chips_carry([sum_wd2]))
    dqs_raw, dg_sq = _headnorm_bwd("swa_q_norm_bwd", dq_s, proj, 3 * HP, HP, swa_q_gains, HP, tm, 1.0, rope=rope)
    dks_raw, dg_sk = _headnorm_bwd("swa_k_norm_bwd", dk_p, proj, 4 * HP, KVB, swa_k_gains, KVB, tm, 1.0, rope=rope, fold=True)
    dvs_raw, _ = _headnorm_bwd("swa_v_fold", dv_p, None, 0, KVB, None, KVB, tm, 1.0, fold=True, norm=False)

    dproj = jnp.concatenate([dqk_raw, dv_f.astype(BF16), dqs_raw, dks_raw, dvs_raw], axis=1)
    dproj_f = jnp.pad(dz_t.T, ((0, 0), (0, LANES - H))).astype(BF16)
    dwin_main = _wgrad_2d("mix_proj_wgrad", u, dproj, min(1024, D), MAIN // 9)
    dwin_f = _wgrad_2d("mix_proj_forget_wgrad", u, dproj_f, min(1024, D), LANES)
    dwin_full = jnp.concatenate([dwin_main[:, :F_OFF], dwin_f[:, :H], dwin_main[:, F_OFF:]], axis=1)
    dwin_g = jnp.transpose(dwin_full.reshape(D, N_DEV, n_in), (1, 0, 2))
    dwout_g = dwout.reshape(N_DEV, D // N_DEV, D)
    tkb = MAIN // 9
    du, sib_mix = _reduce_mm(
        "mix_bwd_in",
        [(dproj, pl.BlockSpec((tm, tkb), lambda i, r: (i, r)), win_main, pl.BlockSpec((D, tkb), lambda i, r: (0, r)))],
        [(dproj_f, pl.BlockSpec((tm, LANES), lambda i, r: (i, 0)), win_f, pl.BlockSpec((D, LANES), lambda i, r: (0, 0)))],
        NT, T, D, tm, 9, carry=_sibling_carry([dwin_g, dwout_g]))
    dh1, dg_mix, dh1h = _rmsnorm_bwd("mix_norm_bwd", du, h1, norm_mix_g, dh2, min(256, T), 0.5)
    sum_win, sum_wout = pair_sums(names[3:5], [dwin_g, dwout_g], sib_mix)

    (da1, db1), (rc_win,) = _ffn_bwd_mid("ffn1_bwd_mid", dh1h, wd1, a1, b1, tm, carry=_chips_carry([sum_win]))
    (dwg1, dwu1), (rc_wout,) = _wgrad_up("ffn1_wgrad_up", n1, da1, db1, min(1024, D), carry=_chips_carry([sum_wout]))
    dwd1, sib_gu = _wgrad_down("ffn1_wgrad_down", hm1, dh1h, min(1024, D), carry=_sibling_carry([dwg1, dwu1]))
    sum_wg1, sum_wu1 = pair_sums(names[0:2], [dwg1, dwu1], sib_gu)
    dn1, (rc_wg1, rc_wu1, sib_d) = _reduce_mm(
        "ffn1_bwd_in", [(da1, aspec, wg1, wspec), (db1, aspec, wu1, wspec)], [], NN, T, D, tm, J,
        carry=_join(_chips_carry([sum_wg1, sum_wu1]), _sibling_carry([dwd1])))
    (sum_wd1,) = pair_sums(names[2:3], [dwd1], [sib_d])
    (dx, dg_ffn1), (rc_wd1,) = _rmsnorm_bwd("ffn1_norm_bwd", dn1, xs, norm_ffn1_g, dh1, min(256, T), None,
                                            carry=_chips_carry([sum_wd1]))

    chip_sums = [sum_wg1, sum_wu1, sum_wd1, sum_win, sum_wout, sum_wg2, sum_wu2, sum_wd2]
    from_chips = [rc_wg1, rc_wu1, rc_wd1, rc_win, rc_wout, rc_wg2, rc_wu2, rc_wd2]
    big_out = [_adam_shard("adam_" + nm, s, r, w, m, v, chip_idx)
               for nm, s, r, w, m, v in zip(names, chip_sums, from_chips, big_w, big_m, big_v)]

    dsinks = jnp.stack([dsink_a.reshape(HP), dsink_b.reshape(HP)], axis=1).reshape(H)
    small_g = [dg_ffn1, dg_mix, dg_ffn2, dg_of, dg_os, db_f, dg_fox[0, 0, :HEAD_DIM], dg_fox[1, 0, :HEAD_DIM],
               dg_sq[0, 0, :HEAD_DIM], dg_sk[0, 0, :HEAD_DIM], dsinks]
    small_w = [norm_ffn1_g, norm_mix_g, norm_ffn2_g, out_norm_fox_g, out_norm_swa_g, b_forget, fox_q_norm_g, fox_k_norm_g,
               swa_q_norm_g, swa_k_norm_g, swa_sinks]
    small_m = [m_norm_ffn1_g, m_norm_mix_g, m_norm_ffn2_g, m_out_norm_fox_g, m_out_norm_swa_g, m_b_forget, m_fox_q_norm_g,
               m_fox_k_norm_g, m_swa_q_norm_g, m_swa_k_norm_g, m_swa_sinks]
    small_v = [v_norm_ffn1_g, v_norm_mix_g, v_norm_ffn2_g, v_out_norm_fox_g, v_out_norm_swa_g, v_b_forget, v_fox_q_norm_g,
               v_fox_k_norm_g, v_swa_q_norm_g, v_swa_k_norm_g, v_swa_sinks]
    gathered = _gather_small(_pack_small(small_g, D))
    small_out = _adam_small("adam_small", gathered, _pack_small(small_w, D), _pack_small(small_m, D), _pack_small(small_v, D))
    small_out = [_unpack_small(p, D, H) for p in small_out]

    order = ["norm_ffn1_g", "ffn1_w_gate", "ffn1_w_up", "ffn1_w_down", "norm_mix_g", "w_in", "b_forget", "fox_q_norm_g", "fox_k_norm_g",
             "swa_q_norm_g", "swa_k_norm_g", "swa_sinks", "out_norm_fox_g", "out_norm_swa_g", "w_out", "norm_ffn2_g",
             "ffn2_w_gate", "ffn2_w_up", "ffn2_w_down"]
    small_names = ["norm_ffn1_g", "norm_mix_g", "norm_ffn2_g", "out_norm_fox_g", "out_norm_swa_g", "b_forget", "fox_q_norm_g",
                   "fox_k_norm_g", "swa_q_norm_g", "swa_k_norm_g", "swa_sinks"]
    result = [loss, dx[None]]
    for kind in range(4):
        for nm in order:
            if nm in names:
                leaf = big_out[names.index(nm)][kind]
                result.append((tr(leaf) if nm in transposed else leaf)[None])
            else:
                result.append(small_out[kind][small_names.index(nm)])
    return tuple(result)


def _headnorm_fwd_scaled(name, proj, col_off, ncb, gains, tm, scale, n_scaled):
    T = proj.shape[0]

    def body(x_ref, g_ref, o_ref):
        xv = x_ref[...]
        lo = _lane_lo(xv.shape)
        y = xv * _head_rstd(xv, lo) * g_ref[...]
        y = y * jnp.where(pl.program_id(0) < n_scaled, scale, 1.0)
        o_ref[...] = y.astype(BF16)

    return pl.pallas_call(
        body, out_shape=jax.ShapeDtypeStruct((T, ncb * LANES), BF16), grid=(ncb, T // tm),
        in_specs=[pl.BlockSpec((tm, LANES), lambda c, i: (i, col_off + c)), pl.BlockSpec((None, 1, LANES), lambda c, i: (c, 0, 0))],
        out_specs=pl.BlockSpec((tm, LANES), lambda c, i: (i, c)), name=name, compiler_params=_params(2))(proj, gains)
```

```python
import functools

import jax
import jax.numpy as jnp
from jax import lax
from jax.experimental import pallas as pl
from jax.experimental.pallas import tpu as pltpu

F32 = jnp.float32
BF16 = jnp.bfloat16

HEAD_DIM = 64
LANES = 128
WINDOW = 128
GQA_GROUP = 4
EPS = 1e-6
ROPE_THETA = 10000.0
ADAM_LR = 0.001
ADAM_B1 = 0.9
ADAM_B2 = 0.999
ADAM_EPS = 1e-08
ADAM_WD = 0.01
ADAM_STEP = 10
N_DEV = 8
NEG = -1e30
VMEM_LIMIT_V7X = 48 * 1024 * 1024
ROW_TILE_CAP = 512
MESH = pl.DeviceIdType.MESH

NN = (((1,), (0,)), ((), ()))
NT = (((1,), (1,)), ((), ()))
TN = (((0,), (0,)), ((), ()))


def _dot(a, b, dims):
    return lax.dot_general(a, b, dims, preferred_element_type=F32)


def _params(n_axes):
    return pltpu.CompilerParams(dimension_semantics=("arbitrary",) * n_axes, vmem_limit_bytes=VMEM_LIMIT_V7X)


def _row_tile(rows, cap=ROW_TILE_CAP):
    best = None
    for t in range(16, min(rows, cap) + 1, 16):
        if rows % t == 0:
            best = t
    return best or rows


def _lane_lo(shape):
    return lax.broadcasted_iota(jnp.int32, shape, len(shape) - 1) < HEAD_DIM


def _keep(sel, x):
    return jnp.where(sel, x.astype(F32), 0.0).astype(BF16)


_HBM = pl.BlockSpec(memory_space=pltpu.HBM)


class _Carry:
    def __init__(self, inputs, out_shapes, scratch, start, finish):
        self.inputs, self.out_shapes, self.scratch, self.start, self.finish = list(inputs), list(out_shapes), list(scratch), start, finish


def _join(*carries):
    def hook(which):
        def run(ins, outs, scr):
            i = o = s = 0
            for c in carries:
                ni, no, ns = len(c.inputs), len(c.out_shapes), len(c.scratch)
                getattr(c, which)(ins[i:i + ni], outs[o:o + no], scr[s:s + ns])
                i, o, s = i + ni, o + no, s + ns
        return run

    return _Carry([a for c in carries for a in c.inputs], [a for c in carries for a in c.out_shapes],
                  [a for c in carries for a in c.scratch], hook("start"), hook("finish"))


def _call(body, *, name, grid, in_specs, out_specs, out_shape, args, scratch_shapes=(), carry=None):
    params = _params(len(grid))
    if carry is None:
        return pl.pallas_call(body, out_shape=out_shape, grid=grid, in_specs=list(in_specs), out_specs=out_specs,
                              scratch_shapes=list(scratch_shapes), name=name, compiler_params=params)(*args)
    single = not isinstance(out_shape, (tuple, list))
    shapes = (out_shape,) if single else tuple(out_shape)
    specs = (out_specs,) if single else tuple(out_specs)
    n_in, n_out, n_scr = len(args), len(shapes), len(scratch_shapes)
    c_in, c_out = len(carry.inputs), len(carry.out_shapes)

    def wrapped(*refs):
        ins, c_ins = refs[:n_in], refs[n_in:n_in + c_in]
        o0 = n_in + c_in
        outs, c_outs = refs[o0:o0 + n_out], refs[o0 + n_out:o0 + n_out + c_out]
        s0 = o0 + n_out + c_out
        scr, c_scr = refs[s0:s0 + n_scr], refs[s0 + n_scr:]
        first = pl.program_id(0) == 0
        last = pl.program_id(0) == grid[0] - 1
        for ax in range(1, len(grid)):
            first = jnp.logical_and(first, pl.program_id(ax) == 0)
            last = jnp.logical_and(last, pl.program_id(ax) == grid[ax] - 1)

        @pl.when(first)
        def _():
            carry.start(c_ins, c_outs, c_scr)

        body(*ins, *outs, *scr)

        @pl.when(last)
        def _():
            carry.finish(c_ins, c_outs, c_scr)

    res = pl.pallas_call(
        wrapped, out_shape=shapes + tuple(carry.out_shapes), grid=grid, in_specs=list(in_specs) + [_HBM] * c_in,
        out_specs=specs + (_HBM,) * c_out, scratch_shapes=list(scratch_shapes) + carry.scratch, name=name,
        compiler_params=params)(*args, *carry.inputs)
    main = res[:n_out]
    return (main[0] if single else tuple(main)), tuple(res[n_out:])


def _rms_bwd(dn, x, g):
    r = lax.rsqrt(jnp.mean(x * x, axis=-1, keepdims=True) + EPS)
    xh = x * r
    dxh = dn * g
    dx = r * (dxh - xh * jnp.mean(dxh * xh, axis=-1, keepdims=True))
    return dx, jnp.sum(dn * xh, axis=0, keepdims=True)


def _rmsnorm_fwd(name, x, g, tm):
    T, D = x.shape

    def body(x_ref, g_ref, o_ref):
        xf = x_ref[...]
        r = lax.rsqrt(jnp.mean(xf * xf, axis=-1, keepdims=True) + EPS)
        o_ref[...] = (xf * r * g_ref[...]).astype(BF16)

    return pl.pallas_call(
        body, out_shape=jax.ShapeDtypeStruct((T, D), BF16), grid=(T // tm,),
        in_specs=[pl.BlockSpec((tm, D), lambda i: (i, 0)), pl.BlockSpec((1, D), lambda i: (0, 0))],
        out_specs=pl.BlockSpec((tm, D), lambda i: (i, 0)), name=name, compiler_params=_params(1))(x, g)


def _outnorm_fwd(name, o_fox, o_swa, g_fox, g_swa, tm):
    T, Dh = o_fox.shape

    def body(a_ref, b_ref, ga_ref, gb_ref, o_ref):
        for ref, g_ref, lo in ((a_ref, ga_ref, 0), (b_ref, gb_ref, Dh)):
            xf = ref[...]
            r = lax.rsqrt(jnp.mean(xf * xf, axis=-1, keepdims=True) + EPS)
            o_ref[:, lo:lo + Dh] = (xf * r * g_ref[...]).astype(BF16)

    row = pl.BlockSpec((tm, Dh), lambda i: (i, 0))
    gain = pl.BlockSpec((1, Dh), lambda i: (0, 0))
    return pl.pallas_call(
        body, out_shape=jax.ShapeDtypeStruct((T, 2 * Dh), BF16), grid=(T // tm,),
        in_specs=[row, row, gain, gain], out_specs=pl.BlockSpec((tm, 2 * Dh), lambda i: (i, 0)),
        name=name, compiler_params=_params(1))(o_fox, o_swa, g_fox, g_swa)


def _outnorm_bwd(name, dhb, wout, half, o, g, tm):
    T, D = dhb.shape
    Dh = o.shape[1]

    def body(a_ref, w_ref, o_ref, g_ref, do_ref, dg_ref):
        don = _dot(a_ref[...], w_ref[...], NT)
        dx, dg = _rms_bwd(don, o_ref[...], g_ref[...])
        do_ref[...] = dx.astype(BF16)

        @pl.when(pl.program_id(0) == 0)
        def _():
            dg_ref[...] = dg

        @pl.when(pl.program_id(0) > 0)
        def _():
            dg_ref[...] += dg

    return pl.pallas_call(
        body, out_shape=(jax.ShapeDtypeStruct((T, Dh), BF16), jax.ShapeDtypeStruct((1, Dh), F32)), grid=(T // tm,),
        in_specs=[pl.BlockSpec((tm, D), lambda i: (i, 0)), pl.BlockSpec((Dh, D), lambda i: (half, 0)),
                  pl.BlockSpec((tm, Dh), lambda i: (i, 0)), pl.BlockSpec((1, Dh), lambda i: (0, 0))],
        out_specs=(pl.BlockSpec((tm, Dh), lambda i: (i, 0)), pl.BlockSpec((1, Dh), lambda i: (0, 0))),
        name=name, compiler_params=_params(1))(dhb, wout, o, g)


def _mm_nn(name, a, b, tm, tn, resid=None, carry=None):
    M, K = a.shape
    N = b.shape[1]

    def body(*refs):
        if resid is None:
            a_ref, b_ref, o_ref = refs
            o_ref[...] = _dot(a_ref[...], b_ref[...], NN)
        else:
            a_ref, b_ref, r_ref, o_ref = refs
            o_ref[...] = r_ref[...] + _dot(a_ref[...], b_ref[...], NN)

    ospec = pl.BlockSpec((tm, tn), lambda n, i: (i, n))
    in_specs = [pl.BlockSpec((tm, K), lambda n, i: (i, 0)), pl.BlockSpec((K, tn), lambda n, i: (0, n))]
    args = [a, b]
    if resid is not None:
        in_specs.append(ospec)
        args.append(resid)
    return _call(body, name=name, grid=(N // tn, M // tm), in_specs=in_specs, out_specs=ospec,
                 out_shape=jax.ShapeDtypeStruct((M, N), F32), args=args, carry=carry)


def _wgrad_2d(name, a, b, tmm, tn):
    T, M = a.shape
    N = b.shape[1]

    def body(a_ref, b_ref, o_ref):
        o_ref[...] = _dot(a_ref[...], b_ref[...], TN).astype(BF16)

    return pl.pallas_call(
        body, out_shape=jax.ShapeDtypeStruct((M, N), BF16), grid=(M // tmm, N // tn),
        in_specs=[pl.BlockSpec((T, tmm), lambda m, n: (0, m)), pl.BlockSpec((T, tn), lambda m, n: (0, n))],
        out_specs=pl.BlockSpec((tmm, tn), lambda m, n: (m, n)), name=name, compiler_params=_params(2))(a, b)


def _wgrad_down(name, hm, df, tn, carry=None):
    J, T, Fs = hm.shape
    D = df.shape[1]

    def body(a_ref, b_ref, o_ref):
        o_ref[...] = _dot(a_ref[...], b_ref[...], TN).astype(BF16)

    return _call(
        body, name=name, grid=(J, D // tn), out_shape=jax.ShapeDtypeStruct((J, Fs, D), BF16),
        in_specs=[pl.BlockSpec((None, T, Fs), lambda j, n: (j, 0, 0)), pl.BlockSpec((T, tn), lambda j, n: (0, n))],
        out_specs=pl.BlockSpec((None, Fs, tn), lambda j, n: (j, 0, n)), args=[hm, df], carry=carry)


def _wgrad_up(name, n, da, db, tn, carry=None):
    T, D = n.shape
    J, _, Fs = da.shape

    def body(n_ref, da_ref, db_ref, og_ref, ou_ref):
        nv = n_ref[...]
        og_ref[...] = _dot(da_ref[...], nv, TN).astype(BF16)
        ou_ref[...] = _dot(db_ref[...], nv, TN).astype(BF16)

    act = pl.BlockSpec((None, T, Fs), lambda j, m: (j, 0, 0))
    out = pl.BlockSpec((None, Fs, tn), lambda j, m: (j, 0, m))
    shape = jax.ShapeDtypeStruct((J, Fs, D), BF16)
    return _call(
        body, name=name, grid=(J, D // tn), out_shape=(shape, shape),
        in_specs=[pl.BlockSpec((T, tn), lambda j, m: (0, m)), act, act], out_specs=(out, out),
        args=[n, da, db], carry=carry)


def _reduce_mm(name, pairs, once, dims, T, D, tm, steps, carry=None):
    n_pairs = len(pairs)
    n_once = len(once)

    def body(*refs):
        pr = refs[:2 * n_pairs]
        on = refs[2 * n_pairs:2 * (n_pairs + n_once)]
        o_ref, acc = refs[-2:]
        r = pl.program_id(1)
        part = _dot(pr[0][...], pr[1][...], dims)
        for p in range(1, n_pairs):
            part = part + _dot(pr[2 * p][...], pr[2 * p + 1][...], dims)

        @pl.when(r == 0)
        def _():
            acc[...] = part

        @pl.when(r > 0)
        def _():
            acc[...] += part

        @pl.when(r == steps - 1)
        def _():
            dn = acc[...]
            for p in range(n_once):
                dn = dn + _dot(on[2 * p][...], on[2 * p + 1][...], dims)
            o_ref[...] = dn

    in_specs, args = [], []
    for a, a_spec, w, w_spec in list(pairs) + list(once):
        in_specs += [a_spec, w_spec]
        args += [a, w]
    row = pl.BlockSpec((tm, D), lambda i, r: (i, 0))
    return _call(body, name=name, grid=(T // tm, steps), in_specs=in_specs, out_specs=row, out_shape=jax.ShapeDtypeStruct((T, D), F32),
                 args=args, scratch_shapes=[pltpu.VMEM((tm, D), F32)], carry=carry)


def _rmsnorm_bwd(name, dn, x, g, dh, tm, bf16_scale, carry=None):
    T, D = x.shape
    emit_bf16 = bf16_scale is not None

    def body(dn_ref, x_ref, g_ref, dh_ref, *outs):
        dxn, dg = _rms_bwd(dn_ref[...], x_ref[...], g_ref[...])
        dx = dh_ref[...] + dxn
        outs[0][...] = dx
        if emit_bf16:
            outs[2][...] = (bf16_scale * dx).astype(BF16)

        @pl.when(pl.program_id(0) == 0)
        def _():
            outs[1][...] = dg

        @pl.when(pl.program_id(0) > 0)
        def _():
            outs[1][...] += dg

    row = pl.BlockSpec((tm, D), lambda i: (i, 0))
    gain = pl.BlockSpec((1, D), lambda i: (0, 0))
    out_shape = [jax.ShapeDtypeStruct((T, D), F32), jax.ShapeDtypeStruct((1, D), F32)]
    out_specs = [row, gain]
    if emit_bf16:
        out_shape.append(jax.ShapeDtypeStruct((T, D), BF16))
        out_specs.append(row)
    return _call(body, name=name, grid=(T // tm,), in_specs=[row, row, gain, row], out_specs=tuple(out_specs),
                 out_shape=tuple(out_shape), args=[dn, x, g, dh], carry=carry)


def _loss_grad(name, y, target, tm):
    T, D = y.shape

    def body(y_ref, t_ref, dy_ref, dyh_ref, sq_ref):
        diff = y_ref[...] - t_ref[...]
        sq = jnp.sum(jnp.sum(diff * diff, axis=1, keepdims=True), axis=0, keepdims=True)
        dy = diff * (1.0 / D)
        dy_ref[...] = dy
        dyh_ref[...] = (0.5 * dy).astype(BF16)

        @pl.when(pl.program_id(0) == 0)
        def _():
            sq_ref[...] = sq

        @pl.when(pl.program_id(0) > 0)
        def _():
            sq_ref[...] += sq

    row = pl.BlockSpec((tm, D), lambda i: (i, 0))
    return pl.pallas_call(
        body, out_shape=(jax.ShapeDtypeStruct((T, D), F32), jax.ShapeDtypeStruct((T, D), BF16), jax.ShapeDtypeStruct((1, 1), F32)),
        grid=(T // tm,), in_specs=[row, row], out_specs=(row, row, pl.BlockSpec((1, 1), lambda i: (0, 0))),
        name=name, compiler_params=_params(1))(y, target)


def _ffn_up(name, n, wg, wu, tm, carry=None):
    T, D = n.shape
    J, Fs, _ = wg.shape

    def body(n_ref, wg_ref, wu_ref, a_ref, b_ref, h_ref):
        xv = n_ref[...]
        a = _dot(xv, wg_ref[...], NT)
        b = _dot(xv, wu_ref[...], NT)
        a_ref[...] = a.astype(BF16)
        b_ref[...] = b.astype(BF16)
        h_ref[...] = (a * jax.nn.sigmoid(a) * b).astype(BF16)

    act = jax.ShapeDtypeStruct((J, T, Fs), BF16)
    wspec = pl.BlockSpec((None, Fs, D), lambda j, i: (j, 0, 0))
    aspec = pl.BlockSpec((None, tm, Fs), lambda j, i: (j, i, 0))
    return _call(
        body, name=name, grid=(J, T // tm), out_shape=(act, act, act),
        in_specs=[pl.BlockSpec((tm, D), lambda j, i: (i, 0)), wspec, wspec], out_specs=(aspec, aspec, aspec),
        args=[n, wg, wu], carry=carry)


def _ffn_down(name, hm, wd, resid, tm, carry=None):
    J, T, Fs = hm.shape
    D = wd.shape[2]

    def body(h_ref, w_ref, r_ref, o_ref, acc):
        j = pl.program_id(1)
        part = _dot(h_ref[...], w_ref[...], NN)

        @pl.when(j == 0)
        def _():
            acc[...] = part

        @pl.when(j > 0)
        def _():
            acc[...] += part

        @pl.when(j == J - 1)
        def _():
            o_ref[...] = r_ref[...] + 0.5 * acc[...]

    row = pl.BlockSpec((tm, D), lambda i, j: (i, 0))
    return _call(
        body, name=name, grid=(T // tm, J), out_shape=jax.ShapeDtypeStruct((T, D), F32),
        in_specs=[pl.BlockSpec((None, tm, Fs), lambda i, j: (j, i, 0)), pl.BlockSpec((None, Fs, D), lambda i, j: (j, 0, 0)), row],
        out_specs=row, scratch_shapes=[pltpu.VMEM((tm, D), F32)], args=[hm, wd, resid], carry=carry)


def _ffn_bwd_mid(name, dfh, wd, a, b, tm, carry=None):
    T, D = dfh.shape
    J, Fs, _ = wd.shape

    def body(df_ref, w_ref, a_ref, b_ref, da_ref, db_ref):
        dhm = _dot(df_ref[...], w_ref[...], NT)
        av = a_ref[...].astype(F32)
        bv = b_ref[...].astype(F32)
        sg = jax.nn.sigmoid(av)
        da_ref[...] = (dhm * bv * (sg * (1.0 + av * (1.0 - sg)))).astype(BF16)
        db_ref[...] = (dhm * (av * sg)).astype(BF16)

    act = jax.ShapeDtypeStruct((J, T, Fs), BF16)
    aspec = pl.BlockSpec((None, tm, Fs), lambda j, i: (j, i, 0))
    return _call(
        body, name=name, grid=(J, T // tm), out_shape=(act, act),
        in_specs=[pl.BlockSpec((tm, D), lambda j, i: (i, 0)), pl.BlockSpec((None, Fs, D), lambda j, i: (j, 0, 0)), aspec, aspec],
        out_specs=(aspec, aspec), args=[dfh, wd, a, b], carry=carry)


def _rot_half(y, lane):
    first = (lane & (HEAD_DIM // 2)) == 0
    return jnp.where(first, pltpu.roll(y, LANES - HEAD_DIM // 2, 1), pltpu.roll(y, HEAD_DIM // 2, 1))


def _head_rstd(x, lo):
    sq = x * x
    ss_a = jnp.sum(jnp.where(lo, sq, 0.0), axis=-1, keepdims=True)
    ss_b = jnp.sum(jnp.where(lo, 0.0, sq), axis=-1, keepdims=True)
    return lax.rsqrt(jnp.where(lo, ss_a, ss_b) * (1.0 / HEAD_DIM) + EPS)


def _headnorm_fwd(name, proj, col_off, ncb, gains, tm, scale, rope=None, dup=False):
    T = proj.shape[0]
    with_rope = rope is not None
    width = 2 * LANES if dup else LANES

    def body(*refs):
        if with_rope:
            x_ref, g_ref, cos_ref, sin_ref, o_ref = refs
        else:
            x_ref, g_ref, o_ref = refs
        xv = x_ref[...]
        lane = lax.broadcasted_iota(jnp.int32, xv.shape, 1)
        lo = lane < HEAD_DIM
        y = xv * _head_rstd(xv, lo) * g_ref[...]
        if with_rope:
            y = y * cos_ref[...] + _rot_half(y, lane) * sin_ref[...]
        y = y * scale
        if dup:
            sw = pltpu.roll(y, HEAD_DIM, 1)
            o_ref[:, :LANES] = jnp.where(lo, y, sw).astype(BF16)
            o_ref[:, LANES:] = jnp.where(lo, sw, y).astype(BF16)
        else:
            o_ref[...] = y.astype(BF16)

    in_specs = [pl.BlockSpec((tm, LANES), lambda c, i: (i, col_off + c)), pl.BlockSpec((None, 1, LANES), lambda c, i: (c, 0, 0))]
    args = [proj, gains]
    if with_rope:
        tab = pl.BlockSpec((tm, LANES), lambda c, i: (i, 0))
        in_specs += [tab, tab]
        args += list(rope)
    return pl.pallas_call(
        body, out_shape=jax.ShapeDtypeStruct((T, ncb * width), BF16), grid=(ncb, T // tm),
        in_specs=in_specs, out_specs=pl.BlockSpec((tm, width), lambda c, i: (i, c)),
        name=name, compiler_params=_params(2))(*args)


def _headnorm_bwd(name, dy, proj, col_off, ncb, gains, group, tm, scale, rope=None, fold=False, norm=True):
    T = dy.shape[0]
    with_rope = rope is not None
    n_groups = ncb // group
    dy_width = 4 * LANES if fold else LANES

    def body(*refs):
        refs = list(refs)
        dy_ref = refs.pop(0)
        x_ref = refs.pop(0) if norm else None
        g_ref = refs.pop(0) if norm else None
        cos_ref = refs.pop(0) if with_rope else None
        sin_ref = refs.pop(0) if with_rope else None
        dx_ref = refs.pop(0)
        dg_ref = refs.pop(0) if norm else None
        c = pl.program_id(0)
        i = pl.program_id(1)
        d = dy_ref[...]
        lane = lax.broadcasted_iota(jnp.int32, (d.shape[0], LANES), 1)
        lo = lane < HEAD_DIM
        if fold:
            t0 = d[:, 0:LANES] + d[:, LANES:2 * LANES]
            t1 = d[:, 2 * LANES:3 * LANES] + d[:, 3 * LANES:4 * LANES]
            d = jnp.where(lo, t0 + pltpu.roll(t0, HEAD_DIM, 1), t1 + pltpu.roll(t1, HEAD_DIM, 1))
        d = d * scale
        if with_rope:
            d = d * cos_ref[...] + _rot_half(d * sin_ref[...], lane)
        if not norm:
            dx_ref[...] = d.astype(BF16)
            return
        xv = x_ref[...]
        gv = g_ref[...]
        r = _head_rstd(xv, lo)
        xh = xv * r
        dxh = d * gv
        pr = dxh * xh
        m_a = jnp.sum(jnp.where(lo, pr, 0.0), axis=-1, keepdims=True)
        m_b = jnp.sum(jnp.where(lo, 0.0, pr), axis=-1, keepdims=True)
        mean = jnp.where(lo, m_a, m_b) * (1.0 / HEAD_DIM)
        dx_ref[...] = (r * (dxh - xh * mean)).astype(BF16)
        dgp = jnp.sum(d * xh, axis=0, keepdims=True)
        dgp = dgp + pltpu.roll(dgp, HEAD_DIM, 1)
        first = jnp.logical_and(c % group == 0, i == 0)

        @pl.when(first)
        def _():
            dg_ref[...] = dgp

        @pl.when(jnp.logical_not(first))
        def _():
            dg_ref[...] += dgp

    in_specs = [pl.BlockSpec((tm, dy_width), lambda c, i: (i, c))]
    args = [dy]
    if norm:
        in_specs += [pl.BlockSpec((tm, LANES), lambda c, i: (i, col_off + c)), pl.BlockSpec((None, 1, LANES), lambda c, i: (c, 0, 0))]
        args += [proj, gains]
    if with_rope:
        tab = pl.BlockSpec((tm, LANES), lambda c, i: (i, 0))
        in_specs += [tab, tab]
        args += list(rope)
    out_shape = [jax.ShapeDtypeStruct((T, ncb * LANES), BF16)]
    out_specs = [pl.BlockSpec((tm, LANES), lambda c, i: (i, c))]
    if norm:
        out_shape.append(jax.ShapeDtypeStruct((n_groups, 1, LANES), F32))
        out_specs.append(pl.BlockSpec((None, 1, LANES), lambda c, i: (c // group, 0, 0)))
    res = pl.pallas_call(
        body, out_shape=tuple(out_shape), grid=(ncb, T // tm), in_specs=in_specs, out_specs=tuple(out_specs),
        name=name, compiler_params=_params(2))(*args)
    return res if norm else (res[0], None)


def _dot_exact(x, tri):
    hi = x.astype(BF16)
    r1 = x - hi.astype(F32)
    mid = r1.astype(BF16)
    lo = (r1 - mid.astype(F32)).astype(BF16)
    return _dot(hi, tri, NN) + _dot(mid, tri, NN) + _dot(lo, tri, NN)


def _forget_fwd(name, zt, bias):
    H, T = zt.shape
    blk = min(256, T)

    def body(z_ref, b_ref, c_ref, s_ref):
        z = z_ref[...] + b_ref[...]
        s_ref[...] = jax.nn.sigmoid(-z)
        lf = jnp.minimum(z, 0.0) - jnp.log(1.0 + jnp.exp(-jnp.abs(z)))
        tri = (lax.broadcasted_iota(jnp.int32, (blk, blk), 0) <= lax.broadcasted_iota(jnp.int32, (blk, blk), 1)).astype(BF16)
        carry = jnp.zeros((H, 1), F32)
        for bi in range(T // blk):
            xb = lf[:, bi * blk:(bi + 1) * blk]
            c_ref[:, bi * blk:(bi + 1) * blk] = _dot_exact(xb, tri) + carry
            carry = carry + jnp.sum(xb, axis=-1, keepdims=True)

    shape = jax.ShapeDtypeStruct((H, T), F32)
    full = pl.BlockSpec((H, T), lambda i: (0, 0))
    return pl.pallas_call(
        body, out_shape=(shape, shape), grid=(1,), in_specs=[full, pl.BlockSpec((H, 1), lambda i: (0, 0))],
        out_specs=(full, full), name=name, compiler_params=_params(1))(zt, bias)


def _forget_bwd(name, dct, drt, sgt):
    H, T = dct.shape
    blk = min(256, T)

    def body(dc_ref, dr_ref, s_ref, dz_ref, db_ref):
        dc = dc_ref[...] + dr_ref[...]
        tri = (lax.broadcasted_iota(jnp.int32, (blk, blk), 0) >= lax.broadcasted_iota(jnp.int32, (blk, blk), 1)).astype(BF16)
        carry = jnp.zeros((H, 1), F32)
        db = jnp.zeros((H, 1), F32)
        for bi in reversed(range(T // blk)):
            xb = dc[:, bi * blk:(bi + 1) * blk]
            dz = (_dot_exact(xb, tri) + carry) * s_ref[:, bi * blk:(bi + 1) * blk]
            dz_ref[:, bi * blk:(bi + 1) * blk] = dz
            db = db + jnp.sum(dz, axis=-1, keepdims=True)
            carry = carry + jnp.sum(xb, axis=-1, keepdims=True)
        db_ref[...] = db

    full = pl.BlockSpec((H, T), lambda i: (0, 0))
    return pl.pallas_call(
        body, out_shape=(jax.ShapeDtypeStruct((H, T), F32), jax.ShapeDtypeStruct((H, 1), F32)), grid=(1,),
        in_specs=[full, full, full], out_specs=(full, pl.BlockSpec((H, 1), lambda i: (0, 0))),
        name=name, compiler_params=_params(1))(dct, drt, sgt)


STRIP = 256


def _fox_fwd(name, qk, v, crow, tq, tk, carry=None):
    T, Dh = v.shape
    HP = Dh // LANES
    nk = T // tk
    assert tk % tq == 0 and tq % STRIP == 0
    n_strips = tq // STRIP

    def body(q_ref, k_ref, v_ref, ra_ref, rb_ref, o_ref, la_ref, lb_ref, s_ref, p_ref, m_ref, l_ref, acc_ref):
        i = pl.program_id(1)
        q2 = q_ref[...]
        lo = _lane_lo((tq, LANES))
        qms = (_keep(lo, q2), _keep(jnp.logical_not(lo), q2))
        r_refs = (ra_ref, rb_ref)
        m_ref[...] = jnp.full(m_ref.shape, NEG, F32)
        l_ref[...] = jnp.zeros(l_ref.shape, F32)
        acc_ref[...] = jnp.zeros(acc_ref.shape, F32)
        rel = lax.broadcasted_iota(jnp.int32, (STRIP, tk), 0) - lax.broadcasted_iota(jnp.int32, (STRIP, tk), 1)

        def chunk(kc, masked):
            start = pl.multiple_of(kc * tk, tk)
            kb = k_ref[pl.ds(start, tk), :]
            vb = v_ref[pl.ds(start, tk), :]
            for h in range(2):
                s_ref[h] = _dot(qms[h], kb, NT)
            for h in range(2):
                cs = r_refs[h][kc]
                for st in range(n_strips):
                    rows = pl.ds(st * STRIP, STRIP)
                    s = s_ref[h, rows, :] - cs
                    if masked:
                        s = jnp.where(rel >= start - (i * tq + st * STRIP), s, NEG)
                    m_old = m_ref[h, rows, :]
                    mn = jnp.maximum(m_old, jnp.max(s, axis=-1, keepdims=True))
                    p = jnp.exp(s - mn)
                    alpha = jnp.exp(m_old - mn)
                    l_ref[h, rows, :] = alpha * l_ref[h, rows, :] + jnp.sum(p, axis=-1, keepdims=True)
                    m_ref[h, rows, :] = mn
                    p_ref[h, rows, :] = p.astype(BF16)
                    acc_ref[h, rows, :] = acc_ref[h, rows, :] * alpha
            for h in range(2):
                acc_ref[h] += _dot(p_ref[h], vb, NN)

        n_full = (i * tq) // tk

        def full_chunk(kc, _):
            chunk(kc, False)
            return 0

        lax.fori_loop(0, n_full, full_chunk, 0)
        chunk(n_full, True)
        o_ref[...] = jnp.where(lo, acc_ref[0] / l_ref[0], acc_ref[1] / l_ref[1])
        la_ref[...] = m_ref[0] + jnp.log(l_ref[0])
        lb_ref[...] = m_ref[1] + jnp.log(l_ref[1])

    row = lambda off: pl.BlockSpec((None, nk, 1, tk), lambda h, i: (2 * h + off, 0, 0, 0))
    lse = jax.ShapeDtypeStruct((HP, T, 1), F32)
    lspec = pl.BlockSpec((None, tq, 1), lambda h, i: (h, i, 0))
    scratch = [pltpu.VMEM((2, tq, tk), F32), pltpu.VMEM((2, tq, tk), BF16), pltpu.VMEM((2, tq, 1), F32),
               pltpu.VMEM((2, tq, 1), F32), pltpu.VMEM((2, tq, LANES), F32)]
    return _call(
        body, name=name, grid=(HP, T // tq), out_shape=(jax.ShapeDtypeStruct((T, Dh), F32), lse, lse),
        in_specs=[pl.BlockSpec((tq, LANES), lambda h, i: (i, h)), pl.BlockSpec((T, LANES), lambda h, i: (0, HP + h)),
                  pl.BlockSpec((T, LANES), lambda h, i: (0, h)), row(0), row(1)],
        out_specs=(pl.BlockSpec((tq, LANES), lambda h, i: (i, h)), lspec, lspec),
        args=[qk, qk, v, crow, crow], scratch_shapes=scratch, carry=carry)


def _fox_bwd(name, qk, v, o, do, crow, lse_a, lse_b, tq, tk, carry=None):
    T, Dh = v.shape
    HP = Dh // LANES
    nk = T // tk
    scale = HEAD_DIM ** -0.5
    assert tk % tq == 0 and tq % STRIP == 0
    n_strips = tq // STRIP

    def body(q_ref, k_ref, v_ref, o_ref, do_ref, ra_ref, rb_ref, la_ref, lb_ref,
             dq_ref, dk_ref, dv_ref, dca_ref, dcb_ref, dra_ref, drb_ref, s_ref, dp_ref, p_ref, ds_ref, dq_acc, dsum_ref):
        i = pl.program_id(1)

        @pl.when(i == 0)
        def _():
            dk_ref[...] = jnp.zeros_like(dk_ref)
            dv_ref[...] = jnp.zeros_like(dv_ref)
            dca_ref[...] = jnp.zeros_like(dca_ref)
            dcb_ref[...] = jnp.zeros_like(dcb_ref)

        q2 = q_ref[...]
        do2 = do_ref[...]
        lo = _lane_lo((tq, LANES))
        hi = jnp.logical_not(lo)
        qms = (_keep(lo, q2), _keep(hi, q2))
        doms = (_keep(lo, do2), _keep(hi, do2))
        prod = do2.astype(F32) * o_ref[...]
        dsum_ref[0] = jnp.sum(jnp.where(lo, prod, 0.0), axis=-1, keepdims=True)
        dsum_ref[1] = jnp.sum(jnp.where(lo, 0.0, prod), axis=-1, keepdims=True)
        r_refs, l_refs, dc_refs, dr_refs = (ra_ref, rb_ref), (la_ref, lb_ref), (dca_ref, dcb_ref), (dra_ref, drb_ref)
        dq_acc[...] = jnp.zeros(dq_acc.shape, F32)
        dra_ref[...] = jnp.zeros(dra_ref.shape, F32)
        drb_ref[...] = jnp.zeros(drb_ref.shape, F32)
        rel = lax.broadcasted_iota(jnp.int32, (STRIP, tk), 0) - lax.broadcasted_iota(jnp.int32, (STRIP, tk), 1)

        def chunk(kc, masked):
            start = pl.multiple_of(kc * tk, tk)
            kb = k_ref[pl.ds(start, tk), :]
            vb = v_ref[pl.ds(start, tk), :]
            for h in range(2):
                s_ref[h] = _dot(qms[h], kb, NT)
                dp_ref[h] = _dot(doms[h], vb, NT)
            for h in range(2):
                cs = r_refs[h][kc]
                col_sum = jnp.zeros((1, tk), F32)
                for st in range(n_strips):
                    rows = pl.ds(st * STRIP, STRIP)
                    s = s_ref[h, rows, :] - cs
                    if masked:
                        s = jnp.where(rel >= start - (i * tq + st * STRIP), s, NEG)
                    p = jnp.exp(s - l_refs[h][rows, :])
                    ds = p * (dp_ref[h, rows, :] - dsum_ref[h, rows, :])
                    p_ref[h, rows, :] = p.astype(BF16)
                    ds_ref[h, rows, :] = ds.astype(BF16)
                    col_sum = col_sum + jnp.sum(ds, axis=0, keepdims=True)
                    dr_refs[h][rows, :] += jnp.sum(ds, axis=-1, keepdims=True)
                dc_refs[h][kc] = dc_refs[h][kc] - col_sum
            dk = _dot(ds_ref[0], qms[0], TN) + _dot(ds_ref[1], qms[1], TN)
            dv = _dot(p_ref[0], doms[0], TN) + _dot(p_ref[1], doms[1], TN)
            dk_ref[pl.ds(start, tk), :] += dk
            dv_ref[pl.ds(start, tk), :] += dv
            for h in range(2):
                dq_acc[h] += _dot(ds_ref[h], kb, NN)

        n_full = (i * tq) // tk

        def full_chunk(kc, _):
            chunk(kc, False)
            return 0

        lax.fori_loop(0, n_full, full_chunk, 0)
        chunk(n_full, True)
        dq_ref[...] = jnp.where(lo, dq_acc[0], dq_acc[1]) * scale

    row = lambda off: pl.BlockSpec((None, nk, 1, tk), lambda h, i: (2 * h + off, 0, 0, 0))
    lspec = pl.BlockSpec((None, tq, 1), lambda h, i: (h, i, 0))
    qspec = pl.BlockSpec((tq, LANES), lambda h, i: (i, h))
    full = pl.BlockSpec((T, LANES), lambda h, i: (0, h))
    dcspec = pl.BlockSpec((None, nk, 1, tk), lambda h, i: (h, 0, 0, 0))
    grad = jax.ShapeDtypeStruct((T, Dh), F32)
    dc = jax.ShapeDtypeStruct((HP, nk, 1, tk), F32)
    dr = jax.ShapeDtypeStruct((HP, T, 1), F32)
    scratch = [pltpu.VMEM((2, tq, tk), F32), pltpu.VMEM((2, tq, tk), F32), pltpu.VMEM((2, tq, tk), BF16), pltpu.VMEM((2, tq, tk), BF16),
               pltpu.VMEM((2, tq, LANES), F32), pltpu.VMEM((2, tq, 1), F32)]
    return _call(
        body, name=name, grid=(HP, T // tq), out_shape=(grad, grad, grad, dc, dc, dr, dr),
        in_specs=[qspec, pl.BlockSpec((T, LANES), lambda h, i: (0, HP + h)), full, qspec, qspec, row(0), row(1), lspec, lspec],
        out_specs=(qspec, full, full, dcspec, dcspec, lspec, lspec),
        args=[qk, qk, v, o, do, crow, crow, lse_a, lse_b], scratch_shapes=scratch, carry=carry)


def _swa_block(n, q_ref, k_ref):
    qs = pl.multiple_of(n * WINDOW, WINDOW)
    ks = pl.multiple_of(jnp.maximum(n - 1, 0) * WINDOW, WINDOW)
    rel = (qs + lax.broadcasted_iota(jnp.int32, (WINDOW, 2 * WINDOW), 0)) - (ks + lax.broadcasted_iota(jnp.int32, (WINDOW, 2 * WINDOW), 1))
    valid = jnp.logical_and(rel >= 0, rel < WINDOW)
    return qs, ks, valid


def _swa_fwd(name, q, kd, vd, sinks, carry=None):
    T, Dh = q.shape
    HP = Dh // LANES

    def body(q_ref, k_ref, v_ref, sa_ref, sb_ref, o_ref, la_ref, lb_ref):
        lo = _lane_lo((WINDOW, LANES))

        def block(n, _):
            qs, ks, valid = _swa_block(n, q_ref, k_ref)
            q2 = q_ref[pl.ds(qs, WINDOW), :]
            kb = k_ref[pl.ds(ks, 2 * WINDOW), :]
            vb = v_ref[pl.ds(ks, 2 * WINDOW), :]
            res = []
            for sel, s_ref in ((lo, sa_ref), (jnp.logical_not(lo), sb_ref)):
                qm = _keep(sel, q2)
                sink = s_ref[...]
                s = jnp.where(valid, _dot(qm, kb, NT), NEG)
                m = jnp.maximum(jnp.max(s, axis=-1, keepdims=True), sink)
                p = jnp.exp(s - m)
                l = jnp.sum(p, axis=-1, keepdims=True) + jnp.exp(sink - m)
                res.append((_dot(p.astype(BF16), vb, NN) / l, m + jnp.log(l)))
            o_ref[pl.ds(qs, WINDOW), :] = jnp.where(lo, res[0][0], res[1][0])
            la_ref[pl.ds(qs, WINDOW), :] = res[0][1]
            lb_ref[pl.ds(qs, WINDOW), :] = res[1][1]
            return 0

        lax.fori_loop(0, T // WINDOW, block, 0, unroll=2)

    full = pl.BlockSpec((T, LANES), lambda h: (0, h))
    kv = pl.BlockSpec((T, LANES), lambda h: (0, h // 2))
    sink = lambda off: pl.BlockSpec((None, 1, 1), lambda h: (2 * h + off, 0, 0))
    lse = jax.ShapeDtypeStruct((HP, T, 1), F32)
    lspec = pl.BlockSpec((None, T, 1), lambda h: (h, 0, 0))
    return _call(
        body, name=name, grid=(HP,), out_shape=(jax.ShapeDtypeStruct((T, Dh), F32), lse, lse),
        in_specs=[full, kv, kv, sink(0), sink(1)], out_specs=(full, lspec, lspec),
        args=[q, kd, vd, sinks, sinks], carry=carry)


def _swa_bwd(name, q, kd, vd, sinks, o, do, lse_a, lse_b, carry=None):
    T, Dh = q.shape
    HP = Dh // LANES
    scale = HEAD_DIM ** -0.5

    def body(q_ref, k_ref, v_ref, sa_ref, sb_ref, o_ref, do_ref, la_ref, lb_ref, dq_ref, dk_ref, dv_ref, dsa_ref, dsb_ref):
        lo = _lane_lo((WINDOW, LANES))
        hi = jnp.logical_not(lo)
        dk_ref[...] = jnp.zeros_like(dk_ref)
        dv_ref[...] = jnp.zeros_like(dv_ref)

        def block(n, dsinks):
            qs, ks, valid = _swa_block(n, q_ref, k_ref)
            q2 = q_ref[pl.ds(qs, WINDOW), :]
            do2 = do_ref[pl.ds(qs, WINDOW), :]
            kb = k_ref[pl.ds(ks, 2 * WINDOW), :]
            vb = v_ref[pl.ds(ks, 2 * WINDOW), :]
            prod = do2.astype(F32) * o_ref[pl.ds(qs, WINDOW), :]
            dqs, new = [], []
            dk = jnp.zeros((2 * WINDOW, LANES), F32)
            dv = jnp.zeros((2 * WINDOW, LANES), F32)
            for sel, s_ref, l_ref, dsink in ((lo, sa_ref, la_ref, dsinks[0]), (hi, sb_ref, lb_ref, dsinks[1])):
                qm = _keep(sel, q2)
                dom = _keep(sel, do2)
                dsum = jnp.sum(jnp.where(sel, prod, 0.0), axis=-1, keepdims=True)
                lse = l_ref[pl.ds(qs, WINDOW), :]
                s = jnp.where(valid, _dot(qm, kb, NT), NEG)
                p = jnp.exp(s - lse)
                ds = p * (_dot(dom, vb, NT) - dsum)
                dsb = ds.astype(BF16)
                dqs.append(_dot(dsb, kb, NN))
                dk = dk + _dot(dsb, qm, TN)
                dv = dv + _dot(p.astype(BF16), dom, TN)
                new.append(dsink - jnp.sum(jnp.exp(s_ref[...] - lse) * dsum, axis=0, keepdims=True))
            dq_ref[pl.ds(qs, WINDOW), :] = jnp.where(lo, dqs[0], dqs[1]) * scale
            dk_ref[pl.ds(ks, 2 * WINDOW), :] += dk
            dv_ref[pl.ds(ks, 2 * WINDOW), :] += dv
            return tuple(new)

        dsa, dsb_ = lax.fori_loop(0, T // WINDOW, block, (jnp.zeros((1, 1), F32), jnp.zeros((1, 1), F32)), unroll=2)
        dsa_ref[...] = dsa
        dsb_ref[...] = dsb_

    full = pl.BlockSpec((T, LANES), lambda h: (0, h))
    kv = pl.BlockSpec((T, LANES), lambda h: (0, h // 2))
    sink = lambda off: pl.BlockSpec((None, 1, 1), lambda h: (2 * h + off, 0, 0))
    lspec = pl.BlockSpec((None, T, 1), lambda h: (h, 0, 0))
    dsink = pl.BlockSpec((None, 1, 1), lambda h: (h, 0, 0))
    grad = jax.ShapeDtypeStruct((T, Dh), F32)
    ds_shape = jax.ShapeDtypeStruct((HP, 1, 1), F32)
    return _call(
        body, name=name, grid=(HP,), out_shape=(grad, grad, grad, ds_shape, ds_shape),
        in_specs=[full, kv, kv, sink(0), sink(1), full, full, lspec, lspec],
        out_specs=(full, full, full, dsink, dsink),
        args=[q, kd, vd, sinks, sinks, o, do, lse_a, lse_b], carry=carry)


def _place():
    return lax.axis_index("x"), lax.axis_index("y"), lax.axis_index("c")


def _run_carry(name, carry):
    c_in, c_out = len(carry.inputs), len(carry.out_shapes)

    def body(*refs):
        ins, outs, scr = refs[:c_in], refs[c_in:c_in + c_out], refs[c_in + c_out:]
        carry.start(ins, outs, scr)
        carry.finish(ins, outs, scr)

    return pl.pallas_call(
        body, out_shape=tuple(carry.out_shapes), in_specs=[_HBM] * c_in, out_specs=tuple([_HBM] * c_out),
        scratch_shapes=carry.scratch, name=name)(*carry.inputs)


def _gather_carry(shards):
    n = len(shards)

    def plan(ins, outs, scr):
        send, recv, local = scr
        x, y, c = _place()
        me, sibling = (x, y, c), (x, y, 1 - c)
        chips = [(1 - x, y), (x, 1 - y), (1 - x, 1 - y)]

        def copy(w, k, block, to, src=None):
            slot = 4 * block[0] + 2 * block[1] + block[2]
            return pltpu.make_async_remote_copy(
                src_ref=outs[w].at[slot] if src is None else src, dst_ref=outs[w].at[slot],
                send_sem=send.at[w, k], recv_sem=recv.at[w, k], device_id=to, device_id_type=MESH)

        own = [pltpu.make_async_copy(ins[w], outs[w].at[4 * x + 2 * y + c], local.at[w]) for w in range(n)]
        first = []
        for w in range(n):
            first.append(copy(w, 0, me, sibling, src=ins[w]))
            first += [copy(w, 1 + j, me, (*chip, c), src=ins[w]) for j, chip in enumerate(chips)]
        return copy, own, first, me, sibling, chips, c

    def start(ins, outs, scr):
        _, own, first, _, _, _, _ = plan(ins, outs, scr)
        for cp in own + first:
            cp.start()

    def finish(ins, outs, scr):
        copy, own, first, me, sibling, chips, c = plan(ins, outs, scr)
        passed = []
        for w in range(n):
            for j, chip in enumerate(chips):
                copy(w, 1 + j, (*chip, c), me).wait_recv()
                fwd = copy(w, 4 + j, (*chip, c), sibling)
                fwd.start()
                passed.append(fwd)
        for w in range(n):
            copy(w, 0, sibling, me).wait_recv()
            for j, chip in enumerate(chips):
                copy(w, 4 + j, (*chip, 1 - c), me).wait_recv()
        for cp in first + passed:
            cp.wait_send()
        for cp in own:
            cp.wait()

    return _Carry(shards, [jax.ShapeDtypeStruct((N_DEV,) + s.shape, s.dtype) for s in shards],
                  [pltpu.SemaphoreType.DMA((n, 7)), pltpu.SemaphoreType.DMA((n, 7)), pltpu.SemaphoreType.DMA((n,))], start, finish)


def _sibling_carry(grads):
    n = len(grads)

    def copies(ins, outs, scr):
        send, recv = scr
        x, y, c = _place()
        return [pltpu.make_async_remote_copy(
            src_ref=ins[w].at[2 * q + (1 - c)], dst_ref=outs[w].at[q], send_sem=send.at[w, q], recv_sem=recv.at[w, q],
            device_id=(x, y, 1 - c), device_id_type=MESH) for w in range(n) for q in range(4)]

    def start(ins, outs, scr):
        for cp in copies(ins, outs, scr):
            cp.start()

    def finish(ins, outs, scr):
        for cp in copies(ins, outs, scr):
            cp.wait()

    return _Carry(grads, [jax.ShapeDtypeStruct((4,) + g.shape[1:], g.dtype) for g in grads],
                  [pltpu.SemaphoreType.DMA((n, 4)), pltpu.SemaphoreType.DMA((n, 4))], start, finish)


def _chips_carry(sums):
    n = len(sums)

    def copies(ins, outs, scr):
        send, recv = scr
        x, y, c = _place()
        chips = [(1 - x, y), (x, 1 - y), (1 - x, 1 - y)]
        return [pltpu.make_async_remote_copy(
            src_ref=ins[w].at[2 * chip[0] + chip[1]], dst_ref=outs[w].at[k], send_sem=send.at[w, k], recv_sem=recv.at[w, k],
            device_id=(*chip, c), device_id_type=MESH) for w in range(n) for k, chip in enumerate(chips)]

    def start(ins, outs, scr):
        for cp in copies(ins, outs, scr):
            cp.start()

    def finish(ins, outs, scr):
        for cp in copies(ins, outs, scr):
            cp.wait()

    return _Carry(sums, [jax.ShapeDtypeStruct((3,) + s.shape[1:], s.dtype) for s in sums],
                  [pltpu.SemaphoreType.DMA((n, 3)), pltpu.SemaphoreType.DMA((n, 3))], start, finish)


def _gather_small(packed):
    R, C = packed.shape

    def body(in_ref, out_ref, send, recv):
        x, y, c = _place()
        mine = 4 * x + 2 * y + c
        out_ref[mine] = in_ref[...]
        copies = []
        for k in range(1, N_DEV):
            peer = (x ^ (k >> 2), y ^ ((k >> 1) & 1), c ^ (k & 1))
            copies.append(pltpu.make_async_remote_copy(
                src_ref=in_ref, dst_ref=out_ref.at[mine], send_sem=send.at[k - 1], recv_sem=recv.at[k - 1],
                device_id=peer, device_id_type=MESH))
        for cp in copies:
            cp.start()
        for cp in copies:
            cp.wait()

    vmem = pl.BlockSpec(memory_space=pltpu.VMEM)
    return pl.pallas_call(
        body, out_shape=jax.ShapeDtypeStruct((N_DEV, R, C), F32), in_specs=[vmem], out_specs=vmem,
        scratch_shapes=[pltpu.SemaphoreType.DMA((N_DEV - 1,)), pltpu.SemaphoreType.DMA((N_DEV - 1,))],
        name="small_grads_all_gather")(packed)


def _adamw(w, g, m, v):
    m = ADAM_B1 * m + (1.0 - ADAM_B1) * g
    v = ADAM_B2 * v + (1.0 - ADAM_B2) * (g * g)
    m_hat = m / (1.0 - ADAM_B1 ** ADAM_STEP)
    v_hat = v / (1.0 - ADAM_B2 ** ADAM_STEP)
    delta = -ADAM_LR * (m_hat / (jnp.sqrt(v_hat) + ADAM_EPS) + ADAM_WD * w)
    return delta, m, v


def _pair_add(name, grads, received, c_idx):
    _, R, C = grads.shape
    tr = _row_tile(R)

    def body(c_ref, g_ref, r_ref, o_ref):
        o_ref[...] = (g_ref[...].astype(F32) + r_ref[...].astype(F32)).astype(BF16)

    blk = pl.BlockSpec((None, tr, C), lambda q, i, c: (q, i, 0))
    return pl.pallas_call(
        body, out_shape=jax.ShapeDtypeStruct((4, R, C), BF16),
        grid_spec=pltpu.PrefetchScalarGridSpec(
            num_scalar_prefetch=1, grid=(4, R // tr),
            in_specs=[pl.BlockSpec((None, tr, C), lambda q, i, c: (2 * q + c[0], i, 0)), blk], out_specs=blk),
        name=name, compiler_params=_params(2))(c_idx, grads, received)


def _adam_shard(name, sums, received, w, m, v, chip_idx):
    R, C = w.shape
    tr = _row_tile(R, 128)

    def body(q_ref, s_ref, r_ref, w_ref, m_ref, v_ref, g_out, d_out, m_out, v_out):
        g = s_ref[...].astype(F32) + r_ref[0].astype(F32) + r_ref[1].astype(F32) + r_ref[2].astype(F32)
        delta, mn, vn = _adamw(w_ref[...], g, m_ref[...], v_ref[...])
        g_out[...] = g
        d_out[...] = delta
        m_out[...] = mn
        v_out[...] = vn

    blk = pl.BlockSpec((tr, C), lambda i, q: (i, 0))
    shape = jax.ShapeDtypeStruct((R, C), F32)
    return pl.pallas_call(
        body, out_shape=(shape,) * 4,
        grid_spec=pltpu.PrefetchScalarGridSpec(
            num_scalar_prefetch=1, grid=(R // tr,),
            in_specs=[pl.BlockSpec((None, tr, C), lambda i, q: (q[0], i, 0)), pl.BlockSpec((3, tr, C), lambda i, q: (0, i, 0)),
                      blk, blk, blk],
            out_specs=(blk,) * 4),
        name=name, compiler_params=_params(1))(chip_idx, sums, received, w, m, v)


def _adam_small(name, gathered, w, m, v):
    R, C = w.shape

    def body(ga_ref, w_ref, m_ref, v_ref, g_out, d_out, m_out, v_out):
        g = ga_ref[0]
        for d in range(1, N_DEV):
            g = g + ga_ref[d]
        delta, mn, vn = _adamw(w_ref[...], g, m_ref[...], v_ref[...])
        g_out[...] = g
        d_out[...] = delta
        m_out[...] = mn
        v_out[...] = vn

    full = pl.BlockSpec((R, C), lambda i: (0, 0))
    shape = jax.ShapeDtypeStruct((R, C), F32)
    return pl.pallas_call(
        body, out_shape=(shape,) * 4, grid=(1,),
        in_specs=[pl.BlockSpec((N_DEV, R, C), lambda i: (0, 0, 0)), full, full, full], out_specs=(full,) * 4,
        name=name, compiler_params=_params(1))(gathered, w, m, v)


def _pack_small(parts, D):
    g1, gmix, g2, gof, gos, bf, gqf, gkf, gqs, gks, sinks = [p.reshape(-1).astype(F32) for p in parts]
    row3 = jnp.concatenate([gof, gos])
    row4 = jnp.zeros((D,), F32)
    for slot, vec in enumerate((bf, gqf, gkf, gqs, gks, sinks)):
        row4 = lax.dynamic_update_slice(row4, vec, (slot * LANES,))
    zero = jnp.zeros((D,), F32)
    return jnp.stack([g1, gmix, g2, row3, row4, zero, zero, zero])


def _unpack_small(packed, D, H):
    Dh = D // 2
    row4 = packed[4]
    short = [row4[s * LANES:s * LANES + n] for s, n in enumerate((H, HEAD_DIM, HEAD_DIM, HEAD_DIM, HEAD_DIM, H))]
    vecs = [packed[0], packed[1], packed[2], packed[3, :Dh], packed[3, Dh:]] + short
    return [v[None, :] for v in vecs]


def kernel(x, positions, norm_ffn1_g, ffn1_w_gate, ffn1_w_up, ffn1_w_down, norm_mix_g, w_in, b_forget, fox_q_norm_g, fox_k_norm_g, swa_q_norm_g, swa_k_norm_g, swa_sinks, out_norm_fox_g, out_norm_swa_g, w_out, norm_ffn2_g, ffn2_w_gate, ffn2_w_up, ffn2_w_down, loss_target, m_norm_ffn1_g, m_ffn1_w_gate, m_ffn1_w_up, m_ffn1_w_down, m_norm_mix_g, m_w_in, m_b_forget, m_fox_q_norm_g, m_fox_k_norm_g, m_swa_q_norm_g, m_swa_k_norm_g, m_swa_sinks, m_out_norm_fox_g, m_out_norm_swa_g, m_w_out, m_norm_ffn2_g, m_ffn2_w_gate, m_ffn2_w_up, m_ffn2_w_down, v_norm_ffn1_g, v_ffn1_w_gate, v_ffn1_w_up, v_ffn1_w_down, v_norm_mix_g, v_w_in, v_b_forget, v_fox_q_norm_g, v_fox_k_norm_g, v_swa_q_norm_g, v_swa_k_norm_g, v_swa_sinks, v_out_norm_fox_g, v_out_norm_swa_g, v_w_out, v_norm_ffn2_g, v_ffn2_w_gate, v_ffn2_w_up, v_ffn2_w_down):
    xs = x[0]
    target = loss_target[0]
    T, D = xs.shape
    Dh = D // 2
    H = Dh // HEAD_DIM
    HP = H // 2
    KVW = (H // GQA_GROUP) * HEAD_DIM
    KVB = KVW // LANES
    MAIN = 4 * Dh + 2 * KVW
    F_OFF = 3 * Dh
    tm = min(ROW_TILE_CAP, T)
    tq = min(512, T)
    tk = min(512, T)
    nk = T // tk
    cx, cy, cc = _place()
    c_idx = jnp.reshape(cc, (1,)).astype(jnp.int32)
    chip_idx = jnp.reshape(2 * cx + cy, (1,)).astype(jnp.int32)

    tr = jnp.transpose
    big_w = [tr(ffn1_w_gate[0]), tr(ffn1_w_up[0]), ffn1_w_down[0], w_in[0], w_out[0], tr(ffn2_w_gate[0]), tr(ffn2_w_up[0]), ffn2_w_down[0]]
    big_m = [tr(m_ffn1_w_gate[0]), tr(m_ffn1_w_up[0]), m_ffn1_w_down[0], m_w_in[0], m_w_out[0], tr(m_ffn2_w_gate[0]), tr(m_ffn2_w_up[0]),
             m_ffn2_w_down[0]]
    big_v = [tr(v_ffn1_w_gate[0]), tr(v_ffn1_w_up[0]), v_ffn1_w_down[0], v_w_in[0], v_w_out[0], tr(v_ffn2_w_gate[0]), tr(v_ffn2_w_up[0]),
             v_ffn2_w_down[0]]
    transposed = {"ffn1_w_gate", "ffn1_w_up", "ffn2_w_gate", "ffn2_w_up"}
    names = ["ffn1_w_gate", "ffn1_w_up", "ffn1_w_down", "w_in", "w_out", "ffn2_w_gate", "ffn2_w_up", "ffn2_w_down"]
    sh = dict(zip(names, [w.astype(BF16) for w in big_w]))
    wg1, wu1 = _run_carry("weights_all_gather", _gather_carry([sh["ffn1_w_gate"], sh["ffn1_w_up"]]))

    lane = jnp.arange(LANES)
    inv_freq = ROPE_THETA ** (-(2.0 * (lane % (HEAD_DIM // 2))).astype(F32) / HEAD_DIM)
    ang = positions[0].astype(F32)[:, None] * inv_freq[None, :]
    cos_t = jnp.cos(ang)
    sin_t = jnp.where((lane & (HEAD_DIM // 2)) == 0, -1.0, 1.0)[None, :] * jnp.sin(ang)
    rope = (cos_t, sin_t)

    def pair_gain(g, blocks):
        return jnp.tile(jnp.concatenate([g[0], g[0]])[None, None, :], (blocks, 1, 1))

    n1 = _rmsnorm_fwd("ffn1_norm", xs, norm_ffn1_g, tm)
    (a1, b1, hm1), (wd1, win_g) = _ffn_up("ffn1_up", n1, wg1, wu1, tm, carry=_gather_carry([sh["ffn1_w_down"], sh["w_in"]]))
    h1, (wout_g, wg2) = _ffn_down("ffn1_down", hm1, wd1, xs, tm, carry=_gather_carry([sh["w_out"], sh["ffn2_w_gate"]]))
    n_in = win_g.shape[2]
    win_full = jnp.transpose(win_g, (1, 0, 2)).reshape(D, N_DEV * n_in)
    win_main = jnp.concatenate([win_full[:, :F_OFF], win_full[:, F_OFF + H:]], axis=1)
    win_f = jnp.pad(win_full[:, F_OFF:F_OFF + H], ((0, 0), (0, LANES - H)))

    u = _rmsnorm_fwd("mix_norm", h1, norm_mix_g, tm)
    proj, (wu2,) = _mm_nn("mix_proj", u, win_main, tm, MAIN // 9, carry=_gather_carry([sh["ffn2_w_up"]]))
    wout = wout_g.reshape(D, D)
    proj_f = _mm_nn("mix_proj_forget", u, win_f, tm, LANES)
    scale = HEAD_DIM ** -0.5
    fox_gains = jnp.concatenate([pair_gain(fox_q_norm_g, HP), pair_gain(fox_k_norm_g, HP)])
    qk_f = _headnorm_fwd_scaled("fox_qk_norm", proj, 0, 2 * HP, fox_gains, tm, scale, HP)
    v_f = proj[:, 2 * Dh:3 * Dh].astype(BF16)
    c_t, sg_t = _forget_fwd("forget_gates", proj_f[:, :H].T, b_forget.reshape(H, 1))
    crow = c_t.reshape(H, nk, 1, tk)
    (o_fox, lse_fa, lse_fb), (wd2,) = _fox_fwd("fox_attention", qk_f, v_f, crow, tq, tk, carry=_gather_carry([sh["ffn2_w_down"]]))

    swa_q_gains = pair_gain(swa_q_norm_g, HP)
    swa_k_gains = pair_gain(swa_k_norm_g, KVB)
    q_s = _headnorm_fwd("swa_q_norm", proj, 3 * HP, HP, swa_q_gains, tm, scale, rope=rope)
    k_d = _headnorm_fwd("swa_k_norm", proj, 4 * HP, KVB, swa_k_gains, tm, 1.0, rope=rope, dup=True)
    v_s = proj[:, 4 * Dh + KVW:].astype(BF16).reshape(T, H // GQA_GROUP, 1, HEAD_DIM)
    v_d = jnp.broadcast_to(v_s, (T, H // GQA_GROUP, 2, HEAD_DIM)).reshape(T, 2 * KVW)
    sinks3 = swa_sinks.reshape(H, 1, 1)
    o_swa, lse_sa, lse_sb = _swa_fwd("swa_attention", q_s, k_d, v_d, sinks3)

    on = _outnorm_fwd("out_norm", o_fox, o_swa, out_norm_fox_g, out_norm_swa_g, tm)
    h2 = _mm_nn("mix_out", on, wout, tm, min(512, D), resid=h1)

    n2 = _rmsnorm_fwd("ffn2_norm", h2, norm_ffn2_g, tm)
    a2, b2, hm2 = _ffn_up("ffn2_up", n2, wg2, wu2, tm)
    y = _ffn_down("ffn2_down", hm2, wd2, h2, tm)
    dy, dyh, sq = _loss_grad("loss_grad", y, target, min(256, T))
    loss = lax.psum(0.5 * sq[0, 0] / D, ("x", "y", "c"))

    J, Fs, _ = wg2.shape
    aspec = pl.BlockSpec((None, tm, Fs), lambda i, j: (j, i, 0))
    wspec = pl.BlockSpec((None, Fs, D), lambda i, j: (j, 0, 0))

    def pair_sums(keys, grads, received):
        return [_pair_add("sum_" + nm, g, r, c_idx) for nm, g, r in zip(keys, grads, received)]

    da2, db2 = _ffn_bwd_mid("ffn2_bwd_mid", dyh, wd2, a2, b2, tm)
    dwd2 = _wgrad_down("ffn2_wgrad_down", hm2, dyh, min(1024, D))
    dwg2, dwu2 = _wgrad_up("ffn2_wgrad_up", n2, da2, db2, min(1024, D))
    dn2, sib2 = _reduce_mm("ffn2_bwd_in", [(da2, aspec, wg2, wspec), (db2, aspec, wu2, wspec)], [], NN, T, D, tm, J,
                           carry=_sibling_carry([dwg2, dwu2, dwd2]))
    dh2, dg_ffn2, dh2b = _rmsnorm_bwd("ffn2_norm_bwd", dn2, h2, norm_ffn2_g, dy, min(256, T), 1.0)
    sum_wg2, sum_wu2, sum_wd2 = pair_sums(names[5:8], [dwg2, dwu2, dwd2], sib2)

    dwout = _wgrad_2d("mix_out_wgrad", on, dh2b, min(512, D), min(1024, D))
    do_fox, dg_of = _outnorm_bwd("out_norm_bwd_fox", dh2b, wout, 0, o_fox, out_norm_fox_g, tm)
    do_swa, dg_os = _outnorm_bwd("out_norm_bwd_swa", dh2b, wout, 1, o_swa, out_norm_swa_g, tm)

    (dq_f, dk_f, dv_f, dc_a, dc_b, dr_a, dr_b), (rc_wg2, rc_wu2) = _fox_bwd(
        "fox_attention_bwd", qk_f, v_f, o_fox, do_fox, crow, lse_fa, lse_fb, tq, tk, carry=_chips_carry([sum_wg2, sum_wu2]))
    dqk_f = jnp.concatenate([dq_f, dk_f], axis=1)
    dqk_raw, dg_fox = _headnorm_bwd("fox_qk_norm_bwd", dqk_f, proj, 0, 2 * HP, fox_gains, HP, tm, 1.0)
    dct = jnp.stack([dc_a.reshape(HP, T), dc_b.reshape(HP, T)], axis=1).reshape(H, T)
    drt = jnp.stack([dr_a.reshape(HP, T), dr_b.reshape(HP, T)], axis=1).reshape(H, T)
    dz_t, db_f = _forget_bwd("forget_gates_bwd", dct, drt, sg_t)

    (dq_s, dk_p, dv_p, dsink_a, dsink_b), (rc_wd2,) = _swa_bwd(
        "swa_attention_bwd", q_s, k_d, v_d, sinks3, o_swa, do_swa, lse_sa, lse_sb, carry=_chips_carry([sum_wd2]))
    dqs_raw, dg_sq = _headnorm_bwd("swa_q_norm_bwd", dq_s, proj, 3 * HP, HP, swa_q_gains, HP, tm, 1.0, rope=rope)
    dks_raw, dg_sk = _headnorm_bwd("swa_k_norm_bwd", dk_p, proj, 4 * HP, KVB, swa_k_gains, KVB, tm, 1.0, rope=rope, fold=True)
    dvs_raw, _ = _headnorm_bwd("swa_v_fold", dv_p, None, 0, KVB, None, KVB, tm, 1.0, fold=True, norm=False)

    dproj = jnp.concatenate([dqk_raw, dv_f.astype(BF16), dqs_raw, dks_raw, dvs_raw], axis=1)
    dproj_f = jnp.pad(dz_t.T, ((0, 0), (0, LANES - H))).astype(BF16)
    dwin_main = _wgrad_2d("mix_proj_wgrad", u, dproj, min(1024, D), MAIN // 9)
    dwin_f = _wgrad_2d("mix_proj_forget_wgrad", u, dproj_f, min(1024, D), LANES)
    dwin_full = jnp.concatenate([dwin_main[:, :F_OFF], dwin_f[:, :H], dwin_main[:, F_OFF:]], axis=1)
    dwin_g = jnp.transpose(dwin_full.reshape(D, N_DEV, n_in), (1, 0, 2))
    dwout_g = dwout.reshape(N_DEV, D // N_DEV, D)
    tkb = MAIN // 9
    du, sib_mix = _reduce_mm(
        "mix_bwd_in",
        [(dproj, pl.BlockSpec((tm, tkb), lambda i, r: (i, r)), win_main, pl.BlockSpec((D, tkb), lambda i, r: (0, r)))],
        [(dproj_f, pl.BlockSpec((tm, LANES), lambda i, r: (i, 0)), win_f, pl.BlockSpec((D, LANES), lambda i, r: (0, 0)))],
        NT, T, D, tm, 9, carry=_sibling_carry([dwin_g, dwout_g]))
    dh1, dg_mix, dh1h = _rmsnorm_bwd("mix_norm_bwd", du, h1, norm_mix_g, dh2, min(256, T), 0.5)
    sum_win, sum_wout = pair_sums(names[3:5], [dwin_g, dwout_g], sib_mix)

    (da1, db1), (rc_win,) = _ffn_bwd_mid("ffn1_bwd_mid", dh1h, wd1, a1, b1, tm, carry=_chips_carry([sum_win]))
    (dwg1, dwu1), (rc_wout,) = _wgrad_up("ffn1_wgrad_up", n1, da1, db1, min(1024, D), carry=_chips_carry([sum_wout]))
    dwd1, sib_gu = _wgrad_down("ffn1_wgrad_down", hm1, dh1h, min(1024, D), carry=_sibling_carry([dwg1, dwu1]))
    sum_wg1, sum_wu1 = pair_sums(names[0:2], [dwg1, dwu1], sib_gu)
    dn1, (rc_wg1, rc_wu1, sib_d) = _reduce_mm(
        "ffn1_bwd_in", [(da1, aspec, wg1, wspec), (db1, aspec, wu1, wspec)], [], NN, T, D, tm, J,
        carry=_join(_chips_carry([sum_wg1, sum_wu1]), _sibling_carry([dwd1])))
    (sum_wd1,) = pair_sums(names[2:3], [dwd1], [sib_d])
    (dx, dg_ffn1), (rc_wd1,) = _rmsnorm_bwd("ffn1_norm_bwd", dn1, xs, norm_ffn1_g, dh1, min(256, T), None,
                                            carry=_chips_carry([sum_wd1]))

    chip_sums = [sum_wg1, sum_wu1, sum_wd1, sum_win, sum_wout, sum_wg2, sum_wu2, sum_wd2]
    from_chips = [rc_wg1, rc_wu1, rc_wd1, rc_win, rc_wout, rc_wg2, rc_wu2, rc_wd2]
    big_out = [_adam_shard("adam_" + nm, s, r, w, m, v, chip_idx)
               for nm, s, r, w, m, v in zip(names, chip_sums, from_chips, big_w, big_m, big_v)]

    dsinks = jnp.stack([dsink_a.reshape(HP), dsink_b.reshape(HP)], axis=1).reshape(H)
    small_g = [dg_ffn1, dg_mix, dg_ffn2, dg_of, dg_os, db_f, dg_fox[0, 0, :HEAD_DIM], dg_fox[1, 0, :HEAD_DIM],
               dg_sq[0, 0, :HEAD_DIM], dg_sk[0, 0, :HEAD_DIM], dsinks]
    small_w = [norm_ffn1_g, norm_mix_g, norm_ffn2_g, out_norm_fox_g, out_norm_swa_g, b_forget, fox_q_norm_g, fox_k_norm_g,
               swa_q_norm_g, swa_k_norm_g, swa_sinks]
    small_m = [m_norm_ffn1_g, m_norm_mix_g, m_norm_ffn2_g, m_out_norm_fox_g, m_out_norm_swa_g, m_b_forget, m_fox_q_norm_g,
               m_fox_k_norm_g, m_swa_q_norm_g, m_swa_k_norm_g, m_swa_sinks]
    small_v = [v_norm_ffn1_g, v_norm_mix_g, v_norm_ffn2_g, v_out_norm_fox_g, v_out_norm_swa_g, v_b_forget, v_fox_q_norm_g,
               v_fox_k_norm_g, v_swa_q_norm_g, v_swa_k_norm_g, v_swa_sinks]
    gathered = _gather_small(_pack_small(small_g, D))
    small_out = _adam_small("adam_small", gathered, _pack_small(small_w, D), _pack_small(small_m, D), _pack_small(small_v, D))
    small_out = [_unpack_small(p, D, H) for p in small_out]

    order = ["norm_ffn1_g", "ffn1_w_gate", "ffn1_w_up", "ffn1_w_down", "norm_mix_g", "w_in", "b_forget", "fox_q_norm_g", "fox_k_norm_g",
             "swa_q_norm_g", "swa_k_norm_g", "swa_sinks", "out_norm_fox_g", "out_norm_swa_g", "w_out", "norm_ffn2_g",
             "ffn2_w_gate", "ffn2_w_up", "ffn2_w_down"]
    small_names = ["norm_ffn1_g", "norm_mix_g", "norm_ffn2_g", "out_norm_fox_g", "out_norm_swa_g", "b_forget", "fox_q_norm_g",
                   "fox_k_norm_g", "swa_q_norm_g", "swa_k_norm_g", "swa_sinks"]
    result = [loss, dx[None]]
    for kind in range(4):
        for nm in order:
            if nm in names:
                leaf = big_out[names.index(nm)][kind]
                result.append((tr(leaf) if nm in transposed else leaf)[None])
            else:
                result.append(small_out[kind][small_names.index(nm)])
    return tuple(result)


def _headnorm_fwd_scaled(name, proj, col_off, ncb, gains, tm, scale, n_scaled):
    T = proj.shape[0]

    def body(x_ref, g_ref, o_ref):
        xv = x_ref[...]
        lo = _lane_lo(xv.shape)
        y = xv * _head_rstd(xv, lo) * g_ref[...]
        y = y * jnp.where(pl.program_id(0) < n_scaled, scale, 1.0)
        o_ref[...] = y.astype(BF16)

    return pl.pallas_call(
        body, out_shape=jax.ShapeDtypeStruct((T, ncb * LANES), BF16), grid=(ncb, T // tm),
        in_specs=[pl.BlockSpec((tm, LANES), lambda c, i: (i, col_off + c)), pl.BlockSpec((None, 1, LANES), lambda c, i: (c, 0, 0))],
        out_specs=pl.BlockSpec((tm, LANES), lambda c, i: (i, c)), name=name, compiler_params=_params(2))(proj, gains)
```

```python
import functools

import jax
import jax.numpy as jnp
from jax import lax
from jax.experimental import pallas as pl
from jax.experimental.pallas import tpu as pltpu

F32 = jnp.float32
BF16 = jnp.bfloat16

HEAD_DIM = 64
LANES = 128
WINDOW = 128
GQA_GROUP = 4
EPS = 1e-6
ROPE_THETA = 10000.0
ADAM_LR = 0.001
ADAM_B1 = 0.9
ADAM_B2 = 0.999
ADAM_EPS = 1e-08
ADAM_WD = 0.01
ADAM_STEP = 10
N_DEV = 8
NEG = -1e30
VMEM_LIMIT_V7X = 48 * 1024 * 1024
ROW_TILE_CAP = 512
MESH = pl.DeviceIdType.MESH

NN = (((1,), (0,)), ((), ()))
NT = (((1,), (1,)), ((), ()))
TN = (((0,), (0,)), ((), ()))


def _dot(a, b, dims):
    return lax.dot_general(a, b, dims, preferred_element_type=F32)


def _params(n_axes):
    return pltpu.CompilerParams(dimension_semantics=("arbitrary",) * n_axes, vmem_limit_bytes=VMEM_LIMIT_V7X)


def _row_tile(rows, cap=ROW_TILE_CAP):
    best = None
    for t in range(16, min(rows, cap) + 1, 16):
        if rows % t == 0:
            best = t
    return best or rows


def _lane_lo(shape):
    return lax.broadcasted_iota(jnp.int32, shape, len(shape) - 1) < HEAD_DIM


def _keep(sel, x):
    return jnp.where(sel, x.astype(F32), 0.0).astype(BF16)


_HBM = pl.BlockSpec(memory_space=pltpu.HBM)


class _Carry:
    def __init__(self, inputs, out_shapes, scratch, start, finish):
        self.inputs, self.out_shapes, self.scratch, self.start, self.finish = list(inputs), list(out_shapes), list(scratch), start, finish


def _join(*carries):
    def hook(which):
        def run(ins, outs, scr):
            i = o = s = 0
            for c in carries:
                ni, no, ns = len(c.inputs), len(c.out_shapes), len(c.scratch)
                getattr(c, which)(ins[i:i + ni], outs[o:o + no], scr[s:s + ns])
                i, o, s = i + ni, o + no, s + ns
        return run

    return _Carry([a for c in carries for a in c.inputs], [a for c in carries for a in c.out_shapes],
                  [a for c in carries for a in c.scratch], hook("start"), hook("finish"))


def _call(body, *, name, grid, in_specs, out_specs, out_shape, args, scratch_shapes=(), carry=None):
    params = _params(len(grid))
    if carry is None:
        return pl.pallas_call(body, out_shape=out_shape, grid=grid, in_specs=list(in_specs), out_specs=out_specs,
                              scratch_shapes=list(scratch_shapes), name=name, compiler_params=params)(*args)
    single = not isinstance(out_shape, (tuple, list))
    shapes = (out_shape,) if single else tuple(out_shape)
    specs = (out_specs,) if single else tuple(out_specs)
    n_in, n_out, n_scr = len(args), len(shapes), len(scratch_shapes)
    c_in, c_out = len(carry.inputs), len(carry.out_shapes)

    def wrapped(*refs):
        ins, c_ins = refs[:n_in], refs[n_in:n_in + c_in]
        o0 = n_in + c_in
        outs, c_outs = refs[o0:o0 + n_out], refs[o0 + n_out:o0 + n_out + c_out]
        s0 = o0 + n_out + c_out
        scr, c_scr = refs[s0:s0 + n_scr], refs[s0 + n_scr:]
        first = pl.program_id(0) == 0
        last = pl.program_id(0) == grid[0] - 1
        for ax in range(1, len(grid)):
            first = jnp.logical_and(first, pl.program_id(ax) == 0)
            last = jnp.logical_and(last, pl.program_id(ax) == grid[ax] - 1)

        @pl.when(first)
        def _():
            carry.start(c_ins, c_outs, c_scr)

        body(*ins, *outs, *scr)

        @pl.when(last)
        def _():
            carry.finish(c_ins, c_outs, c_scr)

    res = pl.pallas_call(
        wrapped, out_shape=shapes + tuple(carry.out_shapes), grid=grid, in_specs=list(in_specs) + [_HBM] * c_in,
        out_specs=specs + (_HBM,) * c_out, scratch_shapes=list(scratch_shapes) + carry.scratch, name=name,
        compiler_params=params)(*args, *carry.inputs)
    main = res[:n_out]
    return (main[0] if single else tuple(main)), tuple(res[n_out:])


def _rms_bwd(dn, x, g):
    r = lax.rsqrt(jnp.mean(x * x, axis=-1, keepdims=True) + EPS)
    xh = x * r
    dxh = dn * g
    dx = r * (dxh - xh * jnp.mean(dxh * xh, axis=-1, keepdims=True))
    return dx, jnp.sum(dn * xh, axis=0, keepdims=True)


def _rmsnorm_fwd(name, x, g, tm):
    T, D = x.shape

    def body(x_ref, g_ref, o_ref):
        xf = x_ref[...]
        r = lax.rsqrt(jnp.mean(xf * xf, axis=-1, keepdims=True) + EPS)
        o_ref[...] = (xf * r * g_ref[...]).astype(BF16)

    return pl.pallas_call(
        body, out_shape=jax.ShapeDtypeStruct((T, D), BF16), grid=(T // tm,),
        in_specs=[pl.BlockSpec((tm, D), lambda i: (i, 0)), pl.BlockSpec((1, D), lambda i: (0, 0))],
        out_specs=pl.BlockSpec((tm, D), lambda i: (i, 0)), name=name, compiler_params=_params(1))(x, g)


def _outnorm_fwd(name, o_fox, o_swa, g_fox, g_swa, tm):
    T, Dh = o_fox.shape

    def body(a_ref, b_ref, ga_ref, gb_ref, o_ref):
        for ref, g_ref, lo in ((a_ref, ga_ref, 0), (b_ref, gb_ref, Dh)):
            xf = ref[...]
            r = lax.rsqrt(jnp.mean(xf * xf, axis=-1, keepdims=True) + EPS)
            o_ref[:, lo:lo + Dh] = (xf * r * g_ref[...]).astype(BF16)

    row = pl.BlockSpec((tm, Dh), lambda i: (i, 0))
    gain = pl.BlockSpec((1, Dh), lambda i: (0, 0))
    return pl.pallas_call(
        body, out_shape=jax.ShapeDtypeStruct((T, 2 * Dh), BF16), grid=(T // tm,),
        in_specs=[row, row, gain, gain], out_specs=pl.BlockSpec((tm, 2 * Dh), lambda i: (i, 0)),
        name=name, compiler_params=_params(1))(o_fox, o_swa, g_fox, g_swa)


def _outnorm_bwd(name, dhb, wout, half, o, g, tm):
    T, D = dhb.shape
    Dh = o.shape[1]

    def body(a_ref, w_ref, o_ref, g_ref, do_ref, dg_ref):
        don = _dot(a_ref[...], w_ref[...], NT)
        dx, dg = _rms_bwd(don, o_ref[...], g_ref[...])
        do_ref[...] = dx.astype(BF16)

        @pl.when(pl.program_id(0) == 0)
        def _():
            dg_ref[...] = dg

        @pl.when(pl.program_id(0) > 0)
        def _():
            dg_ref[...] += dg

    return pl.pallas_call(
        body, out_shape=(jax.ShapeDtypeStruct((T, Dh), BF16), jax.ShapeDtypeStruct((1, Dh), F32)), grid=(T // tm,),
        in_specs=[pl.BlockSpec((tm, D), lambda i: (i, 0)), pl.BlockSpec((Dh, D), lambda i: (half, 0)),
                  pl.BlockSpec((tm, Dh), lambda i: (i, 0)), pl.BlockSpec((1, Dh), lambda i: (0, 0))],
        out_specs=(pl.BlockSpec((tm, Dh), lambda i: (i, 0)), pl.BlockSpec((1, Dh), lambda i: (0, 0))),
        name=name, compiler_params=_params(1))(dhb, wout, o, g)


def _mm_nn(name, a, b, tm, tn, resid=None, carry=None):
    M, K = a.shape
    N = b.shape[1]

    def body(*refs):
        if resid is None:
            a_ref, b_ref, o_ref = refs
            o_ref[...] = _dot(a_ref[...], b_ref[...], NN)
        else:
            a_ref, b_ref, r_ref, o_ref = refs
            o_ref[...] = r_ref[...] + _dot(a_ref[...], b_ref[...], NN)

    ospec = pl.BlockSpec((tm, tn), lambda n, i: (i, n))
    in_specs = [pl.BlockSpec((tm, K), lambda n, i: (i, 0)), pl.BlockSpec((K, tn), lambda n, i: (0, n))]
    args = [a, b]
    if resid is not None:
        in_specs.append(ospec)
        args.append(resid)
    return _call(body, name=name, grid=(N // tn, M // tm), in_specs=in_specs, out_specs=ospec,
                 out_shape=jax.ShapeDtypeStruct((M, N), F32), args=args, carry=carry)


def _wgrad_2d(name, a, b, tmm, tn):
    T, M = a.shape
    N = b.shape[1]

    def body(a_ref, b_ref, o_ref):
        o_ref[...] = _dot(a_ref[...], b_ref[...], TN).astype(BF16)

    return pl.pallas_call(
        body, out_shape=jax.ShapeDtypeStruct((M, N), BF16), grid=(M // tmm, N // tn),
        in_specs=[pl.BlockSpec((T, tmm), lambda m, n: (0, m)), pl.BlockSpec((T, tn), lambda m, n: (0, n))],
        out_specs=pl.BlockSpec((tmm, tn), lambda m, n: (m, n)), name=name, compiler_params=_params(2))(a, b)


def _wgrad_down(name, hm, df, tn, carry=None):
    J, T, Fs = hm.shape
    D = df.shape[1]

    def body(a_ref, b_ref, o_ref):
        o_ref[...] = _dot(a_ref[...], b_ref[...], TN).astype(BF16)

    return _call(
        body, name=name, grid=(J, D // tn), out_shape=jax.ShapeDtypeStruct((J, Fs, D), BF16),
        in_specs=[pl.BlockSpec((None, T, Fs), lambda j, n: (j, 0, 0)), pl.BlockSpec((T, tn), lambda j, n: (0, n))],
        out_specs=pl.BlockSpec((None, Fs, tn), lambda j, n: (j, 0, n)), args=[hm, df], carry=carry)


def _wgrad_up(name, n, da, db, tn, carry=None):
    T, D = n.shape
    J, _, Fs = da.shape

    def body(n_ref, da_ref, db_ref, og_ref, ou_ref):
        nv = n_ref[...]
        og_ref[...] = _dot(da_ref[...], nv, TN).astype(BF16)
        ou_ref[...] = _dot(db_ref[...], nv, TN).astype(BF16)

    act = pl.BlockSpec((None, T, Fs), lambda j, m: (j, 0, 0))
    out = pl.BlockSpec((None, Fs, tn), lambda j, m: (j, 0, m))
    shape = jax.ShapeDtypeStruct((J, Fs, D), BF16)
    return _call(
        body, name=name, grid=(J, D // tn), out_shape=(shape, shape),
        in_specs=[pl.BlockSpec((T, tn), lambda j, m: (0, m)), act, act], out_specs=(out, out),
        args=[n, da, db], carry=carry)


def _reduce_mm(name, pairs, once, dims, T, D, tm, steps, carry=None):
    n_pairs = len(pairs)
    n_once = len(once)

    def body(*refs):
        pr = refs[:2 * n_pairs]
        on = refs[2 * n_pairs:2 * (n_pairs + n_once)]
        o_ref, acc = refs[-2:]
        r = pl.program_id(1)
        part = _dot(pr[0][...], pr[1][...], dims)
        for p in range(1, n_pairs):
            part = part + _dot(pr[2 * p][...], pr[2 * p + 1][...], dims)

        @pl.when(r == 0)
        def _():
            acc[...] = part

        @pl.when(r > 0)
        def _():
            acc[...] += part

        @pl.when(r == steps - 1)
        def _():
            dn = acc[...]
            for p in range(n_once):
                dn = dn + _dot(on[2 * p][...], on[2 * p + 1][...], dims)
            o_ref[...] = dn

    in_specs, args = [], []
    for a, a_spec, w, w_spec in list(pairs) + list(once):
        in_specs += [a_spec, w_spec]
        args += [a, w]
    row = pl.BlockSpec((tm, D), lambda i, r: (i, 0))
    return _call(body, name=name, grid=(T // tm, steps), in_specs=in_specs, out_specs=row, out_shape=jax.ShapeDtypeStruct((T, D), F32),
                 args=args, scratch_shapes=[pltpu.VMEM((tm, D), F32)], carry=carry)


def _rmsnorm_bwd(name, dn, x, g, dh, tm, bf16_scale, carry=None):
    T, D = x.shape
    emit_bf16 = bf16_scale is not None

    def body(dn_ref, x_ref, g_ref, dh_ref, *outs):
        dxn, dg = _rms_bwd(dn_ref[...], x_ref[...], g_ref[...])
        dx = dh_ref[...] + dxn
        outs[0][...] = dx
        if emit_bf16:
            outs[2][...] = (bf16_scale * dx).astype(BF16)

        @pl.when(pl.program_id(0) == 0)
        def _():
            outs[1][...] = dg

        @pl.when(pl.program_id(0) > 0)
        def _():
            outs[1][...] += dg

    row = pl.BlockSpec((tm, D), lambda i: (i, 0))
    gain = pl.BlockSpec((1, D), lambda i: (0, 0))
    out_shape = [jax.ShapeDtypeStruct((T, D), F32), jax.ShapeDtypeStruct((1, D), F32)]
    out_specs = [row, gain]
    if emit_bf16:
        out_shape.append(jax.ShapeDtypeStruct((T, D), BF16))
        out_specs.append(row)
    return _call(body, name=name, grid=(T // tm,), in_specs=[row, row, gain, row], out_specs=tuple(out_specs),
                 out_shape=tuple(out_shape), args=[dn, x, g, dh], carry=carry)


def _loss_grad(name, y, target, tm):
    T, D = y.shape

    def body(y_ref, t_ref, dy_ref, dyh_ref, sq_ref):
        diff = y_ref[...] - t_ref[...]
        sq = jnp.sum(jnp.sum(diff * diff, axis=1, keepdims=True), axis=0, keepdims=True)
        dy = diff * (1.0 / D)
        dy_ref[...] = dy
        dyh_ref[...] = (0.5 * dy).astype(BF16)

        @pl.when(pl.program_id(0) == 0)
        def _():
            sq_ref[...] = sq

        @pl.when(pl.program_id(0) > 0)
        def _():
            sq_ref[...] += sq

    row = pl.BlockSpec((tm, D), lambda i: (i, 0))
    return pl.pallas_call(
        body, out_shape=(jax.ShapeDtypeStruct((T, D), F32), jax.ShapeDtypeStruct((T, D), BF16), jax.ShapeDtypeStruct((1, 1), F32)),
        grid=(T // tm,), in_specs=[row, row], out_specs=(row, row, pl.BlockSpec((1, 1), lambda i: (0, 0))),
        name=name, compiler_params=_params(1))(y, target)


def _ffn_up(name, n, wg, wu, tm, carry=None):
    T, D = n.shape
    J, Fs, _ = wg.shape

    def body(n_ref, wg_ref, wu_ref, a_ref, b_ref, h_ref):
        xv = n_ref[...]
        a = _dot(xv, wg_ref[...], NT)
        b = _dot(xv, wu_ref[...], NT)
        a_ref[...] = a.astype(BF16)
        b_ref[...] = b.astype(BF16)
        h_ref[...] = (a * jax.nn.sigmoid(a) * b).astype(BF16)

    act = jax.ShapeDtypeStruct((J, T, Fs), BF16)
    wspec = pl.BlockSpec((None, Fs, D), lambda j, i: (j, 0, 0))
    aspec = pl.BlockSpec((None, tm, Fs), lambda j, i: (j, i, 0))
    return _call(
        body, name=name, grid=(J, T // tm), out_shape=(act, act, act),
        in_specs=[pl.BlockSpec((tm, D), lambda j, i: (i, 0)), wspec, wspec], out_specs=(aspec, aspec, aspec),
        args=[n, wg, wu], carry=carry)


def _ffn_gate(name, n, wg, tm, carry=None):
    T, D = n.shape
    J, Fs, _ = wg.shape

    def body(n_ref, wg_ref, a_ref):
        a_ref[...] = _dot(n_ref[...], wg_ref[...], NT).astype(BF16)

    aspec = pl.BlockSpec((None, tm, Fs), lambda j, i: (j, i, 0))
    return _call(
        body, name=name, grid=(J, T // tm), out_shape=jax.ShapeDtypeStruct((J, T, Fs), BF16),
        in_specs=[pl.BlockSpec((tm, D), lambda j, i: (i, 0)), pl.BlockSpec((None, Fs, D), lambda j, i: (j, 0, 0))],
        out_specs=aspec, args=[n, wg], carry=carry)


def _ffn_up_only(name, n, wu, a, tm, carry=None):
    T, D = n.shape
    J, Fs, _ = wu.shape

    def body(n_ref, wu_ref, a_ref, b_ref, h_ref):
        b = _dot(n_ref[...], wu_ref[...], NT)
        a = a_ref[...].astype(F32)
        b_ref[...] = b.astype(BF16)
        h_ref[...] = (a * jax.nn.sigmoid(a) * b).astype(BF16)

    act = jax.ShapeDtypeStruct((J, T, Fs), BF16)
    aspec = pl.BlockSpec((None, tm, Fs), lambda j, i: (j, i, 0))
    return _call(
        body, name=name, grid=(J, T // tm), out_shape=(act, act),
        in_specs=[pl.BlockSpec((tm, D), lambda j, i: (i, 0)), pl.BlockSpec((None, Fs, D), lambda j, i: (j, 0, 0)), aspec],
        out_specs=(aspec, aspec), args=[n, wu, a], carry=carry)


def _ffn_down(name, hm, wd, resid, tm, carry=None):
    J, T, Fs = hm.shape
    D = wd.shape[2]

    def body(h_ref, w_ref, r_ref, o_ref, acc):
        j = pl.program_id(1)
        part = _dot(h_ref[...], w_ref[...], NN)

        @pl.when(j == 0)
        def _():
            acc[...] = part

        @pl.when(j > 0)
        def _():
            acc[...] += part

        @pl.when(j == J - 1)
        def _():
            o_ref[...] = r_ref[...] + 0.5 * acc[...]

    row = pl.BlockSpec((tm, D), lambda i, j: (i, 0))
    return _call(
        body, name=name, grid=(T // tm, J), out_shape=jax.ShapeDtypeStruct((T, D), F32),
        in_specs=[pl.BlockSpec((None, tm, Fs), lambda i, j: (j, i, 0)), pl.BlockSpec((None, Fs, D), lambda i, j: (j, 0, 0)), row],
        out_specs=row, scratch_shapes=[pltpu.VMEM((tm, D), F32)], args=[hm, wd, resid], carry=carry)


def _ffn_bwd_mid(name, dfh, wd, a, b, tm, carry=None):
    T, D = dfh.shape
    J, Fs, _ = wd.shape

    def body(df_ref, w_ref, a_ref, b_ref, da_ref, db_ref):
        dhm = _dot(df_ref[...], w_ref[...], NT)
        av = a_ref[...].astype(F32)
        bv = b_ref[...].astype(F32)
        sg = jax.nn.sigmoid(av)
        da_ref[...] = (dhm * bv * (sg * (1.0 + av * (1.0 - sg)))).astype(BF16)
        db_ref[...] = (dhm * (av * sg)).astype(BF16)

    act = jax.ShapeDtypeStruct((J, T, Fs), BF16)
    aspec = pl.BlockSpec((None, tm, Fs), lambda j, i: (j, i, 0))
    return _call(
        body, name=name, grid=(J, T // tm), out_shape=(act, act),
        in_specs=[pl.BlockSpec((tm, D), lambda j, i: (i, 0)), pl.BlockSpec((None, Fs, D), lambda j, i: (j, 0, 0)), aspec, aspec],
        out_specs=(aspec, aspec), args=[dfh, wd, a, b], carry=carry)


def _rot_half(y, lane):
    first = (lane & (HEAD_DIM // 2)) == 0
    return jnp.where(first, pltpu.roll(y, LANES - HEAD_DIM // 2, 1), pltpu.roll(y, HEAD_DIM // 2, 1))


def _head_rstd(x, lo):
    sq = x * x
    ss_a = jnp.sum(jnp.where(lo, sq, 0.0), axis=-1, keepdims=True)
    ss_b = jnp.sum(jnp.where(lo, 0.0, sq), axis=-1, keepdims=True)
    return lax.rsqrt(jnp.where(lo, ss_a, ss_b) * (1.0 / HEAD_DIM) + EPS)


def _headnorm_fwd(name, proj, col_off, ncb, gains, tm, scale, rope=None, dup=False):
    T = proj.shape[0]
    with_rope = rope is not None
    width = 2 * LANES if dup else LANES

    def body(*refs):
        if with_rope:
            x_ref, g_ref, cos_ref, sin_ref, o_ref = refs
        else:
            x_ref, g_ref, o_ref = refs
        xv = x_ref[...]
        lane = lax.broadcasted_iota(jnp.int32, xv.shape, 1)
        lo = lane < HEAD_DIM
        y = xv * _head_rstd(xv, lo) * g_ref[...]
        if with_rope:
            y = y * cos_ref[...] + _rot_half(y, lane) * sin_ref[...]
        y = y * scale
        if dup:
            sw = pltpu.roll(y, HEAD_DIM, 1)
            o_ref[:, :LANES] = jnp.where(lo, y, sw).astype(BF16)
            o_ref[:, LANES:] = jnp.where(lo, sw, y).astype(BF16)
        else:
            o_ref[...] = y.astype(BF16)

    in_specs = [pl.BlockSpec((tm, LANES), lambda c, i: (i, col_off + c)), pl.BlockSpec((None, 1, LANES), lambda c, i: (c, 0, 0))]
    args = [proj, gains]
    if with_rope:
        tab = pl.BlockSpec((tm, LANES), lambda c, i: (i, 0))
        in_specs += [tab, tab]
        args += list(rope)
    return pl.pallas_call(
        body, out_shape=jax.ShapeDtypeStruct((T, ncb * width), BF16), grid=(ncb, T // tm),
        in_specs=in_specs, out_specs=pl.BlockSpec((tm, width), lambda c, i: (i, c)),
        name=name, compiler_params=_params(2))(*args)


def _headnorm_bwd(name, dy, proj, col_off, ncb, gains, group, tm, scale, rope=None, fold=False, norm=True):
    T = dy.shape[0]
    with_rope = rope is not None
    n_groups = ncb // group
    dy_width = 4 * LANES if fold else LANES

    def body(*refs):
        refs = list(refs)
        dy_ref = refs.pop(0)
        x_ref = refs.pop(0) if norm else None
        g_ref = refs.pop(0) if norm else None
        cos_ref = refs.pop(0) if with_rope else None
        sin_ref = refs.pop(0) if with_rope else None
        dx_ref = refs.pop(0)
        dg_ref = refs.pop(0) if norm else None
        c = pl.program_id(0)
        i = pl.program_id(1)
        d = dy_ref[...]
        lane = lax.broadcasted_iota(jnp.int32, (d.shape[0], LANES), 1)
        lo = lane < HEAD_DIM
        if fold:
            t0 = d[:, 0:LANES] + d[:, LANES:2 * LANES]
            t1 = d[:, 2 * LANES:3 * LANES] + d[:, 3 * LANES:4 * LANES]
            d = jnp.where(lo, t0 + pltpu.roll(t0, HEAD_DIM, 1), t1 + pltpu.roll(t1, HEAD_DIM, 1))
        d = d * scale
        if with_rope:
            d = d * cos_ref[...] + _rot_half(d * sin_ref[...], lane)
        if not norm:
            dx_ref[...] = d.astype(BF16)
            return
        xv = x_ref[...]
        gv = g_ref[...]
        r = _head_rstd(xv, lo)
        xh = xv * r
        dxh = d * gv
        pr = dxh * xh
        m_a = jnp.sum(jnp.where(lo, pr, 0.0), axis=-1, keepdims=True)
        m_b = jnp.sum(jnp.where(lo, 0.0, pr), axis=-1, keepdims=True)
        mean = jnp.where(lo, m_a, m_b) * (1.0 / HEAD_DIM)
        dx_ref[...] = (r * (dxh - xh * mean)).astype(BF16)
        dgp = jnp.sum(d * xh, axis=0, keepdims=True)
        dgp = dgp + pltpu.roll(dgp, HEAD_DIM, 1)
        first = jnp.logical_and(c % group == 0, i == 0)

        @pl.when(first)
        def _():
            dg_ref[...] = dgp

        @pl.when(jnp.logical_not(first))
        def _():
            dg_ref[...] += dgp

    in_specs = [pl.BlockSpec((tm, dy_width), lambda c, i: (i, c))]
    args = [dy]
    if norm:
        in_specs += [pl.BlockSpec((tm, LANES), lambda c, i: (i, col_off + c)), pl.BlockSpec((None, 1, LANES), lambda c, i: (c, 0, 0))]
        args += [proj, gains]
    if with_rope:
        tab = pl.BlockSpec((tm, LANES), lambda c, i: (i, 0))
        in_specs += [tab, tab]
        args += list(rope)
    out_shape = [jax.ShapeDtypeStruct((T, ncb * LANES), BF16)]
    out_specs = [pl.BlockSpec((tm, LANES), lambda c, i: (i, c))]
    if norm:
        out_shape.append(jax.ShapeDtypeStruct((n_groups, 1, LANES), F32))
        out_specs.append(pl.BlockSpec((None, 1, LANES), lambda c, i: (c // group, 0, 0)))
    res = pl.pallas_call(
        body, out_shape=tuple(out_shape), grid=(ncb, T // tm), in_specs=in_specs, out_specs=tuple(out_specs),
        name=name, compiler_params=_params(2))(*args)
    return res if norm else (res[0], None)


def _dot_exact(x, tri):
    hi = x.astype(BF16)
    r1 = x - hi.astype(F32)
    mid = r1.astype(BF16)
    lo = (r1 - mid.astype(F32)).astype(BF16)
    return _dot(hi, tri, NN) + _dot(mid, tri, NN) + _dot(lo, tri, NN)


def _forget_fwd(name, zt, bias):
    H, T = zt.shape
    blk = min(256, T)

    def body(z_ref, b_ref, c_ref, s_ref):
        z = z_ref[...] + b_ref[...]
        s_ref[...] = jax.nn.sigmoid(-z)
        lf = jnp.minimum(z, 0.0) - jnp.log(1.0 + jnp.exp(-jnp.abs(z)))
        tri = (lax.broadcasted_iota(jnp.int32, (blk, blk), 0) <= lax.broadcasted_iota(jnp.int32, (blk, blk), 1)).astype(BF16)
        carry = jnp.zeros((H, 1), F32)
        for bi in range(T // blk):
            xb = lf[:, bi * blk:(bi + 1) * blk]
            c_ref[:, bi * blk:(bi + 1) * blk] = _dot_exact(xb, tri) + carry
            carry = carry + jnp.sum(xb, axis=-1, keepdims=True)

    shape = jax.ShapeDtypeStruct((H, T), F32)
    full = pl.BlockSpec((H, T), lambda i: (0, 0))
    return pl.pallas_call(
        body, out_shape=(shape, shape), grid=(1,), in_specs=[full, pl.BlockSpec((H, 1), lambda i: (0, 0))],
        out_specs=(full, full), name=name, compiler_params=_params(1))(zt, bias)


def _forget_bwd(name, dct, drt, sgt):
    H, T = dct.shape
    blk = min(256, T)

    def body(dc_ref, dr_ref, s_ref, dz_ref, db_ref):
        dc = dc_ref[...] + dr_ref[...]
        tri = (lax.broadcasted_iota(jnp.int32, (blk, blk), 0) >= lax.broadcasted_iota(jnp.int32, (blk, blk), 1)).astype(BF16)
        carry = jnp.zeros((H, 1), F32)
        db = jnp.zeros((H, 1), F32)
        for bi in reversed(range(T // blk)):
            xb = dc[:, bi * blk:(bi + 1) * blk]
            dz = (_dot_exact(xb, tri) + carry) * s_ref[:, bi * blk:(bi + 1) * blk]
            dz_ref[:, bi * blk:(bi + 1) * blk] = dz
            db = db + jnp.sum(dz, axis=-1, keepdims=True)
            carry = carry + jnp.sum(xb, axis=-1, keepdims=True)
        db_ref[...] = db

    full = pl.BlockSpec((H, T), lambda i: (0, 0))
    return pl.pallas_call(
        body, out_shape=(jax.ShapeDtypeStruct((H, T), F32), jax.ShapeDtypeStruct((H, 1), F32)), grid=(1,),
        in_specs=[full, full, full], out_specs=(full, pl.BlockSpec((H, 1), lambda i: (0, 0))),
        name=name, compiler_params=_params(1))(dct, drt, sgt)


STRIP = 256


def _fox_fwd(name, qk, v, crow, tq, tk, carry=None):
    T, Dh = v.shape
    HP = Dh // LANES
    nk = T // tk
    assert tk % tq == 0 and tq % STRIP == 0
    n_strips = tq // STRIP

    def body(q_ref, k_ref, v_ref, ra_ref, rb_ref, o_ref, la_ref, lb_ref, s_ref, p_ref, m_ref, l_ref, acc_ref):
        i = pl.program_id(1)
        q2 = q_ref[...]
        lo = _lane_lo((tq, LANES))
        qms = (_keep(lo, q2), _keep(jnp.logical_not(lo), q2))
        r_refs = (ra_ref, rb_ref)
        m_ref[...] = jnp.full(m_ref.shape, NEG, F32)
        l_ref[...] = jnp.zeros(l_ref.shape, F32)
        acc_ref[...] = jnp.zeros(acc_ref.shape, F32)
        rel = lax.broadcasted_iota(jnp.int32, (STRIP, tk), 0) - lax.broadcasted_iota(jnp.int32, (STRIP, tk), 1)

        def chunk(kc, masked):
            start = pl.multiple_of(kc * tk, tk)
            kb = k_ref[pl.ds(start, tk), :]
            vb = v_ref[pl.ds(start, tk), :]
            for h in range(2):
                s_ref[h] = _dot(qms[h], kb, NT)
            for h in range(2):
                cs = r_refs[h][kc]
                for st in range(n_strips):
                    rows = pl.ds(st * STRIP, STRIP)
                    s = s_ref[h, rows, :] - cs
                    if masked:
                        s = jnp.where(rel >= start - (i * tq + st * STRIP), s, NEG)
                    m_old = m_ref[h, rows, :]
                    mn = jnp.maximum(m_old, jnp.max(s, axis=-1, keepdims=True))
                    p = jnp.exp(s - mn)
                    alpha = jnp.exp(m_old - mn)
                    l_ref[h, rows, :] = alpha * l_ref[h, rows, :] + jnp.sum(p, axis=-1, keepdims=True)
                    m_ref[h, rows, :] = mn
                    p_ref[h, rows, :] = p.astype(BF16)
                    acc_ref[h, rows, :] = acc_ref[h, rows, :] * alpha
            for h in range(2):
                acc_ref[h] += _dot(p_ref[h], vb, NN)

        n_full = (i * tq) // tk

        def full_chunk(kc, _):
            chunk(kc, False)
            return 0

        lax.fori_loop(0, n_full, full_chunk, 0)
        chunk(n_full, True)
        o_ref[...] = jnp.where(lo, acc_ref[0] / l_ref[0], acc_ref[1] / l_ref[1])
        la_ref[...] = m_ref[0] + jnp.log(l_ref[0])
        lb_ref[...] = m_ref[1] + jnp.log(l_ref[1])

    row = lambda off: pl.BlockSpec((None, nk, 1, tk), lambda h, i: (2 * h + off, 0, 0, 0))
    lse = jax.ShapeDtypeStruct((HP, T, 1), F32)
    lspec = pl.BlockSpec((None, tq, 1), lambda h, i: (h, i, 0))
    scratch = [pltpu.VMEM((2, tq, tk), F32), pltpu.VMEM((2, tq, tk), BF16), pltpu.VMEM((2, tq, 1), F32),
               pltpu.VMEM((2, tq, 1), F32), pltpu.VMEM((2, tq, LANES), F32)]
    return _call(
        body, name=name, grid=(HP, T // tq), out_shape=(jax.ShapeDtypeStruct((T, Dh), F32), lse, lse),
        in_specs=[pl.BlockSpec((tq, LANES), lambda h, i: (i, h)), pl.BlockSpec((T, LANES), lambda h, i: (0, HP + h)),
                  pl.BlockSpec((T, LANES), lambda h, i: (0, h)), row(0), row(1)],
        out_specs=(pl.BlockSpec((tq, LANES), lambda h, i: (i, h)), lspec, lspec),
        args=[qk, qk, v, crow, crow], scratch_shapes=scratch, carry=carry)


def _fox_bwd(name, qk, v, o, do, crow, lse_a, lse_b, tq, tk, carry=None):
    T, Dh = v.shape
    HP = Dh // LANES
    nk = T // tk
    scale = HEAD_DIM ** -0.5
    assert tk % tq == 0 and tq % STRIP == 0
    n_strips = tq // STRIP

    def body(q_ref, k_ref, v_ref, o_ref, do_ref, ra_ref, rb_ref, la_ref, lb_ref,
             dq_ref, dk_ref, dv_ref, dca_ref, dcb_ref, dra_ref, drb_ref, s_ref, dp_ref, p_ref, ds_ref, dq_acc, dsum_ref):
        i = pl.program_id(1)

        @pl.when(i == 0)
        def _():
            dk_ref[...] = jnp.zeros_like(dk_ref)
            dv_ref[...] = jnp.zeros_like(dv_ref)
            dca_ref[...] = jnp.zeros_like(dca_ref)
            dcb_ref[...] = jnp.zeros_like(dcb_ref)

        q2 = q_ref[...]
        do2 = do_ref[...]
        lo = _lane_lo((tq, LANES))
        hi = jnp.logical_not(lo)
        qms = (_keep(lo, q2), _keep(hi, q2))
        doms = (_keep(lo, do2), _keep(hi, do2))
        prod = do2.astype(F32) * o_ref[...]
        dsum_ref[0] = jnp.sum(jnp.where(lo, prod, 0.0), axis=-1, keepdims=True)
        dsum_ref[1] = jnp.sum(jnp.where(lo, 0.0, prod), axis=-1, keepdims=True)
        r_refs, l_refs, dc_refs, dr_refs = (ra_ref, rb_ref), (la_ref, lb_ref), (dca_ref, dcb_ref), (dra_ref, drb_ref)
        dq_acc[...] = jnp.zeros(dq_acc.shape, F32)
        dra_ref[...] = jnp.zeros(dra_ref.shape, F32)
        drb_ref[...] = jnp.zeros(drb_ref.shape, F32)
        rel = lax.broadcasted_iota(jnp.int32, (STRIP, tk), 0) - lax.broadcasted_iota(jnp.int32, (STRIP, tk), 1)

        def chunk(kc, masked):
            start = pl.multiple_of(kc * tk, tk)
            kb = k_ref[pl.ds(start, tk), :]
            vb = v_ref[pl.ds(start, tk), :]
            for h in range(2):
                s_ref[h] = _dot(qms[h], kb, NT)
                dp_ref[h] = _dot(doms[h], vb, NT)
            for h in range(2):
                cs = r_refs[h][kc]
                col_sum = jnp.zeros((1, tk), F32)
                for st in range(n_strips):
                    rows = pl.ds(st * STRIP, STRIP)
                    s = s_ref[h, rows, :] - cs
                    if masked:
                        s = jnp.where(rel >= start - (i * tq + st * STRIP), s, NEG)
                    p = jnp.exp(s - l_refs[h][rows, :])
                    ds = p * (dp_ref[h, rows, :] - dsum_ref[h, rows, :])
                    p_ref[h, rows, :] = p.astype(BF16)
                    ds_ref[h, rows, :] = ds.astype(BF16)
                    col_sum = col_sum + jnp.sum(ds, axis=0, keepdims=True)
                    dr_refs[h][rows, :] += jnp.sum(ds, axis=-1, keepdims=True)
                dc_refs[h][kc] = dc_refs[h][kc] - col_sum
            dk = _dot(ds_ref[0], qms[0], TN) + _dot(ds_ref[1], qms[1], TN)
            dv = _dot(p_ref[0], doms[0], TN) + _dot(p_ref[1], doms[1], TN)
            dk_ref[pl.ds(start, tk), :] += dk
            dv_ref[pl.ds(start, tk), :] += dv
            for h in range(2):
                dq_acc[h] += _dot(ds_ref[h], kb, NN)

        n_full = (i * tq) // tk

        def full_chunk(kc, _):
            chunk(kc, False)
            return 0

        lax.fori_loop(0, n_full, full_chunk, 0)
        chunk(n_full, True)
        dq_ref[...] = jnp.where(lo, dq_acc[0], dq_acc[1]) * scale

    row = lambda off: pl.BlockSpec((None, nk, 1, tk), lambda h, i: (2 * h + off, 0, 0, 0))
    lspec = pl.BlockSpec((None, tq, 1), lambda h, i: (h, i, 0))
    qspec = pl.BlockSpec((tq, LANES), lambda h, i: (i, h))
    full = pl.BlockSpec((T, LANES), lambda h, i: (0, h))
    dcspec = pl.BlockSpec((None, nk, 1, tk), lambda h, i: (h, 0, 0, 0))
    grad = jax.ShapeDtypeStruct((T, Dh), F32)
    dc = jax.ShapeDtypeStruct((HP, nk, 1, tk), F32)
    dr = jax.ShapeDtypeStruct((HP, T, 1), F32)
    scratch = [pltpu.VMEM((2, tq, tk), F32), pltpu.VMEM((2, tq, tk), F32), pltpu.VMEM((2, tq, tk), BF16), pltpu.VMEM((2, tq, tk), BF16),
               pltpu.VMEM((2, tq, LANES), F32), pltpu.VMEM((2, tq, 1), F32)]
    return _call(
        body, name=name, grid=(HP, T // tq), out_shape=(grad, grad, grad, dc, dc, dr, dr),
        in_specs=[qspec, pl.BlockSpec((T, LANES), lambda h, i: (0, HP + h)), full, qspec, qspec, row(0), row(1), lspec, lspec],
        out_specs=(qspec, full, full, dcspec, dcspec, lspec, lspec),
        args=[qk, qk, v, o, do, crow, crow, lse_a, lse_b], scratch_shapes=scratch, carry=carry)


def _swa_block(n, q_ref, k_ref):
    qs = pl.multiple_of(n * WINDOW, WINDOW)
    ks = pl.multiple_of(jnp.maximum(n - 1, 0) * WINDOW, WINDOW)
    rel = (qs + lax.broadcasted_iota(jnp.int32, (WINDOW, 2 * WINDOW), 0)) - (ks + lax.broadcasted_iota(jnp.int32, (WINDOW, 2 * WINDOW), 1))
    valid = jnp.logical_and(rel >= 0, rel < WINDOW)
    return qs, ks, valid


def _swa_fwd(name, q, kd, vd, sinks, carry=None):
    T, Dh = q.shape
    HP = Dh // LANES

    def body(q_ref, k_ref, v_ref, sa_ref, sb_ref, o_ref, la_ref, lb_ref):
        lo = _lane_lo((WINDOW, LANES))

        def block(n, _):
            qs, ks, valid = _swa_block(n, q_ref, k_ref)
            q2 = q_ref[pl.ds(qs, WINDOW), :]
            kb = k_ref[pl.ds(ks, 2 * WINDOW), :]
            vb = v_ref[pl.ds(ks, 2 * WINDOW), :]
            res = []
            for sel, s_ref in ((lo, sa_ref), (jnp.logical_not(lo), sb_ref)):
                qm = _keep(sel, q2)
                sink = s_ref[...]
                s = jnp.where(valid, _dot(qm, kb, NT), NEG)
                m = jnp.maximum(jnp.max(s, axis=-1, keepdims=True), sink)
                p = jnp.exp(s - m)
                l = jnp.sum(p, axis=-1, keepdims=True) + jnp.exp(sink - m)
                res.append((_dot(p.astype(BF16), vb, NN) / l, m + jnp.log(l)))
            o_ref[pl.ds(qs, WINDOW), :] = jnp.where(lo, res[0][0], res[1][0])
            la_ref[pl.ds(qs, WINDOW), :] = res[0][1]
            lb_ref[pl.ds(qs, WINDOW), :] = res[1][1]
            return 0

        lax.fori_loop(0, T // WINDOW, block, 0, unroll=2)

    full = pl.BlockSpec((T, LANES), lambda h: (0, h))
    kv = pl.BlockSpec((T, LANES), lambda h: (0, h // 2))
    sink = lambda off: pl.BlockSpec((None, 1, 1), lambda h: (2 * h + off, 0, 0))
    lse = jax.ShapeDtypeStruct((HP, T, 1), F32)
    lspec = pl.BlockSpec((None, T, 1), lambda h: (h, 0, 0))
    return _call(
        body, name=name, grid=(HP,), out_shape=(jax.ShapeDtypeStruct((T, Dh), F32), lse, lse),
        in_specs=[full, kv, kv, sink(0), sink(1)], out_specs=(full, lspec, lspec),
        args=[q, kd, vd, sinks, sinks], carry=carry)


def _swa_bwd(name, q, kd, vd, sinks, o, do, lse_a, lse_b, carry=None):
    T, Dh = q.shape
    HP = Dh // LANES
    scale = HEAD_DIM ** -0.5

    def body(q_ref, k_ref, v_ref, sa_ref, sb_ref, o_ref, do_ref, la_ref, lb_ref, dq_ref, dk_ref, dv_ref, dsa_ref, dsb_ref):
        lo = _lane_lo((WINDOW, LANES))
        hi = jnp.logical_not(lo)
        dk_ref[...] = jnp.zeros_like(dk_ref)
        dv_ref[...] = jnp.zeros_like(dv_ref)

        def block(n, dsinks):
            qs, ks, valid = _swa_block(n, q_ref, k_ref)
            q2 = q_ref[pl.ds(qs, WINDOW), :]
            do2 = do_ref[pl.ds(qs, WINDOW), :]
            kb = k_ref[pl.ds(ks, 2 * WINDOW), :]
            vb = v_ref[pl.ds(ks, 2 * WINDOW), :]
            prod = do2.astype(F32) * o_ref[pl.ds(qs, WINDOW), :]
            dqs, new = [], []
            dk = jnp.zeros((2 * WINDOW, LANES), F32)
            dv = jnp.zeros((2 * WINDOW, LANES), F32)
            for sel, s_ref, l_ref, dsink in ((lo, sa_ref, la_ref, dsinks[0]), (hi, sb_ref, lb_ref, dsinks[1])):
                qm = _keep(sel, q2)
                dom = _keep(sel, do2)
                dsum = jnp.sum(jnp.where(sel, prod, 0.0), axis=-1, keepdims=True)
                lse = l_ref[pl.ds(qs, WINDOW), :]
                s = jnp.where(valid, _dot(qm, kb, NT), NEG)
                p = jnp.exp(s - lse)
                ds = p * (_dot(dom, vb, NT) - dsum)
                dsb = ds.astype(BF16)
                dqs.append(_dot(dsb, kb, NN))
                dk = dk + _dot(dsb, qm, TN)
                dv = dv + _dot(p.astype(BF16), dom, TN)
                new.append(dsink - jnp.sum(jnp.exp(s_ref[...] - lse) * dsum, axis=0, keepdims=True))
            dq_ref[pl.ds(qs, WINDOW), :] = jnp.where(lo, dqs[0], dqs[1]) * scale
            dk_ref[pl.ds(ks, 2 * WINDOW), :] += dk
            dv_ref[pl.ds(ks, 2 * WINDOW), :] += dv
            return tuple(new)

        dsa, dsb_ = lax.fori_loop(0, T // WINDOW, block, (jnp.zeros((1, 1), F32), jnp.zeros((1, 1), F32)), unroll=2)
        dsa_ref[...] = dsa
        dsb_ref[...] = dsb_

    full = pl.BlockSpec((T, LANES), lambda h: (0, h))
    kv = pl.BlockSpec((T, LANES), lambda h: (0, h // 2))
    sink = lambda off: pl.BlockSpec((None, 1, 1), lambda h: (2 * h + off, 0, 0))
    lspec = pl.BlockSpec((None, T, 1), lambda h: (h, 0, 0))
    dsink = pl.BlockSpec((None, 1, 1), lambda h: (h, 0, 0))
    grad = jax.ShapeDtypeStruct((T, Dh), F32)
    ds_shape = jax.ShapeDtypeStruct((HP, 1, 1), F32)
    return _call(
        body, name=name, grid=(HP,), out_shape=(grad, grad, grad, ds_shape, ds_shape),
        in_specs=[full, kv, kv, sink(0), sink(1), full, full, lspec, lspec],
        out_specs=(full, full, full, dsink, dsink),
        args=[q, kd, vd, sinks, sinks, o, do, lse_a, lse_b], carry=carry)


def _place():
    return lax.axis_index("x"), lax.axis_index("y"), lax.axis_index("c")


def _run_carry(name, carry):
    c_in, c_out = len(carry.inputs), len(carry.out_shapes)

    def body(*refs):
        ins, outs, scr = refs[:c_in], refs[c_in:c_in + c_out], refs[c_in + c_out:]
        carry.start(ins, outs, scr)
        carry.finish(ins, outs, scr)

    return pl.pallas_call(
        body, out_shape=tuple(carry.out_shapes), in_specs=[_HBM] * c_in, out_specs=tuple([_HBM] * c_out),
        scratch_shapes=carry.scratch, name=name)(*carry.inputs)


def _gather_carry(shards):
    n = len(shards)

    def plan(ins, outs, scr):
        send, recv, local = scr
        x, y, c = _place()
        me, sibling = (x, y, c), (x, y, 1 - c)
        chips = [(1 - x, y), (x, 1 - y), (1 - x, 1 - y)]

        def copy(w, k, block, to, src=None):
            slot = 4 * block[0] + 2 * block[1] + block[2]
            return pltpu.make_async_remote_copy(
                src_ref=outs[w].at[slot] if src is None else src, dst_ref=outs[w].at[slot],
                send_sem=send.at[w, k], recv_sem=recv.at[w, k], device_id=to, device_id_type=MESH)

        own = [pltpu.make_async_copy(ins[w], outs[w].at[4 * x + 2 * y + c], local.at[w]) for w in range(n)]
        first = []
        for w in range(n):
            first.append(copy(w, 0, me, sibling, src=ins[w]))
            first += [copy(w, 1 + j, me, (*chip, c), src=ins[w]) for j, chip in enumerate(chips)]
        return copy, own, first, me, sibling, chips, c

    def start(ins, outs, scr):
        _, own, first, _, _, _, _ = plan(ins, outs, scr)
        for cp in own + first:
            cp.start()

    def finish(ins, outs, scr):
        copy, own, first, me, sibling, chips, c = plan(ins, outs, scr)
        passed = []
        for w in range(n):
            for j, chip in enumerate(chips):
                copy(w, 1 + j, (*chip, c), me).wait_recv()
                fwd = copy(w, 4 + j, (*chip, c), sibling)
                fwd.start()
                passed.append(fwd)
        for w in range(n):
            copy(w, 0, sibling, me).wait_recv()
            for j, chip in enumerate(chips):
                copy(w, 4 + j, (*chip, 1 - c), me).wait_recv()
        for cp in first + passed:
            cp.wait_send()
        for cp in own:
            cp.wait()

    return _Carry(shards, [jax.ShapeDtypeStruct((N_DEV,) + s.shape, s.dtype) for s in shards],
                  [pltpu.SemaphoreType.DMA((n, 7)), pltpu.SemaphoreType.DMA((n, 7)), pltpu.SemaphoreType.DMA((n,))], start, finish)


def _sibling_carry(grads):
    n = len(grads)

    def copies(ins, outs, scr):
        send, recv = scr
        x, y, c = _place()
        return [pltpu.make_async_remote_copy(
            src_ref=ins[w].at[2 * q + (1 - c)], dst_ref=outs[w].at[q], send_sem=send.at[w, q], recv_sem=recv.at[w, q],
            device_id=(x, y, 1 - c), device_id_type=MESH) for w in range(n) for q in range(4)]

    def start(ins, outs, scr):
        for cp in copies(ins, outs, scr):
            cp.start()

    def finish(ins, outs, scr):
        for cp in copies(ins, outs, scr):
            cp.wait()

    return _Carry(grads, [jax.ShapeDtypeStruct((4,) + g.shape[1:], g.dtype) for g in grads],
                  [pltpu.SemaphoreType.DMA((n, 4)), pltpu.SemaphoreType.DMA((n, 4))], start, finish)


def _chips_carry(sums):
    n = len(sums)

    def copies(ins, outs, scr):
        send, recv = scr
        x, y, c = _place()
        chips = [(1 - x, y), (x, 1 - y), (1 - x, 1 - y)]
        return [pltpu.make_async_remote_copy(
            src_ref=ins[w].at[2 * chip[0] + chip[1]], dst_ref=outs[w].at[k], send_sem=send.at[w, k], recv_sem=recv.at[w, k],
            device_id=(*chip, c), device_id_type=MESH) for w in range(n) for k, chip in enumerate(chips)]

    def start(ins, outs, scr):
        for cp in copies(ins, outs, scr):
            cp.start()

    def finish(ins, outs, scr):
        for cp in copies(ins, outs, scr):
            cp.wait()

    return _Carry(sums, [jax.ShapeDtypeStruct((3,) + s.shape[1:], s.dtype) for s in sums],
                  [pltpu.SemaphoreType.DMA((n, 3)), pltpu.SemaphoreType.DMA((n, 3))], start, finish)


def _gather_small(packed):
    R, C = packed.shape

    def body(in_ref, out_ref, send, recv):
        x, y, c = _place()
        mine = 4 * x + 2 * y + c
        out_ref[mine] = in_ref[...]
        copies = []
        for k in range(1, N_DEV):
            peer = (x ^ (k >> 2), y ^ ((k >> 1) & 1), c ^ (k & 1))
            copies.append(pltpu.make_async_remote_copy(
                src_ref=in_ref, dst_ref=out_ref.at[mine], send_sem=send.at[k - 1], recv_sem=recv.at[k - 1],
                device_id=peer, device_id_type=MESH))
        for cp in copies:
            cp.start()
        for cp in copies:
            cp.wait()

    vmem = pl.BlockSpec(memory_space=pltpu.VMEM)
    return pl.pallas_call(
        body, out_shape=jax.ShapeDtypeStruct((N_DEV, R, C), F32), in_specs=[vmem], out_specs=vmem,
        scratch_shapes=[pltpu.SemaphoreType.DMA((N_DEV - 1,)), pltpu.SemaphoreType.DMA((N_DEV - 1,))],
        name="small_grads_all_gather")(packed)


def _adamw(w, g, m, v):
    m = ADAM_B1 * m + (1.0 - ADAM_B1) * g
    v = ADAM_B2 * v + (1.0 - ADAM_B2) * (g * g)
    m_hat = m / (1.0 - ADAM_B1 ** ADAM_STEP)
    v_hat = v / (1.0 - ADAM_B2 ** ADAM_STEP)
    delta = -ADAM_LR * (m_hat / (jnp.sqrt(v_hat) + ADAM_EPS) + ADAM_WD * w)
    return delta, m, v


def _pair_add(name, grads, received, c_idx):
    _, R, C = grads.shape
    tr = _row_tile(R)

    def body(c_ref, g_ref, r_ref, o_ref):
        o_ref[...] = (g_ref[...].astype(F32) + r_ref[...].astype(F32)).astype(BF16)

    blk = pl.BlockSpec((None, tr, C), lambda q, i, c: (q, i, 0))
    return pl.pallas_call(
        body, out_shape=jax.ShapeDtypeStruct((4, R, C), BF16),
        grid_spec=pltpu.PrefetchScalarGridSpec(
            num_scalar_prefetch=1, grid=(4, R // tr),
            in_specs=[pl.BlockSpec((None, tr, C), lambda q, i, c: (2 * q + c[0], i, 0)), blk], out_specs=blk),
        name=name, compiler_params=_params(2))(c_idx, grads, received)


def _adam_shard(name, sums, received, w, m, v, chip_idx):
    R, C = w.shape
    tr = _row_tile(R, 128)

    def body(q_ref, s_ref, r_ref, w_ref, m_ref, v_ref, g_out, d_out, m_out, v_out):
        g = s_ref[...].astype(F32) + r_ref[0].astype(F32) + r_ref[1].astype(F32) + r_ref[2].astype(F32)
        delta, mn, vn = _adamw(w_ref[...], g, m_ref[...], v_ref[...])
        g_out[...] = g
        d_out[...] = delta
        m_out[...] = mn
        v_out[...] = vn

    blk = pl.BlockSpec((tr, C), lambda i, q: (i, 0))
    shape = jax.ShapeDtypeStruct((R, C), F32)
    return pl.pallas_call(
        body, out_shape=(shape,) * 4,
        grid_spec=pltpu.PrefetchScalarGridSpec(
            num_scalar_prefetch=1, grid=(R // tr,),
            in_specs=[pl.BlockSpec((None, tr, C), lambda i, q: (q[0], i, 0)), pl.BlockSpec((3, tr, C), lambda i, q: (0, i, 0)),
                      blk, blk, blk],
            out_specs=(blk,) * 4),
        name=name, compiler_params=_params(1))(chip_idx, sums, received, w, m, v)


def _adam_small(name, gathered, w, m, v):
    R, C = w.shape

    def body(ga_ref, w_ref, m_ref, v_ref, g_out, d_out, m_out, v_out):
        g = ga_ref[0]
        for d in range(1, N_DEV):
            g = g + ga_ref[d]
        delta, mn, vn = _adamw(w_ref[...], g, m_ref[...], v_ref[...])
        g_out[...] = g
        d_out[...] = delta
        m_out[...] = mn
        v_out[...] = vn

    full = pl.BlockSpec((R, C), lambda i: (0, 0))
    shape = jax.ShapeDtypeStruct((R, C), F32)
    return pl.pallas_call(
        body, out_shape=(shape,) * 4, grid=(1,),
        in_specs=[pl.BlockSpec((N_DEV, R, C), lambda i: (0, 0, 0)), full, full, full], out_specs=(full,) * 4,
        name=name, compiler_params=_params(1))(gathered, w, m, v)


def _pack_small(parts, D):
    g1, gmix, g2, gof, gos, bf, gqf, gkf, gqs, gks, sinks = [p.reshape(-1).astype(F32) for p in parts]
    row3 = jnp.concatenate([gof, gos])
    row4 = jnp.zeros((D,), F32)
    for slot, vec in enumerate((bf, gqf, gkf, gqs, gks, sinks)):
        row4 = lax.dynamic_update_slice(row4, vec, (slot * LANES,))
    zero = jnp.zeros((D,), F32)
    return jnp.stack([g1, gmix, g2, row3, row4, zero, zero, zero])


def _unpack_small(packed, D, H):
    Dh = D // 2
    row4 = packed[4]
    short = [row4[s * LANES:s * LANES + n] for s, n in enumerate((H, HEAD_DIM, HEAD_DIM, HEAD_DIM, HEAD_DIM, H))]
    vecs = [packed[0], packed[1], packed[2], packed[3, :Dh], packed[3, Dh:]] + short
    return [v[None, :] for v in vecs]


def kernel(x, positions, norm_ffn1_g, ffn1_w_gate, ffn1_w_up, ffn1_w_down, norm_mix_g, w_in, b_forget, fox_q_norm_g, fox_k_norm_g, swa_q_norm_g, swa_k_norm_g, swa_sinks, out_norm_fox_g, out_norm_swa_g, w_out, norm_ffn2_g, ffn2_w_gate, ffn2_w_up, ffn2_w_down, loss_target, m_norm_ffn1_g, m_ffn1_w_gate, m_ffn1_w_up, m_ffn1_w_down, m_norm_mix_g, m_w_in, m_b_forget, m_fox_q_norm_g, m_fox_k_norm_g, m_swa_q_norm_g, m_swa_k_norm_g, m_swa_sinks, m_out_norm_fox_g, m_out_norm_swa_g, m_w_out, m_norm_ffn2_g, m_ffn2_w_gate, m_ffn2_w_up, m_ffn2_w_down, v_norm_ffn1_g, v_ffn1_w_gate, v_ffn1_w_up, v_ffn1_w_down, v_norm_mix_g, v_w_in, v_b_forget, v_fox_q_norm_g, v_fox_k_norm_g, v_swa_q_norm_g, v_swa_k_norm_g, v_swa_sinks, v_out_norm_fox_g, v_out_norm_swa_g, v_w_out, v_norm_ffn2_g, v_ffn2_w_gate, v_ffn2_w_up, v_ffn2_w_down):
    xs = x[0]
    target = loss_target[0]
    T, D = xs.shape
    Dh = D // 2
    H = Dh // HEAD_DIM
    HP = H // 2
    KVW = (H // GQA_GROUP) * HEAD_DIM
    KVB = KVW // LANES
    MAIN = 4 * Dh + 2 * KVW
    F_OFF = 3 * Dh
    tm = min(ROW_TILE_CAP, T)
    tq = min(512, T)
    tk = min(512, T)
    nk = T // tk
    cx, cy, cc = _place()
    c_idx = jnp.reshape(cc, (1,)).astype(jnp.int32)
    chip_idx = jnp.reshape(2 * cx + cy, (1,)).astype(jnp.int32)

    tr = jnp.transpose
    big_w = [tr(ffn1_w_gate[0]), tr(ffn1_w_up[0]), ffn1_w_down[0], w_in[0], w_out[0], tr(ffn2_w_gate[0]), tr(ffn2_w_up[0]), ffn2_w_down[0]]
    big_m = [tr(m_ffn1_w_gate[0]), tr(m_ffn1_w_up[0]), m_ffn1_w_down[0], m_w_in[0], m_w_out[0], tr(m_ffn2_w_gate[0]), tr(m_ffn2_w_up[0]),
             m_ffn2_w_down[0]]
    big_v = [tr(v_ffn1_w_gate[0]), tr(v_ffn1_w_up[0]), v_ffn1_w_down[0], v_w_in[0], v_w_out[0], tr(v_ffn2_w_gate[0]), tr(v_ffn2_w_up[0]),
             v_ffn2_w_down[0]]
    transposed = {"ffn1_w_gate", "ffn1_w_up", "ffn2_w_gate", "ffn2_w_up"}
    names = ["ffn1_w_gate", "ffn1_w_up", "ffn1_w_down", "w_in", "w_out", "ffn2_w_gate", "ffn2_w_up", "ffn2_w_down"]
    sh = dict(zip(names, [w.astype(BF16) for w in big_w]))
    (wg1,) = _run_carry("weights_all_gather", _gather_carry([sh["ffn1_w_gate"]]))

    lane = jnp.arange(LANES)
    inv_freq = ROPE_THETA ** (-(2.0 * (lane % (HEAD_DIM // 2))).astype(F32) / HEAD_DIM)
    ang = positions[0].astype(F32)[:, None] * inv_freq[None, :]
    cos_t = jnp.cos(ang)
    sin_t = jnp.where((lane & (HEAD_DIM // 2)) == 0, -1.0, 1.0)[None, :] * jnp.sin(ang)
    rope = (cos_t, sin_t)

    def pair_gain(g, blocks):
        return jnp.tile(jnp.concatenate([g[0], g[0]])[None, None, :], (blocks, 1, 1))

    n1 = _rmsnorm_fwd("ffn1_norm", xs, norm_ffn1_g, tm)
    a1, (wu1,) = _ffn_gate("ffn1_gate", n1, wg1, tm, carry=_gather_carry([sh["ffn1_w_up"]]))
    (b1, hm1), (wd1,) = _ffn_up_only("ffn1_up", n1, wu1, a1, tm, carry=_gather_carry([sh["ffn1_w_down"]]))
    h1, (win_g,) = _ffn_down("ffn1_down", hm1, wd1, xs, tm, carry=_gather_carry([sh["w_in"]]))
    n_in = win_g.shape[2]
    win_full = jnp.transpose(win_g, (1, 0, 2)).reshape(D, N_DEV * n_in)
    win_main = jnp.concatenate([win_full[:, :F_OFF], win_full[:, F_OFF + H:]], axis=1)
    win_f = jnp.pad(win_full[:, F_OFF:F_OFF + H], ((0, 0), (0, LANES - H)))

    u = _rmsnorm_fwd("mix_norm", h1, norm_mix_g, tm)
    proj, (wout_g,) = _mm_nn("mix_proj", u, win_main, tm, MAIN // 9, carry=_gather_carry([sh["w_out"]]))
    wout = wout_g.reshape(D, D)
    proj_f = _mm_nn("mix_proj_forget", u, win_f, tm, LANES)
    scale = HEAD_DIM ** -0.5
    fox_gains = jnp.concatenate([pair_gain(fox_q_norm_g, HP), pair_gain(fox_k_norm_g, HP)])
    qk_f = _headnorm_fwd_scaled("fox_qk_norm", proj, 0, 2 * HP, fox_gains, tm, scale, HP)
    v_f = proj[:, 2 * Dh:3 * Dh].astype(BF16)
    c_t, sg_t = _forget_fwd("forget_gates", proj_f[:, :H].T, b_forget.reshape(H, 1))
    crow = c_t.reshape(H, nk, 1, tk)
    (o_fox, lse_fa, lse_fb), (wg2,) = _fox_fwd("fox_attention", qk_f, v_f, crow, tq, tk, carry=_gather_carry([sh["ffn2_w_gate"]]))

    swa_q_gains = pair_gain(swa_q_norm_g, HP)
    swa_k_gains = pair_gain(swa_k_norm_g, KVB)
    q_s = _headnorm_fwd("swa_q_norm", proj, 3 * HP, HP, swa_q_gains, tm, scale, rope=rope)
    k_d = _headnorm_fwd("swa_k_norm", proj, 4 * HP, KVB, swa_k_gains, tm, 1.0, rope=rope, dup=True)
    v_s = proj[:, 4 * Dh + KVW:].astype(BF16).reshape(T, H // GQA_GROUP, 1, HEAD_DIM)
    v_d = jnp.broadcast_to(v_s, (T, H // GQA_GROUP, 2, HEAD_DIM)).reshape(T, 2 * KVW)
    sinks3 = swa_sinks.reshape(H, 1, 1)
    (o_swa, lse_sa, lse_sb), (wu2,) = _swa_fwd("swa_attention", q_s, k_d, v_d, sinks3, carry=_gather_carry([sh["ffn2_w_up"]]))

    on = _outnorm_fwd("out_norm", o_fox, o_swa, out_norm_fox_g, out_norm_swa_g, tm)
    h2 = _mm_nn("mix_out", on, wout, tm, min(512, D), resid=h1)

    n2 = _rmsnorm_fwd("ffn2_norm", h2, norm_ffn2_g, tm)
    (a2, b2, hm2), (wd2,) = _ffn_up("ffn2_up", n2, wg2, wu2, tm, carry=_gather_carry([sh["ffn2_w_down"]]))
    y = _ffn_down("ffn2_down", hm2, wd2, h2, tm)
    dy, dyh, sq = _loss_grad("loss_grad", y, target, min(256, T))
    loss = lax.psum(0.5 * sq[0, 0] / D, ("x", "y", "c"))

    J, Fs, _ = wg2.shape
    aspec = pl.BlockSpec((None, tm, Fs), lambda i, j: (j, i, 0))
    wspec = pl.BlockSpec((None, Fs, D), lambda i, j: (j, 0, 0))

    def pair_sums(keys, grads, received):
        return [_pair_add("sum_" + nm, g, r, c_idx) for nm, g, r in zip(keys, grads, received)]

    da2, db2 = _ffn_bwd_mid("ffn2_bwd_mid", dyh, wd2, a2, b2, tm)
    dwd2 = _wgrad_down("ffn2_wgrad_down", hm2, dyh, min(1024, D))
    dwg2, dwu2 = _wgrad_up("ffn2_wgrad_up", n2, da2, db2, min(1024, D))
    dn2, sib2 = _reduce_mm("ffn2_bwd_in", [(da2, aspec, wg2, wspec), (db2, aspec, wu2, wspec)], [], NN, T, D, tm, J,
                           carry=_sibling_carry([dwg2, dwu2, dwd2]))
    dh2, dg_ffn2, dh2b = _rmsnorm_bwd("ffn2_norm_bwd", dn2, h2, norm_ffn2_g, dy, min(256, T), 1.0)
    sum_wg2, sum_wu2, sum_wd2 = pair_sums(names[5:8], [dwg2, dwu2, dwd2], sib2)

    dwout = _wgrad_2d("mix_out_wgrad", on, dh2b, min(512, D), min(1024, D))
    do_fox, dg_of = _outnorm_bwd("out_norm_bwd_fox", dh2b, wout, 0, o_fox, out_norm_fox_g, tm)
    do_swa, dg_os = _outnorm_bwd("out_norm_bwd_swa", dh2b, wout, 1, o_swa, out_norm_swa_g, tm)

    (dq_f, dk_f, dv_f, dc_a, dc_b, dr_a, dr_b), (rc_wg2,) = _fox_bwd(
        "fox_attention_bwd", qk_f, v_f, o_fox, do_fox, crow, lse_fa, lse_fb, tq, tk, carry=_chips_carry([sum_wg2]))
    dqk_f = jnp.concatenate([dq_f, dk_f], axis=1)
    dqk_raw, dg_fox = _headnorm_bwd("fox_qk_norm_bwd", dqk_f, proj, 0, 2 * HP, fox_gains, HP, tm, 1.0)
    dct = jnp.stack([dc_a.reshape(HP, T), dc_b.reshape(HP, T)], axis=1).reshape(H, T)
    drt = jnp.stack([dr_a.reshape(HP, T), dr_b.reshape(HP, T)], axis=1).reshape(H, T)
    dz_t, db_f = _forget_bwd("forget_gates_bwd", dct, drt, sg_t)

    (dq_s, dk_p, dv_p, dsink_a, dsink_b), (rc_wu2,) = _swa_bwd(
        "swa_attention_bwd", q_s, k_d, v_d, sinks3, o_swa, do_swa, lse_sa, lse_sb, carry=_chips_carry([sum_wu2]))
    dqs_raw, dg_sq = _headnorm_bwd("swa_q_norm_bwd", dq_s, proj, 3 * HP, HP, swa_q_gains, HP, tm, 1.0, rope=rope)
    dks_raw, dg_sk = _headnorm_bwd("swa_k_norm_bwd", dk_p, proj, 4 * HP, KVB, swa_k_gains, KVB, tm, 1.0, rope=rope, fold=True)
    dvs_raw, _ = _headnorm_bwd("swa_v_fold", dv_p, None, 0, KVB, None, KVB, tm, 1.0, fold=True, norm=False)

    dproj = jnp.concatenate([dqk_raw, dv_f.astype(BF16), dqs_raw, dks_raw, dvs_raw], axis=1)
    dproj_f = jnp.pad(dz_t.T, ((0, 0), (0, LANES - H))).astype(BF16)
    dwin_main = _wgrad_2d("mix_proj_wgrad", u, dproj, min(1024, D), MAIN // 9)
    dwin_f = _wgrad_2d("mix_proj_forget_wgrad", u, dproj_f, min(1024, D), LANES)
    dwin_full = jnp.concatenate([dwin_main[:, :F_OFF], dwin_f[:, :H], dwin_main[:, F_OFF:]], axis=1)
    dwin_g = jnp.transpose(dwin_full.reshape(D, N_DEV, n_in), (1, 0, 2))
    dwout_g = dwout.reshape(N_DEV, D // N_DEV, D)
    tkb = MAIN // 9
    du, (rc_wd2, *sib_mix) = _reduce_mm(
        "mix_bwd_in",
        [(dproj, pl.BlockSpec((tm, tkb), lambda i, r: (i, r)), win_main, pl.BlockSpec((D, tkb), lambda i, r: (0, r)))],
        [(dproj_f, pl.BlockSpec((tm, LANES), lambda i, r: (i, 0)), win_f, pl.BlockSpec((D, LANES), lambda i, r: (0, 0)))],
        NT, T, D, tm, 9, carry=_join(_chips_carry([sum_wd2]), _sibling_carry([dwin_g, dwout_g])))
    dh1, dg_mix, dh1h = _rmsnorm_bwd("mix_norm_bwd", du, h1, norm_mix_g, dh2, min(256, T), 0.5)
    sum_win, sum_wout = pair_sums(names[3:5], [dwin_g, dwout_g], sib_mix)

    (da1, db1), (rc_win,) = _ffn_bwd_mid("ffn1_bwd_mid", dh1h, wd1, a1, b1, tm, carry=_chips_carry([sum_win]))
    (dwg1, dwu1), (rc_wout,) = _wgrad_up("ffn1_wgrad_up", n1, da1, db1, min(1024, D), carry=_chips_carry([sum_wout]))
    dwd1, sib_gu = _wgrad_down("ffn1_wgrad_down", hm1, dh1h, min(1024, D), carry=_sibling_carry([dwg1, dwu1]))
    sum_wg1, sum_wu1 = pair_sums(names[0:2], [dwg1, dwu1], sib_gu)
    dn1, (rc_wg1, rc_wu1, sib_d) = _reduce_mm(
        "ffn1_bwd_in", [(da1, aspec, wg1, wspec), (db1, aspec, wu1, wspec)], [], NN, T, D, tm, J,
        carry=_join(_chips_carry([sum_wg1, sum_wu1]), _sibling_carry([dwd1])))
    (sum_wd1,) = pair_sums(names[2:3], [dwd1], [sib_d])
    (dx, dg_ffn1), (rc_wd1,) = _rmsnorm_bwd("ffn1_norm_bwd", dn1, xs, norm_ffn1_g, dh1, min(256, T), None,
                                            carry=_chips_carry([sum_wd1]))

    chip_sums = [sum_wg1, sum_wu1, sum_wd1, sum_win, sum_wout, sum_wg2, sum_wu2, sum_wd2]
    from_chips = [rc_wg1, rc_wu1, rc_wd1, rc_win, rc_wout, rc_wg2, rc_wu2, rc_wd2]
    big_out = [_adam_shard("adam_" + nm, s, r, w, m, v, chip_idx)
               for nm, s, r, w, m, v in zip(names, chip_sums, from_chips, big_w, big_m, big_v)]

    dsinks = jnp.stack([dsink_a.reshape(HP), dsink_b.reshape(HP)], axis=1).reshape(H)
    small_g = [dg_ffn1, dg_mix, dg_ffn2, dg_of, dg_os, db_f, dg_fox[0, 0, :HEAD_DIM], dg_fox[1, 0, :HEAD_DIM],
               dg_sq[0, 0, :HEAD_DIM], dg_sk[0, 0, :HEAD_DIM], dsinks]
    small_w = [norm_ffn1_g, norm_mix_g, norm_ffn2_g, out_norm_fox_g, out_norm_swa_g, b_forget, fox_q_norm_g, fox_k_norm_g,
               swa_q_norm_g, swa_k_norm_g, swa_sinks]
    small_m = [m_norm_ffn1_g, m_norm_mix_g, m_norm_ffn2_g, m_out_norm_fox_g, m_out_norm_swa_g, m_b_forget, m_fox_q_norm_g,
               m_fox_k_norm_g, m_swa_q_norm_g, m_swa_k_norm_g, m_swa_sinks]
    small_v = [v_norm_ffn1_g, v_norm_mix_g, v_norm_ffn2_g, v_out_norm_fox_g, v_out_norm_swa_g, v_b_forget, v_fox_q_norm_g,
               v_fox_k_norm_g, v_swa_q_norm_g, v_swa_k_norm_g, v_swa_sinks]
    gathered = _gather_small(_pack_small(small_g, D))
    small_out = _adam_small("adam_small", gathered, _pack_small(small_w, D), _pack_small(small_m, D), _pack_small(small_v, D))
    small_out = [_unpack_small(p, D, H) for p in small_out]

    order = ["norm_ffn1_g", "ffn1_w_gate", "ffn1_w_up", "ffn1_w_down", "norm_mix_g", "w_in", "b_forget", "fox_q_norm_g", "fox_k_norm_g",
             "swa_q_norm_g", "swa_k_norm_g", "swa_sinks", "out_norm_fox_g", "out_norm_swa_g", "w_out", "norm_ffn2_g",
             "ffn2_w_gate", "ffn2_w_up", "ffn2_w_down"]
    small_names = ["norm_ffn1_g", "norm_mix_g", "norm_ffn2_g", "out_norm_fox_g", "out_norm_swa_g", "b_forget", "fox_q_norm_g",
                   "fox_k_norm_g", "swa_q_norm_g", "swa_k_norm_g", "swa_sinks"]
    result = [loss, dx[None]]
    for kind in range(4):
        for nm in order:
            if nm in names:
                leaf = big_out[names.index(nm)][kind]
                result.append((tr(leaf) if nm in transposed else leaf)[None])
            else:
                result.append(small_out[kind][small_names.index(nm)])
    return tuple(result)


def _headnorm_fwd_scaled(name, proj, col_off, ncb, gains, tm, scale, n_scaled):
    T = proj.shape[0]

    def body(x_ref, g_ref, o_ref):
        xv = x_ref[...]
        lo = _lane_lo(xv.shape)
        y = xv * _head_rstd(xv, lo) * g_ref[...]
        y = y * jnp.where(pl.program_id(0) < n_scaled, scale, 1.0)
        o_ref[...] = y.astype(BF16)

    return pl.pallas_call(
        body, out_shape=jax.ShapeDtypeStruct((T, ncb * LANES), BF16), grid=(ncb, T // tm),
        in_specs=[pl.BlockSpec((tm, LANES), lambda c, i: (i, col_off + c)), pl.BlockSpec((None, 1, LANES), lambda c, i: (c, 0, 0))],
        out_specs=pl.BlockSpec((tm, LANES), lambda c, i: (i, c)), name=name, compiler_params=_params(2))(proj, gains)
```

```python
import functools

import jax
import jax.numpy as jnp
from jax import lax
from jax.experimental import pallas as pl
from jax.experimental.pallas import tpu as pltpu

F32 = jnp.float32
BF16 = jnp.bfloat16

HEAD_DIM = 64
LANES = 128
WINDOW = 128
GQA_GROUP = 4
EPS = 1e-6
ROPE_THETA = 10000.0
ADAM_LR = 0.001
ADAM_B1 = 0.9
ADAM_B2 = 0.999
ADAM_EPS = 1e-08
ADAM_WD = 0.01
ADAM_STEP = 10
N_DEV = 8
NEG = -1e30
VMEM_LIMIT_V7X = 48 * 1024 * 1024
ROW_TILE_CAP = 512
MESH = pl.DeviceIdType.MESH

NN = (((1,), (0,)), ((), ()))
NT = (((1,), (1,)), ((), ()))
TN = (((0,), (0,)), ((), ()))


def _dot(a, b, dims):
    return lax.dot_general(a, b, dims, preferred_element_type=F32)


def _params(n_axes):
    return pltpu.CompilerParams(dimension_semantics=("arbitrary",) * n_axes, vmem_limit_bytes=VMEM_LIMIT_V7X)


def _row_tile(rows, cap=ROW_TILE_CAP):
    best = None
    for t in range(16, min(rows, cap) + 1, 16):
        if rows % t == 0:
            best = t
    return best or rows


def _lane_lo(shape):
    return lax.broadcasted_iota(jnp.int32, shape, len(shape) - 1) < HEAD_DIM


def _keep(sel, x):
    return jnp.where(sel, x.astype(F32), 0.0).astype(BF16)


_HBM = pl.BlockSpec(memory_space=pltpu.HBM)


class _Carry:
    def __init__(self, inputs, out_shapes, scratch, start, finish, middle=None):
        self.inputs, self.out_shapes, self.scratch = list(inputs), list(out_shapes), list(scratch)
        self.start, self.finish, self.middle = start, finish, middle or (lambda ins, outs, scr: None)


def _join(*carries):
    def hook(which):
        def run(ins, outs, scr):
            i = o = s = 0
            for c in carries:
                ni, no, ns = len(c.inputs), len(c.out_shapes), len(c.scratch)
                getattr(c, which)(ins[i:i + ni], outs[o:o + no], scr[s:s + ns])
                i, o, s = i + ni, o + no, s + ns
        return run

    return _Carry([a for c in carries for a in c.inputs], [a for c in carries for a in c.out_shapes],
                  [a for c in carries for a in c.scratch], hook("start"), hook("finish"), hook("middle"))


def _call(body, *, name, grid, in_specs, out_specs, out_shape, args, scratch_shapes=(), carry=None):
    params = _params(len(grid))
    if carry is None:
        return pl.pallas_call(body, out_shape=out_shape, grid=grid, in_specs=list(in_specs), out_specs=out_specs,
                              scratch_shapes=list(scratch_shapes), name=name, compiler_params=params)(*args)
    single = not isinstance(out_shape, (tuple, list))
    shapes = (out_shape,) if single else tuple(out_shape)
    specs = (out_specs,) if single else tuple(out_specs)
    n_in, n_out, n_scr = len(args), len(shapes), len(scratch_shapes)
    c_in, c_out = len(carry.inputs), len(carry.out_shapes)

    def wrapped(*refs):
        ins, c_ins = refs[:n_in], refs[n_in:n_in + c_in]
        o0 = n_in + c_in
        outs, c_outs = refs[o0:o0 + n_out], refs[o0 + n_out:o0 + n_out + c_out]
        s0 = o0 + n_out + c_out
        scr, c_scr = refs[s0:s0 + n_scr], refs[s0 + n_scr:]
        step, total = pl.program_id(0), grid[0]
        for ax in range(1, len(grid)):
            step, total = step * grid[ax] + pl.program_id(ax), total * grid[ax]

        @pl.when(step == 0)
        def _():
            carry.start(c_ins, c_outs, c_scr)

        @pl.when(step == total // 2)
        def _():
            carry.middle(c_ins, c_outs, c_scr)

        body(*ins, *outs, *scr)

        @pl.when(step == total - 1)
        def _():
            carry.finish(c_ins, c_outs, c_scr)

    res = pl.pallas_call(
        wrapped, out_shape=shapes + tuple(carry.out_shapes), grid=grid, in_specs=list(in_specs) + [_HBM] * c_in,
        out_specs=specs + (_HBM,) * c_out, scratch_shapes=list(scratch_shapes) + carry.scratch, name=name,
        compiler_params=params)(*args, *carry.inputs)
    main = res[:n_out]
    return (main[0] if single else tuple(main)), tuple(res[n_out:])


def _rms_bwd(dn, x, g):
    r = lax.rsqrt(jnp.mean(x * x, axis=-1, keepdims=True) + EPS)
    xh = x * r
    dxh = dn * g
    dx = r * (dxh - xh * jnp.mean(dxh * xh, axis=-1, keepdims=True))
    return dx, jnp.sum(dn * xh, axis=0, keepdims=True)


def _rmsnorm_fwd(name, x, g, tm):
    T, D = x.shape

    def body(x_ref, g_ref, o_ref):
        xf = x_ref[...]
        r = lax.rsqrt(jnp.mean(xf * xf, axis=-1, keepdims=True) + EPS)
        o_ref[...] = (xf * r * g_ref[...]).astype(BF16)

    return pl.pallas_call(
        body, out_shape=jax.ShapeDtypeStruct((T, D), BF16), grid=(T // tm,),
        in_specs=[pl.BlockSpec((tm, D), lambda i: (i, 0)), pl.BlockSpec((1, D), lambda i: (0, 0))],
        out_specs=pl.BlockSpec((tm, D), lambda i: (i, 0)), name=name, compiler_params=_params(1))(x, g)


def _outnorm_fwd(name, o_fox, o_swa, g_fox, g_swa, tm):
    T, Dh = o_fox.shape

    def body(a_ref, b_ref, ga_ref, gb_ref, o_ref):
        for ref, g_ref, lo in ((a_ref, ga_ref, 0), (b_ref, gb_ref, Dh)):
            xf = ref[...]
            r = lax.rsqrt(jnp.mean(xf * xf, axis=-1, keepdims=True) + EPS)
            o_ref[:, lo:lo + Dh] = (xf * r * g_ref[...]).astype(BF16)

    row = pl.BlockSpec((tm, Dh), lambda i: (i, 0))
    gain = pl.BlockSpec((1, Dh), lambda i: (0, 0))
    return pl.pallas_call(
        body, out_shape=jax.ShapeDtypeStruct((T, 2 * Dh), BF16), grid=(T // tm,),
        in_specs=[row, row, gain, gain], out_specs=pl.BlockSpec((tm, 2 * Dh), lambda i: (i, 0)),
        name=name, compiler_params=_params(1))(o_fox, o_swa, g_fox, g_swa)


def _outnorm_bwd(name, dhb, wout, half, o, g, tm):
    T, D = dhb.shape
    Dh = o.shape[1]

    def body(a_ref, w_ref, o_ref, g_ref, do_ref, dg_ref):
        don = _dot(a_ref[...], w_ref[...], NT)
        dx, dg = _rms_bwd(don, o_ref[...], g_ref[...])
        do_ref[...] = dx.astype(BF16)

        @pl.when(pl.program_id(0) == 0)
        def _():
            dg_ref[...] = dg

        @pl.when(pl.program_id(0) > 0)
        def _():
            dg_ref[...] += dg

    return pl.pallas_call(
        body, out_shape=(jax.ShapeDtypeStruct((T, Dh), BF16), jax.ShapeDtypeStruct((1, Dh), F32)), grid=(T // tm,),
        in_specs=[pl.BlockSpec((tm, D), lambda i: (i, 0)), pl.BlockSpec((Dh, D), lambda i: (half, 0)),
                  pl.BlockSpec((tm, Dh), lambda i: (i, 0)), pl.BlockSpec((1, Dh), lambda i: (0, 0))],
        out_specs=(pl.BlockSpec((tm, Dh), lambda i: (i, 0)), pl.BlockSpec((1, Dh), lambda i: (0, 0))),
        name=name, compiler_params=_params(1))(dhb, wout, o, g)


def _mm_nn(name, a, b, tm, tn, resid=None, carry=None):
    M, K = a.shape
    N = b.shape[1]

    def body(*refs):
        if resid is None:
            a_ref, b_ref, o_ref = refs
            o_ref[...] = _dot(a_ref[...], b_ref[...], NN)
        else:
            a_ref, b_ref, r_ref, o_ref = refs
            o_ref[...] = r_ref[...] + _dot(a_ref[...], b_ref[...], NN)

    ospec = pl.BlockSpec((tm, tn), lambda n, i: (i, n))
    in_specs = [pl.BlockSpec((tm, K), lambda n, i: (i, 0)), pl.BlockSpec((K, tn), lambda n, i: (0, n))]
    args = [a, b]
    if resid is not None:
        in_specs.append(ospec)
        args.append(resid)
    return _call(body, name=name, grid=(N // tn, M // tm), in_specs=in_specs, out_specs=ospec,
                 out_shape=jax.ShapeDtypeStruct((M, N), F32), args=args, carry=carry)


def _wgrad_2d(name, a, b, tmm, tn):
    T, M = a.shape
    N = b.shape[1]

    def body(a_ref, b_ref, o_ref):
        o_ref[...] = _dot(a_ref[...], b_ref[...], TN).astype(BF16)

    return pl.pallas_call(
        body, out_shape=jax.ShapeDtypeStruct((M, N), BF16), grid=(M // tmm, N // tn),
        in_specs=[pl.BlockSpec((T, tmm), lambda m, n: (0, m)), pl.BlockSpec((T, tn), lambda m, n: (0, n))],
        out_specs=pl.BlockSpec((tmm, tn), lambda m, n: (m, n)), name=name, compiler_params=_params(2))(a, b)


def _wgrad_down(name, hm, df, tn, carry=None):
    J, T, Fs = hm.shape
    D = df.shape[1]

    def body(a_ref, b_ref, o_ref):
        o_ref[...] = _dot(a_ref[...], b_ref[...], TN).astype(BF16)

    return _call(
        body, name=name, grid=(J, D // tn), out_shape=jax.ShapeDtypeStruct((J, Fs, D), BF16),
        in_specs=[pl.BlockSpec((None, T, Fs), lambda j, n: (j, 0, 0)), pl.BlockSpec((T, tn), lambda j, n: (0, n))],
        out_specs=pl.BlockSpec((None, Fs, tn), lambda j, n: (j, 0, n)), args=[hm, df], carry=carry)


def _wgrad_up(name, n, da, db, tn, carry=None):
    T, D = n.shape
    J, _, Fs = da.shape

    def body(n_ref, da_ref, db_ref, og_ref, ou_ref):
        nv = n_ref[...]
        og_ref[...] = _dot(da_ref[...], nv, TN).astype(BF16)
        ou_ref[...] = _dot(db_ref[...], nv, TN).astype(BF16)

    act = pl.BlockSpec((None, T, Fs), lambda j, m: (j, 0, 0))
    out = pl.BlockSpec((None, Fs, tn), lambda j, m: (j, 0, m))
    shape = jax.ShapeDtypeStruct((J, Fs, D), BF16)
    return _call(
        body, name=name, grid=(J, D // tn), out_shape=(shape, shape),
        in_specs=[pl.BlockSpec((T, tn), lambda j, m: (0, m)), act, act], out_specs=(out, out),
        args=[n, da, db], carry=carry)


def _reduce_mm(name, pairs, once, dims, T, D, tm, steps, carry=None):
    n_pairs = len(pairs)
    n_once = len(once)

    def body(*refs):
        pr = refs[:2 * n_pairs]
        on = refs[2 * n_pairs:2 * (n_pairs + n_once)]
        o_ref, acc = refs[-2:]
        r = pl.program_id(1)
        part = _dot(pr[0][...], pr[1][...], dims)
        for p in range(1, n_pairs):
            part = part + _dot(pr[2 * p][...], pr[2 * p + 1][...], dims)

        @pl.when(r == 0)
        def _():
            acc[...] = part

        @pl.when(r > 0)
        def _():
            acc[...] += part

        @pl.when(r == steps - 1)
        def _():
            dn = acc[...]
            for p in range(n_once):
                dn = dn + _dot(on[2 * p][...], on[2 * p + 1][...], dims)
            o_ref[...] = dn

    in_specs, args = [], []
    for a, a_spec, w, w_spec in list(pairs) + list(once):
        in_specs += [a_spec, w_spec]
        args += [a, w]
    row = pl.BlockSpec((tm, D), lambda i, r: (i, 0))
    return _call(body, name=name, grid=(T // tm, steps), in_specs=in_specs, out_specs=row, out_shape=jax.ShapeDtypeStruct((T, D), F32),
                 args=args, scratch_shapes=[pltpu.VMEM((tm, D), F32)], carry=carry)


def _rmsnorm_bwd(name, dn, x, g, dh, tm, bf16_scale, carry=None):
    T, D = x.shape
    emit_bf16 = bf16_scale is not None

    def body(dn_ref, x_ref, g_ref, dh_ref, *outs):
        dxn, dg = _rms_bwd(dn_ref[...], x_ref[...], g_ref[...])
        dx = dh_ref[...] + dxn
        outs[0][...] = dx
        if emit_bf16:
            outs[2][...] = (bf16_scale * dx).astype(BF16)

        @pl.when(pl.program_id(0) == 0)
        def _():
            outs[1][...] = dg

        @pl.when(pl.program_id(0) > 0)
        def _():
            outs[1][...] += dg

    row = pl.BlockSpec((tm, D), lambda i: (i, 0))
    gain = pl.BlockSpec((1, D), lambda i: (0, 0))
    out_shape = [jax.ShapeDtypeStruct((T, D), F32), jax.ShapeDtypeStruct((1, D), F32)]
    out_specs = [row, gain]
    if emit_bf16:
        out_shape.append(jax.ShapeDtypeStruct((T, D), BF16))
        out_specs.append(row)
    return _call(body, name=name, grid=(T // tm,), in_specs=[row, row, gain, row], out_specs=tuple(out_specs),
                 out_shape=tuple(out_shape), args=[dn, x, g, dh], carry=carry)


def _loss_grad(name, y, target, tm):
    T, D = y.shape

    def body(y_ref, t_ref, dy_ref, dyh_ref, sq_ref):
        diff = y_ref[...] - t_ref[...]
        sq = jnp.sum(jnp.sum(diff * diff, axis=1, keepdims=True), axis=0, keepdims=True)
        dy = diff * (1.0 / D)
        dy_ref[...] = dy
        dyh_ref[...] = (0.5 * dy).astype(BF16)

        @pl.when(pl.program_id(0) == 0)
        def _():
            sq_ref[...] = sq

        @pl.when(pl.program_id(0) > 0)
        def _():
            sq_ref[...] += sq

    row = pl.BlockSpec((tm, D), lambda i: (i, 0))
    return pl.pallas_call(
        body, out_shape=(jax.ShapeDtypeStruct((T, D), F32), jax.ShapeDtypeStruct((T, D), BF16), jax.ShapeDtypeStruct((1, 1), F32)),
        grid=(T // tm,), in_specs=[row, row], out_specs=(row, row, pl.BlockSpec((1, 1), lambda i: (0, 0))),
        name=name, compiler_params=_params(1))(y, target)


def _ffn_up(name, n, wg, wu, tm, carry=None):
    T, D = n.shape
    J, Fs, _ = wg.shape

    def body(n_ref, wg_ref, wu_ref, a_ref, b_ref, h_ref):
        xv = n_ref[...]
        a = _dot(xv, wg_ref[...], NT)
        b = _dot(xv, wu_ref[...], NT)
        a_ref[...] = a.astype(BF16)
        b_ref[...] = b.astype(BF16)
        h_ref[...] = (a * jax.nn.sigmoid(a) * b).astype(BF16)

    act = jax.ShapeDtypeStruct((J, T, Fs), BF16)
    wspec = pl.BlockSpec((None, Fs, D), lambda j, i: (j, 0, 0))
    aspec = pl.BlockSpec((None, tm, Fs), lambda j, i: (j, i, 0))
    return _call(
        body, name=name, grid=(J, T // tm), out_shape=(act, act, act),
        in_specs=[pl.BlockSpec((tm, D), lambda j, i: (i, 0)), wspec, wspec], out_specs=(aspec, aspec, aspec),
        args=[n, wg, wu], carry=carry)


def _ffn_gate(name, n, wg, tm, carry=None):
    T, D = n.shape
    J, Fs, _ = wg.shape

    def body(n_ref, wg_ref, a_ref):
        a_ref[...] = _dot(n_ref[...], wg_ref[...], NT).astype(BF16)

    aspec = pl.BlockSpec((None, tm, Fs), lambda j, i: (j, i, 0))
    return _call(
        body, name=name, grid=(J, T // tm), out_shape=jax.ShapeDtypeStruct((J, T, Fs), BF16),
        in_specs=[pl.BlockSpec((tm, D), lambda j, i: (i, 0)), pl.BlockSpec((None, Fs, D), lambda j, i: (j, 0, 0))],
        out_specs=aspec, args=[n, wg], carry=carry)


def _ffn_up_only(name, n, wu, a, tm, carry=None):
    T, D = n.shape
    J, Fs, _ = wu.shape

    def body(n_ref, wu_ref, a_ref, b_ref, h_ref):
        b = _dot(n_ref[...], wu_ref[...], NT)
        a = a_ref[...].astype(F32)
        b_ref[...] = b.astype(BF16)
        h_ref[...] = (a * jax.nn.sigmoid(a) * b).astype(BF16)

    act = jax.ShapeDtypeStruct((J, T, Fs), BF16)
    aspec = pl.BlockSpec((None, tm, Fs), lambda j, i: (j, i, 0))
    return _call(
        body, name=name, grid=(J, T // tm), out_shape=(act, act),
        in_specs=[pl.BlockSpec((tm, D), lambda j, i: (i, 0)), pl.BlockSpec((None, Fs, D), lambda j, i: (j, 0, 0)), aspec],
        out_specs=(aspec, aspec), args=[n, wu, a], carry=carry)


def _ffn_down(name, hm, wd, resid, tm, carry=None):
    J, T, Fs = hm.shape
    D = wd.shape[2]

    def body(h_ref, w_ref, r_ref, o_ref, acc):
        j = pl.program_id(1)
        part = _dot(h_ref[...], w_ref[...], NN)

        @pl.when(j == 0)
        def _():
            acc[...] = part

        @pl.when(j > 0)
        def _():
            acc[...] += part

        @pl.when(j == J - 1)
        def _():
            o_ref[...] = r_ref[...] + 0.5 * acc[...]

    row = pl.BlockSpec((tm, D), lambda i, j: (i, 0))
    return _call(
        body, name=name, grid=(T // tm, J), out_shape=jax.ShapeDtypeStruct((T, D), F32),
        in_specs=[pl.BlockSpec((None, tm, Fs), lambda i, j: (j, i, 0)), pl.BlockSpec((None, Fs, D), lambda i, j: (j, 0, 0)), row],
        out_specs=row, scratch_shapes=[pltpu.VMEM((tm, D), F32)], args=[hm, wd, resid], carry=carry)


def _ffn_bwd_mid(name, dfh, wd, a, b, tm, carry=None):
    T, D = dfh.shape
    J, Fs, _ = wd.shape

    def body(df_ref, w_ref, a_ref, b_ref, da_ref, db_ref):
        dhm = _dot(df_ref[...], w_ref[...], NT)
        av = a_ref[...].astype(F32)
        bv = b_ref[...].astype(F32)
        sg = jax.nn.sigmoid(av)
        da_ref[...] = (dhm * bv * (sg * (1.0 + av * (1.0 - sg)))).astype(BF16)
        db_ref[...] = (dhm * (av * sg)).astype(BF16)

    act = jax.ShapeDtypeStruct((J, T, Fs), BF16)
    aspec = pl.BlockSpec((None, tm, Fs), lambda j, i: (j, i, 0))
    return _call(
        body, name=name, grid=(J, T // tm), out_shape=(act, act),
        in_specs=[pl.BlockSpec((tm, D), lambda j, i: (i, 0)), pl.BlockSpec((None, Fs, D), lambda j, i: (j, 0, 0)), aspec, aspec],
        out_specs=(aspec, aspec), args=[dfh, wd, a, b], carry=carry)


def _rot_half(y, lane):
    first = (lane & (HEAD_DIM // 2)) == 0
    return jnp.where(first, pltpu.roll(y, LANES - HEAD_DIM // 2, 1), pltpu.roll(y, HEAD_DIM // 2, 1))


def _head_rstd(x, lo):
    sq = x * x
    ss_a = jnp.sum(jnp.where(lo, sq, 0.0), axis=-1, keepdims=True)
    ss_b = jnp.sum(jnp.where(lo, 0.0, sq), axis=-1, keepdims=True)
    return lax.rsqrt(jnp.where(lo, ss_a, ss_b) * (1.0 / HEAD_DIM) + EPS)


def _headnorm_fwd(name, proj, col_off, ncb, gains, tm, scale, rope=None, dup=False):
    T = proj.shape[0]
    with_rope = rope is not None
    width = 2 * LANES if dup else LANES

    def body(*refs):
        if with_rope:
            x_ref, g_ref, cos_ref, sin_ref, o_ref = refs
        else:
            x_ref, g_ref, o_ref = refs
        xv = x_ref[...]
        lane = lax.broadcasted_iota(jnp.int32, xv.shape, 1)
        lo = lane < HEAD_DIM
        y = xv * _head_rstd(xv, lo) * g_ref[...]
        if with_rope:
            y = y * cos_ref[...] + _rot_half(y, lane) * sin_ref[...]
        y = y * scale
        if dup:
            sw = pltpu.roll(y, HEAD_DIM, 1)
            o_ref[:, :LANES] = jnp.where(lo, y, sw).astype(BF16)
            o_ref[:, LANES:] = jnp.where(lo, sw, y).astype(BF16)
        else:
            o_ref[...] = y.astype(BF16)

    in_specs = [pl.BlockSpec((tm, LANES), lambda c, i: (i, col_off + c)), pl.BlockSpec((None, 1, LANES), lambda c, i: (c, 0, 0))]
    args = [proj, gains]
    if with_rope:
        tab = pl.BlockSpec((tm, LANES), lambda c, i: (i, 0))
        in_specs += [tab, tab]
        args += list(rope)
    return pl.pallas_call(
        body, out_shape=jax.ShapeDtypeStruct((T, ncb * width), BF16), grid=(ncb, T // tm),
        in_specs=in_specs, out_specs=pl.BlockSpec((tm, width), lambda c, i: (i, c)),
        name=name, compiler_params=_params(2))(*args)


def _headnorm_bwd(name, dy, proj, col_off, ncb, gains, group, tm, scale, rope=None, fold=False, norm=True):
    T = dy.shape[0]
    with_rope = rope is not None
    n_groups = ncb // group
    dy_width = 4 * LANES if fold else LANES

    def body(*refs):
        refs = list(refs)
        dy_ref = refs.pop(0)
        x_ref = refs.pop(0) if norm else None
        g_ref = refs.pop(0) if norm else None
        cos_ref = refs.pop(0) if with_rope else None
        sin_ref = refs.pop(0) if with_rope else None
        dx_ref = refs.pop(0)
        dg_ref = refs.pop(0) if norm else None
        c = pl.program_id(0)
        i = pl.program_id(1)
        d = dy_ref[...]
        lane = lax.broadcasted_iota(jnp.int32, (d.shape[0], LANES), 1)
        lo = lane < HEAD_DIM
        if fold:
            t0 = d[:, 0:LANES] + d[:, LANES:2 * LANES]
            t1 = d[:, 2 * LANES:3 * LANES] + d[:, 3 * LANES:4 * LANES]
            d = jnp.where(lo, t0 + pltpu.roll(t0, HEAD_DIM, 1), t1 + pltpu.roll(t1, HEAD_DIM, 1))
        d = d * scale
        if with_rope:
            d = d * cos_ref[...] + _rot_half(d * sin_ref[...], lane)
        if not norm:
            dx_ref[...] = d.astype(BF16)
            return
        xv = x_ref[...]
        gv = g_ref[...]
        r = _head_rstd(xv, lo)
        xh = xv * r
        dxh = d * gv
        pr = dxh * xh
        m_a = jnp.sum(jnp.where(lo, pr, 0.0), axis=-1, keepdims=True)
        m_b = jnp.sum(jnp.where(lo, 0.0, pr), axis=-1, keepdims=True)
        mean = jnp.where(lo, m_a, m_b) * (1.0 / HEAD_DIM)
        dx_ref[...] = (r * (dxh - xh * mean)).astype(BF16)
        dgp = jnp.sum(d * xh, axis=0, keepdims=True)
        dgp = dgp + pltpu.roll(dgp, HEAD_DIM, 1)
        first = jnp.logical_and(c % group == 0, i == 0)

        @pl.when(first)
        def _():
            dg_ref[...] = dgp

        @pl.when(jnp.logical_not(first))
        def _():
            dg_ref[...] += dgp

    in_specs = [pl.BlockSpec((tm, dy_width), lambda c, i: (i, c))]
    args = [dy]
    if norm:
        in_specs += [pl.BlockSpec((tm, LANES), lambda c, i: (i, col_off + c)), pl.BlockSpec((None, 1, LANES), lambda c, i: (c, 0, 0))]
        args += [proj, gains]
    if with_rope:
        tab = pl.BlockSpec((tm, LANES), lambda c, i: (i, 0))
        in_specs += [tab, tab]
        args += list(rope)
    out_shape = [jax.ShapeDtypeStruct((T, ncb * LANES), BF16)]
    out_specs = [pl.BlockSpec((tm, LANES), lambda c, i: (i, c))]
    if norm:
        out_shape.append(jax.ShapeDtypeStruct((n_groups, 1, LANES), F32))
        out_specs.append(pl.BlockSpec((None, 1, LANES), lambda c, i: (c // group, 0, 0)))
    res = pl.pallas_call(
        body, out_shape=tuple(out_shape), grid=(ncb, T // tm), in_specs=in_specs, out_specs=tuple(out_specs),
        name=name, compiler_params=_params(2))(*args)
    return res if norm else (res[0], None)


def _dot_exact(x, tri):
    hi = x.astype(BF16)
    r1 = x - hi.astype(F32)
    mid = r1.astype(BF16)
    lo = (r1 - mid.astype(F32)).astype(BF16)
    return _dot(hi, tri, NN) + _dot(mid, tri, NN) + _dot(lo, tri, NN)


def _forget_fwd(name, zt, bias):
    H, T = zt.shape
    blk = min(256, T)

    def body(z_ref, b_ref, c_ref, s_ref):
        z = z_ref[...] + b_ref[...]
        s_ref[...] = jax.nn.sigmoid(-z)
        lf = jnp.minimum(z, 0.0) - jnp.log(1.0 + jnp.exp(-jnp.abs(z)))
        tri = (lax.broadcasted_iota(jnp.int32, (blk, blk), 0) <= lax.broadcasted_iota(jnp.int32, (blk, blk), 1)).astype(BF16)
        carry = jnp.zeros((H, 1), F32)
        for bi in range(T // blk):
            xb = lf[:, bi * blk:(bi + 1) * blk]
            c_ref[:, bi * blk:(bi + 1) * blk] = _dot_exact(xb, tri) + carry
            carry = carry + jnp.sum(xb, axis=-1, keepdims=True)

    shape = jax.ShapeDtypeStruct((H, T), F32)
    full = pl.BlockSpec((H, T), lambda i: (0, 0))
    return pl.pallas_call(
        body, out_shape=(shape, shape), grid=(1,), in_specs=[full, pl.BlockSpec((H, 1), lambda i: (0, 0))],
        out_specs=(full, full), name=name, compiler_params=_params(1))(zt, bias)


def _forget_bwd(name, dct, drt, sgt):
    H, T = dct.shape
    blk = min(256, T)

    def body(dc_ref, dr_ref, s_ref, dz_ref, db_ref):
        dc = dc_ref[...] + dr_ref[...]
        tri = (lax.broadcasted_iota(jnp.int32, (blk, blk), 0) >= lax.broadcasted_iota(jnp.int32, (blk, blk), 1)).astype(BF16)
        carry = jnp.zeros((H, 1), F32)
        db = jnp.zeros((H, 1), F32)
        for bi in reversed(range(T // blk)):
            xb = dc[:, bi * blk:(bi + 1) * blk]
            dz = (_dot_exact(xb, tri) + carry) * s_ref[:, bi * blk:(bi + 1) * blk]
            dz_ref[:, bi * blk:(bi + 1) * blk] = dz
            db = db + jnp.sum(dz, axis=-1, keepdims=True)
            carry = carry + jnp.sum(xb, axis=-1, keepdims=True)
        db_ref[...] = db

    full = pl.BlockSpec((H, T), lambda i: (0, 0))
    return pl.pallas_call(
        body, out_shape=(jax.ShapeDtypeStruct((H, T), F32), jax.ShapeDtypeStruct((H, 1), F32)), grid=(1,),
        in_specs=[full, full, full], out_specs=(full, pl.BlockSpec((H, 1), lambda i: (0, 0))),
        name=name, compiler_params=_params(1))(dct, drt, sgt)


STRIP = 256


def _fox_fwd(name, qk, v, crow, tq, tk, carry=None):
    T, Dh = v.shape
    HP = Dh // LANES
    nk = T // tk
    assert tk % tq == 0 and tq % STRIP == 0
    n_strips = tq // STRIP

    def body(q_ref, k_ref, v_ref, ra_ref, rb_ref, o_ref, la_ref, lb_ref, s_ref, p_ref, m_ref, l_ref, acc_ref):
        i = pl.program_id(1)
        q2 = q_ref[...]
        lo = _lane_lo((tq, LANES))
        qms = (_keep(lo, q2), _keep(jnp.logical_not(lo), q2))
        r_refs = (ra_ref, rb_ref)
        m_ref[...] = jnp.full(m_ref.shape, NEG, F32)
        l_ref[...] = jnp.zeros(l_ref.shape, F32)
        acc_ref[...] = jnp.zeros(acc_ref.shape, F32)
        rel = lax.broadcasted_iota(jnp.int32, (STRIP, tk), 0) - lax.broadcasted_iota(jnp.int32, (STRIP, tk), 1)

        def chunk(kc, masked):
            start = pl.multiple_of(kc * tk, tk)
            kb = k_ref[pl.ds(start, tk), :]
            vb = v_ref[pl.ds(start, tk), :]
            for h in range(2):
                s_ref[h] = _dot(qms[h], kb, NT)
            for h in range(2):
                cs = r_refs[h][kc]
                for st in range(n_strips):
                    rows = pl.ds(st * STRIP, STRIP)
                    s = s_ref[h, rows, :] - cs
                    if masked:
                        s = jnp.where(rel >= start - (i * tq + st * STRIP), s, NEG)
                    m_old = m_ref[h, rows, :]
                    mn = jnp.maximum(m_old, jnp.max(s, axis=-1, keepdims=True))
                    p = jnp.exp(s - mn)
                    alpha = jnp.exp(m_old - mn)
                    l_ref[h, rows, :] = alpha * l_ref[h, rows, :] + jnp.sum(p, axis=-1, keepdims=True)
                    m_ref[h, rows, :] = mn
                    p_ref[h, rows, :] = p.astype(BF16)
                    acc_ref[h, rows, :] = acc_ref[h, rows, :] * alpha
            for h in range(2):
                acc_ref[h] += _dot(p_ref[h], vb, NN)

        n_full = (i * tq) // tk

        def full_chunk(kc, _):
            chunk(kc, False)
            return 0

        lax.fori_loop(0, n_full, full_chunk, 0)
        chunk(n_full, True)
        o_ref[...] = jnp.where(lo, acc_ref[0] / l_ref[0], acc_ref[1] / l_ref[1])
        la_ref[...] = m_ref[0] + jnp.log(l_ref[0])
        lb_ref[...] = m_ref[1] + jnp.log(l_ref[1])

    row = lambda off: pl.BlockSpec((None, nk, 1, tk), lambda h, i: (2 * h + off, 0, 0, 0))
    lse = jax.ShapeDtypeStruct((HP, T, 1), F32)
    lspec = pl.BlockSpec((None, tq, 1), lambda h, i: (h, i, 0))
    scratch = [pltpu.VMEM((2, tq, tk), F32), pltpu.VMEM((2, tq, tk), BF16), pltpu.VMEM((2, tq, 1), F32),
               pltpu.VMEM((2, tq, 1), F32), pltpu.VMEM((2, tq, LANES), F32)]
    return _call(
        body, name=name, grid=(HP, T // tq), out_shape=(jax.ShapeDtypeStruct((T, Dh), F32), lse, lse),
        in_specs=[pl.BlockSpec((tq, LANES), lambda h, i: (i, h)), pl.BlockSpec((T, LANES), lambda h, i: (0, HP + h)),
                  pl.BlockSpec((T, LANES), lambda h, i: (0, h)), row(0), row(1)],
        out_specs=(pl.BlockSpec((tq, LANES), lambda h, i: (i, h)), lspec, lspec),
        args=[qk, qk, v, crow, crow], scratch_shapes=scratch, carry=carry)


def _fox_bwd(name, qk, v, o, do, crow, lse_a, lse_b, tq, tk, carry=None):
    T, Dh = v.shape
    HP = Dh // LANES
    nk = T // tk
    scale = HEAD_DIM ** -0.5
    assert tk % tq == 0 and tq % STRIP == 0
    n_strips = tq // STRIP

    def body(q_ref, k_ref, v_ref, o_ref, do_ref, ra_ref, rb_ref, la_ref, lb_ref,
             dq_ref, dk_ref, dv_ref, dca_ref, dcb_ref, dra_ref, drb_ref, s_ref, dp_ref, p_ref, ds_ref, dq_acc, dsum_ref):
        i = pl.program_id(1)

        @pl.when(i == 0)
        def _():
            dk_ref[...] = jnp.zeros_like(dk_ref)
            dv_ref[...] = jnp.zeros_like(dv_ref)
            dca_ref[...] = jnp.zeros_like(dca_ref)
            dcb_ref[...] = jnp.zeros_like(dcb_ref)

        q2 = q_ref[...]
        do2 = do_ref[...]
        lo = _lane_lo((tq, LANES))
        hi = jnp.logical_not(lo)
        qms = (_keep(lo, q2), _keep(hi, q2))
        doms = (_keep(lo, do2), _keep(hi, do2))
        prod = do2.astype(F32) * o_ref[...]
        dsum_ref[0] = jnp.sum(jnp.where(lo, prod, 0.0), axis=-1, keepdims=True)
        dsum_ref[1] = jnp.sum(jnp.where(lo, 0.0, prod), axis=-1, keepdims=True)
        r_refs, l_refs, dc_refs, dr_refs = (ra_ref, rb_ref), (la_ref, lb_ref), (dca_ref, dcb_ref), (dra_ref, drb_ref)
        dq_acc[...] = jnp.zeros(dq_acc.shape, F32)
        dra_ref[...] = jnp.zeros(dra_ref.shape, F32)
        drb_ref[...] = jnp.zeros(drb_ref.shape, F32)
        rel = lax.broadcasted_iota(jnp.int32, (STRIP, tk), 0) - lax.broadcasted_iota(jnp.int32, (STRIP, tk), 1)

        def chunk(kc, masked):
            start = pl.multiple_of(kc * tk, tk)
            kb = k_ref[pl.ds(start, tk), :]
            vb = v_ref[pl.ds(start, tk), :]
            for h in range(2):
                s_ref[h] = _dot(qms[h], kb, NT)
                dp_ref[h] = _dot(doms[h], vb, NT)
            for h in range(2):
                cs = r_refs[h][kc]
                col_sum = jnp.zeros((1, tk), F32)
                for st in range(n_strips):
                    rows = pl.ds(st * STRIP, STRIP)
                    s = s_ref[h, rows, :] - cs
                    if masked:
                        s = jnp.where(rel >= start - (i * tq + st * STRIP), s, NEG)
                    p = jnp.exp(s - l_refs[h][rows, :])
                    ds = p * (dp_ref[h, rows, :] - dsum_ref[h, rows, :])
                    p_ref[h, rows, :] = p.astype(BF16)
                    ds_ref[h, rows, :] = ds.astype(BF16)
                    col_sum = col_sum + jnp.sum(ds, axis=0, keepdims=True)
                    dr_refs[h][rows, :] += jnp.sum(ds, axis=-1, keepdims=True)
                dc_refs[h][kc] = dc_refs[h][kc] - col_sum
            dk = _dot(ds_ref[0], qms[0], TN) + _dot(ds_ref[1], qms[1], TN)
            dv = _dot(p_ref[0], doms[0], TN) + _dot(p_ref[1], doms[1], TN)
            dk_ref[pl.ds(start, tk), :] += dk
            dv_ref[pl.ds(start, tk), :] += dv
            for h in range(2):
                dq_acc[h] += _dot(ds_ref[h], kb, NN)

        n_full = (i * tq) // tk

        def full_chunk(kc, _):
            chunk(kc, False)
            return 0

        lax.fori_loop(0, n_full, full_chunk, 0)
        chunk(n_full, True)
        dq_ref[...] = jnp.where(lo, dq_acc[0], dq_acc[1]) * scale

    row = lambda off: pl.BlockSpec((None, nk, 1, tk), lambda h, i: (2 * h + off, 0, 0, 0))
    lspec = pl.BlockSpec((None, tq, 1), lambda h, i: (h, i, 0))
    qspec = pl.BlockSpec((tq, LANES), lambda h, i: (i, h))
    full = pl.BlockSpec((T, LANES), lambda h, i: (0, h))
    dcspec = pl.BlockSpec((None, nk, 1, tk), lambda h, i: (h, 0, 0, 0))
    grad = jax.ShapeDtypeStruct((T, Dh), F32)
    dc = jax.ShapeDtypeStruct((HP, nk, 1, tk), F32)
    dr = jax.ShapeDtypeStruct((HP, T, 1), F32)
    scratch = [pltpu.VMEM((2, tq, tk), F32), pltpu.VMEM((2, tq, tk), F32), pltpu.VMEM((2, tq, tk), BF16), pltpu.VMEM((2, tq, tk), BF16),
               pltpu.VMEM((2, tq, LANES), F32), pltpu.VMEM((2, tq, 1), F32)]
    return _call(
        body, name=name, grid=(HP, T // tq), out_shape=(grad, grad, grad, dc, dc, dr, dr),
        in_specs=[qspec, pl.BlockSpec((T, LANES), lambda h, i: (0, HP + h)), full, qspec, qspec, row(0), row(1), lspec, lspec],
        out_specs=(qspec, full, full, dcspec, dcspec, lspec, lspec),
        args=[qk, qk, v, o, do, crow, crow, lse_a, lse_b], scratch_shapes=scratch, carry=carry)


def _swa_block(n, q_ref, k_ref):
    qs = pl.multiple_of(n * WINDOW, WINDOW)
    ks = pl.multiple_of(jnp.maximum(n - 1, 0) * WINDOW, WINDOW)
    rel = (qs + lax.broadcasted_iota(jnp.int32, (WINDOW, 2 * WINDOW), 0)) - (ks + lax.broadcasted_iota(jnp.int32, (WINDOW, 2 * WINDOW), 1))
    valid = jnp.logical_and(rel >= 0, rel < WINDOW)
    return qs, ks, valid


def _swa_fwd(name, q, kd, vd, sinks, carry=None):
    T, Dh = q.shape
    HP = Dh // LANES

    def body(q_ref, k_ref, v_ref, sa_ref, sb_ref, o_ref, la_ref, lb_ref):
        lo = _lane_lo((WINDOW, LANES))

        def block(n, _):
            qs, ks, valid = _swa_block(n, q_ref, k_ref)
            q2 = q_ref[pl.ds(qs, WINDOW), :]
            kb = k_ref[pl.ds(ks, 2 * WINDOW), :]
            vb = v_ref[pl.ds(ks, 2 * WINDOW), :]
            res = []
            for sel, s_ref in ((lo, sa_ref), (jnp.logical_not(lo), sb_ref)):
                qm = _keep(sel, q2)
                sink = s_ref[...]
                s = jnp.where(valid, _dot(qm, kb, NT), NEG)
                m = jnp.maximum(jnp.max(s, axis=-1, keepdims=True), sink)
                p = jnp.exp(s - m)
                l = jnp.sum(p, axis=-1, keepdims=True) + jnp.exp(sink - m)
                res.append((_dot(p.astype(BF16), vb, NN) / l, m + jnp.log(l)))
            o_ref[pl.ds(qs, WINDOW), :] = jnp.where(lo, res[0][0], res[1][0])
            la_ref[pl.ds(qs, WINDOW), :] = res[0][1]
            lb_ref[pl.ds(qs, WINDOW), :] = res[1][1]
            return 0

        lax.fori_loop(0, T // WINDOW, block, 0, unroll=2)

    full = pl.BlockSpec((T, LANES), lambda h: (0, h))
    kv = pl.BlockSpec((T, LANES), lambda h: (0, h // 2))
    sink = lambda off: pl.BlockSpec((None, 1, 1), lambda h: (2 * h + off, 0, 0))
    lse = jax.ShapeDtypeStruct((HP, T, 1), F32)
    lspec = pl.BlockSpec((None, T, 1), lambda h: (h, 0, 0))
    return _call(
        body, name=name, grid=(HP,), out_shape=(jax.ShapeDtypeStruct((T, Dh), F32), lse, lse),
        in_specs=[full, kv, kv, sink(0), sink(1)], out_specs=(full, lspec, lspec),
        args=[q, kd, vd, sinks, sinks], carry=carry)


def _swa_bwd(name, q, kd, vd, sinks, o, do, lse_a, lse_b, carry=None):
    T, Dh = q.shape
    HP = Dh // LANES
    scale = HEAD_DIM ** -0.5

    def body(q_ref, k_ref, v_ref, sa_ref, sb_ref, o_ref, do_ref, la_ref, lb_ref, dq_ref, dk_ref, dv_ref, dsa_ref, dsb_ref):
        lo = _lane_lo((WINDOW, LANES))
        hi = jnp.logical_not(lo)
        dk_ref[...] = jnp.zeros_like(dk_ref)
        dv_ref[...] = jnp.zeros_like(dv_ref)

        def block(n, dsinks):
            qs, ks, valid = _swa_block(n, q_ref, k_ref)
            q2 = q_ref[pl.ds(qs, WINDOW), :]
            do2 = do_ref[pl.ds(qs, WINDOW), :]
            kb = k_ref[pl.ds(ks, 2 * WINDOW), :]
            vb = v_ref[pl.ds(ks, 2 * WINDOW), :]
            prod = do2.astype(F32) * o_ref[pl.ds(qs, WINDOW), :]
            dqs, new = [], []
            dk = jnp.zeros((2 * WINDOW, LANES), F32)
            dv = jnp.zeros((2 * WINDOW, LANES), F32)
            for sel, s_ref, l_ref, dsink in ((lo, sa_ref, la_ref, dsinks[0]), (hi, sb_ref, lb_ref, dsinks[1])):
                qm = _keep(sel, q2)
                dom = _keep(sel, do2)
                dsum = jnp.sum(jnp.where(sel, prod, 0.0), axis=-1, keepdims=True)
                lse = l_ref[pl.ds(qs, WINDOW), :]
                s = jnp.where(valid, _dot(qm, kb, NT), NEG)
                p = jnp.exp(s - lse)
                ds = p * (_dot(dom, vb, NT) - dsum)
                dsb = ds.astype(BF16)
                dqs.append(_dot(dsb, kb, NN))
                dk = dk + _dot(dsb, qm, TN)
                dv = dv + _dot(p.astype(BF16), dom, TN)
                new.append(dsink - jnp.sum(jnp.exp(s_ref[...] - lse) * dsum, axis=0, keepdims=True))
            dq_ref[pl.ds(qs, WINDOW), :] = jnp.where(lo, dqs[0], dqs[1]) * scale
            dk_ref[pl.ds(ks, 2 * WINDOW), :] += dk
            dv_ref[pl.ds(ks, 2 * WINDOW), :] += dv
            return tuple(new)

        dsa, dsb_ = lax.fori_loop(0, T // WINDOW, block, (jnp.zeros((1, 1), F32), jnp.zeros((1, 1), F32)), unroll=2)
        dsa_ref[...] = dsa
        dsb_ref[...] = dsb_

    full = pl.BlockSpec((T, LANES), lambda h: (0, h))
    kv = pl.BlockSpec((T, LANES), lambda h: (0, h // 2))
    sink = lambda off: pl.BlockSpec((None, 1, 1), lambda h: (2 * h + off, 0, 0))
    lspec = pl.BlockSpec((None, T, 1), lambda h: (h, 0, 0))
    dsink = pl.BlockSpec((None, 1, 1), lambda h: (h, 0, 0))
    grad = jax.ShapeDtypeStruct((T, Dh), F32)
    ds_shape = jax.ShapeDtypeStruct((HP, 1, 1), F32)
    return _call(
        body, name=name, grid=(HP,), out_shape=(grad, grad, grad, ds_shape, ds_shape),
        in_specs=[full, kv, kv, sink(0), sink(1), full, full, lspec, lspec],
        out_specs=(full, full, full, dsink, dsink),
        args=[q, kd, vd, sinks, sinks, o, do, lse_a, lse_b], carry=carry)


def _place():
    return lax.axis_index("x"), lax.axis_index("y"), lax.axis_index("c")


def _run_carry(name, carry):
    c_in, c_out = len(carry.inputs), len(carry.out_shapes)

    def body(*refs):
        ins, outs, scr = refs[:c_in], refs[c_in:c_in + c_out], refs[c_in + c_out:]
        carry.start(ins, outs, scr)
        carry.middle(ins, outs, scr)
        carry.finish(ins, outs, scr)

    return pl.pallas_call(
        body, out_shape=tuple(carry.out_shapes), in_specs=[_HBM] * c_in, out_specs=tuple([_HBM] * c_out),
        scratch_shapes=carry.scratch, name=name)(*carry.inputs)


def _gather_carry(shards):
    n = len(shards)

    def plan(ins, outs, scr):
        send, recv, local = scr
        x, y, c = _place()
        me, sibling = (x, y, c), (x, y, 1 - c)
        partner, other, diag = (x ^ c, y ^ (1 - c)), (x ^ (1 - c), y ^ c), (1 - x, 1 - y)

        def copy(w, k, block, to, src=None):
            slot = 4 * block[0] + 2 * block[1] + block[2]
            return pltpu.make_async_remote_copy(
                src_ref=outs[w].at[slot] if src is None else src, dst_ref=outs[w].at[slot],
                send_sem=send.at[w, k], recv_sem=recv.at[w, k], device_id=to, device_id_type=MESH)

        own = [pltpu.make_async_copy(ins[w], outs[w].at[4 * x + 2 * y + c], local.at[w]) for w in range(n)]
        return copy, own, me, sibling, partner, other, diag, c

    def start(ins, outs, scr):
        copy, own, me, sibling, partner, other, _, c = plan(ins, outs, scr)
        for cp in own:
            cp.start()
        for w in range(n):
            copy(w, 1, me, (*partner, c), src=ins[w]).start()
            copy(w, 2, me, (*other, c), src=ins[w]).start()
            copy(w, 0, me, sibling, src=ins[w]).start()

    def middle(ins, outs, scr):
        copy, _, me, sibling, partner, other, _, c = plan(ins, outs, scr)
        for w in range(n):
            copy(w, 1, (*partner, c), me).wait_recv()
            copy(w, 3, (*partner, c), (*other, c)).start()
            copy(w, 4, (*partner, c), sibling).start()

    def finish(ins, outs, scr):
        copy, own, me, sibling, partner, other, diag, c = plan(ins, outs, scr)
        for w in range(n):
            copy(w, 2, (*other, c), me).wait_recv()
            copy(w, 5, (*other, c), sibling).start()
        for w in range(n):
            copy(w, 3, (*diag, c), me).wait_recv()
            copy(w, 6, (*diag, c), sibling).start()
        for w in range(n):
            copy(w, 0, sibling, me).wait_recv()
            copy(w, 4, (*other, 1 - c), me).wait_recv()
            copy(w, 5, (*partner, 1 - c), me).wait_recv()
            copy(w, 6, (*diag, 1 - c), me).wait_recv()
        for w in range(n):
            sent = [copy(w, 0, me, sibling, src=ins[w]), copy(w, 1, me, (*partner, c), src=ins[w]), copy(w, 2, me, (*other, c), src=ins[w]),
                    copy(w, 3, (*partner, c), (*other, c)), copy(w, 4, (*partner, c), sibling), copy(w, 5, (*other, c), sibling),
                    copy(w, 6, (*diag, c), sibling)]
            for cp in sent:
                cp.wait_send()
        for cp in own:
            cp.wait()

    return _Carry(shards, [jax.ShapeDtypeStruct((N_DEV,) + s.shape, s.dtype) for s in shards],
                  [pltpu.SemaphoreType.DMA((n, 7)), pltpu.SemaphoreType.DMA((n, 7)), pltpu.SemaphoreType.DMA((n,))], start, finish, middle)


def _sibling_carry(grads):
    n = len(grads)

    def copies(ins, outs, scr):
        send, recv = scr
        x, y, c = _place()
        return [pltpu.make_async_remote_copy(
            src_ref=ins[w].at[2 * q + (1 - c)], dst_ref=outs[w].at[q], send_sem=send.at[w, q], recv_sem=recv.at[w, q],
            device_id=(x, y, 1 - c), device_id_type=MESH) for w in range(n) for q in range(4)]

    def start(ins, outs, scr):
        for cp in copies(ins, outs, scr):
            cp.start()

    def finish(ins, outs, scr):
        for cp in copies(ins, outs, scr):
            cp.wait()

    return _Carry(grads, [jax.ShapeDtypeStruct((4,) + g.shape[1:], g.dtype) for g in grads],
                  [pltpu.SemaphoreType.DMA((n, 4)), pltpu.SemaphoreType.DMA((n, 4))], start, finish)


def _chips_carry(sums):
    n = len(sums)

    def copies(ins, outs, scr):
        send, recv = scr
        x, y, c = _place()
        chips = [(1 - x, y), (x, 1 - y), (1 - x, 1 - y)]
        return [pltpu.make_async_remote_copy(
            src_ref=ins[w].at[2 * chip[0] + chip[1]], dst_ref=outs[w].at[k], send_sem=send.at[w, k], recv_sem=recv.at[w, k],
            device_id=(*chip, c), device_id_type=MESH) for w in range(n) for k, chip in enumerate(chips)]

    def start(ins, outs, scr):
        for cp in copies(ins, outs, scr):
            cp.start()

    def finish(ins, outs, scr):
        for cp in copies(ins, outs, scr):
            cp.wait()

    return _Carry(sums, [jax.ShapeDtypeStruct((3,) + s.shape[1:], s.dtype) for s in sums],
                  [pltpu.SemaphoreType.DMA((n, 3)), pltpu.SemaphoreType.DMA((n, 3))], start, finish)


def _gather_small(packed):
    R, C = packed.shape

    def body(in_ref, out_ref, send, recv):
        x, y, c = _place()
        mine = 4 * x + 2 * y + c
        out_ref[mine] = in_ref[...]
        copies = []
        for k in range(1, N_DEV):
            peer = (x ^ (k >> 2), y ^ ((k >> 1) & 1), c ^ (k & 1))
            copies.append(pltpu.make_async_remote_copy(
                src_ref=in_ref, dst_ref=out_ref.at[mine], send_sem=send.at[k - 1], recv_sem=recv.at[k - 1],
                device_id=peer, device_id_type=MESH))
        for cp in copies:
            cp.start()
        for cp in copies:
            cp.wait()

    vmem = pl.BlockSpec(memory_space=pltpu.VMEM)
    return pl.pallas_call(
        body, out_shape=jax.ShapeDtypeStruct((N_DEV, R, C), F32), in_specs=[vmem], out_specs=vmem,
        scratch_shapes=[pltpu.SemaphoreType.DMA((N_DEV - 1,)), pltpu.SemaphoreType.DMA((N_DEV - 1,))],
        name="small_grads_all_gather")(packed)


def _adamw(w, g, m, v):
    m = ADAM_B1 * m + (1.0 - ADAM_B1) * g
    v = ADAM_B2 * v + (1.0 - ADAM_B2) * (g * g)
    m_hat = m / (1.0 - ADAM_B1 ** ADAM_STEP)
    v_hat = v / (1.0 - ADAM_B2 ** ADAM_STEP)
    delta = -ADAM_LR * (m_hat / (jnp.sqrt(v_hat) + ADAM_EPS) + ADAM_WD * w)
    return delta, m, v


def _pair_add(name, grads, received, c_idx):
    _, R, C = grads.shape
    tr = _row_tile(R)

    def body(c_ref, g_ref, r_ref, o_ref):
        o_ref[...] = (g_ref[...].astype(F32) + r_ref[...].astype(F32)).astype(BF16)

    blk = pl.BlockSpec((None, tr, C), lambda q, i, c: (q, i, 0))
    return pl.pallas_call(
        body, out_shape=jax.ShapeDtypeStruct((4, R, C), BF16),
        grid_spec=pltpu.PrefetchScalarGridSpec(
            num_scalar_prefetch=1, grid=(4, R // tr),
            in_specs=[pl.BlockSpec((None, tr, C), lambda q, i, c: (2 * q + c[0], i, 0)), blk], out_specs=blk),
        name=name, compiler_params=_params(2))(c_idx, grads, received)


def _adam_shard(name, sums, received, w, m, v, chip_idx):
    R, C = w.shape
    tr = _row_tile(R, 128)

    def body(q_ref, s_ref, r_ref, w_ref, m_ref, v_ref, g_out, d_out, m_out, v_out):
        g = s_ref[...].astype(F32) + r_ref[0].astype(F32) + r_ref[1].astype(F32) + r_ref[2].astype(F32)
        delta, mn, vn = _adamw(w_ref[...], g, m_ref[...], v_ref[...])
        g_out[...] = g
        d_out[...] = delta
        m_out[...] = mn
        v_out[...] = vn

    blk = pl.BlockSpec((tr, C), lambda i, q: (i, 0))
    shape = jax.ShapeDtypeStruct((R, C), F32)
    return pl.pallas_call(
        body, out_shape=(shape,) * 4,
        grid_spec=pltpu.PrefetchScalarGridSpec(
            num_scalar_prefetch=1, grid=(R // tr,),
            in_specs=[pl.BlockSpec((None, tr, C), lambda i, q: (q[0], i, 0)), pl.BlockSpec((3, tr, C), lambda i, q: (0, i, 0)),
                      blk, blk, blk],
            out_specs=(blk,) * 4),
        name=name, compiler_params=_params(1))(chip_idx, sums, received, w, m, v)


def _adam_small(name, gathered, w, m, v):
    R, C = w.shape

    def body(ga_ref, w_ref, m_ref, v_ref, g_out, d_out, m_out, v_out):
        g = ga_ref[0]
        for d in range(1, N_DEV):
            g = g + ga_ref[d]
        delta, mn, vn = _adamw(w_ref[...], g, m_ref[...], v_ref[...])
        g_out[...] = g
        d_out[...] = delta
        m_out[...] = mn
        v_out[...] = vn

    full = pl.BlockSpec((R, C), lambda i: (0, 0))
    shape = jax.ShapeDtypeStruct((R, C), F32)
    return pl.pallas_call(
        body, out_shape=(shape,) * 4, grid=(1,),
        in_specs=[pl.BlockSpec((N_DEV, R, C), lambda i: (0, 0, 0)), full, full, full], out_specs=(full,) * 4,
        name=name, compiler_params=_params(1))(gathered, w, m, v)


def _pack_small(parts, D):
    g1, gmix, g2, gof, gos, bf, gqf, gkf, gqs, gks, sinks = [p.reshape(-1).astype(F32) for p in parts]
    row3 = jnp.concatenate([gof, gos])
    row4 = jnp.zeros((D,), F32)
    for slot, vec in enumerate((bf, gqf, gkf, gqs, gks, sinks)):
        row4 = lax.dynamic_update_slice(row4, vec, (slot * LANES,))
    zero = jnp.zeros((D,), F32)
    return jnp.stack([g1, gmix, g2, row3, row4, zero, zero, zero])


def _unpack_small(packed, D, H):
    Dh = D // 2
    row4 = packed[4]
    short = [row4[s * LANES:s * LANES + n] for s, n in enumerate((H, HEAD_DIM, HEAD_DIM, HEAD_DIM, HEAD_DIM, H))]
    vecs = [packed[0], packed[1], packed[2], packed[3, :Dh], packed[3, Dh:]] + short
    return [v[None, :] for v in vecs]


def kernel(x, positions, norm_ffn1_g, ffn1_w_gate, ffn1_w_up, ffn1_w_down, norm_mix_g, w_in, b_forget, fox_q_norm_g, fox_k_norm_g, swa_q_norm_g, swa_k_norm_g, swa_sinks, out_norm_fox_g, out_norm_swa_g, w_out, norm_ffn2_g, ffn2_w_gate, ffn2_w_up, ffn2_w_down, loss_target, m_norm_ffn1_g, m_ffn1_w_gate, m_ffn1_w_up, m_ffn1_w_down, m_norm_mix_g, m_w_in, m_b_forget, m_fox_q_norm_g, m_fox_k_norm_g, m_swa_q_norm_g, m_swa_k_norm_g, m_swa_sinks, m_out_norm_fox_g, m_out_norm_swa_g, m_w_out, m_norm_ffn2_g, m_ffn2_w_gate, m_ffn2_w_up, m_ffn2_w_down, v_norm_ffn1_g, v_ffn1_w_gate, v_ffn1_w_up, v_ffn1_w_down, v_norm_mix_g, v_w_in, v_b_forget, v_fox_q_norm_g, v_fox_k_norm_g, v_swa_q_norm_g, v_swa_k_norm_g, v_swa_sinks, v_out_norm_fox_g, v_out_norm_swa_g, v_w_out, v_norm_ffn2_g, v_ffn2_w_gate, v_ffn2_w_up, v_ffn2_w_down):
    xs = x[0]
    target = loss_target[0]
    T, D = xs.shape
    Dh = D // 2
    H = Dh // HEAD_DIM
    HP = H // 2
    KVW = (H // GQA_GROUP) * HEAD_DIM
    KVB = KVW // LANES
    MAIN = 4 * Dh + 2 * KVW
    F_OFF = 3 * Dh
    tm = min(ROW_TILE_CAP, T)
    tq = min(512, T)
    tk = min(512, T)
    nk = T // tk
    cx, cy, cc = _place()
    c_idx = jnp.reshape(cc, (1,)).astype(jnp.int32)
    chip_idx = jnp.reshape(2 * cx + cy, (1,)).astype(jnp.int32)

    tr = jnp.transpose
    big_w = [tr(ffn1_w_gate[0]), tr(ffn1_w_up[0]), ffn1_w_down[0], w_in[0], w_out[0], tr(ffn2_w_gate[0]), tr(ffn2_w_up[0]), ffn2_w_down[0]]
    big_m = [tr(m_ffn1_w_gate[0]), tr(m_ffn1_w_up[0]), m_ffn1_w_down[0], m_w_in[0], m_w_out[0], tr(m_ffn2_w_gate[0]), tr(m_ffn2_w_up[0]),
             m_ffn2_w_down[0]]
    big_v = [tr(v_ffn1_w_gate[0]), tr(v_ffn1_w_up[0]), v_ffn1_w_down[0], v_w_in[0], v_w_out[0], tr(v_ffn2_w_gate[0]), tr(v_ffn2_w_up[0]),
             v_ffn2_w_down[0]]
    transposed = {"ffn1_w_gate", "ffn1_w_up", "ffn2_w_gate", "ffn2_w_up"}
    names = ["ffn1_w_gate", "ffn1_w_up", "ffn1_w_down", "w_in", "w_out", "ffn2_w_gate", "ffn2_w_up", "ffn2_w_down"]
    sh = dict(zip(names, [w.astype(BF16) for w in big_w]))
    (wg1,) = _run_carry("weights_all_gather", _gather_carry([sh["ffn1_w_gate"]]))

    lane = jnp.arange(LANES)
    inv_freq = ROPE_THETA ** (-(2.0 * (lane % (HEAD_DIM // 2))).astype(F32) / HEAD_DIM)
    ang = positions[0].astype(F32)[:, None] * inv_freq[None, :]
    cos_t = jnp.cos(ang)
    sin_t = jnp.where((lane & (HEAD_DIM // 2)) == 0, -1.0, 1.0)[None, :] * jnp.sin(ang)
    rope = (cos_t, sin_t)

    def pair_gain(g, blocks):
        return jnp.tile(jnp.concatenate([g[0], g[0]])[None, None, :], (blocks, 1, 1))

    n1 = _rmsnorm_fwd("ffn1_norm", xs, norm_ffn1_g, tm)
    a1, (wu1,) = _ffn_gate("ffn1_gate", n1, wg1, tm, carry=_gather_carry([sh["ffn1_w_up"]]))
    (b1, hm1), (wd1,) = _ffn_up_only("ffn1_up", n1, wu1, a1, tm, carry=_gather_carry([sh["ffn1_w_down"]]))
    h1, (win_g,) = _ffn_down("ffn1_down", hm1, wd1, xs, tm, carry=_gather_carry([sh["w_in"]]))
    n_in = win_g.shape[2]
    win_full = jnp.transpose(win_g, (1, 0, 2)).reshape(D, N_DEV * n_in)
    win_main = jnp.concatenate([win_full[:, :F_OFF], win_full[:, F_OFF + H:]], axis=1)
    win_f = jnp.pad(win_full[:, F_OFF:F_OFF + H], ((0, 0), (0, LANES - H)))

    u = _rmsnorm_fwd("mix_norm", h1, norm_mix_g, tm)
    proj, (wout_g,) = _mm_nn("mix_proj", u, win_main, tm, MAIN // 9, carry=_gather_carry([sh["w_out"]]))
    wout = wout_g.reshape(D, D)
    proj_f = _mm_nn("mix_proj_forget", u, win_f, tm, LANES)
    scale = HEAD_DIM ** -0.5
    fox_gains = jnp.concatenate([pair_gain(fox_q_norm_g, HP), pair_gain(fox_k_norm_g, HP)])
    qk_f = _headnorm_fwd_scaled("fox_qk_norm", proj, 0, 2 * HP, fox_gains, tm, scale, HP)
    v_f = proj[:, 2 * Dh:3 * Dh].astype(BF16)
    c_t, sg_t = _forget_fwd("forget_gates", proj_f[:, :H].T, b_forget.reshape(H, 1))
    crow = c_t.reshape(H, nk, 1, tk)
    (o_fox, lse_fa, lse_fb), (wg2,) = _fox_fwd("fox_attention", qk_f, v_f, crow, tq, tk, carry=_gather_carry([sh["ffn2_w_gate"]]))

    swa_q_gains = pair_gain(swa_q_norm_g, HP)
    swa_k_gains = pair_gain(swa_k_norm_g, KVB)
    q_s = _headnorm_fwd("swa_q_norm", proj, 3 * HP, HP, swa_q_gains, tm, scale, rope=rope)
    k_d = _headnorm_fwd("swa_k_norm", proj, 4 * HP, KVB, swa_k_gains, tm, 1.0, rope=rope, dup=True)
    v_s = proj[:, 4 * Dh + KVW:].astype(BF16).reshape(T, H // GQA_GROUP, 1, HEAD_DIM)
    v_d = jnp.broadcast_to(v_s, (T, H // GQA_GROUP, 2, HEAD_DIM)).reshape(T, 2 * KVW)
    sinks3 = swa_sinks.reshape(H, 1, 1)
    (o_swa, lse_sa, lse_sb), (wu2,) = _swa_fwd("swa_attention", q_s, k_d, v_d, sinks3, carry=_gather_carry([sh["ffn2_w_up"]]))

    on = _outnorm_fwd("out_norm", o_fox, o_swa, out_norm_fox_g, out_norm_swa_g, tm)
    h2 = _mm_nn("mix_out", on, wout, tm, min(512, D), resid=h1)

    n2 = _rmsnorm_fwd("ffn2_norm", h2, norm_ffn2_g, tm)
    (a2, b2, hm2), (wd2,) = _ffn_up("ffn2_up", n2, wg2, wu2, tm, carry=_gather_carry([sh["ffn2_w_down"]]))
    y = _ffn_down("ffn2_down", hm2, wd2, h2, tm)
    dy, dyh, sq = _loss_grad("loss_grad", y, target, min(256, T))
    loss = lax.psum(0.5 * sq[0, 0] / D, ("x", "y", "c"))

    J, Fs, _ = wg2.shape
    aspec = pl.BlockSpec((None, tm, Fs), lambda i, j: (j, i, 0))
    wspec = pl.BlockSpec((None, Fs, D), lambda i, j: (j, 0, 0))

    def pair_sums(keys, grads, received):
        return [_pair_add("sum_" + nm, g, r, c_idx) for nm, g, r in zip(keys, grads, received)]

    da2, db2 = _ffn_bwd_mid("ffn2_bwd_mid", dyh, wd2, a2, b2, tm)
    dwd2 = _wgrad_down("ffn2_wgrad_down", hm2, dyh, min(1024, D))
    dwg2, dwu2 = _wgrad_up("ffn2_wgrad_up", n2, da2, db2, min(1024, D))
    dn2, sib2 = _reduce_mm("ffn2_bwd_in", [(da2, aspec, wg2, wspec), (db2, aspec, wu2, wspec)], [], NN, T, D, tm, J,
                           carry=_sibling_carry([dwg2, dwu2, dwd2]))
    dh2, dg_ffn2, dh2b = _rmsnorm_bwd("ffn2_norm_bwd", dn2, h2, norm_ffn2_g, dy, min(256, T), 1.0)
    sum_wg2, sum_wu2, sum_wd2 = pair_sums(names[5:8], [dwg2, dwu2, dwd2], sib2)

    dwout = _wgrad_2d("mix_out_wgrad", on, dh2b, min(512, D), min(1024, D))
    do_fox, dg_of = _outnorm_bwd("out_norm_bwd_fox", dh2b, wout, 0, o_fox, out_norm_fox_g, tm)
    do_swa, dg_os = _outnorm_bwd("out_norm_bwd_swa", dh2b, wout, 1, o_swa, out_norm_swa_g, tm)

    (dq_f, dk_f, dv_f, dc_a, dc_b, dr_a, dr_b), (rc_wg2,) = _fox_bwd(
        "fox_attention_bwd", qk_f, v_f, o_fox, do_fox, crow, lse_fa, lse_fb, tq, tk, carry=_chips_carry([sum_wg2]))
    dqk_f = jnp.concatenate([dq_f, dk_f], axis=1)
    dqk_raw, dg_fox = _headnorm_bwd("fox_qk_norm_bwd", dqk_f, proj, 0, 2 * HP, fox_gains, HP, tm, 1.0)
    dct = jnp.stack([dc_a.reshape(HP, T), dc_b.reshape(HP, T)], axis=1).reshape(H, T)
    drt = jnp.stack([dr_a.reshape(HP, T), dr_b.reshape(HP, T)], axis=1).reshape(H, T)
    dz_t, db_f = _forget_bwd("forget_gates_bwd", dct, drt, sg_t)

    (dq_s, dk_p, dv_p, dsink_a, dsink_b), (rc_wu2,) = _swa_bwd(
        "swa_attention_bwd", q_s, k_d, v_d, sinks3, o_swa, do_swa, lse_sa, lse_sb, carry=_chips_carry([sum_wu2]))
    dqs_raw, dg_sq = _headnorm_bwd("swa_q_norm_bwd", dq_s, proj, 3 * HP, HP, swa_q_gains, HP, tm, 1.0, rope=rope)
    dks_raw, dg_sk = _headnorm_bwd("swa_k_norm_bwd", dk_p, proj, 4 * HP, KVB, swa_k_gains, KVB, tm, 1.0, rope=rope, fold=True)
    dvs_raw, _ = _headnorm_bwd("swa_v_fold", dv_p, None, 0, KVB, None, KVB, tm, 1.0, fold=True, norm=False)

    dproj = jnp.concatenate([dqk_raw, dv_f.astype(BF16), dqs_raw, dks_raw, dvs_raw], axis=1)
    dproj_f = jnp.pad(dz_t.T, ((0, 0), (0, LANES - H))).astype(BF16)
    dwin_main = _wgrad_2d("mix_proj_wgrad", u, dproj, min(1024, D), MAIN // 9)
    dwin_f = _wgrad_2d("mix_proj_forget_wgrad", u, dproj_f, min(1024, D), LANES)
    dwin_full = jnp.concatenate([dwin_main[:, :F_OFF], dwin_f[:, :H], dwin_main[:, F_OFF:]], axis=1)
    dwin_g = jnp.transpose(dwin_full.reshape(D, N_DEV, n_in), (1, 0, 2))
    dwout_g = dwout.reshape(N_DEV, D // N_DEV, D)
    tkb = MAIN // 9
    du, (rc_wd2, *sib_mix) = _reduce_mm(
        "mix_bwd_in",
        [(dproj, pl.BlockSpec((tm, tkb), lambda i, r: (i, r)), win_main, pl.BlockSpec((D, tkb), lambda i, r: (0, r)))],
        [(dproj_f, pl.BlockSpec((tm, LANES), lambda i, r: (i, 0)), win_f, pl.BlockSpec((D, LANES), lambda i, r: (0, 0)))],
        NT, T, D, tm, 9, carry=_join(_chips_carry([sum_wd2]), _sibling_carry([dwin_g, dwout_g])))
    dh1, dg_mix, dh1h = _rmsnorm_bwd("mix_norm_bwd", du, h1, norm_mix_g, dh2, min(256, T), 0.5)
    sum_win, sum_wout = pair_sums(names[3:5], [dwin_g, dwout_g], sib_mix)

    (da1, db1), (rc_win,) = _ffn_bwd_mid("ffn1_bwd_mid", dh1h, wd1, a1, b1, tm, carry=_chips_carry([sum_win]))
    (dwg1, dwu1), (rc_wout,) = _wgrad_up("ffn1_wgrad_up", n1, da1, db1, min(1024, D), carry=_chips_carry([sum_wout]))
    dwd1, sib_gu = _wgrad_down("ffn1_wgrad_down", hm1, dh1h, min(1024, D), carry=_sibling_carry([dwg1, dwu1]))
    sum_wg1, sum_wu1 = pair_sums(names[0:2], [dwg1, dwu1], sib_gu)
    dn1, (rc_wg1, rc_wu1, sib_d) = _reduce_mm(
        "ffn1_bwd_in", [(da1, aspec, wg1, wspec), (db1, aspec, wu1, wspec)], [], NN, T, D, tm, J,
        carry=_join(_chips_carry([sum_wg1, sum_wu1]), _sibling_carry([dwd1])))
    (sum_wd1,) = pair_sums(names[2:3], [dwd1], [sib_d])
    (dx, dg_ffn1), (rc_wd1,) = _rmsnorm_bwd("ffn1_norm_bwd", dn1, xs, norm_ffn1_g, dh1, min(256, T), None,
                                            carry=_chips_carry([sum_wd1]))

    chip_sums = [sum_wg1, sum_wu1, sum_wd1, sum_win, sum_wout, sum_wg2, sum_wu2, sum_wd2]
    from_chips = [rc_wg1, rc_wu1, rc_wd1, rc_win, rc_wout, rc_wg2, rc_wu2, rc_wd2]
    big_out = [_adam_shard("adam_" + nm, s, r, w, m, v, chip_idx)
               for nm, s, r, w, m, v in zip(names, chip_sums, from_chips, big_w, big_m, big_v)]

    dsinks = jnp.stack([dsink_a.reshape(HP), dsink_b.reshape(HP)], axis=1).reshape(H)
    small_g = [dg_ffn1, dg_mix, dg_ffn2, dg_of, dg_os, db_f, dg_fox[0, 0, :HEAD_DIM], dg_fox[1, 0, :HEAD_DIM],
               dg_sq[0, 0, :HEAD_DIM], dg_sk[0, 0, :HEAD_DIM], dsinks]
    small_w = [norm_ffn1_g, norm_mix_g, norm_ffn2_g, out_norm_fox_g, out_norm_swa_g, b_forget, fox_q_norm_g, fox_k_norm_g,
               swa_q_norm_g, swa_k_norm_g, swa_sinks]
    small_m = [m_norm_ffn1_g, m_norm_mix_g, m_norm_ffn2_g, m_out_norm_fox_g, m_out_norm_swa_g, m_b_forget, m_fox_q_norm_g,
               m_fox_k_norm_g, m_swa_q_norm_g, m_swa_k_norm_g, m_swa_sinks]
    small_v = [v_norm_ffn1_g, v_norm_mix_g, v_norm_ffn2_g, v_out_norm_fox_g, v_out_norm_swa_g, v_b_forget, v_fox_q_norm_g,
               v_fox_k_norm_g, v_swa_q_norm_g, v_swa_k_norm_g, v_swa_sinks]
    gathered = _gather_small(_pack_small(small_g, D))
    small_out = _adam_small("adam_small", gathered, _pack_small(small_w, D), _pack_small(small_m, D), _pack_small(small_v, D))
    small_out = [_unpack_small(p, D, H) for p in small_out]

    order = ["norm_ffn1_g", "ffn1_w_gate", "ffn1_w_up", "ffn1_w_down", "norm_mix_g", "w_in", "b_forget", "fox_q_norm_g", "fox_k_norm_g",
             "swa_q_norm_g", "swa_k_norm_g", "swa_sinks", "out_norm_fox_g", "out_norm_swa_g", "w_out", "norm_ffn2_g",
             "ffn2_w_gate", "ffn2_w_up", "ffn2_w_down"]
    small_names = ["norm_ffn1_g", "norm_mix_g", "norm_ffn2_g", "out_norm_fox_g", "out_norm_swa_g", "b_forget", "fox_q_norm_g",
                   "fox_k_norm_g", "swa_q_norm_g", "swa_k_norm_g", "swa_sinks"]
    result = [loss, dx[None]]
    for kind in range(4):
        for nm in order:
            if nm in names:
                leaf = big_out[names.index(nm)][kind]
                result.append((tr(leaf) if nm in transposed else leaf)[None])
            else:
                result.append(small_out[kind][small_names.index(nm)])
    return tuple(result)


def _headnorm_fwd_scaled(name, proj, col_off, ncb, gains, tm, scale, n_scaled):
    T = proj.shape[0]

    def body(x_ref, g_ref, o_ref):
        xv = x_ref[...]
        lo = _lane_lo(xv.shape)
        y = xv * _head_rstd(xv, lo) * g_ref[...]
        y = y * jnp.where(pl.program_id(0) < n_scaled, scale, 1.0)
        o_ref[...] = y.astype(BF16)

    return pl.pallas_call(
        body, out_shape=jax.ShapeDtypeStruct((T, ncb * LANES), BF16), grid=(ncb, T // tm),
        in_specs=[pl.BlockSpec((tm, LANES), lambda c, i: (i, col_off + c)), pl.BlockSpec((None, 1, LANES), lambda c, i: (c, 0, 0))],
        out_specs=pl.BlockSpec((tm, LANES), lambda c, i: (i, c)), name=name, compiler_params=_params(2))(proj, gains)
```

```python
import functools

import jax
import jax.numpy as jnp
from jax import lax
from jax.experimental import pallas as pl
from jax.experimental.pallas import tpu as pltpu

F32 = jnp.float32
BF16 = jnp.bfloat16

HEAD_DIM = 64
LANES = 128
WINDOW = 128
GQA_GROUP = 4
EPS = 1e-6
ROPE_THETA = 10000.0
ADAM_LR = 0.001
ADAM_B1 = 0.9
ADAM_B2 = 0.999
ADAM_EPS = 1e-08
ADAM_WD = 0.01
ADAM_STEP = 10
N_DEV = 8
NEG = -1e30
VMEM_LIMIT_V7X = 48 * 1024 * 1024
ROW_TILE_CAP = 512
MESH = pl.DeviceIdType.MESH

NN = (((1,), (0,)), ((), ()))
NT = (((1,), (1,)), ((), ()))
TN = (((0,), (0,)), ((), ()))


def _dot(a, b, dims):
    return lax.dot_general(a, b, dims, preferred_element_type=F32)


def _params(n_axes):
    return pltpu.CompilerParams(dimension_semantics=("arbitrary",) * n_axes, vmem_limit_bytes=VMEM_LIMIT_V7X)


def _row_tile(rows, cap=ROW_TILE_CAP):
    best = None
    for t in range(16, min(rows, cap) + 1, 16):
        if rows % t == 0:
            best = t
    return best or rows


def _lane_lo(shape):
    return lax.broadcasted_iota(jnp.int32, shape, len(shape) - 1) < HEAD_DIM


def _keep(sel, x):
    return jnp.where(sel, x.astype(F32), 0.0).astype(BF16)


_HBM = pl.BlockSpec(memory_space=pltpu.HBM)


class _Carry:
    def __init__(self, inputs, out_shapes, scratch, start, finish, middle=None):
        self.inputs, self.out_shapes, self.scratch = list(inputs), list(out_shapes), list(scratch)
        self.start, self.finish, self.middle = start, finish, middle or (lambda ins, outs, scr: None)


def _join(*carries):
    def hook(which):
        def run(ins, outs, scr):
            i = o = s = 0
            for c in carries:
                ni, no, ns = len(c.inputs), len(c.out_shapes), len(c.scratch)
                getattr(c, which)(ins[i:i + ni], outs[o:o + no], scr[s:s + ns])
                i, o, s = i + ni, o + no, s + ns
        return run

    return _Carry([a for c in carries for a in c.inputs], [a for c in carries for a in c.out_shapes],
                  [a for c in carries for a in c.scratch], hook("start"), hook("finish"), hook("middle"))


def _call(body, *, name, grid, in_specs, out_specs, out_shape, args, scratch_shapes=(), carry=None):
    params = _params(len(grid))
    if carry is None:
        return pl.pallas_call(body, out_shape=out_shape, grid=grid, in_specs=list(in_specs), out_specs=out_specs,
                              scratch_shapes=list(scratch_shapes), name=name, compiler_params=params)(*args)
    single = not isinstance(out_shape, (tuple, list))
    shapes = (out_shape,) if single else tuple(out_shape)
    specs = (out_specs,) if single else tuple(out_specs)
    n_in, n_out, n_scr = len(args), len(shapes), len(scratch_shapes)
    c_in, c_out = len(carry.inputs), len(carry.out_shapes)

    def wrapped(*refs):
        ins, c_ins = refs[:n_in], refs[n_in:n_in + c_in]
        o0 = n_in + c_in
        outs, c_outs = refs[o0:o0 + n_out], refs[o0 + n_out:o0 + n_out + c_out]
        s0 = o0 + n_out + c_out
        scr, c_scr = refs[s0:s0 + n_scr], refs[s0 + n_scr:]
        step, total = pl.program_id(0), grid[0]
        for ax in range(1, len(grid)):
            step, total = step * grid[ax] + pl.program_id(ax), total * grid[ax]

        @pl.when(step == 0)
        def _():
            carry.start(c_ins, c_outs, c_scr)

        @pl.when(step == total // 2)
        def _():
            carry.middle(c_ins, c_outs, c_scr)

        body(*ins, *outs, *scr)

        @pl.when(step == total - 1)
        def _():
            carry.finish(c_ins, c_outs, c_scr)

    res = pl.pallas_call(
        wrapped, out_shape=shapes + tuple(carry.out_shapes), grid=grid, in_specs=list(in_specs) + [_HBM] * c_in,
        out_specs=specs + (_HBM,) * c_out, scratch_shapes=list(scratch_shapes) + carry.scratch, name=name,
        compiler_params=params)(*args, *carry.inputs)
    main = res[:n_out]
    return (main[0] if single else tuple(main)), tuple(res[n_out:])


def _rms_bwd(dn, x, g):
    r = lax.rsqrt(jnp.mean(x * x, axis=-1, keepdims=True) + EPS)
    xh = x * r
    dxh = dn * g
    dx = r * (dxh - xh * jnp.mean(dxh * xh, axis=-1, keepdims=True))
    return dx, jnp.sum(dn * xh, axis=0, keepdims=True)


def _rmsnorm_fwd(name, x, g, tm):
    T, D = x.shape

    def body(x_ref, g_ref, o_ref):
        xf = x_ref[...]
        r = lax.rsqrt(jnp.mean(xf * xf, axis=-1, keepdims=True) + EPS)
        o_ref[...] = (xf * r * g_ref[...]).astype(BF16)

    return pl.pallas_call(
        body, out_shape=jax.ShapeDtypeStruct((T, D), BF16), grid=(T // tm,),
        in_specs=[pl.BlockSpec((tm, D), lambda i: (i, 0)), pl.BlockSpec((1, D), lambda i: (0, 0))],
        out_specs=pl.BlockSpec((tm, D), lambda i: (i, 0)), name=name, compiler_params=_params(1))(x, g)


def _outnorm_fwd(name, o_fox, o_swa, g_fox, g_swa, tm):
    T, Dh = o_fox.shape

    def body(a_ref, b_ref, ga_ref, gb_ref, o_ref):
        for ref, g_ref, lo in ((a_ref, ga_ref, 0), (b_ref, gb_ref, Dh)):
            xf = ref[...]
            r = lax.rsqrt(jnp.mean(xf * xf, axis=-1, keepdims=True) + EPS)
            o_ref[:, lo:lo + Dh] = (xf * r * g_ref[...]).astype(BF16)

    row = pl.BlockSpec((tm, Dh), lambda i: (i, 0))
    gain = pl.BlockSpec((1, Dh), lambda i: (0, 0))
    return pl.pallas_call(
        body, out_shape=jax.ShapeDtypeStruct((T, 2 * Dh), BF16), grid=(T // tm,),
        in_specs=[row, row, gain, gain], out_specs=pl.BlockSpec((tm, 2 * Dh), lambda i: (i, 0)),
        name=name, compiler_params=_params(1))(o_fox, o_swa, g_fox, g_swa)


def _outnorm_bwd(name, dhb, wout, half, o, g, tm):
    T, D = dhb.shape
    Dh = o.shape[1]

    def body(a_ref, w_ref, o_ref, g_ref, do_ref, dg_ref):
        don = _dot(a_ref[...], w_ref[...], NT)
        dx, dg = _rms_bwd(don, o_ref[...], g_ref[...])
        do_ref[...] = dx.astype(BF16)

        @pl.when(pl.program_id(0) == 0)
        def _():
            dg_ref[...] = dg

        @pl.when(pl.program_id(0) > 0)
        def _():
            dg_ref[...] += dg

    return pl.pallas_call(
        body, out_shape=(jax.ShapeDtypeStruct((T, Dh), BF16), jax.ShapeDtypeStruct((1, Dh), F32)), grid=(T // tm,),
        in_specs=[pl.BlockSpec((tm, D), lambda i: (i, 0)), pl.BlockSpec((Dh, D), lambda i: (half, 0)),
                  pl.BlockSpec((tm, Dh), lambda i: (i, 0)), pl.BlockSpec((1, Dh), lambda i: (0, 0))],
        out_specs=(pl.BlockSpec((tm, Dh), lambda i: (i, 0)), pl.BlockSpec((1, Dh), lambda i: (0, 0))),
        name=name, compiler_params=_params(1))(dhb, wout, o, g)


def _mm(name, a, b, tm, tn, dims=NN, resid=None, carry=None):
    M, K = a.shape
    transposed = dims == NT
    N = b.shape[0] if transposed else b.shape[1]

    def body(*refs):
        if resid is None:
            a_ref, b_ref, o_ref = refs
            o_ref[...] = _dot(a_ref[...], b_ref[...], dims)
        else:
            a_ref, b_ref, r_ref, o_ref = refs
            o_ref[...] = r_ref[...] + _dot(a_ref[...], b_ref[...], dims)

    ospec = pl.BlockSpec((tm, tn), lambda n, i: (i, n))
    bspec = pl.BlockSpec((tn, K), lambda n, i: (n, 0)) if transposed else pl.BlockSpec((K, tn), lambda n, i: (0, n))
    in_specs = [pl.BlockSpec((tm, K), lambda n, i: (i, 0)), bspec]
    args = [a, b]
    if resid is not None:
        in_specs.append(ospec)
        args.append(resid)
    return _call(body, name=name, grid=(N // tn, M // tm), in_specs=in_specs, out_specs=ospec,
                 out_shape=jax.ShapeDtypeStruct((M, N), F32), args=args, carry=carry)


def _wgrad_2d(name, a, b, tmm, tn):
    T, M = a.shape
    N = b.shape[1]

    def body(a_ref, b_ref, o_ref):
        o_ref[...] = _dot(a_ref[...], b_ref[...], TN).astype(BF16)

    return pl.pallas_call(
        body, out_shape=jax.ShapeDtypeStruct((M, N), BF16), grid=(M // tmm, N // tn),
        in_specs=[pl.BlockSpec((T, tmm), lambda m, n: (0, m)), pl.BlockSpec((T, tn), lambda m, n: (0, n))],
        out_specs=pl.BlockSpec((tmm, tn), lambda m, n: (m, n)), name=name, compiler_params=_params(2))(a, b)


def _wgrad_down(name, hm, df, tn, carry=None):
    J, T, Fs = hm.shape
    D = df.shape[1]

    def body(a_ref, b_ref, o_ref):
        o_ref[...] = _dot(a_ref[...], b_ref[...], TN).astype(BF16)

    return _call(
        body, name=name, grid=(J, D // tn), out_shape=jax.ShapeDtypeStruct((J, Fs, D), BF16),
        in_specs=[pl.BlockSpec((None, T, Fs), lambda j, n: (j, 0, 0)), pl.BlockSpec((T, tn), lambda j, n: (0, n))],
        out_specs=pl.BlockSpec((None, Fs, tn), lambda j, n: (j, 0, n)), args=[hm, df], carry=carry)


def _wgrad_up(name, n, da, db, tn, carry=None):
    T, D = n.shape
    J, _, Fs = da.shape

    def body(n_ref, da_ref, db_ref, og_ref, ou_ref):
        nv = n_ref[...]
        og_ref[...] = _dot(da_ref[...], nv, TN).astype(BF16)
        ou_ref[...] = _dot(db_ref[...], nv, TN).astype(BF16)

    act = pl.BlockSpec((None, T, Fs), lambda j, m: (j, 0, 0))
    out = pl.BlockSpec((None, Fs, tn), lambda j, m: (j, 0, m))
    shape = jax.ShapeDtypeStruct((J, Fs, D), BF16)
    return _call(
        body, name=name, grid=(J, D // tn), out_shape=(shape, shape),
        in_specs=[pl.BlockSpec((T, tn), lambda j, m: (0, m)), act, act], out_specs=(out, out),
        args=[n, da, db], carry=carry)


def _reduce_mm(name, pairs, once, dims, T, D, tm, steps, carry=None):
    n_pairs = len(pairs)
    n_once = len(once)

    def body(*refs):
        pr = refs[:2 * n_pairs]
        on = refs[2 * n_pairs:2 * (n_pairs + n_once)]
        o_ref, acc = refs[-2:]
        r = pl.program_id(1)
        part = _dot(pr[0][...], pr[1][...], dims)
        for p in range(1, n_pairs):
            part = part + _dot(pr[2 * p][...], pr[2 * p + 1][...], dims)

        @pl.when(r == 0)
        def _():
            acc[...] = part

        @pl.when(r > 0)
        def _():
            acc[...] += part

        @pl.when(r == steps - 1)
        def _():
            dn = acc[...]
            for p in range(n_once):
                dn = dn + _dot(on[2 * p][...], on[2 * p + 1][...], dims)
            o_ref[...] = dn

    in_specs, args = [], []
    for a, a_spec, w, w_spec in list(pairs) + list(once):
        in_specs += [a_spec, w_spec]
        args += [a, w]
    row = pl.BlockSpec((tm, D), lambda i, r: (i, 0))
    return _call(body, name=name, grid=(T // tm, steps), in_specs=in_specs, out_specs=row, out_shape=jax.ShapeDtypeStruct((T, D), F32),
                 args=args, scratch_shapes=[pltpu.VMEM((tm, D), F32)], carry=carry)


def _rmsnorm_bwd(name, dn, x, g, dh, tm, bf16_scale, carry=None):
    T, D = x.shape
    emit_bf16 = bf16_scale is not None

    def body(dn_ref, x_ref, g_ref, dh_ref, *outs):
        dxn, dg = _rms_bwd(dn_ref[...], x_ref[...], g_ref[...])
        dx = dh_ref[...] + dxn
        outs[0][...] = dx
        if emit_bf16:
            outs[2][...] = (bf16_scale * dx).astype(BF16)

        @pl.when(pl.program_id(0) == 0)
        def _():
            outs[1][...] = dg

        @pl.when(pl.program_id(0) > 0)
        def _():
            outs[1][...] += dg

    row = pl.BlockSpec((tm, D), lambda i: (i, 0))
    gain = pl.BlockSpec((1, D), lambda i: (0, 0))
    out_shape = [jax.ShapeDtypeStruct((T, D), F32), jax.ShapeDtypeStruct((1, D), F32)]
    out_specs = [row, gain]
    if emit_bf16:
        out_shape.append(jax.ShapeDtypeStruct((T, D), BF16))
        out_specs.append(row)
    return _call(body, name=name, grid=(T // tm,), in_specs=[row, row, gain, row], out_specs=tuple(out_specs),
                 out_shape=tuple(out_shape), args=[dn, x, g, dh], carry=carry)


def _loss_grad(name, y, target, tm):
    T, D = y.shape

    def body(y_ref, t_ref, dy_ref, dyh_ref, sq_ref):
        diff = y_ref[...] - t_ref[...]
        sq = jnp.sum(jnp.sum(diff * diff, axis=1, keepdims=True), axis=0, keepdims=True)
        dy = diff * (1.0 / D)
        dy_ref[...] = dy
        dyh_ref[...] = (0.5 * dy).astype(BF16)

        @pl.when(pl.program_id(0) == 0)
        def _():
            sq_ref[...] = sq

        @pl.when(pl.program_id(0) > 0)
        def _():
            sq_ref[...] += sq

    row = pl.BlockSpec((tm, D), lambda i: (i, 0))
    return pl.pallas_call(
        body, out_shape=(jax.ShapeDtypeStruct((T, D), F32), jax.ShapeDtypeStruct((T, D), BF16), jax.ShapeDtypeStruct((1, 1), F32)),
        grid=(T // tm,), in_specs=[row, row], out_specs=(row, row, pl.BlockSpec((1, 1), lambda i: (0, 0))),
        name=name, compiler_params=_params(1))(y, target)


def _ffn_up(name, n, wg, wu, tm, carry=None):
    T, D = n.shape
    J, Fs, _ = wg.shape

    def body(n_ref, wg_ref, wu_ref, a_ref, b_ref, h_ref):
        xv = n_ref[...]
        a = _dot(xv, wg_ref[...], NT)
        b = _dot(xv, wu_ref[...], NT)
        a_ref[...] = a.astype(BF16)
        b_ref[...] = b.astype(BF16)
        h_ref[...] = (a * jax.nn.sigmoid(a) * b).astype(BF16)

    act = jax.ShapeDtypeStruct((J, T, Fs), BF16)
    wspec = pl.BlockSpec((None, Fs, D), lambda j, i: (j, 0, 0))
    aspec = pl.BlockSpec((None, tm, Fs), lambda j, i: (j, i, 0))
    return _call(
        body, name=name, grid=(J, T // tm), out_shape=(act, act, act),
        in_specs=[pl.BlockSpec((tm, D), lambda j, i: (i, 0)), wspec, wspec], out_specs=(aspec, aspec, aspec),
        args=[n, wg, wu], carry=carry)


def _ffn_gate(name, n, wg, tm, carry=None):
    T, D = n.shape
    J, Fs, _ = wg.shape

    def body(n_ref, wg_ref, a_ref):
        a_ref[...] = _dot(n_ref[...], wg_ref[...], NT).astype(BF16)

    aspec = pl.BlockSpec((None, tm, Fs), lambda j, i: (j, i, 0))
    return _call(
        body, name=name, grid=(J, T // tm), out_shape=jax.ShapeDtypeStruct((J, T, Fs), BF16),
        in_specs=[pl.BlockSpec((tm, D), lambda j, i: (i, 0)), pl.BlockSpec((None, Fs, D), lambda j, i: (j, 0, 0))],
        out_specs=aspec, args=[n, wg], carry=carry)


def _ffn_up_only(name, n, wu, a, tm, carry=None):
    T, D = n.shape
    J, Fs, _ = wu.shape

    def body(n_ref, wu_ref, a_ref, b_ref, h_ref):
        b = _dot(n_ref[...], wu_ref[...], NT)
        a = a_ref[...].astype(F32)
        b_ref[...] = b.astype(BF16)
        h_ref[...] = (a * jax.nn.sigmoid(a) * b).astype(BF16)

    act = jax.ShapeDtypeStruct((J, T, Fs), BF16)
    aspec = pl.BlockSpec((None, tm, Fs), lambda j, i: (j, i, 0))
    return _call(
        body, name=name, grid=(J, T // tm), out_shape=(act, act),
        in_specs=[pl.BlockSpec((tm, D), lambda j, i: (i, 0)), pl.BlockSpec((None, Fs, D), lambda j, i: (j, 0, 0)), aspec],
        out_specs=(aspec, aspec), args=[n, wu, a], carry=carry)


def _ffn_down(name, hm, wd, resid, tm, carry=None):
    J, T, Fs = hm.shape
    D = wd.shape[2]

    def body(h_ref, w_ref, r_ref, o_ref, acc):
        j = pl.program_id(1)
        part = _dot(h_ref[...], w_ref[...], NN)

        @pl.when(j == 0)
        def _():
            acc[...] = part

        @pl.when(j > 0)
        def _():
            acc[...] += part

        @pl.when(j == J - 1)
        def _():
            o_ref[...] = r_ref[...] + 0.5 * acc[...]

    row = pl.BlockSpec((tm, D), lambda i, j: (i, 0))
    return _call(
        body, name=name, grid=(T // tm, J), out_shape=jax.ShapeDtypeStruct((T, D), F32),
        in_specs=[pl.BlockSpec((None, tm, Fs), lambda i, j: (j, i, 0)), pl.BlockSpec((None, Fs, D), lambda i, j: (j, 0, 0)), row],
        out_specs=row, scratch_shapes=[pltpu.VMEM((tm, D), F32)], args=[hm, wd, resid], carry=carry)


def _ffn_bwd_mid(name, dfh, wd, a, b, tm, carry=None):
    T, D = dfh.shape
    J, Fs, _ = wd.shape

    def body(df_ref, w_ref, a_ref, b_ref, da_ref, db_ref):
        dhm = _dot(df_ref[...], w_ref[...], NT)
        av = a_ref[...].astype(F32)
        bv = b_ref[...].astype(F32)
        sg = jax.nn.sigmoid(av)
        da_ref[...] = (dhm * bv * (sg * (1.0 + av * (1.0 - sg)))).astype(BF16)
        db_ref[...] = (dhm * (av * sg)).astype(BF16)

    act = jax.ShapeDtypeStruct((J, T, Fs), BF16)
    aspec = pl.BlockSpec((None, tm, Fs), lambda j, i: (j, i, 0))
    return _call(
        body, name=name, grid=(J, T // tm), out_shape=(act, act),
        in_specs=[pl.BlockSpec((tm, D), lambda j, i: (i, 0)), pl.BlockSpec((None, Fs, D), lambda j, i: (j, 0, 0)), aspec, aspec],
        out_specs=(aspec, aspec), args=[dfh, wd, a, b], carry=carry)


def _rot_half(y, lane):
    first = (lane & (HEAD_DIM // 2)) == 0
    return jnp.where(first, pltpu.roll(y, LANES - HEAD_DIM // 2, 1), pltpu.roll(y, HEAD_DIM // 2, 1))


def _head_rstd(x, lo):
    sq = x * x
    ss_a = jnp.sum(jnp.where(lo, sq, 0.0), axis=-1, keepdims=True)
    ss_b = jnp.sum(jnp.where(lo, 0.0, sq), axis=-1, keepdims=True)
    return lax.rsqrt(jnp.where(lo, ss_a, ss_b) * (1.0 / HEAD_DIM) + EPS)


def _headnorm_fwd(name, proj, col_off, ncb, gains, tm, scale, rope=None, dup=False):
    T = proj.shape[0]
    with_rope = rope is not None
    width = 2 * LANES if dup else LANES

    def body(*refs):
        if with_rope:
            x_ref, g_ref, cos_ref, sin_ref, o_ref = refs
        else:
            x_ref, g_ref, o_ref = refs
        xv = x_ref[...]
        lane = lax.broadcasted_iota(jnp.int32, xv.shape, 1)
        lo = lane < HEAD_DIM
        y = xv * _head_rstd(xv, lo) * g_ref[...]
        if with_rope:
            y = y * cos_ref[...] + _rot_half(y, lane) * sin_ref[...]
        y = y * scale
        if dup:
            sw = pltpu.roll(y, HEAD_DIM, 1)
            o_ref[:, :LANES] = jnp.where(lo, y, sw).astype(BF16)
            o_ref[:, LANES:] = jnp.where(lo, sw, y).astype(BF16)
        else:
            o_ref[...] = y.astype(BF16)

    in_specs = [pl.BlockSpec((tm, LANES), lambda c, i: (i, col_off + c)), pl.BlockSpec((None, 1, LANES), lambda c, i: (c, 0, 0))]
    args = [proj, gains]
    if with_rope:
        tab = pl.BlockSpec((tm, LANES), lambda c, i: (i, 0))
        in_specs += [tab, tab]
        args += list(rope)
    return pl.pallas_call(
        body, out_shape=jax.ShapeDtypeStruct((T, ncb * width), BF16), grid=(ncb, T // tm),
        in_specs=in_specs, out_specs=pl.BlockSpec((tm, width), lambda c, i: (i, c)),
        name=name, compiler_params=_params(2))(*args)


def _headnorm_bwd(name, dy, proj, col_off, ncb, gains, group, tm, scale, rope=None, fold=False, norm=True):
    T = dy.shape[0]
    with_rope = rope is not None
    n_groups = ncb // group
    dy_width = 4 * LANES if fold else LANES

    def body(*refs):
        refs = list(refs)
        dy_ref = refs.pop(0)
        x_ref = refs.pop(0) if norm else None
        g_ref = refs.pop(0) if norm else None
        cos_ref = refs.pop(0) if with_rope else None
        sin_ref = refs.pop(0) if with_rope else None
        dx_ref = refs.pop(0)
        dg_ref = refs.pop(0) if norm else None
        c = pl.program_id(0)
        i = pl.program_id(1)
        d = dy_ref[...]
        lane = lax.broadcasted_iota(jnp.int32, (d.shape[0], LANES), 1)
        lo = lane < HEAD_DIM
        if fold:
            t0 = d[:, 0:LANES] + d[:, LANES:2 * LANES]
            t1 = d[:, 2 * LANES:3 * LANES] + d[:, 3 * LANES:4 * LANES]
            d = jnp.where(lo, t0 + pltpu.roll(t0, HEAD_DIM, 1), t1 + pltpu.roll(t1, HEAD_DIM, 1))
        d = d * scale
        if with_rope:
            d = d * cos_ref[...] + _rot_half(d * sin_ref[...], lane)
        if not norm:
            dx_ref[...] = d.astype(BF16)
            return
        xv = x_ref[...]
        gv = g_ref[...]
        r = _head_rstd(xv, lo)
        xh = xv * r
        dxh = d * gv
        pr = dxh * xh
        m_a = jnp.sum(jnp.where(lo, pr, 0.0), axis=-1, keepdims=True)
        m_b = jnp.sum(jnp.where(lo, 0.0, pr), axis=-1, keepdims=True)
        mean = jnp.where(lo, m_a, m_b) * (1.0 / HEAD_DIM)
        dx_ref[...] = (r * (dxh - xh * mean)).astype(BF16)
        dgp = jnp.sum(d * xh, axis=0, keepdims=True)
        dgp = dgp + pltpu.roll(dgp, HEAD_DIM, 1)
        first = jnp.logical_and(c % group == 0, i == 0)

        @pl.when(first)
        def _():
            dg_ref[...] = dgp

        @pl.when(jnp.logical_not(first))
        def _():
            dg_ref[...] += dgp

    in_specs = [pl.BlockSpec((tm, dy_width), lambda c, i: (i, c))]
    args = [dy]
    if norm:
        in_specs += [pl.BlockSpec((tm, LANES), lambda c, i: (i, col_off + c)), pl.BlockSpec((None, 1, LANES), lambda c, i: (c, 0, 0))]
        args += [proj, gains]
    if with_rope:
        tab = pl.BlockSpec((tm, LANES), lambda c, i: (i, 0))
        in_specs += [tab, tab]
        args += list(rope)
    out_shape = [jax.ShapeDtypeStruct((T, ncb * LANES), BF16)]
    out_specs = [pl.BlockSpec((tm, LANES), lambda c, i: (i, c))]
    if norm:
        out_shape.append(jax.ShapeDtypeStruct((n_groups, 1, LANES), F32))
        out_specs.append(pl.BlockSpec((None, 1, LANES), lambda c, i: (c // group, 0, 0)))
    res = pl.pallas_call(
        body, out_shape=tuple(out_shape), grid=(ncb, T // tm), in_specs=in_specs, out_specs=tuple(out_specs),
        name=name, compiler_params=_params(2))(*args)
    return res if norm else (res[0], None)


def _dot_exact(x, tri):
    hi = x.astype(BF16)
    r1 = x - hi.astype(F32)
    mid = r1.astype(BF16)
    lo = (r1 - mid.astype(F32)).astype(BF16)
    return _dot(hi, tri, NN) + _dot(mid, tri, NN) + _dot(lo, tri, NN)


def _forget_fwd(name, zt, bias):
    H, T = zt.shape
    blk = min(256, T)

    def body(z_ref, b_ref, c_ref, s_ref):
        z = z_ref[...] + b_ref[...]
        s_ref[...] = jax.nn.sigmoid(-z)
        lf = jnp.minimum(z, 0.0) - jnp.log(1.0 + jnp.exp(-jnp.abs(z)))
        tri = (lax.broadcasted_iota(jnp.int32, (blk, blk), 0) <= lax.broadcasted_iota(jnp.int32, (blk, blk), 1)).astype(BF16)
        carry = jnp.zeros((H, 1), F32)
        for bi in range(T // blk):
            xb = lf[:, bi * blk:(bi + 1) * blk]
            c_ref[:, bi * blk:(bi + 1) * blk] = _dot_exact(xb, tri) + carry
            carry = carry + jnp.sum(xb, axis=-1, keepdims=True)

    shape = jax.ShapeDtypeStruct((H, T), F32)
    full = pl.BlockSpec((H, T), lambda i: (0, 0))
    return pl.pallas_call(
        body, out_shape=(shape, shape), grid=(1,), in_specs=[full, pl.BlockSpec((H, 1), lambda i: (0, 0))],
        out_specs=(full, full), name=name, compiler_params=_params(1))(zt, bias)


def _forget_bwd(name, dct, drt, sgt):
    H, T = dct.shape
    blk = min(256, T)

    def body(dc_ref, dr_ref, s_ref, dz_ref, db_ref):
        dc = dc_ref[...] + dr_ref[...]
        tri = (lax.broadcasted_iota(jnp.int32, (blk, blk), 0) >= lax.broadcasted_iota(jnp.int32, (blk, blk), 1)).astype(BF16)
        carry = jnp.zeros((H, 1), F32)
        db = jnp.zeros((H, 1), F32)
        for bi in reversed(range(T // blk)):
            xb = dc[:, bi * blk:(bi + 1) * blk]
            dz = (_dot_exact(xb, tri) + carry) * s_ref[:, bi * blk:(bi + 1) * blk]
            dz_ref[:, bi * blk:(bi + 1) * blk] = dz
            db = db + jnp.sum(dz, axis=-1, keepdims=True)
            carry = carry + jnp.sum(xb, axis=-1, keepdims=True)
        db_ref[...] = db

    full = pl.BlockSpec((H, T), lambda i: (0, 0))
    return pl.pallas_call(
        body, out_shape=(jax.ShapeDtypeStruct((H, T), F32), jax.ShapeDtypeStruct((H, 1), F32)), grid=(1,),
        in_specs=[full, full, full], out_specs=(full, pl.BlockSpec((H, 1), lambda i: (0, 0))),
        name=name, compiler_params=_params(1))(dct, drt, sgt)


STRIP = 256


def _fox_fwd(name, qk, v, crow, tq, tk, carry=None):
    T, Dh = v.shape
    HP = Dh // LANES
    nk = T // tk
    assert tk % tq == 0 and tq % STRIP == 0
    n_strips = tq // STRIP

    def body(q_ref, k_ref, v_ref, ra_ref, rb_ref, o_ref, la_ref, lb_ref, s_ref, p_ref, m_ref, l_ref, acc_ref):
        i = pl.program_id(1)
        q2 = q_ref[...]
        lo = _lane_lo((tq, LANES))
        qms = (_keep(lo, q2), _keep(jnp.logical_not(lo), q2))
        r_refs = (ra_ref, rb_ref)
        m_ref[...] = jnp.full(m_ref.shape, NEG, F32)
        l_ref[...] = jnp.zeros(l_ref.shape, F32)
        acc_ref[...] = jnp.zeros(acc_ref.shape, F32)
        rel = lax.broadcasted_iota(jnp.int32, (STRIP, tk), 0) - lax.broadcasted_iota(jnp.int32, (STRIP, tk), 1)

        def chunk(kc, masked):
            start = pl.multiple_of(kc * tk, tk)
            kb = k_ref[pl.ds(start, tk), :]
            vb = v_ref[pl.ds(start, tk), :]
            for h in range(2):
                s_ref[h] = _dot(qms[h], kb, NT)
            for h in range(2):
                cs = r_refs[h][kc]
                for st in range(n_strips):
                    rows = pl.ds(st * STRIP, STRIP)
                    s = s_ref[h, rows, :] - cs
                    if masked:
                        s = jnp.where(rel >= start - (i * tq + st * STRIP), s, NEG)
                    m_old = m_ref[h, rows, :]
                    mn = jnp.maximum(m_old, jnp.max(s, axis=-1, keepdims=True))
                    p = jnp.exp(s - mn)
                    alpha = jnp.exp(m_old - mn)
                    l_ref[h, rows, :] = alpha * l_ref[h, rows, :] + jnp.sum(p, axis=-1, keepdims=True)
                    m_ref[h, rows, :] = mn
                    p_ref[h, rows, :] = p.astype(BF16)
                    acc_ref[h, rows, :] = acc_ref[h, rows, :] * alpha
            for h in range(2):
                acc_ref[h] += _dot(p_ref[h], vb, NN)

        n_full = (i * tq) // tk

        def full_chunk(kc, _):
            chunk(kc, False)
            return 0

        lax.fori_loop(0, n_full, full_chunk, 0)
        chunk(n_full, True)
        o_ref[...] = jnp.where(lo, acc_ref[0] / l_ref[0], acc_ref[1] / l_ref[1])
        la_ref[...] = m_ref[0] + jnp.log(l_ref[0])
        lb_ref[...] = m_ref[1] + jnp.log(l_ref[1])

    row = lambda off: pl.BlockSpec((None, nk, 1, tk), lambda h, i: (2 * h + off, 0, 0, 0))
    lse = jax.ShapeDtypeStruct((HP, T, 1), F32)
    lspec = pl.BlockSpec((None, tq, 1), lambda h, i: (h, i, 0))
    scratch = [pltpu.VMEM((2, tq, tk), F32), pltpu.VMEM((2, tq, tk), BF16), pltpu.VMEM((2, tq, 1), F32),
               pltpu.VMEM((2, tq, 1), F32), pltpu.VMEM((2, tq, LANES), F32)]
    return _call(
        body, name=name, grid=(HP, T // tq), out_shape=(jax.ShapeDtypeStruct((T, Dh), F32), lse, lse),
        in_specs=[pl.BlockSpec((tq, LANES), lambda h, i: (i, h)), pl.BlockSpec((T, LANES), lambda h, i: (0, HP + h)),
                  pl.BlockSpec((T, LANES), lambda h, i: (0, h)), row(0), row(1)],
        out_specs=(pl.BlockSpec((tq, LANES), lambda h, i: (i, h)), lspec, lspec),
        args=[qk, qk, v, crow, crow], scratch_shapes=scratch, carry=carry)


def _fox_bwd(name, qk, v, o, do, crow, lse_a, lse_b, tq, tk, carry=None):
    T, Dh = v.shape
    HP = Dh // LANES
    nk = T // tk
    scale = HEAD_DIM ** -0.5
    assert tk % tq == 0 and tq % STRIP == 0
    n_strips = tq // STRIP

    def body(q_ref, k_ref, v_ref, o_ref, do_ref, ra_ref, rb_ref, la_ref, lb_ref,
             dq_ref, dk_ref, dv_ref, dca_ref, dcb_ref, dra_ref, drb_ref, s_ref, dp_ref, p_ref, ds_ref, dq_acc, dsum_ref):
        i = pl.program_id(1)

        @pl.when(i == 0)
        def _():
            dk_ref[...] = jnp.zeros_like(dk_ref)
            dv_ref[...] = jnp.zeros_like(dv_ref)
            dca_ref[...] = jnp.zeros_like(dca_ref)
            dcb_ref[...] = jnp.zeros_like(dcb_ref)

        q2 = q_ref[...]
        do2 = do_ref[...]
        lo = _lane_lo((tq, LANES))
        hi = jnp.logical_not(lo)
        qms = (_keep(lo, q2), _keep(hi, q2))
        doms = (_keep(lo, do2), _keep(hi, do2))
        prod = do2.astype(F32) * o_ref[...]
        dsum_ref[0] = jnp.sum(jnp.where(lo, prod, 0.0), axis=-1, keepdims=True)
        dsum_ref[1] = jnp.sum(jnp.where(lo, 0.0, prod), axis=-1, keepdims=True)
        r_refs, l_refs, dc_refs, dr_refs = (ra_ref, rb_ref), (la_ref, lb_ref), (dca_ref, dcb_ref), (dra_ref, drb_ref)
        dq_acc[...] = jnp.zeros(dq_acc.shape, F32)
        dra_ref[...] = jnp.zeros(dra_ref.shape, F32)
        drb_ref[...] = jnp.zeros(drb_ref.shape, F32)
        rel = lax.broadcasted_iota(jnp.int32, (STRIP, tk), 0) - lax.broadcasted_iota(jnp.int32, (STRIP, tk), 1)

        def chunk(kc, masked):
            start = pl.multiple_of(kc * tk, tk)
            kb = k_ref[pl.ds(start, tk), :]
            vb = v_ref[pl.ds(start, tk), :]
            for h in range(2):
                s_ref[h] = _dot(qms[h], kb, NT)
                dp_ref[h] = _dot(doms[h], vb, NT)
            for h in range(2):
                cs = r_refs[h][kc]
                col_sum = jnp.zeros((1, tk), F32)
                for st in range(n_strips):
                    rows = pl.ds(st * STRIP, STRIP)
                    s = s_ref[h, rows, :] - cs
                    if masked:
                        s = jnp.where(rel >= start - (i * tq + st * STRIP), s, NEG)
                    p = jnp.exp(s - l_refs[h][rows, :])
                    ds = p * (dp_ref[h, rows, :] - dsum_ref[h, rows, :])
                    p_ref[h, rows, :] = p.astype(BF16)
                    ds_ref[h, rows, :] = ds.astype(BF16)
                    col_sum = col_sum + jnp.sum(ds, axis=0, keepdims=True)
                    dr_refs[h][rows, :] += jnp.sum(ds, axis=-1, keepdims=True)
                dc_refs[h][kc] = dc_refs[h][kc] - col_sum
            dk = _dot(ds_ref[0], qms[0], TN) + _dot(ds_ref[1], qms[1], TN)
            dv = _dot(p_ref[0], doms[0], TN) + _dot(p_ref[1], doms[1], TN)
            dk_ref[pl.ds(start, tk), :] += dk
            dv_ref[pl.ds(start, tk), :] += dv
            for h in range(2):
                dq_acc[h] += _dot(ds_ref[h], kb, NN)

        n_full = (i * tq) // tk

        def full_chunk(kc, _):
            chunk(kc, False)
            return 0

        lax.fori_loop(0, n_full, full_chunk, 0)
        chunk(n_full, True)
        dq_ref[...] = jnp.where(lo, dq_acc[0], dq_acc[1]) * scale

    row = lambda off: pl.BlockSpec((None, nk, 1, tk), lambda h, i: (2 * h + off, 0, 0, 0))
    lspec = pl.BlockSpec((None, tq, 1), lambda h, i: (h, i, 0))
    qspec = pl.BlockSpec((tq, LANES), lambda h, i: (i, h))
    full = pl.BlockSpec((T, LANES), lambda h, i: (0, h))
    dcspec = pl.BlockSpec((None, nk, 1, tk), lambda h, i: (h, 0, 0, 0))
    grad = jax.ShapeDtypeStruct((T, Dh), F32)
    dc = jax.ShapeDtypeStruct((HP, nk, 1, tk), F32)
    dr = jax.ShapeDtypeStruct((HP, T, 1), F32)
    scratch = [pltpu.VMEM((2, tq, tk), F32), pltpu.VMEM((2, tq, tk), F32), pltpu.VMEM((2, tq, tk), BF16), pltpu.VMEM((2, tq, tk), BF16),
               pltpu.VMEM((2, tq, LANES), F32), pltpu.VMEM((2, tq, 1), F32)]
    return _call(
        body, name=name, grid=(HP, T // tq), out_shape=(grad, grad, grad, dc, dc, dr, dr),
        in_specs=[qspec, pl.BlockSpec((T, LANES), lambda h, i: (0, HP + h)), full, qspec, qspec, row(0), row(1), lspec, lspec],
        out_specs=(qspec, full, full, dcspec, dcspec, lspec, lspec),
        args=[qk, qk, v, o, do, crow, crow, lse_a, lse_b], scratch_shapes=scratch, carry=carry)


def _swa_block(n, q_ref, k_ref):
    qs = pl.multiple_of(n * WINDOW, WINDOW)
    ks = pl.multiple_of(jnp.maximum(n - 1, 0) * WINDOW, WINDOW)
    rel = (qs + lax.broadcasted_iota(jnp.int32, (WINDOW, 2 * WINDOW), 0)) - (ks + lax.broadcasted_iota(jnp.int32, (WINDOW, 2 * WINDOW), 1))
    valid = jnp.logical_and(rel >= 0, rel < WINDOW)
    return qs, ks, valid


def _swa_fwd(name, q, kd, vd, sinks, carry=None):
    T, Dh = q.shape
    HP = Dh // LANES

    def body(q_ref, k_ref, v_ref, sa_ref, sb_ref, o_ref, la_ref, lb_ref):
        lo = _lane_lo((WINDOW, LANES))

        def block(n, _):
            qs, ks, valid = _swa_block(n, q_ref, k_ref)
            q2 = q_ref[pl.ds(qs, WINDOW), :]
            kb = k_ref[pl.ds(ks, 2 * WINDOW), :]
            vb = v_ref[pl.ds(ks, 2 * WINDOW), :]
            res = []
            for sel, s_ref in ((lo, sa_ref), (jnp.logical_not(lo), sb_ref)):
                qm = _keep(sel, q2)
                sink = s_ref[...]
                s = jnp.where(valid, _dot(qm, kb, NT), NEG)
                m = jnp.maximum(jnp.max(s, axis=-1, keepdims=True), sink)
                p = jnp.exp(s - m)
                l = jnp.sum(p, axis=-1, keepdims=True) + jnp.exp(sink - m)
                res.append((_dot(p.astype(BF16), vb, NN) / l, m + jnp.log(l)))
            o_ref[pl.ds(qs, WINDOW), :] = jnp.where(lo, res[0][0], res[1][0])
            la_ref[pl.ds(qs, WINDOW), :] = res[0][1]
            lb_ref[pl.ds(qs, WINDOW), :] = res[1][1]
            return 0

        lax.fori_loop(0, T // WINDOW, block, 0, unroll=2)

    full = pl.BlockSpec((T, LANES), lambda h: (0, h))
    kv = pl.BlockSpec((T, LANES), lambda h: (0, h // 2))
    sink = lambda off: pl.BlockSpec((None, 1, 1), lambda h: (2 * h + off, 0, 0))
    lse = jax.ShapeDtypeStruct((HP, T, 1), F32)
    lspec = pl.BlockSpec((None, T, 1), lambda h: (h, 0, 0))
    return _call(
        body, name=name, grid=(HP,), out_shape=(jax.ShapeDtypeStruct((T, Dh), F32), lse, lse),
        in_specs=[full, kv, kv, sink(0), sink(1)], out_specs=(full, lspec, lspec),
        args=[q, kd, vd, sinks, sinks], carry=carry)


def _swa_bwd(name, q, kd, vd, sinks, o, do, lse_a, lse_b, carry=None):
    T, Dh = q.shape
    HP = Dh // LANES
    scale = HEAD_DIM ** -0.5

    def body(q_ref, k_ref, v_ref, sa_ref, sb_ref, o_ref, do_ref, la_ref, lb_ref, dq_ref, dk_ref, dv_ref, dsa_ref, dsb_ref):
        lo = _lane_lo((WINDOW, LANES))
        hi = jnp.logical_not(lo)
        dk_ref[...] = jnp.zeros_like(dk_ref)
        dv_ref[...] = jnp.zeros_like(dv_ref)

        def block(n, dsinks):
            qs, ks, valid = _swa_block(n, q_ref, k_ref)
            q2 = q_ref[pl.ds(qs, WINDOW), :]
            do2 = do_ref[pl.ds(qs, WINDOW), :]
            kb = k_ref[pl.ds(ks, 2 * WINDOW), :]
            vb = v_ref[pl.ds(ks, 2 * WINDOW), :]
            prod = do2.astype(F32) * o_ref[pl.ds(qs, WINDOW), :]
            dqs, new = [], []
            dk = jnp.zeros((2 * WINDOW, LANES), F32)
            dv = jnp.zeros((2 * WINDOW, LANES), F32)
            for sel, s_ref, l_ref, dsink in ((lo, sa_ref, la_ref, dsinks[0]), (hi, sb_ref, lb_ref, dsinks[1])):
                qm = _keep(sel, q2)
                dom = _keep(sel, do2)
                dsum = jnp.sum(jnp.where(sel, prod, 0.0), axis=-1, keepdims=True)
                lse = l_ref[pl.ds(qs, WINDOW), :]
                s = jnp.where(valid, _dot(qm, kb, NT), NEG)
                p = jnp.exp(s - lse)
                ds = p * (_dot(dom, vb, NT) - dsum)
                dsb = ds.astype(BF16)
                dqs.append(_dot(dsb, kb, NN))
                dk = dk + _dot(dsb, qm, TN)
                dv = dv + _dot(p.astype(BF16), dom, TN)
                new.append(dsink - jnp.sum(jnp.exp(s_ref[...] - lse) * dsum, axis=0, keepdims=True))
            dq_ref[pl.ds(qs, WINDOW), :] = jnp.where(lo, dqs[0], dqs[1]) * scale
            dk_ref[pl.ds(ks, 2 * WINDOW), :] += dk
            dv_ref[pl.ds(ks, 2 * WINDOW), :] += dv
            return tuple(new)

        dsa, dsb_ = lax.fori_loop(0, T // WINDOW, block, (jnp.zeros((1, 1), F32), jnp.zeros((1, 1), F32)), unroll=2)
        dsa_ref[...] = dsa
        dsb_ref[...] = dsb_

    full = pl.BlockSpec((T, LANES), lambda h: (0, h))
    kv = pl.BlockSpec((T, LANES), lambda h: (0, h // 2))
    sink = lambda off: pl.BlockSpec((None, 1, 1), lambda h: (2 * h + off, 0, 0))
    lspec = pl.BlockSpec((None, T, 1), lambda h: (h, 0, 0))
    dsink = pl.BlockSpec((None, 1, 1), lambda h: (h, 0, 0))
    grad = jax.ShapeDtypeStruct((T, Dh), F32)
    ds_shape = jax.ShapeDtypeStruct((HP, 1, 1), F32)
    return _call(
        body, name=name, grid=(HP,), out_shape=(grad, grad, grad, ds_shape, ds_shape),
        in_specs=[full, kv, kv, sink(0), sink(1), full, full, lspec, lspec],
        out_specs=(full, full, full, dsink, dsink),
        args=[q, kd, vd, sinks, sinks, o, do, lse_a, lse_b], carry=carry)


def _place():
    return lax.axis_index("x"), lax.axis_index("y"), lax.axis_index("c")


def _run_carry(name, carry):
    c_in, c_out = len(carry.inputs), len(carry.out_shapes)

    def body(*refs):
        ins, outs, scr = refs[:c_in], refs[c_in:c_in + c_out], refs[c_in + c_out:]
        carry.start(ins, outs, scr)
        carry.middle(ins, outs, scr)
        carry.finish(ins, outs, scr)

    return pl.pallas_call(
        body, out_shape=tuple(carry.out_shapes), in_specs=[_HBM] * c_in, out_specs=tuple([_HBM] * c_out),
        scratch_shapes=carry.scratch, name=name)(*carry.inputs)


def _gather_carry(shards):
    n = len(shards)

    def plan(ins, outs, scr):
        send, recv, local = scr
        x, y, c = _place()
        me, sibling = (x, y, c), (x, y, 1 - c)
        partner, other, diag = (x ^ c, y ^ (1 - c)), (x ^ (1 - c), y ^ c), (1 - x, 1 - y)

        def copy(w, k, block, to, src=None):
            slot = 4 * block[0] + 2 * block[1] + block[2]
            return pltpu.make_async_remote_copy(
                src_ref=outs[w].at[slot] if src is None else src, dst_ref=outs[w].at[slot],
                send_sem=send.at[w, k], recv_sem=recv.at[w, k], device_id=to, device_id_type=MESH)

        own = [pltpu.make_async_copy(ins[w], outs[w].at[4 * x + 2 * y + c], local.at[w]) for w in range(n)]
        return copy, own, me, sibling, partner, other, diag, c

    def start(ins, outs, scr):
        copy, own, me, sibling, partner, other, _, c = plan(ins, outs, scr)
        for cp in own:
            cp.start()
        for w in range(n):
            copy(w, 1, me, (*partner, c), src=ins[w]).start()
            copy(w, 2, me, (*other, c), src=ins[w]).start()
            copy(w, 0, me, sibling, src=ins[w]).start()

    def middle(ins, outs, scr):
        copy, _, me, sibling, partner, other, _, c = plan(ins, outs, scr)
        for w in range(n):
            copy(w, 1, (*partner, c), me).wait_recv()
            copy(w, 3, (*partner, c), (*other, c)).start()
            copy(w, 4, (*partner, c), sibling).start()

    def finish(ins, outs, scr):
        copy, own, me, sibling, partner, other, diag, c = plan(ins, outs, scr)
        for w in range(n):
            copy(w, 2, (*other, c), me).wait_recv()
            copy(w, 5, (*other, c), sibling).start()
        for w in range(n):
            copy(w, 3, (*diag, c), me).wait_recv()
            copy(w, 6, (*diag, c), sibling).start()
        for w in range(n):
            copy(w, 0, sibling, me).wait_recv()
            copy(w, 4, (*other, 1 - c), me).wait_recv()
            copy(w, 5, (*partner, 1 - c), me).wait_recv()
            copy(w, 6, (*diag, 1 - c), me).wait_recv()
        for w in range(n):
            sent = [copy(w, 0, me, sibling, src=ins[w]), copy(w, 1, me, (*partner, c), src=ins[w]), copy(w, 2, me, (*other, c), src=ins[w]),
                    copy(w, 3, (*partner, c), (*other, c)), copy(w, 4, (*partner, c), sibling), copy(w, 5, (*other, c), sibling),
                    copy(w, 6, (*diag, c), sibling)]
            for cp in sent:
                cp.wait_send()
        for cp in own:
            cp.wait()

    return _Carry(shards, [jax.ShapeDtypeStruct((N_DEV,) + s.shape, s.dtype) for s in shards],
                  [pltpu.SemaphoreType.DMA((n, 7)), pltpu.SemaphoreType.DMA((n, 7)), pltpu.SemaphoreType.DMA((n,))], start, finish, middle)


def _sibling_carry(grads):
    n = len(grads)

    def copies(ins, outs, scr):
        send, recv = scr
        x, y, c = _place()
        return [pltpu.make_async_remote_copy(
            src_ref=ins[w].at[2 * q + (1 - c)], dst_ref=outs[w].at[q], send_sem=send.at[w, q], recv_sem=recv.at[w, q],
            device_id=(x, y, 1 - c), device_id_type=MESH) for w in range(n) for q in range(4)]

    def start(ins, outs, scr):
        for cp in copies(ins, outs, scr):
            cp.start()

    def finish(ins, outs, scr):
        for cp in copies(ins, outs, scr):
            cp.wait()

    return _Carry(grads, [jax.ShapeDtypeStruct((4,) + g.shape[1:], g.dtype) for g in grads],
                  [pltpu.SemaphoreType.DMA((n, 4)), pltpu.SemaphoreType.DMA((n, 4))], start, finish)


def _chips_carry(sums):
    n = len(sums)

    def copies(ins, outs, scr):
        send, recv = scr
        x, y, c = _place()
        chips = [(1 - x, y), (x, 1 - y), (1 - x, 1 - y)]
        return [pltpu.make_async_remote_copy(
            src_ref=ins[w].at[2 * chip[0] + chip[1]], dst_ref=outs[w].at[k], send_sem=send.at[w, k], recv_sem=recv.at[w, k],
            device_id=(*chip, c), device_id_type=MESH) for w in range(n) for k, chip in enumerate(chips)]

    def start(ins, outs, scr):
        for cp in copies(ins, outs, scr):
            cp.start()

    def finish(ins, outs, scr):
        for cp in copies(ins, outs, scr):
            cp.wait()

    return _Carry(sums, [jax.ShapeDtypeStruct((3,) + s.shape[1:], s.dtype) for s in sums],
                  [pltpu.SemaphoreType.DMA((n, 3)), pltpu.SemaphoreType.DMA((n, 3))], start, finish)


def _gather_small(packed):
    R, C = packed.shape

    def body(in_ref, out_ref, send, recv):
        x, y, c = _place()
        mine = 4 * x + 2 * y + c
        out_ref[mine] = in_ref[...]
        copies = []
        for k in range(1, N_DEV):
            peer = (x ^ (k >> 2), y ^ ((k >> 1) & 1), c ^ (k & 1))
            copies.append(pltpu.make_async_remote_copy(
                src_ref=in_ref, dst_ref=out_ref.at[mine], send_sem=send.at[k - 1], recv_sem=recv.at[k - 1],
                device_id=peer, device_id_type=MESH))
        for cp in copies:
            cp.start()
        for cp in copies:
            cp.wait()

    vmem = pl.BlockSpec(memory_space=pltpu.VMEM)
    return pl.pallas_call(
        body, out_shape=jax.ShapeDtypeStruct((N_DEV, R, C), F32), in_specs=[vmem], out_specs=vmem,
        scratch_shapes=[pltpu.SemaphoreType.DMA((N_DEV - 1,)), pltpu.SemaphoreType.DMA((N_DEV - 1,))],
        name="small_grads_all_gather")(packed)


def _adamw(w, g, m, v):
    m = ADAM_B1 * m + (1.0 - ADAM_B1) * g
    v = ADAM_B2 * v + (1.0 - ADAM_B2) * (g * g)
    m_hat = m / (1.0 - ADAM_B1 ** ADAM_STEP)
    v_hat = v / (1.0 - ADAM_B2 ** ADAM_STEP)
    delta = -ADAM_LR * (m_hat / (jnp.sqrt(v_hat) + ADAM_EPS) + ADAM_WD * w)
    return delta, m, v


def _pair_add(name, grads, received, c_idx):
    _, R, C = grads.shape
    tr = _row_tile(R)

    def body(c_ref, g_ref, r_ref, o_ref):
        o_ref[...] = (g_ref[...].astype(F32) + r_ref[...].astype(F32)).astype(BF16)

    blk = pl.BlockSpec((None, tr, C), lambda q, i, c: (q, i, 0))
    return pl.pallas_call(
        body, out_shape=jax.ShapeDtypeStruct((4, R, C), BF16),
        grid_spec=pltpu.PrefetchScalarGridSpec(
            num_scalar_prefetch=1, grid=(4, R // tr),
            in_specs=[pl.BlockSpec((None, tr, C), lambda q, i, c: (2 * q + c[0], i, 0)), blk], out_specs=blk),
        name=name, compiler_params=_params(2))(c_idx, grads, received)


def _adam_shard(name, sums, received, w, m, v, chip_idx):
    R, C = w.shape
    tr = _row_tile(R, 128)
    tc = C if tr < R or C % (2 * LANES) else 2 * LANES

    def body(q_ref, s_ref, r_ref, w_ref, m_ref, v_ref, g_out, d_out, m_out, v_out):
        g = s_ref[...].astype(F32) + r_ref[0].astype(F32) + r_ref[1].astype(F32) + r_ref[2].astype(F32)
        delta, mn, vn = _adamw(w_ref[...], g, m_ref[...], v_ref[...])
        g_out[...] = g
        d_out[...] = delta
        m_out[...] = mn
        v_out[...] = vn

    blk = pl.BlockSpec((tr, tc), lambda i, j, q: (i, j))
    shape = jax.ShapeDtypeStruct((R, C), F32)
    return pl.pallas_call(
        body, out_shape=(shape,) * 4,
        grid_spec=pltpu.PrefetchScalarGridSpec(
            num_scalar_prefetch=1, grid=(R // tr, C // tc),
            in_specs=[pl.BlockSpec((None, tr, tc), lambda i, j, q: (q[0], i, j)), pl.BlockSpec((3, tr, tc), lambda i, j, q: (0, i, j)),
                      blk, blk, blk],
            out_specs=(blk,) * 4),
        name=name, compiler_params=_params(2))(chip_idx, sums, received, w, m, v)


def _adam_small(name, gathered, w, m, v):
    R, C = w.shape

    def body(ga_ref, w_ref, m_ref, v_ref, g_out, d_out, m_out, v_out):
        g = ga_ref[0]
        for d in range(1, N_DEV):
            g = g + ga_ref[d]
        delta, mn, vn = _adamw(w_ref[...], g, m_ref[...], v_ref[...])
        g_out[...] = g
        d_out[...] = delta
        m_out[...] = mn
        v_out[...] = vn

    full = pl.BlockSpec((R, C), lambda i: (0, 0))
    shape = jax.ShapeDtypeStruct((R, C), F32)
    return pl.pallas_call(
        body, out_shape=(shape,) * 4, grid=(1,),
        in_specs=[pl.BlockSpec((N_DEV, R, C), lambda i: (0, 0, 0)), full, full, full], out_specs=(full,) * 4,
        name=name, compiler_params=_params(1))(gathered, w, m, v)


def _pack_small(parts, D):
    g1, gmix, g2, gof, gos, bf, gqf, gkf, gqs, gks, sinks = [p.reshape(-1).astype(F32) for p in parts]
    row3 = jnp.concatenate([gof, gos])
    row4 = jnp.zeros((D,), F32)
    for slot, vec in enumerate((bf, gqf, gkf, gqs, gks, sinks)):
        row4 = lax.dynamic_update_slice(row4, vec, (slot * LANES,))
    zero = jnp.zeros((D,), F32)
    return jnp.stack([g1, gmix, g2, row3, row4, zero, zero, zero])


def _unpack_small(packed, D, H):
    Dh = D // 2
    row4 = packed[4]
    short = [row4[s * LANES:s * LANES + n] for s, n in enumerate((H, HEAD_DIM, HEAD_DIM, HEAD_DIM, HEAD_DIM, H))]
    vecs = [packed[0], packed[1], packed[2], packed[3, :Dh], packed[3, Dh:]] + short
    return [v[None, :] for v in vecs]


def kernel(x, positions, norm_ffn1_g, ffn1_w_gate, ffn1_w_up, ffn1_w_down, norm_mix_g, w_in, b_forget, fox_q_norm_g, fox_k_norm_g, swa_q_norm_g, swa_k_norm_g, swa_sinks, out_norm_fox_g, out_norm_swa_g, w_out, norm_ffn2_g, ffn2_w_gate, ffn2_w_up, ffn2_w_down, loss_target, m_norm_ffn1_g, m_ffn1_w_gate, m_ffn1_w_up, m_ffn1_w_down, m_norm_mix_g, m_w_in, m_b_forget, m_fox_q_norm_g, m_fox_k_norm_g, m_swa_q_norm_g, m_swa_k_norm_g, m_swa_sinks, m_out_norm_fox_g, m_out_norm_swa_g, m_w_out, m_norm_ffn2_g, m_ffn2_w_gate, m_ffn2_w_up, m_ffn2_w_down, v_norm_ffn1_g, v_ffn1_w_gate, v_ffn1_w_up, v_ffn1_w_down, v_norm_mix_g, v_w_in, v_b_forget, v_fox_q_norm_g, v_fox_k_norm_g, v_swa_q_norm_g, v_swa_k_norm_g, v_swa_sinks, v_out_norm_fox_g, v_out_norm_swa_g, v_w_out, v_norm_ffn2_g, v_ffn2_w_gate, v_ffn2_w_up, v_ffn2_w_down):
    xs = x[0]
    target = loss_target[0]
    T, D = xs.shape
    Dh = D // 2
    H = Dh // HEAD_DIM
    HP = H // 2
    KVW = (H // GQA_GROUP) * HEAD_DIM
    KVB = KVW // LANES
    MAIN = 4 * Dh + 2 * KVW
    F_OFF = 3 * Dh
    tm = min(ROW_TILE_CAP, T)
    tq = min(512, T)
    tk = min(512, T)
    nk = T // tk
    cx, cy, cc = _place()
    c_idx = jnp.reshape(cc, (1,)).astype(jnp.int32)
    chip_idx = jnp.reshape(2 * cx + cy, (1,)).astype(jnp.int32)

    tr = jnp.transpose
    big_w = [tr(ffn1_w_gate[0]), tr(ffn1_w_up[0]), ffn1_w_down[0], tr(w_in[0]), w_out[0], tr(ffn2_w_gate[0]), tr(ffn2_w_up[0]),
             ffn2_w_down[0]]
    big_m = [tr(m_ffn1_w_gate[0]), tr(m_ffn1_w_up[0]), m_ffn1_w_down[0], tr(m_w_in[0]), m_w_out[0], tr(m_ffn2_w_gate[0]),
             tr(m_ffn2_w_up[0]), m_ffn2_w_down[0]]
    big_v = [tr(v_ffn1_w_gate[0]), tr(v_ffn1_w_up[0]), v_ffn1_w_down[0], tr(v_w_in[0]), v_w_out[0], tr(v_ffn2_w_gate[0]),
             tr(v_ffn2_w_up[0]), v_ffn2_w_down[0]]
    transposed = {"ffn1_w_gate", "ffn1_w_up", "w_in", "ffn2_w_gate", "ffn2_w_up"}
    names = ["ffn1_w_gate", "ffn1_w_up", "ffn1_w_down", "w_in", "w_out", "ffn2_w_gate", "ffn2_w_up", "ffn2_w_down"]
    sh = dict(zip(names, [w.astype(BF16) for w in big_w]))
    (wg1,) = _run_carry("weights_all_gather", _gather_carry([sh["ffn1_w_gate"]]))

    lane = jnp.arange(LANES)
    inv_freq = ROPE_THETA ** (-(2.0 * (lane % (HEAD_DIM // 2))).astype(F32) / HEAD_DIM)
    ang = positions[0].astype(F32)[:, None] * inv_freq[None, :]
    cos_t = jnp.cos(ang)
    sin_t = jnp.where((lane & (HEAD_DIM // 2)) == 0, -1.0, 1.0)[None, :] * jnp.sin(ang)
    rope = (cos_t, sin_t)

    def pair_gain(g, blocks):
        return jnp.tile(jnp.concatenate([g[0], g[0]])[None, None, :], (blocks, 1, 1))

    n1 = _rmsnorm_fwd("ffn1_norm", xs, norm_ffn1_g, tm)
    a1, (wu1,) = _ffn_gate("ffn1_gate", n1, wg1, tm, carry=_gather_carry([sh["ffn1_w_up"]]))
    (b1, hm1), (wd1,) = _ffn_up_only("ffn1_up", n1, wu1, a1, tm, carry=_gather_carry([sh["ffn1_w_down"]]))
    h1, (win_g,) = _ffn_down("ffn1_down", hm1, wd1, xs, tm, carry=_gather_carry([sh["w_in"]]))
    n_in = win_g.shape[1]
    win_t = win_g.reshape(N_DEV * n_in, D)
    win_main = jnp.concatenate([win_t[:F_OFF], win_t[F_OFF + H:]], axis=0)
    win_f = jnp.pad(win_t[F_OFF:F_OFF + H], ((0, LANES - H), (0, 0)))

    u = _rmsnorm_fwd("mix_norm", h1, norm_mix_g, tm)
    proj, (wout_g,) = _mm("mix_proj", u, win_main, tm, MAIN // 9, dims=NT, carry=_gather_carry([sh["w_out"]]))
    wout = wout_g.reshape(D, D)
    proj_f = _mm("mix_proj_forget", u, win_f, tm, LANES, dims=NT)
    scale = HEAD_DIM ** -0.5
    fox_gains = jnp.concatenate([pair_gain(fox_q_norm_g, HP), pair_gain(fox_k_norm_g, HP)])
    qk_f = _headnorm_fwd_scaled("fox_qk_norm", proj, 0, 2 * HP, fox_gains, tm, scale, HP)
    v_f = proj[:, 2 * Dh:3 * Dh].astype(BF16)
    c_t, sg_t = _forget_fwd("forget_gates", proj_f[:, :H].T, b_forget.reshape(H, 1))
    crow = c_t.reshape(H, nk, 1, tk)
    (o_fox, lse_fa, lse_fb), (wg2,) = _fox_fwd("fox_attention", qk_f, v_f, crow, tq, tk, carry=_gather_carry([sh["ffn2_w_gate"]]))

    swa_q_gains = pair_gain(swa_q_norm_g, HP)
    swa_k_gains = pair_gain(swa_k_norm_g, KVB)
    q_s = _headnorm_fwd("swa_q_norm", proj, 3 * HP, HP, swa_q_gains, tm, scale, rope=rope)
    k_d = _headnorm_fwd("swa_k_norm", proj, 4 * HP, KVB, swa_k_gains, tm, 1.0, rope=rope, dup=True)
    v_s = proj[:, 4 * Dh + KVW:].astype(BF16).reshape(T, H // GQA_GROUP, 1, HEAD_DIM)
    v_d = jnp.broadcast_to(v_s, (T, H // GQA_GROUP, 2, HEAD_DIM)).reshape(T, 2 * KVW)
    sinks3 = swa_sinks.reshape(H, 1, 1)
    (o_swa, lse_sa, lse_sb), (wu2,) = _swa_fwd("swa_attention", q_s, k_d, v_d, sinks3, carry=_gather_carry([sh["ffn2_w_up"]]))

    on = _outnorm_fwd("out_norm", o_fox, o_swa, out_norm_fox_g, out_norm_swa_g, tm)
    h2 = _mm("mix_out", on, wout, tm, min(512, D), resid=h1)

    n2 = _rmsnorm_fwd("ffn2_norm", h2, norm_ffn2_g, tm)
    (a2, b2, hm2), (wd2,) = _ffn_up("ffn2_up", n2, wg2, wu2, tm, carry=_gather_carry([sh["ffn2_w_down"]]))
    y = _ffn_down("ffn2_down", hm2, wd2, h2, tm)
    dy, dyh, sq = _loss_grad("loss_grad", y, target, min(256, T))
    loss = lax.psum(0.5 * sq[0, 0] / D, ("x", "y", "c"))

    J, Fs, _ = wg2.shape
    aspec = pl.BlockSpec((None, tm, Fs), lambda i, j: (j, i, 0))
    wspec = pl.BlockSpec((None, Fs, D), lambda i, j: (j, 0, 0))

    def pair_sums(keys, grads, received):
        return [_pair_add("sum_" + nm, g, r, c_idx) for nm, g, r in zip(keys, grads, received)]

    da2, db2 = _ffn_bwd_mid("ffn2_bwd_mid", dyh, wd2, a2, b2, tm)
    dwd2 = _wgrad_down("ffn2_wgrad_down", hm2, dyh, min(1024, D))
    dwg2, dwu2 = _wgrad_up("ffn2_wgrad_up", n2, da2, db2, min(1024, D))
    dn2, sib2 = _reduce_mm("ffn2_bwd_in", [(da2, aspec, wg2, wspec), (db2, aspec, wu2, wspec)], [], NN, T, D, tm, J,
                           carry=_sibling_carry([dwg2, dwu2, dwd2]))
    dh2, dg_ffn2, dh2b = _rmsnorm_bwd("ffn2_norm_bwd", dn2, h2, norm_ffn2_g, dy, min(256, T), 1.0)
    sum_wg2, sum_wu2, sum_wd2 = pair_sums(names[5:8], [dwg2, dwu2, dwd2], sib2)

    dwout = _wgrad_2d("mix_out_wgrad", on, dh2b, min(512, D), min(1024, D))
    do_fox, dg_of = _outnorm_bwd("out_norm_bwd_fox", dh2b, wout, 0, o_fox, out_norm_fox_g, tm)
    do_swa, dg_os = _outnorm_bwd("out_norm_bwd_swa", dh2b, wout, 1, o_swa, out_norm_swa_g, tm)

    (dq_f, dk_f, dv_f, dc_a, dc_b, dr_a, dr_b), (rc_wg2,) = _fox_bwd(
        "fox_attention_bwd", qk_f, v_f, o_fox, do_fox, crow, lse_fa, lse_fb, tq, tk, carry=_chips_carry([sum_wg2]))
    dqk_f = jnp.concatenate([dq_f, dk_f], axis=1)
    dqk_raw, dg_fox = _headnorm_bwd("fox_qk_norm_bwd", dqk_f, proj, 0, 2 * HP, fox_gains, HP, tm, 1.0)
    dct = jnp.stack([dc_a.reshape(HP, T), dc_b.reshape(HP, T)], axis=1).reshape(H, T)
    drt = jnp.stack([dr_a.reshape(HP, T), dr_b.reshape(HP, T)], axis=1).reshape(H, T)
    dz_t, db_f = _forget_bwd("forget_gates_bwd", dct, drt, sg_t)

    (dq_s, dk_p, dv_p, dsink_a, dsink_b), (rc_wu2,) = _swa_bwd(
        "swa_attention_bwd", q_s, k_d, v_d, sinks3, o_swa, do_swa, lse_sa, lse_sb, carry=_chips_carry([sum_wu2]))
    dqs_raw, dg_sq = _headnorm_bwd("swa_q_norm_bwd", dq_s, proj, 3 * HP, HP, swa_q_gains, HP, tm, 1.0, rope=rope)
    dks_raw, dg_sk = _headnorm_bwd("swa_k_norm_bwd", dk_p, proj, 4 * HP, KVB, swa_k_gains, KVB, tm, 1.0, rope=rope, fold=True)
    dvs_raw, _ = _headnorm_bwd("swa_v_fold", dv_p, None, 0, KVB, None, KVB, tm, 1.0, fold=True, norm=False)

    dproj = jnp.concatenate([dqk_raw, dv_f.astype(BF16), dqs_raw, dks_raw, dvs_raw], axis=1)
    dproj_f = jnp.pad(dz_t.T, ((0, 0), (0, LANES - H))).astype(BF16)
    dwin_main = _wgrad_2d("mix_proj_wgrad", dproj, u, MAIN // 9, min(1024, D))
    dwin_f = _wgrad_2d("mix_proj_forget_wgrad", dproj_f, u, LANES, min(1024, D))
    dwin_t = jnp.concatenate([dwin_main[:F_OFF], dwin_f[:H], dwin_main[F_OFF:]], axis=0)
    dwin_g = dwin_t.reshape(N_DEV, n_in, D)
    dwout_g = dwout.reshape(N_DEV, D // N_DEV, D)
    tkb = MAIN // 9
    du, (rc_wd2, *sib_mix) = _reduce_mm(
        "mix_bwd_in",
        [(dproj, pl.BlockSpec((tm, tkb), lambda i, r: (i, r)), win_main, pl.BlockSpec((tkb, D), lambda i, r: (r, 0)))],
        [(dproj_f, pl.BlockSpec((tm, LANES), lambda i, r: (i, 0)), win_f, pl.BlockSpec((LANES, D), lambda i, r: (0, 0)))],
        NN, T, D, tm, 9, carry=_join(_chips_carry([sum_wd2]), _sibling_carry([dwin_g, dwout_g])))
    dh1, dg_mix, dh1h = _rmsnorm_bwd("mix_norm_bwd", du, h1, norm_mix_g, dh2, min(256, T), 0.5)
    sum_win, sum_wout = pair_sums(names[3:5], [dwin_g, dwout_g], sib_mix)

    (da1, db1), (rc_win,) = _ffn_bwd_mid("ffn1_bwd_mid", dh1h, wd1, a1, b1, tm, carry=_chips_carry([sum_win]))
    (dwg1, dwu1), (rc_wout,) = _wgrad_up("ffn1_wgrad_up", n1, da1, db1, min(1024, D), carry=_chips_carry([sum_wout]))
    dwd1, sib_gu = _wgrad_down("ffn1_wgrad_down", hm1, dh1h, min(1024, D), carry=_sibling_carry([dwg1, dwu1]))
    sum_wg1, sum_wu1 = pair_sums(names[0:2], [dwg1, dwu1], sib_gu)
    dn1, (rc_wg1, rc_wu1, sib_d) = _reduce_mm(
        "ffn1_bwd_in", [(da1, aspec, wg1, wspec), (db1, aspec, wu1, wspec)], [], NN, T, D, tm, J,
        carry=_join(_chips_carry([sum_wg1, sum_wu1]), _sibling_carry([dwd1])))
    (sum_wd1,) = pair_sums(names[2:3], [dwd1], [sib_d])
    (dx, dg_ffn1), (rc_wd1,) = _rmsnorm_bwd("ffn1_norm_bwd", dn1, xs, norm_ffn1_g, dh1, min(256, T), None,
                                            carry=_chips_carry([sum_wd1]))

    chip_sums = [sum_wg1, sum_wu1, sum_wd1, sum_win, sum_wout, sum_wg2, sum_wu2, sum_wd2]
    from_chips = [rc_wg1, rc_wu1, rc_wd1, rc_win, rc_wout, rc_wg2, rc_wu2, rc_wd2]
    big_out = [_adam_shard("adam_" + nm, s, r, w, m, v, chip_idx)
               for nm, s, r, w, m, v in zip(names, chip_sums, from_chips, big_w, big_m, big_v)]

    dsinks = jnp.stack([dsink_a.reshape(HP), dsink_b.reshape(HP)], axis=1).reshape(H)
    small_g = [dg_ffn1, dg_mix, dg_ffn2, dg_of, dg_os, db_f, dg_fox[0, 0, :HEAD_DIM], dg_fox[1, 0, :HEAD_DIM],
               dg_sq[0, 0, :HEAD_DIM], dg_sk[0, 0, :HEAD_DIM], dsinks]
    small_w = [norm_ffn1_g, norm_mix_g, norm_ffn2_g, out_norm_fox_g, out_norm_swa_g, b_forget, fox_q_norm_g, fox_k_norm_g,
               swa_q_norm_g, swa_k_norm_g, swa_sinks]
    small_m = [m_norm_ffn1_g, m_norm_mix_g, m_norm_ffn2_g, m_out_norm_fox_g, m_out_norm_swa_g, m_b_forget, m_fox_q_norm_g,
               m_fox_k_norm_g, m_swa_q_norm_g, m_swa_k_norm_g, m_swa_sinks]
    small_v = [v_norm_ffn1_g, v_norm_mix_g, v_norm_ffn2_g, v_out_norm_fox_g, v_out_norm_swa_g, v_b_forget, v_fox_q_norm_g,
               v_fox_k_norm_g, v_swa_q_norm_g, v_swa_k_norm_g, v_swa_sinks]
    gathered = _gather_small(_pack_small(small_g, D))
    small_out = _adam_small("adam_small", gathered, _pack_small(small_w, D), _pack_small(small_m, D), _pack_small(small_v, D))
    small_out = [_unpack_small(p, D, H) for p in small_out]

    order = ["norm_ffn1_g", "ffn1_w_gate", "ffn1_w_up", "ffn1_w_down", "norm_mix_g", "w_in", "b_forget", "fox_q_norm_g", "fox_k_norm_g",
             "swa_q_norm_g", "swa_k_norm_g", "swa_sinks", "out_norm_fox_g", "out_norm_swa_g", "w_out", "norm_ffn2_g",
             "ffn2_w_gate", "ffn2_w_up", "ffn2_w_down"]
    small_names = ["norm_ffn1_g", "norm_mix_g", "norm_ffn2_g", "out_norm_fox_g", "out_norm_swa_g", "b_forget", "fox_q_norm_g",
                   "fox_k_norm_g", "swa_q_norm_g", "swa_k_norm_g", "swa_sinks"]
    result = [loss, dx[None]]
    for kind in range(4):
        for nm in order:
            if nm in names:
                leaf = big_out[names.index(nm)][kind]
                result.append((tr(leaf) if nm in transposed else leaf)[None])
            else:
                result.append(small_out[kind][small_names.index(nm)])
    return tuple(result)


def _headnorm_fwd_scaled(name, proj, col_off, ncb, gains, tm, scale, n_scaled):
    T = proj.shape[0]

    def body(x_ref, g_ref, o_ref):
        xv = x_ref[...]
        lo = _lane_lo(xv.shape)
        y = xv * _head_rstd(xv, lo) * g_ref[...]
        y = y * jnp.where(pl.program_id(0) < n_scaled, scale, 1.0)
        o_ref[...] = y.astype(BF16)

    return pl.pallas_call(
        body, out_shape=jax.ShapeDtypeStruct((T, ncb * LANES), BF16), grid=(ncb, T // tm),
        in_specs=[pl.BlockSpec((tm, LANES), lambda c, i: (i, col_off + c)), pl.BlockSpec((None, 1, LANES), lambda c, i: (c, 0, 0))],
        out_specs=pl.BlockSpec((tm, LANES), lambda c, i: (i, c)), name=name, compiler_params=_params(2))(proj, gains)
```

```python
import functools

import jax
import jax.numpy as jnp
from jax import lax
from jax.experimental import pallas as pl
from jax.experimental.pallas import tpu as pltpu

F32 = jnp.float32
BF16 = jnp.bfloat16

HEAD_DIM = 64
LANES = 128
WINDOW = 128
GQA_GROUP = 4
EPS = 1e-6
ROPE_THETA = 10000.0
ADAM_LR = 0.001
ADAM_B1 = 0.9
ADAM_B2 = 0.999
ADAM_EPS = 1e-08
ADAM_WD = 0.01
ADAM_STEP = 10
N_DEV = 8
NEG = -1e30
VMEM_LIMIT_V7X = 48 * 1024 * 1024
ROW_TILE_CAP = 512
MESH = pl.DeviceIdType.MESH

NN = (((1,), (0,)), ((), ()))
NT = (((1,), (1,)), ((), ()))
TN = (((0,), (0,)), ((), ()))


def _dot(a, b, dims):
    return lax.dot_general(a, b, dims, preferred_element_type=F32)


def _params(n_axes):
    return pltpu.CompilerParams(dimension_semantics=("arbitrary",) * n_axes, vmem_limit_bytes=VMEM_LIMIT_V7X)


def _row_tile(rows, cap=ROW_TILE_CAP):
    best = None
    for t in range(16, min(rows, cap) + 1, 16):
        if rows % t == 0:
            best = t
    return best or rows


def _lane_lo(shape):
    return lax.broadcasted_iota(jnp.int32, shape, len(shape) - 1) < HEAD_DIM


def _keep(sel, x):
    return jnp.where(sel, x.astype(F32), 0.0).astype(BF16)


_HBM = pl.BlockSpec(memory_space=pltpu.HBM)


class _Carry:
    def __init__(self, inputs, out_shapes, scratch, start, finish, middle=None):
        self.inputs, self.out_shapes, self.scratch = list(inputs), list(out_shapes), list(scratch)
        self.start, self.finish, self.middle = start, finish, middle or (lambda ins, outs, scr: None)


def _join(*carries):
    def hook(which):
        def run(ins, outs, scr):
            i = o = s = 0
            for c in carries:
                ni, no, ns = len(c.inputs), len(c.out_shapes), len(c.scratch)
                getattr(c, which)(ins[i:i + ni], outs[o:o + no], scr[s:s + ns])
                i, o, s = i + ni, o + no, s + ns
        return run

    return _Carry([a for c in carries for a in c.inputs], [a for c in carries for a in c.out_shapes],
                  [a for c in carries for a in c.scratch], hook("start"), hook("finish"), hook("middle"))


def _call(body, *, name, grid, in_specs, out_specs, out_shape, args, scratch_shapes=(), carry=None):
    params = _params(len(grid))
    if carry is None:
        return pl.pallas_call(body, out_shape=out_shape, grid=grid, in_specs=list(in_specs), out_specs=out_specs,
                              scratch_shapes=list(scratch_shapes), name=name, compiler_params=params)(*args)
    single = not isinstance(out_shape, (tuple, list))
    shapes = (out_shape,) if single else tuple(out_shape)
    specs = (out_specs,) if single else tuple(out_specs)
    n_in, n_out, n_scr = len(args), len(shapes), len(scratch_shapes)
    c_in, c_out = len(carry.inputs), len(carry.out_shapes)

    def wrapped(*refs):
        ins, c_ins = refs[:n_in], refs[n_in:n_in + c_in]
        o0 = n_in + c_in
        outs, c_outs = refs[o0:o0 + n_out], refs[o0 + n_out:o0 + n_out + c_out]
        s0 = o0 + n_out + c_out
        scr, c_scr = refs[s0:s0 + n_scr], refs[s0 + n_scr:]
        step, total = pl.program_id(0), grid[0]
        for ax in range(1, len(grid)):
            step, total = step * grid[ax] + pl.program_id(ax), total * grid[ax]

        @pl.when(step == 0)
        def _():
            carry.start(c_ins, c_outs, c_scr)

        @pl.when(step == total // 2)
        def _():
            carry.middle(c_ins, c_outs, c_scr)

        body(*ins, *outs, *scr)

        @pl.when(step == total - 1)
        def _():
            carry.finish(c_ins, c_outs, c_scr)

    res = pl.pallas_call(
        wrapped, out_shape=shapes + tuple(carry.out_shapes), grid=grid, in_specs=list(in_specs) + [_HBM] * c_in,
        out_specs=specs + (_HBM,) * c_out, scratch_shapes=list(scratch_shapes) + carry.scratch, name=name,
        compiler_params=params)(*args, *carry.inputs)
    main = res[:n_out]
    return (main[0] if single else tuple(main)), tuple(res[n_out:])


def _rms_bwd(dn, x, g):
    r = lax.rsqrt(jnp.mean(x * x, axis=-1, keepdims=True) + EPS)
    xh = x * r
    dxh = dn * g
    dx = r * (dxh - xh * jnp.mean(dxh * xh, axis=-1, keepdims=True))
    return dx, jnp.sum(dn * xh, axis=0, keepdims=True)


def _rmsnorm_fwd(name, x, g, tm, carry=None):
    T, D = x.shape

    def body(x_ref, g_ref, o_ref):
        xf = x_ref[...]
        r = lax.rsqrt(jnp.mean(xf * xf, axis=-1, keepdims=True) + EPS)
        o_ref[...] = (xf * r * g_ref[...]).astype(BF16)

    return _call(
        body, name=name, grid=(T // tm,), out_shape=jax.ShapeDtypeStruct((T, D), BF16),
        in_specs=[pl.BlockSpec((tm, D), lambda i: (i, 0)), pl.BlockSpec((1, D), lambda i: (0, 0))],
        out_specs=pl.BlockSpec((tm, D), lambda i: (i, 0)), args=[x, g], carry=carry)


def _outnorm_fwd(name, o_fox, o_swa, g_fox, g_swa, tm):
    T, Dh = o_fox.shape

    def body(a_ref, b_ref, ga_ref, gb_ref, o_ref):
        for ref, g_ref, lo in ((a_ref, ga_ref, 0), (b_ref, gb_ref, Dh)):
            xf = ref[...]
            r = lax.rsqrt(jnp.mean(xf * xf, axis=-1, keepdims=True) + EPS)
            o_ref[:, lo:lo + Dh] = (xf * r * g_ref[...]).astype(BF16)

    row = pl.BlockSpec((tm, Dh), lambda i: (i, 0))
    gain = pl.BlockSpec((1, Dh), lambda i: (0, 0))
    return pl.pallas_call(
        body, out_shape=jax.ShapeDtypeStruct((T, 2 * Dh), BF16), grid=(T // tm,),
        in_specs=[row, row, gain, gain], out_specs=pl.BlockSpec((tm, 2 * Dh), lambda i: (i, 0)),
        name=name, compiler_params=_params(1))(o_fox, o_swa, g_fox, g_swa)


def _outnorm_bwd(name, dhb, wout, half, o, g, tm):
    T, D = dhb.shape
    Dh = o.shape[1]

    def body(a_ref, w_ref, o_ref, g_ref, do_ref, dg_ref):
        don = _dot(a_ref[...], w_ref[...], NT)
        dx, dg = _rms_bwd(don, o_ref[...], g_ref[...])
        do_ref[...] = dx.astype(BF16)

        @pl.when(pl.program_id(0) == 0)
        def _():
            dg_ref[...] = dg

        @pl.when(pl.program_id(0) > 0)
        def _():
            dg_ref[...] += dg

    return pl.pallas_call(
        body, out_shape=(jax.ShapeDtypeStruct((T, Dh), BF16), jax.ShapeDtypeStruct((1, Dh), F32)), grid=(T // tm,),
        in_specs=[pl.BlockSpec((tm, D), lambda i: (i, 0)), pl.BlockSpec((Dh, D), lambda i: (half, 0)),
                  pl.BlockSpec((tm, Dh), lambda i: (i, 0)), pl.BlockSpec((1, Dh), lambda i: (0, 0))],
        out_specs=(pl.BlockSpec((tm, Dh), lambda i: (i, 0)), pl.BlockSpec((1, Dh), lambda i: (0, 0))),
        name=name, compiler_params=_params(1))(dhb, wout, o, g)


def _mm(name, a, b, tm, tn, dims=NN, resid=None, carry=None):
    M, K = a.shape
    transposed = dims == NT
    N = b.shape[0] if transposed else b.shape[1]

    def body(*refs):
        if resid is None:
            a_ref, b_ref, o_ref = refs
            o_ref[...] = _dot(a_ref[...], b_ref[...], dims)
        else:
            a_ref, b_ref, r_ref, o_ref = refs
            o_ref[...] = r_ref[...] + _dot(a_ref[...], b_ref[...], dims)

    ospec = pl.BlockSpec((tm, tn), lambda n, i: (i, n))
    bspec = pl.BlockSpec((tn, K), lambda n, i: (n, 0)) if transposed else pl.BlockSpec((K, tn), lambda n, i: (0, n))
    in_specs = [pl.BlockSpec((tm, K), lambda n, i: (i, 0)), bspec]
    args = [a, b]
    if resid is not None:
        in_specs.append(ospec)
        args.append(resid)
    return _call(body, name=name, grid=(N // tn, M // tm), in_specs=in_specs, out_specs=ospec,
                 out_shape=jax.ShapeDtypeStruct((M, N), F32), args=args, carry=carry)


def _wgrad_2d(name, a, b, tmm, tn, carry=None):
    T, M = a.shape
    N = b.shape[1]

    def body(a_ref, b_ref, o_ref):
        o_ref[...] = _dot(a_ref[...], b_ref[...], TN).astype(BF16)

    return _call(
        body, name=name, grid=(M // tmm, N // tn), out_shape=jax.ShapeDtypeStruct((M, N), BF16),
        in_specs=[pl.BlockSpec((T, tmm), lambda m, n: (0, m)), pl.BlockSpec((T, tn), lambda m, n: (0, n))],
        out_specs=pl.BlockSpec((tmm, tn), lambda m, n: (m, n)), args=[a, b], carry=carry)


def _wgrad_down(name, hm, df, tn, carry=None):
    J, T, Fs = hm.shape
    D = df.shape[1]

    def body(a_ref, b_ref, o_ref):
        o_ref[...] = _dot(a_ref[...], b_ref[...], TN).astype(BF16)

    return _call(
        body, name=name, grid=(J, D // tn), out_shape=jax.ShapeDtypeStruct((J, Fs, D), BF16),
        in_specs=[pl.BlockSpec((None, T, Fs), lambda j, n: (j, 0, 0)), pl.BlockSpec((T, tn), lambda j, n: (0, n))],
        out_specs=pl.BlockSpec((None, Fs, tn), lambda j, n: (j, 0, n)), args=[hm, df], carry=carry)


def _wgrad_up(name, n, da, db, tn, carry=None):
    T, D = n.shape
    J, _, Fs = da.shape

    def body(n_ref, da_ref, db_ref, og_ref, ou_ref):
        nv = n_ref[...]
        og_ref[...] = _dot(da_ref[...], nv, TN).astype(BF16)
        ou_ref[...] = _dot(db_ref[...], nv, TN).astype(BF16)

    act = pl.BlockSpec((None, T, Fs), lambda j, m: (j, 0, 0))
    out = pl.BlockSpec((None, Fs, tn), lambda j, m: (j, 0, m))
    shape = jax.ShapeDtypeStruct((J, Fs, D), BF16)
    return _call(
        body, name=name, grid=(J, D // tn), out_shape=(shape, shape),
        in_specs=[pl.BlockSpec((T, tn), lambda j, m: (0, m)), act, act], out_specs=(out, out),
        args=[n, da, db], carry=carry)


def _reduce_mm(name, pairs, once, dims, T, D, tm, steps, carry=None):
    n_pairs = len(pairs)
    n_once = len(once)

    def body(*refs):
        pr = refs[:2 * n_pairs]
        on = refs[2 * n_pairs:2 * (n_pairs + n_once)]
        o_ref, acc = refs[-2:]
        r = pl.program_id(1)
        part = _dot(pr[0][...], pr[1][...], dims)
        for p in range(1, n_pairs):
            part = part + _dot(pr[2 * p][...], pr[2 * p + 1][...], dims)

        @pl.when(r == 0)
        def _():
            acc[...] = part

        @pl.when(r > 0)
        def _():
            acc[...] += part

        @pl.when(r == steps - 1)
        def _():
            dn = acc[...]
            for p in range(n_once):
                dn = dn + _dot(on[2 * p][...], on[2 * p + 1][...], dims)
            o_ref[...] = dn

    in_specs, args = [], []
    for a, a_spec, w, w_spec in list(pairs) + list(once):
        in_specs += [a_spec, w_spec]
        args += [a, w]
    row = pl.BlockSpec((tm, D), lambda i, r: (i, 0))
    return _call(body, name=name, grid=(T // tm, steps), in_specs=in_specs, out_specs=row, out_shape=jax.ShapeDtypeStruct((T, D), F32),
                 args=args, scratch_shapes=[pltpu.VMEM((tm, D), F32)], carry=carry)


def _rmsnorm_bwd(name, dn, x, g, dh, tm, bf16_scale, carry=None):
    T, D = x.shape
    emit_bf16 = bf16_scale is not None

    def body(dn_ref, x_ref, g_ref, dh_ref, *outs):
        dxn, dg = _rms_bwd(dn_ref[...], x_ref[...], g_ref[...])
        dx = dh_ref[...] + dxn
        outs[0][...] = dx
        if emit_bf16:
            outs[2][...] = (bf16_scale * dx).astype(BF16)

        @pl.when(pl.program_id(0) == 0)
        def _():
            outs[1][...] = dg

        @pl.when(pl.program_id(0) > 0)
        def _():
            outs[1][...] += dg

    row = pl.BlockSpec((tm, D), lambda i: (i, 0))
    gain = pl.BlockSpec((1, D), lambda i: (0, 0))
    out_shape = [jax.ShapeDtypeStruct((T, D), F32), jax.ShapeDtypeStruct((1, D), F32)]
    out_specs = [row, gain]
    if emit_bf16:
        out_shape.append(jax.ShapeDtypeStruct((T, D), BF16))
        out_specs.append(row)
    return _call(body, name=name, grid=(T // tm,), in_specs=[row, row, gain, row], out_specs=tuple(out_specs),
                 out_shape=tuple(out_shape), args=[dn, x, g, dh], carry=carry)


def _loss_grad(name, y, target, tm):
    T, D = y.shape

    def body(y_ref, t_ref, dy_ref, dyh_ref, sq_ref):
        diff = y_ref[...] - t_ref[...]
        sq = jnp.sum(jnp.sum(diff * diff, axis=1, keepdims=True), axis=0, keepdims=True)
        dy = diff * (1.0 / D)
        dy_ref[...] = dy
        dyh_ref[...] = (0.5 * dy).astype(BF16)

        @pl.when(pl.program_id(0) == 0)
        def _():
            sq_ref[...] = sq

        @pl.when(pl.program_id(0) > 0)
        def _():
            sq_ref[...] += sq

    row = pl.BlockSpec((tm, D), lambda i: (i, 0))
    return pl.pallas_call(
        body, out_shape=(jax.ShapeDtypeStruct((T, D), F32), jax.ShapeDtypeStruct((T, D), BF16), jax.ShapeDtypeStruct((1, 1), F32)),
        grid=(T // tm,), in_specs=[row, row], out_specs=(row, row, pl.BlockSpec((1, 1), lambda i: (0, 0))),
        name=name, compiler_params=_params(1))(y, target)


def _ffn_up(name, n, wg, wu, tm, carry=None):
    T, D = n.shape
    J, Fs, _ = wg.shape

    def body(n_ref, wg_ref, wu_ref, a_ref, b_ref, h_ref):
        xv = n_ref[...]
        a = _dot(xv, wg_ref[...], NT)
        b = _dot(xv, wu_ref[...], NT)
        a_ref[...] = a.astype(BF16)
        b_ref[...] = b.astype(BF16)
        h_ref[...] = (a * jax.nn.sigmoid(a) * b).astype(BF16)

    act = jax.ShapeDtypeStruct((J, T, Fs), BF16)
    wspec = pl.BlockSpec((None, Fs, D), lambda j, i: (j, 0, 0))
    aspec = pl.BlockSpec((None, tm, Fs), lambda j, i: (j, i, 0))
    return _call(
        body, name=name, grid=(J, T // tm), out_shape=(act, act, act),
        in_specs=[pl.BlockSpec((tm, D), lambda j, i: (i, 0)), wspec, wspec], out_specs=(aspec, aspec, aspec),
        args=[n, wg, wu], carry=carry)


def _ffn_gate(name, n, wg, tm, carry=None):
    T, D = n.shape
    J, Fs, _ = wg.shape

    def body(n_ref, wg_ref, a_ref):
        a_ref[...] = _dot(n_ref[...], wg_ref[...], NT).astype(BF16)

    aspec = pl.BlockSpec((None, tm, Fs), lambda j, i: (j, i, 0))
    return _call(
        body, name=name, grid=(J, T // tm), out_shape=jax.ShapeDtypeStruct((J, T, Fs), BF16),
        in_specs=[pl.BlockSpec((tm, D), lambda j, i: (i, 0)), pl.BlockSpec((None, Fs, D), lambda j, i: (j, 0, 0))],
        out_specs=aspec, args=[n, wg], carry=carry)


def _ffn_up_only(name, n, wu, a, tm, carry=None):
    T, D = n.shape
    J, Fs, _ = wu.shape

    def body(n_ref, wu_ref, a_ref, b_ref, h_ref):
        b = _dot(n_ref[...], wu_ref[...], NT)
        a = a_ref[...].astype(F32)
        b_ref[...] = b.astype(BF16)
        h_ref[...] = (a * jax.nn.sigmoid(a) * b).astype(BF16)

    act = jax.ShapeDtypeStruct((J, T, Fs), BF16)
    aspec = pl.BlockSpec((None, tm, Fs), lambda j, i: (j, i, 0))
    return _call(
        body, name=name, grid=(J, T // tm), out_shape=(act, act),
        in_specs=[pl.BlockSpec((tm, D), lambda j, i: (i, 0)), pl.BlockSpec((None, Fs, D), lambda j, i: (j, 0, 0)), aspec],
        out_specs=(aspec, aspec), args=[n, wu, a], carry=carry)


def _ffn_down(name, hm, wd, resid, tm, carry=None):
    J, T, Fs = hm.shape
    D = wd.shape[2]

    def body(h_ref, w_ref, r_ref, o_ref, acc):
        j = pl.program_id(1)
        part = _dot(h_ref[...], w_ref[...], NN)

        @pl.when(j == 0)
        def _():
            acc[...] = part

        @pl.when(j > 0)
        def _():
            acc[...] += part

        @pl.when(j == J - 1)
        def _():
            o_ref[...] = r_ref[...] + 0.5 * acc[...]

    row = pl.BlockSpec((tm, D), lambda i, j: (i, 0))
    return _call(
        body, name=name, grid=(T // tm, J), out_shape=jax.ShapeDtypeStruct((T, D), F32),
        in_specs=[pl.BlockSpec((None, tm, Fs), lambda i, j: (j, i, 0)), pl.BlockSpec((None, Fs, D), lambda i, j: (j, 0, 0)), row],
        out_specs=row, scratch_shapes=[pltpu.VMEM((tm, D), F32)], args=[hm, wd, resid], carry=carry)


def _ffn_bwd_mid(name, dfh, wd, a, b, tm, carry=None):
    T, D = dfh.shape
    J, Fs, _ = wd.shape

    def body(df_ref, w_ref, a_ref, b_ref, da_ref, db_ref):
        dhm = _dot(df_ref[...], w_ref[...], NT)
        av = a_ref[...].astype(F32)
        bv = b_ref[...].astype(F32)
        sg = jax.nn.sigmoid(av)
        da_ref[...] = (dhm * bv * (sg * (1.0 + av * (1.0 - sg)))).astype(BF16)
        db_ref[...] = (dhm * (av * sg)).astype(BF16)

    act = jax.ShapeDtypeStruct((J, T, Fs), BF16)
    aspec = pl.BlockSpec((None, tm, Fs), lambda j, i: (j, i, 0))
    return _call(
        body, name=name, grid=(J, T // tm), out_shape=(act, act),
        in_specs=[pl.BlockSpec((tm, D), lambda j, i: (i, 0)), pl.BlockSpec((None, Fs, D), lambda j, i: (j, 0, 0)), aspec, aspec],
        out_specs=(aspec, aspec), args=[dfh, wd, a, b], carry=carry)


def _rot_half(y, lane):
    first = (lane & (HEAD_DIM // 2)) == 0
    return jnp.where(first, pltpu.roll(y, LANES - HEAD_DIM // 2, 1), pltpu.roll(y, HEAD_DIM // 2, 1))


def _head_rstd(x, lo):
    sq = x * x
    ss_a = jnp.sum(jnp.where(lo, sq, 0.0), axis=-1, keepdims=True)
    ss_b = jnp.sum(jnp.where(lo, 0.0, sq), axis=-1, keepdims=True)
    return lax.rsqrt(jnp.where(lo, ss_a, ss_b) * (1.0 / HEAD_DIM) + EPS)


def _headnorm_fwd(name, proj, col_off, ncb, gains, tm, scale, rope=None, dup=False):
    T = proj.shape[0]
    with_rope = rope is not None
    width = 2 * LANES if dup else LANES

    def body(*refs):
        if with_rope:
            x_ref, g_ref, cos_ref, sin_ref, o_ref = refs
        else:
            x_ref, g_ref, o_ref = refs
        xv = x_ref[...]
        lane = lax.broadcasted_iota(jnp.int32, xv.shape, 1)
        lo = lane < HEAD_DIM
        y = xv * _head_rstd(xv, lo) * g_ref[...]
        if with_rope:
            y = y * cos_ref[...] + _rot_half(y, lane) * sin_ref[...]
        y = y * scale
        if dup:
            sw = pltpu.roll(y, HEAD_DIM, 1)
            o_ref[:, :LANES] = jnp.where(lo, y, sw).astype(BF16)
            o_ref[:, LANES:] = jnp.where(lo, sw, y).astype(BF16)
        else:
            o_ref[...] = y.astype(BF16)

    in_specs = [pl.BlockSpec((tm, LANES), lambda c, i: (i, col_off + c)), pl.BlockSpec((None, 1, LANES), lambda c, i: (c, 0, 0))]
    args = [proj, gains]
    if with_rope:
        tab = pl.BlockSpec((tm, LANES), lambda c, i: (i, 0))
        in_specs += [tab, tab]
        args += list(rope)
    return pl.pallas_call(
        body, out_shape=jax.ShapeDtypeStruct((T, ncb * width), BF16), grid=(ncb, T // tm),
        in_specs=in_specs, out_specs=pl.BlockSpec((tm, width), lambda c, i: (i, c)),
        name=name, compiler_params=_params(2))(*args)


def _headnorm_bwd(name, dy, proj, col_off, ncb, gains, group, tm, scale, rope=None, fold=False, norm=True):
    T = dy.shape[0]
    with_rope = rope is not None
    n_groups = ncb // group
    dy_width = 4 * LANES if fold else LANES

    def body(*refs):
        refs = list(refs)
        dy_ref = refs.pop(0)
        x_ref = refs.pop(0) if norm else None
        g_ref = refs.pop(0) if norm else None
        cos_ref = refs.pop(0) if with_rope else None
        sin_ref = refs.pop(0) if with_rope else None
        dx_ref = refs.pop(0)
        dg_ref = refs.pop(0) if norm else None
        c = pl.program_id(0)
        i = pl.program_id(1)
        d = dy_ref[...]
        lane = lax.broadcasted_iota(jnp.int32, (d.shape[0], LANES), 1)
        lo = lane < HEAD_DIM
        if fold:
            t0 = d[:, 0:LANES] + d[:, LANES:2 * LANES]
            t1 = d[:, 2 * LANES:3 * LANES] + d[:, 3 * LANES:4 * LANES]
            d = jnp.where(lo, t0 + pltpu.roll(t0, HEAD_DIM, 1), t1 + pltpu.roll(t1, HEAD_DIM, 1))
        d = d * scale
        if with_rope:
            d = d * cos_ref[...] + _rot_half(d * sin_ref[...], lane)
        if not norm:
            dx_ref[...] = d.astype(BF16)
            return
        xv = x_ref[...]
        gv = g_ref[...]
        r = _head_rstd(xv, lo)
        xh = xv * r
        dxh = d * gv
        pr = dxh * xh
        m_a = jnp.sum(jnp.where(lo, pr, 0.0), axis=-1, keepdims=True)
        m_b = jnp.sum(jnp.where(lo, 0.0, pr), axis=-1, keepdims=True)
        mean = jnp.where(lo, m_a, m_b) * (1.0 / HEAD_DIM)
        dx_ref[...] = (r * (dxh - xh * mean)).astype(BF16)
        dgp = jnp.sum(d * xh, axis=0, keepdims=True)
        dgp = dgp + pltpu.roll(dgp, HEAD_DIM, 1)
        first = jnp.logical_and(c % group == 0, i == 0)

        @pl.when(first)
        def _():
            dg_ref[...] = dgp

        @pl.when(jnp.logical_not(first))
        def _():
            dg_ref[...] += dgp

    in_specs = [pl.BlockSpec((tm, dy_width), lambda c, i: (i, c))]
    args = [dy]
    if norm:
        in_specs += [pl.BlockSpec((tm, LANES), lambda c, i: (i, col_off + c)), pl.BlockSpec((None, 1, LANES), lambda c, i: (c, 0, 0))]
        args += [proj, gains]
    if with_rope:
        tab = pl.BlockSpec((tm, LANES), lambda c, i: (i, 0))
        in_specs += [tab, tab]
        args += list(rope)
    out_shape = [jax.ShapeDtypeStruct((T, ncb * LANES), BF16)]
    out_specs = [pl.BlockSpec((tm, LANES), lambda c, i: (i, c))]
    if norm:
        out_shape.append(jax.ShapeDtypeStruct((n_groups, 1, LANES), F32))
        out_specs.append(pl.BlockSpec((None, 1, LANES), lambda c, i: (c // group, 0, 0)))
    res = pl.pallas_call(
        body, out_shape=tuple(out_shape), grid=(ncb, T // tm), in_specs=in_specs, out_specs=tuple(out_specs),
        name=name, compiler_params=_params(2))(*args)
    return res if norm else (res[0], None)


def _dot_exact(x, tri):
    hi = x.astype(BF16)
    r1 = x - hi.astype(F32)
    mid = r1.astype(BF16)
    lo = (r1 - mid.astype(F32)).astype(BF16)
    return _dot(hi, tri, NN) + _dot(mid, tri, NN) + _dot(lo, tri, NN)


def _forget_fwd(name, zt, bias):
    H, T = zt.shape
    blk = min(256, T)

    def body(z_ref, b_ref, c_ref, s_ref):
        z = z_ref[...] + b_ref[...]
        s_ref[...] = jax.nn.sigmoid(-z)
        lf = jnp.minimum(z, 0.0) - jnp.log(1.0 + jnp.exp(-jnp.abs(z)))
        tri = (lax.broadcasted_iota(jnp.int32, (blk, blk), 0) <= lax.broadcasted_iota(jnp.int32, (blk, blk), 1)).astype(BF16)
        carry = jnp.zeros((H, 1), F32)
        for bi in range(T // blk):
            xb = lf[:, bi * blk:(bi + 1) * blk]
            c_ref[:, bi * blk:(bi + 1) * blk] = _dot_exact(xb, tri) + carry
            carry = carry + jnp.sum(xb, axis=-1, keepdims=True)

    shape = jax.ShapeDtypeStruct((H, T), F32)
    full = pl.BlockSpec((H, T), lambda i: (0, 0))
    return pl.pallas_call(
        body, out_shape=(shape, shape), grid=(1,), in_specs=[full, pl.BlockSpec((H, 1), lambda i: (0, 0))],
        out_specs=(full, full), name=name, compiler_params=_params(1))(zt, bias)


def _forget_bwd(name, dct, drt, sgt):
    H, T = dct.shape
    blk = min(256, T)

    def body(dc_ref, dr_ref, s_ref, dz_ref, db_ref):
        dc = dc_ref[...] + dr_ref[...]
        tri = (lax.broadcasted_iota(jnp.int32, (blk, blk), 0) >= lax.broadcasted_iota(jnp.int32, (blk, blk), 1)).astype(BF16)
        carry = jnp.zeros((H, 1), F32)
        db = jnp.zeros((H, 1), F32)
        for bi in reversed(range(T // blk)):
            xb = dc[:, bi * blk:(bi + 1) * blk]
            dz = (_dot_exact(xb, tri) + carry) * s_ref[:, bi * blk:(bi + 1) * blk]
            dz_ref[:, bi * blk:(bi + 1) * blk] = dz
            db = db + jnp.sum(dz, axis=-1, keepdims=True)
            carry = carry + jnp.sum(xb, axis=-1, keepdims=True)
        db_ref[...] = db

    full = pl.BlockSpec((H, T), lambda i: (0, 0))
    return pl.pallas_call(
        body, out_shape=(jax.ShapeDtypeStruct((H, T), F32), jax.ShapeDtypeStruct((H, 1), F32)), grid=(1,),
        in_specs=[full, full, full], out_specs=(full, pl.BlockSpec((H, 1), lambda i: (0, 0))),
        name=name, compiler_params=_params(1))(dct, drt, sgt)


STRIP = 256


def _fox_fwd(name, qk, v, crow, tq, tk, carry=None):
    T, Dh = v.shape
    HP = Dh // LANES
    nk = T // tk
    assert tk % tq == 0 and tq % STRIP == 0
    n_strips = tq // STRIP

    def body(q_ref, k_ref, v_ref, ra_ref, rb_ref, o_ref, la_ref, lb_ref, s_ref, p_ref, m_ref, l_ref, acc_ref):
        i = pl.program_id(1)
        q2 = q_ref[...]
        lo = _lane_lo((tq, LANES))
        qms = (_keep(lo, q2), _keep(jnp.logical_not(lo), q2))
        r_refs = (ra_ref, rb_ref)
        m_ref[...] = jnp.full(m_ref.shape, NEG, F32)
        l_ref[...] = jnp.zeros(l_ref.shape, F32)
        acc_ref[...] = jnp.zeros(acc_ref.shape, F32)
        rel = lax.broadcasted_iota(jnp.int32, (STRIP, tk), 0) - lax.broadcasted_iota(jnp.int32, (STRIP, tk), 1)

        def chunk(kc, masked):
            start = pl.multiple_of(kc * tk, tk)
            kb = k_ref[pl.ds(start, tk), :]
            vb = v_ref[pl.ds(start, tk), :]
            for h in range(2):
                s_ref[h] = _dot(qms[h], kb, NT)
            for h in range(2):
                cs = r_refs[h][kc]
                for st in range(n_strips):
                    rows = pl.ds(st * STRIP, STRIP)
                    s = s_ref[h, rows, :] - cs
                    if masked:
                        s = jnp.where(rel >= start - (i * tq + st * STRIP), s, NEG)
                    m_old = m_ref[h, rows, :]
                    mn = jnp.maximum(m_old, jnp.max(s, axis=-1, keepdims=True))
                    p = jnp.exp(s - mn)
                    alpha = jnp.exp(m_old - mn)
                    l_ref[h, rows, :] = alpha * l_ref[h, rows, :] + jnp.sum(p, axis=-1, keepdims=True)
                    m_ref[h, rows, :] = mn
                    p_ref[h, rows, :] = p.astype(BF16)
                    acc_ref[h, rows, :] = acc_ref[h, rows, :] * alpha
            for h in range(2):
                acc_ref[h] += _dot(p_ref[h], vb, NN)

        n_full = (i * tq) // tk

        def full_chunk(kc, _):
            chunk(kc, False)
            return 0

        lax.fori_loop(0, n_full, full_chunk, 0)
        chunk(n_full, True)
        o_ref[...] = jnp.where(lo, acc_ref[0] / l_ref[0], acc_ref[1] / l_ref[1])
        la_ref[...] = m_ref[0] + jnp.log(l_ref[0])
        lb_ref[...] = m_ref[1] + jnp.log(l_ref[1])

    row = lambda off: pl.BlockSpec((None, nk, 1, tk), lambda h, i: (2 * h + off, 0, 0, 0))
    lse = jax.ShapeDtypeStruct((HP, T, 1), F32)
    lspec = pl.BlockSpec((None, tq, 1), lambda h, i: (h, i, 0))
    scratch = [pltpu.VMEM((2, tq, tk), F32), pltpu.VMEM((2, tq, tk), BF16), pltpu.VMEM((2, tq, 1), F32),
               pltpu.VMEM((2, tq, 1), F32), pltpu.VMEM((2, tq, LANES), F32)]
    return _call(
        body, name=name, grid=(HP, T // tq), out_shape=(jax.ShapeDtypeStruct((T, Dh), F32), lse, lse),
        in_specs=[pl.BlockSpec((tq, LANES), lambda h, i: (i, h)), pl.BlockSpec((T, LANES), lambda h, i: (0, HP + h)),
                  pl.BlockSpec((T, LANES), lambda h, i: (0, h)), row(0), row(1)],
        out_specs=(pl.BlockSpec((tq, LANES), lambda h, i: (i, h)), lspec, lspec),
        args=[qk, qk, v, crow, crow], scratch_shapes=scratch, carry=carry)


def _fox_bwd(name, qk, v, o, do, crow, lse_a, lse_b, tq, tk, carry=None):
    T, Dh = v.shape
    HP = Dh // LANES
    nk = T // tk
    scale = HEAD_DIM ** -0.5
    assert tk % tq == 0 and tq % STRIP == 0
    n_strips = tq // STRIP

    def body(q_ref, k_ref, v_ref, o_ref, do_ref, ra_ref, rb_ref, la_ref, lb_ref,
             dq_ref, dk_ref, dv_ref, dca_ref, dcb_ref, dra_ref, drb_ref, s_ref, dp_ref, p_ref, ds_ref, dq_acc, dsum_ref):
        i = pl.program_id(1)

        @pl.when(i == 0)
        def _():
            dk_ref[...] = jnp.zeros_like(dk_ref)
            dv_ref[...] = jnp.zeros_like(dv_ref)
            dca_ref[...] = jnp.zeros_like(dca_ref)
            dcb_ref[...] = jnp.zeros_like(dcb_ref)

        q2 = q_ref[...]
        do2 = do_ref[...]
        lo = _lane_lo((tq, LANES))
        hi = jnp.logical_not(lo)
        qms = (_keep(lo, q2), _keep(hi, q2))
        doms = (_keep(lo, do2), _keep(hi, do2))
        prod = do2.astype(F32) * o_ref[...]
        dsum_ref[0] = jnp.sum(jnp.where(lo, prod, 0.0), axis=-1, keepdims=True)
        dsum_ref[1] = jnp.sum(jnp.where(lo, 0.0, prod), axis=-1, keepdims=True)
        r_refs, l_refs, dc_refs, dr_refs = (ra_ref, rb_ref), (la_ref, lb_ref), (dca_ref, dcb_ref), (dra_ref, drb_ref)
        dq_acc[...] = jnp.zeros(dq_acc.shape, F32)
        dra_ref[...] = jnp.zeros(dra_ref.shape, F32)
        drb_ref[...] = jnp.zeros(drb_ref.shape, F32)
        rel = lax.broadcasted_iota(jnp.int32, (STRIP, tk), 0) - lax.broadcasted_iota(jnp.int32, (STRIP, tk), 1)

        def chunk(kc, masked):
            start = pl.multiple_of(kc * tk, tk)
            kb = k_ref[pl.ds(start, tk), :]
            vb = v_ref[pl.ds(start, tk), :]
            for h in range(2):
                s_ref[h] = _dot(qms[h], kb, NT)
                dp_ref[h] = _dot(doms[h], vb, NT)
            for h in range(2):
                cs = r_refs[h][kc]
                col_sum = jnp.zeros((1, tk), F32)
                for st in range(n_strips):
                    rows = pl.ds(st * STRIP, STRIP)
                    s = s_ref[h, rows, :] - cs
                    if masked:
                        s = jnp.where(rel >= start - (i * tq + st * STRIP), s, NEG)
                    p = jnp.exp(s - l_refs[h][rows, :])
                    ds = p * (dp_ref[h, rows, :] - dsum_ref[h, rows, :])
                    p_ref[h, rows, :] = p.astype(BF16)
                    ds_ref[h, rows, :] = ds.astype(BF16)
                    col_sum = col_sum + jnp.sum(ds, axis=0, keepdims=True)
                    dr_refs[h][rows, :] += jnp.sum(ds, axis=-1, keepdims=True)
                dc_refs[h][kc] = dc_refs[h][kc] - col_sum
            dk = _dot(ds_ref[0], qms[0], TN) + _dot(ds_ref[1], qms[1], TN)
            dv = _dot(p_ref[0], doms[0], TN) + _dot(p_ref[1], doms[1], TN)
            dk_ref[pl.ds(start, tk), :] += dk
            dv_ref[pl.ds(start, tk), :] += dv
            for h in range(2):
                dq_acc[h] += _dot(ds_ref[h], kb, NN)

        n_full = (i * tq) // tk

        def full_chunk(kc, _):
            chunk(kc, False)
            return 0

        lax.fori_loop(0, n_full, full_chunk, 0)
        chunk(n_full, True)
        dq_ref[...] = jnp.where(lo, dq_acc[0], dq_acc[1]) * scale

    row = lambda off: pl.BlockSpec((None, nk, 1, tk), lambda h, i: (2 * h + off, 0, 0, 0))
    lspec = pl.BlockSpec((None, tq, 1), lambda h, i: (h, i, 0))
    qspec = pl.BlockSpec((tq, LANES), lambda h, i: (i, h))
    full = pl.BlockSpec((T, LANES), lambda h, i: (0, h))
    dcspec = pl.BlockSpec((None, nk, 1, tk), lambda h, i: (h, 0, 0, 0))
    grad = jax.ShapeDtypeStruct((T, Dh), F32)
    dc = jax.ShapeDtypeStruct((HP, nk, 1, tk), F32)
    dr = jax.ShapeDtypeStruct((HP, T, 1), F32)
    scratch = [pltpu.VMEM((2, tq, tk), F32), pltpu.VMEM((2, tq, tk), F32), pltpu.VMEM((2, tq, tk), BF16), pltpu.VMEM((2, tq, tk), BF16),
               pltpu.VMEM((2, tq, LANES), F32), pltpu.VMEM((2, tq, 1), F32)]
    return _call(
        body, name=name, grid=(HP, T // tq), out_shape=(grad, grad, grad, dc, dc, dr, dr),
        in_specs=[qspec, pl.BlockSpec((T, LANES), lambda h, i: (0, HP + h)), full, qspec, qspec, row(0), row(1), lspec, lspec],
        out_specs=(qspec, full, full, dcspec, dcspec, lspec, lspec),
        args=[qk, qk, v, o, do, crow, crow, lse_a, lse_b], scratch_shapes=scratch, carry=carry)


def _swa_block(n, q_ref, k_ref):
    qs = pl.multiple_of(n * WINDOW, WINDOW)
    ks = pl.multiple_of(jnp.maximum(n - 1, 0) * WINDOW, WINDOW)
    rel = (qs + lax.broadcasted_iota(jnp.int32, (WINDOW, 2 * WINDOW), 0)) - (ks + lax.broadcasted_iota(jnp.int32, (WINDOW, 2 * WINDOW), 1))
    valid = jnp.logical_and(rel >= 0, rel < WINDOW)
    return qs, ks, valid


def _swa_fwd(name, q, kd, vd, sinks, carry=None):
    T, Dh = q.shape
    HP = Dh // LANES

    def body(q_ref, k_ref, v_ref, sa_ref, sb_ref, o_ref, la_ref, lb_ref):
        lo = _lane_lo((WINDOW, LANES))

        def block(n, _):
            qs, ks, valid = _swa_block(n, q_ref, k_ref)
            q2 = q_ref[pl.ds(qs, WINDOW), :]
            kb = k_ref[pl.ds(ks, 2 * WINDOW), :]
            vb = v_ref[pl.ds(ks, 2 * WINDOW), :]
            res = []
            for sel, s_ref in ((lo, sa_ref), (jnp.logical_not(lo), sb_ref)):
                qm = _keep(sel, q2)
                sink = s_ref[...]
                s = jnp.where(valid, _dot(qm, kb, NT), NEG)
                m = jnp.maximum(jnp.max(s, axis=-1, keepdims=True), sink)
                p = jnp.exp(s - m)
                l = jnp.sum(p, axis=-1, keepdims=True) + jnp.exp(sink - m)
                res.append((_dot(p.astype(BF16), vb, NN) / l, m + jnp.log(l)))
            o_ref[pl.ds(qs, WINDOW), :] = jnp.where(lo, res[0][0], res[1][0])
            la_ref[pl.ds(qs, WINDOW), :] = res[0][1]
            lb_ref[pl.ds(qs, WINDOW), :] = res[1][1]
            return 0

        lax.fori_loop(0, T // WINDOW, block, 0, unroll=2)

    full = pl.BlockSpec((T, LANES), lambda h: (0, h))
    kv = pl.BlockSpec((T, LANES), lambda h: (0, h // 2))
    sink = lambda off: pl.BlockSpec((None, 1, 1), lambda h: (2 * h + off, 0, 0))
    lse = jax.ShapeDtypeStruct((HP, T, 1), F32)
    lspec = pl.BlockSpec((None, T, 1), lambda h: (h, 0, 0))
    return _call(
        body, name=name, grid=(HP,), out_shape=(jax.ShapeDtypeStruct((T, Dh), F32), lse, lse),
        in_specs=[full, kv, kv, sink(0), sink(1)], out_specs=(full, lspec, lspec),
        args=[q, kd, vd, sinks, sinks], carry=carry)


def _swa_bwd(name, q, kd, vd, sinks, o, do, lse_a, lse_b, carry=None):
    T, Dh = q.shape
    HP = Dh // LANES
    scale = HEAD_DIM ** -0.5

    def body(q_ref, k_ref, v_ref, sa_ref, sb_ref, o_ref, do_ref, la_ref, lb_ref, dq_ref, dk_ref, dv_ref, dsa_ref, dsb_ref):
        lo = _lane_lo((WINDOW, LANES))
        hi = jnp.logical_not(lo)
        dk_ref[...] = jnp.zeros_like(dk_ref)
        dv_ref[...] = jnp.zeros_like(dv_ref)

        def block(n, dsinks):
            qs, ks, valid = _swa_block(n, q_ref, k_ref)
            q2 = q_ref[pl.ds(qs, WINDOW), :]
            do2 = do_ref[pl.ds(qs, WINDOW), :]
            kb = k_ref[pl.ds(ks, 2 * WINDOW), :]
            vb = v_ref[pl.ds(ks, 2 * WINDOW), :]
            prod = do2.astype(F32) * o_ref[pl.ds(qs, WINDOW), :]
            dqs, new = [], []
            dk = jnp.zeros((2 * WINDOW, LANES), F32)
            dv = jnp.zeros((2 * WINDOW, LANES), F32)
            for sel, s_ref, l_ref, dsink in ((lo, sa_ref, la_ref, dsinks[0]), (hi, sb_ref, lb_ref, dsinks[1])):
                qm = _keep(sel, q2)
                dom = _keep(sel, do2)
                dsum = jnp.sum(jnp.where(sel, prod, 0.0), axis=-1, keepdims=True)
                lse = l_ref[pl.ds(qs, WINDOW), :]
                s = jnp.where(valid, _dot(qm, kb, NT), NEG)
                p = jnp.exp(s - lse)
                ds = p * (_dot(dom, vb, NT) - dsum)
                dsb = ds.astype(BF16)
                dqs.append(_dot(dsb, kb, NN))
                dk = dk + _dot(dsb, qm, TN)
                dv = dv + _dot(p.astype(BF16), dom, TN)
                new.append(dsink - jnp.sum(jnp.exp(s_ref[...] - lse) * dsum, axis=0, keepdims=True))
            dq_ref[pl.ds(qs, WINDOW), :] = jnp.where(lo, dqs[0], dqs[1]) * scale
            dk_ref[pl.ds(ks, 2 * WINDOW), :] += dk
            dv_ref[pl.ds(ks, 2 * WINDOW), :] += dv
            return tuple(new)

        dsa, dsb_ = lax.fori_loop(0, T // WINDOW, block, (jnp.zeros((1, 1), F32), jnp.zeros((1, 1), F32)), unroll=2)
        dsa_ref[...] = dsa
        dsb_ref[...] = dsb_

    full = pl.BlockSpec((T, LANES), lambda h: (0, h))
    kv = pl.BlockSpec((T, LANES), lambda h: (0, h // 2))
    sink = lambda off: pl.BlockSpec((None, 1, 1), lambda h: (2 * h + off, 0, 0))
    lspec = pl.BlockSpec((None, T, 1), lambda h: (h, 0, 0))
    dsink = pl.BlockSpec((None, 1, 1), lambda h: (h, 0, 0))
    grad = jax.ShapeDtypeStruct((T, Dh), F32)
    ds_shape = jax.ShapeDtypeStruct((HP, 1, 1), F32)
    return _call(
        body, name=name, grid=(HP,), out_shape=(grad, grad, grad, ds_shape, ds_shape),
        in_specs=[full, kv, kv, sink(0), sink(1), full, full, lspec, lspec],
        out_specs=(full, full, full, dsink, dsink),
        args=[q, kd, vd, sinks, sinks, o, do, lse_a, lse_b], carry=carry)


def _place():
    return lax.axis_index("x"), lax.axis_index("y"), lax.axis_index("c")


def _run_carry(name, carry):
    c_in, c_out = len(carry.inputs), len(carry.out_shapes)

    def body(*refs):
        ins, outs, scr = refs[:c_in], refs[c_in:c_in + c_out], refs[c_in + c_out:]
        carry.start(ins, outs, scr)
        carry.middle(ins, outs, scr)
        carry.finish(ins, outs, scr)

    return pl.pallas_call(
        body, out_shape=tuple(carry.out_shapes), in_specs=[_HBM] * c_in, out_specs=tuple([_HBM] * c_out),
        scratch_shapes=carry.scratch, name=name)(*carry.inputs)


def _gather_carry(shards):
    n = len(shards)

    def plan(ins, outs, scr):
        send, recv, local = scr
        x, y, c = _place()
        me, sibling = (x, y, c), (x, y, 1 - c)
        partner, other, diag = (x ^ c, y ^ (1 - c)), (x ^ (1 - c), y ^ c), (1 - x, 1 - y)

        def copy(w, k, block, to, src=None):
            slot = 4 * block[0] + 2 * block[1] + block[2]
            return pltpu.make_async_remote_copy(
                src_ref=outs[w].at[slot] if src is None else src, dst_ref=outs[w].at[slot],
                send_sem=send.at[w, k], recv_sem=recv.at[w, k], device_id=to, device_id_type=MESH)

        own = [pltpu.make_async_copy(ins[w], outs[w].at[4 * x + 2 * y + c], local.at[w]) for w in range(n)]
        return copy, own, me, sibling, partner, other, diag, c

    def start(ins, outs, scr):
        copy, own, me, sibling, partner, other, _, c = plan(ins, outs, scr)
        for cp in own:
            cp.start()
        for w in range(n):
            copy(w, 1, me, (*partner, c), src=ins[w]).start()
            copy(w, 2, me, (*other, c), src=ins[w]).start()
            copy(w, 0, me, sibling, src=ins[w]).start()

    def middle(ins, outs, scr):
        copy, _, me, sibling, partner, other, _, c = plan(ins, outs, scr)
        for w in range(n):
            copy(w, 1, (*partner, c), me).wait_recv()
            copy(w, 3, (*partner, c), (*other, c)).start()
            copy(w, 4, (*partner, c), sibling).start()

    def finish(ins, outs, scr):
        copy, own, me, sibling, partner, other, diag, c = plan(ins, outs, scr)
        for w in range(n):
            copy(w, 2, (*other, c), me).wait_recv()
            copy(w, 5, (*other, c), sibling).start()
        for w in range(n):
            copy(w, 3, (*diag, c), me).wait_recv()
            copy(w, 6, (*diag, c), sibling).start()
        for w in range(n):
            copy(w, 0, sibling, me).wait_recv()
            copy(w, 4, (*other, 1 - c), me).wait_recv()
            copy(w, 5, (*partner, 1 - c), me).wait_recv()
            copy(w, 6, (*diag, 1 - c), me).wait_recv()
        for w in range(n):
            sent = [copy(w, 0, me, sibling, src=ins[w]), copy(w, 1, me, (*partner, c), src=ins[w]), copy(w, 2, me, (*other, c), src=ins[w]),
                    copy(w, 3, (*partner, c), (*other, c)), copy(w, 4, (*partner, c), sibling), copy(w, 5, (*other, c), sibling),
                    copy(w, 6, (*diag, c), sibling)]
            for cp in sent:
                cp.wait_send()
        for cp in own:
            cp.wait()

    return _Carry(shards, [jax.ShapeDtypeStruct((N_DEV,) + s.shape, s.dtype) for s in shards],
                  [pltpu.SemaphoreType.DMA((n, 7)), pltpu.SemaphoreType.DMA((n, 7)), pltpu.SemaphoreType.DMA((n,))], start, finish, middle)


def _sibling_carry(grads):
    n = len(grads)

    def copies(ins, outs, scr):
        send, recv = scr
        x, y, c = _place()
        return [pltpu.make_async_remote_copy(
            src_ref=ins[w].at[2 * q + (1 - c)], dst_ref=outs[w].at[q], send_sem=send.at[w, q], recv_sem=recv.at[w, q],
            device_id=(x, y, 1 - c), device_id_type=MESH) for w in range(n) for q in range(4)]

    def start(ins, outs, scr):
        for cp in copies(ins, outs, scr):
            cp.start()

    def finish(ins, outs, scr):
        for cp in copies(ins, outs, scr):
            cp.wait()

    return _Carry(grads, [jax.ShapeDtypeStruct((4,) + g.shape[1:], g.dtype) for g in grads],
                  [pltpu.SemaphoreType.DMA((n, 4)), pltpu.SemaphoreType.DMA((n, 4))], start, finish)


def _chips_carry(sums):
    n = len(sums)

    def copies(ins, outs, scr):
        send, recv = scr
        x, y, c = _place()
        chips = [(1 - x, y), (x, 1 - y), (1 - x, 1 - y)]
        return [pltpu.make_async_remote_copy(
            src_ref=ins[w].at[2 * chip[0] + chip[1]], dst_ref=outs[w].at[k], send_sem=send.at[w, k], recv_sem=recv.at[w, k],
            device_id=(*chip, c), device_id_type=MESH) for w in range(n) for k, chip in enumerate(chips)]

    def start(ins, outs, scr):
        for cp in copies(ins, outs, scr):
            cp.start()

    def finish(ins, outs, scr):
        for cp in copies(ins, outs, scr):
            cp.wait()

    return _Carry(sums, [jax.ShapeDtypeStruct((3,) + s.shape[1:], s.dtype) for s in sums],
                  [pltpu.SemaphoreType.DMA((n, 3)), pltpu.SemaphoreType.DMA((n, 3))], start, finish)


def _gather_small(packed):
    R, C = packed.shape

    def body(in_ref, out_ref, send, recv):
        x, y, c = _place()
        mine = 4 * x + 2 * y + c
        out_ref[mine] = in_ref[...]
        copies = []
        for k in range(1, N_DEV):
            peer = (x ^ (k >> 2), y ^ ((k >> 1) & 1), c ^ (k & 1))
            copies.append(pltpu.make_async_remote_copy(
                src_ref=in_ref, dst_ref=out_ref.at[mine], send_sem=send.at[k - 1], recv_sem=recv.at[k - 1],
                device_id=peer, device_id_type=MESH))
        for cp in copies:
            cp.start()
        for cp in copies:
            cp.wait()

    vmem = pl.BlockSpec(memory_space=pltpu.VMEM)
    return pl.pallas_call(
        body, out_shape=jax.ShapeDtypeStruct((N_DEV, R, C), F32), in_specs=[vmem], out_specs=vmem,
        scratch_shapes=[pltpu.SemaphoreType.DMA((N_DEV - 1,)), pltpu.SemaphoreType.DMA((N_DEV - 1,))],
        name="small_grads_all_gather")(packed)


def _adamw(w, g, m, v):
    m = ADAM_B1 * m + (1.0 - ADAM_B1) * g
    v = ADAM_B2 * v + (1.0 - ADAM_B2) * (g * g)
    m_hat = m / (1.0 - ADAM_B1 ** ADAM_STEP)
    v_hat = v / (1.0 - ADAM_B2 ** ADAM_STEP)
    delta = -ADAM_LR * (m_hat / (jnp.sqrt(v_hat) + ADAM_EPS) + ADAM_WD * w)
    return delta, m, v


def _pair_add(name, grads, received, c_idx):
    _, R, C = grads.shape
    tr = _row_tile(R)

    def body(c_ref, g_ref, r_ref, o_ref):
        o_ref[...] = (g_ref[...].astype(F32) + r_ref[...].astype(F32)).astype(BF16)

    blk = pl.BlockSpec((None, tr, C), lambda q, i, c: (q, i, 0))
    return pl.pallas_call(
        body, out_shape=jax.ShapeDtypeStruct((4, R, C), BF16),
        grid_spec=pltpu.PrefetchScalarGridSpec(
            num_scalar_prefetch=1, grid=(4, R // tr),
            in_specs=[pl.BlockSpec((None, tr, C), lambda q, i, c: (2 * q + c[0], i, 0)), blk], out_specs=blk),
        name=name, compiler_params=_params(2))(c_idx, grads, received)


def _adam_shard(name, sums, received, w, m, v, chip_idx):
    R, C = w.shape
    tr = _row_tile(R, 128)
    tc = C if tr < R or C % (2 * LANES) else 2 * LANES

    def body(q_ref, s_ref, r_ref, w_ref, m_ref, v_ref, g_out, d_out, m_out, v_out):
        g = s_ref[...].astype(F32) + r_ref[0].astype(F32) + r_ref[1].astype(F32) + r_ref[2].astype(F32)
        delta, mn, vn = _adamw(w_ref[...], g, m_ref[...], v_ref[...])
        g_out[...] = g
        d_out[...] = delta
        m_out[...] = mn
        v_out[...] = vn

    blk = pl.BlockSpec((tr, tc), lambda i, j, q: (i, j))
    shape = jax.ShapeDtypeStruct((R, C), F32)
    return pl.pallas_call(
        body, out_shape=(shape,) * 4,
        grid_spec=pltpu.PrefetchScalarGridSpec(
            num_scalar_prefetch=1, grid=(R // tr, C // tc),
            in_specs=[pl.BlockSpec((None, tr, tc), lambda i, j, q: (q[0], i, j)), pl.BlockSpec((3, tr, tc), lambda i, j, q: (0, i, j)),
                      blk, blk, blk],
            out_specs=(blk,) * 4),
        name=name, compiler_params=_params(2))(chip_idx, sums, received, w, m, v)


def _adam_small(name, gathered, w, m, v):
    R, C = w.shape

    def body(ga_ref, w_ref, m_ref, v_ref, g_out, d_out, m_out, v_out):
        g = ga_ref[0]
        for d in range(1, N_DEV):
            g = g + ga_ref[d]
        delta, mn, vn = _adamw(w_ref[...], g, m_ref[...], v_ref[...])
        g_out[...] = g
        d_out[...] = delta
        m_out[...] = mn
        v_out[...] = vn

    full = pl.BlockSpec((R, C), lambda i: (0, 0))
    shape = jax.ShapeDtypeStruct((R, C), F32)
    return pl.pallas_call(
        body, out_shape=(shape,) * 4, grid=(1,),
        in_specs=[pl.BlockSpec((N_DEV, R, C), lambda i: (0, 0, 0)), full, full, full], out_specs=(full,) * 4,
        name=name, compiler_params=_params(1))(gathered, w, m, v)


def _pack_small(parts, D):
    g1, gmix, g2, gof, gos, bf, gqf, gkf, gqs, gks, sinks = [p.reshape(-1).astype(F32) for p in parts]
    row3 = jnp.concatenate([gof, gos])
    row4 = jnp.zeros((D,), F32)
    for slot, vec in enumerate((bf, gqf, gkf, gqs, gks, sinks)):
        row4 = lax.dynamic_update_slice(row4, vec, (slot * LANES,))
    zero = jnp.zeros((D,), F32)
    return jnp.stack([g1, gmix, g2, row3, row4, zero, zero, zero])


def _unpack_small(packed, D, H):
    Dh = D // 2
    row4 = packed[4]
    short = [row4[s * LANES:s * LANES + n] for s, n in enumerate((H, HEAD_DIM, HEAD_DIM, HEAD_DIM, HEAD_DIM, H))]
    vecs = [packed[0], packed[1], packed[2], packed[3, :Dh], packed[3, Dh:]] + short
    return [v[None, :] for v in vecs]


def kernel(x, positions, norm_ffn1_g, ffn1_w_gate, ffn1_w_up, ffn1_w_down, norm_mix_g, w_in, b_forget, fox_q_norm_g, fox_k_norm_g, swa_q_norm_g, swa_k_norm_g, swa_sinks, out_norm_fox_g, out_norm_swa_g, w_out, norm_ffn2_g, ffn2_w_gate, ffn2_w_up, ffn2_w_down, loss_target, m_norm_ffn1_g, m_ffn1_w_gate, m_ffn1_w_up, m_ffn1_w_down, m_norm_mix_g, m_w_in, m_b_forget, m_fox_q_norm_g, m_fox_k_norm_g, m_swa_q_norm_g, m_swa_k_norm_g, m_swa_sinks, m_out_norm_fox_g, m_out_norm_swa_g, m_w_out, m_norm_ffn2_g, m_ffn2_w_gate, m_ffn2_w_up, m_ffn2_w_down, v_norm_ffn1_g, v_ffn1_w_gate, v_ffn1_w_up, v_ffn1_w_down, v_norm_mix_g, v_w_in, v_b_forget, v_fox_q_norm_g, v_fox_k_norm_g, v_swa_q_norm_g, v_swa_k_norm_g, v_swa_sinks, v_out_norm_fox_g, v_out_norm_swa_g, v_w_out, v_norm_ffn2_g, v_ffn2_w_gate, v_ffn2_w_up, v_ffn2_w_down):
    xs = x[0]
    target = loss_target[0]
    T, D = xs.shape
    Dh = D // 2
    H = Dh // HEAD_DIM
    HP = H // 2
    KVW = (H // GQA_GROUP) * HEAD_DIM
    KVB = KVW // LANES
    MAIN = 4 * Dh + 2 * KVW
    F_OFF = 3 * Dh
    tm = min(ROW_TILE_CAP, T)
    tq = min(512, T)
    tk = min(512, T)
    nk = T // tk
    cx, cy, cc = _place()
    c_idx = jnp.reshape(cc, (1,)).astype(jnp.int32)
    chip_idx = jnp.reshape(2 * cx + cy, (1,)).astype(jnp.int32)

    tr = jnp.transpose
    big_w = [tr(ffn1_w_gate[0]), tr(ffn1_w_up[0]), ffn1_w_down[0], tr(w_in[0]), w_out[0], tr(ffn2_w_gate[0]), tr(ffn2_w_up[0]),
             ffn2_w_down[0]]
    big_m = [tr(m_ffn1_w_gate[0]), tr(m_ffn1_w_up[0]), m_ffn1_w_down[0], tr(m_w_in[0]), m_w_out[0], tr(m_ffn2_w_gate[0]),
             tr(m_ffn2_w_up[0]), m_ffn2_w_down[0]]
    big_v = [tr(v_ffn1_w_gate[0]), tr(v_ffn1_w_up[0]), v_ffn1_w_down[0], tr(v_w_in[0]), v_w_out[0], tr(v_ffn2_w_gate[0]),
             tr(v_ffn2_w_up[0]), v_ffn2_w_down[0]]
    transposed = {"ffn1_w_gate", "ffn1_w_up", "w_in", "ffn2_w_gate", "ffn2_w_up"}
    names = ["ffn1_w_gate", "ffn1_w_up", "ffn1_w_down", "w_in", "w_out", "ffn2_w_gate", "ffn2_w_up", "ffn2_w_down"]
    sh = dict(zip(names, [w.astype(BF16) for w in big_w]))
    lane = jnp.arange(LANES)
    inv_freq = ROPE_THETA ** (-(2.0 * (lane % (HEAD_DIM // 2))).astype(F32) / HEAD_DIM)
    ang = positions[0].astype(F32)[:, None] * inv_freq[None, :]
    cos_t = jnp.cos(ang)
    sin_t = jnp.where((lane & (HEAD_DIM // 2)) == 0, -1.0, 1.0)[None, :] * jnp.sin(ang)
    rope = (cos_t, sin_t)

    def pair_gain(g, blocks):
        return jnp.tile(jnp.concatenate([g[0], g[0]])[None, None, :], (blocks, 1, 1))

    n1, (wg1,) = _rmsnorm_fwd("ffn1_norm", xs, norm_ffn1_g, tm, carry=_gather_carry([sh["ffn1_w_gate"]]))
    a1, (wu1,) = _ffn_gate("ffn1_gate", n1, wg1, tm, carry=_gather_carry([sh["ffn1_w_up"]]))
    (b1, hm1), (wd1,) = _ffn_up_only("ffn1_up", n1, wu1, a1, tm, carry=_gather_carry([sh["ffn1_w_down"]]))
    h1, (win_g,) = _ffn_down("ffn1_down", hm1, wd1, xs, tm, carry=_gather_carry([sh["w_in"]]))
    n_in = win_g.shape[1]
    win_t = win_g.reshape(N_DEV * n_in, D)
    win_main = jnp.concatenate([win_t[:F_OFF], win_t[F_OFF + H:]], axis=0)
    win_f = jnp.pad(win_t[F_OFF:F_OFF + H], ((0, LANES - H), (0, 0)))

    u = _rmsnorm_fwd("mix_norm", h1, norm_mix_g, tm)
    proj, (wout_g,) = _mm("mix_proj", u, win_main, tm, MAIN // 9, dims=NT, carry=_gather_carry([sh["w_out"]]))
    wout = wout_g.reshape(D, D)
    proj_f = _mm("mix_proj_forget", u, win_f, tm, LANES, dims=NT)
    scale = HEAD_DIM ** -0.5
    fox_gains = jnp.concatenate([pair_gain(fox_q_norm_g, HP), pair_gain(fox_k_norm_g, HP)])
    qk_f = _headnorm_fwd_scaled("fox_qk_norm", proj, 0, 2 * HP, fox_gains, T, scale, HP)
    v_f = proj[:, 2 * Dh:3 * Dh].astype(BF16)
    c_t, sg_t = _forget_fwd("forget_gates", proj_f[:, :H].T, b_forget.reshape(H, 1))
    crow = c_t.reshape(H, nk, 1, tk)
    (o_fox, lse_fa, lse_fb), (wg2, wu2) = _fox_fwd("fox_attention", qk_f, v_f, crow, tq, tk,
                                                   carry=_gather_carry([sh["ffn2_w_gate"], sh["ffn2_w_up"]]))

    swa_q_gains = pair_gain(swa_q_norm_g, HP)
    swa_k_gains = pair_gain(swa_k_norm_g, KVB)
    q_s = _headnorm_fwd("swa_q_norm", proj, 3 * HP, HP, swa_q_gains, T, scale, rope=rope)
    k_d = _headnorm_fwd("swa_k_norm", proj, 4 * HP, KVB, swa_k_gains, T, 1.0, rope=rope, dup=True)
    v_s = proj[:, 4 * Dh + KVW:].astype(BF16).reshape(T, H // GQA_GROUP, 1, HEAD_DIM)
    v_d = jnp.broadcast_to(v_s, (T, H // GQA_GROUP, 2, HEAD_DIM)).reshape(T, 2 * KVW)
    sinks3 = swa_sinks.reshape(H, 1, 1)
    o_swa, lse_sa, lse_sb = _swa_fwd("swa_attention", q_s, k_d, v_d, sinks3)

    on = _outnorm_fwd("out_norm", o_fox, o_swa, out_norm_fox_g, out_norm_swa_g, tm)
    h2 = _mm("mix_out", on, wout, tm, min(512, D), resid=h1)

    n2 = _rmsnorm_fwd("ffn2_norm", h2, norm_ffn2_g, tm)
    (a2, b2, hm2), (wd2,) = _ffn_up("ffn2_up", n2, wg2, wu2, tm, carry=_gather_carry([sh["ffn2_w_down"]]))
    y = _ffn_down("ffn2_down", hm2, wd2, h2, tm)
    dy, dyh, sq = _loss_grad("loss_grad", y, target, min(256, T))
    loss = lax.psum(0.5 * sq[0, 0] / D, ("x", "y", "c"))

    J, Fs, _ = wg2.shape
    aspec = pl.BlockSpec((None, tm, Fs), lambda i, j: (j, i, 0))
    wspec = pl.BlockSpec((None, Fs, D), lambda i, j: (j, 0, 0))

    def pair_sums(keys, grads, received):
        return [_pair_add("sum_" + nm, g, r, c_idx) for nm, g, r in zip(keys, grads, received)]

    dwd2 = _wgrad_down("ffn2_wgrad_down", hm2, dyh, min(1024, D))
    (da2, db2), (sib_d2,) = _ffn_bwd_mid("ffn2_bwd_mid", dyh, wd2, a2, b2, tm, carry=_sibling_carry([dwd2]))
    (sum_wd2,) = pair_sums(names[7:8], [dwd2], [sib_d2])
    (dwg2, dwu2), (rc_wd2,) = _wgrad_up("ffn2_wgrad_up", n2, da2, db2, min(1024, D), carry=_chips_carry([sum_wd2]))
    dn2, sib2 = _reduce_mm("ffn2_bwd_in", [(da2, aspec, wg2, wspec), (db2, aspec, wu2, wspec)], [], NN, T, D, tm, J,
                           carry=_sibling_carry([dwg2, dwu2]))
    dh2, dg_ffn2, dh2b = _rmsnorm_bwd("ffn2_norm_bwd", dn2, h2, norm_ffn2_g, dy, min(256, T), 1.0)
    sum_wg2, sum_wu2 = pair_sums(names[5:7], [dwg2, dwu2], sib2)

    dwout = _wgrad_2d("mix_out_wgrad", on, dh2b, min(512, D), min(1024, D))
    dwout_g = dwout.reshape(N_DEV, D // N_DEV, D)
    do_fox, dg_of = _outnorm_bwd("out_norm_bwd_fox", dh2b, wout, 0, o_fox, out_norm_fox_g, tm)
    do_swa, dg_os = _outnorm_bwd("out_norm_bwd_swa", dh2b, wout, 1, o_swa, out_norm_swa_g, tm)

    (dq_f, dk_f, dv_f, dc_a, dc_b, dr_a, dr_b), (rc_wg2, sib_wout) = _fox_bwd(
        "fox_attention_bwd", qk_f, v_f, o_fox, do_fox, crow, lse_fa, lse_fb, tq, tk,
        carry=_join(_chips_carry([sum_wg2]), _sibling_carry([dwout_g])))
    (sum_wout,) = pair_sums(names[4:5], [dwout_g], [sib_wout])
    dqk_f = jnp.concatenate([dq_f, dk_f], axis=1)
    dqk_raw, dg_fox = _headnorm_bwd("fox_qk_norm_bwd", dqk_f, proj, 0, 2 * HP, fox_gains, HP, T, 1.0)
    dct = jnp.stack([dc_a.reshape(HP, T), dc_b.reshape(HP, T)], axis=1).reshape(H, T)
    drt = jnp.stack([dr_a.reshape(HP, T), dr_b.reshape(HP, T)], axis=1).reshape(H, T)
    dz_t, db_f = _forget_bwd("forget_gates_bwd", dct, drt, sg_t)

    (dq_s, dk_p, dv_p, dsink_a, dsink_b), (rc_wu2,) = _swa_bwd(
        "swa_attention_bwd", q_s, k_d, v_d, sinks3, o_swa, do_swa, lse_sa, lse_sb, carry=_chips_carry([sum_wu2]))
    dqs_raw, dg_sq = _headnorm_bwd("swa_q_norm_bwd", dq_s, proj, 3 * HP, HP, swa_q_gains, HP, T, 1.0, rope=rope)
    dks_raw, dg_sk = _headnorm_bwd("swa_k_norm_bwd", dk_p, proj, 4 * HP, KVB, swa_k_gains, KVB, T, 1.0, rope=rope, fold=True)
    dvs_raw, _ = _headnorm_bwd("swa_v_fold", dv_p, None, 0, KVB, None, KVB, T, 1.0, fold=True, norm=False)

    dproj = jnp.concatenate([dqk_raw, dv_f.astype(BF16), dqs_raw, dks_raw, dvs_raw], axis=1)
    dproj_f = jnp.pad(dz_t.T, ((0, 0), (0, LANES - H))).astype(BF16)
    dwin_main, (rc_wout,) = _wgrad_2d("mix_proj_wgrad", dproj, u, MAIN // 9, min(1024, D), carry=_chips_carry([sum_wout]))
    dwin_f = _wgrad_2d("mix_proj_forget_wgrad", dproj_f, u, LANES, min(1024, D))
    dwin_t = jnp.concatenate([dwin_main[:F_OFF], dwin_f[:H], dwin_main[F_OFF:]], axis=0)
    dwin_g = dwin_t.reshape(N_DEV, n_in, D)
    tkb = MAIN // 9
    du, (sib_win,) = _reduce_mm(
        "mix_bwd_in",
        [(dproj, pl.BlockSpec((tm, tkb), lambda i, r: (i, r)), win_main, pl.BlockSpec((tkb, D), lambda i, r: (r, 0)))],
        [(dproj_f, pl.BlockSpec((tm, LANES), lambda i, r: (i, 0)), win_f, pl.BlockSpec((LANES, D), lambda i, r: (0, 0)))],
        NN, T, D, tm, 9, carry=_sibling_carry([dwin_g]))
    dh1, dg_mix, dh1h = _rmsnorm_bwd("mix_norm_bwd", du, h1, norm_mix_g, dh2, min(256, T), 0.5)
    (sum_win,) = pair_sums(names[3:4], [dwin_g], [sib_win])

    (da1, db1), (rc_win,) = _ffn_bwd_mid("ffn1_bwd_mid", dh1h, wd1, a1, b1, tm, carry=_chips_carry([sum_win]))
    dwg1 = _wgrad_down("ffn1_wgrad_gate", da1, n1, min(1024, D))
    dwu1, (sib_g,) = _wgrad_down("ffn1_wgrad_up", db1, n1, min(1024, D), carry=_sibling_carry([dwg1]))
    (sum_wg1,) = pair_sums(names[0:1], [dwg1], [sib_g])
    dwd1, (rc_wg1, sib_u) = _wgrad_down("ffn1_wgrad_down", hm1, dh1h, min(1024, D),
                                        carry=_join(_chips_carry([sum_wg1]), _sibling_carry([dwu1])))
    (sum_wu1,) = pair_sums(names[1:2], [dwu1], [sib_u])
    dn1, (rc_wu1, sib_d) = _reduce_mm(
        "ffn1_bwd_in", [(da1, aspec, wg1, wspec), (db1, aspec, wu1, wspec)], [], NN, T, D, tm, J,
        carry=_join(_chips_carry([sum_wu1]), _sibling_carry([dwd1])))
    (sum_wd1,) = pair_sums(names[2:3], [dwd1], [sib_d])
    (rc_wd1,) = _run_carry("grads_exchange_chips", _chips_carry([sum_wd1]))
    dx, dg_ffn1 = _rmsnorm_bwd("ffn1_norm_bwd", dn1, xs, norm_ffn1_g, dh1, min(256, T), None)

    chip_sums = [sum_wg1, sum_wu1, sum_wd1, sum_win, sum_wout, sum_wg2, sum_wu2, sum_wd2]
    from_chips = [rc_wg1, rc_wu1, rc_wd1, rc_win, rc_wout, rc_wg2, rc_wu2, rc_wd2]
    big_out = [_adam_shard("adam_" + nm, s, r, w, m, v, chip_idx)
               for nm, s, r, w, m, v in zip(names, chip_sums, from_chips, big_w, big_m, big_v)]

    dsinks = jnp.stack([dsink_a.reshape(HP), dsink_b.reshape(HP)], axis=1).reshape(H)
    small_g = [dg_ffn1, dg_mix, dg_ffn2, dg_of, dg_os, db_f, dg_fox[0, 0, :HEAD_DIM], dg_fox[1, 0, :HEAD_DIM],
               dg_sq[0, 0, :HEAD_DIM], dg_sk[0, 0, :HEAD_DIM], dsinks]
    small_w = [norm_ffn1_g, norm_mix_g, norm_ffn2_g, out_norm_fox_g, out_norm_swa_g, b_forget, fox_q_norm_g, fox_k_norm_g,
               swa_q_norm_g, swa_k_norm_g, swa_sinks]
    small_m = [m_norm_ffn1_g, m_norm_mix_g, m_norm_ffn2_g, m_out_norm_fox_g, m_out_norm_swa_g, m_b_forget, m_fox_q_norm_g,
               m_fox_k_norm_g, m_swa_q_norm_g, m_swa_k_norm_g, m_swa_sinks]
    small_v = [v_norm_ffn1_g, v_norm_mix_g, v_norm_ffn2_g, v_out_norm_fox_g, v_out_norm_swa_g, v_b_forget, v_fox_q_norm_g,
               v_fox_k_norm_g, v_swa_q_norm_g, v_swa_k_norm_g, v_swa_sinks]
    gathered = _gather_small(_pack_small(small_g, D))
    small_out = _adam_small("adam_small", gathered, _pack_small(small_w, D), _pack_small(small_m, D), _pack_small(small_v, D))
    small_out = [_unpack_small(p, D, H) for p in small_out]

    order = ["norm_ffn1_g", "ffn1_w_gate", "ffn1_w_up", "ffn1_w_down", "norm_mix_g", "w_in", "b_forget", "fox_q_norm_g", "fox_k_norm_g",
             "swa_q_norm_g", "swa_k_norm_g", "swa_sinks", "out_norm_fox_g", "out_norm_swa_g", "w_out", "norm_ffn2_g",
             "ffn2_w_gate", "ffn2_w_up", "ffn2_w_down"]
    small_names = ["norm_ffn1_g", "norm_mix_g", "norm_ffn2_g", "out_norm_fox_g", "out_norm_swa_g", "b_forget", "fox_q_norm_g",
                   "fox_k_norm_g", "swa_q_norm_g", "swa_k_norm_g", "swa_sinks"]
    result = [loss, dx[None]]
    for kind in range(4):
        for nm in order:
            if nm in names:
                leaf = big_out[names.index(nm)][kind]
                result.append((tr(leaf) if nm in transposed else leaf)[None])
            else:
                result.append(small_out[kind][small_names.index(nm)])
    return tuple(result)


def _headnorm_fwd_scaled(name, proj, col_off, ncb, gains, tm, scale, n_scaled):
    T = proj.shape[0]

    def body(x_ref, g_ref, o_ref):
        xv = x_ref[...]
        lo = _lane_lo(xv.shape)
        y = xv * _head_rstd(xv, lo) * g_ref[...]
        y = y * jnp.where(pl.program_id(0) < n_scaled, scale, 1.0)
        o_ref[...] = y.astype(BF16)

    return pl.pallas_call(
        body, out_shape=jax.ShapeDtypeStruct((T, ncb * LANES), BF16), grid=(ncb, T // tm),
        in_specs=[pl.BlockSpec((tm, LANES), lambda c, i: (i, col_off + c)), pl.BlockSpec((None, 1, LANES), lambda c, i: (c, 0, 0))],
        out_specs=pl.BlockSpec((tm, LANES), lambda c, i: (i, c)), name=name, compiler_params=_params(2))(proj, gains)
```

```python
import functools

import jax
import jax.numpy as jnp
from jax import lax
from jax.experimental import pallas as pl
from jax.experimental.pallas import tpu as pltpu

F32 = jnp.float32
BF16 = jnp.bfloat16

HEAD_DIM = 64
LANES = 128
WINDOW = 128
GQA_GROUP = 4
EPS = 1e-6
ROPE_THETA = 10000.0
ADAM_LR = 0.001
ADAM_B1 = 0.9
ADAM_B2 = 0.999
ADAM_EPS = 1e-08
ADAM_WD = 0.01
ADAM_STEP = 10
N_DEV = 8
NEG = -1e30
VMEM_LIMIT_V7X = 48 * 1024 * 1024
ROW_TILE_CAP = 512
MESH = pl.DeviceIdType.MESH

NN = (((1,), (0,)), ((), ()))
NT = (((1,), (1,)), ((), ()))
TN = (((0,), (0,)), ((), ()))


def _dot(a, b, dims):
    return lax.dot_general(a, b, dims, preferred_element_type=F32)


def _params(n_axes):
    return pltpu.CompilerParams(dimension_semantics=("arbitrary",) * n_axes, vmem_limit_bytes=VMEM_LIMIT_V7X)


def _row_tile(rows, cap=ROW_TILE_CAP):
    best = None
    for t in range(16, min(rows, cap) + 1, 16):
        if rows % t == 0:
            best = t
    return best or rows


def _lane_lo(shape):
    return lax.broadcasted_iota(jnp.int32, shape, len(shape) - 1) < HEAD_DIM


def _keep(sel, x):
    return jnp.where(sel, x.astype(F32), 0.0).astype(BF16)


_HBM = pl.BlockSpec(memory_space=pltpu.HBM)


class _Carry:
    def __init__(self, inputs, out_shapes, scratch, start, finish, middle=None):
        self.inputs, self.out_shapes, self.scratch = list(inputs), list(out_shapes), list(scratch)
        self.start, self.finish, self.middle = start, finish, middle or (lambda ins, outs, scr: None)


def _join(*carries):
    def hook(which):
        def run(ins, outs, scr):
            i = o = s = 0
            for c in carries:
                ni, no, ns = len(c.inputs), len(c.out_shapes), len(c.scratch)
                getattr(c, which)(ins[i:i + ni], outs[o:o + no], scr[s:s + ns])
                i, o, s = i + ni, o + no, s + ns
        return run

    return _Carry([a for c in carries for a in c.inputs], [a for c in carries for a in c.out_shapes],
                  [a for c in carries for a in c.scratch], hook("start"), hook("finish"), hook("middle"))


def _call(body, *, name, grid, in_specs, out_specs, out_shape, args, scratch_shapes=(), carry=None):
    params = _params(len(grid))
    if carry is None:
        return pl.pallas_call(body, out_shape=out_shape, grid=grid, in_specs=list(in_specs), out_specs=out_specs,
                              scratch_shapes=list(scratch_shapes), name=name, compiler_params=params)(*args)
    single = not isinstance(out_shape, (tuple, list))
    shapes = (out_shape,) if single else tuple(out_shape)
    specs = (out_specs,) if single else tuple(out_specs)
    n_in, n_out, n_scr = len(args), len(shapes), len(scratch_shapes)
    c_in, c_out = len(carry.inputs), len(carry.out_shapes)

    def wrapped(*refs):
        ins, c_ins = refs[:n_in], refs[n_in:n_in + c_in]
        o0 = n_in + c_in
        outs, c_outs = refs[o0:o0 + n_out], refs[o0 + n_out:o0 + n_out + c_out]
        s0 = o0 + n_out + c_out
        scr, c_scr = refs[s0:s0 + n_scr], refs[s0 + n_scr:]
        step, total = pl.program_id(0), grid[0]
        for ax in range(1, len(grid)):
            step, total = step * grid[ax] + pl.program_id(ax), total * grid[ax]

        @pl.when(step == 0)
        def _():
            carry.start(c_ins, c_outs, c_scr)

        @pl.when(step == total // 2)
        def _():
            carry.middle(c_ins, c_outs, c_scr)

        body(*ins, *outs, *scr)

        @pl.when(step == total - 1)
        def _():
            carry.finish(c_ins, c_outs, c_scr)

    res = pl.pallas_call(
        wrapped, out_shape=shapes + tuple(carry.out_shapes), grid=grid, in_specs=list(in_specs) + [_HBM] * c_in,
        out_specs=specs + (_HBM,) * c_out, scratch_shapes=list(scratch_shapes) + carry.scratch, name=name,
        compiler_params=params)(*args, *carry.inputs)
    main = res[:n_out]
    return (main[0] if single else tuple(main)), tuple(res[n_out:])


def _rms_bwd(dn, x, g):
    r = lax.rsqrt(jnp.mean(x * x, axis=-1, keepdims=True) + EPS)
    xh = x * r
    dxh = dn * g
    dx = r * (dxh - xh * jnp.mean(dxh * xh, axis=-1, keepdims=True))
    return dx, jnp.sum(dn * xh, axis=0, keepdims=True)


def _rmsnorm_fwd(name, x, g, tm, carry=None):
    T, D = x.shape

    def body(x_ref, g_ref, o_ref):
        xf = x_ref[...]
        r = lax.rsqrt(jnp.mean(xf * xf, axis=-1, keepdims=True) + EPS)
        o_ref[...] = (xf * r * g_ref[...]).astype(BF16)

    return _call(
        body, name=name, grid=(T // tm,), out_shape=jax.ShapeDtypeStruct((T, D), BF16),
        in_specs=[pl.BlockSpec((tm, D), lambda i: (i, 0)), pl.BlockSpec((1, D), lambda i: (0, 0))],
        out_specs=pl.BlockSpec((tm, D), lambda i: (i, 0)), args=[x, g], carry=carry)


def _outnorm_fwd(name, o_fox, o_swa, g_fox, g_swa, tm):
    T, Dh = o_fox.shape

    def body(a_ref, b_ref, ga_ref, gb_ref, o_ref):
        for ref, g_ref, lo in ((a_ref, ga_ref, 0), (b_ref, gb_ref, Dh)):
            xf = ref[...]
            r = lax.rsqrt(jnp.mean(xf * xf, axis=-1, keepdims=True) + EPS)
            o_ref[:, lo:lo + Dh] = (xf * r * g_ref[...]).astype(BF16)

    row = pl.BlockSpec((tm, Dh), lambda i: (i, 0))
    gain = pl.BlockSpec((1, Dh), lambda i: (0, 0))
    return pl.pallas_call(
        body, out_shape=jax.ShapeDtypeStruct((T, 2 * Dh), BF16), grid=(T // tm,),
        in_specs=[row, row, gain, gain], out_specs=pl.BlockSpec((tm, 2 * Dh), lambda i: (i, 0)),
        name=name, compiler_params=_params(1))(o_fox, o_swa, g_fox, g_swa)


def _outnorm_bwd(name, dhb, wout, half, o, g, tm):
    T, D = dhb.shape
    Dh = o.shape[1]

    def body(a_ref, w_ref, o_ref, g_ref, do_ref, dg_ref):
        don = _dot(a_ref[...], w_ref[...], NT)
        dx, dg = _rms_bwd(don, o_ref[...], g_ref[...])
        do_ref[...] = dx.astype(BF16)

        @pl.when(pl.program_id(0) == 0)
        def _():
            dg_ref[...] = dg

        @pl.when(pl.program_id(0) > 0)
        def _():
            dg_ref[...] += dg

    return pl.pallas_call(
        body, out_shape=(jax.ShapeDtypeStruct((T, Dh), BF16), jax.ShapeDtypeStruct((1, Dh), F32)), grid=(T // tm,),
        in_specs=[pl.BlockSpec((tm, D), lambda i: (i, 0)), pl.BlockSpec((Dh, D), lambda i: (half, 0)),
                  pl.BlockSpec((tm, Dh), lambda i: (i, 0)), pl.BlockSpec((1, Dh), lambda i: (0, 0))],
        out_specs=(pl.BlockSpec((tm, Dh), lambda i: (i, 0)), pl.BlockSpec((1, Dh), lambda i: (0, 0))),
        name=name, compiler_params=_params(1))(dhb, wout, o, g)


def _mm(name, a, b, tm, tn, dims=NN, resid=None, carry=None):
    M, K = a.shape
    transposed = dims == NT
    N = b.shape[0] if transposed else b.shape[1]

    def body(*refs):
        if resid is None:
            a_ref, b_ref, o_ref = refs
            o_ref[...] = _dot(a_ref[...], b_ref[...], dims)
        else:
            a_ref, b_ref, r_ref, o_ref = refs
            o_ref[...] = r_ref[...] + _dot(a_ref[...], b_ref[...], dims)

    ospec = pl.BlockSpec((tm, tn), lambda n, i: (i, n))
    bspec = pl.BlockSpec((tn, K), lambda n, i: (n, 0)) if transposed else pl.BlockSpec((K, tn), lambda n, i: (0, n))
    in_specs = [pl.BlockSpec((tm, K), lambda n, i: (i, 0)), bspec]
    args = [a, b]
    if resid is not None:
        in_specs.append(ospec)
        args.append(resid)
    return _call(body, name=name, grid=(N // tn, M // tm), in_specs=in_specs, out_specs=ospec,
                 out_shape=jax.ShapeDtypeStruct((M, N), F32), args=args, carry=carry)


def _wgrad_2d(name, a, b, tmm, tn, carry=None):
    T, M = a.shape
    N = b.shape[1]

    def body(a_ref, b_ref, o_ref):
        o_ref[...] = _dot(a_ref[...], b_ref[...], TN).astype(BF16)

    return _call(
        body, name=name, grid=(M // tmm, N // tn), out_shape=jax.ShapeDtypeStruct((M, N), BF16),
        in_specs=[pl.BlockSpec((T, tmm), lambda m, n: (0, m)), pl.BlockSpec((T, tn), lambda m, n: (0, n))],
        out_specs=pl.BlockSpec((tmm, tn), lambda m, n: (m, n)), args=[a, b], carry=carry)


def _wgrad_down(name, hm, df, tn, carry=None):
    J, T, Fs = hm.shape
    D = df.shape[1]

    def body(a_ref, b_ref, o_ref):
        o_ref[...] = _dot(a_ref[...], b_ref[...], TN).astype(BF16)

    return _call(
        body, name=name, grid=(J, D // tn), out_shape=jax.ShapeDtypeStruct((J, Fs, D), BF16),
        in_specs=[pl.BlockSpec((None, T, Fs), lambda j, n: (j, 0, 0)), pl.BlockSpec((T, tn), lambda j, n: (0, n))],
        out_specs=pl.BlockSpec((None, Fs, tn), lambda j, n: (j, 0, n)), args=[hm, df], carry=carry)


def _wgrad_up(name, n, da, db, tn, carry=None):
    T, D = n.shape
    J, _, Fs = da.shape

    def body(n_ref, da_ref, db_ref, og_ref, ou_ref):
        nv = n_ref[...]
        og_ref[...] = _dot(da_ref[...], nv, TN).astype(BF16)
        ou_ref[...] = _dot(db_ref[...], nv, TN).astype(BF16)

    act = pl.BlockSpec((None, T, Fs), lambda j, m: (j, 0, 0))
    out = pl.BlockSpec((None, Fs, tn), lambda j, m: (j, 0, m))
    shape = jax.ShapeDtypeStruct((J, Fs, D), BF16)
    return _call(
        body, name=name, grid=(J, D // tn), out_shape=(shape, shape),
        in_specs=[pl.BlockSpec((T, tn), lambda j, m: (0, m)), act, act], out_specs=(out, out),
        args=[n, da, db], carry=carry)


def _reduce_mm(name, pairs, once, dims, T, D, tm, steps, carry=None):
    n_pairs = len(pairs)
    n_once = len(once)

    def body(*refs):
        pr = refs[:2 * n_pairs]
        on = refs[2 * n_pairs:2 * (n_pairs + n_once)]
        o_ref, acc = refs[-2:]
        r = pl.program_id(1)
        part = _dot(pr[0][...], pr[1][...], dims)
        for p in range(1, n_pairs):
            part = part + _dot(pr[2 * p][...], pr[2 * p + 1][...], dims)

        @pl.when(r == 0)
        def _():
            acc[...] = part

        @pl.when(r > 0)
        def _():
            acc[...] += part

        @pl.when(r == steps - 1)
        def _():
            dn = acc[...]
            for p in range(n_once):
                dn = dn + _dot(on[2 * p][...], on[2 * p + 1][...], dims)
            o_ref[...] = dn

    in_specs, args = [], []
    for a, a_spec, w, w_spec in list(pairs) + list(once):
        in_specs += [a_spec, w_spec]
        args += [a, w]
    row = pl.BlockSpec((tm, D), lambda i, r: (i, 0))
    return _call(body, name=name, grid=(T // tm, steps), in_specs=in_specs, out_specs=row, out_shape=jax.ShapeDtypeStruct((T, D), F32),
                 args=args, scratch_shapes=[pltpu.VMEM((tm, D), F32)], carry=carry)


def _rmsnorm_bwd(name, dn, x, g, dh, tm, bf16_scale, carry=None):
    T, D = x.shape
    emit_bf16 = bf16_scale is not None

    def body(dn_ref, x_ref, g_ref, dh_ref, *outs):
        dxn, dg = _rms_bwd(dn_ref[...], x_ref[...], g_ref[...])
        dx = dh_ref[...] + dxn
        outs[0][...] = dx
        if emit_bf16:
            outs[2][...] = (bf16_scale * dx).astype(BF16)

        @pl.when(pl.program_id(0) == 0)
        def _():
            outs[1][...] = dg

        @pl.when(pl.program_id(0) > 0)
        def _():
            outs[1][...] += dg

    row = pl.BlockSpec((tm, D), lambda i: (i, 0))
    gain = pl.BlockSpec((1, D), lambda i: (0, 0))
    out_shape = [jax.ShapeDtypeStruct((T, D), F32), jax.ShapeDtypeStruct((1, D), F32)]
    out_specs = [row, gain]
    if emit_bf16:
        out_shape.append(jax.ShapeDtypeStruct((T, D), BF16))
        out_specs.append(row)
    return _call(body, name=name, grid=(T // tm,), in_specs=[row, row, gain, row], out_specs=tuple(out_specs),
                 out_shape=tuple(out_shape), args=[dn, x, g, dh], carry=carry)


def _loss_grad(name, y, target, tm):
    T, D = y.shape

    def body(y_ref, t_ref, dy_ref, dyh_ref, sq_ref):
        diff = y_ref[...] - t_ref[...]
        sq = jnp.sum(jnp.sum(diff * diff, axis=1, keepdims=True), axis=0, keepdims=True)
        dy = diff * (1.0 / D)
        dy_ref[...] = dy
        dyh_ref[...] = (0.5 * dy).astype(BF16)

        @pl.when(pl.program_id(0) == 0)
        def _():
            sq_ref[...] = sq

        @pl.when(pl.program_id(0) > 0)
        def _():
            sq_ref[...] += sq

    row = pl.BlockSpec((tm, D), lambda i: (i, 0))
    return pl.pallas_call(
        body, out_shape=(jax.ShapeDtypeStruct((T, D), F32), jax.ShapeDtypeStruct((T, D), BF16), jax.ShapeDtypeStruct((1, 1), F32)),
        grid=(T // tm,), in_specs=[row, row], out_specs=(row, row, pl.BlockSpec((1, 1), lambda i: (0, 0))),
        name=name, compiler_params=_params(1))(y, target)


def _ffn_up(name, n, wg, wu, tm, carry=None):
    T, D = n.shape
    J, Fs, _ = wg.shape

    def body(n_ref, wg_ref, wu_ref, a_ref, b_ref, h_ref):
        xv = n_ref[...]
        a = _dot(xv, wg_ref[...], NT)
        b = _dot(xv, wu_ref[...], NT)
        a_ref[...] = a.astype(BF16)
        b_ref[...] = b.astype(BF16)
        h_ref[...] = (a * jax.nn.sigmoid(a) * b).astype(BF16)

    act = jax.ShapeDtypeStruct((J, T, Fs), BF16)
    wspec = pl.BlockSpec((None, Fs, D), lambda j, i: (j, 0, 0))
    aspec = pl.BlockSpec((None, tm, Fs), lambda j, i: (j, i, 0))
    return _call(
        body, name=name, grid=(J, T // tm), out_shape=(act, act, act),
        in_specs=[pl.BlockSpec((tm, D), lambda j, i: (i, 0)), wspec, wspec], out_specs=(aspec, aspec, aspec),
        args=[n, wg, wu], carry=carry)


def _ffn_gate(name, n, wg, tm, carry=None):
    T, D = n.shape
    J, Fs, _ = wg.shape

    def body(n_ref, wg_ref, a_ref):
        a_ref[...] = _dot(n_ref[...], wg_ref[...], NT).astype(BF16)

    aspec = pl.BlockSpec((None, tm, Fs), lambda j, i: (j, i, 0))
    return _call(
        body, name=name, grid=(J, T // tm), out_shape=jax.ShapeDtypeStruct((J, T, Fs), BF16),
        in_specs=[pl.BlockSpec((tm, D), lambda j, i: (i, 0)), pl.BlockSpec((None, Fs, D), lambda j, i: (j, 0, 0))],
        out_specs=aspec, args=[n, wg], carry=carry)


def _ffn_up_only(name, n, wu, a, tm, carry=None):
    T, D = n.shape
    J, Fs, _ = wu.shape

    def body(n_ref, wu_ref, a_ref, b_ref, h_ref):
        b = _dot(n_ref[...], wu_ref[...], NT)
        a = a_ref[...].astype(F32)
        b_ref[...] = b.astype(BF16)
        h_ref[...] = (a * jax.nn.sigmoid(a) * b).astype(BF16)

    act = jax.ShapeDtypeStruct((J, T, Fs), BF16)
    aspec = pl.BlockSpec((None, tm, Fs), lambda j, i: (j, i, 0))
    return _call(
        body, name=name, grid=(J, T // tm), out_shape=(act, act),
        in_specs=[pl.BlockSpec((tm, D), lambda j, i: (i, 0)), pl.BlockSpec((None, Fs, D), lambda j, i: (j, 0, 0)), aspec],
        out_specs=(aspec, aspec), args=[n, wu, a], carry=carry)


def _ffn_down(name, hm, wd, resid, tm, carry=None):
    J, T, Fs = hm.shape
    D = wd.shape[2]

    def body(h_ref, w_ref, r_ref, o_ref, acc):
        j = pl.program_id(1)
        part = _dot(h_ref[...], w_ref[...], NN)

        @pl.when(j == 0)
        def _():
            acc[...] = part

        @pl.when(j > 0)
        def _():
            acc[...] += part

        @pl.when(j == J - 1)
        def _():
            o_ref[...] = r_ref[...] + 0.5 * acc[...]

    row = pl.BlockSpec((tm, D), lambda i, j: (i, 0))
    return _call(
        body, name=name, grid=(T // tm, J), out_shape=jax.ShapeDtypeStruct((T, D), F32),
        in_specs=[pl.BlockSpec((None, tm, Fs), lambda i, j: (j, i, 0)), pl.BlockSpec((None, Fs, D), lambda i, j: (j, 0, 0)), row],
        out_specs=row, scratch_shapes=[pltpu.VMEM((tm, D), F32)], args=[hm, wd, resid], carry=carry)


def _ffn_bwd_mid(name, dfh, wd, a, b, tm, carry=None):
    T, D = dfh.shape
    J, Fs, _ = wd.shape

    def body(df_ref, w_ref, a_ref, b_ref, da_ref, db_ref):
        dhm = _dot(df_ref[...], w_ref[...], NT)
        av = a_ref[...].astype(F32)
        bv = b_ref[...].astype(F32)
        sg = jax.nn.sigmoid(av)
        da_ref[...] = (dhm * bv * (sg * (1.0 + av * (1.0 - sg)))).astype(BF16)
        db_ref[...] = (dhm * (av * sg)).astype(BF16)

    act = jax.ShapeDtypeStruct((J, T, Fs), BF16)
    aspec = pl.BlockSpec((None, tm, Fs), lambda j, i: (j, i, 0))
    return _call(
        body, name=name, grid=(J, T // tm), out_shape=(act, act),
        in_specs=[pl.BlockSpec((tm, D), lambda j, i: (i, 0)), pl.BlockSpec((None, Fs, D), lambda j, i: (j, 0, 0)), aspec, aspec],
        out_specs=(aspec, aspec), args=[dfh, wd, a, b], carry=carry)


def _rot_half(y, lane):
    first = (lane & (HEAD_DIM // 2)) == 0
    return jnp.where(first, pltpu.roll(y, LANES - HEAD_DIM // 2, 1), pltpu.roll(y, HEAD_DIM // 2, 1))


def _head_rstd(x, lo):
    sq = x * x
    ss_a = jnp.sum(jnp.where(lo, sq, 0.0), axis=-1, keepdims=True)
    ss_b = jnp.sum(jnp.where(lo, 0.0, sq), axis=-1, keepdims=True)
    return lax.rsqrt(jnp.where(lo, ss_a, ss_b) * (1.0 / HEAD_DIM) + EPS)


def _headnorm_fwd(name, proj, col_off, ncb, gains, tm, scale, rope=None, dup=False):
    T = proj.shape[0]
    with_rope = rope is not None
    width = 2 * LANES if dup else LANES

    def body(*refs):
        if with_rope:
            x_ref, g_ref, cos_ref, sin_ref, o_ref = refs
        else:
            x_ref, g_ref, o_ref = refs
        xv = x_ref[...]
        lane = lax.broadcasted_iota(jnp.int32, xv.shape, 1)
        lo = lane < HEAD_DIM
        y = xv * _head_rstd(xv, lo) * g_ref[...]
        if with_rope:
            y = y * cos_ref[...] + _rot_half(y, lane) * sin_ref[...]
        y = y * scale
        if dup:
            sw = pltpu.roll(y, HEAD_DIM, 1)
            o_ref[:, :LANES] = jnp.where(lo, y, sw).astype(BF16)
            o_ref[:, LANES:] = jnp.where(lo, sw, y).astype(BF16)
        else:
            o_ref[...] = y.astype(BF16)

    in_specs = [pl.BlockSpec((tm, LANES), lambda c, i: (i, col_off + c)), pl.BlockSpec((None, 1, LANES), lambda c, i: (c, 0, 0))]
    args = [proj, gains]
    if with_rope:
        tab = pl.BlockSpec((tm, LANES), lambda c, i: (i, 0))
        in_specs += [tab, tab]
        args += list(rope)
    return pl.pallas_call(
        body, out_shape=jax.ShapeDtypeStruct((T, ncb * width), BF16), grid=(ncb, T // tm),
        in_specs=in_specs, out_specs=pl.BlockSpec((tm, width), lambda c, i: (i, c)),
        name=name, compiler_params=_params(2))(*args)


def _headnorm_bwd(name, dy, proj, col_off, ncb, gains, group, tm, scale, rope=None, fold=False, norm=True):
    T = dy.shape[0]
    with_rope = rope is not None
    n_groups = ncb // group
    dy_width = 4 * LANES if fold else LANES

    def body(*refs):
        refs = list(refs)
        dy_ref = refs.pop(0)
        x_ref = refs.pop(0) if norm else None
        g_ref = refs.pop(0) if norm else None
        cos_ref = refs.pop(0) if with_rope else None
        sin_ref = refs.pop(0) if with_rope else None
        dx_ref = refs.pop(0)
        dg_ref = refs.pop(0) if norm else None
        c = pl.program_id(0)
        i = pl.program_id(1)
        d = dy_ref[...]
        lane = lax.broadcasted_iota(jnp.int32, (d.shape[0], LANES), 1)
        lo = lane < HEAD_DIM
        if fold:
            t0 = d[:, 0:LANES] + d[:, LANES:2 * LANES]
            t1 = d[:, 2 * LANES:3 * LANES] + d[:, 3 * LANES:4 * LANES]
            d = jnp.where(lo, t0 + pltpu.roll(t0, HEAD_DIM, 1), t1 + pltpu.roll(t1, HEAD_DIM, 1))
        d = d * scale
        if with_rope:
            d = d * cos_ref[...] + _rot_half(d * sin_ref[...], lane)
        if not norm:
            dx_ref[...] = d.astype(BF16)
            return
        xv = x_ref[...]
        gv = g_ref[...]
        r = _head_rstd(xv, lo)
        xh = xv * r
        dxh = d * gv
        pr = dxh * xh
        m_a = jnp.sum(jnp.where(lo, pr, 0.0), axis=-1, keepdims=True)
        m_b = jnp.sum(jnp.where(lo, 0.0, pr), axis=-1, keepdims=True)
        mean = jnp.where(lo, m_a, m_b) * (1.0 / HEAD_DIM)
        dx_ref[...] = (r * (dxh - xh * mean)).astype(BF16)
        dgp = jnp.sum(d * xh, axis=0, keepdims=True)
        dgp = dgp + pltpu.roll(dgp, HEAD_DIM, 1)
        first = jnp.logical_and(c % group == 0, i == 0)

        @pl.when(first)
        def _():
            dg_ref[...] = dgp

        @pl.when(jnp.logical_not(first))
        def _():
            dg_ref[...] += dgp

    in_specs = [pl.BlockSpec((tm, dy_width), lambda c, i: (i, c))]
    args = [dy]
    if norm:
        in_specs += [pl.BlockSpec((tm, LANES), lambda c, i: (i, col_off + c)), pl.BlockSpec((None, 1, LANES), lambda c, i: (c, 0, 0))]
        args += [proj, gains]
    if with_rope:
        tab = pl.BlockSpec((tm, LANES), lambda c, i: (i, 0))
        in_specs += [tab, tab]
        args += list(rope)
    out_shape = [jax.ShapeDtypeStruct((T, ncb * LANES), BF16)]
    out_specs = [pl.BlockSpec((tm, LANES), lambda c, i: (i, c))]
    if norm:
        out_shape.append(jax.ShapeDtypeStruct((n_groups, 1, LANES), F32))
        out_specs.append(pl.BlockSpec((None, 1, LANES), lambda c, i: (c // group, 0, 0)))
    res = pl.pallas_call(
        body, out_shape=tuple(out_shape), grid=(ncb, T // tm), in_specs=in_specs, out_specs=tuple(out_specs),
        name=name, compiler_params=_params(2))(*args)
    return res if norm else (res[0], None)


def _dot_exact(x, tri):
    hi = x.astype(BF16)
    r1 = x - hi.astype(F32)
    mid = r1.astype(BF16)
    lo = (r1 - mid.astype(F32)).astype(BF16)
    return _dot(hi, tri, NN) + _dot(mid, tri, NN) + _dot(lo, tri, NN)


def _forget_fwd(name, zt, bias):
    H, T = zt.shape
    blk = min(256, T)

    def body(z_ref, b_ref, c_ref, s_ref):
        z = z_ref[...] + b_ref[...]
        s_ref[...] = jax.nn.sigmoid(-z)
        lf = jnp.minimum(z, 0.0) - jnp.log(1.0 + jnp.exp(-jnp.abs(z)))
        tri = (lax.broadcasted_iota(jnp.int32, (blk, blk), 0) <= lax.broadcasted_iota(jnp.int32, (blk, blk), 1)).astype(BF16)
        carry = jnp.zeros((H, 1), F32)
        for bi in range(T // blk):
            xb = lf[:, bi * blk:(bi + 1) * blk]
            c_ref[:, bi * blk:(bi + 1) * blk] = _dot_exact(xb, tri) + carry
            carry = carry + jnp.sum(xb, axis=-1, keepdims=True)

    shape = jax.ShapeDtypeStruct((H, T), F32)
    full = pl.BlockSpec((H, T), lambda i: (0, 0))
    return pl.pallas_call(
        body, out_shape=(shape, shape), grid=(1,), in_specs=[full, pl.BlockSpec((H, 1), lambda i: (0, 0))],
        out_specs=(full, full), name=name, compiler_params=_params(1))(zt, bias)


def _forget_bwd(name, dct, drt, sgt):
    H, T = dct.shape
    blk = min(256, T)

    def body(dc_ref, dr_ref, s_ref, dz_ref, db_ref):
        dc = dc_ref[...] + dr_ref[...]
        tri = (lax.broadcasted_iota(jnp.int32, (blk, blk), 0) >= lax.broadcasted_iota(jnp.int32, (blk, blk), 1)).astype(BF16)
        carry = jnp.zeros((H, 1), F32)
        db = jnp.zeros((H, 1), F32)
        for bi in reversed(range(T // blk)):
            xb = dc[:, bi * blk:(bi + 1) * blk]
            dz = (_dot_exact(xb, tri) + carry) * s_ref[:, bi * blk:(bi + 1) * blk]
            dz_ref[:, bi * blk:(bi + 1) * blk] = dz
            db = db + jnp.sum(dz, axis=-1, keepdims=True)
            carry = carry + jnp.sum(xb, axis=-1, keepdims=True)
        db_ref[...] = db

    full = pl.BlockSpec((H, T), lambda i: (0, 0))
    return pl.pallas_call(
        body, out_shape=(jax.ShapeDtypeStruct((H, T), F32), jax.ShapeDtypeStruct((H, 1), F32)), grid=(1,),
        in_specs=[full, full, full], out_specs=(full, pl.BlockSpec((H, 1), lambda i: (0, 0))),
        name=name, compiler_params=_params(1))(dct, drt, sgt)


STRIP = 256


def _fox_fwd(name, qk, v, crow, tq, tk, carry=None):
    T, Dh = v.shape
    HP = Dh // LANES
    nk = T // tk
    assert tk % tq == 0 and tq % STRIP == 0
    n_strips = tq // STRIP

    def body(q_ref, k_ref, v_ref, ra_ref, rb_ref, o_ref, la_ref, lb_ref, s_ref, p_ref, m_ref, l_ref, acc_ref):
        i = pl.program_id(1)
        q2 = q_ref[...]
        lo = _lane_lo((tq, LANES))
        qms = (_keep(lo, q2), _keep(jnp.logical_not(lo), q2))
        r_refs = (ra_ref, rb_ref)
        m_ref[...] = jnp.full(m_ref.shape, NEG, F32)
        l_ref[...] = jnp.zeros(l_ref.shape, F32)
        acc_ref[...] = jnp.zeros(acc_ref.shape, F32)
        rel = lax.broadcasted_iota(jnp.int32, (STRIP, tk), 0) - lax.broadcasted_iota(jnp.int32, (STRIP, tk), 1)

        def chunk(kc, masked):
            start = pl.multiple_of(kc * tk, tk)
            kb = k_ref[pl.ds(start, tk), :]
            vb = v_ref[pl.ds(start, tk), :]
            for h in range(2):
                s_ref[h] = _dot(qms[h], kb, NT)
            for h in range(2):
                cs = r_refs[h][kc]
                for st in range(n_strips):
                    rows = pl.ds(st * STRIP, STRIP)
                    s = s_ref[h, rows, :] - cs
                    if masked:
                        s = jnp.where(rel >= start - (i * tq + st * STRIP), s, NEG)
                    m_old = m_ref[h, rows, :]
                    mn = jnp.maximum(m_old, jnp.max(s, axis=-1, keepdims=True))
                    p = jnp.exp(s - mn)
                    alpha = jnp.exp(m_old - mn)
                    l_ref[h, rows, :] = alpha * l_ref[h, rows, :] + jnp.sum(p, axis=-1, keepdims=True)
                    m_ref[h, rows, :] = mn
                    p_ref[h, rows, :] = p.astype(BF16)
                    acc_ref[h, rows, :] = acc_ref[h, rows, :] * alpha
            for h in range(2):
                acc_ref[h] += _dot(p_ref[h], vb, NN)

        n_full = (i * tq) // tk

        def full_chunk(kc, _):
            chunk(kc, False)
            return 0

        lax.fori_loop(0, n_full, full_chunk, 0)
        chunk(n_full, True)
        o_ref[...] = jnp.where(lo, acc_ref[0] / l_ref[0], acc_ref[1] / l_ref[1])
        la_ref[...] = m_ref[0] + jnp.log(l_ref[0])
        lb_ref[...] = m_ref[1] + jnp.log(l_ref[1])

    row = lambda off: pl.BlockSpec((None, nk, 1, tk), lambda h, i: (2 * h + off, 0, 0, 0))
    lse = jax.ShapeDtypeStruct((HP, T, 1), F32)
    lspec = pl.BlockSpec((None, tq, 1), lambda h, i: (h, i, 0))
    scratch = [pltpu.VMEM((2, tq, tk), F32), pltpu.VMEM((2, tq, tk), BF16), pltpu.VMEM((2, tq, 1), F32),
               pltpu.VMEM((2, tq, 1), F32), pltpu.VMEM((2, tq, LANES), F32)]
    return _call(
        body, name=name, grid=(HP, T // tq), out_shape=(jax.ShapeDtypeStruct((T, Dh), F32), lse, lse),
        in_specs=[pl.BlockSpec((tq, LANES), lambda h, i: (i, h)), pl.BlockSpec((T, LANES), lambda h, i: (0, HP + h)),
                  pl.BlockSpec((T, LANES), lambda h, i: (0, h)), row(0), row(1)],
        out_specs=(pl.BlockSpec((tq, LANES), lambda h, i: (i, h)), lspec, lspec),
        args=[qk, qk, v, crow, crow], scratch_shapes=scratch, carry=carry)


def _fox_bwd(name, qk, v, o, do, crow, lse_a, lse_b, tq, tk, carry=None):
    T, Dh = v.shape
    HP = Dh // LANES
    nk = T // tk
    scale = HEAD_DIM ** -0.5
    assert tk % tq == 0 and tq % STRIP == 0
    n_strips = tq // STRIP

    def body(q_ref, k_ref, v_ref, o_ref, do_ref, ra_ref, rb_ref, la_ref, lb_ref,
             dq_ref, dk_ref, dv_ref, dca_ref, dcb_ref, dra_ref, drb_ref, s_ref, dp_ref, p_ref, ds_ref, dq_acc, dsum_ref):
        i = pl.program_id(1)

        @pl.when(i == 0)
        def _():
            dk_ref[...] = jnp.zeros_like(dk_ref)
            dv_ref[...] = jnp.zeros_like(dv_ref)
            dca_ref[...] = jnp.zeros_like(dca_ref)
            dcb_ref[...] = jnp.zeros_like(dcb_ref)

        q2 = q_ref[...]
        do2 = do_ref[...]
        lo = _lane_lo((tq, LANES))
        hi = jnp.logical_not(lo)
        qms = (_keep(lo, q2), _keep(hi, q2))
        doms = (_keep(lo, do2), _keep(hi, do2))
        prod = do2.astype(F32) * o_ref[...]
        dsum_ref[0] = jnp.sum(jnp.where(lo, prod, 0.0), axis=-1, keepdims=True)
        dsum_ref[1] = jnp.sum(jnp.where(lo, 0.0, prod), axis=-1, keepdims=True)
        r_refs, l_refs, dc_refs, dr_refs = (ra_ref, rb_ref), (la_ref, lb_ref), (dca_ref, dcb_ref), (dra_ref, drb_ref)
        dq_acc[...] = jnp.zeros(dq_acc.shape, F32)
        dra_ref[...] = jnp.zeros(dra_ref.shape, F32)
        drb_ref[...] = jnp.zeros(drb_ref.shape, F32)
        rel = lax.broadcasted_iota(jnp.int32, (STRIP, tk), 0) - lax.broadcasted_iota(jnp.int32, (STRIP, tk), 1)

        def chunk(kc, masked):
            start = pl.multiple_of(kc * tk, tk)
            kb = k_ref[pl.ds(start, tk), :]
            vb = v_ref[pl.ds(start, tk), :]
            for h in range(2):
                s_ref[h] = _dot(qms[h], kb, NT)
                dp_ref[h] = _dot(doms[h], vb, NT)
            for h in range(2):
                cs = r_refs[h][kc]
                col_sum = jnp.zeros((1, tk), F32)
                for st in range(n_strips):
                    rows = pl.ds(st * STRIP, STRIP)
                    s = s_ref[h, rows, :] - cs
                    if masked:
                        s = jnp.where(rel >= start - (i * tq + st * STRIP), s, NEG)
                    p = jnp.exp(s - l_refs[h][rows, :])
                    ds = p * (dp_ref[h, rows, :] - dsum_ref[h, rows, :])
                    p_ref[h, rows, :] = p.astype(BF16)
                    ds_ref[h, rows, :] = ds.astype(BF16)
                    col_sum = col_sum + jnp.sum(ds, axis=0, keepdims=True)
                    dr_refs[h][rows, :] += jnp.sum(ds, axis=-1, keepdims=True)
                dc_refs[h][kc] = dc_refs[h][kc] - col_sum
            dk = _dot(ds_ref[0], qms[0], TN) + _dot(ds_ref[1], qms[1], TN)
            dv = _dot(p_ref[0], doms[0], TN) + _dot(p_ref[1], doms[1], TN)
            dk_ref[pl.ds(start, tk), :] += dk
            dv_ref[pl.ds(start, tk), :] += dv
            for h in range(2):
                dq_acc[h] += _dot(ds_ref[h], kb, NN)

        n_full = (i * tq) // tk

        def full_chunk(kc, _):
            chunk(kc, False)
            return 0

        lax.fori_loop(0, n_full, full_chunk, 0)
        chunk(n_full, True)
        dq_ref[...] = jnp.where(lo, dq_acc[0], dq_acc[1]) * scale

    row = lambda off: pl.BlockSpec((None, nk, 1, tk), lambda h, i: (2 * h + off, 0, 0, 0))
    lspec = pl.BlockSpec((None, tq, 1), lambda h, i: (h, i, 0))
    qspec = pl.BlockSpec((tq, LANES), lambda h, i: (i, h))
    full = pl.BlockSpec((T, LANES), lambda h, i: (0, h))
    dcspec = pl.BlockSpec((None, nk, 1, tk), lambda h, i: (h, 0, 0, 0))
    grad = jax.ShapeDtypeStruct((T, Dh), F32)
    dc = jax.ShapeDtypeStruct((HP, nk, 1, tk), F32)
    dr = jax.ShapeDtypeStruct((HP, T, 1), F32)
    scratch = [pltpu.VMEM((2, tq, tk), F32), pltpu.VMEM((2, tq, tk), F32), pltpu.VMEM((2, tq, tk), BF16), pltpu.VMEM((2, tq, tk), BF16),
               pltpu.VMEM((2, tq, LANES), F32), pltpu.VMEM((2, tq, 1), F32)]
    return _call(
        body, name=name, grid=(HP, T // tq), out_shape=(grad, grad, grad, dc, dc, dr, dr),
        in_specs=[qspec, pl.BlockSpec((T, LANES), lambda h, i: (0, HP + h)), full, qspec, qspec, row(0), row(1), lspec, lspec],
        out_specs=(qspec, full, full, dcspec, dcspec, lspec, lspec),
        args=[qk, qk, v, o, do, crow, crow, lse_a, lse_b], scratch_shapes=scratch, carry=carry)


def _swa_block(n, q_ref, k_ref):
    qs = pl.multiple_of(n * WINDOW, WINDOW)
    ks = pl.multiple_of(jnp.maximum(n - 1, 0) * WINDOW, WINDOW)
    rel = (qs + lax.broadcasted_iota(jnp.int32, (WINDOW, 2 * WINDOW), 0)) - (ks + lax.broadcasted_iota(jnp.int32, (WINDOW, 2 * WINDOW), 1))
    valid = jnp.logical_and(rel >= 0, rel < WINDOW)
    return qs, ks, valid


def _swa_fwd(name, q, kd, vd, sinks, carry=None):
    T, Dh = q.shape
    HP = Dh // LANES

    def body(q_ref, k_ref, v_ref, sa_ref, sb_ref, o_ref, la_ref, lb_ref):
        lo = _lane_lo((WINDOW, LANES))

        def block(n, _):
            qs, ks, valid = _swa_block(n, q_ref, k_ref)
            q2 = q_ref[pl.ds(qs, WINDOW), :]
            kb = k_ref[pl.ds(ks, 2 * WINDOW), :]
            vb = v_ref[pl.ds(ks, 2 * WINDOW), :]
            res = []
            for sel, s_ref in ((lo, sa_ref), (jnp.logical_not(lo), sb_ref)):
                qm = _keep(sel, q2)
                sink = s_ref[...]
                s = jnp.where(valid, _dot(qm, kb, NT), NEG)
                m = jnp.maximum(jnp.max(s, axis=-1, keepdims=True), sink)
                p = jnp.exp(s - m)
                l = jnp.sum(p, axis=-1, keepdims=True) + jnp.exp(sink - m)
                res.append((_dot(p.astype(BF16), vb, NN) / l, m + jnp.log(l)))
            o_ref[pl.ds(qs, WINDOW), :] = jnp.where(lo, res[0][0], res[1][0])
            la_ref[pl.ds(qs, WINDOW), :] = res[0][1]
            lb_ref[pl.ds(qs, WINDOW), :] = res[1][1]
            return 0

        lax.fori_loop(0, T // WINDOW, block, 0, unroll=2)

    full = pl.BlockSpec((T, LANES), lambda h: (0, h))
    kv = pl.BlockSpec((T, LANES), lambda h: (0, h // 2))
    sink = lambda off: pl.BlockSpec((None, 1, 1), lambda h: (2 * h + off, 0, 0))
    lse = jax.ShapeDtypeStruct((HP, T, 1), F32)
    lspec = pl.BlockSpec((None, T, 1), lambda h: (h, 0, 0))
    return _call(
        body, name=name, grid=(HP,), out_shape=(jax.ShapeDtypeStruct((T, Dh), F32), lse, lse),
        in_specs=[full, kv, kv, sink(0), sink(1)], out_specs=(full, lspec, lspec),
        args=[q, kd, vd, sinks, sinks], carry=carry)


def _swa_bwd(name, q, kd, vd, sinks, o, do, lse_a, lse_b, carry=None):
    T, Dh = q.shape
    HP = Dh // LANES
    scale = HEAD_DIM ** -0.5

    def body(q_ref, k_ref, v_ref, sa_ref, sb_ref, o_ref, do_ref, la_ref, lb_ref, dq_ref, dk_ref, dv_ref, dsa_ref, dsb_ref):
        lo = _lane_lo((WINDOW, LANES))
        hi = jnp.logical_not(lo)
        dk_ref[...] = jnp.zeros_like(dk_ref)
        dv_ref[...] = jnp.zeros_like(dv_ref)

        def block(n, dsinks):
            qs, ks, valid = _swa_block(n, q_ref, k_ref)
            q2 = q_ref[pl.ds(qs, WINDOW), :]
            do2 = do_ref[pl.ds(qs, WINDOW), :]
            kb = k_ref[pl.ds(ks, 2 * WINDOW), :]
            vb = v_ref[pl.ds(ks, 2 * WINDOW), :]
            prod = do2.astype(F32) * o_ref[pl.ds(qs, WINDOW), :]
            dqs, new = [], []
            dk = jnp.zeros((2 * WINDOW, LANES), F32)
            dv = jnp.zeros((2 * WINDOW, LANES), F32)
            for sel, s_ref, l_ref, dsink in ((lo, sa_ref, la_ref, dsinks[0]), (hi, sb_ref, lb_ref, dsinks[1])):
                qm = _keep(sel, q2)
                dom = _keep(sel, do2)
                dsum = jnp.sum(jnp.where(sel, prod, 0.0), axis=-1, keepdims=True)
                lse = l_ref[pl.ds(qs, WINDOW), :]
                s = jnp.where(valid, _dot(qm, kb, NT), NEG)
                p = jnp.exp(s - lse)
                ds = p * (_dot(dom, vb, NT) - dsum)
                dsb = ds.astype(BF16)
                dqs.append(_dot(dsb, kb, NN))
                dk = dk + _dot(dsb, qm, TN)
                dv = dv + _dot(p.astype(BF16), dom, TN)
                new.append(dsink - jnp.sum(jnp.exp(s_ref[...] - lse) * dsum, axis=0, keepdims=True))
            dq_ref[pl.ds(qs, WINDOW), :] = jnp.where(lo, dqs[0], dqs[1]) * scale
            dk_ref[pl.ds(ks, 2 * WINDOW), :] += dk
            dv_ref[pl.ds(ks, 2 * WINDOW), :] += dv
            return tuple(new)

        dsa, dsb_ = lax.fori_loop(0, T // WINDOW, block, (jnp.zeros((1, 1), F32), jnp.zeros((1, 1), F32)), unroll=2)
        dsa_ref[...] = dsa
        dsb_ref[...] = dsb_

    full = pl.BlockSpec((T, LANES), lambda h: (0, h))
    kv = pl.BlockSpec((T, LANES), lambda h: (0, h // 2))
    sink = lambda off: pl.BlockSpec((None, 1, 1), lambda h: (2 * h + off, 0, 0))
    lspec = pl.BlockSpec((None, T, 1), lambda h: (h, 0, 0))
    dsink = pl.BlockSpec((None, 1, 1), lambda h: (h, 0, 0))
    grad = jax.ShapeDtypeStruct((T, Dh), F32)
    ds_shape = jax.ShapeDtypeStruct((HP, 1, 1), F32)
    return _call(
        body, name=name, grid=(HP,), out_shape=(grad, grad, grad, ds_shape, ds_shape),
        in_specs=[full, kv, kv, sink(0), sink(1), full, full, lspec, lspec],
        out_specs=(full, full, full, dsink, dsink),
        args=[q, kd, vd, sinks, sinks, o, do, lse_a, lse_b], carry=carry)


def _place():
    return lax.axis_index("x"), lax.axis_index("y"), lax.axis_index("c")


def _run_carry(name, carry):
    c_in, c_out = len(carry.inputs), len(carry.out_shapes)

    def body(*refs):
        ins, outs, scr = refs[:c_in], refs[c_in:c_in + c_out], refs[c_in + c_out:]
        carry.start(ins, outs, scr)
        carry.middle(ins, outs, scr)
        carry.finish(ins, outs, scr)

    return pl.pallas_call(
        body, out_shape=tuple(carry.out_shapes), in_specs=[_HBM] * c_in, out_specs=tuple([_HBM] * c_out),
        scratch_shapes=carry.scratch, name=name)(*carry.inputs)


def _gather_carry(shards):
    n = len(shards)

    def plan(ins, outs, scr):
        send, recv, local = scr
        x, y, c = _place()
        me, sibling = (x, y, c), (x, y, 1 - c)
        partner, other, diag = (x ^ c, y ^ (1 - c)), (x ^ (1 - c), y ^ c), (1 - x, 1 - y)

        def copy(w, k, block, to, src=None):
            slot = 4 * block[0] + 2 * block[1] + block[2]
            return pltpu.make_async_remote_copy(
                src_ref=outs[w].at[slot] if src is None else src, dst_ref=outs[w].at[slot],
                send_sem=send.at[w, k], recv_sem=recv.at[w, k], device_id=to, device_id_type=MESH)

        own = [pltpu.make_async_copy(ins[w], outs[w].at[4 * x + 2 * y + c], local.at[w]) for w in range(n)]
        return copy, own, me, sibling, partner, other, diag, c

    def start(ins, outs, scr):
        copy, own, me, sibling, partner, other, _, c = plan(ins, outs, scr)
        for cp in own:
            cp.start()
        for w in range(n):
            copy(w, 1, me, (*partner, c), src=ins[w]).start()
            copy(w, 2, me, (*other, c), src=ins[w]).start()
            copy(w, 0, me, sibling, src=ins[w]).start()

    def middle(ins, outs, scr):
        copy, _, me, sibling, partner, other, _, c = plan(ins, outs, scr)
        for w in range(n):
            copy(w, 1, (*partner, c), me).wait_recv()
            copy(w, 3, (*partner, c), (*other, c)).start()
            copy(w, 4, (*partner, c), sibling).start()

    def finish(ins, outs, scr):
        copy, own, me, sibling, partner, other, diag, c = plan(ins, outs, scr)
        for w in range(n):
            copy(w, 2, (*other, c), me).wait_recv()
            copy(w, 5, (*other, c), sibling).start()
        for w in range(n):
            copy(w, 3, (*diag, c), me).wait_recv()
            copy(w, 6, (*diag, c), sibling).start()
        for w in range(n):
            copy(w, 0, sibling, me).wait_recv()
            copy(w, 4, (*other, 1 - c), me).wait_recv()
            copy(w, 5, (*partner, 1 - c), me).wait_recv()
            copy(w, 6, (*diag, 1 - c), me).wait_recv()
        for w in range(n):
            sent = [copy(w, 0, me, sibling, src=ins[w]), copy(w, 1, me, (*partner, c), src=ins[w]), copy(w, 2, me, (*other, c), src=ins[w]),
                    copy(w, 3, (*partner, c), (*other, c)), copy(w, 4, (*partner, c), sibling), copy(w, 5, (*other, c), sibling),
                    copy(w, 6, (*diag, c), sibling)]
            for cp in sent:
                cp.wait_send()
        for cp in own:
            cp.wait()

    return _Carry(shards, [jax.ShapeDtypeStruct((N_DEV,) + s.shape, s.dtype) for s in shards],
                  [pltpu.SemaphoreType.DMA((n, 7)), pltpu.SemaphoreType.DMA((n, 7)), pltpu.SemaphoreType.DMA((n,))], start, finish, middle)


def _sibling_carry(grads):
    n = len(grads)

    def copies(ins, outs, scr):
        send, recv = scr
        x, y, c = _place()
        return [pltpu.make_async_remote_copy(
            src_ref=ins[w].at[2 * q + (1 - c)], dst_ref=outs[w].at[q], send_sem=send.at[w, q], recv_sem=recv.at[w, q],
            device_id=(x, y, 1 - c), device_id_type=MESH) for w in range(n) for q in range(4)]

    def start(ins, outs, scr):
        for cp in copies(ins, outs, scr):
            cp.start()

    def finish(ins, outs, scr):
        for cp in copies(ins, outs, scr):
            cp.wait()

    return _Carry(grads, [jax.ShapeDtypeStruct((4,) + g.shape[1:], g.dtype) for g in grads],
                  [pltpu.SemaphoreType.DMA((n, 4)), pltpu.SemaphoreType.DMA((n, 4))], start, finish)


def _to_partner_carry(sums):
    n = len(sums)

    def copies(ins, outs, scr):
        send, recv = scr
        x, y, c = _place()
        partner, diag = (x ^ c, y ^ (1 - c)), (1 - x, 1 - y)
        cps = []
        for w in range(n):
            for k, chip in enumerate((partner, diag)):
                cps.append(pltpu.make_async_remote_copy(
                    src_ref=ins[w].at[2 * chip[0] + chip[1]], dst_ref=outs[2 * w + k], send_sem=send.at[w, k], recv_sem=recv.at[w, k],
                    device_id=(*partner, c), device_id_type=MESH))
        return cps

    def start(ins, outs, scr):
        for cp in copies(ins, outs, scr):
            cp.start()

    def finish(ins, outs, scr):
        for cp in copies(ins, outs, scr):
            cp.wait()

    return _Carry(sums, [jax.ShapeDtypeStruct(s.shape[1:], s.dtype) for s in sums for _ in range(2)],
                  [pltpu.SemaphoreType.DMA((n, 2)), pltpu.SemaphoreType.DMA((n, 2))], start, finish)


def _to_other_carry(blocks):
    n = len(blocks)

    def copies(ins, outs, scr):
        send, recv = scr
        x, y, c = _place()
        return [pltpu.make_async_remote_copy(
            src_ref=ins[w], dst_ref=outs[w], send_sem=send.at[w], recv_sem=recv.at[w],
            device_id=(x ^ (1 - c), y ^ c, c), device_id_type=MESH) for w in range(n)]

    def start(ins, outs, scr):
        for cp in copies(ins, outs, scr):
            cp.start()

    def finish(ins, outs, scr):
        for cp in copies(ins, outs, scr):
            cp.wait()

    return _Carry(blocks, [jax.ShapeDtypeStruct(b.shape, b.dtype) for b in blocks],
                  [pltpu.SemaphoreType.DMA((n,)), pltpu.SemaphoreType.DMA((n,))], start, finish)


def _gather_small(packed):
    R, C = packed.shape

    def body(in_ref, out_ref, send, recv):
        x, y, c = _place()
        mine = 4 * x + 2 * y + c
        out_ref[mine] = in_ref[...]
        copies = []
        for k in range(1, N_DEV):
            peer = (x ^ (k >> 2), y ^ ((k >> 1) & 1), c ^ (k & 1))
            copies.append(pltpu.make_async_remote_copy(
                src_ref=in_ref, dst_ref=out_ref.at[mine], send_sem=send.at[k - 1], recv_sem=recv.at[k - 1],
                device_id=peer, device_id_type=MESH))
        for cp in copies:
            cp.start()
        for cp in copies:
            cp.wait()

    vmem = pl.BlockSpec(memory_space=pltpu.VMEM)
    return pl.pallas_call(
        body, out_shape=jax.ShapeDtypeStruct((N_DEV, R, C), F32), in_specs=[vmem], out_specs=vmem,
        scratch_shapes=[pltpu.SemaphoreType.DMA((N_DEV - 1,)), pltpu.SemaphoreType.DMA((N_DEV - 1,))],
        name="small_grads_all_gather")(packed)


def _adamw(w, g, m, v):
    m = ADAM_B1 * m + (1.0 - ADAM_B1) * g
    v = ADAM_B2 * v + (1.0 - ADAM_B2) * (g * g)
    m_hat = m / (1.0 - ADAM_B1 ** ADAM_STEP)
    v_hat = v / (1.0 - ADAM_B2 ** ADAM_STEP)
    delta = -ADAM_LR * (m_hat / (jnp.sqrt(v_hat) + ADAM_EPS) + ADAM_WD * w)
    return delta, m, v


def _pair_add(name, grads, received, c_idx):
    _, R, C = grads.shape
    tr = _row_tile(R)

    def body(c_ref, g_ref, r_ref, o_ref):
        o_ref[...] = (g_ref[...].astype(F32) + r_ref[...].astype(F32)).astype(BF16)

    blk = pl.BlockSpec((None, tr, C), lambda q, i, c: (q, i, 0))
    return pl.pallas_call(
        body, out_shape=jax.ShapeDtypeStruct((4, R, C), BF16),
        grid_spec=pltpu.PrefetchScalarGridSpec(
            num_scalar_prefetch=1, grid=(4, R // tr),
            in_specs=[pl.BlockSpec((None, tr, C), lambda q, i, c: (2 * q + c[0], i, 0)), blk], out_specs=blk),
        name=name, compiler_params=_params(2))(c_idx, grads, received)


def _relay_add(name, sums, relayed, other_idx):
    _, R, C = sums.shape
    tr = _row_tile(R)

    def body(q_ref, s_ref, r_ref, o_ref):
        o_ref[...] = (s_ref[...].astype(F32) + r_ref[...].astype(F32)).astype(BF16)

    blk = pl.BlockSpec((tr, C), lambda i, q: (i, 0))
    return pl.pallas_call(
        body, out_shape=jax.ShapeDtypeStruct((R, C), BF16),
        grid_spec=pltpu.PrefetchScalarGridSpec(
            num_scalar_prefetch=1, grid=(R // tr,),
            in_specs=[pl.BlockSpec((None, tr, C), lambda i, q: (q[0], i, 0)), blk], out_specs=blk),
        name=name, compiler_params=_params(1))(other_idx, sums, relayed)


def _adam_shard(name, sums, received, w, m, v, chip_idx):
    R, C = w.shape
    tr = _row_tile(R, 128)
    tc = C if tr < R or C % (2 * LANES) else 2 * LANES

    def body(q_ref, s_ref, ra_ref, rb_ref, w_ref, m_ref, v_ref, g_out, d_out, m_out, v_out):
        g = s_ref[...].astype(F32) + ra_ref[...].astype(F32) + rb_ref[...].astype(F32)
        delta, mn, vn = _adamw(w_ref[...], g, m_ref[...], v_ref[...])
        g_out[...] = g
        d_out[...] = delta
        m_out[...] = mn
        v_out[...] = vn

    blk = pl.BlockSpec((tr, tc), lambda i, j, q: (i, j))
    shape = jax.ShapeDtypeStruct((R, C), F32)
    return pl.pallas_call(
        body, out_shape=(shape,) * 4,
        grid_spec=pltpu.PrefetchScalarGridSpec(
            num_scalar_prefetch=1, grid=(R // tr, C // tc),
            in_specs=[pl.BlockSpec((None, tr, tc), lambda i, j, q: (q[0], i, j)), blk, blk, blk, blk, blk],
            out_specs=(blk,) * 4),
        name=name, compiler_params=_params(2))(chip_idx, sums, received[0], received[1], w, m, v)


def _adam_small(name, gathered, w, m, v):
    R, C = w.shape

    def body(ga_ref, w_ref, m_ref, v_ref, g_out, d_out, m_out, v_out):
        g = ga_ref[0]
        for d in range(1, N_DEV):
            g = g + ga_ref[d]
        delta, mn, vn = _adamw(w_ref[...], g, m_ref[...], v_ref[...])
        g_out[...] = g
        d_out[...] = delta
        m_out[...] = mn
        v_out[...] = vn

    full = pl.BlockSpec((R, C), lambda i: (0, 0))
    shape = jax.ShapeDtypeStruct((R, C), F32)
    return pl.pallas_call(
        body, out_shape=(shape,) * 4, grid=(1,),
        in_specs=[pl.BlockSpec((N_DEV, R, C), lambda i: (0, 0, 0)), full, full, full], out_specs=(full,) * 4,
        name=name, compiler_params=_params(1))(gathered, w, m, v)


def _pack_small(parts, D):
    g1, gmix, g2, gof, gos, bf, gqf, gkf, gqs, gks, sinks = [p.reshape(-1).astype(F32) for p in parts]
    row3 = jnp.concatenate([gof, gos])
    row4 = jnp.zeros((D,), F32)
    for slot, vec in enumerate((bf, gqf, gkf, gqs, gks, sinks)):
        row4 = lax.dynamic_update_slice(row4, vec, (slot * LANES,))
    zero = jnp.zeros((D,), F32)
    return jnp.stack([g1, gmix, g2, row3, row4, zero, zero, zero])


def _unpack_small(packed, D, H):
    Dh = D // 2
    row4 = packed[4]
    short = [row4[s * LANES:s * LANES + n] for s, n in enumerate((H, HEAD_DIM, HEAD_DIM, HEAD_DIM, HEAD_DIM, H))]
    vecs = [packed[0], packed[1], packed[2], packed[3, :Dh], packed[3, Dh:]] + short
    return [v[None, :] for v in vecs]


def kernel(x, positions, norm_ffn1_g, ffn1_w_gate, ffn1_w_up, ffn1_w_down, norm_mix_g, w_in, b_forget, fox_q_norm_g, fox_k_norm_g, swa_q_norm_g, swa_k_norm_g, swa_sinks, out_norm_fox_g, out_norm_swa_g, w_out, norm_ffn2_g, ffn2_w_gate, ffn2_w_up, ffn2_w_down, loss_target, m_norm_ffn1_g, m_ffn1_w_gate, m_ffn1_w_up, m_ffn1_w_down, m_norm_mix_g, m_w_in, m_b_forget, m_fox_q_norm_g, m_fox_k_norm_g, m_swa_q_norm_g, m_swa_k_norm_g, m_swa_sinks, m_out_norm_fox_g, m_out_norm_swa_g, m_w_out, m_norm_ffn2_g, m_ffn2_w_gate, m_ffn2_w_up, m_ffn2_w_down, v_norm_ffn1_g, v_ffn1_w_gate, v_ffn1_w_up, v_ffn1_w_down, v_norm_mix_g, v_w_in, v_b_forget, v_fox_q_norm_g, v_fox_k_norm_g, v_swa_q_norm_g, v_swa_k_norm_g, v_swa_sinks, v_out_norm_fox_g, v_out_norm_swa_g, v_w_out, v_norm_ffn2_g, v_ffn2_w_gate, v_ffn2_w_up, v_ffn2_w_down):
    xs = x[0]
    target = loss_target[0]
    T, D = xs.shape
    Dh = D // 2
    H = Dh // HEAD_DIM
    HP = H // 2
    KVW = (H // GQA_GROUP) * HEAD_DIM
    KVB = KVW // LANES
    MAIN = 4 * Dh + 2 * KVW
    F_OFF = 3 * Dh
    tm = min(ROW_TILE_CAP, T)
    tq = min(512, T)
    tk = min(512, T)
    nk = T // tk
    cx, cy, cc = _place()
    c_idx = jnp.reshape(cc, (1,)).astype(jnp.int32)
    chip_idx = jnp.reshape(2 * cx + cy, (1,)).astype(jnp.int32)
    other_idx = jnp.reshape(2 * (cx ^ (1 - cc)) + (cy ^ cc), (1,)).astype(jnp.int32)

    tr = jnp.transpose
    big_w = [tr(ffn1_w_gate[0]), tr(ffn1_w_up[0]), ffn1_w_down[0], tr(w_in[0]), w_out[0], tr(ffn2_w_gate[0]), tr(ffn2_w_up[0]),
             ffn2_w_down[0]]
    big_m = [tr(m_ffn1_w_gate[0]), tr(m_ffn1_w_up[0]), m_ffn1_w_down[0], tr(m_w_in[0]), m_w_out[0], tr(m_ffn2_w_gate[0]),
             tr(m_ffn2_w_up[0]), m_ffn2_w_down[0]]
    big_v = [tr(v_ffn1_w_gate[0]), tr(v_ffn1_w_up[0]), v_ffn1_w_down[0], tr(v_w_in[0]), v_w_out[0], tr(v_ffn2_w_gate[0]),
             tr(v_ffn2_w_up[0]), v_ffn2_w_down[0]]
    transposed = {"ffn1_w_gate", "ffn1_w_up", "w_in", "ffn2_w_gate", "ffn2_w_up"}
    names = ["ffn1_w_gate", "ffn1_w_up", "ffn1_w_down", "w_in", "w_out", "ffn2_w_gate", "ffn2_w_up", "ffn2_w_down"]
    sh = dict(zip(names, [w.astype(BF16) for w in big_w]))
    lane = jnp.arange(LANES)
    inv_freq = ROPE_THETA ** (-(2.0 * (lane % (HEAD_DIM // 2))).astype(F32) / HEAD_DIM)
    ang = positions[0].astype(F32)[:, None] * inv_freq[None, :]
    cos_t = jnp.cos(ang)
    sin_t = jnp.where((lane & (HEAD_DIM // 2)) == 0, -1.0, 1.0)[None, :] * jnp.sin(ang)
    rope = (cos_t, sin_t)

    def pair_gain(g, blocks):
        return jnp.tile(jnp.concatenate([g[0], g[0]])[None, None, :], (blocks, 1, 1))

    n1, (wg1,) = _rmsnorm_fwd("ffn1_norm", xs, norm_ffn1_g, tm, carry=_gather_carry([sh["ffn1_w_gate"]]))
    a1, (wu1,) = _ffn_gate("ffn1_gate", n1, wg1, tm, carry=_gather_carry([sh["ffn1_w_up"]]))
    (b1, hm1), (wd1,) = _ffn_up_only("ffn1_up", n1, wu1, a1, tm, carry=_gather_carry([sh["ffn1_w_down"]]))
    h1, (win_g,) = _ffn_down("ffn1_down", hm1, wd1, xs, tm, carry=_gather_carry([sh["w_in"]]))
    n_in = win_g.shape[1]
    win_t = win_g.reshape(N_DEV * n_in, D)
    win_main = jnp.concatenate([win_t[:F_OFF], win_t[F_OFF + H:]], axis=0)
    win_f = jnp.pad(win_t[F_OFF:F_OFF + H], ((0, LANES - H), (0, 0)))

    u = _rmsnorm_fwd("mix_norm", h1, norm_mix_g, tm)
    proj, (wout_g,) = _mm("mix_proj", u, win_main, tm, MAIN // 9, dims=NT, carry=_gather_carry([sh["w_out"]]))
    wout = wout_g.reshape(D, D)
    proj_f = _mm("mix_proj_forget", u, win_f, tm, LANES, dims=NT)
    scale = HEAD_DIM ** -0.5
    fox_gains = jnp.concatenate([pair_gain(fox_q_norm_g, HP), pair_gain(fox_k_norm_g, HP)])
    qk_f = _headnorm_fwd_scaled("fox_qk_norm", proj, 0, 2 * HP, fox_gains, T, scale, HP)
    v_f = proj[:, 2 * Dh:3 * Dh].astype(BF16)
    c_t, sg_t = _forget_fwd("forget_gates", proj_f[:, :H].T, b_forget.reshape(H, 1))
    crow = c_t.reshape(H, nk, 1, tk)
    (o_fox, lse_fa, lse_fb), (wg2, wu2) = _fox_fwd("fox_attention", qk_f, v_f, crow, tq, tk,
                                                   carry=_gather_carry([sh["ffn2_w_gate"], sh["ffn2_w_up"]]))

    swa_q_gains = pair_gain(swa_q_norm_g, HP)
    swa_k_gains = pair_gain(swa_k_norm_g, KVB)
    q_s = _headnorm_fwd("swa_q_norm", proj, 3 * HP, HP, swa_q_gains, T, scale, rope=rope)
    k_d = _headnorm_fwd("swa_k_norm", proj, 4 * HP, KVB, swa_k_gains, T, 1.0, rope=rope, dup=True)
    v_s = proj[:, 4 * Dh + KVW:].astype(BF16).reshape(T, H // GQA_GROUP, 1, HEAD_DIM)
    v_d = jnp.broadcast_to(v_s, (T, H // GQA_GROUP, 2, HEAD_DIM)).reshape(T, 2 * KVW)
    sinks3 = swa_sinks.reshape(H, 1, 1)
    o_swa, lse_sa, lse_sb = _swa_fwd("swa_attention", q_s, k_d, v_d, sinks3)

    on = _outnorm_fwd("out_norm", o_fox, o_swa, out_norm_fox_g, out_norm_swa_g, tm)
    h2 = _mm("mix_out", on, wout, tm, min(512, D), resid=h1)

    n2 = _rmsnorm_fwd("ffn2_norm", h2, norm_ffn2_g, tm)
    (a2, b2, hm2), (wd2,) = _ffn_up("ffn2_up", n2, wg2, wu2, tm, carry=_gather_carry([sh["ffn2_w_down"]]))
    y = _ffn_down("ffn2_down", hm2, wd2, h2, tm)
    dy, dyh, sq = _loss_grad("loss_grad", y, target, min(256, T))
    loss = lax.psum(0.5 * sq[0, 0] / D, ("x", "y", "c"))

    J, Fs, _ = wg2.shape
    aspec = pl.BlockSpec((None, tm, Fs), lambda i, j: (j, i, 0))
    wspec = pl.BlockSpec((None, Fs, D), lambda i, j: (j, 0, 0))
    got = {}

    def pair_sums(keys, grads, received):
        return [_pair_add("sum_" + nm, g, r, c_idx) for nm, g, r in zip(keys, grads, received)]

    def relay_sums(keys, sums, hop1):
        out = []
        for i, (nm, s) in enumerate(zip(keys, sums)):
            got[nm] = [hop1[2 * i]]
            out.append(_relay_add("relay_" + nm, s, hop1[2 * i + 1], other_idx))
        return out

    def arrived(keys, hop2):
        for nm, blk in zip(keys, hop2):
            got[nm].append(blk)

    dwd2 = _wgrad_down("ffn2_wgrad_down", hm2, dyh, min(1024, D))
    (da2, db2), (sib_d2,) = _ffn_bwd_mid("ffn2_bwd_mid", dyh, wd2, a2, b2, tm, carry=_sibling_carry([dwd2]))
    (sum_wd2,) = pair_sums(names[7:8], [dwd2], [sib_d2])
    (dwg2, dwu2), hop1 = _wgrad_up("ffn2_wgrad_up", n2, da2, db2, min(1024, D), carry=_to_partner_carry([sum_wd2]))
    (t_wd2,) = relay_sums(names[7:8], [sum_wd2], hop1)
    dn2, (via_wd2, *sib2) = _reduce_mm("ffn2_bwd_in", [(da2, aspec, wg2, wspec), (db2, aspec, wu2, wspec)], [], NN, T, D, tm, J,
                                       carry=_join(_to_other_carry([t_wd2]), _sibling_carry([dwg2, dwu2])))
    arrived(names[7:8], [via_wd2])
    dh2, dg_ffn2, dh2b = _rmsnorm_bwd("ffn2_norm_bwd", dn2, h2, norm_ffn2_g, dy, min(256, T), 1.0)
    sum_wg2, sum_wu2 = pair_sums(names[5:7], [dwg2, dwu2], sib2)

    dwout = _wgrad_2d("mix_out_wgrad", on, dh2b, min(512, D), min(1024, D))
    dwout_g = dwout.reshape(N_DEV, D // N_DEV, D)
    do_fox, dg_of = _outnorm_bwd("out_norm_bwd_fox", dh2b, wout, 0, o_fox, out_norm_fox_g, tm)
    do_swa, dg_os = _outnorm_bwd("out_norm_bwd_swa", dh2b, wout, 1, o_swa, out_norm_swa_g, tm)

    (dq_f, dk_f, dv_f, dc_a, dc_b, dr_a, dr_b), (*hop1, sib_wout) = _fox_bwd(
        "fox_attention_bwd", qk_f, v_f, o_fox, do_fox, crow, lse_fa, lse_fb, tq, tk,
        carry=_join(_to_partner_carry([sum_wg2, sum_wu2]), _sibling_carry([dwout_g])))
    t_wg2, t_wu2 = relay_sums(names[5:7], [sum_wg2, sum_wu2], hop1)
    (sum_wout,) = pair_sums(names[4:5], [dwout_g], [sib_wout])
    dqk_f = jnp.concatenate([dq_f, dk_f], axis=1)
    dqk_raw, dg_fox = _headnorm_bwd("fox_qk_norm_bwd", dqk_f, proj, 0, 2 * HP, fox_gains, HP, T, 1.0)
    dct = jnp.stack([dc_a.reshape(HP, T), dc_b.reshape(HP, T)], axis=1).reshape(H, T)
    drt = jnp.stack([dr_a.reshape(HP, T), dr_b.reshape(HP, T)], axis=1).reshape(H, T)
    dz_t, db_f = _forget_bwd("forget_gates_bwd", dct, drt, sg_t)

    (dq_s, dk_p, dv_p, dsink_a, dsink_b), hop2 = _swa_bwd(
        "swa_attention_bwd", q_s, k_d, v_d, sinks3, o_swa, do_swa, lse_sa, lse_sb, carry=_to_other_carry([t_wg2, t_wu2]))
    arrived(names[5:7], hop2)
    dqs_raw, dg_sq = _headnorm_bwd("swa_q_norm_bwd", dq_s, proj, 3 * HP, HP, swa_q_gains, HP, T, 1.0, rope=rope)
    dks_raw, dg_sk = _headnorm_bwd("swa_k_norm_bwd", dk_p, proj, 4 * HP, KVB, swa_k_gains, KVB, T, 1.0, rope=rope, fold=True)
    dvs_raw, _ = _headnorm_bwd("swa_v_fold", dv_p, None, 0, KVB, None, KVB, T, 1.0, fold=True, norm=False)

    dproj = jnp.concatenate([dqk_raw, dv_f.astype(BF16), dqs_raw, dks_raw, dvs_raw], axis=1)
    dproj_f = jnp.pad(dz_t.T, ((0, 0), (0, LANES - H))).astype(BF16)
    dwin_main, hop1 = _wgrad_2d("mix_proj_wgrad", dproj, u, MAIN // 9, min(1024, D), carry=_to_partner_carry([sum_wout]))
    (t_wout,) = relay_sums(names[4:5], [sum_wout], hop1)
    dwin_f = _wgrad_2d("mix_proj_forget_wgrad", dproj_f, u, LANES, min(1024, D))
    dwin_t = jnp.concatenate([dwin_main[:F_OFF], dwin_f[:H], dwin_main[F_OFF:]], axis=0)
    dwin_g = dwin_t.reshape(N_DEV, n_in, D)
    tkb = MAIN // 9
    du, (via_wout, sib_win) = _reduce_mm(
        "mix_bwd_in",
        [(dproj, pl.BlockSpec((tm, tkb), lambda i, r: (i, r)), win_main, pl.BlockSpec((tkb, D), lambda i, r: (r, 0)))],
        [(dproj_f, pl.BlockSpec((tm, LANES), lambda i, r: (i, 0)), win_f, pl.BlockSpec((LANES, D), lambda i, r: (0, 0)))],
        NN, T, D, tm, 9, carry=_join(_to_other_carry([t_wout]), _sibling_carry([dwin_g])))
    arrived(names[4:5], [via_wout])
    dh1, dg_mix, dh1h = _rmsnorm_bwd("mix_norm_bwd", du, h1, norm_mix_g, dh2, min(256, T), 0.5)
    (sum_win,) = pair_sums(names[3:4], [dwin_g], [sib_win])

    (da1, db1), hop1 = _ffn_bwd_mid("ffn1_bwd_mid", dh1h, wd1, a1, b1, tm, carry=_to_partner_carry([sum_win]))
    (t_win,) = relay_sums(names[3:4], [sum_win], hop1)
    dwg1, hop2 = _wgrad_down("ffn1_wgrad_gate", da1, n1, min(1024, D), carry=_to_other_carry([t_win]))
    arrived(names[3:4], hop2)
    dwu1, (sib_g,) = _wgrad_down("ffn1_wgrad_up", db1, n1, min(1024, D), carry=_sibling_carry([dwg1]))
    (sum_wg1,) = pair_sums(names[0:1], [dwg1], [sib_g])
    dwd1, (*hop1, sib_u) = _wgrad_down("ffn1_wgrad_down", hm1, dh1h, min(1024, D),
                                       carry=_join(_to_partner_carry([sum_wg1]), _sibling_carry([dwu1])))
    (t_wg1,) = relay_sums(names[0:1], [sum_wg1], hop1)
    (sum_wu1,) = pair_sums(names[1:2], [dwu1], [sib_u])
    dn1, (via_wg1, *hop1, sib_d) = _reduce_mm(
        "ffn1_bwd_in", [(da1, aspec, wg1, wspec), (db1, aspec, wu1, wspec)], [], NN, T, D, tm, J,
        carry=_join(_to_other_carry([t_wg1]), _to_partner_carry([sum_wu1]), _sibling_carry([dwd1])))
    arrived(names[0:1], [via_wg1])
    (t_wu1,) = relay_sums(names[1:2], [sum_wu1], hop1)
    (sum_wd1,) = pair_sums(names[2:3], [dwd1], [sib_d])
    via_wu1, *hop1 = _run_carry("grads_exchange_a", _join(_to_other_carry([t_wu1]), _to_partner_carry([sum_wd1])))
    arrived(names[1:2], [via_wu1])
    (t_wd1,) = relay_sums(names[2:3], [sum_wd1], hop1)
    arrived(names[2:3], _run_carry("grads_exchange_b", _to_other_carry([t_wd1])))
    dx, dg_ffn1 = _rmsnorm_bwd("ffn1_norm_bwd", dn1, xs, norm_ffn1_g, dh1, min(256, T), None)

    chip_sums = [sum_wg1, sum_wu1, sum_wd1, sum_win, sum_wout, sum_wg2, sum_wu2, sum_wd2]
    big_out = [_adam_shard("adam_" + nm, s, got[nm], w, m, v, chip_idx)
               for nm, s, w, m, v in zip(names, chip_sums, big_w, big_m, big_v)]

    dsinks = jnp.stack([dsink_a.reshape(HP), dsink_b.reshape(HP)], axis=1).reshape(H)
    small_g = [dg_ffn1, dg_mix, dg_ffn2, dg_of, dg_os, db_f, dg_fox[0, 0, :HEAD_DIM], dg_fox[1, 0, :HEAD_DIM],
               dg_sq[0, 0, :HEAD_DIM], dg_sk[0, 0, :HEAD_DIM], dsinks]
    small_w = [norm_ffn1_g, norm_mix_g, norm_ffn2_g, out_norm_fox_g, out_norm_swa_g, b_forget, fox_q_norm_g, fox_k_norm_g,
               swa_q_norm_g, swa_k_norm_g, swa_sinks]
    small_m = [m_norm_ffn1_g, m_norm_mix_g, m_norm_ffn2_g, m_out_norm_fox_g, m_out_norm_swa_g, m_b_forget, m_fox_q_norm_g,
               m_fox_k_norm_g, m_swa_q_norm_g, m_swa_k_norm_g, m_swa_sinks]
    small_v = [v_norm_ffn1_g, v_norm_mix_g, v_norm_ffn2_g, v_out_norm_fox_g, v_out_norm_swa_g, v_b_forget, v_fox_q_norm_g,
               v_fox_k_norm_g, v_swa_q_norm_g, v_swa_k_norm_g, v_swa_sinks]
    gathered = _gather_small(_pack_small(small_g, D))
    small_out = _adam_small("adam_small", gathered, _pack_small(small_w, D), _pack_small(small_m, D), _pack_small(small_v, D))
    small_out = [_unpack_small(p, D, H) for p in small_out]

    order = ["norm_ffn1_g", "ffn1_w_gate", "ffn1_w_up", "ffn1_w_down", "norm_mix_g", "w_in", "b_forget", "fox_q_norm_g", "fox_k_norm_g",
             "swa_q_norm_g", "swa_k_norm_g", "swa_sinks", "out_norm_fox_g", "out_norm_swa_g", "w_out", "norm_ffn2_g",
             "ffn2_w_gate", "ffn2_w_up", "ffn2_w_down"]
    small_names = ["norm_ffn1_g", "norm_mix_g", "norm_ffn2_g", "out_norm_fox_g", "out_norm_swa_g", "b_forget", "fox_q_norm_g",
                   "fox_k_norm_g", "swa_q_norm_g", "swa_k_norm_g", "swa_sinks"]
    result = [loss, dx[None]]
    for kind in range(4):
        for nm in order:
            if nm in names:
                leaf = big_out[names.index(nm)][kind]
                result.append((tr(leaf) if nm in transposed else leaf)[None])
            else:
                result.append(small_out[kind][small_names.index(nm)])
    return tuple(result)


def _headnorm_fwd_scaled(name, proj, col_off, ncb, gains, tm, scale, n_scaled):
    T = proj.shape[0]

    def body(x_ref, g_ref, o_ref):
        xv = x_ref[...]
        lo = _lane_lo(xv.shape)
        y = xv * _head_rstd(xv, lo) * g_ref[...]
        y = y * jnp.where(pl.program_id(0) < n_scaled, scale, 1.0)
        o_ref[...] = y.astype(BF16)

    return pl.pallas_call(
        body, out_shape=jax.ShapeDtypeStruct((T, ncb * LANES), BF16), grid=(ncb, T // tm),
        in_specs=[pl.BlockSpec((tm, LANES), lambda c, i: (i, col_off + c)), pl.BlockSpec((None, 1, LANES), lambda c, i: (c, 0, 0))],
        out_specs=pl.BlockSpec((tm, LANES), lambda c, i: (i, c)), name=name, compiler_params=_params(2))(proj, gains)
```

```python
import functools

import jax
import jax.numpy as jnp
from jax import lax
from jax.experimental import pallas as pl
from jax.experimental.pallas import tpu as pltpu

F32 = jnp.float32
BF16 = jnp.bfloat16

HEAD_DIM = 64
LANES = 128
WINDOW = 128
GQA_GROUP = 4
EPS = 1e-6
ROPE_THETA = 10000.0
ADAM_LR = 0.001
ADAM_B1 = 0.9
ADAM_B2 = 0.999
ADAM_EPS = 1e-08
ADAM_WD = 0.01
ADAM_STEP = 10
N_DEV = 8
NEG = -1e30
VMEM_LIMIT_V7X = 48 * 1024 * 1024
ROW_TILE_CAP = 512
MESH = pl.DeviceIdType.MESH

NN = (((1,), (0,)), ((), ()))
NT = (((1,), (1,)), ((), ()))
TN = (((0,), (0,)), ((), ()))


def _dot(a, b, dims):
    return lax.dot_general(a, b, dims, preferred_element_type=F32)


def _params(n_axes):
    return pltpu.CompilerParams(dimension_semantics=("arbitrary",) * n_axes, vmem_limit_bytes=VMEM_LIMIT_V7X)


def _row_tile(rows, cap=ROW_TILE_CAP):
    best = None
    for t in range(16, min(rows, cap) + 1, 16):
        if rows % t == 0:
            best = t
    return best or rows


def _lane_lo(shape):
    return lax.broadcasted_iota(jnp.int32, shape, len(shape) - 1) < HEAD_DIM


def _keep(sel, x):
    return jnp.where(sel, x.astype(F32), 0.0).astype(BF16)


_HBM = pl.BlockSpec(memory_space=pltpu.HBM)


class _Carry:
    def __init__(self, inputs, out_shapes, scratch, start, finish, middle=None):
        self.inputs, self.out_shapes, self.scratch = list(inputs), list(out_shapes), list(scratch)
        self.start, self.finish, self.middle = start, finish, middle or (lambda ins, outs, scr: None)


def _join(*carries):
    def hook(which):
        def run(ins, outs, scr):
            i = o = s = 0
            for c in carries:
                ni, no, ns = len(c.inputs), len(c.out_shapes), len(c.scratch)
                getattr(c, which)(ins[i:i + ni], outs[o:o + no], scr[s:s + ns])
                i, o, s = i + ni, o + no, s + ns
        return run

    return _Carry([a for c in carries for a in c.inputs], [a for c in carries for a in c.out_shapes],
                  [a for c in carries for a in c.scratch], hook("start"), hook("finish"), hook("middle"))


def _call(body, *, name, grid, in_specs, out_specs, out_shape, args, scratch_shapes=(), carry=None):
    params = _params(len(grid))
    if carry is None:
        return pl.pallas_call(body, out_shape=out_shape, grid=grid, in_specs=list(in_specs), out_specs=out_specs,
                              scratch_shapes=list(scratch_shapes), name=name, compiler_params=params)(*args)
    single = not isinstance(out_shape, (tuple, list))
    shapes = (out_shape,) if single else tuple(out_shape)
    specs = (out_specs,) if single else tuple(out_specs)
    n_in, n_out, n_scr = len(args), len(shapes), len(scratch_shapes)
    c_in, c_out = len(carry.inputs), len(carry.out_shapes)

    def wrapped(*refs):
        ins, c_ins = refs[:n_in], refs[n_in:n_in + c_in]
        o0 = n_in + c_in
        outs, c_outs = refs[o0:o0 + n_out], refs[o0 + n_out:o0 + n_out + c_out]
        s0 = o0 + n_out + c_out
        scr, c_scr = refs[s0:s0 + n_scr], refs[s0 + n_scr:]
        step, total = pl.program_id(0), grid[0]
        for ax in range(1, len(grid)):
            step, total = step * grid[ax] + pl.program_id(ax), total * grid[ax]

        @pl.when(step == 0)
        def _():
            carry.start(c_ins, c_outs, c_scr)

        @pl.when(step == total // 2)
        def _():
            carry.middle(c_ins, c_outs, c_scr)

        body(*ins, *outs, *scr)

        @pl.when(step == total - 1)
        def _():
            carry.finish(c_ins, c_outs, c_scr)

    res = pl.pallas_call(
        wrapped, out_shape=shapes + tuple(carry.out_shapes), grid=grid, in_specs=list(in_specs) + [_HBM] * c_in,
        out_specs=specs + (_HBM,) * c_out, scratch_shapes=list(scratch_shapes) + carry.scratch, name=name,
        compiler_params=params)(*args, *carry.inputs)
    main = res[:n_out]
    return (main[0] if single else tuple(main)), tuple(res[n_out:])


def _rms_bwd(dn, x, g):
    r = lax.rsqrt(jnp.mean(x * x, axis=-1, keepdims=True) + EPS)
    xh = x * r
    dxh = dn * g
    dx = r * (dxh - xh * jnp.mean(dxh * xh, axis=-1, keepdims=True))
    return dx, jnp.sum(dn * xh, axis=0, keepdims=True)


def _rmsnorm_fwd(name, x, g, tm, carry=None):
    T, D = x.shape

    def body(x_ref, g_ref, o_ref):
        xf = x_ref[...]
        r = lax.rsqrt(jnp.mean(xf * xf, axis=-1, keepdims=True) + EPS)
        o_ref[...] = (xf * r * g_ref[...]).astype(BF16)

    return _call(
        body, name=name, grid=(T // tm,), out_shape=jax.ShapeDtypeStruct((T, D), BF16),
        in_specs=[pl.BlockSpec((tm, D), lambda i: (i, 0)), pl.BlockSpec((1, D), lambda i: (0, 0))],
        out_specs=pl.BlockSpec((tm, D), lambda i: (i, 0)), args=[x, g], carry=carry)


def _outnorm_fwd(name, o_fox, o_swa, g_fox, g_swa, tm):
    T, Dh = o_fox.shape

    def body(a_ref, b_ref, ga_ref, gb_ref, o_ref):
        for ref, g_ref, lo in ((a_ref, ga_ref, 0), (b_ref, gb_ref, Dh)):
            xf = ref[...]
            r = lax.rsqrt(jnp.mean(xf * xf, axis=-1, keepdims=True) + EPS)
            o_ref[:, lo:lo + Dh] = (xf * r * g_ref[...]).astype(BF16)

    row = pl.BlockSpec((tm, Dh), lambda i: (i, 0))
    gain = pl.BlockSpec((1, Dh), lambda i: (0, 0))
    return pl.pallas_call(
        body, out_shape=jax.ShapeDtypeStruct((T, 2 * Dh), BF16), grid=(T // tm,),
        in_specs=[row, row, gain, gain], out_specs=pl.BlockSpec((tm, 2 * Dh), lambda i: (i, 0)),
        name=name, compiler_params=_params(1))(o_fox, o_swa, g_fox, g_swa)


def _outnorm_bwd(name, dhb, wout, half, o, g, tm):
    T, D = dhb.shape
    Dh = o.shape[1]

    def body(a_ref, w_ref, o_ref, g_ref, do_ref, dg_ref):
        don = _dot(a_ref[...], w_ref[...], NT)
        dx, dg = _rms_bwd(don, o_ref[...], g_ref[...])
        do_ref[...] = dx.astype(BF16)

        @pl.when(pl.program_id(0) == 0)
        def _():
            dg_ref[...] = dg

        @pl.when(pl.program_id(0) > 0)
        def _():
            dg_ref[...] += dg

    return pl.pallas_call(
        body, out_shape=(jax.ShapeDtypeStruct((T, Dh), BF16), jax.ShapeDtypeStruct((1, Dh), F32)), grid=(T // tm,),
        in_specs=[pl.BlockSpec((tm, D), lambda i: (i, 0)), pl.BlockSpec((Dh, D), lambda i: (half, 0)),
                  pl.BlockSpec((tm, Dh), lambda i: (i, 0)), pl.BlockSpec((1, Dh), lambda i: (0, 0))],
        out_specs=(pl.BlockSpec((tm, Dh), lambda i: (i, 0)), pl.BlockSpec((1, Dh), lambda i: (0, 0))),
        name=name, compiler_params=_params(1))(dhb, wout, o, g)


def _mm(name, a, b, tm, tn, dims=NN, resid=None, carry=None):
    M, K = a.shape
    transposed = dims == NT
    N = b.shape[0] if transposed else b.shape[1]

    def body(*refs):
        if resid is None:
            a_ref, b_ref, o_ref = refs
            o_ref[...] = _dot(a_ref[...], b_ref[...], dims)
        else:
            a_ref, b_ref, r_ref, o_ref = refs
            o_ref[...] = r_ref[...] + _dot(a_ref[...], b_ref[...], dims)

    ospec = pl.BlockSpec((tm, tn), lambda n, i: (i, n))
    bspec = pl.BlockSpec((tn, K), lambda n, i: (n, 0)) if transposed else pl.BlockSpec((K, tn), lambda n, i: (0, n))
    in_specs = [pl.BlockSpec((tm, K), lambda n, i: (i, 0)), bspec]
    args = [a, b]
    if resid is not None:
        in_specs.append(ospec)
        args.append(resid)
    return _call(body, name=name, grid=(N // tn, M // tm), in_specs=in_specs, out_specs=ospec,
                 out_shape=jax.ShapeDtypeStruct((M, N), F32), args=args, carry=carry)


def _wgrad_2d(name, a, b, tmm, tn, carry=None):
    T, M = a.shape
    N = b.shape[1]

    def body(a_ref, b_ref, o_ref):
        o_ref[...] = _dot(a_ref[...], b_ref[...], TN).astype(BF16)

    return _call(
        body, name=name, grid=(M // tmm, N // tn), out_shape=jax.ShapeDtypeStruct((M, N), BF16),
        in_specs=[pl.BlockSpec((T, tmm), lambda m, n: (0, m)), pl.BlockSpec((T, tn), lambda m, n: (0, n))],
        out_specs=pl.BlockSpec((tmm, tn), lambda m, n: (m, n)), args=[a, b], carry=carry)


def _wgrad_down(name, hm, df, tn, carry=None):
    J, T, Fs = hm.shape
    D = df.shape[1]

    def body(a_ref, b_ref, o_ref):
        o_ref[...] = _dot(a_ref[...], b_ref[...], TN).astype(BF16)

    return _call(
        body, name=name, grid=(J, D // tn), out_shape=jax.ShapeDtypeStruct((J, Fs, D), BF16),
        in_specs=[pl.BlockSpec((None, T, Fs), lambda j, n: (j, 0, 0)), pl.BlockSpec((T, tn), lambda j, n: (0, n))],
        out_specs=pl.BlockSpec((None, Fs, tn), lambda j, n: (j, 0, n)), args=[hm, df], carry=carry)


def _wgrad_up(name, n, da, db, tn, carry=None):
    T, D = n.shape
    J, _, Fs = da.shape

    def body(n_ref, da_ref, db_ref, og_ref, ou_ref):
        nv = n_ref[...]
        og_ref[...] = _dot(da_ref[...], nv, TN).astype(BF16)
        ou_ref[...] = _dot(db_ref[...], nv, TN).astype(BF16)

    act = pl.BlockSpec((None, T, Fs), lambda j, m: (j, 0, 0))
    out = pl.BlockSpec((None, Fs, tn), lambda j, m: (j, 0, m))
    shape = jax.ShapeDtypeStruct((J, Fs, D), BF16)
    return _call(
        body, name=name, grid=(J, D // tn), out_shape=(shape, shape),
        in_specs=[pl.BlockSpec((T, tn), lambda j, m: (0, m)), act, act], out_specs=(out, out),
        args=[n, da, db], carry=carry)


def _reduce_mm(name, pairs, once, dims, T, D, tm, steps, init=None, carry=None):
    n_pairs = len(pairs)
    n_once = len(once)
    n_mm = 2 * (n_pairs + n_once)

    def body(*refs):
        pr = refs[:2 * n_pairs]
        on = refs[2 * n_pairs:n_mm]
        o_ref, acc = refs[-2:]
        r = pl.program_id(1)
        part = _dot(pr[0][...], pr[1][...], dims)
        for p in range(1, n_pairs):
            part = part + _dot(pr[2 * p][...], pr[2 * p + 1][...], dims)

        @pl.when(r == 0)
        def _():
            acc[...] = part if init is None else refs[n_mm][...] + part

        @pl.when(r > 0)
        def _():
            acc[...] += part

        @pl.when(r == steps - 1)
        def _():
            dn = acc[...]
            for p in range(n_once):
                dn = dn + _dot(on[2 * p][...], on[2 * p + 1][...], dims)
            o_ref[...] = dn

    in_specs, args = [], []
    for a, a_spec, w, w_spec in list(pairs) + list(once):
        in_specs += [a_spec, w_spec]
        args += [a, w]
    row = pl.BlockSpec((tm, D), lambda i, r: (i, 0))
    if init is not None:
        in_specs.append(row)
        args.append(init)
    return _call(body, name=name, grid=(T // tm, steps), in_specs=in_specs, out_specs=row, out_shape=jax.ShapeDtypeStruct((T, D), F32),
                 args=args, scratch_shapes=[pltpu.VMEM((tm, D), F32)], carry=carry)


def _rmsnorm_bwd(name, dn, x, g, dh, tm, bf16_scale, carry=None):
    T, D = x.shape
    emit_bf16 = bf16_scale is not None

    def body(dn_ref, x_ref, g_ref, dh_ref, *outs):
        dxn, dg = _rms_bwd(dn_ref[...], x_ref[...], g_ref[...])
        dx = dh_ref[...] + dxn
        outs[0][...] = dx
        if emit_bf16:
            outs[2][...] = (bf16_scale * dx).astype(BF16)

        @pl.when(pl.program_id(0) == 0)
        def _():
            outs[1][...] = dg

        @pl.when(pl.program_id(0) > 0)
        def _():
            outs[1][...] += dg

    row = pl.BlockSpec((tm, D), lambda i: (i, 0))
    gain = pl.BlockSpec((1, D), lambda i: (0, 0))
    out_shape = [jax.ShapeDtypeStruct((T, D), F32), jax.ShapeDtypeStruct((1, D), F32)]
    out_specs = [row, gain]
    if emit_bf16:
        out_shape.append(jax.ShapeDtypeStruct((T, D), BF16))
        out_specs.append(row)
    return _call(body, name=name, grid=(T // tm,), in_specs=[row, row, gain, row], out_specs=tuple(out_specs),
                 out_shape=tuple(out_shape), args=[dn, x, g, dh], carry=carry)


def _loss_grad(name, y, target, tm):
    T, D = y.shape

    def body(y_ref, t_ref, dy_ref, dyh_ref, sq_ref):
        diff = y_ref[...] - t_ref[...]
        sq = jnp.sum(jnp.sum(diff * diff, axis=1, keepdims=True), axis=0, keepdims=True)
        dy = diff * (1.0 / D)
        dy_ref[...] = dy
        dyh_ref[...] = (0.5 * dy).astype(BF16)

        @pl.when(pl.program_id(0) == 0)
        def _():
            sq_ref[...] = sq

        @pl.when(pl.program_id(0) > 0)
        def _():
            sq_ref[...] += sq

    row = pl.BlockSpec((tm, D), lambda i: (i, 0))
    return pl.pallas_call(
        body, out_shape=(jax.ShapeDtypeStruct((T, D), F32), jax.ShapeDtypeStruct((T, D), BF16), jax.ShapeDtypeStruct((1, 1), F32)),
        grid=(T // tm,), in_specs=[row, row], out_specs=(row, row, pl.BlockSpec((1, 1), lambda i: (0, 0))),
        name=name, compiler_params=_params(1))(y, target)


def _ffn_up(name, n, wg, wu, tm, carry=None):
    T, D = n.shape
    J, Fs, _ = wg.shape

    def body(n_ref, wg_ref, wu_ref, a_ref, b_ref, h_ref):
        xv = n_ref[...]
        a = _dot(xv, wg_ref[...], NT)
        b = _dot(xv, wu_ref[...], NT)
        a_ref[...] = a.astype(BF16)
        b_ref[...] = b.astype(BF16)
        h_ref[...] = (a * jax.nn.sigmoid(a) * b).astype(BF16)

    act = jax.ShapeDtypeStruct((J, T, Fs), BF16)
    wspec = pl.BlockSpec((None, Fs, D), lambda j, i: (j, 0, 0))
    aspec = pl.BlockSpec((None, tm, Fs), lambda j, i: (j, i, 0))
    return _call(
        body, name=name, grid=(J, T // tm), out_shape=(act, act, act),
        in_specs=[pl.BlockSpec((tm, D), lambda j, i: (i, 0)), wspec, wspec], out_specs=(aspec, aspec, aspec),
        args=[n, wg, wu], carry=carry)


def _ffn_gate(name, n, wg, tm, carry=None):
    T, D = n.shape
    J, Fs, _ = wg.shape

    def body(n_ref, wg_ref, a_ref):
        a_ref[...] = _dot(n_ref[...], wg_ref[...], NT).astype(BF16)

    aspec = pl.BlockSpec((None, tm, Fs), lambda j, i: (j, i, 0))
    return _call(
        body, name=name, grid=(J, T // tm), out_shape=jax.ShapeDtypeStruct((J, T, Fs), BF16),
        in_specs=[pl.BlockSpec((tm, D), lambda j, i: (i, 0)), pl.BlockSpec((None, Fs, D), lambda j, i: (j, 0, 0))],
        out_specs=aspec, args=[n, wg], carry=carry)


def _ffn_up_only(name, n, wu, a, tm, carry=None):
    T, D = n.shape
    J, Fs, _ = wu.shape

    def body(n_ref, wu_ref, a_ref, b_ref, h_ref):
        b = _dot(n_ref[...], wu_ref[...], NT)
        a = a_ref[...].astype(F32)
        b_ref[...] = b.astype(BF16)
        h_ref[...] = (a * jax.nn.sigmoid(a) * b).astype(BF16)

    act = jax.ShapeDtypeStruct((J, T, Fs), BF16)
    aspec = pl.BlockSpec((None, tm, Fs), lambda j, i: (j, i, 0))
    return _call(
        body, name=name, grid=(J, T // tm), out_shape=(act, act),
        in_specs=[pl.BlockSpec((tm, D), lambda j, i: (i, 0)), pl.BlockSpec((None, Fs, D), lambda j, i: (j, 0, 0)), aspec],
        out_specs=(aspec, aspec), args=[n, wu, a], carry=carry)


def _ffn_down(name, hm, wd, resid, tm, carry=None):
    J, T, Fs = hm.shape
    D = wd.shape[2]

    def body(h_ref, w_ref, r_ref, o_ref, acc):
        j = pl.program_id(1)
        part = _dot(h_ref[...], w_ref[...], NN)

        @pl.when(j == 0)
        def _():
            acc[...] = part

        @pl.when(j > 0)
        def _():
            acc[...] += part

        @pl.when(j == J - 1)
        def _():
            o_ref[...] = r_ref[...] + 0.5 * acc[...]

    row = pl.BlockSpec((tm, D), lambda i, j: (i, 0))
    return _call(
        body, name=name, grid=(T // tm, J), out_shape=jax.ShapeDtypeStruct((T, D), F32),
        in_specs=[pl.BlockSpec((None, tm, Fs), lambda i, j: (j, i, 0)), pl.BlockSpec((None, Fs, D), lambda i, j: (j, 0, 0)), row],
        out_specs=row, scratch_shapes=[pltpu.VMEM((tm, D), F32)], args=[hm, wd, resid], carry=carry)


def _ffn_bwd_mid(name, dfh, wd, a, b, tm, carry=None):
    T, D = dfh.shape
    J, Fs, _ = wd.shape

    def body(df_ref, w_ref, a_ref, b_ref, da_ref, db_ref):
        dhm = _dot(df_ref[...], w_ref[...], NT)
        av = a_ref[...].astype(F32)
        bv = b_ref[...].astype(F32)
        sg = jax.nn.sigmoid(av)
        da_ref[...] = (dhm * bv * (sg * (1.0 + av * (1.0 - sg)))).astype(BF16)
        db_ref[...] = (dhm * (av * sg)).astype(BF16)

    act = jax.ShapeDtypeStruct((J, T, Fs), BF16)
    aspec = pl.BlockSpec((None, tm, Fs), lambda j, i: (j, i, 0))
    return _call(
        body, name=name, grid=(J, T // tm), out_shape=(act, act),
        in_specs=[pl.BlockSpec((tm, D), lambda j, i: (i, 0)), pl.BlockSpec((None, Fs, D), lambda j, i: (j, 0, 0)), aspec, aspec],
        out_specs=(aspec, aspec), args=[dfh, wd, a, b], carry=carry)


def _rot_half(y, lane):
    first = (lane & (HEAD_DIM // 2)) == 0
    return jnp.where(first, pltpu.roll(y, LANES - HEAD_DIM // 2, 1), pltpu.roll(y, HEAD_DIM // 2, 1))


def _head_rstd(x, lo):
    sq = x * x
    ss_a = jnp.sum(jnp.where(lo, sq, 0.0), axis=-1, keepdims=True)
    ss_b = jnp.sum(jnp.where(lo, 0.0, sq), axis=-1, keepdims=True)
    return lax.rsqrt(jnp.where(lo, ss_a, ss_b) * (1.0 / HEAD_DIM) + EPS)


def _headnorm_fwd(name, proj, col_off, ncb, gains, tm, scale, rope=None, dup=False):
    T = proj.shape[0]
    with_rope = rope is not None
    width = 2 * LANES if dup else LANES

    def body(*refs):
        if with_rope:
            x_ref, g_ref, cos_ref, sin_ref, o_ref = refs
        else:
            x_ref, g_ref, o_ref = refs
        xv = x_ref[...]
        lane = lax.broadcasted_iota(jnp.int32, xv.shape, 1)
        lo = lane < HEAD_DIM
        y = xv * _head_rstd(xv, lo) * g_ref[...]
        if with_rope:
            y = y * cos_ref[...] + _rot_half(y, lane) * sin_ref[...]
        y = y * scale
        if dup:
            sw = pltpu.roll(y, HEAD_DIM, 1)
            o_ref[:, :LANES] = jnp.where(lo, y, sw).astype(BF16)
            o_ref[:, LANES:] = jnp.where(lo, sw, y).astype(BF16)
        else:
            o_ref[...] = y.astype(BF16)

    in_specs = [pl.BlockSpec((tm, LANES), lambda c, i: (i, col_off + c)), pl.BlockSpec((None, 1, LANES), lambda c, i: (c, 0, 0))]
    args = [proj, gains]
    if with_rope:
        tab = pl.BlockSpec((tm, LANES), lambda c, i: (i, 0))
        in_specs += [tab, tab]
        args += list(rope)
    return pl.pallas_call(
        body, out_shape=jax.ShapeDtypeStruct((T, ncb * width), BF16), grid=(ncb, T // tm),
        in_specs=in_specs, out_specs=pl.BlockSpec((tm, width), lambda c, i: (i, c)),
        name=name, compiler_params=_params(2))(*args)


def _headnorm_bwd(name, dy, proj, col_off, ncb, gains, group, tm, scale, rope=None, fold=False, norm=True):
    T = dy.shape[0]
    with_rope = rope is not None
    n_groups = ncb // group
    dy_width = 4 * LANES if fold else LANES

    def body(*refs):
        refs = list(refs)
        dy_ref = refs.pop(0)
        x_ref = refs.pop(0) if norm else None
        g_ref = refs.pop(0) if norm else None
        cos_ref = refs.pop(0) if with_rope else None
        sin_ref = refs.pop(0) if with_rope else None
        dx_ref = refs.pop(0)
        dg_ref = refs.pop(0) if norm else None
        c = pl.program_id(0)
        i = pl.program_id(1)
        d = dy_ref[...]
        lane = lax.broadcasted_iota(jnp.int32, (d.shape[0], LANES), 1)
        lo = lane < HEAD_DIM
        if fold:
            t0 = d[:, 0:LANES] + d[:, LANES:2 * LANES]
            t1 = d[:, 2 * LANES:3 * LANES] + d[:, 3 * LANES:4 * LANES]
            d = jnp.where(lo, t0 + pltpu.roll(t0, HEAD_DIM, 1), t1 + pltpu.roll(t1, HEAD_DIM, 1))
        d = d * scale
        if with_rope:
            d = d * cos_ref[...] + _rot_half(d * sin_ref[...], lane)
        if not norm:
            dx_ref[...] = d.astype(BF16)
            return
        xv = x_ref[...]
        gv = g_ref[...]
        r = _head_rstd(xv, lo)
        xh = xv * r
        dxh = d * gv
        pr = dxh * xh
        m_a = jnp.sum(jnp.where(lo, pr, 0.0), axis=-1, keepdims=True)
        m_b = jnp.sum(jnp.where(lo, 0.0, pr), axis=-1, keepdims=True)
        mean = jnp.where(lo, m_a, m_b) * (1.0 / HEAD_DIM)
        dx_ref[...] = (r * (dxh - xh * mean)).astype(BF16)
        dgp = jnp.sum(d * xh, axis=0, keepdims=True)
        dgp = dgp + pltpu.roll(dgp, HEAD_DIM, 1)
        first = jnp.logical_and(c % group == 0, i == 0)

        @pl.when(first)
        def _():
            dg_ref[...] = dgp

        @pl.when(jnp.logical_not(first))
        def _():
            dg_ref[...] += dgp

    in_specs = [pl.BlockSpec((tm, dy_width), lambda c, i: (i, c))]
    args = [dy]
    if norm:
        in_specs += [pl.BlockSpec((tm, LANES), lambda c, i: (i, col_off + c)), pl.BlockSpec((None, 1, LANES), lambda c, i: (c, 0, 0))]
        args += [proj, gains]
    if with_rope:
        tab = pl.BlockSpec((tm, LANES), lambda c, i: (i, 0))
        in_specs += [tab, tab]
        args += list(rope)
    out_shape = [jax.ShapeDtypeStruct((T, ncb * LANES), BF16)]
    out_specs = [pl.BlockSpec((tm, LANES), lambda c, i: (i, c))]
    if norm:
        out_shape.append(jax.ShapeDtypeStruct((n_groups, 1, LANES), F32))
        out_specs.append(pl.BlockSpec((None, 1, LANES), lambda c, i: (c // group, 0, 0)))
    res = pl.pallas_call(
        body, out_shape=tuple(out_shape), grid=(ncb, T // tm), in_specs=in_specs, out_specs=tuple(out_specs),
        name=name, compiler_params=_params(2))(*args)
    return res if norm else (res[0], None)


def _dot_exact(x, tri):
    hi = x.astype(BF16)
    r1 = x - hi.astype(F32)
    mid = r1.astype(BF16)
    lo = (r1 - mid.astype(F32)).astype(BF16)
    return _dot(hi, tri, NN) + _dot(mid, tri, NN) + _dot(lo, tri, NN)


def _forget_fwd(name, zt, bias):
    H, T = zt.shape
    blk = min(256, T)

    def body(z_ref, b_ref, c_ref, s_ref):
        z = z_ref[...] + b_ref[...]
        s_ref[...] = jax.nn.sigmoid(-z)
        lf = jnp.minimum(z, 0.0) - jnp.log(1.0 + jnp.exp(-jnp.abs(z)))
        tri = (lax.broadcasted_iota(jnp.int32, (blk, blk), 0) <= lax.broadcasted_iota(jnp.int32, (blk, blk), 1)).astype(BF16)
        carry = jnp.zeros((H, 1), F32)
        for bi in range(T // blk):
            xb = lf[:, bi * blk:(bi + 1) * blk]
            c_ref[:, bi * blk:(bi + 1) * blk] = _dot_exact(xb, tri) + carry
            carry = carry + jnp.sum(xb, axis=-1, keepdims=True)

    shape = jax.ShapeDtypeStruct((H, T), F32)
    full = pl.BlockSpec((H, T), lambda i: (0, 0))
    return pl.pallas_call(
        body, out_shape=(shape, shape), grid=(1,), in_specs=[full, pl.BlockSpec((H, 1), lambda i: (0, 0))],
        out_specs=(full, full), name=name, compiler_params=_params(1))(zt, bias)


def _forget_bwd(name, dct, drt, sgt):
    H, T = dct.shape
    blk = min(256, T)

    def body(dc_ref, dr_ref, s_ref, dz_ref, db_ref):
        dc = dc_ref[...] + dr_ref[...]
        tri = (lax.broadcasted_iota(jnp.int32, (blk, blk), 0) >= lax.broadcasted_iota(jnp.int32, (blk, blk), 1)).astype(BF16)
        carry = jnp.zeros((H, 1), F32)
        db = jnp.zeros((H, 1), F32)
        for bi in reversed(range(T // blk)):
            xb = dc[:, bi * blk:(bi + 1) * blk]
            dz = (_dot_exact(xb, tri) + carry) * s_ref[:, bi * blk:(bi + 1) * blk]
            dz_ref[:, bi * blk:(bi + 1) * blk] = dz
            db = db + jnp.sum(dz, axis=-1, keepdims=True)
            carry = carry + jnp.sum(xb, axis=-1, keepdims=True)
        db_ref[...] = db

    full = pl.BlockSpec((H, T), lambda i: (0, 0))
    return pl.pallas_call(
        body, out_shape=(jax.ShapeDtypeStruct((H, T), F32), jax.ShapeDtypeStruct((H, 1), F32)), grid=(1,),
        in_specs=[full, full, full], out_specs=(full, pl.BlockSpec((H, 1), lambda i: (0, 0))),
        name=name, compiler_params=_params(1))(dct, drt, sgt)


STRIP = 256


def _fox_fwd(name, qk, v, crow, tq, tk, carry=None):
    T, Dh = v.shape
    HP = Dh // LANES
    nk = T // tk
    assert tk % tq == 0 and tq % STRIP == 0
    n_strips = tq // STRIP

    def body(q_ref, k_ref, v_ref, ra_ref, rb_ref, o_ref, la_ref, lb_ref, s_ref, p_ref, m_ref, l_ref, acc_ref):
        i = pl.program_id(1)
        q2 = q_ref[...]
        lo = _lane_lo((tq, LANES))
        qms = (_keep(lo, q2), _keep(jnp.logical_not(lo), q2))
        r_refs = (ra_ref, rb_ref)
        m_ref[...] = jnp.full(m_ref.shape, NEG, F32)
        l_ref[...] = jnp.zeros(l_ref.shape, F32)
        acc_ref[...] = jnp.zeros(acc_ref.shape, F32)
        rel = lax.broadcasted_iota(jnp.int32, (STRIP, tk), 0) - lax.broadcasted_iota(jnp.int32, (STRIP, tk), 1)

        def chunk(kc, masked):
            start = pl.multiple_of(kc * tk, tk)
            kb = k_ref[pl.ds(start, tk), :]
            vb = v_ref[pl.ds(start, tk), :]
            for h in range(2):
                s_ref[h] = _dot(qms[h], kb, NT)
            for h in range(2):
                cs = r_refs[h][kc]
                for st in range(n_strips):
                    rows = pl.ds(st * STRIP, STRIP)
                    s = s_ref[h, rows, :] - cs
                    if masked:
                        s = jnp.where(rel >= start - (i * tq + st * STRIP), s, NEG)
                    m_old = m_ref[h, rows, :]
                    mn = jnp.maximum(m_old, jnp.max(s, axis=-1, keepdims=True))
                    p = jnp.exp(s - mn)
                    alpha = jnp.exp(m_old - mn)
                    l_ref[h, rows, :] = alpha * l_ref[h, rows, :] + jnp.sum(p, axis=-1, keepdims=True)
                    m_ref[h, rows, :] = mn
                    p_ref[h, rows, :] = p.astype(BF16)
                    acc_ref[h, rows, :] = acc_ref[h, rows, :] * alpha
            for h in range(2):
                acc_ref[h] += _dot(p_ref[h], vb, NN)

        n_full = (i * tq) // tk

        def full_chunk(kc, _):
            chunk(kc, False)
            return 0

        lax.fori_loop(0, n_full, full_chunk, 0)
        chunk(n_full, True)
        o_ref[...] = jnp.where(lo, acc_ref[0] / l_ref[0], acc_ref[1] / l_ref[1])
        la_ref[...] = m_ref[0] + jnp.log(l_ref[0])
        lb_ref[...] = m_ref[1] + jnp.log(l_ref[1])

    row = lambda off: pl.BlockSpec((None, nk, 1, tk), lambda h, i: (2 * h + off, 0, 0, 0))
    lse = jax.ShapeDtypeStruct((HP, T, 1), F32)
    lspec = pl.BlockSpec((None, tq, 1), lambda h, i: (h, i, 0))
    scratch = [pltpu.VMEM((2, tq, tk), F32), pltpu.VMEM((2, tq, tk), BF16), pltpu.VMEM((2, tq, 1), F32),
               pltpu.VMEM((2, tq, 1), F32), pltpu.VMEM((2, tq, LANES), F32)]
    return _call(
        body, name=name, grid=(HP, T // tq), out_shape=(jax.ShapeDtypeStruct((T, Dh), F32), lse, lse),
        in_specs=[pl.BlockSpec((tq, LANES), lambda h, i: (i, h)), pl.BlockSpec((T, LANES), lambda h, i: (0, HP + h)),
                  pl.BlockSpec((T, LANES), lambda h, i: (0, h)), row(0), row(1)],
        out_specs=(pl.BlockSpec((tq, LANES), lambda h, i: (i, h)), lspec, lspec),
        args=[qk, qk, v, crow, crow], scratch_shapes=scratch, carry=carry)


def _fox_bwd(name, qk, v, o, do, crow, lse_a, lse_b, tq, tk, carry=None):
    T, Dh = v.shape
    HP = Dh // LANES
    nk = T // tk
    scale = HEAD_DIM ** -0.5
    assert tk % tq == 0 and tq % STRIP == 0
    n_strips = tq // STRIP

    def body(q_ref, k_ref, v_ref, o_ref, do_ref, ra_ref, rb_ref, la_ref, lb_ref,
             dq_ref, dk_ref, dv_ref, dca_ref, dcb_ref, dra_ref, drb_ref, s_ref, dp_ref, p_ref, ds_ref, dq_acc, dsum_ref):
        i = pl.program_id(1)

        @pl.when(i == 0)
        def _():
            dk_ref[...] = jnp.zeros_like(dk_ref)
            dv_ref[...] = jnp.zeros_like(dv_ref)
            dca_ref[...] = jnp.zeros_like(dca_ref)
            dcb_ref[...] = jnp.zeros_like(dcb_ref)

        q2 = q_ref[...]
        do2 = do_ref[...]
        lo = _lane_lo((tq, LANES))
        hi = jnp.logical_not(lo)
        qms = (_keep(lo, q2), _keep(hi, q2))
        doms = (_keep(lo, do2), _keep(hi, do2))
        prod = do2.astype(F32) * o_ref[...]
        dsum_ref[0] = jnp.sum(jnp.where(lo, prod, 0.0), axis=-1, keepdims=True)
        dsum_ref[1] = jnp.sum(jnp.where(lo, 0.0, prod), axis=-1, keepdims=True)
        r_refs, l_refs, dc_refs, dr_refs = (ra_ref, rb_ref), (la_ref, lb_ref), (dca_ref, dcb_ref), (dra_ref, drb_ref)
        dq_acc[...] = jnp.zeros(dq_acc.shape, F32)
        dra_ref[...] = jnp.zeros(dra_ref.shape, F32)
        drb_ref[...] = jnp.zeros(drb_ref.shape, F32)
        rel = lax.broadcasted_iota(jnp.int32, (STRIP, tk), 0) - lax.broadcasted_iota(jnp.int32, (STRIP, tk), 1)

        def chunk(kc, masked):
            start = pl.multiple_of(kc * tk, tk)
            kb = k_ref[pl.ds(start, tk), :]
            vb = v_ref[pl.ds(start, tk), :]
            for h in range(2):
                s_ref[h] = _dot(qms[h], kb, NT)
                dp_ref[h] = _dot(doms[h], vb, NT)
            for h in range(2):
                cs = r_refs[h][kc]
                col_sum = jnp.zeros((1, tk), F32)
                for st in range(n_strips):
                    rows = pl.ds(st * STRIP, STRIP)
                    s = s_ref[h, rows, :] - cs
                    if masked:
                        s = jnp.where(rel >= start - (i * tq + st * STRIP), s, NEG)
                    p = jnp.exp(s - l_refs[h][rows, :])
                    ds = p * (dp_ref[h, rows, :] - dsum_ref[h, rows, :])
                    p_ref[h, rows, :] = p.astype(BF16)
                    ds_ref[h, rows, :] = ds.astype(BF16)
                    col_sum = col_sum + jnp.sum(ds, axis=0, keepdims=True)
                    dr_refs[h][rows, :] += jnp.sum(ds, axis=-1, keepdims=True)
                dc_refs[h][kc] = dc_refs[h][kc] - col_sum
            dk = _dot(ds_ref[0], qms[0], TN) + _dot(ds_ref[1], qms[1], TN)
            dv = _dot(p_ref[0], doms[0], TN) + _dot(p_ref[1], doms[1], TN)
            dk_ref[pl.ds(start, tk), :] += dk
            dv_ref[pl.ds(start, tk), :] += dv
            for h in range(2):
                dq_acc[h] += _dot(ds_ref[h], kb, NN)

        n_full = (i * tq) // tk

        def full_chunk(kc, _):
            chunk(kc, False)
            return 0

        lax.fori_loop(0, n_full, full_chunk, 0)
        chunk(n_full, True)
        dq_ref[...] = jnp.where(lo, dq_acc[0], dq_acc[1]) * scale

    row = lambda off: pl.BlockSpec((None, nk, 1, tk), lambda h, i: (2 * h + off, 0, 0, 0))
    lspec = pl.BlockSpec((None, tq, 1), lambda h, i: (h, i, 0))
    qspec = pl.BlockSpec((tq, LANES), lambda h, i: (i, h))
    full = pl.BlockSpec((T, LANES), lambda h, i: (0, h))
    dcspec = pl.BlockSpec((None, nk, 1, tk), lambda h, i: (h, 0, 0, 0))
    grad = jax.ShapeDtypeStruct((T, Dh), F32)
    dc = jax.ShapeDtypeStruct((HP, nk, 1, tk), F32)
    dr = jax.ShapeDtypeStruct((HP, T, 1), F32)
    scratch = [pltpu.VMEM((2, tq, tk), F32), pltpu.VMEM((2, tq, tk), F32), pltpu.VMEM((2, tq, tk), BF16), pltpu.VMEM((2, tq, tk), BF16),
               pltpu.VMEM((2, tq, LANES), F32), pltpu.VMEM((2, tq, 1), F32)]
    return _call(
        body, name=name, grid=(HP, T // tq), out_shape=(grad, grad, grad, dc, dc, dr, dr),
        in_specs=[qspec, pl.BlockSpec((T, LANES), lambda h, i: (0, HP + h)), full, qspec, qspec, row(0), row(1), lspec, lspec],
        out_specs=(qspec, full, full, dcspec, dcspec, lspec, lspec),
        args=[qk, qk, v, o, do, crow, crow, lse_a, lse_b], scratch_shapes=scratch, carry=carry)


def _swa_block(n, q_ref, k_ref):
    qs = pl.multiple_of(n * WINDOW, WINDOW)
    ks = pl.multiple_of(jnp.maximum(n - 1, 0) * WINDOW, WINDOW)
    rel = (qs + lax.broadcasted_iota(jnp.int32, (WINDOW, 2 * WINDOW), 0)) - (ks + lax.broadcasted_iota(jnp.int32, (WINDOW, 2 * WINDOW), 1))
    valid = jnp.logical_and(rel >= 0, rel < WINDOW)
    return qs, ks, valid


def _swa_fwd(name, q, kd, vd, sinks, carry=None):
    T, Dh = q.shape
    HP = Dh // LANES

    def body(q_ref, k_ref, v_ref, sa_ref, sb_ref, o_ref, la_ref, lb_ref):
        lo = _lane_lo((WINDOW, LANES))

        def block(n, _):
            qs, ks, valid = _swa_block(n, q_ref, k_ref)
            q2 = q_ref[pl.ds(qs, WINDOW), :]
            kb = k_ref[pl.ds(ks, 2 * WINDOW), :]
            vb = v_ref[pl.ds(ks, 2 * WINDOW), :]
            res = []
            for sel, s_ref in ((lo, sa_ref), (jnp.logical_not(lo), sb_ref)):
                qm = _keep(sel, q2)
                sink = s_ref[...]
                s = jnp.where(valid, _dot(qm, kb, NT), NEG)
                m = jnp.maximum(jnp.max(s, axis=-1, keepdims=True), sink)
                p = jnp.exp(s - m)
                l = jnp.sum(p, axis=-1, keepdims=True) + jnp.exp(sink - m)
                res.append((_dot(p.astype(BF16), vb, NN) / l, m + jnp.log(l)))
            o_ref[pl.ds(qs, WINDOW), :] = jnp.where(lo, res[0][0], res[1][0])
            la_ref[pl.ds(qs, WINDOW), :] = res[0][1]
            lb_ref[pl.ds(qs, WINDOW), :] = res[1][1]
            return 0

        lax.fori_loop(0, T // WINDOW, block, 0, unroll=2)

    full = pl.BlockSpec((T, LANES), lambda h: (0, h))
    kv = pl.BlockSpec((T, LANES), lambda h: (0, h // 2))
    sink = lambda off: pl.BlockSpec((None, 1, 1), lambda h: (2 * h + off, 0, 0))
    lse = jax.ShapeDtypeStruct((HP, T, 1), F32)
    lspec = pl.BlockSpec((None, T, 1), lambda h: (h, 0, 0))
    return _call(
        body, name=name, grid=(HP,), out_shape=(jax.ShapeDtypeStruct((T, Dh), F32), lse, lse),
        in_specs=[full, kv, kv, sink(0), sink(1)], out_specs=(full, lspec, lspec),
        args=[q, kd, vd, sinks, sinks], carry=carry)


def _swa_bwd(name, q, kd, vd, sinks, o, do, lse_a, lse_b, carry=None):
    T, Dh = q.shape
    HP = Dh // LANES
    scale = HEAD_DIM ** -0.5

    def body(q_ref, k_ref, v_ref, sa_ref, sb_ref, o_ref, do_ref, la_ref, lb_ref, dq_ref, dk_ref, dv_ref, dsa_ref, dsb_ref):
        lo = _lane_lo((WINDOW, LANES))
        hi = jnp.logical_not(lo)
        dk_ref[...] = jnp.zeros_like(dk_ref)
        dv_ref[...] = jnp.zeros_like(dv_ref)

        def block(n, dsinks):
            qs, ks, valid = _swa_block(n, q_ref, k_ref)
            q2 = q_ref[pl.ds(qs, WINDOW), :]
            do2 = do_ref[pl.ds(qs, WINDOW), :]
            kb = k_ref[pl.ds(ks, 2 * WINDOW), :]
            vb = v_ref[pl.ds(ks, 2 * WINDOW), :]
            prod = do2.astype(F32) * o_ref[pl.ds(qs, WINDOW), :]
            dqs, new = [], []
            dk = jnp.zeros((2 * WINDOW, LANES), F32)
            dv = jnp.zeros((2 * WINDOW, LANES), F32)
            for sel, s_ref, l_ref, dsink in ((lo, sa_ref, la_ref, dsinks[0]), (hi, sb_ref, lb_ref, dsinks[1])):
                qm = _keep(sel, q2)
                dom = _keep(sel, do2)
                dsum = jnp.sum(jnp.where(sel, prod, 0.0), axis=-1, keepdims=True)
                lse = l_ref[pl.ds(qs, WINDOW), :]
                s = jnp.where(valid, _dot(qm, kb, NT), NEG)
                p = jnp.exp(s - lse)
                ds = p * (_dot(dom, vb, NT) - dsum)
                dsb = ds.astype(BF16)
                dqs.append(_dot(dsb, kb, NN))
                dk = dk + _dot(dsb, qm, TN)
                dv = dv + _dot(p.astype(BF16), dom, TN)
                new.append(dsink - jnp.sum(jnp.exp(s_ref[...] - lse) * dsum, axis=0, keepdims=True))
            dq_ref[pl.ds(qs, WINDOW), :] = jnp.where(lo, dqs[0], dqs[1]) * scale
            dk_ref[pl.ds(ks, 2 * WINDOW), :] += dk
            dv_ref[pl.ds(ks, 2 * WINDOW), :] += dv
            return tuple(new)

        dsa, dsb_ = lax.fori_loop(0, T // WINDOW, block, (jnp.zeros((1, 1), F32), jnp.zeros((1, 1), F32)), unroll=2)
        dsa_ref[...] = dsa
        dsb_ref[...] = dsb_

    full = pl.BlockSpec((T, LANES), lambda h: (0, h))
    kv = pl.BlockSpec((T, LANES), lambda h: (0, h // 2))
    sink = lambda off: pl.BlockSpec((None, 1, 1), lambda h: (2 * h + off, 0, 0))
    lspec = pl.BlockSpec((None, T, 1), lambda h: (h, 0, 0))
    dsink = pl.BlockSpec((None, 1, 1), lambda h: (h, 0, 0))
    grad = jax.ShapeDtypeStruct((T, Dh), F32)
    ds_shape = jax.ShapeDtypeStruct((HP, 1, 1), F32)
    return _call(
        body, name=name, grid=(HP,), out_shape=(grad, grad, grad, ds_shape, ds_shape),
        in_specs=[full, kv, kv, sink(0), sink(1), full, full, lspec, lspec],
        out_specs=(full, full, full, dsink, dsink),
        args=[q, kd, vd, sinks, sinks, o, do, lse_a, lse_b], carry=carry)


def _place():
    return lax.axis_index("x"), lax.axis_index("y"), lax.axis_index("c")


def _run_carry(name, carry):
    c_in, c_out = len(carry.inputs), len(carry.out_shapes)

    def body(*refs):
        ins, outs, scr = refs[:c_in], refs[c_in:c_in + c_out], refs[c_in + c_out:]
        carry.start(ins, outs, scr)
        carry.middle(ins, outs, scr)
        carry.finish(ins, outs, scr)

    return pl.pallas_call(
        body, out_shape=tuple(carry.out_shapes), in_specs=[_HBM] * c_in, out_specs=tuple([_HBM] * c_out),
        scratch_shapes=carry.scratch, name=name)(*carry.inputs)


def _gather_carry(shards):
    n = len(shards)

    def plan(ins, outs, scr):
        send, recv, local = scr
        x, y, c = _place()
        me, sibling = (x, y, c), (x, y, 1 - c)
        partner, other, diag = (x ^ c, y ^ (1 - c)), (x ^ (1 - c), y ^ c), (1 - x, 1 - y)

        def copy(w, k, block, to, src=None):
            slot = 4 * block[0] + 2 * block[1] + block[2]
            return pltpu.make_async_remote_copy(
                src_ref=outs[w].at[slot] if src is None else src, dst_ref=outs[w].at[slot],
                send_sem=send.at[w, k], recv_sem=recv.at[w, k], device_id=to, device_id_type=MESH)

        own = [pltpu.make_async_copy(ins[w], outs[w].at[4 * x + 2 * y + c], local.at[w]) for w in range(n)]
        return copy, own, me, sibling, partner, other, diag, c

    def start(ins, outs, scr):
        copy, own, me, sibling, partner, other, _, c = plan(ins, outs, scr)
        for cp in own:
            cp.start()
        for w in range(n):
            copy(w, 1, me, (*partner, c), src=ins[w]).start()
            copy(w, 2, me, (*other, c), src=ins[w]).start()
            copy(w, 0, me, sibling, src=ins[w]).start()

    def middle(ins, outs, scr):
        copy, _, me, sibling, partner, other, _, c = plan(ins, outs, scr)
        for w in range(n):
            copy(w, 1, (*partner, c), me).wait_recv()
            copy(w, 3, (*partner, c), (*other, c)).start()
            copy(w, 4, (*partner, c), sibling).start()

    def finish(ins, outs, scr):
        copy, own, me, sibling, partner, other, diag, c = plan(ins, outs, scr)
        for w in range(n):
            copy(w, 2, (*other, c), me).wait_recv()
            copy(w, 5, (*other, c), sibling).start()
        for w in range(n):
            copy(w, 3, (*diag, c), me).wait_recv()
            copy(w, 6, (*diag, c), sibling).start()
        for w in range(n):
            copy(w, 0, sibling, me).wait_recv()
            copy(w, 4, (*other, 1 - c), me).wait_recv()
            copy(w, 5, (*partner, 1 - c), me).wait_recv()
            copy(w, 6, (*diag, 1 - c), me).wait_recv()
        for w in range(n):
            sent = [copy(w, 0, me, sibling, src=ins[w]), copy(w, 1, me, (*partner, c), src=ins[w]), copy(w, 2, me, (*other, c), src=ins[w]),
                    copy(w, 3, (*partner, c), (*other, c)), copy(w, 4, (*partner, c), sibling), copy(w, 5, (*other, c), sibling),
                    copy(w, 6, (*diag, c), sibling)]
            for cp in sent:
                cp.wait_send()
        for cp in own:
            cp.wait()

    return _Carry(shards, [jax.ShapeDtypeStruct((N_DEV,) + s.shape, s.dtype) for s in shards],
                  [pltpu.SemaphoreType.DMA((n, 7)), pltpu.SemaphoreType.DMA((n, 7)), pltpu.SemaphoreType.DMA((n,))], start, finish, middle)


def _sibling_carry(grads):
    n = len(grads)

    def copies(ins, outs, scr):
        send, recv = scr
        x, y, c = _place()
        return [pltpu.make_async_remote_copy(
            src_ref=ins[w].at[2 * q + (1 - c)], dst_ref=outs[w].at[q], send_sem=send.at[w, q], recv_sem=recv.at[w, q],
            device_id=(x, y, 1 - c), device_id_type=MESH) for w in range(n) for q in range(4)]

    def start(ins, outs, scr):
        for cp in copies(ins, outs, scr):
            cp.start()

    def finish(ins, outs, scr):
        for cp in copies(ins, outs, scr):
            cp.wait()

    return _Carry(grads, [jax.ShapeDtypeStruct((4,) + g.shape[1:], g.dtype) for g in grads],
                  [pltpu.SemaphoreType.DMA((n, 4)), pltpu.SemaphoreType.DMA((n, 4))], start, finish)


def _to_partner_carry(sums):
    n = len(sums)

    def copies(ins, outs, scr):
        send, recv = scr
        x, y, c = _place()
        partner, diag = (x ^ c, y ^ (1 - c)), (1 - x, 1 - y)
        cps = []
        for w in range(n):
            for k, chip in enumerate((partner, diag)):
                cps.append(pltpu.make_async_remote_copy(
                    src_ref=ins[w].at[2 * chip[0] + chip[1]], dst_ref=outs[2 * w + k], send_sem=send.at[w, k], recv_sem=recv.at[w, k],
                    device_id=(*partner, c), device_id_type=MESH))
        return cps

    def start(ins, outs, scr):
        for cp in copies(ins, outs, scr):
            cp.start()

    def finish(ins, outs, scr):
        for cp in copies(ins, outs, scr):
            cp.wait()

    return _Carry(sums, [jax.ShapeDtypeStruct(s.shape[1:], s.dtype) for s in sums for _ in range(2)],
                  [pltpu.SemaphoreType.DMA((n, 2)), pltpu.SemaphoreType.DMA((n, 2))], start, finish)


def _to_other_carry(blocks):
    n = len(blocks)

    def copies(ins, outs, scr):
        send, recv = scr
        x, y, c = _place()
        return [pltpu.make_async_remote_copy(
            src_ref=ins[w], dst_ref=outs[w], send_sem=send.at[w], recv_sem=recv.at[w],
            device_id=(x ^ (1 - c), y ^ c, c), device_id_type=MESH) for w in range(n)]

    def start(ins, outs, scr):
        for cp in copies(ins, outs, scr):
            cp.start()

    def finish(ins, outs, scr):
        for cp in copies(ins, outs, scr):
            cp.wait()

    return _Carry(blocks, [jax.ShapeDtypeStruct(b.shape, b.dtype) for b in blocks],
                  [pltpu.SemaphoreType.DMA((n,)), pltpu.SemaphoreType.DMA((n,))], start, finish)


def _gather_small(packed):
    R, C = packed.shape

    def body(in_ref, out_ref, send, recv):
        x, y, c = _place()
        mine = 4 * x + 2 * y + c
        out_ref[mine] = in_ref[...]
        copies = []
        for k in range(1, N_DEV):
            peer = (x ^ (k >> 2), y ^ ((k >> 1) & 1), c ^ (k & 1))
            copies.append(pltpu.make_async_remote_copy(
                src_ref=in_ref, dst_ref=out_ref.at[mine], send_sem=send.at[k - 1], recv_sem=recv.at[k - 1],
                device_id=peer, device_id_type=MESH))
        for cp in copies:
            cp.start()
        for cp in copies:
            cp.wait()

    vmem = pl.BlockSpec(memory_space=pltpu.VMEM)
    return pl.pallas_call(
        body, out_shape=jax.ShapeDtypeStruct((N_DEV, R, C), F32), in_specs=[vmem], out_specs=vmem,
        scratch_shapes=[pltpu.SemaphoreType.DMA((N_DEV - 1,)), pltpu.SemaphoreType.DMA((N_DEV - 1,))],
        name="small_grads_all_gather")(packed)


def _adamw(w, g, m, v):
    m = ADAM_B1 * m + (1.0 - ADAM_B1) * g
    v = ADAM_B2 * v + (1.0 - ADAM_B2) * (g * g)
    m_hat = m / (1.0 - ADAM_B1 ** ADAM_STEP)
    v_hat = v / (1.0 - ADAM_B2 ** ADAM_STEP)
    delta = -ADAM_LR * (m_hat / (jnp.sqrt(v_hat) + ADAM_EPS) + ADAM_WD * w)
    return delta, m, v


def _pair_add(name, grads, received, c_idx):
    _, R, C = grads.shape
    tr = _row_tile(R)

    def body(c_ref, g_ref, r_ref, o_ref):
        o_ref[...] = (g_ref[...].astype(F32) + r_ref[...].astype(F32)).astype(BF16)

    blk = pl.BlockSpec((None, tr, C), lambda q, i, c: (q, i, 0))
    return pl.pallas_call(
        body, out_shape=jax.ShapeDtypeStruct((4, R, C), BF16),
        grid_spec=pltpu.PrefetchScalarGridSpec(
            num_scalar_prefetch=1, grid=(4, R // tr),
            in_specs=[pl.BlockSpec((None, tr, C), lambda q, i, c: (2 * q + c[0], i, 0)), blk], out_specs=blk),
        name=name, compiler_params=_params(2))(c_idx, grads, received)


def _relay_add(name, sums, relayed, other_idx):
    _, R, C = sums.shape
    tr = _row_tile(R)

    def body(q_ref, s_ref, r_ref, o_ref):
        o_ref[...] = (s_ref[...].astype(F32) + r_ref[...].astype(F32)).astype(BF16)

    blk = pl.BlockSpec((tr, C), lambda i, q: (i, 0))
    return pl.pallas_call(
        body, out_shape=jax.ShapeDtypeStruct((R, C), BF16),
        grid_spec=pltpu.PrefetchScalarGridSpec(
            num_scalar_prefetch=1, grid=(R // tr,),
            in_specs=[pl.BlockSpec((None, tr, C), lambda i, q: (q[0], i, 0)), blk], out_specs=blk),
        name=name, compiler_params=_params(1))(other_idx, sums, relayed)


def _adam_shard(name, sums, received, w, m, v, chip_idx):
    R, C = w.shape
    tr = _row_tile(R, 128)
    tc = C if tr < R or C % (2 * LANES) else 2 * LANES

    def body(q_ref, s_ref, ra_ref, rb_ref, w_ref, m_ref, v_ref, g_out, d_out, m_out, v_out):
        g = s_ref[...].astype(F32) + ra_ref[...].astype(F32) + rb_ref[...].astype(F32)
        delta, mn, vn = _adamw(w_ref[...], g, m_ref[...], v_ref[...])
        g_out[...] = g
        d_out[...] = delta
        m_out[...] = mn
        v_out[...] = vn

    blk = pl.BlockSpec((tr, tc), lambda i, j, q: (i, j))
    shape = jax.ShapeDtypeStruct((R, C), F32)
    return pl.pallas_call(
        body, out_shape=(shape,) * 4,
        grid_spec=pltpu.PrefetchScalarGridSpec(
            num_scalar_prefetch=1, grid=(R // tr, C // tc),
            in_specs=[pl.BlockSpec((None, tr, tc), lambda i, j, q: (q[0], i, j)), blk, blk, blk, blk, blk],
            out_specs=(blk,) * 4),
        name=name, compiler_params=_params(2))(chip_idx, sums, received[0], received[1], w, m, v)


def _adam_small(name, gathered, w, m, v):
    R, C = w.shape

    def body(ga_ref, w_ref, m_ref, v_ref, g_out, d_out, m_out, v_out):
        g = ga_ref[0]
        for d in range(1, N_DEV):
            g = g + ga_ref[d]
        delta, mn, vn = _adamw(w_ref[...], g, m_ref[...], v_ref[...])
        g_out[...] = g
        d_out[...] = delta
        m_out[...] = mn
        v_out[...] = vn

    full = pl.BlockSpec((R, C), lambda i: (0, 0))
    shape = jax.ShapeDtypeStruct((R, C), F32)
    return pl.pallas_call(
        body, out_shape=(shape,) * 4, grid=(1,),
        in_specs=[pl.BlockSpec((N_DEV, R, C), lambda i: (0, 0, 0)), full, full, full], out_specs=(full,) * 4,
        name=name, compiler_params=_params(1))(gathered, w, m, v)


def _pack_small(parts, D):
    g1, gmix, g2, gof, gos, bf, gqf, gkf, gqs, gks, sinks = [p.reshape(-1).astype(F32) for p in parts]
    row3 = jnp.concatenate([gof, gos])
    row4 = jnp.zeros((D,), F32)
    for slot, vec in enumerate((bf, gqf, gkf, gqs, gks, sinks)):
        row4 = lax.dynamic_update_slice(row4, vec, (slot * LANES,))
    zero = jnp.zeros((D,), F32)
    return jnp.stack([g1, gmix, g2, row3, row4, zero, zero, zero])


def _unpack_small(packed, D, H):
    Dh = D // 2
    row4 = packed[4]
    short = [row4[s * LANES:s * LANES + n] for s, n in enumerate((H, HEAD_DIM, HEAD_DIM, HEAD_DIM, HEAD_DIM, H))]
    vecs = [packed[0], packed[1], packed[2], packed[3, :Dh], packed[3, Dh:]] + short
    return [v[None, :] for v in vecs]


def kernel(x, positions, norm_ffn1_g, ffn1_w_gate, ffn1_w_up, ffn1_w_down, norm_mix_g, w_in, b_forget, fox_q_norm_g, fox_k_norm_g, swa_q_norm_g, swa_k_norm_g, swa_sinks, out_norm_fox_g, out_norm_swa_g, w_out, norm_ffn2_g, ffn2_w_gate, ffn2_w_up, ffn2_w_down, loss_target, m_norm_ffn1_g, m_ffn1_w_gate, m_ffn1_w_up, m_ffn1_w_down, m_norm_mix_g, m_w_in, m_b_forget, m_fox_q_norm_g, m_fox_k_norm_g, m_swa_q_norm_g, m_swa_k_norm_g, m_swa_sinks, m_out_norm_fox_g, m_out_norm_swa_g, m_w_out, m_norm_ffn2_g, m_ffn2_w_gate, m_ffn2_w_up, m_ffn2_w_down, v_norm_ffn1_g, v_ffn1_w_gate, v_ffn1_w_up, v_ffn1_w_down, v_norm_mix_g, v_w_in, v_b_forget, v_fox_q_norm_g, v_fox_k_norm_g, v_swa_q_norm_g, v_swa_k_norm_g, v_swa_sinks, v_out_norm_fox_g, v_out_norm_swa_g, v_w_out, v_norm_ffn2_g, v_ffn2_w_gate, v_ffn2_w_up, v_ffn2_w_down):
    xs = x[0]
    target = loss_target[0]
    T, D = xs.shape
    Dh = D // 2
    H = Dh // HEAD_DIM
    HP = H // 2
    KVW = (H // GQA_GROUP) * HEAD_DIM
    KVB = KVW // LANES
    MAIN = 4 * Dh + 2 * KVW
    F_OFF = 3 * Dh
    tm = min(ROW_TILE_CAP, T)
    tq = min(512, T)
    tk = min(512, T)
    nk = T // tk
    cx, cy, cc = _place()
    c_idx = jnp.reshape(cc, (1,)).astype(jnp.int32)
    chip_idx = jnp.reshape(2 * cx + cy, (1,)).astype(jnp.int32)
    other_idx = jnp.reshape(2 * (cx ^ (1 - cc)) + (cy ^ cc), (1,)).astype(jnp.int32)

    tr = jnp.transpose
    big_w = [tr(ffn1_w_gate[0]), tr(ffn1_w_up[0]), ffn1_w_down[0], tr(w_in[0]), w_out[0], tr(ffn2_w_gate[0]), tr(ffn2_w_up[0]),
             ffn2_w_down[0]]
    big_m = [tr(m_ffn1_w_gate[0]), tr(m_ffn1_w_up[0]), m_ffn1_w_down[0], tr(m_w_in[0]), m_w_out[0], tr(m_ffn2_w_gate[0]),
             tr(m_ffn2_w_up[0]), m_ffn2_w_down[0]]
    big_v = [tr(v_ffn1_w_gate[0]), tr(v_ffn1_w_up[0]), v_ffn1_w_down[0], tr(v_w_in[0]), v_w_out[0], tr(v_ffn2_w_gate[0]),
             tr(v_ffn2_w_up[0]), v_ffn2_w_down[0]]
    transposed = {"ffn1_w_gate", "ffn1_w_up", "w_in", "ffn2_w_gate", "ffn2_w_up"}
    names = ["ffn1_w_gate", "ffn1_w_up", "ffn1_w_down", "w_in", "w_out", "ffn2_w_gate", "ffn2_w_up", "ffn2_w_down"]
    sh = dict(zip(names, [w.astype(BF16) for w in big_w]))
    lane = jnp.arange(LANES)
    inv_freq = ROPE_THETA ** (-(2.0 * (lane % (HEAD_DIM // 2))).astype(F32) / HEAD_DIM)
    ang = positions[0].astype(F32)[:, None] * inv_freq[None, :]
    cos_t = jnp.cos(ang)
    sin_t = jnp.where((lane & (HEAD_DIM // 2)) == 0, -1.0, 1.0)[None, :] * jnp.sin(ang)
    rope = (cos_t, sin_t)

    def pair_gain(g, blocks):
        return jnp.tile(jnp.concatenate([g[0], g[0]])[None, None, :], (blocks, 1, 1))

    n1, (wg1,) = _rmsnorm_fwd("ffn1_norm", xs, norm_ffn1_g, tm, carry=_gather_carry([sh["ffn1_w_gate"]]))
    a1, (wu1,) = _ffn_gate("ffn1_gate", n1, wg1, tm, carry=_gather_carry([sh["ffn1_w_up"]]))
    (b1, hm1), (wd1,) = _ffn_up_only("ffn1_up", n1, wu1, a1, tm, carry=_gather_carry([sh["ffn1_w_down"]]))
    h1, (win_g,) = _ffn_down("ffn1_down", hm1, wd1, xs, tm, carry=_gather_carry([sh["w_in"]]))
    n_in = win_g.shape[1]
    win_t = win_g.reshape(N_DEV * n_in, D)
    win_main = jnp.concatenate([win_t[:F_OFF], win_t[F_OFF + H:]], axis=0)
    win_f = jnp.pad(win_t[F_OFF:F_OFF + H], ((0, LANES - H), (0, 0)))

    u = _rmsnorm_fwd("mix_norm", h1, norm_mix_g, tm)
    proj, (wout_g,) = _mm("mix_proj", u, win_main, tm, MAIN // 9, dims=NT, carry=_gather_carry([sh["w_out"]]))
    wout = wout_g.reshape(D, D)
    proj_f = _mm("mix_proj_forget", u, win_f, tm, LANES, dims=NT)
    scale = HEAD_DIM ** -0.5
    fox_gains = jnp.concatenate([pair_gain(fox_q_norm_g, HP), pair_gain(fox_k_norm_g, HP)])
    qk_f = _headnorm_fwd_scaled("fox_qk_norm", proj, 0, 2 * HP, fox_gains, T, scale, HP)
    v_f = proj[:, 2 * Dh:3 * Dh].astype(BF16)
    c_t, sg_t = _forget_fwd("forget_gates", proj_f[:, :H].T, b_forget.reshape(H, 1))
    crow = c_t.reshape(H, nk, 1, tk)
    (o_fox, lse_fa, lse_fb), (wg2, wu2) = _fox_fwd("fox_attention", qk_f, v_f, crow, tq, tk,
                                                   carry=_gather_carry([sh["ffn2_w_gate"], sh["ffn2_w_up"]]))

    swa_q_gains = pair_gain(swa_q_norm_g, HP)
    swa_k_gains = pair_gain(swa_k_norm_g, KVB)
    q_s = _headnorm_fwd("swa_q_norm", proj, 3 * HP, HP, swa_q_gains, T, scale, rope=rope)
    k_d = _headnorm_fwd("swa_k_norm", proj, 4 * HP, KVB, swa_k_gains, T, 1.0, rope=rope, dup=True)
    v_s = proj[:, 4 * Dh + KVW:].astype(BF16).reshape(T, H // GQA_GROUP, 1, HEAD_DIM)
    v_d = jnp.broadcast_to(v_s, (T, H // GQA_GROUP, 2, HEAD_DIM)).reshape(T, 2 * KVW)
    sinks3 = swa_sinks.reshape(H, 1, 1)
    o_swa, lse_sa, lse_sb = _swa_fwd("swa_attention", q_s, k_d, v_d, sinks3)

    on = _outnorm_fwd("out_norm", o_fox, o_swa, out_norm_fox_g, out_norm_swa_g, tm)
    h2 = _mm("mix_out", on, wout, tm, min(512, D), resid=h1)

    n2 = _rmsnorm_fwd("ffn2_norm", h2, norm_ffn2_g, tm)
    (a2, b2, hm2), (wd2,) = _ffn_up("ffn2_up", n2, wg2, wu2, tm, carry=_gather_carry([sh["ffn2_w_down"]]))
    y = _ffn_down("ffn2_down", hm2, wd2, h2, tm)
    dy, dyh, sq = _loss_grad("loss_grad", y, target, min(256, T))
    loss = lax.psum(0.5 * sq[0, 0] / D, ("x", "y", "c"))

    J, Fs, _ = wg2.shape
    aspec = pl.BlockSpec((None, tm, Fs), lambda i, j: (j, i, 0))
    wspec = pl.BlockSpec((None, Fs, D), lambda i, j: (j, 0, 0))
    got = {}

    def pair_sums(keys, grads, received):
        return [_pair_add("sum_" + nm, g, r, c_idx) for nm, g, r in zip(keys, grads, received)]

    def relay_sums(keys, sums, hop1):
        out = []
        for i, (nm, s) in enumerate(zip(keys, sums)):
            got[nm] = [hop1[2 * i]]
            out.append(_relay_add("relay_" + nm, s, hop1[2 * i + 1], other_idx))
        return out

    def arrived(keys, hop2):
        for nm, blk in zip(keys, hop2):
            got[nm].append(blk)

    dwd2 = _wgrad_down("ffn2_wgrad_down", hm2, dyh, min(1024, D))
    (da2, db2), (sib_d2,) = _ffn_bwd_mid("ffn2_bwd_mid", dyh, wd2, a2, b2, tm, carry=_sibling_carry([dwd2]))
    (sum_wd2,) = pair_sums(names[7:8], [dwd2], [sib_d2])
    (dwg2, dwu2), hop1 = _wgrad_up("ffn2_wgrad_up", n2, da2, db2, min(1024, D), carry=_to_partner_carry([sum_wd2]))
    (t_wd2,) = relay_sums(names[7:8], [sum_wd2], hop1)
    dn2, (via_wd2, *sib2) = _reduce_mm("ffn2_bwd_in", [(da2, aspec, wg2, wspec), (db2, aspec, wu2, wspec)], [], NN, T, D, tm, J,
                                       carry=_join(_to_other_carry([t_wd2]), _sibling_carry([dwg2, dwu2])))
    arrived(names[7:8], [via_wd2])
    dh2, dg_ffn2, dh2b = _rmsnorm_bwd("ffn2_norm_bwd", dn2, h2, norm_ffn2_g, dy, min(256, T), 1.0)
    sum_wg2, sum_wu2 = pair_sums(names[5:7], [dwg2, dwu2], sib2)

    dwout = _wgrad_2d("mix_out_wgrad", on, dh2b, min(512, D), min(1024, D))
    dwout_g = dwout.reshape(N_DEV, D // N_DEV, D)
    do_fox, dg_of = _outnorm_bwd("out_norm_bwd_fox", dh2b, wout, 0, o_fox, out_norm_fox_g, tm)
    do_swa, dg_os = _outnorm_bwd("out_norm_bwd_swa", dh2b, wout, 1, o_swa, out_norm_swa_g, tm)

    (dq_f, dk_f, dv_f, dc_a, dc_b, dr_a, dr_b), (*hop1, sib_wout) = _fox_bwd(
        "fox_attention_bwd", qk_f, v_f, o_fox, do_fox, crow, lse_fa, lse_fb, tq, tk,
        carry=_join(_to_partner_carry([sum_wg2, sum_wu2]), _sibling_carry([dwout_g])))
    t_wg2, t_wu2 = relay_sums(names[5:7], [sum_wg2, sum_wu2], hop1)
    (sum_wout,) = pair_sums(names[4:5], [dwout_g], [sib_wout])
    dqk_f = jnp.concatenate([dq_f, dk_f], axis=1)
    dqk_raw, dg_fox = _headnorm_bwd("fox_qk_norm_bwd", dqk_f, proj, 0, 2 * HP, fox_gains, HP, T, 1.0)
    dct = jnp.stack([dc_a.reshape(HP, T), dc_b.reshape(HP, T)], axis=1).reshape(H, T)
    drt = jnp.stack([dr_a.reshape(HP, T), dr_b.reshape(HP, T)], axis=1).reshape(H, T)
    dz_t, db_f = _forget_bwd("forget_gates_bwd", dct, drt, sg_t)

    (dq_s, dk_p, dv_p, dsink_a, dsink_b), hop2 = _swa_bwd(
        "swa_attention_bwd", q_s, k_d, v_d, sinks3, o_swa, do_swa, lse_sa, lse_sb, carry=_to_other_carry([t_wg2, t_wu2]))
    arrived(names[5:7], hop2)
    dqs_raw, dg_sq = _headnorm_bwd("swa_q_norm_bwd", dq_s, proj, 3 * HP, HP, swa_q_gains, HP, T, 1.0, rope=rope)
    dks_raw, dg_sk = _headnorm_bwd("swa_k_norm_bwd", dk_p, proj, 4 * HP, KVB, swa_k_gains, KVB, T, 1.0, rope=rope, fold=True)
    dvs_raw, _ = _headnorm_bwd("swa_v_fold", dv_p, None, 0, KVB, None, KVB, T, 1.0, fold=True, norm=False)

    dproj = jnp.concatenate([dqk_raw, dv_f.astype(BF16), dqs_raw, dks_raw, dvs_raw], axis=1)
    dproj_f = jnp.pad(dz_t.T, ((0, 0), (0, LANES - H))).astype(BF16)
    dwin_main, hop1 = _wgrad_2d("mix_proj_wgrad", dproj, u, MAIN // 9, min(1024, D), carry=_to_partner_carry([sum_wout]))
    (t_wout,) = relay_sums(names[4:5], [sum_wout], hop1)
    dwin_f = _wgrad_2d("mix_proj_forget_wgrad", dproj_f, u, LANES, min(1024, D))
    dwin_t = jnp.concatenate([dwin_main[:F_OFF], dwin_f[:H], dwin_main[F_OFF:]], axis=0)
    dwin_g = dwin_t.reshape(N_DEV, n_in, D)
    tkb = MAIN // 9
    du, (via_wout, sib_win) = _reduce_mm(
        "mix_bwd_in",
        [(dproj, pl.BlockSpec((tm, tkb), lambda i, r: (i, r)), win_main, pl.BlockSpec((tkb, D), lambda i, r: (r, 0)))],
        [(dproj_f, pl.BlockSpec((tm, LANES), lambda i, r: (i, 0)), win_f, pl.BlockSpec((LANES, D), lambda i, r: (0, 0)))],
        NN, T, D, tm, 9, carry=_join(_to_other_carry([t_wout]), _sibling_carry([dwin_g])))
    arrived(names[4:5], [via_wout])
    dh1, dg_mix, dh1h = _rmsnorm_bwd("mix_norm_bwd", du, h1, norm_mix_g, dh2, min(256, T), 0.5)
    (sum_win,) = pair_sums(names[3:4], [dwin_g], [sib_win])

    (da1, db1), hop1 = _ffn_bwd_mid("ffn1_bwd_mid", dh1h, wd1, a1, b1, tm, carry=_to_partner_carry([sum_win]))
    (t_win,) = relay_sums(names[3:4], [sum_win], hop1)
    dwg1, hop2 = _wgrad_down("ffn1_wgrad_gate", da1, n1, min(1024, D), carry=_to_other_carry([t_win]))
    arrived(names[3:4], hop2)
    dwu1, (sib_g,) = _wgrad_down("ffn1_wgrad_up", db1, n1, min(1024, D), carry=_sibling_carry([dwg1]))
    (sum_wg1,) = pair_sums(names[0:1], [dwg1], [sib_g])
    dwd1, (*hop1, sib_u) = _wgrad_down("ffn1_wgrad_down", hm1, dh1h, min(1024, D),
                                       carry=_join(_to_partner_carry([sum_wg1]), _sibling_carry([dwu1])))
    (t_wg1,) = relay_sums(names[0:1], [sum_wg1], hop1)
    (sum_wu1,) = pair_sums(names[1:2], [dwu1], [sib_u])
    dn1_gate, (via_wg1, *hop1, sib_d) = _reduce_mm(
        "ffn1_bwd_in_gate", [(da1, aspec, wg1, wspec)], [], NN, T, D, tm, J,
        carry=_join(_to_other_carry([t_wg1]), _to_partner_carry([sum_wu1]), _sibling_carry([dwd1])))
    arrived(names[0:1], [via_wg1])
    (t_wu1,) = relay_sums(names[1:2], [sum_wu1], hop1)
    (sum_wd1,) = pair_sums(names[2:3], [dwd1], [sib_d])
    dn1, (via_wu1, *hop1) = _reduce_mm(
        "ffn1_bwd_in_up", [(db1, aspec, wu1, wspec)], [], NN, T, D, tm, J, init=dn1_gate,
        carry=_join(_to_other_carry([t_wu1]), _to_partner_carry([sum_wd1])))
    arrived(names[1:2], [via_wu1])
    (t_wd1,) = relay_sums(names[2:3], [sum_wd1], hop1)
    arrived(names[2:3], _run_carry("grads_exchange", _to_other_carry([t_wd1])))
    dx, dg_ffn1 = _rmsnorm_bwd("ffn1_norm_bwd", dn1, xs, norm_ffn1_g, dh1, min(256, T), None)

    chip_sums = [sum_wg1, sum_wu1, sum_wd1, sum_win, sum_wout, sum_wg2, sum_wu2, sum_wd2]
    big_out = [_adam_shard("adam_" + nm, s, got[nm], w, m, v, chip_idx)
               for nm, s, w, m, v in zip(names, chip_sums, big_w, big_m, big_v)]

    dsinks = jnp.stack([dsink_a.reshape(HP), dsink_b.reshape(HP)], axis=1).reshape(H)
    small_g = [dg_ffn1, dg_mix, dg_ffn2, dg_of, dg_os, db_f, dg_fox[0, 0, :HEAD_DIM], dg_fox[1, 0, :HEAD_DIM],
               dg_sq[0, 0, :HEAD_DIM], dg_sk[0, 0, :HEAD_DIM], dsinks]
    small_w = [norm_ffn1_g, norm_mix_g, norm_ffn2_g, out_norm_fox_g, out_norm_swa_g, b_forget, fox_q_norm_g, fox_k_norm_g,
               swa_q_norm_g, swa_k_norm_g, swa_sinks]
    small_m = [m_norm_ffn1_g, m_norm_mix_g, m_norm_ffn2_g, m_out_norm_fox_g, m_out_norm_swa_g, m_b_forget, m_fox_q_norm_g,
               m_fox_k_norm_g, m_swa_q_norm_g, m_swa_k_norm_g, m_swa_sinks]
    small_v = [v_norm_ffn1_g, v_norm_mix_g, v_norm_ffn2_g, v_out_norm_fox_g, v_out_norm_swa_g, v_b_forget, v_fox_q_norm_g,
               v_fox_k_norm_g, v_swa_q_norm_g, v_swa_k_norm_g, v_swa_sinks]
    gathered = _gather_small(_pack_small(small_g, D))
    small_out = _adam_small("adam_small", gathered, _pack_small(small_w, D), _pack_small(small_m, D), _pack_small(small_v, D))
    small_out = [_unpack_small(p, D, H) for p in small_out]

    order = ["norm_ffn1_g", "ffn1_w_gate", "ffn1_w_up", "ffn1_w_down", "norm_mix_g", "w_in", "b_forget", "fox_q_norm_g", "fox_k_norm_g",
             "swa_q_norm_g", "swa_k_norm_g", "swa_sinks", "out_norm_fox_g", "out_norm_swa_g", "w_out", "norm_ffn2_g",
             "ffn2_w_gate", "ffn2_w_up", "ffn2_w_down"]
    small_names = ["norm_ffn1_g", "norm_mix_g", "norm_ffn2_g", "out_norm_fox_g", "out_norm_swa_g", "b_forget", "fox_q_norm_g",
                   "fox_k_norm_g", "swa_q_norm_g", "swa_k_norm_g", "swa_sinks"]
    result = [loss, dx[None]]
    for kind in range(4):
        for nm in order:
            if nm in names:
                leaf = big_out[names.index(nm)][kind]
                result.append((tr(leaf) if nm in transposed else leaf)[None])
            else:
                result.append(small_out[kind][small_names.index(nm)])
    return tuple(result)


def _headnorm_fwd_scaled(name, proj, col_off, ncb, gains, tm, scale, n_scaled):
    T = proj.shape[0]

    def body(x_ref, g_ref, o_ref):
        xv = x_ref[...]
        lo = _lane_lo(xv.shape)
        y = xv * _head_rstd(xv, lo) * g_ref[...]
        y = y * jnp.where(pl.program_id(0) < n_scaled, scale, 1.0)
        o_ref[...] = y.astype(BF16)

    return pl.pallas_call(
        body, out_shape=jax.ShapeDtypeStruct((T, ncb * LANES), BF16), grid=(ncb, T // tm),
        in_specs=[pl.BlockSpec((tm, LANES), lambda c, i: (i, col_off + c)), pl.BlockSpec((None, 1, LANES), lambda c, i: (c, 0, 0))],
        out_specs=pl.BlockSpec((tm, LANES), lambda c, i: (i, c)), name=name, compiler_params=_params(2))(proj, gains)
```

```python
import functools

import jax
import jax.numpy as jnp
from jax import lax
from jax.experimental import pallas as pl
from jax.experimental.pallas import tpu as pltpu

F32 = jnp.float32
BF16 = jnp.bfloat16

HEAD_DIM = 64
LANES = 128
WINDOW = 128
GQA_GROUP = 4
EPS = 1e-6
ROPE_THETA = 10000.0
ADAM_LR = 0.001
ADAM_B1 = 0.9
ADAM_B2 = 0.999
ADAM_EPS = 1e-08
ADAM_WD = 0.01
ADAM_STEP = 10
N_DEV = 8
NEG = -1e30
VMEM_LIMIT_V7X = 48 * 1024 * 1024
ROW_TILE_CAP = 512
MESH = pl.DeviceIdType.MESH

NN = (((1,), (0,)), ((), ()))
NT = (((1,), (1,)), ((), ()))
TN = (((0,), (0,)), ((), ()))


def _dot(a, b, dims):
    return lax.dot_general(a, b, dims, preferred_element_type=F32)


def _params(n_axes):
    return pltpu.CompilerParams(dimension_semantics=("arbitrary",) * n_axes, vmem_limit_bytes=VMEM_LIMIT_V7X)


def _row_tile(rows, cap=ROW_TILE_CAP):
    best = None
    for t in range(16, min(rows, cap) + 1, 16):
        if rows % t == 0:
            best = t
    return best or rows


def _lane_lo(shape):
    return lax.broadcasted_iota(jnp.int32, shape, len(shape) - 1) < HEAD_DIM


def _keep(sel, x):
    return jnp.where(sel, x.astype(F32), 0.0).astype(BF16)


_HBM = pl.BlockSpec(memory_space=pltpu.HBM)


class _Carry:
    def __init__(self, inputs, out_shapes, scratch, start, finish, middle=None):
        self.inputs, self.out_shapes, self.scratch = list(inputs), list(out_shapes), list(scratch)
        self.start, self.finish, self.middle = start, finish, middle or (lambda ins, outs, scr: None)


def _join(*carries):
    def hook(which):
        def run(ins, outs, scr):
            i = o = s = 0
            for c in carries:
                ni, no, ns = len(c.inputs), len(c.out_shapes), len(c.scratch)
                getattr(c, which)(ins[i:i + ni], outs[o:o + no], scr[s:s + ns])
                i, o, s = i + ni, o + no, s + ns
        return run

    return _Carry([a for c in carries for a in c.inputs], [a for c in carries for a in c.out_shapes],
                  [a for c in carries for a in c.scratch], hook("start"), hook("finish"), hook("middle"))


def _call(body, *, name, grid, in_specs, out_specs, out_shape, args, scratch_shapes=(), carry=None):
    params = _params(len(grid))
    if carry is None:
        return pl.pallas_call(body, out_shape=out_shape, grid=grid, in_specs=list(in_specs), out_specs=out_specs,
                              scratch_shapes=list(scratch_shapes), name=name, compiler_params=params)(*args)
    single = not isinstance(out_shape, (tuple, list))
    shapes = (out_shape,) if single else tuple(out_shape)
    specs = (out_specs,) if single else tuple(out_specs)
    n_in, n_out, n_scr = len(args), len(shapes), len(scratch_shapes)
    c_in, c_out = len(carry.inputs), len(carry.out_shapes)

    def wrapped(*refs):
        ins, c_ins = refs[:n_in], refs[n_in:n_in + c_in]
        o0 = n_in + c_in
        outs, c_outs = refs[o0:o0 + n_out], refs[o0 + n_out:o0 + n_out + c_out]
        s0 = o0 + n_out + c_out
        scr, c_scr = refs[s0:s0 + n_scr], refs[s0 + n_scr:]
        step, total = pl.program_id(0), grid[0]
        for ax in range(1, len(grid)):
            step, total = step * grid[ax] + pl.program_id(ax), total * grid[ax]

        @pl.when(step == 0)
        def _():
            carry.start(c_ins, c_outs, c_scr)

        @pl.when(step == total // 2)
        def _():
            carry.middle(c_ins, c_outs, c_scr)

        body(*ins, *outs, *scr)

        @pl.when(step == total - 1)
        def _():
            carry.finish(c_ins, c_outs, c_scr)

    res = pl.pallas_call(
        wrapped, out_shape=shapes + tuple(carry.out_shapes), grid=grid, in_specs=list(in_specs) + [_HBM] * c_in,
        out_specs=specs + (_HBM,) * c_out, scratch_shapes=list(scratch_shapes) + carry.scratch, name=name,
        compiler_params=params)(*args, *carry.inputs)
    main = res[:n_out]
    return (main[0] if single else tuple(main)), tuple(res[n_out:])


def _rms_bwd(dn, x, g):
    r = lax.rsqrt(jnp.mean(x * x, axis=-1, keepdims=True) + EPS)
    xh = x * r
    dxh = dn * g
    dx = r * (dxh - xh * jnp.mean(dxh * xh, axis=-1, keepdims=True))
    return dx, jnp.sum(dn * xh, axis=0, keepdims=True)


def _rmsnorm_fwd(name, x, g, tm, carry=None):
    T, D = x.shape

    def body(x_ref, g_ref, o_ref):
        xf = x_ref[...]
        r = lax.rsqrt(jnp.mean(xf * xf, axis=-1, keepdims=True) + EPS)
        o_ref[...] = (xf * r * g_ref[...]).astype(BF16)

    return _call(
        body, name=name, grid=(T // tm,), out_shape=jax.ShapeDtypeStruct((T, D), BF16),
        in_specs=[pl.BlockSpec((tm, D), lambda i: (i, 0)), pl.BlockSpec((1, D), lambda i: (0, 0))],
        out_specs=pl.BlockSpec((tm, D), lambda i: (i, 0)), args=[x, g], carry=carry)


def _outnorm_fwd(name, o_fox, o_swa, g_fox, g_swa, tm):
    T, Dh = o_fox.shape

    def body(a_ref, b_ref, ga_ref, gb_ref, o_ref):
        for ref, g_ref, lo in ((a_ref, ga_ref, 0), (b_ref, gb_ref, Dh)):
            xf = ref[...]
            r = lax.rsqrt(jnp.mean(xf * xf, axis=-1, keepdims=True) + EPS)
            o_ref[:, lo:lo + Dh] = (xf * r * g_ref[...]).astype(BF16)

    row = pl.BlockSpec((tm, Dh), lambda i: (i, 0))
    gain = pl.BlockSpec((1, Dh), lambda i: (0, 0))
    return pl.pallas_call(
        body, out_shape=jax.ShapeDtypeStruct((T, 2 * Dh), BF16), grid=(T // tm,),
        in_specs=[row, row, gain, gain], out_specs=pl.BlockSpec((tm, 2 * Dh), lambda i: (i, 0)),
        name=name, compiler_params=_params(1))(o_fox, o_swa, g_fox, g_swa)


def _outnorm_bwd(name, dhb, wout, half, o, g, tm):
    T, D = dhb.shape
    Dh = o.shape[1]

    def body(a_ref, w_ref, o_ref, g_ref, do_ref, dg_ref):
        don = _dot(a_ref[...], w_ref[...], NT)
        dx, dg = _rms_bwd(don, o_ref[...], g_ref[...])
        do_ref[...] = dx.astype(BF16)

        @pl.when(pl.program_id(0) == 0)
        def _():
            dg_ref[...] = dg

        @pl.when(pl.program_id(0) > 0)
        def _():
            dg_ref[...] += dg

    return pl.pallas_call(
        body, out_shape=(jax.ShapeDtypeStruct((T, Dh), BF16), jax.ShapeDtypeStruct((1, Dh), F32)), grid=(T // tm,),
        in_specs=[pl.BlockSpec((tm, D), lambda i: (i, 0)), pl.BlockSpec((Dh, D), lambda i: (half, 0)),
                  pl.BlockSpec((tm, Dh), lambda i: (i, 0)), pl.BlockSpec((1, Dh), lambda i: (0, 0))],
        out_specs=(pl.BlockSpec((tm, Dh), lambda i: (i, 0)), pl.BlockSpec((1, Dh), lambda i: (0, 0))),
        name=name, compiler_params=_params(1))(dhb, wout, o, g)


def _mm(name, a, b, tm, tn, dims=NN, resid=None, carry=None):
    M, K = a.shape
    transposed = dims == NT
    N = b.shape[0] if transposed else b.shape[1]

    def body(*refs):
        if resid is None:
            a_ref, b_ref, o_ref = refs
            o_ref[...] = _dot(a_ref[...], b_ref[...], dims)
        else:
            a_ref, b_ref, r_ref, o_ref = refs
            o_ref[...] = r_ref[...] + _dot(a_ref[...], b_ref[...], dims)

    ospec = pl.BlockSpec((tm, tn), lambda n, i: (i, n))
    bspec = pl.BlockSpec((tn, K), lambda n, i: (n, 0)) if transposed else pl.BlockSpec((K, tn), lambda n, i: (0, n))
    in_specs = [pl.BlockSpec((tm, K), lambda n, i: (i, 0)), bspec]
    args = [a, b]
    if resid is not None:
        in_specs.append(ospec)
        args.append(resid)
    return _call(body, name=name, grid=(N // tn, M // tm), in_specs=in_specs, out_specs=ospec,
                 out_shape=jax.ShapeDtypeStruct((M, N), F32), args=args, carry=carry)


def _wgrad_2d(name, a, b, tmm, tn, carry=None):
    T, M = a.shape
    N = b.shape[1]

    def body(a_ref, b_ref, o_ref):
        o_ref[...] = _dot(a_ref[...], b_ref[...], TN).astype(BF16)

    return _call(
        body, name=name, grid=(M // tmm, N // tn), out_shape=jax.ShapeDtypeStruct((M, N), BF16),
        in_specs=[pl.BlockSpec((T, tmm), lambda m, n: (0, m)), pl.BlockSpec((T, tn), lambda m, n: (0, n))],
        out_specs=pl.BlockSpec((tmm, tn), lambda m, n: (m, n)), args=[a, b], carry=carry)


def _wgrad_down(name, hm, df, tn, carry=None):
    J, T, Fs = hm.shape
    D = df.shape[1]

    def body(a_ref, b_ref, o_ref):
        o_ref[...] = _dot(a_ref[...], b_ref[...], TN).astype(BF16)

    return _call(
        body, name=name, grid=(J, D // tn), out_shape=jax.ShapeDtypeStruct((J, Fs, D), BF16),
        in_specs=[pl.BlockSpec((None, T, Fs), lambda j, n: (j, 0, 0)), pl.BlockSpec((T, tn), lambda j, n: (0, n))],
        out_specs=pl.BlockSpec((None, Fs, tn), lambda j, n: (j, 0, n)), args=[hm, df], carry=carry)


def _wgrad_up(name, n, da, db, tn, carry=None):
    T, D = n.shape
    J, _, Fs = da.shape

    def body(n_ref, da_ref, db_ref, og_ref, ou_ref):
        nv = n_ref[...]
        og_ref[...] = _dot(da_ref[...], nv, TN).astype(BF16)
        ou_ref[...] = _dot(db_ref[...], nv, TN).astype(BF16)

    act = pl.BlockSpec((None, T, Fs), lambda j, m: (j, 0, 0))
    out = pl.BlockSpec((None, Fs, tn), lambda j, m: (j, 0, m))
    shape = jax.ShapeDtypeStruct((J, Fs, D), BF16)
    return _call(
        body, name=name, grid=(J, D // tn), out_shape=(shape, shape),
        in_specs=[pl.BlockSpec((T, tn), lambda j, m: (0, m)), act, act], out_specs=(out, out),
        args=[n, da, db], carry=carry)


def _reduce_mm(name, pairs, once, dims, T, D, tm, steps, init=None, carry=None):
    n_pairs = len(pairs)
    n_once = len(once)
    n_mm = 2 * (n_pairs + n_once)

    def body(*refs):
        pr = refs[:2 * n_pairs]
        on = refs[2 * n_pairs:n_mm]
        o_ref, acc = refs[-2:]
        r = pl.program_id(1)

        @pl.when(r == 0)
        def _():
            acc[...] = jnp.zeros(acc.shape, F32) if init is None else refs[n_mm][...]

        for p in range(n_pairs):
            acc[...] += _dot(pr[2 * p][...], pr[2 * p + 1][...], dims)

        @pl.when(r == steps - 1)
        def _():
            dn = acc[...]
            for p in range(n_once):
                dn = dn + _dot(on[2 * p][...], on[2 * p + 1][...], dims)
            o_ref[...] = dn

    in_specs, args = [], []
    for a, a_spec, w, w_spec in list(pairs) + list(once):
        in_specs += [a_spec, w_spec]
        args += [a, w]
    row = pl.BlockSpec((tm, D), lambda i, r: (i, 0))
    if init is not None:
        in_specs.append(row)
        args.append(init)
    return _call(body, name=name, grid=(T // tm, steps), in_specs=in_specs, out_specs=row, out_shape=jax.ShapeDtypeStruct((T, D), F32),
                 args=args, scratch_shapes=[pltpu.VMEM((tm, D), F32)], carry=carry)


def _rmsnorm_bwd(name, dn, x, g, dh, tm, bf16_scale, carry=None):
    T, D = x.shape
    emit_bf16 = bf16_scale is not None

    def body(dn_ref, x_ref, g_ref, dh_ref, *outs):
        dxn, dg = _rms_bwd(dn_ref[...], x_ref[...], g_ref[...])
        dx = dh_ref[...] + dxn
        outs[0][...] = dx
        if emit_bf16:
            outs[2][...] = (bf16_scale * dx).astype(BF16)

        @pl.when(pl.program_id(0) == 0)
        def _():
            outs[1][...] = dg

        @pl.when(pl.program_id(0) > 0)
        def _():
            outs[1][...] += dg

    row = pl.BlockSpec((tm, D), lambda i: (i, 0))
    gain = pl.BlockSpec((1, D), lambda i: (0, 0))
    out_shape = [jax.ShapeDtypeStruct((T, D), F32), jax.ShapeDtypeStruct((1, D), F32)]
    out_specs = [row, gain]
    if emit_bf16:
        out_shape.append(jax.ShapeDtypeStruct((T, D), BF16))
        out_specs.append(row)
    return _call(body, name=name, grid=(T // tm,), in_specs=[row, row, gain, row], out_specs=tuple(out_specs),
                 out_shape=tuple(out_shape), args=[dn, x, g, dh], carry=carry)


def _loss_grad(name, y, target, tm):
    T, D = y.shape

    def body(y_ref, t_ref, dy_ref, dyh_ref, sq_ref):
        diff = y_ref[...] - t_ref[...]
        sq = jnp.sum(jnp.sum(diff * diff, axis=1, keepdims=True), axis=0, keepdims=True)
        dy = diff * (1.0 / D)
        dy_ref[...] = dy
        dyh_ref[...] = (0.5 * dy).astype(BF16)

        @pl.when(pl.program_id(0) == 0)
        def _():
            sq_ref[...] = sq

        @pl.when(pl.program_id(0) > 0)
        def _():
            sq_ref[...] += sq

    row = pl.BlockSpec((tm, D), lambda i: (i, 0))
    return pl.pallas_call(
        body, out_shape=(jax.ShapeDtypeStruct((T, D), F32), jax.ShapeDtypeStruct((T, D), BF16), jax.ShapeDtypeStruct((1, 1), F32)),
        grid=(T // tm,), in_specs=[row, row], out_specs=(row, row, pl.BlockSpec((1, 1), lambda i: (0, 0))),
        name=name, compiler_params=_params(1))(y, target)


def _ffn_up(name, n, wg, wu, tm, carry=None):
    T, D = n.shape
    J, Fs, _ = wg.shape

    def body(n_ref, wg_ref, wu_ref, a_ref, b_ref, h_ref):
        xv = n_ref[...]
        a = _dot(xv, wg_ref[...], NT)
        b = _dot(xv, wu_ref[...], NT)
        a_ref[...] = a.astype(BF16)
        b_ref[...] = b.astype(BF16)
        h_ref[...] = (a * jax.nn.sigmoid(a) * b).astype(BF16)

    act = jax.ShapeDtypeStruct((J, T, Fs), BF16)
    wspec = pl.BlockSpec((None, Fs, D), lambda j, i: (j, 0, 0))
    aspec = pl.BlockSpec((None, tm, Fs), lambda j, i: (j, i, 0))
    return _call(
        body, name=name, grid=(J, T // tm), out_shape=(act, act, act),
        in_specs=[pl.BlockSpec((tm, D), lambda j, i: (i, 0)), wspec, wspec], out_specs=(aspec, aspec, aspec),
        args=[n, wg, wu], carry=carry)


def _ffn_gate(name, n, wg, tm, carry=None):
    T, D = n.shape
    J, Fs, _ = wg.shape

    def body(n_ref, wg_ref, a_ref):
        a_ref[...] = _dot(n_ref[...], wg_ref[...], NT).astype(BF16)

    aspec = pl.BlockSpec((None, tm, Fs), lambda j, i: (j, i, 0))
    return _call(
        body, name=name, grid=(J, T // tm), out_shape=jax.ShapeDtypeStruct((J, T, Fs), BF16),
        in_specs=[pl.BlockSpec((tm, D), lambda j, i: (i, 0)), pl.BlockSpec((None, Fs, D), lambda j, i: (j, 0, 0))],
        out_specs=aspec, args=[n, wg], carry=carry)


def _ffn_up_only(name, n, wu, a, tm, carry=None):
    T, D = n.shape
    J, Fs, _ = wu.shape

    def body(n_ref, wu_ref, a_ref, b_ref, h_ref):
        b = _dot(n_ref[...], wu_ref[...], NT)
        a = a_ref[...].astype(F32)
        b_ref[...] = b.astype(BF16)
        h_ref[...] = (a * jax.nn.sigmoid(a) * b).astype(BF16)

    act = jax.ShapeDtypeStruct((J, T, Fs), BF16)
    aspec = pl.BlockSpec((None, tm, Fs), lambda j, i: (j, i, 0))
    return _call(
        body, name=name, grid=(J, T // tm), out_shape=(act, act),
        in_specs=[pl.BlockSpec((tm, D), lambda j, i: (i, 0)), pl.BlockSpec((None, Fs, D), lambda j, i: (j, 0, 0)), aspec],
        out_specs=(aspec, aspec), args=[n, wu, a], carry=carry)


def _ffn_down(name, hm, wd, resid, tm, tn, carry=None):
    J, T, Fs = hm.shape
    D = wd.shape[2]

    def body(h_ref, w_ref, r_ref, o_ref, acc):
        j = pl.program_id(2)

        @pl.when(j == 0)
        def _():
            acc[...] = jnp.zeros(acc.shape, F32)

        acc[...] += _dot(h_ref[...], w_ref[...], NN)

        @pl.when(j == J - 1)
        def _():
            o_ref[...] = r_ref[...] + 0.5 * acc[...]

    tile = pl.BlockSpec((tm, tn), lambda i, n, j: (i, n))
    return _call(
        body, name=name, grid=(T // tm, D // tn, J), out_shape=jax.ShapeDtypeStruct((T, D), F32),
        in_specs=[pl.BlockSpec((None, tm, Fs), lambda i, n, j: (j, i, 0)), pl.BlockSpec((None, Fs, tn), lambda i, n, j: (j, 0, n)), tile],
        out_specs=tile, scratch_shapes=[pltpu.VMEM((tm, tn), F32)], args=[hm, wd, resid], carry=carry)


def _ffn_bwd_mid(name, dfh, wd, a, b, tm, carry=None):
    T, D = dfh.shape
    J, Fs, _ = wd.shape

    def body(df_ref, w_ref, a_ref, b_ref, da_ref, db_ref):
        dhm = _dot(df_ref[...], w_ref[...], NT)
        av = a_ref[...].astype(F32)
        bv = b_ref[...].astype(F32)
        sg = jax.nn.sigmoid(av)
        da_ref[...] = (dhm * bv * (sg * (1.0 + av * (1.0 - sg)))).astype(BF16)
        db_ref[...] = (dhm * (av * sg)).astype(BF16)

    act = jax.ShapeDtypeStruct((J, T, Fs), BF16)
    aspec = pl.BlockSpec((None, tm, Fs), lambda j, i: (j, i, 0))
    return _call(
        body, name=name, grid=(J, T // tm), out_shape=(act, act),
        in_specs=[pl.BlockSpec((tm, D), lambda j, i: (i, 0)), pl.BlockSpec((None, Fs, D), lambda j, i: (j, 0, 0)), aspec, aspec],
        out_specs=(aspec, aspec), args=[dfh, wd, a, b], carry=carry)


def _rot_half(y, lane):
    first = (lane & (HEAD_DIM // 2)) == 0
    return jnp.where(first, pltpu.roll(y, LANES - HEAD_DIM // 2, 1), pltpu.roll(y, HEAD_DIM // 2, 1))


def _head_rstd(x, lo):
    sq = x * x
    ss_a = jnp.sum(jnp.where(lo, sq, 0.0), axis=-1, keepdims=True)
    ss_b = jnp.sum(jnp.where(lo, 0.0, sq), axis=-1, keepdims=True)
    return lax.rsqrt(jnp.where(lo, ss_a, ss_b) * (1.0 / HEAD_DIM) + EPS)


def _headnorm_fwd(name, proj, col_off, ncb, gains, tm, scale, rope=None, dup=False):
    T = proj.shape[0]
    with_rope = rope is not None
    width = 2 * LANES if dup else LANES

    def body(*refs):
        if with_rope:
            x_ref, g_ref, cos_ref, sin_ref, o_ref = refs
        else:
            x_ref, g_ref, o_ref = refs
        xv = x_ref[...]
        lane = lax.broadcasted_iota(jnp.int32, xv.shape, 1)
        lo = lane < HEAD_DIM
        y = xv * _head_rstd(xv, lo) * g_ref[...]
        if with_rope:
            y = y * cos_ref[...] + _rot_half(y, lane) * sin_ref[...]
        y = y * scale
        if dup:
            sw = pltpu.roll(y, HEAD_DIM, 1)
            o_ref[:, :LANES] = jnp.where(lo, y, sw).astype(BF16)
            o_ref[:, LANES:] = jnp.where(lo, sw, y).astype(BF16)
        else:
            o_ref[...] = y.astype(BF16)

    in_specs = [pl.BlockSpec((tm, LANES), lambda c, i: (i, col_off + c)), pl.BlockSpec((None, 1, LANES), lambda c, i: (c, 0, 0))]
    args = [proj, gains]
    if with_rope:
        tab = pl.BlockSpec((tm, LANES), lambda c, i: (i, 0))
        in_specs += [tab, tab]
        args += list(rope)
    return pl.pallas_call(
        body, out_shape=jax.ShapeDtypeStruct((T, ncb * width), BF16), grid=(ncb, T // tm),
        in_specs=in_specs, out_specs=pl.BlockSpec((tm, width), lambda c, i: (i, c)),
        name=name, compiler_params=_params(2))(*args)


def _headnorm_bwd(name, dy, proj, col_off, ncb, gains, group, tm, scale, rope=None, fold=False, norm=True):
    T = dy.shape[0]
    with_rope = rope is not None
    n_groups = ncb // group
    dy_width = 4 * LANES if fold else LANES

    def body(*refs):
        refs = list(refs)
        dy_ref = refs.pop(0)
        x_ref = refs.pop(0) if norm else None
        g_ref = refs.pop(0) if norm else None
        cos_ref = refs.pop(0) if with_rope else None
        sin_ref = refs.pop(0) if with_rope else None
        dx_ref = refs.pop(0)
        dg_ref = refs.pop(0) if norm else None
        c = pl.program_id(0)
        i = pl.program_id(1)
        d = dy_ref[...]
        lane = lax.broadcasted_iota(jnp.int32, (d.shape[0], LANES), 1)
        lo = lane < HEAD_DIM
        if fold:
            t0 = d[:, 0:LANES] + d[:, LANES:2 * LANES]
            t1 = d[:, 2 * LANES:3 * LANES] + d[:, 3 * LANES:4 * LANES]
            d = jnp.where(lo, t0 + pltpu.roll(t0, HEAD_DIM, 1), t1 + pltpu.roll(t1, HEAD_DIM, 1))
        d = d * scale
        if with_rope:
            d = d * cos_ref[...] + _rot_half(d * sin_ref[...], lane)
        if not norm:
            dx_ref[...] = d.astype(BF16)
            return
        xv = x_ref[...]
        gv = g_ref[...]
        r = _head_rstd(xv, lo)
        xh = xv * r
        dxh = d * gv
        pr = dxh * xh
        m_a = jnp.sum(jnp.where(lo, pr, 0.0), axis=-1, keepdims=True)
        m_b = jnp.sum(jnp.where(lo, 0.0, pr), axis=-1, keepdims=True)
        mean = jnp.where(lo, m_a, m_b) * (1.0 / HEAD_DIM)
        dx_ref[...] = (r * (dxh - xh * mean)).astype(BF16)
        dgp = jnp.sum(d * xh, axis=0, keepdims=True)
        dgp = dgp + pltpu.roll(dgp, HEAD_DIM, 1)
        first = jnp.logical_and(c % group == 0, i == 0)

        @pl.when(first)
        def _():
            dg_ref[...] = dgp

        @pl.when(jnp.logical_not(first))
        def _():
            dg_ref[...] += dgp

    in_specs = [pl.BlockSpec((tm, dy_width), lambda c, i: (i, c))]
    args = [dy]
    if norm:
        in_specs += [pl.BlockSpec((tm, LANES), lambda c, i: (i, col_off + c)), pl.BlockSpec((None, 1, LANES), lambda c, i: (c, 0, 0))]
        args += [proj, gains]
    if with_rope:
        tab = pl.BlockSpec((tm, LANES), lambda c, i: (i, 0))
        in_specs += [tab, tab]
        args += list(rope)
    out_shape = [jax.ShapeDtypeStruct((T, ncb * LANES), BF16)]
    out_specs = [pl.BlockSpec((tm, LANES), lambda c, i: (i, c))]
    if norm:
        out_shape.append(jax.ShapeDtypeStruct((n_groups, 1, LANES), F32))
        out_specs.append(pl.BlockSpec((None, 1, LANES), lambda c, i: (c // group, 0, 0)))
    res = pl.pallas_call(
        body, out_shape=tuple(out_shape), grid=(ncb, T // tm), in_specs=in_specs, out_specs=tuple(out_specs),
        name=name, compiler_params=_params(2))(*args)
    return res if norm else (res[0], None)


def _dot_exact(x, tri):
    hi = x.astype(BF16)
    r1 = x - hi.astype(F32)
    mid = r1.astype(BF16)
    lo = (r1 - mid.astype(F32)).astype(BF16)
    return _dot(hi, tri, NN) + _dot(mid, tri, NN) + _dot(lo, tri, NN)


def _forget_fwd(name, zt, bias):
    H, T = zt.shape
    blk = min(256, T)

    def body(z_ref, b_ref, c_ref, s_ref):
        z = z_ref[...] + b_ref[...]
        s_ref[...] = jax.nn.sigmoid(-z)
        lf = jnp.minimum(z, 0.0) - jnp.log(1.0 + jnp.exp(-jnp.abs(z)))
        tri = (lax.broadcasted_iota(jnp.int32, (blk, blk), 0) <= lax.broadcasted_iota(jnp.int32, (blk, blk), 1)).astype(BF16)
        carry = jnp.zeros((H, 1), F32)
        for bi in range(T // blk):
            xb = lf[:, bi * blk:(bi + 1) * blk]
            c_ref[:, bi * blk:(bi + 1) * blk] = _dot_exact(xb, tri) + carry
            carry = carry + jnp.sum(xb, axis=-1, keepdims=True)

    shape = jax.ShapeDtypeStruct((H, T), F32)
    full = pl.BlockSpec((H, T), lambda i: (0, 0))
    return pl.pallas_call(
        body, out_shape=(shape, shape), grid=(1,), in_specs=[full, pl.BlockSpec((H, 1), lambda i: (0, 0))],
        out_specs=(full, full), name=name, compiler_params=_params(1))(zt, bias)


def _forget_bwd(name, dct, drt, sgt):
    H, T = dct.shape
    blk = min(256, T)

    def body(dc_ref, dr_ref, s_ref, dz_ref, db_ref):
        dc = dc_ref[...] + dr_ref[...]
        tri = (lax.broadcasted_iota(jnp.int32, (blk, blk), 0) >= lax.broadcasted_iota(jnp.int32, (blk, blk), 1)).astype(BF16)
        carry = jnp.zeros((H, 1), F32)
        db = jnp.zeros((H, 1), F32)
        for bi in reversed(range(T // blk)):
            xb = dc[:, bi * blk:(bi + 1) * blk]
            dz = (_dot_exact(xb, tri) + carry) * s_ref[:, bi * blk:(bi + 1) * blk]
            dz_ref[:, bi * blk:(bi + 1) * blk] = dz
            db = db + jnp.sum(dz, axis=-1, keepdims=True)
            carry = carry + jnp.sum(xb, axis=-1, keepdims=True)
        db_ref[...] = db

    full = pl.BlockSpec((H, T), lambda i: (0, 0))
    return pl.pallas_call(
        body, out_shape=(jax.ShapeDtypeStruct((H, T), F32), jax.ShapeDtypeStruct((H, 1), F32)), grid=(1,),
        in_specs=[full, full, full], out_specs=(full, pl.BlockSpec((H, 1), lambda i: (0, 0))),
        name=name, compiler_params=_params(1))(dct, drt, sgt)


STRIP = 256


def _fox_fwd(name, qk, v, crow, tq, tk, carry=None):
    T, Dh = v.shape
    HP = Dh // LANES
    nk = T // tk
    assert tk % tq == 0 and tq % STRIP == 0
    n_strips = tq // STRIP

    def body(q_ref, k_ref, v_ref, ra_ref, rb_ref, o_ref, la_ref, lb_ref, s_ref, p_ref, m_ref, l_ref, acc_ref):
        i = pl.program_id(1)
        q2 = q_ref[...]
        lo = _lane_lo((tq, LANES))
        qms = (_keep(lo, q2), _keep(jnp.logical_not(lo), q2))
        r_refs = (ra_ref, rb_ref)
        m_ref[...] = jnp.full(m_ref.shape, NEG, F32)
        l_ref[...] = jnp.zeros(l_ref.shape, F32)
        acc_ref[...] = jnp.zeros(acc_ref.shape, F32)
        rel = lax.broadcasted_iota(jnp.int32, (STRIP, tk), 0) - lax.broadcasted_iota(jnp.int32, (STRIP, tk), 1)

        def chunk(kc, masked):
            start = pl.multiple_of(kc * tk, tk)
            kb = k_ref[pl.ds(start, tk), :]
            vb = v_ref[pl.ds(start, tk), :]
            for h in range(2):
                s_ref[h] = _dot(qms[h], kb, NT)
            for h in range(2):
                cs = r_refs[h][kc]
                for st in range(n_strips):
                    rows = pl.ds(st * STRIP, STRIP)
                    s = s_ref[h, rows, :] - cs
                    if masked:
                        s = jnp.where(rel >= start - (i * tq + st * STRIP), s, NEG)
                    m_old = m_ref[h, rows, :]
                    mn = jnp.maximum(m_old, jnp.max(s, axis=-1, keepdims=True))
                    p = jnp.exp(s - mn)
                    alpha = jnp.exp(m_old - mn)
                    l_ref[h, rows, :] = alpha * l_ref[h, rows, :] + jnp.sum(p, axis=-1, keepdims=True)
                    m_ref[h, rows, :] = mn
                    p_ref[h, rows, :] = p.astype(BF16)
                    acc_ref[h, rows, :] = acc_ref[h, rows, :] * alpha
            for h in range(2):
                acc_ref[h] += _dot(p_ref[h], vb, NN)

        n_full = (i * tq) // tk

        def full_chunk(kc, _):
            chunk(kc, False)
            return 0

        lax.fori_loop(0, n_full, full_chunk, 0)
        chunk(n_full, True)
        o_ref[...] = jnp.where(lo, acc_ref[0] / l_ref[0], acc_ref[1] / l_ref[1])
        la_ref[...] = m_ref[0] + jnp.log(l_ref[0])
        lb_ref[...] = m_ref[1] + jnp.log(l_ref[1])

    row = lambda off: pl.BlockSpec((None, nk, 1, tk), lambda h, i: (2 * h + off, 0, 0, 0))
    lse = jax.ShapeDtypeStruct((HP, T, 1), F32)
    lspec = pl.BlockSpec((None, tq, 1), lambda h, i: (h, i, 0))
    scratch = [pltpu.VMEM((2, tq, tk), F32), pltpu.VMEM((2, tq, tk), BF16), pltpu.VMEM((2, tq, 1), F32),
               pltpu.VMEM((2, tq, 1), F32), pltpu.VMEM((2, tq, LANES), F32)]
    return _call(
        body, name=name, grid=(HP, T // tq), out_shape=(jax.ShapeDtypeStruct((T, Dh), F32), lse, lse),
        in_specs=[pl.BlockSpec((tq, LANES), lambda h, i: (i, h)), pl.BlockSpec((T, LANES), lambda h, i: (0, HP + h)),
                  pl.BlockSpec((T, LANES), lambda h, i: (0, h)), row(0), row(1)],
        out_specs=(pl.BlockSpec((tq, LANES), lambda h, i: (i, h)), lspec, lspec),
        args=[qk, qk, v, crow, crow], scratch_shapes=scratch, carry=carry)


def _fox_bwd(name, qk, v, o, do, crow, lse_a, lse_b, tq, tk, carry=None):
    T, Dh = v.shape
    HP = Dh // LANES
    nk = T // tk
    scale = HEAD_DIM ** -0.5
    assert tk % tq == 0 and tq % STRIP == 0
    n_strips = tq // STRIP

    def body(q_ref, k_ref, v_ref, o_ref, do_ref, ra_ref, rb_ref, la_ref, lb_ref,
             dq_ref, dk_ref, dv_ref, dca_ref, dcb_ref, dra_ref, drb_ref, s_ref, dp_ref, p_ref, ds_ref, dq_acc, dsum_ref):
        i = pl.program_id(1)

        @pl.when(i == 0)
        def _():
            dk_ref[...] = jnp.zeros_like(dk_ref)
            dv_ref[...] = jnp.zeros_like(dv_ref)
            dca_ref[...] = jnp.zeros_like(dca_ref)
            dcb_ref[...] = jnp.zeros_like(dcb_ref)

        q2 = q_ref[...]
        do2 = do_ref[...]
        lo = _lane_lo((tq, LANES))
        hi = jnp.logical_not(lo)
        qms = (_keep(lo, q2), _keep(hi, q2))
        doms = (_keep(lo, do2), _keep(hi, do2))
        prod = do2.astype(F32) * o_ref[...]
        dsum_ref[0] = jnp.sum(jnp.where(lo, prod, 0.0), axis=-1, keepdims=True)
        dsum_ref[1] = jnp.sum(jnp.where(lo, 0.0, prod), axis=-1, keepdims=True)
        r_refs, l_refs, dc_refs, dr_refs = (ra_ref, rb_ref), (la_ref, lb_ref), (dca_ref, dcb_ref), (dra_ref, drb_ref)
        dq_acc[...] = jnp.zeros(dq_acc.shape, F32)
        dra_ref[...] = jnp.zeros(dra_ref.shape, F32)
        drb_ref[...] = jnp.zeros(drb_ref.shape, F32)
        rel = lax.broadcasted_iota(jnp.int32, (STRIP, tk), 0) - lax.broadcasted_iota(jnp.int32, (STRIP, tk), 1)

        def chunk(kc, masked):
            start = pl.multiple_of(kc * tk, tk)
            kb = k_ref[pl.ds(start, tk), :]
            vb = v_ref[pl.ds(start, tk), :]
            for h in range(2):
                s_ref[h] = _dot(qms[h], kb, NT)
                dp_ref[h] = _dot(doms[h], vb, NT)
            for h in range(2):
                cs = r_refs[h][kc]
                col_sum = jnp.zeros((1, tk), F32)
                for st in range(n_strips):
                    rows = pl.ds(st * STRIP, STRIP)
                    s = s_ref[h, rows, :] - cs
                    if masked:
                        s = jnp.where(rel >= start - (i * tq + st * STRIP), s, NEG)
                    p = jnp.exp(s - l_refs[h][rows, :])
                    ds = p * (dp_ref[h, rows, :] - dsum_ref[h, rows, :])
                    p_ref[h, rows, :] = p.astype(BF16)
                    ds_ref[h, rows, :] = ds.astype(BF16)
                    col_sum = col_sum + jnp.sum(ds, axis=0, keepdims=True)
                    dr_refs[h][rows, :] += jnp.sum(ds, axis=-1, keepdims=True)
                dc_refs[h][kc] = dc_refs[h][kc] - col_sum
            dk = _dot(ds_ref[0], qms[0], TN) + _dot(ds_ref[1], qms[1], TN)
            dv = _dot(p_ref[0], doms[0], TN) + _dot(p_ref[1], doms[1], TN)
            dk_ref[pl.ds(start, tk), :] += dk
            dv_ref[pl.ds(start, tk), :] += dv
            for h in range(2):
                dq_acc[h] += _dot(ds_ref[h], kb, NN)

        n_full = (i * tq) // tk

        def full_chunk(kc, _):
            chunk(kc, False)
            return 0

        lax.fori_loop(0, n_full, full_chunk, 0)
        chunk(n_full, True)
        dq_ref[...] = jnp.where(lo, dq_acc[0], dq_acc[1]) * scale

    row = lambda off: pl.BlockSpec((None, nk, 1, tk), lambda h, i: (2 * h + off, 0, 0, 0))
    lspec = pl.BlockSpec((None, tq, 1), lambda h, i: (h, i, 0))
    qspec = pl.BlockSpec((tq, LANES), lambda h, i: (i, h))
    full = pl.BlockSpec((T, LANES), lambda h, i: (0, h))
    dcspec = pl.BlockSpec((None, nk, 1, tk), lambda h, i: (h, 0, 0, 0))
    grad = jax.ShapeDtypeStruct((T, Dh), F32)
    dc = jax.ShapeDtypeStruct((HP, nk, 1, tk), F32)
    dr = jax.ShapeDtypeStruct((HP, T, 1), F32)
    scratch = [pltpu.VMEM((2, tq, tk), F32), pltpu.VMEM((2, tq, tk), F32), pltpu.VMEM((2, tq, tk), BF16), pltpu.VMEM((2, tq, tk), BF16),
               pltpu.VMEM((2, tq, LANES), F32), pltpu.VMEM((2, tq, 1), F32)]
    return _call(
        body, name=name, grid=(HP, T // tq), out_shape=(grad, grad, grad, dc, dc, dr, dr),
        in_specs=[qspec, pl.BlockSpec((T, LANES), lambda h, i: (0, HP + h)), full, qspec, qspec, row(0), row(1), lspec, lspec],
        out_specs=(qspec, full, full, dcspec, dcspec, lspec, lspec),
        args=[qk, qk, v, o, do, crow, crow, lse_a, lse_b], scratch_shapes=scratch, carry=carry)


def _swa_block(n, q_ref, k_ref):
    qs = pl.multiple_of(n * WINDOW, WINDOW)
    ks = pl.multiple_of(jnp.maximum(n - 1, 0) * WINDOW, WINDOW)
    rel = (qs + lax.broadcasted_iota(jnp.int32, (WINDOW, 2 * WINDOW), 0)) - (ks + lax.broadcasted_iota(jnp.int32, (WINDOW, 2 * WINDOW), 1))
    valid = jnp.logical_and(rel >= 0, rel < WINDOW)
    return qs, ks, valid


def _swa_fwd(name, q, kd, vd, sinks, carry=None):
    T, Dh = q.shape
    HP = Dh // LANES

    def body(q_ref, k_ref, v_ref, sa_ref, sb_ref, o_ref, la_ref, lb_ref):
        lo = _lane_lo((WINDOW, LANES))

        def block(n, _):
            qs, ks, valid = _swa_block(n, q_ref, k_ref)
            q2 = q_ref[pl.ds(qs, WINDOW), :]
            kb = k_ref[pl.ds(ks, 2 * WINDOW), :]
            vb = v_ref[pl.ds(ks, 2 * WINDOW), :]
            res = []
            for sel, s_ref in ((lo, sa_ref), (jnp.logical_not(lo), sb_ref)):
                qm = _keep(sel, q2)
                sink = s_ref[...]
                s = jnp.where(valid, _dot(qm, kb, NT), NEG)
                m = jnp.maximum(jnp.max(s, axis=-1, keepdims=True), sink)
                p = jnp.exp(s - m)
                l = jnp.sum(p, axis=-1, keepdims=True) + jnp.exp(sink - m)
                res.append((_dot(p.astype(BF16), vb, NN) / l, m + jnp.log(l)))
            o_ref[pl.ds(qs, WINDOW), :] = jnp.where(lo, res[0][0], res[1][0])
            la_ref[pl.ds(qs, WINDOW), :] = res[0][1]
            lb_ref[pl.ds(qs, WINDOW), :] = res[1][1]
            return 0

        lax.fori_loop(0, T // WINDOW, block, 0, unroll=2)

    full = pl.BlockSpec((T, LANES), lambda h: (0, h))
    kv = pl.BlockSpec((T, LANES), lambda h: (0, h // 2))
    sink = lambda off: pl.BlockSpec((None, 1, 1), lambda h: (2 * h + off, 0, 0))
    lse = jax.ShapeDtypeStruct((HP, T, 1), F32)
    lspec = pl.BlockSpec((None, T, 1), lambda h: (h, 0, 0))
    return _call(
        body, name=name, grid=(HP,), out_shape=(jax.ShapeDtypeStruct((T, Dh), F32), lse, lse),
        in_specs=[full, kv, kv, sink(0), sink(1)], out_specs=(full, lspec, lspec),
        args=[q, kd, vd, sinks, sinks], carry=carry)


def _swa_bwd(name, q, kd, vd, sinks, o, do, lse_a, lse_b, carry=None):
    T, Dh = q.shape
    HP = Dh // LANES
    scale = HEAD_DIM ** -0.5

    def body(q_ref, k_ref, v_ref, sa_ref, sb_ref, o_ref, do_ref, la_ref, lb_ref, dq_ref, dk_ref, dv_ref, dsa_ref, dsb_ref):
        lo = _lane_lo((WINDOW, LANES))
        hi = jnp.logical_not(lo)
        dk_ref[...] = jnp.zeros_like(dk_ref)
        dv_ref[...] = jnp.zeros_like(dv_ref)

        def block(n, dsinks):
            qs, ks, valid = _swa_block(n, q_ref, k_ref)
            q2 = q_ref[pl.ds(qs, WINDOW), :]
            do2 = do_ref[pl.ds(qs, WINDOW), :]
            kb = k_ref[pl.ds(ks, 2 * WINDOW), :]
            vb = v_ref[pl.ds(ks, 2 * WINDOW), :]
            prod = do2.astype(F32) * o_ref[pl.ds(qs, WINDOW), :]
            dqs, new = [], []
            dk = jnp.zeros((2 * WINDOW, LANES), F32)
            dv = jnp.zeros((2 * WINDOW, LANES), F32)
            for sel, s_ref, l_ref, dsink in ((lo, sa_ref, la_ref, dsinks[0]), (hi, sb_ref, lb_ref, dsinks[1])):
                qm = _keep(sel, q2)
                dom = _keep(sel, do2)
                dsum = jnp.sum(jnp.where(sel, prod, 0.0), axis=-1, keepdims=True)
                lse = l_ref[pl.ds(qs, WINDOW), :]
                s = jnp.where(valid, _dot(qm, kb, NT), NEG)
                p = jnp.exp(s - lse)
                ds = p * (_dot(dom, vb, NT) - dsum)
                dsb = ds.astype(BF16)
                dqs.append(_dot(dsb, kb, NN))
                dk = dk + _dot(dsb, qm, TN)
                dv = dv + _dot(p.astype(BF16), dom, TN)
                new.append(dsink - jnp.sum(jnp.exp(s_ref[...] - lse) * dsum, axis=0, keepdims=True))
            dq_ref[pl.ds(qs, WINDOW), :] = jnp.where(lo, dqs[0], dqs[1]) * scale
            dk_ref[pl.ds(ks, 2 * WINDOW), :] += dk
            dv_ref[pl.ds(ks, 2 * WINDOW), :] += dv
            return tuple(new)

        dsa, dsb_ = lax.fori_loop(0, T // WINDOW, block, (jnp.zeros((1, 1), F32), jnp.zeros((1, 1), F32)), unroll=2)
        dsa_ref[...] = dsa
        dsb_ref[...] = dsb_

    full = pl.BlockSpec((T, LANES), lambda h: (0, h))
    kv = pl.BlockSpec((T, LANES), lambda h: (0, h // 2))
    sink = lambda off: pl.BlockSpec((None, 1, 1), lambda h: (2 * h + off, 0, 0))
    lspec = pl.BlockSpec((None, T, 1), lambda h: (h, 0, 0))
    dsink = pl.BlockSpec((None, 1, 1), lambda h: (h, 0, 0))
    grad = jax.ShapeDtypeStruct((T, Dh), F32)
    ds_shape = jax.ShapeDtypeStruct((HP, 1, 1), F32)
    return _call(
        body, name=name, grid=(HP,), out_shape=(grad, grad, grad, ds_shape, ds_shape),
        in_specs=[full, kv, kv, sink(0), sink(1), full, full, lspec, lspec],
        out_specs=(full, full, full, dsink, dsink),
        args=[q, kd, vd, sinks, sinks, o, do, lse_a, lse_b], carry=carry)


def _place():
    return lax.axis_index("x"), lax.axis_index("y"), lax.axis_index("c")


def _run_carry(name, carry):
    c_in, c_out = len(carry.inputs), len(carry.out_shapes)

    def body(*refs):
        ins, outs, scr = refs[:c_in], refs[c_in:c_in + c_out], refs[c_in + c_out:]
        carry.start(ins, outs, scr)
        carry.middle(ins, outs, scr)
        carry.finish(ins, outs, scr)

    return pl.pallas_call(
        body, out_shape=tuple(carry.out_shapes), in_specs=[_HBM] * c_in, out_specs=tuple([_HBM] * c_out),
        scratch_shapes=carry.scratch, name=name)(*carry.inputs)


def _gather_carry(shards):
    n = len(shards)

    def plan(ins, outs, scr):
        send, recv, local = scr
        x, y, c = _place()
        me, sibling = (x, y, c), (x, y, 1 - c)
        partner, other, diag = (x ^ c, y ^ (1 - c)), (x ^ (1 - c), y ^ c), (1 - x, 1 - y)

        def copy(w, k, block, to, src=None):
            slot = 4 * block[0] + 2 * block[1] + block[2]
            return pltpu.make_async_remote_copy(
                src_ref=outs[w].at[slot] if src is None else src, dst_ref=outs[w].at[slot],
                send_sem=send.at[w, k], recv_sem=recv.at[w, k], device_id=to, device_id_type=MESH)

        own = [pltpu.make_async_copy(ins[w], outs[w].at[4 * x + 2 * y + c], local.at[w]) for w in range(n)]
        return copy, own, me, sibling, partner, other, diag, c

    def start(ins, outs, scr):
        copy, own, me, sibling, partner, other, _, c = plan(ins, outs, scr)
        for cp in own:
            cp.start()
        for w in range(n):
            copy(w, 1, me, (*partner, c), src=ins[w]).start()
            copy(w, 2, me, (*other, c), src=ins[w]).start()
            copy(w, 0, me, sibling, src=ins[w]).start()

    def middle(ins, outs, scr):
        copy, _, me, sibling, partner, other, _, c = plan(ins, outs, scr)
        for w in range(n):
            copy(w, 1, (*partner, c), me).wait_recv()
            copy(w, 3, (*partner, c), (*other, c)).start()
            copy(w, 4, (*partner, c), sibling).start()

    def finish(ins, outs, scr):
        copy, own, me, sibling, partner, other, diag, c = plan(ins, outs, scr)
        for w in range(n):
            copy(w, 2, (*other, c), me).wait_recv()
            copy(w, 5, (*other, c), sibling).start()
        for w in range(n):
            copy(w, 3, (*diag, c), me).wait_recv()
            copy(w, 6, (*diag, c), sibling).start()
        for w in range(n):
            copy(w, 0, sibling, me).wait_recv()
            copy(w, 4, (*other, 1 - c), me).wait_recv()
            copy(w, 5, (*partner, 1 - c), me).wait_recv()
            copy(w, 6, (*diag, 1 - c), me).wait_recv()
        for w in range(n):
            sent = [copy(w, 0, me, sibling, src=ins[w]), copy(w, 1, me, (*partner, c), src=ins[w]), copy(w, 2, me, (*other, c), src=ins[w]),
                    copy(w, 3, (*partner, c), (*other, c)), copy(w, 4, (*partner, c), sibling), copy(w, 5, (*other, c), sibling),
                    copy(w, 6, (*diag, c), sibling)]
            for cp in sent:
                cp.wait_send()
        for cp in own:
            cp.wait()

    return _Carry(shards, [jax.ShapeDtypeStruct((N_DEV,) + s.shape, s.dtype) for s in shards],
                  [pltpu.SemaphoreType.DMA((n, 7)), pltpu.SemaphoreType.DMA((n, 7)), pltpu.SemaphoreType.DMA((n,))], start, finish, middle)


def _sibling_carry(grads):
    n = len(grads)

    def copies(ins, outs, scr):
        send, recv = scr
        x, y, c = _place()
        return [pltpu.make_async_remote_copy(
            src_ref=ins[w].at[2 * q + (1 - c)], dst_ref=outs[w].at[q], send_sem=send.at[w, q], recv_sem=recv.at[w, q],
            device_id=(x, y, 1 - c), device_id_type=MESH) for w in range(n) for q in range(4)]

    def start(ins, outs, scr):
        for cp in copies(ins, outs, scr):
            cp.start()

    def finish(ins, outs, scr):
        for cp in copies(ins, outs, scr):
            cp.wait()

    return _Carry(grads, [jax.ShapeDtypeStruct((4,) + g.shape[1:], g.dtype) for g in grads],
                  [pltpu.SemaphoreType.DMA((n, 4)), pltpu.SemaphoreType.DMA((n, 4))], start, finish)


def _to_partner_carry(sums):
    n = len(sums)

    def copies(ins, outs, scr):
        send, recv = scr
        x, y, c = _place()
        partner, diag = (x ^ c, y ^ (1 - c)), (1 - x, 1 - y)
        cps = []
        for w in range(n):
            for k, chip in enumerate((partner, diag)):
                cps.append(pltpu.make_async_remote_copy(
                    src_ref=ins[w].at[2 * chip[0] + chip[1]], dst_ref=outs[2 * w + k], send_sem=send.at[w, k], recv_sem=recv.at[w, k],
                    device_id=(*partner, c), device_id_type=MESH))
        return cps

    def start(ins, outs, scr):
        for cp in copies(ins, outs, scr):
            cp.start()

    def finish(ins, outs, scr):
        for cp in copies(ins, outs, scr):
            cp.wait()

    return _Carry(sums, [jax.ShapeDtypeStruct(s.shape[1:], s.dtype) for s in sums for _ in range(2)],
                  [pltpu.SemaphoreType.DMA((n, 2)), pltpu.SemaphoreType.DMA((n, 2))], start, finish)


def _to_other_carry(blocks):
    n = len(blocks)

    def copies(ins, outs, scr):
        send, recv = scr
        x, y, c = _place()
        return [pltpu.make_async_remote_copy(
            src_ref=ins[w], dst_ref=outs[w], send_sem=send.at[w], recv_sem=recv.at[w],
            device_id=(x ^ (1 - c), y ^ c, c), device_id_type=MESH) for w in range(n)]

    def start(ins, outs, scr):
        for cp in copies(ins, outs, scr):
            cp.start()

    def finish(ins, outs, scr):
        for cp in copies(ins, outs, scr):
            cp.wait()

    return _Carry(blocks, [jax.ShapeDtypeStruct(b.shape, b.dtype) for b in blocks],
                  [pltpu.SemaphoreType.DMA((n,)), pltpu.SemaphoreType.DMA((n,))], start, finish)


def _gather_small(packed):
    R, C = packed.shape

    def body(in_ref, out_ref, send, recv):
        x, y, c = _place()
        mine = 4 * x + 2 * y + c
        out_ref[mine] = in_ref[...]
        copies = []
        for k in range(1, N_DEV):
            peer = (x ^ (k >> 2), y ^ ((k >> 1) & 1), c ^ (k & 1))
            copies.append(pltpu.make_async_remote_copy(
                src_ref=in_ref, dst_ref=out_ref.at[mine], send_sem=send.at[k - 1], recv_sem=recv.at[k - 1],
                device_id=peer, device_id_type=MESH))
        for cp in copies:
            cp.start()
        for cp in copies:
            cp.wait()

    vmem = pl.BlockSpec(memory_space=pltpu.VMEM)
    return pl.pallas_call(
        body, out_shape=jax.ShapeDtypeStruct((N_DEV, R, C), F32), in_specs=[vmem], out_specs=vmem,
        scratch_shapes=[pltpu.SemaphoreType.DMA((N_DEV - 1,)), pltpu.SemaphoreType.DMA((N_DEV - 1,))],
        name="small_grads_all_gather")(packed)


def _adamw(w, g, m, v):
    m = ADAM_B1 * m + (1.0 - ADAM_B1) * g
    v = ADAM_B2 * v + (1.0 - ADAM_B2) * (g * g)
    m_hat = m / (1.0 - ADAM_B1 ** ADAM_STEP)
    v_hat = v / (1.0 - ADAM_B2 ** ADAM_STEP)
    delta = -ADAM_LR * (m_hat / (jnp.sqrt(v_hat) + ADAM_EPS) + ADAM_WD * w)
    return delta, m, v


def _pair_add(name, grads, received, c_idx):
    _, R, C = grads.shape
    tr = _row_tile(R)

    def body(c_ref, g_ref, r_ref, o_ref):
        o_ref[...] = (g_ref[...].astype(F32) + r_ref[...].astype(F32)).astype(BF16)

    blk = pl.BlockSpec((None, tr, C), lambda q, i, c: (q, i, 0))
    return pl.pallas_call(
        body, out_shape=jax.ShapeDtypeStruct((4, R, C), BF16),
        grid_spec=pltpu.PrefetchScalarGridSpec(
            num_scalar_prefetch=1, grid=(4, R // tr),
            in_specs=[pl.BlockSpec((None, tr, C), lambda q, i, c: (2 * q + c[0], i, 0)), blk], out_specs=blk),
        name=name, compiler_params=_params(2))(c_idx, grads, received)


def _relay_add(name, sums, relayed, other_idx):
    _, R, C = sums.shape
    tr = _row_tile(R)

    def body(q_ref, s_ref, r_ref, o_ref):
        o_ref[...] = (s_ref[...].astype(F32) + r_ref[...].astype(F32)).astype(BF16)

    blk = pl.BlockSpec((tr, C), lambda i, q: (i, 0))
    return pl.pallas_call(
        body, out_shape=jax.ShapeDtypeStruct((R, C), BF16),
        grid_spec=pltpu.PrefetchScalarGridSpec(
            num_scalar_prefetch=1, grid=(R // tr,),
            in_specs=[pl.BlockSpec((None, tr, C), lambda i, q: (q[0], i, 0)), blk], out_specs=blk),
        name=name, compiler_params=_params(1))(other_idx, sums, relayed)


def _adam_shard(name, sums, received, w, m, v, chip_idx):
    R, C = w.shape
    tr = _row_tile(R, 128)
    tc = C if tr < R or C % (2 * LANES) else 2 * LANES

    def body(q_ref, s_ref, ra_ref, rb_ref, w_ref, m_ref, v_ref, g_out, d_out, m_out, v_out):
        g = s_ref[...].astype(F32) + ra_ref[...].astype(F32) + rb_ref[...].astype(F32)
        delta, mn, vn = _adamw(w_ref[...], g, m_ref[...], v_ref[...])
        g_out[...] = g
        d_out[...] = delta
        m_out[...] = mn
        v_out[...] = vn

    blk = pl.BlockSpec((tr, tc), lambda i, j, q: (i, j))
    shape = jax.ShapeDtypeStruct((R, C), F32)
    return pl.pallas_call(
        body, out_shape=(shape,) * 4,
        grid_spec=pltpu.PrefetchScalarGridSpec(
            num_scalar_prefetch=1, grid=(R // tr, C // tc),
            in_specs=[pl.BlockSpec((None, tr, tc), lambda i, j, q: (q[0], i, j)), blk, blk, blk, blk, blk],
            out_specs=(blk,) * 4),
        name=name, compiler_params=_params(2))(chip_idx, sums, received[0], received[1], w, m, v)


def _adam_small(name, gathered, w, m, v):
    R, C = w.shape

    def body(ga_ref, w_ref, m_ref, v_ref, g_out, d_out, m_out, v_out):
        g = ga_ref[0]
        for d in range(1, N_DEV):
            g = g + ga_ref[d]
        delta, mn, vn = _adamw(w_ref[...], g, m_ref[...], v_ref[...])
        g_out[...] = g
        d_out[...] = delta
        m_out[...] = mn
        v_out[...] = vn

    full = pl.BlockSpec((R, C), lambda i: (0, 0))
    shape = jax.ShapeDtypeStruct((R, C), F32)
    return pl.pallas_call(
        body, out_shape=(shape,) * 4, grid=(1,),
        in_specs=[pl.BlockSpec((N_DEV, R, C), lambda i: (0, 0, 0)), full, full, full], out_specs=(full,) * 4,
        name=name, compiler_params=_params(1))(gathered, w, m, v)


def _pack_small(parts, D):
    g1, gmix, g2, gof, gos, bf, gqf, gkf, gqs, gks, sinks = [p.reshape(-1).astype(F32) for p in parts]
    row3 = jnp.concatenate([gof, gos])
    row4 = jnp.zeros((D,), F32)
    for slot, vec in enumerate((bf, gqf, gkf, gqs, gks, sinks)):
        row4 = lax.dynamic_update_slice(row4, vec, (slot * LANES,))
    zero = jnp.zeros((D,), F32)
    return jnp.stack([g1, gmix, g2, row3, row4, zero, zero, zero])


def _unpack_small(packed, D, H):
    Dh = D // 2
    row4 = packed[4]
    short = [row4[s * LANES:s * LANES + n] for s, n in enumerate((H, HEAD_DIM, HEAD_DIM, HEAD_DIM, HEAD_DIM, H))]
    vecs = [packed[0], packed[1], packed[2], packed[3, :Dh], packed[3, Dh:]] + short
    return [v[None, :] for v in vecs]


def kernel(x, positions, norm_ffn1_g, ffn1_w_gate, ffn1_w_up, ffn1_w_down, norm_mix_g, w_in, b_forget, fox_q_norm_g, fox_k_norm_g, swa_q_norm_g, swa_k_norm_g, swa_sinks, out_norm_fox_g, out_norm_swa_g, w_out, norm_ffn2_g, ffn2_w_gate, ffn2_w_up, ffn2_w_down, loss_target, m_norm_ffn1_g, m_ffn1_w_gate, m_ffn1_w_up, m_ffn1_w_down, m_norm_mix_g, m_w_in, m_b_forget, m_fox_q_norm_g, m_fox_k_norm_g, m_swa_q_norm_g, m_swa_k_norm_g, m_swa_sinks, m_out_norm_fox_g, m_out_norm_swa_g, m_w_out, m_norm_ffn2_g, m_ffn2_w_gate, m_ffn2_w_up, m_ffn2_w_down, v_norm_ffn1_g, v_ffn1_w_gate, v_ffn1_w_up, v_ffn1_w_down, v_norm_mix_g, v_w_in, v_b_forget, v_fox_q_norm_g, v_fox_k_norm_g, v_swa_q_norm_g, v_swa_k_norm_g, v_swa_sinks, v_out_norm_fox_g, v_out_norm_swa_g, v_w_out, v_norm_ffn2_g, v_ffn2_w_gate, v_ffn2_w_up, v_ffn2_w_down):
    xs = x[0]
    target = loss_target[0]
    T, D = xs.shape
    Dh = D // 2
    H = Dh // HEAD_DIM
    HP = H // 2
    KVW = (H // GQA_GROUP) * HEAD_DIM
    KVB = KVW // LANES
    MAIN = 4 * Dh + 2 * KVW
    F_OFF = 3 * Dh
    tm = min(ROW_TILE_CAP, T)
    tq = min(512, T)
    tk = min(512, T)
    nk = T // tk
    cx, cy, cc = _place()
    c_idx = jnp.reshape(cc, (1,)).astype(jnp.int32)
    chip_idx = jnp.reshape(2 * cx + cy, (1,)).astype(jnp.int32)
    other_idx = jnp.reshape(2 * (cx ^ (1 - cc)) + (cy ^ cc), (1,)).astype(jnp.int32)

    tr = jnp.transpose
    big_w = [tr(ffn1_w_gate[0]), tr(ffn1_w_up[0]), ffn1_w_down[0], tr(w_in[0]), w_out[0], tr(ffn2_w_gate[0]), tr(ffn2_w_up[0]),
             ffn2_w_down[0]]
    big_m = [tr(m_ffn1_w_gate[0]), tr(m_ffn1_w_up[0]), m_ffn1_w_down[0], tr(m_w_in[0]), m_w_out[0], tr(m_ffn2_w_gate[0]),
             tr(m_ffn2_w_up[0]), m_ffn2_w_down[0]]
    big_v = [tr(v_ffn1_w_gate[0]), tr(v_ffn1_w_up[0]), v_ffn1_w_down[0], tr(v_w_in[0]), v_w_out[0], tr(v_ffn2_w_gate[0]),
             tr(v_ffn2_w_up[0]), v_ffn2_w_down[0]]
    transposed = {"ffn1_w_gate", "ffn1_w_up", "w_in", "ffn2_w_gate", "ffn2_w_up"}
    names = ["ffn1_w_gate", "ffn1_w_up", "ffn1_w_down", "w_in", "w_out", "ffn2_w_gate", "ffn2_w_up", "ffn2_w_down"]
    sh = dict(zip(names, [w.astype(BF16) for w in big_w]))
    lane = jnp.arange(LANES)
    inv_freq = ROPE_THETA ** (-(2.0 * (lane % (HEAD_DIM // 2))).astype(F32) / HEAD_DIM)
    ang = positions[0].astype(F32)[:, None] * inv_freq[None, :]
    cos_t = jnp.cos(ang)
    sin_t = jnp.where((lane & (HEAD_DIM // 2)) == 0, -1.0, 1.0)[None, :] * jnp.sin(ang)
    rope = (cos_t, sin_t)

    def pair_gain(g, blocks):
        return jnp.tile(jnp.concatenate([g[0], g[0]])[None, None, :], (blocks, 1, 1))

    n1, (wg1,) = _rmsnorm_fwd("ffn1_norm", xs, norm_ffn1_g, tm, carry=_gather_carry([sh["ffn1_w_gate"]]))
    a1, (wu1,) = _ffn_gate("ffn1_gate", n1, wg1, tm, carry=_gather_carry([sh["ffn1_w_up"]]))
    (b1, hm1), (wd1,) = _ffn_up_only("ffn1_up", n1, wu1, a1, tm, carry=_gather_carry([sh["ffn1_w_down"]]))
    h1, (win_g,) = _ffn_down("ffn1_down", hm1, wd1, xs, tm, D, carry=_gather_carry([sh["w_in"]]))
    n_in = win_g.shape[1]
    win_t = win_g.reshape(N_DEV * n_in, D)
    win_main = jnp.concatenate([win_t[:F_OFF], win_t[F_OFF + H:]], axis=0)
    win_f = jnp.pad(win_t[F_OFF:F_OFF + H], ((0, LANES - H), (0, 0)))

    u = _rmsnorm_fwd("mix_norm", h1, norm_mix_g, tm)
    proj, (wout_g,) = _mm("mix_proj", u, win_main, tm, MAIN // 9, dims=NT, carry=_gather_carry([sh["w_out"]]))
    wout = wout_g.reshape(D, D)
    proj_f = _mm("mix_proj_forget", u, win_f, tm, LANES, dims=NT)
    scale = HEAD_DIM ** -0.5
    fox_gains = jnp.concatenate([pair_gain(fox_q_norm_g, HP), pair_gain(fox_k_norm_g, HP)])
    qk_f = _headnorm_fwd_scaled("fox_qk_norm", proj, 0, 2 * HP, fox_gains, T, scale, HP)
    v_f = proj[:, 2 * Dh:3 * Dh].astype(BF16)
    c_t, sg_t = _forget_fwd("forget_gates", proj_f[:, :H].T, b_forget.reshape(H, 1))
    crow = c_t.reshape(H, nk, 1, tk)
    (o_fox, lse_fa, lse_fb), (wg2, wu2) = _fox_fwd("fox_attention", qk_f, v_f, crow, tq, tk,
                                                   carry=_gather_carry([sh["ffn2_w_gate"], sh["ffn2_w_up"]]))

    swa_q_gains = pair_gain(swa_q_norm_g, HP)
    swa_k_gains = pair_gain(swa_k_norm_g, KVB)
    q_s = _headnorm_fwd("swa_q_norm", proj, 3 * HP, HP, swa_q_gains, T, scale, rope=rope)
    k_d = _headnorm_fwd("swa_k_norm", proj, 4 * HP, KVB, swa_k_gains, T, 1.0, rope=rope, dup=True)
    v_s = proj[:, 4 * Dh + KVW:].astype(BF16).reshape(T, H // GQA_GROUP, 1, HEAD_DIM)
    v_d = jnp.broadcast_to(v_s, (T, H // GQA_GROUP, 2, HEAD_DIM)).reshape(T, 2 * KVW)
    sinks3 = swa_sinks.reshape(H, 1, 1)
    o_swa, lse_sa, lse_sb = _swa_fwd("swa_attention", q_s, k_d, v_d, sinks3)

    on = _outnorm_fwd("out_norm", o_fox, o_swa, out_norm_fox_g, out_norm_swa_g, tm)
    h2 = _mm("mix_out", on, wout, tm, min(512, D), resid=h1)

    n2 = _rmsnorm_fwd("ffn2_norm", h2, norm_ffn2_g, tm)
    (a2, b2, hm2), (wd2,) = _ffn_up("ffn2_up", n2, wg2, wu2, tm, carry=_gather_carry([sh["ffn2_w_down"]]))
    y = _ffn_down("ffn2_down", hm2, wd2, h2, tm, D)
    dy, dyh, sq = _loss_grad("loss_grad", y, target, min(256, T))
    loss = lax.psum(0.5 * sq[0, 0] / D, ("x", "y", "c"))

    J, Fs, _ = wg2.shape
    aspec = pl.BlockSpec((None, tm, Fs), lambda i, j: (j, i, 0))
    wspec = pl.BlockSpec((None, Fs, D), lambda i, j: (j, 0, 0))
    got = {}

    def pair_sums(keys, grads, received):
        return [_pair_add("sum_" + nm, g, r, c_idx) for nm, g, r in zip(keys, grads, received)]

    def relay_sums(keys, sums, hop1):
        out = []
        for i, (nm, s) in enumerate(zip(keys, sums)):
            got[nm] = [hop1[2 * i]]
            out.append(_relay_add("relay_" + nm, s, hop1[2 * i + 1], other_idx))
        return out

    def arrived(keys, hop2):
        for nm, blk in zip(keys, hop2):
            got[nm].append(blk)

    dwd2 = _wgrad_down("ffn2_wgrad_down", hm2, dyh, min(1024, D))
    (da2, db2), (sib_d2,) = _ffn_bwd_mid("ffn2_bwd_mid", dyh, wd2, a2, b2, tm, carry=_sibling_carry([dwd2]))
    (sum_wd2,) = pair_sums(names[7:8], [dwd2], [sib_d2])
    (dwg2, dwu2), hop1 = _wgrad_up("ffn2_wgrad_up", n2, da2, db2, min(1024, D), carry=_to_partner_carry([sum_wd2]))
    (t_wd2,) = relay_sums(names[7:8], [sum_wd2], hop1)
    dn2, (via_wd2, *sib2) = _reduce_mm("ffn2_bwd_in", [(da2, aspec, wg2, wspec), (db2, aspec, wu2, wspec)], [], NN, T, D, tm, J,
                                       carry=_join(_to_other_carry([t_wd2]), _sibling_carry([dwg2, dwu2])))
    arrived(names[7:8], [via_wd2])
    dh2, dg_ffn2, dh2b = _rmsnorm_bwd("ffn2_norm_bwd", dn2, h2, norm_ffn2_g, dy, min(256, T), 1.0)
    sum_wg2, sum_wu2 = pair_sums(names[5:7], [dwg2, dwu2], sib2)

    dwout = _wgrad_2d("mix_out_wgrad", on, dh2b, min(512, D), min(1024, D))
    dwout_g = dwout.reshape(N_DEV, D // N_DEV, D)
    do_fox, dg_of = _outnorm_bwd("out_norm_bwd_fox", dh2b, wout, 0, o_fox, out_norm_fox_g, tm)
    do_swa, dg_os = _outnorm_bwd("out_norm_bwd_swa", dh2b, wout, 1, o_swa, out_norm_swa_g, tm)

    (dq_f, dk_f, dv_f, dc_a, dc_b, dr_a, dr_b), (*hop1, sib_wout) = _fox_bwd(
        "fox_attention_bwd", qk_f, v_f, o_fox, do_fox, crow, lse_fa, lse_fb, tq, tk,
        carry=_join(_to_partner_carry([sum_wg2, sum_wu2]), _sibling_carry([dwout_g])))
    t_wg2, t_wu2 = relay_sums(names[5:7], [sum_wg2, sum_wu2], hop1)
    (sum_wout,) = pair_sums(names[4:5], [dwout_g], [sib_wout])
    dqk_f = jnp.concatenate([dq_f, dk_f], axis=1)
    dqk_raw, dg_fox = _headnorm_bwd("fox_qk_norm_bwd", dqk_f, proj, 0, 2 * HP, fox_gains, HP, T, 1.0)
    dct = jnp.stack([dc_a.reshape(HP, T), dc_b.reshape(HP, T)], axis=1).reshape(H, T)
    drt = jnp.stack([dr_a.reshape(HP, T), dr_b.reshape(HP, T)], axis=1).reshape(H, T)
    dz_t, db_f = _forget_bwd("forget_gates_bwd", dct, drt, sg_t)

    (dq_s, dk_p, dv_p, dsink_a, dsink_b), hop2 = _swa_bwd(
        "swa_attention_bwd", q_s, k_d, v_d, sinks3, o_swa, do_swa, lse_sa, lse_sb, carry=_to_other_carry([t_wg2, t_wu2]))
    arrived(names[5:7], hop2)
    dqs_raw, dg_sq = _headnorm_bwd("swa_q_norm_bwd", dq_s, proj, 3 * HP, HP, swa_q_gains, HP, T, 1.0, rope=rope)
    dks_raw, dg_sk = _headnorm_bwd("swa_k_norm_bwd", dk_p, proj, 4 * HP, KVB, swa_k_gains, KVB, T, 1.0, rope=rope, fold=True)
    dvs_raw, _ = _headnorm_bwd("swa_v_fold", dv_p, None, 0, KVB, None, KVB, T, 1.0, fold=True, norm=False)

    dproj = jnp.concatenate([dqk_raw, dv_f.astype(BF16), dqs_raw, dks_raw, dvs_raw], axis=1)
    dproj_f = jnp.pad(dz_t.T, ((0, 0), (0, LANES - H))).astype(BF16)
    dwin_main, hop1 = _wgrad_2d("mix_proj_wgrad", dproj, u, MAIN // 9, min(1024, D), carry=_to_partner_carry([sum_wout]))
    (t_wout,) = relay_sums(names[4:5], [sum_wout], hop1)
    dwin_f = _wgrad_2d("mix_proj_forget_wgrad", dproj_f, u, LANES, min(1024, D))
    dwin_t = jnp.concatenate([dwin_main[:F_OFF], dwin_f[:H], dwin_main[F_OFF:]], axis=0)
    dwin_g = dwin_t.reshape(N_DEV, n_in, D)
    tkb = MAIN // 9
    du, (via_wout, sib_win) = _reduce_mm(
        "mix_bwd_in",
        [(dproj, pl.BlockSpec((tm, tkb), lambda i, r: (i, r)), win_main, pl.BlockSpec((tkb, D), lambda i, r: (r, 0)))],
        [(dproj_f, pl.BlockSpec((tm, LANES), lambda i, r: (i, 0)), win_f, pl.BlockSpec((LANES, D), lambda i, r: (0, 0)))],
        NN, T, D, tm, 9, carry=_join(_to_other_carry([t_wout]), _sibling_carry([dwin_g])))
    arrived(names[4:5], [via_wout])
    dh1, dg_mix, dh1h = _rmsnorm_bwd("mix_norm_bwd", du, h1, norm_mix_g, dh2, min(256, T), 0.5)
    (sum_win,) = pair_sums(names[3:4], [dwin_g], [sib_win])

    dwd1, hop1 = _wgrad_down("ffn1_wgrad_down", hm1, dh1h, min(1024, D), carry=_to_partner_carry([sum_win]))
    (t_win,) = relay_sums(names[3:4], [sum_win], hop1)
    (da1, db1), (via_win, sib_d) = _ffn_bwd_mid("ffn1_bwd_mid", dh1h, wd1, a1, b1, tm,
                                                carry=_join(_to_other_carry([t_win]), _sibling_carry([dwd1])))
    arrived(names[3:4], [via_win])
    (sum_wd1,) = pair_sums(names[2:3], [dwd1], [sib_d])
    dwg1, hop1 = _wgrad_down("ffn1_wgrad_gate", da1, n1, min(1024, D), carry=_to_partner_carry([sum_wd1]))
    (t_wd1,) = relay_sums(names[2:3], [sum_wd1], hop1)
    dwu1, (via_wd1, sib_g) = _wgrad_down("ffn1_wgrad_up", db1, n1, min(1024, D),
                                         carry=_join(_to_other_carry([t_wd1]), _sibling_carry([dwg1])))
    arrived(names[2:3], [via_wd1])
    (sum_wg1,) = pair_sums(names[0:1], [dwg1], [sib_g])
    dn1_gate, (*hop1, sib_u) = _reduce_mm(
        "ffn1_bwd_in_gate", [(da1, aspec, wg1, wspec)], [], NN, T, D, tm, J,
        carry=_join(_to_partner_carry([sum_wg1]), _sibling_carry([dwu1])))
    (t_wg1,) = relay_sums(names[0:1], [sum_wg1], hop1)
    (sum_wu1,) = pair_sums(names[1:2], [dwu1], [sib_u])
    dn1, (via_wg1, *hop1) = _reduce_mm(
        "ffn1_bwd_in_up", [(db1, aspec, wu1, wspec)], [], NN, T, D, tm, J, init=dn1_gate,
        carry=_join(_to_other_carry([t_wg1]), _to_partner_carry([sum_wu1])))
    arrived(names[0:1], [via_wg1])
    (t_wu1,) = relay_sums(names[1:2], [sum_wu1], hop1)
    arrived(names[1:2], _run_carry("grads_exchange", _to_other_carry([t_wu1])))
    dx, dg_ffn1 = _rmsnorm_bwd("ffn1_norm_bwd", dn1, xs, norm_ffn1_g, dh1, min(256, T), None)

    chip_sums = [sum_wg1, sum_wu1, sum_wd1, sum_win, sum_wout, sum_wg2, sum_wu2, sum_wd2]
    big_out = [_adam_shard("adam_" + nm, s, got[nm], w, m, v, chip_idx)
               for nm, s, w, m, v in zip(names, chip_sums, big_w, big_m, big_v)]

    dsinks = jnp.stack([dsink_a.reshape(HP), dsink_b.reshape(HP)], axis=1).reshape(H)
    small_g = [dg_ffn1, dg_mix, dg_ffn2, dg_of, dg_os, db_f, dg_fox[0, 0, :HEAD_DIM], dg_fox[1, 0, :HEAD_DIM],
               dg_sq[0, 0, :HEAD_DIM], dg_sk[0, 0, :HEAD_DIM], dsinks]
    small_w = [norm_ffn1_g, norm_mix_g, norm_ffn2_g, out_norm_fox_g, out_norm_swa_g, b_forget, fox_q_norm_g, fox_k_norm_g,
               swa_q_norm_g, swa_k_norm_g, swa_sinks]
    small_m = [m_norm_ffn1_g, m_norm_mix_g, m_norm_ffn2_g, m_out_norm_fox_g, m_out_norm_swa_g, m_b_forget, m_fox_q_norm_g,
               m_fox_k_norm_g, m_swa_q_norm_g, m_swa_k_norm_g, m_swa_sinks]
    small_v = [v_norm_ffn1_g, v_norm_mix_g, v_norm_ffn2_g, v_out_norm_fox_g, v_out_norm_swa_g, v_b_forget, v_fox_q_norm_g,
               v_fox_k_norm_g, v_swa_q_norm_g, v_swa_k_norm_g, v_swa_sinks]
    gathered = _gather_small(_pack_small(small_g, D))
    small_out = _adam_small("adam_small", gathered, _pack_small(small_w, D), _pack_small(small_m, D), _pack_small(small_v, D))
    small_out = [_unpack_small(p, D, H) for p in small_out]

    order = ["norm_ffn1_g", "ffn1_w_gate", "ffn1_w_up", "ffn1_w_down", "norm_mix_g", "w_in", "b_forget", "fox_q_norm_g", "fox_k_norm_g",
             "swa_q_norm_g", "swa_k_norm_g", "swa_sinks", "out_norm_fox_g", "out_norm_swa_g", "w_out", "norm_ffn2_g",
             "ffn2_w_gate", "ffn2_w_up", "ffn2_w_down"]
    small_names = ["norm_ffn1_g", "norm_mix_g", "norm_ffn2_g", "out_norm_fox_g", "out_norm_swa_g", "b_forget", "fox_q_norm_g",
                   "fox_k_norm_g", "swa_q_norm_g", "swa_k_norm_g", "swa_sinks"]
    result = [loss, dx[None]]
    for kind in range(4):
        for nm in order:
            if nm in names:
                leaf = big_out[names.index(nm)][kind]
                result.append((tr(leaf) if nm in transposed else leaf)[None])
            else:
                result.append(small_out[kind][small_names.index(nm)])
    return tuple(result)


def _headnorm_fwd_scaled(name, proj, col_off, ncb, gains, tm, scale, n_scaled):
    T = proj.shape[0]

    def body(x_ref, g_ref, o_ref):
        xv = x_ref[...]
        lo = _lane_lo(xv.shape)
        y = xv * _head_rstd(xv, lo) * g_ref[...]
        y = y * jnp.where(pl.program_id(0) < n_scaled, scale, 1.0)
        o_ref[...] = y.astype(BF16)

    return pl.pallas_call(
        body, out_shape=jax.ShapeDtypeStruct((T, ncb * LANES), BF16), grid=(ncb, T // tm),
        in_specs=[pl.BlockSpec((tm, LANES), lambda c, i: (i, col_off + c)), pl.BlockSpec((None, 1, LANES), lambda c, i: (c, 0, 0))],
        out_specs=pl.BlockSpec((tm, LANES), lambda c, i: (i, c)), name=name, compiler_params=_params(2))(proj, gains)
```

```python
import functools

import jax
import jax.numpy as jnp
from jax import lax
from jax.experimental import pallas as pl
from jax.experimental.pallas import tpu as pltpu

F32 = jnp.float32
BF16 = jnp.bfloat16

HEAD_DIM = 64
LANES = 128
WINDOW = 128
GQA_GROUP = 4
EPS = 1e-6
ROPE_THETA = 10000.0
ADAM_LR = 0.001
ADAM_B1 = 0.9
ADAM_B2 = 0.999
ADAM_EPS = 1e-08
ADAM_WD = 0.01
ADAM_STEP = 10
N_DEV = 8
NEG = -1e30
VMEM_LIMIT_V7X = 48 * 1024 * 1024
ROW_TILE_CAP = 512
MESH = pl.DeviceIdType.MESH

NN = (((1,), (0,)), ((), ()))
NT = (((1,), (1,)), ((), ()))
TN = (((0,), (0,)), ((), ()))


def _dot(a, b, dims):
    return lax.dot_general(a, b, dims, preferred_element_type=F32)


def _params(n_axes):
    return pltpu.CompilerParams(dimension_semantics=("arbitrary",) * n_axes, vmem_limit_bytes=VMEM_LIMIT_V7X)


def _row_tile(rows, cap=ROW_TILE_CAP):
    best = None
    for t in range(16, min(rows, cap) + 1, 16):
        if rows % t == 0:
            best = t
    return best or rows


def _lane_lo(shape):
    return lax.broadcasted_iota(jnp.int32, shape, len(shape) - 1) < HEAD_DIM


def _keep(sel, x):
    return jnp.where(sel, x.astype(F32), 0.0).astype(BF16)


_HBM = pl.BlockSpec(memory_space=pltpu.HBM)


class _Carry:
    def __init__(self, inputs, out_shapes, scratch, start, finish, middle=None):
        self.inputs, self.out_shapes, self.scratch = list(inputs), list(out_shapes), list(scratch)
        self.start, self.finish, self.middle = start, finish, middle or (lambda ins, outs, scr: None)


def _join(*carries):
    def hook(which):
        def run(ins, outs, scr):
            i = o = s = 0
            for c in carries:
                ni, no, ns = len(c.inputs), len(c.out_shapes), len(c.scratch)
                getattr(c, which)(ins[i:i + ni], outs[o:o + no], scr[s:s + ns])
                i, o, s = i + ni, o + no, s + ns
        return run

    return _Carry([a for c in carries for a in c.inputs], [a for c in carries for a in c.out_shapes],
                  [a for c in carries for a in c.scratch], hook("start"), hook("finish"), hook("middle"))


def _call(body, *, name, grid, in_specs, out_specs, out_shape, args, scratch_shapes=(), carry=None):
    params = _params(len(grid))
    if carry is None:
        return pl.pallas_call(body, out_shape=out_shape, grid=grid, in_specs=list(in_specs), out_specs=out_specs,
                              scratch_shapes=list(scratch_shapes), name=name, compiler_params=params)(*args)
    single = not isinstance(out_shape, (tuple, list))
    shapes = (out_shape,) if single else tuple(out_shape)
    specs = (out_specs,) if single else tuple(out_specs)
    n_in, n_out, n_scr = len(args), len(shapes), len(scratch_shapes)
    c_in, c_out = len(carry.inputs), len(carry.out_shapes)

    def wrapped(*refs):
        ins, c_ins = refs[:n_in], refs[n_in:n_in + c_in]
        o0 = n_in + c_in
        outs, c_outs = refs[o0:o0 + n_out], refs[o0 + n_out:o0 + n_out + c_out]
        s0 = o0 + n_out + c_out
        scr, c_scr = refs[s0:s0 + n_scr], refs[s0 + n_scr:]
        step, total = pl.program_id(0), grid[0]
        for ax in range(1, len(grid)):
            step, total = step * grid[ax] + pl.program_id(ax), total * grid[ax]

        @pl.when(step == 0)
        def _():
            carry.start(c_ins, c_outs, c_scr)

        @pl.when(step == total // 2)
        def _():
            carry.middle(c_ins, c_outs, c_scr)

        body(*ins, *outs, *scr)

        @pl.when(step == total - 1)
        def _():
            carry.finish(c_ins, c_outs, c_scr)

    res = pl.pallas_call(
        wrapped, out_shape=shapes + tuple(carry.out_shapes), grid=grid, in_specs=list(in_specs) + [_HBM] * c_in,
        out_specs=specs + (_HBM,) * c_out, scratch_shapes=list(scratch_shapes) + carry.scratch, name=name,
        compiler_params=params)(*args, *carry.inputs)
    main = res[:n_out]
    return (main[0] if single else tuple(main)), tuple(res[n_out:])


def _rms_bwd(dn, x, g):
    r = lax.rsqrt(jnp.mean(x * x, axis=-1, keepdims=True) + EPS)
    xh = x * r
    dxh = dn * g
    dx = r * (dxh - xh * jnp.mean(dxh * xh, axis=-1, keepdims=True))
    return dx, jnp.sum(dn * xh, axis=0, keepdims=True)


def _rmsnorm_fwd(name, x, g, tm, carry=None):
    T, D = x.shape

    def body(x_ref, g_ref, o_ref):
        xf = x_ref[...]
        r = lax.rsqrt(jnp.mean(xf * xf, axis=-1, keepdims=True) + EPS)
        o_ref[...] = (xf * r * g_ref[...]).astype(BF16)

    return _call(
        body, name=name, grid=(T // tm,), out_shape=jax.ShapeDtypeStruct((T, D), BF16),
        in_specs=[pl.BlockSpec((tm, D), lambda i: (i, 0)), pl.BlockSpec((1, D), lambda i: (0, 0))],
        out_specs=pl.BlockSpec((tm, D), lambda i: (i, 0)), args=[x, g], carry=carry)


def _outnorm_fwd(name, o_fox, o_swa, g_fox, g_swa, tm):
    T, Dh = o_fox.shape

    def body(a_ref, b_ref, ga_ref, gb_ref, o_ref):
        for ref, g_ref, lo in ((a_ref, ga_ref, 0), (b_ref, gb_ref, Dh)):
            xf = ref[...]
            r = lax.rsqrt(jnp.mean(xf * xf, axis=-1, keepdims=True) + EPS)
            o_ref[:, lo:lo + Dh] = (xf * r * g_ref[...]).astype(BF16)

    row = pl.BlockSpec((tm, Dh), lambda i: (i, 0))
    gain = pl.BlockSpec((1, Dh), lambda i: (0, 0))
    return pl.pallas_call(
        body, out_shape=jax.ShapeDtypeStruct((T, 2 * Dh), BF16), grid=(T // tm,),
        in_specs=[row, row, gain, gain], out_specs=pl.BlockSpec((tm, 2 * Dh), lambda i: (i, 0)),
        name=name, compiler_params=_params(1))(o_fox, o_swa, g_fox, g_swa)


def _outnorm_bwd(name, dhb, wout, half, o, g, tm):
    T, D = dhb.shape
    Dh = o.shape[1]

    def body(a_ref, w_ref, o_ref, g_ref, do_ref, dg_ref):
        don = _dot(a_ref[...], w_ref[...], NT)
        dx, dg = _rms_bwd(don, o_ref[...], g_ref[...])
        do_ref[...] = dx.astype(BF16)

        @pl.when(pl.program_id(0) == 0)
        def _():
            dg_ref[...] = dg

        @pl.when(pl.program_id(0) > 0)
        def _():
            dg_ref[...] += dg

    return pl.pallas_call(
        body, out_shape=(jax.ShapeDtypeStruct((T, Dh), BF16), jax.ShapeDtypeStruct((1, Dh), F32)), grid=(T // tm,),
        in_specs=[pl.BlockSpec((tm, D), lambda i: (i, 0)), pl.BlockSpec((Dh, D), lambda i: (half, 0)),
                  pl.BlockSpec((tm, Dh), lambda i: (i, 0)), pl.BlockSpec((1, Dh), lambda i: (0, 0))],
        out_specs=(pl.BlockSpec((tm, Dh), lambda i: (i, 0)), pl.BlockSpec((1, Dh), lambda i: (0, 0))),
        name=name, compiler_params=_params(1))(dhb, wout, o, g)


def _mm(name, a, b, tm, tn, dims=NN, resid=None, carry=None):
    M, K = a.shape
    transposed = dims == NT
    N = b.shape[0] if transposed else b.shape[1]

    def body(*refs):
        if resid is None:
            a_ref, b_ref, o_ref = refs
            o_ref[...] = _dot(a_ref[...], b_ref[...], dims)
        else:
            a_ref, b_ref, r_ref, o_ref = refs
            o_ref[...] = r_ref[...] + _dot(a_ref[...], b_ref[...], dims)

    ospec = pl.BlockSpec((tm, tn), lambda n, i: (i, n))
    bspec = pl.BlockSpec((tn, K), lambda n, i: (n, 0)) if transposed else pl.BlockSpec((K, tn), lambda n, i: (0, n))
    in_specs = [pl.BlockSpec((tm, K), lambda n, i: (i, 0)), bspec]
    args = [a, b]
    if resid is not None:
        in_specs.append(ospec)
        args.append(resid)
    return _call(body, name=name, grid=(N // tn, M // tm), in_specs=in_specs, out_specs=ospec,
                 out_shape=jax.ShapeDtypeStruct((M, N), F32), args=args, carry=carry)


def _wgrad_2d(name, a, b, tmm, tn, carry=None):
    T, M = a.shape
    N = b.shape[1]

    def body(a_ref, b_ref, o_ref):
        o_ref[...] = _dot(a_ref[...], b_ref[...], TN).astype(BF16)

    return _call(
        body, name=name, grid=(M // tmm, N // tn), out_shape=jax.ShapeDtypeStruct((M, N), BF16),
        in_specs=[pl.BlockSpec((T, tmm), lambda m, n: (0, m)), pl.BlockSpec((T, tn), lambda m, n: (0, n))],
        out_specs=pl.BlockSpec((tmm, tn), lambda m, n: (m, n)), args=[a, b], carry=carry)


def _wgrad_down(name, hm, df, tn, carry=None):
    J, T, Fs = hm.shape
    D = df.shape[1]

    def body(a_ref, b_ref, o_ref):
        o_ref[...] = _dot(a_ref[...], b_ref[...], TN).astype(BF16)

    return _call(
        body, name=name, grid=(J, D // tn), out_shape=jax.ShapeDtypeStruct((J, Fs, D), BF16),
        in_specs=[pl.BlockSpec((None, T, Fs), lambda j, n: (j, 0, 0)), pl.BlockSpec((T, tn), lambda j, n: (0, n))],
        out_specs=pl.BlockSpec((None, Fs, tn), lambda j, n: (j, 0, n)), args=[hm, df], carry=carry)


def _wgrad_up(name, n, da, db, tn, carry=None):
    T, D = n.shape
    J, _, Fs = da.shape

    def body(n_ref, da_ref, db_ref, og_ref, ou_ref):
        nv = n_ref[...]
        og_ref[...] = _dot(da_ref[...], nv, TN).astype(BF16)
        ou_ref[...] = _dot(db_ref[...], nv, TN).astype(BF16)

    act = pl.BlockSpec((None, T, Fs), lambda j, m: (j, 0, 0))
    out = pl.BlockSpec((None, Fs, tn), lambda j, m: (j, 0, m))
    shape = jax.ShapeDtypeStruct((J, Fs, D), BF16)
    return _call(
        body, name=name, grid=(J, D // tn), out_shape=(shape, shape),
        in_specs=[pl.BlockSpec((T, tn), lambda j, m: (0, m)), act, act], out_specs=(out, out),
        args=[n, da, db], carry=carry)


def _reduce_mm(name, pairs, once, dims, T, D, tm, steps, init=None, carry=None):
    n_pairs = len(pairs)
    n_once = len(once)
    n_mm = 2 * (n_pairs + n_once)

    def body(*refs):
        pr = refs[:2 * n_pairs]
        on = refs[2 * n_pairs:n_mm]
        o_ref, acc = refs[-2:]
        r = pl.program_id(1)

        @pl.when(r == 0)
        def _():
            acc[...] = jnp.zeros(acc.shape, F32) if init is None else refs[n_mm][...]

        for p in range(n_pairs):
            acc[...] += _dot(pr[2 * p][...], pr[2 * p + 1][...], dims)

        @pl.when(r == steps - 1)
        def _():
            dn = acc[...]
            for p in range(n_once):
                dn = dn + _dot(on[2 * p][...], on[2 * p + 1][...], dims)
            o_ref[...] = dn

    in_specs, args = [], []
    for a, a_spec, w, w_spec in list(pairs) + list(once):
        in_specs += [a_spec, w_spec]
        args += [a, w]
    row = pl.BlockSpec((tm, D), lambda i, r: (i, 0))
    if init is not None:
        in_specs.append(row)
        args.append(init)
    return _call(body, name=name, grid=(T // tm, steps), in_specs=in_specs, out_specs=row, out_shape=jax.ShapeDtypeStruct((T, D), F32),
                 args=args, scratch_shapes=[pltpu.VMEM((tm, D), F32)], carry=carry)


def _rmsnorm_bwd(name, dn, x, g, dh, tm, bf16_scale, carry=None):
    T, D = x.shape
    emit_bf16 = bf16_scale is not None

    def body(dn_ref, x_ref, g_ref, dh_ref, *outs):
        dxn, dg = _rms_bwd(dn_ref[...], x_ref[...], g_ref[...])
        dx = dh_ref[...] + dxn
        outs[0][...] = dx
        if emit_bf16:
            outs[2][...] = (bf16_scale * dx).astype(BF16)

        @pl.when(pl.program_id(0) == 0)
        def _():
            outs[1][...] = dg

        @pl.when(pl.program_id(0) > 0)
        def _():
            outs[1][...] += dg

    row = pl.BlockSpec((tm, D), lambda i: (i, 0))
    gain = pl.BlockSpec((1, D), lambda i: (0, 0))
    out_shape = [jax.ShapeDtypeStruct((T, D), F32), jax.ShapeDtypeStruct((1, D), F32)]
    out_specs = [row, gain]
    if emit_bf16:
        out_shape.append(jax.ShapeDtypeStruct((T, D), BF16))
        out_specs.append(row)
    return _call(body, name=name, grid=(T // tm,), in_specs=[row, row, gain, row], out_specs=tuple(out_specs),
                 out_shape=tuple(out_shape), args=[dn, x, g, dh], carry=carry)


def _loss_grad(name, y, target, tm):
    T, D = y.shape

    def body(y_ref, t_ref, dy_ref, dyh_ref, sq_ref):
        diff = y_ref[...] - t_ref[...]
        sq = jnp.sum(jnp.sum(diff * diff, axis=1, keepdims=True), axis=0, keepdims=True)
        dy = diff * (1.0 / D)
        dy_ref[...] = dy
        dyh_ref[...] = (0.5 * dy).astype(BF16)

        @pl.when(pl.program_id(0) == 0)
        def _():
            sq_ref[...] = sq

        @pl.when(pl.program_id(0) > 0)
        def _():
            sq_ref[...] += sq

    row = pl.BlockSpec((tm, D), lambda i: (i, 0))
    return pl.pallas_call(
        body, out_shape=(jax.ShapeDtypeStruct((T, D), F32), jax.ShapeDtypeStruct((T, D), BF16), jax.ShapeDtypeStruct((1, 1), F32)),
        grid=(T // tm,), in_specs=[row, row], out_specs=(row, row, pl.BlockSpec((1, 1), lambda i: (0, 0))),
        name=name, compiler_params=_params(1))(y, target)


def _ffn_up(name, n, wg, wu, tm, carry=None):
    T, D = n.shape
    J, Fs, _ = wg.shape

    def body(n_ref, wg_ref, wu_ref, a_ref, b_ref, h_ref):
        xv = n_ref[...]
        a = _dot(xv, wg_ref[...], NT)
        b = _dot(xv, wu_ref[...], NT)
        a_ref[...] = a.astype(BF16)
        b_ref[...] = b.astype(BF16)
        h_ref[...] = (a * jax.nn.sigmoid(a) * b).astype(BF16)

    act = jax.ShapeDtypeStruct((J, T, Fs), BF16)
    wspec = pl.BlockSpec((None, Fs, D), lambda j, i: (j, 0, 0))
    aspec = pl.BlockSpec((None, tm, Fs), lambda j, i: (j, i, 0))
    return _call(
        body, name=name, grid=(J, T // tm), out_shape=(act, act, act),
        in_specs=[pl.BlockSpec((tm, D), lambda j, i: (i, 0)), wspec, wspec], out_specs=(aspec, aspec, aspec),
        args=[n, wg, wu], carry=carry)


def _ffn_gate(name, n, wg, tm, carry=None):
    T, D = n.shape
    J, Fs, _ = wg.shape

    def body(n_ref, wg_ref, a_ref):
        a_ref[...] = _dot(n_ref[...], wg_ref[...], NT).astype(BF16)

    aspec = pl.BlockSpec((None, tm, Fs), lambda j, i: (j, i, 0))
    return _call(
        body, name=name, grid=(J, T // tm), out_shape=jax.ShapeDtypeStruct((J, T, Fs), BF16),
        in_specs=[pl.BlockSpec((tm, D), lambda j, i: (i, 0)), pl.BlockSpec((None, Fs, D), lambda j, i: (j, 0, 0))],
        out_specs=aspec, args=[n, wg], carry=carry)


def _ffn_up_only(name, n, wu, a, tm, carry=None):
    T, D = n.shape
    J, Fs, _ = wu.shape

    def body(n_ref, wu_ref, a_ref, b_ref, h_ref):
        b = _dot(n_ref[...], wu_ref[...], NT)
        a = a_ref[...].astype(F32)
        b_ref[...] = b.astype(BF16)
        h_ref[...] = (a * jax.nn.sigmoid(a) * b).astype(BF16)

    act = jax.ShapeDtypeStruct((J, T, Fs), BF16)
    aspec = pl.BlockSpec((None, tm, Fs), lambda j, i: (j, i, 0))
    return _call(
        body, name=name, grid=(J, T // tm), out_shape=(act, act),
        in_specs=[pl.BlockSpec((tm, D), lambda j, i: (i, 0)), pl.BlockSpec((None, Fs, D), lambda j, i: (j, 0, 0)), aspec],
        out_specs=(aspec, aspec), args=[n, wu, a], carry=carry)


def _ffn_down(name, hm, wd, resid, tm, tn, carry=None):
    J, T, Fs = hm.shape
    D = wd.shape[2]

    def body(h_ref, w_ref, r_ref, o_ref, acc):
        j = pl.program_id(2)

        @pl.when(j == 0)
        def _():
            acc[...] = jnp.zeros(acc.shape, F32)

        acc[...] += _dot(h_ref[...], w_ref[...], NN)

        @pl.when(j == J - 1)
        def _():
            o_ref[...] = r_ref[...] + 0.5 * acc[...]

    tile = pl.BlockSpec((tm, tn), lambda i, n, j: (i, n))
    return _call(
        body, name=name, grid=(T // tm, D // tn, J), out_shape=jax.ShapeDtypeStruct((T, D), F32),
        in_specs=[pl.BlockSpec((None, tm, Fs), lambda i, n, j: (j, i, 0)), pl.BlockSpec((None, Fs, tn), lambda i, n, j: (j, 0, n)), tile],
        out_specs=tile, scratch_shapes=[pltpu.VMEM((tm, tn), F32)], args=[hm, wd, resid], carry=carry)


def _ffn_bwd_mid(name, dfh, wd, a, b, tm, carry=None):
    T, D = dfh.shape
    J, Fs, _ = wd.shape

    def body(df_ref, w_ref, a_ref, b_ref, da_ref, db_ref):
        dhm = _dot(df_ref[...], w_ref[...], NT)
        av = a_ref[...].astype(F32)
        bv = b_ref[...].astype(F32)
        sg = jax.nn.sigmoid(av)
        da_ref[...] = (dhm * bv * (sg * (1.0 + av * (1.0 - sg)))).astype(BF16)
        db_ref[...] = (dhm * (av * sg)).astype(BF16)

    act = jax.ShapeDtypeStruct((J, T, Fs), BF16)
    aspec = pl.BlockSpec((None, tm, Fs), lambda j, i: (j, i, 0))
    return _call(
        body, name=name, grid=(J, T // tm), out_shape=(act, act),
        in_specs=[pl.BlockSpec((tm, D), lambda j, i: (i, 0)), pl.BlockSpec((None, Fs, D), lambda j, i: (j, 0, 0)), aspec, aspec],
        out_specs=(aspec, aspec), args=[dfh, wd, a, b], carry=carry)


def _rot_half(y, lane):
    first = (lane & (HEAD_DIM // 2)) == 0
    return jnp.where(first, pltpu.roll(y, LANES - HEAD_DIM // 2, 1), pltpu.roll(y, HEAD_DIM // 2, 1))


def _head_rstd(x, lo):
    sq = x * x
    ss_a = jnp.sum(jnp.where(lo, sq, 0.0), axis=-1, keepdims=True)
    ss_b = jnp.sum(jnp.where(lo, 0.0, sq), axis=-1, keepdims=True)
    return lax.rsqrt(jnp.where(lo, ss_a, ss_b) * (1.0 / HEAD_DIM) + EPS)


def _headnorm_fwd(name, proj, col_off, ncb, gains, tm, scale, rope=None, dup=False):
    T = proj.shape[0]
    with_rope = rope is not None
    width = 2 * LANES if dup else LANES

    def body(*refs):
        if with_rope:
            x_ref, g_ref, cos_ref, sin_ref, o_ref = refs
        else:
            x_ref, g_ref, o_ref = refs
        xv = x_ref[...]
        lane = lax.broadcasted_iota(jnp.int32, xv.shape, 1)
        lo = lane < HEAD_DIM
        y = xv * _head_rstd(xv, lo) * g_ref[...]
        if with_rope:
            y = y * cos_ref[...] + _rot_half(y, lane) * sin_ref[...]
        y = y * scale
        if dup:
            sw = pltpu.roll(y, HEAD_DIM, 1)
            o_ref[:, :LANES] = jnp.where(lo, y, sw).astype(BF16)
            o_ref[:, LANES:] = jnp.where(lo, sw, y).astype(BF16)
        else:
            o_ref[...] = y.astype(BF16)

    in_specs = [pl.BlockSpec((tm, LANES), lambda c, i: (i, col_off + c)), pl.BlockSpec((None, 1, LANES), lambda c, i: (c, 0, 0))]
    args = [proj, gains]
    if with_rope:
        tab = pl.BlockSpec((tm, LANES), lambda c, i: (i, 0))
        in_specs += [tab, tab]
        args += list(rope)
    return pl.pallas_call(
        body, out_shape=jax.ShapeDtypeStruct((T, ncb * width), BF16), grid=(ncb, T // tm),
        in_specs=in_specs, out_specs=pl.BlockSpec((tm, width), lambda c, i: (i, c)),
        name=name, compiler_params=_params(2))(*args)


def _headnorm_bwd(name, dy, proj, col_off, ncb, gains, group, tm, scale, rope=None, fold=False, norm=True):
    T = dy.shape[0]
    with_rope = rope is not None
    n_groups = ncb // group
    dy_width = 4 * LANES if fold else LANES

    def body(*refs):
        refs = list(refs)
        dy_ref = refs.pop(0)
        x_ref = refs.pop(0) if norm else None
        g_ref = refs.pop(0) if norm else None
        cos_ref = refs.pop(0) if with_rope else None
        sin_ref = refs.pop(0) if with_rope else None
        dx_ref = refs.pop(0)
        dg_ref = refs.pop(0) if norm else None
        c = pl.program_id(0)
        i = pl.program_id(1)
        d = dy_ref[...]
        lane = lax.broadcasted_iota(jnp.int32, (d.shape[0], LANES), 1)
        lo = lane < HEAD_DIM
        if fold:
            t0 = d[:, 0:LANES] + d[:, LANES:2 * LANES]
            t1 = d[:, 2 * LANES:3 * LANES] + d[:, 3 * LANES:4 * LANES]
            d = jnp.where(lo, t0 + pltpu.roll(t0, HEAD_DIM, 1), t1 + pltpu.roll(t1, HEAD_DIM, 1))
        d = d * scale
        if with_rope:
            d = d * cos_ref[...] + _rot_half(d * sin_ref[...], lane)
        if not norm:
            dx_ref[...] = d.astype(BF16)
            return
        xv = x_ref[...]
        gv = g_ref[...]
        r = _head_rstd(xv, lo)
        xh = xv * r
        dxh = d * gv
        pr = dxh * xh
        m_a = jnp.sum(jnp.where(lo, pr, 0.0), axis=-1, keepdims=True)
        m_b = jnp.sum(jnp.where(lo, 0.0, pr), axis=-1, keepdims=True)
        mean = jnp.where(lo, m_a, m_b) * (1.0 / HEAD_DIM)
        dx_ref[...] = (r * (dxh - xh * mean)).astype(BF16)
        dgp = jnp.sum(d * xh, axis=0, keepdims=True)
        dgp = dgp + pltpu.roll(dgp, HEAD_DIM, 1)
        first = jnp.logical_and(c % group == 0, i == 0)

        @pl.when(first)
        def _():
            dg_ref[...] = dgp

        @pl.when(jnp.logical_not(first))
        def _():
            dg_ref[...] += dgp

    in_specs = [pl.BlockSpec((tm, dy_width), lambda c, i: (i, c))]
    args = [dy]
    if norm:
        in_specs += [pl.BlockSpec((tm, LANES), lambda c, i: (i, col_off + c)), pl.BlockSpec((None, 1, LANES), lambda c, i: (c, 0, 0))]
        args += [proj, gains]
    if with_rope:
        tab = pl.BlockSpec((tm, LANES), lambda c, i: (i, 0))
        in_specs += [tab, tab]
        args += list(rope)
    out_shape = [jax.ShapeDtypeStruct((T, ncb * LANES), BF16)]
    out_specs = [pl.BlockSpec((tm, LANES), lambda c, i: (i, c))]
    if norm:
        out_shape.append(jax.ShapeDtypeStruct((n_groups, 1, LANES), F32))
        out_specs.append(pl.BlockSpec((None, 1, LANES), lambda c, i: (c // group, 0, 0)))
    res = pl.pallas_call(
        body, out_shape=tuple(out_shape), grid=(ncb, T // tm), in_specs=in_specs, out_specs=tuple(out_specs),
        name=name, compiler_params=_params(2))(*args)
    return res if norm else (res[0], None)


def _dot_exact(x, tri):
    hi = x.astype(BF16)
    r1 = x - hi.astype(F32)
    mid = r1.astype(BF16)
    lo = (r1 - mid.astype(F32)).astype(BF16)
    return _dot(hi, tri, NN) + _dot(mid, tri, NN) + _dot(lo, tri, NN)


def _forget_fwd(name, zt, bias):
    H, T = zt.shape
    blk = min(256, T)

    def body(z_ref, b_ref, c_ref, s_ref):
        z = z_ref[...] + b_ref[...]
        s_ref[...] = jax.nn.sigmoid(-z)
        lf = jnp.minimum(z, 0.0) - jnp.log(1.0 + jnp.exp(-jnp.abs(z)))
        tri = (lax.broadcasted_iota(jnp.int32, (blk, blk), 0) <= lax.broadcasted_iota(jnp.int32, (blk, blk), 1)).astype(BF16)
        carry = jnp.zeros((H, 1), F32)
        for bi in range(T // blk):
            xb = lf[:, bi * blk:(bi + 1) * blk]
            c_ref[:, bi * blk:(bi + 1) * blk] = _dot_exact(xb, tri) + carry
            carry = carry + jnp.sum(xb, axis=-1, keepdims=True)

    shape = jax.ShapeDtypeStruct((H, T), F32)
    full = pl.BlockSpec((H, T), lambda i: (0, 0))
    return pl.pallas_call(
        body, out_shape=(shape, shape), grid=(1,), in_specs=[full, pl.BlockSpec((H, 1), lambda i: (0, 0))],
        out_specs=(full, full), name=name, compiler_params=_params(1))(zt, bias)


def _forget_bwd(name, dct, drt, sgt):
    H, T = dct.shape
    blk = min(256, T)

    def body(dc_ref, dr_ref, s_ref, dz_ref, db_ref):
        dc = dc_ref[...] + dr_ref[...]
        tri = (lax.broadcasted_iota(jnp.int32, (blk, blk), 0) >= lax.broadcasted_iota(jnp.int32, (blk, blk), 1)).astype(BF16)
        carry = jnp.zeros((H, 1), F32)
        db = jnp.zeros((H, 1), F32)
        for bi in reversed(range(T // blk)):
            xb = dc[:, bi * blk:(bi + 1) * blk]
            dz = (_dot_exact(xb, tri) + carry) * s_ref[:, bi * blk:(bi + 1) * blk]
            dz_ref[:, bi * blk:(bi + 1) * blk] = dz
            db = db + jnp.sum(dz, axis=-1, keepdims=True)
            carry = carry + jnp.sum(xb, axis=-1, keepdims=True)
        db_ref[...] = db

    full = pl.BlockSpec((H, T), lambda i: (0, 0))
    return pl.pallas_call(
        body, out_shape=(jax.ShapeDtypeStruct((H, T), F32), jax.ShapeDtypeStruct((H, 1), F32)), grid=(1,),
        in_specs=[full, full, full], out_specs=(full, pl.BlockSpec((H, 1), lambda i: (0, 0))),
        name=name, compiler_params=_params(1))(dct, drt, sgt)


STRIP = 256


def _fox_fwd(name, qk, v, crow, tq, tk, carry=None):
    T, Dh = v.shape
    HP = Dh // LANES
    nk = T // tk
    assert tk % tq == 0 and tq % STRIP == 0
    n_strips = tq // STRIP

    def body(q_ref, k_ref, v_ref, ra_ref, rb_ref, o_ref, la_ref, lb_ref, s_ref, p_ref, m_ref, l_ref, acc_ref):
        i = pl.program_id(1)
        q2 = q_ref[...]
        lo = _lane_lo((tq, LANES))
        qms = (_keep(lo, q2), _keep(jnp.logical_not(lo), q2))
        r_refs = (ra_ref, rb_ref)
        m_ref[...] = jnp.full(m_ref.shape, NEG, F32)
        l_ref[...] = jnp.zeros(l_ref.shape, F32)
        acc_ref[...] = jnp.zeros(acc_ref.shape, F32)
        rel = lax.broadcasted_iota(jnp.int32, (STRIP, tk), 0) - lax.broadcasted_iota(jnp.int32, (STRIP, tk), 1)

        def chunk(kc, masked):
            start = pl.multiple_of(kc * tk, tk)
            kb = k_ref[pl.ds(start, tk), :]
            vb = v_ref[pl.ds(start, tk), :]
            for h in range(2):
                s_ref[h] = _dot(qms[h], kb, NT)
            for h in range(2):
                cs = r_refs[h][kc]
                for st in range(n_strips):
                    rows = pl.ds(st * STRIP, STRIP)
                    s = s_ref[h, rows, :] - cs
                    if masked:
                        s = jnp.where(rel >= start - (i * tq + st * STRIP), s, NEG)
                    m_old = m_ref[h, rows, :]
                    mn = jnp.maximum(m_old, jnp.max(s, axis=-1, keepdims=True))
                    p = jnp.exp(s - mn)
                    alpha = jnp.exp(m_old - mn)
                    l_ref[h, rows, :] = alpha * l_ref[h, rows, :] + jnp.sum(p, axis=-1, keepdims=True)
                    m_ref[h, rows, :] = mn
                    p_ref[h, rows, :] = p.astype(BF16)
                    acc_ref[h, rows, :] = acc_ref[h, rows, :] * alpha
            for h in range(2):
                acc_ref[h] += _dot(p_ref[h], vb, NN)

        n_full = (i * tq) // tk

        def full_chunk(kc, _):
            chunk(kc, False)
            return 0

        lax.fori_loop(0, n_full, full_chunk, 0)
        chunk(n_full, True)
        o_ref[...] = jnp.where(lo, acc_ref[0] / l_ref[0], acc_ref[1] / l_ref[1])
        la_ref[...] = m_ref[0] + jnp.log(l_ref[0])
        lb_ref[...] = m_ref[1] + jnp.log(l_ref[1])

    row = lambda off: pl.BlockSpec((None, nk, 1, tk), lambda h, i: (2 * h + off, 0, 0, 0))
    lse = jax.ShapeDtypeStruct((HP, T, 1), F32)
    lspec = pl.BlockSpec((None, tq, 1), lambda h, i: (h, i, 0))
    scratch = [pltpu.VMEM((2, tq, tk), F32), pltpu.VMEM((2, tq, tk), BF16), pltpu.VMEM((2, tq, 1), F32),
               pltpu.VMEM((2, tq, 1), F32), pltpu.VMEM((2, tq, LANES), F32)]
    return _call(
        body, name=name, grid=(HP, T // tq), out_shape=(jax.ShapeDtypeStruct((T, Dh), F32), lse, lse),
        in_specs=[pl.BlockSpec((tq, LANES), lambda h, i: (i, h)), pl.BlockSpec((T, LANES), lambda h, i: (0, HP + h)),
                  pl.BlockSpec((T, LANES), lambda h, i: (0, h)), row(0), row(1)],
        out_specs=(pl.BlockSpec((tq, LANES), lambda h, i: (i, h)), lspec, lspec),
        args=[qk, qk, v, crow, crow], scratch_shapes=scratch, carry=carry)


def _fox_bwd(name, qk, v, o, do, crow, lse_a, lse_b, tq, tk, carry=None):
    T, Dh = v.shape
    HP = Dh // LANES
    nk = T // tk
    scale = HEAD_DIM ** -0.5
    assert tk % tq == 0 and tq % STRIP == 0
    n_strips = tq // STRIP

    def body(q_ref, k_ref, v_ref, o_ref, do_ref, ra_ref, rb_ref, la_ref, lb_ref,
             dq_ref, dk_ref, dv_ref, dca_ref, dcb_ref, dra_ref, drb_ref, s_ref, dp_ref, p_ref, ds_ref, dq_acc, dsum_ref):
        i = pl.program_id(1)

        @pl.when(i == 0)
        def _():
            dk_ref[...] = jnp.zeros_like(dk_ref)
            dv_ref[...] = jnp.zeros_like(dv_ref)
            dca_ref[...] = jnp.zeros_like(dca_ref)
            dcb_ref[...] = jnp.zeros_like(dcb_ref)

        q2 = q_ref[...]
        do2 = do_ref[...]
        lo = _lane_lo((tq, LANES))
        hi = jnp.logical_not(lo)
        qms = (_keep(lo, q2), _keep(hi, q2))
        doms = (_keep(lo, do2), _keep(hi, do2))
        prod = do2.astype(F32) * o_ref[...]
        dsum_ref[0] = jnp.sum(jnp.where(lo, prod, 0.0), axis=-1, keepdims=True)
        dsum_ref[1] = jnp.sum(jnp.where(lo, 0.0, prod), axis=-1, keepdims=True)
        r_refs, l_refs, dc_refs, dr_refs = (ra_ref, rb_ref), (la_ref, lb_ref), (dca_ref, dcb_ref), (dra_ref, drb_ref)
        dq_acc[...] = jnp.zeros(dq_acc.shape, F32)
        dra_ref[...] = jnp.zeros(dra_ref.shape, F32)
        drb_ref[...] = jnp.zeros(drb_ref.shape, F32)
        rel = lax.broadcasted_iota(jnp.int32, (STRIP, tk), 0) - lax.broadcasted_iota(jnp.int32, (STRIP, tk), 1)

        def chunk(kc, masked):
            start = pl.multiple_of(kc * tk, tk)
            kb = k_ref[pl.ds(start, tk), :]
            vb = v_ref[pl.ds(start, tk), :]
            for h in range(2):
                s_ref[h] = _dot(qms[h], kb, NT)
                dp_ref[h] = _dot(doms[h], vb, NT)
            for h in range(2):
                cs = r_refs[h][kc]
                col_sum = jnp.zeros((1, tk), F32)
                for st in range(n_strips):
                    rows = pl.ds(st * STRIP, STRIP)
                    s = s_ref[h, rows, :] - cs
                    if masked:
                        s = jnp.where(rel >= start - (i * tq + st * STRIP), s, NEG)
                    p = jnp.exp(s - l_refs[h][rows, :])
                    ds = p * (dp_ref[h, rows, :] - dsum_ref[h, rows, :])
                    p_ref[h, rows, :] = p.astype(BF16)
                    ds_ref[h, rows, :] = ds.astype(BF16)
                    col_sum = col_sum + jnp.sum(ds, axis=0, keepdims=True)
                    dr_refs[h][rows, :] += jnp.sum(ds, axis=-1, keepdims=True)
                dc_refs[h][kc] = dc_refs[h][kc] - col_sum
            dk = _dot(ds_ref[0], qms[0], TN) + _dot(ds_ref[1], qms[1], TN)
            dv = _dot(p_ref[0], doms[0], TN) + _dot(p_ref[1], doms[1], TN)
            dk_ref[pl.ds(start, tk), :] += dk
            dv_ref[pl.ds(start, tk), :] += dv
            for h in range(2):
                dq_acc[h] += _dot(ds_ref[h], kb, NN)

        n_full = (i * tq) // tk

        def full_chunk(kc, _):
            chunk(kc, False)
            return 0

        lax.fori_loop(0, n_full, full_chunk, 0)
        chunk(n_full, True)
        dq_ref[...] = jnp.where(lo, dq_acc[0], dq_acc[1]) * scale

    row = lambda off: pl.BlockSpec((None, nk, 1, tk), lambda h, i: (2 * h + off, 0, 0, 0))
    lspec = pl.BlockSpec((None, tq, 1), lambda h, i: (h, i, 0))
    qspec = pl.BlockSpec((tq, LANES), lambda h, i: (i, h))
    full = pl.BlockSpec((T, LANES), lambda h, i: (0, h))
    dcspec = pl.BlockSpec((None, nk, 1, tk), lambda h, i: (h, 0, 0, 0))
    grad = jax.ShapeDtypeStruct((T, Dh), F32)
    dc = jax.ShapeDtypeStruct((HP, nk, 1, tk), F32)
    dr = jax.ShapeDtypeStruct((HP, T, 1), F32)
    scratch = [pltpu.VMEM((2, tq, tk), F32), pltpu.VMEM((2, tq, tk), F32), pltpu.VMEM((2, tq, tk), BF16), pltpu.VMEM((2, tq, tk), BF16),
               pltpu.VMEM((2, tq, LANES), F32), pltpu.VMEM((2, tq, 1), F32)]
    return _call(
        body, name=name, grid=(HP, T // tq), out_shape=(grad, grad, grad, dc, dc, dr, dr),
        in_specs=[qspec, pl.BlockSpec((T, LANES), lambda h, i: (0, HP + h)), full, qspec, qspec, row(0), row(1), lspec, lspec],
        out_specs=(qspec, full, full, dcspec, dcspec, lspec, lspec),
        args=[qk, qk, v, o, do, crow, crow, lse_a, lse_b], scratch_shapes=scratch, carry=carry)


SWA_GROUP = 2
SWA_GROUP_BWD = 4


def _swa_block(n, q_ref, k_ref):
    qs = pl.multiple_of(n * WINDOW, WINDOW)
    ks = pl.multiple_of(jnp.maximum(n - 1, 0) * WINDOW, WINDOW)
    rel = (qs + lax.broadcasted_iota(jnp.int32, (WINDOW, 2 * WINDOW), 0)) - (ks + lax.broadcasted_iota(jnp.int32, (WINDOW, 2 * WINDOW), 1))
    valid = jnp.logical_and(rel >= 0, rel < WINDOW)
    return qs, ks, valid


def _swa_fwd(name, q, kd, vd, sinks, carry=None):
    T, Dh = q.shape
    HP = Dh // LANES

    def body(q_ref, k_ref, v_ref, sa_ref, sb_ref, o_ref, la_ref, lb_ref):
        lo = _lane_lo((WINDOW, LANES))

        def block(n, _):
            qs, ks, valid = _swa_block(n, q_ref, k_ref)
            q2 = q_ref[pl.ds(qs, WINDOW), :]
            kb = k_ref[pl.ds(ks, 2 * WINDOW), :]
            vb = v_ref[pl.ds(ks, 2 * WINDOW), :]
            res = []
            for sel, s_ref in ((lo, sa_ref), (jnp.logical_not(lo), sb_ref)):
                qm = _keep(sel, q2)
                sink = s_ref[...]
                s = jnp.where(valid, _dot(qm, kb, NT), NEG)
                m = jnp.maximum(jnp.max(s, axis=-1, keepdims=True), sink)
                p = jnp.exp(s - m)
                l = jnp.sum(p, axis=-1, keepdims=True) + jnp.exp(sink - m)
                res.append((_dot(p.astype(BF16), vb, NN) / l, m + jnp.log(l)))
            o_ref[pl.ds(qs, WINDOW), :] = jnp.where(lo, res[0][0], res[1][0])
            la_ref[pl.ds(qs, WINDOW), :] = res[0][1]
            lb_ref[pl.ds(qs, WINDOW), :] = res[1][1]
            return 0

        assert (T // WINDOW) % SWA_GROUP == 0

        def group(g, c):
            for b in range(SWA_GROUP):
                c = block(g * SWA_GROUP + b, c)
            return c

        lax.fori_loop(0, T // WINDOW // SWA_GROUP, group, 0)

    full = pl.BlockSpec((T, LANES), lambda h: (0, h))
    kv = pl.BlockSpec((T, LANES), lambda h: (0, h // 2))
    sink = lambda off: pl.BlockSpec((None, 1, 1), lambda h: (2 * h + off, 0, 0))
    lse = jax.ShapeDtypeStruct((HP, T, 1), F32)
    lspec = pl.BlockSpec((None, T, 1), lambda h: (h, 0, 0))
    return _call(
        body, name=name, grid=(HP,), out_shape=(jax.ShapeDtypeStruct((T, Dh), F32), lse, lse),
        in_specs=[full, kv, kv, sink(0), sink(1)], out_specs=(full, lspec, lspec),
        args=[q, kd, vd, sinks, sinks], carry=carry)


def _swa_bwd(name, q, kd, vd, sinks, o, do, lse_a, lse_b, carry=None):
    T, Dh = q.shape
    HP = Dh // LANES
    scale = HEAD_DIM ** -0.5

    def body(q_ref, k_ref, v_ref, sa_ref, sb_ref, o_ref, do_ref, la_ref, lb_ref, dq_ref, dk_ref, dv_ref, dsa_ref, dsb_ref):
        lo = _lane_lo((WINDOW, LANES))
        hi = jnp.logical_not(lo)
        dk_ref[...] = jnp.zeros_like(dk_ref)
        dv_ref[...] = jnp.zeros_like(dv_ref)

        def block(n, dsinks):
            qs, ks, valid = _swa_block(n, q_ref, k_ref)
            q2 = q_ref[pl.ds(qs, WINDOW), :]
            do2 = do_ref[pl.ds(qs, WINDOW), :]
            kb = k_ref[pl.ds(ks, 2 * WINDOW), :]
            vb = v_ref[pl.ds(ks, 2 * WINDOW), :]
            prod = do2.astype(F32) * o_ref[pl.ds(qs, WINDOW), :]
            dqs, new = [], []
            dk = jnp.zeros((2 * WINDOW, LANES), F32)
            dv = jnp.zeros((2 * WINDOW, LANES), F32)
            for sel, s_ref, l_ref, dsink in ((lo, sa_ref, la_ref, dsinks[0]), (hi, sb_ref, lb_ref, dsinks[1])):
                qm = _keep(sel, q2)
                dom = _keep(sel, do2)
                dsum = jnp.sum(jnp.where(sel, prod, 0.0), axis=-1, keepdims=True)
                lse = l_ref[pl.ds(qs, WINDOW), :]
                s = jnp.where(valid, _dot(qm, kb, NT), NEG)
                p = jnp.exp(s - lse)
                ds = p * (_dot(dom, vb, NT) - dsum)
                dsb = ds.astype(BF16)
                dqs.append(_dot(dsb, kb, NN))
                dk = dk + _dot(dsb, qm, TN)
                dv = dv + _dot(p.astype(BF16), dom, TN)
                new.append(dsink - jnp.sum(jnp.exp(s_ref[...] - lse) * dsum, axis=0, keepdims=True))
            dq_ref[pl.ds(qs, WINDOW), :] = jnp.where(lo, dqs[0], dqs[1]) * scale
            dk_ref[pl.ds(ks, 2 * WINDOW), :] += dk
            dv_ref[pl.ds(ks, 2 * WINDOW), :] += dv
            return tuple(new)

        assert (T // WINDOW) % SWA_GROUP_BWD == 0

        def group(g, c):
            for b in range(SWA_GROUP_BWD):
                c = block(g * SWA_GROUP_BWD + b, c)
            return c

        dsa, dsb_ = lax.fori_loop(0, T // WINDOW // SWA_GROUP_BWD, group, (jnp.zeros((1, 1), F32), jnp.zeros((1, 1), F32)))
        dsa_ref[...] = dsa
        dsb_ref[...] = dsb_

    full = pl.BlockSpec((T, LANES), lambda h: (0, h))
    kv = pl.BlockSpec((T, LANES), lambda h: (0, h // 2))
    sink = lambda off: pl.BlockSpec((None, 1, 1), lambda h: (2 * h + off, 0, 0))
    lspec = pl.BlockSpec((None, T, 1), lambda h: (h, 0, 0))
    dsink = pl.BlockSpec((None, 1, 1), lambda h: (h, 0, 0))
    grad = jax.ShapeDtypeStruct((T, Dh), F32)
    ds_shape = jax.ShapeDtypeStruct((HP, 1, 1), F32)
    return _call(
        body, name=name, grid=(HP,), out_shape=(grad, grad, grad, ds_shape, ds_shape),
        in_specs=[full, kv, kv, sink(0), sink(1), full, full, lspec, lspec],
        out_specs=(full, full, full, dsink, dsink),
        args=[q, kd, vd, sinks, sinks, o, do, lse_a, lse_b], carry=carry)


def _place():
    return lax.axis_index("x"), lax.axis_index("y"), lax.axis_index("c")


def _run_carry(name, carry):
    c_in, c_out = len(carry.inputs), len(carry.out_shapes)

    def body(*refs):
        ins, outs, scr = refs[:c_in], refs[c_in:c_in + c_out], refs[c_in + c_out:]
        carry.start(ins, outs, scr)
        carry.middle(ins, outs, scr)
        carry.finish(ins, outs, scr)

    return pl.pallas_call(
        body, out_shape=tuple(carry.out_shapes), in_specs=[_HBM] * c_in, out_specs=tuple([_HBM] * c_out),
        scratch_shapes=carry.scratch, name=name)(*carry.inputs)


def _gather_carry(shards):
    n = len(shards)

    def plan(ins, outs, scr):
        send, recv, local = scr
        x, y, c = _place()
        me, sibling = (x, y, c), (x, y, 1 - c)
        partner, other, diag = (x ^ c, y ^ (1 - c)), (x ^ (1 - c), y ^ c), (1 - x, 1 - y)

        def copy(w, k, block, to, src=None):
            slot = 4 * block[0] + 2 * block[1] + block[2]
            return pltpu.make_async_remote_copy(
                src_ref=outs[w].at[slot] if src is None else src, dst_ref=outs[w].at[slot],
                send_sem=send.at[w, k], recv_sem=recv.at[w, k], device_id=to, device_id_type=MESH)

        own = [pltpu.make_async_copy(ins[w], outs[w].at[4 * x + 2 * y + c], local.at[w]) for w in range(n)]
        return copy, own, me, sibling, partner, other, diag, c

    def start(ins, outs, scr):
        copy, own, me, sibling, partner, other, _, c = plan(ins, outs, scr)
        for cp in own:
            cp.start()
        for w in range(n):
            copy(w, 1, me, (*partner, c), src=ins[w]).start()
            copy(w, 2, me, (*other, c), src=ins[w]).start()
            copy(w, 0, me, sibling, src=ins[w]).start()

    def middle(ins, outs, scr):
        copy, _, me, sibling, partner, other, _, c = plan(ins, outs, scr)
        for w in range(n):
            copy(w, 1, (*partner, c), me).wait_recv()
            copy(w, 3, (*partner, c), (*other, c)).start()
            copy(w, 4, (*partner, c), sibling).start()
        for w in range(n):
            copy(w, 2, (*other, c), me).wait_recv()
            copy(w, 5, (*other, c), sibling).start()

    def finish(ins, outs, scr):
        copy, own, me, sibling, partner, other, diag, c = plan(ins, outs, scr)
        for w in range(n):
            copy(w, 3, (*diag, c), me).wait_recv()
            copy(w, 6, (*diag, c), sibling).start()
        for w in range(n):
            copy(w, 0, sibling, me).wait_recv()
            copy(w, 4, (*other, 1 - c), me).wait_recv()
            copy(w, 5, (*partner, 1 - c), me).wait_recv()
            copy(w, 6, (*diag, 1 - c), me).wait_recv()
        for w in range(n):
            sent = [copy(w, 0, me, sibling, src=ins[w]), copy(w, 1, me, (*partner, c), src=ins[w]), copy(w, 2, me, (*other, c), src=ins[w]),
                    copy(w, 3, (*partner, c), (*other, c)), copy(w, 4, (*partner, c), sibling), copy(w, 5, (*other, c), sibling),
                    copy(w, 6, (*diag, c), sibling)]
            for cp in sent:
                cp.wait_send()
        for cp in own:
            cp.wait()

    return _Carry(shards, [jax.ShapeDtypeStruct((N_DEV,) + s.shape, s.dtype) for s in shards],
                  [pltpu.SemaphoreType.DMA((n, 7)), pltpu.SemaphoreType.DMA((n, 7)), pltpu.SemaphoreType.DMA((n,))], start, finish, middle)


def _sibling_carry(grads):
    n = len(grads)

    def copies(ins, outs, scr):
        send, recv = scr
        x, y, c = _place()
        return [pltpu.make_async_remote_copy(
            src_ref=ins[w].at[2 * q + (1 - c)], dst_ref=outs[w].at[q], send_sem=send.at[w, q], recv_sem=recv.at[w, q],
            device_id=(x, y, 1 - c), device_id_type=MESH) for w in range(n) for q in range(4)]

    def start(ins, outs, scr):
        for cp in copies(ins, outs, scr):
            cp.start()

    def finish(ins, outs, scr):
        for cp in copies(ins, outs, scr):
            cp.wait()

    return _Carry(grads, [jax.ShapeDtypeStruct((4,) + g.shape[1:], g.dtype) for g in grads],
                  [pltpu.SemaphoreType.DMA((n, 4)), pltpu.SemaphoreType.DMA((n, 4))], start, finish)


def _to_partner_carry(sums):
    n = len(sums)

    def copies(ins, outs, scr):
        send, recv = scr
        x, y, c = _place()
        partner, diag = (x ^ c, y ^ (1 - c)), (1 - x, 1 - y)
        cps = []
        for w in range(n):
            for k, chip in enumerate((partner, diag)):
                cps.append(pltpu.make_async_remote_copy(
                    src_ref=ins[w].at[2 * chip[0] + chip[1]], dst_ref=outs[2 * w + k], send_sem=send.at[w, k], recv_sem=recv.at[w, k],
                    device_id=(*partner, c), device_id_type=MESH))
        return cps

    def start(ins, outs, scr):
        for cp in copies(ins, outs, scr):
            cp.start()

    def finish(ins, outs, scr):
        for cp in copies(ins, outs, scr):
            cp.wait()

    return _Carry(sums, [jax.ShapeDtypeStruct(s.shape[1:], s.dtype) for s in sums for _ in range(2)],
                  [pltpu.SemaphoreType.DMA((n, 2)), pltpu.SemaphoreType.DMA((n, 2))], start, finish)


def _to_other_carry(blocks):
    n = len(blocks)

    def copies(ins, outs, scr):
        send, recv = scr
        x, y, c = _place()
        return [pltpu.make_async_remote_copy(
            src_ref=ins[w], dst_ref=outs[w], send_sem=send.at[w], recv_sem=recv.at[w],
            device_id=(x ^ (1 - c), y ^ c, c), device_id_type=MESH) for w in range(n)]

    def start(ins, outs, scr):
        for cp in copies(ins, outs, scr):
            cp.start()

    def finish(ins, outs, scr):
        for cp in copies(ins, outs, scr):
            cp.wait()

    return _Carry(blocks, [jax.ShapeDtypeStruct(b.shape, b.dtype) for b in blocks],
                  [pltpu.SemaphoreType.DMA((n,)), pltpu.SemaphoreType.DMA((n,))], start, finish)


def _gather_small(packed):
    R, C = packed.shape

    def body(in_ref, out_ref, send, recv):
        x, y, c = _place()
        mine = 4 * x + 2 * y + c
        out_ref[mine] = in_ref[...]
        copies = []
        for k in range(1, N_DEV):
            peer = (x ^ (k >> 2), y ^ ((k >> 1) & 1), c ^ (k & 1))
            copies.append(pltpu.make_async_remote_copy(
                src_ref=in_ref, dst_ref=out_ref.at[mine], send_sem=send.at[k - 1], recv_sem=recv.at[k - 1],
                device_id=peer, device_id_type=MESH))
        for cp in copies:
            cp.start()
        for cp in copies:
            cp.wait()

    vmem = pl.BlockSpec(memory_space=pltpu.VMEM)
    return pl.pallas_call(
        body, out_shape=jax.ShapeDtypeStruct((N_DEV, R, C), F32), in_specs=[vmem], out_specs=vmem,
        scratch_shapes=[pltpu.SemaphoreType.DMA((N_DEV - 1,)), pltpu.SemaphoreType.DMA((N_DEV - 1,))],
        name="small_grads_all_gather")(packed)


def _adamw(w, g, m, v):
    m = ADAM_B1 * m + (1.0 - ADAM_B1) * g
    v = ADAM_B2 * v + (1.0 - ADAM_B2) * (g * g)
    m_hat = m / (1.0 - ADAM_B1 ** ADAM_STEP)
    v_hat = v / (1.0 - ADAM_B2 ** ADAM_STEP)
    delta = -ADAM_LR * (m_hat / (jnp.sqrt(v_hat) + ADAM_EPS) + ADAM_WD * w)
    return delta, m, v


def _pair_add(name, grads, received, c_idx):
    _, R, C = grads.shape
    tr = _row_tile(R)

    def body(c_ref, g_ref, r_ref, o_ref):
        o_ref[...] = (g_ref[...].astype(F32) + r_ref[...].astype(F32)).astype(BF16)

    blk = pl.BlockSpec((None, tr, C), lambda q, i, c: (q, i, 0))
    return pl.pallas_call(
        body, out_shape=jax.ShapeDtypeStruct((4, R, C), BF16),
        grid_spec=pltpu.PrefetchScalarGridSpec(
            num_scalar_prefetch=1, grid=(4, R // tr),
            in_specs=[pl.BlockSpec((None, tr, C), lambda q, i, c: (2 * q + c[0], i, 0)), blk], out_specs=blk),
        name=name, compiler_params=_params(2))(c_idx, grads, received)


def _relay_add(name, sums, relayed, other_idx):
    _, R, C = sums.shape
    tr = _row_tile(R)

    def body(q_ref, s_ref, r_ref, o_ref):
        o_ref[...] = (s_ref[...].astype(F32) + r_ref[...].astype(F32)).astype(BF16)

    blk = pl.BlockSpec((tr, C), lambda i, q: (i, 0))
    return pl.pallas_call(
        body, out_shape=jax.ShapeDtypeStruct((R, C), BF16),
        grid_spec=pltpu.PrefetchScalarGridSpec(
            num_scalar_prefetch=1, grid=(R // tr,),
            in_specs=[pl.BlockSpec((None, tr, C), lambda i, q: (q[0], i, 0)), blk], out_specs=blk),
        name=name, compiler_params=_params(1))(other_idx, sums, relayed)


def _adam_shard(name, sums, received, w, m, v, chip_idx):
    R, C = w.shape
    tr = _row_tile(R, 128)
    tc = C if tr < R or C % (2 * LANES) else 2 * LANES

    def body(q_ref, s_ref, ra_ref, rb_ref, w_ref, m_ref, v_ref, g_out, d_out, m_out, v_out):
        g = s_ref[...].astype(F32) + ra_ref[...].astype(F32) + rb_ref[...].astype(F32)
        delta, mn, vn = _adamw(w_ref[...], g, m_ref[...], v_ref[...])
        g_out[...] = g
        d_out[...] = delta
        m_out[...] = mn
        v_out[...] = vn

    blk = pl.BlockSpec((tr, tc), lambda i, j, q: (i, j))
    shape = jax.ShapeDtypeStruct((R, C), F32)
    return pl.pallas_call(
        body, out_shape=(shape,) * 4,
        grid_spec=pltpu.PrefetchScalarGridSpec(
            num_scalar_prefetch=1, grid=(R // tr, C // tc),
            in_specs=[pl.BlockSpec((None, tr, tc), lambda i, j, q: (q[0], i, j)), blk, blk, blk, blk, blk],
            out_specs=(blk,) * 4),
        name=name, compiler_params=_params(2))(chip_idx, sums, received[0], received[1], w, m, v)


def _adam_small(name, gathered, w, m, v):
    R, C = w.shape

    def body(ga_ref, w_ref, m_ref, v_ref, g_out, d_out, m_out, v_out):
        g = ga_ref[0]
        for d in range(1, N_DEV):
            g = g + ga_ref[d]
        delta, mn, vn = _adamw(w_ref[...], g, m_ref[...], v_ref[...])
        g_out[...] = g
        d_out[...] = delta
        m_out[...] = mn
        v_out[...] = vn

    full = pl.BlockSpec((R, C), lambda i: (0, 0))
    shape = jax.ShapeDtypeStruct((R, C), F32)
    return pl.pallas_call(
        body, out_shape=(shape,) * 4, grid=(1,),
        in_specs=[pl.BlockSpec((N_DEV, R, C), lambda i: (0, 0, 0)), full, full, full], out_specs=(full,) * 4,
        name=name, compiler_params=_params(1))(gathered, w, m, v)


def _pack_small(parts, D):
    g1, gmix, g2, gof, gos, bf, gqf, gkf, gqs, gks, sinks = [p.reshape(-1).astype(F32) for p in parts]
    row3 = jnp.concatenate([gof, gos])
    row4 = jnp.zeros((D,), F32)
    for slot, vec in enumerate((bf, gqf, gkf, gqs, gks, sinks)):
        row4 = lax.dynamic_update_slice(row4, vec, (slot * LANES,))
    zero = jnp.zeros((D,), F32)
    return jnp.stack([g1, gmix, g2, row3, row4, zero, zero, zero])


def _unpack_small(packed, D, H):
    Dh = D // 2
    row4 = packed[4]
    short = [row4[s * LANES:s * LANES + n] for s, n in enumerate((H, HEAD_DIM, HEAD_DIM, HEAD_DIM, HEAD_DIM, H))]
    vecs = [packed[0], packed[1], packed[2], packed[3, :Dh], packed[3, Dh:]] + short
    return [v[None, :] for v in vecs]


def kernel(x, positions, norm_ffn1_g, ffn1_w_gate, ffn1_w_up, ffn1_w_down, norm_mix_g, w_in, b_forget, fox_q_norm_g, fox_k_norm_g, swa_q_norm_g, swa_k_norm_g, swa_sinks, out_norm_fox_g, out_norm_swa_g, w_out, norm_ffn2_g, ffn2_w_gate, ffn2_w_up, ffn2_w_down, loss_target, m_norm_ffn1_g, m_ffn1_w_gate, m_ffn1_w_up, m_ffn1_w_down, m_norm_mix_g, m_w_in, m_b_forget, m_fox_q_norm_g, m_fox_k_norm_g, m_swa_q_norm_g, m_swa_k_norm_g, m_swa_sinks, m_out_norm_fox_g, m_out_norm_swa_g, m_w_out, m_norm_ffn2_g, m_ffn2_w_gate, m_ffn2_w_up, m_ffn2_w_down, v_norm_ffn1_g, v_ffn1_w_gate, v_ffn1_w_up, v_ffn1_w_down, v_norm_mix_g, v_w_in, v_b_forget, v_fox_q_norm_g, v_fox_k_norm_g, v_swa_q_norm_g, v_swa_k_norm_g, v_swa_sinks, v_out_norm_fox_g, v_out_norm_swa_g, v_w_out, v_norm_ffn2_g, v_ffn2_w_gate, v_ffn2_w_up, v_ffn2_w_down):
    xs = x[0]
    target = loss_target[0]
    T, D = xs.shape
    Dh = D // 2
    H = Dh // HEAD_DIM
    HP = H // 2
    KVW = (H // GQA_GROUP) * HEAD_DIM
    KVB = KVW // LANES
    MAIN = 4 * Dh + 2 * KVW
    F_OFF = 3 * Dh
    tm = min(ROW_TILE_CAP, T)
    tq = min(512, T)
    tk = min(512, T)
    nk = T // tk
    cx, cy, cc = _place()
    c_idx = jnp.reshape(cc, (1,)).astype(jnp.int32)
    chip_idx = jnp.reshape(2 * cx + cy, (1,)).astype(jnp.int32)
    other_idx = jnp.reshape(2 * (cx ^ (1 - cc)) + (cy ^ cc), (1,)).astype(jnp.int32)

    tr = jnp.transpose
    big_w = [tr(ffn1_w_gate[0]), tr(ffn1_w_up[0]), ffn1_w_down[0], tr(w_in[0]), w_out[0], tr(ffn2_w_gate[0]), tr(ffn2_w_up[0]),
             ffn2_w_down[0]]
    big_m = [tr(m_ffn1_w_gate[0]), tr(m_ffn1_w_up[0]), m_ffn1_w_down[0], tr(m_w_in[0]), m_w_out[0], tr(m_ffn2_w_gate[0]),
             tr(m_ffn2_w_up[0]), m_ffn2_w_down[0]]
    big_v = [tr(v_ffn1_w_gate[0]), tr(v_ffn1_w_up[0]), v_ffn1_w_down[0], tr(v_w_in[0]), v_w_out[0], tr(v_ffn2_w_gate[0]),
             tr(v_ffn2_w_up[0]), v_ffn2_w_down[0]]
    transposed = {"ffn1_w_gate", "ffn1_w_up", "w_in", "ffn2_w_gate", "ffn2_w_up"}
    names = ["ffn1_w_gate", "ffn1_w_up", "ffn1_w_down", "w_in", "w_out", "ffn2_w_gate", "ffn2_w_up", "ffn2_w_down"]
    sh = dict(zip(names, [w.astype(BF16) for w in big_w]))
    lane = jnp.arange(LANES)
    inv_freq = ROPE_THETA ** (-(2.0 * (lane % (HEAD_DIM // 2))).astype(F32) / HEAD_DIM)
    ang = positions[0].astype(F32)[:, None] * inv_freq[None, :]
    cos_t = jnp.cos(ang)
    sin_t = jnp.where((lane & (HEAD_DIM // 2)) == 0, -1.0, 1.0)[None, :] * jnp.sin(ang)
    rope = (cos_t, sin_t)

    def pair_gain(g, blocks):
        return jnp.tile(jnp.concatenate([g[0], g[0]])[None, None, :], (blocks, 1, 1))

    n1, (wg1,) = _rmsnorm_fwd("ffn1_norm", xs, norm_ffn1_g, tm, carry=_gather_carry([sh["ffn1_w_gate"]]))
    a1, (wu1,) = _ffn_gate("ffn1_gate", n1, wg1, tm, carry=_gather_carry([sh["ffn1_w_up"]]))
    (b1, hm1), (wd1,) = _ffn_up_only("ffn1_up", n1, wu1, a1, tm, carry=_gather_carry([sh["ffn1_w_down"]]))
    h1, (win_g,) = _ffn_down("ffn1_down", hm1, wd1, xs, tm, D, carry=_gather_carry([sh["w_in"]]))
    n_in = win_g.shape[1]
    win_t = win_g.reshape(N_DEV * n_in, D)
    win_main = jnp.concatenate([win_t[:F_OFF], win_t[F_OFF + H:]], axis=0)
    win_f = jnp.pad(win_t[F_OFF:F_OFF + H], ((0, LANES - H), (0, 0)))

    u = _rmsnorm_fwd("mix_norm", h1, norm_mix_g, tm)
    proj, (wout_g,) = _mm("mix_proj", u, win_main, tm, MAIN // 9, dims=NT, carry=_gather_carry([sh["w_out"]]))
    wout = wout_g.reshape(D, D)
    proj_f = _mm("mix_proj_forget", u, win_f, tm, LANES, dims=NT)
    scale = HEAD_DIM ** -0.5
    fox_gains = jnp.concatenate([pair_gain(fox_q_norm_g, HP), pair_gain(fox_k_norm_g, HP)])
    qk_f = _headnorm_fwd_scaled("fox_qk_norm", proj, 0, 2 * HP, fox_gains, T, scale, HP)
    v_f = proj[:, 2 * Dh:3 * Dh].astype(BF16)
    c_t, sg_t = _forget_fwd("forget_gates", proj_f[:, :H].T, b_forget.reshape(H, 1))
    crow = c_t.reshape(H, nk, 1, tk)
    (o_fox, lse_fa, lse_fb), (wg2, wu2) = _fox_fwd("fox_attention", qk_f, v_f, crow, tq, tk,
                                                   carry=_gather_carry([sh["ffn2_w_gate"], sh["ffn2_w_up"]]))

    swa_q_gains = pair_gain(swa_q_norm_g, HP)
    swa_k_gains = pair_gain(swa_k_norm_g, KVB)
    q_s = _headnorm_fwd("swa_q_norm", proj, 3 * HP, HP, swa_q_gains, T, scale, rope=rope)
    k_d = _headnorm_fwd("swa_k_norm", proj, 4 * HP, KVB, swa_k_gains, T, 1.0, rope=rope, dup=True)
    v_s = proj[:, 4 * Dh + KVW:].astype(BF16).reshape(T, H // GQA_GROUP, 1, HEAD_DIM)
    v_d = jnp.broadcast_to(v_s, (T, H // GQA_GROUP, 2, HEAD_DIM)).reshape(T, 2 * KVW)
    sinks3 = swa_sinks.reshape(H, 1, 1)
    o_swa, lse_sa, lse_sb = _swa_fwd("swa_attention", q_s, k_d, v_d, sinks3)

    on = _outnorm_fwd("out_norm", o_fox, o_swa, out_norm_fox_g, out_norm_swa_g, tm)
    h2 = _mm("mix_out", on, wout, tm, min(512, D), resid=h1)

    n2 = _rmsnorm_fwd("ffn2_norm", h2, norm_ffn2_g, tm)
    (a2, b2, hm2), (wd2,) = _ffn_up("ffn2_up", n2, wg2, wu2, tm, carry=_gather_carry([sh["ffn2_w_down"]]))
    y = _ffn_down("ffn2_down", hm2, wd2, h2, tm, D)
    dy, dyh, sq = _loss_grad("loss_grad", y, target, min(256, T))
    loss = lax.psum(0.5 * sq[0, 0] / D, ("x", "y", "c"))

    J, Fs, _ = wg2.shape
    aspec = pl.BlockSpec((None, tm, Fs), lambda i, j: (j, i, 0))
    wspec = pl.BlockSpec((None, Fs, D), lambda i, j: (j, 0, 0))
    got = {}

    def pair_sums(keys, grads, received):
        return [_pair_add("sum_" + nm, g, r, c_idx) for nm, g, r in zip(keys, grads, received)]

    def relay_sums(keys, sums, hop1):
        out = []
        for i, (nm, s) in enumerate(zip(keys, sums)):
            got[nm] = [hop1[2 * i]]
            out.append(_relay_add("relay_" + nm, s, hop1[2 * i + 1], other_idx))
        return out

    def arrived(keys, hop2):
        for nm, blk in zip(keys, hop2):
            got[nm].append(blk)

    dwd2 = _wgrad_down("ffn2_wgrad_down", hm2, dyh, min(1024, D))
    (da2, db2), (sib_d2,) = _ffn_bwd_mid("ffn2_bwd_mid", dyh, wd2, a2, b2, tm, carry=_sibling_carry([dwd2]))
    (sum_wd2,) = pair_sums(names[7:8], [dwd2], [sib_d2])
    (dwg2, dwu2), hop1 = _wgrad_up("ffn2_wgrad_up", n2, da2, db2, min(1024, D), carry=_to_partner_carry([sum_wd2]))
    (t_wd2,) = relay_sums(names[7:8], [sum_wd2], hop1)
    dn2, (via_wd2, *sib2) = _reduce_mm("ffn2_bwd_in", [(da2, aspec, wg2, wspec), (db2, aspec, wu2, wspec)], [], NN, T, D, tm, J,
                                       carry=_join(_to_other_carry([t_wd2]), _sibling_carry([dwg2, dwu2])))
    arrived(names[7:8], [via_wd2])
    dh2, dg_ffn2, dh2b = _rmsnorm_bwd("ffn2_norm_bwd", dn2, h2, norm_ffn2_g, dy, min(256, T), 1.0)
    sum_wg2, sum_wu2 = pair_sums(names[5:7], [dwg2, dwu2], sib2)

    dwout = _wgrad_2d("mix_out_wgrad", on, dh2b, min(512, D), min(1024, D))
    dwout_g = dwout.reshape(N_DEV, D // N_DEV, D)
    do_fox, dg_of = _outnorm_bwd("out_norm_bwd_fox", dh2b, wout, 0, o_fox, out_norm_fox_g, tm)
    do_swa, dg_os = _outnorm_bwd("out_norm_bwd_swa", dh2b, wout, 1, o_swa, out_norm_swa_g, tm)

    (dq_f, dk_f, dv_f, dc_a, dc_b, dr_a, dr_b), (*hop1, sib_wout) = _fox_bwd(
        "fox_attention_bwd", qk_f, v_f, o_fox, do_fox, crow, lse_fa, lse_fb, tq, tk,
        carry=_join(_to_partner_carry([sum_wg2, sum_wu2]), _sibling_carry([dwout_g])))
    t_wg2, t_wu2 = relay_sums(names[5:7], [sum_wg2, sum_wu2], hop1)
    (sum_wout,) = pair_sums(names[4:5], [dwout_g], [sib_wout])
    dqk_f = jnp.concatenate([dq_f, dk_f], axis=1)
    dqk_raw, dg_fox = _headnorm_bwd("fox_qk_norm_bwd", dqk_f, proj, 0, 2 * HP, fox_gains, HP, T, 1.0)
    dct = jnp.stack([dc_a.reshape(HP, T), dc_b.reshape(HP, T)], axis=1).reshape(H, T)
    drt = jnp.stack([dr_a.reshape(HP, T), dr_b.reshape(HP, T)], axis=1).reshape(H, T)
    dz_t, db_f = _forget_bwd("forget_gates_bwd", dct, drt, sg_t)

    (dq_s, dk_p, dv_p, dsink_a, dsink_b), hop2 = _swa_bwd(
        "swa_attention_bwd", q_s, k_d, v_d, sinks3, o_swa, do_swa, lse_sa, lse_sb, carry=_to_other_carry([t_wg2, t_wu2]))
    arrived(names[5:7], hop2)
    dqs_raw, dg_sq = _headnorm_bwd("swa_q_norm_bwd", dq_s, proj, 3 * HP, HP, swa_q_gains, HP, T, 1.0, rope=rope)
    dks_raw, dg_sk = _headnorm_bwd("swa_k_norm_bwd", dk_p, proj, 4 * HP, KVB, swa_k_gains, KVB, T, 1.0, rope=rope, fold=True)
    dvs_raw, _ = _headnorm_bwd("swa_v_fold", dv_p, None, 0, KVB, None, KVB, T, 1.0, fold=True, norm=False)

    dproj = jnp.concatenate([dqk_raw, dv_f.astype(BF16), dqs_raw, dks_raw, dvs_raw], axis=1)
    dproj_f = jnp.pad(dz_t.T, ((0, 0), (0, LANES - H))).astype(BF16)
    dwin_main, hop1 = _wgrad_2d("mix_proj_wgrad", dproj, u, MAIN // 9, min(1024, D), carry=_to_partner_carry([sum_wout]))
    (t_wout,) = relay_sums(names[4:5], [sum_wout], hop1)
    dwin_f = _wgrad_2d("mix_proj_forget_wgrad", dproj_f, u, LANES, min(1024, D))
    dwin_t = jnp.concatenate([dwin_main[:F_OFF], dwin_f[:H], dwin_main[F_OFF:]], axis=0)
    dwin_g = dwin_t.reshape(N_DEV, n_in, D)
    tkb = MAIN // 9
    du, (via_wout, sib_win) = _reduce_mm(
        "mix_bwd_in",
        [(dproj, pl.BlockSpec((tm, tkb), lambda i, r: (i, r)), win_main, pl.BlockSpec((tkb, D), lambda i, r: (r, 0)))],
        [(dproj_f, pl.BlockSpec((tm, LANES), lambda i, r: (i, 0)), win_f, pl.BlockSpec((LANES, D), lambda i, r: (0, 0)))],
        NN, T, D, tm, 9, carry=_join(_to_other_carry([t_wout]), _sibling_carry([dwin_g])))
    arrived(names[4:5], [via_wout])
    dh1, dg_mix, dh1h = _rmsnorm_bwd("mix_norm_bwd", du, h1, norm_mix_g, dh2, min(256, T), 0.5)
    (sum_win,) = pair_sums(names[3:4], [dwin_g], [sib_win])

    dwd1, hop1 = _wgrad_down("ffn1_wgrad_down", hm1, dh1h, min(1024, D), carry=_to_partner_carry([sum_win]))
    (t_win,) = relay_sums(names[3:4], [sum_win], hop1)
    (da1, db1), (via_win, sib_d) = _ffn_bwd_mid("ffn1_bwd_mid", dh1h, wd1, a1, b1, tm,
                                                carry=_join(_to_other_carry([t_win]), _sibling_carry([dwd1])))
    arrived(names[3:4], [via_win])
    (sum_wd1,) = pair_sums(names[2:3], [dwd1], [sib_d])
    dwg1, hop1 = _wgrad_down("ffn1_wgrad_gate", da1, n1, min(1024, D), carry=_to_partner_carry([sum_wd1]))
    (t_wd1,) = relay_sums(names[2:3], [sum_wd1], hop1)
    dwu1, (via_wd1, sib_g) = _wgrad_down("ffn1_wgrad_up", db1, n1, min(1024, D),
                                         carry=_join(_to_other_carry([t_wd1]), _sibling_carry([dwg1])))
    arrived(names[2:3], [via_wd1])
    (sum_wg1,) = pair_sums(names[0:1], [dwg1], [sib_g])
    dn1_gate, (*hop1, sib_u) = _reduce_mm(
        "ffn1_bwd_in_gate", [(da1, aspec, wg1, wspec)], [], NN, T, D, tm, J,
        carry=_join(_to_partner_carry([sum_wg1]), _sibling_carry([dwu1])))
    (t_wg1,) = relay_sums(names[0:1], [sum_wg1], hop1)
    (sum_wu1,) = pair_sums(names[1:2], [dwu1], [sib_u])
    dn1, (via_wg1, *hop1) = _reduce_mm(
        "ffn1_bwd_in_up", [(db1, aspec, wu1, wspec)], [], NN, T, D, tm, J, init=dn1_gate,
        carry=_join(_to_other_carry([t_wg1]), _to_partner_carry([sum_wu1])))
    arrived(names[0:1], [via_wg1])
    (t_wu1,) = relay_sums(names[1:2], [sum_wu1], hop1)
    arrived(names[1:2], _run_carry("grads_exchange", _to_other_carry([t_wu1])))
    dx, dg_ffn1 = _rmsnorm_bwd("ffn1_norm_bwd", dn1, xs, norm_ffn1_g, dh1, min(256, T), None)

    chip_sums = [sum_wg1, sum_wu1, sum_wd1, sum_win, sum_wout, sum_wg2, sum_wu2, sum_wd2]
    big_out = [_adam_shard("adam_" + nm, s, got[nm], w, m, v, chip_idx)
               for nm, s, w, m, v in zip(names, chip_sums, big_w, big_m, big_v)]

    dsinks = jnp.stack([dsink_a.reshape(HP), dsink_b.reshape(HP)], axis=1).reshape(H)
    small_g = [dg_ffn1, dg_mix, dg_ffn2, dg_of, dg_os, db_f, dg_fox[0, 0, :HEAD_DIM], dg_fox[1, 0, :HEAD_DIM],
               dg_sq[0, 0, :HEAD_DIM], dg_sk[0, 0, :HEAD_DIM], dsinks]
    small_w = [norm_ffn1_g, norm_mix_g, norm_ffn2_g, out_norm_fox_g, out_norm_swa_g, b_forget, fox_q_norm_g, fox_k_norm_g,
               swa_q_norm_g, swa_k_norm_g, swa_sinks]
    small_m = [m_norm_ffn1_g, m_norm_mix_g, m_norm_ffn2_g, m_out_norm_fox_g, m_out_norm_swa_g, m_b_forget, m_fox_q_norm_g,
               m_fox_k_norm_g, m_swa_q_norm_g, m_swa_k_norm_g, m_swa_sinks]
    small_v = [v_norm_ffn1_g, v_norm_mix_g, v_norm_ffn2_g, v_out_norm_fox_g, v_out_norm_swa_g, v_b_forget, v_fox_q_norm_g,
               v_fox_k_norm_g, v_swa_q_norm_g, v_swa_k_norm_g, v_swa_sinks]
    gathered = _gather_small(_pack_small(small_g, D))
    small_out = _adam_small("adam_small", gathered, _pack_small(small_w, D), _pack_small(small_m, D), _pack_small(small_v, D))
    small_out = [_unpack_small(p, D, H) for p in small_out]

    order = ["norm_ffn1_g", "ffn1_w_gate", "ffn1_w_up", "ffn1_w_down", "norm_mix_g", "w_in", "b_forget", "fox_q_norm_g", "fox_k_norm_g",
             "swa_q_norm_g", "swa_k_norm_g", "swa_sinks", "out_norm_fox_g", "out_norm_swa_g", "w_out", "norm_ffn2_g",
             "ffn2_w_gate", "ffn2_w_up", "ffn2_w_down"]
    small_names = ["norm_ffn1_g", "norm_mix_g", "norm_ffn2_g", "out_norm_fox_g", "out_norm_swa_g", "b_forget", "fox_q_norm_g",
                   "fox_k_norm_g", "swa_q_norm_g", "swa_k_norm_g", "swa_sinks"]
    result = [loss, dx[None]]
    for kind in range(4):
        for nm in order:
            if nm in names:
                leaf = big_out[names.index(nm)][kind]
                result.append((tr(leaf) if nm in transposed else leaf)[None])
            else:
                result.append(small_out[kind][small_names.index(nm)])
    return tuple(result)


def _headnorm_fwd_scaled(name, proj, col_off, ncb, gains, tm, scale, n_scaled):
    T = proj.shape[0]

    def body(x_ref, g_ref, o_ref):
        xv = x_ref[...]
        lo = _lane_lo(xv.shape)
        y = xv * _head_rstd(xv, lo) * g_ref[...]
        y = y * jnp.where(pl.program_id(0) < n_scaled, scale, 1.0)
        o_ref[...] = y.astype(BF16)

    return pl.pallas_call(
        body, out_shape=jax.ShapeDtypeStruct((T, ncb * LANES), BF16), grid=(ncb, T // tm),
        in_specs=[pl.BlockSpec((tm, LANES), lambda c, i: (i, col_off + c)), pl.BlockSpec((None, 1, LANES), lambda c, i: (c, 0, 0))],
        out_specs=pl.BlockSpec((tm, LANES), lambda c, i: (i, c)), name=name, compiler_params=_params(2))(proj, gains)
```

```python
import functools

import jax
import jax.numpy as jnp
from jax import lax
from jax.experimental import pallas as pl
from jax.experimental.pallas import tpu as pltpu

F32 = jnp.float32
BF16 = jnp.bfloat16

HEAD_DIM = 64
LANES = 128
WINDOW = 128
GQA_GROUP = 4
EPS = 1e-6
ROPE_THETA = 10000.0
ADAM_LR = 0.001
ADAM_B1 = 0.9
ADAM_B2 = 0.999
ADAM_EPS = 1e-08
ADAM_WD = 0.01
ADAM_STEP = 10
N_DEV = 8
NEG = -1e30
VMEM_LIMIT_V7X = 48 * 1024 * 1024
ROW_TILE_CAP = 512
MESH = pl.DeviceIdType.MESH

NN = (((1,), (0,)), ((), ()))
NT = (((1,), (1,)), ((), ()))
TN = (((0,), (0,)), ((), ()))


def _dot(a, b, dims):
    return lax.dot_general(a, b, dims, preferred_element_type=F32)


def _params(n_axes):
    return pltpu.CompilerParams(dimension_semantics=("arbitrary",) * n_axes, vmem_limit_bytes=VMEM_LIMIT_V7X)


def _row_tile(rows, cap=ROW_TILE_CAP):
    best = None
    for t in range(16, min(rows, cap) + 1, 16):
        if rows % t == 0:
            best = t
    return best or rows


def _lane_lo(shape):
    return lax.broadcasted_iota(jnp.int32, shape, len(shape) - 1) < HEAD_DIM


def _keep(sel, x):
    return jnp.where(sel, x.astype(F32), 0.0).astype(BF16)


_HBM = pl.BlockSpec(memory_space=pltpu.HBM)


class _Carry:
    def __init__(self, inputs, out_shapes, scratch, start, finish, middle=None):
        self.inputs, self.out_shapes, self.scratch = list(inputs), list(out_shapes), list(scratch)
        self.start, self.finish, self.middle = start, finish, middle or (lambda ins, outs, scr: None)


def _join(*carries):
    def hook(which):
        def run(ins, outs, scr):
            i = o = s = 0
            for c in carries:
                ni, no, ns = len(c.inputs), len(c.out_shapes), len(c.scratch)
                getattr(c, which)(ins[i:i + ni], outs[o:o + no], scr[s:s + ns])
                i, o, s = i + ni, o + no, s + ns
        return run

    return _Carry([a for c in carries for a in c.inputs], [a for c in carries for a in c.out_shapes],
                  [a for c in carries for a in c.scratch], hook("start"), hook("finish"), hook("middle"))


def _call(body, *, name, grid, in_specs, out_specs, out_shape, args, scratch_shapes=(), carry=None):
    params = _params(len(grid))
    if carry is None:
        return pl.pallas_call(body, out_shape=out_shape, grid=grid, in_specs=list(in_specs), out_specs=out_specs,
                              scratch_shapes=list(scratch_shapes), name=name, compiler_params=params)(*args)
    single = not isinstance(out_shape, (tuple, list))
    shapes = (out_shape,) if single else tuple(out_shape)
    specs = (out_specs,) if single else tuple(out_specs)
    n_in, n_out, n_scr = len(args), len(shapes), len(scratch_shapes)
    c_in, c_out = len(carry.inputs), len(carry.out_shapes)

    def wrapped(*refs):
        ins, c_ins = refs[:n_in], refs[n_in:n_in + c_in]
        o0 = n_in + c_in
        outs, c_outs = refs[o0:o0 + n_out], refs[o0 + n_out:o0 + n_out + c_out]
        s0 = o0 + n_out + c_out
        scr, c_scr = refs[s0:s0 + n_scr], refs[s0 + n_scr:]
        step, total = pl.program_id(0), grid[0]
        for ax in range(1, len(grid)):
            step, total = step * grid[ax] + pl.program_id(ax), total * grid[ax]

        @pl.when(step == 0)
        def _():
            carry.start(c_ins, c_outs, c_scr)

        @pl.when(step == total // 2)
        def _():
            carry.middle(c_ins, c_outs, c_scr)

        body(*ins, *outs, *scr)

        @pl.when(step == total - 1)
        def _():
            carry.finish(c_ins, c_outs, c_scr)

    res = pl.pallas_call(
        wrapped, out_shape=shapes + tuple(carry.out_shapes), grid=grid, in_specs=list(in_specs) + [_HBM] * c_in,
        out_specs=specs + (_HBM,) * c_out, scratch_shapes=list(scratch_shapes) + carry.scratch, name=name,
        compiler_params=params)(*args, *carry.inputs)
    main = res[:n_out]
    return (main[0] if single else tuple(main)), tuple(res[n_out:])


def _rms_bwd(dn, x, g):
    r = lax.rsqrt(jnp.mean(x * x, axis=-1, keepdims=True) + EPS)
    xh = x * r
    dxh = dn * g
    dx = r * (dxh - xh * jnp.mean(dxh * xh, axis=-1, keepdims=True))
    return dx, jnp.sum(dn * xh, axis=0, keepdims=True)


def _rmsnorm_fwd(name, x, g, tm, carry=None):
    T, D = x.shape

    def body(x_ref, g_ref, o_ref):
        xf = x_ref[...]
        r = lax.rsqrt(jnp.mean(xf * xf, axis=-1, keepdims=True) + EPS)
        o_ref[...] = (xf * r * g_ref[...]).astype(BF16)

    return _call(
        body, name=name, grid=(T // tm,), out_shape=jax.ShapeDtypeStruct((T, D), BF16),
        in_specs=[pl.BlockSpec((tm, D), lambda i: (i, 0)), pl.BlockSpec((1, D), lambda i: (0, 0))],
        out_specs=pl.BlockSpec((tm, D), lambda i: (i, 0)), args=[x, g], carry=carry)


def _outnorm_fwd(name, o_fox, o_swa, g_fox, g_swa, tm):
    T, Dh = o_fox.shape

    def body(a_ref, b_ref, ga_ref, gb_ref, o_ref):
        for ref, g_ref, lo in ((a_ref, ga_ref, 0), (b_ref, gb_ref, Dh)):
            xf = ref[...]
            r = lax.rsqrt(jnp.mean(xf * xf, axis=-1, keepdims=True) + EPS)
            o_ref[:, lo:lo + Dh] = (xf * r * g_ref[...]).astype(BF16)

    row = pl.BlockSpec((tm, Dh), lambda i: (i, 0))
    gain = pl.BlockSpec((1, Dh), lambda i: (0, 0))
    return pl.pallas_call(
        body, out_shape=jax.ShapeDtypeStruct((T, 2 * Dh), BF16), grid=(T // tm,),
        in_specs=[row, row, gain, gain], out_specs=pl.BlockSpec((tm, 2 * Dh), lambda i: (i, 0)),
        name=name, compiler_params=_params(1))(o_fox, o_swa, g_fox, g_swa)


def _outnorm_bwd(name, dhb, wout, half, o, g, tm):
    T, D = dhb.shape
    Dh = o.shape[1]

    def body(a_ref, w_ref, o_ref, g_ref, do_ref, dg_ref):
        don = _dot(a_ref[...], w_ref[...], NT)
        dx, dg = _rms_bwd(don, o_ref[...], g_ref[...])
        do_ref[...] = dx.astype(BF16)

        @pl.when(pl.program_id(0) == 0)
        def _():
            dg_ref[...] = dg

        @pl.when(pl.program_id(0) > 0)
        def _():
            dg_ref[...] += dg

    return pl.pallas_call(
        body, out_shape=(jax.ShapeDtypeStruct((T, Dh), BF16), jax.ShapeDtypeStruct((1, Dh), F32)), grid=(T // tm,),
        in_specs=[pl.BlockSpec((tm, D), lambda i: (i, 0)), pl.BlockSpec((Dh, D), lambda i: (half, 0)),
                  pl.BlockSpec((tm, Dh), lambda i: (i, 0)), pl.BlockSpec((1, Dh), lambda i: (0, 0))],
        out_specs=(pl.BlockSpec((tm, Dh), lambda i: (i, 0)), pl.BlockSpec((1, Dh), lambda i: (0, 0))),
        name=name, compiler_params=_params(1))(dhb, wout, o, g)


def _mm(name, a, b, tm, tn, dims=NN, resid=None, carry=None):
    M, K = a.shape
    transposed = dims == NT
    N = b.shape[0] if transposed else b.shape[1]

    def body(*refs):
        if resid is None:
            a_ref, b_ref, o_ref = refs
            o_ref[...] = _dot(a_ref[...], b_ref[...], dims)
        else:
            a_ref, b_ref, r_ref, o_ref = refs
            o_ref[...] = r_ref[...] + _dot(a_ref[...], b_ref[...], dims)

    ospec = pl.BlockSpec((tm, tn), lambda n, i: (i, n))
    bspec = pl.BlockSpec((tn, K), lambda n, i: (n, 0)) if transposed else pl.BlockSpec((K, tn), lambda n, i: (0, n))
    in_specs = [pl.BlockSpec((tm, K), lambda n, i: (i, 0)), bspec]
    args = [a, b]
    if resid is not None:
        in_specs.append(ospec)
        args.append(resid)
    return _call(body, name=name, grid=(N // tn, M // tm), in_specs=in_specs, out_specs=ospec,
                 out_shape=jax.ShapeDtypeStruct((M, N), F32), args=args, carry=carry)


def _wgrad_2d(name, a, b, tmm, tn, carry=None):
    T, M = a.shape
    N = b.shape[1]

    def body(a_ref, b_ref, o_ref):
        o_ref[...] = _dot(a_ref[...], b_ref[...], TN).astype(BF16)

    return _call(
        body, name=name, grid=(M // tmm, N // tn), out_shape=jax.ShapeDtypeStruct((M, N), BF16),
        in_specs=[pl.BlockSpec((T, tmm), lambda m, n: (0, m)), pl.BlockSpec((T, tn), lambda m, n: (0, n))],
        out_specs=pl.BlockSpec((tmm, tn), lambda m, n: (m, n)), args=[a, b], carry=carry)


def _wgrad_down(name, hm, df, tn, carry=None):
    J, T, Fs = hm.shape
    D = df.shape[1]

    def body(a_ref, b_ref, o_ref):
        o_ref[...] = _dot(a_ref[...], b_ref[...], TN).astype(BF16)

    return _call(
        body, name=name, grid=(J, D // tn), out_shape=jax.ShapeDtypeStruct((J, Fs, D), BF16),
        in_specs=[pl.BlockSpec((None, T, Fs), lambda j, n: (j, 0, 0)), pl.BlockSpec((T, tn), lambda j, n: (0, n))],
        out_specs=pl.BlockSpec((None, Fs, tn), lambda j, n: (j, 0, n)), args=[hm, df], carry=carry)


def _wgrad_up(name, n, da, db, tn, carry=None):
    T, D = n.shape
    J, _, Fs = da.shape

    def body(n_ref, da_ref, db_ref, og_ref, ou_ref):
        nv = n_ref[...]
        og_ref[...] = _dot(da_ref[...], nv, TN).astype(BF16)
        ou_ref[...] = _dot(db_ref[...], nv, TN).astype(BF16)

    act = pl.BlockSpec((None, T, Fs), lambda j, m: (j, 0, 0))
    out = pl.BlockSpec((None, Fs, tn), lambda j, m: (j, 0, m))
    shape = jax.ShapeDtypeStruct((J, Fs, D), BF16)
    return _call(
        body, name=name, grid=(J, D // tn), out_shape=(shape, shape),
        in_specs=[pl.BlockSpec((T, tn), lambda j, m: (0, m)), act, act], out_specs=(out, out),
        args=[n, da, db], carry=carry)


def _reduce_mm(name, pairs, once, dims, T, D, tm, steps, init=None, carry=None):
    n_pairs = len(pairs)
    n_once = len(once)
    n_mm = 2 * (n_pairs + n_once)

    def body(*refs):
        pr = refs[:2 * n_pairs]
        on = refs[2 * n_pairs:n_mm]
        o_ref, acc = refs[-2:]
        r = pl.program_id(1)

        @pl.when(r == 0)
        def _():
            acc[...] = jnp.zeros(acc.shape, F32) if init is None else refs[n_mm][...]

        for p in range(n_pairs):
            acc[...] += _dot(pr[2 * p][...], pr[2 * p + 1][...], dims)

        @pl.when(r == steps - 1)
        def _():
            dn = acc[...]
            for p in range(n_once):
                dn = dn + _dot(on[2 * p][...], on[2 * p + 1][...], dims)
            o_ref[...] = dn

    in_specs, args = [], []
    for a, a_spec, w, w_spec in list(pairs) + list(once):
        in_specs += [a_spec, w_spec]
        args += [a, w]
    row = pl.BlockSpec((tm, D), lambda i, r: (i, 0))
    if init is not None:
        in_specs.append(row)
        args.append(init)
    return _call(body, name=name, grid=(T // tm, steps), in_specs=in_specs, out_specs=row, out_shape=jax.ShapeDtypeStruct((T, D), F32),
                 args=args, scratch_shapes=[pltpu.VMEM((tm, D), F32)], carry=carry)


def _rmsnorm_bwd(name, dn, x, g, dh, tm, bf16_scale, carry=None):
    T, D = x.shape
    emit_bf16 = bf16_scale is not None

    def body(dn_ref, x_ref, g_ref, dh_ref, *outs):
        dxn, dg = _rms_bwd(dn_ref[...], x_ref[...], g_ref[...])
        dx = dh_ref[...] + dxn
        outs[0][...] = dx
        if emit_bf16:
            outs[2][...] = (bf16_scale * dx).astype(BF16)

        @pl.when(pl.program_id(0) == 0)
        def _():
            outs[1][...] = dg

        @pl.when(pl.program_id(0) > 0)
        def _():
            outs[1][...] += dg

    row = pl.BlockSpec((tm, D), lambda i: (i, 0))
    gain = pl.BlockSpec((1, D), lambda i: (0, 0))
    out_shape = [jax.ShapeDtypeStruct((T, D), F32), jax.ShapeDtypeStruct((1, D), F32)]
    out_specs = [row, gain]
    if emit_bf16:
        out_shape.append(jax.ShapeDtypeStruct((T, D), BF16))
        out_specs.append(row)
    return _call(body, name=name, grid=(T // tm,), in_specs=[row, row, gain, row], out_specs=tuple(out_specs),
                 out_shape=tuple(out_shape), args=[dn, x, g, dh], carry=carry)


def _loss_grad(name, y, target, tm):
    T, D = y.shape

    def body(y_ref, t_ref, dy_ref, dyh_ref, sq_ref):
        diff = y_ref[...] - t_ref[...]
        sq = jnp.sum(jnp.sum(diff * diff, axis=1, keepdims=True), axis=0, keepdims=True)
        dy = diff * (1.0 / D)
        dy_ref[...] = dy
        dyh_ref[...] = (0.5 * dy).astype(BF16)

        @pl.when(pl.program_id(0) == 0)
        def _():
            sq_ref[...] = sq

        @pl.when(pl.program_id(0) > 0)
        def _():
            sq_ref[...] += sq

    row = pl.BlockSpec((tm, D), lambda i: (i, 0))
    return pl.pallas_call(
        body, out_shape=(jax.ShapeDtypeStruct((T, D), F32), jax.ShapeDtypeStruct((T, D), BF16), jax.ShapeDtypeStruct((1, 1), F32)),
        grid=(T // tm,), in_specs=[row, row], out_specs=(row, row, pl.BlockSpec((1, 1), lambda i: (0, 0))),
        name=name, compiler_params=_params(1))(y, target)


def _ffn_up(name, n, wg, wu, tm, carry=None):
    T, D = n.shape
    J, Fs, _ = wg.shape

    def body(n_ref, wg_ref, wu_ref, a_ref, b_ref, h_ref):
        xv = n_ref[...]
        a = _dot(xv, wg_ref[...], NT)
        b = _dot(xv, wu_ref[...], NT)
        a_ref[...] = a.astype(BF16)
        b_ref[...] = b.astype(BF16)
        h_ref[...] = (a * jax.nn.sigmoid(a) * b).astype(BF16)

    act = jax.ShapeDtypeStruct((J, T, Fs), BF16)
    wspec = pl.BlockSpec((None, Fs, D), lambda j, i: (j, 0, 0))
    aspec = pl.BlockSpec((None, tm, Fs), lambda j, i: (j, i, 0))
    return _call(
        body, name=name, grid=(J, T // tm), out_shape=(act, act, act),
        in_specs=[pl.BlockSpec((tm, D), lambda j, i: (i, 0)), wspec, wspec], out_specs=(aspec, aspec, aspec),
        args=[n, wg, wu], carry=carry)


def _ffn_gate(name, n, wg, tm, carry=None):
    T, D = n.shape
    J, Fs, _ = wg.shape

    def body(n_ref, wg_ref, a_ref):
        a_ref[...] = _dot(n_ref[...], wg_ref[...], NT).astype(BF16)

    aspec = pl.BlockSpec((None, tm, Fs), lambda j, i: (j, i, 0))
    return _call(
        body, name=name, grid=(J, T // tm), out_shape=jax.ShapeDtypeStruct((J, T, Fs), BF16),
        in_specs=[pl.BlockSpec((tm, D), lambda j, i: (i, 0)), pl.BlockSpec((None, Fs, D), lambda j, i: (j, 0, 0))],
        out_specs=aspec, args=[n, wg], carry=carry)


def _ffn_up_only(name, n, wu, a, tm, carry=None):
    T, D = n.shape
    J, Fs, _ = wu.shape

    def body(n_ref, wu_ref, a_ref, b_ref, h_ref):
        b = _dot(n_ref[...], wu_ref[...], NT)
        a = a_ref[...].astype(F32)
        b_ref[...] = b.astype(BF16)
        h_ref[...] = (a * jax.nn.sigmoid(a) * b).astype(BF16)

    act = jax.ShapeDtypeStruct((J, T, Fs), BF16)
    aspec = pl.BlockSpec((None, tm, Fs), lambda j, i: (j, i, 0))
    return _call(
        body, name=name, grid=(J, T // tm), out_shape=(act, act),
        in_specs=[pl.BlockSpec((tm, D), lambda j, i: (i, 0)), pl.BlockSpec((None, Fs, D), lambda j, i: (j, 0, 0)), aspec],
        out_specs=(aspec, aspec), args=[n, wu, a], carry=carry)


def _ffn_down(name, hm, wd, resid, tm, tn, carry=None):
    J, T, Fs = hm.shape
    D = wd.shape[2]

    def body(h_ref, w_ref, r_ref, o_ref, acc):
        j = pl.program_id(2)

        @pl.when(j == 0)
        def _():
            acc[...] = jnp.zeros(acc.shape, F32)

        acc[...] += _dot(h_ref[...], w_ref[...], NN)

        @pl.when(j == J - 1)
        def _():
            o_ref[...] = r_ref[...] + 0.5 * acc[...]

    tile = pl.BlockSpec((tm, tn), lambda i, n, j: (i, n))
    return _call(
        body, name=name, grid=(T // tm, D // tn, J), out_shape=jax.ShapeDtypeStruct((T, D), F32),
        in_specs=[pl.BlockSpec((None, tm, Fs), lambda i, n, j: (j, i, 0)), pl.BlockSpec((None, Fs, tn), lambda i, n, j: (j, 0, n)), tile],
        out_specs=tile, scratch_shapes=[pltpu.VMEM((tm, tn), F32)], args=[hm, wd, resid], carry=carry)


def _ffn_bwd_mid(name, dfh, wd, a, b, tm, carry=None):
    T, D = dfh.shape
    J, Fs, _ = wd.shape

    def body(df_ref, w_ref, a_ref, b_ref, da_ref, db_ref):
        dhm = _dot(df_ref[...], w_ref[...], NT)
        av = a_ref[...].astype(F32)
        bv = b_ref[...].astype(F32)
        sg = jax.nn.sigmoid(av)
        da_ref[...] = (dhm * bv * (sg * (1.0 + av * (1.0 - sg)))).astype(BF16)
        db_ref[...] = (dhm * (av * sg)).astype(BF16)

    act = jax.ShapeDtypeStruct((J, T, Fs), BF16)
    aspec = pl.BlockSpec((None, tm, Fs), lambda j, i: (j, i, 0))
    return _call(
        body, name=name, grid=(J, T // tm), out_shape=(act, act),
        in_specs=[pl.BlockSpec((tm, D), lambda j, i: (i, 0)), pl.BlockSpec((None, Fs, D), lambda j, i: (j, 0, 0)), aspec, aspec],
        out_specs=(aspec, aspec), args=[dfh, wd, a, b], carry=carry)


def _rot_half(y, lane):
    first = (lane & (HEAD_DIM // 2)) == 0
    return jnp.where(first, pltpu.roll(y, LANES - HEAD_DIM // 2, 1), pltpu.roll(y, HEAD_DIM // 2, 1))


def _head_rstd(x, lo):
    sq = x * x
    ss_a = jnp.sum(jnp.where(lo, sq, 0.0), axis=-1, keepdims=True)
    ss_b = jnp.sum(jnp.where(lo, 0.0, sq), axis=-1, keepdims=True)
    return lax.rsqrt(jnp.where(lo, ss_a, ss_b) * (1.0 / HEAD_DIM) + EPS)


def _headnorm_fwd(name, proj, col_off, ncb, gains, tm, scale, rope=None, dup=False):
    T = proj.shape[0]
    with_rope = rope is not None
    width = 2 * LANES if dup else LANES

    def body(*refs):
        if with_rope:
            x_ref, g_ref, cos_ref, sin_ref, o_ref = refs
        else:
            x_ref, g_ref, o_ref = refs
        xv = x_ref[...]
        lane = lax.broadcasted_iota(jnp.int32, xv.shape, 1)
        lo = lane < HEAD_DIM
        y = xv * _head_rstd(xv, lo) * g_ref[...]
        if with_rope:
            y = y * cos_ref[...] + _rot_half(y, lane) * sin_ref[...]
        y = y * scale
        if dup:
            sw = pltpu.roll(y, HEAD_DIM, 1)
            o_ref[:, :LANES] = jnp.where(lo, y, sw).astype(BF16)
            o_ref[:, LANES:] = jnp.where(lo, sw, y).astype(BF16)
        else:
            o_ref[...] = y.astype(BF16)

    in_specs = [pl.BlockSpec((tm, LANES), lambda c, i: (i, col_off + c)), pl.BlockSpec((None, 1, LANES), lambda c, i: (c, 0, 0))]
    args = [proj, gains]
    if with_rope:
        tab = pl.BlockSpec((tm, LANES), lambda c, i: (i, 0))
        in_specs += [tab, tab]
        args += list(rope)
    return pl.pallas_call(
        body, out_shape=jax.ShapeDtypeStruct((T, ncb * width), BF16), grid=(ncb, T // tm),
        in_specs=in_specs, out_specs=pl.BlockSpec((tm, width), lambda c, i: (i, c)),
        name=name, compiler_params=_params(2))(*args)


def _headnorm_bwd(name, dy, proj, col_off, ncb, gains, group, tm, scale, rope=None, fold=False, norm=True):
    T = dy.shape[0]
    with_rope = rope is not None
    n_groups = ncb // group
    dy_width = 4 * LANES if fold else LANES

    def body(*refs):
        refs = list(refs)
        dy_ref = refs.pop(0)
        x_ref = refs.pop(0) if norm else None
        g_ref = refs.pop(0) if norm else None
        cos_ref = refs.pop(0) if with_rope else None
        sin_ref = refs.pop(0) if with_rope else None
        dx_ref = refs.pop(0)
        dg_ref = refs.pop(0) if norm else None
        c = pl.program_id(0)
        i = pl.program_id(1)
        d = dy_ref[...]
        lane = lax.broadcasted_iota(jnp.int32, (d.shape[0], LANES), 1)
        lo = lane < HEAD_DIM
        if fold:
            t0 = d[:, 0:LANES] + d[:, LANES:2 * LANES]
            t1 = d[:, 2 * LANES:3 * LANES] + d[:, 3 * LANES:4 * LANES]
            d = jnp.where(lo, t0 + pltpu.roll(t0, HEAD_DIM, 1), t1 + pltpu.roll(t1, HEAD_DIM, 1))
        d = d * scale
        if with_rope:
            d = d * cos_ref[...] + _rot_half(d * sin_ref[...], lane)
        if not norm:
            dx_ref[...] = d.astype(BF16)
            return
        xv = x_ref[...]
        gv = g_ref[...]
        r = _head_rstd(xv, lo)
        xh = xv * r
        dxh = d * gv
        pr = dxh * xh
        m_a = jnp.sum(jnp.where(lo, pr, 0.0), axis=-1, keepdims=True)
        m_b = jnp.sum(jnp.where(lo, 0.0, pr), axis=-1, keepdims=True)
        mean = jnp.where(lo, m_a, m_b) * (1.0 / HEAD_DIM)
        dx_ref[...] = (r * (dxh - xh * mean)).astype(BF16)
        dgp = jnp.sum(d * xh, axis=0, keepdims=True)
        dgp = dgp + pltpu.roll(dgp, HEAD_DIM, 1)
        first = jnp.logical_and(c % group == 0, i == 0)

        @pl.when(first)
        def _():
            dg_ref[...] = dgp

        @pl.when(jnp.logical_not(first))
        def _():
            dg_ref[...] += dgp

    in_specs = [pl.BlockSpec((tm, dy_width), lambda c, i: (i, c))]
    args = [dy]
    if norm:
        in_specs += [pl.BlockSpec((tm, LANES), lambda c, i: (i, col_off + c)), pl.BlockSpec((None, 1, LANES), lambda c, i: (c, 0, 0))]
        args += [proj, gains]
    if with_rope:
        tab = pl.BlockSpec((tm, LANES), lambda c, i: (i, 0))
        in_specs += [tab, tab]
        args += list(rope)
    out_shape = [jax.ShapeDtypeStruct((T, ncb * LANES), BF16)]
    out_specs = [pl.BlockSpec((tm, LANES), lambda c, i: (i, c))]
    if norm:
        out_shape.append(jax.ShapeDtypeStruct((n_groups, 1, LANES), F32))
        out_specs.append(pl.BlockSpec((None, 1, LANES), lambda c, i: (c // group, 0, 0)))
    res = pl.pallas_call(
        body, out_shape=tuple(out_shape), grid=(ncb, T // tm), in_specs=in_specs, out_specs=tuple(out_specs),
        name=name, compiler_params=_params(2))(*args)
    return res if norm else (res[0], None)


def _dot_exact(x, tri):
    hi = x.astype(BF16)
    r1 = x - hi.astype(F32)
    mid = r1.astype(BF16)
    lo = (r1 - mid.astype(F32)).astype(BF16)
    return _dot(hi, tri, NN) + _dot(mid, tri, NN) + _dot(lo, tri, NN)


def _forget_fwd(name, zt, bias):
    H, T = zt.shape
    blk = min(256, T)

    def body(z_ref, b_ref, c_ref, s_ref):
        z = z_ref[...] + b_ref[...]
        s_ref[...] = jax.nn.sigmoid(-z)
        lf = jnp.minimum(z, 0.0) - jnp.log(1.0 + jnp.exp(-jnp.abs(z)))
        tri = (lax.broadcasted_iota(jnp.int32, (blk, blk), 0) <= lax.broadcasted_iota(jnp.int32, (blk, blk), 1)).astype(BF16)
        carry = jnp.zeros((H, 1), F32)
        for bi in range(T // blk):
            xb = lf[:, bi * blk:(bi + 1) * blk]
            c_ref[:, bi * blk:(bi + 1) * blk] = _dot_exact(xb, tri) + carry
            carry = carry + jnp.sum(xb, axis=-1, keepdims=True)

    shape = jax.ShapeDtypeStruct((H, T), F32)
    full = pl.BlockSpec((H, T), lambda i: (0, 0))
    return pl.pallas_call(
        body, out_shape=(shape, shape), grid=(1,), in_specs=[full, pl.BlockSpec((H, 1), lambda i: (0, 0))],
        out_specs=(full, full), name=name, compiler_params=_params(1))(zt, bias)


def _forget_bwd(name, dct, drt, sgt):
    H, T = dct.shape
    blk = min(256, T)

    def body(dc_ref, dr_ref, s_ref, dz_ref, db_ref):
        dc = dc_ref[...] + dr_ref[...]
        tri = (lax.broadcasted_iota(jnp.int32, (blk, blk), 0) >= lax.broadcasted_iota(jnp.int32, (blk, blk), 1)).astype(BF16)
        carry = jnp.zeros((H, 1), F32)
        db = jnp.zeros((H, 1), F32)
        for bi in reversed(range(T // blk)):
            xb = dc[:, bi * blk:(bi + 1) * blk]
            dz = (_dot_exact(xb, tri) + carry) * s_ref[:, bi * blk:(bi + 1) * blk]
            dz_ref[:, bi * blk:(bi + 1) * blk] = dz
            db = db + jnp.sum(dz, axis=-1, keepdims=True)
            carry = carry + jnp.sum(xb, axis=-1, keepdims=True)
        db_ref[...] = db

    full = pl.BlockSpec((H, T), lambda i: (0, 0))
    return pl.pallas_call(
        body, out_shape=(jax.ShapeDtypeStruct((H, T), F32), jax.ShapeDtypeStruct((H, 1), F32)), grid=(1,),
        in_specs=[full, full, full], out_specs=(full, pl.BlockSpec((H, 1), lambda i: (0, 0))),
        name=name, compiler_params=_params(1))(dct, drt, sgt)


STRIP = 256


def _fox_fwd(name, qk, v, crow, tq, tk, carry=None):
    T, Dh = v.shape
    HP = Dh // LANES
    nk = T // tk
    assert tk % tq == 0 and tq % STRIP == 0
    n_strips = tq // STRIP

    def body(q_ref, k_ref, v_ref, ra_ref, rb_ref, o_ref, la_ref, lb_ref, s_ref, p_ref, m_ref, l_ref, acc_ref):
        i = pl.program_id(1)
        q2 = q_ref[...]
        lo = _lane_lo((tq, LANES))
        q_st = jnp.concatenate([_keep(lo, q2), _keep(jnp.logical_not(lo), q2)], axis=0)
        r_refs = (ra_ref, rb_ref)
        m_ref[...] = jnp.full(m_ref.shape, NEG, F32)
        l_ref[...] = jnp.zeros(l_ref.shape, F32)
        acc_ref[...] = jnp.zeros(acc_ref.shape, F32)
        rel = lax.broadcasted_iota(jnp.int32, (STRIP, tk), 0) - lax.broadcasted_iota(jnp.int32, (STRIP, tk), 1)

        def chunk(kc, masked):
            start = pl.multiple_of(kc * tk, tk)
            kb = k_ref[pl.ds(start, tk), :]
            vb = v_ref[pl.ds(start, tk), :]
            s_ref[...] = _dot(q_st, kb, NT)
            for h in range(2):
                cs = r_refs[h][kc]
                for st in range(n_strips):
                    rows = pl.ds(h * tq + st * STRIP, STRIP)
                    s = s_ref[rows, :] - cs
                    if masked:
                        s = jnp.where(rel >= start - (i * tq + st * STRIP), s, NEG)
                    m_old = m_ref[rows, :]
                    mn = jnp.maximum(m_old, jnp.max(s, axis=-1, keepdims=True))
                    p = jnp.exp(s - mn)
                    alpha = jnp.exp(m_old - mn)
                    l_ref[rows, :] = alpha * l_ref[rows, :] + jnp.sum(p, axis=-1, keepdims=True)
                    m_ref[rows, :] = mn
                    p_ref[rows, :] = p.astype(BF16)
                    acc_ref[rows, :] = acc_ref[rows, :] * alpha
            acc_ref[...] += _dot(p_ref[...], vb, NN)

        n_full = (i * tq) // tk

        def full_chunk(kc, _):
            chunk(kc, False)
            return 0

        lax.fori_loop(0, n_full, full_chunk, 0)
        chunk(n_full, True)
        top, bot = pl.ds(0, tq), pl.ds(tq, tq)
        o_ref[...] = jnp.where(lo, acc_ref[top, :] / l_ref[top, :], acc_ref[bot, :] / l_ref[bot, :])
        la_ref[...] = m_ref[top, :] + jnp.log(l_ref[top, :])
        lb_ref[...] = m_ref[bot, :] + jnp.log(l_ref[bot, :])

    row = lambda off: pl.BlockSpec((None, nk, 1, tk), lambda h, i: (2 * h + off, 0, 0, 0))
    lse = jax.ShapeDtypeStruct((HP, T, 1), F32)
    lspec = pl.BlockSpec((None, tq, 1), lambda h, i: (h, i, 0))
    scratch = [pltpu.VMEM((2 * tq, tk), F32), pltpu.VMEM((2 * tq, tk), BF16), pltpu.VMEM((2 * tq, 1), F32),
               pltpu.VMEM((2 * tq, 1), F32), pltpu.VMEM((2 * tq, LANES), F32)]
    return _call(
        body, name=name, grid=(HP, T // tq), out_shape=(jax.ShapeDtypeStruct((T, Dh), F32), lse, lse),
        in_specs=[pl.BlockSpec((tq, LANES), lambda h, i: (i, h)), pl.BlockSpec((T, LANES), lambda h, i: (0, HP + h)),
                  pl.BlockSpec((T, LANES), lambda h, i: (0, h)), row(0), row(1)],
        out_specs=(pl.BlockSpec((tq, LANES), lambda h, i: (i, h)), lspec, lspec),
        args=[qk, qk, v, crow, crow], scratch_shapes=scratch, carry=carry)


def _fox_bwd(name, qk, v, o, do, crow, lse_a, lse_b, tq, tk, carry=None):
    T, Dh = v.shape
    HP = Dh // LANES
    nk = T // tk
    scale = HEAD_DIM ** -0.5
    assert tk % tq == 0 and tq % STRIP == 0
    n_strips = tq // STRIP

    def body(q_ref, k_ref, v_ref, o_ref, do_ref, ra_ref, rb_ref, la_ref, lb_ref,
             dq_ref, dk_ref, dv_ref, dca_ref, dcb_ref, dra_ref, drb_ref, s_ref, dp_ref, p_ref, ds_ref, dq_acc, dsum_ref):
        i = pl.program_id(1)

        @pl.when(i == 0)
        def _():
            dk_ref[...] = jnp.zeros_like(dk_ref)
            dv_ref[...] = jnp.zeros_like(dv_ref)
            dca_ref[...] = jnp.zeros_like(dca_ref)
            dcb_ref[...] = jnp.zeros_like(dcb_ref)

        q2 = q_ref[...]
        do2 = do_ref[...]
        lo = _lane_lo((tq, LANES))
        hi = jnp.logical_not(lo)
        q_st = jnp.concatenate([_keep(lo, q2), _keep(hi, q2)], axis=0)
        do_st = jnp.concatenate([_keep(lo, do2), _keep(hi, do2)], axis=0)
        prod = do2.astype(F32) * o_ref[...]
        dsum_ref[pl.ds(0, tq), :] = jnp.sum(jnp.where(lo, prod, 0.0), axis=-1, keepdims=True)
        dsum_ref[pl.ds(tq, tq), :] = jnp.sum(jnp.where(lo, 0.0, prod), axis=-1, keepdims=True)
        r_refs, l_refs, dc_refs, dr_refs = (ra_ref, rb_ref), (la_ref, lb_ref), (dca_ref, dcb_ref), (dra_ref, drb_ref)
        dq_acc[...] = jnp.zeros(dq_acc.shape, F32)
        dra_ref[...] = jnp.zeros(dra_ref.shape, F32)
        drb_ref[...] = jnp.zeros(drb_ref.shape, F32)
        rel = lax.broadcasted_iota(jnp.int32, (STRIP, tk), 0) - lax.broadcasted_iota(jnp.int32, (STRIP, tk), 1)

        def chunk(kc, masked):
            start = pl.multiple_of(kc * tk, tk)
            kb = k_ref[pl.ds(start, tk), :]
            vb = v_ref[pl.ds(start, tk), :]
            s_ref[...] = _dot(q_st, kb, NT)
            dp_ref[...] = _dot(do_st, vb, NT)
            for h in range(2):
                cs = r_refs[h][kc]
                col_sum = jnp.zeros((1, tk), F32)
                for st in range(n_strips):
                    rows = pl.ds(st * STRIP, STRIP)
                    both = pl.ds(h * tq + st * STRIP, STRIP)
                    s = s_ref[both, :] - cs
                    if masked:
                        s = jnp.where(rel >= start - (i * tq + st * STRIP), s, NEG)
                    p = jnp.exp(s - l_refs[h][rows, :])
                    ds = p * (dp_ref[both, :] - dsum_ref[both, :])
                    p_ref[both, :] = p.astype(BF16)
                    ds_ref[both, :] = ds.astype(BF16)
                    col_sum = col_sum + jnp.sum(ds, axis=0, keepdims=True)
                    dr_refs[h][rows, :] += jnp.sum(ds, axis=-1, keepdims=True)
                dc_refs[h][kc] = dc_refs[h][kc] - col_sum
            dk_ref[pl.ds(start, tk), :] += _dot(ds_ref[...], q_st, TN)
            dv_ref[pl.ds(start, tk), :] += _dot(p_ref[...], do_st, TN)
            dq_acc[...] += _dot(ds_ref[...], kb, NN)

        n_full = (i * tq) // tk

        def full_chunk(kc, _):
            chunk(kc, False)
            return 0

        lax.fori_loop(0, n_full, full_chunk, 0)
        chunk(n_full, True)
        dq_ref[...] = jnp.where(lo, dq_acc[pl.ds(0, tq), :], dq_acc[pl.ds(tq, tq), :]) * scale

    row = lambda off: pl.BlockSpec((None, nk, 1, tk), lambda h, i: (2 * h + off, 0, 0, 0))
    lspec = pl.BlockSpec((None, tq, 1), lambda h, i: (h, i, 0))
    qspec = pl.BlockSpec((tq, LANES), lambda h, i: (i, h))
    full = pl.BlockSpec((T, LANES), lambda h, i: (0, h))
    dcspec = pl.BlockSpec((None, nk, 1, tk), lambda h, i: (h, 0, 0, 0))
    grad = jax.ShapeDtypeStruct((T, Dh), F32)
    dc = jax.ShapeDtypeStruct((HP, nk, 1, tk), F32)
    dr = jax.ShapeDtypeStruct((HP, T, 1), F32)
    scratch = [pltpu.VMEM((2 * tq, tk), F32), pltpu.VMEM((2 * tq, tk), F32), pltpu.VMEM((2 * tq, tk), BF16), pltpu.VMEM((2 * tq, tk), BF16),
               pltpu.VMEM((2 * tq, LANES), F32), pltpu.VMEM((2 * tq, 1), F32)]
    return _call(
        body, name=name, grid=(HP, T // tq), out_shape=(grad, grad, grad, dc, dc, dr, dr),
        in_specs=[qspec, pl.BlockSpec((T, LANES), lambda h, i: (0, HP + h)), full, qspec, qspec, row(0), row(1), lspec, lspec],
        out_specs=(qspec, full, full, dcspec, dcspec, lspec, lspec),
        args=[qk, qk, v, o, do, crow, crow, lse_a, lse_b], scratch_shapes=scratch, carry=carry)


SWA_GROUP = 2
SWA_GROUP_BWD = 4


def _swa_block(n, q_ref, k_ref):
    qs = pl.multiple_of(n * WINDOW, WINDOW)
    ks = pl.multiple_of(jnp.maximum(n - 1, 0) * WINDOW, WINDOW)
    rel = (qs + lax.broadcasted_iota(jnp.int32, (WINDOW, 2 * WINDOW), 0)) - (ks + lax.broadcasted_iota(jnp.int32, (WINDOW, 2 * WINDOW), 1))
    valid = jnp.logical_and(rel >= 0, rel < WINDOW)
    return qs, ks, valid


def _swa_fwd(name, q, kd, vd, sinks, carry=None):
    T, Dh = q.shape
    HP = Dh // LANES

    def body(q_ref, k_ref, v_ref, sa_ref, sb_ref, o_ref, la_ref, lb_ref):
        lo = _lane_lo((WINDOW, LANES))

        def block(n, _):
            qs, ks, valid = _swa_block(n, q_ref, k_ref)
            q2 = q_ref[pl.ds(qs, WINDOW), :]
            kb = k_ref[pl.ds(ks, 2 * WINDOW), :]
            vb = v_ref[pl.ds(ks, 2 * WINDOW), :]
            res = []
            for sel, s_ref in ((lo, sa_ref), (jnp.logical_not(lo), sb_ref)):
                qm = _keep(sel, q2)
                sink = s_ref[...]
                s = jnp.where(valid, _dot(qm, kb, NT), NEG)
                m = jnp.maximum(jnp.max(s, axis=-1, keepdims=True), sink)
                p = jnp.exp(s - m)
                l = jnp.sum(p, axis=-1, keepdims=True) + jnp.exp(sink - m)
                res.append((_dot(p.astype(BF16), vb, NN) / l, m + jnp.log(l)))
            o_ref[pl.ds(qs, WINDOW), :] = jnp.where(lo, res[0][0], res[1][0])
            la_ref[pl.ds(qs, WINDOW), :] = res[0][1]
            lb_ref[pl.ds(qs, WINDOW), :] = res[1][1]
            return 0

        assert (T // WINDOW) % SWA_GROUP == 0

        def group(g, c):
            for b in range(SWA_GROUP):
                c = block(g * SWA_GROUP + b, c)
            return c

        lax.fori_loop(0, T // WINDOW // SWA_GROUP, group, 0)

    full = pl.BlockSpec((T, LANES), lambda h: (0, h))
    kv = pl.BlockSpec((T, LANES), lambda h: (0, h // 2))
    sink = lambda off: pl.BlockSpec((None, 1, 1), lambda h: (2 * h + off, 0, 0))
    lse = jax.ShapeDtypeStruct((HP, T, 1), F32)
    lspec = pl.BlockSpec((None, T, 1), lambda h: (h, 0, 0))
    return _call(
        body, name=name, grid=(HP,), out_shape=(jax.ShapeDtypeStruct((T, Dh), F32), lse, lse),
        in_specs=[full, kv, kv, sink(0), sink(1)], out_specs=(full, lspec, lspec),
        args=[q, kd, vd, sinks, sinks], carry=carry)


def _swa_bwd(name, q, kd, vd, sinks, o, do, lse_a, lse_b, carry=None):
    T, Dh = q.shape
    HP = Dh // LANES
    scale = HEAD_DIM ** -0.5

    def body(q_ref, k_ref, v_ref, sa_ref, sb_ref, o_ref, do_ref, la_ref, lb_ref, dq_ref, dk_ref, dv_ref, dsa_ref, dsb_ref):
        lo = _lane_lo((WINDOW, LANES))
        hi = jnp.logical_not(lo)
        dk_ref[...] = jnp.zeros_like(dk_ref)
        dv_ref[...] = jnp.zeros_like(dv_ref)

        def block(n, dsinks):
            qs, ks, valid = _swa_block(n, q_ref, k_ref)
            q2 = q_ref[pl.ds(qs, WINDOW), :]
            do2 = do_ref[pl.ds(qs, WINDOW), :]
            kb = k_ref[pl.ds(ks, 2 * WINDOW), :]
            vb = v_ref[pl.ds(ks, 2 * WINDOW), :]
            prod = do2.astype(F32) * o_ref[pl.ds(qs, WINDOW), :]
            dqs, new = [], []
            dk = jnp.zeros((2 * WINDOW, LANES), F32)
            dv = jnp.zeros((2 * WINDOW, LANES), F32)
            for sel, s_ref, l_ref, dsink in ((lo, sa_ref, la_ref, dsinks[0]), (hi, sb_ref, lb_ref, dsinks[1])):
                qm = _keep(sel, q2)
                dom = _keep(sel, do2)
                dsum = jnp.sum(jnp.where(sel, prod, 0.0), axis=-1, keepdims=True)
                lse = l_ref[pl.ds(qs, WINDOW), :]
                s = jnp.where(valid, _dot(qm, kb, NT), NEG)
                p = jnp.exp(s - lse)
                ds = p * (_dot(dom, vb, NT) - dsum)
                dsb = ds.astype(BF16)
                dqs.append(_dot(dsb, kb, NN))
                dk = dk + _dot(dsb, qm, TN)
                dv = dv + _dot(p.astype(BF16), dom, TN)
                new.append(dsink - jnp.sum(jnp.exp(s_ref[...] - lse) * dsum, axis=0, keepdims=True))
            dq_ref[pl.ds(qs, WINDOW), :] = jnp.where(lo, dqs[0], dqs[1]) * scale
            dk_ref[pl.ds(ks, 2 * WINDOW), :] += dk
            dv_ref[pl.ds(ks, 2 * WINDOW), :] += dv
            return tuple(new)

        assert (T // WINDOW) % SWA_GROUP_BWD == 0

        def group(g, c):
            for b in range(SWA_GROUP_BWD):
                c = block(g * SWA_GROUP_BWD + b, c)
            return c

        dsa, dsb_ = lax.fori_loop(0, T // WINDOW // SWA_GROUP_BWD, group, (jnp.zeros((1, 1), F32), jnp.zeros((1, 1), F32)))
        dsa_ref[...] = dsa
        dsb_ref[...] = dsb_

    full = pl.BlockSpec((T, LANES), lambda h: (0, h))
    kv = pl.BlockSpec((T, LANES), lambda h: (0, h // 2))
    sink = lambda off: pl.BlockSpec((None, 1, 1), lambda h: (2 * h + off, 0, 0))
    lspec = pl.BlockSpec((None, T, 1), lambda h: (h, 0, 0))
    dsink = pl.BlockSpec((None, 1, 1), lambda h: (h, 0, 0))
    grad = jax.ShapeDtypeStruct((T, Dh), F32)
    ds_shape = jax.ShapeDtypeStruct((HP, 1, 1), F32)
    return _call(
        body, name=name, grid=(HP,), out_shape=(grad, grad, grad, ds_shape, ds_shape),
        in_specs=[full, kv, kv, sink(0), sink(1), full, full, lspec, lspec],
        out_specs=(full, full, full, dsink, dsink),
        args=[q, kd, vd, sinks, sinks, o, do, lse_a, lse_b], carry=carry)


def _place():
    return lax.axis_index("x"), lax.axis_index("y"), lax.axis_index("c")


def _run_carry(name, carry):
    c_in, c_out = len(carry.inputs), len(carry.out_shapes)

    def body(*refs):
        ins, outs, scr = refs[:c_in], refs[c_in:c_in + c_out], refs[c_in + c_out:]
        carry.start(ins, outs, scr)
        carry.middle(ins, outs, scr)
        carry.finish(ins, outs, scr)

    return pl.pallas_call(
        body, out_shape=tuple(carry.out_shapes), in_specs=[_HBM] * c_in, out_specs=tuple([_HBM] * c_out),
        scratch_shapes=carry.scratch, name=name)(*carry.inputs)


def _gather_carry(shards):
    n = len(shards)

    def plan(ins, outs, scr):
        send, recv, local = scr
        x, y, c = _place()
        me, sibling = (x, y, c), (x, y, 1 - c)
        partner, other, diag = (x ^ c, y ^ (1 - c)), (x ^ (1 - c), y ^ c), (1 - x, 1 - y)

        def copy(w, k, block, to, src=None):
            slot = 4 * block[0] + 2 * block[1] + block[2]
            return pltpu.make_async_remote_copy(
                src_ref=outs[w].at[slot] if src is None else src, dst_ref=outs[w].at[slot],
                send_sem=send.at[w, k], recv_sem=recv.at[w, k], device_id=to, device_id_type=MESH)

        own = [pltpu.make_async_copy(ins[w], outs[w].at[4 * x + 2 * y + c], local.at[w]) for w in range(n)]
        return copy, own, me, sibling, partner, other, diag, c

    def start(ins, outs, scr):
        copy, own, me, sibling, partner, other, _, c = plan(ins, outs, scr)
        for cp in own:
            cp.start()
        for w in range(n):
            copy(w, 1, me, (*partner, c), src=ins[w]).start()
            copy(w, 2, me, (*other, c), src=ins[w]).start()
            copy(w, 0, me, sibling, src=ins[w]).start()

    def middle(ins, outs, scr):
        copy, _, me, sibling, partner, other, _, c = plan(ins, outs, scr)
        for w in range(n):
            copy(w, 1, (*partner, c), me).wait_recv()
            copy(w, 3, (*partner, c), (*other, c)).start()
            copy(w, 4, (*partner, c), sibling).start()
        for w in range(n):
            copy(w, 2, (*other, c), me).wait_recv()
            copy(w, 5, (*other, c), sibling).start()

    def finish(ins, outs, scr):
        copy, own, me, sibling, partner, other, diag, c = plan(ins, outs, scr)
        for w in range(n):
            copy(w, 3, (*diag, c), me).wait_recv()
            copy(w, 6, (*diag, c), sibling).start()
        for w in range(n):
            copy(w, 0, sibling, me).wait_recv()
            copy(w, 4, (*other, 1 - c), me).wait_recv()
            copy(w, 5, (*partner, 1 - c), me).wait_recv()
            copy(w, 6, (*diag, 1 - c), me).wait_recv()
        for w in range(n):
            sent = [copy(w, 0, me, sibling, src=ins[w]), copy(w, 1, me, (*partner, c), src=ins[w]), copy(w, 2, me, (*other, c), src=ins[w]),
                    copy(w, 3, (*partner, c), (*other, c)), copy(w, 4, (*partner, c), sibling), copy(w, 5, (*other, c), sibling),
                    copy(w, 6, (*diag, c), sibling)]
            for cp in sent:
                cp.wait_send()
        for cp in own:
            cp.wait()

    return _Carry(shards, [jax.ShapeDtypeStruct((N_DEV,) + s.shape, s.dtype) for s in shards],
                  [pltpu.SemaphoreType.DMA((n, 7)), pltpu.SemaphoreType.DMA((n, 7)), pltpu.SemaphoreType.DMA((n,))], start, finish, middle)


def _sibling_carry(grads):
    n = len(grads)

    def copies(ins, outs, scr):
        send, recv = scr
        x, y, c = _place()
        return [pltpu.make_async_remote_copy(
            src_ref=ins[w].at[2 * q + (1 - c)], dst_ref=outs[w].at[q], send_sem=send.at[w, q], recv_sem=recv.at[w, q],
            device_id=(x, y, 1 - c), device_id_type=MESH) for w in range(n) for q in range(4)]

    def start(ins, outs, scr):
        for cp in copies(ins, outs, scr):
            cp.start()

    def finish(ins, outs, scr):
        for cp in copies(ins, outs, scr):
            cp.wait()

    return _Carry(grads, [jax.ShapeDtypeStruct((4,) + g.shape[1:], g.dtype) for g in grads],
                  [pltpu.SemaphoreType.DMA((n, 4)), pltpu.SemaphoreType.DMA((n, 4))], start, finish)


def _to_partner_carry(sums):
    n = len(sums)

    def copies(ins, outs, scr):
        send, recv = scr
        x, y, c = _place()
        partner, diag = (x ^ c, y ^ (1 - c)), (1 - x, 1 - y)
        cps = []
        for w in range(n):
            for k, chip in enumerate((partner, diag)):
                cps.append(pltpu.make_async_remote_copy(
                    src_ref=ins[w].at[2 * chip[0] + chip[1]], dst_ref=outs[2 * w + k], send_sem=send.at[w, k], recv_sem=recv.at[w, k],
                    device_id=(*partner, c), device_id_type=MESH))
        return cps

    def start(ins, outs, scr):
        for cp in copies(ins, outs, scr):
            cp.start()

    def finish(ins, outs, scr):
        for cp in copies(ins, outs, scr):
            cp.wait()

    return _Carry(sums, [jax.ShapeDtypeStruct(s.shape[1:], s.dtype) for s in sums for _ in range(2)],
                  [pltpu.SemaphoreType.DMA((n, 2)), pltpu.SemaphoreType.DMA((n, 2))], start, finish)


def _to_other_carry(blocks):
    n = len(blocks)

    def copies(ins, outs, scr):
        send, recv = scr
        x, y, c = _place()
        return [pltpu.make_async_remote_copy(
            src_ref=ins[w], dst_ref=outs[w], send_sem=send.at[w], recv_sem=recv.at[w],
            device_id=(x ^ (1 - c), y ^ c, c), device_id_type=MESH) for w in range(n)]

    def start(ins, outs, scr):
        for cp in copies(ins, outs, scr):
            cp.start()

    def finish(ins, outs, scr):
        for cp in copies(ins, outs, scr):
            cp.wait()

    return _Carry(blocks, [jax.ShapeDtypeStruct(b.shape, b.dtype) for b in blocks],
                  [pltpu.SemaphoreType.DMA((n,)), pltpu.SemaphoreType.DMA((n,))], start, finish)


def _gather_small(packed):
    R, C = packed.shape

    def body(in_ref, out_ref, send, recv):
        x, y, c = _place()
        mine = 4 * x + 2 * y + c
        out_ref[mine] = in_ref[...]
        copies = []
        for k in range(1, N_DEV):
            peer = (x ^ (k >> 2), y ^ ((k >> 1) & 1), c ^ (k & 1))
            copies.append(pltpu.make_async_remote_copy(
                src_ref=in_ref, dst_ref=out_ref.at[mine], send_sem=send.at[k - 1], recv_sem=recv.at[k - 1],
                device_id=peer, device_id_type=MESH))
        for cp in copies:
            cp.start()
        for cp in copies:
            cp.wait()

    vmem = pl.BlockSpec(memory_space=pltpu.VMEM)
    return pl.pallas_call(
        body, out_shape=jax.ShapeDtypeStruct((N_DEV, R, C), F32), in_specs=[vmem], out_specs=vmem,
        scratch_shapes=[pltpu.SemaphoreType.DMA((N_DEV - 1,)), pltpu.SemaphoreType.DMA((N_DEV - 1,))],
        name="small_grads_all_gather")(packed)


def _adamw(w, g, m, v):
    m = ADAM_B1 * m + (1.0 - ADAM_B1) * g
    v = ADAM_B2 * v + (1.0 - ADAM_B2) * (g * g)
    m_hat = m / (1.0 - ADAM_B1 ** ADAM_STEP)
    v_hat = v / (1.0 - ADAM_B2 ** ADAM_STEP)
    delta = -ADAM_LR * (m_hat / (jnp.sqrt(v_hat) + ADAM_EPS) + ADAM_WD * w)
    return delta, m, v


def _pair_add(name, grads, received, c_idx):
    _, R, C = grads.shape
    tr = _row_tile(R)

    def body(c_ref, g_ref, r_ref, o_ref):
        o_ref[...] = (g_ref[...].astype(F32) + r_ref[...].astype(F32)).astype(BF16)

    blk = pl.BlockSpec((None, tr, C), lambda q, i, c: (q, i, 0))
    return pl.pallas_call(
        body, out_shape=jax.ShapeDtypeStruct((4, R, C), BF16),
        grid_spec=pltpu.PrefetchScalarGridSpec(
            num_scalar_prefetch=1, grid=(4, R // tr),
            in_specs=[pl.BlockSpec((None, tr, C), lambda q, i, c: (2 * q + c[0], i, 0)), blk], out_specs=blk),
        name=name, compiler_params=_params(2))(c_idx, grads, received)


def _relay_add(name, sums, relayed, other_idx):
    _, R, C = sums.shape
    tr = _row_tile(R)

    def body(q_ref, s_ref, r_ref, o_ref):
        o_ref[...] = (s_ref[...].astype(F32) + r_ref[...].astype(F32)).astype(BF16)

    blk = pl.BlockSpec((tr, C), lambda i, q: (i, 0))
    return pl.pallas_call(
        body, out_shape=jax.ShapeDtypeStruct((R, C), BF16),
        grid_spec=pltpu.PrefetchScalarGridSpec(
            num_scalar_prefetch=1, grid=(R // tr,),
            in_specs=[pl.BlockSpec((None, tr, C), lambda i, q: (q[0], i, 0)), blk], out_specs=blk),
        name=name, compiler_params=_params(1))(other_idx, sums, relayed)


def _adam_shard(name, sums, received, w, m, v, chip_idx):
    R, C = w.shape
    tr = _row_tile(R, 128)
    tc = C if tr < R or C % (2 * LANES) else 2 * LANES

    def body(q_ref, s_ref, ra_ref, rb_ref, w_ref, m_ref, v_ref, g_out, d_out, m_out, v_out):
        g = s_ref[...].astype(F32) + ra_ref[...].astype(F32) + rb_ref[...].astype(F32)
        delta, mn, vn = _adamw(w_ref[...], g, m_ref[...], v_ref[...])
        g_out[...] = g
        d_out[...] = delta
        m_out[...] = mn
        v_out[...] = vn

    blk = pl.BlockSpec((tr, tc), lambda i, j, q: (i, j))
    shape = jax.ShapeDtypeStruct((R, C), F32)
    return pl.pallas_call(
        body, out_shape=(shape,) * 4,
        grid_spec=pltpu.PrefetchScalarGridSpec(
            num_scalar_prefetch=1, grid=(R // tr, C // tc),
            in_specs=[pl.BlockSpec((None, tr, tc), lambda i, j, q: (q[0], i, j)), blk, blk, blk, blk, blk],
            out_specs=(blk,) * 4),
        name=name, compiler_params=_params(2))(chip_idx, sums, received[0], received[1], w, m, v)


def _adam_small(name, gathered, w, m, v):
    R, C = w.shape

    def body(ga_ref, w_ref, m_ref, v_ref, g_out, d_out, m_out, v_out):
        g = ga_ref[0]
        for d in range(1, N_DEV):
            g = g + ga_ref[d]
        delta, mn, vn = _adamw(w_ref[...], g, m_ref[...], v_ref[...])
        g_out[...] = g
        d_out[...] = delta
        m_out[...] = mn
        v_out[...] = vn

    full = pl.BlockSpec((R, C), lambda i: (0, 0))
    shape = jax.ShapeDtypeStruct((R, C), F32)
    return pl.pallas_call(
        body, out_shape=(shape,) * 4, grid=(1,),
        in_specs=[pl.BlockSpec((N_DEV, R, C), lambda i: (0, 0, 0)), full, full, full], out_specs=(full,) * 4,
        name=name, compiler_params=_params(1))(gathered, w, m, v)


def _pack_small(parts, D):
    g1, gmix, g2, gof, gos, bf, gqf, gkf, gqs, gks, sinks = [p.reshape(-1).astype(F32) for p in parts]
    row3 = jnp.concatenate([gof, gos])
    row4 = jnp.zeros((D,), F32)
    for slot, vec in enumerate((bf, gqf, gkf, gqs, gks, sinks)):
        row4 = lax.dynamic_update_slice(row4, vec, (slot * LANES,))
    zero = jnp.zeros((D,), F32)
    return jnp.stack([g1, gmix, g2, row3, row4, zero, zero, zero])


def _unpack_small(packed, D, H):
    Dh = D // 2
    row4 = packed[4]
    short = [row4[s * LANES:s * LANES + n] for s, n in enumerate((H, HEAD_DIM, HEAD_DIM, HEAD_DIM, HEAD_DIM, H))]
    vecs = [packed[0], packed[1], packed[2], packed[3, :Dh], packed[3, Dh:]] + short
    return [v[None, :] for v in vecs]


def kernel(x, positions, norm_ffn1_g, ffn1_w_gate, ffn1_w_up, ffn1_w_down, norm_mix_g, w_in, b_forget, fox_q_norm_g, fox_k_norm_g, swa_q_norm_g, swa_k_norm_g, swa_sinks, out_norm_fox_g, out_norm_swa_g, w_out, norm_ffn2_g, ffn2_w_gate, ffn2_w_up, ffn2_w_down, loss_target, m_norm_ffn1_g, m_ffn1_w_gate, m_ffn1_w_up, m_ffn1_w_down, m_norm_mix_g, m_w_in, m_b_forget, m_fox_q_norm_g, m_fox_k_norm_g, m_swa_q_norm_g, m_swa_k_norm_g, m_swa_sinks, m_out_norm_fox_g, m_out_norm_swa_g, m_w_out, m_norm_ffn2_g, m_ffn2_w_gate, m_ffn2_w_up, m_ffn2_w_down, v_norm_ffn1_g, v_ffn1_w_gate, v_ffn1_w_up, v_ffn1_w_down, v_norm_mix_g, v_w_in, v_b_forget, v_fox_q_norm_g, v_fox_k_norm_g, v_swa_q_norm_g, v_swa_k_norm_g, v_swa_sinks, v_out_norm_fox_g, v_out_norm_swa_g, v_w_out, v_norm_ffn2_g, v_ffn2_w_gate, v_ffn2_w_up, v_ffn2_w_down):
    xs = x[0]
    target = loss_target[0]
    T, D = xs.shape
    Dh = D // 2
    H = Dh // HEAD_DIM
    HP = H // 2
    KVW = (H // GQA_GROUP) * HEAD_DIM
    KVB = KVW // LANES
    MAIN = 4 * Dh + 2 * KVW
    F_OFF = 3 * Dh
    tm = min(ROW_TILE_CAP, T)
    tq = min(512, T)
    tk = min(512, T)
    nk = T // tk
    cx, cy, cc = _place()
    c_idx = jnp.reshape(cc, (1,)).astype(jnp.int32)
    chip_idx = jnp.reshape(2 * cx + cy, (1,)).astype(jnp.int32)
    other_idx = jnp.reshape(2 * (cx ^ (1 - cc)) + (cy ^ cc), (1,)).astype(jnp.int32)

    tr = jnp.transpose
    big_w = [tr(ffn1_w_gate[0]), tr(ffn1_w_up[0]), ffn1_w_down[0], tr(w_in[0]), w_out[0], tr(ffn2_w_gate[0]), tr(ffn2_w_up[0]),
             ffn2_w_down[0]]
    big_m = [tr(m_ffn1_w_gate[0]), tr(m_ffn1_w_up[0]), m_ffn1_w_down[0], tr(m_w_in[0]), m_w_out[0], tr(m_ffn2_w_gate[0]),
             tr(m_ffn2_w_up[0]), m_ffn2_w_down[0]]
    big_v = [tr(v_ffn1_w_gate[0]), tr(v_ffn1_w_up[0]), v_ffn1_w_down[0], tr(v_w_in[0]), v_w_out[0], tr(v_ffn2_w_gate[0]),
             tr(v_ffn2_w_up[0]), v_ffn2_w_down[0]]
    transposed = {"ffn1_w_gate", "ffn1_w_up", "w_in", "ffn2_w_gate", "ffn2_w_up"}
    names = ["ffn1_w_gate", "ffn1_w_up", "ffn1_w_down", "w_in", "w_out", "ffn2_w_gate", "ffn2_w_up", "ffn2_w_down"]
    sh = dict(zip(names, [w.astype(BF16) for w in big_w]))
    lane = jnp.arange(LANES)
    inv_freq = ROPE_THETA ** (-(2.0 * (lane % (HEAD_DIM // 2))).astype(F32) / HEAD_DIM)
    ang = positions[0].astype(F32)[:, None] * inv_freq[None, :]
    cos_t = jnp.cos(ang)
    sin_t = jnp.where((lane & (HEAD_DIM // 2)) == 0, -1.0, 1.0)[None, :] * jnp.sin(ang)
    rope = (cos_t, sin_t)

    def pair_gain(g, blocks):
        return jnp.tile(jnp.concatenate([g[0], g[0]])[None, None, :], (blocks, 1, 1))

    n1, (wg1,) = _rmsnorm_fwd("ffn1_norm", xs, norm_ffn1_g, tm, carry=_gather_carry([sh["ffn1_w_gate"]]))
    a1, (wu1,) = _ffn_gate("ffn1_gate", n1, wg1, tm, carry=_gather_carry([sh["ffn1_w_up"]]))
    (b1, hm1), (wd1,) = _ffn_up_only("ffn1_up", n1, wu1, a1, tm, carry=_gather_carry([sh["ffn1_w_down"]]))
    h1, (win_g,) = _ffn_down("ffn1_down", hm1, wd1, xs, tm, D, carry=_gather_carry([sh["w_in"]]))
    n_in = win_g.shape[1]
    win_t = win_g.reshape(N_DEV * n_in, D)
    win_main = jnp.concatenate([win_t[:F_OFF], win_t[F_OFF + H:]], axis=0)
    win_f = jnp.pad(win_t[F_OFF:F_OFF + H], ((0, LANES - H), (0, 0)))

    u = _rmsnorm_fwd("mix_norm", h1, norm_mix_g, tm)
    proj, (wout_g,) = _mm("mix_proj", u, win_main, tm, MAIN // 9, dims=NT, carry=_gather_carry([sh["w_out"]]))
    wout = wout_g.reshape(D, D)
    proj_f = _mm("mix_proj_forget", u, win_f, tm, LANES, dims=NT)
    scale = HEAD_DIM ** -0.5
    fox_gains = jnp.concatenate([pair_gain(fox_q_norm_g, HP), pair_gain(fox_k_norm_g, HP)])
    qk_f = _headnorm_fwd_scaled("fox_qk_norm", proj, 0, 2 * HP, fox_gains, T, scale, HP)
    v_f = proj[:, 2 * Dh:3 * Dh].astype(BF16)
    c_t, sg_t = _forget_fwd("forget_gates", proj_f[:, :H].T, b_forget.reshape(H, 1))
    crow = c_t.reshape(H, nk, 1, tk)
    (o_fox, lse_fa, lse_fb), (wg2, wu2) = _fox_fwd("fox_attention", qk_f, v_f, crow, tq, tk,
                                                   carry=_gather_carry([sh["ffn2_w_gate"], sh["ffn2_w_up"]]))

    swa_q_gains = pair_gain(swa_q_norm_g, HP)
    swa_k_gains = pair_gain(swa_k_norm_g, KVB)
    q_s = _headnorm_fwd("swa_q_norm", proj, 3 * HP, HP, swa_q_gains, T, scale, rope=rope)
    k_d = _headnorm_fwd("swa_k_norm", proj, 4 * HP, KVB, swa_k_gains, T, 1.0, rope=rope, dup=True)
    v_s = proj[:, 4 * Dh + KVW:].astype(BF16).reshape(T, H // GQA_GROUP, 1, HEAD_DIM)
    v_d = jnp.broadcast_to(v_s, (T, H // GQA_GROUP, 2, HEAD_DIM)).reshape(T, 2 * KVW)
    sinks3 = swa_sinks.reshape(H, 1, 1)
    o_swa, lse_sa, lse_sb = _swa_fwd("swa_attention", q_s, k_d, v_d, sinks3)

    on = _outnorm_fwd("out_norm", o_fox, o_swa, out_norm_fox_g, out_norm_swa_g, tm)
    h2 = _mm("mix_out", on, wout, tm, min(512, D), resid=h1)

    n2 = _rmsnorm_fwd("ffn2_norm", h2, norm_ffn2_g, tm)
    (a2, b2, hm2), (wd2,) = _ffn_up("ffn2_up", n2, wg2, wu2, tm, carry=_gather_carry([sh["ffn2_w_down"]]))
    y = _ffn_down("ffn2_down", hm2, wd2, h2, tm, D)
    dy, dyh, sq = _loss_grad("loss_grad", y, target, min(256, T))
    loss = lax.psum(0.5 * sq[0, 0] / D, ("x", "y", "c"))

    J, Fs, _ = wg2.shape
    aspec = pl.BlockSpec((None, tm, Fs), lambda i, j: (j, i, 0))
    wspec = pl.BlockSpec((None, Fs, D), lambda i, j: (j, 0, 0))
    got = {}

    def pair_sums(keys, grads, received):
        return [_pair_add("sum_" + nm, g, r, c_idx) for nm, g, r in zip(keys, grads, received)]

    def relay_sums(keys, sums, hop1):
        out = []
        for i, (nm, s) in enumerate(zip(keys, sums)):
            got[nm] = [hop1[2 * i]]
            out.append(_relay_add("relay_" + nm, s, hop1[2 * i + 1], other_idx))
        return out

    def arrived(keys, hop2):
        for nm, blk in zip(keys, hop2):
            got[nm].append(blk)

    dwd2 = _wgrad_down("ffn2_wgrad_down", hm2, dyh, min(1024, D))
    (da2, db2), (sib_d2,) = _ffn_bwd_mid("ffn2_bwd_mid", dyh, wd2, a2, b2, tm, carry=_sibling_carry([dwd2]))
    (sum_wd2,) = pair_sums(names[7:8], [dwd2], [sib_d2])
    (dwg2, dwu2), hop1 = _wgrad_up("ffn2_wgrad_up", n2, da2, db2, min(1024, D), carry=_to_partner_carry([sum_wd2]))
    (t_wd2,) = relay_sums(names[7:8], [sum_wd2], hop1)
    dn2, (via_wd2, *sib2) = _reduce_mm("ffn2_bwd_in", [(da2, aspec, wg2, wspec), (db2, aspec, wu2, wspec)], [], NN, T, D, tm, J,
                                       carry=_join(_to_other_carry([t_wd2]), _sibling_carry([dwg2, dwu2])))
    arrived(names[7:8], [via_wd2])
    dh2, dg_ffn2, dh2b = _rmsnorm_bwd("ffn2_norm_bwd", dn2, h2, norm_ffn2_g, dy, min(256, T), 1.0)
    sum_wg2, sum_wu2 = pair_sums(names[5:7], [dwg2, dwu2], sib2)

    dwout = _wgrad_2d("mix_out_wgrad", on, dh2b, min(512, D), min(1024, D))
    dwout_g = dwout.reshape(N_DEV, D // N_DEV, D)
    do_fox, dg_of = _outnorm_bwd("out_norm_bwd_fox", dh2b, wout, 0, o_fox, out_norm_fox_g, tm)
    do_swa, dg_os = _outnorm_bwd("out_norm_bwd_swa", dh2b, wout, 1, o_swa, out_norm_swa_g, tm)

    (dq_f, dk_f, dv_f, dc_a, dc_b, dr_a, dr_b), (*hop1, sib_wout) = _fox_bwd(
        "fox_attention_bwd", qk_f, v_f, o_fox, do_fox, crow, lse_fa, lse_fb, tq, tk,
        carry=_join(_to_partner_carry([sum_wg2, sum_wu2]), _sibling_carry([dwout_g])))
    t_wg2, t_wu2 = relay_sums(names[5:7], [sum_wg2, sum_wu2], hop1)
    (sum_wout,) = pair_sums(names[4:5], [dwout_g], [sib_wout])
    dqk_f = jnp.concatenate([dq_f, dk_f], axis=1)
    dqk_raw, dg_fox = _headnorm_bwd("fox_qk_norm_bwd", dqk_f, proj, 0, 2 * HP, fox_gains, HP, T, 1.0)
    dct = jnp.stack([dc_a.reshape(HP, T), dc_b.reshape(HP, T)], axis=1).reshape(H, T)
    drt = jnp.stack([dr_a.reshape(HP, T), dr_b.reshape(HP, T)], axis=1).reshape(H, T)
    dz_t, db_f = _forget_bwd("forget_gates_bwd", dct, drt, sg_t)

    (dq_s, dk_p, dv_p, dsink_a, dsink_b), hop2 = _swa_bwd(
        "swa_attention_bwd", q_s, k_d, v_d, sinks3, o_swa, do_swa, lse_sa, lse_sb, carry=_to_other_carry([t_wg2, t_wu2]))
    arrived(names[5:7], hop2)
    dqs_raw, dg_sq = _headnorm_bwd("swa_q_norm_bwd", dq_s, proj, 3 * HP, HP, swa_q_gains, HP, T, 1.0, rope=rope)
    dks_raw, dg_sk = _headnorm_bwd("swa_k_norm_bwd", dk_p, proj, 4 * HP, KVB, swa_k_gains, KVB, T, 1.0, rope=rope, fold=True)
    dvs_raw, _ = _headnorm_bwd("swa_v_fold", dv_p, None, 0, KVB, None, KVB, T, 1.0, fold=True, norm=False)

    dproj = jnp.concatenate([dqk_raw, dv_f.astype(BF16), dqs_raw, dks_raw, dvs_raw], axis=1)
    dproj_f = jnp.pad(dz_t.T, ((0, 0), (0, LANES - H))).astype(BF16)
    dwin_main, hop1 = _wgrad_2d("mix_proj_wgrad", dproj, u, MAIN // 9, min(1024, D), carry=_to_partner_carry([sum_wout]))
    (t_wout,) = relay_sums(names[4:5], [sum_wout], hop1)
    dwin_f = _wgrad_2d("mix_proj_forget_wgrad", dproj_f, u, LANES, min(1024, D))
    dwin_t = jnp.concatenate([dwin_main[:F_OFF], dwin_f[:H], dwin_main[F_OFF:]], axis=0)
    dwin_g = dwin_t.reshape(N_DEV, n_in, D)
    tkb = MAIN // 9
    du, (via_wout, sib_win) = _reduce_mm(
        "mix_bwd_in",
        [(dproj, pl.BlockSpec((tm, tkb), lambda i, r: (i, r)), win_main, pl.BlockSpec((tkb, D), lambda i, r: (r, 0)))],
        [(dproj_f, pl.BlockSpec((tm, LANES), lambda i, r: (i, 0)), win_f, pl.BlockSpec((LANES, D), lambda i, r: (0, 0)))],
        NN, T, D, tm, 9, carry=_join(_to_other_carry([t_wout]), _sibling_carry([dwin_g])))
    arrived(names[4:5], [via_wout])
    dh1, dg_mix, dh1h = _rmsnorm_bwd("mix_norm_bwd", du, h1, norm_mix_g, dh2, min(256, T), 0.5)
    (sum_win,) = pair_sums(names[3:4], [dwin_g], [sib_win])

    dwd1, hop1 = _wgrad_down("ffn1_wgrad_down", hm1, dh1h, min(1024, D), carry=_to_partner_carry([sum_win]))
    (t_win,) = relay_sums(names[3:4], [sum_win], hop1)
    (da1, db1), (via_win, sib_d) = _ffn_bwd_mid("ffn1_bwd_mid", dh1h, wd1, a1, b1, tm,
                                                carry=_join(_to_other_carry([t_win]), _sibling_carry([dwd1])))
    arrived(names[3:4], [via_win])
    (sum_wd1,) = pair_sums(names[2:3], [dwd1], [sib_d])
    dwg1, hop1 = _wgrad_down("ffn1_wgrad_gate", da1, n1, min(1024, D), carry=_to_partner_carry([sum_wd1]))
    (t_wd1,) = relay_sums(names[2:3], [sum_wd1], hop1)
    dwu1, (via_wd1, sib_g) = _wgrad_down("ffn1_wgrad_up", db1, n1, min(1024, D),
                                         carry=_join(_to_other_carry([t_wd1]), _sibling_carry([dwg1])))
    arrived(names[2:3], [via_wd1])
    (sum_wg1,) = pair_sums(names[0:1], [dwg1], [sib_g])
    dn1_gate, (*hop1, sib_u) = _reduce_mm(
        "ffn1_bwd_in_gate", [(da1, aspec, wg1, wspec)], [], NN, T, D, tm, J,
        carry=_join(_to_partner_carry([sum_wg1]), _sibling_carry([dwu1])))
    (t_wg1,) = relay_sums(names[0:1], [sum_wg1], hop1)
    (sum_wu1,) = pair_sums(names[1:2], [dwu1], [sib_u])
    dn1, (via_wg1, *hop1) = _reduce_mm(
        "ffn1_bwd_in_up", [(db1, aspec, wu1, wspec)], [], NN, T, D, tm, J, init=dn1_gate,
        carry=_join(_to_other_carry([t_wg1]), _to_partner_carry([sum_wu1])))
    arrived(names[0:1], [via_wg1])
    (t_wu1,) = relay_sums(names[1:2], [sum_wu1], hop1)
    arrived(names[1:2], _run_carry("grads_exchange", _to_other_carry([t_wu1])))
    dx, dg_ffn1 = _rmsnorm_bwd("ffn1_norm_bwd", dn1, xs, norm_ffn1_g, dh1, min(256, T), None)

    chip_sums = [sum_wg1, sum_wu1, sum_wd1, sum_win, sum_wout, sum_wg2, sum_wu2, sum_wd2]
    big_out = [_adam_shard("adam_" + nm, s, got[nm], w, m, v, chip_idx)
               for nm, s, w, m, v in zip(names, chip_sums, big_w, big_m, big_v)]

    dsinks = jnp.stack([dsink_a.reshape(HP), dsink_b.reshape(HP)], axis=1).reshape(H)
    small_g = [dg_ffn1, dg_mix, dg_ffn2, dg_of, dg_os, db_f, dg_fox[0, 0, :HEAD_DIM], dg_fox[1, 0, :HEAD_DIM],
               dg_sq[0, 0, :HEAD_DIM], dg_sk[0, 0, :HEAD_DIM], dsinks]
    small_w = [norm_ffn1_g, norm_mix_g, norm_ffn2_g, out_norm_fox_g, out_norm_swa_g, b_forget, fox_q_norm_g, fox_k_norm_g,
               swa_q_norm_g, swa_k_norm_g, swa_sinks]
    small_m = [m_norm_ffn1_g, m_norm_mix_g, m_norm_ffn2_g, m_out_norm_fox_g, m_out_norm_swa_g, m_b_forget, m_fox_q_norm_g,
               m_fox_k_norm_g, m_swa_q_norm_g, m_swa_k_norm_g, m_swa_sinks]
    small_v = [v_norm_ffn1_g, v_norm_mix_g, v_norm_ffn2_g, v_out_norm_fox_g, v_out_norm_swa_g, v_b_forget, v_fox_q_norm_g,
               v_fox_k_norm_g, v_swa_q_norm_g, v_swa_k_norm_g, v_swa_sinks]
    gathered = _gather_small(_pack_small(small_g, D))
    small_out = _adam_small("adam_small", gathered, _pack_small(small_w, D), _pack_small(small_m, D), _pack_small(small_v, D))
    small_out = [_unpack_small(p, D, H) for p in small_out]

    order = ["norm_ffn1_g", "ffn1_w_gate", "ffn1_w_up", "ffn1_w_down", "norm_mix_g", "w_in", "b_forget", "fox_q_norm_g", "fox_k_norm_g",
             "swa_q_norm_g", "swa_k_norm_g", "swa_sinks", "out_norm_fox_g", "out_norm_swa_g", "w_out", "norm_ffn2_g",
             "ffn2_w_gate", "ffn2_w_up", "ffn2_w_down"]
    small_names = ["norm_ffn1_g", "norm_mix_g", "norm_ffn2_g", "out_norm_fox_g", "out_norm_swa_g", "b_forget", "fox_q_norm_g",
                   "fox_k_norm_g", "swa_q_norm_g", "swa_k_norm_g", "swa_sinks"]
    result = [loss, dx[None]]
    for kind in range(4):
        for nm in order:
            if nm in names:
                leaf = big_out[names.index(nm)][kind]
                result.append((tr(leaf) if nm in transposed else leaf)[None])
            else:
                result.append(small_out[kind][small_names.index(nm)])
    return tuple(result)


def _headnorm_fwd_scaled(name, proj, col_off, ncb, gains, tm, scale, n_scaled):
    T = proj.shape[0]

    def body(x_ref, g_ref, o_ref):
        xv = x_ref[...]
        lo = _lane_lo(xv.shape)
        y = xv * _head_rstd(xv, lo) * g_ref[...]
        y = y * jnp.where(pl.program_id(0) < n_scaled, scale, 1.0)
        o_ref[...] = y.astype(BF16)

    return pl.pallas_call(
        body, out_shape=jax.ShapeDtypeStruct((T, ncb * LANES), BF16), grid=(ncb, T // tm),
        in_specs=[pl.BlockSpec((tm, LANES), lambda c, i: (i, col_off + c)), pl.BlockSpec((None, 1, LANES), lambda c, i: (c, 0, 0))],
        out_specs=pl.BlockSpec((tm, LANES), lambda c, i: (i, c)), name=name, compiler_params=_params(2))(proj, gains)
```

```python
import functools

import jax
import jax.numpy as jnp
from jax import lax
from jax.experimental import pallas as pl
from jax.experimental.pallas import tpu as pltpu

F32 = jnp.float32
BF16 = jnp.bfloat16

HEAD_DIM = 64
LANES = 128
WINDOW = 128
GQA_GROUP = 4
EPS = 1e-6
ROPE_THETA = 10000.0
ADAM_LR = 0.001
ADAM_B1 = 0.9
ADAM_B2 = 0.999
ADAM_EPS = 1e-08
ADAM_WD = 0.01
ADAM_STEP = 10
N_DEV = 8
NEG = -1e30
VMEM_LIMIT_V7X = 48 * 1024 * 1024
ROW_TILE_CAP = 512
MESH = pl.DeviceIdType.MESH

NN = (((1,), (0,)), ((), ()))
NT = (((1,), (1,)), ((), ()))
TN = (((0,), (0,)), ((), ()))


def _dot(a, b, dims):
    return lax.dot_general(a, b, dims, preferred_element_type=F32)


def _params(n_axes):
    return pltpu.CompilerParams(dimension_semantics=("arbitrary",) * n_axes, vmem_limit_bytes=VMEM_LIMIT_V7X)


def _row_tile(rows, cap=ROW_TILE_CAP):
    best = None
    for t in range(16, min(rows, cap) + 1, 16):
        if rows % t == 0:
            best = t
    return best or rows


def _lane_lo(shape):
    return lax.broadcasted_iota(jnp.int32, shape, len(shape) - 1) < HEAD_DIM


def _keep(sel, x):
    return jnp.where(sel, x.astype(F32), 0.0).astype(BF16)


_HBM = pl.BlockSpec(memory_space=pltpu.HBM)


class _Carry:
    def __init__(self, inputs, out_shapes, scratch, start, finish, middle=None):
        self.inputs, self.out_shapes, self.scratch = list(inputs), list(out_shapes), list(scratch)
        self.start, self.finish, self.middle = start, finish, middle or (lambda ins, outs, scr: None)


def _join(*carries):
    def hook(which):
        def run(ins, outs, scr):
            i = o = s = 0
            for c in carries:
                ni, no, ns = len(c.inputs), len(c.out_shapes), len(c.scratch)
                getattr(c, which)(ins[i:i + ni], outs[o:o + no], scr[s:s + ns])
                i, o, s = i + ni, o + no, s + ns
        return run

    return _Carry([a for c in carries for a in c.inputs], [a for c in carries for a in c.out_shapes],
                  [a for c in carries for a in c.scratch], hook("start"), hook("finish"), hook("middle"))


def _call(body, *, name, grid, in_specs, out_specs, out_shape, args, scratch_shapes=(), carry=None):
    params = _params(len(grid))
    if carry is None:
        return pl.pallas_call(body, out_shape=out_shape, grid=grid, in_specs=list(in_specs), out_specs=out_specs,
                              scratch_shapes=list(scratch_shapes), name=name, compiler_params=params)(*args)
    single = not isinstance(out_shape, (tuple, list))
    shapes = (out_shape,) if single else tuple(out_shape)
    specs = (out_specs,) if single else tuple(out_specs)
    n_in, n_out, n_scr = len(args), len(shapes), len(scratch_shapes)
    c_in, c_out = len(carry.inputs), len(carry.out_shapes)

    def wrapped(*refs):
        ins, c_ins = refs[:n_in], refs[n_in:n_in + c_in]
        o0 = n_in + c_in
        outs, c_outs = refs[o0:o0 + n_out], refs[o0 + n_out:o0 + n_out + c_out]
        s0 = o0 + n_out + c_out
        scr, c_scr = refs[s0:s0 + n_scr], refs[s0 + n_scr:]
        step, total = pl.program_id(0), grid[0]
        for ax in range(1, len(grid)):
            step, total = step * grid[ax] + pl.program_id(ax), total * grid[ax]

        @pl.when(step == 0)
        def _():
            carry.start(c_ins, c_outs, c_scr)

        @pl.when(step == total // 2)
        def _():
            carry.middle(c_ins, c_outs, c_scr)

        body(*ins, *outs, *scr)

        @pl.when(step == total - 1)
        def _():
            carry.finish(c_ins, c_outs, c_scr)

    res = pl.pallas_call(
        wrapped, out_shape=shapes + tuple(carry.out_shapes), grid=grid, in_specs=list(in_specs) + [_HBM] * c_in,
        out_specs=specs + (_HBM,) * c_out, scratch_shapes=list(scratch_shapes) + carry.scratch, name=name,
        compiler_params=params)(*args, *carry.inputs)
    main = res[:n_out]
    return (main[0] if single else tuple(main)), tuple(res[n_out:])


def _rms_bwd(dn, x, g):
    r = lax.rsqrt(jnp.mean(x * x, axis=-1, keepdims=True) + EPS)
    xh = x * r
    dxh = dn * g
    dx = r * (dxh - xh * jnp.mean(dxh * xh, axis=-1, keepdims=True))
    return dx, jnp.sum(dn * xh, axis=0, keepdims=True)


def _rmsnorm_fwd(name, x, g, tm, carry=None):
    T, D = x.shape

    def body(x_ref, g_ref, o_ref):
        xf = x_ref[...]
        r = lax.rsqrt(jnp.mean(xf * xf, axis=-1, keepdims=True) + EPS)
        o_ref[...] = (xf * r * g_ref[...]).astype(BF16)

    return _call(
        body, name=name, grid=(T // tm,), out_shape=jax.ShapeDtypeStruct((T, D), BF16),
        in_specs=[pl.BlockSpec((tm, D), lambda i: (i, 0)), pl.BlockSpec((1, D), lambda i: (0, 0))],
        out_specs=pl.BlockSpec((tm, D), lambda i: (i, 0)), args=[x, g], carry=carry)


def _outnorm_fwd(name, o_fox, o_swa, g_fox, g_swa, tm):
    T, Dh = o_fox.shape

    def body(a_ref, b_ref, ga_ref, gb_ref, o_ref):
        for ref, g_ref, lo in ((a_ref, ga_ref, 0), (b_ref, gb_ref, Dh)):
            xf = ref[...]
            r = lax.rsqrt(jnp.mean(xf * xf, axis=-1, keepdims=True) + EPS)
            o_ref[:, lo:lo + Dh] = (xf * r * g_ref[...]).astype(BF16)

    row = pl.BlockSpec((tm, Dh), lambda i: (i, 0))
    gain = pl.BlockSpec((1, Dh), lambda i: (0, 0))
    return pl.pallas_call(
        body, out_shape=jax.ShapeDtypeStruct((T, 2 * Dh), BF16), grid=(T // tm,),
        in_specs=[row, row, gain, gain], out_specs=pl.BlockSpec((tm, 2 * Dh), lambda i: (i, 0)),
        name=name, compiler_params=_params(1))(o_fox, o_swa, g_fox, g_swa)


def _outnorm_bwd(name, dhb, wout, half, o, g, tm):
    T, D = dhb.shape
    Dh = o.shape[1]

    def body(a_ref, w_ref, o_ref, g_ref, do_ref, dg_ref):
        don = _dot(a_ref[...], w_ref[...], NT)
        dx, dg = _rms_bwd(don, o_ref[...], g_ref[...])
        do_ref[...] = dx.astype(BF16)

        @pl.when(pl.program_id(0) == 0)
        def _():
            dg_ref[...] = dg

        @pl.when(pl.program_id(0) > 0)
        def _():
            dg_ref[...] += dg

    return pl.pallas_call(
        body, out_shape=(jax.ShapeDtypeStruct((T, Dh), BF16), jax.ShapeDtypeStruct((1, Dh), F32)), grid=(T // tm,),
        in_specs=[pl.BlockSpec((tm, D), lambda i: (i, 0)), pl.BlockSpec((Dh, D), lambda i: (half, 0)),
                  pl.BlockSpec((tm, Dh), lambda i: (i, 0)), pl.BlockSpec((1, Dh), lambda i: (0, 0))],
        out_specs=(pl.BlockSpec((tm, Dh), lambda i: (i, 0)), pl.BlockSpec((1, Dh), lambda i: (0, 0))),
        name=name, compiler_params=_params(1))(dhb, wout, o, g)


def _mm(name, a, b, tm, tn, dims=NN, resid=None, carry=None):
    M, K = a.shape
    transposed = dims == NT
    N = b.shape[0] if transposed else b.shape[1]

    def body(*refs):
        if resid is None:
            a_ref, b_ref, o_ref = refs
            o_ref[...] = _dot(a_ref[...], b_ref[...], dims)
        else:
            a_ref, b_ref, r_ref, o_ref = refs
            o_ref[...] = r_ref[...] + _dot(a_ref[...], b_ref[...], dims)

    ospec = pl.BlockSpec((tm, tn), lambda n, i: (i, n))
    bspec = pl.BlockSpec((tn, K), lambda n, i: (n, 0)) if transposed else pl.BlockSpec((K, tn), lambda n, i: (0, n))
    in_specs = [pl.BlockSpec((tm, K), lambda n, i: (i, 0)), bspec]
    args = [a, b]
    if resid is not None:
        in_specs.append(ospec)
        args.append(resid)
    return _call(body, name=name, grid=(N // tn, M // tm), in_specs=in_specs, out_specs=ospec,
                 out_shape=jax.ShapeDtypeStruct((M, N), F32), args=args, carry=carry)


def _wgrad_2d(name, a, b, tmm, tn, carry=None):
    T, M = a.shape
    N = b.shape[1]

    def body(a_ref, b_ref, o_ref):
        o_ref[...] = _dot(a_ref[...], b_ref[...], TN).astype(BF16)

    return _call(
        body, name=name, grid=(M // tmm, N // tn), out_shape=jax.ShapeDtypeStruct((M, N), BF16),
        in_specs=[pl.BlockSpec((T, tmm), lambda m, n: (0, m)), pl.BlockSpec((T, tn), lambda m, n: (0, n))],
        out_specs=pl.BlockSpec((tmm, tn), lambda m, n: (m, n)), args=[a, b], carry=carry)


def _wgrad_down(name, hm, df, tn, carry=None):
    J, T, Fs = hm.shape
    D = df.shape[1]

    def body(a_ref, b_ref, o_ref):
        o_ref[...] = _dot(a_ref[...], b_ref[...], TN).astype(BF16)

    return _call(
        body, name=name, grid=(J, D // tn), out_shape=jax.ShapeDtypeStruct((J, Fs, D), BF16),
        in_specs=[pl.BlockSpec((None, T, Fs), lambda j, n: (j, 0, 0)), pl.BlockSpec((T, tn), lambda j, n: (0, n))],
        out_specs=pl.BlockSpec((None, Fs, tn), lambda j, n: (j, 0, n)), args=[hm, df], carry=carry)


def _wgrad_up(name, n, da, db, tn, carry=None):
    T, D = n.shape
    J, _, Fs = da.shape

    def body(n_ref, da_ref, db_ref, og_ref, ou_ref):
        nv = n_ref[...]
        og_ref[...] = _dot(da_ref[...], nv, TN).astype(BF16)
        ou_ref[...] = _dot(db_ref[...], nv, TN).astype(BF16)

    act = pl.BlockSpec((None, T, Fs), lambda j, m: (j, 0, 0))
    out = pl.BlockSpec((None, Fs, tn), lambda j, m: (j, 0, m))
    shape = jax.ShapeDtypeStruct((J, Fs, D), BF16)
    return _call(
        body, name=name, grid=(J, D // tn), out_shape=(shape, shape),
        in_specs=[pl.BlockSpec((T, tn), lambda j, m: (0, m)), act, act], out_specs=(out, out),
        args=[n, da, db], carry=carry)


def _reduce_mm(name, pairs, once, dims, T, D, tm, steps, init=None, carry=None):
    n_pairs = len(pairs)
    n_once = len(once)
    n_mm = 2 * (n_pairs + n_once)

    def body(*refs):
        pr = refs[:2 * n_pairs]
        on = refs[2 * n_pairs:n_mm]
        o_ref, acc = refs[-2:]
        r = pl.program_id(1)

        @pl.when(r == 0)
        def _():
            acc[...] = jnp.zeros(acc.shape, F32) if init is None else refs[n_mm][...]

        for p in range(n_pairs):
            acc[...] += _dot(pr[2 * p][...], pr[2 * p + 1][...], dims)

        @pl.when(r == steps - 1)
        def _():
            dn = acc[...]
            for p in range(n_once):
                dn = dn + _dot(on[2 * p][...], on[2 * p + 1][...], dims)
            o_ref[...] = dn

    in_specs, args = [], []
    for a, a_spec, w, w_spec in list(pairs) + list(once):
        in_specs += [a_spec, w_spec]
        args += [a, w]
    row = pl.BlockSpec((tm, D), lambda i, r: (i, 0))
    if init is not None:
        in_specs.append(row)
        args.append(init)
    return _call(body, name=name, grid=(T // tm, steps), in_specs=in_specs, out_specs=row, out_shape=jax.ShapeDtypeStruct((T, D), F32),
                 args=args, scratch_shapes=[pltpu.VMEM((tm, D), F32)], carry=carry)


def _rmsnorm_bwd(name, dn, x, g, dh, tm, bf16_scale, carry=None):
    T, D = x.shape
    emit_bf16 = bf16_scale is not None

    def body(dn_ref, x_ref, g_ref, dh_ref, *outs):
        dxn, dg = _rms_bwd(dn_ref[...], x_ref[...], g_ref[...])
        dx = dh_ref[...] + dxn
        outs[0][...] = dx
        if emit_bf16:
            outs[2][...] = (bf16_scale * dx).astype(BF16)

        @pl.when(pl.program_id(0) == 0)
        def _():
            outs[1][...] = dg

        @pl.when(pl.program_id(0) > 0)
        def _():
            outs[1][...] += dg

    row = pl.BlockSpec((tm, D), lambda i: (i, 0))
    gain = pl.BlockSpec((1, D), lambda i: (0, 0))
    out_shape = [jax.ShapeDtypeStruct((T, D), F32), jax.ShapeDtypeStruct((1, D), F32)]
    out_specs = [row, gain]
    if emit_bf16:
        out_shape.append(jax.ShapeDtypeStruct((T, D), BF16))
        out_specs.append(row)
    return _call(body, name=name, grid=(T // tm,), in_specs=[row, row, gain, row], out_specs=tuple(out_specs),
                 out_shape=tuple(out_shape), args=[dn, x, g, dh], carry=carry)


def _loss_grad(name, y, target, tm):
    T, D = y.shape

    def body(y_ref, t_ref, dy_ref, dyh_ref, sq_ref):
        diff = y_ref[...] - t_ref[...]
        sq = jnp.sum(jnp.sum(diff * diff, axis=1, keepdims=True), axis=0, keepdims=True)
        dy = diff * (1.0 / D)
        dy_ref[...] = dy
        dyh_ref[...] = (0.5 * dy).astype(BF16)

        @pl.when(pl.program_id(0) == 0)
        def _():
            sq_ref[...] = sq

        @pl.when(pl.program_id(0) > 0)
        def _():
            sq_ref[...] += sq

    row = pl.BlockSpec((tm, D), lambda i: (i, 0))
    return pl.pallas_call(
        body, out_shape=(jax.ShapeDtypeStruct((T, D), F32), jax.ShapeDtypeStruct((T, D), BF16), jax.ShapeDtypeStruct((1, 1), F32)),
        grid=(T // tm,), in_specs=[row, row], out_specs=(row, row, pl.BlockSpec((1, 1), lambda i: (0, 0))),
        name=name, compiler_params=_params(1))(y, target)


def _ffn_up(name, n, wg, wu, tm, carry=None):
    T, D = n.shape
    J, Fs, _ = wg.shape

    def body(n_ref, wg_ref, wu_ref, a_ref, b_ref, h_ref):
        xv = n_ref[...]
        a = _dot(xv, wg_ref[...], NT)
        b = _dot(xv, wu_ref[...], NT)
        a_ref[...] = a.astype(BF16)
        b_ref[...] = b.astype(BF16)
        h_ref[...] = (a * jax.nn.sigmoid(a) * b).astype(BF16)

    act = jax.ShapeDtypeStruct((J, T, Fs), BF16)
    wspec = pl.BlockSpec((None, Fs, D), lambda j, i: (j, 0, 0))
    aspec = pl.BlockSpec((None, tm, Fs), lambda j, i: (j, i, 0))
    return _call(
        body, name=name, grid=(J, T // tm), out_shape=(act, act, act),
        in_specs=[pl.BlockSpec((tm, D), lambda j, i: (i, 0)), wspec, wspec], out_specs=(aspec, aspec, aspec),
        args=[n, wg, wu], carry=carry)


def _ffn_gate(name, n, wg, tm, carry=None):
    T, D = n.shape
    J, Fs, _ = wg.shape

    def body(n_ref, wg_ref, a_ref):
        a_ref[...] = _dot(n_ref[...], wg_ref[...], NT).astype(BF16)

    aspec = pl.BlockSpec((None, tm, Fs), lambda j, i: (j, i, 0))
    return _call(
        body, name=name, grid=(J, T // tm), out_shape=jax.ShapeDtypeStruct((J, T, Fs), BF16),
        in_specs=[pl.BlockSpec((tm, D), lambda j, i: (i, 0)), pl.BlockSpec((None, Fs, D), lambda j, i: (j, 0, 0))],
        out_specs=aspec, args=[n, wg], carry=carry)


def _ffn_up_only(name, n, wu, a, tm, carry=None):
    T, D = n.shape
    J, Fs, _ = wu.shape

    def body(n_ref, wu_ref, a_ref, b_ref, h_ref):
        b = _dot(n_ref[...], wu_ref[...], NT)
        a = a_ref[...].astype(F32)
        b_ref[...] = b.astype(BF16)
        h_ref[...] = (a * jax.nn.sigmoid(a) * b).astype(BF16)

    act = jax.ShapeDtypeStruct((J, T, Fs), BF16)
    aspec = pl.BlockSpec((None, tm, Fs), lambda j, i: (j, i, 0))
    return _call(
        body, name=name, grid=(J, T // tm), out_shape=(act, act),
        in_specs=[pl.BlockSpec((tm, D), lambda j, i: (i, 0)), pl.BlockSpec((None, Fs, D), lambda j, i: (j, 0, 0)), aspec],
        out_specs=(aspec, aspec), args=[n, wu, a], carry=carry)


def _ffn_down(name, hm, wd, resid, tm, tn, carry=None):
    J, T, Fs = hm.shape
    D = wd.shape[2]

    def body(h_ref, w_ref, r_ref, o_ref, acc):
        j = pl.program_id(2)

        @pl.when(j == 0)
        def _():
            acc[...] = jnp.zeros(acc.shape, F32)

        acc[...] += _dot(h_ref[...], w_ref[...], NN)

        @pl.when(j == J - 1)
        def _():
            o_ref[...] = r_ref[...] + 0.5 * acc[...]

    tile = pl.BlockSpec((tm, tn), lambda i, n, j: (i, n))
    return _call(
        body, name=name, grid=(T // tm, D // tn, J), out_shape=jax.ShapeDtypeStruct((T, D), F32),
        in_specs=[pl.BlockSpec((None, tm, Fs), lambda i, n, j: (j, i, 0)), pl.BlockSpec((None, Fs, tn), lambda i, n, j: (j, 0, n)), tile],
        out_specs=tile, scratch_shapes=[pltpu.VMEM((tm, tn), F32)], args=[hm, wd, resid], carry=carry)


def _ffn_bwd_mid(name, dfh, wd, a, b, tm, carry=None):
    T, D = dfh.shape
    J, Fs, _ = wd.shape

    def body(df_ref, w_ref, a_ref, b_ref, da_ref, db_ref):
        dhm = _dot(df_ref[...], w_ref[...], NT)
        av = a_ref[...].astype(F32)
        bv = b_ref[...].astype(F32)
        sg = jax.nn.sigmoid(av)
        da_ref[...] = (dhm * bv * (sg * (1.0 + av * (1.0 - sg)))).astype(BF16)
        db_ref[...] = (dhm * (av * sg)).astype(BF16)

    act = jax.ShapeDtypeStruct((J, T, Fs), BF16)
    aspec = pl.BlockSpec((None, tm, Fs), lambda j, i: (j, i, 0))
    return _call(
        body, name=name, grid=(J, T // tm), out_shape=(act, act),
        in_specs=[pl.BlockSpec((tm, D), lambda j, i: (i, 0)), pl.BlockSpec((None, Fs, D), lambda j, i: (j, 0, 0)), aspec, aspec],
        out_specs=(aspec, aspec), args=[dfh, wd, a, b], carry=carry)


def _rot_half(y, lane):
    first = (lane & (HEAD_DIM // 2)) == 0
    return jnp.where(first, pltpu.roll(y, LANES - HEAD_DIM // 2, 1), pltpu.roll(y, HEAD_DIM // 2, 1))


def _head_rstd(x, lo):
    sq = x * x
    ss_a = jnp.sum(jnp.where(lo, sq, 0.0), axis=-1, keepdims=True)
    ss_b = jnp.sum(jnp.where(lo, 0.0, sq), axis=-1, keepdims=True)
    return lax.rsqrt(jnp.where(lo, ss_a, ss_b) * (1.0 / HEAD_DIM) + EPS)


def _headnorm_fwd(name, proj, col_off, ncb, gains, tm, scale, rope=None, dup=False):
    T = proj.shape[0]
    with_rope = rope is not None
    width = 2 * LANES if dup else LANES

    def body(*refs):
        if with_rope:
            x_ref, g_ref, cos_ref, sin_ref, o_ref = refs
        else:
            x_ref, g_ref, o_ref = refs
        xv = x_ref[...]
        lane = lax.broadcasted_iota(jnp.int32, xv.shape, 1)
        lo = lane < HEAD_DIM
        y = xv * _head_rstd(xv, lo) * g_ref[...]
        if with_rope:
            y = y * cos_ref[...] + _rot_half(y, lane) * sin_ref[...]
        y = y * scale
        if dup:
            sw = pltpu.roll(y, HEAD_DIM, 1)
            o_ref[:, :LANES] = jnp.where(lo, y, sw).astype(BF16)
            o_ref[:, LANES:] = jnp.where(lo, sw, y).astype(BF16)
        else:
            o_ref[...] = y.astype(BF16)

    in_specs = [pl.BlockSpec((tm, LANES), lambda c, i: (i, col_off + c)), pl.BlockSpec((None, 1, LANES), lambda c, i: (c, 0, 0))]
    args = [proj, gains]
    if with_rope:
        tab = pl.BlockSpec((tm, LANES), lambda c, i: (i, 0))
        in_specs += [tab, tab]
        args += list(rope)
    return pl.pallas_call(
        body, out_shape=jax.ShapeDtypeStruct((T, ncb * width), BF16), grid=(ncb, T // tm),
        in_specs=in_specs, out_specs=pl.BlockSpec((tm, width), lambda c, i: (i, c)),
        name=name, compiler_params=_params(2))(*args)


def _headnorm_bwd(name, dy, proj, col_off, ncb, gains, group, tm, scale, rope=None, fold=False, norm=True):
    T = dy.shape[0]
    with_rope = rope is not None
    n_groups = ncb // group
    dy_width = 4 * LANES if fold else LANES

    def body(*refs):
        refs = list(refs)
        dy_ref = refs.pop(0)
        x_ref = refs.pop(0) if norm else None
        g_ref = refs.pop(0) if norm else None
        cos_ref = refs.pop(0) if with_rope else None
        sin_ref = refs.pop(0) if with_rope else None
        dx_ref = refs.pop(0)
        dg_ref = refs.pop(0) if norm else None
        c = pl.program_id(0)
        i = pl.program_id(1)
        d = dy_ref[...]
        lane = lax.broadcasted_iota(jnp.int32, (d.shape[0], LANES), 1)
        lo = lane < HEAD_DIM
        if fold:
            t0 = d[:, 0:LANES] + d[:, LANES:2 * LANES]
            t1 = d[:, 2 * LANES:3 * LANES] + d[:, 3 * LANES:4 * LANES]
            d = jnp.where(lo, t0 + pltpu.roll(t0, HEAD_DIM, 1), t1 + pltpu.roll(t1, HEAD_DIM, 1))
        d = d * scale
        if with_rope:
            d = d * cos_ref[...] + _rot_half(d * sin_ref[...], lane)
        if not norm:
            dx_ref[...] = d.astype(BF16)
            return
        xv = x_ref[...]
        gv = g_ref[...]
        r = _head_rstd(xv, lo)
        xh = xv * r
        dxh = d * gv
        pr = dxh * xh
        m_a = jnp.sum(jnp.where(lo, pr, 0.0), axis=-1, keepdims=True)
        m_b = jnp.sum(jnp.where(lo, 0.0, pr), axis=-1, keepdims=True)
        mean = jnp.where(lo, m_a, m_b) * (1.0 / HEAD_DIM)
        dx_ref[...] = (r * (dxh - xh * mean)).astype(BF16)
        dgp = jnp.sum(d * xh, axis=0, keepdims=True)
        dgp = dgp + pltpu.roll(dgp, HEAD_DIM, 1)
        first = jnp.logical_and(c % group == 0, i == 0)

        @pl.when(first)
        def _():
            dg_ref[...] = dgp

        @pl.when(jnp.logical_not(first))
        def _():
            dg_ref[...] += dgp

    in_specs = [pl.BlockSpec((tm, dy_width), lambda c, i: (i, c))]
    args = [dy]
    if norm:
        in_specs += [pl.BlockSpec((tm, LANES), lambda c, i: (i, col_off + c)), pl.BlockSpec((None, 1, LANES), lambda c, i: (c, 0, 0))]
        args += [proj, gains]
    if with_rope:
        tab = pl.BlockSpec((tm, LANES), lambda c, i: (i, 0))
        in_specs += [tab, tab]
        args += list(rope)
    out_shape = [jax.ShapeDtypeStruct((T, ncb * LANES), BF16)]
    out_specs = [pl.BlockSpec((tm, LANES), lambda c, i: (i, c))]
    if norm:
        out_shape.append(jax.ShapeDtypeStruct((n_groups, 1, LANES), F32))
        out_specs.append(pl.BlockSpec((None, 1, LANES), lambda c, i: (c // group, 0, 0)))
    res = pl.pallas_call(
        body, out_shape=tuple(out_shape), grid=(ncb, T // tm), in_specs=in_specs, out_specs=tuple(out_specs),
        name=name, compiler_params=_params(2))(*args)
    return res if norm else (res[0], None)


def _dot_exact(x, tri):
    hi = x.astype(BF16)
    r1 = x - hi.astype(F32)
    mid = r1.astype(BF16)
    lo = (r1 - mid.astype(F32)).astype(BF16)
    return _dot(hi, tri, NN) + _dot(mid, tri, NN) + _dot(lo, tri, NN)


def _forget_fwd(name, zt, bias):
    H, T = zt.shape
    blk = min(256, T)

    def body(z_ref, b_ref, c_ref, s_ref):
        z = z_ref[...] + b_ref[...]
        s_ref[...] = jax.nn.sigmoid(-z)
        lf = jnp.minimum(z, 0.0) - jnp.log(1.0 + jnp.exp(-jnp.abs(z)))
        tri = (lax.broadcasted_iota(jnp.int32, (blk, blk), 0) <= lax.broadcasted_iota(jnp.int32, (blk, blk), 1)).astype(BF16)
        carry = jnp.zeros((H, 1), F32)
        for bi in range(T // blk):
            xb = lf[:, bi * blk:(bi + 1) * blk]
            c_ref[:, bi * blk:(bi + 1) * blk] = _dot_exact(xb, tri) + carry
            carry = carry + jnp.sum(xb, axis=-1, keepdims=True)

    shape = jax.ShapeDtypeStruct((H, T), F32)
    full = pl.BlockSpec((H, T), lambda i: (0, 0))
    return pl.pallas_call(
        body, out_shape=(shape, shape), grid=(1,), in_specs=[full, pl.BlockSpec((H, 1), lambda i: (0, 0))],
        out_specs=(full, full), name=name, compiler_params=_params(1))(zt, bias)


def _forget_bwd(name, dct, drt, sgt):
    H, T = dct.shape
    blk = min(256, T)

    def body(dc_ref, dr_ref, s_ref, dz_ref, db_ref):
        dc = dc_ref[...] + dr_ref[...]
        tri = (lax.broadcasted_iota(jnp.int32, (blk, blk), 0) >= lax.broadcasted_iota(jnp.int32, (blk, blk), 1)).astype(BF16)
        carry = jnp.zeros((H, 1), F32)
        db = jnp.zeros((H, 1), F32)
        for bi in reversed(range(T // blk)):
            xb = dc[:, bi * blk:(bi + 1) * blk]
            dz = (_dot_exact(xb, tri) + carry) * s_ref[:, bi * blk:(bi + 1) * blk]
            dz_ref[:, bi * blk:(bi + 1) * blk] = dz
            db = db + jnp.sum(dz, axis=-1, keepdims=True)
            carry = carry + jnp.sum(xb, axis=-1, keepdims=True)
        db_ref[...] = db

    full = pl.BlockSpec((H, T), lambda i: (0, 0))
    return pl.pallas_call(
        body, out_shape=(jax.ShapeDtypeStruct((H, T), F32), jax.ShapeDtypeStruct((H, 1), F32)), grid=(1,),
        in_specs=[full, full, full], out_specs=(full, pl.BlockSpec((H, 1), lambda i: (0, 0))),
        name=name, compiler_params=_params(1))(dct, drt, sgt)


STRIP = 256


def _fox_fwd(name, qk, v, crow, tq, tk, carry=None):
    T, Dh = v.shape
    HP = Dh // LANES
    nk = T // tk
    assert tk % tq == 0 and tq % STRIP == 0
    n_strips = tq // STRIP

    def body(q_ref, k_ref, v_ref, ra_ref, rb_ref, o_ref, la_ref, lb_ref, s_ref, p_ref, m_ref, l_ref, acc_ref):
        i = pl.program_id(1)
        q2 = q_ref[...]
        lo = _lane_lo((tq, LANES))
        q_st = jnp.concatenate([_keep(lo, q2), _keep(jnp.logical_not(lo), q2)], axis=0)
        r_refs = (ra_ref, rb_ref)
        m_ref[...] = jnp.full(m_ref.shape, NEG, F32)
        l_ref[...] = jnp.zeros(l_ref.shape, F32)
        acc_ref[...] = jnp.zeros(acc_ref.shape, F32)
        rel = lax.broadcasted_iota(jnp.int32, (STRIP, tk), 0) - lax.broadcasted_iota(jnp.int32, (STRIP, tk), 1)

        def chunk(kc, masked):
            start = pl.multiple_of(kc * tk, tk)
            kb = k_ref[pl.ds(start, tk), :]
            vb = v_ref[pl.ds(start, tk), :]
            s_ref[...] = _dot(q_st, kb, NT)
            for h in range(2):
                cs = r_refs[h][kc]
                for st in range(n_strips):
                    rows = pl.ds(h * tq + st * STRIP, STRIP)
                    s = s_ref[rows, :] - cs
                    if masked:
                        s = jnp.where(rel >= start - (i * tq + st * STRIP), s, NEG)
                    m_old = m_ref[rows, :]
                    mn = jnp.maximum(m_old, jnp.max(s, axis=-1, keepdims=True))
                    p = jnp.exp(s - mn)
                    alpha = jnp.exp(m_old - mn)
                    l_ref[rows, :] = alpha * l_ref[rows, :] + jnp.sum(p, axis=-1, keepdims=True)
                    m_ref[rows, :] = mn
                    p_ref[rows, :] = p.astype(BF16)
                    acc_ref[rows, :] = acc_ref[rows, :] * alpha
            acc_ref[...] += _dot(p_ref[...], vb, NN)

        n_full = (i * tq) // tk

        def full_chunk(kc, _):
            chunk(kc, False)
            return 0

        lax.fori_loop(0, n_full, full_chunk, 0)
        chunk(n_full, True)
        top, bot = pl.ds(0, tq), pl.ds(tq, tq)
        o_ref[...] = jnp.where(lo, acc_ref[top, :] / l_ref[top, :], acc_ref[bot, :] / l_ref[bot, :])
        la_ref[...] = m_ref[top, :] + jnp.log(l_ref[top, :])
        lb_ref[...] = m_ref[bot, :] + jnp.log(l_ref[bot, :])

    row = lambda off: pl.BlockSpec((None, nk, 1, tk), lambda h, i: (2 * h + off, 0, 0, 0))
    lse = jax.ShapeDtypeStruct((HP, T, 1), F32)
    lspec = pl.BlockSpec((None, tq, 1), lambda h, i: (h, i, 0))
    scratch = [pltpu.VMEM((2 * tq, tk), F32), pltpu.VMEM((2 * tq, tk), BF16), pltpu.VMEM((2 * tq, 1), F32),
               pltpu.VMEM((2 * tq, 1), F32), pltpu.VMEM((2 * tq, LANES), F32)]
    return _call(
        body, name=name, grid=(HP, T // tq), out_shape=(jax.ShapeDtypeStruct((T, Dh), F32), lse, lse),
        in_specs=[pl.BlockSpec((tq, LANES), lambda h, i: (i, h)), pl.BlockSpec((T, LANES), lambda h, i: (0, HP + h)),
                  pl.BlockSpec((T, LANES), lambda h, i: (0, h)), row(0), row(1)],
        out_specs=(pl.BlockSpec((tq, LANES), lambda h, i: (i, h)), lspec, lspec),
        args=[qk, qk, v, crow, crow], scratch_shapes=scratch, carry=carry)


def _fox_bwd(name, qk, v, o, do, crow, lse_a, lse_b, tq, tk, carry=None):
    T, Dh = v.shape
    HP = Dh // LANES
    nk = T // tk
    scale = HEAD_DIM ** -0.5
    assert tk % tq == 0 and tq % STRIP == 0
    n_strips = tq // STRIP

    def body(q_ref, k_ref, v_ref, o_ref, do_ref, ra_ref, rb_ref, la_ref, lb_ref,
             dq_ref, dk_ref, dv_ref, dca_ref, dcb_ref, dra_ref, drb_ref, s_ref, dp_ref, p_ref, ds_ref, dq_acc, dsum_ref):
        i = pl.program_id(1)

        @pl.when(i == 0)
        def _():
            dk_ref[...] = jnp.zeros_like(dk_ref)
            dv_ref[...] = jnp.zeros_like(dv_ref)
            dca_ref[...] = jnp.zeros_like(dca_ref)
            dcb_ref[...] = jnp.zeros_like(dcb_ref)

        q2 = q_ref[...]
        do2 = do_ref[...]
        lo = _lane_lo((tq, LANES))
        hi = jnp.logical_not(lo)
        q_st = jnp.concatenate([_keep(lo, q2), _keep(hi, q2)], axis=0)
        do_st = jnp.concatenate([_keep(lo, do2), _keep(hi, do2)], axis=0)
        prod = do2.astype(F32) * o_ref[...]
        dsum_ref[pl.ds(0, tq), :] = jnp.sum(jnp.where(lo, prod, 0.0), axis=-1, keepdims=True)
        dsum_ref[pl.ds(tq, tq), :] = jnp.sum(jnp.where(lo, 0.0, prod), axis=-1, keepdims=True)
        r_refs, l_refs, dc_refs, dr_refs = (ra_ref, rb_ref), (la_ref, lb_ref), (dca_ref, dcb_ref), (dra_ref, drb_ref)
        dq_acc[...] = jnp.zeros(dq_acc.shape, F32)
        dra_ref[...] = jnp.zeros(dra_ref.shape, F32)
        drb_ref[...] = jnp.zeros(drb_ref.shape, F32)
        rel = lax.broadcasted_iota(jnp.int32, (STRIP, tk), 0) - lax.broadcasted_iota(jnp.int32, (STRIP, tk), 1)

        def chunk(kc, masked):
            start = pl.multiple_of(kc * tk, tk)
            kb = k_ref[pl.ds(start, tk), :]
            vb = v_ref[pl.ds(start, tk), :]
            s_ref[...] = _dot(q_st, kb, NT)
            dp_ref[...] = _dot(do_st, vb, NT)
            for h in range(2):
                cs = r_refs[h][kc]
                col_sum = jnp.zeros((1, tk), F32)
                for st in range(n_strips):
                    rows = pl.ds(st * STRIP, STRIP)
                    both = pl.ds(h * tq + st * STRIP, STRIP)
                    s = s_ref[both, :] - cs
                    if masked:
                        s = jnp.where(rel >= start - (i * tq + st * STRIP), s, NEG)
                    p = jnp.exp(s - l_refs[h][rows, :])
                    ds = p * (dp_ref[both, :] - dsum_ref[both, :])
                    p_ref[both, :] = p.astype(BF16)
                    ds_ref[both, :] = ds.astype(BF16)
                    col_sum = col_sum + jnp.sum(ds, axis=0, keepdims=True)
                    dr_refs[h][rows, :] += jnp.sum(ds, axis=-1, keepdims=True)
                dc_refs[h][kc] = dc_refs[h][kc] - col_sum
            dk_ref[pl.ds(start, tk), :] += _dot(ds_ref[...], q_st, TN)
            dv_ref[pl.ds(start, tk), :] += _dot(p_ref[...], do_st, TN)
            dq_acc[...] += _dot(ds_ref[...], kb, NN)

        n_full = (i * tq) // tk

        def full_chunk(kc, _):
            chunk(kc, False)
            return 0

        lax.fori_loop(0, n_full, full_chunk, 0)
        chunk(n_full, True)
        dq_ref[...] = jnp.where(lo, dq_acc[pl.ds(0, tq), :], dq_acc[pl.ds(tq, tq), :]) * scale

    row = lambda off: pl.BlockSpec((None, nk, 1, tk), lambda h, i: (2 * h + off, 0, 0, 0))
    lspec = pl.BlockSpec((None, tq, 1), lambda h, i: (h, i, 0))
    qspec = pl.BlockSpec((tq, LANES), lambda h, i: (i, h))
    full = pl.BlockSpec((T, LANES), lambda h, i: (0, h))
    dcspec = pl.BlockSpec((None, nk, 1, tk), lambda h, i: (h, 0, 0, 0))
    grad = jax.ShapeDtypeStruct((T, Dh), F32)
    dc = jax.ShapeDtypeStruct((HP, nk, 1, tk), F32)
    dr = jax.ShapeDtypeStruct((HP, T, 1), F32)
    scratch = [pltpu.VMEM((2 * tq, tk), F32), pltpu.VMEM((2 * tq, tk), F32), pltpu.VMEM((2 * tq, tk), BF16), pltpu.VMEM((2 * tq, tk), BF16),
               pltpu.VMEM((2 * tq, LANES), F32), pltpu.VMEM((2 * tq, 1), F32)]
    return _call(
        body, name=name, grid=(HP, T // tq), out_shape=(grad, grad, grad, dc, dc, dr, dr),
        in_specs=[qspec, pl.BlockSpec((T, LANES), lambda h, i: (0, HP + h)), full, qspec, qspec, row(0), row(1), lspec, lspec],
        out_specs=(qspec, full, full, dcspec, dcspec, lspec, lspec),
        args=[qk, qk, v, o, do, crow, crow, lse_a, lse_b], scratch_shapes=scratch, carry=carry)


SWA_GROUP = 2
SWA_GROUP_BWD = 4


def _swa_block(n, q_ref, k_ref):
    qs = pl.multiple_of(n * WINDOW, WINDOW)
    ks = pl.multiple_of(jnp.maximum(n - 1, 0) * WINDOW, WINDOW)
    rel = (qs + lax.broadcasted_iota(jnp.int32, (WINDOW, 2 * WINDOW), 0)) - (ks + lax.broadcasted_iota(jnp.int32, (WINDOW, 2 * WINDOW), 1))
    valid = jnp.logical_and(rel >= 0, rel < WINDOW)
    return qs, ks, valid


def _swa_fwd(name, q, kd, vd, sinks, carry=None):
    T, Dh = q.shape
    HP = Dh // LANES

    def body(q_ref, k_ref, v_ref, sa_ref, sb_ref, o_ref, la_ref, lb_ref):
        lo = _lane_lo((WINDOW, LANES))

        top = lax.broadcasted_iota(jnp.int32, (2 * WINDOW, 1), 0) < WINDOW
        sink = jnp.where(top, sa_ref[...], sb_ref[...])

        def block(n, _):
            qs, ks, valid = _swa_block(n, q_ref, k_ref)
            q2 = q_ref[pl.ds(qs, WINDOW), :]
            kb = k_ref[pl.ds(ks, 2 * WINDOW), :]
            vb = v_ref[pl.ds(ks, 2 * WINDOW), :]
            q_st = jnp.concatenate([_keep(lo, q2), _keep(jnp.logical_not(lo), q2)], axis=0)
            s = jnp.where(jnp.concatenate([valid, valid], axis=0), _dot(q_st, kb, NT), NEG)
            m = jnp.maximum(jnp.max(s, axis=-1, keepdims=True), sink)
            p = jnp.exp(s - m)
            l = jnp.sum(p, axis=-1, keepdims=True) + jnp.exp(sink - m)
            o2 = _dot(p.astype(BF16), vb, NN) / l
            lse = m + jnp.log(l)
            o_ref[pl.ds(qs, WINDOW), :] = jnp.where(lo, o2[:WINDOW], o2[WINDOW:])
            la_ref[pl.ds(qs, WINDOW), :] = lse[:WINDOW]
            lb_ref[pl.ds(qs, WINDOW), :] = lse[WINDOW:]
            return 0

        assert (T // WINDOW) % SWA_GROUP == 0

        def group(g, c):
            for b in range(SWA_GROUP):
                c = block(g * SWA_GROUP + b, c)
            return c

        lax.fori_loop(0, T // WINDOW // SWA_GROUP, group, 0)

    full = pl.BlockSpec((T, LANES), lambda h: (0, h))
    kv = pl.BlockSpec((T, LANES), lambda h: (0, h // 2))
    sink = lambda off: pl.BlockSpec((None, 1, 1), lambda h: (2 * h + off, 0, 0))
    lse = jax.ShapeDtypeStruct((HP, T, 1), F32)
    lspec = pl.BlockSpec((None, T, 1), lambda h: (h, 0, 0))
    return _call(
        body, name=name, grid=(HP,), out_shape=(jax.ShapeDtypeStruct((T, Dh), F32), lse, lse),
        in_specs=[full, kv, kv, sink(0), sink(1)], out_specs=(full, lspec, lspec),
        args=[q, kd, vd, sinks, sinks], carry=carry)


def _swa_bwd(name, q, kd, vd, sinks, o, do, lse_a, lse_b, carry=None):
    T, Dh = q.shape
    HP = Dh // LANES
    scale = HEAD_DIM ** -0.5

    def body(q_ref, k_ref, v_ref, sa_ref, sb_ref, o_ref, do_ref, la_ref, lb_ref, dq_ref, dk_ref, dv_ref, dsa_ref, dsb_ref):
        lo = _lane_lo((WINDOW, LANES))
        hi = jnp.logical_not(lo)
        dk_ref[...] = jnp.zeros_like(dk_ref)
        dv_ref[...] = jnp.zeros_like(dv_ref)

        top = lax.broadcasted_iota(jnp.int32, (2 * WINDOW, 1), 0) < WINDOW
        sink = jnp.where(top, sa_ref[...], sb_ref[...])

        def block(n, dsinks):
            qs, ks, valid = _swa_block(n, q_ref, k_ref)
            rows = pl.ds(qs, WINDOW)
            q2 = q_ref[rows, :]
            do2 = do_ref[rows, :]
            kb = k_ref[pl.ds(ks, 2 * WINDOW), :]
            vb = v_ref[pl.ds(ks, 2 * WINDOW), :]
            prod = do2.astype(F32) * o_ref[rows, :]
            q_st = jnp.concatenate([_keep(lo, q2), _keep(hi, q2)], axis=0)
            do_st = jnp.concatenate([_keep(lo, do2), _keep(hi, do2)], axis=0)
            dsum = jnp.concatenate([jnp.sum(jnp.where(lo, prod, 0.0), axis=-1, keepdims=True),
                                    jnp.sum(jnp.where(lo, 0.0, prod), axis=-1, keepdims=True)], axis=0)
            lse = jnp.concatenate([la_ref[rows, :], lb_ref[rows, :]], axis=0)
            s = jnp.where(jnp.concatenate([valid, valid], axis=0), _dot(q_st, kb, NT), NEG)
            p = jnp.exp(s - lse)
            ds = p * (_dot(do_st, vb, NT) - dsum)
            dsb = ds.astype(BF16)
            dq2 = _dot(dsb, kb, NN)
            dq_ref[rows, :] = jnp.where(lo, dq2[:WINDOW], dq2[WINDOW:]) * scale
            dk_ref[pl.ds(ks, 2 * WINDOW), :] += _dot(dsb, q_st, TN)
            dv_ref[pl.ds(ks, 2 * WINDOW), :] += _dot(p.astype(BF16), do_st, TN)
            gone = jnp.exp(sink - lse) * dsum
            return (dsinks[0] - jnp.sum(gone[:WINDOW], axis=0, keepdims=True),
                    dsinks[1] - jnp.sum(gone[WINDOW:], axis=0, keepdims=True))

        assert (T // WINDOW) % SWA_GROUP_BWD == 0

        def group(g, c):
            for b in range(SWA_GROUP_BWD):
                c = block(g * SWA_GROUP_BWD + b, c)
            return c

        dsa, dsb_ = lax.fori_loop(0, T // WINDOW // SWA_GROUP_BWD, group, (jnp.zeros((1, 1), F32), jnp.zeros((1, 1), F32)))
        dsa_ref[...] = dsa
        dsb_ref[...] = dsb_

    full = pl.BlockSpec((T, LANES), lambda h: (0, h))
    kv = pl.BlockSpec((T, LANES), lambda h: (0, h // 2))
    sink = lambda off: pl.BlockSpec((None, 1, 1), lambda h: (2 * h + off, 0, 0))
    lspec = pl.BlockSpec((None, T, 1), lambda h: (h, 0, 0))
    dsink = pl.BlockSpec((None, 1, 1), lambda h: (h, 0, 0))
    grad = jax.ShapeDtypeStruct((T, Dh), F32)
    ds_shape = jax.ShapeDtypeStruct((HP, 1, 1), F32)
    return _call(
        body, name=name, grid=(HP,), out_shape=(grad, grad, grad, ds_shape, ds_shape),
        in_specs=[full, kv, kv, sink(0), sink(1), full, full, lspec, lspec],
        out_specs=(full, full, full, dsink, dsink),
        args=[q, kd, vd, sinks, sinks, o, do, lse_a, lse_b], carry=carry)


def _place():
    return lax.axis_index("x"), lax.axis_index("y"), lax.axis_index("c")


def _run_carry(name, carry):
    c_in, c_out = len(carry.inputs), len(carry.out_shapes)

    def body(*refs):
        ins, outs, scr = refs[:c_in], refs[c_in:c_in + c_out], refs[c_in + c_out:]
        carry.start(ins, outs, scr)
        carry.middle(ins, outs, scr)
        carry.finish(ins, outs, scr)

    return pl.pallas_call(
        body, out_shape=tuple(carry.out_shapes), in_specs=[_HBM] * c_in, out_specs=tuple([_HBM] * c_out),
        scratch_shapes=carry.scratch, name=name)(*carry.inputs)


def _gather_carry(shards):
    n = len(shards)

    def plan(ins, outs, scr):
        send, recv, local = scr
        x, y, c = _place()
        me, sibling = (x, y, c), (x, y, 1 - c)
        partner, other, diag = (x ^ c, y ^ (1 - c)), (x ^ (1 - c), y ^ c), (1 - x, 1 - y)

        def copy(w, k, block, to, src=None):
            slot = 4 * block[0] + 2 * block[1] + block[2]
            return pltpu.make_async_remote_copy(
                src_ref=outs[w].at[slot] if src is None else src, dst_ref=outs[w].at[slot],
                send_sem=send.at[w, k], recv_sem=recv.at[w, k], device_id=to, device_id_type=MESH)

        def own():
            return [pltpu.make_async_copy(ins[w], outs[w].at[4 * x + 2 * y + c], local.at[w]) for w in range(n)]

        return copy, own, me, sibling, partner, other, diag, c

    def start(ins, outs, scr):
        copy, own, me, sibling, partner, other, _, c = plan(ins, outs, scr)
        for cp in own():
            cp.start()
        for w in range(n):
            copy(w, 1, me, (*partner, c), src=ins[w]).start()
            copy(w, 2, me, (*other, c), src=ins[w]).start()
            copy(w, 0, me, sibling, src=ins[w]).start()

    def middle(ins, outs, scr):
        copy, _, me, sibling, partner, other, _, c = plan(ins, outs, scr)
        for w in range(n):
            copy(w, 1, (*partner, c), me).wait_recv()
            copy(w, 3, (*partner, c), (*other, c)).start()
            copy(w, 4, (*partner, c), sibling).start()
        for w in range(n):
            copy(w, 2, (*other, c), me).wait_recv()
            copy(w, 5, (*other, c), sibling).start()

    def finish(ins, outs, scr):
        copy, own, me, sibling, partner, other, diag, c = plan(ins, outs, scr)
        for w in range(n):
            copy(w, 3, (*diag, c), me).wait_recv()
            copy(w, 6, (*diag, c), sibling).start()
        for w in range(n):
            copy(w, 0, sibling, me).wait_recv()
            copy(w, 4, (*other, 1 - c), me).wait_recv()
            copy(w, 5, (*partner, 1 - c), me).wait_recv()
            copy(w, 6, (*diag, 1 - c), me).wait_recv()
        for w in range(n):
            sent = [copy(w, 0, me, sibling, src=ins[w]), copy(w, 1, me, (*partner, c), src=ins[w]), copy(w, 2, me, (*other, c), src=ins[w]),
                    copy(w, 3, (*partner, c), (*other, c)), copy(w, 4, (*partner, c), sibling), copy(w, 5, (*other, c), sibling),
                    copy(w, 6, (*diag, c), sibling)]
            for cp in sent:
                cp.wait_send()
        for cp in own():
            cp.wait()

    return _Carry(shards, [jax.ShapeDtypeStruct((N_DEV,) + s.shape, s.dtype) for s in shards],
                  [pltpu.SemaphoreType.DMA((n, 7)), pltpu.SemaphoreType.DMA((n, 7)), pltpu.SemaphoreType.DMA((n,))], start, finish, middle)


def _sibling_carry(grads):
    n = len(grads)

    def copies(ins, outs, scr):
        send, recv = scr
        x, y, c = _place()
        return [pltpu.make_async_remote_copy(
            src_ref=ins[w].at[2 * q + (1 - c)], dst_ref=outs[w].at[q], send_sem=send.at[w, q], recv_sem=recv.at[w, q],
            device_id=(x, y, 1 - c), device_id_type=MESH) for w in range(n) for q in range(4)]

    def start(ins, outs, scr):
        for cp in copies(ins, outs, scr):
            cp.start()

    def finish(ins, outs, scr):
        for cp in copies(ins, outs, scr):
            cp.wait()

    return _Carry(grads, [jax.ShapeDtypeStruct((4,) + g.shape[1:], g.dtype) for g in grads],
                  [pltpu.SemaphoreType.DMA((n, 4)), pltpu.SemaphoreType.DMA((n, 4))], start, finish)


def _to_partner_carry(sums):
    n = len(sums)

    def copies(ins, outs, scr):
        send, recv = scr
        x, y, c = _place()
        partner, diag = (x ^ c, y ^ (1 - c)), (1 - x, 1 - y)
        cps = []
        for w in range(n):
            for k, chip in enumerate((partner, diag)):
                cps.append(pltpu.make_async_remote_copy(
                    src_ref=ins[w].at[2 * chip[0] + chip[1]], dst_ref=outs[2 * w + k], send_sem=send.at[w, k], recv_sem=recv.at[w, k],
                    device_id=(*partner, c), device_id_type=MESH))
        return cps

    def start(ins, outs, scr):
        for cp in copies(ins, outs, scr):
            cp.start()

    def finish(ins, outs, scr):
        for cp in copies(ins, outs, scr):
            cp.wait()

    return _Carry(sums, [jax.ShapeDtypeStruct(s.shape[1:], s.dtype) for s in sums for _ in range(2)],
                  [pltpu.SemaphoreType.DMA((n, 2)), pltpu.SemaphoreType.DMA((n, 2))], start, finish)


def _to_other_carry(blocks):
    n = len(blocks)

    def copies(ins, outs, scr):
        send, recv = scr
        x, y, c = _place()
        return [pltpu.make_async_remote_copy(
            src_ref=ins[w], dst_ref=outs[w], send_sem=send.at[w], recv_sem=recv.at[w],
            device_id=(x ^ (1 - c), y ^ c, c), device_id_type=MESH) for w in range(n)]

    def start(ins, outs, scr):
        for cp in copies(ins, outs, scr):
            cp.start()

    def finish(ins, outs, scr):
        for cp in copies(ins, outs, scr):
            cp.wait()

    return _Carry(blocks, [jax.ShapeDtypeStruct(b.shape, b.dtype) for b in blocks],
                  [pltpu.SemaphoreType.DMA((n,)), pltpu.SemaphoreType.DMA((n,))], start, finish)


def _gather_small(packed):
    R, C = packed.shape

    def body(in_ref, out_ref, send, recv):
        x, y, c = _place()
        mine = 4 * x + 2 * y + c
        out_ref[mine] = in_ref[...]
        copies = []
        for k in range(1, N_DEV):
            peer = (x ^ (k >> 2), y ^ ((k >> 1) & 1), c ^ (k & 1))
            copies.append(pltpu.make_async_remote_copy(
                src_ref=in_ref, dst_ref=out_ref.at[mine], send_sem=send.at[k - 1], recv_sem=recv.at[k - 1],
                device_id=peer, device_id_type=MESH))
        for cp in copies:
            cp.start()
        for cp in copies:
            cp.wait()

    vmem = pl.BlockSpec(memory_space=pltpu.VMEM)
    return pl.pallas_call(
        body, out_shape=jax.ShapeDtypeStruct((N_DEV, R, C), F32), in_specs=[vmem], out_specs=vmem,
        scratch_shapes=[pltpu.SemaphoreType.DMA((N_DEV - 1,)), pltpu.SemaphoreType.DMA((N_DEV - 1,))],
        name="small_grads_all_gather")(packed)


def _adamw(w, g, m, v):
    m = ADAM_B1 * m + (1.0 - ADAM_B1) * g
    v = ADAM_B2 * v + (1.0 - ADAM_B2) * (g * g)
    m_hat = m / (1.0 - ADAM_B1 ** ADAM_STEP)
    v_hat = v / (1.0 - ADAM_B2 ** ADAM_STEP)
    delta = -ADAM_LR * (m_hat / (jnp.sqrt(v_hat) + ADAM_EPS) + ADAM_WD * w)
    return delta, m, v


def _pair_add(name, grads, received, c_idx):
    _, R, C = grads.shape
    tr = _row_tile(R)

    def body(c_ref, g_ref, r_ref, o_ref):
        o_ref[...] = (g_ref[...].astype(F32) + r_ref[...].astype(F32)).astype(BF16)

    blk = pl.BlockSpec((None, tr, C), lambda q, i, c: (q, i, 0))
    return pl.pallas_call(
        body, out_shape=jax.ShapeDtypeStruct((4, R, C), BF16),
        grid_spec=pltpu.PrefetchScalarGridSpec(
            num_scalar_prefetch=1, grid=(4, R // tr),
            in_specs=[pl.BlockSpec((None, tr, C), lambda q, i, c: (2 * q + c[0], i, 0)), blk], out_specs=blk),
        name=name, compiler_params=_params(2))(c_idx, grads, received)


def _relay_add(name, sums, relayed, other_idx):
    _, R, C = sums.shape
    tr = _row_tile(R)

    def body(q_ref, s_ref, r_ref, o_ref):
        o_ref[...] = (s_ref[...].astype(F32) + r_ref[...].astype(F32)).astype(BF16)

    blk = pl.BlockSpec((tr, C), lambda i, q: (i, 0))
    return pl.pallas_call(
        body, out_shape=jax.ShapeDtypeStruct((R, C), BF16),
        grid_spec=pltpu.PrefetchScalarGridSpec(
            num_scalar_prefetch=1, grid=(R // tr,),
            in_specs=[pl.BlockSpec((None, tr, C), lambda i, q: (q[0], i, 0)), blk], out_specs=blk),
        name=name, compiler_params=_params(1))(other_idx, sums, relayed)


def _adam_shard(name, sums, received, w, m, v, chip_idx):
    R, C = w.shape
    tr = _row_tile(R, 128)
    tc = C if tr < R or C % (2 * LANES) else 2 * LANES

    def body(q_ref, s_ref, ra_ref, rb_ref, w_ref, m_ref, v_ref, g_out, d_out, m_out, v_out):
        g = s_ref[...].astype(F32) + ra_ref[...].astype(F32) + rb_ref[...].astype(F32)
        delta, mn, vn = _adamw(w_ref[...], g, m_ref[...], v_ref[...])
        g_out[...] = g
        d_out[...] = delta
        m_out[...] = mn
        v_out[...] = vn

    blk = pl.BlockSpec((tr, tc), lambda i, j, q: (i, j))
    shape = jax.ShapeDtypeStruct((R, C), F32)
    return pl.pallas_call(
        body, out_shape=(shape,) * 4,
        grid_spec=pltpu.PrefetchScalarGridSpec(
            num_scalar_prefetch=1, grid=(R // tr, C // tc),
            in_specs=[pl.BlockSpec((None, tr, tc), lambda i, j, q: (q[0], i, j)), blk, blk, blk, blk, blk],
            out_specs=(blk,) * 4),
        name=name, compiler_params=_params(2))(chip_idx, sums, received[0], received[1], w, m, v)


def _adam_small(name, gathered, w, m, v):
    R, C = w.shape

    def body(ga_ref, w_ref, m_ref, v_ref, g_out, d_out, m_out, v_out):
        g = ga_ref[0]
        for d in range(1, N_DEV):
            g = g + ga_ref[d]
        delta, mn, vn = _adamw(w_ref[...], g, m_ref[...], v_ref[...])
        g_out[...] = g
        d_out[...] = delta
        m_out[...] = mn
        v_out[...] = vn

    full = pl.BlockSpec((R, C), lambda i: (0, 0))
    shape = jax.ShapeDtypeStruct((R, C), F32)
    return pl.pallas_call(
        body, out_shape=(shape,) * 4, grid=(1,),
        in_specs=[pl.BlockSpec((N_DEV, R, C), lambda i: (0, 0, 0)), full, full, full], out_specs=(full,) * 4,
        name=name, compiler_params=_params(1))(gathered, w, m, v)


def _pack_small(parts, D):
    g1, gmix, g2, gof, gos, bf, gqf, gkf, gqs, gks, sinks = [p.reshape(-1).astype(F32) for p in parts]
    row3 = jnp.concatenate([gof, gos])
    row4 = jnp.zeros((D,), F32)
    for slot, vec in enumerate((bf, gqf, gkf, gqs, gks, sinks)):
        row4 = lax.dynamic_update_slice(row4, vec, (slot * LANES,))
    zero = jnp.zeros((D,), F32)
    return jnp.stack([g1, gmix, g2, row3, row4, zero, zero, zero])


def _unpack_small(packed, D, H):
    Dh = D // 2
    row4 = packed[4]
    short = [row4[s * LANES:s * LANES + n] for s, n in enumerate((H, HEAD_DIM, HEAD_DIM, HEAD_DIM, HEAD_DIM, H))]
    vecs = [packed[0], packed[1], packed[2], packed[3, :Dh], packed[3, Dh:]] + short
    return [v[None, :] for v in vecs]


def kernel(x, positions, norm_ffn1_g, ffn1_w_gate, ffn1_w_up, ffn1_w_down, norm_mix_g, w_in, b_forget, fox_q_norm_g, fox_k_norm_g, swa_q_norm_g, swa_k_norm_g, swa_sinks, out_norm_fox_g, out_norm_swa_g, w_out, norm_ffn2_g, ffn2_w_gate, ffn2_w_up, ffn2_w_down, loss_target, m_norm_ffn1_g, m_ffn1_w_gate, m_ffn1_w_up, m_ffn1_w_down, m_norm_mix_g, m_w_in, m_b_forget, m_fox_q_norm_g, m_fox_k_norm_g, m_swa_q_norm_g, m_swa_k_norm_g, m_swa_sinks, m_out_norm_fox_g, m_out_norm_swa_g, m_w_out, m_norm_ffn2_g, m_ffn2_w_gate, m_ffn2_w_up, m_ffn2_w_down, v_norm_ffn1_g, v_ffn1_w_gate, v_ffn1_w_up, v_ffn1_w_down, v_norm_mix_g, v_w_in, v_b_forget, v_fox_q_norm_g, v_fox_k_norm_g, v_swa_q_norm_g, v_swa_k_norm_g, v_swa_sinks, v_out_norm_fox_g, v_out_norm_swa_g, v_w_out, v_norm_ffn2_g, v_ffn2_w_gate, v_ffn2_w_up, v_ffn2_w_down):
    xs = x[0]
    target = loss_target[0]
    T, D = xs.shape
    Dh = D // 2
    H = Dh // HEAD_DIM
    HP = H // 2
    KVW = (H // GQA_GROUP) * HEAD_DIM
    KVB = KVW // LANES
    MAIN = 4 * Dh + 2 * KVW
    F_OFF = 3 * Dh
    tm = min(ROW_TILE_CAP, T)
    tq = min(512, T)
    tk = min(512, T)
    nk = T // tk
    cx, cy, cc = _place()
    c_idx = jnp.reshape(cc, (1,)).astype(jnp.int32)
    chip_idx = jnp.reshape(2 * cx + cy, (1,)).astype(jnp.int32)
    other_idx = jnp.reshape(2 * (cx ^ (1 - cc)) + (cy ^ cc), (1,)).astype(jnp.int32)

    tr = jnp.transpose
    big_w = [tr(ffn1_w_gate[0]), tr(ffn1_w_up[0]), ffn1_w_down[0], tr(w_in[0]), w_out[0], tr(ffn2_w_gate[0]), tr(ffn2_w_up[0]),
             ffn2_w_down[0]]
    big_m = [tr(m_ffn1_w_gate[0]), tr(m_ffn1_w_up[0]), m_ffn1_w_down[0], tr(m_w_in[0]), m_w_out[0], tr(m_ffn2_w_gate[0]),
             tr(m_ffn2_w_up[0]), m_ffn2_w_down[0]]
    big_v = [tr(v_ffn1_w_gate[0]), tr(v_ffn1_w_up[0]), v_ffn1_w_down[0], tr(v_w_in[0]), v_w_out[0], tr(v_ffn2_w_gate[0]),
             tr(v_ffn2_w_up[0]), v_ffn2_w_down[0]]
    transposed = {"ffn1_w_gate", "ffn1_w_up", "w_in", "ffn2_w_gate", "ffn2_w_up"}
    names = ["ffn1_w_gate", "ffn1_w_up", "ffn1_w_down", "w_in", "w_out", "ffn2_w_gate", "ffn2_w_up", "ffn2_w_down"]
    sh = dict(zip(names, [w.astype(BF16) for w in big_w]))
    lane = jnp.arange(LANES)
    inv_freq = ROPE_THETA ** (-(2.0 * (lane % (HEAD_DIM // 2))).astype(F32) / HEAD_DIM)
    ang = positions[0].astype(F32)[:, None] * inv_freq[None, :]
    cos_t = jnp.cos(ang)
    sin_t = jnp.where((lane & (HEAD_DIM // 2)) == 0, -1.0, 1.0)[None, :] * jnp.sin(ang)
    rope = (cos_t, sin_t)

    def pair_gain(g, blocks):
        return jnp.tile(jnp.concatenate([g[0], g[0]])[None, None, :], (blocks, 1, 1))

    n1, (wg1,) = _rmsnorm_fwd("ffn1_norm", xs, norm_ffn1_g, tm, carry=_gather_carry([sh["ffn1_w_gate"]]))
    a1, (wu1,) = _ffn_gate("ffn1_gate", n1, wg1, tm, carry=_gather_carry([sh["ffn1_w_up"]]))
    (b1, hm1), (wd1,) = _ffn_up_only("ffn1_up", n1, wu1, a1, tm, carry=_gather_carry([sh["ffn1_w_down"]]))
    h1, (win_g,) = _ffn_down("ffn1_down", hm1, wd1, xs, tm, D, carry=_gather_carry([sh["w_in"]]))
    n_in = win_g.shape[1]
    win_t = win_g.reshape(N_DEV * n_in, D)
    win_main = jnp.concatenate([win_t[:F_OFF], win_t[F_OFF + H:]], axis=0)
    win_f = jnp.pad(win_t[F_OFF:F_OFF + H], ((0, LANES - H), (0, 0)))

    u = _rmsnorm_fwd("mix_norm", h1, norm_mix_g, tm)
    proj, (wout_g,) = _mm("mix_proj", u, win_main, tm, MAIN // 9, dims=NT, carry=_gather_carry([sh["w_out"]]))
    wout = wout_g.reshape(D, D)
    proj_f = _mm("mix_proj_forget", u, win_f, tm, LANES, dims=NT)
    scale = HEAD_DIM ** -0.5
    fox_gains = jnp.concatenate([pair_gain(fox_q_norm_g, HP), pair_gain(fox_k_norm_g, HP)])
    qk_f = _headnorm_fwd_scaled("fox_qk_norm", proj, 0, 2 * HP, fox_gains, T, scale, HP)
    v_f = proj[:, 2 * Dh:3 * Dh].astype(BF16)
    c_t, sg_t = _forget_fwd("forget_gates", proj_f[:, :H].T, b_forget.reshape(H, 1))
    crow = c_t.reshape(H, nk, 1, tk)
    (o_fox, lse_fa, lse_fb), (wg2, wu2) = _fox_fwd("fox_attention", qk_f, v_f, crow, tq, tk,
                                                   carry=_gather_carry([sh["ffn2_w_gate"], sh["ffn2_w_up"]]))

    swa_q_gains = pair_gain(swa_q_norm_g, HP)
    swa_k_gains = pair_gain(swa_k_norm_g, KVB)
    q_s = _headnorm_fwd("swa_q_norm", proj, 3 * HP, HP, swa_q_gains, T, scale, rope=rope)
    k_d = _headnorm_fwd("swa_k_norm", proj, 4 * HP, KVB, swa_k_gains, T, 1.0, rope=rope, dup=True)
    v_s = proj[:, 4 * Dh + KVW:].astype(BF16).reshape(T, H // GQA_GROUP, 1, HEAD_DIM)
    v_d = jnp.broadcast_to(v_s, (T, H // GQA_GROUP, 2, HEAD_DIM)).reshape(T, 2 * KVW)
    sinks3 = swa_sinks.reshape(H, 1, 1)
    o_swa, lse_sa, lse_sb = _swa_fwd("swa_attention", q_s, k_d, v_d, sinks3)

    on = _outnorm_fwd("out_norm", o_fox, o_swa, out_norm_fox_g, out_norm_swa_g, tm)
    h2 = _mm("mix_out", on, wout, tm, min(512, D), resid=h1)

    n2 = _rmsnorm_fwd("ffn2_norm", h2, norm_ffn2_g, tm)
    (a2, b2, hm2), (wd2,) = _ffn_up("ffn2_up", n2, wg2, wu2, tm, carry=_gather_carry([sh["ffn2_w_down"]]))
    y = _ffn_down("ffn2_down", hm2, wd2, h2, tm, D)
    dy, dyh, sq = _loss_grad("loss_grad", y, target, min(256, T))
    loss = lax.psum(0.5 * sq[0, 0] / D, ("x", "y", "c"))

    J, Fs, _ = wg2.shape
    aspec = pl.BlockSpec((None, tm, Fs), lambda i, j: (j, i, 0))
    wspec = pl.BlockSpec((None, Fs, D), lambda i, j: (j, 0, 0))
    got = {}

    def pair_sums(keys, grads, received):
        return [_pair_add("sum_" + nm, g, r, c_idx) for nm, g, r in zip(keys, grads, received)]

    def relay_sums(keys, sums, hop1):
        out = []
        for i, (nm, s) in enumerate(zip(keys, sums)):
            got[nm] = [hop1[2 * i]]
            out.append(_relay_add("relay_" + nm, s, hop1[2 * i + 1], other_idx))
        return out

    def arrived(keys, hop2):
        for nm, blk in zip(keys, hop2):
            got[nm].append(blk)

    dwd2 = _wgrad_down("ffn2_wgrad_down", hm2, dyh, min(1024, D))
    (da2, db2), (sib_d2,) = _ffn_bwd_mid("ffn2_bwd_mid", dyh, wd2, a2, b2, tm, carry=_sibling_carry([dwd2]))
    (sum_wd2,) = pair_sums(names[7:8], [dwd2], [sib_d2])
    (dwg2, dwu2), hop1 = _wgrad_up("ffn2_wgrad_up", n2, da2, db2, min(1024, D), carry=_to_partner_carry([sum_wd2]))
    (t_wd2,) = relay_sums(names[7:8], [sum_wd2], hop1)
    dn2, (via_wd2, *sib2) = _reduce_mm("ffn2_bwd_in", [(da2, aspec, wg2, wspec), (db2, aspec, wu2, wspec)], [], NN, T, D, tm, J,
                                       carry=_join(_to_other_carry([t_wd2]), _sibling_carry([dwg2, dwu2])))
    arrived(names[7:8], [via_wd2])
    dh2, dg_ffn2, dh2b = _rmsnorm_bwd("ffn2_norm_bwd", dn2, h2, norm_ffn2_g, dy, min(256, T), 1.0)
    sum_wg2, sum_wu2 = pair_sums(names[5:7], [dwg2, dwu2], sib2)

    dwout = _wgrad_2d("mix_out_wgrad", on, dh2b, min(512, D), min(1024, D))
    dwout_g = dwout.reshape(N_DEV, D // N_DEV, D)
    do_fox, dg_of = _outnorm_bwd("out_norm_bwd_fox", dh2b, wout, 0, o_fox, out_norm_fox_g, tm)
    do_swa, dg_os = _outnorm_bwd("out_norm_bwd_swa", dh2b, wout, 1, o_swa, out_norm_swa_g, tm)

    (dq_f, dk_f, dv_f, dc_a, dc_b, dr_a, dr_b), (*hop1, sib_wout) = _fox_bwd(
        "fox_attention_bwd", qk_f, v_f, o_fox, do_fox, crow, lse_fa, lse_fb, tq, tk,
        carry=_join(_to_partner_carry([sum_wg2, sum_wu2]), _sibling_carry([dwout_g])))
    t_wg2, t_wu2 = relay_sums(names[5:7], [sum_wg2, sum_wu2], hop1)
    (sum_wout,) = pair_sums(names[4:5], [dwout_g], [sib_wout])
    dqf_raw, dg_fq = _headnorm_bwd("fox_q_norm_bwd", dq_f, proj, 0, HP, fox_gains[:HP], HP, T, 1.0)
    dkf_raw, dg_fk = _headnorm_bwd("fox_k_norm_bwd", dk_f, proj, HP, HP, fox_gains[HP:], HP, T, 1.0)
    dct = jnp.stack([dc_a.reshape(HP, T), dc_b.reshape(HP, T)], axis=1).reshape(H, T)
    drt = jnp.stack([dr_a.reshape(HP, T), dr_b.reshape(HP, T)], axis=1).reshape(H, T)
    dz_t, db_f = _forget_bwd("forget_gates_bwd", dct, drt, sg_t)

    (dq_s, dk_p, dv_p, dsink_a, dsink_b), hop2 = _swa_bwd(
        "swa_attention_bwd", q_s, k_d, v_d, sinks3, o_swa, do_swa, lse_sa, lse_sb, carry=_to_other_carry([t_wg2, t_wu2]))
    arrived(names[5:7], hop2)
    dqs_raw, dg_sq = _headnorm_bwd("swa_q_norm_bwd", dq_s, proj, 3 * HP, HP, swa_q_gains, HP, T, 1.0, rope=rope)
    dks_raw, dg_sk = _headnorm_bwd("swa_k_norm_bwd", dk_p, proj, 4 * HP, KVB, swa_k_gains, KVB, T, 1.0, rope=rope, fold=True)
    dvs_raw, _ = _headnorm_bwd("swa_v_fold", dv_p, None, 0, KVB, None, KVB, T, 1.0, fold=True, norm=False)

    dproj = jnp.concatenate([dqf_raw, dkf_raw, dv_f.astype(BF16), dqs_raw, dks_raw, dvs_raw], axis=1)
    dproj_f = jnp.pad(dz_t.T, ((0, 0), (0, LANES - H))).astype(BF16)
    dwin_main, hop1 = _wgrad_2d("mix_proj_wgrad", dproj, u, MAIN // 9, min(1024, D), carry=_to_partner_carry([sum_wout]))
    (t_wout,) = relay_sums(names[4:5], [sum_wout], hop1)
    dwin_f = _wgrad_2d("mix_proj_forget_wgrad", dproj_f, u, LANES, min(1024, D))
    dwin_t = jnp.concatenate([dwin_main[:F_OFF], dwin_f[:H], dwin_main[F_OFF:]], axis=0)
    dwin_g = dwin_t.reshape(N_DEV, n_in, D)
    tkb = MAIN // 9
    du, (via_wout, sib_win) = _reduce_mm(
        "mix_bwd_in",
        [(dproj, pl.BlockSpec((tm, tkb), lambda i, r: (i, r)), win_main, pl.BlockSpec((tkb, D), lambda i, r: (r, 0)))],
        [(dproj_f, pl.BlockSpec((tm, LANES), lambda i, r: (i, 0)), win_f, pl.BlockSpec((LANES, D), lambda i, r: (0, 0)))],
        NN, T, D, tm, 9, carry=_join(_to_other_carry([t_wout]), _sibling_carry([dwin_g])))
    arrived(names[4:5], [via_wout])
    dh1, dg_mix, dh1h = _rmsnorm_bwd("mix_norm_bwd", du, h1, norm_mix_g, dh2, min(256, T), 0.5)
    (sum_win,) = pair_sums(names[3:4], [dwin_g], [sib_win])

    dwd1, hop1 = _wgrad_down("ffn1_wgrad_down", hm1, dh1h, min(1024, D), carry=_to_partner_carry([sum_win]))
    (t_win,) = relay_sums(names[3:4], [sum_win], hop1)
    (da1, db1), (via_win, sib_d) = _ffn_bwd_mid("ffn1_bwd_mid", dh1h, wd1, a1, b1, tm,
                                                carry=_join(_to_other_carry([t_win]), _sibling_carry([dwd1])))
    arrived(names[3:4], [via_win])
    (sum_wd1,) = pair_sums(names[2:3], [dwd1], [sib_d])
    dwg1, hop1 = _wgrad_down("ffn1_wgrad_gate", da1, n1, min(1024, D), carry=_to_partner_carry([sum_wd1]))
    (t_wd1,) = relay_sums(names[2:3], [sum_wd1], hop1)
    dwu1, (via_wd1, sib_g) = _wgrad_down("ffn1_wgrad_up", db1, n1, min(1024, D),
                                         carry=_join(_to_other_carry([t_wd1]), _sibling_carry([dwg1])))
    arrived(names[2:3], [via_wd1])
    (sum_wg1,) = pair_sums(names[0:1], [dwg1], [sib_g])
    dn1_gate, (*hop1, sib_u) = _reduce_mm(
        "ffn1_bwd_in_gate", [(da1, aspec, wg1, wspec)], [], NN, T, D, tm, J,
        carry=_join(_to_partner_carry([sum_wg1]), _sibling_carry([dwu1])))
    (t_wg1,) = relay_sums(names[0:1], [sum_wg1], hop1)
    (sum_wu1,) = pair_sums(names[1:2], [dwu1], [sib_u])
    dn1, (via_wg1, *hop1) = _reduce_mm(
        "ffn1_bwd_in_up", [(db1, aspec, wu1, wspec)], [], NN, T, D, tm, J, init=dn1_gate,
        carry=_join(_to_other_carry([t_wg1]), _to_partner_carry([sum_wu1])))
    arrived(names[0:1], [via_wg1])
    (t_wu1,) = relay_sums(names[1:2], [sum_wu1], hop1)
    arrived(names[1:2], _run_carry("grads_exchange", _to_other_carry([t_wu1])))
    dx, dg_ffn1 = _rmsnorm_bwd("ffn1_norm_bwd", dn1, xs, norm_ffn1_g, dh1, min(256, T), None)

    chip_sums = [sum_wg1, sum_wu1, sum_wd1, sum_win, sum_wout, sum_wg2, sum_wu2, sum_wd2]
    big_out = [_adam_shard("adam_" + nm, s, got[nm], w, m, v, chip_idx)
               for nm, s, w, m, v in zip(names, chip_sums, big_w, big_m, big_v)]

    dsinks = jnp.stack([dsink_a.reshape(HP), dsink_b.reshape(HP)], axis=1).reshape(H)
    small_g = [dg_ffn1, dg_mix, dg_ffn2, dg_of, dg_os, db_f, dg_fq[0, 0, :HEAD_DIM], dg_fk[0, 0, :HEAD_DIM],
               dg_sq[0, 0, :HEAD_DIM], dg_sk[0, 0, :HEAD_DIM], dsinks]
    small_w = [norm_ffn1_g, norm_mix_g, norm_ffn2_g, out_norm_fox_g, out_norm_swa_g, b_forget, fox_q_norm_g, fox_k_norm_g,
               swa_q_norm_g, swa_k_norm_g, swa_sinks]
    small_m = [m_norm_ffn1_g, m_norm_mix_g, m_norm_ffn2_g, m_out_norm_fox_g, m_out_norm_swa_g, m_b_forget, m_fox_q_norm_g,
               m_fox_k_norm_g, m_swa_q_norm_g, m_swa_k_norm_g, m_swa_sinks]
    small_v = [v_norm_ffn1_g, v_norm_mix_g, v_norm_ffn2_g, v_out_norm_fox_g, v_out_norm_swa_g, v_b_forget, v_fox_q_norm_g,
               v_fox_k_norm_g, v_swa_q_norm_g, v_swa_k_norm_g, v_swa_sinks]
    gathered = _gather_small(_pack_small(small_g, D))
    small_out = _adam_small("adam_small", gathered, _pack_small(small_w, D), _pack_small(small_m, D), _pack_small(small_v, D))
    small_out = [_unpack_small(p, D, H) for p in small_out]

    order = ["norm_ffn1_g", "ffn1_w_gate", "ffn1_w_up", "ffn1_w_down", "norm_mix_g", "w_in", "b_forget", "fox_q_norm_g", "fox_k_norm_g",
             "swa_q_norm_g", "swa_k_norm_g", "swa_sinks", "out_norm_fox_g", "out_norm_swa_g", "w_out", "norm_ffn2_g",
             "ffn2_w_gate", "ffn2_w_up", "ffn2_w_down"]
    small_names = ["norm_ffn1_g", "norm_mix_g", "norm_ffn2_g", "out_norm_fox_g", "out_norm_swa_g", "b_forget", "fox_q_norm_g",
                   "fox_k_norm_g", "swa_q_norm_g", "swa_k_norm_g", "swa_sinks"]
    result = [loss, dx[None]]
    for kind in range(4):
        for nm in order:
            if nm in names:
                leaf = big_out[names.index(nm)][kind]
                result.append((tr(leaf) if nm in transposed else leaf)[None])
            else:
                result.append(small_out[kind][small_names.index(nm)])
    return tuple(result)


def _headnorm_fwd_scaled(name, proj, col_off, ncb, gains, tm, scale, n_scaled):
    T = proj.shape[0]

    def body(x_ref, g_ref, o_ref):
        xv = x_ref[...]
        lo = _lane_lo(xv.shape)
        y = xv * _head_rstd(xv, lo) * g_ref[...]
        y = y * jnp.where(pl.program_id(0) < n_scaled, scale, 1.0)
        o_ref[...] = y.astype(BF16)

    return pl.pallas_call(
        body, out_shape=jax.ShapeDtypeStruct((T, ncb * LANES), BF16), grid=(ncb, T // tm),
        in_specs=[pl.BlockSpec((tm, LANES), lambda c, i: (i, col_off + c)), pl.BlockSpec((None, 1, LANES), lambda c, i: (c, 0, 0))],
        out_specs=pl.BlockSpec((tm, LANES), lambda c, i: (i, c)), name=name, compiler_params=_params(2))(proj, gains)
```

```python
import functools

import jax
import jax.numpy as jnp
from jax import lax
from jax.experimental import pallas as pl
from jax.experimental.pallas import tpu as pltpu

F32 = jnp.float32
BF16 = jnp.bfloat16

HEAD_DIM = 64
LANES = 128
WINDOW = 128
GQA_GROUP = 4
EPS = 1e-6
ROPE_THETA = 10000.0
ADAM_LR = 0.001
ADAM_B1 = 0.9
ADAM_B2 = 0.999
ADAM_EPS = 1e-08
ADAM_WD = 0.01
ADAM_STEP = 10
N_DEV = 8
NEG = -1e30
VMEM_LIMIT_V7X = 48 * 1024 * 1024
ROW_TILE_CAP = 512
MESH = pl.DeviceIdType.MESH

NN = (((1,), (0,)), ((), ()))
NT = (((1,), (1,)), ((), ()))
TN = (((0,), (0,)), ((), ()))


def _dot(a, b, dims):
    return lax.dot_general(a, b, dims, preferred_element_type=F32)


def _params(n_axes):
    return pltpu.CompilerParams(dimension_semantics=("arbitrary",) * n_axes, vmem_limit_bytes=VMEM_LIMIT_V7X)


def _row_tile(rows, cap=ROW_TILE_CAP):
    best = None
    for t in range(16, min(rows, cap) + 1, 16):
        if rows % t == 0:
            best = t
    return best or rows


def _lane_lo(shape):
    return lax.broadcasted_iota(jnp.int32, shape, len(shape) - 1) < HEAD_DIM


def _keep(sel, x):
    return jnp.where(sel, x.astype(F32), 0.0).astype(BF16)


_HBM = pl.BlockSpec(memory_space=pltpu.HBM)


class _Carry:
    def __init__(self, inputs, out_shapes, scratch, start, finish, middle=None):
        self.inputs, self.out_shapes, self.scratch = list(inputs), list(out_shapes), list(scratch)
        self.start, self.finish, self.middle = start, finish, middle or (lambda ins, outs, scr: None)


def _join(*carries):
    def hook(which):
        def run(ins, outs, scr):
            i = o = s = 0
            for c in carries:
                ni, no, ns = len(c.inputs), len(c.out_shapes), len(c.scratch)
                getattr(c, which)(ins[i:i + ni], outs[o:o + no], scr[s:s + ns])
                i, o, s = i + ni, o + no, s + ns
        return run

    return _Carry([a for c in carries for a in c.inputs], [a for c in carries for a in c.out_shapes],
                  [a for c in carries for a in c.scratch], hook("start"), hook("finish"), hook("middle"))


def _call(body, *, name, grid, in_specs, out_specs, out_shape, args, scratch_shapes=(), carry=None):
    params = _params(len(grid))
    if carry is None:
        return pl.pallas_call(body, out_shape=out_shape, grid=grid, in_specs=list(in_specs), out_specs=out_specs,
                              scratch_shapes=list(scratch_shapes), name=name, compiler_params=params)(*args)
    single = not isinstance(out_shape, (tuple, list))
    shapes = (out_shape,) if single else tuple(out_shape)
    specs = (out_specs,) if single else tuple(out_specs)
    n_in, n_out, n_scr = len(args), len(shapes), len(scratch_shapes)
    c_in, c_out = len(carry.inputs), len(carry.out_shapes)

    def wrapped(*refs):
        ins, c_ins = refs[:n_in], refs[n_in:n_in + c_in]
        o0 = n_in + c_in
        outs, c_outs = refs[o0:o0 + n_out], refs[o0 + n_out:o0 + n_out + c_out]
        s0 = o0 + n_out + c_out
        scr, c_scr = refs[s0:s0 + n_scr], refs[s0 + n_scr:]
        step, total = pl.program_id(0), grid[0]
        for ax in range(1, len(grid)):
            step, total = step * grid[ax] + pl.program_id(ax), total * grid[ax]

        @pl.when(step == 0)
        def _():
            carry.start(c_ins, c_outs, c_scr)

        @pl.when(step == total // 2)
        def _():
            carry.middle(c_ins, c_outs, c_scr)

        body(*ins, *outs, *scr)

        @pl.when(step == total - 1)
        def _():
            carry.finish(c_ins, c_outs, c_scr)

    res = pl.pallas_call(
        wrapped, out_shape=shapes + tuple(carry.out_shapes), grid=grid, in_specs=list(in_specs) + [_HBM] * c_in,
        out_specs=specs + (_HBM,) * c_out, scratch_shapes=list(scratch_shapes) + carry.scratch, name=name,
        compiler_params=params)(*args, *carry.inputs)
    main = res[:n_out]
    return (main[0] if single else tuple(main)), tuple(res[n_out:])


def _rms_bwd(dn, x, g):
    r = lax.rsqrt(jnp.mean(x * x, axis=-1, keepdims=True) + EPS)
    xh = x * r
    dxh = dn * g
    dx = r * (dxh - xh * jnp.mean(dxh * xh, axis=-1, keepdims=True))
    return dx, jnp.sum(dn * xh, axis=0, keepdims=True)


def _rmsnorm_fwd(name, x, g, tm, carry=None):
    T, D = x.shape

    def body(x_ref, g_ref, o_ref):
        xf = x_ref[...]
        r = lax.rsqrt(jnp.mean(xf * xf, axis=-1, keepdims=True) + EPS)
        o_ref[...] = (xf * r * g_ref[...]).astype(BF16)

    return _call(
        body, name=name, grid=(T // tm,), out_shape=jax.ShapeDtypeStruct((T, D), BF16),
        in_specs=[pl.BlockSpec((tm, D), lambda i: (i, 0)), pl.BlockSpec((1, D), lambda i: (0, 0))],
        out_specs=pl.BlockSpec((tm, D), lambda i: (i, 0)), args=[x, g], carry=carry)


def _outnorm_fwd(name, o_fox, o_swa, g_fox, g_swa, tm):
    T, Dh = o_fox.shape

    def body(a_ref, b_ref, ga_ref, gb_ref, o_ref):
        for ref, g_ref, lo in ((a_ref, ga_ref, 0), (b_ref, gb_ref, Dh)):
            xf = ref[...]
            r = lax.rsqrt(jnp.mean(xf * xf, axis=-1, keepdims=True) + EPS)
            o_ref[:, lo:lo + Dh] = (xf * r * g_ref[...]).astype(BF16)

    row = pl.BlockSpec((tm, Dh), lambda i: (i, 0))
    gain = pl.BlockSpec((1, Dh), lambda i: (0, 0))
    return pl.pallas_call(
        body, out_shape=jax.ShapeDtypeStruct((T, 2 * Dh), BF16), grid=(T // tm,),
        in_specs=[row, row, gain, gain], out_specs=pl.BlockSpec((tm, 2 * Dh), lambda i: (i, 0)),
        name=name, compiler_params=_params(1))(o_fox, o_swa, g_fox, g_swa)


def _outnorm_bwd(name, dhb, wout, half, o, g, tm):
    T, D = dhb.shape
    Dh = o.shape[1]

    def body(a_ref, w_ref, o_ref, g_ref, do_ref, dg_ref):
        don = _dot(a_ref[...], w_ref[...], NT)
        dx, dg = _rms_bwd(don, o_ref[...], g_ref[...])
        do_ref[...] = dx.astype(BF16)

        @pl.when(pl.program_id(0) == 0)
        def _():
            dg_ref[...] = dg

        @pl.when(pl.program_id(0) > 0)
        def _():
            dg_ref[...] += dg

    return pl.pallas_call(
        body, out_shape=(jax.ShapeDtypeStruct((T, Dh), BF16), jax.ShapeDtypeStruct((1, Dh), F32)), grid=(T // tm,),
        in_specs=[pl.BlockSpec((tm, D), lambda i: (i, 0)), pl.BlockSpec((Dh, D), lambda i: (half, 0)),
                  pl.BlockSpec((tm, Dh), lambda i: (i, 0)), pl.BlockSpec((1, Dh), lambda i: (0, 0))],
        out_specs=(pl.BlockSpec((tm, Dh), lambda i: (i, 0)), pl.BlockSpec((1, Dh), lambda i: (0, 0))),
        name=name, compiler_params=_params(1))(dhb, wout, o, g)


def _mm(name, a, b, tm, tn, dims=NN, resid=None, carry=None):
    M, K = a.shape
    transposed = dims == NT
    N = b.shape[0] if transposed else b.shape[1]

    def body(*refs):
        if resid is None:
            a_ref, b_ref, o_ref = refs
            o_ref[...] = _dot(a_ref[...], b_ref[...], dims)
        else:
            a_ref, b_ref, r_ref, o_ref = refs
            o_ref[...] = r_ref[...] + _dot(a_ref[...], b_ref[...], dims)

    ospec = pl.BlockSpec((tm, tn), lambda n, i: (i, n))
    bspec = pl.BlockSpec((tn, K), lambda n, i: (n, 0)) if transposed else pl.BlockSpec((K, tn), lambda n, i: (0, n))
    in_specs = [pl.BlockSpec((tm, K), lambda n, i: (i, 0)), bspec]
    args = [a, b]
    if resid is not None:
        in_specs.append(ospec)
        args.append(resid)
    return _call(body, name=name, grid=(N // tn, M // tm), in_specs=in_specs, out_specs=ospec,
                 out_shape=jax.ShapeDtypeStruct((M, N), F32), args=args, carry=carry)


def _wgrad_2d(name, a, b, tmm, tn, carry=None):
    T, M = a.shape
    N = b.shape[1]

    def body(a_ref, b_ref, o_ref):
        o_ref[...] = _dot(a_ref[...], b_ref[...], TN).astype(BF16)

    return _call(
        body, name=name, grid=(M // tmm, N // tn), out_shape=jax.ShapeDtypeStruct((M, N), BF16),
        in_specs=[pl.BlockSpec((T, tmm), lambda m, n: (0, m)), pl.BlockSpec((T, tn), lambda m, n: (0, n))],
        out_specs=pl.BlockSpec((tmm, tn), lambda m, n: (m, n)), args=[a, b], carry=carry)


def _wgrad_down(name, hm, df, tn, carry=None):
    J, T, Fs = hm.shape
    D = df.shape[1]

    def body(a_ref, b_ref, o_ref):
        o_ref[...] = _dot(a_ref[...], b_ref[...], TN).astype(BF16)

    return _call(
        body, name=name, grid=(J, D // tn), out_shape=jax.ShapeDtypeStruct((J, Fs, D), BF16),
        in_specs=[pl.BlockSpec((None, T, Fs), lambda j, n: (j, 0, 0)), pl.BlockSpec((T, tn), lambda j, n: (0, n))],
        out_specs=pl.BlockSpec((None, Fs, tn), lambda j, n: (j, 0, n)), args=[hm, df], carry=carry)


def _wgrad_up(name, n, da, db, tn, carry=None):
    T, D = n.shape
    J, _, Fs = da.shape

    def body(n_ref, da_ref, db_ref, og_ref, ou_ref):
        nv = n_ref[...]
        og_ref[...] = _dot(da_ref[...], nv, TN).astype(BF16)
        ou_ref[...] = _dot(db_ref[...], nv, TN).astype(BF16)

    act = pl.BlockSpec((None, T, Fs), lambda j, m: (j, 0, 0))
    out = pl.BlockSpec((None, Fs, tn), lambda j, m: (j, 0, m))
    shape = jax.ShapeDtypeStruct((J, Fs, D), BF16)
    return _call(
        body, name=name, grid=(J, D // tn), out_shape=(shape, shape),
        in_specs=[pl.BlockSpec((T, tn), lambda j, m: (0, m)), act, act], out_specs=(out, out),
        args=[n, da, db], carry=carry)


def _reduce_mm(name, pairs, once, dims, T, D, tm, steps, init=None, carry=None):
    n_pairs = len(pairs)
    n_once = len(once)
    n_mm = 2 * (n_pairs + n_once)

    def body(*refs):
        pr = refs[:2 * n_pairs]
        on = refs[2 * n_pairs:n_mm]
        o_ref, acc = refs[-2:]
        r = pl.program_id(1)

        @pl.when(r == 0)
        def _():
            acc[...] = jnp.zeros(acc.shape, F32) if init is None else refs[n_mm][...]

        for p in range(n_pairs):
            acc[...] += _dot(pr[2 * p][...], pr[2 * p + 1][...], dims)

        @pl.when(r == steps - 1)
        def _():
            dn = acc[...]
            for p in range(n_once):
                dn = dn + _dot(on[2 * p][...], on[2 * p + 1][...], dims)
            o_ref[...] = dn

    in_specs, args = [], []
    for a, a_spec, w, w_spec in list(pairs) + list(once):
        in_specs += [a_spec, w_spec]
        args += [a, w]
    row = pl.BlockSpec((tm, D), lambda i, r: (i, 0))
    if init is not None:
        in_specs.append(row)
        args.append(init)
    return _call(body, name=name, grid=(T // tm, steps), in_specs=in_specs, out_specs=row, out_shape=jax.ShapeDtypeStruct((T, D), F32),
                 args=args, scratch_shapes=[pltpu.VMEM((tm, D), F32)], carry=carry)


def _rmsnorm_bwd(name, dn, x, g, dh, tm, bf16_scale, carry=None):
    T, D = x.shape
    emit_bf16 = bf16_scale is not None

    def body(dn_ref, x_ref, g_ref, dh_ref, *outs):
        dxn, dg = _rms_bwd(dn_ref[...], x_ref[...], g_ref[...])
        dx = dh_ref[...] + dxn
        outs[0][...] = dx
        if emit_bf16:
            outs[2][...] = (bf16_scale * dx).astype(BF16)

        @pl.when(pl.program_id(0) == 0)
        def _():
            outs[1][...] = dg

        @pl.when(pl.program_id(0) > 0)
        def _():
            outs[1][...] += dg

    row = pl.BlockSpec((tm, D), lambda i: (i, 0))
    gain = pl.BlockSpec((1, D), lambda i: (0, 0))
    out_shape = [jax.ShapeDtypeStruct((T, D), F32), jax.ShapeDtypeStruct((1, D), F32)]
    out_specs = [row, gain]
    if emit_bf16:
        out_shape.append(jax.ShapeDtypeStruct((T, D), BF16))
        out_specs.append(row)
    return _call(body, name=name, grid=(T // tm,), in_specs=[row, row, gain, row], out_specs=tuple(out_specs),
                 out_shape=tuple(out_shape), args=[dn, x, g, dh], carry=carry)


def _loss_grad(name, y, target, tm):
    T, D = y.shape

    def body(y_ref, t_ref, dy_ref, dyh_ref, sq_ref):
        diff = y_ref[...] - t_ref[...]
        sq = jnp.sum(jnp.sum(diff * diff, axis=1, keepdims=True), axis=0, keepdims=True)
        dy = diff * (1.0 / D)
        dy_ref[...] = dy
        dyh_ref[...] = (0.5 * dy).astype(BF16)

        @pl.when(pl.program_id(0) == 0)
        def _():
            sq_ref[...] = sq

        @pl.when(pl.program_id(0) > 0)
        def _():
            sq_ref[...] += sq

    row = pl.BlockSpec((tm, D), lambda i: (i, 0))
    return pl.pallas_call(
        body, out_shape=(jax.ShapeDtypeStruct((T, D), F32), jax.ShapeDtypeStruct((T, D), BF16), jax.ShapeDtypeStruct((1, 1), F32)),
        grid=(T // tm,), in_specs=[row, row], out_specs=(row, row, pl.BlockSpec((1, 1), lambda i: (0, 0))),
        name=name, compiler_params=_params(1))(y, target)


def _ffn_up(name, n, wg, wu, tm, carry=None):
    T, D = n.shape
    J, Fs, _ = wg.shape

    def body(n_ref, wg_ref, wu_ref, a_ref, b_ref, h_ref):
        xv = n_ref[...]
        a = _dot(xv, wg_ref[...], NT)
        b = _dot(xv, wu_ref[...], NT)
        a_ref[...] = a.astype(BF16)
        b_ref[...] = b.astype(BF16)
        h_ref[...] = (a * jax.nn.sigmoid(a) * b).astype(BF16)

    act = jax.ShapeDtypeStruct((J, T, Fs), BF16)
    wspec = pl.BlockSpec((None, Fs, D), lambda j, i: (j, 0, 0))
    aspec = pl.BlockSpec((None, tm, Fs), lambda j, i: (j, i, 0))
    return _call(
        body, name=name, grid=(J, T // tm), out_shape=(act, act, act),
        in_specs=[pl.BlockSpec((tm, D), lambda j, i: (i, 0)), wspec, wspec], out_specs=(aspec, aspec, aspec),
        args=[n, wg, wu], carry=carry)


def _ffn_gate(name, n, wg, tm, carry=None):
    T, D = n.shape
    J, Fs, _ = wg.shape

    def body(n_ref, wg_ref, a_ref):
        a_ref[...] = _dot(n_ref[...], wg_ref[...], NT).astype(BF16)

    aspec = pl.BlockSpec((None, tm, Fs), lambda j, i: (j, i, 0))
    return _call(
        body, name=name, grid=(J, T // tm), out_shape=jax.ShapeDtypeStruct((J, T, Fs), BF16),
        in_specs=[pl.BlockSpec((tm, D), lambda j, i: (i, 0)), pl.BlockSpec((None, Fs, D), lambda j, i: (j, 0, 0))],
        out_specs=aspec, args=[n, wg], carry=carry)


def _ffn_up_only(name, n, wu, a, tm, carry=None):
    T, D = n.shape
    J, Fs, _ = wu.shape

    def body(n_ref, wu_ref, a_ref, b_ref, h_ref):
        b = _dot(n_ref[...], wu_ref[...], NT)
        a = a_ref[...].astype(F32)
        b_ref[...] = b.astype(BF16)
        h_ref[...] = (a * jax.nn.sigmoid(a) * b).astype(BF16)

    act = jax.ShapeDtypeStruct((J, T, Fs), BF16)
    aspec = pl.BlockSpec((None, tm, Fs), lambda j, i: (j, i, 0))
    return _call(
        body, name=name, grid=(J, T // tm), out_shape=(act, act),
        in_specs=[pl.BlockSpec((tm, D), lambda j, i: (i, 0)), pl.BlockSpec((None, Fs, D), lambda j, i: (j, 0, 0)), aspec],
        out_specs=(aspec, aspec), args=[n, wu, a], carry=carry)


def _ffn_down(name, hm, wd, resid, tm, tn, carry=None):
    J, T, Fs = hm.shape
    D = wd.shape[2]

    def body(h_ref, w_ref, r_ref, o_ref, acc):
        j = pl.program_id(2)

        @pl.when(j == 0)
        def _():
            acc[...] = jnp.zeros(acc.shape, F32)

        acc[...] += _dot(h_ref[...], w_ref[...], NN)

        @pl.when(j == J - 1)
        def _():
            o_ref[...] = r_ref[...] + 0.5 * acc[...]

    tile = pl.BlockSpec((tm, tn), lambda i, n, j: (i, n))
    return _call(
        body, name=name, grid=(T // tm, D // tn, J), out_shape=jax.ShapeDtypeStruct((T, D), F32),
        in_specs=[pl.BlockSpec((None, tm, Fs), lambda i, n, j: (j, i, 0)), pl.BlockSpec((None, Fs, tn), lambda i, n, j: (j, 0, n)), tile],
        out_specs=tile, scratch_shapes=[pltpu.VMEM((tm, tn), F32)], args=[hm, wd, resid], carry=carry)


def _ffn_bwd_mid(name, dfh, wd, a, b, tm, carry=None):
    T, D = dfh.shape
    J, Fs, _ = wd.shape

    def body(df_ref, w_ref, a_ref, b_ref, da_ref, db_ref):
        dhm = _dot(df_ref[...], w_ref[...], NT)
        av = a_ref[...].astype(F32)
        bv = b_ref[...].astype(F32)
        sg = jax.nn.sigmoid(av)
        da_ref[...] = (dhm * bv * (sg * (1.0 + av * (1.0 - sg)))).astype(BF16)
        db_ref[...] = (dhm * (av * sg)).astype(BF16)

    act = jax.ShapeDtypeStruct((J, T, Fs), BF16)
    aspec = pl.BlockSpec((None, tm, Fs), lambda j, i: (j, i, 0))
    return _call(
        body, name=name, grid=(J, T // tm), out_shape=(act, act),
        in_specs=[pl.BlockSpec((tm, D), lambda j, i: (i, 0)), pl.BlockSpec((None, Fs, D), lambda j, i: (j, 0, 0)), aspec, aspec],
        out_specs=(aspec, aspec), args=[dfh, wd, a, b], carry=carry)


def _rot_half(y, lane):
    first = (lane & (HEAD_DIM // 2)) == 0
    return jnp.where(first, pltpu.roll(y, LANES - HEAD_DIM // 2, 1), pltpu.roll(y, HEAD_DIM // 2, 1))


def _head_rstd(x, lo):
    sq = x * x
    ss_a = jnp.sum(jnp.where(lo, sq, 0.0), axis=-1, keepdims=True)
    ss_b = jnp.sum(jnp.where(lo, 0.0, sq), axis=-1, keepdims=True)
    return lax.rsqrt(jnp.where(lo, ss_a, ss_b) * (1.0 / HEAD_DIM) + EPS)


def _headnorm_fwd(name, proj, col_off, ncb, gains, tm, scale, rope=None, dup=False):
    T = proj.shape[0]
    with_rope = rope is not None
    width = 2 * LANES if dup else LANES

    def body(*refs):
        if with_rope:
            x_ref, g_ref, cos_ref, sin_ref, o_ref = refs
        else:
            x_ref, g_ref, o_ref = refs
        xv = x_ref[...]
        lane = lax.broadcasted_iota(jnp.int32, xv.shape, 1)
        lo = lane < HEAD_DIM
        y = xv * _head_rstd(xv, lo) * g_ref[...]
        if with_rope:
            y = y * cos_ref[...] + _rot_half(y, lane) * sin_ref[...]
        y = y * scale
        if dup:
            sw = pltpu.roll(y, HEAD_DIM, 1)
            o_ref[:, :LANES] = jnp.where(lo, y, sw).astype(BF16)
            o_ref[:, LANES:] = jnp.where(lo, sw, y).astype(BF16)
        else:
            o_ref[...] = y.astype(BF16)

    in_specs = [pl.BlockSpec((tm, LANES), lambda c, i: (i, col_off + c)), pl.BlockSpec((None, 1, LANES), lambda c, i: (c, 0, 0))]
    args = [proj, gains]
    if with_rope:
        tab = pl.BlockSpec((tm, LANES), lambda c, i: (i, 0))
        in_specs += [tab, tab]
        args += list(rope)
    return pl.pallas_call(
        body, out_shape=jax.ShapeDtypeStruct((T, ncb * width), BF16), grid=(ncb, T // tm),
        in_specs=in_specs, out_specs=pl.BlockSpec((tm, width), lambda c, i: (i, c)),
        name=name, compiler_params=_params(2))(*args)


def _headnorm_bwd(name, dy, proj, col_off, ncb, gains, group, tm, scale, rope=None, fold=False, norm=True):
    T = dy.shape[0]
    with_rope = rope is not None
    n_groups = ncb // group
    dy_width = 4 * LANES if fold else LANES

    def body(*refs):
        refs = list(refs)
        dy_ref = refs.pop(0)
        x_ref = refs.pop(0) if norm else None
        g_ref = refs.pop(0) if norm else None
        cos_ref = refs.pop(0) if with_rope else None
        sin_ref = refs.pop(0) if with_rope else None
        dx_ref = refs.pop(0)
        dg_ref = refs.pop(0) if norm else None
        c = pl.program_id(0)
        i = pl.program_id(1)
        d = dy_ref[...]
        lane = lax.broadcasted_iota(jnp.int32, (d.shape[0], LANES), 1)
        lo = lane < HEAD_DIM
        if fold:
            t0 = d[:, 0:LANES] + d[:, LANES:2 * LANES]
            t1 = d[:, 2 * LANES:3 * LANES] + d[:, 3 * LANES:4 * LANES]
            d = jnp.where(lo, t0 + pltpu.roll(t0, HEAD_DIM, 1), t1 + pltpu.roll(t1, HEAD_DIM, 1))
        d = d * scale
        if with_rope:
            d = d * cos_ref[...] + _rot_half(d * sin_ref[...], lane)
        if not norm:
            dx_ref[...] = d.astype(BF16)
            return
        xv = x_ref[...]
        gv = g_ref[...]
        r = _head_rstd(xv, lo)
        xh = xv * r
        dxh = d * gv
        pr = dxh * xh
        m_a = jnp.sum(jnp.where(lo, pr, 0.0), axis=-1, keepdims=True)
        m_b = jnp.sum(jnp.where(lo, 0.0, pr), axis=-1, keepdims=True)
        mean = jnp.where(lo, m_a, m_b) * (1.0 / HEAD_DIM)
        dx_ref[...] = (r * (dxh - xh * mean)).astype(BF16)
        dgp = jnp.sum(d * xh, axis=0, keepdims=True)
        dgp = dgp + pltpu.roll(dgp, HEAD_DIM, 1)
        first = jnp.logical_and(c % group == 0, i == 0)

        @pl.when(first)
        def _():
            dg_ref[...] = dgp

        @pl.when(jnp.logical_not(first))
        def _():
            dg_ref[...] += dgp

    in_specs = [pl.BlockSpec((tm, dy_width), lambda c, i: (i, c))]
    args = [dy]
    if norm:
        in_specs += [pl.BlockSpec((tm, LANES), lambda c, i: (i, col_off + c)), pl.BlockSpec((None, 1, LANES), lambda c, i: (c, 0, 0))]
        args += [proj, gains]
    if with_rope:
        tab = pl.BlockSpec((tm, LANES), lambda c, i: (i, 0))
        in_specs += [tab, tab]
        args += list(rope)
    out_shape = [jax.ShapeDtypeStruct((T, ncb * LANES), BF16)]
    out_specs = [pl.BlockSpec((tm, LANES), lambda c, i: (i, c))]
    if norm:
        out_shape.append(jax.ShapeDtypeStruct((n_groups, 1, LANES), F32))
        out_specs.append(pl.BlockSpec((None, 1, LANES), lambda c, i: (c // group, 0, 0)))
    res = pl.pallas_call(
        body, out_shape=tuple(out_shape), grid=(ncb, T // tm), in_specs=in_specs, out_specs=tuple(out_specs),
        name=name, compiler_params=_params(2))(*args)
    return res if norm else (res[0], None)


def _dot_exact(x, tri):
    hi = x.astype(BF16)
    r1 = x - hi.astype(F32)
    mid = r1.astype(BF16)
    lo = (r1 - mid.astype(F32)).astype(BF16)
    return _dot(hi, tri, NN) + _dot(mid, tri, NN) + _dot(lo, tri, NN)


def _forget_fwd(name, zt, bias):
    H, T = zt.shape
    blk = min(256, T)

    def body(z_ref, b_ref, c_ref, s_ref):
        z = z_ref[...] + b_ref[...]
        s_ref[...] = jax.nn.sigmoid(-z)
        lf = jnp.minimum(z, 0.0) - jnp.log(1.0 + jnp.exp(-jnp.abs(z)))
        tri = (lax.broadcasted_iota(jnp.int32, (blk, blk), 0) <= lax.broadcasted_iota(jnp.int32, (blk, blk), 1)).astype(BF16)
        carry = jnp.zeros((H, 1), F32)
        for bi in range(T // blk):
            xb = lf[:, bi * blk:(bi + 1) * blk]
            c_ref[:, bi * blk:(bi + 1) * blk] = _dot_exact(xb, tri) + carry
            carry = carry + jnp.sum(xb, axis=-1, keepdims=True)

    shape = jax.ShapeDtypeStruct((H, T), F32)
    full = pl.BlockSpec((H, T), lambda i: (0, 0))
    return pl.pallas_call(
        body, out_shape=(shape, shape), grid=(1,), in_specs=[full, pl.BlockSpec((H, 1), lambda i: (0, 0))],
        out_specs=(full, full), name=name, compiler_params=_params(1))(zt, bias)


def _forget_bwd(name, dct, drt, sgt):
    H, T = dct.shape
    blk = min(256, T)

    def body(dc_ref, dr_ref, s_ref, dz_ref, db_ref):
        dc = dc_ref[...] + dr_ref[...]
        tri = (lax.broadcasted_iota(jnp.int32, (blk, blk), 0) >= lax.broadcasted_iota(jnp.int32, (blk, blk), 1)).astype(BF16)
        carry = jnp.zeros((H, 1), F32)
        db = jnp.zeros((H, 1), F32)
        for bi in reversed(range(T // blk)):
            xb = dc[:, bi * blk:(bi + 1) * blk]
            dz = (_dot_exact(xb, tri) + carry) * s_ref[:, bi * blk:(bi + 1) * blk]
            dz_ref[:, bi * blk:(bi + 1) * blk] = dz
            db = db + jnp.sum(dz, axis=-1, keepdims=True)
            carry = carry + jnp.sum(xb, axis=-1, keepdims=True)
        db_ref[...] = db

    full = pl.BlockSpec((H, T), lambda i: (0, 0))
    return pl.pallas_call(
        body, out_shape=(jax.ShapeDtypeStruct((H, T), F32), jax.ShapeDtypeStruct((H, 1), F32)), grid=(1,),
        in_specs=[full, full, full], out_specs=(full, pl.BlockSpec((H, 1), lambda i: (0, 0))),
        name=name, compiler_params=_params(1))(dct, drt, sgt)


STRIP = 256


def _fox_fwd(name, qk, v, crow, tq, tk, carry=None):
    T, Dh = v.shape
    HP = Dh // LANES
    nk = T // tk
    assert tk % tq == 0 and tq % STRIP == 0
    n_strips = tq // STRIP

    def body(q_ref, k_ref, v_ref, ra_ref, rb_ref, o_ref, la_ref, lb_ref, s_ref, p_ref, m_ref, l_ref, acc_ref):
        i = pl.program_id(1)
        q2 = q_ref[...]
        lo = _lane_lo((tq, LANES))
        q_st = jnp.concatenate([_keep(lo, q2), _keep(jnp.logical_not(lo), q2)], axis=0)
        r_refs = (ra_ref, rb_ref)
        m_ref[...] = jnp.full(m_ref.shape, NEG, F32)
        l_ref[...] = jnp.zeros(l_ref.shape, F32)
        acc_ref[...] = jnp.zeros(acc_ref.shape, F32)
        rel = lax.broadcasted_iota(jnp.int32, (STRIP, tk), 0) - lax.broadcasted_iota(jnp.int32, (STRIP, tk), 1)

        def chunk(kc, masked):
            start = pl.multiple_of(kc * tk, tk)
            kb = k_ref[pl.ds(start, tk), :]
            vb = v_ref[pl.ds(start, tk), :]
            s_ref[...] = _dot(q_st, kb, NT)
            for h in range(2):
                cs = r_refs[h][kc]
                for st in range(n_strips):
                    rows = pl.ds(h * tq + st * STRIP, STRIP)
                    s = s_ref[rows, :] - cs
                    if masked:
                        s = jnp.where(rel >= start - (i * tq + st * STRIP), s, NEG)
                    m_old = m_ref[rows, :]
                    mn = jnp.maximum(m_old, jnp.max(s, axis=-1, keepdims=True))
                    p = jnp.exp(s - mn)
                    alpha = jnp.exp(m_old - mn)
                    l_ref[rows, :] = alpha * l_ref[rows, :] + jnp.sum(p, axis=-1, keepdims=True)
                    m_ref[rows, :] = mn
                    p_ref[rows, :] = p.astype(BF16)
                    acc_ref[rows, :] = acc_ref[rows, :] * alpha
            acc_ref[...] += _dot(p_ref[...], vb, NN)

        n_full = (i * tq) // tk

        def full_chunk(kc, _):
            chunk(kc, False)
            return 0

        lax.fori_loop(0, n_full, full_chunk, 0)
        chunk(n_full, True)
        top, bot = pl.ds(0, tq), pl.ds(tq, tq)
        o_ref[...] = jnp.where(lo, acc_ref[top, :] / l_ref[top, :], acc_ref[bot, :] / l_ref[bot, :])
        la_ref[...] = m_ref[top, :] + jnp.log(l_ref[top, :])
        lb_ref[...] = m_ref[bot, :] + jnp.log(l_ref[bot, :])

    row = lambda off: pl.BlockSpec((None, nk, 1, tk), lambda h, i: (2 * h + off, 0, 0, 0))
    lse = jax.ShapeDtypeStruct((HP, T, 1), F32)
    lspec = pl.BlockSpec((None, tq, 1), lambda h, i: (h, i, 0))
    scratch = [pltpu.VMEM((2 * tq, tk), F32), pltpu.VMEM((2 * tq, tk), BF16), pltpu.VMEM((2 * tq, 1), F32),
               pltpu.VMEM((2 * tq, 1), F32), pltpu.VMEM((2 * tq, LANES), F32)]
    return _call(
        body, name=name, grid=(HP, T // tq), out_shape=(jax.ShapeDtypeStruct((T, Dh), F32), lse, lse),
        in_specs=[pl.BlockSpec((tq, LANES), lambda h, i: (i, h)), pl.BlockSpec((T, LANES), lambda h, i: (0, HP + h)),
                  pl.BlockSpec((T, LANES), lambda h, i: (0, h)), row(0), row(1)],
        out_specs=(pl.BlockSpec((tq, LANES), lambda h, i: (i, h)), lspec, lspec),
        args=[qk, qk, v, crow, crow], scratch_shapes=scratch, carry=carry)


def _fox_bwd(name, qk, v, o, do, crow, lse_a, lse_b, tq, tk, carry=None):
    T, Dh = v.shape
    HP = Dh // LANES
    nk = T // tk
    scale = HEAD_DIM ** -0.5
    assert tk % tq == 0 and tq % STRIP == 0
    n_strips = tq // STRIP

    def body(q_ref, k_ref, v_ref, o_ref, do_ref, ra_ref, rb_ref, la_ref, lb_ref,
             dq_ref, dk_ref, dv_ref, dca_ref, dcb_ref, dra_ref, drb_ref, s_ref, dp_ref, p_ref, ds_ref, dq_acc, dsum_ref):
        i = pl.program_id(1)

        @pl.when(i == 0)
        def _():
            dk_ref[...] = jnp.zeros_like(dk_ref)
            dv_ref[...] = jnp.zeros_like(dv_ref)
            dca_ref[...] = jnp.zeros_like(dca_ref)
            dcb_ref[...] = jnp.zeros_like(dcb_ref)

        q2 = q_ref[...]
        do2 = do_ref[...]
        lo = _lane_lo((tq, LANES))
        hi = jnp.logical_not(lo)
        q_st = jnp.concatenate([_keep(lo, q2), _keep(hi, q2)], axis=0)
        do_st = jnp.concatenate([_keep(lo, do2), _keep(hi, do2)], axis=0)
        prod = do2.astype(F32) * o_ref[...]
        dsum_ref[pl.ds(0, tq), :] = jnp.sum(jnp.where(lo, prod, 0.0), axis=-1, keepdims=True)
        dsum_ref[pl.ds(tq, tq), :] = jnp.sum(jnp.where(lo, 0.0, prod), axis=-1, keepdims=True)
        r_refs, l_refs, dc_refs, dr_refs = (ra_ref, rb_ref), (la_ref, lb_ref), (dca_ref, dcb_ref), (dra_ref, drb_ref)
        dq_acc[...] = jnp.zeros(dq_acc.shape, F32)
        dra_ref[...] = jnp.zeros(dra_ref.shape, F32)
        drb_ref[...] = jnp.zeros(drb_ref.shape, F32)
        rel = lax.broadcasted_iota(jnp.int32, (STRIP, tk), 0) - lax.broadcasted_iota(jnp.int32, (STRIP, tk), 1)

        def chunk(kc, masked):
            start = pl.multiple_of(kc * tk, tk)
            kb = k_ref[pl.ds(start, tk), :]
            vb = v_ref[pl.ds(start, tk), :]
            s_ref[...] = _dot(q_st, kb, NT)
            dp_ref[...] = _dot(do_st, vb, NT)
            for h in range(2):
                cs = r_refs[h][kc]
                col_sum = jnp.zeros((1, tk), F32)
                for st in range(n_strips):
                    rows = pl.ds(st * STRIP, STRIP)
                    both = pl.ds(h * tq + st * STRIP, STRIP)
                    s = s_ref[both, :] - cs
                    if masked:
                        s = jnp.where(rel >= start - (i * tq + st * STRIP), s, NEG)
                    p = jnp.exp(s - l_refs[h][rows, :])
                    ds = p * (dp_ref[both, :] - dsum_ref[both, :])
                    p_ref[both, :] = p.astype(BF16)
                    ds_ref[both, :] = ds.astype(BF16)
                    col_sum = col_sum + jnp.sum(ds, axis=0, keepdims=True)
                    dr_refs[h][rows, :] += jnp.sum(ds, axis=-1, keepdims=True)
                dc_refs[h][kc] = dc_refs[h][kc] - col_sum
            dk_ref[pl.ds(start, tk), :] += _dot(ds_ref[...], q_st, TN)
            dv_ref[pl.ds(start, tk), :] += _dot(p_ref[...], do_st, TN)
            dq_acc[...] += _dot(ds_ref[...], kb, NN)

        n_full = (i * tq) // tk

        def full_chunk(kc, _):
            chunk(kc, False)
            return 0

        lax.fori_loop(0, n_full, full_chunk, 0)
        chunk(n_full, True)
        dq_ref[...] = jnp.where(lo, dq_acc[pl.ds(0, tq), :], dq_acc[pl.ds(tq, tq), :]) * scale

    row = lambda off: pl.BlockSpec((None, nk, 1, tk), lambda h, i: (2 * h + off, 0, 0, 0))
    lspec = pl.BlockSpec((None, tq, 1), lambda h, i: (h, i, 0))
    qspec = pl.BlockSpec((tq, LANES), lambda h, i: (i, h))
    full = pl.BlockSpec((T, LANES), lambda h, i: (0, h))
    dcspec = pl.BlockSpec((None, nk, 1, tk), lambda h, i: (h, 0, 0, 0))
    grad = jax.ShapeDtypeStruct((T, Dh), F32)
    dc = jax.ShapeDtypeStruct((HP, nk, 1, tk), F32)
    dr = jax.ShapeDtypeStruct((HP, T, 1), F32)
    scratch = [pltpu.VMEM((2 * tq, tk), F32), pltpu.VMEM((2 * tq, tk), F32), pltpu.VMEM((2 * tq, tk), BF16), pltpu.VMEM((2 * tq, tk), BF16),
               pltpu.VMEM((2 * tq, LANES), F32), pltpu.VMEM((2 * tq, 1), F32)]
    return _call(
        body, name=name, grid=(HP, T // tq), out_shape=(grad, grad, grad, dc, dc, dr, dr),
        in_specs=[qspec, pl.BlockSpec((T, LANES), lambda h, i: (0, HP + h)), full, qspec, qspec, row(0), row(1), lspec, lspec],
        out_specs=(qspec, full, full, dcspec, dcspec, lspec, lspec),
        args=[qk, qk, v, o, do, crow, crow, lse_a, lse_b], scratch_shapes=scratch, carry=carry)


SWA_GROUP = 2
SWA_GROUP_BWD = 4


def _swa_block(n, q_ref, k_ref):
    qs = pl.multiple_of(n * WINDOW, WINDOW)
    ks = pl.multiple_of(jnp.maximum(n - 1, 0) * WINDOW, WINDOW)
    rel = (qs + lax.broadcasted_iota(jnp.int32, (WINDOW, 2 * WINDOW), 0)) - (ks + lax.broadcasted_iota(jnp.int32, (WINDOW, 2 * WINDOW), 1))
    valid = jnp.logical_and(rel >= 0, rel < WINDOW)
    return qs, ks, valid


def _swa_fwd(name, q, kd, vd, sinks, carry=None):
    T, Dh = q.shape
    HP = Dh // LANES

    def body(q_ref, k_ref, v_ref, sa_ref, sb_ref, o_ref, la_ref, lb_ref):
        lo = _lane_lo((WINDOW, LANES))

        top = lax.broadcasted_iota(jnp.int32, (2 * WINDOW, 1), 0) < WINDOW
        sink = jnp.where(top, sa_ref[...], sb_ref[...])

        def block(n, _):
            qs, ks, valid = _swa_block(n, q_ref, k_ref)
            q2 = q_ref[pl.ds(qs, WINDOW), :]
            kb = k_ref[pl.ds(ks, 2 * WINDOW), :]
            vb = v_ref[pl.ds(ks, 2 * WINDOW), :]
            q_st = jnp.concatenate([_keep(lo, q2), _keep(jnp.logical_not(lo), q2)], axis=0)
            s = jnp.where(jnp.concatenate([valid, valid], axis=0), _dot(q_st, kb, NT), NEG)
            m = jnp.maximum(jnp.max(s, axis=-1, keepdims=True), sink)
            p = jnp.exp(s - m)
            l = jnp.sum(p, axis=-1, keepdims=True) + jnp.exp(sink - m)
            o2 = _dot(p.astype(BF16), vb, NN) / l
            lse = m + jnp.log(l)
            o_ref[pl.ds(qs, WINDOW), :] = jnp.where(lo, o2[:WINDOW], o2[WINDOW:])
            la_ref[pl.ds(qs, WINDOW), :] = lse[:WINDOW]
            lb_ref[pl.ds(qs, WINDOW), :] = lse[WINDOW:]
            return 0

        assert (T // WINDOW) % SWA_GROUP == 0

        def group(g, c):
            for b in range(SWA_GROUP):
                c = block(g * SWA_GROUP + b, c)
            return c

        lax.fori_loop(0, T // WINDOW // SWA_GROUP, group, 0)

    full = pl.BlockSpec((T, LANES), lambda h: (0, h))
    kv = pl.BlockSpec((T, LANES), lambda h: (0, h // 2))
    sink = lambda off: pl.BlockSpec((None, 1, 1), lambda h: (2 * h + off, 0, 0))
    lse = jax.ShapeDtypeStruct((HP, T, 1), F32)
    lspec = pl.BlockSpec((None, T, 1), lambda h: (h, 0, 0))
    return _call(
        body, name=name, grid=(HP,), out_shape=(jax.ShapeDtypeStruct((T, Dh), F32), lse, lse),
        in_specs=[full, kv, kv, sink(0), sink(1)], out_specs=(full, lspec, lspec),
        args=[q, kd, vd, sinks, sinks], carry=carry)


def _swa_bwd(name, q, kd, vd, sinks, o, do, lse_a, lse_b, carry=None):
    T, Dh = q.shape
    HP = Dh // LANES
    scale = HEAD_DIM ** -0.5

    def body(q_ref, k_ref, v_ref, sa_ref, sb_ref, o_ref, do_ref, la_ref, lb_ref, dq_ref, dk_ref, dv_ref, dsa_ref, dsb_ref):
        lo = _lane_lo((WINDOW, LANES))
        hi = jnp.logical_not(lo)
        dk_ref[...] = jnp.zeros_like(dk_ref)
        dv_ref[...] = jnp.zeros_like(dv_ref)

        top = lax.broadcasted_iota(jnp.int32, (2 * WINDOW, 1), 0) < WINDOW
        sink = jnp.where(top, sa_ref[...], sb_ref[...])

        def block(n, dsinks):
            qs, ks, valid = _swa_block(n, q_ref, k_ref)
            rows = pl.ds(qs, WINDOW)
            q2 = q_ref[rows, :]
            do2 = do_ref[rows, :]
            kb = k_ref[pl.ds(ks, 2 * WINDOW), :]
            vb = v_ref[pl.ds(ks, 2 * WINDOW), :]
            prod = do2.astype(F32) * o_ref[rows, :]
            q_st = jnp.concatenate([_keep(lo, q2), _keep(hi, q2)], axis=0)
            do_st = jnp.concatenate([_keep(lo, do2), _keep(hi, do2)], axis=0)
            dsum = jnp.concatenate([jnp.sum(jnp.where(lo, prod, 0.0), axis=-1, keepdims=True),
                                    jnp.sum(jnp.where(lo, 0.0, prod), axis=-1, keepdims=True)], axis=0)
            lse = jnp.concatenate([la_ref[rows, :], lb_ref[rows, :]], axis=0)
            s = jnp.where(jnp.concatenate([valid, valid], axis=0), _dot(q_st, kb, NT), NEG)
            p = jnp.exp(s - lse)
            ds = p * (_dot(do_st, vb, NT) - dsum)
            dsb = ds.astype(BF16)
            dq2 = _dot(dsb, kb, NN)
            dq_ref[rows, :] = jnp.where(lo, dq2[:WINDOW], dq2[WINDOW:]) * scale
            dk_ref[pl.ds(ks, 2 * WINDOW), :] += _dot(dsb, q_st, TN)
            dv_ref[pl.ds(ks, 2 * WINDOW), :] += _dot(p.astype(BF16), do_st, TN)
            gone = jnp.exp(sink - lse) * dsum
            return (dsinks[0] - jnp.sum(gone[:WINDOW], axis=0, keepdims=True),
                    dsinks[1] - jnp.sum(gone[WINDOW:], axis=0, keepdims=True))

        assert (T // WINDOW) % SWA_GROUP_BWD == 0

        def group(g, c):
            for b in range(SWA_GROUP_BWD):
                c = block(g * SWA_GROUP_BWD + b, c)
            return c

        dsa, dsb_ = lax.fori_loop(0, T // WINDOW // SWA_GROUP_BWD, group, (jnp.zeros((1, 1), F32), jnp.zeros((1, 1), F32)))
        dsa_ref[...] = dsa
        dsb_ref[...] = dsb_

    full = pl.BlockSpec((T, LANES), lambda h: (0, h))
    kv = pl.BlockSpec((T, LANES), lambda h: (0, h // 2))
    sink = lambda off: pl.BlockSpec((None, 1, 1), lambda h: (2 * h + off, 0, 0))
    lspec = pl.BlockSpec((None, T, 1), lambda h: (h, 0, 0))
    dsink = pl.BlockSpec((None, 1, 1), lambda h: (h, 0, 0))
    grad = jax.ShapeDtypeStruct((T, Dh), F32)
    ds_shape = jax.ShapeDtypeStruct((HP, 1, 1), F32)
    return _call(
        body, name=name, grid=(HP,), out_shape=(grad, grad, grad, ds_shape, ds_shape),
        in_specs=[full, kv, kv, sink(0), sink(1), full, full, lspec, lspec],
        out_specs=(full, full, full, dsink, dsink),
        args=[q, kd, vd, sinks, sinks, o, do, lse_a, lse_b], carry=carry)


def _place():
    return lax.axis_index("x"), lax.axis_index("y"), lax.axis_index("c")


def _run_carry(name, carry):
    c_in, c_out = len(carry.inputs), len(carry.out_shapes)

    def body(*refs):
        ins, outs, scr = refs[:c_in], refs[c_in:c_in + c_out], refs[c_in + c_out:]
        carry.start(ins, outs, scr)
        carry.middle(ins, outs, scr)
        carry.finish(ins, outs, scr)

    return pl.pallas_call(
        body, out_shape=tuple(carry.out_shapes), in_specs=[_HBM] * c_in, out_specs=tuple([_HBM] * c_out),
        scratch_shapes=carry.scratch, name=name)(*carry.inputs)


def _gather_carry(shards):
    n = len(shards)

    def plan(ins, outs, scr):
        send, recv, local = scr
        x, y, c = _place()
        me, sibling = (x, y, c), (x, y, 1 - c)
        partner, other, diag = (x ^ c, y ^ (1 - c)), (x ^ (1 - c), y ^ c), (1 - x, 1 - y)

        def copy(w, k, block, to, src=None):
            slot = 4 * block[0] + 2 * block[1] + block[2]
            return pltpu.make_async_remote_copy(
                src_ref=outs[w].at[slot] if src is None else src, dst_ref=outs[w].at[slot],
                send_sem=send.at[w, k], recv_sem=recv.at[w, k], device_id=to, device_id_type=MESH)

        def own():
            return [pltpu.make_async_copy(ins[w], outs[w].at[4 * x + 2 * y + c], local.at[w]) for w in range(n)]

        return copy, own, me, sibling, partner, other, diag, c

    def start(ins, outs, scr):
        copy, own, me, sibling, partner, other, _, c = plan(ins, outs, scr)
        for cp in own():
            cp.start()
        for w in range(n):
            copy(w, 1, me, (*partner, c), src=ins[w]).start()
            copy(w, 2, me, (*other, c), src=ins[w]).start()
            copy(w, 0, me, sibling, src=ins[w]).start()

    def middle(ins, outs, scr):
        copy, _, me, sibling, partner, other, _, c = plan(ins, outs, scr)
        for w in range(n):
            copy(w, 1, (*partner, c), me).wait_recv()
            copy(w, 3, (*partner, c), (*other, c)).start()
            copy(w, 4, (*partner, c), sibling).start()
        for w in range(n):
            copy(w, 2, (*other, c), me).wait_recv()
            copy(w, 5, (*other, c), sibling).start()

    def finish(ins, outs, scr):
        copy, own, me, sibling, partner, other, diag, c = plan(ins, outs, scr)
        for w in range(n):
            copy(w, 3, (*diag, c), me).wait_recv()
            copy(w, 6, (*diag, c), sibling).start()
        for w in range(n):
            copy(w, 0, sibling, me).wait_recv()
            copy(w, 4, (*other, 1 - c), me).wait_recv()
            copy(w, 5, (*partner, 1 - c), me).wait_recv()
            copy(w, 6, (*diag, 1 - c), me).wait_recv()
        for w in range(n):
            sent = [copy(w, 0, me, sibling, src=ins[w]), copy(w, 1, me, (*partner, c), src=ins[w]), copy(w, 2, me, (*other, c), src=ins[w]),
                    copy(w, 3, (*partner, c), (*other, c)), copy(w, 4, (*partner, c), sibling), copy(w, 5, (*other, c), sibling),
                    copy(w, 6, (*diag, c), sibling)]
            for cp in sent:
                cp.wait_send()
        for cp in own():
            cp.wait()

    return _Carry(shards, [jax.ShapeDtypeStruct((N_DEV,) + s.shape, s.dtype) for s in shards],
                  [pltpu.SemaphoreType.DMA((n, 7)), pltpu.SemaphoreType.DMA((n, 7)), pltpu.SemaphoreType.DMA((n,))], start, finish, middle)


def _sibling_carry(grads):
    n = len(grads)

    def copies(ins, outs, scr):
        send, recv = scr
        x, y, c = _place()
        return [pltpu.make_async_remote_copy(
            src_ref=ins[w].at[2 * q + (1 - c)], dst_ref=outs[w].at[q], send_sem=send.at[w, q], recv_sem=recv.at[w, q],
            device_id=(x, y, 1 - c), device_id_type=MESH) for w in range(n) for q in range(4)]

    def start(ins, outs, scr):
        for cp in copies(ins, outs, scr):
            cp.start()

    def finish(ins, outs, scr):
        for cp in copies(ins, outs, scr):
            cp.wait()

    return _Carry(grads, [jax.ShapeDtypeStruct((4,) + g.shape[1:], g.dtype) for g in grads],
                  [pltpu.SemaphoreType.DMA((n, 4)), pltpu.SemaphoreType.DMA((n, 4))], start, finish)


def _to_partner_carry(sums):
    n = len(sums)

    def copies(ins, outs, scr):
        send, recv = scr
        x, y, c = _place()
        return [pltpu.make_async_remote_copy(
            src_ref=ins[w].at[k], dst_ref=outs[2 * w + k], send_sem=send.at[w, k], recv_sem=recv.at[w, k],
            device_id=(x ^ c, y ^ (1 - c), c), device_id_type=MESH) for w in range(n) for k in range(2)]

    def start(ins, outs, scr):
        for cp in copies(ins, outs, scr):
            cp.start()

    def finish(ins, outs, scr):
        for cp in copies(ins, outs, scr):
            cp.wait()

    return _Carry(sums, [jax.ShapeDtypeStruct(s.shape[1:], s.dtype) for s in sums for _ in range(2)],
                  [pltpu.SemaphoreType.DMA((n, 2)), pltpu.SemaphoreType.DMA((n, 2))], start, finish)


def _to_other_carry(blocks):
    n = len(blocks)

    def copies(ins, outs, scr):
        send, recv = scr
        x, y, c = _place()
        return [pltpu.make_async_remote_copy(
            src_ref=ins[w], dst_ref=outs[w], send_sem=send.at[w], recv_sem=recv.at[w],
            device_id=(x ^ (1 - c), y ^ c, c), device_id_type=MESH) for w in range(n)]

    def start(ins, outs, scr):
        for cp in copies(ins, outs, scr):
            cp.start()

    def finish(ins, outs, scr):
        for cp in copies(ins, outs, scr):
            cp.wait()

    return _Carry(blocks, [jax.ShapeDtypeStruct(b.shape, b.dtype) for b in blocks],
                  [pltpu.SemaphoreType.DMA((n,)), pltpu.SemaphoreType.DMA((n,))], start, finish)


def _gather_small(packed):
    R, C = packed.shape

    def body(in_ref, out_ref, send, recv):
        x, y, c = _place()
        mine = 4 * x + 2 * y + c
        out_ref[mine] = in_ref[...]
        copies = []
        for k in range(1, N_DEV):
            peer = (x ^ (k >> 2), y ^ ((k >> 1) & 1), c ^ (k & 1))
            copies.append(pltpu.make_async_remote_copy(
                src_ref=in_ref, dst_ref=out_ref.at[mine], send_sem=send.at[k - 1], recv_sem=recv.at[k - 1],
                device_id=peer, device_id_type=MESH))
        for cp in copies:
            cp.start()
        for cp in copies:
            cp.wait()

    vmem = pl.BlockSpec(memory_space=pltpu.VMEM)
    return pl.pallas_call(
        body, out_shape=jax.ShapeDtypeStruct((N_DEV, R, C), F32), in_specs=[vmem], out_specs=vmem,
        scratch_shapes=[pltpu.SemaphoreType.DMA((N_DEV - 1,)), pltpu.SemaphoreType.DMA((N_DEV - 1,))],
        name="small_grads_all_gather")(packed)


def _adamw(w, g, m, v):
    m = ADAM_B1 * m + (1.0 - ADAM_B1) * g
    v = ADAM_B2 * v + (1.0 - ADAM_B2) * (g * g)
    m_hat = m / (1.0 - ADAM_B1 ** ADAM_STEP)
    v_hat = v / (1.0 - ADAM_B2 ** ADAM_STEP)
    delta = -ADAM_LR * (m_hat / (jnp.sqrt(v_hat) + ADAM_EPS) + ADAM_WD * w)
    return delta, m, v


def _pair_add(name, grads, received, slots):
    _, R, C = grads.shape
    tr = _row_tile(R)

    def body(s_ref, g_ref, r_ref, o_ref):
        o_ref[...] = (g_ref[...].astype(F32) + r_ref[...].astype(F32)).astype(BF16)

    return pl.pallas_call(
        body, out_shape=jax.ShapeDtypeStruct((3, R, C), BF16),
        grid_spec=pltpu.PrefetchScalarGridSpec(
            num_scalar_prefetch=1, grid=(3, R // tr),
            in_specs=[pl.BlockSpec((None, tr, C), lambda k, i, s: (s[k], i, 0)), pl.BlockSpec((None, tr, C), lambda k, i, s: (s[3 + k], i, 0))],
            out_specs=pl.BlockSpec((None, tr, C), lambda k, i, s: (k, i, 0))),
        name=name, compiler_params=_params(2))(slots, grads, received)


def _relay_add(name, sums, relayed):
    _, R, C = sums.shape
    tr = _row_tile(R)

    def body(s_ref, r_ref, o_ref):
        o_ref[...] = (s_ref[...].astype(F32) + r_ref[...].astype(F32)).astype(BF16)

    blk = pl.BlockSpec((tr, C), lambda i: (i, 0))
    return pl.pallas_call(
        body, out_shape=jax.ShapeDtypeStruct((R, C), BF16), grid=(R // tr,),
        in_specs=[pl.BlockSpec((None, tr, C), lambda i: (2, i, 0)), blk], out_specs=blk,
        name=name, compiler_params=_params(1))(sums, relayed)


def _adam_shard(name, grads, from_sibling, received, w, m, v, own):
    R, C = w.shape
    tr = _row_tile(R, 128)
    tc = C if tr < R or C % (2 * LANES) else 2 * LANES

    def body(o_ref, g_ref, s_ref, ra_ref, rb_ref, w_ref, m_ref, v_ref, g_out, d_out, m_out, v_out):
        g = (g_ref[...].astype(F32) + s_ref[...].astype(F32)) + ra_ref[...].astype(F32) + rb_ref[...].astype(F32)
        delta, mn, vn = _adamw(w_ref[...], g, m_ref[...], v_ref[...])
        g_out[...] = g
        d_out[...] = delta
        m_out[...] = mn
        v_out[...] = vn

    blk = pl.BlockSpec((tr, tc), lambda i, j, o: (i, j))
    shape = jax.ShapeDtypeStruct((R, C), F32)
    return pl.pallas_call(
        body, out_shape=(shape,) * 4,
        grid_spec=pltpu.PrefetchScalarGridSpec(
            num_scalar_prefetch=1, grid=(R // tr, C // tc),
            in_specs=[pl.BlockSpec((None, tr, tc), lambda i, j, o: (o[0], i, j)), pl.BlockSpec((None, tr, tc), lambda i, j, o: (o[1], i, j)),
                      blk, blk, blk, blk, blk],
            out_specs=(blk,) * 4),
        name=name, compiler_params=_params(2))(own, grads, from_sibling, received[0], received[1], w, m, v)


def _adam_small(name, gathered, w, m, v):
    R, C = w.shape

    def body(ga_ref, w_ref, m_ref, v_ref, g_out, d_out, m_out, v_out):
        g = ga_ref[0]
        for d in range(1, N_DEV):
            g = g + ga_ref[d]
        delta, mn, vn = _adamw(w_ref[...], g, m_ref[...], v_ref[...])
        g_out[...] = g
        d_out[...] = delta
        m_out[...] = mn
        v_out[...] = vn

    full = pl.BlockSpec((R, C), lambda i: (0, 0))
    shape = jax.ShapeDtypeStruct((R, C), F32)
    return pl.pallas_call(
        body, out_shape=(shape,) * 4, grid=(1,),
        in_specs=[pl.BlockSpec((N_DEV, R, C), lambda i: (0, 0, 0)), full, full, full], out_specs=(full,) * 4,
        name=name, compiler_params=_params(1))(gathered, w, m, v)


def _pack_small(parts, D, scalar=None):
    g1, gmix, g2, gof, gos, bf, gqf, gkf, gqs, gks, sinks = [p.reshape(-1).astype(F32) for p in parts]
    row3 = jnp.concatenate([gof, gos])
    row4 = jnp.zeros((D,), F32)
    for slot, vec in enumerate((bf, gqf, gkf, gqs, gks, sinks)):
        row4 = lax.dynamic_update_slice(row4, vec, (slot * LANES,))
    zero = jnp.zeros((D,), F32)
    row5 = zero if scalar is None else lax.dynamic_update_slice(zero, jnp.reshape(scalar, (1,)).astype(F32), (0,))
    return jnp.stack([g1, gmix, g2, row3, row4, row5, zero, zero])


def _unpack_small(packed, D, H):
    Dh = D // 2
    row4 = packed[4]
    short = [row4[s * LANES:s * LANES + n] for s, n in enumerate((H, HEAD_DIM, HEAD_DIM, HEAD_DIM, HEAD_DIM, H))]
    vecs = [packed[0], packed[1], packed[2], packed[3, :Dh], packed[3, Dh:]] + short
    return [v[None, :] for v in vecs]


def kernel(x, positions, norm_ffn1_g, ffn1_w_gate, ffn1_w_up, ffn1_w_down, norm_mix_g, w_in, b_forget, fox_q_norm_g, fox_k_norm_g, swa_q_norm_g, swa_k_norm_g, swa_sinks, out_norm_fox_g, out_norm_swa_g, w_out, norm_ffn2_g, ffn2_w_gate, ffn2_w_up, ffn2_w_down, loss_target, m_norm_ffn1_g, m_ffn1_w_gate, m_ffn1_w_up, m_ffn1_w_down, m_norm_mix_g, m_w_in, m_b_forget, m_fox_q_norm_g, m_fox_k_norm_g, m_swa_q_norm_g, m_swa_k_norm_g, m_swa_sinks, m_out_norm_fox_g, m_out_norm_swa_g, m_w_out, m_norm_ffn2_g, m_ffn2_w_gate, m_ffn2_w_up, m_ffn2_w_down, v_norm_ffn1_g, v_ffn1_w_gate, v_ffn1_w_up, v_ffn1_w_down, v_norm_mix_g, v_w_in, v_b_forget, v_fox_q_norm_g, v_fox_k_norm_g, v_swa_q_norm_g, v_swa_k_norm_g, v_swa_sinks, v_out_norm_fox_g, v_out_norm_swa_g, v_w_out, v_norm_ffn2_g, v_ffn2_w_gate, v_ffn2_w_up, v_ffn2_w_down):
    xs = x[0]
    target = loss_target[0]
    T, D = xs.shape
    Dh = D // 2
    H = Dh // HEAD_DIM
    HP = H // 2
    KVW = (H // GQA_GROUP) * HEAD_DIM
    KVB = KVW // LANES
    MAIN = 4 * Dh + 2 * KVW
    F_OFF = 3 * Dh
    tm = min(ROW_TILE_CAP, T)
    tq = min(512, T)
    tk = min(512, T)
    nk = T // tk
    cx, cy, cc = _place()
    near = [2 * (cx ^ cc) + (cy ^ (1 - cc)), 2 * (1 - cx) + (1 - cy), 2 * (cx ^ (1 - cc)) + (cy ^ cc)]
    slots = jnp.stack([2 * q + cc for q in near] + near).astype(jnp.int32)
    own = jnp.stack([4 * cx + 2 * cy + cc, 2 * cx + cy]).astype(jnp.int32)

    tr = jnp.transpose
    big_w = [tr(ffn1_w_gate[0]), tr(ffn1_w_up[0]), ffn1_w_down[0], tr(w_in[0]), w_out[0], tr(ffn2_w_gate[0]), tr(ffn2_w_up[0]),
             ffn2_w_down[0]]
    big_m = [tr(m_ffn1_w_gate[0]), tr(m_ffn1_w_up[0]), m_ffn1_w_down[0], tr(m_w_in[0]), m_w_out[0], tr(m_ffn2_w_gate[0]),
             tr(m_ffn2_w_up[0]), m_ffn2_w_down[0]]
    big_v = [tr(v_ffn1_w_gate[0]), tr(v_ffn1_w_up[0]), v_ffn1_w_down[0], tr(v_w_in[0]), v_w_out[0], tr(v_ffn2_w_gate[0]),
             tr(v_ffn2_w_up[0]), v_ffn2_w_down[0]]
    transposed = {"ffn1_w_gate", "ffn1_w_up", "w_in", "ffn2_w_gate", "ffn2_w_up"}
    names = ["ffn1_w_gate", "ffn1_w_up", "ffn1_w_down", "w_in", "w_out", "ffn2_w_gate", "ffn2_w_up", "ffn2_w_down"]
    sh = dict(zip(names, [w.astype(BF16) for w in big_w]))
    lane = jnp.arange(LANES)
    inv_freq = ROPE_THETA ** (-(2.0 * (lane % (HEAD_DIM // 2))).astype(F32) / HEAD_DIM)
    ang = positions[0].astype(F32)[:, None] * inv_freq[None, :]
    cos_t = jnp.cos(ang)
    sin_t = jnp.where((lane & (HEAD_DIM // 2)) == 0, -1.0, 1.0)[None, :] * jnp.sin(ang)
    rope = (cos_t, sin_t)

    def pair_gain(g, blocks):
        return jnp.tile(jnp.concatenate([g[0], g[0]])[None, None, :], (blocks, 1, 1))

    n1, (wg1,) = _rmsnorm_fwd("ffn1_norm", xs, norm_ffn1_g, tm, carry=_gather_carry([sh["ffn1_w_gate"]]))
    a1, (wu1,) = _ffn_gate("ffn1_gate", n1, wg1, tm, carry=_gather_carry([sh["ffn1_w_up"]]))
    (b1, hm1), (wd1,) = _ffn_up_only("ffn1_up", n1, wu1, a1, tm, carry=_gather_carry([sh["ffn1_w_down"]]))
    h1, (win_g,) = _ffn_down("ffn1_down", hm1, wd1, xs, tm, D, carry=_gather_carry([sh["w_in"]]))
    n_in = win_g.shape[1]
    win_t = win_g.reshape(N_DEV * n_in, D)
    win_main = jnp.concatenate([win_t[:F_OFF], win_t[F_OFF + H:]], axis=0)
    win_f = jnp.pad(win_t[F_OFF:F_OFF + H], ((0, LANES - H), (0, 0)))

    u = _rmsnorm_fwd("mix_norm", h1, norm_mix_g, tm)
    proj, (wout_g,) = _mm("mix_proj", u, win_main, tm, MAIN // 9, dims=NT, carry=_gather_carry([sh["w_out"]]))
    wout = wout_g.reshape(D, D)
    proj_f = _mm("mix_proj_forget", u, win_f, tm, LANES, dims=NT)
    scale = HEAD_DIM ** -0.5
    fox_gains = jnp.concatenate([pair_gain(fox_q_norm_g, HP), pair_gain(fox_k_norm_g, HP)])
    qk_f = _headnorm_fwd_scaled("fox_qk_norm", proj, 0, 2 * HP, fox_gains, T, scale, HP)
    v_f = proj[:, 2 * Dh:3 * Dh].astype(BF16)
    c_t, sg_t = _forget_fwd("forget_gates", proj_f[:, :H].T, b_forget.reshape(H, 1))
    crow = c_t.reshape(H, nk, 1, tk)
    (o_fox, lse_fa, lse_fb), (wg2, wu2) = _fox_fwd("fox_attention", qk_f, v_f, crow, tq, tk,
                                                   carry=_gather_carry([sh["ffn2_w_gate"], sh["ffn2_w_up"]]))

    swa_q_gains = pair_gain(swa_q_norm_g, HP)
    swa_k_gains = pair_gain(swa_k_norm_g, KVB)
    q_s = _headnorm_fwd("swa_q_norm", proj, 3 * HP, HP, swa_q_gains, T, scale, rope=rope)
    k_d = _headnorm_fwd("swa_k_norm", proj, 4 * HP, KVB, swa_k_gains, T, 1.0, rope=rope, dup=True)
    v_s = proj[:, 4 * Dh + KVW:].astype(BF16).reshape(T, H // GQA_GROUP, 1, HEAD_DIM)
    v_d = jnp.broadcast_to(v_s, (T, H // GQA_GROUP, 2, HEAD_DIM)).reshape(T, 2 * KVW)
    sinks3 = swa_sinks.reshape(H, 1, 1)
    o_swa, lse_sa, lse_sb = _swa_fwd("swa_attention", q_s, k_d, v_d, sinks3)

    on = _outnorm_fwd("out_norm", o_fox, o_swa, out_norm_fox_g, out_norm_swa_g, tm)
    h2 = _mm("mix_out", on, wout, tm, min(512, D), resid=h1)

    n2 = _rmsnorm_fwd("ffn2_norm", h2, norm_ffn2_g, tm)
    (a2, b2, hm2), (wd2,) = _ffn_up("ffn2_up", n2, wg2, wu2, tm, carry=_gather_carry([sh["ffn2_w_down"]]))
    y = _ffn_down("ffn2_down", hm2, wd2, h2, tm, D)
    dy, dyh, sq = _loss_grad("loss_grad", y, target, min(256, T))
    loss_part = 0.5 * sq[0, 0] / D

    J, Fs, _ = wg2.shape
    aspec = pl.BlockSpec((None, tm, Fs), lambda i, j: (j, i, 0))
    wspec = pl.BlockSpec((None, Fs, D), lambda i, j: (j, 0, 0))
    got = {}
    local = {}

    def pair_sums(keys, grads, received):
        for nm, g, r in zip(keys, grads, received):
            local[nm] = (g, r)
        return [_pair_add("sum_" + nm, g, r, slots) for nm, g, r in zip(keys, grads, received)]

    def relay_sums(keys, sums, hop1):
        out = []
        for i, (nm, s) in enumerate(zip(keys, sums)):
            got[nm] = [hop1[2 * i]]
            out.append(_relay_add("relay_" + nm, s, hop1[2 * i + 1]))
        return out

    def arrived(keys, hop2):
        for nm, blk in zip(keys, hop2):
            got[nm].append(blk)

    dwd2 = _wgrad_down("ffn2_wgrad_down", hm2, dyh, min(1024, D))
    (da2, db2), (sib_d2,) = _ffn_bwd_mid("ffn2_bwd_mid", dyh, wd2, a2, b2, tm, carry=_sibling_carry([dwd2]))
    (sum_wd2,) = pair_sums(names[7:8], [dwd2], [sib_d2])
    (dwg2, dwu2), hop1 = _wgrad_up("ffn2_wgrad_up", n2, da2, db2, min(1024, D), carry=_to_partner_carry([sum_wd2]))
    (t_wd2,) = relay_sums(names[7:8], [sum_wd2], hop1)
    dn2, (via_wd2, *sib2) = _reduce_mm("ffn2_bwd_in", [(da2, aspec, wg2, wspec), (db2, aspec, wu2, wspec)], [], NN, T, D, tm, J,
                                       carry=_join(_to_other_carry([t_wd2]), _sibling_carry([dwg2, dwu2])))
    arrived(names[7:8], [via_wd2])
    dh2, dg_ffn2, dh2b = _rmsnorm_bwd("ffn2_norm_bwd", dn2, h2, norm_ffn2_g, dy, min(256, T), 1.0)
    sum_wg2, sum_wu2 = pair_sums(names[5:7], [dwg2, dwu2], sib2)

    dwout = _wgrad_2d("mix_out_wgrad", on, dh2b, min(512, D), min(1024, D))
    dwout_g = dwout.reshape(N_DEV, D // N_DEV, D)
    do_fox, dg_of = _outnorm_bwd("out_norm_bwd_fox", dh2b, wout, 0, o_fox, out_norm_fox_g, tm)
    do_swa, dg_os = _outnorm_bwd("out_norm_bwd_swa", dh2b, wout, 1, o_swa, out_norm_swa_g, tm)

    (dq_f, dk_f, dv_f, dc_a, dc_b, dr_a, dr_b), (*hop1, sib_wout) = _fox_bwd(
        "fox_attention_bwd", qk_f, v_f, o_fox, do_fox, crow, lse_fa, lse_fb, tq, tk,
        carry=_join(_to_partner_carry([sum_wg2, sum_wu2]), _sibling_carry([dwout_g])))
    t_wg2, t_wu2 = relay_sums(names[5:7], [sum_wg2, sum_wu2], hop1)
    (sum_wout,) = pair_sums(names[4:5], [dwout_g], [sib_wout])
    dqf_raw, dg_fq = _headnorm_bwd("fox_q_norm_bwd", dq_f, proj, 0, HP, fox_gains[:HP], HP, T, 1.0)
    dkf_raw, dg_fk = _headnorm_bwd("fox_k_norm_bwd", dk_f, proj, HP, HP, fox_gains[HP:], HP, T, 1.0)
    dct = jnp.stack([dc_a.reshape(HP, T), dc_b.reshape(HP, T)], axis=1).reshape(H, T)
    drt = jnp.stack([dr_a.reshape(HP, T), dr_b.reshape(HP, T)], axis=1).reshape(H, T)
    dz_t, db_f = _forget_bwd("forget_gates_bwd", dct, drt, sg_t)

    (dq_s, dk_p, dv_p, dsink_a, dsink_b), hop2 = _swa_bwd(
        "swa_attention_bwd", q_s, k_d, v_d, sinks3, o_swa, do_swa, lse_sa, lse_sb, carry=_to_other_carry([t_wg2, t_wu2]))
    arrived(names[5:7], hop2)
    dqs_raw, dg_sq = _headnorm_bwd("swa_q_norm_bwd", dq_s, proj, 3 * HP, HP, swa_q_gains, HP, T, 1.0, rope=rope)
    dks_raw, dg_sk = _headnorm_bwd("swa_k_norm_bwd", dk_p, proj, 4 * HP, KVB, swa_k_gains, KVB, T, 1.0, rope=rope, fold=True)
    dvs_raw, _ = _headnorm_bwd("swa_v_fold", dv_p, None, 0, KVB, None, KVB, T, 1.0, fold=True, norm=False)

    dproj = jnp.concatenate([dqf_raw, dkf_raw, dv_f.astype(BF16), dqs_raw, dks_raw, dvs_raw], axis=1)
    dproj_f = jnp.pad(dz_t.T, ((0, 0), (0, LANES - H))).astype(BF16)
    dwin_main, hop1 = _wgrad_2d("mix_proj_wgrad", dproj, u, MAIN // 9, min(1024, D), carry=_to_partner_carry([sum_wout]))
    (t_wout,) = relay_sums(names[4:5], [sum_wout], hop1)
    dwin_f = _wgrad_2d("mix_proj_forget_wgrad", dproj_f, u, LANES, min(1024, D))
    dwin_t = jnp.concatenate([dwin_main[:F_OFF], dwin_f[:H], dwin_main[F_OFF:]], axis=0)
    dwin_g = dwin_t.reshape(N_DEV, n_in, D)
    tkb = MAIN // 9
    du, (via_wout, sib_win) = _reduce_mm(
        "mix_bwd_in",
        [(dproj, pl.BlockSpec((tm, tkb), lambda i, r: (i, r)), win_main, pl.BlockSpec((tkb, D), lambda i, r: (r, 0)))],
        [(dproj_f, pl.BlockSpec((tm, LANES), lambda i, r: (i, 0)), win_f, pl.BlockSpec((LANES, D), lambda i, r: (0, 0)))],
        NN, T, D, tm, 9, carry=_join(_to_other_carry([t_wout]), _sibling_carry([dwin_g])))
    arrived(names[4:5], [via_wout])
    dh1, dg_mix, dh1h = _rmsnorm_bwd("mix_norm_bwd", du, h1, norm_mix_g, dh2, min(256, T), 0.5)
    (sum_win,) = pair_sums(names[3:4], [dwin_g], [sib_win])

    dwd1, hop1 = _wgrad_down("ffn1_wgrad_down", hm1, dh1h, min(1024, D), carry=_to_partner_carry([sum_win]))
    (t_win,) = relay_sums(names[3:4], [sum_win], hop1)
    (da1, db1), (via_win, sib_d) = _ffn_bwd_mid("ffn1_bwd_mid", dh1h, wd1, a1, b1, tm,
                                                carry=_join(_to_other_carry([t_win]), _sibling_carry([dwd1])))
    arrived(names[3:4], [via_win])
    (sum_wd1,) = pair_sums(names[2:3], [dwd1], [sib_d])
    dwg1, hop1 = _wgrad_down("ffn1_wgrad_gate", da1, n1, min(1024, D), carry=_to_partner_carry([sum_wd1]))
    (t_wd1,) = relay_sums(names[2:3], [sum_wd1], hop1)
    dwu1, (via_wd1, sib_g) = _wgrad_down("ffn1_wgrad_up", db1, n1, min(1024, D),
                                         carry=_join(_to_other_carry([t_wd1]), _sibling_carry([dwg1])))
    arrived(names[2:3], [via_wd1])
    (sum_wg1,) = pair_sums(names[0:1], [dwg1], [sib_g])
    dn1_gate, (*hop1, sib_u) = _reduce_mm(
        "ffn1_bwd_in_gate", [(da1, aspec, wg1, wspec)], [], NN, T, D, tm, J,
        carry=_join(_to_partner_carry([sum_wg1]), _sibling_carry([dwu1])))
    (t_wg1,) = relay_sums(names[0:1], [sum_wg1], hop1)
    (sum_wu1,) = pair_sums(names[1:2], [dwu1], [sib_u])
    dn1, (via_wg1, *hop1) = _reduce_mm(
        "ffn1_bwd_in_up", [(db1, aspec, wu1, wspec)], [], NN, T, D, tm, J, init=dn1_gate,
        carry=_join(_to_other_carry([t_wg1]), _to_partner_carry([sum_wu1])))
    arrived(names[0:1], [via_wg1])
    (t_wu1,) = relay_sums(names[1:2], [sum_wu1], hop1)
    arrived(names[1:2], _run_carry("grads_exchange", _to_other_carry([t_wu1])))
    dx, dg_ffn1 = _rmsnorm_bwd("ffn1_norm_bwd", dn1, xs, norm_ffn1_g, dh1, min(256, T), None)

    big_out = [_adam_shard("adam_" + nm, local[nm][0], local[nm][1], got[nm], w, m, v, own)
               for nm, w, m, v in zip(names, big_w, big_m, big_v)]

    dsinks = jnp.stack([dsink_a.reshape(HP), dsink_b.reshape(HP)], axis=1).reshape(H)
    small_g = [dg_ffn1, dg_mix, dg_ffn2, dg_of, dg_os, db_f, dg_fq[0, 0, :HEAD_DIM], dg_fk[0, 0, :HEAD_DIM],
               dg_sq[0, 0, :HEAD_DIM], dg_sk[0, 0, :HEAD_DIM], dsinks]
    small_w = [norm_ffn1_g, norm_mix_g, norm_ffn2_g, out_norm_fox_g, out_norm_swa_g, b_forget, fox_q_norm_g, fox_k_norm_g,
               swa_q_norm_g, swa_k_norm_g, swa_sinks]
    small_m = [m_norm_ffn1_g, m_norm_mix_g, m_norm_ffn2_g, m_out_norm_fox_g, m_out_norm_swa_g, m_b_forget, m_fox_q_norm_g,
               m_fox_k_norm_g, m_swa_q_norm_g, m_swa_k_norm_g, m_swa_sinks]
    small_v = [v_norm_ffn1_g, v_norm_mix_g, v_norm_ffn2_g, v_out_norm_fox_g, v_out_norm_swa_g, v_b_forget, v_fox_q_norm_g,
               v_fox_k_norm_g, v_swa_q_norm_g, v_swa_k_norm_g, v_swa_sinks]
    gathered = _gather_small(_pack_small(small_g, D, loss_part))
    small_out = _adam_small("adam_small", gathered, _pack_small(small_w, D), _pack_small(small_m, D), _pack_small(small_v, D))
    loss = small_out[0][5, 0]
    small_out = [_unpack_small(p, D, H) for p in small_out]

    order = ["norm_ffn1_g", "ffn1_w_gate", "ffn1_w_up", "ffn1_w_down", "norm_mix_g", "w_in", "b_forget", "fox_q_norm_g", "fox_k_norm_g",
             "swa_q_norm_g", "swa_k_norm_g", "swa_sinks", "out_norm_fox_g", "out_norm_swa_g", "w_out", "norm_ffn2_g",
             "ffn2_w_gate", "ffn2_w_up", "ffn2_w_down"]
    small_names = ["norm_ffn1_g", "norm_mix_g", "norm_ffn2_g", "out_norm_fox_g", "out_norm_swa_g", "b_forget", "fox_q_norm_g",
                   "fox_k_norm_g", "swa_q_norm_g", "swa_k_norm_g", "swa_sinks"]
    result = [loss, dx[None]]
    for kind in range(4):
        for nm in order:
            if nm in names:
                leaf = big_out[names.index(nm)][kind]
                result.append((tr(leaf) if nm in transposed else leaf)[None])
            else:
                result.append(small_out[kind][small_names.index(nm)])
    return tuple(result)


def _headnorm_fwd_scaled(name, proj, col_off, ncb, gains, tm, scale, n_scaled):
    T = proj.shape[0]

    def body(x_ref, g_ref, o_ref):
        xv = x_ref[...]
        lo = _lane_lo(xv.shape)
        y = xv * _head_rstd(xv, lo) * g_ref[...]
        y = y * jnp.where(pl.program_id(0) < n_scaled, scale, 1.0)
        o_ref[...] = y.astype(BF16)

    return pl.pallas_call(
        body, out_shape=jax.ShapeDtypeStruct((T, ncb * LANES), BF16), grid=(ncb, T // tm),
        in_specs=[pl.BlockSpec((tm, LANES), lambda c, i: (i, col_off + c)), pl.BlockSpec((None, 1, LANES), lambda c, i: (c, 0, 0))],
        out_specs=pl.BlockSpec((tm, LANES), lambda c, i: (i, c)), name=name, compiler_params=_params(2))(proj, gains)
```

```python
import functools

import jax
import jax.numpy as jnp
from jax import lax
from jax.experimental import pallas as pl
from jax.experimental.pallas import tpu as pltpu

F32 = jnp.float32
BF16 = jnp.bfloat16

HEAD_DIM = 64
LANES = 128
WINDOW = 128
GQA_GROUP = 4
EPS = 1e-6
ROPE_THETA = 10000.0
ADAM_LR = 0.001
ADAM_B1 = 0.9
ADAM_B2 = 0.999
ADAM_EPS = 1e-08
ADAM_WD = 0.01
ADAM_STEP = 10
N_DEV = 8
NEG = -1e30
VMEM_LIMIT_V7X = 48 * 1024 * 1024
ROW_TILE_CAP = 512
MESH = pl.DeviceIdType.MESH

NN = (((1,), (0,)), ((), ()))
NT = (((1,), (1,)), ((), ()))
TN = (((0,), (0,)), ((), ()))


def _dot(a, b, dims):
    return lax.dot_general(a, b, dims, preferred_element_type=F32)


def _params(n_axes):
    return pltpu.CompilerParams(dimension_semantics=("arbitrary",) * n_axes, vmem_limit_bytes=VMEM_LIMIT_V7X)


def _row_tile(rows, cap=ROW_TILE_CAP):
    best = None
    for t in range(16, min(rows, cap) + 1, 16):
        if rows % t == 0:
            best = t
    return best or rows


def _lane_lo(shape):
    return lax.broadcasted_iota(jnp.int32, shape, len(shape) - 1) < HEAD_DIM


def _keep(sel, x):
    return jnp.where(sel, x.astype(F32), 0.0).astype(BF16)


_HBM = pl.BlockSpec(memory_space=pltpu.HBM)


class _Carry:
    def __init__(self, inputs, out_shapes, scratch, start, finish, middle=None):
        self.inputs, self.out_shapes, self.scratch = list(inputs), list(out_shapes), list(scratch)
        self.start, self.finish, self.middle = start, finish, middle or (lambda ins, outs, scr: None)


def _join(*carries):
    def hook(which):
        def run(ins, outs, scr):
            i = o = s = 0
            for c in carries:
                ni, no, ns = len(c.inputs), len(c.out_shapes), len(c.scratch)
                getattr(c, which)(ins[i:i + ni], outs[o:o + no], scr[s:s + ns])
                i, o, s = i + ni, o + no, s + ns
        return run

    return _Carry([a for c in carries for a in c.inputs], [a for c in carries for a in c.out_shapes],
                  [a for c in carries for a in c.scratch], hook("start"), hook("finish"), hook("middle"))


def _call(body, *, name, grid, in_specs, out_specs, out_shape, args, scratch_shapes=(), carry=None):
    params = _params(len(grid))
    if carry is None:
        return pl.pallas_call(body, out_shape=out_shape, grid=grid, in_specs=list(in_specs), out_specs=out_specs,
                              scratch_shapes=list(scratch_shapes), name=name, compiler_params=params)(*args)
    single = not isinstance(out_shape, (tuple, list))
    shapes = (out_shape,) if single else tuple(out_shape)
    specs = (out_specs,) if single else tuple(out_specs)
    n_in, n_out, n_scr = len(args), len(shapes), len(scratch_shapes)
    c_in, c_out = len(carry.inputs), len(carry.out_shapes)

    def wrapped(*refs):
        ins, c_ins = refs[:n_in], refs[n_in:n_in + c_in]
        o0 = n_in + c_in
        outs, c_outs = refs[o0:o0 + n_out], refs[o0 + n_out:o0 + n_out + c_out]
        s0 = o0 + n_out + c_out
        scr, c_scr = refs[s0:s0 + n_scr], refs[s0 + n_scr:]
        step, total = pl.program_id(0), grid[0]
        for ax in range(1, len(grid)):
            step, total = step * grid[ax] + pl.program_id(ax), total * grid[ax]

        @pl.when(step == 0)
        def _():
            carry.start(c_ins, c_outs, c_scr)

        @pl.when(step == total // 2)
        def _():
            carry.middle(c_ins, c_outs, c_scr)

        body(*ins, *outs, *scr)

        @pl.when(step == total - 1)
        def _():
            carry.finish(c_ins, c_outs, c_scr)

    res = pl.pallas_call(
        wrapped, out_shape=shapes + tuple(carry.out_shapes), grid=grid, in_specs=list(in_specs) + [_HBM] * c_in,
        out_specs=specs + (_HBM,) * c_out, scratch_shapes=list(scratch_shapes) + carry.scratch, name=name,
        compiler_params=params)(*args, *carry.inputs)
    main = res[:n_out]
    return (main[0] if single else tuple(main)), tuple(res[n_out:])


def _rms_bwd(dn, x, g):
    r = lax.rsqrt(jnp.mean(x * x, axis=-1, keepdims=True) + EPS)
    xh = x * r
    dxh = dn * g
    dx = r * (dxh - xh * jnp.mean(dxh * xh, axis=-1, keepdims=True))
    return dx, jnp.sum(dn * xh, axis=0, keepdims=True)


def _rmsnorm_fwd(name, x, g, tm, carry=None):
    T, D = x.shape

    def body(x_ref, g_ref, o_ref):
        xf = x_ref[...]
        r = lax.rsqrt(jnp.mean(xf * xf, axis=-1, keepdims=True) + EPS)
        o_ref[...] = (xf * r * g_ref[...]).astype(BF16)

    return _call(
        body, name=name, grid=(T // tm,), out_shape=jax.ShapeDtypeStruct((T, D), BF16),
        in_specs=[pl.BlockSpec((tm, D), lambda i: (i, 0)), pl.BlockSpec((1, D), lambda i: (0, 0))],
        out_specs=pl.BlockSpec((tm, D), lambda i: (i, 0)), args=[x, g], carry=carry)


def _outnorm_fwd(name, o_fox, o_swa, g_fox, g_swa, tm):
    T, Dh = o_fox.shape

    def body(a_ref, b_ref, ga_ref, gb_ref, o_ref):
        for ref, g_ref, lo in ((a_ref, ga_ref, 0), (b_ref, gb_ref, Dh)):
            xf = ref[...]
            r = lax.rsqrt(jnp.mean(xf * xf, axis=-1, keepdims=True) + EPS)
            o_ref[:, lo:lo + Dh] = (xf * r * g_ref[...]).astype(BF16)

    row = pl.BlockSpec((tm, Dh), lambda i: (i, 0))
    gain = pl.BlockSpec((1, Dh), lambda i: (0, 0))
    return pl.pallas_call(
        body, out_shape=jax.ShapeDtypeStruct((T, 2 * Dh), BF16), grid=(T // tm,),
        in_specs=[row, row, gain, gain], out_specs=pl.BlockSpec((tm, 2 * Dh), lambda i: (i, 0)),
        name=name, compiler_params=_params(1))(o_fox, o_swa, g_fox, g_swa)


def _outnorm_bwd(name, dhb, wout, half, o, g, tm):
    T, D = dhb.shape
    Dh = o.shape[1]

    def body(a_ref, w_ref, o_ref, g_ref, do_ref, dg_ref):
        don = _dot(a_ref[...], w_ref[...], NT)
        dx, dg = _rms_bwd(don, o_ref[...], g_ref[...])
        do_ref[...] = dx.astype(BF16)

        @pl.when(pl.program_id(0) == 0)
        def _():
            dg_ref[...] = dg

        @pl.when(pl.program_id(0) > 0)
        def _():
            dg_ref[...] += dg

    return pl.pallas_call(
        body, out_shape=(jax.ShapeDtypeStruct((T, Dh), BF16), jax.ShapeDtypeStruct((1, Dh), F32)), grid=(T // tm,),
        in_specs=[pl.BlockSpec((tm, D), lambda i: (i, 0)), pl.BlockSpec((Dh, D), lambda i: (half, 0)),
                  pl.BlockSpec((tm, Dh), lambda i: (i, 0)), pl.BlockSpec((1, Dh), lambda i: (0, 0))],
        out_specs=(pl.BlockSpec((tm, Dh), lambda i: (i, 0)), pl.BlockSpec((1, Dh), lambda i: (0, 0))),
        name=name, compiler_params=_params(1))(dhb, wout, o, g)


def _mm(name, a, b, tm, tn, dims=NN, resid=None, carry=None):
    M, K = a.shape
    transposed = dims == NT
    N = b.shape[0] if transposed else b.shape[1]

    def body(*refs):
        if resid is None:
            a_ref, b_ref, o_ref = refs
            o_ref[...] = _dot(a_ref[...], b_ref[...], dims)
        else:
            a_ref, b_ref, r_ref, o_ref = refs
            o_ref[...] = r_ref[...] + _dot(a_ref[...], b_ref[...], dims)

    ospec = pl.BlockSpec((tm, tn), lambda n, i: (i, n))
    bspec = pl.BlockSpec((tn, K), lambda n, i: (n, 0)) if transposed else pl.BlockSpec((K, tn), lambda n, i: (0, n))
    in_specs = [pl.BlockSpec((tm, K), lambda n, i: (i, 0)), bspec]
    args = [a, b]
    if resid is not None:
        in_specs.append(ospec)
        args.append(resid)
    return _call(body, name=name, grid=(N // tn, M // tm), in_specs=in_specs, out_specs=ospec,
                 out_shape=jax.ShapeDtypeStruct((M, N), F32), args=args, carry=carry)


def _wgrad_2d(name, a, b, tmm, tn, carry=None):
    T, M = a.shape
    N = b.shape[1]

    def body(a_ref, b_ref, o_ref):
        o_ref[...] = _dot(a_ref[...], b_ref[...], TN).astype(BF16)

    return _call(
        body, name=name, grid=(M // tmm, N // tn), out_shape=jax.ShapeDtypeStruct((M, N), BF16),
        in_specs=[pl.BlockSpec((T, tmm), lambda m, n: (0, m)), pl.BlockSpec((T, tn), lambda m, n: (0, n))],
        out_specs=pl.BlockSpec((tmm, tn), lambda m, n: (m, n)), args=[a, b], carry=carry)


def _wgrad_down(name, hm, df, tn, carry=None):
    J, T, Fs = hm.shape
    D = df.shape[1]

    def body(a_ref, b_ref, o_ref):
        o_ref[...] = _dot(a_ref[...], b_ref[...], TN).astype(BF16)

    return _call(
        body, name=name, grid=(J, D // tn), out_shape=jax.ShapeDtypeStruct((J, Fs, D), BF16),
        in_specs=[pl.BlockSpec((None, T, Fs), lambda j, n: (j, 0, 0)), pl.BlockSpec((T, tn), lambda j, n: (0, n))],
        out_specs=pl.BlockSpec((None, Fs, tn), lambda j, n: (j, 0, n)), args=[hm, df], carry=carry)


def _wgrad_up(name, n, da, db, tn, carry=None):
    T, D = n.shape
    J, _, Fs = da.shape

    def body(n_ref, da_ref, db_ref, og_ref, ou_ref):
        nv = n_ref[...]
        og_ref[...] = _dot(da_ref[...], nv, TN).astype(BF16)
        ou_ref[...] = _dot(db_ref[...], nv, TN).astype(BF16)

    act = pl.BlockSpec((None, T, Fs), lambda j, m: (j, 0, 0))
    out = pl.BlockSpec((None, Fs, tn), lambda j, m: (j, 0, m))
    shape = jax.ShapeDtypeStruct((J, Fs, D), BF16)
    return _call(
        body, name=name, grid=(J, D // tn), out_shape=(shape, shape),
        in_specs=[pl.BlockSpec((T, tn), lambda j, m: (0, m)), act, act], out_specs=(out, out),
        args=[n, da, db], carry=carry)


def _reduce_mm(name, pairs, once, dims, T, D, tm, steps, init=None, carry=None):
    n_pairs = len(pairs)
    n_once = len(once)
    n_mm = 2 * (n_pairs + n_once)

    def body(*refs):
        pr = refs[:2 * n_pairs]
        on = refs[2 * n_pairs:n_mm]
        o_ref, acc = refs[-2:]
        r = pl.program_id(1)

        @pl.when(r == 0)
        def _():
            acc[...] = jnp.zeros(acc.shape, F32) if init is None else refs[n_mm][...]

        for p in range(n_pairs):
            acc[...] += _dot(pr[2 * p][...], pr[2 * p + 1][...], dims)

        @pl.when(r == steps - 1)
        def _():
            dn = acc[...]
            for p in range(n_once):
                dn = dn + _dot(on[2 * p][...], on[2 * p + 1][...], dims)
            o_ref[...] = dn

    in_specs, args = [], []
    for a, a_spec, w, w_spec in list(pairs) + list(once):
        in_specs += [a_spec, w_spec]
        args += [a, w]
    row = pl.BlockSpec((tm, D), lambda i, r: (i, 0))
    if init is not None:
        in_specs.append(row)
        args.append(init)
    return _call(body, name=name, grid=(T // tm, steps), in_specs=in_specs, out_specs=row, out_shape=jax.ShapeDtypeStruct((T, D), F32),
                 args=args, scratch_shapes=[pltpu.VMEM((tm, D), F32)], carry=carry)


def _rmsnorm_bwd(name, dn, x, g, dh, tm, bf16_scale, carry=None):
    T, D = x.shape
    emit_bf16 = bf16_scale is not None

    def body(dn_ref, x_ref, g_ref, dh_ref, *outs):
        dxn, dg = _rms_bwd(dn_ref[...], x_ref[...], g_ref[...])
        dx = dh_ref[...] + dxn
        outs[0][...] = dx
        if emit_bf16:
            outs[2][...] = (bf16_scale * dx).astype(BF16)

        @pl.when(pl.program_id(0) == 0)
        def _():
            outs[1][...] = dg

        @pl.when(pl.program_id(0) > 0)
        def _():
            outs[1][...] += dg

    row = pl.BlockSpec((tm, D), lambda i: (i, 0))
    gain = pl.BlockSpec((1, D), lambda i: (0, 0))
    out_shape = [jax.ShapeDtypeStruct((T, D), F32), jax.ShapeDtypeStruct((1, D), F32)]
    out_specs = [row, gain]
    if emit_bf16:
        out_shape.append(jax.ShapeDtypeStruct((T, D), BF16))
        out_specs.append(row)
    return _call(body, name=name, grid=(T // tm,), in_specs=[row, row, gain, row], out_specs=tuple(out_specs),
                 out_shape=tuple(out_shape), args=[dn, x, g, dh], carry=carry)


def _loss_grad(name, y, target, tm):
    T, D = y.shape

    def body(y_ref, t_ref, dy_ref, dyh_ref, sq_ref):
        diff = y_ref[...] - t_ref[...]
        sq = jnp.sum(jnp.sum(diff * diff, axis=1, keepdims=True), axis=0, keepdims=True)
        dy = diff * (1.0 / D)
        dy_ref[...] = dy
        dyh_ref[...] = (0.5 * dy).astype(BF16)

        @pl.when(pl.program_id(0) == 0)
        def _():
            sq_ref[...] = sq

        @pl.when(pl.program_id(0) > 0)
        def _():
            sq_ref[...] += sq

    row = pl.BlockSpec((tm, D), lambda i: (i, 0))
    return pl.pallas_call(
        body, out_shape=(jax.ShapeDtypeStruct((T, D), F32), jax.ShapeDtypeStruct((T, D), BF16), jax.ShapeDtypeStruct((1, 1), F32)),
        grid=(T // tm,), in_specs=[row, row], out_specs=(row, row, pl.BlockSpec((1, 1), lambda i: (0, 0))),
        name=name, compiler_params=_params(1))(y, target)


def _ffn_up(name, n, wg, wu, tm, carry=None):
    T, D = n.shape
    J, Fs, _ = wg.shape

    def body(n_ref, wg_ref, wu_ref, a_ref, b_ref, h_ref):
        xv = n_ref[...]
        a = _dot(xv, wg_ref[...], NT)
        b = _dot(xv, wu_ref[...], NT)
        a_ref[...] = a.astype(BF16)
        b_ref[...] = b.astype(BF16)
        h_ref[...] = (a * jax.nn.sigmoid(a) * b).astype(BF16)

    act = jax.ShapeDtypeStruct((J, T, Fs), BF16)
    wspec = pl.BlockSpec((None, Fs, D), lambda j, i: (j, 0, 0))
    aspec = pl.BlockSpec((None, tm, Fs), lambda j, i: (j, i, 0))
    return _call(
        body, name=name, grid=(J, T // tm), out_shape=(act, act, act),
        in_specs=[pl.BlockSpec((tm, D), lambda j, i: (i, 0)), wspec, wspec], out_specs=(aspec, aspec, aspec),
        args=[n, wg, wu], carry=carry)


def _ffn_gate(name, n, wg, tm, carry=None):
    T, D = n.shape
    J, Fs, _ = wg.shape

    def body(n_ref, wg_ref, a_ref):
        a_ref[...] = _dot(n_ref[...], wg_ref[...], NT).astype(BF16)

    aspec = pl.BlockSpec((None, tm, Fs), lambda j, i: (j, i, 0))
    return _call(
        body, name=name, grid=(J, T // tm), out_shape=jax.ShapeDtypeStruct((J, T, Fs), BF16),
        in_specs=[pl.BlockSpec((tm, D), lambda j, i: (i, 0)), pl.BlockSpec((None, Fs, D), lambda j, i: (j, 0, 0))],
        out_specs=aspec, args=[n, wg], carry=carry)


def _ffn_up_only(name, n, wu, a, tm, carry=None):
    T, D = n.shape
    J, Fs, _ = wu.shape

    def body(n_ref, wu_ref, a_ref, b_ref, h_ref):
        b = _dot(n_ref[...], wu_ref[...], NT)
        a = a_ref[...].astype(F32)
        b_ref[...] = b.astype(BF16)
        h_ref[...] = (a * jax.nn.sigmoid(a) * b).astype(BF16)

    act = jax.ShapeDtypeStruct((J, T, Fs), BF16)
    aspec = pl.BlockSpec((None, tm, Fs), lambda j, i: (j, i, 0))
    return _call(
        body, name=name, grid=(J, T // tm), out_shape=(act, act),
        in_specs=[pl.BlockSpec((tm, D), lambda j, i: (i, 0)), pl.BlockSpec((None, Fs, D), lambda j, i: (j, 0, 0)), aspec],
        out_specs=(aspec, aspec), args=[n, wu, a], carry=carry)


def _ffn_down(name, hm, wd, resid, tm, tn, carry=None):
    J, T, Fs = hm.shape
    D = wd.shape[2]

    def body(h_ref, w_ref, r_ref, o_ref, acc):
        j = pl.program_id(2)

        @pl.when(j == 0)
        def _():
            acc[...] = jnp.zeros(acc.shape, F32)

        acc[...] += _dot(h_ref[...], w_ref[...], NN)

        @pl.when(j == J - 1)
        def _():
            o_ref[...] = r_ref[...] + 0.5 * acc[...]

    tile = pl.BlockSpec((tm, tn), lambda i, n, j: (i, n))
    return _call(
        body, name=name, grid=(T // tm, D // tn, J), out_shape=jax.ShapeDtypeStruct((T, D), F32),
        in_specs=[pl.BlockSpec((None, tm, Fs), lambda i, n, j: (j, i, 0)), pl.BlockSpec((None, Fs, tn), lambda i, n, j: (j, 0, n)), tile],
        out_specs=tile, scratch_shapes=[pltpu.VMEM((tm, tn), F32)], args=[hm, wd, resid], carry=carry)


def _ffn_bwd_mid(name, dfh, wd, a, b, tm, carry=None):
    T, D = dfh.shape
    J, Fs, _ = wd.shape

    def body(df_ref, w_ref, a_ref, b_ref, da_ref, db_ref):
        dhm = _dot(df_ref[...], w_ref[...], NT)
        av = a_ref[...].astype(F32)
        bv = b_ref[...].astype(F32)
        sg = jax.nn.sigmoid(av)
        da_ref[...] = (dhm * bv * (sg * (1.0 + av * (1.0 - sg)))).astype(BF16)
        db_ref[...] = (dhm * (av * sg)).astype(BF16)

    act = jax.ShapeDtypeStruct((J, T, Fs), BF16)
    aspec = pl.BlockSpec((None, tm, Fs), lambda j, i: (j, i, 0))
    return _call(
        body, name=name, grid=(J, T // tm), out_shape=(act, act),
        in_specs=[pl.BlockSpec((tm, D), lambda j, i: (i, 0)), pl.BlockSpec((None, Fs, D), lambda j, i: (j, 0, 0)), aspec, aspec],
        out_specs=(aspec, aspec), args=[dfh, wd, a, b], carry=carry)


def _rot_half(y, lane):
    first = (lane & (HEAD_DIM // 2)) == 0
    return jnp.where(first, pltpu.roll(y, LANES - HEAD_DIM // 2, 1), pltpu.roll(y, HEAD_DIM // 2, 1))


def _head_rstd(x, lo):
    sq = x * x
    ss_a = jnp.sum(jnp.where(lo, sq, 0.0), axis=-1, keepdims=True)
    ss_b = jnp.sum(jnp.where(lo, 0.0, sq), axis=-1, keepdims=True)
    return lax.rsqrt(jnp.where(lo, ss_a, ss_b) * (1.0 / HEAD_DIM) + EPS)


def _headnorm_fwd(name, proj, col_off, ncb, gains, tm, scale, rope=None, dup=False):
    T = proj.shape[0]
    with_rope = rope is not None
    width = 2 * LANES if dup else LANES

    def body(*refs):
        if with_rope:
            x_ref, g_ref, cos_ref, sin_ref, o_ref = refs
        else:
            x_ref, g_ref, o_ref = refs
        xv = x_ref[...]
        lane = lax.broadcasted_iota(jnp.int32, xv.shape, 1)
        lo = lane < HEAD_DIM
        y = xv * _head_rstd(xv, lo) * g_ref[...]
        if with_rope:
            y = y * cos_ref[...] + _rot_half(y, lane) * sin_ref[...]
        y = y * scale
        if dup:
            sw = pltpu.roll(y, HEAD_DIM, 1)
            o_ref[:, :LANES] = jnp.where(lo, y, sw).astype(BF16)
            o_ref[:, LANES:] = jnp.where(lo, sw, y).astype(BF16)
        else:
            o_ref[...] = y.astype(BF16)

    in_specs = [pl.BlockSpec((tm, LANES), lambda c, i: (i, col_off + c)), pl.BlockSpec((None, 1, LANES), lambda c, i: (c, 0, 0))]
    args = [proj, gains]
    if with_rope:
        tab = pl.BlockSpec((tm, LANES), lambda c, i: (i, 0))
        in_specs += [tab, tab]
        args += list(rope)
    return pl.pallas_call(
        body, out_shape=jax.ShapeDtypeStruct((T, ncb * width), BF16), grid=(ncb, T // tm),
        in_specs=in_specs, out_specs=pl.BlockSpec((tm, width), lambda c, i: (i, c)),
        name=name, compiler_params=_params(2))(*args)


def _headnorm_bwd(name, dy, proj, col_off, ncb, gains, group, tm, scale, rope=None, fold=False, norm=True):
    T = dy.shape[0]
    with_rope = rope is not None
    n_groups = ncb // group
    dy_width = 4 * LANES if fold else LANES

    def body(*refs):
        refs = list(refs)
        dy_ref = refs.pop(0)
        x_ref = refs.pop(0) if norm else None
        g_ref = refs.pop(0) if norm else None
        cos_ref = refs.pop(0) if with_rope else None
        sin_ref = refs.pop(0) if with_rope else None
        dx_ref = refs.pop(0)
        dg_ref = refs.pop(0) if norm else None
        c = pl.program_id(0)
        i = pl.program_id(1)
        d = dy_ref[...]
        lane = lax.broadcasted_iota(jnp.int32, (d.shape[0], LANES), 1)
        lo = lane < HEAD_DIM
        if fold:
            t0 = d[:, 0:LANES] + d[:, LANES:2 * LANES]
            t1 = d[:, 2 * LANES:3 * LANES] + d[:, 3 * LANES:4 * LANES]
            d = jnp.where(lo, t0 + pltpu.roll(t0, HEAD_DIM, 1), t1 + pltpu.roll(t1, HEAD_DIM, 1))
        d = d * scale
        if with_rope:
            d = d * cos_ref[...] + _rot_half(d * sin_ref[...], lane)
        if not norm:
            dx_ref[...] = d.astype(BF16)
            return
        xv = x_ref[...]
        gv = g_ref[...]
        r = _head_rstd(xv, lo)
        xh = xv * r
        dxh = d * gv
        pr = dxh * xh
        m_a = jnp.sum(jnp.where(lo, pr, 0.0), axis=-1, keepdims=True)
        m_b = jnp.sum(jnp.where(lo, 0.0, pr), axis=-1, keepdims=True)
        mean = jnp.where(lo, m_a, m_b) * (1.0 / HEAD_DIM)
        dx_ref[...] = (r * (dxh - xh * mean)).astype(BF16)
        dgp = jnp.sum(d * xh, axis=0, keepdims=True)
        dgp = dgp + pltpu.roll(dgp, HEAD_DIM, 1)
        first = jnp.logical_and(c % group == 0, i == 0)

        @pl.when(first)
        def _():
            dg_ref[...] = dgp

        @pl.when(jnp.logical_not(first))
        def _():
            dg_ref[...] += dgp

    in_specs = [pl.BlockSpec((tm, dy_width), lambda c, i: (i, c))]
    args = [dy]
    if norm:
        in_specs += [pl.BlockSpec((tm, LANES), lambda c, i: (i, col_off + c)), pl.BlockSpec((None, 1, LANES), lambda c, i: (c, 0, 0))]
        args += [proj, gains]
    if with_rope:
        tab = pl.BlockSpec((tm, LANES), lambda c, i: (i, 0))
        in_specs += [tab, tab]
        args += list(rope)
    out_shape = [jax.ShapeDtypeStruct((T, ncb * LANES), BF16)]
    out_specs = [pl.BlockSpec((tm, LANES), lambda c, i: (i, c))]
    if norm:
        out_shape.append(jax.ShapeDtypeStruct((n_groups, 1, LANES), F32))
        out_specs.append(pl.BlockSpec((None, 1, LANES), lambda c, i: (c // group, 0, 0)))
    res = pl.pallas_call(
        body, out_shape=tuple(out_shape), grid=(ncb, T // tm), in_specs=in_specs, out_specs=tuple(out_specs),
        name=name, compiler_params=_params(2))(*args)
    return res if norm else (res[0], None)


def _dot_exact(x, tri):
    hi = x.astype(BF16)
    r1 = x - hi.astype(F32)
    mid = r1.astype(BF16)
    lo = (r1 - mid.astype(F32)).astype(BF16)
    return _dot(hi, tri, NN) + _dot(mid, tri, NN) + _dot(lo, tri, NN)


def _forget_fwd(name, zt, bias):
    H, T = zt.shape
    blk = min(256, T)

    def body(z_ref, b_ref, c_ref, s_ref):
        z = z_ref[...] + b_ref[...]
        s_ref[...] = jax.nn.sigmoid(-z)
        lf = jnp.minimum(z, 0.0) - jnp.log(1.0 + jnp.exp(-jnp.abs(z)))
        tri = (lax.broadcasted_iota(jnp.int32, (blk, blk), 0) <= lax.broadcasted_iota(jnp.int32, (blk, blk), 1)).astype(BF16)
        carry = jnp.zeros((H, 1), F32)
        for bi in range(T // blk):
            xb = lf[:, bi * blk:(bi + 1) * blk]
            c_ref[:, bi * blk:(bi + 1) * blk] = _dot_exact(xb, tri) + carry
            carry = carry + jnp.sum(xb, axis=-1, keepdims=True)

    shape = jax.ShapeDtypeStruct((H, T), F32)
    full = pl.BlockSpec((H, T), lambda i: (0, 0))
    return pl.pallas_call(
        body, out_shape=(shape, shape), grid=(1,), in_specs=[full, pl.BlockSpec((H, 1), lambda i: (0, 0))],
        out_specs=(full, full), name=name, compiler_params=_params(1))(zt, bias)


def _forget_bwd(name, dct, drt, sgt):
    H, T = dct.shape
    blk = min(256, T)

    def body(dc_ref, dr_ref, s_ref, dz_ref, db_ref):
        dc = dc_ref[...] + dr_ref[...]
        tri = (lax.broadcasted_iota(jnp.int32, (blk, blk), 0) >= lax.broadcasted_iota(jnp.int32, (blk, blk), 1)).astype(BF16)
        carry = jnp.zeros((H, 1), F32)
        db = jnp.zeros((H, 1), F32)
        for bi in reversed(range(T // blk)):
            xb = dc[:, bi * blk:(bi + 1) * blk]
            dz = (_dot_exact(xb, tri) + carry) * s_ref[:, bi * blk:(bi + 1) * blk]
            dz_ref[:, bi * blk:(bi + 1) * blk] = dz
            db = db + jnp.sum(dz, axis=-1, keepdims=True)
            carry = carry + jnp.sum(xb, axis=-1, keepdims=True)
        db_ref[...] = db

    full = pl.BlockSpec((H, T), lambda i: (0, 0))
    return pl.pallas_call(
        body, out_shape=(jax.ShapeDtypeStruct((H, T), F32), jax.ShapeDtypeStruct((H, 1), F32)), grid=(1,),
        in_specs=[full, full, full], out_specs=(full, pl.BlockSpec((H, 1), lambda i: (0, 0))),
        name=name, compiler_params=_params(1))(dct, drt, sgt)


STRIP = 256


def _fox_fwd(name, qk, v, crow, tq, tk, carry=None):
    T, Dh = v.shape
    HP = Dh // LANES
    nk = T // tk
    assert tk % tq == 0 and tq % STRIP == 0
    n_strips = tq // STRIP

    def body(q_ref, k_ref, v_ref, ra_ref, rb_ref, o_ref, la_ref, lb_ref, s_ref, p_ref, m_ref, l_ref, acc_ref):
        i = pl.program_id(1)
        q2 = q_ref[...]
        lo = _lane_lo((tq, LANES))
        q_st = jnp.concatenate([_keep(lo, q2), _keep(jnp.logical_not(lo), q2)], axis=0)
        r_refs = (ra_ref, rb_ref)
        m_ref[...] = jnp.full(m_ref.shape, NEG, F32)
        l_ref[...] = jnp.zeros(l_ref.shape, F32)
        acc_ref[...] = jnp.zeros(acc_ref.shape, F32)
        rel = lax.broadcasted_iota(jnp.int32, (STRIP, tk), 0) - lax.broadcasted_iota(jnp.int32, (STRIP, tk), 1)

        def chunk(kc, masked):
            start = pl.multiple_of(kc * tk, tk)
            kb = k_ref[pl.ds(start, tk), :]
            vb = v_ref[pl.ds(start, tk), :]
            s_ref[...] = _dot(q_st, kb, NT)
            for h in range(2):
                cs = r_refs[h][kc]
                for st in range(n_strips):
                    rows = pl.ds(h * tq + st * STRIP, STRIP)
                    s = s_ref[rows, :] - cs
                    if masked:
                        s = jnp.where(rel >= start - (i * tq + st * STRIP), s, NEG)
                    m_old = m_ref[rows, :]
                    mn = jnp.maximum(m_old, jnp.max(s, axis=-1, keepdims=True))
                    p = jnp.exp(s - mn)
                    alpha = jnp.exp(m_old - mn)
                    l_ref[rows, :] = alpha * l_ref[rows, :] + jnp.sum(p, axis=-1, keepdims=True)
                    m_ref[rows, :] = mn
                    p_ref[rows, :] = p.astype(BF16)
                    acc_ref[rows, :] = acc_ref[rows, :] * alpha
            acc_ref[...] += _dot(p_ref[...], vb, NN)

        n_full = (i * tq) // tk

        def full_chunk(kc, _):
            chunk(kc, False)
            return 0

        lax.fori_loop(0, n_full, full_chunk, 0)
        chunk(n_full, True)
        top, bot = pl.ds(0, tq), pl.ds(tq, tq)
        o_ref[...] = jnp.where(lo, acc_ref[top, :] / l_ref[top, :], acc_ref[bot, :] / l_ref[bot, :])
        la_ref[...] = m_ref[top, :] + jnp.log(l_ref[top, :])
        lb_ref[...] = m_ref[bot, :] + jnp.log(l_ref[bot, :])

    row = lambda off: pl.BlockSpec((None, nk, 1, tk), lambda h, i: (2 * h + off, 0, 0, 0))
    lse = jax.ShapeDtypeStruct((HP, T, 1), F32)
    lspec = pl.BlockSpec((None, tq, 1), lambda h, i: (h, i, 0))
    scratch = [pltpu.VMEM((2 * tq, tk), F32), pltpu.VMEM((2 * tq, tk), BF16), pltpu.VMEM((2 * tq, 1), F32),
               pltpu.VMEM((2 * tq, 1), F32), pltpu.VMEM((2 * tq, LANES), F32)]
    return _call(
        body, name=name, grid=(HP, T // tq), out_shape=(jax.ShapeDtypeStruct((T, Dh), F32), lse, lse),
        in_specs=[pl.BlockSpec((tq, LANES), lambda h, i: (i, h)), pl.BlockSpec((T, LANES), lambda h, i: (0, HP + h)),
                  pl.BlockSpec((T, LANES), lambda h, i: (0, h)), row(0), row(1)],
        out_specs=(pl.BlockSpec((tq, LANES), lambda h, i: (i, h)), lspec, lspec),
        args=[qk, qk, v, crow, crow], scratch_shapes=scratch, carry=carry)


def _fox_bwd(name, qk, v, o, do, crow, lse_a, lse_b, tq, tk, carry=None):
    T, Dh = v.shape
    HP = Dh // LANES
    nk = T // tk
    scale = HEAD_DIM ** -0.5
    assert tk % tq == 0 and tq % STRIP == 0
    n_strips = tq // STRIP

    def body(q_ref, k_ref, v_ref, o_ref, do_ref, ra_ref, rb_ref, la_ref, lb_ref,
             dq_ref, dk_ref, dv_ref, dca_ref, dcb_ref, dra_ref, drb_ref, s_ref, dp_ref, p_ref, ds_ref, dq_acc, dsum_ref):
        i = pl.program_id(1)

        @pl.when(i == 0)
        def _():
            dk_ref[...] = jnp.zeros_like(dk_ref)
            dv_ref[...] = jnp.zeros_like(dv_ref)
            dca_ref[...] = jnp.zeros_like(dca_ref)
            dcb_ref[...] = jnp.zeros_like(dcb_ref)

        q2 = q_ref[...]
        do2 = do_ref[...]
        lo = _lane_lo((tq, LANES))
        hi = jnp.logical_not(lo)
        q_st = jnp.concatenate([_keep(lo, q2), _keep(hi, q2)], axis=0)
        do_st = jnp.concatenate([_keep(lo, do2), _keep(hi, do2)], axis=0)
        prod = do2.astype(F32) * o_ref[...]
        dsum_ref[pl.ds(0, tq), :] = jnp.sum(jnp.where(lo, prod, 0.0), axis=-1, keepdims=True)
        dsum_ref[pl.ds(tq, tq), :] = jnp.sum(jnp.where(lo, 0.0, prod), axis=-1, keepdims=True)
        r_refs, l_refs, dc_refs, dr_refs = (ra_ref, rb_ref), (la_ref, lb_ref), (dca_ref, dcb_ref), (dra_ref, drb_ref)
        dq_acc[...] = jnp.zeros(dq_acc.shape, F32)
        dra_ref[...] = jnp.zeros(dra_ref.shape, F32)
        drb_ref[...] = jnp.zeros(drb_ref.shape, F32)
        rel = lax.broadcasted_iota(jnp.int32, (STRIP, tk), 0) - lax.broadcasted_iota(jnp.int32, (STRIP, tk), 1)

        def chunk(kc, masked):
            start = pl.multiple_of(kc * tk, tk)
            kb = k_ref[pl.ds(start, tk), :]
            vb = v_ref[pl.ds(start, tk), :]
            s_ref[...] = _dot(q_st, kb, NT)
            dp_ref[...] = _dot(do_st, vb, NT)
            for h in range(2):
                cs = r_refs[h][kc]
                col_sum = jnp.zeros((1, tk), F32)
                for st in range(n_strips):
                    rows = pl.ds(st * STRIP, STRIP)
                    both = pl.ds(h * tq + st * STRIP, STRIP)
                    s = s_ref[both, :] - cs
                    if masked:
                        s = jnp.where(rel >= start - (i * tq + st * STRIP), s, NEG)
                    p = jnp.exp(s - l_refs[h][rows, :])
                    ds = p * (dp_ref[both, :] - dsum_ref[both, :])
                    p_ref[both, :] = p.astype(BF16)
                    ds_ref[both, :] = ds.astype(BF16)
                    col_sum = col_sum + jnp.sum(ds, axis=0, keepdims=True)
                    dr_refs[h][rows, :] += jnp.sum(ds, axis=-1, keepdims=True)
                dc_refs[h][kc] = dc_refs[h][kc] - col_sum
            dk_ref[pl.ds(start, tk), :] += _dot(ds_ref[...], q_st, TN)
            dv_ref[pl.ds(start, tk), :] += _dot(p_ref[...], do_st, TN)
            dq_acc[...] += _dot(ds_ref[...], kb, NN)

        n_full = (i * tq) // tk

        def full_chunk(kc, _):
            chunk(kc, False)
            return 0

        lax.fori_loop(0, n_full, full_chunk, 0)
        chunk(n_full, True)
        dq_ref[...] = jnp.where(lo, dq_acc[pl.ds(0, tq), :], dq_acc[pl.ds(tq, tq), :]) * scale

    row = lambda off: pl.BlockSpec((None, nk, 1, tk), lambda h, i: (2 * h + off, 0, 0, 0))
    lspec = pl.BlockSpec((None, tq, 1), lambda h, i: (h, i, 0))
    qspec = pl.BlockSpec((tq, LANES), lambda h, i: (i, h))
    full = pl.BlockSpec((T, LANES), lambda h, i: (0, h))
    dcspec = pl.BlockSpec((None, nk, 1, tk), lambda h, i: (h, 0, 0, 0))
    grad = jax.ShapeDtypeStruct((T, Dh), F32)
    dc = jax.ShapeDtypeStruct((HP, nk, 1, tk), F32)
    dr = jax.ShapeDtypeStruct((HP, T, 1), F32)
    scratch = [pltpu.VMEM((2 * tq, tk), F32), pltpu.VMEM((2 * tq, tk), F32), pltpu.VMEM((2 * tq, tk), BF16), pltpu.VMEM((2 * tq, tk), BF16),
               pltpu.VMEM((2 * tq, LANES), F32), pltpu.VMEM((2 * tq, 1), F32)]
    return _call(
        body, name=name, grid=(HP, T // tq), out_shape=(grad, grad, grad, dc, dc, dr, dr),
        in_specs=[qspec, pl.BlockSpec((T, LANES), lambda h, i: (0, HP + h)), full, qspec, qspec, row(0), row(1), lspec, lspec],
        out_specs=(qspec, full, full, dcspec, dcspec, lspec, lspec),
        args=[qk, qk, v, o, do, crow, crow, lse_a, lse_b], scratch_shapes=scratch, carry=carry)


SWA_GROUP = 2
SWA_GROUP_BWD = 4


def _swa_block(n, q_ref, k_ref):
    qs = pl.multiple_of(n * WINDOW, WINDOW)
    ks = pl.multiple_of(jnp.maximum(n - 1, 0) * WINDOW, WINDOW)
    rel = (qs + lax.broadcasted_iota(jnp.int32, (WINDOW, 2 * WINDOW), 0)) - (ks + lax.broadcasted_iota(jnp.int32, (WINDOW, 2 * WINDOW), 1))
    valid = jnp.logical_and(rel >= 0, rel < WINDOW)
    return qs, ks, valid


def _swa_fwd(name, q, kd, vd, sinks, carry=None):
    T, Dh = q.shape
    HP = Dh // LANES

    def body(q_ref, k_ref, v_ref, sa_ref, sb_ref, o_ref, la_ref, lb_ref):
        lo = _lane_lo((WINDOW, LANES))

        top = lax.broadcasted_iota(jnp.int32, (2 * WINDOW, 1), 0) < WINDOW
        sink = jnp.where(top, sa_ref[...], sb_ref[...])

        def block(n, _):
            qs, ks, valid = _swa_block(n, q_ref, k_ref)
            q2 = q_ref[pl.ds(qs, WINDOW), :]
            kb = k_ref[pl.ds(ks, 2 * WINDOW), :]
            vb = v_ref[pl.ds(ks, 2 * WINDOW), :]
            q_st = jnp.concatenate([_keep(lo, q2), _keep(jnp.logical_not(lo), q2)], axis=0)
            s = jnp.where(jnp.concatenate([valid, valid], axis=0), _dot(q_st, kb, NT), NEG)
            m = jnp.maximum(jnp.max(s, axis=-1, keepdims=True), sink)
            p = jnp.exp(s - m)
            l = jnp.sum(p, axis=-1, keepdims=True) + jnp.exp(sink - m)
            o2 = _dot(p.astype(BF16), vb, NN) / l
            lse = m + jnp.log(l)
            o_ref[pl.ds(qs, WINDOW), :] = jnp.where(lo, o2[:WINDOW], o2[WINDOW:])
            la_ref[pl.ds(qs, WINDOW), :] = lse[:WINDOW]
            lb_ref[pl.ds(qs, WINDOW), :] = lse[WINDOW:]
            return 0

        assert (T // WINDOW) % SWA_GROUP == 0

        def group(g, c):
            for b in range(SWA_GROUP):
                c = block(g * SWA_GROUP + b, c)
            return c

        lax.fori_loop(0, T // WINDOW // SWA_GROUP, group, 0)

    full = pl.BlockSpec((T, LANES), lambda h: (0, h))
    kv = pl.BlockSpec((T, LANES), lambda h: (0, h // 2))
    sink = lambda off: pl.BlockSpec((None, 1, 1), lambda h: (2 * h + off, 0, 0))
    lse = jax.ShapeDtypeStruct((HP, T, 1), F32)
    lspec = pl.BlockSpec((None, T, 1), lambda h: (h, 0, 0))
    return _call(
        body, name=name, grid=(HP,), out_shape=(jax.ShapeDtypeStruct((T, Dh), F32), lse, lse),
        in_specs=[full, kv, kv, sink(0), sink(1)], out_specs=(full, lspec, lspec),
        args=[q, kd, vd, sinks, sinks], carry=carry)


def _swa_bwd(name, q, kd, vd, sinks, o, do, lse_a, lse_b, carry=None):
    T, Dh = q.shape
    HP = Dh // LANES
    scale = HEAD_DIM ** -0.5

    def body(q_ref, k_ref, v_ref, sa_ref, sb_ref, o_ref, do_ref, la_ref, lb_ref, dq_ref, dk_ref, dv_ref, dsa_ref, dsb_ref):
        lo = _lane_lo((WINDOW, LANES))
        hi = jnp.logical_not(lo)
        dk_ref[...] = jnp.zeros_like(dk_ref)
        dv_ref[...] = jnp.zeros_like(dv_ref)

        top = lax.broadcasted_iota(jnp.int32, (2 * WINDOW, 1), 0) < WINDOW
        sink = jnp.where(top, sa_ref[...], sb_ref[...])

        def block(n, dsinks):
            qs, ks, valid = _swa_block(n, q_ref, k_ref)
            rows = pl.ds(qs, WINDOW)
            q2 = q_ref[rows, :]
            do2 = do_ref[rows, :]
            kb = k_ref[pl.ds(ks, 2 * WINDOW), :]
            vb = v_ref[pl.ds(ks, 2 * WINDOW), :]
            prod = do2.astype(F32) * o_ref[rows, :]
            q_st = jnp.concatenate([_keep(lo, q2), _keep(hi, q2)], axis=0)
            do_st = jnp.concatenate([_keep(lo, do2), _keep(hi, do2)], axis=0)
            dsum = jnp.concatenate([jnp.sum(jnp.where(lo, prod, 0.0), axis=-1, keepdims=True),
                                    jnp.sum(jnp.where(lo, 0.0, prod), axis=-1, keepdims=True)], axis=0)
            lse = jnp.concatenate([la_ref[rows, :], lb_ref[rows, :]], axis=0)
            s = jnp.where(jnp.concatenate([valid, valid], axis=0), _dot(q_st, kb, NT), NEG)
            p = jnp.exp(s - lse)
            ds = p * (_dot(do_st, vb, NT) - dsum)
            dsb = ds.astype(BF16)
            dq2 = _dot(dsb, kb, NN)
            dq_ref[rows, :] = jnp.where(lo, dq2[:WINDOW], dq2[WINDOW:]) * scale
            dk_ref[pl.ds(ks, 2 * WINDOW), :] += _dot(dsb, q_st, TN)
            dv_ref[pl.ds(ks, 2 * WINDOW), :] += _dot(p.astype(BF16), do_st, TN)
            gone = jnp.exp(sink - lse) * dsum
            return (dsinks[0] - jnp.sum(gone[:WINDOW], axis=0, keepdims=True),
                    dsinks[1] - jnp.sum(gone[WINDOW:], axis=0, keepdims=True))

        assert (T // WINDOW) % SWA_GROUP_BWD == 0

        def group(g, c):
            for b in range(SWA_GROUP_BWD):
                c = block(g * SWA_GROUP_BWD + b, c)
            return c

        dsa, dsb_ = lax.fori_loop(0, T // WINDOW // SWA_GROUP_BWD, group, (jnp.zeros((1, 1), F32), jnp.zeros((1, 1), F32)))
        dsa_ref[...] = dsa
        dsb_ref[...] = dsb_

    full = pl.BlockSpec((T, LANES), lambda h: (0, h))
    kv = pl.BlockSpec((T, LANES), lambda h: (0, h // 2))
    sink = lambda off: pl.BlockSpec((None, 1, 1), lambda h: (2 * h + off, 0, 0))
    lspec = pl.BlockSpec((None, T, 1), lambda h: (h, 0, 0))
    dsink = pl.BlockSpec((None, 1, 1), lambda h: (h, 0, 0))
    grad = jax.ShapeDtypeStruct((T, Dh), F32)
    ds_shape = jax.ShapeDtypeStruct((HP, 1, 1), F32)
    return _call(
        body, name=name, grid=(HP,), out_shape=(grad, grad, grad, ds_shape, ds_shape),
        in_specs=[full, kv, kv, sink(0), sink(1), full, full, lspec, lspec],
        out_specs=(full, full, full, dsink, dsink),
        args=[q, kd, vd, sinks, sinks, o, do, lse_a, lse_b], carry=carry)


def _place():
    return lax.axis_index("x"), lax.axis_index("y"), lax.axis_index("c")


def _run_carry(name, carry):
    c_in, c_out = len(carry.inputs), len(carry.out_shapes)

    def body(*refs):
        ins, outs, scr = refs[:c_in], refs[c_in:c_in + c_out], refs[c_in + c_out:]
        carry.start(ins, outs, scr)
        carry.middle(ins, outs, scr)
        carry.finish(ins, outs, scr)

    return pl.pallas_call(
        body, out_shape=tuple(carry.out_shapes), in_specs=[_HBM] * c_in, out_specs=tuple([_HBM] * c_out),
        scratch_shapes=carry.scratch, name=name)(*carry.inputs)


def _gather_carry(shards):
    n = len(shards)

    def plan(ins, outs, scr):
        send, recv, local = scr
        x, y, c = _place()
        me, sibling = (x, y, c), (x, y, 1 - c)
        partner, other, diag = (x ^ c, y ^ (1 - c)), (x ^ (1 - c), y ^ c), (1 - x, 1 - y)

        def copy(w, k, block, to, src=None):
            slot = 4 * block[0] + 2 * block[1] + block[2]
            return pltpu.make_async_remote_copy(
                src_ref=outs[w].at[slot] if src is None else src, dst_ref=outs[w].at[slot],
                send_sem=send.at[w, k], recv_sem=recv.at[w, k], device_id=to, device_id_type=MESH)

        def own():
            return [pltpu.make_async_copy(ins[w], outs[w].at[4 * x + 2 * y + c], local.at[w]) for w in range(n)]

        return copy, own, me, sibling, partner, other, diag, c

    def start(ins, outs, scr):
        copy, own, me, sibling, partner, other, _, c = plan(ins, outs, scr)
        for cp in own():
            cp.start()
        for w in range(n):
            copy(w, 1, me, (*partner, c), src=ins[w]).start()
            copy(w, 2, me, (*other, c), src=ins[w]).start()
            copy(w, 0, me, sibling, src=ins[w]).start()

    def middle(ins, outs, scr):
        copy, _, me, sibling, partner, other, _, c = plan(ins, outs, scr)
        for w in range(n):
            copy(w, 1, (*partner, c), me).wait_recv()
            copy(w, 3, (*partner, c), (*other, c)).start()
            copy(w, 4, (*partner, c), sibling).start()
        for w in range(n):
            copy(w, 2, (*other, c), me).wait_recv()
            copy(w, 5, (*other, c), sibling).start()

    def finish(ins, outs, scr):
        copy, own, me, sibling, partner, other, diag, c = plan(ins, outs, scr)
        for w in range(n):
            copy(w, 3, (*diag, c), me).wait_recv()
            copy(w, 6, (*diag, c), sibling).start()
        for w in range(n):
            copy(w, 0, sibling, me).wait_recv()
            copy(w, 4, (*other, 1 - c), me).wait_recv()
            copy(w, 5, (*partner, 1 - c), me).wait_recv()
            copy(w, 6, (*diag, 1 - c), me).wait_recv()
        for w in range(n):
            sent = [copy(w, 0, me, sibling, src=ins[w]), copy(w, 1, me, (*partner, c), src=ins[w]), copy(w, 2, me, (*other, c), src=ins[w]),
                    copy(w, 3, (*partner, c), (*other, c)), copy(w, 4, (*partner, c), sibling), copy(w, 5, (*other, c), sibling),
                    copy(w, 6, (*diag, c), sibling)]
            for cp in sent:
                cp.wait_send()
        for cp in own():
            cp.wait()

    return _Carry(shards, [jax.ShapeDtypeStruct((N_DEV,) + s.shape, s.dtype) for s in shards],
                  [pltpu.SemaphoreType.DMA((n, 7)), pltpu.SemaphoreType.DMA((n, 7)), pltpu.SemaphoreType.DMA((n,))], start, finish, middle)


def _sibling_carry(grads):
    n = len(grads)

    def copies(ins, outs, scr):
        send, recv = scr
        x, y, c = _place()
        return [pltpu.make_async_remote_copy(
            src_ref=ins[w].at[2 * q + (1 - c)], dst_ref=outs[w].at[q], send_sem=send.at[w, q], recv_sem=recv.at[w, q],
            device_id=(x, y, 1 - c), device_id_type=MESH) for w in range(n) for q in range(4)]

    def start(ins, outs, scr):
        for cp in copies(ins, outs, scr):
            cp.start()

    def finish(ins, outs, scr):
        for cp in copies(ins, outs, scr):
            cp.wait()

    return _Carry(grads, [jax.ShapeDtypeStruct((4,) + g.shape[1:], g.dtype) for g in grads],
                  [pltpu.SemaphoreType.DMA((n, 4)), pltpu.SemaphoreType.DMA((n, 4))], start, finish)


def _to_partner_carry(sums):
    n = len(sums)

    def copies(ins, outs, scr):
        send, recv = scr
        x, y, c = _place()
        return [pltpu.make_async_remote_copy(
            src_ref=ins[w].at[k], dst_ref=outs[2 * w + k], send_sem=send.at[w, k], recv_sem=recv.at[w, k],
            device_id=(x ^ c, y ^ (1 - c), c), device_id_type=MESH) for w in range(n) for k in range(2)]

    def start(ins, outs, scr):
        for cp in copies(ins, outs, scr):
            cp.start()

    def finish(ins, outs, scr):
        for cp in copies(ins, outs, scr):
            cp.wait()

    return _Carry(sums, [jax.ShapeDtypeStruct(s.shape[1:], s.dtype) for s in sums for _ in range(2)],
                  [pltpu.SemaphoreType.DMA((n, 2)), pltpu.SemaphoreType.DMA((n, 2))], start, finish)


def _to_other_carry(blocks):
    n = len(blocks)

    def copies(ins, outs, scr):
        send, recv = scr
        x, y, c = _place()
        return [pltpu.make_async_remote_copy(
            src_ref=ins[w], dst_ref=outs[w], send_sem=send.at[w], recv_sem=recv.at[w],
            device_id=(x ^ (1 - c), y ^ c, c), device_id_type=MESH) for w in range(n)]

    def start(ins, outs, scr):
        for cp in copies(ins, outs, scr):
            cp.start()

    def finish(ins, outs, scr):
        for cp in copies(ins, outs, scr):
            cp.wait()

    return _Carry(blocks, [jax.ShapeDtypeStruct(b.shape, b.dtype) for b in blocks],
                  [pltpu.SemaphoreType.DMA((n,)), pltpu.SemaphoreType.DMA((n,))], start, finish)


def _gather_small(packed):
    R, C = packed.shape

    def body(in_ref, out_ref, send, recv):
        x, y, c = _place()
        mine = 4 * x + 2 * y + c
        out_ref[mine] = in_ref[...]
        copies = []
        for k in range(1, N_DEV):
            peer = (x ^ (k >> 2), y ^ ((k >> 1) & 1), c ^ (k & 1))
            copies.append(pltpu.make_async_remote_copy(
                src_ref=in_ref, dst_ref=out_ref.at[mine], send_sem=send.at[k - 1], recv_sem=recv.at[k - 1],
                device_id=peer, device_id_type=MESH))
        for cp in copies:
            cp.start()
        for cp in copies:
            cp.wait()

    vmem = pl.BlockSpec(memory_space=pltpu.VMEM)
    return pl.pallas_call(
        body, out_shape=jax.ShapeDtypeStruct((N_DEV, R, C), F32), in_specs=[vmem], out_specs=vmem,
        scratch_shapes=[pltpu.SemaphoreType.DMA((N_DEV - 1,)), pltpu.SemaphoreType.DMA((N_DEV - 1,))],
        name="small_grads_all_gather")(packed)


def _adamw(w, g, m, v):
    m = ADAM_B1 * m + (1.0 - ADAM_B1) * g
    v = ADAM_B2 * v + (1.0 - ADAM_B2) * (g * g)
    m_hat = m / (1.0 - ADAM_B1 ** ADAM_STEP)
    v_hat = v / (1.0 - ADAM_B2 ** ADAM_STEP)
    delta = -ADAM_LR * (m_hat / (jnp.sqrt(v_hat) + ADAM_EPS) + ADAM_WD * w)
    return delta, m, v


def _pair_add(name, grads, received, slots):
    _, R, C = grads.shape
    tr = _row_tile(R)

    def body(s_ref, g_ref, r_ref, o_ref):
        o_ref[...] = (g_ref[...].astype(F32) + r_ref[...].astype(F32)).astype(BF16)

    return pl.pallas_call(
        body, out_shape=jax.ShapeDtypeStruct((3, R, C), BF16),
        grid_spec=pltpu.PrefetchScalarGridSpec(
            num_scalar_prefetch=1, grid=(3, R // tr),
            in_specs=[pl.BlockSpec((None, tr, C), lambda k, i, s: (s[k], i, 0)), pl.BlockSpec((None, tr, C), lambda k, i, s: (s[3 + k], i, 0))],
            out_specs=pl.BlockSpec((None, tr, C), lambda k, i, s: (k, i, 0))),
        name=name, compiler_params=_params(2))(slots, grads, received)


def _relay_add(name, sums, relayed):
    _, R, C = sums.shape
    tr = _row_tile(R)

    def body(s_ref, r_ref, o_ref):
        o_ref[...] = (s_ref[...].astype(F32) + r_ref[...].astype(F32)).astype(BF16)

    blk = pl.BlockSpec((tr, C), lambda i: (i, 0))
    return pl.pallas_call(
        body, out_shape=jax.ShapeDtypeStruct((R, C), BF16), grid=(R // tr,),
        in_specs=[pl.BlockSpec((None, tr, C), lambda i: (2, i, 0)), blk], out_specs=blk,
        name=name, compiler_params=_params(1))(sums, relayed)


def _adam_shard(name, grads, from_sibling, received, w, m, v, own):
    R, C = w.shape
    tr = _row_tile(R, 128)
    tc = C if tr < R or C % (2 * LANES) else 2 * LANES

    def body(o_ref, g_ref, s_ref, ra_ref, rb_ref, w_ref, m_ref, v_ref, g_out, d_out, m_out, v_out):
        g = (g_ref[...].astype(F32) + s_ref[...].astype(F32)) + ra_ref[...].astype(F32) + rb_ref[...].astype(F32)
        delta, mn, vn = _adamw(w_ref[...], g, m_ref[...], v_ref[...])
        g_out[...] = g
        d_out[...] = delta
        m_out[...] = mn
        v_out[...] = vn

    blk = pl.BlockSpec((tr, tc), lambda i, j, o: (i, j))
    shape = jax.ShapeDtypeStruct((R, C), F32)
    return pl.pallas_call(
        body, out_shape=(shape,) * 4,
        grid_spec=pltpu.PrefetchScalarGridSpec(
            num_scalar_prefetch=1, grid=(R // tr, C // tc),
            in_specs=[pl.BlockSpec((None, tr, tc), lambda i, j, o: (o[0], i, j)), pl.BlockSpec((None, tr, tc), lambda i, j, o: (o[1], i, j)),
                      blk, blk, blk, blk, blk],
            out_specs=(blk,) * 4),
        name=name, compiler_params=_params(2))(own, grads, from_sibling, received[0], received[1], w, m, v)


def _adam_small(name, gathered, w, m, v):
    R, C = w.shape

    def body(ga_ref, w_ref, m_ref, v_ref, g_out, d_out, m_out, v_out):
        g = ga_ref[0]
        for d in range(1, N_DEV):
            g = g + ga_ref[d]
        delta, mn, vn = _adamw(w_ref[...], g, m_ref[...], v_ref[...])
        g_out[...] = g
        d_out[...] = delta
        m_out[...] = mn
        v_out[...] = vn

    full = pl.BlockSpec((R, C), lambda i: (0, 0))
    shape = jax.ShapeDtypeStruct((R, C), F32)
    return pl.pallas_call(
        body, out_shape=(shape,) * 4, grid=(1,),
        in_specs=[pl.BlockSpec((N_DEV, R, C), lambda i: (0, 0, 0)), full, full, full], out_specs=(full,) * 4,
        name=name, compiler_params=_params(1))(gathered, w, m, v)


def _pack_small(parts, D, scalar=None):
    g1, gmix, g2, gof, gos, bf, gqf, gkf, gqs, gks, sinks = [p.reshape(-1).astype(F32) for p in parts]
    row3 = jnp.concatenate([gof, gos])
    row4 = jnp.zeros((D,), F32)
    for slot, vec in enumerate((bf, gqf, gkf, gqs, gks, sinks)):
        row4 = lax.dynamic_update_slice(row4, vec, (slot * LANES,))
    zero = jnp.zeros((D,), F32)
    row5 = zero if scalar is None else lax.dynamic_update_slice(zero, jnp.reshape(scalar, (1,)).astype(F32), (0,))
    return jnp.stack([g1, gmix, g2, row3, row4, row5, zero, zero])


def _unpack_small(packed, D, H):
    Dh = D // 2
    row4 = packed[4]
    short = [row4[s * LANES:s * LANES + n] for s, n in enumerate((H, HEAD_DIM, HEAD_DIM, HEAD_DIM, HEAD_DIM, H))]
    vecs = [packed[0], packed[1], packed[2], packed[3, :Dh], packed[3, Dh:]] + short
    return [v[None, :] for v in vecs]


def kernel(x, positions, norm_ffn1_g, ffn1_w_gate, ffn1_w_up, ffn1_w_down, norm_mix_g, w_in, b_forget, fox_q_norm_g, fox_k_norm_g, swa_q_norm_g, swa_k_norm_g, swa_sinks, out_norm_fox_g, out_norm_swa_g, w_out, norm_ffn2_g, ffn2_w_gate, ffn2_w_up, ffn2_w_down, loss_target, m_norm_ffn1_g, m_ffn1_w_gate, m_ffn1_w_up, m_ffn1_w_down, m_norm_mix_g, m_w_in, m_b_forget, m_fox_q_norm_g, m_fox_k_norm_g, m_swa_q_norm_g, m_swa_k_norm_g, m_swa_sinks, m_out_norm_fox_g, m_out_norm_swa_g, m_w_out, m_norm_ffn2_g, m_ffn2_w_gate, m_ffn2_w_up, m_ffn2_w_down, v_norm_ffn1_g, v_ffn1_w_gate, v_ffn1_w_up, v_ffn1_w_down, v_norm_mix_g, v_w_in, v_b_forget, v_fox_q_norm_g, v_fox_k_norm_g, v_swa_q_norm_g, v_swa_k_norm_g, v_swa_sinks, v_out_norm_fox_g, v_out_norm_swa_g, v_w_out, v_norm_ffn2_g, v_ffn2_w_gate, v_ffn2_w_up, v_ffn2_w_down):
    xs = x[0]
    target = loss_target[0]
    T, D = xs.shape
    Dh = D // 2
    H = Dh // HEAD_DIM
    HP = H // 2
    KVW = (H // GQA_GROUP) * HEAD_DIM
    KVB = KVW // LANES
    MAIN = 4 * Dh + 2 * KVW
    F_OFF = 3 * Dh
    tm = min(ROW_TILE_CAP, T)
    tm2 = min(2 * ROW_TILE_CAP, T)
    tq = min(512, T)
    tk = min(512, T)
    nk = T // tk
    cx, cy, cc = _place()
    near = [2 * (cx ^ cc) + (cy ^ (1 - cc)), 2 * (1 - cx) + (1 - cy), 2 * (cx ^ (1 - cc)) + (cy ^ cc)]
    slots = jnp.stack([2 * q + cc for q in near] + near).astype(jnp.int32)
    own = jnp.stack([4 * cx + 2 * cy + cc, 2 * cx + cy]).astype(jnp.int32)

    tr = jnp.transpose
    big_w = [tr(ffn1_w_gate[0]), tr(ffn1_w_up[0]), ffn1_w_down[0], tr(w_in[0]), w_out[0], tr(ffn2_w_gate[0]), tr(ffn2_w_up[0]),
             ffn2_w_down[0]]
    big_m = [tr(m_ffn1_w_gate[0]), tr(m_ffn1_w_up[0]), m_ffn1_w_down[0], tr(m_w_in[0]), m_w_out[0], tr(m_ffn2_w_gate[0]),
             tr(m_ffn2_w_up[0]), m_ffn2_w_down[0]]
    big_v = [tr(v_ffn1_w_gate[0]), tr(v_ffn1_w_up[0]), v_ffn1_w_down[0], tr(v_w_in[0]), v_w_out[0], tr(v_ffn2_w_gate[0]),
             tr(v_ffn2_w_up[0]), v_ffn2_w_down[0]]
    transposed = {"ffn1_w_gate", "ffn1_w_up", "w_in", "ffn2_w_gate", "ffn2_w_up"}
    names = ["ffn1_w_gate", "ffn1_w_up", "ffn1_w_down", "w_in", "w_out", "ffn2_w_gate", "ffn2_w_up", "ffn2_w_down"]
    sh = dict(zip(names, [w.astype(BF16) for w in big_w]))
    lane = jnp.arange(LANES)
    inv_freq = ROPE_THETA ** (-(2.0 * (lane % (HEAD_DIM // 2))).astype(F32) / HEAD_DIM)
    ang = positions[0].astype(F32)[:, None] * inv_freq[None, :]
    cos_t = jnp.cos(ang)
    sin_t = jnp.where((lane & (HEAD_DIM // 2)) == 0, -1.0, 1.0)[None, :] * jnp.sin(ang)
    rope = (cos_t, sin_t)

    def pair_gain(g, blocks):
        return jnp.tile(jnp.concatenate([g[0], g[0]])[None, None, :], (blocks, 1, 1))

    n1, (wg1,) = _rmsnorm_fwd("ffn1_norm", xs, norm_ffn1_g, tm, carry=_gather_carry([sh["ffn1_w_gate"]]))
    a1, (wu1,) = _ffn_gate("ffn1_gate", n1, wg1, tm, carry=_gather_carry([sh["ffn1_w_up"]]))
    (b1, hm1), (wd1,) = _ffn_up_only("ffn1_up", n1, wu1, a1, tm, carry=_gather_carry([sh["ffn1_w_down"]]))
    h1, (win_g,) = _ffn_down("ffn1_down", hm1, wd1, xs, tm, D, carry=_gather_carry([sh["w_in"]]))
    n_in = win_g.shape[1]
    win_t = win_g.reshape(N_DEV * n_in, D)
    win_main = jnp.concatenate([win_t[:F_OFF], win_t[F_OFF + H:]], axis=0)
    win_f = jnp.pad(win_t[F_OFF:F_OFF + H], ((0, LANES - H), (0, 0)))

    u = _rmsnorm_fwd("mix_norm", h1, norm_mix_g, tm)
    proj, (wout_g,) = _mm("mix_proj", u, win_main, T, MAIN // 9, dims=NT, carry=_gather_carry([sh["w_out"]]))
    wout = wout_g.reshape(D, D)
    proj_f = _mm("mix_proj_forget", u, win_f, T, LANES, dims=NT)
    scale = HEAD_DIM ** -0.5
    fox_gains = jnp.concatenate([pair_gain(fox_q_norm_g, HP), pair_gain(fox_k_norm_g, HP)])
    qk_f = _headnorm_fwd_scaled("fox_qk_norm", proj, 0, 2 * HP, fox_gains, T, scale, HP)
    v_f = proj[:, 2 * Dh:3 * Dh].astype(BF16)
    c_t, sg_t = _forget_fwd("forget_gates", proj_f[:, :H].T, b_forget.reshape(H, 1))
    crow = c_t.reshape(H, nk, 1, tk)
    (o_fox, lse_fa, lse_fb), (wg2, wu2) = _fox_fwd("fox_attention", qk_f, v_f, crow, tq, tk,
                                                   carry=_gather_carry([sh["ffn2_w_gate"], sh["ffn2_w_up"]]))

    swa_q_gains = pair_gain(swa_q_norm_g, HP)
    swa_k_gains = pair_gain(swa_k_norm_g, KVB)
    q_s = _headnorm_fwd("swa_q_norm", proj, 3 * HP, HP, swa_q_gains, T, scale, rope=rope)
    k_d = _headnorm_fwd("swa_k_norm", proj, 4 * HP, KVB, swa_k_gains, T, 1.0, rope=rope, dup=True)
    v_s = proj[:, 4 * Dh + KVW:].astype(BF16).reshape(T, H // GQA_GROUP, 1, HEAD_DIM)
    v_d = jnp.broadcast_to(v_s, (T, H // GQA_GROUP, 2, HEAD_DIM)).reshape(T, 2 * KVW)
    sinks3 = swa_sinks.reshape(H, 1, 1)
    o_swa, lse_sa, lse_sb = _swa_fwd("swa_attention", q_s, k_d, v_d, sinks3)

    on = _outnorm_fwd("out_norm", o_fox, o_swa, out_norm_fox_g, out_norm_swa_g, tm)
    h2 = _mm("mix_out", on, wout, tm2, min(512, D), resid=h1)

    n2 = _rmsnorm_fwd("ffn2_norm", h2, norm_ffn2_g, tm)
    (a2, b2, hm2), (wd2,) = _ffn_up("ffn2_up", n2, wg2, wu2, tm2, carry=_gather_carry([sh["ffn2_w_down"]]))
    y = _ffn_down("ffn2_down", hm2, wd2, h2, tm, D)
    dy, dyh, sq = _loss_grad("loss_grad", y, target, min(256, T))
    loss_part = 0.5 * sq[0, 0] / D

    J, Fs, _ = wg2.shape
    aspec = pl.BlockSpec((None, tm, Fs), lambda i, j: (j, i, 0))
    wspec = pl.BlockSpec((None, Fs, D), lambda i, j: (j, 0, 0))
    got = {}
    local = {}

    def pair_sums(keys, grads, received):
        for nm, g, r in zip(keys, grads, received):
            local[nm] = (g, r)
        return [_pair_add("sum_" + nm, g, r, slots) for nm, g, r in zip(keys, grads, received)]

    def relay_sums(keys, sums, hop1):
        out = []
        for i, (nm, s) in enumerate(zip(keys, sums)):
            got[nm] = [hop1[2 * i]]
            out.append(_relay_add("relay_" + nm, s, hop1[2 * i + 1]))
        return out

    def arrived(keys, hop2):
        for nm, blk in zip(keys, hop2):
            got[nm].append(blk)

    dwd2 = _wgrad_down("ffn2_wgrad_down", hm2, dyh, min(1024, D))
    (da2, db2), (sib_d2,) = _ffn_bwd_mid("ffn2_bwd_mid", dyh, wd2, a2, b2, tm2, carry=_sibling_carry([dwd2]))
    (sum_wd2,) = pair_sums(names[7:8], [dwd2], [sib_d2])
    (dwg2, dwu2), hop1 = _wgrad_up("ffn2_wgrad_up", n2, da2, db2, min(1024, D), carry=_to_partner_carry([sum_wd2]))
    (t_wd2,) = relay_sums(names[7:8], [sum_wd2], hop1)
    dn2, (via_wd2, *sib2) = _reduce_mm("ffn2_bwd_in", [(da2, aspec, wg2, wspec), (db2, aspec, wu2, wspec)], [], NN, T, D, tm, J,
                                       carry=_join(_to_other_carry([t_wd2]), _sibling_carry([dwg2, dwu2])))
    arrived(names[7:8], [via_wd2])
    dh2, dg_ffn2, dh2b = _rmsnorm_bwd("ffn2_norm_bwd", dn2, h2, norm_ffn2_g, dy, min(256, T), 1.0)
    sum_wg2, sum_wu2 = pair_sums(names[5:7], [dwg2, dwu2], sib2)

    dwout = _wgrad_2d("mix_out_wgrad", on, dh2b, min(512, D), min(1024, D))
    dwout_g = dwout.reshape(N_DEV, D // N_DEV, D)
    do_fox, dg_of = _outnorm_bwd("out_norm_bwd_fox", dh2b, wout, 0, o_fox, out_norm_fox_g, tm)
    do_swa, dg_os = _outnorm_bwd("out_norm_bwd_swa", dh2b, wout, 1, o_swa, out_norm_swa_g, tm)

    (dq_f, dk_f, dv_f, dc_a, dc_b, dr_a, dr_b), (*hop1, sib_wout) = _fox_bwd(
        "fox_attention_bwd", qk_f, v_f, o_fox, do_fox, crow, lse_fa, lse_fb, tq, tk,
        carry=_join(_to_partner_carry([sum_wg2, sum_wu2]), _sibling_carry([dwout_g])))
    t_wg2, t_wu2 = relay_sums(names[5:7], [sum_wg2, sum_wu2], hop1)
    (sum_wout,) = pair_sums(names[4:5], [dwout_g], [sib_wout])
    dqf_raw, dg_fq = _headnorm_bwd("fox_q_norm_bwd", dq_f, proj, 0, HP, fox_gains[:HP], HP, T, 1.0)
    dkf_raw, dg_fk = _headnorm_bwd("fox_k_norm_bwd", dk_f, proj, HP, HP, fox_gains[HP:], HP, T, 1.0)
    dct = jnp.stack([dc_a.reshape(HP, T), dc_b.reshape(HP, T)], axis=1).reshape(H, T)
    drt = jnp.stack([dr_a.reshape(HP, T), dr_b.reshape(HP, T)], axis=1).reshape(H, T)
    dz_t, db_f = _forget_bwd("forget_gates_bwd", dct, drt, sg_t)

    (dq_s, dk_p, dv_p, dsink_a, dsink_b), hop2 = _swa_bwd(
        "swa_attention_bwd", q_s, k_d, v_d, sinks3, o_swa, do_swa, lse_sa, lse_sb, carry=_to_other_carry([t_wg2, t_wu2]))
    arrived(names[5:7], hop2)
    dqs_raw, dg_sq = _headnorm_bwd("swa_q_norm_bwd", dq_s, proj, 3 * HP, HP, swa_q_gains, HP, T, 1.0, rope=rope)
    dks_raw, dg_sk = _headnorm_bwd("swa_k_norm_bwd", dk_p, proj, 4 * HP, KVB, swa_k_gains, KVB, T, 1.0, rope=rope, fold=True)
    dvs_raw, _ = _headnorm_bwd("swa_v_fold", dv_p, None, 0, KVB, None, KVB, T, 1.0, fold=True, norm=False)

    dproj = jnp.concatenate([dqf_raw, dkf_raw, dv_f.astype(BF16), dqs_raw, dks_raw, dvs_raw], axis=1)
    dproj_f = jnp.pad(dz_t.T, ((0, 0), (0, LANES - H))).astype(BF16)
    dwin_main, hop1 = _wgrad_2d("mix_proj_wgrad", dproj, u, MAIN // 9, min(1024, D), carry=_to_partner_carry([sum_wout]))
    (t_wout,) = relay_sums(names[4:5], [sum_wout], hop1)
    dwin_f = _wgrad_2d("mix_proj_forget_wgrad", dproj_f, u, LANES, min(1024, D))
    dwin_t = jnp.concatenate([dwin_main[:F_OFF], dwin_f[:H], dwin_main[F_OFF:]], axis=0)
    dwin_g = dwin_t.reshape(N_DEV, n_in, D)
    tkb = MAIN // 9
    du, (via_wout, sib_win) = _reduce_mm(
        "mix_bwd_in",
        [(dproj, pl.BlockSpec((tm2, tkb), lambda i, r: (i, r)), win_main, pl.BlockSpec((tkb, D), lambda i, r: (r, 0)))],
        [(dproj_f, pl.BlockSpec((tm2, LANES), lambda i, r: (i, 0)), win_f, pl.BlockSpec((LANES, D), lambda i, r: (0, 0)))],
        NN, T, D, tm2, 9, carry=_join(_to_other_carry([t_wout]), _sibling_carry([dwin_g])))
    arrived(names[4:5], [via_wout])
    dh1, dg_mix, dh1h = _rmsnorm_bwd("mix_norm_bwd", du, h1, norm_mix_g, dh2, min(256, T), 0.5)
    (sum_win,) = pair_sums(names[3:4], [dwin_g], [sib_win])

    dwd1, hop1 = _wgrad_down("ffn1_wgrad_down", hm1, dh1h, min(1024, D), carry=_to_partner_carry([sum_win]))
    (t_win,) = relay_sums(names[3:4], [sum_win], hop1)
    (da1, db1), (via_win, sib_d) = _ffn_bwd_mid("ffn1_bwd_mid", dh1h, wd1, a1, b1, tm2,
                                                carry=_join(_to_other_carry([t_win]), _sibling_carry([dwd1])))
    arrived(names[3:4], [via_win])
    (sum_wd1,) = pair_sums(names[2:3], [dwd1], [sib_d])
    dwg1, hop1 = _wgrad_down("ffn1_wgrad_gate", da1, n1, min(1024, D), carry=_to_partner_carry([sum_wd1]))
    (t_wd1,) = relay_sums(names[2:3], [sum_wd1], hop1)
    dwu1, (via_wd1, sib_g) = _wgrad_down("ffn1_wgrad_up", db1, n1, min(1024, D),
                                         carry=_join(_to_other_carry([t_wd1]), _sibling_carry([dwg1])))
    arrived(names[2:3], [via_wd1])
    (sum_wg1,) = pair_sums(names[0:1], [dwg1], [sib_g])
    dn1_gate, (*hop1, sib_u) = _reduce_mm(
        "ffn1_bwd_in_gate", [(da1, aspec, wg1, wspec)], [], NN, T, D, tm, J,
        carry=_join(_to_partner_carry([sum_wg1]), _sibling_carry([dwu1])))
    (t_wg1,) = relay_sums(names[0:1], [sum_wg1], hop1)
    (sum_wu1,) = pair_sums(names[1:2], [dwu1], [sib_u])
    dn1, (via_wg1, *hop1) = _reduce_mm(
        "ffn1_bwd_in_up", [(db1, aspec, wu1, wspec)], [], NN, T, D, tm, J, init=dn1_gate,
        carry=_join(_to_other_carry([t_wg1]), _to_partner_carry([sum_wu1])))
    arrived(names[0:1], [via_wg1])
    (t_wu1,) = relay_sums(names[1:2], [sum_wu1], hop1)
    arrived(names[1:2], _run_carry("grads_exchange", _to_other_carry([t_wu1])))
    dx, dg_ffn1 = _rmsnorm_bwd("ffn1_norm_bwd", dn1, xs, norm_ffn1_g, dh1, min(256, T), None)

    big_out = [_adam_shard("adam_" + nm, local[nm][0], local[nm][1], got[nm], w, m, v, own)
               for nm, w, m, v in zip(names, big_w, big_m, big_v)]

    dsinks = jnp.stack([dsink_a.reshape(HP), dsink_b.reshape(HP)], axis=1).reshape(H)
    small_g = [dg_ffn1, dg_mix, dg_ffn2, dg_of, dg_os, db_f, dg_fq[0, 0, :HEAD_DIM], dg_fk[0, 0, :HEAD_DIM],
               dg_sq[0, 0, :HEAD_DIM], dg_sk[0, 0, :HEAD_DIM], dsinks]
    small_w = [norm_ffn1_g, norm_mix_g, norm_ffn2_g, out_norm_fox_g, out_norm_swa_g, b_forget, fox_q_norm_g, fox_k_norm_g,
               swa_q_norm_g, swa_k_norm_g, swa_sinks]
    small_m = [m_norm_ffn1_g, m_norm_mix_g, m_norm_ffn2_g, m_out_norm_fox_g, m_out_norm_swa_g, m_b_forget, m_fox_q_norm_g,
               m_fox_k_norm_g, m_swa_q_norm_g, m_swa_k_norm_g, m_swa_sinks]
    small_v = [v_norm_ffn1_g, v_norm_mix_g, v_norm_ffn2_g, v_out_norm_fox_g, v_out_norm_swa_g, v_b_forget, v_fox_q_norm_g,
               v_fox_k_norm_g, v_swa_q_norm_g, v_swa_k_norm_g, v_swa_sinks]
    gathered = _gather_small(_pack_small(small_g, D, loss_part))
    small_out = _adam_small("adam_small", gathered, _pack_small(small_w, D), _pack_small(small_m, D), _pack_small(small_v, D))
    loss = small_out[0][5, 0]
    small_out = [_unpack_small(p, D, H) for p in small_out]

    order = ["norm_ffn1_g", "ffn1_w_gate", "ffn1_w_up", "ffn1_w_down", "norm_mix_g", "w_in", "b_forget", "fox_q_norm_g", "fox_k_norm_g",
             "swa_q_norm_g", "swa_k_norm_g", "swa_sinks", "out_norm_fox_g", "out_norm_swa_g", "w_out", "norm_ffn2_g",
             "ffn2_w_gate", "ffn2_w_up", "ffn2_w_down"]
    small_names = ["norm_ffn1_g", "norm_mix_g", "norm_ffn2_g", "out_norm_fox_g", "out_norm_swa_g", "b_forget", "fox_q_norm_g",
                   "fox_k_norm_g", "swa_q_norm_g", "swa_k_norm_g", "swa_sinks"]
    result = [loss, dx[None]]
    for kind in range(4):
        for nm in order:
            if nm in names:
                leaf = big_out[names.index(nm)][kind]
                result.append((tr(leaf) if nm in transposed else leaf)[None])
            else:
                result.append(small_out[kind][small_names.index(nm)])
    return tuple(result)


def _headnorm_fwd_scaled(name, proj, col_off, ncb, gains, tm, scale, n_scaled):
    T = proj.shape[0]

    def body(x_ref, g_ref, o_ref):
        xv = x_ref[...]
        lo = _lane_lo(xv.shape)
        y = xv * _head_rstd(xv, lo) * g_ref[...]
        y = y * jnp.where(pl.program_id(0) < n_scaled, scale, 1.0)
        o_ref[...] = y.astype(BF16)

    return pl.pallas_call(
        body, out_shape=jax.ShapeDtypeStruct((T, ncb * LANES), BF16), grid=(ncb, T // tm),
        in_specs=[pl.BlockSpec((tm, LANES), lambda c, i: (i, col_off + c)), pl.BlockSpec((None, 1, LANES), lambda c, i: (c, 0, 0))],
        out_specs=pl.BlockSpec((tm, LANES), lambda c, i: (i, c)), name=name, compiler_params=_params(2))(proj, gains)
```

```python
import functools

import jax
import jax.numpy as jnp
from jax import lax
from jax.experimental import pallas as pl
from jax.experimental.pallas import tpu as pltpu

F32 = jnp.float32
BF16 = jnp.bfloat16

HEAD_DIM = 64
LANES = 128
WINDOW = 128
GQA_GROUP = 4
EPS = 1e-6
ROPE_THETA = 10000.0
ADAM_LR = 0.001
ADAM_B1 = 0.9
ADAM_B2 = 0.999
ADAM_EPS = 1e-08
ADAM_WD = 0.01
ADAM_STEP = 10
N_DEV = 8
NEG = -1e30
VMEM_LIMIT_V7X = 48 * 1024 * 1024
ROW_TILE_CAP = 512
MESH = pl.DeviceIdType.MESH

NN = (((1,), (0,)), ((), ()))
NT = (((1,), (1,)), ((), ()))
TN = (((0,), (0,)), ((), ()))


def _dot(a, b, dims):
    return lax.dot_general(a, b, dims, preferred_element_type=F32)


def _params(n_axes):
    return pltpu.CompilerParams(dimension_semantics=("arbitrary",) * n_axes, vmem_limit_bytes=VMEM_LIMIT_V7X)


def _row_tile(rows, cap=ROW_TILE_CAP):
    best = None
    for t in range(16, min(rows, cap) + 1, 16):
        if rows % t == 0:
            best = t
    return best or rows


def _lane_lo(shape):
    return lax.broadcasted_iota(jnp.int32, shape, len(shape) - 1) < HEAD_DIM


def _keep(sel, x):
    return jnp.where(sel, x.astype(F32), 0.0).astype(BF16)


_HBM = pl.BlockSpec(memory_space=pltpu.HBM)


class _Carry:
    def __init__(self, inputs, out_shapes, scratch, start, finish, middle=None):
        self.inputs, self.out_shapes, self.scratch = list(inputs), list(out_shapes), list(scratch)
        self.start, self.finish, self.middle = start, finish, middle or (lambda ins, outs, scr: None)


def _join(*carries):
    def hook(which):
        def run(ins, outs, scr):
            i = o = s = 0
            for c in carries:
                ni, no, ns = len(c.inputs), len(c.out_shapes), len(c.scratch)
                getattr(c, which)(ins[i:i + ni], outs[o:o + no], scr[s:s + ns])
                i, o, s = i + ni, o + no, s + ns
        return run

    return _Carry([a for c in carries for a in c.inputs], [a for c in carries for a in c.out_shapes],
                  [a for c in carries for a in c.scratch], hook("start"), hook("finish"), hook("middle"))


def _call(body, *, name, grid, in_specs, out_specs, out_shape, args, scratch_shapes=(), carry=None):
    params = _params(len(grid))
    if carry is None:
        return pl.pallas_call(body, out_shape=out_shape, grid=grid, in_specs=list(in_specs), out_specs=out_specs,
                              scratch_shapes=list(scratch_shapes), name=name, compiler_params=params)(*args)
    single = not isinstance(out_shape, (tuple, list))
    shapes = (out_shape,) if single else tuple(out_shape)
    specs = (out_specs,) if single else tuple(out_specs)
    n_in, n_out, n_scr = len(args), len(shapes), len(scratch_shapes)
    c_in, c_out = len(carry.inputs), len(carry.out_shapes)

    def wrapped(*refs):
        ins, c_ins = refs[:n_in], refs[n_in:n_in + c_in]
        o0 = n_in + c_in
        outs, c_outs = refs[o0:o0 + n_out], refs[o0 + n_out:o0 + n_out + c_out]
        s0 = o0 + n_out + c_out
        scr, c_scr = refs[s0:s0 + n_scr], refs[s0 + n_scr:]
        step, total = pl.program_id(0), grid[0]
        for ax in range(1, len(grid)):
            step, total = step * grid[ax] + pl.program_id(ax), total * grid[ax]

        @pl.when(step == 0)
        def _():
            carry.start(c_ins, c_outs, c_scr)

        @pl.when(step == total // 2)
        def _():
            carry.middle(c_ins, c_outs, c_scr)

        body(*ins, *outs, *scr)

        @pl.when(step == total - 1)
        def _():
            carry.finish(c_ins, c_outs, c_scr)

    res = pl.pallas_call(
        wrapped, out_shape=shapes + tuple(carry.out_shapes), grid=grid, in_specs=list(in_specs) + [_HBM] * c_in,
        out_specs=specs + (_HBM,) * c_out, scratch_shapes=list(scratch_shapes) + carry.scratch, name=name,
        compiler_params=params)(*args, *carry.inputs)
    main = res[:n_out]
    return (main[0] if single else tuple(main)), tuple(res[n_out:])


def _rms_bwd(dn, x, g):
    r = lax.rsqrt(jnp.mean(x * x, axis=-1, keepdims=True) + EPS)
    xh = x * r
    dxh = dn * g
    dx = r * (dxh - xh * jnp.mean(dxh * xh, axis=-1, keepdims=True))
    return dx, jnp.sum(dn * xh, axis=0, keepdims=True)


def _rmsnorm_fwd(name, x, g, tm, carry=None):
    T, D = x.shape

    def body(x_ref, g_ref, o_ref):
        xf = x_ref[...]
        r = lax.rsqrt(jnp.mean(xf * xf, axis=-1, keepdims=True) + EPS)
        o_ref[...] = (xf * r * g_ref[...]).astype(BF16)

    return _call(
        body, name=name, grid=(T // tm,), out_shape=jax.ShapeDtypeStruct((T, D), BF16),
        in_specs=[pl.BlockSpec((tm, D), lambda i: (i, 0)), pl.BlockSpec((1, D), lambda i: (0, 0))],
        out_specs=pl.BlockSpec((tm, D), lambda i: (i, 0)), args=[x, g], carry=carry)


def _outnorm_fwd(name, o_fox, o_swa, g_fox, g_swa, tm):
    T, Dh = o_fox.shape

    def body(a_ref, b_ref, ga_ref, gb_ref, o_ref):
        for ref, g_ref, lo in ((a_ref, ga_ref, 0), (b_ref, gb_ref, Dh)):
            xf = ref[...]
            r = lax.rsqrt(jnp.mean(xf * xf, axis=-1, keepdims=True) + EPS)
            o_ref[:, lo:lo + Dh] = (xf * r * g_ref[...]).astype(BF16)

    row = pl.BlockSpec((tm, Dh), lambda i: (i, 0))
    gain = pl.BlockSpec((1, Dh), lambda i: (0, 0))
    return pl.pallas_call(
        body, out_shape=jax.ShapeDtypeStruct((T, 2 * Dh), BF16), grid=(T // tm,),
        in_specs=[row, row, gain, gain], out_specs=pl.BlockSpec((tm, 2 * Dh), lambda i: (i, 0)),
        name=name, compiler_params=_params(1))(o_fox, o_swa, g_fox, g_swa)


def _outnorm_bwd(name, dhb, wout, half, o, g, tm):
    T, D = dhb.shape
    Dh = o.shape[1]

    def body(a_ref, w_ref, o_ref, g_ref, do_ref, dg_ref):
        don = _dot(a_ref[...], w_ref[...], NT)
        dx, dg = _rms_bwd(don, o_ref[...], g_ref[...])
        do_ref[...] = dx.astype(BF16)

        @pl.when(pl.program_id(0) == 0)
        def _():
            dg_ref[...] = dg

        @pl.when(pl.program_id(0) > 0)
        def _():
            dg_ref[...] += dg

    return pl.pallas_call(
        body, out_shape=(jax.ShapeDtypeStruct((T, Dh), BF16), jax.ShapeDtypeStruct((1, Dh), F32)), grid=(T // tm,),
        in_specs=[pl.BlockSpec((tm, D), lambda i: (i, 0)), pl.BlockSpec((Dh, D), lambda i: (half, 0)),
                  pl.BlockSpec((tm, Dh), lambda i: (i, 0)), pl.BlockSpec((1, Dh), lambda i: (0, 0))],
        out_specs=(pl.BlockSpec((tm, Dh), lambda i: (i, 0)), pl.BlockSpec((1, Dh), lambda i: (0, 0))),
        name=name, compiler_params=_params(1))(dhb, wout, o, g)


def _mm(name, a, b, tm, tn, dims=NN, resid=None, carry=None):
    M, K = a.shape
    transposed = dims == NT
    N = b.shape[0] if transposed else b.shape[1]

    def body(*refs):
        if resid is None:
            a_ref, b_ref, o_ref = refs
            o_ref[...] = _dot(a_ref[...], b_ref[...], dims)
        else:
            a_ref, b_ref, r_ref, o_ref = refs
            o_ref[...] = r_ref[...] + _dot(a_ref[...], b_ref[...], dims)

    ospec = pl.BlockSpec((tm, tn), lambda n, i: (i, n))
    bspec = pl.BlockSpec((tn, K), lambda n, i: (n, 0)) if transposed else pl.BlockSpec((K, tn), lambda n, i: (0, n))
    in_specs = [pl.BlockSpec((tm, K), lambda n, i: (i, 0)), bspec]
    args = [a, b]
    if resid is not None:
        in_specs.append(ospec)
        args.append(resid)
    return _call(body, name=name, grid=(N // tn, M // tm), in_specs=in_specs, out_specs=ospec,
                 out_shape=jax.ShapeDtypeStruct((M, N), F32), args=args, carry=carry)


def _wgrad_2d(name, a, b, tmm, tn, carry=None):
    T, M = a.shape
    N = b.shape[1]

    def body(a_ref, b_ref, o_ref):
        o_ref[...] = _dot(a_ref[...], b_ref[...], TN).astype(BF16)

    return _call(
        body, name=name, grid=(M // tmm, N // tn), out_shape=jax.ShapeDtypeStruct((M, N), BF16),
        in_specs=[pl.BlockSpec((T, tmm), lambda m, n: (0, m)), pl.BlockSpec((T, tn), lambda m, n: (0, n))],
        out_specs=pl.BlockSpec((tmm, tn), lambda m, n: (m, n)), args=[a, b], carry=carry)


def _wgrad_down(name, hm, df, tn, carry=None):
    J, T, Fs = hm.shape
    D = df.shape[1]

    def body(a_ref, b_ref, o_ref):
        o_ref[...] = _dot(a_ref[...], b_ref[...], TN).astype(BF16)

    return _call(
        body, name=name, grid=(J, D // tn), out_shape=jax.ShapeDtypeStruct((J, Fs, D), BF16),
        in_specs=[pl.BlockSpec((None, T, Fs), lambda j, n: (j, 0, 0)), pl.BlockSpec((T, tn), lambda j, n: (0, n))],
        out_specs=pl.BlockSpec((None, Fs, tn), lambda j, n: (j, 0, n)), args=[hm, df], carry=carry)


def _wgrad_up(name, n, da, db, tn, carry=None):
    T, D = n.shape
    J, _, Fs = da.shape

    def body(n_ref, da_ref, db_ref, og_ref, ou_ref):
        nv = n_ref[...]
        og_ref[...] = _dot(da_ref[...], nv, TN).astype(BF16)
        ou_ref[...] = _dot(db_ref[...], nv, TN).astype(BF16)

    act = pl.BlockSpec((None, T, Fs), lambda j, m: (j, 0, 0))
    out = pl.BlockSpec((None, Fs, tn), lambda j, m: (j, 0, m))
    shape = jax.ShapeDtypeStruct((J, Fs, D), BF16)
    return _call(
        body, name=name, grid=(J, D // tn), out_shape=(shape, shape),
        in_specs=[pl.BlockSpec((T, tn), lambda j, m: (0, m)), act, act], out_specs=(out, out),
        args=[n, da, db], carry=carry)


def _reduce_mm(name, pairs, once, dims, T, D, tm, steps, init=None, carry=None):
    n_pairs = len(pairs)
    n_once = len(once)
    n_mm = 2 * (n_pairs + n_once)

    def body(*refs):
        pr = refs[:2 * n_pairs]
        on = refs[2 * n_pairs:n_mm]
        o_ref, acc = refs[-2:]
        r = pl.program_id(1)

        @pl.when(r == 0)
        def _():
            acc[...] = jnp.zeros(acc.shape, F32) if init is None else refs[n_mm][...]

        for p in range(n_pairs):
            acc[...] += _dot(pr[2 * p][...], pr[2 * p + 1][...], dims)

        @pl.when(r == steps - 1)
        def _():
            dn = acc[...]
            for p in range(n_once):
                dn = dn + _dot(on[2 * p][...], on[2 * p + 1][...], dims)
            o_ref[...] = dn

    in_specs, args = [], []
    for a, a_spec, w, w_spec in list(pairs) + list(once):
        in_specs += [a_spec, w_spec]
        args += [a, w]
    row = pl.BlockSpec((tm, D), lambda i, r: (i, 0))
    if init is not None:
        in_specs.append(row)
        args.append(init)
    return _call(body, name=name, grid=(T // tm, steps), in_specs=in_specs, out_specs=row, out_shape=jax.ShapeDtypeStruct((T, D), F32),
                 args=args, scratch_shapes=[pltpu.VMEM((tm, D), F32)], carry=carry)


def _rmsnorm_bwd(name, dn, x, g, dh, tm, bf16_scale, carry=None):
    T, D = x.shape
    emit_bf16 = bf16_scale is not None

    def body(dn_ref, x_ref, g_ref, dh_ref, *outs):
        dxn, dg = _rms_bwd(dn_ref[...], x_ref[...], g_ref[...])
        dx = dh_ref[...] + dxn
        outs[0][...] = dx
        if emit_bf16:
            outs[2][...] = (bf16_scale * dx).astype(BF16)

        @pl.when(pl.program_id(0) == 0)
        def _():
            outs[1][...] = dg

        @pl.when(pl.program_id(0) > 0)
        def _():
            outs[1][...] += dg

    row = pl.BlockSpec((tm, D), lambda i: (i, 0))
    gain = pl.BlockSpec((1, D), lambda i: (0, 0))
    out_shape = [jax.ShapeDtypeStruct((T, D), F32), jax.ShapeDtypeStruct((1, D), F32)]
    out_specs = [row, gain]
    if emit_bf16:
        out_shape.append(jax.ShapeDtypeStruct((T, D), BF16))
        out_specs.append(row)
    return _call(body, name=name, grid=(T // tm,), in_specs=[row, row, gain, row], out_specs=tuple(out_specs),
                 out_shape=tuple(out_shape), args=[dn, x, g, dh], carry=carry)


def _loss_grad(name, y, target, tm):
    T, D = y.shape

    def body(y_ref, t_ref, dy_ref, dyh_ref, sq_ref):
        diff = y_ref[...] - t_ref[...]
        sq = jnp.sum(jnp.sum(diff * diff, axis=1, keepdims=True), axis=0, keepdims=True)
        dy = diff * (1.0 / D)
        dy_ref[...] = dy
        dyh_ref[...] = (0.5 * dy).astype(BF16)

        @pl.when(pl.program_id(0) == 0)
        def _():
            sq_ref[...] = sq

        @pl.when(pl.program_id(0) > 0)
        def _():
            sq_ref[...] += sq

    row = pl.BlockSpec((tm, D), lambda i: (i, 0))
    return pl.pallas_call(
        body, out_shape=(jax.ShapeDtypeStruct((T, D), F32), jax.ShapeDtypeStruct((T, D), BF16), jax.ShapeDtypeStruct((1, 1), F32)),
        grid=(T // tm,), in_specs=[row, row], out_specs=(row, row, pl.BlockSpec((1, 1), lambda i: (0, 0))),
        name=name, compiler_params=_params(1))(y, target)


def _ffn_up(name, n, wg, wu, tm, carry=None):
    T, D = n.shape
    J, Fs, _ = wg.shape

    def body(n_ref, wg_ref, wu_ref, a_ref, b_ref, h_ref):
        xv = n_ref[...]
        a = _dot(xv, wg_ref[...], NT)
        b = _dot(xv, wu_ref[...], NT)
        a_ref[...] = a.astype(BF16)
        b_ref[...] = b.astype(BF16)
        h_ref[...] = (a * jax.nn.sigmoid(a) * b).astype(BF16)

    act = jax.ShapeDtypeStruct((J, T, Fs), BF16)
    wspec = pl.BlockSpec((None, Fs, D), lambda j, i: (j, 0, 0))
    aspec = pl.BlockSpec((None, tm, Fs), lambda j, i: (j, i, 0))
    return _call(
        body, name=name, grid=(J, T // tm), out_shape=(act, act, act),
        in_specs=[pl.BlockSpec((tm, D), lambda j, i: (i, 0)), wspec, wspec], out_specs=(aspec, aspec, aspec),
        args=[n, wg, wu], carry=carry)


def _ffn_gate(name, n, wg, tm, carry=None):
    T, D = n.shape
    J, Fs, _ = wg.shape

    def body(n_ref, wg_ref, a_ref):
        a_ref[...] = _dot(n_ref[...], wg_ref[...], NT).astype(BF16)

    aspec = pl.BlockSpec((None, tm, Fs), lambda j, i: (j, i, 0))
    return _call(
        body, name=name, grid=(J, T // tm), out_shape=jax.ShapeDtypeStruct((J, T, Fs), BF16),
        in_specs=[pl.BlockSpec((tm, D), lambda j, i: (i, 0)), pl.BlockSpec((None, Fs, D), lambda j, i: (j, 0, 0))],
        out_specs=aspec, args=[n, wg], carry=carry)


def _ffn_up_only(name, n, wu, a, tm, carry=None):
    T, D = n.shape
    J, Fs, _ = wu.shape

    def body(n_ref, wu_ref, a_ref, b_ref, h_ref):
        b = _dot(n_ref[...], wu_ref[...], NT)
        a = a_ref[...].astype(F32)
        b_ref[...] = b.astype(BF16)
        h_ref[...] = (a * jax.nn.sigmoid(a) * b).astype(BF16)

    act = jax.ShapeDtypeStruct((J, T, Fs), BF16)
    aspec = pl.BlockSpec((None, tm, Fs), lambda j, i: (j, i, 0))
    return _call(
        body, name=name, grid=(J, T // tm), out_shape=(act, act),
        in_specs=[pl.BlockSpec((tm, D), lambda j, i: (i, 0)), pl.BlockSpec((None, Fs, D), lambda j, i: (j, 0, 0)), aspec],
        out_specs=(aspec, aspec), args=[n, wu, a], carry=carry)


def _ffn_down(name, hm, wd, resid, tm, tn, carry=None):
    J, T, Fs = hm.shape
    D = wd.shape[2]

    def body(h_ref, w_ref, r_ref, o_ref, acc):
        j = pl.program_id(2)

        @pl.when(j == 0)
        def _():
            acc[...] = jnp.zeros(acc.shape, F32)

        acc[...] += _dot(h_ref[...], w_ref[...], NN)

        @pl.when(j == J - 1)
        def _():
            o_ref[...] = r_ref[...] + 0.5 * acc[...]

    tile = pl.BlockSpec((tm, tn), lambda i, n, j: (i, n))
    return _call(
        body, name=name, grid=(T // tm, D // tn, J), out_shape=jax.ShapeDtypeStruct((T, D), F32),
        in_specs=[pl.BlockSpec((None, tm, Fs), lambda i, n, j: (j, i, 0)), pl.BlockSpec((None, Fs, tn), lambda i, n, j: (j, 0, n)), tile],
        out_specs=tile, scratch_shapes=[pltpu.VMEM((tm, tn), F32)], args=[hm, wd, resid], carry=carry)


def _ffn_bwd_mid(name, dfh, wd, a, b, tm, carry=None):
    T, D = dfh.shape
    J, Fs, _ = wd.shape

    def body(df_ref, w_ref, a_ref, b_ref, da_ref, db_ref):
        dhm = _dot(df_ref[...], w_ref[...], NT)
        av = a_ref[...].astype(F32)
        bv = b_ref[...].astype(F32)
        sg = jax.nn.sigmoid(av)
        da_ref[...] = (dhm * bv * (sg * (1.0 + av * (1.0 - sg)))).astype(BF16)
        db_ref[...] = (dhm * (av * sg)).astype(BF16)

    act = jax.ShapeDtypeStruct((J, T, Fs), BF16)
    aspec = pl.BlockSpec((None, tm, Fs), lambda j, i: (j, i, 0))
    return _call(
        body, name=name, grid=(J, T // tm), out_shape=(act, act),
        in_specs=[pl.BlockSpec((tm, D), lambda j, i: (i, 0)), pl.BlockSpec((None, Fs, D), lambda j, i: (j, 0, 0)), aspec, aspec],
        out_specs=(aspec, aspec), args=[dfh, wd, a, b], carry=carry)


def _rot_half(y, lane):
    first = (lane & (HEAD_DIM // 2)) == 0
    return jnp.where(first, pltpu.roll(y, LANES - HEAD_DIM // 2, 1), pltpu.roll(y, HEAD_DIM // 2, 1))


def _head_rstd(x, lo):
    sq = x * x
    ss_a = jnp.sum(jnp.where(lo, sq, 0.0), axis=-1, keepdims=True)
    ss_b = jnp.sum(jnp.where(lo, 0.0, sq), axis=-1, keepdims=True)
    return lax.rsqrt(jnp.where(lo, ss_a, ss_b) * (1.0 / HEAD_DIM) + EPS)


def _headnorm_fwd(name, proj, col_off, ncb, gains, tm, scale, rope=None, dup=False):
    T = proj.shape[0]
    with_rope = rope is not None
    width = 2 * LANES if dup else LANES

    def body(*refs):
        if with_rope:
            x_ref, g_ref, cos_ref, sin_ref, o_ref = refs
        else:
            x_ref, g_ref, o_ref = refs
        xv = x_ref[...]
        lane = lax.broadcasted_iota(jnp.int32, xv.shape, 1)
        lo = lane < HEAD_DIM
        y = xv * _head_rstd(xv, lo) * g_ref[...]
        if with_rope:
            y = y * cos_ref[...] + _rot_half(y, lane) * sin_ref[...]
        y = y * scale
        if dup:
            sw = pltpu.roll(y, HEAD_DIM, 1)
            o_ref[:, :LANES] = jnp.where(lo, y, sw).astype(BF16)
            o_ref[:, LANES:] = jnp.where(lo, sw, y).astype(BF16)
        else:
            o_ref[...] = y.astype(BF16)

    in_specs = [pl.BlockSpec((tm, LANES), lambda c, i: (i, col_off + c)), pl.BlockSpec((None, 1, LANES), lambda c, i: (c, 0, 0))]
    args = [proj, gains]
    if with_rope:
        tab = pl.BlockSpec((tm, LANES), lambda c, i: (i, 0))
        in_specs += [tab, tab]
        args += list(rope)
    return pl.pallas_call(
        body, out_shape=jax.ShapeDtypeStruct((T, ncb * width), BF16), grid=(ncb, T // tm),
        in_specs=in_specs, out_specs=pl.BlockSpec((tm, width), lambda c, i: (i, c)),
        name=name, compiler_params=_params(2))(*args)


def _headnorm_bwd(name, dy, proj, col_off, ncb, gains, group, tm, scale, rope=None, fold=False, norm=True):
    T = dy.shape[0]
    with_rope = rope is not None
    n_groups = ncb // group
    dy_width = 4 * LANES if fold else LANES

    def body(*refs):
        refs = list(refs)
        dy_ref = refs.pop(0)
        x_ref = refs.pop(0) if norm else None
        g_ref = refs.pop(0) if norm else None
        cos_ref = refs.pop(0) if with_rope else None
        sin_ref = refs.pop(0) if with_rope else None
        dx_ref = refs.pop(0)
        dg_ref = refs.pop(0) if norm else None
        c = pl.program_id(0)
        i = pl.program_id(1)
        d = dy_ref[...]
        lane = lax.broadcasted_iota(jnp.int32, (d.shape[0], LANES), 1)
        lo = lane < HEAD_DIM
        if fold:
            t0 = d[:, 0:LANES] + d[:, LANES:2 * LANES]
            t1 = d[:, 2 * LANES:3 * LANES] + d[:, 3 * LANES:4 * LANES]
            d = jnp.where(lo, t0 + pltpu.roll(t0, HEAD_DIM, 1), t1 + pltpu.roll(t1, HEAD_DIM, 1))
        d = d * scale
        if with_rope:
            d = d * cos_ref[...] + _rot_half(d * sin_ref[...], lane)
        if not norm:
            dx_ref[...] = d.astype(BF16)
            return
        xv = x_ref[...]
        gv = g_ref[...]
        r = _head_rstd(xv, lo)
        xh = xv * r
        dxh = d * gv
        pr = dxh * xh
        m_a = jnp.sum(jnp.where(lo, pr, 0.0), axis=-1, keepdims=True)
        m_b = jnp.sum(jnp.where(lo, 0.0, pr), axis=-1, keepdims=True)
        mean = jnp.where(lo, m_a, m_b) * (1.0 / HEAD_DIM)
        dx_ref[...] = (r * (dxh - xh * mean)).astype(BF16)
        dgp = jnp.sum(d * xh, axis=0, keepdims=True)
        dgp = dgp + pltpu.roll(dgp, HEAD_DIM, 1)
        first = jnp.logical_and(c % group == 0, i == 0)

        @pl.when(first)
        def _():
            dg_ref[...] = dgp

        @pl.when(jnp.logical_not(first))
        def _():
            dg_ref[...] += dgp

    in_specs = [pl.BlockSpec((tm, dy_width), lambda c, i: (i, c))]
    args = [dy]
    if norm:
        in_specs += [pl.BlockSpec((tm, LANES), lambda c, i: (i, col_off + c)), pl.BlockSpec((None, 1, LANES), lambda c, i: (c, 0, 0))]
        args += [proj, gains]
    if with_rope:
        tab = pl.BlockSpec((tm, LANES), lambda c, i: (i, 0))
        in_specs += [tab, tab]
        args += list(rope)
    out_shape = [jax.ShapeDtypeStruct((T, ncb * LANES), BF16)]
    out_specs = [pl.BlockSpec((tm, LANES), lambda c, i: (i, c))]
    if norm:
        out_shape.append(jax.ShapeDtypeStruct((n_groups, 1, LANES), F32))
        out_specs.append(pl.BlockSpec((None, 1, LANES), lambda c, i: (c // group, 0, 0)))
    res = pl.pallas_call(
        body, out_shape=tuple(out_shape), grid=(ncb, T // tm), in_specs=in_specs, out_specs=tuple(out_specs),
        name=name, compiler_params=_params(2))(*args)
    return res if norm else (res[0], None)


def _dot_exact(x, tri):
    hi = x.astype(BF16)
    r1 = x - hi.astype(F32)
    mid = r1.astype(BF16)
    lo = (r1 - mid.astype(F32)).astype(BF16)
    return _dot(hi, tri, NN) + _dot(mid, tri, NN) + _dot(lo, tri, NN)


def _forget_fwd(name, zt, bias):
    H, T = zt.shape
    blk = min(256, T)

    def body(z_ref, b_ref, c_ref, s_ref):
        z = z_ref[...] + b_ref[...]
        s_ref[...] = jax.nn.sigmoid(-z)
        lf = jnp.minimum(z, 0.0) - jnp.log(1.0 + jnp.exp(-jnp.abs(z)))
        tri = (lax.broadcasted_iota(jnp.int32, (blk, blk), 0) <= lax.broadcasted_iota(jnp.int32, (blk, blk), 1)).astype(BF16)
        carry = jnp.zeros((H, 1), F32)
        for bi in range(T // blk):
            xb = lf[:, bi * blk:(bi + 1) * blk]
            c_ref[:, bi * blk:(bi + 1) * blk] = _dot_exact(xb, tri) + carry
            carry = carry + jnp.sum(xb, axis=-1, keepdims=True)

    shape = jax.ShapeDtypeStruct((H, T), F32)
    full = pl.BlockSpec((H, T), lambda i: (0, 0))
    return pl.pallas_call(
        body, out_shape=(shape, shape), grid=(1,), in_specs=[full, pl.BlockSpec((H, 1), lambda i: (0, 0))],
        out_specs=(full, full), name=name, compiler_params=_params(1))(zt, bias)


def _forget_bwd(name, dct, drt, sgt):
    H, T = dct.shape
    blk = min(256, T)

    def body(dc_ref, dr_ref, s_ref, dz_ref, db_ref):
        dc = dc_ref[...] + dr_ref[...]
        tri = (lax.broadcasted_iota(jnp.int32, (blk, blk), 0) >= lax.broadcasted_iota(jnp.int32, (blk, blk), 1)).astype(BF16)
        carry = jnp.zeros((H, 1), F32)
        db = jnp.zeros((H, 1), F32)
        for bi in reversed(range(T // blk)):
            xb = dc[:, bi * blk:(bi + 1) * blk]
            dz = (_dot_exact(xb, tri) + carry) * s_ref[:, bi * blk:(bi + 1) * blk]
            dz_ref[:, bi * blk:(bi + 1) * blk] = dz
            db = db + jnp.sum(dz, axis=-1, keepdims=True)
            carry = carry + jnp.sum(xb, axis=-1, keepdims=True)
        db_ref[...] = db

    full = pl.BlockSpec((H, T), lambda i: (0, 0))
    return pl.pallas_call(
        body, out_shape=(jax.ShapeDtypeStruct((H, T), F32), jax.ShapeDtypeStruct((H, 1), F32)), grid=(1,),
        in_specs=[full, full, full], out_specs=(full, pl.BlockSpec((H, 1), lambda i: (0, 0))),
        name=name, compiler_params=_params(1))(dct, drt, sgt)


STRIP = 256


def _fox_fwd(name, qk, v, crow, tq, tk, carry=None):
    T, Dh = v.shape
    HP = Dh // LANES
    nk = T // tk
    assert tk % tq == 0 and tq % STRIP == 0
    n_strips = tq // STRIP

    def body(q_ref, k_ref, v_ref, ra_ref, rb_ref, o_ref, la_ref, lb_ref, s_ref, p_ref, m_ref, l_ref, acc_ref):
        i = pl.program_id(1)
        q2 = q_ref[...]
        lo = _lane_lo((tq, LANES))
        q_st = jnp.concatenate([_keep(lo, q2), _keep(jnp.logical_not(lo), q2)], axis=0)
        r_refs = (ra_ref, rb_ref)
        m_ref[...] = jnp.full(m_ref.shape, NEG, F32)
        l_ref[...] = jnp.zeros(l_ref.shape, F32)
        acc_ref[...] = jnp.zeros(acc_ref.shape, F32)
        rel = lax.broadcasted_iota(jnp.int32, (STRIP, tk), 0) - lax.broadcasted_iota(jnp.int32, (STRIP, tk), 1)

        def chunk(kc, masked):
            start = pl.multiple_of(kc * tk, tk)
            kb = k_ref[pl.ds(start, tk), :]
            vb = v_ref[pl.ds(start, tk), :]
            s_ref[...] = _dot(q_st, kb, NT)
            for h in range(2):
                cs = r_refs[h][kc]
                for st in range(n_strips):
                    rows = pl.ds(h * tq + st * STRIP, STRIP)
                    s = s_ref[rows, :] - cs
                    if masked:
                        s = jnp.where(rel >= start - (i * tq + st * STRIP), s, NEG)
                    m_old = m_ref[rows, :]
                    mn = jnp.maximum(m_old, jnp.max(s, axis=-1, keepdims=True))
                    p = jnp.exp(s - mn)
                    alpha = jnp.exp(m_old - mn)
                    l_ref[rows, :] = alpha * l_ref[rows, :] + jnp.sum(p, axis=-1, keepdims=True)
                    m_ref[rows, :] = mn
                    p_ref[rows, :] = p.astype(BF16)
                    acc_ref[rows, :] = acc_ref[rows, :] * alpha
            acc_ref[...] += _dot(p_ref[...], vb, NN)

        n_full = (i * tq) // tk

        def full_chunk(kc, _):
            chunk(kc, False)
            return 0

        lax.fori_loop(0, n_full, full_chunk, 0)
        chunk(n_full, True)
        top, bot = pl.ds(0, tq), pl.ds(tq, tq)
        o_ref[...] = jnp.where(lo, acc_ref[top, :] / l_ref[top, :], acc_ref[bot, :] / l_ref[bot, :])
        la_ref[...] = m_ref[top, :] + jnp.log(l_ref[top, :])
        lb_ref[...] = m_ref[bot, :] + jnp.log(l_ref[bot, :])

    row = lambda off: pl.BlockSpec((None, nk, 1, tk), lambda h, i: (2 * h + off, 0, 0, 0))
    lse = jax.ShapeDtypeStruct((HP, T, 1), F32)
    lspec = pl.BlockSpec((None, tq, 1), lambda h, i: (h, i, 0))
    scratch = [pltpu.VMEM((2 * tq, tk), F32), pltpu.VMEM((2 * tq, tk), BF16), pltpu.VMEM((2 * tq, 1), F32),
               pltpu.VMEM((2 * tq, 1), F32), pltpu.VMEM((2 * tq, LANES), F32)]
    return _call(
        body, name=name, grid=(HP, T // tq), out_shape=(jax.ShapeDtypeStruct((T, Dh), F32), lse, lse),
        in_specs=[pl.BlockSpec((tq, LANES), lambda h, i: (i, h)), pl.BlockSpec((T, LANES), lambda h, i: (0, HP + h)),
                  pl.BlockSpec((T, LANES), lambda h, i: (0, h)), row(0), row(1)],
        out_specs=(pl.BlockSpec((tq, LANES), lambda h, i: (i, h)), lspec, lspec),
        args=[qk, qk, v, crow, crow], scratch_shapes=scratch, carry=carry)


def _fox_bwd(name, qk, v, o, do, crow, lse_a, lse_b, tq, tk, carry=None):
    T, Dh = v.shape
    HP = Dh // LANES
    nk = T // tk
    scale = HEAD_DIM ** -0.5
    assert tk % tq == 0 and tq % STRIP == 0
    n_strips = tq // STRIP

    def body(q_ref, k_ref, v_ref, o_ref, do_ref, ra_ref, rb_ref, la_ref, lb_ref,
             dq_ref, dk_ref, dv_ref, dca_ref, dcb_ref, dra_ref, drb_ref, s_ref, dp_ref, p_ref, ds_ref, dq_acc, dsum_ref):
        i = pl.program_id(1)

        @pl.when(i == 0)
        def _():
            dk_ref[...] = jnp.zeros_like(dk_ref)
            dv_ref[...] = jnp.zeros_like(dv_ref)
            dca_ref[...] = jnp.zeros_like(dca_ref)
            dcb_ref[...] = jnp.zeros_like(dcb_ref)

        q2 = q_ref[...]
        do2 = do_ref[...]
        lo = _lane_lo((tq, LANES))
        hi = jnp.logical_not(lo)
        q_st = jnp.concatenate([_keep(lo, q2), _keep(hi, q2)], axis=0)
        do_st = jnp.concatenate([_keep(lo, do2), _keep(hi, do2)], axis=0)
        prod = do2.astype(F32) * o_ref[...]
        dsum_ref[pl.ds(0, tq), :] = jnp.sum(jnp.where(lo, prod, 0.0), axis=-1, keepdims=True)
        dsum_ref[pl.ds(tq, tq), :] = jnp.sum(jnp.where(lo, 0.0, prod), axis=-1, keepdims=True)
        r_refs, l_refs, dc_refs, dr_refs = (ra_ref, rb_ref), (la_ref, lb_ref), (dca_ref, dcb_ref), (dra_ref, drb_ref)
        dq_acc[...] = jnp.zeros(dq_acc.shape, F32)
        dra_ref[...] = jnp.zeros(dra_ref.shape, F32)
        drb_ref[...] = jnp.zeros(drb_ref.shape, F32)
        rel = lax.broadcasted_iota(jnp.int32, (STRIP, tk), 0) - lax.broadcasted_iota(jnp.int32, (STRIP, tk), 1)

        def chunk(kc, masked):
            start = pl.multiple_of(kc * tk, tk)
            kb = k_ref[pl.ds(start, tk), :]
            vb = v_ref[pl.ds(start, tk), :]
            s_ref[...] = _dot(q_st, kb, NT)
            dp_ref[...] = _dot(do_st, vb, NT)
            for h in range(2):
                cs = r_refs[h][kc]
                col_sum = jnp.zeros((1, tk), F32)
                for st in range(n_strips):
                    rows = pl.ds(st * STRIP, STRIP)
                    both = pl.ds(h * tq + st * STRIP, STRIP)
                    s = s_ref[both, :] - cs
                    if masked:
                        s = jnp.where(rel >= start - (i * tq + st * STRIP), s, NEG)
                    p = jnp.exp(s - l_refs[h][rows, :])
                    ds = p * (dp_ref[both, :] - dsum_ref[both, :])
                    p_ref[both, :] = p.astype(BF16)
                    ds_ref[both, :] = ds.astype(BF16)
                    col_sum = col_sum + jnp.sum(ds, axis=0, keepdims=True)
                    dr_refs[h][rows, :] += jnp.sum(ds, axis=-1, keepdims=True)
                dc_refs[h][kc] = dc_refs[h][kc] - col_sum
            dk_ref[pl.ds(start, tk), :] += _dot(ds_ref[...], q_st, TN)
            dv_ref[pl.ds(start, tk), :] += _dot(p_ref[...], do_st, TN)
            dq_acc[...] += _dot(ds_ref[...], kb, NN)

        n_full = (i * tq) // tk

        def full_chunk(kc, _):
            chunk(kc, False)
            return 0

        lax.fori_loop(0, n_full, full_chunk, 0)
        chunk(n_full, True)
        dq_ref[...] = jnp.where(lo, dq_acc[pl.ds(0, tq), :], dq_acc[pl.ds(tq, tq), :]) * scale

    row = lambda off: pl.BlockSpec((None, nk, 1, tk), lambda h, i: (2 * h + off, 0, 0, 0))
    lspec = pl.BlockSpec((None, tq, 1), lambda h, i: (h, i, 0))
    qspec = pl.BlockSpec((tq, LANES), lambda h, i: (i, h))
    full = pl.BlockSpec((T, LANES), lambda h, i: (0, h))
    dcspec = pl.BlockSpec((None, nk, 1, tk), lambda h, i: (h, 0, 0, 0))
    grad = jax.ShapeDtypeStruct((T, Dh), F32)
    dc = jax.ShapeDtypeStruct((HP, nk, 1, tk), F32)
    dr = jax.ShapeDtypeStruct((HP, T, 1), F32)
    scratch = [pltpu.VMEM((2 * tq, tk), F32), pltpu.VMEM((2 * tq, tk), F32), pltpu.VMEM((2 * tq, tk), BF16), pltpu.VMEM((2 * tq, tk), BF16),
               pltpu.VMEM((2 * tq, LANES), F32), pltpu.VMEM((2 * tq, 1), F32)]
    return _call(
        body, name=name, grid=(HP, T // tq), out_shape=(grad, grad, grad, dc, dc, dr, dr),
        in_specs=[qspec, pl.BlockSpec((T, LANES), lambda h, i: (0, HP + h)), full, qspec, qspec, row(0), row(1), lspec, lspec],
        out_specs=(qspec, full, full, dcspec, dcspec, lspec, lspec),
        args=[qk, qk, v, o, do, crow, crow, lse_a, lse_b], scratch_shapes=scratch, carry=carry)


SWA_GROUP = 2
SWA_GROUP_BWD = 4


def _swa_block(n, q_ref, k_ref):
    qs = pl.multiple_of(n * WINDOW, WINDOW)
    ks = pl.multiple_of(jnp.maximum(n - 1, 0) * WINDOW, WINDOW)
    rel = (qs + lax.broadcasted_iota(jnp.int32, (WINDOW, 2 * WINDOW), 0)) - (ks + lax.broadcasted_iota(jnp.int32, (WINDOW, 2 * WINDOW), 1))
    valid = jnp.logical_and(rel >= 0, rel < WINDOW)
    return qs, ks, valid


def _swa_fwd(name, q, kd, vd, sinks, carry=None):
    T, Dh = q.shape
    HP = Dh // LANES

    def body(q_ref, k_ref, v_ref, sa_ref, sb_ref, o_ref, la_ref, lb_ref):
        lo = _lane_lo((WINDOW, LANES))

        top = lax.broadcasted_iota(jnp.int32, (2 * WINDOW, 1), 0) < WINDOW
        sink = jnp.where(top, sa_ref[...], sb_ref[...])

        def block(n, _):
            qs, ks, valid = _swa_block(n, q_ref, k_ref)
            q2 = q_ref[pl.ds(qs, WINDOW), :]
            kb = k_ref[pl.ds(ks, 2 * WINDOW), :]
            vb = v_ref[pl.ds(ks, 2 * WINDOW), :]
            q_st = jnp.concatenate([_keep(lo, q2), _keep(jnp.logical_not(lo), q2)], axis=0)
            s = jnp.where(jnp.concatenate([valid, valid], axis=0), _dot(q_st, kb, NT), NEG)
            m = jnp.maximum(jnp.max(s, axis=-1, keepdims=True), sink)
            p = jnp.exp(s - m)
            l = jnp.sum(p, axis=-1, keepdims=True) + jnp.exp(sink - m)
            o2 = _dot(p.astype(BF16), vb, NN) / l
            lse = m + jnp.log(l)
            o_ref[pl.ds(qs, WINDOW), :] = jnp.where(lo, o2[:WINDOW], o2[WINDOW:])
            la_ref[pl.ds(qs, WINDOW), :] = lse[:WINDOW]
            lb_ref[pl.ds(qs, WINDOW), :] = lse[WINDOW:]
            return 0

        assert (T // WINDOW) % SWA_GROUP == 0

        def group(g, c):
            for b in range(SWA_GROUP):
                c = block(g * SWA_GROUP + b, c)
            return c

        lax.fori_loop(0, T // WINDOW // SWA_GROUP, group, 0)

    full = pl.BlockSpec((T, LANES), lambda h: (0, h))
    kv = pl.BlockSpec((T, LANES), lambda h: (0, h // 2))
    sink = lambda off: pl.BlockSpec((None, 1, 1), lambda h: (2 * h + off, 0, 0))
    lse = jax.ShapeDtypeStruct((HP, T, 1), F32)
    lspec = pl.BlockSpec((None, T, 1), lambda h: (h, 0, 0))
    return _call(
        body, name=name, grid=(HP,), out_shape=(jax.ShapeDtypeStruct((T, Dh), F32), lse, lse),
        in_specs=[full, kv, kv, sink(0), sink(1)], out_specs=(full, lspec, lspec),
        args=[q, kd, vd, sinks, sinks], carry=carry)


def _swa_bwd(name, q, kd, vd, sinks, o, do, lse_a, lse_b, carry=None):
    T, Dh = q.shape
    HP = Dh // LANES
    scale = HEAD_DIM ** -0.5

    def body(q_ref, k_ref, v_ref, sa_ref, sb_ref, o_ref, do_ref, la_ref, lb_ref, dq_ref, dk_ref, dv_ref, dsa_ref, dsb_ref):
        lo = _lane_lo((WINDOW, LANES))
        hi = jnp.logical_not(lo)
        dk_ref[...] = jnp.zeros_like(dk_ref)
        dv_ref[...] = jnp.zeros_like(dv_ref)

        top = lax.broadcasted_iota(jnp.int32, (2 * WINDOW, 1), 0) < WINDOW
        sink = jnp.where(top, sa_ref[...], sb_ref[...])

        def block(n, dsinks):
            qs, ks, valid = _swa_block(n, q_ref, k_ref)
            rows = pl.ds(qs, WINDOW)
            q2 = q_ref[rows, :]
            do2 = do_ref[rows, :]
            kb = k_ref[pl.ds(ks, 2 * WINDOW), :]
            vb = v_ref[pl.ds(ks, 2 * WINDOW), :]
            prod = do2.astype(F32) * o_ref[rows, :]
            q_st = jnp.concatenate([_keep(lo, q2), _keep(hi, q2)], axis=0)
            do_st = jnp.concatenate([_keep(lo, do2), _keep(hi, do2)], axis=0)
            dsum = jnp.concatenate([jnp.sum(jnp.where(lo, prod, 0.0), axis=-1, keepdims=True),
                                    jnp.sum(jnp.where(lo, 0.0, prod), axis=-1, keepdims=True)], axis=0)
            lse = jnp.concatenate([la_ref[rows, :], lb_ref[rows, :]], axis=0)
            s = jnp.where(jnp.concatenate([valid, valid], axis=0), _dot(q_st, kb, NT), NEG)
            p = jnp.exp(s - lse)
            ds = p * (_dot(do_st, vb, NT) - dsum)
            dsb = ds.astype(BF16)
            dq2 = _dot(dsb, kb, NN)
            dq_ref[rows, :] = jnp.where(lo, dq2[:WINDOW], dq2[WINDOW:]) * scale
            dk_ref[pl.ds(ks, 2 * WINDOW), :] += _dot(dsb, q_st, TN)
            dv_ref[pl.ds(ks, 2 * WINDOW), :] += _dot(p.astype(BF16), do_st, TN)
            gone = jnp.exp(sink - lse) * dsum
            return (dsinks[0] - jnp.sum(gone[:WINDOW], axis=0, keepdims=True),
                    dsinks[1] - jnp.sum(gone[WINDOW:], axis=0, keepdims=True))

        assert (T // WINDOW) % SWA_GROUP_BWD == 0

        def group(g, c):
            for b in range(SWA_GROUP_BWD):
                c = block(g * SWA_GROUP_BWD + b, c)
            return c

        dsa, dsb_ = lax.fori_loop(0, T // WINDOW // SWA_GROUP_BWD, group, (jnp.zeros((1, 1), F32), jnp.zeros((1, 1), F32)))
        dsa_ref[...] = dsa
        dsb_ref[...] = dsb_

    full = pl.BlockSpec((T, LANES), lambda h: (0, h))
    kv = pl.BlockSpec((T, LANES), lambda h: (0, h // 2))
    sink = lambda off: pl.BlockSpec((None, 1, 1), lambda h: (2 * h + off, 0, 0))
    lspec = pl.BlockSpec((None, T, 1), lambda h: (h, 0, 0))
    dsink = pl.BlockSpec((None, 1, 1), lambda h: (h, 0, 0))
    grad = jax.ShapeDtypeStruct((T, Dh), F32)
    ds_shape = jax.ShapeDtypeStruct((HP, 1, 1), F32)
    return _call(
        body, name=name, grid=(HP,), out_shape=(grad, grad, grad, ds_shape, ds_shape),
        in_specs=[full, kv, kv, sink(0), sink(1), full, full, lspec, lspec],
        out_specs=(full, full, full, dsink, dsink),
        args=[q, kd, vd, sinks, sinks, o, do, lse_a, lse_b], carry=carry)


def _place():
    return lax.axis_index("x"), lax.axis_index("y"), lax.axis_index("c")


def _run_carry(name, carry):
    c_in, c_out = len(carry.inputs), len(carry.out_shapes)

    def body(*refs):
        ins, outs, scr = refs[:c_in], refs[c_in:c_in + c_out], refs[c_in + c_out:]
        carry.start(ins, outs, scr)
        carry.middle(ins, outs, scr)
        carry.finish(ins, outs, scr)

    return pl.pallas_call(
        body, out_shape=tuple(carry.out_shapes), in_specs=[_HBM] * c_in, out_specs=tuple([_HBM] * c_out),
        scratch_shapes=carry.scratch, name=name)(*carry.inputs)


def _gather_carry(shards):
    n = len(shards)

    def plan(ins, outs, scr):
        send, recv, local = scr
        x, y, c = _place()
        me, sibling = (x, y, c), (x, y, 1 - c)
        partner, other, diag = (x ^ c, y ^ (1 - c)), (x ^ (1 - c), y ^ c), (1 - x, 1 - y)

        def copy(w, k, block, to, src=None):
            slot = 4 * block[0] + 2 * block[1] + block[2]
            return pltpu.make_async_remote_copy(
                src_ref=outs[w].at[slot] if src is None else src, dst_ref=outs[w].at[slot],
                send_sem=send.at[w, k], recv_sem=recv.at[w, k], device_id=to, device_id_type=MESH)

        def own():
            return [pltpu.make_async_copy(ins[w], outs[w].at[4 * x + 2 * y + c], local.at[w]) for w in range(n)]

        return copy, own, me, sibling, partner, other, diag, c

    def start(ins, outs, scr):
        copy, own, me, sibling, partner, other, _, c = plan(ins, outs, scr)
        for cp in own():
            cp.start()
        for w in range(n):
            copy(w, 1, me, (*partner, c), src=ins[w]).start()
            copy(w, 2, me, (*other, c), src=ins[w]).start()
            copy(w, 0, me, sibling, src=ins[w]).start()

    def middle(ins, outs, scr):
        copy, _, me, sibling, partner, other, _, c = plan(ins, outs, scr)
        for w in range(n):
            copy(w, 1, (*partner, c), me).wait_recv()
            copy(w, 3, (*partner, c), (*other, c)).start()
            copy(w, 4, (*partner, c), sibling).start()
        for w in range(n):
            copy(w, 2, (*other, c), me).wait_recv()
            copy(w, 5, (*other, c), sibling).start()

    def finish(ins, outs, scr):
        copy, own, me, sibling, partner, other, diag, c = plan(ins, outs, scr)
        for w in range(n):
            copy(w, 3, (*diag, c), me).wait_recv()
            copy(w, 6, (*diag, c), sibling).start()
        for w in range(n):
            copy(w, 0, sibling, me).wait_recv()
            copy(w, 4, (*other, 1 - c), me).wait_recv()
            copy(w, 5, (*partner, 1 - c), me).wait_recv()
            copy(w, 6, (*diag, 1 - c), me).wait_recv()
        for w in range(n):
            sent = [copy(w, 0, me, sibling, src=ins[w]), copy(w, 1, me, (*partner, c), src=ins[w]), copy(w, 2, me, (*other, c), src=ins[w]),
                    copy(w, 3, (*partner, c), (*other, c)), copy(w, 4, (*partner, c), sibling), copy(w, 5, (*other, c), sibling),
                    copy(w, 6, (*diag, c), sibling)]
            for cp in sent:
                cp.wait_send()
        for cp in own():
            cp.wait()

    return _Carry(shards, [jax.ShapeDtypeStruct((N_DEV,) + s.shape, s.dtype) for s in shards],
                  [pltpu.SemaphoreType.DMA((n, 7)), pltpu.SemaphoreType.DMA((n, 7)), pltpu.SemaphoreType.DMA((n,))], start, finish, middle)


def _sibling_carry(grads):
    n = len(grads)

    def copies(ins, outs, scr):
        send, recv = scr
        x, y, c = _place()
        return [pltpu.make_async_remote_copy(
            src_ref=ins[w].at[2 * q + (1 - c)], dst_ref=outs[w].at[q], send_sem=send.at[w, q], recv_sem=recv.at[w, q],
            device_id=(x, y, 1 - c), device_id_type=MESH) for w in range(n) for q in range(4)]

    def start(ins, outs, scr):
        for cp in copies(ins, outs, scr):
            cp.start()

    def finish(ins, outs, scr):
        for cp in copies(ins, outs, scr):
            cp.wait()

    return _Carry(grads, [jax.ShapeDtypeStruct((4,) + g.shape[1:], g.dtype) for g in grads],
                  [pltpu.SemaphoreType.DMA((n, 4)), pltpu.SemaphoreType.DMA((n, 4))], start, finish)


def _to_partner_carry(sums):
    n = len(sums)

    def copies(ins, outs, scr):
        send, recv = scr
        x, y, c = _place()
        return [pltpu.make_async_remote_copy(
            src_ref=ins[w].at[k], dst_ref=outs[2 * w + k], send_sem=send.at[w, k], recv_sem=recv.at[w, k],
            device_id=(x ^ c, y ^ (1 - c), c), device_id_type=MESH) for w in range(n) for k in range(2)]

    def start(ins, outs, scr):
        for cp in copies(ins, outs, scr):
            cp.start()

    def finish(ins, outs, scr):
        for cp in copies(ins, outs, scr):
            cp.wait()

    return _Carry(sums, [jax.ShapeDtypeStruct(s.shape[1:], s.dtype) for s in sums for _ in range(2)],
                  [pltpu.SemaphoreType.DMA((n, 2)), pltpu.SemaphoreType.DMA((n, 2))], start, finish)


def _to_other_carry(blocks):
    n = len(blocks)

    def copies(ins, outs, scr):
        send, recv = scr
        x, y, c = _place()
        return [pltpu.make_async_remote_copy(
            src_ref=ins[w], dst_ref=outs[w], send_sem=send.at[w], recv_sem=recv.at[w],
            device_id=(x ^ (1 - c), y ^ c, c), device_id_type=MESH) for w in range(n)]

    def start(ins, outs, scr):
        for cp in copies(ins, outs, scr):
            cp.start()

    def finish(ins, outs, scr):
        for cp in copies(ins, outs, scr):
            cp.wait()

    return _Carry(blocks, [jax.ShapeDtypeStruct(b.shape, b.dtype) for b in blocks],
                  [pltpu.SemaphoreType.DMA((n,)), pltpu.SemaphoreType.DMA((n,))], start, finish)


def _gather_small(packed):
    R, C = packed.shape

    def body(in_ref, out_ref, send, recv):
        x, y, c = _place()
        mine = 4 * x + 2 * y + c
        out_ref[mine] = in_ref[...]
        copies = []
        for k in range(1, N_DEV):
            peer = (x ^ (k >> 2), y ^ ((k >> 1) & 1), c ^ (k & 1))
            copies.append(pltpu.make_async_remote_copy(
                src_ref=in_ref, dst_ref=out_ref.at[mine], send_sem=send.at[k - 1], recv_sem=recv.at[k - 1],
                device_id=peer, device_id_type=MESH))
        for cp in copies:
            cp.start()
        for cp in copies:
            cp.wait()

    vmem = pl.BlockSpec(memory_space=pltpu.VMEM)
    return pl.pallas_call(
        body, out_shape=jax.ShapeDtypeStruct((N_DEV, R, C), F32), in_specs=[vmem], out_specs=vmem,
        scratch_shapes=[pltpu.SemaphoreType.DMA((N_DEV - 1,)), pltpu.SemaphoreType.DMA((N_DEV - 1,))],
        name="small_grads_all_gather")(packed)


def _adamw(w, g, m, v):
    m = ADAM_B1 * m + (1.0 - ADAM_B1) * g
    v = ADAM_B2 * v + (1.0 - ADAM_B2) * (g * g)
    m_hat = m / (1.0 - ADAM_B1 ** ADAM_STEP)
    v_hat = v / (1.0 - ADAM_B2 ** ADAM_STEP)
    delta = -ADAM_LR * (m_hat / (jnp.sqrt(v_hat) + ADAM_EPS) + ADAM_WD * w)
    return delta, m, v


def _pair_add(name, grads, received, slots):
    _, R, C = grads.shape
    tr = _row_tile(R)

    def body(s_ref, g_ref, r_ref, o_ref):
        o_ref[...] = (g_ref[...].astype(F32) + r_ref[...].astype(F32)).astype(BF16)

    return pl.pallas_call(
        body, out_shape=jax.ShapeDtypeStruct((3, R, C), BF16),
        grid_spec=pltpu.PrefetchScalarGridSpec(
            num_scalar_prefetch=1, grid=(3, R // tr),
            in_specs=[pl.BlockSpec((None, tr, C), lambda k, i, s: (s[k], i, 0)), pl.BlockSpec((None, tr, C), lambda k, i, s: (s[3 + k], i, 0))],
            out_specs=pl.BlockSpec((None, tr, C), lambda k, i, s: (k, i, 0))),
        name=name, compiler_params=_params(2))(slots, grads, received)


def _relay_add(name, sums, relayed):
    _, R, C = sums.shape
    tr = _row_tile(R)

    def body(s_ref, r_ref, o_ref):
        o_ref[...] = (s_ref[...].astype(F32) + r_ref[...].astype(F32)).astype(BF16)

    blk = pl.BlockSpec((tr, C), lambda i: (i, 0))
    return pl.pallas_call(
        body, out_shape=jax.ShapeDtypeStruct((R, C), BF16), grid=(R // tr,),
        in_specs=[pl.BlockSpec((None, tr, C), lambda i: (2, i, 0)), blk], out_specs=blk,
        name=name, compiler_params=_params(1))(sums, relayed)


def _adam_shard(name, grads, from_sibling, received, w, m, v, own):
    R, C = w.shape
    tr = _row_tile(R, 256)
    tc = C if tr < R or C % (4 * LANES) else 4 * LANES

    def body(o_ref, g_ref, s_ref, ra_ref, rb_ref, w_ref, m_ref, v_ref, g_out, d_out, m_out, v_out):
        g = (g_ref[...].astype(F32) + s_ref[...].astype(F32)) + ra_ref[...].astype(F32) + rb_ref[...].astype(F32)
        delta, mn, vn = _adamw(w_ref[...], g, m_ref[...], v_ref[...])
        g_out[...] = g
        d_out[...] = delta
        m_out[...] = mn
        v_out[...] = vn

    blk = pl.BlockSpec((tr, tc), lambda i, j, o: (i, j))
    shape = jax.ShapeDtypeStruct((R, C), F32)
    return pl.pallas_call(
        body, out_shape=(shape,) * 4,
        grid_spec=pltpu.PrefetchScalarGridSpec(
            num_scalar_prefetch=1, grid=(R // tr, C // tc),
            in_specs=[pl.BlockSpec((None, tr, tc), lambda i, j, o: (o[0], i, j)), pl.BlockSpec((None, tr, tc), lambda i, j, o: (o[1], i, j)),
                      blk, blk, blk, blk, blk],
            out_specs=(blk,) * 4),
        name=name, compiler_params=_params(2))(own, grads, from_sibling, received[0], received[1], w, m, v)


def _adam_small(name, gathered, w, m, v):
    R, C = w.shape

    def body(ga_ref, w_ref, m_ref, v_ref, g_out, d_out, m_out, v_out):
        g = ga_ref[0]
        for d in range(1, N_DEV):
            g = g + ga_ref[d]
        delta, mn, vn = _adamw(w_ref[...], g, m_ref[...], v_ref[...])
        g_out[...] = g
        d_out[...] = delta
        m_out[...] = mn
        v_out[...] = vn

    full = pl.BlockSpec((R, C), lambda i: (0, 0))
    shape = jax.ShapeDtypeStruct((R, C), F32)
    return pl.pallas_call(
        body, out_shape=(shape,) * 4, grid=(1,),
        in_specs=[pl.BlockSpec((N_DEV, R, C), lambda i: (0, 0, 0)), full, full, full], out_specs=(full,) * 4,
        name=name, compiler_params=_params(1))(gathered, w, m, v)


def _pack_small(parts, D, scalar=None):
    g1, gmix, g2, gof, gos, bf, gqf, gkf, gqs, gks, sinks = [p.reshape(-1).astype(F32) for p in parts]
    row3 = jnp.concatenate([gof, gos])
    row4 = jnp.zeros((D,), F32)
    for slot, vec in enumerate((bf, gqf, gkf, gqs, gks, sinks)):
        row4 = lax.dynamic_update_slice(row4, vec, (slot * LANES,))
    zero = jnp.zeros((D,), F32)
    row5 = zero if scalar is None else lax.dynamic_update_slice(zero, jnp.reshape(scalar, (1,)).astype(F32), (0,))
    return jnp.stack([g1, gmix, g2, row3, row4, row5, zero, zero])


def _unpack_small(packed, D, H):
    Dh = D // 2
    row4 = packed[4]
    short = [row4[s * LANES:s * LANES + n] for s, n in enumerate((H, HEAD_DIM, HEAD_DIM, HEAD_DIM, HEAD_DIM, H))]
    vecs = [packed[0], packed[1], packed[2], packed[3, :Dh], packed[3, Dh:]] + short
    return [v[None, :] for v in vecs]


def kernel(x, positions, norm_ffn1_g, ffn1_w_gate, ffn1_w_up, ffn1_w_down, norm_mix_g, w_in, b_forget, fox_q_norm_g, fox_k_norm_g, swa_q_norm_g, swa_k_norm_g, swa_sinks, out_norm_fox_g, out_norm_swa_g, w_out, norm_ffn2_g, ffn2_w_gate, ffn2_w_up, ffn2_w_down, loss_target, m_norm_ffn1_g, m_ffn1_w_gate, m_ffn1_w_up, m_ffn1_w_down, m_norm_mix_g, m_w_in, m_b_forget, m_fox_q_norm_g, m_fox_k_norm_g, m_swa_q_norm_g, m_swa_k_norm_g, m_swa_sinks, m_out_norm_fox_g, m_out_norm_swa_g, m_w_out, m_norm_ffn2_g, m_ffn2_w_gate, m_ffn2_w_up, m_ffn2_w_down, v_norm_ffn1_g, v_ffn1_w_gate, v_ffn1_w_up, v_ffn1_w_down, v_norm_mix_g, v_w_in, v_b_forget, v_fox_q_norm_g, v_fox_k_norm_g, v_swa_q_norm_g, v_swa_k_norm_g, v_swa_sinks, v_out_norm_fox_g, v_out_norm_swa_g, v_w_out, v_norm_ffn2_g, v_ffn2_w_gate, v_ffn2_w_up, v_ffn2_w_down):
    xs = x[0]
    target = loss_target[0]
    T, D = xs.shape
    Dh = D // 2
    H = Dh // HEAD_DIM
    HP = H // 2
    KVW = (H // GQA_GROUP) * HEAD_DIM
    KVB = KVW // LANES
    MAIN = 4 * Dh + 2 * KVW
    F_OFF = 3 * Dh
    tm = min(ROW_TILE_CAP, T)
    tm2 = min(2 * ROW_TILE_CAP, T)
    tq = min(512, T)
    tk = min(512, T)
    nk = T // tk
    cx, cy, cc = _place()
    near = [2 * (cx ^ cc) + (cy ^ (1 - cc)), 2 * (1 - cx) + (1 - cy), 2 * (cx ^ (1 - cc)) + (cy ^ cc)]
    slots = jnp.stack([2 * q + cc for q in near] + near).astype(jnp.int32)
    own = jnp.stack([4 * cx + 2 * cy + cc, 2 * cx + cy]).astype(jnp.int32)

    tr = jnp.transpose
    big_w = [tr(ffn1_w_gate[0]), tr(ffn1_w_up[0]), ffn1_w_down[0], tr(w_in[0]), w_out[0], tr(ffn2_w_gate[0]), tr(ffn2_w_up[0]),
             ffn2_w_down[0]]
    big_m = [tr(m_ffn1_w_gate[0]), tr(m_ffn1_w_up[0]), m_ffn1_w_down[0], tr(m_w_in[0]), m_w_out[0], tr(m_ffn2_w_gate[0]),
             tr(m_ffn2_w_up[0]), m_ffn2_w_down[0]]
    big_v = [tr(v_ffn1_w_gate[0]), tr(v_ffn1_w_up[0]), v_ffn1_w_down[0], tr(v_w_in[0]), v_w_out[0], tr(v_ffn2_w_gate[0]),
             tr(v_ffn2_w_up[0]), v_ffn2_w_down[0]]
    transposed = {"ffn1_w_gate", "ffn1_w_up", "w_in", "ffn2_w_gate", "ffn2_w_up"}
    names = ["ffn1_w_gate", "ffn1_w_up", "ffn1_w_down", "w_in", "w_out", "ffn2_w_gate", "ffn2_w_up", "ffn2_w_down"]
    sh = dict(zip(names, [w.astype(BF16) for w in big_w]))
    lane = jnp.arange(LANES)
    inv_freq = ROPE_THETA ** (-(2.0 * (lane % (HEAD_DIM // 2))).astype(F32) / HEAD_DIM)
    ang = positions[0].astype(F32)[:, None] * inv_freq[None, :]
    cos_t = jnp.cos(ang)
    sin_t = jnp.where((lane & (HEAD_DIM // 2)) == 0, -1.0, 1.0)[None, :] * jnp.sin(ang)
    rope = (cos_t, sin_t)

    def pair_gain(g, blocks):
        return jnp.tile(jnp.concatenate([g[0], g[0]])[None, None, :], (blocks, 1, 1))

    n1, (wg1,) = _rmsnorm_fwd("ffn1_norm", xs, norm_ffn1_g, tm, carry=_gather_carry([sh["ffn1_w_gate"]]))
    a1, (wu1,) = _ffn_gate("ffn1_gate", n1, wg1, tm, carry=_gather_carry([sh["ffn1_w_up"]]))
    (b1, hm1), (wd1,) = _ffn_up_only("ffn1_up", n1, wu1, a1, tm, carry=_gather_carry([sh["ffn1_w_down"]]))
    h1, (win_g,) = _ffn_down("ffn1_down", hm1, wd1, xs, tm, D, carry=_gather_carry([sh["w_in"]]))
    n_in = win_g.shape[1]
    win_t = win_g.reshape(N_DEV * n_in, D)
    win_main = jnp.concatenate([win_t[:F_OFF], win_t[F_OFF + H:]], axis=0)
    win_f = jnp.pad(win_t[F_OFF:F_OFF + H], ((0, LANES - H), (0, 0)))

    u = _rmsnorm_fwd("mix_norm", h1, norm_mix_g, tm)
    proj, (wout_g,) = _mm("mix_proj", u, win_main, T, MAIN // 9, dims=NT, carry=_gather_carry([sh["w_out"]]))
    wout = wout_g.reshape(D, D)
    proj_f = _mm("mix_proj_forget", u, win_f, T, LANES, dims=NT)
    scale = HEAD_DIM ** -0.5
    fox_gains = jnp.concatenate([pair_gain(fox_q_norm_g, HP), pair_gain(fox_k_norm_g, HP)])
    qk_f = _headnorm_fwd_scaled("fox_qk_norm", proj, 0, 2 * HP, fox_gains, T, scale, HP)
    v_f = proj[:, 2 * Dh:3 * Dh].astype(BF16)
    c_t, sg_t = _forget_fwd("forget_gates", proj_f[:, :H].T, b_forget.reshape(H, 1))
    crow = c_t.reshape(H, nk, 1, tk)
    (o_fox, lse_fa, lse_fb), (wg2, wu2) = _fox_fwd("fox_attention", qk_f, v_f, crow, tq, tk,
                                                   carry=_gather_carry([sh["ffn2_w_gate"], sh["ffn2_w_up"]]))

    swa_q_gains = pair_gain(swa_q_norm_g, HP)
    swa_k_gains = pair_gain(swa_k_norm_g, KVB)
    q_s = _headnorm_fwd("swa_q_norm", proj, 3 * HP, HP, swa_q_gains, T, scale, rope=rope)
    k_d = _headnorm_fwd("swa_k_norm", proj, 4 * HP, KVB, swa_k_gains, T, 1.0, rope=rope, dup=True)
    v_s = proj[:, 4 * Dh + KVW:].astype(BF16).reshape(T, H // GQA_GROUP, 1, HEAD_DIM)
    v_d = jnp.broadcast_to(v_s, (T, H // GQA_GROUP, 2, HEAD_DIM)).reshape(T, 2 * KVW)
    sinks3 = swa_sinks.reshape(H, 1, 1)
    o_swa, lse_sa, lse_sb = _swa_fwd("swa_attention", q_s, k_d, v_d, sinks3)

    on = _outnorm_fwd("out_norm", o_fox, o_swa, out_norm_fox_g, out_norm_swa_g, tm)
    h2 = _mm("mix_out", on, wout, tm2, min(512, D), resid=h1)

    n2 = _rmsnorm_fwd("ffn2_norm", h2, norm_ffn2_g, tm)
    (a2, b2, hm2), (wd2,) = _ffn_up("ffn2_up", n2, wg2, wu2, tm2, carry=_gather_carry([sh["ffn2_w_down"]]))
    y = _ffn_down("ffn2_down", hm2, wd2, h2, tm, D)
    dy, dyh, sq = _loss_grad("loss_grad", y, target, tm)
    loss_part = 0.5 * sq[0, 0] / D

    J, Fs, _ = wg2.shape
    aspec = pl.BlockSpec((None, tm, Fs), lambda i, j: (j, i, 0))
    wspec = pl.BlockSpec((None, Fs, D), lambda i, j: (j, 0, 0))
    got = {}
    local = {}

    def pair_sums(keys, grads, received):
        for nm, g, r in zip(keys, grads, received):
            local[nm] = (g, r)
        return [_pair_add("sum_" + nm, g, r, slots) for nm, g, r in zip(keys, grads, received)]

    def relay_sums(keys, sums, hop1):
        out = []
        for i, (nm, s) in enumerate(zip(keys, sums)):
            got[nm] = [hop1[2 * i]]
            out.append(_relay_add("relay_" + nm, s, hop1[2 * i + 1]))
        return out

    def arrived(keys, hop2):
        for nm, blk in zip(keys, hop2):
            got[nm].append(blk)

    dwd2 = _wgrad_down("ffn2_wgrad_down", hm2, dyh, D)
    (da2, db2), (sib_d2,) = _ffn_bwd_mid("ffn2_bwd_mid", dyh, wd2, a2, b2, tm2, carry=_sibling_carry([dwd2]))
    (sum_wd2,) = pair_sums(names[7:8], [dwd2], [sib_d2])
    (dwg2, dwu2), hop1 = _wgrad_up("ffn2_wgrad_up", n2, da2, db2, min(1024, D), carry=_to_partner_carry([sum_wd2]))
    (t_wd2,) = relay_sums(names[7:8], [sum_wd2], hop1)
    aspec2 = pl.BlockSpec((None, tm2, Fs), lambda i, j: (j, i, 0))
    dn2, (via_wd2, *sib2) = _reduce_mm("ffn2_bwd_in", [(da2, aspec2, wg2, wspec), (db2, aspec2, wu2, wspec)], [], NN, T, D, tm2, J,
                                       carry=_join(_to_other_carry([t_wd2]), _sibling_carry([dwg2, dwu2])))
    arrived(names[7:8], [via_wd2])
    dh2, dg_ffn2, dh2b = _rmsnorm_bwd("ffn2_norm_bwd", dn2, h2, norm_ffn2_g, dy, min(256, T), 1.0)
    sum_wg2, sum_wu2 = pair_sums(names[5:7], [dwg2, dwu2], sib2)

    dwout = _wgrad_2d("mix_out_wgrad", on, dh2b, min(512, D), D)
    dwout_g = dwout.reshape(N_DEV, D // N_DEV, D)
    do_fox, dg_of = _outnorm_bwd("out_norm_bwd_fox", dh2b, wout, 0, o_fox, out_norm_fox_g, tm)
    do_swa, dg_os = _outnorm_bwd("out_norm_bwd_swa", dh2b, wout, 1, o_swa, out_norm_swa_g, tm)

    (dq_f, dk_f, dv_f, dc_a, dc_b, dr_a, dr_b), (*hop1, sib_wout) = _fox_bwd(
        "fox_attention_bwd", qk_f, v_f, o_fox, do_fox, crow, lse_fa, lse_fb, tq, tk,
        carry=_join(_to_partner_carry([sum_wg2, sum_wu2]), _sibling_carry([dwout_g])))
    t_wg2, t_wu2 = relay_sums(names[5:7], [sum_wg2, sum_wu2], hop1)
    (sum_wout,) = pair_sums(names[4:5], [dwout_g], [sib_wout])
    dqf_raw, dg_fq = _headnorm_bwd("fox_q_norm_bwd", dq_f, proj, 0, HP, fox_gains[:HP], HP, T, 1.0)
    dkf_raw, dg_fk = _headnorm_bwd("fox_k_norm_bwd", dk_f, proj, HP, HP, fox_gains[HP:], HP, T, 1.0)
    dct = jnp.stack([dc_a.reshape(HP, T), dc_b.reshape(HP, T)], axis=1).reshape(H, T)
    drt = jnp.stack([dr_a.reshape(HP, T), dr_b.reshape(HP, T)], axis=1).reshape(H, T)
    dz_t, db_f = _forget_bwd("forget_gates_bwd", dct, drt, sg_t)

    (dq_s, dk_p, dv_p, dsink_a, dsink_b), hop2 = _swa_bwd(
        "swa_attention_bwd", q_s, k_d, v_d, sinks3, o_swa, do_swa, lse_sa, lse_sb, carry=_to_other_carry([t_wg2, t_wu2]))
    arrived(names[5:7], hop2)
    dqs_raw, dg_sq = _headnorm_bwd("swa_q_norm_bwd", dq_s, proj, 3 * HP, HP, swa_q_gains, HP, T, 1.0, rope=rope)
    dks_raw, dg_sk = _headnorm_bwd("swa_k_norm_bwd", dk_p, proj, 4 * HP, KVB, swa_k_gains, KVB, T, 1.0, rope=rope, fold=True)
    dvs_raw, _ = _headnorm_bwd("swa_v_fold", dv_p, None, 0, KVB, None, KVB, T, 1.0, fold=True, norm=False)

    dproj = jnp.concatenate([dqf_raw, dkf_raw, dv_f.astype(BF16), dqs_raw, dks_raw, dvs_raw], axis=1)
    dproj_f = jnp.pad(dz_t.T, ((0, 0), (0, LANES - H))).astype(BF16)
    dwin_main, hop1 = _wgrad_2d("mix_proj_wgrad", dproj, u, MAIN // 9, D, carry=_to_partner_carry([sum_wout]))
    (t_wout,) = relay_sums(names[4:5], [sum_wout], hop1)
    dwin_f = _wgrad_2d("mix_proj_forget_wgrad", dproj_f, u, LANES, min(1024, D))
    dwin_t = jnp.concatenate([dwin_main[:F_OFF], dwin_f[:H], dwin_main[F_OFF:]], axis=0)
    dwin_g = dwin_t.reshape(N_DEV, n_in, D)
    tkb = MAIN // 9
    du, (via_wout, sib_win) = _reduce_mm(
        "mix_bwd_in",
        [(dproj, pl.BlockSpec((tm2, tkb), lambda i, r: (i, r)), win_main, pl.BlockSpec((tkb, D), lambda i, r: (r, 0)))],
        [(dproj_f, pl.BlockSpec((tm2, LANES), lambda i, r: (i, 0)), win_f, pl.BlockSpec((LANES, D), lambda i, r: (0, 0)))],
        NN, T, D, tm2, 9, carry=_join(_to_other_carry([t_wout]), _sibling_carry([dwin_g])))
    arrived(names[4:5], [via_wout])
    dh1, dg_mix, dh1h = _rmsnorm_bwd("mix_norm_bwd", du, h1, norm_mix_g, dh2, min(256, T), 0.5)
    (sum_win,) = pair_sums(names[3:4], [dwin_g], [sib_win])

    dwd1, hop1 = _wgrad_down("ffn1_wgrad_down", hm1, dh1h, D, carry=_to_partner_carry([sum_win]))
    (t_win,) = relay_sums(names[3:4], [sum_win], hop1)
    (da1, db1), (via_win, sib_d) = _ffn_bwd_mid("ffn1_bwd_mid", dh1h, wd1, a1, b1, tm2,
                                                carry=_join(_to_other_carry([t_win]), _sibling_carry([dwd1])))
    arrived(names[3:4], [via_win])
    (sum_wd1,) = pair_sums(names[2:3], [dwd1], [sib_d])
    dwg1, hop1 = _wgrad_down("ffn1_wgrad_gate", da1, n1, D, carry=_to_partner_carry([sum_wd1]))
    (t_wd1,) = relay_sums(names[2:3], [sum_wd1], hop1)
    dwu1, (via_wd1, sib_g) = _wgrad_down("ffn1_wgrad_up", db1, n1, D,
                                         carry=_join(_to_other_carry([t_wd1]), _sibling_carry([dwg1])))
    arrived(names[2:3], [via_wd1])
    (sum_wg1,) = pair_sums(names[0:1], [dwg1], [sib_g])
    dn1_gate, (*hop1, sib_u) = _reduce_mm(
        "ffn1_bwd_in_gate", [(da1, aspec, wg1, wspec)], [], NN, T, D, tm, J,
        carry=_join(_to_partner_carry([sum_wg1]), _sibling_carry([dwu1])))
    (t_wg1,) = relay_sums(names[0:1], [sum_wg1], hop1)
    (sum_wu1,) = pair_sums(names[1:2], [dwu1], [sib_u])
    dn1, (via_wg1, *hop1) = _reduce_mm(
        "ffn1_bwd_in_up", [(db1, aspec, wu1, wspec)], [], NN, T, D, tm, J, init=dn1_gate,
        carry=_join(_to_other_carry([t_wg1]), _to_partner_carry([sum_wu1])))
    arrived(names[0:1], [via_wg1])
    (t_wu1,) = relay_sums(names[1:2], [sum_wu1], hop1)
    arrived(names[1:2], _run_carry("grads_exchange", _to_other_carry([t_wu1])))
    dx, dg_ffn1 = _rmsnorm_bwd("ffn1_norm_bwd", dn1, xs, norm_ffn1_g, dh1, min(256, T), None)

    big_out = [_adam_shard("adam_" + nm, local[nm][0], local[nm][1], got[nm], w, m, v, own)
               for nm, w, m, v in zip(names, big_w, big_m, big_v)]

    dsinks = jnp.stack([dsink_a.reshape(HP), dsink_b.reshape(HP)], axis=1).reshape(H)
    small_g = [dg_ffn1, dg_mix, dg_ffn2, dg_of, dg_os, db_f, dg_fq[0, 0, :HEAD_DIM], dg_fk[0, 0, :HEAD_DIM],
               dg_sq[0, 0, :HEAD_DIM], dg_sk[0, 0, :HEAD_DIM], dsinks]
    small_w = [norm_ffn1_g, norm_mix_g, norm_ffn2_g, out_norm_fox_g, out_norm_swa_g, b_forget, fox_q_norm_g, fox_k_norm_g,
               swa_q_norm_g, swa_k_norm_g, swa_sinks]
    small_m = [m_norm_ffn1_g, m_norm_mix_g, m_norm_ffn2_g, m_out_norm_fox_g, m_out_norm_swa_g, m_b_forget, m_fox_q_norm_g,
               m_fox_k_norm_g, m_swa_q_norm_g, m_swa_k_norm_g, m_swa_sinks]
    small_v = [v_norm_ffn1_g, v_norm_mix_g, v_norm_ffn2_g, v_out_norm_fox_g, v_out_norm_swa_g, v_b_forget, v_fox_q_norm_g,
               v_fox_k_norm_g, v_swa_q_norm_g, v_swa_k_norm_g, v_swa_sinks]
    gathered = _gather_small(_pack_small(small_g, D, loss_part))
    small_out = _adam_small("adam_small", gathered, _pack_small(small_w, D), _pack_small(small_m, D), _pack_small(small_v, D))
    loss = small_out[0][5, 0]
    small_out = [_unpack_small(p, D, H) for p in small_out]

    order = ["norm_ffn1_g", "ffn1_w_gate", "ffn1_w_up", "ffn1_w_down", "norm_mix_g", "w_in", "b_forget", "fox_q_norm_g", "fox_k_norm_g",
             "swa_q_norm_g", "swa_k_norm_g", "swa_sinks", "out_norm_fox_g", "out_norm_swa_g", "w_out", "norm_ffn2_g",
             "ffn2_w_gate", "ffn2_w_up", "ffn2_w_down"]
    small_names = ["norm_ffn1_g", "norm_mix_g", "norm_ffn2_g", "out_norm_fox_g", "out_norm_swa_g", "b_forget", "fox_q_norm_g",
                   "fox_k_norm_g", "swa_q_norm_g", "swa_k_norm_g", "swa_sinks"]
    result = [loss, dx[None]]
    for kind in range(4):
        for nm in order:
            if nm in names:
                leaf = big_out[names.index(nm)][kind]
                result.append((tr(leaf) if nm in transposed else leaf)[None])
            else:
                result.append(small_out[kind][small_names.index(nm)])
    return tuple(result)


def _headnorm_fwd_scaled(name, proj, col_off, ncb, gains, tm, scale, n_scaled):
    T = proj.shape[0]

    def body(x_ref, g_ref, o_ref):
        xv = x_ref[...]
        lo = _lane_lo(xv.shape)
        y = xv * _head_rstd(xv, lo) * g_ref[...]
        y = y * jnp.where(pl.program_id(0) < n_scaled, scale, 1.0)
        o_ref[...] = y.astype(BF16)

    return pl.pallas_call(
        body, out_shape=jax.ShapeDtypeStruct((T, ncb * LANES), BF16), grid=(ncb, T // tm),
        in_specs=[pl.BlockSpec((tm, LANES), lambda c, i: (i, col_off + c)), pl.BlockSpec((None, 1, LANES), lambda c, i: (c, 0, 0))],
        out_specs=pl.BlockSpec((tm, LANES), lambda c, i: (i, c)), name=name, compiler_params=_params(2))(proj, gains)
```

```python
import functools

import jax
import jax.numpy as jnp
from jax import lax
from jax.experimental import pallas as pl
from jax.experimental.pallas import tpu as pltpu

F32 = jnp.float32
BF16 = jnp.bfloat16

HEAD_DIM = 64
LANES = 128
WINDOW = 128
GQA_GROUP = 4
EPS = 1e-6
ROPE_THETA = 10000.0
ADAM_LR = 0.001
ADAM_B1 = 0.9
ADAM_B2 = 0.999
ADAM_EPS = 1e-08
ADAM_WD = 0.01
ADAM_STEP = 10
N_DEV = 8
NEG = -1e30
VMEM_LIMIT_V7X = 48 * 1024 * 1024
ROW_TILE_CAP = 512
MESH = pl.DeviceIdType.MESH

NN = (((1,), (0,)), ((), ()))
NT = (((1,), (1,)), ((), ()))
TN = (((0,), (0,)), ((), ()))


def _dot(a, b, dims):
    return lax.dot_general(a, b, dims, preferred_element_type=F32)


def _params(n_axes):
    return pltpu.CompilerParams(dimension_semantics=("arbitrary",) * n_axes, vmem_limit_bytes=VMEM_LIMIT_V7X)


def _row_tile(rows, cap=ROW_TILE_CAP):
    best = None
    for t in range(16, min(rows, cap) + 1, 16):
        if rows % t == 0:
            best = t
    return best or rows


def _lane_lo(shape):
    return lax.broadcasted_iota(jnp.int32, shape, len(shape) - 1) < HEAD_DIM


def _keep(sel, x):
    return jnp.where(sel, x.astype(F32), 0.0).astype(BF16)


_HBM = pl.BlockSpec(memory_space=pltpu.HBM)


class _Carry:
    def __init__(self, inputs, out_shapes, scratch, start, finish, middle=None):
        self.inputs, self.out_shapes, self.scratch = list(inputs), list(out_shapes), list(scratch)
        self.start, self.finish, self.middle = start, finish, middle or (lambda ins, outs, scr: None)


def _join(*carries):
    def hook(which):
        def run(ins, outs, scr):
            i = o = s = 0
            for c in carries:
                ni, no, ns = len(c.inputs), len(c.out_shapes), len(c.scratch)
                getattr(c, which)(ins[i:i + ni], outs[o:o + no], scr[s:s + ns])
                i, o, s = i + ni, o + no, s + ns
        return run

    return _Carry([a for c in carries for a in c.inputs], [a for c in carries for a in c.out_shapes],
                  [a for c in carries for a in c.scratch], hook("start"), hook("finish"), hook("middle"))


def _call(body, *, name, grid, in_specs, out_specs, out_shape, args, scratch_shapes=(), carry=None):
    params = _params(len(grid))
    if carry is None:
        return pl.pallas_call(body, out_shape=out_shape, grid=grid, in_specs=list(in_specs), out_specs=out_specs,
                              scratch_shapes=list(scratch_shapes), name=name, compiler_params=params)(*args)
    single = not isinstance(out_shape, (tuple, list))
    shapes = (out_shape,) if single else tuple(out_shape)
    specs = (out_specs,) if single else tuple(out_specs)
    n_in, n_out, n_scr = len(args), len(shapes), len(scratch_shapes)
    c_in, c_out = len(carry.inputs), len(carry.out_shapes)

    def wrapped(*refs):
        ins, c_ins = refs[:n_in], refs[n_in:n_in + c_in]
        o0 = n_in + c_in
        outs, c_outs = refs[o0:o0 + n_out], refs[o0 + n_out:o0 + n_out + c_out]
        s0 = o0 + n_out + c_out
        scr, c_scr = refs[s0:s0 + n_scr], refs[s0 + n_scr:]
        step, total = pl.program_id(0), grid[0]
        for ax in range(1, len(grid)):
            step, total = step * grid[ax] + pl.program_id(ax), total * grid[ax]

        @pl.when(step == 0)
        def _():
            carry.start(c_ins, c_outs, c_scr)

        @pl.when(step == total // 2)
        def _():
            carry.middle(c_ins, c_outs, c_scr)

        body(*ins, *outs, *scr)

        @pl.when(step == total - 1)
        def _():
            carry.finish(c_ins, c_outs, c_scr)

    res = pl.pallas_call(
        wrapped, out_shape=shapes + tuple(carry.out_shapes), grid=grid, in_specs=list(in_specs) + [_HBM] * c_in,
        out_specs=specs + (_HBM,) * c_out, scratch_shapes=list(scratch_shapes) + carry.scratch, name=name,
        compiler_params=params)(*args, *carry.inputs)
    main = res[:n_out]
    return (main[0] if single else tuple(main)), tuple(res[n_out:])


def _rms_bwd(dn, x, g):
    r = lax.rsqrt(jnp.mean(x * x, axis=-1, keepdims=True) + EPS)
    xh = x * r
    dxh = dn * g
    dx = r * (dxh - xh * jnp.mean(dxh * xh, axis=-1, keepdims=True))
    return dx, jnp.sum(dn * xh, axis=0, keepdims=True)


def _rmsnorm_fwd(name, x, g, tm, carry=None):
    T, D = x.shape

    def body(x_ref, g_ref, o_ref):
        xf = x_ref[...]
        r = lax.rsqrt(jnp.mean(xf * xf, axis=-1, keepdims=True) + EPS)
        o_ref[...] = (xf * r * g_ref[...]).astype(BF16)

    return _call(
        body, name=name, grid=(T // tm,), out_shape=jax.ShapeDtypeStruct((T, D), BF16),
        in_specs=[pl.BlockSpec((tm, D), lambda i: (i, 0)), pl.BlockSpec((1, D), lambda i: (0, 0))],
        out_specs=pl.BlockSpec((tm, D), lambda i: (i, 0)), args=[x, g], carry=carry)


def _outnorm_fwd(name, o_fox, o_swa, g_fox, g_swa, tm):
    T, Dh = o_fox.shape

    def body(a_ref, b_ref, ga_ref, gb_ref, o_ref):
        for ref, g_ref, lo in ((a_ref, ga_ref, 0), (b_ref, gb_ref, Dh)):
            xf = ref[...]
            r = lax.rsqrt(jnp.mean(xf * xf, axis=-1, keepdims=True) + EPS)
            o_ref[:, lo:lo + Dh] = (xf * r * g_ref[...]).astype(BF16)

    row = pl.BlockSpec((tm, Dh), lambda i: (i, 0))
    gain = pl.BlockSpec((1, Dh), lambda i: (0, 0))
    return pl.pallas_call(
        body, out_shape=jax.ShapeDtypeStruct((T, 2 * Dh), BF16), grid=(T // tm,),
        in_specs=[row, row, gain, gain], out_specs=pl.BlockSpec((tm, 2 * Dh), lambda i: (i, 0)),
        name=name, compiler_params=_params(1))(o_fox, o_swa, g_fox, g_swa)


def _outnorm_bwd(name, dhb, wout, half, o, g, tm):
    T, D = dhb.shape
    Dh = o.shape[1]

    def body(a_ref, w_ref, o_ref, g_ref, do_ref, dg_ref):
        don = _dot(a_ref[...], w_ref[...], NT)
        dx, dg = _rms_bwd(don, o_ref[...], g_ref[...])
        do_ref[...] = dx.astype(BF16)

        @pl.when(pl.program_id(0) == 0)
        def _():
            dg_ref[...] = dg

        @pl.when(pl.program_id(0) > 0)
        def _():
            dg_ref[...] += dg

    return pl.pallas_call(
        body, out_shape=(jax.ShapeDtypeStruct((T, Dh), BF16), jax.ShapeDtypeStruct((1, Dh), F32)), grid=(T // tm,),
        in_specs=[pl.BlockSpec((tm, D), lambda i: (i, 0)), pl.BlockSpec((Dh, D), lambda i: (half, 0)),
                  pl.BlockSpec((tm, Dh), lambda i: (i, 0)), pl.BlockSpec((1, Dh), lambda i: (0, 0))],
        out_specs=(pl.BlockSpec((tm, Dh), lambda i: (i, 0)), pl.BlockSpec((1, Dh), lambda i: (0, 0))),
        name=name, compiler_params=_params(1))(dhb, wout, o, g)


def _mm(name, a, b, tm, tn, dims=NN, resid=None, carry=None):
    M, K = a.shape
    transposed = dims == NT
    N = b.shape[0] if transposed else b.shape[1]

    def body(*refs):
        if resid is None:
            a_ref, b_ref, o_ref = refs
            o_ref[...] = _dot(a_ref[...], b_ref[...], dims)
        else:
            a_ref, b_ref, r_ref, o_ref = refs
            o_ref[...] = r_ref[...] + _dot(a_ref[...], b_ref[...], dims)

    ospec = pl.BlockSpec((tm, tn), lambda n, i: (i, n))
    bspec = pl.BlockSpec((tn, K), lambda n, i: (n, 0)) if transposed else pl.BlockSpec((K, tn), lambda n, i: (0, n))
    in_specs = [pl.BlockSpec((tm, K), lambda n, i: (i, 0)), bspec]
    args = [a, b]
    if resid is not None:
        in_specs.append(ospec)
        args.append(resid)
    return _call(body, name=name, grid=(N // tn, M // tm), in_specs=in_specs, out_specs=ospec,
                 out_shape=jax.ShapeDtypeStruct((M, N), F32), args=args, carry=carry)


def _wgrad_2d(name, a, b, tmm, tn, carry=None):
    T, M = a.shape
    N = b.shape[1]

    def body(a_ref, b_ref, o_ref):
        o_ref[...] = _dot(a_ref[...], b_ref[...], TN).astype(BF16)

    return _call(
        body, name=name, grid=(M // tmm, N // tn), out_shape=jax.ShapeDtypeStruct((M, N), BF16),
        in_specs=[pl.BlockSpec((T, tmm), lambda m, n: (0, m)), pl.BlockSpec((T, tn), lambda m, n: (0, n))],
        out_specs=pl.BlockSpec((tmm, tn), lambda m, n: (m, n)), args=[a, b], carry=carry)


def _wgrad_down(name, hm, df, tn, carry=None):
    J, T, Fs = hm.shape
    D = df.shape[1]

    def body(a_ref, b_ref, o_ref):
        o_ref[...] = _dot(a_ref[...], b_ref[...], TN).astype(BF16)

    return _call(
        body, name=name, grid=(J, D // tn), out_shape=jax.ShapeDtypeStruct((J, Fs, D), BF16),
        in_specs=[pl.BlockSpec((None, T, Fs), lambda j, n: (j, 0, 0)), pl.BlockSpec((T, tn), lambda j, n: (0, n))],
        out_specs=pl.BlockSpec((None, Fs, tn), lambda j, n: (j, 0, n)), args=[hm, df], carry=carry)


def _wgrad_up(name, n, da, db, tn, carry=None):
    T, D = n.shape
    J, _, Fs = da.shape

    def body(n_ref, da_ref, db_ref, og_ref, ou_ref):
        nv = n_ref[...]
        og_ref[...] = _dot(da_ref[...], nv, TN).astype(BF16)
        ou_ref[...] = _dot(db_ref[...], nv, TN).astype(BF16)

    act = pl.BlockSpec((None, T, Fs), lambda j, m: (j, 0, 0))
    out = pl.BlockSpec((None, Fs, tn), lambda j, m: (j, 0, m))
    shape = jax.ShapeDtypeStruct((J, Fs, D), BF16)
    return _call(
        body, name=name, grid=(J, D // tn), out_shape=(shape, shape),
        in_specs=[pl.BlockSpec((T, tn), lambda j, m: (0, m)), act, act], out_specs=(out, out),
        args=[n, da, db], carry=carry)


def _reduce_mm(name, pairs, once, dims, T, D, tm, steps, init=None, carry=None):
    n_pairs = len(pairs)
    n_once = len(once)
    n_mm = 2 * (n_pairs + n_once)

    def body(*refs):
        pr = refs[:2 * n_pairs]
        on = refs[2 * n_pairs:n_mm]
        o_ref = refs[-1]
        r = pl.program_id(1)

        @pl.when(r == 0)
        def _():
            o_ref[...] = jnp.zeros(o_ref.shape, F32) if init is None else refs[n_mm][...]

        for p in range(n_pairs):
            o_ref[...] += _dot(pr[2 * p][...], pr[2 * p + 1][...], dims)

        if n_once:
            @pl.when(r == steps - 1)
            def _():
                for p in range(n_once):
                    o_ref[...] += _dot(on[2 * p][...], on[2 * p + 1][...], dims)

    in_specs, args = [], []
    for a, a_spec, w, w_spec in list(pairs) + list(once):
        in_specs += [a_spec, w_spec]
        args += [a, w]
    row = pl.BlockSpec((tm, D), lambda i, r: (i, 0))
    if init is not None:
        in_specs.append(row)
        args.append(init)
    return _call(body, name=name, grid=(T // tm, steps), in_specs=in_specs, out_specs=row, out_shape=jax.ShapeDtypeStruct((T, D), F32),
                 args=args, carry=carry)


def _rmsnorm_bwd(name, dn, x, g, dh, tm, bf16_scale, carry=None):
    T, D = x.shape
    emit_bf16 = bf16_scale is not None

    def body(dn_ref, x_ref, g_ref, dh_ref, *outs):
        dxn, dg = _rms_bwd(dn_ref[...], x_ref[...], g_ref[...])
        dx = dh_ref[...] + dxn
        outs[0][...] = dx
        if emit_bf16:
            outs[2][...] = (bf16_scale * dx).astype(BF16)

        @pl.when(pl.program_id(0) == 0)
        def _():
            outs[1][...] = dg

        @pl.when(pl.program_id(0) > 0)
        def _():
            outs[1][...] += dg

    row = pl.BlockSpec((tm, D), lambda i: (i, 0))
    gain = pl.BlockSpec((1, D), lambda i: (0, 0))
    out_shape = [jax.ShapeDtypeStruct((T, D), F32), jax.ShapeDtypeStruct((1, D), F32)]
    out_specs = [row, gain]
    if emit_bf16:
        out_shape.append(jax.ShapeDtypeStruct((T, D), BF16))
        out_specs.append(row)
    return _call(body, name=name, grid=(T // tm,), in_specs=[row, row, gain, row], out_specs=tuple(out_specs),
                 out_shape=tuple(out_shape), args=[dn, x, g, dh], carry=carry)


def _loss_grad(name, y, target, tm):
    T, D = y.shape

    def body(y_ref, t_ref, dy_ref, dyh_ref, sq_ref):
        diff = y_ref[...] - t_ref[...]
        sq = jnp.sum(jnp.sum(diff * diff, axis=1, keepdims=True), axis=0, keepdims=True)
        dy = diff * (1.0 / D)
        dy_ref[...] = dy
        dyh_ref[...] = (0.5 * dy).astype(BF16)

        @pl.when(pl.program_id(0) == 0)
        def _():
            sq_ref[...] = sq

        @pl.when(pl.program_id(0) > 0)
        def _():
            sq_ref[...] += sq

    row = pl.BlockSpec((tm, D), lambda i: (i, 0))
    return pl.pallas_call(
        body, out_shape=(jax.ShapeDtypeStruct((T, D), F32), jax.ShapeDtypeStruct((T, D), BF16), jax.ShapeDtypeStruct((1, 1), F32)),
        grid=(T // tm,), in_specs=[row, row], out_specs=(row, row, pl.BlockSpec((1, 1), lambda i: (0, 0))),
        name=name, compiler_params=_params(1))(y, target)


def _ffn_up(name, n, wg, wu, tm, carry=None):
    T, D = n.shape
    J, Fs, _ = wg.shape

    def body(n_ref, wg_ref, wu_ref, a_ref, b_ref, h_ref):
        xv = n_ref[...]
        a = _dot(xv, wg_ref[...], NT)
        b = _dot(xv, wu_ref[...], NT)
        a_ref[...] = a.astype(BF16)
        b_ref[...] = b.astype(BF16)
        h_ref[...] = (a * jax.nn.sigmoid(a) * b).astype(BF16)

    act = jax.ShapeDtypeStruct((J, T, Fs), BF16)
    wspec = pl.BlockSpec((None, Fs, D), lambda j, i: (j, 0, 0))
    aspec = pl.BlockSpec((None, tm, Fs), lambda j, i: (j, i, 0))
    return _call(
        body, name=name, grid=(J, T // tm), out_shape=(act, act, act),
        in_specs=[pl.BlockSpec((tm, D), lambda j, i: (i, 0)), wspec, wspec], out_specs=(aspec, aspec, aspec),
        args=[n, wg, wu], carry=carry)


def _ffn_gate(name, n, wg, tm, carry=None):
    T, D = n.shape
    J, Fs, _ = wg.shape

    def body(n_ref, wg_ref, a_ref):
        a_ref[...] = _dot(n_ref[...], wg_ref[...], NT).astype(BF16)

    aspec = pl.BlockSpec((None, tm, Fs), lambda j, i: (j, i, 0))
    return _call(
        body, name=name, grid=(J, T // tm), out_shape=jax.ShapeDtypeStruct((J, T, Fs), BF16),
        in_specs=[pl.BlockSpec((tm, D), lambda j, i: (i, 0)), pl.BlockSpec((None, Fs, D), lambda j, i: (j, 0, 0))],
        out_specs=aspec, args=[n, wg], carry=carry)


def _ffn_up_only(name, n, wu, a, tm, carry=None):
    T, D = n.shape
    J, Fs, _ = wu.shape

    def body(n_ref, wu_ref, a_ref, b_ref, h_ref):
        b = _dot(n_ref[...], wu_ref[...], NT)
        a = a_ref[...].astype(F32)
        b_ref[...] = b.astype(BF16)
        h_ref[...] = (a * jax.nn.sigmoid(a) * b).astype(BF16)

    act = jax.ShapeDtypeStruct((J, T, Fs), BF16)
    aspec = pl.BlockSpec((None, tm, Fs), lambda j, i: (j, i, 0))
    return _call(
        body, name=name, grid=(J, T // tm), out_shape=(act, act),
        in_specs=[pl.BlockSpec((tm, D), lambda j, i: (i, 0)), pl.BlockSpec((None, Fs, D), lambda j, i: (j, 0, 0)), aspec],
        out_specs=(aspec, aspec), args=[n, wu, a], carry=carry)


def _ffn_down(name, hm, wd, resid, tm, tn, carry=None):
    J, T, Fs = hm.shape
    D = wd.shape[2]

    def body(h_ref, w_ref, r_ref, o_ref):
        @pl.when(pl.program_id(2) == 0)
        def _():
            o_ref[...] = r_ref[...]

        o_ref[...] += _dot(h_ref[...] * 0.5, w_ref[...], NN)

    tile = pl.BlockSpec((tm, tn), lambda i, n, j: (i, n))
    return _call(
        body, name=name, grid=(T // tm, D // tn, J), out_shape=jax.ShapeDtypeStruct((T, D), F32),
        in_specs=[pl.BlockSpec((None, tm, Fs), lambda i, n, j: (j, i, 0)), pl.BlockSpec((None, Fs, tn), lambda i, n, j: (j, 0, n)), tile],
        out_specs=tile, args=[hm, wd, resid], carry=carry)


def _ffn_bwd_mid(name, dfh, wd, a, b, tm, carry=None):
    T, D = dfh.shape
    J, Fs, _ = wd.shape

    def body(df_ref, w_ref, a_ref, b_ref, da_ref, db_ref):
        dhm = _dot(df_ref[...], w_ref[...], NT)
        av = a_ref[...].astype(F32)
        bv = b_ref[...].astype(F32)
        sg = jax.nn.sigmoid(av)
        da_ref[...] = (dhm * bv * (sg * (1.0 + av * (1.0 - sg)))).astype(BF16)
        db_ref[...] = (dhm * (av * sg)).astype(BF16)

    act = jax.ShapeDtypeStruct((J, T, Fs), BF16)
    aspec = pl.BlockSpec((None, tm, Fs), lambda j, i: (j, i, 0))
    return _call(
        body, name=name, grid=(J, T // tm), out_shape=(act, act),
        in_specs=[pl.BlockSpec((tm, D), lambda j, i: (i, 0)), pl.BlockSpec((None, Fs, D), lambda j, i: (j, 0, 0)), aspec, aspec],
        out_specs=(aspec, aspec), args=[dfh, wd, a, b], carry=carry)


def _rot_half(y, lane):
    first = (lane & (HEAD_DIM // 2)) == 0
    return jnp.where(first, pltpu.roll(y, LANES - HEAD_DIM // 2, 1), pltpu.roll(y, HEAD_DIM // 2, 1))


def _head_rstd(x, lo):
    sq = x * x
    ss_a = jnp.sum(jnp.where(lo, sq, 0.0), axis=-1, keepdims=True)
    ss_b = jnp.sum(jnp.where(lo, 0.0, sq), axis=-1, keepdims=True)
    return lax.rsqrt(jnp.where(lo, ss_a, ss_b) * (1.0 / HEAD_DIM) + EPS)


def _headnorm_fwd(name, proj, col_off, ncb, gains, tm, scale, rope=None, dup=False):
    T = proj.shape[0]
    with_rope = rope is not None
    width = 2 * LANES if dup else LANES

    def body(*refs):
        if with_rope:
            x_ref, g_ref, cos_ref, sin_ref, o_ref = refs
        else:
            x_ref, g_ref, o_ref = refs
        xv = x_ref[...]
        lane = lax.broadcasted_iota(jnp.int32, xv.shape, 1)
        lo = lane < HEAD_DIM
        y = xv * _head_rstd(xv, lo) * g_ref[...]
        if with_rope:
            y = y * cos_ref[...] + _rot_half(y, lane) * sin_ref[...]
        y = y * scale
        if dup:
            sw = pltpu.roll(y, HEAD_DIM, 1)
            o_ref[:, :LANES] = jnp.where(lo, y, sw).astype(BF16)
            o_ref[:, LANES:] = jnp.where(lo, sw, y).astype(BF16)
        else:
            o_ref[...] = y.astype(BF16)

    in_specs = [pl.BlockSpec((tm, LANES), lambda c, i: (i, col_off + c)), pl.BlockSpec((None, 1, LANES), lambda c, i: (c, 0, 0))]
    args = [proj, gains]
    if with_rope:
        tab = pl.BlockSpec((tm, LANES), lambda c, i: (i, 0))
        in_specs += [tab, tab]
        args += list(rope)
    return pl.pallas_call(
        body, out_shape=jax.ShapeDtypeStruct((T, ncb * width), BF16), grid=(ncb, T // tm),
        in_specs=in_specs, out_specs=pl.BlockSpec((tm, width), lambda c, i: (i, c)),
        name=name, compiler_params=_params(2))(*args)


def _headnorm_bwd(name, dy, proj, col_off, ncb, gains, group, tm, scale, rope=None, fold=False, norm=True):
    T = dy.shape[0]
    with_rope = rope is not None
    n_groups = ncb // group
    dy_width = 4 * LANES if fold else LANES

    def body(*refs):
        refs = list(refs)
        dy_ref = refs.pop(0)
        x_ref = refs.pop(0) if norm else None
        g_ref = refs.pop(0) if norm else None
        cos_ref = refs.pop(0) if with_rope else None
        sin_ref = refs.pop(0) if with_rope else None
        dx_ref = refs.pop(0)
        dg_ref = refs.pop(0) if norm else None
        c = pl.program_id(0)
        i = pl.program_id(1)
        d = dy_ref[...]
        lane = lax.broadcasted_iota(jnp.int32, (d.shape[0], LANES), 1)
        lo = lane < HEAD_DIM
        if fold:
            t0 = d[:, 0:LANES] + d[:, LANES:2 * LANES]
            t1 = d[:, 2 * LANES:3 * LANES] + d[:, 3 * LANES:4 * LANES]
            d = jnp.where(lo, t0 + pltpu.roll(t0, HEAD_DIM, 1), t1 + pltpu.roll(t1, HEAD_DIM, 1))
        d = d * scale
        if with_rope:
            d = d * cos_ref[...] + _rot_half(d * sin_ref[...], lane)
        if not norm:
            dx_ref[...] = d.astype(BF16)
            return
        xv = x_ref[...]
        gv = g_ref[...]
        r = _head_rstd(xv, lo)
        xh = xv * r
        dxh = d * gv
        pr = dxh * xh
        m_a = jnp.sum(jnp.where(lo, pr, 0.0), axis=-1, keepdims=True)
        m_b = jnp.sum(jnp.where(lo, 0.0, pr), axis=-1, keepdims=True)
        mean = jnp.where(lo, m_a, m_b) * (1.0 / HEAD_DIM)
        dx_ref[...] = (r * (dxh - xh * mean)).astype(BF16)
        dgp = jnp.sum(d * xh, axis=0, keepdims=True)
        dgp = dgp + pltpu.roll(dgp, HEAD_DIM, 1)
        first = jnp.logical_and(c % group == 0, i == 0)

        @pl.when(first)
        def _():
            dg_ref[...] = dgp

        @pl.when(jnp.logical_not(first))
        def _():
            dg_ref[...] += dgp

    in_specs = [pl.BlockSpec((tm, dy_width), lambda c, i: (i, c))]
    args = [dy]
    if norm:
        in_specs += [pl.BlockSpec((tm, LANES), lambda c, i: (i, col_off + c)), pl.BlockSpec((None, 1, LANES), lambda c, i: (c, 0, 0))]
        args += [proj, gains]
    if with_rope:
        tab = pl.BlockSpec((tm, LANES), lambda c, i: (i, 0))
        in_specs += [tab, tab]
        args += list(rope)
    out_shape = [jax.ShapeDtypeStruct((T, ncb * LANES), BF16)]
    out_specs = [pl.BlockSpec((tm, LANES), lambda c, i: (i, c))]
    if norm:
        out_shape.append(jax.ShapeDtypeStruct((n_groups, 1, LANES), F32))
        out_specs.append(pl.BlockSpec((None, 1, LANES), lambda c, i: (c // group, 0, 0)))
    res = pl.pallas_call(
        body, out_shape=tuple(out_shape), grid=(ncb, T // tm), in_specs=in_specs, out_specs=tuple(out_specs),
        name=name, compiler_params=_params(2))(*args)
    return res if norm else (res[0], None)


def _dot_exact(x, tri):
    hi = x.astype(BF16)
    r1 = x - hi.astype(F32)
    mid = r1.astype(BF16)
    lo = (r1 - mid.astype(F32)).astype(BF16)
    return _dot(hi, tri, NN) + _dot(mid, tri, NN) + _dot(lo, tri, NN)


def _forget_fwd(name, zt, bias):
    H, T = zt.shape
    blk = min(256, T)

    def body(z_ref, b_ref, c_ref, s_ref):
        z = z_ref[...] + b_ref[...]
        s_ref[...] = jax.nn.sigmoid(-z)
        lf = jnp.minimum(z, 0.0) - jnp.log(1.0 + jnp.exp(-jnp.abs(z)))
        tri = (lax.broadcasted_iota(jnp.int32, (blk, blk), 0) <= lax.broadcasted_iota(jnp.int32, (blk, blk), 1)).astype(BF16)
        carry = jnp.zeros((H, 1), F32)
        for bi in range(T // blk):
            xb = lf[:, bi * blk:(bi + 1) * blk]
            c_ref[:, bi * blk:(bi + 1) * blk] = _dot_exact(xb, tri) + carry
            carry = carry + jnp.sum(xb, axis=-1, keepdims=True)

    shape = jax.ShapeDtypeStruct((H, T), F32)
    full = pl.BlockSpec((H, T), lambda i: (0, 0))
    return pl.pallas_call(
        body, out_shape=(shape, shape), grid=(1,), in_specs=[full, pl.BlockSpec((H, 1), lambda i: (0, 0))],
        out_specs=(full, full), name=name, compiler_params=_params(1))(zt, bias)


def _forget_bwd(name, dct, drt, sgt):
    H, T = dct.shape
    blk = min(256, T)

    def body(dc_ref, dr_ref, s_ref, dz_ref, db_ref):
        dc = dc_ref[...] + dr_ref[...]
        tri = (lax.broadcasted_iota(jnp.int32, (blk, blk), 0) >= lax.broadcasted_iota(jnp.int32, (blk, blk), 1)).astype(BF16)
        carry = jnp.zeros((H, 1), F32)
        db = jnp.zeros((H, 1), F32)
        for bi in reversed(range(T // blk)):
            xb = dc[:, bi * blk:(bi + 1) * blk]
            dz = (_dot_exact(xb, tri) + carry) * s_ref[:, bi * blk:(bi + 1) * blk]
            dz_ref[:, bi * blk:(bi + 1) * blk] = dz
            db = db + jnp.sum(dz, axis=-1, keepdims=True)
            carry = carry + jnp.sum(xb, axis=-1, keepdims=True)
        db_ref[...] = db

    full = pl.BlockSpec((H, T), lambda i: (0, 0))
    return pl.pallas_call(
        body, out_shape=(jax.ShapeDtypeStruct((H, T), F32), jax.ShapeDtypeStruct((H, 1), F32)), grid=(1,),
        in_specs=[full, full, full], out_specs=(full, pl.BlockSpec((H, 1), lambda i: (0, 0))),
        name=name, compiler_params=_params(1))(dct, drt, sgt)


STRIP = 256


def _fox_fwd(name, qk, v, crow, tq, tk, carry=None):
    T, Dh = v.shape
    HP = Dh // LANES
    nk = T // tk
    assert tk % tq == 0 and tq % STRIP == 0
    n_strips = tq // STRIP

    def body(q_ref, k_ref, v_ref, ra_ref, rb_ref, o_ref, la_ref, lb_ref, s_ref, p_ref, m_ref, l_ref, acc_ref):
        i = pl.program_id(1)
        q2 = q_ref[...]
        lo = _lane_lo((tq, LANES))
        q_st = jnp.concatenate([_keep(lo, q2), _keep(jnp.logical_not(lo), q2)], axis=0)
        r_refs = (ra_ref, rb_ref)
        m_ref[...] = jnp.full(m_ref.shape, NEG, F32)
        l_ref[...] = jnp.zeros(l_ref.shape, F32)
        acc_ref[...] = jnp.zeros(acc_ref.shape, F32)
        rel = lax.broadcasted_iota(jnp.int32, (STRIP, tk), 0) - lax.broadcasted_iota(jnp.int32, (STRIP, tk), 1)

        def chunk(kc, masked):
            start = pl.multiple_of(kc * tk, tk)
            kb = k_ref[pl.ds(start, tk), :]
            vb = v_ref[pl.ds(start, tk), :]
            s_ref[...] = _dot(q_st, kb, NT)
            for h in range(2):
                cs = r_refs[h][kc]
                for st in range(n_strips):
                    rows = pl.ds(h * tq + st * STRIP, STRIP)
                    s = s_ref[rows, :] - cs
                    if masked:
                        s = jnp.where(rel >= start - (i * tq + st * STRIP), s, NEG)
                    m_old = m_ref[rows, :]
                    mn = jnp.maximum(m_old, jnp.max(s, axis=-1, keepdims=True))
                    p = jnp.exp(s - mn)
                    alpha = jnp.exp(m_old - mn)
                    l_ref[rows, :] = alpha * l_ref[rows, :] + jnp.sum(p, axis=-1, keepdims=True)
                    m_ref[rows, :] = mn
                    p_ref[rows, :] = p.astype(BF16)
                    acc_ref[rows, :] = acc_ref[rows, :] * alpha
            acc_ref[...] += _dot(p_ref[...], vb, NN)

        n_full = (i * tq) // tk

        def full_chunk(kc, _):
            chunk(kc, False)
            return 0

        lax.fori_loop(0, n_full, full_chunk, 0)
        chunk(n_full, True)
        top, bot = pl.ds(0, tq), pl.ds(tq, tq)
        o_ref[...] = jnp.where(lo, acc_ref[top, :] / l_ref[top, :], acc_ref[bot, :] / l_ref[bot, :])
        la_ref[...] = m_ref[top, :] + jnp.log(l_ref[top, :])
        lb_ref[...] = m_ref[bot, :] + jnp.log(l_ref[bot, :])

    row = lambda off: pl.BlockSpec((None, nk, 1, tk), lambda h, i: (2 * h + off, 0, 0, 0))
    lse = jax.ShapeDtypeStruct((HP, T, 1), F32)
    lspec = pl.BlockSpec((None, tq, 1), lambda h, i: (h, i, 0))
    scratch = [pltpu.VMEM((2 * tq, tk), F32), pltpu.VMEM((2 * tq, tk), BF16), pltpu.VMEM((2 * tq, 1), F32),
               pltpu.VMEM((2 * tq, 1), F32), pltpu.VMEM((2 * tq, LANES), F32)]
    return _call(
        body, name=name, grid=(HP, T // tq), out_shape=(jax.ShapeDtypeStruct((T, Dh), F32), lse, lse),
        in_specs=[pl.BlockSpec((tq, LANES), lambda h, i: (i, h)), pl.BlockSpec((T, LANES), lambda h, i: (0, HP + h)),
                  pl.BlockSpec((T, LANES), lambda h, i: (0, h)), row(0), row(1)],
        out_specs=(pl.BlockSpec((tq, LANES), lambda h, i: (i, h)), lspec, lspec),
        args=[qk, qk, v, crow, crow], scratch_shapes=scratch, carry=carry)


def _fox_bwd(name, qk, v, o, do, crow, lse_a, lse_b, tq, tk, carry=None):
    T, Dh = v.shape
    HP = Dh // LANES
    nk = T // tk
    scale = HEAD_DIM ** -0.5
    assert tk % tq == 0 and tq % STRIP == 0
    n_strips = tq // STRIP

    def body(q_ref, k_ref, v_ref, o_ref, do_ref, ra_ref, rb_ref, la_ref, lb_ref,
             dq_ref, dk_ref, dv_ref, dca_ref, dcb_ref, dra_ref, drb_ref, s_ref, dp_ref, p_ref, ds_ref, dq_acc, dsum_ref):
        i = pl.program_id(1)

        @pl.when(i == 0)
        def _():
            dk_ref[...] = jnp.zeros_like(dk_ref)
            dv_ref[...] = jnp.zeros_like(dv_ref)
            dca_ref[...] = jnp.zeros_like(dca_ref)
            dcb_ref[...] = jnp.zeros_like(dcb_ref)

        q2 = q_ref[...]
        do2 = do_ref[...]
        lo = _lane_lo((tq, LANES))
        hi = jnp.logical_not(lo)
        q_st = jnp.concatenate([_keep(lo, q2), _keep(hi, q2)], axis=0)
        do_st = jnp.concatenate([_keep(lo, do2), _keep(hi, do2)], axis=0)
        prod = do2.astype(F32) * o_ref[...]
        dsum_ref[pl.ds(0, tq), :] = jnp.sum(jnp.where(lo, prod, 0.0), axis=-1, keepdims=True)
        dsum_ref[pl.ds(tq, tq), :] = jnp.sum(jnp.where(lo, 0.0, prod), axis=-1, keepdims=True)
        r_refs, l_refs, dc_refs, dr_refs = (ra_ref, rb_ref), (la_ref, lb_ref), (dca_ref, dcb_ref), (dra_ref, drb_ref)
        dq_acc[...] = jnp.zeros(dq_acc.shape, F32)
        dra_ref[...] = jnp.zeros(dra_ref.shape, F32)
        drb_ref[...] = jnp.zeros(drb_ref.shape, F32)
        rel = lax.broadcasted_iota(jnp.int32, (STRIP, tk), 0) - lax.broadcasted_iota(jnp.int32, (STRIP, tk), 1)

        def chunk(kc, masked):
            start = pl.multiple_of(kc * tk, tk)
            kb = k_ref[pl.ds(start, tk), :]
            vb = v_ref[pl.ds(start, tk), :]
            s_ref[...] = _dot(q_st, kb, NT)
            dp_ref[...] = _dot(do_st, vb, NT)
            for h in range(2):
                cs = r_refs[h][kc]
                col_sum = jnp.zeros((1, tk), F32)
                for st in range(n_strips):
                    rows = pl.ds(st * STRIP, STRIP)
                    both = pl.ds(h * tq + st * STRIP, STRIP)
                    s = s_ref[both, :] - cs
                    if masked:
                        s = jnp.where(rel >= start - (i * tq + st * STRIP), s, NEG)
                    p = jnp.exp(s - l_refs[h][rows, :])
                    ds = p * (dp_ref[both, :] - dsum_ref[both, :])
                    p_ref[both, :] = p.astype(BF16)
                    ds_ref[both, :] = ds.astype(BF16)
                    col_sum = col_sum + jnp.sum(ds, axis=0, keepdims=True)
                    dr_refs[h][rows, :] += jnp.sum(ds, axis=-1, keepdims=True)
                dc_refs[h][kc] = dc_refs[h][kc] - col_sum
            dk_ref[pl.ds(start, tk), :] += _dot(ds_ref[...], q_st, TN)
            dv_ref[pl.ds(start, tk), :] += _dot(p_ref[...], do_st, TN)
            dq_acc[...] += _dot(ds_ref[...], kb, NN)

        n_full = (i * tq) // tk

        def full_chunk(kc, _):
            chunk(kc, False)
            return 0

        lax.fori_loop(0, n_full, full_chunk, 0)
        chunk(n_full, True)
        dq_ref[...] = jnp.where(lo, dq_acc[pl.ds(0, tq), :], dq_acc[pl.ds(tq, tq), :]) * scale

    row = lambda off: pl.BlockSpec((None, nk, 1, tk), lambda h, i: (2 * h + off, 0, 0, 0))
    lspec = pl.BlockSpec((None, tq, 1), lambda h, i: (h, i, 0))
    qspec = pl.BlockSpec((tq, LANES), lambda h, i: (i, h))
    full = pl.BlockSpec((T, LANES), lambda h, i: (0, h))
    dcspec = pl.BlockSpec((None, nk, 1, tk), lambda h, i: (h, 0, 0, 0))
    grad = jax.ShapeDtypeStruct((T, Dh), F32)
    dc = jax.ShapeDtypeStruct((HP, nk, 1, tk), F32)
    dr = jax.ShapeDtypeStruct((HP, T, 1), F32)
    scratch = [pltpu.VMEM((2 * tq, tk), F32), pltpu.VMEM((2 * tq, tk), F32), pltpu.VMEM((2 * tq, tk), BF16), pltpu.VMEM((2 * tq, tk), BF16),
               pltpu.VMEM((2 * tq, LANES), F32), pltpu.VMEM((2 * tq, 1), F32)]
    return _call(
        body, name=name, grid=(HP, T // tq), out_shape=(grad, grad, grad, dc, dc, dr, dr),
        in_specs=[qspec, pl.BlockSpec((T, LANES), lambda h, i: (0, HP + h)), full, qspec, qspec, row(0), row(1), lspec, lspec],
        out_specs=(qspec, full, full, dcspec, dcspec, lspec, lspec),
        args=[qk, qk, v, o, do, crow, crow, lse_a, lse_b], scratch_shapes=scratch, carry=carry)


SWA_GROUP = 2
SWA_GROUP_BWD = 4


def _swa_block(n, q_ref, k_ref):
    qs = pl.multiple_of(n * WINDOW, WINDOW)
    ks = pl.multiple_of(jnp.maximum(n - 1, 0) * WINDOW, WINDOW)
    rel = (qs + lax.broadcasted_iota(jnp.int32, (WINDOW, 2 * WINDOW), 0)) - (ks + lax.broadcasted_iota(jnp.int32, (WINDOW, 2 * WINDOW), 1))
    valid = jnp.logical_and(rel >= 0, rel < WINDOW)
    return qs, ks, valid


def _swa_fwd(name, q, kd, vd, sinks, carry=None):
    T, Dh = q.shape
    HP = Dh // LANES

    def body(q_ref, k_ref, v_ref, sa_ref, sb_ref, o_ref, la_ref, lb_ref):
        lo = _lane_lo((WINDOW, LANES))

        top = lax.broadcasted_iota(jnp.int32, (2 * WINDOW, 1), 0) < WINDOW
        sink = jnp.where(top, sa_ref[...], sb_ref[...])

        def block(n, _):
            qs, ks, valid = _swa_block(n, q_ref, k_ref)
            q2 = q_ref[pl.ds(qs, WINDOW), :]
            kb = k_ref[pl.ds(ks, 2 * WINDOW), :]
            vb = v_ref[pl.ds(ks, 2 * WINDOW), :]
            q_st = jnp.concatenate([_keep(lo, q2), _keep(jnp.logical_not(lo), q2)], axis=0)
            s = jnp.where(jnp.concatenate([valid, valid], axis=0), _dot(q_st, kb, NT), NEG)
            m = jnp.maximum(jnp.max(s, axis=-1, keepdims=True), sink)
            p = jnp.exp(s - m)
            l = jnp.sum(p, axis=-1, keepdims=True) + jnp.exp(sink - m)
            o2 = _dot(p.astype(BF16), vb, NN) / l
            lse = m + jnp.log(l)
            o_ref[pl.ds(qs, WINDOW), :] = jnp.where(lo, o2[:WINDOW], o2[WINDOW:])
            la_ref[pl.ds(qs, WINDOW), :] = lse[:WINDOW]
            lb_ref[pl.ds(qs, WINDOW), :] = lse[WINDOW:]
            return 0

        assert (T // WINDOW) % SWA_GROUP == 0

        def group(g, c):
            for b in range(SWA_GROUP):
                c = block(g * SWA_GROUP + b, c)
            return c

        lax.fori_loop(0, T // WINDOW // SWA_GROUP, group, 0)

    full = pl.BlockSpec((T, LANES), lambda h: (0, h))
    kv = pl.BlockSpec((T, LANES), lambda h: (0, h // 2))
    sink = lambda off: pl.BlockSpec((None, 1, 1), lambda h: (2 * h + off, 0, 0))
    lse = jax.ShapeDtypeStruct((HP, T, 1), F32)
    lspec = pl.BlockSpec((None, T, 1), lambda h: (h, 0, 0))
    return _call(
        body, name=name, grid=(HP,), out_shape=(jax.ShapeDtypeStruct((T, Dh), F32), lse, lse),
        in_specs=[full, kv, kv, sink(0), sink(1)], out_specs=(full, lspec, lspec),
        args=[q, kd, vd, sinks, sinks], carry=carry)


def _swa_bwd(name, q, kd, vd, sinks, o, do, lse_a, lse_b, carry=None):
    T, Dh = q.shape
    HP = Dh // LANES
    scale = HEAD_DIM ** -0.5

    def body(q_ref, k_ref, v_ref, sa_ref, sb_ref, o_ref, do_ref, la_ref, lb_ref, dq_ref, dk_ref, dv_ref, dsa_ref, dsb_ref):
        lo = _lane_lo((WINDOW, LANES))
        hi = jnp.logical_not(lo)
        dk_ref[...] = jnp.zeros_like(dk_ref)
        dv_ref[...] = jnp.zeros_like(dv_ref)

        top = lax.broadcasted_iota(jnp.int32, (2 * WINDOW, 1), 0) < WINDOW
        sink = jnp.where(top, sa_ref[...], sb_ref[...])

        def block(n, dsinks):
            qs, ks, valid = _swa_block(n, q_ref, k_ref)
            rows = pl.ds(qs, WINDOW)
            q2 = q_ref[rows, :]
            do2 = do_ref[rows, :]
            kb = k_ref[pl.ds(ks, 2 * WINDOW), :]
            vb = v_ref[pl.ds(ks, 2 * WINDOW), :]
            prod = do2.astype(F32) * o_ref[rows, :]
            q_st = jnp.concatenate([_keep(lo, q2), _keep(hi, q2)], axis=0)
            do_st = jnp.concatenate([_keep(lo, do2), _keep(hi, do2)], axis=0)
            dsum = jnp.concatenate([jnp.sum(jnp.where(lo, prod, 0.0), axis=-1, keepdims=True),
                                    jnp.sum(jnp.where(lo, 0.0, prod), axis=-1, keepdims=True)], axis=0)
            lse = jnp.concatenate([la_ref[rows, :], lb_ref[rows, :]], axis=0)
            s = jnp.where(jnp.concatenate([valid, valid], axis=0), _dot(q_st, kb, NT), NEG)
            p = jnp.exp(s - lse)
            ds = p * (_dot(do_st, vb, NT) - dsum)
            dsb = ds.astype(BF16)
            dq2 = _dot(dsb, kb, NN)
            dq_ref[rows, :] = jnp.where(lo, dq2[:WINDOW], dq2[WINDOW:]) * scale
            dk_ref[pl.ds(ks, 2 * WINDOW), :] += _dot(dsb, q_st, TN)
            dv_ref[pl.ds(ks, 2 * WINDOW), :] += _dot(p.astype(BF16), do_st, TN)
            gone = jnp.exp(sink - lse) * dsum
            return (dsinks[0] - jnp.sum(gone[:WINDOW], axis=0, keepdims=True),
                    dsinks[1] - jnp.sum(gone[WINDOW:], axis=0, keepdims=True))

        assert (T // WINDOW) % SWA_GROUP_BWD == 0

        def group(g, c):
            for b in range(SWA_GROUP_BWD):
                c = block(g * SWA_GROUP_BWD + b, c)
            return c

        dsa, dsb_ = lax.fori_loop(0, T // WINDOW // SWA_GROUP_BWD, group, (jnp.zeros((1, 1), F32), jnp.zeros((1, 1), F32)))
        dsa_ref[...] = dsa
        dsb_ref[...] = dsb_

    full = pl.BlockSpec((T, LANES), lambda h: (0, h))
    kv = pl.BlockSpec((T, LANES), lambda h: (0, h // 2))
    sink = lambda off: pl.BlockSpec((None, 1, 1), lambda h: (2 * h + off, 0, 0))
    lspec = pl.BlockSpec((None, T, 1), lambda h: (h, 0, 0))
    dsink = pl.BlockSpec((None, 1, 1), lambda h: (h, 0, 0))
    grad = jax.ShapeDtypeStruct((T, Dh), F32)
    ds_shape = jax.ShapeDtypeStruct((HP, 1, 1), F32)
    return _call(
        body, name=name, grid=(HP,), out_shape=(grad, grad, grad, ds_shape, ds_shape),
        in_specs=[full, kv, kv, sink(0), sink(1), full, full, lspec, lspec],
        out_specs=(full, full, full, dsink, dsink),
        args=[q, kd, vd, sinks, sinks, o, do, lse_a, lse_b], carry=carry)


def _place():
    return lax.axis_index("x"), lax.axis_index("y"), lax.axis_index("c")


def _run_carry(name, carry):
    c_in, c_out = len(carry.inputs), len(carry.out_shapes)

    def body(*refs):
        ins, outs, scr = refs[:c_in], refs[c_in:c_in + c_out], refs[c_in + c_out:]
        carry.start(ins, outs, scr)
        carry.middle(ins, outs, scr)
        carry.finish(ins, outs, scr)

    return pl.pallas_call(
        body, out_shape=tuple(carry.out_shapes), in_specs=[_HBM] * c_in, out_specs=tuple([_HBM] * c_out),
        scratch_shapes=carry.scratch, name=name)(*carry.inputs)


def _gather_carry(shards):
    n = len(shards)

    def plan(ins, outs, scr):
        send, recv, local = scr
        x, y, c = _place()
        me, sibling = (x, y, c), (x, y, 1 - c)
        partner, other, diag = (x ^ c, y ^ (1 - c)), (x ^ (1 - c), y ^ c), (1 - x, 1 - y)

        def copy(w, k, block, to, src=None):
            slot = 4 * block[0] + 2 * block[1] + block[2]
            return pltpu.make_async_remote_copy(
                src_ref=outs[w].at[slot] if src is None else src, dst_ref=outs[w].at[slot],
                send_sem=send.at[w, k], recv_sem=recv.at[w, k], device_id=to, device_id_type=MESH)

        def own():
            return [pltpu.make_async_copy(ins[w], outs[w].at[4 * x + 2 * y + c], local.at[w]) for w in range(n)]

        return copy, own, me, sibling, partner, other, diag, c

    def start(ins, outs, scr):
        copy, own, me, sibling, partner, other, _, c = plan(ins, outs, scr)
        for cp in own():
            cp.start()
        for w in range(n):
            copy(w, 1, me, (*partner, c), src=ins[w]).start()
            copy(w, 2, me, (*other, c), src=ins[w]).start()
            copy(w, 0, me, sibling, src=ins[w]).start()

    def middle(ins, outs, scr):
        copy, _, me, sibling, partner, other, _, c = plan(ins, outs, scr)
        for w in range(n):
            copy(w, 1, (*partner, c), me).wait_recv()
            copy(w, 3, (*partner, c), (*other, c)).start()
            copy(w, 4, (*partner, c), sibling).start()
        for w in range(n):
            copy(w, 2, (*other, c), me).wait_recv()
            copy(w, 5, (*other, c), sibling).start()

    def finish(ins, outs, scr):
        copy, own, me, sibling, partner, other, diag, c = plan(ins, outs, scr)
        for w in range(n):
            copy(w, 3, (*diag, c), me).wait_recv()
            copy(w, 6, (*diag, c), sibling).start()
        for w in range(n):
            copy(w, 0, sibling, me).wait_recv()
            copy(w, 4, (*other, 1 - c), me).wait_recv()
            copy(w, 5, (*partner, 1 - c), me).wait_recv()
            copy(w, 6, (*diag, 1 - c), me).wait_recv()
        for w in range(n):
            sent = [copy(w, 0, me, sibling, src=ins[w]), copy(w, 1, me, (*partner, c), src=ins[w]), copy(w, 2, me, (*other, c), src=ins[w]),
                    copy(w, 3, (*partner, c), (*other, c)), copy(w, 4, (*partner, c), sibling), copy(w, 5, (*other, c), sibling),
                    copy(w, 6, (*diag, c), sibling)]
            for cp in sent:
                cp.wait_send()
        for cp in own():
            cp.wait()

    return _Carry(shards, [jax.ShapeDtypeStruct((N_DEV,) + s.shape, s.dtype) for s in shards],
                  [pltpu.SemaphoreType.DMA((n, 7)), pltpu.SemaphoreType.DMA((n, 7)), pltpu.SemaphoreType.DMA((n,))], start, finish, middle)


def _sibling_carry(grads):
    n = len(grads)

    def copies(ins, outs, scr):
        send, recv = scr
        x, y, c = _place()
        return [pltpu.make_async_remote_copy(
            src_ref=ins[w].at[2 * q + (1 - c)], dst_ref=outs[w].at[q], send_sem=send.at[w, q], recv_sem=recv.at[w, q],
            device_id=(x, y, 1 - c), device_id_type=MESH) for w in range(n) for q in range(4)]

    def start(ins, outs, scr):
        for cp in copies(ins, outs, scr):
            cp.start()

    def finish(ins, outs, scr):
        for cp in copies(ins, outs, scr):
            cp.wait()

    return _Carry(grads, [jax.ShapeDtypeStruct((4,) + g.shape[1:], g.dtype) for g in grads],
                  [pltpu.SemaphoreType.DMA((n, 4)), pltpu.SemaphoreType.DMA((n, 4))], start, finish)


def _to_partner_carry(sums):
    n = len(sums)

    def copies(ins, outs, scr):
        send, recv = scr
        x, y, c = _place()
        return [pltpu.make_async_remote_copy(
            src_ref=ins[w].at[k], dst_ref=outs[2 * w + k], send_sem=send.at[w, k], recv_sem=recv.at[w, k],
            device_id=(x ^ c, y ^ (1 - c), c), device_id_type=MESH) for w in range(n) for k in range(2)]

    def start(ins, outs, scr):
        for cp in copies(ins, outs, scr):
            cp.start()

    def finish(ins, outs, scr):
        for cp in copies(ins, outs, scr):
            cp.wait()

    return _Carry(sums, [jax.ShapeDtypeStruct(s.shape[1:], s.dtype) for s in sums for _ in range(2)],
                  [pltpu.SemaphoreType.DMA((n, 2)), pltpu.SemaphoreType.DMA((n, 2))], start, finish)


def _to_other_carry(blocks):
    n = len(blocks)

    def copies(ins, outs, scr):
        send, recv = scr
        x, y, c = _place()
        return [pltpu.make_async_remote_copy(
            src_ref=ins[w], dst_ref=outs[w], send_sem=send.at[w], recv_sem=recv.at[w],
            device_id=(x ^ (1 - c), y ^ c, c), device_id_type=MESH) for w in range(n)]

    def start(ins, outs, scr):
        for cp in copies(ins, outs, scr):
            cp.start()

    def finish(ins, outs, scr):
        for cp in copies(ins, outs, scr):
            cp.wait()

    return _Carry(blocks, [jax.ShapeDtypeStruct(b.shape, b.dtype) for b in blocks],
                  [pltpu.SemaphoreType.DMA((n,)), pltpu.SemaphoreType.DMA((n,))], start, finish)


def _gather_small(packed):
    R, C = packed.shape

    def body(in_ref, out_ref, send, recv):
        x, y, c = _place()
        mine = 4 * x + 2 * y + c
        out_ref[mine] = in_ref[...]
        copies = []
        for k in range(1, N_DEV):
            peer = (x ^ (k >> 2), y ^ ((k >> 1) & 1), c ^ (k & 1))
            copies.append(pltpu.make_async_remote_copy(
                src_ref=in_ref, dst_ref=out_ref.at[mine], send_sem=send.at[k - 1], recv_sem=recv.at[k - 1],
                device_id=peer, device_id_type=MESH))
        for cp in copies:
            cp.start()
        for cp in copies:
            cp.wait()

    vmem = pl.BlockSpec(memory_space=pltpu.VMEM)
    return pl.pallas_call(
        body, out_shape=jax.ShapeDtypeStruct((N_DEV, R, C), F32), in_specs=[vmem], out_specs=vmem,
        scratch_shapes=[pltpu.SemaphoreType.DMA((N_DEV - 1,)), pltpu.SemaphoreType.DMA((N_DEV - 1,))],
        name="small_grads_all_gather")(packed)


def _adamw(w, g, m, v):
    m = ADAM_B1 * m + (1.0 - ADAM_B1) * g
    v = ADAM_B2 * v + (1.0 - ADAM_B2) * (g * g)
    m_hat = m / (1.0 - ADAM_B1 ** ADAM_STEP)
    v_hat = v / (1.0 - ADAM_B2 ** ADAM_STEP)
    delta = -ADAM_LR * (m_hat / (jnp.sqrt(v_hat) + ADAM_EPS) + ADAM_WD * w)
    return delta, m, v


def _pair_add(name, grads, received, slots):
    _, R, C = grads.shape
    tr = _row_tile(R, 2 * ROW_TILE_CAP)

    def body(s_ref, g_ref, r_ref, o_ref):
        o_ref[...] = (g_ref[...].astype(F32) + r_ref[...].astype(F32)).astype(BF16)

    return pl.pallas_call(
        body, out_shape=jax.ShapeDtypeStruct((3, R, C), BF16),
        grid_spec=pltpu.PrefetchScalarGridSpec(
            num_scalar_prefetch=1, grid=(3, R // tr),
            in_specs=[pl.BlockSpec((None, tr, C), lambda k, i, s: (s[k], i, 0)), pl.BlockSpec((None, tr, C), lambda k, i, s: (s[3 + k], i, 0))],
            out_specs=pl.BlockSpec((None, tr, C), lambda k, i, s: (k, i, 0))),
        name=name, compiler_params=_params(2))(slots, grads, received)


def _relay_add(name, sums, relayed):
    _, R, C = sums.shape
    tr = _row_tile(R, 2 * ROW_TILE_CAP)

    def body(s_ref, r_ref, o_ref):
        o_ref[...] = (s_ref[...].astype(F32) + r_ref[...].astype(F32)).astype(BF16)

    blk = pl.BlockSpec((tr, C), lambda i: (i, 0))
    return pl.pallas_call(
        body, out_shape=jax.ShapeDtypeStruct((R, C), BF16), grid=(R // tr,),
        in_specs=[pl.BlockSpec((None, tr, C), lambda i: (2, i, 0)), blk], out_specs=blk,
        name=name, compiler_params=_params(1))(sums, relayed)


def _adam_shard(name, grads, from_sibling, received, w, m, v, own):
    R, C = w.shape
    tr = _row_tile(R, 256)
    tc = C if tr < R or C % (4 * LANES) else 4 * LANES

    def body(o_ref, g_ref, s_ref, ra_ref, rb_ref, w_ref, m_ref, v_ref, g_out, d_out, m_out, v_out):
        g = (g_ref[...].astype(F32) + s_ref[...].astype(F32)) + ra_ref[...].astype(F32) + rb_ref[...].astype(F32)
        delta, mn, vn = _adamw(w_ref[...], g, m_ref[...], v_ref[...])
        g_out[...] = g
        d_out[...] = delta
        m_out[...] = mn
        v_out[...] = vn

    blk = pl.BlockSpec((tr, tc), lambda i, j, o: (i, j))
    shape = jax.ShapeDtypeStruct((R, C), F32)
    return pl.pallas_call(
        body, out_shape=(shape,) * 4,
        grid_spec=pltpu.PrefetchScalarGridSpec(
            num_scalar_prefetch=1, grid=(R // tr, C // tc),
            in_specs=[pl.BlockSpec((None, tr, tc), lambda i, j, o: (o[0], i, j)), pl.BlockSpec((None, tr, tc), lambda i, j, o: (o[1], i, j)),
                      blk, blk, blk, blk, blk],
            out_specs=(blk,) * 4),
        name=name, compiler_params=_params(2))(own, grads, from_sibling, received[0], received[1], w, m, v)


def _adam_small(name, gathered, w, m, v):
    R, C = w.shape

    def body(ga_ref, w_ref, m_ref, v_ref, g_out, d_out, m_out, v_out):
        g = ga_ref[0]
        for d in range(1, N_DEV):
            g = g + ga_ref[d]
        delta, mn, vn = _adamw(w_ref[...], g, m_ref[...], v_ref[...])
        g_out[...] = g
        d_out[...] = delta
        m_out[...] = mn
        v_out[...] = vn

    full = pl.BlockSpec((R, C), lambda i: (0, 0))
    shape = jax.ShapeDtypeStruct((R, C), F32)
    return pl.pallas_call(
        body, out_shape=(shape,) * 4, grid=(1,),
        in_specs=[pl.BlockSpec((N_DEV, R, C), lambda i: (0, 0, 0)), full, full, full], out_specs=(full,) * 4,
        name=name, compiler_params=_params(1))(gathered, w, m, v)


def _pack_small(parts, D, scalar=None):
    g1, gmix, g2, gof, gos, bf, gqf, gkf, gqs, gks, sinks = [p.reshape(-1).astype(F32) for p in parts]
    row3 = jnp.concatenate([gof, gos])
    row4 = jnp.zeros((D,), F32)
    for slot, vec in enumerate((bf, gqf, gkf, gqs, gks, sinks)):
        row4 = lax.dynamic_update_slice(row4, vec, (slot * LANES,))
    zero = jnp.zeros((D,), F32)
    row5 = zero if scalar is None else lax.dynamic_update_slice(zero, jnp.reshape(scalar, (1,)).astype(F32), (0,))
    return jnp.stack([g1, gmix, g2, row3, row4, row5, zero, zero])


def _unpack_small(packed, D, H):
    Dh = D // 2
    row4 = packed[4]
    short = [row4[s * LANES:s * LANES + n] for s, n in enumerate((H, HEAD_DIM, HEAD_DIM, HEAD_DIM, HEAD_DIM, H))]
    vecs = [packed[0], packed[1], packed[2], packed[3, :Dh], packed[3, Dh:]] + short
    return [v[None, :] for v in vecs]


def kernel(x, positions, norm_ffn1_g, ffn1_w_gate, ffn1_w_up, ffn1_w_down, norm_mix_g, w_in, b_forget, fox_q_norm_g, fox_k_norm_g, swa_q_norm_g, swa_k_norm_g, swa_sinks, out_norm_fox_g, out_norm_swa_g, w_out, norm_ffn2_g, ffn2_w_gate, ffn2_w_up, ffn2_w_down, loss_target, m_norm_ffn1_g, m_ffn1_w_gate, m_ffn1_w_up, m_ffn1_w_down, m_norm_mix_g, m_w_in, m_b_forget, m_fox_q_norm_g, m_fox_k_norm_g, m_swa_q_norm_g, m_swa_k_norm_g, m_swa_sinks, m_out_norm_fox_g, m_out_norm_swa_g, m_w_out, m_norm_ffn2_g, m_ffn2_w_gate, m_ffn2_w_up, m_ffn2_w_down, v_norm_ffn1_g, v_ffn1_w_gate, v_ffn1_w_up, v_ffn1_w_down, v_norm_mix_g, v_w_in, v_b_forget, v_fox_q_norm_g, v_fox_k_norm_g, v_swa_q_norm_g, v_swa_k_norm_g, v_swa_sinks, v_out_norm_fox_g, v_out_norm_swa_g, v_w_out, v_norm_ffn2_g, v_ffn2_w_gate, v_ffn2_w_up, v_ffn2_w_down):
    xs = x[0]
    target = loss_target[0]
    T, D = xs.shape
    Dh = D // 2
    H = Dh // HEAD_DIM
    HP = H // 2
    KVW = (H // GQA_GROUP) * HEAD_DIM
    KVB = KVW // LANES
    MAIN = 4 * Dh + 2 * KVW
    F_OFF = 3 * Dh
    tm = min(ROW_TILE_CAP, T)
    tm2 = min(2 * ROW_TILE_CAP, T)
    tq = min(512, T)
    tk = min(512, T)
    nk = T // tk
    cx, cy, cc = _place()
    near = [2 * (cx ^ cc) + (cy ^ (1 - cc)), 2 * (1 - cx) + (1 - cy), 2 * (cx ^ (1 - cc)) + (cy ^ cc)]
    slots = jnp.stack([2 * q + cc for q in near] + near).astype(jnp.int32)
    own = jnp.stack([4 * cx + 2 * cy + cc, 2 * cx + cy]).astype(jnp.int32)

    tr = jnp.transpose
    big_w = [tr(ffn1_w_gate[0]), tr(ffn1_w_up[0]), ffn1_w_down[0], tr(w_in[0]), w_out[0], tr(ffn2_w_gate[0]), tr(ffn2_w_up[0]),
             ffn2_w_down[0]]
    big_m = [tr(m_ffn1_w_gate[0]), tr(m_ffn1_w_up[0]), m_ffn1_w_down[0], tr(m_w_in[0]), m_w_out[0], tr(m_ffn2_w_gate[0]),
             tr(m_ffn2_w_up[0]), m_ffn2_w_down[0]]
    big_v = [tr(v_ffn1_w_gate[0]), tr(v_ffn1_w_up[0]), v_ffn1_w_down[0], tr(v_w_in[0]), v_w_out[0], tr(v_ffn2_w_gate[0]),
             tr(v_ffn2_w_up[0]), v_ffn2_w_down[0]]
    transposed = {"ffn1_w_gate", "ffn1_w_up", "w_in", "ffn2_w_gate", "ffn2_w_up"}
    names = ["ffn1_w_gate", "ffn1_w_up", "ffn1_w_down", "w_in", "w_out", "ffn2_w_gate", "ffn2_w_up", "ffn2_w_down"]
    sh = dict(zip(names, [w.astype(BF16) for w in big_w]))
    lane = jnp.arange(LANES)
    inv_freq = ROPE_THETA ** (-(2.0 * (lane % (HEAD_DIM // 2))).astype(F32) / HEAD_DIM)
    ang = positions[0].astype(F32)[:, None] * inv_freq[None, :]
    cos_t = jnp.cos(ang)
    sin_t = jnp.where((lane & (HEAD_DIM // 2)) == 0, -1.0, 1.0)[None, :] * jnp.sin(ang)
    rope = (cos_t, sin_t)

    def pair_gain(g, blocks):
        return jnp.tile(jnp.concatenate([g[0], g[0]])[None, None, :], (blocks, 1, 1))

    n1, (wg1,) = _rmsnorm_fwd("ffn1_norm", xs, norm_ffn1_g, tm, carry=_gather_carry([sh["ffn1_w_gate"]]))
    a1, (wu1,) = _ffn_gate("ffn1_gate", n1, wg1, tm, carry=_gather_carry([sh["ffn1_w_up"]]))
    (b1, hm1), (wd1,) = _ffn_up_only("ffn1_up", n1, wu1, a1, tm, carry=_gather_carry([sh["ffn1_w_down"]]))
    h1, (win_g,) = _ffn_down("ffn1_down", hm1, wd1, xs, tm, D, carry=_gather_carry([sh["w_in"]]))
    n_in = win_g.shape[1]
    win_t = win_g.reshape(N_DEV * n_in, D)
    win_main = jnp.concatenate([win_t[:F_OFF], win_t[F_OFF + H:]], axis=0)
    win_f = jnp.pad(win_t[F_OFF:F_OFF + H], ((0, LANES - H), (0, 0)))

    u = _rmsnorm_fwd("mix_norm", h1, norm_mix_g, tm)
    proj, (wout_g,) = _mm("mix_proj", u, win_main, T, MAIN // 9, dims=NT, carry=_gather_carry([sh["w_out"]]))
    wout = wout_g.reshape(D, D)
    proj_f = _mm("mix_proj_forget", u, win_f, T, LANES, dims=NT)
    scale = HEAD_DIM ** -0.5
    fox_gains = jnp.concatenate([pair_gain(fox_q_norm_g, HP), pair_gain(fox_k_norm_g, HP)])
    qk_f = _headnorm_fwd_scaled("fox_qk_norm", proj, 0, 2 * HP, fox_gains, T, scale, HP)
    v_f = proj[:, 2 * Dh:3 * Dh].astype(BF16)
    c_t, sg_t = _forget_fwd("forget_gates", proj_f[:, :H].T, b_forget.reshape(H, 1))
    crow = c_t.reshape(H, nk, 1, tk)
    (o_fox, lse_fa, lse_fb), (wg2, wu2) = _fox_fwd("fox_attention", qk_f, v_f, crow, tq, tk,
                                                   carry=_gather_carry([sh["ffn2_w_gate"], sh["ffn2_w_up"]]))

    swa_q_gains = pair_gain(swa_q_norm_g, HP)
    swa_k_gains = pair_gain(swa_k_norm_g, KVB)
    q_s = _headnorm_fwd("swa_q_norm", proj, 3 * HP, HP, swa_q_gains, T, scale, rope=rope)
    k_d = _headnorm_fwd("swa_k_norm", proj, 4 * HP, KVB, swa_k_gains, T, 1.0, rope=rope, dup=True)
    v_s = proj[:, 4 * Dh + KVW:].astype(BF16).reshape(T, H // GQA_GROUP, 1, HEAD_DIM)
    v_d = jnp.broadcast_to(v_s, (T, H // GQA_GROUP, 2, HEAD_DIM)).reshape(T, 2 * KVW)
    sinks3 = swa_sinks.reshape(H, 1, 1)
    o_swa, lse_sa, lse_sb = _swa_fwd("swa_attention", q_s, k_d, v_d, sinks3)

    on = _outnorm_fwd("out_norm", o_fox, o_swa, out_norm_fox_g, out_norm_swa_g, tm)
    h2 = _mm("mix_out", on, wout, tm2, min(512, D), resid=h1)

    n2 = _rmsnorm_fwd("ffn2_norm", h2, norm_ffn2_g, tm)
    (a2, b2, hm2), (wd2,) = _ffn_up("ffn2_up", n2, wg2, wu2, tm2, carry=_gather_carry([sh["ffn2_w_down"]]))
    y = _ffn_down("ffn2_down", hm2, wd2, h2, tm2, D)
    dy, dyh, sq = _loss_grad("loss_grad", y, target, tm)
    loss_part = 0.5 * sq[0, 0] / D

    J, Fs, _ = wg2.shape
    aspec = pl.BlockSpec((None, tm, Fs), lambda i, j: (j, i, 0))
    wspec = pl.BlockSpec((None, Fs, D), lambda i, j: (j, 0, 0))
    got = {}
    local = {}

    def pair_sums(keys, grads, received):
        for nm, g, r in zip(keys, grads, received):
            local[nm] = (g, r)
        return [_pair_add("sum_" + nm, g, r, slots) for nm, g, r in zip(keys, grads, received)]

    def relay_sums(keys, sums, hop1):
        out = []
        for i, (nm, s) in enumerate(zip(keys, sums)):
            got[nm] = [hop1[2 * i]]
            out.append(_relay_add("relay_" + nm, s, hop1[2 * i + 1]))
        return out

    def arrived(keys, hop2):
        for nm, blk in zip(keys, hop2):
            got[nm].append(blk)

    dwd2 = _wgrad_down("ffn2_wgrad_down", hm2, dyh, D)
    (da2, db2), (sib_d2,) = _ffn_bwd_mid("ffn2_bwd_mid", dyh, wd2, a2, b2, tm2, carry=_sibling_carry([dwd2]))
    (sum_wd2,) = pair_sums(names[7:8], [dwd2], [sib_d2])
    (dwg2, dwu2), hop1 = _wgrad_up("ffn2_wgrad_up", n2, da2, db2, min(1024, D), carry=_to_partner_carry([sum_wd2]))
    (t_wd2,) = relay_sums(names[7:8], [sum_wd2], hop1)
    aspec2 = pl.BlockSpec((None, tm2, Fs), lambda i, j: (j, i, 0))
    dn2, (via_wd2, *sib2) = _reduce_mm("ffn2_bwd_in", [(da2, aspec2, wg2, wspec), (db2, aspec2, wu2, wspec)], [], NN, T, D, tm2, J,
                                       carry=_join(_to_other_carry([t_wd2]), _sibling_carry([dwg2, dwu2])))
    arrived(names[7:8], [via_wd2])
    dh2, dg_ffn2, dh2b = _rmsnorm_bwd("ffn2_norm_bwd", dn2, h2, norm_ffn2_g, dy, min(256, T), 1.0)
    sum_wg2, sum_wu2 = pair_sums(names[5:7], [dwg2, dwu2], sib2)

    dwout = _wgrad_2d("mix_out_wgrad", on, dh2b, min(512, D), D)
    dwout_g = dwout.reshape(N_DEV, D // N_DEV, D)
    do_fox, dg_of = _outnorm_bwd("out_norm_bwd_fox", dh2b, wout, 0, o_fox, out_norm_fox_g, tm)
    do_swa, dg_os = _outnorm_bwd("out_norm_bwd_swa", dh2b, wout, 1, o_swa, out_norm_swa_g, tm)

    (dq_f, dk_f, dv_f, dc_a, dc_b, dr_a, dr_b), (*hop1, sib_wout) = _fox_bwd(
        "fox_attention_bwd", qk_f, v_f, o_fox, do_fox, crow, lse_fa, lse_fb, tq, tk,
        carry=_join(_to_partner_carry([sum_wg2, sum_wu2]), _sibling_carry([dwout_g])))
    t_wg2, t_wu2 = relay_sums(names[5:7], [sum_wg2, sum_wu2], hop1)
    (sum_wout,) = pair_sums(names[4:5], [dwout_g], [sib_wout])
    dqf_raw, dg_fq = _headnorm_bwd("fox_q_norm_bwd", dq_f, proj, 0, HP, fox_gains[:HP], HP, T, 1.0)
    dkf_raw, dg_fk = _headnorm_bwd("fox_k_norm_bwd", dk_f, proj, HP, HP, fox_gains[HP:], HP, T, 1.0)
    dct = jnp.stack([dc_a.reshape(HP, T), dc_b.reshape(HP, T)], axis=1).reshape(H, T)
    drt = jnp.stack([dr_a.reshape(HP, T), dr_b.reshape(HP, T)], axis=1).reshape(H, T)
    dz_t, db_f = _forget_bwd("forget_gates_bwd", dct, drt, sg_t)

    (dq_s, dk_p, dv_p, dsink_a, dsink_b), hop2 = _swa_bwd(
        "swa_attention_bwd", q_s, k_d, v_d, sinks3, o_swa, do_swa, lse_sa, lse_sb, carry=_to_other_carry([t_wg2, t_wu2]))
    arrived(names[5:7], hop2)
    dqs_raw, dg_sq = _headnorm_bwd("swa_q_norm_bwd", dq_s, proj, 3 * HP, HP, swa_q_gains, HP, T, 1.0, rope=rope)
    dks_raw, dg_sk = _headnorm_bwd("swa_k_norm_bwd", dk_p, proj, 4 * HP, KVB, swa_k_gains, KVB, T, 1.0, rope=rope, fold=True)
    dvs_raw, _ = _headnorm_bwd("swa_v_fold", dv_p, None, 0, KVB, None, KVB, T, 1.0, fold=True, norm=False)

    dproj = jnp.concatenate([dqf_raw, dkf_raw, dv_f.astype(BF16), dqs_raw, dks_raw, dvs_raw], axis=1)
    dproj_f = jnp.pad(dz_t.T, ((0, 0), (0, LANES - H))).astype(BF16)
    dwin_main, hop1 = _wgrad_2d("mix_proj_wgrad", dproj, u, MAIN // 9, D, carry=_to_partner_carry([sum_wout]))
    (t_wout,) = relay_sums(names[4:5], [sum_wout], hop1)
    dwin_f = _wgrad_2d("mix_proj_forget_wgrad", dproj_f, u, LANES, min(1024, D))
    dwin_t = jnp.concatenate([dwin_main[:F_OFF], dwin_f[:H], dwin_main[F_OFF:]], axis=0)
    dwin_g = dwin_t.reshape(N_DEV, n_in, D)
    tkb = MAIN // 9
    du, (via_wout, sib_win) = _reduce_mm(
        "mix_bwd_in",
        [(dproj, pl.BlockSpec((tm2, tkb), lambda i, r: (i, r)), win_main, pl.BlockSpec((tkb, D), lambda i, r: (r, 0)))],
        [(dproj_f, pl.BlockSpec((tm2, LANES), lambda i, r: (i, 0)), win_f, pl.BlockSpec((LANES, D), lambda i, r: (0, 0)))],
        NN, T, D, tm2, 9, carry=_join(_to_other_carry([t_wout]), _sibling_carry([dwin_g])))
    arrived(names[4:5], [via_wout])
    dh1, dg_mix, dh1h = _rmsnorm_bwd("mix_norm_bwd", du, h1, norm_mix_g, dh2, min(256, T), 0.5)
    (sum_win,) = pair_sums(names[3:4], [dwin_g], [sib_win])

    dwd1, hop1 = _wgrad_down("ffn1_wgrad_down", hm1, dh1h, D, carry=_to_partner_carry([sum_win]))
    (t_win,) = relay_sums(names[3:4], [sum_win], hop1)
    (da1, db1), (via_win, sib_d) = _ffn_bwd_mid("ffn1_bwd_mid", dh1h, wd1, a1, b1, tm2,
                                                carry=_join(_to_other_carry([t_win]), _sibling_carry([dwd1])))
    arrived(names[3:4], [via_win])
    (sum_wd1,) = pair_sums(names[2:3], [dwd1], [sib_d])
    dwg1, hop1 = _wgrad_down("ffn1_wgrad_gate", da1, n1, D, carry=_to_partner_carry([sum_wd1]))
    (t_wd1,) = relay_sums(names[2:3], [sum_wd1], hop1)
    dwu1, (via_wd1, sib_g) = _wgrad_down("ffn1_wgrad_up", db1, n1, D,
                                         carry=_join(_to_other_carry([t_wd1]), _sibling_carry([dwg1])))
    arrived(names[2:3], [via_wd1])
    (sum_wg1,) = pair_sums(names[0:1], [dwg1], [sib_g])
    dn1_gate, (*hop1, sib_u) = _reduce_mm(
        "ffn1_bwd_in_gate", [(da1, aspec, wg1, wspec)], [], NN, T, D, tm, J,
        carry=_join(_to_partner_carry([sum_wg1]), _sibling_carry([dwu1])))
    (t_wg1,) = relay_sums(names[0:1], [sum_wg1], hop1)
    (sum_wu1,) = pair_sums(names[1:2], [dwu1], [sib_u])
    dn1, (via_wg1, *hop1) = _reduce_mm(
        "ffn1_bwd_in_up", [(db1, aspec, wu1, wspec)], [], NN, T, D, tm, J, init=dn1_gate,
        carry=_join(_to_other_carry([t_wg1]), _to_partner_carry([sum_wu1])))
    arrived(names[0:1], [via_wg1])
    (t_wu1,) = relay_sums(names[1:2], [sum_wu1], hop1)
    arrived(names[1:2], _run_carry("grads_exchange", _to_other_carry([t_wu1])))
    dx, dg_ffn1 = _rmsnorm_bwd("ffn1_norm_bwd", dn1, xs, norm_ffn1_g, dh1, min(256, T), None)

    big_out = [_adam_shard("adam_" + nm, local[nm][0], local[nm][1], got[nm], w, m, v, own)
               for nm, w, m, v in zip(names, big_w, big_m, big_v)]

    dsinks = jnp.stack([dsink_a.reshape(HP), dsink_b.reshape(HP)], axis=1).reshape(H)
    small_g = [dg_ffn1, dg_mix, dg_ffn2, dg_of, dg_os, db_f, dg_fq[0, 0, :HEAD_DIM], dg_fk[0, 0, :HEAD_DIM],
               dg_sq[0, 0, :HEAD_DIM], dg_sk[0, 0, :HEAD_DIM], dsinks]
    small_w = [norm_ffn1_g, norm_mix_g, norm_ffn2_g, out_norm_fox_g, out_norm_swa_g, b_forget, fox_q_norm_g, fox_k_norm_g,
               swa_q_norm_g, swa_k_norm_g, swa_sinks]
    small_m = [m_norm_ffn1_g, m_norm_mix_g, m_norm_ffn2_g, m_out_norm_fox_g, m_out_norm_swa_g, m_b_forget, m_fox_q_norm_g,
               m_fox_k_norm_g, m_swa_q_norm_g, m_swa_k_norm_g, m_swa_sinks]
    small_v = [v_norm_ffn1_g, v_norm_mix_g, v_norm_ffn2_g, v_out_norm_fox_g, v_out_norm_swa_g, v_b_forget, v_fox_q_norm_g,
               v_fox_k_norm_g, v_swa_q_norm_g, v_swa_k_norm_g, v_swa_sinks]
    gathered = _gather_small(_pack_small(small_g, D, loss_part))
    small_out = _adam_small("adam_small", gathered, _pack_small(small_w, D), _pack_small(small_m, D), _pack_small(small_v, D))
    loss = small_out[0][5, 0]
    small_out = [_unpack_small(p, D, H) for p in small_out]

    order = ["norm_ffn1_g", "ffn1_w_gate", "ffn1_w_up", "ffn1_w_down", "norm_mix_g", "w_in", "b_forget", "fox_q_norm_g", "fox_k_norm_g",
             "swa_q_norm_g", "swa_k_norm_g", "swa_sinks", "out_norm_fox_g", "out_norm_swa_g", "w_out", "norm_ffn2_g",
             "ffn2_w_gate", "ffn2_w_up", "ffn2_w_down"]
    small_names = ["norm_ffn1_g", "norm_mix_g", "norm_ffn2_g", "out_norm_fox_g", "out_norm_swa_g", "b_forget", "fox_q_norm_g",
                   "fox_k_norm_g", "swa_q_norm_g", "swa_k_norm_g", "swa_sinks"]
    result = [loss, dx[None]]
    for kind in range(4):
        for nm in order:
            if nm in names:
                leaf = big_out[names.index(nm)][kind]
                result.append((tr(leaf) if nm in transposed else leaf)[None])
            else:
                result.append(small_out[kind][small_names.index(nm)])
    return tuple(result)


def _headnorm_fwd_scaled(name, proj, col_off, ncb, gains, tm, scale, n_scaled):
    T = proj.shape[0]

    def body(x_ref, g_ref, o_ref):
        xv = x_ref[...]
        lo = _lane_lo(xv.shape)
        y = xv * _head_rstd(xv, lo) * g_ref[...]
        y = y * jnp.where(pl.program_id(0) < n_scaled, scale, 1.0)
        o_ref[...] = y.astype(BF16)

    return pl.pallas_call(
        body, out_shape=jax.ShapeDtypeStruct((T, ncb * LANES), BF16), grid=(ncb, T // tm),
        in_specs=[pl.BlockSpec((tm, LANES), lambda c, i: (i, col_off + c)), pl.BlockSpec((None, 1, LANES), lambda c, i: (c, 0, 0))],
        out_specs=pl.BlockSpec((tm, LANES), lambda c, i: (i, c)), name=name, compiler_params=_params(2))(proj, gains)
```

```python
import jax
import jax.numpy as jnp
from jax import lax
from jax.experimental import pallas as pl
from jax.experimental.pallas import tpu as pltpu

F32 = jnp.float32
BF16 = jnp.bfloat16

HEAD_DIM = 64
LANES = 128
WINDOW = 128
GQA_GROUP = 4
EPS = 1e-6
ROPE_THETA = 10000.0
ADAM_LR = 0.001
ADAM_B1 = 0.9
ADAM_B2 = 0.999
ADAM_EPS = 1e-08
ADAM_WD = 0.01
ADAM_STEP = 10
N_DEV = 8
NEG = -1e30
VMEM_LIMIT_V7X = 48 * 1024 * 1024
ROW_TILE_CAP = 512
MESH = pl.DeviceIdType.MESH

NN = (((1,), (0,)), ((), ()))
NT = (((1,), (1,)), ((), ()))
TN = (((0,), (0,)), ((), ()))


def _dot(a, b, dims):
    return lax.dot_general(a, b, dims, preferred_element_type=F32)


def _params(n_axes):
    return pltpu.CompilerParams(dimension_semantics=("arbitrary",) * n_axes, vmem_limit_bytes=VMEM_LIMIT_V7X)


def _row_tile(rows, cap=ROW_TILE_CAP):
    best = None
    for t in range(16, min(rows, cap) + 1, 16):
        if rows % t == 0:
            best = t
    return best or rows


def _lane_lo(shape):
    return lax.broadcasted_iota(jnp.int32, shape, len(shape) - 1) < HEAD_DIM


def _keep(sel, x):
    return jnp.where(sel, x.astype(F32), 0.0).astype(BF16)


_HBM = pl.BlockSpec(memory_space=pltpu.HBM)


class _Carry:
    def __init__(self, inputs, out_shapes, scratch, start, finish, middle=None):
        self.inputs, self.out_shapes, self.scratch = list(inputs), list(out_shapes), list(scratch)
        self.start, self.finish, self.middle = start, finish, middle or (lambda ins, outs, scr: None)


def _join(*carries):
    def hook(which):
        def run(ins, outs, scr):
            i = o = s = 0
            for c in carries:
                ni, no, ns = len(c.inputs), len(c.out_shapes), len(c.scratch)
                getattr(c, which)(ins[i:i + ni], outs[o:o + no], scr[s:s + ns])
                i, o, s = i + ni, o + no, s + ns
        return run

    return _Carry([a for c in carries for a in c.inputs], [a for c in carries for a in c.out_shapes],
                  [a for c in carries for a in c.scratch], hook("start"), hook("finish"), hook("middle"))


def _call(body, *, name, grid, in_specs, out_specs, out_shape, args, scratch_shapes=(), carry=None):
    params = _params(len(grid))
    if carry is None:
        return pl.pallas_call(body, out_shape=out_shape, grid=grid, in_specs=list(in_specs), out_specs=out_specs,
                              scratch_shapes=list(scratch_shapes), name=name, compiler_params=params)(*args)
    single = not isinstance(out_shape, (tuple, list))
    shapes = (out_shape,) if single else tuple(out_shape)
    specs = (out_specs,) if single else tuple(out_specs)
    n_in, n_out, n_scr = len(args), len(shapes), len(scratch_shapes)
    c_in, c_out = len(carry.inputs), len(carry.out_shapes)

    def wrapped(*refs):
        ins, c_ins = refs[:n_in], refs[n_in:n_in + c_in]
        o0 = n_in + c_in
        outs, c_outs = refs[o0:o0 + n_out], refs[o0 + n_out:o0 + n_out + c_out]
        s0 = o0 + n_out + c_out
        scr, c_scr = refs[s0:s0 + n_scr], refs[s0 + n_scr:]
        step, total = pl.program_id(0), grid[0]
        for ax in range(1, len(grid)):
            step, total = step * grid[ax] + pl.program_id(ax), total * grid[ax]

        @pl.when(step == 0)
        def _():
            carry.start(c_ins, c_outs, c_scr)

        @pl.when(step == total // 2)
        def _():
            carry.middle(c_ins, c_outs, c_scr)

        body(*ins, *outs, *scr)

        @pl.when(step == total - 1)
        def _():
            carry.finish(c_ins, c_outs, c_scr)

    res = pl.pallas_call(
        wrapped, out_shape=shapes + tuple(carry.out_shapes), grid=grid, in_specs=list(in_specs) + [_HBM] * c_in,
        out_specs=specs + (_HBM,) * c_out, scratch_shapes=list(scratch_shapes) + carry.scratch, name=name,
        compiler_params=params)(*args, *carry.inputs)
    main = res[:n_out]
    return (main[0] if single else tuple(main)), tuple(res[n_out:])


def _rms_bwd(dn, x, g):
    r = lax.rsqrt(jnp.mean(x * x, axis=-1, keepdims=True) + EPS)
    xh = x * r
    dxh = dn * g
    dx = r * (dxh - xh * jnp.mean(dxh * xh, axis=-1, keepdims=True))
    return dx, jnp.sum(dn * xh, axis=0, keepdims=True)


def _rmsnorm_fwd(name, x, g, tm, carry=None):
    T, D = x.shape

    def body(x_ref, g_ref, o_ref):
        xf = x_ref[...]
        r = lax.rsqrt(jnp.mean(xf * xf, axis=-1, keepdims=True) + EPS)
        o_ref[...] = (xf * r * g_ref[...]).astype(BF16)

    return _call(
        body, name=name, grid=(T // tm,), out_shape=jax.ShapeDtypeStruct((T, D), BF16),
        in_specs=[pl.BlockSpec((tm, D), lambda i: (i, 0)), pl.BlockSpec((1, D), lambda i: (0, 0))],
        out_specs=pl.BlockSpec((tm, D), lambda i: (i, 0)), args=[x, g], carry=carry)


def _outnorm_fwd(name, o_fox, o_swa, g_fox, g_swa, tm):
    T, Dh = o_fox.shape

    def body(a_ref, b_ref, ga_ref, gb_ref, o_ref):
        for ref, g_ref, lo in ((a_ref, ga_ref, 0), (b_ref, gb_ref, Dh)):
            xf = ref[...]
            r = lax.rsqrt(jnp.mean(xf * xf, axis=-1, keepdims=True) + EPS)
            o_ref[:, lo:lo + Dh] = (xf * r * g_ref[...]).astype(BF16)

    row = pl.BlockSpec((tm, Dh), lambda i: (i, 0))
    gain = pl.BlockSpec((1, Dh), lambda i: (0, 0))
    return pl.pallas_call(
        body, out_shape=jax.ShapeDtypeStruct((T, 2 * Dh), BF16), grid=(T // tm,),
        in_specs=[row, row, gain, gain], out_specs=pl.BlockSpec((tm, 2 * Dh), lambda i: (i, 0)),
        name=name, compiler_params=_params(1))(o_fox, o_swa, g_fox, g_swa)


def _outnorm_bwd(name, dhb, wout, half, o, g, tm):
    T, D = dhb.shape
    Dh = o.shape[1]

    def body(a_ref, w_ref, o_ref, g_ref, do_ref, dg_ref):
        don = _dot(a_ref[...], w_ref[...], NT)
        dx, dg = _rms_bwd(don, o_ref[...], g_ref[...])
        do_ref[...] = dx.astype(BF16)

        @pl.when(pl.program_id(0) == 0)
        def _():
            dg_ref[...] = dg

        @pl.when(pl.program_id(0) > 0)
        def _():
            dg_ref[...] += dg

    return pl.pallas_call(
        body, out_shape=(jax.ShapeDtypeStruct((T, Dh), BF16), jax.ShapeDtypeStruct((1, Dh), F32)), grid=(T // tm,),
        in_specs=[pl.BlockSpec((tm, D), lambda i: (i, 0)), pl.BlockSpec((Dh, D), lambda i: (half, 0)),
                  pl.BlockSpec((tm, Dh), lambda i: (i, 0)), pl.BlockSpec((1, Dh), lambda i: (0, 0))],
        out_specs=(pl.BlockSpec((tm, Dh), lambda i: (i, 0)), pl.BlockSpec((1, Dh), lambda i: (0, 0))),
        name=name, compiler_params=_params(1))(dhb, wout, o, g)


def _mm(name, a, b, tm, tn, dims=NN, resid=None, carry=None):
    M, K = a.shape
    transposed = dims == NT
    N = b.shape[0] if transposed else b.shape[1]

    def body(*refs):
        if resid is None:
            a_ref, b_ref, o_ref = refs
            o_ref[...] = _dot(a_ref[...], b_ref[...], dims)
        else:
            a_ref, b_ref, r_ref, o_ref = refs
            o_ref[...] = r_ref[...] + _dot(a_ref[...], b_ref[...], dims)

    ospec = pl.BlockSpec((tm, tn), lambda n, i: (i, n))
    bspec = pl.BlockSpec((tn, K), lambda n, i: (n, 0)) if transposed else pl.BlockSpec((K, tn), lambda n, i: (0, n))
    in_specs = [pl.BlockSpec((tm, K), lambda n, i: (i, 0)), bspec]
    args = [a, b]
    if resid is not None:
        in_specs.append(ospec)
        args.append(resid)
    return _call(body, name=name, grid=(N // tn, M // tm), in_specs=in_specs, out_specs=ospec,
                 out_shape=jax.ShapeDtypeStruct((M, N), F32), args=args, carry=carry)


def _wgrad_2d(name, a, b, tmm, tn, carry=None):
    T, M = a.shape
    N = b.shape[1]

    def body(a_ref, b_ref, o_ref):
        o_ref[...] = _dot(a_ref[...], b_ref[...], TN).astype(BF16)

    return _call(
        body, name=name, grid=(M // tmm, N // tn), out_shape=jax.ShapeDtypeStruct((M, N), BF16),
        in_specs=[pl.BlockSpec((T, tmm), lambda m, n: (0, m)), pl.BlockSpec((T, tn), lambda m, n: (0, n))],
        out_specs=pl.BlockSpec((tmm, tn), lambda m, n: (m, n)), args=[a, b], carry=carry)


def _wgrad_down(name, hm, df, tn, carry=None):
    J, T, Fs = hm.shape
    D = df.shape[1]

    def body(a_ref, b_ref, o_ref):
        o_ref[...] = _dot(a_ref[...], b_ref[...], TN).astype(BF16)

    return _call(
        body, name=name, grid=(J, D // tn), out_shape=jax.ShapeDtypeStruct((J, Fs, D), BF16),
        in_specs=[pl.BlockSpec((None, T, Fs), lambda j, n: (j, 0, 0)), pl.BlockSpec((T, tn), lambda j, n: (0, n))],
        out_specs=pl.BlockSpec((None, Fs, tn), lambda j, n: (j, 0, n)), args=[hm, df], carry=carry)


def _wgrad_up(name, n, da, db, tn, carry=None):
    T, D = n.shape
    J, _, Fs = da.shape

    def body(n_ref, da_ref, db_ref, og_ref, ou_ref):
        nv = n_ref[...]
        og_ref[...] = _dot(da_ref[...], nv, TN).astype(BF16)
        ou_ref[...] = _dot(db_ref[...], nv, TN).astype(BF16)

    act = pl.BlockSpec((None, T, Fs), lambda j, m: (j, 0, 0))
    out = pl.BlockSpec((None, Fs, tn), lambda j, m: (j, 0, m))
    shape = jax.ShapeDtypeStruct((J, Fs, D), BF16)
    return _call(
        body, name=name, grid=(J, D // tn), out_shape=(shape, shape),
        in_specs=[pl.BlockSpec((T, tn), lambda j, m: (0, m)), act, act], out_specs=(out, out),
        args=[n, da, db], carry=carry)


def _reduce_mm(name, pairs, once, dims, T, D, tm, steps, init=None, carry=None):
    n_pairs = len(pairs)
    n_once = len(once)
    n_mm = 2 * (n_pairs + n_once)

    def body(*refs):
        pr = refs[:2 * n_pairs]
        on = refs[2 * n_pairs:n_mm]
        o_ref = refs[-1]
        r = pl.program_id(1)

        @pl.when(r == 0)
        def _():
            o_ref[...] = jnp.zeros(o_ref.shape, F32) if init is None else refs[n_mm][...]

        for p in range(n_pairs):
            o_ref[...] += _dot(pr[2 * p][...], pr[2 * p + 1][...], dims)

        if n_once:
            @pl.when(r == steps - 1)
            def _():
                for p in range(n_once):
                    o_ref[...] += _dot(on[2 * p][...], on[2 * p + 1][...], dims)

    in_specs, args = [], []
    for a, a_spec, w, w_spec in list(pairs) + list(once):
        in_specs += [a_spec, w_spec]
        args += [a, w]
    row = pl.BlockSpec((tm, D), lambda i, r: (i, 0))
    if init is not None:
        in_specs.append(row)
        args.append(init)
    return _call(body, name=name, grid=(T // tm, steps), in_specs=in_specs, out_specs=row, out_shape=jax.ShapeDtypeStruct((T, D), F32),
                 args=args, carry=carry)


def _rmsnorm_bwd(name, dn, x, g, dh, tm, bf16_scale, carry=None):
    T, D = x.shape
    emit_bf16 = bf16_scale is not None

    def body(dn_ref, x_ref, g_ref, dh_ref, *outs):
        dxn, dg = _rms_bwd(dn_ref[...], x_ref[...], g_ref[...])
        dx = dh_ref[...] + dxn
        outs[0][...] = dx
        if emit_bf16:
            outs[2][...] = (bf16_scale * dx).astype(BF16)

        @pl.when(pl.program_id(0) == 0)
        def _():
            outs[1][...] = dg

        @pl.when(pl.program_id(0) > 0)
        def _():
            outs[1][...] += dg

    row = pl.BlockSpec((tm, D), lambda i: (i, 0))
    gain = pl.BlockSpec((1, D), lambda i: (0, 0))
    out_shape = [jax.ShapeDtypeStruct((T, D), F32), jax.ShapeDtypeStruct((1, D), F32)]
    out_specs = [row, gain]
    if emit_bf16:
        out_shape.append(jax.ShapeDtypeStruct((T, D), BF16))
        out_specs.append(row)
    return _call(body, name=name, grid=(T // tm,), in_specs=[row, row, gain, row], out_specs=tuple(out_specs),
                 out_shape=tuple(out_shape), args=[dn, x, g, dh], carry=carry)


def _loss_grad(name, y, target, tm):
    T, D = y.shape

    def body(y_ref, t_ref, dy_ref, dyh_ref, sq_ref):
        diff = y_ref[...] - t_ref[...]
        sq = jnp.sum(jnp.sum(diff * diff, axis=1, keepdims=True), axis=0, keepdims=True)
        dy = diff * (1.0 / D)
        dy_ref[...] = dy
        dyh_ref[...] = (0.5 * dy).astype(BF16)

        @pl.when(pl.program_id(0) == 0)
        def _():
            sq_ref[...] = sq

        @pl.when(pl.program_id(0) > 0)
        def _():
            sq_ref[...] += sq

    row = pl.BlockSpec((tm, D), lambda i: (i, 0))
    return pl.pallas_call(
        body, out_shape=(jax.ShapeDtypeStruct((T, D), F32), jax.ShapeDtypeStruct((T, D), BF16), jax.ShapeDtypeStruct((1, 1), F32)),
        grid=(T // tm,), in_specs=[row, row], out_specs=(row, row, pl.BlockSpec((1, 1), lambda i: (0, 0))),
        name=name, compiler_params=_params(1))(y, target)


def _ffn_up(name, n, wg, wu, tm, carry=None):
    T, D = n.shape
    J, Fs, _ = wg.shape

    def body(n_ref, wg_ref, wu_ref, a_ref, b_ref, h_ref):
        xv = n_ref[...]
        a = _dot(xv, wg_ref[...], NT)
        b = _dot(xv, wu_ref[...], NT)
        a_ref[...] = a.astype(BF16)
        b_ref[...] = b.astype(BF16)
        h_ref[...] = (a * jax.nn.sigmoid(a) * b).astype(BF16)

    act = jax.ShapeDtypeStruct((J, T, Fs), BF16)
    wspec = pl.BlockSpec((None, Fs, D), lambda j, i: (j, 0, 0))
    aspec = pl.BlockSpec((None, tm, Fs), lambda j, i: (j, i, 0))
    return _call(
        body, name=name, grid=(J, T // tm), out_shape=(act, act, act),
        in_specs=[pl.BlockSpec((tm, D), lambda j, i: (i, 0)), wspec, wspec], out_specs=(aspec, aspec, aspec),
        args=[n, wg, wu], carry=carry)


def _ffn_gate(name, n, wg, tm, carry=None):
    T, D = n.shape
    J, Fs, _ = wg.shape

    def body(n_ref, wg_ref, a_ref):
        a_ref[...] = _dot(n_ref[...], wg_ref[...], NT).astype(BF16)

    aspec = pl.BlockSpec((None, tm, Fs), lambda j, i: (j, i, 0))
    return _call(
        body, name=name, grid=(J, T // tm), out_shape=jax.ShapeDtypeStruct((J, T, Fs), BF16),
        in_specs=[pl.BlockSpec((tm, D), lambda j, i: (i, 0)), pl.BlockSpec((None, Fs, D), lambda j, i: (j, 0, 0))],
        out_specs=aspec, args=[n, wg], carry=carry)


def _ffn_up_only(name, n, wu, a, tm, carry=None):
    T, D = n.shape
    J, Fs, _ = wu.shape

    def body(n_ref, wu_ref, a_ref, b_ref, h_ref):
        b = _dot(n_ref[...], wu_ref[...], NT)
        a = a_ref[...].astype(F32)
        b_ref[...] = b.astype(BF16)
        h_ref[...] = (a * jax.nn.sigmoid(a) * b).astype(BF16)

    act = jax.ShapeDtypeStruct((J, T, Fs), BF16)
    aspec = pl.BlockSpec((None, tm, Fs), lambda j, i: (j, i, 0))
    return _call(
        body, name=name, grid=(J, T // tm), out_shape=(act, act),
        in_specs=[pl.BlockSpec((tm, D), lambda j, i: (i, 0)), pl.BlockSpec((None, Fs, D), lambda j, i: (j, 0, 0)), aspec],
        out_specs=(aspec, aspec), args=[n, wu, a], carry=carry)


def _ffn_down(name, hm, wd, resid, tm, tn, carry=None):
    J, T, Fs = hm.shape
    D = wd.shape[2]

    def body(h_ref, w_ref, r_ref, o_ref):
        @pl.when(pl.program_id(2) == 0)
        def _():
            o_ref[...] = r_ref[...]

        o_ref[...] += _dot(h_ref[...] * 0.5, w_ref[...], NN)

    tile = pl.BlockSpec((tm, tn), lambda i, n, j: (i, n))
    return _call(
        body, name=name, grid=(T // tm, D // tn, J), out_shape=jax.ShapeDtypeStruct((T, D), F32),
        in_specs=[pl.BlockSpec((None, tm, Fs), lambda i, n, j: (j, i, 0)), pl.BlockSpec((None, Fs, tn), lambda i, n, j: (j, 0, n)), tile],
        out_specs=tile, args=[hm, wd, resid], carry=carry)


def _ffn_bwd_mid(name, dfh, wd, a, b, tm, carry=None):
    T, D = dfh.shape
    J, Fs, _ = wd.shape

    def body(df_ref, w_ref, a_ref, b_ref, da_ref, db_ref):
        dhm = _dot(df_ref[...], w_ref[...], NT)
        av = a_ref[...].astype(F32)
        bv = b_ref[...].astype(F32)
        sg = jax.nn.sigmoid(av)
        da_ref[...] = (dhm * bv * (sg * (1.0 + av * (1.0 - sg)))).astype(BF16)
        db_ref[...] = (dhm * (av * sg)).astype(BF16)

    act = jax.ShapeDtypeStruct((J, T, Fs), BF16)
    aspec = pl.BlockSpec((None, tm, Fs), lambda j, i: (j, i, 0))
    return _call(
        body, name=name, grid=(J, T // tm), out_shape=(act, act),
        in_specs=[pl.BlockSpec((tm, D), lambda j, i: (i, 0)), pl.BlockSpec((None, Fs, D), lambda j, i: (j, 0, 0)), aspec, aspec],
        out_specs=(aspec, aspec), args=[dfh, wd, a, b], carry=carry)


def _rot_half(y, lane):
    first = (lane & (HEAD_DIM // 2)) == 0
    return jnp.where(first, pltpu.roll(y, LANES - HEAD_DIM // 2, 1), pltpu.roll(y, HEAD_DIM // 2, 1))


def _head_rstd(x, lo):
    sq = x * x
    ss_a = jnp.sum(jnp.where(lo, sq, 0.0), axis=-1, keepdims=True)
    ss_b = jnp.sum(jnp.where(lo, 0.0, sq), axis=-1, keepdims=True)
    return lax.rsqrt(jnp.where(lo, ss_a, ss_b) * (1.0 / HEAD_DIM) + EPS)


def _headnorm_fwd(name, proj, col_off, ncb, gains, tm, scale, rope=None, dup=False):
    T = proj.shape[0]
    with_rope = rope is not None
    width = 2 * LANES if dup else LANES

    def body(*refs):
        if with_rope:
            x_ref, g_ref, cos_ref, sin_ref, o_ref = refs
        else:
            x_ref, g_ref, o_ref = refs
        xv = x_ref[...]
        lane = lax.broadcasted_iota(jnp.int32, xv.shape, 1)
        lo = lane < HEAD_DIM
        y = xv * _head_rstd(xv, lo) * g_ref[...]
        if with_rope:
            y = y * cos_ref[...] + _rot_half(y, lane) * sin_ref[...]
        y = y * scale
        if dup:
            sw = pltpu.roll(y, HEAD_DIM, 1)
            o_ref[:, :LANES] = jnp.where(lo, y, sw).astype(BF16)
            o_ref[:, LANES:] = jnp.where(lo, sw, y).astype(BF16)
        else:
            o_ref[...] = y.astype(BF16)

    in_specs = [pl.BlockSpec((tm, LANES), lambda c, i: (i, col_off + c)), pl.BlockSpec((None, 1, LANES), lambda c, i: (c, 0, 0))]
    args = [proj, gains]
    if with_rope:
        tab = pl.BlockSpec((tm, LANES), lambda c, i: (i, 0))
        in_specs += [tab, tab]
        args += list(rope)
    return pl.pallas_call(
        body, out_shape=jax.ShapeDtypeStruct((T, ncb * width), BF16), grid=(ncb, T // tm),
        in_specs=in_specs, out_specs=pl.BlockSpec((tm, width), lambda c, i: (i, c)),
        name=name, compiler_params=_params(2))(*args)


def _headnorm_bwd(name, dy, proj, col_off, ncb, gains, group, tm, scale, rope=None, fold=False, norm=True):
    T = dy.shape[0]
    with_rope = rope is not None
    n_groups = ncb // group
    dy_width = 4 * LANES if fold else LANES

    def body(*refs):
        refs = list(refs)
        dy_ref = refs.pop(0)
        x_ref = refs.pop(0) if norm else None
        g_ref = refs.pop(0) if norm else None
        cos_ref = refs.pop(0) if with_rope else None
        sin_ref = refs.pop(0) if with_rope else None
        dx_ref = refs.pop(0)
        dg_ref = refs.pop(0) if norm else None
        c = pl.program_id(0)
        i = pl.program_id(1)
        d = dy_ref[...]
        lane = lax.broadcasted_iota(jnp.int32, (d.shape[0], LANES), 1)
        lo = lane < HEAD_DIM
        if fold:
            t0 = d[:, 0:LANES] + d[:, LANES:2 * LANES]
            t1 = d[:, 2 * LANES:3 * LANES] + d[:, 3 * LANES:4 * LANES]
            d = jnp.where(lo, t0 + pltpu.roll(t0, HEAD_DIM, 1), t1 + pltpu.roll(t1, HEAD_DIM, 1))
        d = d * scale
        if with_rope:
            d = d * cos_ref[...] + _rot_half(d * sin_ref[...], lane)
        if not norm:
            dx_ref[...] = d.astype(BF16)
            return
        xv = x_ref[...]
        gv = g_ref[...]
        r = _head_rstd(xv, lo)
        xh = xv * r
        dxh = d * gv
        pr = dxh * xh
        m_a = jnp.sum(jnp.where(lo, pr, 0.0), axis=-1, keepdims=True)
        m_b = jnp.sum(jnp.where(lo, 0.0, pr), axis=-1, keepdims=True)
        mean = jnp.where(lo, m_a, m_b) * (1.0 / HEAD_DIM)
        dx_ref[...] = (r * (dxh - xh * mean)).astype(BF16)
        dgp = jnp.sum(d * xh, axis=0, keepdims=True)
        dgp = dgp + pltpu.roll(dgp, HEAD_DIM, 1)
        first = jnp.logical_and(c % group == 0, i == 0)

        @pl.when(first)
        def _():
            dg_ref[...] = dgp

        @pl.when(jnp.logical_not(first))
        def _():
            dg_ref[...] += dgp

    in_specs = [pl.BlockSpec((tm, dy_width), lambda c, i: (i, c))]
    args = [dy]
    if norm:
        in_specs += [pl.BlockSpec((tm, LANES), lambda c, i: (i, col_off + c)), pl.BlockSpec((None, 1, LANES), lambda c, i: (c, 0, 0))]
        args += [proj, gains]
    if with_rope:
        tab = pl.BlockSpec((tm, LANES), lambda c, i: (i, 0))
        in_specs += [tab, tab]
        args += list(rope)
    out_shape = [jax.ShapeDtypeStruct((T, ncb * LANES), BF16)]
    out_specs = [pl.BlockSpec((tm, LANES), lambda c, i: (i, c))]
    if norm:
        out_shape.append(jax.ShapeDtypeStruct((n_groups, 1, LANES), F32))
        out_specs.append(pl.BlockSpec((None, 1, LANES), lambda c, i: (c // group, 0, 0)))
    res = pl.pallas_call(
        body, out_shape=tuple(out_shape), grid=(ncb, T // tm), in_specs=in_specs, out_specs=tuple(out_specs),
        name=name, compiler_params=_params(2))(*args)
    return res if norm else (res[0], None)


def _dot_exact(x, tri):
    hi = x.astype(BF16)
    r1 = x - hi.astype(F32)
    mid = r1.astype(BF16)
    lo = (r1 - mid.astype(F32)).astype(BF16)
    return _dot(hi, tri, NN) + _dot(mid, tri, NN) + _dot(lo, tri, NN)


def _forget_fwd(name, zt, bias):
    H, T = zt.shape
    blk = min(256, T)

    def body(z_ref, b_ref, c_ref, s_ref):
        z = z_ref[...] + b_ref[...]
        s_ref[...] = jax.nn.sigmoid(-z)
        lf = jnp.minimum(z, 0.0) - jnp.log(1.0 + jnp.exp(-jnp.abs(z)))
        tri = (lax.broadcasted_iota(jnp.int32, (blk, blk), 0) <= lax.broadcasted_iota(jnp.int32, (blk, blk), 1)).astype(BF16)
        carry = jnp.zeros((H, 1), F32)
        for bi in range(T // blk):
            xb = lf[:, bi * blk:(bi + 1) * blk]
            c_ref[:, bi * blk:(bi + 1) * blk] = _dot_exact(xb, tri) + carry
            carry = carry + jnp.sum(xb, axis=-1, keepdims=True)

    shape = jax.ShapeDtypeStruct((H, T), F32)
    full = pl.BlockSpec((H, T), lambda i: (0, 0))
    return pl.pallas_call(
        body, out_shape=(shape, shape), grid=(1,), in_specs=[full, pl.BlockSpec((H, 1), lambda i: (0, 0))],
        out_specs=(full, full), name=name, compiler_params=_params(1))(zt, bias)


def _forget_bwd(name, dct, drt, sgt):
    H, T = dct.shape
    blk = min(256, T)

    def body(dc_ref, dr_ref, s_ref, dz_ref, db_ref):
        dc = dc_ref[...] + dr_ref[...]
        tri = (lax.broadcasted_iota(jnp.int32, (blk, blk), 0) >= lax.broadcasted_iota(jnp.int32, (blk, blk), 1)).astype(BF16)
        carry = jnp.zeros((H, 1), F32)
        db = jnp.zeros((H, 1), F32)
        for bi in reversed(range(T // blk)):
            xb = dc[:, bi * blk:(bi + 1) * blk]
            dz = (_dot_exact(xb, tri) + carry) * s_ref[:, bi * blk:(bi + 1) * blk]
            dz_ref[:, bi * blk:(bi + 1) * blk] = dz
            db = db + jnp.sum(dz, axis=-1, keepdims=True)
            carry = carry + jnp.sum(xb, axis=-1, keepdims=True)
        db_ref[...] = db

    full = pl.BlockSpec((H, T), lambda i: (0, 0))
    return pl.pallas_call(
        body, out_shape=(jax.ShapeDtypeStruct((H, T), F32), jax.ShapeDtypeStruct((H, 1), F32)), grid=(1,),
        in_specs=[full, full, full], out_specs=(full, pl.BlockSpec((H, 1), lambda i: (0, 0))),
        name=name, compiler_params=_params(1))(dct, drt, sgt)


FOX_STRIP_FWD = 128
FOX_STRIP_BWD = 256


def _fox_fwd(name, qk, v, crow, tq, tk, strip, carry=None):
    T, Dh = v.shape
    HP = Dh // LANES
    nk = T // tk
    assert tk % tq == 0 and tq % strip == 0
    n_strips = tq // strip

    def body(q_ref, k_ref, v_ref, ra_ref, rb_ref, o_ref, la_ref, lb_ref, s_ref, p_ref, m_ref, l_ref, acc_ref):
        i = pl.program_id(1)
        q2 = q_ref[...]
        lo = _lane_lo((tq, LANES))
        q_st = jnp.concatenate([_keep(lo, q2), _keep(jnp.logical_not(lo), q2)], axis=0)
        r_refs = (ra_ref, rb_ref)
        m_ref[...] = jnp.full(m_ref.shape, NEG, F32)
        l_ref[...] = jnp.zeros(l_ref.shape, F32)
        acc_ref[...] = jnp.zeros(acc_ref.shape, F32)
        rel = lax.broadcasted_iota(jnp.int32, (strip, tk), 0) - lax.broadcasted_iota(jnp.int32, (strip, tk), 1)

        def chunk(kc, masked):
            start = pl.multiple_of(kc * tk, tk)
            kb = k_ref[pl.ds(start, tk), :]
            vb = v_ref[pl.ds(start, tk), :]
            s_ref[...] = _dot(q_st, kb, NT)
            for h in range(2):
                cs = r_refs[h][kc]
                for st in range(n_strips):
                    rows = pl.ds(h * tq + st * strip, strip)
                    s = s_ref[rows, :] - cs
                    if masked:
                        s = jnp.where(rel >= start - (i * tq + st * strip), s, NEG)
                    m_old = m_ref[rows, :]
                    mn = jnp.maximum(m_old, jnp.max(s, axis=-1, keepdims=True))
                    p = jnp.exp(s - mn)
                    alpha = jnp.exp(m_old - mn)
                    l_ref[rows, :] = alpha * l_ref[rows, :] + jnp.sum(p, axis=-1, keepdims=True)
                    m_ref[rows, :] = mn
                    p_ref[rows, :] = p.astype(BF16)
                    acc_ref[rows, :] = acc_ref[rows, :] * alpha
            acc_ref[...] += _dot(p_ref[...], vb, NN)

        n_full = (i * tq) // tk

        def full_chunk(kc, _):
            chunk(kc, False)
            return 0

        lax.fori_loop(0, n_full, full_chunk, 0)
        chunk(n_full, True)
        top, bot = pl.ds(0, tq), pl.ds(tq, tq)
        o_ref[...] = jnp.where(lo, acc_ref[top, :] / l_ref[top, :], acc_ref[bot, :] / l_ref[bot, :])
        la_ref[...] = m_ref[top, :] + jnp.log(l_ref[top, :])
        lb_ref[...] = m_ref[bot, :] + jnp.log(l_ref[bot, :])

    row = lambda off: pl.BlockSpec((None, nk, 1, tk), lambda h, i: (2 * h + off, 0, 0, 0))
    lse = jax.ShapeDtypeStruct((HP, T, 1), F32)
    lspec = pl.BlockSpec((None, tq, 1), lambda h, i: (h, i, 0))
    scratch = [pltpu.VMEM((2 * tq, tk), F32), pltpu.VMEM((2 * tq, tk), BF16), pltpu.VMEM((2 * tq, 1), F32),
               pltpu.VMEM((2 * tq, 1), F32), pltpu.VMEM((2 * tq, LANES), F32)]
    return _call(
        body, name=name, grid=(HP, T // tq), out_shape=(jax.ShapeDtypeStruct((T, Dh), F32), lse, lse),
        in_specs=[pl.BlockSpec((tq, LANES), lambda h, i: (i, h)), pl.BlockSpec((T, LANES), lambda h, i: (0, HP + h)),
                  pl.BlockSpec((T, LANES), lambda h, i: (0, h)), row(0), row(1)],
        out_specs=(pl.BlockSpec((tq, LANES), lambda h, i: (i, h)), lspec, lspec),
        args=[qk, qk, v, crow, crow], scratch_shapes=scratch, carry=carry)


def _fox_bwd(name, qk, v, o, do, crow, lse_a, lse_b, tq, tk, strip, carry=None):
    T, Dh = v.shape
    HP = Dh // LANES
    nk = T // tk
    scale = HEAD_DIM ** -0.5
    assert tk % tq == 0 and tq % strip == 0
    n_strips = tq // strip

    def body(q_ref, k_ref, v_ref, o_ref, do_ref, ra_ref, rb_ref, la_ref, lb_ref,
             dq_ref, dk_ref, dv_ref, dca_ref, dcb_ref, dra_ref, drb_ref, s_ref, dp_ref, p_ref, ds_ref, dq_acc, dsum_ref):
        i = pl.program_id(1)

        @pl.when(i == 0)
        def _():
            dk_ref[...] = jnp.zeros_like(dk_ref)
            dv_ref[...] = jnp.zeros_like(dv_ref)
            dca_ref[...] = jnp.zeros_like(dca_ref)
            dcb_ref[...] = jnp.zeros_like(dcb_ref)

        q2 = q_ref[...]
        do2 = do_ref[...]
        lo = _lane_lo((tq, LANES))
        hi = jnp.logical_not(lo)
        q_st = jnp.concatenate([_keep(lo, q2), _keep(hi, q2)], axis=0)
        do_st = jnp.concatenate([_keep(lo, do2), _keep(hi, do2)], axis=0)
        prod = do2.astype(F32) * o_ref[...]
        dsum_ref[pl.ds(0, tq), :] = jnp.sum(jnp.where(lo, prod, 0.0), axis=-1, keepdims=True)
        dsum_ref[pl.ds(tq, tq), :] = jnp.sum(jnp.where(lo, 0.0, prod), axis=-1, keepdims=True)
        r_refs, l_refs, dc_refs, dr_refs = (ra_ref, rb_ref), (la_ref, lb_ref), (dca_ref, dcb_ref), (dra_ref, drb_ref)
        dq_acc[...] = jnp.zeros(dq_acc.shape, F32)
        dra_ref[...] = jnp.zeros(dra_ref.shape, F32)
        drb_ref[...] = jnp.zeros(drb_ref.shape, F32)
        rel = lax.broadcasted_iota(jnp.int32, (strip, tk), 0) - lax.broadcasted_iota(jnp.int32, (strip, tk), 1)

        def chunk(kc, masked):
            start = pl.multiple_of(kc * tk, tk)
            kb = k_ref[pl.ds(start, tk), :]
            vb = v_ref[pl.ds(start, tk), :]
            s_ref[...] = _dot(q_st, kb, NT)
            dp_ref[...] = _dot(do_st, vb, NT)
            for h in range(2):
                cs = r_refs[h][kc]
                col_sum = jnp.zeros((1, tk), F32)
                for st in range(n_strips):
                    rows = pl.ds(st * strip, strip)
                    both = pl.ds(h * tq + st * strip, strip)
                    s = s_ref[both, :] - cs
                    if masked:
                        s = jnp.where(rel >= start - (i * tq + st * strip), s, NEG)
                    p = jnp.exp(s - l_refs[h][rows, :])
                    ds = p * (dp_ref[both, :] - dsum_ref[both, :])
                    p_ref[both, :] = p.astype(BF16)
                    ds_ref[both, :] = ds.astype(BF16)
                    col_sum = col_sum + jnp.sum(ds, axis=0, keepdims=True)
                    dr_refs[h][rows, :] += jnp.sum(ds, axis=-1, keepdims=True)
                dc_refs[h][kc] = dc_refs[h][kc] - col_sum
            dk_ref[pl.ds(start, tk), :] += _dot(ds_ref[...], q_st, TN)
            dv_ref[pl.ds(start, tk), :] += _dot(p_ref[...], do_st, TN)
            dq_acc[...] += _dot(ds_ref[...], kb, NN)

        n_full = (i * tq) // tk

        def full_chunk(kc, _):
            chunk(kc, False)
            return 0

        lax.fori_loop(0, n_full, full_chunk, 0)
        chunk(n_full, True)
        dq_ref[...] = jnp.where(lo, dq_acc[pl.ds(0, tq), :], dq_acc[pl.ds(tq, tq), :]) * scale

    row = lambda off: pl.BlockSpec((None, nk, 1, tk), lambda h, i: (2 * h + off, 0, 0, 0))
    lspec = pl.BlockSpec((None, tq, 1), lambda h, i: (h, i, 0))
    qspec = pl.BlockSpec((tq, LANES), lambda h, i: (i, h))
    full = pl.BlockSpec((T, LANES), lambda h, i: (0, h))
    dcspec = pl.BlockSpec((None, nk, 1, tk), lambda h, i: (h, 0, 0, 0))
    grad = jax.ShapeDtypeStruct((T, Dh), F32)
    dc = jax.ShapeDtypeStruct((HP, nk, 1, tk), F32)
    dr = jax.ShapeDtypeStruct((HP, T, 1), F32)
    scratch = [pltpu.VMEM((2 * tq, tk), F32), pltpu.VMEM((2 * tq, tk), F32), pltpu.VMEM((2 * tq, tk), BF16), pltpu.VMEM((2 * tq, tk), BF16),
               pltpu.VMEM((2 * tq, LANES), F32), pltpu.VMEM((2 * tq, 1), F32)]
    return _call(
        body, name=name, grid=(HP, T // tq), out_shape=(grad, grad, grad, dc, dc, dr, dr),
        in_specs=[qspec, pl.BlockSpec((T, LANES), lambda h, i: (0, HP + h)), full, qspec, qspec, row(0), row(1), lspec, lspec],
        out_specs=(qspec, full, full, dcspec, dcspec, lspec, lspec),
        args=[qk, qk, v, o, do, crow, crow, lse_a, lse_b], scratch_shapes=scratch, carry=carry)


SWA_GROUP = 2
SWA_GROUP_BWD = 4


def _swa_block(n, q_ref, k_ref):
    qs = pl.multiple_of(n * WINDOW, WINDOW)
    ks = pl.multiple_of(jnp.maximum(n - 1, 0) * WINDOW, WINDOW)
    rel = (qs + lax.broadcasted_iota(jnp.int32, (WINDOW, 2 * WINDOW), 0)) - (ks + lax.broadcasted_iota(jnp.int32, (WINDOW, 2 * WINDOW), 1))
    valid = jnp.logical_and(rel >= 0, rel < WINDOW)
    return qs, ks, valid


def _swa_fwd(name, q, kd, vd, sinks, carry=None):
    T, Dh = q.shape
    HP = Dh // LANES

    def body(q_ref, k_ref, v_ref, sa_ref, sb_ref, o_ref, la_ref, lb_ref):
        lo = _lane_lo((WINDOW, LANES))

        top = lax.broadcasted_iota(jnp.int32, (2 * WINDOW, 1), 0) < WINDOW
        sink = jnp.where(top, sa_ref[...], sb_ref[...])

        def block(n, _):
            qs, ks, valid = _swa_block(n, q_ref, k_ref)
            q2 = q_ref[pl.ds(qs, WINDOW), :]
            kb = k_ref[pl.ds(ks, 2 * WINDOW), :]
            vb = v_ref[pl.ds(ks, 2 * WINDOW), :]
            q_st = jnp.concatenate([_keep(lo, q2), _keep(jnp.logical_not(lo), q2)], axis=0)
            s = jnp.where(jnp.concatenate([valid, valid], axis=0), _dot(q_st, kb, NT), NEG)
            m = jnp.maximum(jnp.max(s, axis=-1, keepdims=True), sink)
            p = jnp.exp(s - m)
            l = jnp.sum(p, axis=-1, keepdims=True) + jnp.exp(sink - m)
            o2 = _dot(p.astype(BF16), vb, NN) / l
            lse = m + jnp.log(l)
            o_ref[pl.ds(qs, WINDOW), :] = jnp.where(lo, o2[:WINDOW], o2[WINDOW:])
            la_ref[pl.ds(qs, WINDOW), :] = lse[:WINDOW]
            lb_ref[pl.ds(qs, WINDOW), :] = lse[WINDOW:]
            return 0

        assert (T // WINDOW) % SWA_GROUP == 0

        def group(g, c):
            for b in range(SWA_GROUP):
                c = block(g * SWA_GROUP + b, c)
            return c

        lax.fori_loop(0, T // WINDOW // SWA_GROUP, group, 0)

    full = pl.BlockSpec((T, LANES), lambda h: (0, h))
    kv = pl.BlockSpec((T, LANES), lambda h: (0, h // 2))
    sink = lambda off: pl.BlockSpec((None, 1, 1), lambda h: (2 * h + off, 0, 0))
    lse = jax.ShapeDtypeStruct((HP, T, 1), F32)
    lspec = pl.BlockSpec((None, T, 1), lambda h: (h, 0, 0))
    return _call(
        body, name=name, grid=(HP,), out_shape=(jax.ShapeDtypeStruct((T, Dh), F32), lse, lse),
        in_specs=[full, kv, kv, sink(0), sink(1)], out_specs=(full, lspec, lspec),
        args=[q, kd, vd, sinks, sinks], carry=carry)


def _swa_bwd(name, q, kd, vd, sinks, o, do, lse_a, lse_b, carry=None):
    T, Dh = q.shape
    HP = Dh // LANES
    scale = HEAD_DIM ** -0.5

    def body(q_ref, k_ref, v_ref, sa_ref, sb_ref, o_ref, do_ref, la_ref, lb_ref, dq_ref, dk_ref, dv_ref, dsa_ref, dsb_ref):
        lo = _lane_lo((WINDOW, LANES))
        hi = jnp.logical_not(lo)
        dk_ref[...] = jnp.zeros_like(dk_ref)
        dv_ref[...] = jnp.zeros_like(dv_ref)

        top = lax.broadcasted_iota(jnp.int32, (2 * WINDOW, 1), 0) < WINDOW
        sink = jnp.where(top, sa_ref[...], sb_ref[...])

        def block(n, dsinks):
            qs, ks, valid = _swa_block(n, q_ref, k_ref)
            rows = pl.ds(qs, WINDOW)
            q2 = q_ref[rows, :]
            do2 = do_ref[rows, :]
            kb = k_ref[pl.ds(ks, 2 * WINDOW), :]
            vb = v_ref[pl.ds(ks, 2 * WINDOW), :]
            prod = do2.astype(F32) * o_ref[rows, :]
            q_st = jnp.concatenate([_keep(lo, q2), _keep(hi, q2)], axis=0)
            do_st = jnp.concatenate([_keep(lo, do2), _keep(hi, do2)], axis=0)
            dsum = jnp.concatenate([jnp.sum(jnp.where(lo, prod, 0.0), axis=-1, keepdims=True),
                                    jnp.sum(jnp.where(lo, 0.0, prod), axis=-1, keepdims=True)], axis=0)
            lse = jnp.concatenate([la_ref[rows, :], lb_ref[rows, :]], axis=0)
            s = jnp.where(jnp.concatenate([valid, valid], axis=0), _dot(q_st, kb, NT), NEG)
            p = jnp.exp(s - lse)
            ds = p * (_dot(do_st, vb, NT) - dsum)
            dsb = ds.astype(BF16)
            dq2 = _dot(dsb, kb, NN)
            dq_ref[rows, :] = jnp.where(lo, dq2[:WINDOW], dq2[WINDOW:]) * scale
            dk_ref[pl.ds(ks, 2 * WINDOW), :] += _dot(dsb, q_st, TN)
            dv_ref[pl.ds(ks, 2 * WINDOW), :] += _dot(p.astype(BF16), do_st, TN)
            gone = jnp.exp(sink - lse) * dsum
            return (dsinks[0] - jnp.sum(gone[:WINDOW], axis=0, keepdims=True),
                    dsinks[1] - jnp.sum(gone[WINDOW:], axis=0, keepdims=True))

        assert (T // WINDOW) % SWA_GROUP_BWD == 0

        def group(g, c):
            for b in range(SWA_GROUP_BWD):
                c = block(g * SWA_GROUP_BWD + b, c)
            return c

        dsa, dsb_ = lax.fori_loop(0, T // WINDOW // SWA_GROUP_BWD, group, (jnp.zeros((1, 1), F32), jnp.zeros((1, 1), F32)))
        dsa_ref[...] = dsa
        dsb_ref[...] = dsb_

    full = pl.BlockSpec((T, LANES), lambda h: (0, h))
    kv = pl.BlockSpec((T, LANES), lambda h: (0, h // 2))
    sink = lambda off: pl.BlockSpec((None, 1, 1), lambda h: (2 * h + off, 0, 0))
    lspec = pl.BlockSpec((None, T, 1), lambda h: (h, 0, 0))
    dsink = pl.BlockSpec((None, 1, 1), lambda h: (h, 0, 0))
    grad = jax.ShapeDtypeStruct((T, Dh), F32)
    ds_shape = jax.ShapeDtypeStruct((HP, 1, 1), F32)
    return _call(
        body, name=name, grid=(HP,), out_shape=(grad, grad, grad, ds_shape, ds_shape),
        in_specs=[full, kv, kv, sink(0), sink(1), full, full, lspec, lspec],
        out_specs=(full, full, full, dsink, dsink),
        args=[q, kd, vd, sinks, sinks, o, do, lse_a, lse_b], carry=carry)


def _place():
    return lax.axis_index("x"), lax.axis_index("y"), lax.axis_index("c")


def _run_carry(name, carry):
    c_in, c_out = len(carry.inputs), len(carry.out_shapes)

    def body(*refs):
        ins, outs, scr = refs[:c_in], refs[c_in:c_in + c_out], refs[c_in + c_out:]
        carry.start(ins, outs, scr)
        carry.middle(ins, outs, scr)
        carry.finish(ins, outs, scr)

    return pl.pallas_call(
        body, out_shape=tuple(carry.out_shapes), in_specs=[_HBM] * c_in, out_specs=tuple([_HBM] * c_out),
        scratch_shapes=carry.scratch, name=name)(*carry.inputs)


def _gather_carry(shards):
    n = len(shards)

    def plan(ins, outs, scr):
        send, recv, local = scr
        x, y, c = _place()
        me, sibling = (x, y, c), (x, y, 1 - c)
        partner, other, diag = (x ^ c, y ^ (1 - c)), (x ^ (1 - c), y ^ c), (1 - x, 1 - y)

        def copy(w, k, block, to, src=None):
            slot = 4 * block[0] + 2 * block[1] + block[2]
            return pltpu.make_async_remote_copy(
                src_ref=outs[w].at[slot] if src is None else src, dst_ref=outs[w].at[slot],
                send_sem=send.at[w, k], recv_sem=recv.at[w, k], device_id=to, device_id_type=MESH)

        def own():
            return [pltpu.make_async_copy(ins[w], outs[w].at[4 * x + 2 * y + c], local.at[w]) for w in range(n)]

        return copy, own, me, sibling, partner, other, diag, c

    def start(ins, outs, scr):
        copy, own, me, sibling, partner, other, _, c = plan(ins, outs, scr)
        for cp in own():
            cp.start()
        for w in range(n):
            copy(w, 1, me, (*partner, c), src=ins[w]).start()
            copy(w, 2, me, (*other, c), src=ins[w]).start()
            copy(w, 0, me, sibling, src=ins[w]).start()

    def middle(ins, outs, scr):
        copy, _, me, sibling, partner, other, _, c = plan(ins, outs, scr)
        for w in range(n):
            copy(w, 1, (*partner, c), me).wait_recv()
            copy(w, 3, (*partner, c), (*other, c)).start()
            copy(w, 4, (*partner, c), sibling).start()
        for w in range(n):
            copy(w, 2, (*other, c), me).wait_recv()
            copy(w, 5, (*other, c), sibling).start()

    def finish(ins, outs, scr):
        copy, own, me, sibling, partner, other, diag, c = plan(ins, outs, scr)
        for w in range(n):
            copy(w, 3, (*diag, c), me).wait_recv()
            copy(w, 6, (*diag, c), sibling).start()
        for w in range(n):
            copy(w, 0, sibling, me).wait_recv()
            copy(w, 4, (*other, 1 - c), me).wait_recv()
            copy(w, 5, (*partner, 1 - c), me).wait_recv()
            copy(w, 6, (*diag, 1 - c), me).wait_recv()
        for w in range(n):
            sent = [copy(w, 0, me, sibling, src=ins[w]), copy(w, 1, me, (*partner, c), src=ins[w]), copy(w, 2, me, (*other, c), src=ins[w]),
                    copy(w, 3, (*partner, c), (*other, c)), copy(w, 4, (*partner, c), sibling), copy(w, 5, (*other, c), sibling),
                    copy(w, 6, (*diag, c), sibling)]
            for cp in sent:
                cp.wait_send()
        for cp in own():
            cp.wait()

    return _Carry(shards, [jax.ShapeDtypeStruct((N_DEV,) + s.shape, s.dtype) for s in shards],
                  [pltpu.SemaphoreType.DMA((n, 7)), pltpu.SemaphoreType.DMA((n, 7)), pltpu.SemaphoreType.DMA((n,))], start, finish, middle)


def _sibling_carry(grads):
    n = len(grads)

    def copies(ins, outs, scr):
        send, recv = scr
        x, y, c = _place()
        return [pltpu.make_async_remote_copy(
            src_ref=ins[w].at[2 * q + (1 - c)], dst_ref=outs[w].at[q], send_sem=send.at[w, q], recv_sem=recv.at[w, q],
            device_id=(x, y, 1 - c), device_id_type=MESH) for w in range(n) for q in range(4)]

    def start(ins, outs, scr):
        for cp in copies(ins, outs, scr):
            cp.start()

    def finish(ins, outs, scr):
        for cp in copies(ins, outs, scr):
            cp.wait()

    return _Carry(grads, [jax.ShapeDtypeStruct((4,) + g.shape[1:], g.dtype) for g in grads],
                  [pltpu.SemaphoreType.DMA((n, 4)), pltpu.SemaphoreType.DMA((n, 4))], start, finish)


def _to_partner_carry(sums):
    n = len(sums)

    def copies(ins, outs, scr):
        send, recv = scr
        x, y, c = _place()
        return [pltpu.make_async_remote_copy(
            src_ref=ins[w].at[k], dst_ref=outs[2 * w + k], send_sem=send.at[w, k], recv_sem=recv.at[w, k],
            device_id=(x ^ c, y ^ (1 - c), c), device_id_type=MESH) for w in range(n) for k in range(2)]

    def start(ins, outs, scr):
        for cp in copies(ins, outs, scr):
            cp.start()

    def finish(ins, outs, scr):
        for cp in copies(ins, outs, scr):
            cp.wait()

    return _Carry(sums, [jax.ShapeDtypeStruct(s.shape[1:], s.dtype) for s in sums for _ in range(2)],
                  [pltpu.SemaphoreType.DMA((n, 2)), pltpu.SemaphoreType.DMA((n, 2))], start, finish)


def _to_other_carry(blocks):
    n = len(blocks)

    def copies(ins, outs, scr):
        send, recv = scr
        x, y, c = _place()
        return [pltpu.make_async_remote_copy(
            src_ref=ins[w], dst_ref=outs[w], send_sem=send.at[w], recv_sem=recv.at[w],
            device_id=(x ^ (1 - c), y ^ c, c), device_id_type=MESH) for w in range(n)]

    def start(ins, outs, scr):
        for cp in copies(ins, outs, scr):
            cp.start()

    def finish(ins, outs, scr):
        for cp in copies(ins, outs, scr):
            cp.wait()

    return _Carry(blocks, [jax.ShapeDtypeStruct(b.shape, b.dtype) for b in blocks],
                  [pltpu.SemaphoreType.DMA((n,)), pltpu.SemaphoreType.DMA((n,))], start, finish)


def _gather_small(packed):
    R, C = packed.shape

    def body(in_ref, out_ref, send, recv):
        x, y, c = _place()
        mine = 4 * x + 2 * y + c
        out_ref[mine] = in_ref[...]
        copies = []
        for k in range(1, N_DEV):
            peer = (x ^ (k >> 2), y ^ ((k >> 1) & 1), c ^ (k & 1))
            copies.append(pltpu.make_async_remote_copy(
                src_ref=in_ref, dst_ref=out_ref.at[mine], send_sem=send.at[k - 1], recv_sem=recv.at[k - 1],
                device_id=peer, device_id_type=MESH))
        for cp in copies:
            cp.start()
        for cp in copies:
            cp.wait()

    vmem = pl.BlockSpec(memory_space=pltpu.VMEM)
    return pl.pallas_call(
        body, out_shape=jax.ShapeDtypeStruct((N_DEV, R, C), F32), in_specs=[vmem], out_specs=vmem,
        scratch_shapes=[pltpu.SemaphoreType.DMA((N_DEV - 1,)), pltpu.SemaphoreType.DMA((N_DEV - 1,))],
        name="small_grads_all_gather")(packed)


def _adamw(w, g, m, v):
    m = ADAM_B1 * m + (1.0 - ADAM_B1) * g
    v = ADAM_B2 * v + (1.0 - ADAM_B2) * (g * g)
    m_hat = m / (1.0 - ADAM_B1 ** ADAM_STEP)
    v_hat = v / (1.0 - ADAM_B2 ** ADAM_STEP)
    delta = -ADAM_LR * (m_hat / (jnp.sqrt(v_hat) + ADAM_EPS) + ADAM_WD * w)
    return delta, m, v


def _pair_add(name, grads, received, slots):
    _, R, C = grads.shape
    tr = _row_tile(R, 2 * ROW_TILE_CAP)

    def body(s_ref, g_ref, r_ref, o_ref):
        o_ref[...] = (g_ref[...].astype(F32) + r_ref[...].astype(F32)).astype(BF16)

    return pl.pallas_call(
        body, out_shape=jax.ShapeDtypeStruct((3, R, C), BF16),
        grid_spec=pltpu.PrefetchScalarGridSpec(
            num_scalar_prefetch=1, grid=(3, R // tr),
            in_specs=[pl.BlockSpec((None, tr, C), lambda k, i, s: (s[k], i, 0)), pl.BlockSpec((None, tr, C), lambda k, i, s: (s[3 + k], i, 0))],
            out_specs=pl.BlockSpec((None, tr, C), lambda k, i, s: (k, i, 0))),
        name=name, compiler_params=_params(2))(slots, grads, received)


def _relay_add(name, sums, relayed):
    _, R, C = sums.shape
    tr = _row_tile(R, 2 * ROW_TILE_CAP)

    def body(s_ref, r_ref, o_ref):
        o_ref[...] = (s_ref[...].astype(F32) + r_ref[...].astype(F32)).astype(BF16)

    blk = pl.BlockSpec((tr, C), lambda i: (i, 0))
    return pl.pallas_call(
        body, out_shape=jax.ShapeDtypeStruct((R, C), BF16), grid=(R // tr,),
        in_specs=[pl.BlockSpec((None, tr, C), lambda i: (2, i, 0)), blk], out_specs=blk,
        name=name, compiler_params=_params(1))(sums, relayed)


def _adam_shard(name, grads, from_sibling, received, w, m, v, own):
    R, C = w.shape
    tr = _row_tile(R, 256)
    tc = C if tr < R or C % (4 * LANES) else 4 * LANES

    def body(o_ref, g_ref, s_ref, ra_ref, rb_ref, w_ref, m_ref, v_ref, g_out, d_out, m_out, v_out):
        g = (g_ref[...].astype(F32) + s_ref[...].astype(F32)) + ra_ref[...].astype(F32) + rb_ref[...].astype(F32)
        delta, mn, vn = _adamw(w_ref[...], g, m_ref[...], v_ref[...])
        g_out[...] = g
        d_out[...] = delta
        m_out[...] = mn
        v_out[...] = vn

    blk = pl.BlockSpec((tr, tc), lambda i, j, o: (i, j))
    shape = jax.ShapeDtypeStruct((R, C), F32)
    return pl.pallas_call(
        body, out_shape=(shape,) * 4,
        grid_spec=pltpu.PrefetchScalarGridSpec(
            num_scalar_prefetch=1, grid=(R // tr, C // tc),
            in_specs=[pl.BlockSpec((None, tr, tc), lambda i, j, o: (o[0], i, j)), pl.BlockSpec((None, tr, tc), lambda i, j, o: (o[1], i, j)),
                      blk, blk, blk, blk, blk],
            out_specs=(blk,) * 4),
        name=name, compiler_params=_params(2))(own, grads, from_sibling, received[0], received[1], w, m, v)


def _adam_small(name, gathered, w, m, v):
    R, C = w.shape

    def body(ga_ref, w_ref, m_ref, v_ref, g_out, d_out, m_out, v_out):
        g = ga_ref[0]
        for d in range(1, N_DEV):
            g = g + ga_ref[d]
        delta, mn, vn = _adamw(w_ref[...], g, m_ref[...], v_ref[...])
        g_out[...] = g
        d_out[...] = delta
        m_out[...] = mn
        v_out[...] = vn

    full = pl.BlockSpec((R, C), lambda i: (0, 0))
    shape = jax.ShapeDtypeStruct((R, C), F32)
    return pl.pallas_call(
        body, out_shape=(shape,) * 4, grid=(1,),
        in_specs=[pl.BlockSpec((N_DEV, R, C), lambda i: (0, 0, 0)), full, full, full], out_specs=(full,) * 4,
        name=name, compiler_params=_params(1))(gathered, w, m, v)


def _pack_small(parts, D, scalar=None):
    g1, gmix, g2, gof, gos, bf, gqf, gkf, gqs, gks, sinks = [p.reshape(-1).astype(F32) for p in parts]
    row3 = jnp.concatenate([gof, gos])
    row4 = jnp.zeros((D,), F32)
    for slot, vec in enumerate((bf, gqf, gkf, gqs, gks, sinks)):
        row4 = lax.dynamic_update_slice(row4, vec, (slot * LANES,))
    zero = jnp.zeros((D,), F32)
    row5 = zero if scalar is None else lax.dynamic_update_slice(zero, jnp.reshape(scalar, (1,)).astype(F32), (0,))
    return jnp.stack([g1, gmix, g2, row3, row4, row5, zero, zero])


def _unpack_small(packed, D, H):
    Dh = D // 2
    row4 = packed[4]
    short = [row4[s * LANES:s * LANES + n] for s, n in enumerate((H, HEAD_DIM, HEAD_DIM, HEAD_DIM, HEAD_DIM, H))]
    vecs = [packed[0], packed[1], packed[2], packed[3, :Dh], packed[3, Dh:]] + short
    return [v[None, :] for v in vecs]


def kernel(x, positions, norm_ffn1_g, ffn1_w_gate, ffn1_w_up, ffn1_w_down, norm_mix_g, w_in, b_forget, fox_q_norm_g, fox_k_norm_g, swa_q_norm_g, swa_k_norm_g, swa_sinks, out_norm_fox_g, out_norm_swa_g, w_out, norm_ffn2_g, ffn2_w_gate, ffn2_w_up, ffn2_w_down, loss_target, m_norm_ffn1_g, m_ffn1_w_gate, m_ffn1_w_up, m_ffn1_w_down, m_norm_mix_g, m_w_in, m_b_forget, m_fox_q_norm_g, m_fox_k_norm_g, m_swa_q_norm_g, m_swa_k_norm_g, m_swa_sinks, m_out_norm_fox_g, m_out_norm_swa_g, m_w_out, m_norm_ffn2_g, m_ffn2_w_gate, m_ffn2_w_up, m_ffn2_w_down, v_norm_ffn1_g, v_ffn1_w_gate, v_ffn1_w_up, v_ffn1_w_down, v_norm_mix_g, v_w_in, v_b_forget, v_fox_q_norm_g, v_fox_k_norm_g, v_swa_q_norm_g, v_swa_k_norm_g, v_swa_sinks, v_out_norm_fox_g, v_out_norm_swa_g, v_w_out, v_norm_ffn2_g, v_ffn2_w_gate, v_ffn2_w_up, v_ffn2_w_down):
    xs = x[0]
    target = loss_target[0]
    T, D = xs.shape
    Dh = D // 2
    H = Dh // HEAD_DIM
    HP = H // 2
    KVW = (H // GQA_GROUP) * HEAD_DIM
    KVB = KVW // LANES
    MAIN = 4 * Dh + 2 * KVW
    F_OFF = 3 * Dh
    tm = min(ROW_TILE_CAP, T)
    tm2 = min(2 * ROW_TILE_CAP, T)
    tq = min(512, T)
    tk = min(512, T)
    nk = T // tk
    cx, cy, cc = _place()
    near = [2 * (cx ^ cc) + (cy ^ (1 - cc)), 2 * (1 - cx) + (1 - cy), 2 * (cx ^ (1 - cc)) + (cy ^ cc)]
    slots = jnp.stack([2 * q + cc for q in near] + near).astype(jnp.int32)
    own = jnp.stack([4 * cx + 2 * cy + cc, 2 * cx + cy]).astype(jnp.int32)

    tr = jnp.transpose
    big_w = [tr(ffn1_w_gate[0]), tr(ffn1_w_up[0]), ffn1_w_down[0], tr(w_in[0]), w_out[0], tr(ffn2_w_gate[0]), tr(ffn2_w_up[0]),
             ffn2_w_down[0]]
    big_m = [tr(m_ffn1_w_gate[0]), tr(m_ffn1_w_up[0]), m_ffn1_w_down[0], tr(m_w_in[0]), m_w_out[0], tr(m_ffn2_w_gate[0]),
             tr(m_ffn2_w_up[0]), m_ffn2_w_down[0]]
    big_v = [tr(v_ffn1_w_gate[0]), tr(v_ffn1_w_up[0]), v_ffn1_w_down[0], tr(v_w_in[0]), v_w_out[0], tr(v_ffn2_w_gate[0]),
             tr(v_ffn2_w_up[0]), v_ffn2_w_down[0]]
    transposed = {"ffn1_w_gate", "ffn1_w_up", "w_in", "ffn2_w_gate", "ffn2_w_up"}
    names = ["ffn1_w_gate", "ffn1_w_up", "ffn1_w_down", "w_in", "w_out", "ffn2_w_gate", "ffn2_w_up", "ffn2_w_down"]
    sh = dict(zip(names, [w.astype(BF16) for w in big_w]))
    lane = jnp.arange(LANES)
    inv_freq = ROPE_THETA ** (-(2.0 * (lane % (HEAD_DIM // 2))).astype(F32) / HEAD_DIM)
    ang = positions[0].astype(F32)[:, None] * inv_freq[None, :]
    cos_t = jnp.cos(ang)
    sin_t = jnp.where((lane & (HEAD_DIM // 2)) == 0, -1.0, 1.0)[None, :] * jnp.sin(ang)
    rope = (cos_t, sin_t)

    def pair_gain(g, blocks):
        return jnp.tile(jnp.concatenate([g[0], g[0]])[None, None, :], (blocks, 1, 1))

    n1, (wg1,) = _rmsnorm_fwd("ffn1_norm", xs, norm_ffn1_g, tm, carry=_gather_carry([sh["ffn1_w_gate"]]))
    a1, (wu1,) = _ffn_gate("ffn1_gate", n1, wg1, tm, carry=_gather_carry([sh["ffn1_w_up"]]))
    (b1, hm1), (wd1,) = _ffn_up_only("ffn1_up", n1, wu1, a1, tm, carry=_gather_carry([sh["ffn1_w_down"]]))
    h1, (win_g,) = _ffn_down("ffn1_down", hm1, wd1, xs, tm, D, carry=_gather_carry([sh["w_in"]]))
    n_in = win_g.shape[1]
    win_t = win_g.reshape(N_DEV * n_in, D)
    win_main = jnp.concatenate([win_t[:F_OFF], win_t[F_OFF + H:]], axis=0)
    win_f = jnp.pad(win_t[F_OFF:F_OFF + H], ((0, LANES - H), (0, 0)))

    u = _rmsnorm_fwd("mix_norm", h1, norm_mix_g, tm)
    proj, (wout_g,) = _mm("mix_proj", u, win_main, T, MAIN // 9, dims=NT, carry=_gather_carry([sh["w_out"]]))
    wout = wout_g.reshape(D, D)
    proj_f = _mm("mix_proj_forget", u, win_f, T, LANES, dims=NT)
    scale = HEAD_DIM ** -0.5
    fox_gains = jnp.concatenate([pair_gain(fox_q_norm_g, HP), pair_gain(fox_k_norm_g, HP)])
    qk_f = _headnorm_fwd_scaled("fox_qk_norm", proj, 0, 2 * HP, fox_gains, T, scale, HP)
    v_f = proj[:, 2 * Dh:3 * Dh].astype(BF16)
    c_t, sg_t = _forget_fwd("forget_gates", proj_f[:, :H].T, b_forget.reshape(H, 1))
    crow = c_t.reshape(H, nk, 1, tk)
    (o_fox, lse_fa, lse_fb), (wg2, wu2) = _fox_fwd("fox_attention", qk_f, v_f, crow, tq, tk, min(FOX_STRIP_FWD, tq),
                                                   carry=_gather_carry([sh["ffn2_w_gate"], sh["ffn2_w_up"]]))

    swa_q_gains = pair_gain(swa_q_norm_g, HP)
    swa_k_gains = pair_gain(swa_k_norm_g, KVB)
    q_s = _headnorm_fwd("swa_q_norm", proj, 3 * HP, HP, swa_q_gains, T, scale, rope=rope)
    k_d = _headnorm_fwd("swa_k_norm", proj, 4 * HP, KVB, swa_k_gains, T, 1.0, rope=rope, dup=True)
    v_s = proj[:, 4 * Dh + KVW:].astype(BF16).reshape(T, H // GQA_GROUP, 1, HEAD_DIM)
    v_d = jnp.broadcast_to(v_s, (T, H // GQA_GROUP, 2, HEAD_DIM)).reshape(T, 2 * KVW)
    sinks3 = swa_sinks.reshape(H, 1, 1)
    o_swa, lse_sa, lse_sb = _swa_fwd("swa_attention", q_s, k_d, v_d, sinks3)

    on = _outnorm_fwd("out_norm", o_fox, o_swa, out_norm_fox_g, out_norm_swa_g, tm)
    h2 = _mm("mix_out", on, wout, tm2, min(512, D), resid=h1)

    n2 = _rmsnorm_fwd("ffn2_norm", h2, norm_ffn2_g, tm)
    (a2, b2, hm2), (wd2,) = _ffn_up("ffn2_up", n2, wg2, wu2, tm2, carry=_gather_carry([sh["ffn2_w_down"]]))
    y = _ffn_down("ffn2_down", hm2, wd2, h2, tm2, D)
    dy, dyh, sq = _loss_grad("loss_grad", y, target, tm)
    loss_part = 0.5 * sq[0, 0] / D

    J, Fs, _ = wg2.shape
    aspec = pl.BlockSpec((None, tm, Fs), lambda i, j: (j, i, 0))
    wspec = pl.BlockSpec((None, Fs, D), lambda i, j: (j, 0, 0))
    got = {}
    local = {}

    def pair_sums(keys, grads, received):
        for nm, g, r in zip(keys, grads, received):
            local[nm] = (g, r)
        return [_pair_add("sum_" + nm, g, r, slots) for nm, g, r in zip(keys, grads, received)]

    def relay_sums(keys, sums, hop1):
        out = []
        for i, (nm, s) in enumerate(zip(keys, sums)):
            got[nm] = [hop1[2 * i]]
            out.append(_relay_add("relay_" + nm, s, hop1[2 * i + 1]))
        return out

    def arrived(keys, hop2):
        for nm, blk in zip(keys, hop2):
            got[nm].append(blk)

    dwd2 = _wgrad_down("ffn2_wgrad_down", hm2, dyh, D)
    (da2, db2), (sib_d2,) = _ffn_bwd_mid("ffn2_bwd_mid", dyh, wd2, a2, b2, tm2, carry=_sibling_carry([dwd2]))
    (sum_wd2,) = pair_sums(names[7:8], [dwd2], [sib_d2])
    (dwg2, dwu2), hop1 = _wgrad_up("ffn2_wgrad_up", n2, da2, db2, min(1024, D), carry=_to_partner_carry([sum_wd2]))
    (t_wd2,) = relay_sums(names[7:8], [sum_wd2], hop1)
    aspec2 = pl.BlockSpec((None, tm2, Fs), lambda i, j: (j, i, 0))
    dn2, (via_wd2, *sib2) = _reduce_mm("ffn2_bwd_in", [(da2, aspec2, wg2, wspec), (db2, aspec2, wu2, wspec)], [], NN, T, D, tm2, J,
                                       carry=_join(_to_other_carry([t_wd2]), _sibling_carry([dwg2, dwu2])))
    arrived(names[7:8], [via_wd2])
    dh2, dg_ffn2, dh2b = _rmsnorm_bwd("ffn2_norm_bwd", dn2, h2, norm_ffn2_g, dy, min(256, T), 1.0)
    sum_wg2, sum_wu2 = pair_sums(names[5:7], [dwg2, dwu2], sib2)

    dwout = _wgrad_2d("mix_out_wgrad", on, dh2b, min(512, D), D)
    dwout_g = dwout.reshape(N_DEV, D // N_DEV, D)
    do_fox, dg_of = _outnorm_bwd("out_norm_bwd_fox", dh2b, wout, 0, o_fox, out_norm_fox_g, tm)
    do_swa, dg_os = _outnorm_bwd("out_norm_bwd_swa", dh2b, wout, 1, o_swa, out_norm_swa_g, tm)

    (dq_f, dk_f, dv_f, dc_a, dc_b, dr_a, dr_b), (*hop1, sib_wout) = _fox_bwd(
        "fox_attention_bwd", qk_f, v_f, o_fox, do_fox, crow, lse_fa, lse_fb, tq, tk, min(FOX_STRIP_BWD, tq),
        carry=_join(_to_partner_carry([sum_wg2, sum_wu2]), _sibling_carry([dwout_g])))
    t_wg2, t_wu2 = relay_sums(names[5:7], [sum_wg2, sum_wu2], hop1)
    (sum_wout,) = pair_sums(names[4:5], [dwout_g], [sib_wout])
    dqf_raw, dg_fq = _headnorm_bwd("fox_q_norm_bwd", dq_f, proj, 0, HP, fox_gains[:HP], HP, T, 1.0)
    dkf_raw, dg_fk = _headnorm_bwd("fox_k_norm_bwd", dk_f, proj, HP, HP, fox_gains[HP:], HP, T, 1.0)
    dct = jnp.stack([dc_a.reshape(HP, T), dc_b.reshape(HP, T)], axis=1).reshape(H, T)
    drt = jnp.stack([dr_a.reshape(HP, T), dr_b.reshape(HP, T)], axis=1).reshape(H, T)
    dz_t, db_f = _forget_bwd("forget_gates_bwd", dct, drt, sg_t)

    (dq_s, dk_p, dv_p, dsink_a, dsink_b), hop2 = _swa_bwd(
        "swa_attention_bwd", q_s, k_d, v_d, sinks3, o_swa, do_swa, lse_sa, lse_sb, carry=_to_other_carry([t_wg2, t_wu2]))
    arrived(names[5:7], hop2)
    dqs_raw, dg_sq = _headnorm_bwd("swa_q_norm_bwd", dq_s, proj, 3 * HP, HP, swa_q_gains, HP, T, 1.0, rope=rope)
    dks_raw, dg_sk = _headnorm_bwd("swa_k_norm_bwd", dk_p, proj, 4 * HP, KVB, swa_k_gains, KVB, T, 1.0, rope=rope, fold=True)
    dvs_raw, _ = _headnorm_bwd("swa_v_fold", dv_p, None, 0, KVB, None, KVB, T, 1.0, fold=True, norm=False)

    dproj = jnp.concatenate([dqf_raw, dkf_raw, dv_f.astype(BF16), dqs_raw, dks_raw, dvs_raw], axis=1)
    dproj_f = jnp.pad(dz_t.T, ((0, 0), (0, LANES - H))).astype(BF16)
    dwin_main, hop1 = _wgrad_2d("mix_proj_wgrad", dproj, u, MAIN // 9, D, carry=_to_partner_carry([sum_wout]))
    (t_wout,) = relay_sums(names[4:5], [sum_wout], hop1)
    dwin_f = _wgrad_2d("mix_proj_forget_wgrad", dproj_f, u, LANES, min(1024, D))
    dwin_t = jnp.concatenate([dwin_main[:F_OFF], dwin_f[:H], dwin_main[F_OFF:]], axis=0)
    dwin_g = dwin_t.reshape(N_DEV, n_in, D)
    tkb = MAIN // 9
    du, (via_wout, sib_win) = _reduce_mm(
        "mix_bwd_in",
        [(dproj, pl.BlockSpec((tm2, tkb), lambda i, r: (i, r)), win_main, pl.BlockSpec((tkb, D), lambda i, r: (r, 0)))],
        [(dproj_f, pl.BlockSpec((tm2, LANES), lambda i, r: (i, 0)), win_f, pl.BlockSpec((LANES, D), lambda i, r: (0, 0)))],
        NN, T, D, tm2, 9, carry=_join(_to_other_carry([t_wout]), _sibling_carry([dwin_g])))
    arrived(names[4:5], [via_wout])
    dh1, dg_mix, dh1h = _rmsnorm_bwd("mix_norm_bwd", du, h1, norm_mix_g, dh2, min(256, T), 0.5)
    (sum_win,) = pair_sums(names[3:4], [dwin_g], [sib_win])

    dwd1, hop1 = _wgrad_down("ffn1_wgrad_down", hm1, dh1h, D, carry=_to_partner_carry([sum_win]))
    (t_win,) = relay_sums(names[3:4], [sum_win], hop1)
    (da1, db1), (via_win, sib_d) = _ffn_bwd_mid("ffn1_bwd_mid", dh1h, wd1, a1, b1, tm2,
                                                carry=_join(_to_other_carry([t_win]), _sibling_carry([dwd1])))
    arrived(names[3:4], [via_win])
    (sum_wd1,) = pair_sums(names[2:3], [dwd1], [sib_d])
    dwg1, hop1 = _wgrad_down("ffn1_wgrad_gate", da1, n1, D, carry=_to_partner_carry([sum_wd1]))
    (t_wd1,) = relay_sums(names[2:3], [sum_wd1], hop1)
    dwu1, (via_wd1, sib_g) = _wgrad_down("ffn1_wgrad_up", db1, n1, D,
                                         carry=_join(_to_other_carry([t_wd1]), _sibling_carry([dwg1])))
    arrived(names[2:3], [via_wd1])
    (sum_wg1,) = pair_sums(names[0:1], [dwg1], [sib_g])
    dn1_gate, (*hop1, sib_u) = _reduce_mm(
        "ffn1_bwd_in_gate", [(da1, aspec2, wg1, wspec)], [], NN, T, D, tm2, J,
        carry=_join(_to_partner_carry([sum_wg1]), _sibling_carry([dwu1])))
    (t_wg1,) = relay_sums(names[0:1], [sum_wg1], hop1)
    (sum_wu1,) = pair_sums(names[1:2], [dwu1], [sib_u])
    dn1, (via_wg1, *hop1) = _reduce_mm(
        "ffn1_bwd_in_up", [(db1, aspec2, wu1, wspec)], [], NN, T, D, tm2, J, init=dn1_gate,
        carry=_join(_to_other_carry([t_wg1]), _to_partner_carry([sum_wu1])))
    arrived(names[0:1], [via_wg1])
    (t_wu1,) = relay_sums(names[1:2], [sum_wu1], hop1)
    arrived(names[1:2], _run_carry("grads_exchange", _to_other_carry([t_wu1])))
    dx, dg_ffn1 = _rmsnorm_bwd("ffn1_norm_bwd", dn1, xs, norm_ffn1_g, dh1, min(256, T), None)

    big_out = [_adam_shard("adam_" + nm, local[nm][0], local[nm][1], got[nm], w, m, v, own)
               for nm, w, m, v in zip(names, big_w, big_m, big_v)]

    dsinks = jnp.stack([dsink_a.reshape(HP), dsink_b.reshape(HP)], axis=1).reshape(H)
    small_g = [dg_ffn1, dg_mix, dg_ffn2, dg_of, dg_os, db_f, dg_fq[0, 0, :HEAD_DIM], dg_fk[0, 0, :HEAD_DIM],
               dg_sq[0, 0, :HEAD_DIM], dg_sk[0, 0, :HEAD_DIM], dsinks]
    small_w = [norm_ffn1_g, norm_mix_g, norm_ffn2_g, out_norm_fox_g, out_norm_swa_g, b_forget, fox_q_norm_g, fox_k_norm_g,
               swa_q_norm_g, swa_k_norm_g, swa_sinks]
    small_m = [m_norm_ffn1_g, m_norm_mix_g, m_norm_ffn2_g, m_out_norm_fox_g, m_out_norm_swa_g, m_b_forget, m_fox_q_norm_g,
               m_fox_k_norm_g, m_swa_q_norm_g, m_swa_k_norm_g, m_swa_sinks]
    small_v = [v_norm_ffn1_g, v_norm_mix_g, v_norm_ffn2_g, v_out_norm_fox_g, v_out_norm_swa_g, v_b_forget, v_fox_q_norm_g,
               v_fox_k_norm_g, v_swa_q_norm_g, v_swa_k_norm_g, v_swa_sinks]
    gathered = _gather_small(_pack_small(small_g, D, loss_part))
    small_out = _adam_small("adam_small", gathered, _pack_small(small_w, D), _pack_small(small_m, D), _pack_small(small_v, D))
    loss = small_out[0][5, 0]
    small_out = [_unpack_small(p, D, H) for p in small_out]

    order = ["norm_ffn1_g", "ffn1_w_gate", "ffn1_w_up", "ffn1_w_down", "norm_mix_g", "w_in", "b_forget", "fox_q_norm_g", "fox_k_norm_g",
             "swa_q_norm_g", "swa_k_norm_g", "swa_sinks", "out_norm_fox_g", "out_norm_swa_g", "w_out", "norm_ffn2_g",
             "ffn2_w_gate", "ffn2_w_up", "ffn2_w_down"]
    small_names = ["norm_ffn1_g", "norm_mix_g", "norm_ffn2_g", "out_norm_fox_g", "out_norm_swa_g", "b_forget", "fox_q_norm_g",
                   "fox_k_norm_g", "swa_q_norm_g", "swa_k_norm_g", "swa_sinks"]
    result = [loss, dx[None]]
    for kind in range(4):
        for nm in order:
            if nm in names:
                leaf = big_out[names.index(nm)][kind]
                result.append((tr(leaf) if nm in transposed else leaf)[None])
            else:
                result.append(small_out[kind][small_names.index(nm)])
    return tuple(result)


def _headnorm_fwd_scaled(name, proj, col_off, ncb, gains, tm, scale, n_scaled):
    T = proj.shape[0]

    def body(x_ref, g_ref, o_ref):
        xv = x_ref[...]
        lo = _lane_lo(xv.shape)
        y = xv * _head_rstd(xv, lo) * g_ref[...]
        y = y * jnp.where(pl.program_id(0) < n_scaled, scale, 1.0)
        o_ref[...] = y.astype(BF16)

    return pl.pallas_call(
        body, out_shape=jax.ShapeDtypeStruct((T, ncb * LANES), BF16), grid=(ncb, T // tm),
        in_specs=[pl.BlockSpec((tm, LANES), lambda c, i: (i, col_off + c)), pl.BlockSpec((None, 1, LANES), lambda c, i: (c, 0, 0))],
        out_specs=pl.BlockSpec((tm, LANES), lambda c, i: (i, c)), name=name, compiler_params=_params(2))(proj, gains)
```

```python
import jax
import jax.numpy as jnp
from jax import lax
from jax.experimental import pallas as pl
from jax.experimental.pallas import tpu as pltpu

F32 = jnp.float32
BF16 = jnp.bfloat16

HEAD_DIM = 64
LANES = 128
WINDOW = 128
GQA_GROUP = 4
EPS = 1e-6
ROPE_THETA = 10000.0
ADAM_LR = 0.001
ADAM_B1 = 0.9
ADAM_B2 = 0.999
ADAM_EPS = 1e-08
ADAM_WD = 0.01
ADAM_STEP = 10
N_DEV = 8
NEG = -1e30
VMEM_LIMIT_V7X = 48 * 1024 * 1024
ROW_TILE_CAP = 512
MESH = pl.DeviceIdType.MESH

NN = (((1,), (0,)), ((), ()))
NT = (((1,), (1,)), ((), ()))
TN = (((0,), (0,)), ((), ()))


def _dot(a, b, dims):
    return lax.dot_general(a, b, dims, preferred_element_type=F32)


def _params(n_axes):
    return pltpu.CompilerParams(dimension_semantics=("arbitrary",) * n_axes, vmem_limit_bytes=VMEM_LIMIT_V7X)


def _row_tile(rows, cap=ROW_TILE_CAP):
    best = None
    for t in range(16, min(rows, cap) + 1, 16):
        if rows % t == 0:
            best = t
    return best or rows


def _lane_lo(shape):
    return lax.broadcasted_iota(jnp.int32, shape, len(shape) - 1) < HEAD_DIM


def _keep(sel, x):
    return jnp.where(sel, x.astype(F32), 0.0).astype(BF16)


_HBM = pl.BlockSpec(memory_space=pltpu.HBM)


class _Carry:
    def __init__(self, inputs, out_shapes, scratch, start, finish, middle=None):
        self.inputs, self.out_shapes, self.scratch = list(inputs), list(out_shapes), list(scratch)
        self.start, self.finish, self.middle = start, finish, middle or (lambda ins, outs, scr: None)


def _join(*carries):
    def hook(which):
        def run(ins, outs, scr):
            i = o = s = 0
            for c in carries:
                ni, no, ns = len(c.inputs), len(c.out_shapes), len(c.scratch)
                getattr(c, which)(ins[i:i + ni], outs[o:o + no], scr[s:s + ns])
                i, o, s = i + ni, o + no, s + ns
        return run

    return _Carry([a for c in carries for a in c.inputs], [a for c in carries for a in c.out_shapes],
                  [a for c in carries for a in c.scratch], hook("start"), hook("finish"), hook("middle"))


def _call(body, *, name, grid, in_specs, out_specs, out_shape, args, scratch_shapes=(), carry=None):
    params = _params(len(grid))
    if carry is None:
        return pl.pallas_call(body, out_shape=out_shape, grid=grid, in_specs=list(in_specs), out_specs=out_specs,
                              scratch_shapes=list(scratch_shapes), name=name, compiler_params=params)(*args)
    single = not isinstance(out_shape, (tuple, list))
    shapes = (out_shape,) if single else tuple(out_shape)
    specs = (out_specs,) if single else tuple(out_specs)
    n_in, n_out, n_scr = len(args), len(shapes), len(scratch_shapes)
    c_in, c_out = len(carry.inputs), len(carry.out_shapes)

    def wrapped(*refs):
        ins, c_ins = refs[:n_in], refs[n_in:n_in + c_in]
        o0 = n_in + c_in
        outs, c_outs = refs[o0:o0 + n_out], refs[o0 + n_out:o0 + n_out + c_out]
        s0 = o0 + n_out + c_out
        scr, c_scr = refs[s0:s0 + n_scr], refs[s0 + n_scr:]
        step, total = pl.program_id(0), grid[0]
        for ax in range(1, len(grid)):
            step, total = step * grid[ax] + pl.program_id(ax), total * grid[ax]

        @pl.when(step == 0)
        def _():
            carry.start(c_ins, c_outs, c_scr)

        @pl.when(step == total // 2)
        def _():
            carry.middle(c_ins, c_outs, c_scr)

        body(*ins, *outs, *scr)

        @pl.when(step == total - 1)
        def _():
            carry.finish(c_ins, c_outs, c_scr)

    res = pl.pallas_call(
        wrapped, out_shape=shapes + tuple(carry.out_shapes), grid=grid, in_specs=list(in_specs) + [_HBM] * c_in,
        out_specs=specs + (_HBM,) * c_out, scratch_shapes=list(scratch_shapes) + carry.scratch, name=name,
        compiler_params=params)(*args, *carry.inputs)
    main = res[:n_out]
    return (main[0] if single else tuple(main)), tuple(res[n_out:])


def _rms_bwd(dn, x, g):
    r = lax.rsqrt(jnp.mean(x * x, axis=-1, keepdims=True) + EPS)
    xh = x * r
    dxh = dn * g
    dx = r * (dxh - xh * jnp.mean(dxh * xh, axis=-1, keepdims=True))
    return dx, jnp.sum(dn * xh, axis=0, keepdims=True)


def _rmsnorm_fwd(name, x, g, tm, carry=None):
    T, D = x.shape

    def body(x_ref, g_ref, o_ref):
        xf = x_ref[...]
        r = lax.rsqrt(jnp.mean(xf * xf, axis=-1, keepdims=True) + EPS)
        o_ref[...] = (xf * r * g_ref[...]).astype(BF16)

    return _call(
        body, name=name, grid=(T // tm,), out_shape=jax.ShapeDtypeStruct((T, D), BF16),
        in_specs=[pl.BlockSpec((tm, D), lambda i: (i, 0)), pl.BlockSpec((1, D), lambda i: (0, 0))],
        out_specs=pl.BlockSpec((tm, D), lambda i: (i, 0)), args=[x, g], carry=carry)


def _outnorm_fwd(name, o_fox, o_swa, g_fox, g_swa, tm):
    T, Dh = o_fox.shape

    def body(a_ref, b_ref, ga_ref, gb_ref, o_ref):
        for ref, g_ref, lo in ((a_ref, ga_ref, 0), (b_ref, gb_ref, Dh)):
            xf = ref[...]
            r = lax.rsqrt(jnp.mean(xf * xf, axis=-1, keepdims=True) + EPS)
            o_ref[:, lo:lo + Dh] = (xf * r * g_ref[...]).astype(BF16)

    row = pl.BlockSpec((tm, Dh), lambda i: (i, 0))
    gain = pl.BlockSpec((1, Dh), lambda i: (0, 0))
    return pl.pallas_call(
        body, out_shape=jax.ShapeDtypeStruct((T, 2 * Dh), BF16), grid=(T // tm,),
        in_specs=[row, row, gain, gain], out_specs=pl.BlockSpec((tm, 2 * Dh), lambda i: (i, 0)),
        name=name, compiler_params=_params(1))(o_fox, o_swa, g_fox, g_swa)


def _outnorm_bwd(name, dhb, wout, half, o, g, tm):
    T, D = dhb.shape
    Dh = o.shape[1]

    def body(a_ref, w_ref, o_ref, g_ref, do_ref, dg_ref):
        don = _dot(a_ref[...], w_ref[...], NT)
        dx, dg = _rms_bwd(don, o_ref[...], g_ref[...])
        do_ref[...] = dx.astype(BF16)

        @pl.when(pl.program_id(0) == 0)
        def _():
            dg_ref[...] = dg

        @pl.when(pl.program_id(0) > 0)
        def _():
            dg_ref[...] += dg

    return pl.pallas_call(
        body, out_shape=(jax.ShapeDtypeStruct((T, Dh), BF16), jax.ShapeDtypeStruct((1, Dh), F32)), grid=(T // tm,),
        in_specs=[pl.BlockSpec((tm, D), lambda i: (i, 0)), pl.BlockSpec((Dh, D), lambda i: (half, 0)),
                  pl.BlockSpec((tm, Dh), lambda i: (i, 0)), pl.BlockSpec((1, Dh), lambda i: (0, 0))],
        out_specs=(pl.BlockSpec((tm, Dh), lambda i: (i, 0)), pl.BlockSpec((1, Dh), lambda i: (0, 0))),
        name=name, compiler_params=_params(1))(dhb, wout, o, g)


def _mm(name, a, b, tm, tn, dims=NN, resid=None, carry=None, b_rows=None):
    M, K = a.shape
    transposed = dims == NT
    N = b.shape[0] if transposed else b.shape[1]
    if b_rows is not None:
        N = b_rows[0] * tn

    def body(*refs):
        if resid is None:
            a_ref, b_ref, o_ref = refs
            o_ref[...] = _dot(a_ref[...], b_ref[...], dims)
        else:
            a_ref, b_ref, r_ref, o_ref = refs
            o_ref[...] = r_ref[...] + _dot(a_ref[...], b_ref[...], dims)

    ospec = pl.BlockSpec((tm, tn), lambda n, i: (i, n))
    bspec = pl.BlockSpec((tn, K), lambda n, i: (n, 0)) if transposed else pl.BlockSpec((K, tn), lambda n, i: (0, n))
    if b_rows is not None:
        bspec = pl.BlockSpec((pl.Element(tn), pl.Element(K)), lambda n, i: (b_rows[1](n), 0))
    in_specs = [pl.BlockSpec((tm, K), lambda n, i: (i, 0)), bspec]
    args = [a, b]
    if resid is not None:
        in_specs.append(ospec)
        args.append(resid)
    return _call(body, name=name, grid=(N // tn, M // tm), in_specs=in_specs, out_specs=ospec,
                 out_shape=jax.ShapeDtypeStruct((M, N), F32), args=args, carry=carry)


def _wgrad_2d(name, a, b, tmm, tn, carry=None, out_rows=None):
    T, M = a.shape
    N = b.shape[1]

    def body(a_ref, b_ref, o_ref):
        o_ref[...] = _dot(a_ref[...], b_ref[...], TN).astype(BF16)

    out_spec = pl.BlockSpec((tmm, tn), lambda m, n: (m, n))
    if out_rows is not None:
        out_spec = pl.BlockSpec((pl.Element(tmm), pl.Element(tn)), lambda m, n: (out_rows[1](m), n * tn))
    return _call(
        body, name=name, grid=(M // tmm, N // tn), out_shape=jax.ShapeDtypeStruct((M if out_rows is None else out_rows[0], N), BF16),
        in_specs=[pl.BlockSpec((T, tmm), lambda m, n: (0, m)), pl.BlockSpec((T, tn), lambda m, n: (0, n))],
        out_specs=out_spec, args=[a, b], carry=carry)


def _wgrad_down(name, hm, df, tn, carry=None):
    J, T, Fs = hm.shape
    D = df.shape[1]

    def body(a_ref, b_ref, o_ref):
        o_ref[...] = _dot(a_ref[...], b_ref[...], TN).astype(BF16)

    return _call(
        body, name=name, grid=(J, D // tn), out_shape=jax.ShapeDtypeStruct((J, Fs, D), BF16),
        in_specs=[pl.BlockSpec((None, T, Fs), lambda j, n: (j, 0, 0)), pl.BlockSpec((T, tn), lambda j, n: (0, n))],
        out_specs=pl.BlockSpec((None, Fs, tn), lambda j, n: (j, 0, n)), args=[hm, df], carry=carry)


def _wgrad_up(name, n, da, db, tn, carry=None):
    T, D = n.shape
    J, _, Fs = da.shape

    def body(n_ref, da_ref, db_ref, og_ref, ou_ref):
        nv = n_ref[...]
        og_ref[...] = _dot(da_ref[...], nv, TN).astype(BF16)
        ou_ref[...] = _dot(db_ref[...], nv, TN).astype(BF16)

    act = pl.BlockSpec((None, T, Fs), lambda j, m: (j, 0, 0))
    out = pl.BlockSpec((None, Fs, tn), lambda j, m: (j, 0, m))
    shape = jax.ShapeDtypeStruct((J, Fs, D), BF16)
    return _call(
        body, name=name, grid=(J, D // tn), out_shape=(shape, shape),
        in_specs=[pl.BlockSpec((T, tn), lambda j, m: (0, m)), act, act], out_specs=(out, out),
        args=[n, da, db], carry=carry)


def _reduce_mm(name, pairs, once, dims, T, D, tm, steps, init=None, carry=None):
    n_pairs = len(pairs)
    n_once = len(once)
    n_mm = 2 * (n_pairs + n_once)

    def body(*refs):
        pr = refs[:2 * n_pairs]
        on = refs[2 * n_pairs:n_mm]
        o_ref = refs[-1]
        r = pl.program_id(1)

        @pl.when(r == 0)
        def _():
            o_ref[...] = jnp.zeros(o_ref.shape, F32) if init is None else refs[n_mm][...]

        for p in range(n_pairs):
            o_ref[...] += _dot(pr[2 * p][...], pr[2 * p + 1][...], dims)

        if n_once:
            @pl.when(r == steps - 1)
            def _():
                for p in range(n_once):
                    o_ref[...] += _dot(on[2 * p][...], on[2 * p + 1][...], dims)

    in_specs, args = [], []
    for a, a_spec, w, w_spec in list(pairs) + list(once):
        in_specs += [a_spec, w_spec]
        args += [a, w]
    row = pl.BlockSpec((tm, D), lambda i, r: (i, 0))
    if init is not None:
        in_specs.append(row)
        args.append(init)
    return _call(body, name=name, grid=(T // tm, steps), in_specs=in_specs, out_specs=row, out_shape=jax.ShapeDtypeStruct((T, D), F32),
                 args=args, carry=carry)


def _rmsnorm_bwd(name, dn, x, g, dh, tm, bf16_scale, carry=None):
    T, D = x.shape
    emit_bf16 = bf16_scale is not None

    def body(dn_ref, x_ref, g_ref, dh_ref, *outs):
        dxn, dg = _rms_bwd(dn_ref[...], x_ref[...], g_ref[...])
        dx = dh_ref[...] + dxn
        outs[0][...] = dx
        if emit_bf16:
            outs[2][...] = (bf16_scale * dx).astype(BF16)

        @pl.when(pl.program_id(0) == 0)
        def _():
            outs[1][...] = dg

        @pl.when(pl.program_id(0) > 0)
        def _():
            outs[1][...] += dg

    row = pl.BlockSpec((tm, D), lambda i: (i, 0))
    gain = pl.BlockSpec((1, D), lambda i: (0, 0))
    out_shape = [jax.ShapeDtypeStruct((T, D), F32), jax.ShapeDtypeStruct((1, D), F32)]
    out_specs = [row, gain]
    if emit_bf16:
        out_shape.append(jax.ShapeDtypeStruct((T, D), BF16))
        out_specs.append(row)
    return _call(body, name=name, grid=(T // tm,), in_specs=[row, row, gain, row], out_specs=tuple(out_specs),
                 out_shape=tuple(out_shape), args=[dn, x, g, dh], carry=carry)


def _loss_grad(name, y, target, tm):
    T, D = y.shape

    def body(y_ref, t_ref, dy_ref, dyh_ref, sq_ref):
        diff = y_ref[...] - t_ref[...]
        sq = jnp.sum(jnp.sum(diff * diff, axis=1, keepdims=True), axis=0, keepdims=True)
        dy = diff * (1.0 / D)
        dy_ref[...] = dy
        dyh_ref[...] = (0.5 * dy).astype(BF16)

        @pl.when(pl.program_id(0) == 0)
        def _():
            sq_ref[...] = sq

        @pl.when(pl.program_id(0) > 0)
        def _():
            sq_ref[...] += sq

    row = pl.BlockSpec((tm, D), lambda i: (i, 0))
    return pl.pallas_call(
        body, out_shape=(jax.ShapeDtypeStruct((T, D), F32), jax.ShapeDtypeStruct((T, D), BF16), jax.ShapeDtypeStruct((1, 1), F32)),
        grid=(T // tm,), in_specs=[row, row], out_specs=(row, row, pl.BlockSpec((1, 1), lambda i: (0, 0))),
        name=name, compiler_params=_params(1))(y, target)


def _ffn_up(name, n, wg, wu, tm, carry=None):
    T, D = n.shape
    J, Fs, _ = wg.shape

    def body(n_ref, wg_ref, wu_ref, a_ref, b_ref, h_ref):
        xv = n_ref[...]
        a = _dot(xv, wg_ref[...], NT)
        b = _dot(xv, wu_ref[...], NT)
        a_ref[...] = a.astype(BF16)
        b_ref[...] = b.astype(BF16)
        h_ref[...] = (a * jax.nn.sigmoid(a) * b).astype(BF16)

    act = jax.ShapeDtypeStruct((J, T, Fs), BF16)
    wspec = pl.BlockSpec((None, Fs, D), lambda j, i: (j, 0, 0))
    aspec = pl.BlockSpec((None, tm, Fs), lambda j, i: (j, i, 0))
    return _call(
        body, name=name, grid=(J, T // tm), out_shape=(act, act, act),
        in_specs=[pl.BlockSpec((tm, D), lambda j, i: (i, 0)), wspec, wspec], out_specs=(aspec, aspec, aspec),
        args=[n, wg, wu], carry=carry)


def _ffn_gate(name, n, wg, tm, carry=None):
    T, D = n.shape
    J, Fs, _ = wg.shape

    def body(n_ref, wg_ref, a_ref):
        a_ref[...] = _dot(n_ref[...], wg_ref[...], NT).astype(BF16)

    aspec = pl.BlockSpec((None, tm, Fs), lambda j, i: (j, i, 0))
    return _call(
        body, name=name, grid=(J, T // tm), out_shape=jax.ShapeDtypeStruct((J, T, Fs), BF16),
        in_specs=[pl.BlockSpec((tm, D), lambda j, i: (i, 0)), pl.BlockSpec((None, Fs, D), lambda j, i: (j, 0, 0))],
        out_specs=aspec, args=[n, wg], carry=carry)


def _ffn_up_only(name, n, wu, a, tm, carry=None):
    T, D = n.shape
    J, Fs, _ = wu.shape

    def body(n_ref, wu_ref, a_ref, b_ref, h_ref):
        b = _dot(n_ref[...], wu_ref[...], NT)
        a = a_ref[...].astype(F32)
        b_ref[...] = b.astype(BF16)
        h_ref[...] = (a * jax.nn.sigmoid(a) * b).astype(BF16)

    act = jax.ShapeDtypeStruct((J, T, Fs), BF16)
    aspec = pl.BlockSpec((None, tm, Fs), lambda j, i: (j, i, 0))
    return _call(
        body, name=name, grid=(J, T // tm), out_shape=(act, act),
        in_specs=[pl.BlockSpec((tm, D), lambda j, i: (i, 0)), pl.BlockSpec((None, Fs, D), lambda j, i: (j, 0, 0)), aspec],
        out_specs=(aspec, aspec), args=[n, wu, a], carry=carry)


def _ffn_down(name, hm, wd, resid, tm, tn, carry=None):
    J, T, Fs = hm.shape
    D = wd.shape[2]

    def body(h_ref, w_ref, r_ref, o_ref):
        @pl.when(pl.program_id(2) == 0)
        def _():
            o_ref[...] = r_ref[...]

        o_ref[...] += _dot(h_ref[...] * 0.5, w_ref[...], NN)

    tile = pl.BlockSpec((tm, tn), lambda i, n, j: (i, n))
    return _call(
        body, name=name, grid=(T // tm, D // tn, J), out_shape=jax.ShapeDtypeStruct((T, D), F32),
        in_specs=[pl.BlockSpec((None, tm, Fs), lambda i, n, j: (j, i, 0)), pl.BlockSpec((None, Fs, tn), lambda i, n, j: (j, 0, n)), tile],
        out_specs=tile, args=[hm, wd, resid], carry=carry)


def _ffn_bwd_mid(name, dfh, wd, a, b, tm, carry=None):
    T, D = dfh.shape
    J, Fs, _ = wd.shape

    def body(df_ref, w_ref, a_ref, b_ref, da_ref, db_ref):
        dhm = _dot(df_ref[...], w_ref[...], NT)
        av = a_ref[...].astype(F32)
        bv = b_ref[...].astype(F32)
        sg = jax.nn.sigmoid(av)
        da_ref[...] = (dhm * bv * (sg * (1.0 + av * (1.0 - sg)))).astype(BF16)
        db_ref[...] = (dhm * (av * sg)).astype(BF16)

    act = jax.ShapeDtypeStruct((J, T, Fs), BF16)
    aspec = pl.BlockSpec((None, tm, Fs), lambda j, i: (j, i, 0))
    return _call(
        body, name=name, grid=(J, T // tm), out_shape=(act, act),
        in_specs=[pl.BlockSpec((tm, D), lambda j, i: (i, 0)), pl.BlockSpec((None, Fs, D), lambda j, i: (j, 0, 0)), aspec, aspec],
        out_specs=(aspec, aspec), args=[dfh, wd, a, b], carry=carry)


def _rot_half(y, lane):
    first = (lane & (HEAD_DIM // 2)) == 0
    return jnp.where(first, pltpu.roll(y, LANES - HEAD_DIM // 2, 1), pltpu.roll(y, HEAD_DIM // 2, 1))


def _head_rstd(x, lo):
    sq = x * x
    ss_a = jnp.sum(jnp.where(lo, sq, 0.0), axis=-1, keepdims=True)
    ss_b = jnp.sum(jnp.where(lo, 0.0, sq), axis=-1, keepdims=True)
    return lax.rsqrt(jnp.where(lo, ss_a, ss_b) * (1.0 / HEAD_DIM) + EPS)


def _headnorm_fwd(name, proj, col_off, ncb, gains, tm, scale, rope=None, dup=False):
    T = proj.shape[0]
    with_rope = rope is not None
    width = 2 * LANES if dup else LANES

    def body(*refs):
        if with_rope:
            x_ref, g_ref, cos_ref, sin_ref, o_ref = refs
        else:
            x_ref, g_ref, o_ref = refs
        xv = x_ref[...]
        lane = lax.broadcasted_iota(jnp.int32, xv.shape, 1)
        lo = lane < HEAD_DIM
        y = xv * _head_rstd(xv, lo) * g_ref[...]
        if with_rope:
            y = y * cos_ref[...] + _rot_half(y, lane) * sin_ref[...]
        y = y * scale
        if dup:
            sw = pltpu.roll(y, HEAD_DIM, 1)
            o_ref[:, :LANES] = jnp.where(lo, y, sw).astype(BF16)
            o_ref[:, LANES:] = jnp.where(lo, sw, y).astype(BF16)
        else:
            o_ref[...] = y.astype(BF16)

    in_specs = [pl.BlockSpec((tm, LANES), lambda c, i: (i, col_off + c)), pl.BlockSpec((None, 1, LANES), lambda c, i: (c, 0, 0))]
    args = [proj, gains]
    if with_rope:
        tab = pl.BlockSpec((tm, LANES), lambda c, i: (i, 0))
        in_specs += [tab, tab]
        args += list(rope)
    return pl.pallas_call(
        body, out_shape=jax.ShapeDtypeStruct((T, ncb * width), BF16), grid=(ncb, T // tm),
        in_specs=in_specs, out_specs=pl.BlockSpec((tm, width), lambda c, i: (i, c)),
        name=name, compiler_params=_params(2))(*args)


def _headnorm_bwd(name, dy, proj, col_off, ncb, gains, group, tm, scale, rope=None, fold=False, norm=True):
    T = dy.shape[0]
    with_rope = rope is not None
    n_groups = ncb // group
    dy_width = 4 * LANES if fold else LANES

    def body(*refs):
        refs = list(refs)
        dy_ref = refs.pop(0)
        x_ref = refs.pop(0) if norm else None
        g_ref = refs.pop(0) if norm else None
        cos_ref = refs.pop(0) if with_rope else None
        sin_ref = refs.pop(0) if with_rope else None
        dx_ref = refs.pop(0)
        dg_ref = refs.pop(0) if norm else None
        c = pl.program_id(0)
        i = pl.program_id(1)
        d = dy_ref[...]
        lane = lax.broadcasted_iota(jnp.int32, (d.shape[0], LANES), 1)
        lo = lane < HEAD_DIM
        if fold:
            t0 = d[:, 0:LANES] + d[:, LANES:2 * LANES]
            t1 = d[:, 2 * LANES:3 * LANES] + d[:, 3 * LANES:4 * LANES]
            d = jnp.where(lo, t0 + pltpu.roll(t0, HEAD_DIM, 1), t1 + pltpu.roll(t1, HEAD_DIM, 1))
        d = d * scale
        if with_rope:
            d = d * cos_ref[...] + _rot_half(d * sin_ref[...], lane)
        if not norm:
            dx_ref[...] = d.astype(BF16)
            return
        xv = x_ref[...]
        gv = g_ref[...]
        r = _head_rstd(xv, lo)
        xh = xv * r
        dxh = d * gv
        pr = dxh * xh
        m_a = jnp.sum(jnp.where(lo, pr, 0.0), axis=-1, keepdims=True)
        m_b = jnp.sum(jnp.where(lo, 0.0, pr), axis=-1, keepdims=True)
        mean = jnp.where(lo, m_a, m_b) * (1.0 / HEAD_DIM)
        dx_ref[...] = (r * (dxh - xh * mean)).astype(BF16)
        dgp = jnp.sum(d * xh, axis=0, keepdims=True)
        dgp = dgp + pltpu.roll(dgp, HEAD_DIM, 1)
        first = jnp.logical_and(c % group == 0, i == 0)

        @pl.when(first)
        def _():
            dg_ref[...] = dgp

        @pl.when(jnp.logical_not(first))
        def _():
            dg_ref[...] += dgp

    in_specs = [pl.BlockSpec((tm, dy_width), lambda c, i: (i, c))]
    args = [dy]
    if norm:
        in_specs += [pl.BlockSpec((tm, LANES), lambda c, i: (i, col_off + c)), pl.BlockSpec((None, 1, LANES), lambda c, i: (c, 0, 0))]
        args += [proj, gains]
    if with_rope:
        tab = pl.BlockSpec((tm, LANES), lambda c, i: (i, 0))
        in_specs += [tab, tab]
        args += list(rope)
    out_shape = [jax.ShapeDtypeStruct((T, ncb * LANES), BF16)]
    out_specs = [pl.BlockSpec((tm, LANES), lambda c, i: (i, c))]
    if norm:
        out_shape.append(jax.ShapeDtypeStruct((n_groups, 1, LANES), F32))
        out_specs.append(pl.BlockSpec((None, 1, LANES), lambda c, i: (c // group, 0, 0)))
    res = pl.pallas_call(
        body, out_shape=tuple(out_shape), grid=(ncb, T // tm), in_specs=in_specs, out_specs=tuple(out_specs),
        name=name, compiler_params=_params(2))(*args)
    return res if norm else (res[0], None)


def _dot_exact(x, tri):
    hi = x.astype(BF16)
    r1 = x - hi.astype(F32)
    mid = r1.astype(BF16)
    lo = (r1 - mid.astype(F32)).astype(BF16)
    return _dot(hi, tri, NN) + _dot(mid, tri, NN) + _dot(lo, tri, NN)


def _forget_fwd(name, zt, bias):
    H, T = zt.shape
    blk = min(256, T)

    def body(z_ref, b_ref, c_ref, s_ref):
        z = z_ref[...] + b_ref[...]
        s_ref[...] = jax.nn.sigmoid(-z)
        lf = jnp.minimum(z, 0.0) - jnp.log(1.0 + jnp.exp(-jnp.abs(z)))
        tri = (lax.broadcasted_iota(jnp.int32, (blk, blk), 0) <= lax.broadcasted_iota(jnp.int32, (blk, blk), 1)).astype(BF16)
        carry = jnp.zeros((H, 1), F32)
        for bi in range(T // blk):
            xb = lf[:, bi * blk:(bi + 1) * blk]
            c_ref[:, bi * blk:(bi + 1) * blk] = _dot_exact(xb, tri) + carry
            carry = carry + jnp.sum(xb, axis=-1, keepdims=True)

    shape = jax.ShapeDtypeStruct((H, T), F32)
    full = pl.BlockSpec((H, T), lambda i: (0, 0))
    return pl.pallas_call(
        body, out_shape=(shape, shape), grid=(1,), in_specs=[full, pl.BlockSpec((H, 1), lambda i: (0, 0))],
        out_specs=(full, full), name=name, compiler_params=_params(1))(zt, bias)


def _forget_bwd(name, dct, drt, sgt):
    H, T = dct.shape
    blk = min(256, T)

    def body(dc_ref, dr_ref, s_ref, dz_ref, db_ref):
        dc = dc_ref[...] + dr_ref[...]
        tri = (lax.broadcasted_iota(jnp.int32, (blk, blk), 0) >= lax.broadcasted_iota(jnp.int32, (blk, blk), 1)).astype(BF16)
        carry = jnp.zeros((H, 1), F32)
        db = jnp.zeros((H, 1), F32)
        for bi in reversed(range(T // blk)):
            xb = dc[:, bi * blk:(bi + 1) * blk]
            dz = (_dot_exact(xb, tri) + carry) * s_ref[:, bi * blk:(bi + 1) * blk]
            dz_ref[:, bi * blk:(bi + 1) * blk] = dz
            db = db + jnp.sum(dz, axis=-1, keepdims=True)
            carry = carry + jnp.sum(xb, axis=-1, keepdims=True)
        db_ref[...] = db

    full = pl.BlockSpec((H, T), lambda i: (0, 0))
    return pl.pallas_call(
        body, out_shape=(jax.ShapeDtypeStruct((H, T), F32), jax.ShapeDtypeStruct((H, 1), F32)), grid=(1,),
        in_specs=[full, full, full], out_specs=(full, pl.BlockSpec((H, 1), lambda i: (0, 0))),
        name=name, compiler_params=_params(1))(dct, drt, sgt)


FOX_STRIP_FWD = 128
FOX_STRIP_BWD = 256


def _fox_fwd(name, qk, v, crow, tq, tk, strip, carry=None):
    T, Dh = v.shape
    HP = Dh // LANES
    nk = T // tk
    assert tk % tq == 0 and tq % strip == 0
    n_strips = tq // strip

    def body(q_ref, k_ref, v_ref, ra_ref, rb_ref, o_ref, la_ref, lb_ref, s_ref, p_ref, m_ref, l_ref, acc_ref):
        i = pl.program_id(1)
        q2 = q_ref[...]
        lo = _lane_lo((tq, LANES))
        q_st = jnp.concatenate([_keep(lo, q2), _keep(jnp.logical_not(lo), q2)], axis=0)
        r_refs = (ra_ref, rb_ref)
        m_ref[...] = jnp.full(m_ref.shape, NEG, F32)
        l_ref[...] = jnp.zeros(l_ref.shape, F32)
        acc_ref[...] = jnp.zeros(acc_ref.shape, F32)
        rel = lax.broadcasted_iota(jnp.int32, (strip, tk), 0) - lax.broadcasted_iota(jnp.int32, (strip, tk), 1)

        def chunk(kc, masked):
            start = pl.multiple_of(kc * tk, tk)
            kb = k_ref[pl.ds(start, tk), :]
            vb = v_ref[pl.ds(start, tk), :]
            s_ref[...] = _dot(q_st, kb, NT)
            for h in range(2):
                cs = r_refs[h][kc]
                for st in range(n_strips):
                    rows = pl.ds(h * tq + st * strip, strip)
                    s = s_ref[rows, :] - cs
                    if masked:
                        s = jnp.where(rel >= start - (i * tq + st * strip), s, NEG)
                    m_old = m_ref[rows, :]
                    mn = jnp.maximum(m_old, jnp.max(s, axis=-1, keepdims=True))
                    p = jnp.exp(s - mn)
                    alpha = jnp.exp(m_old - mn)
                    l_ref[rows, :] = alpha * l_ref[rows, :] + jnp.sum(p, axis=-1, keepdims=True)
                    m_ref[rows, :] = mn
                    p_ref[rows, :] = p.astype(BF16)
                    acc_ref[rows, :] = acc_ref[rows, :] * alpha
            acc_ref[...] += _dot(p_ref[...], vb, NN)

        n_full = (i * tq) // tk

        def full_chunk(kc, _):
            chunk(kc, False)
            return 0

        lax.fori_loop(0, n_full, full_chunk, 0)
        chunk(n_full, True)
        top, bot = pl.ds(0, tq), pl.ds(tq, tq)
        o_ref[...] = jnp.where(lo, acc_ref[top, :] / l_ref[top, :], acc_ref[bot, :] / l_ref[bot, :])
        la_ref[...] = m_ref[top, :] + jnp.log(l_ref[top, :])
        lb_ref[...] = m_ref[bot, :] + jnp.log(l_ref[bot, :])

    row = lambda off: pl.BlockSpec((None, nk, 1, tk), lambda h, i: (2 * h + off, 0, 0, 0))
    lse = jax.ShapeDtypeStruct((HP, T, 1), F32)
    lspec = pl.BlockSpec((None, tq, 1), lambda h, i: (h, i, 0))
    scratch = [pltpu.VMEM((2 * tq, tk), F32), pltpu.VMEM((2 * tq, tk), BF16), pltpu.VMEM((2 * tq, 1), F32),
               pltpu.VMEM((2 * tq, 1), F32), pltpu.VMEM((2 * tq, LANES), F32)]
    return _call(
        body, name=name, grid=(HP, T // tq), out_shape=(jax.ShapeDtypeStruct((T, Dh), F32), lse, lse),
        in_specs=[pl.BlockSpec((tq, LANES), lambda h, i: (i, h)), pl.BlockSpec((T, LANES), lambda h, i: (0, HP + h)),
                  pl.BlockSpec((T, LANES), lambda h, i: (0, h)), row(0), row(1)],
        out_specs=(pl.BlockSpec((tq, LANES), lambda h, i: (i, h)), lspec, lspec),
        args=[qk, qk, v, crow, crow], scratch_shapes=scratch, carry=carry)


def _fox_bwd(name, qk, v, o, do, crow, lse_a, lse_b, tq, tk, strip, carry=None):
    T, Dh = v.shape
    HP = Dh // LANES
    nk = T // tk
    scale = HEAD_DIM ** -0.5
    assert tk % tq == 0 and tq % strip == 0
    n_strips = tq // strip

    def body(q_ref, k_ref, v_ref, o_ref, do_ref, ra_ref, rb_ref, la_ref, lb_ref,
             dq_ref, dk_ref, dv_ref, dca_ref, dcb_ref, dra_ref, drb_ref, s_ref, dp_ref, p_ref, ds_ref, dq_acc, dsum_ref):
        i = pl.program_id(1)

        @pl.when(i == 0)
        def _():
            dk_ref[...] = jnp.zeros_like(dk_ref)
            dv_ref[...] = jnp.zeros_like(dv_ref)
            dca_ref[...] = jnp.zeros_like(dca_ref)
            dcb_ref[...] = jnp.zeros_like(dcb_ref)

        q2 = q_ref[...]
        do2 = do_ref[...]
        lo = _lane_lo((tq, LANES))
        hi = jnp.logical_not(lo)
        q_st = jnp.concatenate([_keep(lo, q2), _keep(hi, q2)], axis=0)
        do_st = jnp.concatenate([_keep(lo, do2), _keep(hi, do2)], axis=0)
        prod = do2.astype(F32) * o_ref[...]
        dsum_ref[pl.ds(0, tq), :] = jnp.sum(jnp.where(lo, prod, 0.0), axis=-1, keepdims=True)
        dsum_ref[pl.ds(tq, tq), :] = jnp.sum(jnp.where(lo, 0.0, prod), axis=-1, keepdims=True)
        r_refs, l_refs, dc_refs, dr_refs = (ra_ref, rb_ref), (la_ref, lb_ref), (dca_ref, dcb_ref), (dra_ref, drb_ref)
        dq_acc[...] = jnp.zeros(dq_acc.shape, F32)
        dra_ref[...] = jnp.zeros(dra_ref.shape, F32)
        drb_ref[...] = jnp.zeros(drb_ref.shape, F32)
        rel = lax.broadcasted_iota(jnp.int32, (strip, tk), 0) - lax.broadcasted_iota(jnp.int32, (strip, tk), 1)

        def chunk(kc, masked):
            start = pl.multiple_of(kc * tk, tk)
            kb = k_ref[pl.ds(start, tk), :]
            vb = v_ref[pl.ds(start, tk), :]
            s_ref[...] = _dot(q_st, kb, NT)
            dp_ref[...] = _dot(do_st, vb, NT)
            for h in range(2):
                cs = r_refs[h][kc]
                col_sum = jnp.zeros((1, tk), F32)
                for st in range(n_strips):
                    rows = pl.ds(st * strip, strip)
                    both = pl.ds(h * tq + st * strip, strip)
                    s = s_ref[both, :] - cs
                    if masked:
                        s = jnp.where(rel >= start - (i * tq + st * strip), s, NEG)
                    p = jnp.exp(s - l_refs[h][rows, :])
                    ds = p * (dp_ref[both, :] - dsum_ref[both, :])
                    p_ref[both, :] = p.astype(BF16)
                    ds_ref[both, :] = ds.astype(BF16)
                    col_sum = col_sum + jnp.sum(ds, axis=0, keepdims=True)
                    dr_refs[h][rows, :] += jnp.sum(ds, axis=-1, keepdims=True)
                dc_refs[h][kc] = dc_refs[h][kc] - col_sum
            dk_ref[pl.ds(start, tk), :] += _dot(ds_ref[...], q_st, TN)
            dv_ref[pl.ds(start, tk), :] += _dot(p_ref[...], do_st, TN)
            dq_acc[...] += _dot(ds_ref[...], kb, NN)

        n_full = (i * tq) // tk

        def full_chunk(kc, _):
            chunk(kc, False)
            return 0

        lax.fori_loop(0, n_full, full_chunk, 0)
        chunk(n_full, True)
        dq_ref[...] = jnp.where(lo, dq_acc[pl.ds(0, tq), :], dq_acc[pl.ds(tq, tq), :]) * scale

    row = lambda off: pl.BlockSpec((None, nk, 1, tk), lambda h, i: (2 * h + off, 0, 0, 0))
    lspec = pl.BlockSpec((None, tq, 1), lambda h, i: (h, i, 0))
    qspec = pl.BlockSpec((tq, LANES), lambda h, i: (i, h))
    full = pl.BlockSpec((T, LANES), lambda h, i: (0, h))
    dcspec = pl.BlockSpec((None, nk, 1, tk), lambda h, i: (h, 0, 0, 0))
    grad = jax.ShapeDtypeStruct((T, Dh), F32)
    dc = jax.ShapeDtypeStruct((HP, nk, 1, tk), F32)
    dr = jax.ShapeDtypeStruct((HP, T, 1), F32)
    scratch = [pltpu.VMEM((2 * tq, tk), F32), pltpu.VMEM((2 * tq, tk), F32), pltpu.VMEM((2 * tq, tk), BF16), pltpu.VMEM((2 * tq, tk), BF16),
               pltpu.VMEM((2 * tq, LANES), F32), pltpu.VMEM((2 * tq, 1), F32)]
    return _call(
        body, name=name, grid=(HP, T // tq), out_shape=(grad, grad, grad, dc, dc, dr, dr),
        in_specs=[qspec, pl.BlockSpec((T, LANES), lambda h, i: (0, HP + h)), full, qspec, qspec, row(0), row(1), lspec, lspec],
        out_specs=(qspec, full, full, dcspec, dcspec, lspec, lspec),
        args=[qk, qk, v, o, do, crow, crow, lse_a, lse_b], scratch_shapes=scratch, carry=carry)


SWA_GROUP = 2
SWA_GROUP_BWD = 4


def _swa_block(n, q_ref, k_ref):
    qs = pl.multiple_of(n * WINDOW, WINDOW)
    ks = pl.multiple_of(jnp.maximum(n - 1, 0) * WINDOW, WINDOW)
    rel = (qs + lax.broadcasted_iota(jnp.int32, (WINDOW, 2 * WINDOW), 0)) - (ks + lax.broadcasted_iota(jnp.int32, (WINDOW, 2 * WINDOW), 1))
    valid = jnp.logical_and(rel >= 0, rel < WINDOW)
    return qs, ks, valid


def _swa_fwd(name, q, kd, vd, sinks, carry=None):
    T, Dh = q.shape
    HP = Dh // LANES

    def body(q_ref, k_ref, v_ref, sa_ref, sb_ref, o_ref, la_ref, lb_ref):
        lo = _lane_lo((WINDOW, LANES))

        top = lax.broadcasted_iota(jnp.int32, (2 * WINDOW, 1), 0) < WINDOW
        sink = jnp.where(top, sa_ref[...], sb_ref[...])

        def block(n, _):
            qs, ks, valid = _swa_block(n, q_ref, k_ref)
            q2 = q_ref[pl.ds(qs, WINDOW), :]
            kb = k_ref[pl.ds(ks, 2 * WINDOW), :]
            vb = v_ref[pl.ds(ks, 2 * WINDOW), :]
            q_st = jnp.concatenate([_keep(lo, q2), _keep(jnp.logical_not(lo), q2)], axis=0)
            s = jnp.where(jnp.concatenate([valid, valid], axis=0), _dot(q_st, kb, NT), NEG)
            m = jnp.maximum(jnp.max(s, axis=-1, keepdims=True), sink)
            p = jnp.exp(s - m)
            l = jnp.sum(p, axis=-1, keepdims=True) + jnp.exp(sink - m)
            o2 = _dot(p.astype(BF16), vb, NN) / l
            lse = m + jnp.log(l)
            o_ref[pl.ds(qs, WINDOW), :] = jnp.where(lo, o2[:WINDOW], o2[WINDOW:])
            la_ref[pl.ds(qs, WINDOW), :] = lse[:WINDOW]
            lb_ref[pl.ds(qs, WINDOW), :] = lse[WINDOW:]
            return 0

        assert (T // WINDOW) % SWA_GROUP == 0

        def group(g, c):
            for b in range(SWA_GROUP):
                c = block(g * SWA_GROUP + b, c)
            return c

        lax.fori_loop(0, T // WINDOW // SWA_GROUP, group, 0)

    full = pl.BlockSpec((T, LANES), lambda h: (0, h))
    kv = pl.BlockSpec((T, LANES), lambda h: (0, h // 2))
    sink = lambda off: pl.BlockSpec((None, 1, 1), lambda h: (2 * h + off, 0, 0))
    lse = jax.ShapeDtypeStruct((HP, T, 1), F32)
    lspec = pl.BlockSpec((None, T, 1), lambda h: (h, 0, 0))
    return _call(
        body, name=name, grid=(HP,), out_shape=(jax.ShapeDtypeStruct((T, Dh), F32), lse, lse),
        in_specs=[full, kv, kv, sink(0), sink(1)], out_specs=(full, lspec, lspec),
        args=[q, kd, vd, sinks, sinks], carry=carry)


def _swa_bwd(name, q, kd, vd, sinks, o, do, lse_a, lse_b, carry=None):
    T, Dh = q.shape
    HP = Dh // LANES
    scale = HEAD_DIM ** -0.5

    def body(q_ref, k_ref, v_ref, sa_ref, sb_ref, o_ref, do_ref, la_ref, lb_ref, dq_ref, dk_ref, dv_ref, dsa_ref, dsb_ref):
        lo = _lane_lo((WINDOW, LANES))
        hi = jnp.logical_not(lo)
        dk_ref[...] = jnp.zeros_like(dk_ref)
        dv_ref[...] = jnp.zeros_like(dv_ref)

        top = lax.broadcasted_iota(jnp.int32, (2 * WINDOW, 1), 0) < WINDOW
        sink = jnp.where(top, sa_ref[...], sb_ref[...])

        def block(n, dsinks):
            qs, ks, valid = _swa_block(n, q_ref, k_ref)
            rows = pl.ds(qs, WINDOW)
            q2 = q_ref[rows, :]
            do2 = do_ref[rows, :]
            kb = k_ref[pl.ds(ks, 2 * WINDOW), :]
            vb = v_ref[pl.ds(ks, 2 * WINDOW), :]
            prod = do2.astype(F32) * o_ref[rows, :]
            q_st = jnp.concatenate([_keep(lo, q2), _keep(hi, q2)], axis=0)
            do_st = jnp.concatenate([_keep(lo, do2), _keep(hi, do2)], axis=0)
            dsum = jnp.concatenate([jnp.sum(jnp.where(lo, prod, 0.0), axis=-1, keepdims=True),
                                    jnp.sum(jnp.where(lo, 0.0, prod), axis=-1, keepdims=True)], axis=0)
            lse = jnp.concatenate([la_ref[rows, :], lb_ref[rows, :]], axis=0)
            s = jnp.where(jnp.concatenate([valid, valid], axis=0), _dot(q_st, kb, NT), NEG)
            p = jnp.exp(s - lse)
            ds = p * (_dot(do_st, vb, NT) - dsum)
            dsb = ds.astype(BF16)
            dq2 = _dot(dsb, kb, NN)
            dq_ref[rows, :] = jnp.where(lo, dq2[:WINDOW], dq2[WINDOW:]) * scale
            dk_ref[pl.ds(ks, 2 * WINDOW), :] += _dot(dsb, q_st, TN)
            dv_ref[pl.ds(ks, 2 * WINDOW), :] += _dot(p.astype(BF16), do_st, TN)
            gone = jnp.exp(sink - lse) * dsum
            return (dsinks[0] - jnp.sum(gone[:WINDOW], axis=0, keepdims=True),
                    dsinks[1] - jnp.sum(gone[WINDOW:], axis=0, keepdims=True))

        assert (T // WINDOW) % SWA_GROUP_BWD == 0

        def group(g, c):
            for b in range(SWA_GROUP_BWD):
                c = block(g * SWA_GROUP_BWD + b, c)
            return c

        dsa, dsb_ = lax.fori_loop(0, T // WINDOW // SWA_GROUP_BWD, group, (jnp.zeros((1, 1), F32), jnp.zeros((1, 1), F32)))
        dsa_ref[...] = dsa
        dsb_ref[...] = dsb_

    full = pl.BlockSpec((T, LANES), lambda h: (0, h))
    kv = pl.BlockSpec((T, LANES), lambda h: (0, h // 2))
    sink = lambda off: pl.BlockSpec((None, 1, 1), lambda h: (2 * h + off, 0, 0))
    lspec = pl.BlockSpec((None, T, 1), lambda h: (h, 0, 0))
    dsink = pl.BlockSpec((None, 1, 1), lambda h: (h, 0, 0))
    grad = jax.ShapeDtypeStruct((T, Dh), F32)
    ds_shape = jax.ShapeDtypeStruct((HP, 1, 1), F32)
    return _call(
        body, name=name, grid=(HP,), out_shape=(grad, grad, grad, ds_shape, ds_shape),
        in_specs=[full, kv, kv, sink(0), sink(1), full, full, lspec, lspec],
        out_specs=(full, full, full, dsink, dsink),
        args=[q, kd, vd, sinks, sinks, o, do, lse_a, lse_b], carry=carry)


def _place():
    return lax.axis_index("x"), lax.axis_index("y"), lax.axis_index("c")


def _run_carry(name, carry):
    c_in, c_out = len(carry.inputs), len(carry.out_shapes)

    def body(*refs):
        ins, outs, scr = refs[:c_in], refs[c_in:c_in + c_out], refs[c_in + c_out:]
        carry.start(ins, outs, scr)
        carry.middle(ins, outs, scr)
        carry.finish(ins, outs, scr)

    return pl.pallas_call(
        body, out_shape=tuple(carry.out_shapes), in_specs=[_HBM] * c_in, out_specs=tuple([_HBM] * c_out),
        scratch_shapes=carry.scratch, name=name)(*carry.inputs)


def _gather_carry(shards):
    n = len(shards)

    def plan(ins, outs, scr):
        send, recv, local = scr
        x, y, c = _place()
        me, sibling = (x, y, c), (x, y, 1 - c)
        partner, other, diag = (x ^ c, y ^ (1 - c)), (x ^ (1 - c), y ^ c), (1 - x, 1 - y)

        def copy(w, k, block, to, src=None):
            slot = 4 * block[0] + 2 * block[1] + block[2]
            return pltpu.make_async_remote_copy(
                src_ref=outs[w].at[slot] if src is None else src, dst_ref=outs[w].at[slot],
                send_sem=send.at[w, k], recv_sem=recv.at[w, k], device_id=to, device_id_type=MESH)

        def own():
            return [pltpu.make_async_copy(ins[w], outs[w].at[4 * x + 2 * y + c], local.at[w]) for w in range(n)]

        return copy, own, me, sibling, partner, other, diag, c

    def start(ins, outs, scr):
        copy, own, me, sibling, partner, other, _, c = plan(ins, outs, scr)
        for cp in own():
            cp.start()
        for w in range(n):
            copy(w, 1, me, (*partner, c), src=ins[w]).start()
            copy(w, 2, me, (*other, c), src=ins[w]).start()
            copy(w, 0, me, sibling, src=ins[w]).start()

    def middle(ins, outs, scr):
        copy, _, me, sibling, partner, other, _, c = plan(ins, outs, scr)
        for w in range(n):
            copy(w, 1, (*partner, c), me).wait_recv()
            copy(w, 3, (*partner, c), (*other, c)).start()
            copy(w, 4, (*partner, c), sibling).start()
        for w in range(n):
            copy(w, 2, (*other, c), me).wait_recv()
            copy(w, 5, (*other, c), sibling).start()

    def finish(ins, outs, scr):
        copy, own, me, sibling, partner, other, diag, c = plan(ins, outs, scr)
        for w in range(n):
            copy(w, 3, (*diag, c), me).wait_recv()
            copy(w, 6, (*diag, c), sibling).start()
        for w in range(n):
            copy(w, 0, sibling, me).wait_recv()
            copy(w, 4, (*other, 1 - c), me).wait_recv()
            copy(w, 5, (*partner, 1 - c), me).wait_recv()
            copy(w, 6, (*diag, 1 - c), me).wait_recv()
        for w in range(n):
            sent = [copy(w, 0, me, sibling, src=ins[w]), copy(w, 1, me, (*partner, c), src=ins[w]), copy(w, 2, me, (*other, c), src=ins[w]),
                    copy(w, 3, (*partner, c), (*other, c)), copy(w, 4, (*partner, c), sibling), copy(w, 5, (*other, c), sibling),
                    copy(w, 6, (*diag, c), sibling)]
            for cp in sent:
                cp.wait_send()
        for cp in own():
            cp.wait()

    return _Carry(shards, [jax.ShapeDtypeStruct((N_DEV,) + s.shape, s.dtype) for s in shards],
                  [pltpu.SemaphoreType.DMA((n, 7)), pltpu.SemaphoreType.DMA((n, 7)), pltpu.SemaphoreType.DMA((n,))], start, finish, middle)


def _sibling_carry(grads):
    n = len(grads)

    def copies(ins, outs, scr):
        send, recv = scr
        x, y, c = _place()
        return [pltpu.make_async_remote_copy(
            src_ref=ins[w].at[2 * q + (1 - c)], dst_ref=outs[w].at[q], send_sem=send.at[w, q], recv_sem=recv.at[w, q],
            device_id=(x, y, 1 - c), device_id_type=MESH) for w in range(n) for q in range(4)]

    def start(ins, outs, scr):
        for cp in copies(ins, outs, scr):
            cp.start()

    def finish(ins, outs, scr):
        for cp in copies(ins, outs, scr):
            cp.wait()

    return _Carry(grads, [jax.ShapeDtypeStruct((4,) + g.shape[1:], g.dtype) for g in grads],
                  [pltpu.SemaphoreType.DMA((n, 4)), pltpu.SemaphoreType.DMA((n, 4))], start, finish)


def _to_partner_carry(sums):
    n = len(sums)

    def copies(ins, outs, scr):
        send, recv = scr
        x, y, c = _place()
        return [pltpu.make_async_remote_copy(
            src_ref=ins[w].at[k], dst_ref=outs[2 * w + k], send_sem=send.at[w, k], recv_sem=recv.at[w, k],
            device_id=(x ^ c, y ^ (1 - c), c), device_id_type=MESH) for w in range(n) for k in range(2)]

    def start(ins, outs, scr):
        for cp in copies(ins, outs, scr):
            cp.start()

    def finish(ins, outs, scr):
        for cp in copies(ins, outs, scr):
            cp.wait()

    return _Carry(sums, [jax.ShapeDtypeStruct(s.shape[1:], s.dtype) for s in sums for _ in range(2)],
                  [pltpu.SemaphoreType.DMA((n, 2)), pltpu.SemaphoreType.DMA((n, 2))], start, finish)


def _to_other_carry(blocks):
    n = len(blocks)

    def copies(ins, outs, scr):
        send, recv = scr
        x, y, c = _place()
        return [pltpu.make_async_remote_copy(
            src_ref=ins[w], dst_ref=outs[w], send_sem=send.at[w], recv_sem=recv.at[w],
            device_id=(x ^ (1 - c), y ^ c, c), device_id_type=MESH) for w in range(n)]

    def start(ins, outs, scr):
        for cp in copies(ins, outs, scr):
            cp.start()

    def finish(ins, outs, scr):
        for cp in copies(ins, outs, scr):
            cp.wait()

    return _Carry(blocks, [jax.ShapeDtypeStruct(b.shape, b.dtype) for b in blocks],
                  [pltpu.SemaphoreType.DMA((n,)), pltpu.SemaphoreType.DMA((n,))], start, finish)


def _gather_small(packed):
    R, C = packed.shape

    def body(in_ref, out_ref, send, recv):
        x, y, c = _place()
        mine = 4 * x + 2 * y + c
        out_ref[mine] = in_ref[...]
        copies = []
        for k in range(1, N_DEV):
            peer = (x ^ (k >> 2), y ^ ((k >> 1) & 1), c ^ (k & 1))
            copies.append(pltpu.make_async_remote_copy(
                src_ref=in_ref, dst_ref=out_ref.at[mine], send_sem=send.at[k - 1], recv_sem=recv.at[k - 1],
                device_id=peer, device_id_type=MESH))
        for cp in copies:
            cp.start()
        for cp in copies:
            cp.wait()

    vmem = pl.BlockSpec(memory_space=pltpu.VMEM)
    return pl.pallas_call(
        body, out_shape=jax.ShapeDtypeStruct((N_DEV, R, C), F32), in_specs=[vmem], out_specs=vmem,
        scratch_shapes=[pltpu.SemaphoreType.DMA((N_DEV - 1,)), pltpu.SemaphoreType.DMA((N_DEV - 1,))],
        name="small_grads_all_gather")(packed)


def _adamw(w, g, m, v):
    m = ADAM_B1 * m + (1.0 - ADAM_B1) * g
    v = ADAM_B2 * v + (1.0 - ADAM_B2) * (g * g)
    m_hat = m / (1.0 - ADAM_B1 ** ADAM_STEP)
    v_hat = v / (1.0 - ADAM_B2 ** ADAM_STEP)
    delta = -ADAM_LR * (m_hat / (jnp.sqrt(v_hat) + ADAM_EPS) + ADAM_WD * w)
    return delta, m, v


def _pair_add(name, grads, received, slots):
    _, R, C = grads.shape
    tr = _row_tile(R, 2 * ROW_TILE_CAP)

    def body(s_ref, g_ref, r_ref, o_ref):
        o_ref[...] = (g_ref[...].astype(F32) + r_ref[...].astype(F32)).astype(BF16)

    return pl.pallas_call(
        body, out_shape=jax.ShapeDtypeStruct((3, R, C), BF16),
        grid_spec=pltpu.PrefetchScalarGridSpec(
            num_scalar_prefetch=1, grid=(3, R // tr),
            in_specs=[pl.BlockSpec((None, tr, C), lambda k, i, s: (s[k], i, 0)), pl.BlockSpec((None, tr, C), lambda k, i, s: (s[3 + k], i, 0))],
            out_specs=pl.BlockSpec((None, tr, C), lambda k, i, s: (k, i, 0))),
        name=name, compiler_params=_params(2))(slots, grads, received)


def _relay_add(name, sums, relayed):
    _, R, C = sums.shape
    tr = _row_tile(R, 2 * ROW_TILE_CAP)

    def body(s_ref, r_ref, o_ref):
        o_ref[...] = (s_ref[...].astype(F32) + r_ref[...].astype(F32)).astype(BF16)

    blk = pl.BlockSpec((tr, C), lambda i: (i, 0))
    return pl.pallas_call(
        body, out_shape=jax.ShapeDtypeStruct((R, C), BF16), grid=(R // tr,),
        in_specs=[pl.BlockSpec((None, tr, C), lambda i: (2, i, 0)), blk], out_specs=blk,
        name=name, compiler_params=_params(1))(sums, relayed)


def _adam_shard(name, grads, from_sibling, received, w, m, v, own):
    R, C = w.shape
    tr = _row_tile(R, 256)
    tc = C if tr < R or C % (4 * LANES) else 4 * LANES

    def body(o_ref, g_ref, s_ref, ra_ref, rb_ref, w_ref, m_ref, v_ref, g_out, d_out, m_out, v_out):
        g = (g_ref[...].astype(F32) + s_ref[...].astype(F32)) + ra_ref[...].astype(F32) + rb_ref[...].astype(F32)
        delta, mn, vn = _adamw(w_ref[...], g, m_ref[...], v_ref[...])
        g_out[...] = g
        d_out[...] = delta
        m_out[...] = mn
        v_out[...] = vn

    blk = pl.BlockSpec((tr, tc), lambda i, j, o: (i, j))
    shape = jax.ShapeDtypeStruct((R, C), F32)
    return pl.pallas_call(
        body, out_shape=(shape,) * 4,
        grid_spec=pltpu.PrefetchScalarGridSpec(
            num_scalar_prefetch=1, grid=(R // tr, C // tc),
            in_specs=[pl.BlockSpec((None, tr, tc), lambda i, j, o: (o[0], i, j)), pl.BlockSpec((None, tr, tc), lambda i, j, o: (o[1], i, j)),
                      blk, blk, blk, blk, blk],
            out_specs=(blk,) * 4),
        name=name, compiler_params=_params(2))(own, grads, from_sibling, received[0], received[1], w, m, v)


def _adam_small(name, gathered, w, m, v):
    R, C = w.shape

    def body(ga_ref, w_ref, m_ref, v_ref, g_out, d_out, m_out, v_out):
        g = ga_ref[0]
        for d in range(1, N_DEV):
            g = g + ga_ref[d]
        delta, mn, vn = _adamw(w_ref[...], g, m_ref[...], v_ref[...])
        g_out[...] = g
        d_out[...] = delta
        m_out[...] = mn
        v_out[...] = vn

    full = pl.BlockSpec((R, C), lambda i: (0, 0))
    shape = jax.ShapeDtypeStruct((R, C), F32)
    return pl.pallas_call(
        body, out_shape=(shape,) * 4, grid=(1,),
        in_specs=[pl.BlockSpec((N_DEV, R, C), lambda i: (0, 0, 0)), full, full, full], out_specs=(full,) * 4,
        name=name, compiler_params=_params(1))(gathered, w, m, v)


def _pack_small(parts, D, scalar=None):
    g1, gmix, g2, gof, gos, bf, gqf, gkf, gqs, gks, sinks = [p.reshape(-1).astype(F32) for p in parts]
    row3 = jnp.concatenate([gof, gos])
    row4 = jnp.zeros((D,), F32)
    for slot, vec in enumerate((bf, gqf, gkf, gqs, gks, sinks)):
        row4 = lax.dynamic_update_slice(row4, vec, (slot * LANES,))
    zero = jnp.zeros((D,), F32)
    row5 = zero if scalar is None else lax.dynamic_update_slice(zero, jnp.reshape(scalar, (1,)).astype(F32), (0,))
    return jnp.stack([g1, gmix, g2, row3, row4, row5, zero, zero])


def _unpack_small(packed, D, H):
    Dh = D // 2
    row4 = packed[4]
    short = [row4[s * LANES:s * LANES + n] for s, n in enumerate((H, HEAD_DIM, HEAD_DIM, HEAD_DIM, HEAD_DIM, H))]
    vecs = [packed[0], packed[1], packed[2], packed[3, :Dh], packed[3, Dh:]] + short
    return [v[None, :] for v in vecs]


def kernel(x, positions, norm_ffn1_g, ffn1_w_gate, ffn1_w_up, ffn1_w_down, norm_mix_g, w_in, b_forget, fox_q_norm_g, fox_k_norm_g, swa_q_norm_g, swa_k_norm_g, swa_sinks, out_norm_fox_g, out_norm_swa_g, w_out, norm_ffn2_g, ffn2_w_gate, ffn2_w_up, ffn2_w_down, loss_target, m_norm_ffn1_g, m_ffn1_w_gate, m_ffn1_w_up, m_ffn1_w_down, m_norm_mix_g, m_w_in, m_b_forget, m_fox_q_norm_g, m_fox_k_norm_g, m_swa_q_norm_g, m_swa_k_norm_g, m_swa_sinks, m_out_norm_fox_g, m_out_norm_swa_g, m_w_out, m_norm_ffn2_g, m_ffn2_w_gate, m_ffn2_w_up, m_ffn2_w_down, v_norm_ffn1_g, v_ffn1_w_gate, v_ffn1_w_up, v_ffn1_w_down, v_norm_mix_g, v_w_in, v_b_forget, v_fox_q_norm_g, v_fox_k_norm_g, v_swa_q_norm_g, v_swa_k_norm_g, v_swa_sinks, v_out_norm_fox_g, v_out_norm_swa_g, v_w_out, v_norm_ffn2_g, v_ffn2_w_gate, v_ffn2_w_up, v_ffn2_w_down):
    xs = x[0]
    target = loss_target[0]
    T, D = xs.shape
    Dh = D // 2
    H = Dh // HEAD_DIM
    HP = H // 2
    KVW = (H // GQA_GROUP) * HEAD_DIM
    KVB = KVW // LANES
    MAIN = 4 * Dh + 2 * KVW
    F_OFF = 3 * Dh
    tm = min(ROW_TILE_CAP, T)
    tm2 = min(2 * ROW_TILE_CAP, T)
    tq = min(512, T)
    tk = min(512, T)
    nk = T // tk
    cx, cy, cc = _place()
    near = [2 * (cx ^ cc) + (cy ^ (1 - cc)), 2 * (1 - cx) + (1 - cy), 2 * (cx ^ (1 - cc)) + (cy ^ cc)]
    slots = jnp.stack([2 * q + cc for q in near] + near).astype(jnp.int32)
    own = jnp.stack([4 * cx + 2 * cy + cc, 2 * cx + cy]).astype(jnp.int32)

    tr = jnp.transpose
    big_w = [tr(ffn1_w_gate[0]), tr(ffn1_w_up[0]), ffn1_w_down[0], tr(w_in[0]), w_out[0], tr(ffn2_w_gate[0]), tr(ffn2_w_up[0]),
             ffn2_w_down[0]]
    big_m = [tr(m_ffn1_w_gate[0]), tr(m_ffn1_w_up[0]), m_ffn1_w_down[0], tr(m_w_in[0]), m_w_out[0], tr(m_ffn2_w_gate[0]),
             tr(m_ffn2_w_up[0]), m_ffn2_w_down[0]]
    big_v = [tr(v_ffn1_w_gate[0]), tr(v_ffn1_w_up[0]), v_ffn1_w_down[0], tr(v_w_in[0]), v_w_out[0], tr(v_ffn2_w_gate[0]),
             tr(v_ffn2_w_up[0]), v_ffn2_w_down[0]]
    transposed = {"ffn1_w_gate", "ffn1_w_up", "w_in", "ffn2_w_gate", "ffn2_w_up"}
    names = ["ffn1_w_gate", "ffn1_w_up", "ffn1_w_down", "w_in", "w_out", "ffn2_w_gate", "ffn2_w_up", "ffn2_w_down"]
    sh = dict(zip(names, [w.astype(BF16) for w in big_w]))
    lane = jnp.arange(LANES)
    inv_freq = ROPE_THETA ** (-(2.0 * (lane % (HEAD_DIM // 2))).astype(F32) / HEAD_DIM)
    ang = positions[0].astype(F32)[:, None] * inv_freq[None, :]
    cos_t = jnp.cos(ang)
    sin_t = jnp.where((lane & (HEAD_DIM // 2)) == 0, -1.0, 1.0)[None, :] * jnp.sin(ang)
    rope = (cos_t, sin_t)

    def pair_gain(g, blocks):
        return jnp.tile(jnp.concatenate([g[0], g[0]])[None, None, :], (blocks, 1, 1))

    n1, (wg1,) = _rmsnorm_fwd("ffn1_norm", xs, norm_ffn1_g, tm, carry=_gather_carry([sh["ffn1_w_gate"]]))
    a1, (wu1,) = _ffn_gate("ffn1_gate", n1, wg1, tm, carry=_gather_carry([sh["ffn1_w_up"]]))
    (b1, hm1), (wd1,) = _ffn_up_only("ffn1_up", n1, wu1, a1, tm, carry=_gather_carry([sh["ffn1_w_down"]]))
    h1, (win_g,) = _ffn_down("ffn1_down", hm1, wd1, xs, tm, D, carry=_gather_carry([sh["w_in"]]))
    n_in = win_g.shape[1]
    win_t = win_g.reshape(N_DEV * n_in, D)
    win_f = jnp.pad(win_t[F_OFF:F_OFF + H], ((0, LANES - H), (0, 0)))
    tkb = MAIN // 9
    assert F_OFF % tkb == 0

    def main_row(r):
        return pl.multiple_of(r * tkb + H * (r >= F_OFF // tkb).astype(jnp.int32), min(H, tkb))

    u = _rmsnorm_fwd("mix_norm", h1, norm_mix_g, tm)
    proj, (wout_g,) = _mm("mix_proj", u, win_t, T, tkb, dims=NT, carry=_gather_carry([sh["w_out"]]), b_rows=(9, main_row))
    wout = wout_g.reshape(D, D)
    proj_f = _mm("mix_proj_forget", u, win_f, T, LANES, dims=NT)
    scale = HEAD_DIM ** -0.5
    fox_gains = jnp.concatenate([pair_gain(fox_q_norm_g, HP), pair_gain(fox_k_norm_g, HP)])
    qk_f = _headnorm_fwd_scaled("fox_qk_norm", proj, 0, 2 * HP, fox_gains, T, scale, HP)
    v_f = proj[:, 2 * Dh:3 * Dh].astype(BF16)
    c_t, sg_t = _forget_fwd("forget_gates", proj_f[:, :H].T, b_forget.reshape(H, 1))
    crow = c_t.reshape(H, nk, 1, tk)
    (o_fox, lse_fa, lse_fb), (wg2, wu2) = _fox_fwd("fox_attention", qk_f, v_f, crow, tq, tk, min(FOX_STRIP_FWD, tq),
                                                   carry=_gather_carry([sh["ffn2_w_gate"], sh["ffn2_w_up"]]))

    swa_q_gains = pair_gain(swa_q_norm_g, HP)
    swa_k_gains = pair_gain(swa_k_norm_g, KVB)
    q_s = _headnorm_fwd("swa_q_norm", proj, 3 * HP, HP, swa_q_gains, T, scale, rope=rope)
    k_d = _headnorm_fwd("swa_k_norm", proj, 4 * HP, KVB, swa_k_gains, T, 1.0, rope=rope, dup=True)
    v_s = proj[:, 4 * Dh + KVW:].astype(BF16).reshape(T, H // GQA_GROUP, 1, HEAD_DIM)
    v_d = jnp.broadcast_to(v_s, (T, H // GQA_GROUP, 2, HEAD_DIM)).reshape(T, 2 * KVW)
    sinks3 = swa_sinks.reshape(H, 1, 1)
    o_swa, lse_sa, lse_sb = _swa_fwd("swa_attention", q_s, k_d, v_d, sinks3)

    on = _outnorm_fwd("out_norm", o_fox, o_swa, out_norm_fox_g, out_norm_swa_g, tm)
    h2 = _mm("mix_out", on, wout, tm2, min(512, D), resid=h1)

    n2 = _rmsnorm_fwd("ffn2_norm", h2, norm_ffn2_g, tm)
    (a2, b2, hm2), (wd2,) = _ffn_up("ffn2_up", n2, wg2, wu2, tm2, carry=_gather_carry([sh["ffn2_w_down"]]))
    y = _ffn_down("ffn2_down", hm2, wd2, h2, tm2, D)
    dy, dyh, sq = _loss_grad("loss_grad", y, target, tm)
    loss_part = 0.5 * sq[0, 0] / D

    J, Fs, _ = wg2.shape
    aspec = pl.BlockSpec((None, tm, Fs), lambda i, j: (j, i, 0))
    wspec = pl.BlockSpec((None, Fs, D), lambda i, j: (j, 0, 0))
    got = {}
    local = {}

    def pair_sums(keys, grads, received):
        for nm, g, r in zip(keys, grads, received):
            local[nm] = (g, r)
        return [_pair_add("sum_" + nm, g, r, slots) for nm, g, r in zip(keys, grads, received)]

    def relay_sums(keys, sums, hop1):
        out = []
        for i, (nm, s) in enumerate(zip(keys, sums)):
            got[nm] = [hop1[2 * i]]
            out.append(_relay_add("relay_" + nm, s, hop1[2 * i + 1]))
        return out

    def arrived(keys, hop2):
        for nm, blk in zip(keys, hop2):
            got[nm].append(blk)

    dwd2 = _wgrad_down("ffn2_wgrad_down", hm2, dyh, D)
    (da2, db2), (sib_d2,) = _ffn_bwd_mid("ffn2_bwd_mid", dyh, wd2, a2, b2, tm2, carry=_sibling_carry([dwd2]))
    (sum_wd2,) = pair_sums(names[7:8], [dwd2], [sib_d2])
    (dwg2, dwu2), hop1 = _wgrad_up("ffn2_wgrad_up", n2, da2, db2, min(1024, D), carry=_to_partner_carry([sum_wd2]))
    (t_wd2,) = relay_sums(names[7:8], [sum_wd2], hop1)
    aspec2 = pl.BlockSpec((None, tm2, Fs), lambda i, j: (j, i, 0))
    dn2, (via_wd2, *sib2) = _reduce_mm("ffn2_bwd_in", [(da2, aspec2, wg2, wspec), (db2, aspec2, wu2, wspec)], [], NN, T, D, tm2, J,
                                       carry=_join(_to_other_carry([t_wd2]), _sibling_carry([dwg2, dwu2])))
    arrived(names[7:8], [via_wd2])
    dh2, dg_ffn2, dh2b = _rmsnorm_bwd("ffn2_norm_bwd", dn2, h2, norm_ffn2_g, dy, min(256, T), 1.0)
    sum_wg2, sum_wu2 = pair_sums(names[5:7], [dwg2, dwu2], sib2)

    dwout = _wgrad_2d("mix_out_wgrad", on, dh2b, min(512, D), D)
    dwout_g = dwout.reshape(N_DEV, D // N_DEV, D)
    do_fox, dg_of = _outnorm_bwd("out_norm_bwd_fox", dh2b, wout, 0, o_fox, out_norm_fox_g, tm)
    do_swa, dg_os = _outnorm_bwd("out_norm_bwd_swa", dh2b, wout, 1, o_swa, out_norm_swa_g, tm)

    (dq_f, dk_f, dv_f, dc_a, dc_b, dr_a, dr_b), (*hop1, sib_wout) = _fox_bwd(
        "fox_attention_bwd", qk_f, v_f, o_fox, do_fox, crow, lse_fa, lse_fb, tq, tk, min(FOX_STRIP_BWD, tq),
        carry=_join(_to_partner_carry([sum_wg2, sum_wu2]), _sibling_carry([dwout_g])))
    t_wg2, t_wu2 = relay_sums(names[5:7], [sum_wg2, sum_wu2], hop1)
    (sum_wout,) = pair_sums(names[4:5], [dwout_g], [sib_wout])
    dqf_raw, dg_fq = _headnorm_bwd("fox_q_norm_bwd", dq_f, proj, 0, HP, fox_gains[:HP], HP, T, 1.0)
    dkf_raw, dg_fk = _headnorm_bwd("fox_k_norm_bwd", dk_f, proj, HP, HP, fox_gains[HP:], HP, T, 1.0)
    dct = jnp.stack([dc_a.reshape(HP, T), dc_b.reshape(HP, T)], axis=1).reshape(H, T)
    drt = jnp.stack([dr_a.reshape(HP, T), dr_b.reshape(HP, T)], axis=1).reshape(H, T)
    dz_t, db_f = _forget_bwd("forget_gates_bwd", dct, drt, sg_t)

    (dq_s, dk_p, dv_p, dsink_a, dsink_b), hop2 = _swa_bwd(
        "swa_attention_bwd", q_s, k_d, v_d, sinks3, o_swa, do_swa, lse_sa, lse_sb, carry=_to_other_carry([t_wg2, t_wu2]))
    arrived(names[5:7], hop2)
    dqs_raw, dg_sq = _headnorm_bwd("swa_q_norm_bwd", dq_s, proj, 3 * HP, HP, swa_q_gains, HP, T, 1.0, rope=rope)
    dks_raw, dg_sk = _headnorm_bwd("swa_k_norm_bwd", dk_p, proj, 4 * HP, KVB, swa_k_gains, KVB, T, 1.0, rope=rope, fold=True)
    dvs_raw, _ = _headnorm_bwd("swa_v_fold", dv_p, None, 0, KVB, None, KVB, T, 1.0, fold=True, norm=False)

    dproj = jnp.concatenate([dqf_raw, dkf_raw, dv_f.astype(BF16), dqs_raw, dks_raw, dvs_raw], axis=1)
    dproj_f = jnp.pad(dz_t.T, ((0, 0), (0, LANES - H))).astype(BF16)
    dwin_t, hop1 = _wgrad_2d("mix_proj_wgrad", dproj, u, tkb, D, carry=_to_partner_carry([sum_wout]),
                             out_rows=(N_DEV * n_in, main_row))
    (t_wout,) = relay_sums(names[4:5], [sum_wout], hop1)
    dwin_f = _wgrad_2d("mix_proj_forget_wgrad", dproj_f, u, LANES, min(1024, D))
    dwin_t = lax.dynamic_update_slice(dwin_t, dwin_f[:H], (F_OFF, 0))
    dwin_g = dwin_t.reshape(N_DEV, n_in, D)
    du, (via_wout, sib_win) = _reduce_mm(
        "mix_bwd_in",
        [(dproj, pl.BlockSpec((tm2, tkb), lambda i, r: (i, r)), win_t, pl.BlockSpec((pl.Element(tkb), pl.Element(D)), lambda i, r: (main_row(r), 0)))],
        [(dproj_f, pl.BlockSpec((tm2, LANES), lambda i, r: (i, 0)), win_f, pl.BlockSpec((LANES, D), lambda i, r: (0, 0)))],
        NN, T, D, tm2, 9, carry=_join(_to_other_carry([t_wout]), _sibling_carry([dwin_g])))
    arrived(names[4:5], [via_wout])
    dh1, dg_mix, dh1h = _rmsnorm_bwd("mix_norm_bwd", du, h1, norm_mix_g, dh2, min(256, T), 0.5)
    (sum_win,) = pair_sums(names[3:4], [dwin_g], [sib_win])

    dwd1, hop1 = _wgrad_down("ffn1_wgrad_down", hm1, dh1h, D, carry=_to_partner_carry([sum_win]))
    (t_win,) = relay_sums(names[3:4], [sum_win], hop1)
    (da1, db1), (via_win, sib_d) = _ffn_bwd_mid("ffn1_bwd_mid", dh1h, wd1, a1, b1, tm2,
                                                carry=_join(_to_other_carry([t_win]), _sibling_carry([dwd1])))
    arrived(names[3:4], [via_win])
    (sum_wd1,) = pair_sums(names[2:3], [dwd1], [sib_d])
    dwg1, hop1 = _wgrad_down("ffn1_wgrad_gate", da1, n1, D, carry=_to_partner_carry([sum_wd1]))
    (t_wd1,) = relay_sums(names[2:3], [sum_wd1], hop1)
    dwu1, (via_wd1, sib_g) = _wgrad_down("ffn1_wgrad_up", db1, n1, D,
                                         carry=_join(_to_other_carry([t_wd1]), _sibling_carry([dwg1])))
    arrived(names[2:3], [via_wd1])
    (sum_wg1,) = pair_sums(names[0:1], [dwg1], [sib_g])
    dn1_gate, (*hop1, sib_u) = _reduce_mm(
        "ffn1_bwd_in_gate", [(da1, aspec2, wg1, wspec)], [], NN, T, D, tm2, J,
        carry=_join(_to_partner_carry([sum_wg1]), _sibling_carry([dwu1])))
    (t_wg1,) = relay_sums(names[0:1], [sum_wg1], hop1)
    (sum_wu1,) = pair_sums(names[1:2], [dwu1], [sib_u])
    dn1, (via_wg1, *hop1) = _reduce_mm(
        "ffn1_bwd_in_up", [(db1, aspec2, wu1, wspec)], [], NN, T, D, tm2, J, init=dn1_gate,
        carry=_join(_to_other_carry([t_wg1]), _to_partner_carry([sum_wu1])))
    arrived(names[0:1], [via_wg1])
    (t_wu1,) = relay_sums(names[1:2], [sum_wu1], hop1)
    arrived(names[1:2], _run_carry("grads_exchange", _to_other_carry([t_wu1])))
    dx, dg_ffn1 = _rmsnorm_bwd("ffn1_norm_bwd", dn1, xs, norm_ffn1_g, dh1, min(256, T), None)

    big_out = [_adam_shard("adam_" + nm, local[nm][0], local[nm][1], got[nm], w, m, v, own)
               for nm, w, m, v in zip(names, big_w, big_m, big_v)]

    dsinks = jnp.stack([dsink_a.reshape(HP), dsink_b.reshape(HP)], axis=1).reshape(H)
    small_g = [dg_ffn1, dg_mix, dg_ffn2, dg_of, dg_os, db_f, dg_fq[0, 0, :HEAD_DIM], dg_fk[0, 0, :HEAD_DIM],
               dg_sq[0, 0, :HEAD_DIM], dg_sk[0, 0, :HEAD_DIM], dsinks]
    small_w = [norm_ffn1_g, norm_mix_g, norm_ffn2_g, out_norm_fox_g, out_norm_swa_g, b_forget, fox_q_norm_g, fox_k_norm_g,
               swa_q_norm_g, swa_k_norm_g, swa_sinks]
    small_m = [m_norm_ffn1_g, m_norm_mix_g, m_norm_ffn2_g, m_out_norm_fox_g, m_out_norm_swa_g, m_b_forget, m_fox_q_norm_g,
               m_fox_k_norm_g, m_swa_q_norm_g, m_swa_k_norm_g, m_swa_sinks]
    small_v = [v_norm_ffn1_g, v_norm_mix_g, v_norm_ffn2_g, v_out_norm_fox_g, v_out_norm_swa_g, v_b_forget, v_fox_q_norm_g,
               v_fox_k_norm_g, v_swa_q_norm_g, v_swa_k_norm_g, v_swa_sinks]
    gathered = _gather_small(_pack_small(small_g, D, loss_part))
    small_out = _adam_small("adam_small", gathered, _pack_small(small_w, D), _pack_small(small_m, D), _pack_small(small_v, D))
    loss = small_out[0][5, 0]
    small_out = [_unpack_small(p, D, H) for p in small_out]

    order = ["norm_ffn1_g", "ffn1_w_gate", "ffn1_w_up", "ffn1_w_down", "norm_mix_g", "w_in", "b_forget", "fox_q_norm_g", "fox_k_norm_g",
             "swa_q_norm_g", "swa_k_norm_g", "swa_sinks", "out_norm_fox_g", "out_norm_swa_g", "w_out", "norm_ffn2_g",
             "ffn2_w_gate", "ffn2_w_up", "ffn2_w_down"]
    small_names = ["norm_ffn1_g", "norm_mix_g", "norm_ffn2_g", "out_norm_fox_g", "out_norm_swa_g", "b_forget", "fox_q_norm_g",
                   "fox_k_norm_g", "swa_q_norm_g", "swa_k_norm_g", "swa_sinks"]
    result = [loss, dx[None]]
    for kind in range(4):
        for nm in order:
            if nm in names:
                leaf = big_out[names.index(nm)][kind]
                result.append((tr(leaf) if nm in transposed else leaf)[None])
            else:
                result.append(small_out[kind][small_names.index(nm)])
    return tuple(result)


def _headnorm_fwd_scaled(name, proj, col_off, ncb, gains, tm, scale, n_scaled):
    T = proj.shape[0]

    def body(x_ref, g_ref, o_ref):
        xv = x_ref[...]
        lo = _lane_lo(xv.shape)
        y = xv * _head_rstd(xv, lo) * g_ref[...]
        y = y * jnp.where(pl.program_id(0) < n_scaled, scale, 1.0)
        o_ref[...] = y.astype(BF16)

    return pl.pallas_call(
        body, out_shape=jax.ShapeDtypeStruct((T, ncb * LANES), BF16), grid=(ncb, T // tm),
        in_specs=[pl.BlockSpec((tm, LANES), lambda c, i: (i, col_off + c)), pl.BlockSpec((None, 1, LANES), lambda c, i: (c, 0, 0))],
        out_specs=pl.BlockSpec((tm, LANES), lambda c, i: (i, c)), name=name, compiler_params=_params(2))(proj, gains)
```

```python
import jax
import jax.numpy as jnp
from jax import lax
from jax.experimental import pallas as pl
from jax.experimental.pallas import tpu as pltpu

F32 = jnp.float32
BF16 = jnp.bfloat16

HEAD_DIM = 64
LANES = 128
WINDOW = 128
GQA_GROUP = 4
EPS = 1e-6
ROPE_THETA = 10000.0
ADAM_LR = 0.001
ADAM_B1 = 0.9
ADAM_B2 = 0.999
ADAM_EPS = 1e-08
ADAM_WD = 0.01
ADAM_STEP = 10
N_DEV = 8
NEG = -1e30
VMEM_LIMIT_V7X = 48 * 1024 * 1024
ROW_TILE_CAP = 512
MESH = pl.DeviceIdType.MESH

NN = (((1,), (0,)), ((), ()))
NT = (((1,), (1,)), ((), ()))
TN = (((0,), (0,)), ((), ()))


def _dot(a, b, dims):
    return lax.dot_general(a, b, dims, preferred_element_type=F32)


def _params(n_axes):
    return pltpu.CompilerParams(dimension_semantics=("arbitrary",) * n_axes, vmem_limit_bytes=VMEM_LIMIT_V7X)


def _row_tile(rows, cap=ROW_TILE_CAP):
    best = None
    for t in range(16, min(rows, cap) + 1, 16):
        if rows % t == 0:
            best = t
    return best or rows


def _lane_lo(shape):
    return lax.broadcasted_iota(jnp.int32, shape, len(shape) - 1) < HEAD_DIM


def _keep(sel, x):
    return jnp.where(sel, x.astype(F32), 0.0).astype(BF16)


_HBM = pl.BlockSpec(memory_space=pltpu.HBM)


class _Carry:
    def __init__(self, inputs, out_shapes, scratch, start, finish, middle=None):
        self.inputs, self.out_shapes, self.scratch = list(inputs), list(out_shapes), list(scratch)
        self.start, self.finish, self.middle = start, finish, middle or (lambda ins, outs, scr: None)


def _join(*carries):
    def hook(which):
        def run(ins, outs, scr):
            i = o = s = 0
            for c in carries:
                ni, no, ns = len(c.inputs), len(c.out_shapes), len(c.scratch)
                getattr(c, which)(ins[i:i + ni], outs[o:o + no], scr[s:s + ns])
                i, o, s = i + ni, o + no, s + ns
        return run

    return _Carry([a for c in carries for a in c.inputs], [a for c in carries for a in c.out_shapes],
                  [a for c in carries for a in c.scratch], hook("start"), hook("finish"), hook("middle"))


def _call(body, *, name, grid, in_specs, out_specs, out_shape, args, scratch_shapes=(), carry=None):
    params = _params(len(grid))
    if carry is None:
        return pl.pallas_call(body, out_shape=out_shape, grid=grid, in_specs=list(in_specs), out_specs=out_specs,
                              scratch_shapes=list(scratch_shapes), name=name, compiler_params=params)(*args)
    single = not isinstance(out_shape, (tuple, list))
    shapes = (out_shape,) if single else tuple(out_shape)
    specs = (out_specs,) if single else tuple(out_specs)
    n_in, n_out, n_scr = len(args), len(shapes), len(scratch_shapes)
    c_in, c_out = len(carry.inputs), len(carry.out_shapes)

    def wrapped(*refs):
        ins, c_ins = refs[:n_in], refs[n_in:n_in + c_in]
        o0 = n_in + c_in
        outs, c_outs = refs[o0:o0 + n_out], refs[o0 + n_out:o0 + n_out + c_out]
        s0 = o0 + n_out + c_out
        scr, c_scr = refs[s0:s0 + n_scr], refs[s0 + n_scr:]
        step, total = pl.program_id(0), grid[0]
        for ax in range(1, len(grid)):
            step, total = step * grid[ax] + pl.program_id(ax), total * grid[ax]

        @pl.when(step == 0)
        def _():
            carry.start(c_ins, c_outs, c_scr)

        @pl.when(step == total // 2)
        def _():
            carry.middle(c_ins, c_outs, c_scr)

        body(*ins, *outs, *scr)

        @pl.when(step == total - 1)
        def _():
            carry.finish(c_ins, c_outs, c_scr)

    res = pl.pallas_call(
        wrapped, out_shape=shapes + tuple(carry.out_shapes), grid=grid, in_specs=list(in_specs) + [_HBM] * c_in,
        out_specs=specs + (_HBM,) * c_out, scratch_shapes=list(scratch_shapes) + carry.scratch, name=name,
        compiler_params=params)(*args, *carry.inputs)
    main = res[:n_out]
    return (main[0] if single else tuple(main)), tuple(res[n_out:])


def _rms_bwd(dn, x, g):
    r = lax.rsqrt(jnp.mean(x * x, axis=-1, keepdims=True) + EPS)
    xh = x * r
    dxh = dn * g
    dx = r * (dxh - xh * jnp.mean(dxh * xh, axis=-1, keepdims=True))
    return dx, jnp.sum(dn * xh, axis=0, keepdims=True)


def _rmsnorm_fwd(name, x, g, tm, carry=None):
    T, D = x.shape

    def body(x_ref, g_ref, o_ref):
        xf = x_ref[...]
        r = lax.rsqrt(jnp.mean(xf * xf, axis=-1, keepdims=True) + EPS)
        o_ref[...] = (xf * r * g_ref[...]).astype(BF16)

    return _call(
        body, name=name, grid=(T // tm,), out_shape=jax.ShapeDtypeStruct((T, D), BF16),
        in_specs=[pl.BlockSpec((tm, D), lambda i: (i, 0)), pl.BlockSpec((1, D), lambda i: (0, 0))],
        out_specs=pl.BlockSpec((tm, D), lambda i: (i, 0)), args=[x, g], carry=carry)


def _outnorm_fwd(name, o_fox, o_swa, g_fox, g_swa, tm):
    T, Dh = o_fox.shape

    def body(a_ref, b_ref, ga_ref, gb_ref, o_ref):
        for ref, g_ref, lo in ((a_ref, ga_ref, 0), (b_ref, gb_ref, Dh)):
            xf = ref[...]
            r = lax.rsqrt(jnp.mean(xf * xf, axis=-1, keepdims=True) + EPS)
            o_ref[:, lo:lo + Dh] = (xf * r * g_ref[...]).astype(BF16)

    row = pl.BlockSpec((tm, Dh), lambda i: (i, 0))
    gain = pl.BlockSpec((1, Dh), lambda i: (0, 0))
    return pl.pallas_call(
        body, out_shape=jax.ShapeDtypeStruct((T, 2 * Dh), BF16), grid=(T // tm,),
        in_specs=[row, row, gain, gain], out_specs=pl.BlockSpec((tm, 2 * Dh), lambda i: (i, 0)),
        name=name, compiler_params=_params(1))(o_fox, o_swa, g_fox, g_swa)


def _outnorm_bwd(name, dhb, wout, half, o, g, tm):
    T, D = dhb.shape
    Dh = o.shape[1]

    def body(a_ref, w_ref, o_ref, g_ref, do_ref, dg_ref):
        don = _dot(a_ref[...], w_ref[...], NT)
        dx, dg = _rms_bwd(don, o_ref[...], g_ref[...])
        do_ref[...] = dx.astype(BF16)

        @pl.when(pl.program_id(0) == 0)
        def _():
            dg_ref[...] = dg

        @pl.when(pl.program_id(0) > 0)
        def _():
            dg_ref[...] += dg

    return pl.pallas_call(
        body, out_shape=(jax.ShapeDtypeStruct((T, Dh), BF16), jax.ShapeDtypeStruct((1, Dh), F32)), grid=(T // tm,),
        in_specs=[pl.BlockSpec((tm, D), lambda i: (i, 0)), pl.BlockSpec((Dh, D), lambda i: (half, 0)),
                  pl.BlockSpec((tm, Dh), lambda i: (i, 0)), pl.BlockSpec((1, Dh), lambda i: (0, 0))],
        out_specs=(pl.BlockSpec((tm, Dh), lambda i: (i, 0)), pl.BlockSpec((1, Dh), lambda i: (0, 0))),
        name=name, compiler_params=_params(1))(dhb, wout, o, g)


def _mm(name, a, b, tm, tn, dims=NN, resid=None, carry=None, b_rows=None, bf16_copy=False):
    M, K = a.shape
    transposed = dims == NT
    N = b.shape[0] if transposed else b.shape[1]
    if b_rows is not None:
        N = b_rows[0] * tn

    def body(*refs):
        if bf16_copy:
            a_ref, b_ref, o_ref, ob_ref = refs
            res = _dot(a_ref[...], b_ref[...], dims)
            o_ref[...] = res
            ob_ref[...] = res.astype(BF16)
        elif resid is None:
            a_ref, b_ref, o_ref = refs
            o_ref[...] = _dot(a_ref[...], b_ref[...], dims)
        else:
            a_ref, b_ref, r_ref, o_ref = refs
            o_ref[...] = r_ref[...] + _dot(a_ref[...], b_ref[...], dims)

    ospec = pl.BlockSpec((tm, tn), lambda n, i: (i, n))
    bspec = pl.BlockSpec((tn, K), lambda n, i: (n, 0)) if transposed else pl.BlockSpec((K, tn), lambda n, i: (0, n))
    if b_rows is not None:
        bspec = pl.BlockSpec((pl.Element(tn), pl.Element(K)), lambda n, i: (b_rows[1](n), 0))
    in_specs = [pl.BlockSpec((tm, K), lambda n, i: (i, 0)), bspec]
    args = [a, b]
    if resid is not None:
        in_specs.append(ospec)
        args.append(resid)
    if bf16_copy:
        assert resid is None
        return _call(body, name=name, grid=(N // tn, M // tm), in_specs=in_specs, out_specs=(ospec, ospec),
                     out_shape=(jax.ShapeDtypeStruct((M, N), F32), jax.ShapeDtypeStruct((M, N), BF16)), args=args, carry=carry)
    return _call(body, name=name, grid=(N // tn, M // tm), in_specs=in_specs, out_specs=ospec,
                 out_shape=jax.ShapeDtypeStruct((M, N), F32), args=args, carry=carry)


def _wgrad_2d(name, a, b, tmm, tn, carry=None, out_rows=None):
    T, M = a.shape
    N = b.shape[1]

    def body(a_ref, b_ref, o_ref):
        o_ref[...] = _dot(a_ref[...], b_ref[...], TN).astype(BF16)

    out_spec = pl.BlockSpec((tmm, tn), lambda m, n: (m, n))
    if out_rows is not None:
        out_spec = pl.BlockSpec((pl.Element(tmm), pl.Element(tn)), lambda m, n: (out_rows[1](m), n * tn))
    return _call(
        body, name=name, grid=(M // tmm, N // tn), out_shape=jax.ShapeDtypeStruct((M if out_rows is None else out_rows[0], N), BF16),
        in_specs=[pl.BlockSpec((T, tmm), lambda m, n: (0, m)), pl.BlockSpec((T, tn), lambda m, n: (0, n))],
        out_specs=out_spec, args=[a, b], carry=carry)


def _wgrad_down(name, hm, df, tn, carry=None):
    J, T, Fs = hm.shape
    D = df.shape[1]

    def body(a_ref, b_ref, o_ref):
        o_ref[...] = _dot(a_ref[...], b_ref[...], TN).astype(BF16)

    return _call(
        body, name=name, grid=(J, D // tn), out_shape=jax.ShapeDtypeStruct((J, Fs, D), BF16),
        in_specs=[pl.BlockSpec((None, T, Fs), lambda j, n: (j, 0, 0)), pl.BlockSpec((T, tn), lambda j, n: (0, n))],
        out_specs=pl.BlockSpec((None, Fs, tn), lambda j, n: (j, 0, n)), args=[hm, df], carry=carry)


def _wgrad_up(name, n, da, db, tn, carry=None):
    T, D = n.shape
    J, _, Fs = da.shape

    def body(n_ref, da_ref, db_ref, og_ref, ou_ref):
        nv = n_ref[...]
        og_ref[...] = _dot(da_ref[...], nv, TN).astype(BF16)
        ou_ref[...] = _dot(db_ref[...], nv, TN).astype(BF16)

    act = pl.BlockSpec((None, T, Fs), lambda j, m: (j, 0, 0))
    out = pl.BlockSpec((None, Fs, tn), lambda j, m: (j, 0, m))
    shape = jax.ShapeDtypeStruct((J, Fs, D), BF16)
    return _call(
        body, name=name, grid=(J, D // tn), out_shape=(shape, shape),
        in_specs=[pl.BlockSpec((T, tn), lambda j, m: (0, m)), act, act], out_specs=(out, out),
        args=[n, da, db], carry=carry)


def _reduce_mm(name, pairs, once, dims, T, D, tm, steps, init=None, carry=None):
    n_pairs = len(pairs)
    n_once = len(once)
    n_mm = 2 * (n_pairs + n_once)

    def body(*refs):
        pr = refs[:2 * n_pairs]
        on = refs[2 * n_pairs:n_mm]
        o_ref = refs[-1]
        r = pl.program_id(1)

        @pl.when(r == 0)
        def _():
            o_ref[...] = jnp.zeros(o_ref.shape, F32) if init is None else refs[n_mm][...]

        for p in range(n_pairs):
            o_ref[...] += _dot(pr[2 * p][...], pr[2 * p + 1][...], dims)

        if n_once:
            @pl.when(r == steps - 1)
            def _():
                for p in range(n_once):
                    o_ref[...] += _dot(on[2 * p][...], on[2 * p + 1][...], dims)

    in_specs, args = [], []
    for a, a_spec, w, w_spec in list(pairs) + list(once):
        in_specs += [a_spec, w_spec]
        args += [a, w]
    row = pl.BlockSpec((tm, D), lambda i, r: (i, 0))
    if init is not None:
        in_specs.append(row)
        args.append(init)
    return _call(body, name=name, grid=(T // tm, steps), in_specs=in_specs, out_specs=row, out_shape=jax.ShapeDtypeStruct((T, D), F32),
                 args=args, carry=carry)


def _rmsnorm_bwd(name, dn, x, g, dh, tm, bf16_scale, carry=None):
    T, D = x.shape
    emit_bf16 = bf16_scale is not None

    def body(dn_ref, x_ref, g_ref, dh_ref, *outs):
        dxn, dg = _rms_bwd(dn_ref[...], x_ref[...], g_ref[...])
        dx = dh_ref[...] + dxn
        outs[0][...] = dx
        if emit_bf16:
            outs[2][...] = (bf16_scale * dx).astype(BF16)

        @pl.when(pl.program_id(0) == 0)
        def _():
            outs[1][...] = dg

        @pl.when(pl.program_id(0) > 0)
        def _():
            outs[1][...] += dg

    row = pl.BlockSpec((tm, D), lambda i: (i, 0))
    gain = pl.BlockSpec((1, D), lambda i: (0, 0))
    out_shape = [jax.ShapeDtypeStruct((T, D), F32), jax.ShapeDtypeStruct((1, D), F32)]
    out_specs = [row, gain]
    if emit_bf16:
        out_shape.append(jax.ShapeDtypeStruct((T, D), BF16))
        out_specs.append(row)
    return _call(body, name=name, grid=(T // tm,), in_specs=[row, row, gain, row], out_specs=tuple(out_specs),
                 out_shape=tuple(out_shape), args=[dn, x, g, dh], carry=carry)


def _loss_grad(name, y, target, tm):
    T, D = y.shape

    def body(y_ref, t_ref, dy_ref, dyh_ref, sq_ref):
        diff = y_ref[...] - t_ref[...]
        sq = jnp.sum(jnp.sum(diff * diff, axis=1, keepdims=True), axis=0, keepdims=True)
        dy = diff * (1.0 / D)
        dy_ref[...] = dy
        dyh_ref[...] = (0.5 * dy).astype(BF16)

        @pl.when(pl.program_id(0) == 0)
        def _():
            sq_ref[...] = sq

        @pl.when(pl.program_id(0) > 0)
        def _():
            sq_ref[...] += sq

    row = pl.BlockSpec((tm, D), lambda i: (i, 0))
    return pl.pallas_call(
        body, out_shape=(jax.ShapeDtypeStruct((T, D), F32), jax.ShapeDtypeStruct((T, D), BF16), jax.ShapeDtypeStruct((1, 1), F32)),
        grid=(T // tm,), in_specs=[row, row], out_specs=(row, row, pl.BlockSpec((1, 1), lambda i: (0, 0))),
        name=name, compiler_params=_params(1))(y, target)


def _ffn_up(name, n, wg, wu, tm, carry=None):
    T, D = n.shape
    J, Fs, _ = wg.shape

    def body(n_ref, wg_ref, wu_ref, a_ref, b_ref, h_ref):
        xv = n_ref[...]
        a = _dot(xv, wg_ref[...], NT)
        b = _dot(xv, wu_ref[...], NT)
        a_ref[...] = a.astype(BF16)
        b_ref[...] = b.astype(BF16)
        h_ref[...] = (a * jax.nn.sigmoid(a) * b).astype(BF16)

    act = jax.ShapeDtypeStruct((J, T, Fs), BF16)
    wspec = pl.BlockSpec((None, Fs, D), lambda j, i: (j, 0, 0))
    aspec = pl.BlockSpec((None, tm, Fs), lambda j, i: (j, i, 0))
    return _call(
        body, name=name, grid=(J, T // tm), out_shape=(act, act, act),
        in_specs=[pl.BlockSpec((tm, D), lambda j, i: (i, 0)), wspec, wspec], out_specs=(aspec, aspec, aspec),
        args=[n, wg, wu], carry=carry)


def _ffn_gate(name, n, wg, tm, carry=None):
    T, D = n.shape
    J, Fs, _ = wg.shape

    def body(n_ref, wg_ref, a_ref):
        a_ref[...] = _dot(n_ref[...], wg_ref[...], NT).astype(BF16)

    aspec = pl.BlockSpec((None, tm, Fs), lambda j, i: (j, i, 0))
    return _call(
        body, name=name, grid=(J, T // tm), out_shape=jax.ShapeDtypeStruct((J, T, Fs), BF16),
        in_specs=[pl.BlockSpec((tm, D), lambda j, i: (i, 0)), pl.BlockSpec((None, Fs, D), lambda j, i: (j, 0, 0))],
        out_specs=aspec, args=[n, wg], carry=carry)


def _ffn_up_only(name, n, wu, a, tm, carry=None):
    T, D = n.shape
    J, Fs, _ = wu.shape

    def body(n_ref, wu_ref, a_ref, b_ref, h_ref):
        b = _dot(n_ref[...], wu_ref[...], NT)
        a = a_ref[...].astype(F32)
        b_ref[...] = b.astype(BF16)
        h_ref[...] = (a * jax.nn.sigmoid(a) * b).astype(BF16)

    act = jax.ShapeDtypeStruct((J, T, Fs), BF16)
    aspec = pl.BlockSpec((None, tm, Fs), lambda j, i: (j, i, 0))
    return _call(
        body, name=name, grid=(J, T // tm), out_shape=(act, act),
        in_specs=[pl.BlockSpec((tm, D), lambda j, i: (i, 0)), pl.BlockSpec((None, Fs, D), lambda j, i: (j, 0, 0)), aspec],
        out_specs=(aspec, aspec), args=[n, wu, a], carry=carry)


def _ffn_down(name, hm, wd, resid, tm, tn, carry=None):
    J, T, Fs = hm.shape
    D = wd.shape[2]

    def body(h_ref, w_ref, r_ref, o_ref):
        @pl.when(pl.program_id(2) == 0)
        def _():
            o_ref[...] = r_ref[...]

        o_ref[...] += _dot(h_ref[...] * 0.5, w_ref[...], NN)

    tile = pl.BlockSpec((tm, tn), lambda i, n, j: (i, n))
    return _call(
        body, name=name, grid=(T // tm, D // tn, J), out_shape=jax.ShapeDtypeStruct((T, D), F32),
        in_specs=[pl.BlockSpec((None, tm, Fs), lambda i, n, j: (j, i, 0)), pl.BlockSpec((None, Fs, tn), lambda i, n, j: (j, 0, n)), tile],
        out_specs=tile, args=[hm, wd, resid], carry=carry)


def _ffn_bwd_mid(name, dfh, wd, a, b, tm, carry=None):
    T, D = dfh.shape
    J, Fs, _ = wd.shape

    def body(df_ref, w_ref, a_ref, b_ref, da_ref, db_ref):
        dhm = _dot(df_ref[...], w_ref[...], NT)
        av = a_ref[...].astype(F32)
        bv = b_ref[...].astype(F32)
        sg = jax.nn.sigmoid(av)
        da_ref[...] = (dhm * bv * (sg * (1.0 + av * (1.0 - sg)))).astype(BF16)
        db_ref[...] = (dhm * (av * sg)).astype(BF16)

    act = jax.ShapeDtypeStruct((J, T, Fs), BF16)
    aspec = pl.BlockSpec((None, tm, Fs), lambda j, i: (j, i, 0))
    return _call(
        body, name=name, grid=(J, T // tm), out_shape=(act, act),
        in_specs=[pl.BlockSpec((tm, D), lambda j, i: (i, 0)), pl.BlockSpec((None, Fs, D), lambda j, i: (j, 0, 0)), aspec, aspec],
        out_specs=(aspec, aspec), args=[dfh, wd, a, b], carry=carry)


def _rot_half(y, lane):
    first = (lane & (HEAD_DIM // 2)) == 0
    return jnp.where(first, pltpu.roll(y, LANES - HEAD_DIM // 2, 1), pltpu.roll(y, HEAD_DIM // 2, 1))


def _head_rstd(x, lo):
    sq = x * x
    ss_a = jnp.sum(jnp.where(lo, sq, 0.0), axis=-1, keepdims=True)
    ss_b = jnp.sum(jnp.where(lo, 0.0, sq), axis=-1, keepdims=True)
    return lax.rsqrt(jnp.where(lo, ss_a, ss_b) * (1.0 / HEAD_DIM) + EPS)


def _headnorm_fwd(name, proj, col_off, ncb, gains, tm, scale, rope=None, dup=False):
    T = proj.shape[0]
    with_rope = rope is not None
    width = 2 * LANES if dup else LANES

    def body(*refs):
        if with_rope:
            x_ref, g_ref, cos_ref, sin_ref, o_ref = refs
        else:
            x_ref, g_ref, o_ref = refs
        xv = x_ref[...]
        lane = lax.broadcasted_iota(jnp.int32, xv.shape, 1)
        lo = lane < HEAD_DIM
        y = xv * _head_rstd(xv, lo) * g_ref[...]
        if with_rope:
            y = y * cos_ref[...] + _rot_half(y, lane) * sin_ref[...]
        y = y * scale
        if dup:
            sw = pltpu.roll(y, HEAD_DIM, 1)
            o_ref[:, :LANES] = jnp.where(lo, y, sw).astype(BF16)
            o_ref[:, LANES:] = jnp.where(lo, sw, y).astype(BF16)
        else:
            o_ref[...] = y.astype(BF16)

    in_specs = [pl.BlockSpec((tm, LANES), lambda c, i: (i, col_off + c)), pl.BlockSpec((None, 1, LANES), lambda c, i: (c, 0, 0))]
    args = [proj, gains]
    if with_rope:
        tab = pl.BlockSpec((tm, LANES), lambda c, i: (i, 0))
        in_specs += [tab, tab]
        args += list(rope)
    return pl.pallas_call(
        body, out_shape=jax.ShapeDtypeStruct((T, ncb * width), BF16), grid=(ncb, T // tm),
        in_specs=in_specs, out_specs=pl.BlockSpec((tm, width), lambda c, i: (i, c)),
        name=name, compiler_params=_params(2))(*args)


def _headnorm_bwd(name, dy, proj, col_off, ncb, gains, group, tm, scale, rope=None, fold=False, norm=True):
    T = dy.shape[0]
    with_rope = rope is not None
    n_groups = ncb // group
    dy_width = 4 * LANES if fold else LANES

    def body(*refs):
        refs = list(refs)
        dy_ref = refs.pop(0)
        x_ref = refs.pop(0) if norm else None
        g_ref = refs.pop(0) if norm else None
        cos_ref = refs.pop(0) if with_rope else None
        sin_ref = refs.pop(0) if with_rope else None
        dx_ref = refs.pop(0)
        dg_ref = refs.pop(0) if norm else None
        c = pl.program_id(0)
        i = pl.program_id(1)
        d = dy_ref[...]
        lane = lax.broadcasted_iota(jnp.int32, (d.shape[0], LANES), 1)
        lo = lane < HEAD_DIM
        if fold:
            t0 = d[:, 0:LANES] + d[:, LANES:2 * LANES]
            t1 = d[:, 2 * LANES:3 * LANES] + d[:, 3 * LANES:4 * LANES]
            d = jnp.where(lo, t0 + pltpu.roll(t0, HEAD_DIM, 1), t1 + pltpu.roll(t1, HEAD_DIM, 1))
        d = d * scale
        if with_rope:
            d = d * cos_ref[...] + _rot_half(d * sin_ref[...], lane)
        if not norm:
            dx_ref[...] = d.astype(BF16)
            return
        xv = x_ref[...]
        gv = g_ref[...]
        r = _head_rstd(xv, lo)
        xh = xv * r
        dxh = d * gv
        pr = dxh * xh
        m_a = jnp.sum(jnp.where(lo, pr, 0.0), axis=-1, keepdims=True)
        m_b = jnp.sum(jnp.where(lo, 0.0, pr), axis=-1, keepdims=True)
        mean = jnp.where(lo, m_a, m_b) * (1.0 / HEAD_DIM)
        dx_ref[...] = (r * (dxh - xh * mean)).astype(BF16)
        dgp = jnp.sum(d * xh, axis=0, keepdims=True)
        dgp = dgp + pltpu.roll(dgp, HEAD_DIM, 1)
        first = jnp.logical_and(c % group == 0, i == 0)

        @pl.when(first)
        def _():
            dg_ref[...] = dgp

        @pl.when(jnp.logical_not(first))
        def _():
            dg_ref[...] += dgp

    in_specs = [pl.BlockSpec((tm, dy_width), lambda c, i: (i, c))]
    args = [dy]
    if norm:
        in_specs += [pl.BlockSpec((tm, LANES), lambda c, i: (i, col_off + c)), pl.BlockSpec((None, 1, LANES), lambda c, i: (c, 0, 0))]
        args += [proj, gains]
    if with_rope:
        tab = pl.BlockSpec((tm, LANES), lambda c, i: (i, 0))
        in_specs += [tab, tab]
        args += list(rope)
    out_shape = [jax.ShapeDtypeStruct((T, ncb * LANES), BF16)]
    out_specs = [pl.BlockSpec((tm, LANES), lambda c, i: (i, c))]
    if norm:
        out_shape.append(jax.ShapeDtypeStruct((n_groups, 1, LANES), F32))
        out_specs.append(pl.BlockSpec((None, 1, LANES), lambda c, i: (c // group, 0, 0)))
    res = pl.pallas_call(
        body, out_shape=tuple(out_shape), grid=(ncb, T // tm), in_specs=in_specs, out_specs=tuple(out_specs),
        name=name, compiler_params=_params(2))(*args)
    return res if norm else (res[0], None)


def _dot_exact(x, tri):
    hi = x.astype(BF16)
    r1 = x - hi.astype(F32)
    mid = r1.astype(BF16)
    lo = (r1 - mid.astype(F32)).astype(BF16)
    return _dot(hi, tri, NN) + _dot(mid, tri, NN) + _dot(lo, tri, NN)


def _forget_fwd(name, zt, bias):
    H, T = zt.shape
    blk = min(256, T)

    def body(z_ref, b_ref, c_ref, s_ref):
        z = z_ref[...] + b_ref[...]
        s_ref[...] = jax.nn.sigmoid(-z)
        lf = jnp.minimum(z, 0.0) - jnp.log(1.0 + jnp.exp(-jnp.abs(z)))
        tri = (lax.broadcasted_iota(jnp.int32, (blk, blk), 0) <= lax.broadcasted_iota(jnp.int32, (blk, blk), 1)).astype(BF16)
        carry = jnp.zeros((H, 1), F32)
        for bi in range(T // blk):
            xb = lf[:, bi * blk:(bi + 1) * blk]
            c_ref[:, bi * blk:(bi + 1) * blk] = _dot_exact(xb, tri) + carry
            carry = carry + jnp.sum(xb, axis=-1, keepdims=True)

    shape = jax.ShapeDtypeStruct((H, T), F32)
    full = pl.BlockSpec((H, T), lambda i: (0, 0))
    return pl.pallas_call(
        body, out_shape=(shape, shape), grid=(1,), in_specs=[full, pl.BlockSpec((H, 1), lambda i: (0, 0))],
        out_specs=(full, full), name=name, compiler_params=_params(1))(zt, bias)


def _forget_bwd(name, dct, drt, sgt):
    H, T = dct.shape
    blk = min(256, T)

    def body(dc_ref, dr_ref, s_ref, dz_ref, db_ref):
        dc = dc_ref[...] + dr_ref[...]
        tri = (lax.broadcasted_iota(jnp.int32, (blk, blk), 0) >= lax.broadcasted_iota(jnp.int32, (blk, blk), 1)).astype(BF16)
        carry = jnp.zeros((H, 1), F32)
        db = jnp.zeros((H, 1), F32)
        for bi in reversed(range(T // blk)):
            xb = dc[:, bi * blk:(bi + 1) * blk]
            dz = (_dot_exact(xb, tri) + carry) * s_ref[:, bi * blk:(bi + 1) * blk]
            dz_ref[:, bi * blk:(bi + 1) * blk] = dz
            db = db + jnp.sum(dz, axis=-1, keepdims=True)
            carry = carry + jnp.sum(xb, axis=-1, keepdims=True)
        db_ref[...] = db

    full = pl.BlockSpec((H, T), lambda i: (0, 0))
    return pl.pallas_call(
        body, out_shape=(jax.ShapeDtypeStruct((H, T), F32), jax.ShapeDtypeStruct((H, 1), F32)), grid=(1,),
        in_specs=[full, full, full], out_specs=(full, pl.BlockSpec((H, 1), lambda i: (0, 0))),
        name=name, compiler_params=_params(1))(dct, drt, sgt)


FOX_STRIP_FWD = 128
FOX_STRIP_BWD = 256


def _fox_fwd(name, qk, v, crow, tq, tk, strip, carry=None):
    T, Dh = v.shape
    HP = Dh // LANES
    nk = T // tk
    assert tk % tq == 0 and tq % strip == 0
    n_strips = tq // strip

    def body(q_ref, k_ref, v_ref, ra_ref, rb_ref, o_ref, la_ref, lb_ref, s_ref, p_ref, m_ref, l_ref, acc_ref):
        i = pl.program_id(1)
        q2 = q_ref[...]
        lo = _lane_lo((tq, LANES))
        q_st = jnp.concatenate([_keep(lo, q2), _keep(jnp.logical_not(lo), q2)], axis=0)
        r_refs = (ra_ref, rb_ref)
        m_ref[...] = jnp.full(m_ref.shape, NEG, F32)
        l_ref[...] = jnp.zeros(l_ref.shape, F32)
        acc_ref[...] = jnp.zeros(acc_ref.shape, F32)
        rel = lax.broadcasted_iota(jnp.int32, (strip, tk), 0) - lax.broadcasted_iota(jnp.int32, (strip, tk), 1)

        def chunk(kc, masked):
            start = pl.multiple_of(kc * tk, tk)
            kb = k_ref[pl.ds(start, tk), :]
            vb = v_ref[pl.ds(start, tk), :]
            s_ref[...] = _dot(q_st, kb, NT)
            for h in range(2):
                cs = r_refs[h][kc]
                for st in range(n_strips):
                    rows = pl.ds(h * tq + st * strip, strip)
                    s = s_ref[rows, :] - cs
                    if masked:
                        s = jnp.where(rel >= start - (i * tq + st * strip), s, NEG)
                    m_old = m_ref[rows, :]
                    mn = jnp.maximum(m_old, jnp.max(s, axis=-1, keepdims=True))
                    p = jnp.exp(s - mn)
                    alpha = jnp.exp(m_old - mn)
                    l_ref[rows, :] = alpha * l_ref[rows, :] + jnp.sum(p, axis=-1, keepdims=True)
                    m_ref[rows, :] = mn
                    p_ref[rows, :] = p.astype(BF16)
                    acc_ref[rows, :] = acc_ref[rows, :] * alpha
            acc_ref[...] += _dot(p_ref[...], vb, NN)

        n_full = (i * tq) // tk

        def full_chunk(kc, _):
            chunk(kc, False)
            return 0

        lax.fori_loop(0, n_full, full_chunk, 0)
        chunk(n_full, True)
        top, bot = pl.ds(0, tq), pl.ds(tq, tq)
        o_ref[...] = jnp.where(lo, acc_ref[top, :] / l_ref[top, :], acc_ref[bot, :] / l_ref[bot, :])
        la_ref[...] = m_ref[top, :] + jnp.log(l_ref[top, :])
        lb_ref[...] = m_ref[bot, :] + jnp.log(l_ref[bot, :])

    row = lambda off: pl.BlockSpec((None, nk, 1, tk), lambda h, i: (2 * h + off, 0, 0, 0))
    lse = jax.ShapeDtypeStruct((HP, T, 1), F32)
    lspec = pl.BlockSpec((None, tq, 1), lambda h, i: (h, i, 0))
    scratch = [pltpu.VMEM((2 * tq, tk), F32), pltpu.VMEM((2 * tq, tk), BF16), pltpu.VMEM((2 * tq, 1), F32),
               pltpu.VMEM((2 * tq, 1), F32), pltpu.VMEM((2 * tq, LANES), F32)]
    return _call(
        body, name=name, grid=(HP, T // tq), out_shape=(jax.ShapeDtypeStruct((T, Dh), F32), lse, lse),
        in_specs=[pl.BlockSpec((tq, LANES), lambda h, i: (i, h)), pl.BlockSpec((T, LANES), lambda h, i: (0, HP + h)),
                  pl.BlockSpec((T, LANES), lambda h, i: (0, h)), row(0), row(1)],
        out_specs=(pl.BlockSpec((tq, LANES), lambda h, i: (i, h)), lspec, lspec),
        args=[qk, qk, v, crow, crow], scratch_shapes=scratch, carry=carry)


def _fox_bwd(name, qk, v, o, do, crow, lse_a, lse_b, tq, tk, strip, carry=None):
    T, Dh = v.shape
    HP = Dh // LANES
    nk = T // tk
    scale = HEAD_DIM ** -0.5
    assert tk % tq == 0 and tq % strip == 0
    n_strips = tq // strip

    def body(q_ref, k_ref, v_ref, o_ref, do_ref, ra_ref, rb_ref, la_ref, lb_ref,
             dq_ref, dk_ref, dv_ref, dca_ref, dcb_ref, dra_ref, drb_ref, s_ref, dp_ref, p_ref, ds_ref, dq_acc, dsum_ref):
        i = pl.program_id(1)

        @pl.when(i == 0)
        def _():
            dk_ref[...] = jnp.zeros_like(dk_ref)
            dv_ref[...] = jnp.zeros_like(dv_ref)
            dca_ref[...] = jnp.zeros_like(dca_ref)
            dcb_ref[...] = jnp.zeros_like(dcb_ref)

        q2 = q_ref[...]
        do2 = do_ref[...]
        lo = _lane_lo((tq, LANES))
        hi = jnp.logical_not(lo)
        q_st = jnp.concatenate([_keep(lo, q2), _keep(hi, q2)], axis=0)
        do_st = jnp.concatenate([_keep(lo, do2), _keep(hi, do2)], axis=0)
        prod = do2.astype(F32) * o_ref[...]
        dsum_ref[pl.ds(0, tq), :] = jnp.sum(jnp.where(lo, prod, 0.0), axis=-1, keepdims=True)
        dsum_ref[pl.ds(tq, tq), :] = jnp.sum(jnp.where(lo, 0.0, prod), axis=-1, keepdims=True)
        r_refs, l_refs, dc_refs, dr_refs = (ra_ref, rb_ref), (la_ref, lb_ref), (dca_ref, dcb_ref), (dra_ref, drb_ref)
        dq_acc[...] = jnp.zeros(dq_acc.shape, F32)
        dra_ref[...] = jnp.zeros(dra_ref.shape, F32)
        drb_ref[...] = jnp.zeros(drb_ref.shape, F32)
        rel = lax.broadcasted_iota(jnp.int32, (strip, tk), 0) - lax.broadcasted_iota(jnp.int32, (strip, tk), 1)

        def chunk(kc, masked):
            start = pl.multiple_of(kc * tk, tk)
            kb = k_ref[pl.ds(start, tk), :]
            vb = v_ref[pl.ds(start, tk), :]
            s_ref[...] = _dot(q_st, kb, NT)
            dp_ref[...] = _dot(do_st, vb, NT)
            for h in range(2):
                cs = r_refs[h][kc]
                col_sum = jnp.zeros((1, tk), F32)
                for st in range(n_strips):
                    rows = pl.ds(st * strip, strip)
                    both = pl.ds(h * tq + st * strip, strip)
                    s = s_ref[both, :] - cs
                    if masked:
                        s = jnp.where(rel >= start - (i * tq + st * strip), s, NEG)
                    p = jnp.exp(s - l_refs[h][rows, :])
                    ds = p * (dp_ref[both, :] - dsum_ref[both, :])
                    p_ref[both, :] = p.astype(BF16)
                    ds_ref[both, :] = ds.astype(BF16)
                    col_sum = col_sum + jnp.sum(ds, axis=0, keepdims=True)
                    dr_refs[h][rows, :] += jnp.sum(ds, axis=-1, keepdims=True)
                dc_refs[h][kc] = dc_refs[h][kc] - col_sum
            dk_ref[pl.ds(start, tk), :] += _dot(ds_ref[...], q_st, TN)
            dv_ref[pl.ds(start, tk), :] += _dot(p_ref[...], do_st, TN)
            dq_acc[...] += _dot(ds_ref[...], kb, NN)

        n_full = (i * tq) // tk

        def full_chunk(kc, _):
            chunk(kc, False)
            return 0

        lax.fori_loop(0, n_full, full_chunk, 0)
        chunk(n_full, True)
        dq_ref[...] = jnp.where(lo, dq_acc[pl.ds(0, tq), :], dq_acc[pl.ds(tq, tq), :]) * scale

    row = lambda off: pl.BlockSpec((None, nk, 1, tk), lambda h, i: (2 * h + off, 0, 0, 0))
    lspec = pl.BlockSpec((None, tq, 1), lambda h, i: (h, i, 0))
    qspec = pl.BlockSpec((tq, LANES), lambda h, i: (i, h))
    full = pl.BlockSpec((T, LANES), lambda h, i: (0, h))
    dcspec = pl.BlockSpec((None, nk, 1, tk), lambda h, i: (h, 0, 0, 0))
    grad = jax.ShapeDtypeStruct((T, Dh), F32)
    dc = jax.ShapeDtypeStruct((HP, nk, 1, tk), F32)
    dr = jax.ShapeDtypeStruct((HP, T, 1), F32)
    scratch = [pltpu.VMEM((2 * tq, tk), F32), pltpu.VMEM((2 * tq, tk), F32), pltpu.VMEM((2 * tq, tk), BF16), pltpu.VMEM((2 * tq, tk), BF16),
               pltpu.VMEM((2 * tq, LANES), F32), pltpu.VMEM((2 * tq, 1), F32)]
    return _call(
        body, name=name, grid=(HP, T // tq), out_shape=(grad, grad, grad, dc, dc, dr, dr),
        in_specs=[qspec, pl.BlockSpec((T, LANES), lambda h, i: (0, HP + h)), full, qspec, qspec, row(0), row(1), lspec, lspec],
        out_specs=(qspec, full, full, dcspec, dcspec, lspec, lspec),
        args=[qk, qk, v, o, do, crow, crow, lse_a, lse_b], scratch_shapes=scratch, carry=carry)


SWA_GROUP = 2
SWA_GROUP_BWD = 4


def _swa_block(n, q_ref, k_ref):
    qs = pl.multiple_of(n * WINDOW, WINDOW)
    ks = pl.multiple_of(jnp.maximum(n - 1, 0) * WINDOW, WINDOW)
    rel = (qs + lax.broadcasted_iota(jnp.int32, (WINDOW, 2 * WINDOW), 0)) - (ks + lax.broadcasted_iota(jnp.int32, (WINDOW, 2 * WINDOW), 1))
    valid = jnp.logical_and(rel >= 0, rel < WINDOW)
    return qs, ks, valid


def _swa_fwd(name, q, kd, vd, sinks, carry=None):
    T, Dh = q.shape
    HP = Dh // LANES

    def body(q_ref, k_ref, v_ref, sa_ref, sb_ref, o_ref, la_ref, lb_ref):
        lo = _lane_lo((WINDOW, LANES))

        top = lax.broadcasted_iota(jnp.int32, (2 * WINDOW, 1), 0) < WINDOW
        sink = jnp.where(top, sa_ref[...], sb_ref[...])

        def block(n, _):
            qs, ks, valid = _swa_block(n, q_ref, k_ref)
            q2 = q_ref[pl.ds(qs, WINDOW), :]
            kb = k_ref[pl.ds(ks, 2 * WINDOW), :]
            vb = v_ref[pl.ds(ks, 2 * WINDOW), :]
            q_st = jnp.concatenate([_keep(lo, q2), _keep(jnp.logical_not(lo), q2)], axis=0)
            s = jnp.where(jnp.concatenate([valid, valid], axis=0), _dot(q_st, kb, NT), NEG)
            m = jnp.maximum(jnp.max(s, axis=-1, keepdims=True), sink)
            p = jnp.exp(s - m)
            l = jnp.sum(p, axis=-1, keepdims=True) + jnp.exp(sink - m)
            o2 = _dot(p.astype(BF16), vb, NN) / l
            lse = m + jnp.log(l)
            o_ref[pl.ds(qs, WINDOW), :] = jnp.where(lo, o2[:WINDOW], o2[WINDOW:])
            la_ref[pl.ds(qs, WINDOW), :] = lse[:WINDOW]
            lb_ref[pl.ds(qs, WINDOW), :] = lse[WINDOW:]
            return 0

        assert (T // WINDOW) % SWA_GROUP == 0

        def group(g, c):
            for b in range(SWA_GROUP):
                c = block(g * SWA_GROUP + b, c)
            return c

        lax.fori_loop(0, T // WINDOW // SWA_GROUP, group, 0)

    full = pl.BlockSpec((T, LANES), lambda h: (0, h))
    kv = pl.BlockSpec((T, LANES), lambda h: (0, h // 2))
    sink = lambda off: pl.BlockSpec((None, 1, 1), lambda h: (2 * h + off, 0, 0))
    lse = jax.ShapeDtypeStruct((HP, T, 1), F32)
    lspec = pl.BlockSpec((None, T, 1), lambda h: (h, 0, 0))
    return _call(
        body, name=name, grid=(HP,), out_shape=(jax.ShapeDtypeStruct((T, Dh), F32), lse, lse),
        in_specs=[full, kv, kv, sink(0), sink(1)], out_specs=(full, lspec, lspec),
        args=[q, kd, vd, sinks, sinks], carry=carry)


def _swa_bwd(name, q, kd, vd, sinks, o, do, lse_a, lse_b, carry=None):
    T, Dh = q.shape
    HP = Dh // LANES
    scale = HEAD_DIM ** -0.5

    def body(q_ref, k_ref, v_ref, sa_ref, sb_ref, o_ref, do_ref, la_ref, lb_ref, dq_ref, dk_ref, dv_ref, dsa_ref, dsb_ref):
        lo = _lane_lo((WINDOW, LANES))
        hi = jnp.logical_not(lo)
        dk_ref[...] = jnp.zeros_like(dk_ref)
        dv_ref[...] = jnp.zeros_like(dv_ref)

        top = lax.broadcasted_iota(jnp.int32, (2 * WINDOW, 1), 0) < WINDOW
        sink = jnp.where(top, sa_ref[...], sb_ref[...])

        def block(n, dsinks):
            qs, ks, valid = _swa_block(n, q_ref, k_ref)
            rows = pl.ds(qs, WINDOW)
            q2 = q_ref[rows, :]
            do2 = do_ref[rows, :]
            kb = k_ref[pl.ds(ks, 2 * WINDOW), :]
            vb = v_ref[pl.ds(ks, 2 * WINDOW), :]
            prod = do2.astype(F32) * o_ref[rows, :]
            q_st = jnp.concatenate([_keep(lo, q2), _keep(hi, q2)], axis=0)
            do_st = jnp.concatenate([_keep(lo, do2), _keep(hi, do2)], axis=0)
            dsum = jnp.concatenate([jnp.sum(jnp.where(lo, prod, 0.0), axis=-1, keepdims=True),
                                    jnp.sum(jnp.where(lo, 0.0, prod), axis=-1, keepdims=True)], axis=0)
            lse = jnp.concatenate([la_ref[rows, :], lb_ref[rows, :]], axis=0)
            s = jnp.where(jnp.concatenate([valid, valid], axis=0), _dot(q_st, kb, NT), NEG)
            p = jnp.exp(s - lse)
            ds = p * (_dot(do_st, vb, NT) - dsum)
            dsb = ds.astype(BF16)
            dq2 = _dot(dsb, kb, NN)
            dq_ref[rows, :] = jnp.where(lo, dq2[:WINDOW], dq2[WINDOW:]) * scale
            dk_ref[pl.ds(ks, 2 * WINDOW), :] += _dot(dsb, q_st, TN)
            dv_ref[pl.ds(ks, 2 * WINDOW), :] += _dot(p.astype(BF16), do_st, TN)
            gone = jnp.exp(sink - lse) * dsum
            return (dsinks[0] - jnp.sum(gone[:WINDOW], axis=0, keepdims=True),
                    dsinks[1] - jnp.sum(gone[WINDOW:], axis=0, keepdims=True))

        assert (T // WINDOW) % SWA_GROUP_BWD == 0

        def group(g, c):
            for b in range(SWA_GROUP_BWD):
                c = block(g * SWA_GROUP_BWD + b, c)
            return c

        dsa, dsb_ = lax.fori_loop(0, T // WINDOW // SWA_GROUP_BWD, group, (jnp.zeros((1, 1), F32), jnp.zeros((1, 1), F32)))
        dsa_ref[...] = dsa
        dsb_ref[...] = dsb_

    full = pl.BlockSpec((T, LANES), lambda h: (0, h))
    kv = pl.BlockSpec((T, LANES), lambda h: (0, h // 2))
    sink = lambda off: pl.BlockSpec((None, 1, 1), lambda h: (2 * h + off, 0, 0))
    lspec = pl.BlockSpec((None, T, 1), lambda h: (h, 0, 0))
    dsink = pl.BlockSpec((None, 1, 1), lambda h: (h, 0, 0))
    grad = jax.ShapeDtypeStruct((T, Dh), F32)
    ds_shape = jax.ShapeDtypeStruct((HP, 1, 1), F32)
    return _call(
        body, name=name, grid=(HP,), out_shape=(grad, grad, grad, ds_shape, ds_shape),
        in_specs=[full, kv, kv, sink(0), sink(1), full, full, lspec, lspec],
        out_specs=(full, full, full, dsink, dsink),
        args=[q, kd, vd, sinks, sinks, o, do, lse_a, lse_b], carry=carry)


def _place():
    return lax.axis_index("x"), lax.axis_index("y"), lax.axis_index("c")


def _run_carry(name, carry):
    c_in, c_out = len(carry.inputs), len(carry.out_shapes)

    def body(*refs):
        ins, outs, scr = refs[:c_in], refs[c_in:c_in + c_out], refs[c_in + c_out:]
        carry.start(ins, outs, scr)
        carry.middle(ins, outs, scr)
        carry.finish(ins, outs, scr)

    return pl.pallas_call(
        body, out_shape=tuple(carry.out_shapes), in_specs=[_HBM] * c_in, out_specs=tuple([_HBM] * c_out),
        scratch_shapes=carry.scratch, name=name)(*carry.inputs)


def _gather_carry(shards):
    n = len(shards)

    def plan(ins, outs, scr):
        send, recv, local = scr
        x, y, c = _place()
        me, sibling = (x, y, c), (x, y, 1 - c)
        partner, other, diag = (x ^ c, y ^ (1 - c)), (x ^ (1 - c), y ^ c), (1 - x, 1 - y)

        def copy(w, k, block, to, src=None):
            slot = 4 * block[0] + 2 * block[1] + block[2]
            return pltpu.make_async_remote_copy(
                src_ref=outs[w].at[slot] if src is None else src, dst_ref=outs[w].at[slot],
                send_sem=send.at[w, k], recv_sem=recv.at[w, k], device_id=to, device_id_type=MESH)

        def own():
            return [pltpu.make_async_copy(ins[w], outs[w].at[4 * x + 2 * y + c], local.at[w]) for w in range(n)]

        return copy, own, me, sibling, partner, other, diag, c

    def start(ins, outs, scr):
        copy, own, me, sibling, partner, other, _, c = plan(ins, outs, scr)
        for cp in own():
            cp.start()
        for w in range(n):
            copy(w, 1, me, (*partner, c), src=ins[w]).start()
            copy(w, 2, me, (*other, c), src=ins[w]).start()
            copy(w, 0, me, sibling, src=ins[w]).start()

    def middle(ins, outs, scr):
        copy, _, me, sibling, partner, other, _, c = plan(ins, outs, scr)
        for w in range(n):
            copy(w, 1, (*partner, c), me).wait_recv()
            copy(w, 3, (*partner, c), (*other, c)).start()
            copy(w, 4, (*partner, c), sibling).start()
        for w in range(n):
            copy(w, 2, (*other, c), me).wait_recv()
            copy(w, 5, (*other, c), sibling).start()

    def finish(ins, outs, scr):
        copy, own, me, sibling, partner, other, diag, c = plan(ins, outs, scr)
        for w in range(n):
            copy(w, 3, (*diag, c), me).wait_recv()
            copy(w, 6, (*diag, c), sibling).start()
        for w in range(n):
            copy(w, 0, sibling, me).wait_recv()
            copy(w, 4, (*other, 1 - c), me).wait_recv()
            copy(w, 5, (*partner, 1 - c), me).wait_recv()
            copy(w, 6, (*diag, 1 - c), me).wait_recv()
        for w in range(n):
            sent = [copy(w, 0, me, sibling, src=ins[w]), copy(w, 1, me, (*partner, c), src=ins[w]), copy(w, 2, me, (*other, c), src=ins[w]),
                    copy(w, 3, (*partner, c), (*other, c)), copy(w, 4, (*partner, c), sibling), copy(w, 5, (*other, c), sibling),
                    copy(w, 6, (*diag, c), sibling)]
            for cp in sent:
                cp.wait_send()
        for cp in own():
            cp.wait()

    return _Carry(shards, [jax.ShapeDtypeStruct((N_DEV,) + s.shape, s.dtype) for s in shards],
                  [pltpu.SemaphoreType.DMA((n, 7)), pltpu.SemaphoreType.DMA((n, 7)), pltpu.SemaphoreType.DMA((n,))], start, finish, middle)


def _sibling_carry(grads):
    n = len(grads)

    def copies(ins, outs, scr):
        send, recv = scr
        x, y, c = _place()
        return [pltpu.make_async_remote_copy(
            src_ref=ins[w].at[2 * q + (1 - c)], dst_ref=outs[w].at[q], send_sem=send.at[w, q], recv_sem=recv.at[w, q],
            device_id=(x, y, 1 - c), device_id_type=MESH) for w in range(n) for q in range(4)]

    def start(ins, outs, scr):
        for cp in copies(ins, outs, scr):
            cp.start()

    def finish(ins, outs, scr):
        for cp in copies(ins, outs, scr):
            cp.wait()

    return _Carry(grads, [jax.ShapeDtypeStruct((4,) + g.shape[1:], g.dtype) for g in grads],
                  [pltpu.SemaphoreType.DMA((n, 4)), pltpu.SemaphoreType.DMA((n, 4))], start, finish)


def _to_partner_carry(sums):
    n = len(sums)

    def copies(ins, outs, scr):
        send, recv = scr
        x, y, c = _place()
        return [pltpu.make_async_remote_copy(
            src_ref=ins[w].at[k], dst_ref=outs[2 * w + k], send_sem=send.at[w, k], recv_sem=recv.at[w, k],
            device_id=(x ^ c, y ^ (1 - c), c), device_id_type=MESH) for w in range(n) for k in range(2)]

    def start(ins, outs, scr):
        for cp in copies(ins, outs, scr):
            cp.start()

    def finish(ins, outs, scr):
        for cp in copies(ins, outs, scr):
            cp.wait()

    return _Carry(sums, [jax.ShapeDtypeStruct(s.shape[1:], s.dtype) for s in sums for _ in range(2)],
                  [pltpu.SemaphoreType.DMA((n, 2)), pltpu.SemaphoreType.DMA((n, 2))], start, finish)


def _to_other_carry(blocks):
    n = len(blocks)

    def copies(ins, outs, scr):
        send, recv = scr
        x, y, c = _place()
        return [pltpu.make_async_remote_copy(
            src_ref=ins[w], dst_ref=outs[w], send_sem=send.at[w], recv_sem=recv.at[w],
            device_id=(x ^ (1 - c), y ^ c, c), device_id_type=MESH) for w in range(n)]

    def start(ins, outs, scr):
        for cp in copies(ins, outs, scr):
            cp.start()

    def finish(ins, outs, scr):
        for cp in copies(ins, outs, scr):
            cp.wait()

    return _Carry(blocks, [jax.ShapeDtypeStruct(b.shape, b.dtype) for b in blocks],
                  [pltpu.SemaphoreType.DMA((n,)), pltpu.SemaphoreType.DMA((n,))], start, finish)


def _gather_small(packed):
    R, C = packed.shape

    def body(in_ref, out_ref, send, recv):
        x, y, c = _place()
        mine = 4 * x + 2 * y + c
        out_ref[mine] = in_ref[...]
        copies = []
        for k in range(1, N_DEV):
            peer = (x ^ (k >> 2), y ^ ((k >> 1) & 1), c ^ (k & 1))
            copies.append(pltpu.make_async_remote_copy(
                src_ref=in_ref, dst_ref=out_ref.at[mine], send_sem=send.at[k - 1], recv_sem=recv.at[k - 1],
                device_id=peer, device_id_type=MESH))
        for cp in copies:
            cp.start()
        for cp in copies:
            cp.wait()

    vmem = pl.BlockSpec(memory_space=pltpu.VMEM)
    return pl.pallas_call(
        body, out_shape=jax.ShapeDtypeStruct((N_DEV, R, C), F32), in_specs=[vmem], out_specs=vmem,
        scratch_shapes=[pltpu.SemaphoreType.DMA((N_DEV - 1,)), pltpu.SemaphoreType.DMA((N_DEV - 1,))],
        name="small_grads_all_gather")(packed)


def _adamw(w, g, m, v):
    m = ADAM_B1 * m + (1.0 - ADAM_B1) * g
    v = ADAM_B2 * v + (1.0 - ADAM_B2) * (g * g)
    m_hat = m / (1.0 - ADAM_B1 ** ADAM_STEP)
    v_hat = v / (1.0 - ADAM_B2 ** ADAM_STEP)
    delta = -ADAM_LR * (m_hat / (jnp.sqrt(v_hat) + ADAM_EPS) + ADAM_WD * w)
    return delta, m, v


def _pair_add(name, grads, received, slots):
    _, R, C = grads.shape
    tr = _row_tile(R, 2 * ROW_TILE_CAP)

    def body(s_ref, g_ref, r_ref, o_ref):
        o_ref[...] = (g_ref[...].astype(F32) + r_ref[...].astype(F32)).astype(BF16)

    return pl.pallas_call(
        body, out_shape=jax.ShapeDtypeStruct((3, R, C), BF16),
        grid_spec=pltpu.PrefetchScalarGridSpec(
            num_scalar_prefetch=1, grid=(3, R // tr),
            in_specs=[pl.BlockSpec((None, tr, C), lambda k, i, s: (s[k], i, 0)), pl.BlockSpec((None, tr, C), lambda k, i, s: (s[3 + k], i, 0))],
            out_specs=pl.BlockSpec((None, tr, C), lambda k, i, s: (k, i, 0))),
        name=name, compiler_params=_params(2))(slots, grads, received)


def _relay_add(name, sums, relayed):
    _, R, C = sums.shape
    tr = _row_tile(R, 2 * ROW_TILE_CAP)

    def body(s_ref, r_ref, o_ref):
        o_ref[...] = (s_ref[...].astype(F32) + r_ref[...].astype(F32)).astype(BF16)

    blk = pl.BlockSpec((tr, C), lambda i: (i, 0))
    return pl.pallas_call(
        body, out_shape=jax.ShapeDtypeStruct((R, C), BF16), grid=(R // tr,),
        in_specs=[pl.BlockSpec((None, tr, C), lambda i: (2, i, 0)), blk], out_specs=blk,
        name=name, compiler_params=_params(1))(sums, relayed)


def _adam_shard(name, grads, from_sibling, received, w, m, v, own):
    R, C = w.shape
    tr = _row_tile(R, 256)
    tc = C if tr < R or C % (4 * LANES) else 4 * LANES

    def body(o_ref, g_ref, s_ref, ra_ref, rb_ref, w_ref, m_ref, v_ref, g_out, d_out, m_out, v_out):
        g = (g_ref[...].astype(F32) + s_ref[...].astype(F32)) + ra_ref[...].astype(F32) + rb_ref[...].astype(F32)
        delta, mn, vn = _adamw(w_ref[...], g, m_ref[...], v_ref[...])
        g_out[...] = g
        d_out[...] = delta
        m_out[...] = mn
        v_out[...] = vn

    blk = pl.BlockSpec((tr, tc), lambda i, j, o: (i, j))
    shape = jax.ShapeDtypeStruct((R, C), F32)
    return pl.pallas_call(
        body, out_shape=(shape,) * 4,
        grid_spec=pltpu.PrefetchScalarGridSpec(
            num_scalar_prefetch=1, grid=(R // tr, C // tc),
            in_specs=[pl.BlockSpec((None, tr, tc), lambda i, j, o: (o[0], i, j)), pl.BlockSpec((None, tr, tc), lambda i, j, o: (o[1], i, j)),
                      blk, blk, blk, blk, blk],
            out_specs=(blk,) * 4),
        name=name, compiler_params=_params(2))(own, grads, from_sibling, received[0], received[1], w, m, v)


def _adam_small(name, gathered, w, m, v):
    R, C = w.shape

    def body(ga_ref, w_ref, m_ref, v_ref, g_out, d_out, m_out, v_out):
        g = ga_ref[0]
        for d in range(1, N_DEV):
            g = g + ga_ref[d]
        delta, mn, vn = _adamw(w_ref[...], g, m_ref[...], v_ref[...])
        g_out[...] = g
        d_out[...] = delta
        m_out[...] = mn
        v_out[...] = vn

    full = pl.BlockSpec((R, C), lambda i: (0, 0))
    shape = jax.ShapeDtypeStruct((R, C), F32)
    return pl.pallas_call(
        body, out_shape=(shape,) * 4, grid=(1,),
        in_specs=[pl.BlockSpec((N_DEV, R, C), lambda i: (0, 0, 0)), full, full, full], out_specs=(full,) * 4,
        name=name, compiler_params=_params(1))(gathered, w, m, v)


def _pack_small(parts, D, scalar=None):
    g1, gmix, g2, gof, gos, bf, gqf, gkf, gqs, gks, sinks = [p.reshape(-1).astype(F32) for p in parts]
    row3 = jnp.concatenate([gof, gos])
    row4 = jnp.zeros((D,), F32)
    for slot, vec in enumerate((bf, gqf, gkf, gqs, gks, sinks)):
        row4 = lax.dynamic_update_slice(row4, vec, (slot * LANES,))
    zero = jnp.zeros((D,), F32)
    row5 = zero if scalar is None else lax.dynamic_update_slice(zero, jnp.reshape(scalar, (1,)).astype(F32), (0,))
    return jnp.stack([g1, gmix, g2, row3, row4, row5, zero, zero])


def _unpack_small(packed, D, H):
    Dh = D // 2
    row4 = packed[4]
    short = [row4[s * LANES:s * LANES + n] for s, n in enumerate((H, HEAD_DIM, HEAD_DIM, HEAD_DIM, HEAD_DIM, H))]
    vecs = [packed[0], packed[1], packed[2], packed[3, :Dh], packed[3, Dh:]] + short
    return [v[None, :] for v in vecs]


def kernel(x, positions, norm_ffn1_g, ffn1_w_gate, ffn1_w_up, ffn1_w_down, norm_mix_g, w_in, b_forget, fox_q_norm_g, fox_k_norm_g, swa_q_norm_g, swa_k_norm_g, swa_sinks, out_norm_fox_g, out_norm_swa_g, w_out, norm_ffn2_g, ffn2_w_gate, ffn2_w_up, ffn2_w_down, loss_target, m_norm_ffn1_g, m_ffn1_w_gate, m_ffn1_w_up, m_ffn1_w_down, m_norm_mix_g, m_w_in, m_b_forget, m_fox_q_norm_g, m_fox_k_norm_g, m_swa_q_norm_g, m_swa_k_norm_g, m_swa_sinks, m_out_norm_fox_g, m_out_norm_swa_g, m_w_out, m_norm_ffn2_g, m_ffn2_w_gate, m_ffn2_w_up, m_ffn2_w_down, v_norm_ffn1_g, v_ffn1_w_gate, v_ffn1_w_up, v_ffn1_w_down, v_norm_mix_g, v_w_in, v_b_forget, v_fox_q_norm_g, v_fox_k_norm_g, v_swa_q_norm_g, v_swa_k_norm_g, v_swa_sinks, v_out_norm_fox_g, v_out_norm_swa_g, v_w_out, v_norm_ffn2_g, v_ffn2_w_gate, v_ffn2_w_up, v_ffn2_w_down):
    xs = x[0]
    target = loss_target[0]
    T, D = xs.shape
    Dh = D // 2
    H = Dh // HEAD_DIM
    HP = H // 2
    KVW = (H // GQA_GROUP) * HEAD_DIM
    KVB = KVW // LANES
    MAIN = 4 * Dh + 2 * KVW
    F_OFF = 3 * Dh
    tm = min(ROW_TILE_CAP, T)
    tm2 = min(2 * ROW_TILE_CAP, T)
    tq = min(512, T)
    tk = min(512, T)
    nk = T // tk
    cx, cy, cc = _place()
    near = [2 * (cx ^ cc) + (cy ^ (1 - cc)), 2 * (1 - cx) + (1 - cy), 2 * (cx ^ (1 - cc)) + (cy ^ cc)]
    slots = jnp.stack([2 * q + cc for q in near] + near).astype(jnp.int32)
    own = jnp.stack([4 * cx + 2 * cy + cc, 2 * cx + cy]).astype(jnp.int32)

    tr = jnp.transpose
    big_w = [tr(ffn1_w_gate[0]), tr(ffn1_w_up[0]), ffn1_w_down[0], tr(w_in[0]), w_out[0], tr(ffn2_w_gate[0]), tr(ffn2_w_up[0]),
             ffn2_w_down[0]]
    big_m = [tr(m_ffn1_w_gate[0]), tr(m_ffn1_w_up[0]), m_ffn1_w_down[0], tr(m_w_in[0]), m_w_out[0], tr(m_ffn2_w_gate[0]),
             tr(m_ffn2_w_up[0]), m_ffn2_w_down[0]]
    big_v = [tr(v_ffn1_w_gate[0]), tr(v_ffn1_w_up[0]), v_ffn1_w_down[0], tr(v_w_in[0]), v_w_out[0], tr(v_ffn2_w_gate[0]),
             tr(v_ffn2_w_up[0]), v_ffn2_w_down[0]]
    transposed = {"ffn1_w_gate", "ffn1_w_up", "w_in", "ffn2_w_gate", "ffn2_w_up"}
    names = ["ffn1_w_gate", "ffn1_w_up", "ffn1_w_down", "w_in", "w_out", "ffn2_w_gate", "ffn2_w_up", "ffn2_w_down"]
    sh = dict(zip(names, [w.astype(BF16) for w in big_w]))
    lane = jnp.arange(LANES)
    inv_freq = ROPE_THETA ** (-(2.0 * (lane % (HEAD_DIM // 2))).astype(F32) / HEAD_DIM)
    ang = positions[0].astype(F32)[:, None] * inv_freq[None, :]
    cos_t = jnp.cos(ang)
    sin_t = jnp.where((lane & (HEAD_DIM // 2)) == 0, -1.0, 1.0)[None, :] * jnp.sin(ang)
    rope = (cos_t, sin_t)

    def pair_gain(g, blocks):
        return jnp.tile(jnp.concatenate([g[0], g[0]])[None, None, :], (blocks, 1, 1))

    n1, (wg1,) = _rmsnorm_fwd("ffn1_norm", xs, norm_ffn1_g, tm, carry=_gather_carry([sh["ffn1_w_gate"]]))
    a1, (wu1,) = _ffn_gate("ffn1_gate", n1, wg1, tm, carry=_gather_carry([sh["ffn1_w_up"]]))
    (b1, hm1), (wd1,) = _ffn_up_only("ffn1_up", n1, wu1, a1, tm, carry=_gather_carry([sh["ffn1_w_down"]]))
    h1, (win_g,) = _ffn_down("ffn1_down", hm1, wd1, xs, tm, D, carry=_gather_carry([sh["w_in"]]))
    n_in = win_g.shape[1]
    win_t = win_g.reshape(N_DEV * n_in, D)
    win_f = jnp.pad(win_t[F_OFF:F_OFF + H], ((0, LANES - H), (0, 0)))
    tkb = MAIN // 9
    assert F_OFF % tkb == 0

    def main_row(r):
        return pl.multiple_of(r * tkb + H * (r >= F_OFF // tkb).astype(jnp.int32), min(H, tkb))

    u = _rmsnorm_fwd("mix_norm", h1, norm_mix_g, tm)
    (proj, proj_b), (wout_g,) = _mm("mix_proj", u, win_t, T, tkb, dims=NT, carry=_gather_carry([sh["w_out"]]),
                                    b_rows=(9, main_row), bf16_copy=True)
    wout = wout_g.reshape(D, D)
    proj_f = _mm("mix_proj_forget", u, win_f, T, LANES, dims=NT)
    scale = HEAD_DIM ** -0.5
    fox_gains = jnp.concatenate([pair_gain(fox_q_norm_g, HP), pair_gain(fox_k_norm_g, HP)])
    qk_f = _headnorm_fwd_scaled("fox_qk_norm", proj, 0, 2 * HP, fox_gains, T, scale, HP)
    v_f = proj_b[:, 2 * Dh:3 * Dh]
    c_t, sg_t = _forget_fwd("forget_gates", proj_f[:, :H].T, b_forget.reshape(H, 1))
    crow = c_t.reshape(H, nk, 1, tk)
    (o_fox, lse_fa, lse_fb), (wg2, wu2) = _fox_fwd("fox_attention", qk_f, v_f, crow, tq, tk, min(FOX_STRIP_FWD, tq),
                                                   carry=_gather_carry([sh["ffn2_w_gate"], sh["ffn2_w_up"]]))

    swa_q_gains = pair_gain(swa_q_norm_g, HP)
    swa_k_gains = pair_gain(swa_k_norm_g, KVB)
    q_s = _headnorm_fwd("swa_q_norm", proj, 3 * HP, HP, swa_q_gains, T, scale, rope=rope)
    k_d = _headnorm_fwd("swa_k_norm", proj, 4 * HP, KVB, swa_k_gains, T, 1.0, rope=rope, dup=True)
    v_s = proj_b[:, 4 * Dh + KVW:].reshape(T, H // GQA_GROUP, 1, HEAD_DIM)
    v_d = jnp.broadcast_to(v_s, (T, H // GQA_GROUP, 2, HEAD_DIM)).reshape(T, 2 * KVW)
    sinks3 = swa_sinks.reshape(H, 1, 1)
    o_swa, lse_sa, lse_sb = _swa_fwd("swa_attention", q_s, k_d, v_d, sinks3)

    on = _outnorm_fwd("out_norm", o_fox, o_swa, out_norm_fox_g, out_norm_swa_g, tm)
    h2 = _mm("mix_out", on, wout, tm2, min(512, D), resid=h1)

    n2 = _rmsnorm_fwd("ffn2_norm", h2, norm_ffn2_g, tm)
    (a2, b2, hm2), (wd2,) = _ffn_up("ffn2_up", n2, wg2, wu2, tm2, carry=_gather_carry([sh["ffn2_w_down"]]))
    y = _ffn_down("ffn2_down", hm2, wd2, h2, tm2, D)
    dy, dyh, sq = _loss_grad("loss_grad", y, target, tm)
    loss_part = 0.5 * sq[0, 0] / D

    J, Fs, _ = wg2.shape
    aspec = pl.BlockSpec((None, tm, Fs), lambda i, j: (j, i, 0))
    wspec = pl.BlockSpec((None, Fs, D), lambda i, j: (j, 0, 0))
    got = {}
    local = {}

    def pair_sums(keys, grads, received):
        for nm, g, r in zip(keys, grads, received):
            local[nm] = (g, r)
        return [_pair_add("sum_" + nm, g, r, slots) for nm, g, r in zip(keys, grads, received)]

    def relay_sums(keys, sums, hop1):
        out = []
        for i, (nm, s) in enumerate(zip(keys, sums)):
            got[nm] = [hop1[2 * i]]
            out.append(_relay_add("relay_" + nm, s, hop1[2 * i + 1]))
        return out

    def arrived(keys, hop2):
        for nm, blk in zip(keys, hop2):
            got[nm].append(blk)

    dwd2 = _wgrad_down("ffn2_wgrad_down", hm2, dyh, D)
    (da2, db2), (sib_d2,) = _ffn_bwd_mid("ffn2_bwd_mid", dyh, wd2, a2, b2, tm2, carry=_sibling_carry([dwd2]))
    (sum_wd2,) = pair_sums(names[7:8], [dwd2], [sib_d2])
    (dwg2, dwu2), hop1 = _wgrad_up("ffn2_wgrad_up", n2, da2, db2, min(1024, D), carry=_to_partner_carry([sum_wd2]))
    (t_wd2,) = relay_sums(names[7:8], [sum_wd2], hop1)
    aspec2 = pl.BlockSpec((None, tm2, Fs), lambda i, j: (j, i, 0))
    dn2, (via_wd2, *sib2) = _reduce_mm("ffn2_bwd_in", [(da2, aspec2, wg2, wspec), (db2, aspec2, wu2, wspec)], [], NN, T, D, tm2, J,
                                       carry=_join(_to_other_carry([t_wd2]), _sibling_carry([dwg2, dwu2])))
    arrived(names[7:8], [via_wd2])
    dh2, dg_ffn2, dh2b = _rmsnorm_bwd("ffn2_norm_bwd", dn2, h2, norm_ffn2_g, dy, min(256, T), 1.0)
    sum_wg2, sum_wu2 = pair_sums(names[5:7], [dwg2, dwu2], sib2)

    dwout = _wgrad_2d("mix_out_wgrad", on, dh2b, min(512, D), D)
    dwout_g = dwout.reshape(N_DEV, D // N_DEV, D)
    do_fox, dg_of = _outnorm_bwd("out_norm_bwd_fox", dh2b, wout, 0, o_fox, out_norm_fox_g, tm)
    do_swa, dg_os = _outnorm_bwd("out_norm_bwd_swa", dh2b, wout, 1, o_swa, out_norm_swa_g, tm)

    (dq_f, dk_f, dv_f, dc_a, dc_b, dr_a, dr_b), (*hop1, sib_wout) = _fox_bwd(
        "fox_attention_bwd", qk_f, v_f, o_fox, do_fox, crow, lse_fa, lse_fb, tq, tk, min(FOX_STRIP_BWD, tq),
        carry=_join(_to_partner_carry([sum_wg2, sum_wu2]), _sibling_carry([dwout_g])))
    t_wg2, t_wu2 = relay_sums(names[5:7], [sum_wg2, sum_wu2], hop1)
    (sum_wout,) = pair_sums(names[4:5], [dwout_g], [sib_wout])
    dqf_raw, dg_fq = _headnorm_bwd("fox_q_norm_bwd", dq_f, proj, 0, HP, fox_gains[:HP], HP, T, 1.0)
    dkf_raw, dg_fk = _headnorm_bwd("fox_k_norm_bwd", dk_f, proj, HP, HP, fox_gains[HP:], HP, T, 1.0)
    dct = jnp.stack([dc_a.reshape(HP, T), dc_b.reshape(HP, T)], axis=1).reshape(H, T)
    drt = jnp.stack([dr_a.reshape(HP, T), dr_b.reshape(HP, T)], axis=1).reshape(H, T)
    dz_t, db_f = _forget_bwd("forget_gates_bwd", dct, drt, sg_t)

    (dq_s, dk_p, dv_p, dsink_a, dsink_b), hop2 = _swa_bwd(
        "swa_attention_bwd", q_s, k_d, v_d, sinks3, o_swa, do_swa, lse_sa, lse_sb, carry=_to_other_carry([t_wg2, t_wu2]))
    arrived(names[5:7], hop2)
    dqs_raw, dg_sq = _headnorm_bwd("swa_q_norm_bwd", dq_s, proj, 3 * HP, HP, swa_q_gains, HP, T, 1.0, rope=rope)
    dks_raw, dg_sk = _headnorm_bwd("swa_k_norm_bwd", dk_p, proj, 4 * HP, KVB, swa_k_gains, KVB, T, 1.0, rope=rope, fold=True)
    dvs_raw, _ = _headnorm_bwd("swa_v_fold", dv_p, None, 0, KVB, None, KVB, T, 1.0, fold=True, norm=False)

    dproj = jnp.concatenate([dqf_raw, dkf_raw, dv_f.astype(BF16), dqs_raw, dks_raw, dvs_raw], axis=1)
    dproj_f = jnp.pad(dz_t.T, ((0, 0), (0, LANES - H))).astype(BF16)
    dwin_t, hop1 = _wgrad_2d("mix_proj_wgrad", dproj, u, tkb, D, carry=_to_partner_carry([sum_wout]),
                             out_rows=(N_DEV * n_in, main_row))
    (t_wout,) = relay_sums(names[4:5], [sum_wout], hop1)
    dwin_f = _wgrad_2d("mix_proj_forget_wgrad", dproj_f, u, LANES, min(1024, D))
    dwin_t = lax.dynamic_update_slice(dwin_t, dwin_f[:H], (F_OFF, 0))
    dwin_g = dwin_t.reshape(N_DEV, n_in, D)
    du, (via_wout, sib_win) = _reduce_mm(
        "mix_bwd_in",
        [(dproj, pl.BlockSpec((tm2, tkb), lambda i, r: (i, r)), win_t, pl.BlockSpec((pl.Element(tkb), pl.Element(D)), lambda i, r: (main_row(r), 0)))],
        [(dproj_f, pl.BlockSpec((tm2, LANES), lambda i, r: (i, 0)), win_f, pl.BlockSpec((LANES, D), lambda i, r: (0, 0)))],
        NN, T, D, tm2, 9, carry=_join(_to_other_carry([t_wout]), _sibling_carry([dwin_g])))
    arrived(names[4:5], [via_wout])
    dh1, dg_mix, dh1h = _rmsnorm_bwd("mix_norm_bwd", du, h1, norm_mix_g, dh2, min(256, T), 0.5)
    (sum_win,) = pair_sums(names[3:4], [dwin_g], [sib_win])

    dwd1, hop1 = _wgrad_down("ffn1_wgrad_down", hm1, dh1h, D, carry=_to_partner_carry([sum_win]))
    (t_win,) = relay_sums(names[3:4], [sum_win], hop1)
    (da1, db1), (via_win, sib_d) = _ffn_bwd_mid("ffn1_bwd_mid", dh1h, wd1, a1, b1, tm2,
                                                carry=_join(_to_other_carry([t_win]), _sibling_carry([dwd1])))
    arrived(names[3:4], [via_win])
    (sum_wd1,) = pair_sums(names[2:3], [dwd1], [sib_d])
    dwg1, hop1 = _wgrad_down("ffn1_wgrad_gate", da1, n1, D, carry=_to_partner_carry([sum_wd1]))
    (t_wd1,) = relay_sums(names[2:3], [sum_wd1], hop1)
    dwu1, (via_wd1, sib_g) = _wgrad_down("ffn1_wgrad_up", db1, n1, D,
                                         carry=_join(_to_other_carry([t_wd1]), _sibling_carry([dwg1])))
    arrived(names[2:3], [via_wd1])
    (sum_wg1,) = pair_sums(names[0:1], [dwg1], [sib_g])
    dn1_gate, (*hop1, sib_u) = _reduce_mm(
        "ffn1_bwd_in_gate", [(da1, aspec2, wg1, wspec)], [], NN, T, D, tm2, J,
        carry=_join(_to_partner_carry([sum_wg1]), _sibling_carry([dwu1])))
    (t_wg1,) = relay_sums(names[0:1], [sum_wg1], hop1)
    (sum_wu1,) = pair_sums(names[1:2], [dwu1], [sib_u])
    dn1, (via_wg1, *hop1) = _reduce_mm(
        "ffn1_bwd_in_up", [(db1, aspec2, wu1, wspec)], [], NN, T, D, tm2, J, init=dn1_gate,
        carry=_join(_to_other_carry([t_wg1]), _to_partner_carry([sum_wu1])))
    arrived(names[0:1], [via_wg1])
    (t_wu1,) = relay_sums(names[1:2], [sum_wu1], hop1)
    arrived(names[1:2], _run_carry("grads_exchange", _to_other_carry([t_wu1])))
    dx, dg_ffn1 = _rmsnorm_bwd("ffn1_norm_bwd", dn1, xs, norm_ffn1_g, dh1, min(256, T), None)

    big_out = [_adam_shard("adam_" + nm, local[nm][0], local[nm][1], got[nm], w, m, v, own)
               for nm, w, m, v in zip(names, big_w, big_m, big_v)]

    dsinks = jnp.stack([dsink_a.reshape(HP), dsink_b.reshape(HP)], axis=1).reshape(H)
    small_g = [dg_ffn1, dg_mix, dg_ffn2, dg_of, dg_os, db_f, dg_fq[0, 0, :HEAD_DIM], dg_fk[0, 0, :HEAD_DIM],
               dg_sq[0, 0, :HEAD_DIM], dg_sk[0, 0, :HEAD_DIM], dsinks]
    small_w = [norm_ffn1_g, norm_mix_g, norm_ffn2_g, out_norm_fox_g, out_norm_swa_g, b_forget, fox_q_norm_g, fox_k_norm_g,
               swa_q_norm_g, swa_k_norm_g, swa_sinks]
    small_m = [m_norm_ffn1_g, m_norm_mix_g, m_norm_ffn2_g, m_out_norm_fox_g, m_out_norm_swa_g, m_b_forget, m_fox_q_norm_g,
               m_fox_k_norm_g, m_swa_q_norm_g, m_swa_k_norm_g, m_swa_sinks]
    small_v = [v_norm_ffn1_g, v_norm_mix_g, v_norm_ffn2_g, v_out_norm_fox_g, v_out_norm_swa_g, v_b_forget, v_fox_q_norm_g,
               v_fox_k_norm_g, v_swa_q_norm_g, v_swa_k_norm_g, v_swa_sinks]
    gathered = _gather_small(_pack_small(small_g, D, loss_part))
    small_out = _adam_small("adam_small", gathered, _pack_small(small_w, D), _pack_small(small_m, D), _pack_small(small_v, D))
    loss = small_out[0][5, 0]
    small_out = [_unpack_small(p, D, H) for p in small_out]

    order = ["norm_ffn1_g", "ffn1_w_gate", "ffn1_w_up", "ffn1_w_down", "norm_mix_g", "w_in", "b_forget", "fox_q_norm_g", "fox_k_norm_g",
             "swa_q_norm_g", "swa_k_norm_g", "swa_sinks", "out_norm_fox_g", "out_norm_swa_g", "w_out", "norm_ffn2_g",
             "ffn2_w_gate", "ffn2_w_up", "ffn2_w_down"]
    small_names = ["norm_ffn1_g", "norm_mix_g", "norm_ffn2_g", "out_norm_fox_g", "out_norm_swa_g", "b_forget", "fox_q_norm_g",
                   "fox_k_norm_g", "swa_q_norm_g", "swa_k_norm_g", "swa_sinks"]
    result = [loss, dx[None]]
    for kind in range(4):
        for nm in order:
            if nm in names:
                leaf = big_out[names.index(nm)][kind]
                result.append((tr(leaf) if nm in transposed else leaf)[None])
            else:
                result.append(small_out[kind][small_names.index(nm)])
    return tuple(result)


def _headnorm_fwd_scaled(name, proj, col_off, ncb, gains, tm, scale, n_scaled):
    T = proj.shape[0]

    def body(x_ref, g_ref, o_ref):
        xv = x_ref[...]
        lo = _lane_lo(xv.shape)
        y = xv * _head_rstd(xv, lo) * g_ref[...]
        y = y * jnp.where(pl.program_id(0) < n_scaled, scale, 1.0)
        o_ref[...] = y.astype(BF16)

    return pl.pallas_call(
        body, out_shape=jax.ShapeDtypeStruct((T, ncb * LANES), BF16), grid=(ncb, T // tm),
        in_specs=[pl.BlockSpec((tm, LANES), lambda c, i: (i, col_off + c)), pl.BlockSpec((None, 1, LANES), lambda c, i: (c, 0, 0))],
        out_specs=pl.BlockSpec((tm, LANES), lambda c, i: (i, c)), name=name, compiler_params=_params(2))(proj, gains)
```

```python
import jax
import jax.numpy as jnp
from jax import lax
from jax.experimental import pallas as pl
from jax.experimental.pallas import tpu as pltpu

F32 = jnp.float32
BF16 = jnp.bfloat16

HEAD_DIM = 64
LANES = 128
WINDOW = 128
GQA_GROUP = 4
EPS = 1e-6
ROPE_THETA = 10000.0
ADAM_LR = 0.001
ADAM_B1 = 0.9
ADAM_B2 = 0.999
ADAM_EPS = 1e-08
ADAM_WD = 0.01
ADAM_STEP = 10
N_DEV = 8
NEG = -1e30
VMEM_LIMIT_V7X = 48 * 1024 * 1024
ROW_TILE_CAP = 512
MESH = pl.DeviceIdType.MESH

NN = (((1,), (0,)), ((), ()))
NT = (((1,), (1,)), ((), ()))
TN = (((0,), (0,)), ((), ()))


def _dot(a, b, dims):
    return lax.dot_general(a, b, dims, preferred_element_type=F32)


def _params(n_axes):
    return pltpu.CompilerParams(dimension_semantics=("arbitrary",) * n_axes, vmem_limit_bytes=VMEM_LIMIT_V7X)


def _row_tile(rows, cap=ROW_TILE_CAP):
    best = None
    for t in range(16, min(rows, cap) + 1, 16):
        if rows % t == 0:
            best = t
    return best or rows


def _lane_lo(shape):
    return lax.broadcasted_iota(jnp.int32, shape, len(shape) - 1) < HEAD_DIM


def _keep(sel, x):
    return jnp.where(sel, x.astype(F32), 0.0).astype(BF16)


_HBM = pl.BlockSpec(memory_space=pltpu.HBM)


class _Carry:
    def __init__(self, inputs, out_shapes, scratch, start, finish, middle=None):
        self.inputs, self.out_shapes, self.scratch = list(inputs), list(out_shapes), list(scratch)
        self.start, self.finish, self.middle = start, finish, middle or (lambda ins, outs, scr: None)


def _join(*carries):
    def hook(which):
        def run(ins, outs, scr):
            i = o = s = 0
            for c in carries:
                ni, no, ns = len(c.inputs), len(c.out_shapes), len(c.scratch)
                getattr(c, which)(ins[i:i + ni], outs[o:o + no], scr[s:s + ns])
                i, o, s = i + ni, o + no, s + ns
        return run

    return _Carry([a for c in carries for a in c.inputs], [a for c in carries for a in c.out_shapes],
                  [a for c in carries for a in c.scratch], hook("start"), hook("finish"), hook("middle"))


def _call(body, *, name, grid, in_specs, out_specs, out_shape, args, scratch_shapes=(), carry=None):
    params = _params(len(grid))
    if carry is None:
        return pl.pallas_call(body, out_shape=out_shape, grid=grid, in_specs=list(in_specs), out_specs=out_specs,
                              scratch_shapes=list(scratch_shapes), name=name, compiler_params=params)(*args)
    single = not isinstance(out_shape, (tuple, list))
    shapes = (out_shape,) if single else tuple(out_shape)
    specs = (out_specs,) if single else tuple(out_specs)
    n_in, n_out, n_scr = len(args), len(shapes), len(scratch_shapes)
    c_in, c_out = len(carry.inputs), len(carry.out_shapes)

    def wrapped(*refs):
        ins, c_ins = refs[:n_in], refs[n_in:n_in + c_in]
        o0 = n_in + c_in
        outs, c_outs = refs[o0:o0 + n_out], refs[o0 + n_out:o0 + n_out + c_out]
        s0 = o0 + n_out + c_out
        scr, c_scr = refs[s0:s0 + n_scr], refs[s0 + n_scr:]
        step, total = pl.program_id(0), grid[0]
        for ax in range(1, len(grid)):
            step, total = step * grid[ax] + pl.program_id(ax), total * grid[ax]

        @pl.when(step == 0)
        def _():
            carry.start(c_ins, c_outs, c_scr)

        @pl.when(step == total // 2)
        def _():
            carry.middle(c_ins, c_outs, c_scr)

        body(*ins, *outs, *scr)

        @pl.when(step == total - 1)
        def _():
            carry.finish(c_ins, c_outs, c_scr)

    res = pl.pallas_call(
        wrapped, out_shape=shapes + tuple(carry.out_shapes), grid=grid, in_specs=list(in_specs) + [_HBM] * c_in,
        out_specs=specs + (_HBM,) * c_out, scratch_shapes=list(scratch_shapes) + carry.scratch, name=name,
        compiler_params=params)(*args, *carry.inputs)
    main = res[:n_out]
    return (main[0] if single else tuple(main)), tuple(res[n_out:])


def _rms_bwd(dn, x, g):
    r = lax.rsqrt(jnp.mean(x * x, axis=-1, keepdims=True) + EPS)
    xh = x * r
    dxh = dn * g
    dx = r * (dxh - xh * jnp.mean(dxh * xh, axis=-1, keepdims=True))
    return dx, jnp.sum(dn * xh, axis=0, keepdims=True)


def _rmsnorm_fwd(name, x, g, tm, carry=None):
    T, D = x.shape

    def body(x_ref, g_ref, o_ref):
        xf = x_ref[...]
        r = lax.rsqrt(jnp.mean(xf * xf, axis=-1, keepdims=True) + EPS)
        o_ref[...] = (xf * r * g_ref[...]).astype(BF16)

    return _call(
        body, name=name, grid=(T // tm,), out_shape=jax.ShapeDtypeStruct((T, D), BF16),
        in_specs=[pl.BlockSpec((tm, D), lambda i: (i, 0)), pl.BlockSpec((1, D), lambda i: (0, 0))],
        out_specs=pl.BlockSpec((tm, D), lambda i: (i, 0)), args=[x, g], carry=carry)


def _outnorm_fwd(name, o_fox, o_swa, g_fox, g_swa, tm):
    T, Dh = o_fox.shape

    def body(a_ref, b_ref, ga_ref, gb_ref, o_ref):
        for ref, g_ref, lo in ((a_ref, ga_ref, 0), (b_ref, gb_ref, Dh)):
            xf = ref[...]
            r = lax.rsqrt(jnp.mean(xf * xf, axis=-1, keepdims=True) + EPS)
            o_ref[:, lo:lo + Dh] = (xf * r * g_ref[...]).astype(BF16)

    row = pl.BlockSpec((tm, Dh), lambda i: (i, 0))
    gain = pl.BlockSpec((1, Dh), lambda i: (0, 0))
    return pl.pallas_call(
        body, out_shape=jax.ShapeDtypeStruct((T, 2 * Dh), BF16), grid=(T // tm,),
        in_specs=[row, row, gain, gain], out_specs=pl.BlockSpec((tm, 2 * Dh), lambda i: (i, 0)),
        name=name, compiler_params=_params(1))(o_fox, o_swa, g_fox, g_swa)


def _outnorm_bwd(name, dhb, wout, half, o, g, tm):
    T, D = dhb.shape
    Dh = o.shape[1]

    def body(a_ref, w_ref, o_ref, g_ref, do_ref, dg_ref):
        don = _dot(a_ref[...], w_ref[...], NT)
        dx, dg = _rms_bwd(don, o_ref[...], g_ref[...])
        do_ref[...] = dx.astype(BF16)

        @pl.when(pl.program_id(0) == 0)
        def _():
            dg_ref[...] = dg

        @pl.when(pl.program_id(0) > 0)
        def _():
            dg_ref[...] += dg

    return pl.pallas_call(
        body, out_shape=(jax.ShapeDtypeStruct((T, Dh), BF16), jax.ShapeDtypeStruct((1, Dh), F32)), grid=(T // tm,),
        in_specs=[pl.BlockSpec((tm, D), lambda i: (i, 0)), pl.BlockSpec((Dh, D), lambda i: (half, 0)),
                  pl.BlockSpec((tm, Dh), lambda i: (i, 0)), pl.BlockSpec((1, Dh), lambda i: (0, 0))],
        out_specs=(pl.BlockSpec((tm, Dh), lambda i: (i, 0)), pl.BlockSpec((1, Dh), lambda i: (0, 0))),
        name=name, compiler_params=_params(1))(dhb, wout, o, g)


def _mm(name, a, b, tm, tn, dims=NN, resid=None, carry=None, b_rows=None, bf16_copy=False):
    M, K = a.shape
    transposed = dims == NT
    N = b.shape[0] if transposed else b.shape[1]
    if b_rows is not None:
        N = b_rows[0] * tn

    def body(*refs):
        if bf16_copy:
            a_ref, b_ref, o_ref, ob_ref = refs
            res = _dot(a_ref[...], b_ref[...], dims)
            o_ref[...] = res
            ob_ref[...] = res.astype(BF16)
        elif resid is None:
            a_ref, b_ref, o_ref = refs
            o_ref[...] = _dot(a_ref[...], b_ref[...], dims)
        else:
            a_ref, b_ref, r_ref, o_ref = refs
            o_ref[...] = r_ref[...] + _dot(a_ref[...], b_ref[...], dims)

    ospec = pl.BlockSpec((tm, tn), lambda n, i: (i, n))
    bspec = pl.BlockSpec((tn, K), lambda n, i: (n, 0)) if transposed else pl.BlockSpec((K, tn), lambda n, i: (0, n))
    if b_rows is not None:
        bspec = pl.BlockSpec((pl.Element(tn), pl.Element(K)), lambda n, i: (b_rows[1](n), 0))
    in_specs = [pl.BlockSpec((tm, K), lambda n, i: (i, 0)), bspec]
    args = [a, b]
    if resid is not None:
        in_specs.append(ospec)
        args.append(resid)
    if bf16_copy:
        assert resid is None
        return _call(body, name=name, grid=(N // tn, M // tm), in_specs=in_specs, out_specs=(ospec, ospec),
                     out_shape=(jax.ShapeDtypeStruct((M, N), F32), jax.ShapeDtypeStruct((M, N), BF16)), args=args, carry=carry)
    return _call(body, name=name, grid=(N // tn, M // tm), in_specs=in_specs, out_specs=ospec,
                 out_shape=jax.ShapeDtypeStruct((M, N), F32), args=args, carry=carry)


def _wgrad_2d(name, a, b, tmm, tn, carry=None, out_rows=None):
    T, M = a.shape
    N = b.shape[1]

    def body(a_ref, b_ref, o_ref):
        o_ref[...] = _dot(a_ref[...], b_ref[...], TN).astype(BF16)

    out_spec = pl.BlockSpec((tmm, tn), lambda m, n: (m, n))
    if out_rows is not None:
        out_spec = pl.BlockSpec((pl.Element(tmm), pl.Element(tn)), lambda m, n: (out_rows[1](m), n * tn))
    return _call(
        body, name=name, grid=(M // tmm, N // tn), out_shape=jax.ShapeDtypeStruct((M if out_rows is None else out_rows[0], N), BF16),
        in_specs=[pl.BlockSpec((T, tmm), lambda m, n: (0, m)), pl.BlockSpec((T, tn), lambda m, n: (0, n))],
        out_specs=out_spec, args=[a, b], carry=carry)


def _wgrad_down(name, hm, df, tn, carry=None):
    J, T, Fs = hm.shape
    D = df.shape[1]

    def body(a_ref, b_ref, o_ref):
        o_ref[...] = _dot(a_ref[...], b_ref[...], TN).astype(BF16)

    return _call(
        body, name=name, grid=(J, D // tn), out_shape=jax.ShapeDtypeStruct((J, Fs, D), BF16),
        in_specs=[pl.BlockSpec((None, T, Fs), lambda j, n: (j, 0, 0)), pl.BlockSpec((T, tn), lambda j, n: (0, n))],
        out_specs=pl.BlockSpec((None, Fs, tn), lambda j, n: (j, 0, n)), args=[hm, df], carry=carry)


def _wgrad_up(name, n, da, db, tn, carry=None):
    T, D = n.shape
    J, _, Fs = da.shape

    def body(n_ref, da_ref, db_ref, og_ref, ou_ref):
        nv = n_ref[...]
        og_ref[...] = _dot(da_ref[...], nv, TN).astype(BF16)
        ou_ref[...] = _dot(db_ref[...], nv, TN).astype(BF16)

    act = pl.BlockSpec((None, T, Fs), lambda j, m: (j, 0, 0))
    out = pl.BlockSpec((None, Fs, tn), lambda j, m: (j, 0, m))
    shape = jax.ShapeDtypeStruct((J, Fs, D), BF16)
    return _call(
        body, name=name, grid=(J, D // tn), out_shape=(shape, shape),
        in_specs=[pl.BlockSpec((T, tn), lambda j, m: (0, m)), act, act], out_specs=(out, out),
        args=[n, da, db], carry=carry)


def _reduce_mm(name, pairs, once, dims, T, D, tm, steps, init=None, carry=None):
    n_pairs = len(pairs)
    n_once = len(once)
    n_mm = 2 * (n_pairs + n_once)

    def body(*refs):
        pr = refs[:2 * n_pairs]
        on = refs[2 * n_pairs:n_mm]
        o_ref = refs[-1]
        r = pl.program_id(1)

        @pl.when(r == 0)
        def _():
            o_ref[...] = jnp.zeros(o_ref.shape, F32) if init is None else refs[n_mm][...]

        for p in range(n_pairs):
            o_ref[...] += _dot(pr[2 * p][...], pr[2 * p + 1][...], dims)

        if n_once:
            @pl.when(r == steps - 1)
            def _():
                for p in range(n_once):
                    o_ref[...] += _dot(on[2 * p][...], on[2 * p + 1][...], dims)

    in_specs, args = [], []
    for a, a_spec, w, w_spec in list(pairs) + list(once):
        in_specs += [a_spec, w_spec]
        args += [a, w]
    row = pl.BlockSpec((tm, D), lambda i, r: (i, 0))
    if init is not None:
        in_specs.append(row)
        args.append(init)
    return _call(body, name=name, grid=(T // tm, steps), in_specs=in_specs, out_specs=row, out_shape=jax.ShapeDtypeStruct((T, D), F32),
                 args=args, carry=carry)


def _rmsnorm_bwd(name, dn, x, g, dh, tm, bf16_scale, carry=None):
    T, D = x.shape
    emit_bf16 = bf16_scale is not None

    def body(dn_ref, x_ref, g_ref, dh_ref, *outs):
        dxn, dg = _rms_bwd(dn_ref[...], x_ref[...], g_ref[...])
        dx = dh_ref[...] + dxn
        outs[0][...] = dx
        if emit_bf16:
            outs[2][...] = (bf16_scale * dx).astype(BF16)

        @pl.when(pl.program_id(0) == 0)
        def _():
            outs[1][...] = dg

        @pl.when(pl.program_id(0) > 0)
        def _():
            outs[1][...] += dg

    row = pl.BlockSpec((tm, D), lambda i: (i, 0))
    gain = pl.BlockSpec((1, D), lambda i: (0, 0))
    out_shape = [jax.ShapeDtypeStruct((T, D), F32), jax.ShapeDtypeStruct((1, D), F32)]
    out_specs = [row, gain]
    if emit_bf16:
        out_shape.append(jax.ShapeDtypeStruct((T, D), BF16))
        out_specs.append(row)
    return _call(body, name=name, grid=(T // tm,), in_specs=[row, row, gain, row], out_specs=tuple(out_specs),
                 out_shape=tuple(out_shape), args=[dn, x, g, dh], carry=carry)


def _loss_grad(name, y, target, tm):
    T, D = y.shape

    def body(y_ref, t_ref, dy_ref, dyh_ref, sq_ref):
        diff = y_ref[...] - t_ref[...]
        sq = jnp.sum(jnp.sum(diff * diff, axis=1, keepdims=True), axis=0, keepdims=True)
        dy = diff * (1.0 / D)
        dy_ref[...] = dy
        dyh_ref[...] = (0.5 * dy).astype(BF16)

        @pl.when(pl.program_id(0) == 0)
        def _():
            sq_ref[...] = sq

        @pl.when(pl.program_id(0) > 0)
        def _():
            sq_ref[...] += sq

    row = pl.BlockSpec((tm, D), lambda i: (i, 0))
    return pl.pallas_call(
        body, out_shape=(jax.ShapeDtypeStruct((T, D), F32), jax.ShapeDtypeStruct((T, D), BF16), jax.ShapeDtypeStruct((1, 1), F32)),
        grid=(T // tm,), in_specs=[row, row], out_specs=(row, row, pl.BlockSpec((1, 1), lambda i: (0, 0))),
        name=name, compiler_params=_params(1))(y, target)


def _ffn_up(name, n, wg, wu, tm, carry=None):
    T, D = n.shape
    J, Fs, _ = wg.shape

    def body(n_ref, wg_ref, wu_ref, a_ref, b_ref, h_ref):
        xv = n_ref[...]
        a = _dot(xv, wg_ref[...], NT)
        b = _dot(xv, wu_ref[...], NT)
        a_ref[...] = a.astype(BF16)
        b_ref[...] = b.astype(BF16)
        h_ref[...] = (a * jax.nn.sigmoid(a) * b).astype(BF16)

    act = jax.ShapeDtypeStruct((J, T, Fs), BF16)
    wspec = pl.BlockSpec((None, Fs, D), lambda j, i: (j, 0, 0))
    aspec = pl.BlockSpec((None, tm, Fs), lambda j, i: (j, i, 0))
    return _call(
        body, name=name, grid=(J, T // tm), out_shape=(act, act, act),
        in_specs=[pl.BlockSpec((tm, D), lambda j, i: (i, 0)), wspec, wspec], out_specs=(aspec, aspec, aspec),
        args=[n, wg, wu], carry=carry)


def _ffn_gate(name, n, wg, tm, carry=None):
    T, D = n.shape
    J, Fs, _ = wg.shape

    def body(n_ref, wg_ref, a_ref):
        a_ref[...] = _dot(n_ref[...], wg_ref[...], NT).astype(BF16)

    aspec = pl.BlockSpec((None, tm, Fs), lambda j, i: (j, i, 0))
    return _call(
        body, name=name, grid=(J, T // tm), out_shape=jax.ShapeDtypeStruct((J, T, Fs), BF16),
        in_specs=[pl.BlockSpec((tm, D), lambda j, i: (i, 0)), pl.BlockSpec((None, Fs, D), lambda j, i: (j, 0, 0))],
        out_specs=aspec, args=[n, wg], carry=carry)


def _ffn_up_only(name, n, wu, a, tm, carry=None):
    T, D = n.shape
    J, Fs, _ = wu.shape

    def body(n_ref, wu_ref, a_ref, b_ref, h_ref):
        b = _dot(n_ref[...], wu_ref[...], NT)
        a = a_ref[...].astype(F32)
        b_ref[...] = b.astype(BF16)
        h_ref[...] = (a * jax.nn.sigmoid(a) * b).astype(BF16)

    act = jax.ShapeDtypeStruct((J, T, Fs), BF16)
    aspec = pl.BlockSpec((None, tm, Fs), lambda j, i: (j, i, 0))
    return _call(
        body, name=name, grid=(J, T // tm), out_shape=(act, act),
        in_specs=[pl.BlockSpec((tm, D), lambda j, i: (i, 0)), pl.BlockSpec((None, Fs, D), lambda j, i: (j, 0, 0)), aspec],
        out_specs=(aspec, aspec), args=[n, wu, a], carry=carry)


def _ffn_down(name, hm, wd, resid, tm, tn, carry=None):
    J, T, Fs = hm.shape
    D = wd.shape[2]

    def body(h_ref, w_ref, r_ref, o_ref):
        @pl.when(pl.program_id(2) == 0)
        def _():
            o_ref[...] = r_ref[...]

        o_ref[...] += _dot(h_ref[...] * 0.5, w_ref[...], NN)

    tile = pl.BlockSpec((tm, tn), lambda i, n, j: (i, n))
    return _call(
        body, name=name, grid=(T // tm, D // tn, J), out_shape=jax.ShapeDtypeStruct((T, D), F32),
        in_specs=[pl.BlockSpec((None, tm, Fs), lambda i, n, j: (j, i, 0)), pl.BlockSpec((None, Fs, tn), lambda i, n, j: (j, 0, n)), tile],
        out_specs=tile, args=[hm, wd, resid], carry=carry)


def _ffn_bwd_mid(name, dfh, wd, a, b, tm, carry=None):
    T, D = dfh.shape
    J, Fs, _ = wd.shape

    def body(df_ref, w_ref, a_ref, b_ref, da_ref, db_ref):
        dhm = _dot(df_ref[...], w_ref[...], NT)
        av = a_ref[...].astype(F32)
        bv = b_ref[...].astype(F32)
        sg = jax.nn.sigmoid(av)
        da_ref[...] = (dhm * bv * (sg * (1.0 + av * (1.0 - sg)))).astype(BF16)
        db_ref[...] = (dhm * (av * sg)).astype(BF16)

    act = jax.ShapeDtypeStruct((J, T, Fs), BF16)
    aspec = pl.BlockSpec((None, tm, Fs), lambda j, i: (j, i, 0))
    return _call(
        body, name=name, grid=(J, T // tm), out_shape=(act, act),
        in_specs=[pl.BlockSpec((tm, D), lambda j, i: (i, 0)), pl.BlockSpec((None, Fs, D), lambda j, i: (j, 0, 0)), aspec, aspec],
        out_specs=(aspec, aspec), args=[dfh, wd, a, b], carry=carry)


def _rot_half(y, lane):
    first = (lane & (HEAD_DIM // 2)) == 0
    return jnp.where(first, pltpu.roll(y, LANES - HEAD_DIM // 2, 1), pltpu.roll(y, HEAD_DIM // 2, 1))


def _head_rstd(x, lo):
    sq = x * x
    ss_a = jnp.sum(jnp.where(lo, sq, 0.0), axis=-1, keepdims=True)
    ss_b = jnp.sum(jnp.where(lo, 0.0, sq), axis=-1, keepdims=True)
    return lax.rsqrt(jnp.where(lo, ss_a, ss_b) * (1.0 / HEAD_DIM) + EPS)


def _headnorm_fwd(name, proj, col_off, ncb, gains, tm, scale, rope=None, dup=False):
    T = proj.shape[0]
    with_rope = rope is not None
    width = 2 * LANES if dup else LANES

    def body(*refs):
        if with_rope:
            x_ref, g_ref, cos_ref, sin_ref, o_ref = refs
        else:
            x_ref, g_ref, o_ref = refs
        xv = x_ref[...]
        lane = lax.broadcasted_iota(jnp.int32, xv.shape, 1)
        lo = lane < HEAD_DIM
        y = xv * _head_rstd(xv, lo) * g_ref[...]
        if with_rope:
            y = y * cos_ref[...] + _rot_half(y, lane) * sin_ref[...]
        y = y * scale
        if dup:
            sw = pltpu.roll(y, HEAD_DIM, 1)
            o_ref[:, :LANES] = jnp.where(lo, y, sw).astype(BF16)
            o_ref[:, LANES:] = jnp.where(lo, sw, y).astype(BF16)
        else:
            o_ref[...] = y.astype(BF16)

    in_specs = [pl.BlockSpec((tm, LANES), lambda c, i: (i, col_off + c)), pl.BlockSpec((None, 1, LANES), lambda c, i: (c, 0, 0))]
    args = [proj, gains]
    if with_rope:
        tab = pl.BlockSpec((tm, LANES), lambda c, i: (i, 0))
        in_specs += [tab, tab]
        args += list(rope)
    return pl.pallas_call(
        body, out_shape=jax.ShapeDtypeStruct((T, ncb * width), BF16), grid=(ncb, T // tm),
        in_specs=in_specs, out_specs=pl.BlockSpec((tm, width), lambda c, i: (i, c)),
        name=name, compiler_params=_params(2))(*args)


def _headnorm_bwd(name, dy, proj, col_off, ncb, gains, group, tm, scale, rope=None, fold=False, norm=True, into=None):
    T = dy.shape[0]
    with_rope = rope is not None
    n_groups = ncb // group
    dy_width = 4 * LANES if fold else LANES

    def body(*refs):
        refs = list(refs)
        dy_ref = refs.pop(0)
        x_ref = refs.pop(0) if norm else None
        g_ref = refs.pop(0) if norm else None
        cos_ref = refs.pop(0) if with_rope else None
        sin_ref = refs.pop(0) if with_rope else None
        if into is not None:
            refs.pop(0)
        dx_ref = refs.pop(0)
        dg_ref = refs.pop(0) if norm else None
        c = pl.program_id(0)
        i = pl.program_id(1)
        d = dy_ref[...]
        lane = lax.broadcasted_iota(jnp.int32, (d.shape[0], LANES), 1)
        lo = lane < HEAD_DIM
        if fold:
            t0 = d[:, 0:LANES] + d[:, LANES:2 * LANES]
            t1 = d[:, 2 * LANES:3 * LANES] + d[:, 3 * LANES:4 * LANES]
            d = jnp.where(lo, t0 + pltpu.roll(t0, HEAD_DIM, 1), t1 + pltpu.roll(t1, HEAD_DIM, 1))
        d = d * scale
        if with_rope:
            d = d * cos_ref[...] + _rot_half(d * sin_ref[...], lane)
        if not norm:
            dx_ref[...] = d.astype(BF16)
            return
        xv = x_ref[...]
        gv = g_ref[...]
        r = _head_rstd(xv, lo)
        xh = xv * r
        dxh = d * gv
        pr = dxh * xh
        m_a = jnp.sum(jnp.where(lo, pr, 0.0), axis=-1, keepdims=True)
        m_b = jnp.sum(jnp.where(lo, 0.0, pr), axis=-1, keepdims=True)
        mean = jnp.where(lo, m_a, m_b) * (1.0 / HEAD_DIM)
        dx_ref[...] = (r * (dxh - xh * mean)).astype(BF16)
        dgp = jnp.sum(d * xh, axis=0, keepdims=True)
        dgp = dgp + pltpu.roll(dgp, HEAD_DIM, 1)
        first = jnp.logical_and(c % group == 0, i == 0)

        @pl.when(first)
        def _():
            dg_ref[...] = dgp

        @pl.when(jnp.logical_not(first))
        def _():
            dg_ref[...] += dgp

    in_specs = [pl.BlockSpec((tm, dy_width), lambda c, i: (i, c))]
    args = [dy]
    if norm:
        in_specs += [pl.BlockSpec((tm, LANES), lambda c, i: (i, col_off + c)), pl.BlockSpec((None, 1, LANES), lambda c, i: (c, 0, 0))]
        args += [proj, gains]
    if with_rope:
        tab = pl.BlockSpec((tm, LANES), lambda c, i: (i, 0))
        in_specs += [tab, tab]
        args += list(rope)
    out_shape = [jax.ShapeDtypeStruct((T, ncb * LANES), BF16)]
    out_off = 0
    aliases = {}
    if into is not None:
        buf, out_off = into
        assert buf.dtype == BF16 and buf.shape[0] == T and (out_off + ncb) * LANES <= buf.shape[1], buf.shape
        aliases = {len(args): 0}
        in_specs.append(_HBM)
        args.append(buf)
        out_shape = [jax.ShapeDtypeStruct(buf.shape, BF16)]
    out_specs = [pl.BlockSpec((tm, LANES), lambda c, i: (i, out_off + c))]
    if norm:
        out_shape.append(jax.ShapeDtypeStruct((n_groups, 1, LANES), F32))
        out_specs.append(pl.BlockSpec((None, 1, LANES), lambda c, i: (c // group, 0, 0)))
    res = pl.pallas_call(
        body, out_shape=tuple(out_shape), grid=(ncb, T // tm), in_specs=in_specs, out_specs=tuple(out_specs),
        input_output_aliases=aliases, name=name, compiler_params=_params(2))(*args)
    return res if norm else (res[0], None)


def _dot_exact(x, tri):
    hi = x.astype(BF16)
    r1 = x - hi.astype(F32)
    mid = r1.astype(BF16)
    lo = (r1 - mid.astype(F32)).astype(BF16)
    return _dot(hi, tri, NN) + _dot(mid, tri, NN) + _dot(lo, tri, NN)


def _forget_fwd(name, zt, bias):
    H, T = zt.shape
    blk = min(256, T)

    def body(z_ref, b_ref, c_ref, s_ref):
        z = z_ref[...] + b_ref[...]
        s_ref[...] = jax.nn.sigmoid(-z)
        lf = jnp.minimum(z, 0.0) - jnp.log(1.0 + jnp.exp(-jnp.abs(z)))
        tri = (lax.broadcasted_iota(jnp.int32, (blk, blk), 0) <= lax.broadcasted_iota(jnp.int32, (blk, blk), 1)).astype(BF16)
        carry = jnp.zeros((H, 1), F32)
        for bi in range(T // blk):
            xb = lf[:, bi * blk:(bi + 1) * blk]
            c_ref[:, bi * blk:(bi + 1) * blk] = _dot_exact(xb, tri) + carry
            carry = carry + jnp.sum(xb, axis=-1, keepdims=True)

    shape = jax.ShapeDtypeStruct((H, T), F32)
    full = pl.BlockSpec((H, T), lambda i: (0, 0))
    return pl.pallas_call(
        body, out_shape=(shape, shape), grid=(1,), in_specs=[full, pl.BlockSpec((H, 1), lambda i: (0, 0))],
        out_specs=(full, full), name=name, compiler_params=_params(1))(zt, bias)


def _forget_bwd(name, dct, drt, sgt):
    H, T = dct.shape
    blk = min(256, T)

    def body(dc_ref, dr_ref, s_ref, dz_ref, db_ref):
        dc = dc_ref[...] + dr_ref[...]
        tri = (lax.broadcasted_iota(jnp.int32, (blk, blk), 0) >= lax.broadcasted_iota(jnp.int32, (blk, blk), 1)).astype(BF16)
        carry = jnp.zeros((H, 1), F32)
        db = jnp.zeros((H, 1), F32)
        for bi in reversed(range(T // blk)):
            xb = dc[:, bi * blk:(bi + 1) * blk]
            dz = (_dot_exact(xb, tri) + carry) * s_ref[:, bi * blk:(bi + 1) * blk]
            dz_ref[:, bi * blk:(bi + 1) * blk] = dz
            db = db + jnp.sum(dz, axis=-1, keepdims=True)
            carry = carry + jnp.sum(xb, axis=-1, keepdims=True)
        db_ref[...] = db

    full = pl.BlockSpec((H, T), lambda i: (0, 0))
    return pl.pallas_call(
        body, out_shape=(jax.ShapeDtypeStruct((H, T), F32), jax.ShapeDtypeStruct((H, 1), F32)), grid=(1,),
        in_specs=[full, full, full], out_specs=(full, pl.BlockSpec((H, 1), lambda i: (0, 0))),
        name=name, compiler_params=_params(1))(dct, drt, sgt)


FOX_STRIP_FWD = 128
FOX_STRIP_BWD = 256


def _fox_fwd(name, qk, v, crow, tq, tk, strip, carry=None):
    T, Dh = v.shape
    HP = Dh // LANES
    nk = T // tk
    assert tk % tq == 0 and tq % strip == 0
    n_strips = tq // strip

    def body(q_ref, k_ref, v_ref, ra_ref, rb_ref, o_ref, la_ref, lb_ref, s_ref, p_ref, m_ref, l_ref, acc_ref):
        i = pl.program_id(1)
        q2 = q_ref[...]
        lo = _lane_lo((tq, LANES))
        q_st = jnp.concatenate([_keep(lo, q2), _keep(jnp.logical_not(lo), q2)], axis=0)
        r_refs = (ra_ref, rb_ref)
        m_ref[...] = jnp.full(m_ref.shape, NEG, F32)
        l_ref[...] = jnp.zeros(l_ref.shape, F32)
        acc_ref[...] = jnp.zeros(acc_ref.shape, F32)
        rel = lax.broadcasted_iota(jnp.int32, (strip, tk), 0) - lax.broadcasted_iota(jnp.int32, (strip, tk), 1)

        def chunk(kc, masked):
            start = pl.multiple_of(kc * tk, tk)
            kb = k_ref[pl.ds(start, tk), :]
            vb = v_ref[pl.ds(start, tk), :]
            s_ref[...] = _dot(q_st, kb, NT)
            for h in range(2):
                cs = r_refs[h][kc]
                for st in range(n_strips):
                    rows = pl.ds(h * tq + st * strip, strip)
                    s = s_ref[rows, :] - cs
                    if masked:
                        s = jnp.where(rel >= start - (i * tq + st * strip), s, NEG)
                    m_old = m_ref[rows, :]
                    mn = jnp.maximum(m_old, jnp.max(s, axis=-1, keepdims=True))
                    p = jnp.exp(s - mn)
                    alpha = jnp.exp(m_old - mn)
                    l_ref[rows, :] = alpha * l_ref[rows, :] + jnp.sum(p, axis=-1, keepdims=True)
                    m_ref[rows, :] = mn
                    p_ref[rows, :] = p.astype(BF16)
                    acc_ref[rows, :] = acc_ref[rows, :] * alpha
            acc_ref[...] += _dot(p_ref[...], vb, NN)

        n_full = (i * tq) // tk

        def full_chunk(kc, _):
            chunk(kc, False)
            return 0

        lax.fori_loop(0, n_full, full_chunk, 0)
        chunk(n_full, True)
        top, bot = pl.ds(0, tq), pl.ds(tq, tq)
        o_ref[...] = jnp.where(lo, acc_ref[top, :] / l_ref[top, :], acc_ref[bot, :] / l_ref[bot, :])
        la_ref[...] = m_ref[top, :] + jnp.log(l_ref[top, :])
        lb_ref[...] = m_ref[bot, :] + jnp.log(l_ref[bot, :])

    row = lambda off: pl.BlockSpec((None, nk, 1, tk), lambda h, i: (2 * h + off, 0, 0, 0))
    lse = jax.ShapeDtypeStruct((HP, T, 1), F32)
    lspec = pl.BlockSpec((None, tq, 1), lambda h, i: (h, i, 0))
    scratch = [pltpu.VMEM((2 * tq, tk), F32), pltpu.VMEM((2 * tq, tk), BF16), pltpu.VMEM((2 * tq, 1), F32),
               pltpu.VMEM((2 * tq, 1), F32), pltpu.VMEM((2 * tq, LANES), F32)]
    return _call(
        body, name=name, grid=(HP, T // tq), out_shape=(jax.ShapeDtypeStruct((T, Dh), F32), lse, lse),
        in_specs=[pl.BlockSpec((tq, LANES), lambda h, i: (i, h)), pl.BlockSpec((T, LANES), lambda h, i: (0, HP + h)),
                  pl.BlockSpec((T, LANES), lambda h, i: (0, h)), row(0), row(1)],
        out_specs=(pl.BlockSpec((tq, LANES), lambda h, i: (i, h)), lspec, lspec),
        args=[qk, qk, v, crow, crow], scratch_shapes=scratch, carry=carry)


def _fox_bwd(name, qk, v, o, do, crow, lse_a, lse_b, tq, tk, strip, carry=None):
    T, Dh = v.shape
    HP = Dh // LANES
    nk = T // tk
    scale = HEAD_DIM ** -0.5
    assert tk % tq == 0 and tq % strip == 0
    n_strips = tq // strip

    def body(q_ref, k_ref, v_ref, o_ref, do_ref, ra_ref, rb_ref, la_ref, lb_ref,
             dq_ref, dk_ref, dv_ref, dca_ref, dcb_ref, dra_ref, drb_ref, s_ref, dp_ref, p_ref, ds_ref, dq_acc, dsum_ref):
        i = pl.program_id(1)

        @pl.when(i == 0)
        def _():
            dk_ref[...] = jnp.zeros_like(dk_ref)
            dv_ref[...] = jnp.zeros_like(dv_ref)
            dca_ref[...] = jnp.zeros_like(dca_ref)
            dcb_ref[...] = jnp.zeros_like(dcb_ref)

        q2 = q_ref[...]
        do2 = do_ref[...]
        lo = _lane_lo((tq, LANES))
        hi = jnp.logical_not(lo)
        q_st = jnp.concatenate([_keep(lo, q2), _keep(hi, q2)], axis=0)
        do_st = jnp.concatenate([_keep(lo, do2), _keep(hi, do2)], axis=0)
        prod = do2.astype(F32) * o_ref[...]
        dsum_ref[pl.ds(0, tq), :] = jnp.sum(jnp.where(lo, prod, 0.0), axis=-1, keepdims=True)
        dsum_ref[pl.ds(tq, tq), :] = jnp.sum(jnp.where(lo, 0.0, prod), axis=-1, keepdims=True)
        r_refs, l_refs, dc_refs, dr_refs = (ra_ref, rb_ref), (la_ref, lb_ref), (dca_ref, dcb_ref), (dra_ref, drb_ref)
        dq_acc[...] = jnp.zeros(dq_acc.shape, F32)
        dra_ref[...] = jnp.zeros(dra_ref.shape, F32)
        drb_ref[...] = jnp.zeros(drb_ref.shape, F32)
        rel = lax.broadcasted_iota(jnp.int32, (strip, tk), 0) - lax.broadcasted_iota(jnp.int32, (strip, tk), 1)

        def chunk(kc, masked):
            start = pl.multiple_of(kc * tk, tk)
            kb = k_ref[pl.ds(start, tk), :]
            vb = v_ref[pl.ds(start, tk), :]
            s_ref[...] = _dot(q_st, kb, NT)
            dp_ref[...] = _dot(do_st, vb, NT)
            for h in range(2):
                cs = r_refs[h][kc]
                col_sum = jnp.zeros((1, tk), F32)
                for st in range(n_strips):
                    rows = pl.ds(st * strip, strip)
                    both = pl.ds(h * tq + st * strip, strip)
                    s = s_ref[both, :] - cs
                    if masked:
                        s = jnp.where(rel >= start - (i * tq + st * strip), s, NEG)
                    p = jnp.exp(s - l_refs[h][rows, :])
                    ds = p * (dp_ref[both, :] - dsum_ref[both, :])
                    p_ref[both, :] = p.astype(BF16)
                    ds_ref[both, :] = ds.astype(BF16)
                    col_sum = col_sum + jnp.sum(ds, axis=0, keepdims=True)
                    dr_refs[h][rows, :] += jnp.sum(ds, axis=-1, keepdims=True)
                dc_refs[h][kc] = dc_refs[h][kc] - col_sum
            dk_ref[pl.ds(start, tk), :] += _dot(ds_ref[...], q_st, TN)
            dv_ref[pl.ds(start, tk), :] += _dot(p_ref[...], do_st, TN)
            dq_acc[...] += _dot(ds_ref[...], kb, NN)

        n_full = (i * tq) // tk

        def full_chunk(kc, _):
            chunk(kc, False)
            return 0

        lax.fori_loop(0, n_full, full_chunk, 0)
        chunk(n_full, True)
        dq_ref[...] = jnp.where(lo, dq_acc[pl.ds(0, tq), :], dq_acc[pl.ds(tq, tq), :]) * scale

    row = lambda off: pl.BlockSpec((None, nk, 1, tk), lambda h, i: (2 * h + off, 0, 0, 0))
    lspec = pl.BlockSpec((None, tq, 1), lambda h, i: (h, i, 0))
    qspec = pl.BlockSpec((tq, LANES), lambda h, i: (i, h))
    full = pl.BlockSpec((T, LANES), lambda h, i: (0, h))
    dcspec = pl.BlockSpec((None, nk, 1, tk), lambda h, i: (h, 0, 0, 0))
    grad = jax.ShapeDtypeStruct((T, Dh), F32)
    dc = jax.ShapeDtypeStruct((HP, nk, 1, tk), F32)
    dr = jax.ShapeDtypeStruct((HP, T, 1), F32)
    scratch = [pltpu.VMEM((2 * tq, tk), F32), pltpu.VMEM((2 * tq, tk), F32), pltpu.VMEM((2 * tq, tk), BF16), pltpu.VMEM((2 * tq, tk), BF16),
               pltpu.VMEM((2 * tq, LANES), F32), pltpu.VMEM((2 * tq, 1), F32)]
    return _call(
        body, name=name, grid=(HP, T // tq), out_shape=(grad, grad, grad, dc, dc, dr, dr),
        in_specs=[qspec, pl.BlockSpec((T, LANES), lambda h, i: (0, HP + h)), full, qspec, qspec, row(0), row(1), lspec, lspec],
        out_specs=(qspec, full, full, dcspec, dcspec, lspec, lspec),
        args=[qk, qk, v, o, do, crow, crow, lse_a, lse_b], scratch_shapes=scratch, carry=carry)


SWA_GROUP = 2
SWA_GROUP_BWD = 4


def _swa_block(n, q_ref, k_ref):
    qs = pl.multiple_of(n * WINDOW, WINDOW)
    ks = pl.multiple_of(jnp.maximum(n - 1, 0) * WINDOW, WINDOW)
    rel = (qs + lax.broadcasted_iota(jnp.int32, (WINDOW, 2 * WINDOW), 0)) - (ks + lax.broadcasted_iota(jnp.int32, (WINDOW, 2 * WINDOW), 1))
    valid = jnp.logical_and(rel >= 0, rel < WINDOW)
    return qs, ks, valid


def _swa_fwd(name, q, kd, vd, sinks, carry=None):
    T, Dh = q.shape
    HP = Dh // LANES

    def body(q_ref, k_ref, v_ref, sa_ref, sb_ref, o_ref, la_ref, lb_ref):
        lo = _lane_lo((WINDOW, LANES))

        top = lax.broadcasted_iota(jnp.int32, (2 * WINDOW, 1), 0) < WINDOW
        sink = jnp.where(top, sa_ref[...], sb_ref[...])

        def block(n, _):
            qs, ks, valid = _swa_block(n, q_ref, k_ref)
            q2 = q_ref[pl.ds(qs, WINDOW), :]
            kb = k_ref[pl.ds(ks, 2 * WINDOW), :]
            vb = v_ref[pl.ds(ks, 2 * WINDOW), :]
            q_st = jnp.concatenate([_keep(lo, q2), _keep(jnp.logical_not(lo), q2)], axis=0)
            s = jnp.where(jnp.concatenate([valid, valid], axis=0), _dot(q_st, kb, NT), NEG)
            m = jnp.maximum(jnp.max(s, axis=-1, keepdims=True), sink)
            p = jnp.exp(s - m)
            l = jnp.sum(p, axis=-1, keepdims=True) + jnp.exp(sink - m)
            o2 = _dot(p.astype(BF16), vb, NN) / l
            lse = m + jnp.log(l)
            o_ref[pl.ds(qs, WINDOW), :] = jnp.where(lo, o2[:WINDOW], o2[WINDOW:])
            la_ref[pl.ds(qs, WINDOW), :] = lse[:WINDOW]
            lb_ref[pl.ds(qs, WINDOW), :] = lse[WINDOW:]
            return 0

        assert (T // WINDOW) % SWA_GROUP == 0

        def group(g, c):
            for b in range(SWA_GROUP):
                c = block(g * SWA_GROUP + b, c)
            return c

        lax.fori_loop(0, T // WINDOW // SWA_GROUP, group, 0)

    full = pl.BlockSpec((T, LANES), lambda h: (0, h))
    kv = pl.BlockSpec((T, LANES), lambda h: (0, h // 2))
    sink = lambda off: pl.BlockSpec((None, 1, 1), lambda h: (2 * h + off, 0, 0))
    lse = jax.ShapeDtypeStruct((HP, T, 1), F32)
    lspec = pl.BlockSpec((None, T, 1), lambda h: (h, 0, 0))
    return _call(
        body, name=name, grid=(HP,), out_shape=(jax.ShapeDtypeStruct((T, Dh), F32), lse, lse),
        in_specs=[full, kv, kv, sink(0), sink(1)], out_specs=(full, lspec, lspec),
        args=[q, kd, vd, sinks, sinks], carry=carry)


def _swa_bwd(name, q, kd, vd, sinks, o, do, lse_a, lse_b, carry=None):
    T, Dh = q.shape
    HP = Dh // LANES
    scale = HEAD_DIM ** -0.5

    def body(q_ref, k_ref, v_ref, sa_ref, sb_ref, o_ref, do_ref, la_ref, lb_ref, dq_ref, dk_ref, dv_ref, dsa_ref, dsb_ref):
        lo = _lane_lo((WINDOW, LANES))
        hi = jnp.logical_not(lo)
        dk_ref[...] = jnp.zeros_like(dk_ref)
        dv_ref[...] = jnp.zeros_like(dv_ref)

        top = lax.broadcasted_iota(jnp.int32, (2 * WINDOW, 1), 0) < WINDOW
        sink = jnp.where(top, sa_ref[...], sb_ref[...])

        def block(n, dsinks):
            qs, ks, valid = _swa_block(n, q_ref, k_ref)
            rows = pl.ds(qs, WINDOW)
            q2 = q_ref[rows, :]
            do2 = do_ref[rows, :]
            kb = k_ref[pl.ds(ks, 2 * WINDOW), :]
            vb = v_ref[pl.ds(ks, 2 * WINDOW), :]
            prod = do2.astype(F32) * o_ref[rows, :]
            q_st = jnp.concatenate([_keep(lo, q2), _keep(hi, q2)], axis=0)
            do_st = jnp.concatenate([_keep(lo, do2), _keep(hi, do2)], axis=0)
            dsum = jnp.concatenate([jnp.sum(jnp.where(lo, prod, 0.0), axis=-1, keepdims=True),
                                    jnp.sum(jnp.where(lo, 0.0, prod), axis=-1, keepdims=True)], axis=0)
            lse = jnp.concatenate([la_ref[rows, :], lb_ref[rows, :]], axis=0)
            s = jnp.where(jnp.concatenate([valid, valid], axis=0), _dot(q_st, kb, NT), NEG)
            p = jnp.exp(s - lse)
            ds = p * (_dot(do_st, vb, NT) - dsum)
            dsb = ds.astype(BF16)
            dq2 = _dot(dsb, kb, NN)
            dq_ref[rows, :] = jnp.where(lo, dq2[:WINDOW], dq2[WINDOW:]) * scale
            dk_ref[pl.ds(ks, 2 * WINDOW), :] += _dot(dsb, q_st, TN)
            dv_ref[pl.ds(ks, 2 * WINDOW), :] += _dot(p.astype(BF16), do_st, TN)
            gone = jnp.exp(sink - lse) * dsum
            return (dsinks[0] - jnp.sum(gone[:WINDOW], axis=0, keepdims=True),
                    dsinks[1] - jnp.sum(gone[WINDOW:], axis=0, keepdims=True))

        assert (T // WINDOW) % SWA_GROUP_BWD == 0

        def group(g, c):
            for b in range(SWA_GROUP_BWD):
                c = block(g * SWA_GROUP_BWD + b, c)
            return c

        dsa, dsb_ = lax.fori_loop(0, T // WINDOW // SWA_GROUP_BWD, group, (jnp.zeros((1, 1), F32), jnp.zeros((1, 1), F32)))
        dsa_ref[...] = dsa
        dsb_ref[...] = dsb_

    full = pl.BlockSpec((T, LANES), lambda h: (0, h))
    kv = pl.BlockSpec((T, LANES), lambda h: (0, h // 2))
    sink = lambda off: pl.BlockSpec((None, 1, 1), lambda h: (2 * h + off, 0, 0))
    lspec = pl.BlockSpec((None, T, 1), lambda h: (h, 0, 0))
    dsink = pl.BlockSpec((None, 1, 1), lambda h: (h, 0, 0))
    grad = jax.ShapeDtypeStruct((T, Dh), F32)
    ds_shape = jax.ShapeDtypeStruct((HP, 1, 1), F32)
    return _call(
        body, name=name, grid=(HP,), out_shape=(grad, grad, grad, ds_shape, ds_shape),
        in_specs=[full, kv, kv, sink(0), sink(1), full, full, lspec, lspec],
        out_specs=(full, full, full, dsink, dsink),
        args=[q, kd, vd, sinks, sinks, o, do, lse_a, lse_b], carry=carry)


def _place():
    return lax.axis_index("x"), lax.axis_index("y"), lax.axis_index("c")


def _run_carry(name, carry):
    c_in, c_out = len(carry.inputs), len(carry.out_shapes)

    def body(*refs):
        ins, outs, scr = refs[:c_in], refs[c_in:c_in + c_out], refs[c_in + c_out:]
        carry.start(ins, outs, scr)
        carry.middle(ins, outs, scr)
        carry.finish(ins, outs, scr)

    return pl.pallas_call(
        body, out_shape=tuple(carry.out_shapes), in_specs=[_HBM] * c_in, out_specs=tuple([_HBM] * c_out),
        scratch_shapes=carry.scratch, name=name)(*carry.inputs)


def _gather_carry(shards):
    n = len(shards)

    def plan(ins, outs, scr):
        send, recv, local = scr
        x, y, c = _place()
        me, sibling = (x, y, c), (x, y, 1 - c)
        partner, other, diag = (x ^ c, y ^ (1 - c)), (x ^ (1 - c), y ^ c), (1 - x, 1 - y)

        def copy(w, k, block, to, src=None):
            slot = 4 * block[0] + 2 * block[1] + block[2]
            return pltpu.make_async_remote_copy(
                src_ref=outs[w].at[slot] if src is None else src, dst_ref=outs[w].at[slot],
                send_sem=send.at[w, k], recv_sem=recv.at[w, k], device_id=to, device_id_type=MESH)

        def own():
            return [pltpu.make_async_copy(ins[w], outs[w].at[4 * x + 2 * y + c], local.at[w]) for w in range(n)]

        return copy, own, me, sibling, partner, other, diag, c

    def start(ins, outs, scr):
        copy, own, me, sibling, partner, other, _, c = plan(ins, outs, scr)
        for cp in own():
            cp.start()
        for w in range(n):
            copy(w, 1, me, (*partner, c), src=ins[w]).start()
            copy(w, 2, me, (*other, c), src=ins[w]).start()
            copy(w, 0, me, sibling, src=ins[w]).start()

    def middle(ins, outs, scr):
        copy, _, me, sibling, partner, other, _, c = plan(ins, outs, scr)
        for w in range(n):
            copy(w, 1, (*partner, c), me).wait_recv()
            copy(w, 3, (*partner, c), (*other, c)).start()
            copy(w, 4, (*partner, c), sibling).start()
        for w in range(n):
            copy(w, 2, (*other, c), me).wait_recv()
            copy(w, 5, (*other, c), sibling).start()

    def finish(ins, outs, scr):
        copy, own, me, sibling, partner, other, diag, c = plan(ins, outs, scr)
        for w in range(n):
            copy(w, 3, (*diag, c), me).wait_recv()
            copy(w, 6, (*diag, c), sibling).start()
        for w in range(n):
            copy(w, 0, sibling, me).wait_recv()
            copy(w, 4, (*other, 1 - c), me).wait_recv()
            copy(w, 5, (*partner, 1 - c), me).wait_recv()
            copy(w, 6, (*diag, 1 - c), me).wait_recv()
        for w in range(n):
            sent = [copy(w, 0, me, sibling, src=ins[w]), copy(w, 1, me, (*partner, c), src=ins[w]), copy(w, 2, me, (*other, c), src=ins[w]),
                    copy(w, 3, (*partner, c), (*other, c)), copy(w, 4, (*partner, c), sibling), copy(w, 5, (*other, c), sibling),
                    copy(w, 6, (*diag, c), sibling)]
            for cp in sent:
                cp.wait_send()
        for cp in own():
            cp.wait()

    return _Carry(shards, [jax.ShapeDtypeStruct((N_DEV,) + s.shape, s.dtype) for s in shards],
                  [pltpu.SemaphoreType.DMA((n, 7)), pltpu.SemaphoreType.DMA((n, 7)), pltpu.SemaphoreType.DMA((n,))], start, finish, middle)


def _sibling_carry(grads):
    n = len(grads)

    def copies(ins, outs, scr):
        send, recv = scr
        x, y, c = _place()
        return [pltpu.make_async_remote_copy(
            src_ref=ins[w].at[2 * q + (1 - c)], dst_ref=outs[w].at[q], send_sem=send.at[w, q], recv_sem=recv.at[w, q],
            device_id=(x, y, 1 - c), device_id_type=MESH) for w in range(n) for q in range(4)]

    def start(ins, outs, scr):
        for cp in copies(ins, outs, scr):
            cp.start()

    def finish(ins, outs, scr):
        for cp in copies(ins, outs, scr):
            cp.wait()

    return _Carry(grads, [jax.ShapeDtypeStruct((4,) + g.shape[1:], g.dtype) for g in grads],
                  [pltpu.SemaphoreType.DMA((n, 4)), pltpu.SemaphoreType.DMA((n, 4))], start, finish)


def _to_partner_carry(sums):
    n = len(sums)

    def copies(ins, outs, scr):
        send, recv = scr
        x, y, c = _place()
        return [pltpu.make_async_remote_copy(
            src_ref=ins[w].at[k], dst_ref=outs[2 * w + k], send_sem=send.at[w, k], recv_sem=recv.at[w, k],
            device_id=(x ^ c, y ^ (1 - c), c), device_id_type=MESH) for w in range(n) for k in range(2)]

    def start(ins, outs, scr):
        for cp in copies(ins, outs, scr):
            cp.start()

    def finish(ins, outs, scr):
        for cp in copies(ins, outs, scr):
            cp.wait()

    return _Carry(sums, [jax.ShapeDtypeStruct(s.shape[1:], s.dtype) for s in sums for _ in range(2)],
                  [pltpu.SemaphoreType.DMA((n, 2)), pltpu.SemaphoreType.DMA((n, 2))], start, finish)


def _to_other_carry(blocks):
    n = len(blocks)

    def copies(ins, outs, scr):
        send, recv = scr
        x, y, c = _place()
        return [pltpu.make_async_remote_copy(
            src_ref=ins[w], dst_ref=outs[w], send_sem=send.at[w], recv_sem=recv.at[w],
            device_id=(x ^ (1 - c), y ^ c, c), device_id_type=MESH) for w in range(n)]

    def start(ins, outs, scr):
        for cp in copies(ins, outs, scr):
            cp.start()

    def finish(ins, outs, scr):
        for cp in copies(ins, outs, scr):
            cp.wait()

    return _Carry(blocks, [jax.ShapeDtypeStruct(b.shape, b.dtype) for b in blocks],
                  [pltpu.SemaphoreType.DMA((n,)), pltpu.SemaphoreType.DMA((n,))], start, finish)


def _gather_small(packed):
    R, C = packed.shape

    def body(in_ref, out_ref, send, recv):
        x, y, c = _place()
        mine = 4 * x + 2 * y + c
        out_ref[mine] = in_ref[...]
        copies = []
        for k in range(1, N_DEV):
            peer = (x ^ (k >> 2), y ^ ((k >> 1) & 1), c ^ (k & 1))
            copies.append(pltpu.make_async_remote_copy(
                src_ref=in_ref, dst_ref=out_ref.at[mine], send_sem=send.at[k - 1], recv_sem=recv.at[k - 1],
                device_id=peer, device_id_type=MESH))
        for cp in copies:
            cp.start()
        for cp in copies:
            cp.wait()

    vmem = pl.BlockSpec(memory_space=pltpu.VMEM)
    return pl.pallas_call(
        body, out_shape=jax.ShapeDtypeStruct((N_DEV, R, C), F32), in_specs=[vmem], out_specs=vmem,
        scratch_shapes=[pltpu.SemaphoreType.DMA((N_DEV - 1,)), pltpu.SemaphoreType.DMA((N_DEV - 1,))],
        name="small_grads_all_gather")(packed)


def _adamw(w, g, m, v):
    m = ADAM_B1 * m + (1.0 - ADAM_B1) * g
    v = ADAM_B2 * v + (1.0 - ADAM_B2) * (g * g)
    m_hat = m / (1.0 - ADAM_B1 ** ADAM_STEP)
    v_hat = v / (1.0 - ADAM_B2 ** ADAM_STEP)
    delta = -ADAM_LR * (m_hat / (jnp.sqrt(v_hat) + ADAM_EPS) + ADAM_WD * w)
    return delta, m, v


def _pair_add(name, grads, received, slots):
    _, R, C = grads.shape
    tr = _row_tile(R, 2 * ROW_TILE_CAP)

    def body(s_ref, g_ref, r_ref, o_ref):
        o_ref[...] = (g_ref[...].astype(F32) + r_ref[...].astype(F32)).astype(BF16)

    return pl.pallas_call(
        body, out_shape=jax.ShapeDtypeStruct((3, R, C), BF16),
        grid_spec=pltpu.PrefetchScalarGridSpec(
            num_scalar_prefetch=1, grid=(3, R // tr),
            in_specs=[pl.BlockSpec((None, tr, C), lambda k, i, s: (s[k], i, 0)), pl.BlockSpec((None, tr, C), lambda k, i, s: (s[3 + k], i, 0))],
            out_specs=pl.BlockSpec((None, tr, C), lambda k, i, s: (k, i, 0))),
        name=name, compiler_params=_params(2))(slots, grads, received)


def _relay_add(name, sums, relayed):
    _, R, C = sums.shape
    tr = _row_tile(R, 2 * ROW_TILE_CAP)

    def body(s_ref, r_ref, o_ref):
        o_ref[...] = (s_ref[...].astype(F32) + r_ref[...].astype(F32)).astype(BF16)

    blk = pl.BlockSpec((tr, C), lambda i: (i, 0))
    return pl.pallas_call(
        body, out_shape=jax.ShapeDtypeStruct((R, C), BF16), grid=(R // tr,),
        in_specs=[pl.BlockSpec((None, tr, C), lambda i: (2, i, 0)), blk], out_specs=blk,
        name=name, compiler_params=_params(1))(sums, relayed)


def _adam_shard(name, grads, from_sibling, received, w, m, v, own):
    R, C = w.shape
    tr = _row_tile(R, 256)
    tc = C if tr < R or C % (4 * LANES) else 4 * LANES

    def body(o_ref, g_ref, s_ref, ra_ref, rb_ref, w_ref, m_ref, v_ref, g_out, d_out, m_out, v_out):
        g = (g_ref[...].astype(F32) + s_ref[...].astype(F32)) + ra_ref[...].astype(F32) + rb_ref[...].astype(F32)
        delta, mn, vn = _adamw(w_ref[...], g, m_ref[...], v_ref[...])
        g_out[...] = g
        d_out[...] = delta
        m_out[...] = mn
        v_out[...] = vn

    blk = pl.BlockSpec((tr, tc), lambda i, j, o: (i, j))
    shape = jax.ShapeDtypeStruct((R, C), F32)
    return pl.pallas_call(
        body, out_shape=(shape,) * 4,
        grid_spec=pltpu.PrefetchScalarGridSpec(
            num_scalar_prefetch=1, grid=(R // tr, C // tc),
            in_specs=[pl.BlockSpec((None, tr, tc), lambda i, j, o: (o[0], i, j)), pl.BlockSpec((None, tr, tc), lambda i, j, o: (o[1], i, j)),
                      blk, blk, blk, blk, blk],
            out_specs=(blk,) * 4),
        name=name, compiler_params=_params(2))(own, grads, from_sibling, received[0], received[1], w, m, v)


def _adam_small(name, gathered, w, m, v):
    R, C = w.shape

    def body(ga_ref, w_ref, m_ref, v_ref, g_out, d_out, m_out, v_out):
        g = ga_ref[0]
        for d in range(1, N_DEV):
            g = g + ga_ref[d]
        delta, mn, vn = _adamw(w_ref[...], g, m_ref[...], v_ref[...])
        g_out[...] = g
        d_out[...] = delta
        m_out[...] = mn
        v_out[...] = vn

    full = pl.BlockSpec((R, C), lambda i: (0, 0))
    shape = jax.ShapeDtypeStruct((R, C), F32)
    return pl.pallas_call(
        body, out_shape=(shape,) * 4, grid=(1,),
        in_specs=[pl.BlockSpec((N_DEV, R, C), lambda i: (0, 0, 0)), full, full, full], out_specs=(full,) * 4,
        name=name, compiler_params=_params(1))(gathered, w, m, v)


def _pack_small(parts, D, scalar=None):
    g1, gmix, g2, gof, gos, bf, gqf, gkf, gqs, gks, sinks = [p.reshape(-1).astype(F32) for p in parts]
    row3 = jnp.concatenate([gof, gos])
    row4 = jnp.zeros((D,), F32)
    for slot, vec in enumerate((bf, gqf, gkf, gqs, gks, sinks)):
        row4 = lax.dynamic_update_slice(row4, vec, (slot * LANES,))
    zero = jnp.zeros((D,), F32)
    row5 = zero if scalar is None else lax.dynamic_update_slice(zero, jnp.reshape(scalar, (1,)).astype(F32), (0,))
    return jnp.stack([g1, gmix, g2, row3, row4, row5, zero, zero])


def _unpack_small(packed, D, H):
    Dh = D // 2
    row4 = packed[4]
    short = [row4[s * LANES:s * LANES + n] for s, n in enumerate((H, HEAD_DIM, HEAD_DIM, HEAD_DIM, HEAD_DIM, H))]
    vecs = [packed[0], packed[1], packed[2], packed[3, :Dh], packed[3, Dh:]] + short
    return [v[None, :] for v in vecs]


def kernel(x, positions, norm_ffn1_g, ffn1_w_gate, ffn1_w_up, ffn1_w_down, norm_mix_g, w_in, b_forget, fox_q_norm_g, fox_k_norm_g, swa_q_norm_g, swa_k_norm_g, swa_sinks, out_norm_fox_g, out_norm_swa_g, w_out, norm_ffn2_g, ffn2_w_gate, ffn2_w_up, ffn2_w_down, loss_target, m_norm_ffn1_g, m_ffn1_w_gate, m_ffn1_w_up, m_ffn1_w_down, m_norm_mix_g, m_w_in, m_b_forget, m_fox_q_norm_g, m_fox_k_norm_g, m_swa_q_norm_g, m_swa_k_norm_g, m_swa_sinks, m_out_norm_fox_g, m_out_norm_swa_g, m_w_out, m_norm_ffn2_g, m_ffn2_w_gate, m_ffn2_w_up, m_ffn2_w_down, v_norm_ffn1_g, v_ffn1_w_gate, v_ffn1_w_up, v_ffn1_w_down, v_norm_mix_g, v_w_in, v_b_forget, v_fox_q_norm_g, v_fox_k_norm_g, v_swa_q_norm_g, v_swa_k_norm_g, v_swa_sinks, v_out_norm_fox_g, v_out_norm_swa_g, v_w_out, v_norm_ffn2_g, v_ffn2_w_gate, v_ffn2_w_up, v_ffn2_w_down):
    xs = x[0]
    target = loss_target[0]
    T, D = xs.shape
    Dh = D // 2
    H = Dh // HEAD_DIM
    HP = H // 2
    KVW = (H // GQA_GROUP) * HEAD_DIM
    KVB = KVW // LANES
    MAIN = 4 * Dh + 2 * KVW
    F_OFF = 3 * Dh
    tm = min(ROW_TILE_CAP, T)
    tm2 = min(2 * ROW_TILE_CAP, T)
    tq = min(512, T)
    tk = min(512, T)
    nk = T // tk
    cx, cy, cc = _place()
    near = [2 * (cx ^ cc) + (cy ^ (1 - cc)), 2 * (1 - cx) + (1 - cy), 2 * (cx ^ (1 - cc)) + (cy ^ cc)]
    slots = jnp.stack([2 * q + cc for q in near] + near).astype(jnp.int32)
    own = jnp.stack([4 * cx + 2 * cy + cc, 2 * cx + cy]).astype(jnp.int32)

    tr = jnp.transpose
    big_w = [tr(ffn1_w_gate[0]), tr(ffn1_w_up[0]), ffn1_w_down[0], tr(w_in[0]), w_out[0], tr(ffn2_w_gate[0]), tr(ffn2_w_up[0]),
             ffn2_w_down[0]]
    big_m = [tr(m_ffn1_w_gate[0]), tr(m_ffn1_w_up[0]), m_ffn1_w_down[0], tr(m_w_in[0]), m_w_out[0], tr(m_ffn2_w_gate[0]),
             tr(m_ffn2_w_up[0]), m_ffn2_w_down[0]]
    big_v = [tr(v_ffn1_w_gate[0]), tr(v_ffn1_w_up[0]), v_ffn1_w_down[0], tr(v_w_in[0]), v_w_out[0], tr(v_ffn2_w_gate[0]),
             tr(v_ffn2_w_up[0]), v_ffn2_w_down[0]]
    transposed = {"ffn1_w_gate", "ffn1_w_up", "w_in", "ffn2_w_gate", "ffn2_w_up"}
    names = ["ffn1_w_gate", "ffn1_w_up", "ffn1_w_down", "w_in", "w_out", "ffn2_w_gate", "ffn2_w_up", "ffn2_w_down"]
    sh = dict(zip(names, [w.astype(BF16) for w in big_w]))
    lane = jnp.arange(LANES)
    inv_freq = ROPE_THETA ** (-(2.0 * (lane % (HEAD_DIM // 2))).astype(F32) / HEAD_DIM)
    ang = positions[0].astype(F32)[:, None] * inv_freq[None, :]
    cos_t = jnp.cos(ang)
    sin_t = jnp.where((lane & (HEAD_DIM // 2)) == 0, -1.0, 1.0)[None, :] * jnp.sin(ang)
    rope = (cos_t, sin_t)

    def pair_gain(g, blocks):
        return jnp.tile(jnp.concatenate([g[0], g[0]])[None, None, :], (blocks, 1, 1))

    n1, (wg1,) = _rmsnorm_fwd("ffn1_norm", xs, norm_ffn1_g, tm, carry=_gather_carry([sh["ffn1_w_gate"]]))
    a1, (wu1,) = _ffn_gate("ffn1_gate", n1, wg1, tm, carry=_gather_carry([sh["ffn1_w_up"]]))
    (b1, hm1), (wd1,) = _ffn_up_only("ffn1_up", n1, wu1, a1, tm, carry=_gather_carry([sh["ffn1_w_down"]]))
    h1, (win_g,) = _ffn_down("ffn1_down", hm1, wd1, xs, tm, D, carry=_gather_carry([sh["w_in"]]))
    n_in = win_g.shape[1]
    win_t = win_g.reshape(N_DEV * n_in, D)
    win_f = jnp.pad(win_t[F_OFF:F_OFF + H], ((0, LANES - H), (0, 0)))
    tkb = MAIN // 9
    assert F_OFF % tkb == 0

    def main_row(r):
        return pl.multiple_of(r * tkb + H * (r >= F_OFF // tkb).astype(jnp.int32), min(H, tkb))

    u = _rmsnorm_fwd("mix_norm", h1, norm_mix_g, tm)
    (proj, proj_b), (wout_g,) = _mm("mix_proj", u, win_t, T, tkb, dims=NT, carry=_gather_carry([sh["w_out"]]),
                                    b_rows=(9, main_row), bf16_copy=True)
    wout = wout_g.reshape(D, D)
    proj_f = _mm("mix_proj_forget", u, win_f, T, LANES, dims=NT)
    scale = HEAD_DIM ** -0.5
    fox_gains = jnp.concatenate([pair_gain(fox_q_norm_g, HP), pair_gain(fox_k_norm_g, HP)])
    qk_f = _headnorm_fwd_scaled("fox_qk_norm", proj, 0, 2 * HP, fox_gains, T, scale, HP)
    v_f = proj_b[:, 2 * Dh:3 * Dh]
    c_t, sg_t = _forget_fwd("forget_gates", proj_f[:, :H].T, b_forget.reshape(H, 1))
    crow = c_t.reshape(H, nk, 1, tk)
    (o_fox, lse_fa, lse_fb), (wg2, wu2) = _fox_fwd("fox_attention", qk_f, v_f, crow, tq, tk, min(FOX_STRIP_FWD, tq),
                                                   carry=_gather_carry([sh["ffn2_w_gate"], sh["ffn2_w_up"]]))

    swa_q_gains = pair_gain(swa_q_norm_g, HP)
    swa_k_gains = pair_gain(swa_k_norm_g, KVB)
    q_s = _headnorm_fwd("swa_q_norm", proj, 3 * HP, HP, swa_q_gains, T, scale, rope=rope)
    k_d = _headnorm_fwd("swa_k_norm", proj, 4 * HP, KVB, swa_k_gains, T, 1.0, rope=rope, dup=True)
    v_s = proj_b[:, 4 * Dh + KVW:].reshape(T, H // GQA_GROUP, 1, HEAD_DIM)
    v_d = jnp.broadcast_to(v_s, (T, H // GQA_GROUP, 2, HEAD_DIM)).reshape(T, 2 * KVW)
    sinks3 = swa_sinks.reshape(H, 1, 1)
    o_swa, lse_sa, lse_sb = _swa_fwd("swa_attention", q_s, k_d, v_d, sinks3)

    on = _outnorm_fwd("out_norm", o_fox, o_swa, out_norm_fox_g, out_norm_swa_g, tm)
    h2 = _mm("mix_out", on, wout, tm2, min(512, D), resid=h1)

    n2 = _rmsnorm_fwd("ffn2_norm", h2, norm_ffn2_g, tm)
    (a2, b2, hm2), (wd2,) = _ffn_up("ffn2_up", n2, wg2, wu2, tm2, carry=_gather_carry([sh["ffn2_w_down"]]))
    y = _ffn_down("ffn2_down", hm2, wd2, h2, tm2, D)
    dy, dyh, sq = _loss_grad("loss_grad", y, target, tm)
    loss_part = 0.5 * sq[0, 0] / D

    J, Fs, _ = wg2.shape
    aspec = pl.BlockSpec((None, tm, Fs), lambda i, j: (j, i, 0))
    wspec = pl.BlockSpec((None, Fs, D), lambda i, j: (j, 0, 0))
    got = {}
    local = {}

    def pair_sums(keys, grads, received):
        for nm, g, r in zip(keys, grads, received):
            local[nm] = (g, r)
        return [_pair_add("sum_" + nm, g, r, slots) for nm, g, r in zip(keys, grads, received)]

    def relay_sums(keys, sums, hop1):
        out = []
        for i, (nm, s) in enumerate(zip(keys, sums)):
            got[nm] = [hop1[2 * i]]
            out.append(_relay_add("relay_" + nm, s, hop1[2 * i + 1]))
        return out

    def arrived(keys, hop2):
        for nm, blk in zip(keys, hop2):
            got[nm].append(blk)

    dwd2 = _wgrad_down("ffn2_wgrad_down", hm2, dyh, D)
    (da2, db2), (sib_d2,) = _ffn_bwd_mid("ffn2_bwd_mid", dyh, wd2, a2, b2, tm2, carry=_sibling_carry([dwd2]))
    (sum_wd2,) = pair_sums(names[7:8], [dwd2], [sib_d2])
    (dwg2, dwu2), hop1 = _wgrad_up("ffn2_wgrad_up", n2, da2, db2, min(1024, D), carry=_to_partner_carry([sum_wd2]))
    (t_wd2,) = relay_sums(names[7:8], [sum_wd2], hop1)
    aspec2 = pl.BlockSpec((None, tm2, Fs), lambda i, j: (j, i, 0))
    dn2, (via_wd2, *sib2) = _reduce_mm("ffn2_bwd_in", [(da2, aspec2, wg2, wspec), (db2, aspec2, wu2, wspec)], [], NN, T, D, tm2, J,
                                       carry=_join(_to_other_carry([t_wd2]), _sibling_carry([dwg2, dwu2])))
    arrived(names[7:8], [via_wd2])
    dh2, dg_ffn2, dh2b = _rmsnorm_bwd("ffn2_norm_bwd", dn2, h2, norm_ffn2_g, dy, min(256, T), 1.0)
    sum_wg2, sum_wu2 = pair_sums(names[5:7], [dwg2, dwu2], sib2)

    dwout = _wgrad_2d("mix_out_wgrad", on, dh2b, min(512, D), D)
    dwout_g = dwout.reshape(N_DEV, D // N_DEV, D)
    do_fox, dg_of = _outnorm_bwd("out_norm_bwd_fox", dh2b, wout, 0, o_fox, out_norm_fox_g, tm)
    do_swa, dg_os = _outnorm_bwd("out_norm_bwd_swa", dh2b, wout, 1, o_swa, out_norm_swa_g, tm)

    (dq_f, dk_f, dv_f, dc_a, dc_b, dr_a, dr_b), (*hop1, sib_wout) = _fox_bwd(
        "fox_attention_bwd", qk_f, v_f, o_fox, do_fox, crow, lse_fa, lse_fb, tq, tk, min(FOX_STRIP_BWD, tq),
        carry=_join(_to_partner_carry([sum_wg2, sum_wu2]), _sibling_carry([dwout_g])))
    t_wg2, t_wu2 = relay_sums(names[5:7], [sum_wg2, sum_wu2], hop1)
    (sum_wout,) = pair_sums(names[4:5], [dwout_g], [sib_wout])
    dproj = lax.empty((T, MAIN), BF16)
    dproj, dg_fq = _headnorm_bwd("fox_q_norm_bwd", dq_f, proj, 0, HP, fox_gains[:HP], HP, T, 1.0, into=(dproj, 0))
    dproj, dg_fk = _headnorm_bwd("fox_k_norm_bwd", dk_f, proj, HP, HP, fox_gains[HP:], HP, T, 1.0, into=(dproj, HP))
    dproj, _ = _headnorm_bwd("fox_v_cast", dv_f, None, 0, HP, None, HP, T, 1.0, norm=False, into=(dproj, 2 * HP))
    dct = jnp.stack([dc_a.reshape(HP, T), dc_b.reshape(HP, T)], axis=1).reshape(H, T)
    drt = jnp.stack([dr_a.reshape(HP, T), dr_b.reshape(HP, T)], axis=1).reshape(H, T)
    dz_t, db_f = _forget_bwd("forget_gates_bwd", dct, drt, sg_t)

    (dq_s, dk_p, dv_p, dsink_a, dsink_b), hop2 = _swa_bwd(
        "swa_attention_bwd", q_s, k_d, v_d, sinks3, o_swa, do_swa, lse_sa, lse_sb, carry=_to_other_carry([t_wg2, t_wu2]))
    arrived(names[5:7], hop2)
    dproj, dg_sq = _headnorm_bwd("swa_q_norm_bwd", dq_s, proj, 3 * HP, HP, swa_q_gains, HP, T, 1.0, rope=rope, into=(dproj, 3 * HP))
    dproj, dg_sk = _headnorm_bwd("swa_k_norm_bwd", dk_p, proj, 4 * HP, KVB, swa_k_gains, KVB, T, 1.0, rope=rope, fold=True,
                                 into=(dproj, 4 * HP))
    dproj, _ = _headnorm_bwd("swa_v_fold", dv_p, None, 0, KVB, None, KVB, T, 1.0, fold=True, norm=False, into=(dproj, 4 * HP + KVB))
    dproj_f = jnp.pad(dz_t.T, ((0, 0), (0, LANES - H))).astype(BF16)
    dwin_t, hop1 = _wgrad_2d("mix_proj_wgrad", dproj, u, tkb, D, carry=_to_partner_carry([sum_wout]),
                             out_rows=(N_DEV * n_in, main_row))
    (t_wout,) = relay_sums(names[4:5], [sum_wout], hop1)
    dwin_f = _wgrad_2d("mix_proj_forget_wgrad", dproj_f, u, LANES, min(1024, D))
    dwin_t = lax.dynamic_update_slice(dwin_t, dwin_f[:H], (F_OFF, 0))
    dwin_g = dwin_t.reshape(N_DEV, n_in, D)
    du, (via_wout, sib_win) = _reduce_mm(
        "mix_bwd_in",
        [(dproj, pl.BlockSpec((tm2, tkb), lambda i, r: (i, r)), win_t, pl.BlockSpec((pl.Element(tkb), pl.Element(D)), lambda i, r: (main_row(r), 0)))],
        [(dproj_f, pl.BlockSpec((tm2, LANES), lambda i, r: (i, 0)), win_f, pl.BlockSpec((LANES, D), lambda i, r: (0, 0)))],
        NN, T, D, tm2, 9, carry=_join(_to_other_carry([t_wout]), _sibling_carry([dwin_g])))
    arrived(names[4:5], [via_wout])
    dh1, dg_mix, dh1h = _rmsnorm_bwd("mix_norm_bwd", du, h1, norm_mix_g, dh2, min(256, T), 0.5)
    (sum_win,) = pair_sums(names[3:4], [dwin_g], [sib_win])

    dwd1, hop1 = _wgrad_down("ffn1_wgrad_down", hm1, dh1h, D, carry=_to_partner_carry([sum_win]))
    (t_win,) = relay_sums(names[3:4], [sum_win], hop1)
    (da1, db1), (via_win, sib_d) = _ffn_bwd_mid("ffn1_bwd_mid", dh1h, wd1, a1, b1, tm2,
                                                carry=_join(_to_other_carry([t_win]), _sibling_carry([dwd1])))
    arrived(names[3:4], [via_win])
    (sum_wd1,) = pair_sums(names[2:3], [dwd1], [sib_d])
    dwg1, hop1 = _wgrad_down("ffn1_wgrad_gate", da1, n1, D, carry=_to_partner_carry([sum_wd1]))
    (t_wd1,) = relay_sums(names[2:3], [sum_wd1], hop1)
    dwu1, (via_wd1, sib_g) = _wgrad_down("ffn1_wgrad_up", db1, n1, D,
                                         carry=_join(_to_other_carry([t_wd1]), _sibling_carry([dwg1])))
    arrived(names[2:3], [via_wd1])
    (sum_wg1,) = pair_sums(names[0:1], [dwg1], [sib_g])
    dn1_gate, (*hop1, sib_u) = _reduce_mm(
        "ffn1_bwd_in_gate", [(da1, aspec2, wg1, wspec)], [], NN, T, D, tm2, J,
        carry=_join(_to_partner_carry([sum_wg1]), _sibling_carry([dwu1])))
    (t_wg1,) = relay_sums(names[0:1], [sum_wg1], hop1)
    (sum_wu1,) = pair_sums(names[1:2], [dwu1], [sib_u])
    dn1, (via_wg1, *hop1) = _reduce_mm(
        "ffn1_bwd_in_up", [(db1, aspec2, wu1, wspec)], [], NN, T, D, tm2, J, init=dn1_gate,
        carry=_join(_to_other_carry([t_wg1]), _to_partner_carry([sum_wu1])))
    arrived(names[0:1], [via_wg1])
    (t_wu1,) = relay_sums(names[1:2], [sum_wu1], hop1)
    arrived(names[1:2], _run_carry("grads_exchange", _to_other_carry([t_wu1])))
    dx, dg_ffn1 = _rmsnorm_bwd("ffn1_norm_bwd", dn1, xs, norm_ffn1_g, dh1, min(256, T), None)

    big_out = [_adam_shard("adam_" + nm, local[nm][0], local[nm][1], got[nm], w, m, v, own)
               for nm, w, m, v in zip(names, big_w, big_m, big_v)]

    dsinks = jnp.stack([dsink_a.reshape(HP), dsink_b.reshape(HP)], axis=1).reshape(H)
    small_g = [dg_ffn1, dg_mix, dg_ffn2, dg_of, dg_os, db_f, dg_fq[0, 0, :HEAD_DIM], dg_fk[0, 0, :HEAD_DIM],
               dg_sq[0, 0, :HEAD_DIM], dg_sk[0, 0, :HEAD_DIM], dsinks]
    small_w = [norm_ffn1_g, norm_mix_g, norm_ffn2_g, out_norm_fox_g, out_norm_swa_g, b_forget, fox_q_norm_g, fox_k_norm_g,
               swa_q_norm_g, swa_k_norm_g, swa_sinks]
    small_m = [m_norm_ffn1_g, m_norm_mix_g, m_norm_ffn2_g, m_out_norm_fox_g, m_out_norm_swa_g, m_b_forget, m_fox_q_norm_g,
               m_fox_k_norm_g, m_swa_q_norm_g, m_swa_k_norm_g, m_swa_sinks]
    small_v = [v_norm_ffn1_g, v_norm_mix_g, v_norm_ffn2_g, v_out_norm_fox_g, v_out_norm_swa_g, v_b_forget, v_fox_q_norm_g,
               v_fox_k_norm_g, v_swa_q_norm_g, v_swa_k_norm_g, v_swa_sinks]
    gathered = _gather_small(_pack_small(small_g, D, loss_part))
    small_out = _adam_small("adam_small", gathered, _pack_small(small_w, D), _pack_small(small_m, D), _pack_small(small_v, D))
    loss = small_out[0][5, 0]
    small_out = [_unpack_small(p, D, H) for p in small_out]

    order = ["norm_ffn1_g", "ffn1_w_gate", "ffn1_w_up", "ffn1_w_down", "norm_mix_g", "w_in", "b_forget", "fox_q_norm_g", "fox_k_norm_g",
             "swa_q_norm_g", "swa_k_norm_g", "swa_sinks", "out_norm_fox_g", "out_norm_swa_g", "w_out", "norm_ffn2_g",
             "ffn2_w_gate", "ffn2_w_up", "ffn2_w_down"]
    small_names = ["norm_ffn1_g", "norm_mix_g", "norm_ffn2_g", "out_norm_fox_g", "out_norm_swa_g", "b_forget", "fox_q_norm_g",
                   "fox_k_norm_g", "swa_q_norm_g", "swa_k_norm_g", "swa_sinks"]
    result = [loss, dx[None]]
    for kind in range(4):
        for nm in order:
            if nm in names:
                leaf = big_out[names.index(nm)][kind]
                result.append((tr(leaf) if nm in transposed else leaf)[None])
            else:
                result.append(small_out[kind][small_names.index(nm)])
    return tuple(result)


def _headnorm_fwd_scaled(name, proj, col_off, ncb, gains, tm, scale, n_scaled):
    T = proj.shape[0]

    def body(x_ref, g_ref, o_ref):
        xv = x_ref[...]
        lo = _lane_lo(xv.shape)
        y = xv * _head_rstd(xv, lo) * g_ref[...]
        y = y * jnp.where(pl.program_id(0) < n_scaled, scale, 1.0)
        o_ref[...] = y.astype(BF16)

    return pl.pallas_call(
        body, out_shape=jax.ShapeDtypeStruct((T, ncb * LANES), BF16), grid=(ncb, T // tm),
        in_specs=[pl.BlockSpec((tm, LANES), lambda c, i: (i, col_off + c)), pl.BlockSpec((None, 1, LANES), lambda c, i: (c, 0, 0))],
        out_specs=pl.BlockSpec((tm, LANES), lambda c, i: (i, c)), name=name, compiler_params=_params(2))(proj, gains)
```

```python
import jax
import jax.numpy as jnp
from jax import lax
from jax.experimental import pallas as pl
from jax.experimental.pallas import tpu as pltpu

F32 = jnp.float32
BF16 = jnp.bfloat16

HEAD_DIM = 64
LANES = 128
WINDOW = 128
GQA_GROUP = 4
EPS = 1e-6
ROPE_THETA = 10000.0
ADAM_LR = 0.001
ADAM_B1 = 0.9
ADAM_B2 = 0.999
ADAM_EPS = 1e-08
ADAM_WD = 0.01
ADAM_STEP = 10
N_DEV = 8
NEG = -1e30
VMEM_LIMIT_V7X = 48 * 1024 * 1024
ROW_TILE_CAP = 512
MESH = pl.DeviceIdType.MESH

NN = (((1,), (0,)), ((), ()))
NT = (((1,), (1,)), ((), ()))
TN = (((0,), (0,)), ((), ()))


def _dot(a, b, dims):
    return lax.dot_general(a, b, dims, preferred_element_type=F32)


def _params(n_axes):
    return pltpu.CompilerParams(dimension_semantics=("arbitrary",) * n_axes, vmem_limit_bytes=VMEM_LIMIT_V7X)


def _row_tile(rows, cap=ROW_TILE_CAP):
    best = None
    for t in range(16, min(rows, cap) + 1, 16):
        if rows % t == 0:
            best = t
    return best or rows


def _lane_lo(shape):
    return lax.broadcasted_iota(jnp.int32, shape, len(shape) - 1) < HEAD_DIM


def _keep(sel, x):
    return jnp.where(sel, x.astype(F32), 0.0).astype(BF16)


_HBM = pl.BlockSpec(memory_space=pltpu.HBM)


class _Carry:
    def __init__(self, inputs, out_shapes, scratch, start, finish, middle=None):
        self.inputs, self.out_shapes, self.scratch = list(inputs), list(out_shapes), list(scratch)
        self.start, self.finish, self.middle = start, finish, middle or (lambda ins, outs, scr: None)


def _join(*carries):
    def hook(which):
        def run(ins, outs, scr):
            i = o = s = 0
            for c in carries:
                ni, no, ns = len(c.inputs), len(c.out_shapes), len(c.scratch)
                getattr(c, which)(ins[i:i + ni], outs[o:o + no], scr[s:s + ns])
                i, o, s = i + ni, o + no, s + ns
        return run

    return _Carry([a for c in carries for a in c.inputs], [a for c in carries for a in c.out_shapes],
                  [a for c in carries for a in c.scratch], hook("start"), hook("finish"), hook("middle"))


def _call(body, *, name, grid, in_specs, out_specs, out_shape, args, scratch_shapes=(), carry=None):
    params = _params(len(grid))
    if carry is None:
        return pl.pallas_call(body, out_shape=out_shape, grid=grid, in_specs=list(in_specs), out_specs=out_specs,
                              scratch_shapes=list(scratch_shapes), name=name, compiler_params=params)(*args)
    single = not isinstance(out_shape, (tuple, list))
    shapes = (out_shape,) if single else tuple(out_shape)
    specs = (out_specs,) if single else tuple(out_specs)
    n_in, n_out, n_scr = len(args), len(shapes), len(scratch_shapes)
    c_in, c_out = len(carry.inputs), len(carry.out_shapes)

    def wrapped(*refs):
        ins, c_ins = refs[:n_in], refs[n_in:n_in + c_in]
        o0 = n_in + c_in
        outs, c_outs = refs[o0:o0 + n_out], refs[o0 + n_out:o0 + n_out + c_out]
        s0 = o0 + n_out + c_out
        scr, c_scr = refs[s0:s0 + n_scr], refs[s0 + n_scr:]
        step, total = pl.program_id(0), grid[0]
        for ax in range(1, len(grid)):
            step, total = step * grid[ax] + pl.program_id(ax), total * grid[ax]

        @pl.when(step == 0)
        def _():
            carry.start(c_ins, c_outs, c_scr)

        @pl.when(step == total // 2)
        def _():
            carry.middle(c_ins, c_outs, c_scr)

        body(*ins, *outs, *scr)

        @pl.when(step == total - 1)
        def _():
            carry.finish(c_ins, c_outs, c_scr)

    res = pl.pallas_call(
        wrapped, out_shape=shapes + tuple(carry.out_shapes), grid=grid, in_specs=list(in_specs) + [_HBM] * c_in,
        out_specs=specs + (_HBM,) * c_out, scratch_shapes=list(scratch_shapes) + carry.scratch, name=name,
        compiler_params=params)(*args, *carry.inputs)
    main = res[:n_out]
    return (main[0] if single else tuple(main)), tuple(res[n_out:])


def _rms_bwd(dn, x, g):
    r = lax.rsqrt(jnp.mean(x * x, axis=-1, keepdims=True) + EPS)
    xh = x * r
    dxh = dn * g
    dx = r * (dxh - xh * jnp.mean(dxh * xh, axis=-1, keepdims=True))
    return dx, jnp.sum(dn * xh, axis=0, keepdims=True)


def _rmsnorm_fwd(name, x, g, tm, carry=None):
    T, D = x.shape

    def body(x_ref, g_ref, o_ref):
        xf = x_ref[...]
        r = lax.rsqrt(jnp.mean(xf * xf, axis=-1, keepdims=True) + EPS)
        o_ref[...] = (xf * r * g_ref[...]).astype(BF16)

    return _call(
        body, name=name, grid=(T // tm,), out_shape=jax.ShapeDtypeStruct((T, D), BF16),
        in_specs=[pl.BlockSpec((tm, D), lambda i: (i, 0)), pl.BlockSpec((1, D), lambda i: (0, 0))],
        out_specs=pl.BlockSpec((tm, D), lambda i: (i, 0)), args=[x, g], carry=carry)


def _outnorm_fwd(name, o_fox, o_swa, g_fox, g_swa, tm):
    T, Dh = o_fox.shape

    def body(a_ref, b_ref, ga_ref, gb_ref, o_ref):
        for ref, g_ref, lo in ((a_ref, ga_ref, 0), (b_ref, gb_ref, Dh)):
            xf = ref[...]
            r = lax.rsqrt(jnp.mean(xf * xf, axis=-1, keepdims=True) + EPS)
            o_ref[:, lo:lo + Dh] = (xf * r * g_ref[...]).astype(BF16)

    row = pl.BlockSpec((tm, Dh), lambda i: (i, 0))
    gain = pl.BlockSpec((1, Dh), lambda i: (0, 0))
    return pl.pallas_call(
        body, out_shape=jax.ShapeDtypeStruct((T, 2 * Dh), BF16), grid=(T // tm,),
        in_specs=[row, row, gain, gain], out_specs=pl.BlockSpec((tm, 2 * Dh), lambda i: (i, 0)),
        name=name, compiler_params=_params(1))(o_fox, o_swa, g_fox, g_swa)


def _outnorm_bwd(name, dhb, wout, half, o, g, tm):
    T, D = dhb.shape
    Dh = o.shape[1]

    def body(a_ref, w_ref, o_ref, g_ref, do_ref, dg_ref):
        don = _dot(a_ref[...], w_ref[...], NT)
        dx, dg = _rms_bwd(don, o_ref[...], g_ref[...])
        do_ref[...] = dx.astype(BF16)

        @pl.when(pl.program_id(0) == 0)
        def _():
            dg_ref[...] = dg

        @pl.when(pl.program_id(0) > 0)
        def _():
            dg_ref[...] += dg

    return pl.pallas_call(
        body, out_shape=(jax.ShapeDtypeStruct((T, Dh), BF16), jax.ShapeDtypeStruct((1, Dh), F32)), grid=(T // tm,),
        in_specs=[pl.BlockSpec((tm, D), lambda i: (i, 0)), pl.BlockSpec((Dh, D), lambda i: (half, 0)),
                  pl.BlockSpec((tm, Dh), lambda i: (i, 0)), pl.BlockSpec((1, Dh), lambda i: (0, 0))],
        out_specs=(pl.BlockSpec((tm, Dh), lambda i: (i, 0)), pl.BlockSpec((1, Dh), lambda i: (0, 0))),
        name=name, compiler_params=_params(1))(dhb, wout, o, g)


def _mm(name, a, b, tm, tn, dims=NN, resid=None, carry=None, b_rows=None, bf16_copy=False):
    M, K = a.shape
    transposed = dims == NT
    N = b.shape[0] if transposed else b.shape[1]
    if b_rows is not None:
        N = b_rows[0] * tn

    def body(*refs):
        if bf16_copy:
            a_ref, b_ref, o_ref, ob_ref = refs
            res = _dot(a_ref[...], b_ref[...], dims)
            o_ref[...] = res
            ob_ref[...] = res.astype(BF16)
        elif resid is None:
            a_ref, b_ref, o_ref = refs
            o_ref[...] = _dot(a_ref[...], b_ref[...], dims)
        else:
            a_ref, b_ref, r_ref, o_ref = refs
            o_ref[...] = r_ref[...] + _dot(a_ref[...], b_ref[...], dims)

    ospec = pl.BlockSpec((tm, tn), lambda n, i: (i, n))
    bspec = pl.BlockSpec((tn, K), lambda n, i: (n, 0)) if transposed else pl.BlockSpec((K, tn), lambda n, i: (0, n))
    if b_rows is not None:
        bspec = pl.BlockSpec((pl.Element(tn), pl.Element(K)), lambda n, i: (b_rows[1](n), 0))
    in_specs = [pl.BlockSpec((tm, K), lambda n, i: (i, 0)), bspec]
    args = [a, b]
    if resid is not None:
        in_specs.append(ospec)
        args.append(resid)
    if bf16_copy:
        assert resid is None
        return _call(body, name=name, grid=(N // tn, M // tm), in_specs=in_specs, out_specs=(ospec, ospec),
                     out_shape=(jax.ShapeDtypeStruct((M, N), F32), jax.ShapeDtypeStruct((M, N), BF16)), args=args, carry=carry)
    return _call(body, name=name, grid=(N // tn, M // tm), in_specs=in_specs, out_specs=ospec,
                 out_shape=jax.ShapeDtypeStruct((M, N), F32), args=args, carry=carry)


def _wgrad_2d(name, a, b, tmm, tn, carry=None, out_rows=None):
    T, M = a.shape
    N = b.shape[1]

    def body(a_ref, b_ref, o_ref):
        o_ref[...] = _dot(a_ref[...], b_ref[...], TN).astype(BF16)

    out_spec = pl.BlockSpec((tmm, tn), lambda m, n: (m, n))
    if out_rows is not None:
        out_spec = pl.BlockSpec((pl.Element(tmm), pl.Element(tn)), lambda m, n: (out_rows[1](m), n * tn))
    return _call(
        body, name=name, grid=(M // tmm, N // tn), out_shape=jax.ShapeDtypeStruct((M if out_rows is None else out_rows[0], N), BF16),
        in_specs=[pl.BlockSpec((T, tmm), lambda m, n: (0, m)), pl.BlockSpec((T, tn), lambda m, n: (0, n))],
        out_specs=out_spec, args=[a, b], carry=carry)


def _wgrad_down(name, hm, df, tn, carry=None):
    J, T, Fs = hm.shape
    D = df.shape[1]

    def body(a_ref, b_ref, o_ref):
        o_ref[...] = _dot(a_ref[...], b_ref[...], TN).astype(BF16)

    return _call(
        body, name=name, grid=(J, D // tn), out_shape=jax.ShapeDtypeStruct((J, Fs, D), BF16),
        in_specs=[pl.BlockSpec((None, T, Fs), lambda j, n: (j, 0, 0)), pl.BlockSpec((T, tn), lambda j, n: (0, n))],
        out_specs=pl.BlockSpec((None, Fs, tn), lambda j, n: (j, 0, n)), args=[hm, df], carry=carry)


def _wgrad_up(name, n, da, db, tn, carry=None):
    T, D = n.shape
    J, _, Fs = da.shape

    def body(n_ref, da_ref, db_ref, og_ref, ou_ref):
        nv = n_ref[...]
        og_ref[...] = _dot(da_ref[...], nv, TN).astype(BF16)
        ou_ref[...] = _dot(db_ref[...], nv, TN).astype(BF16)

    act = pl.BlockSpec((None, T, Fs), lambda j, m: (j, 0, 0))
    out = pl.BlockSpec((None, Fs, tn), lambda j, m: (j, 0, m))
    shape = jax.ShapeDtypeStruct((J, Fs, D), BF16)
    return _call(
        body, name=name, grid=(J, D // tn), out_shape=(shape, shape),
        in_specs=[pl.BlockSpec((T, tn), lambda j, m: (0, m)), act, act], out_specs=(out, out),
        args=[n, da, db], carry=carry)


def _reduce_mm(name, pairs, once, dims, T, D, tm, steps, init=None, carry=None):
    n_pairs = len(pairs)
    n_once = len(once)
    n_mm = 2 * (n_pairs + n_once)

    def body(*refs):
        pr = refs[:2 * n_pairs]
        on = refs[2 * n_pairs:n_mm]
        o_ref = refs[-1]
        r = pl.program_id(1)

        @pl.when(r == 0)
        def _():
            o_ref[...] = jnp.zeros(o_ref.shape, F32) if init is None else refs[n_mm][...]

        for p in range(n_pairs):
            o_ref[...] += _dot(pr[2 * p][...], pr[2 * p + 1][...], dims)

        if n_once:
            @pl.when(r == steps - 1)
            def _():
                for p in range(n_once):
                    o_ref[...] += _dot(on[2 * p][...], on[2 * p + 1][...], dims)

    in_specs, args = [], []
    for a, a_spec, w, w_spec in list(pairs) + list(once):
        in_specs += [a_spec, w_spec]
        args += [a, w]
    row = pl.BlockSpec((tm, D), lambda i, r: (i, 0))
    if init is not None:
        in_specs.append(row)
        args.append(init)
    return _call(body, name=name, grid=(T // tm, steps), in_specs=in_specs, out_specs=row, out_shape=jax.ShapeDtypeStruct((T, D), F32),
                 args=args, carry=carry)


def _rmsnorm_bwd(name, dn, x, g, dh, tm, bf16_scale, carry=None):
    T, D = x.shape
    emit_bf16 = bf16_scale is not None

    def body(dn_ref, x_ref, g_ref, dh_ref, *outs):
        dxn, dg = _rms_bwd(dn_ref[...], x_ref[...], g_ref[...])
        dx = dh_ref[...] + dxn
        outs[0][...] = dx
        if emit_bf16:
            outs[2][...] = (bf16_scale * dx).astype(BF16)

        @pl.when(pl.program_id(0) == 0)
        def _():
            outs[1][...] = dg

        @pl.when(pl.program_id(0) > 0)
        def _():
            outs[1][...] += dg

    row = pl.BlockSpec((tm, D), lambda i: (i, 0))
    gain = pl.BlockSpec((1, D), lambda i: (0, 0))
    out_shape = [jax.ShapeDtypeStruct((T, D), F32), jax.ShapeDtypeStruct((1, D), F32)]
    out_specs = [row, gain]
    if emit_bf16:
        out_shape.append(jax.ShapeDtypeStruct((T, D), BF16))
        out_specs.append(row)
    return _call(body, name=name, grid=(T // tm,), in_specs=[row, row, gain, row], out_specs=tuple(out_specs),
                 out_shape=tuple(out_shape), args=[dn, x, g, dh], carry=carry)


def _loss_grad(name, y, target, tm):
    T, D = y.shape

    def body(y_ref, t_ref, dy_ref, dyh_ref, sq_ref):
        diff = y_ref[...] - t_ref[...]
        sq = jnp.sum(jnp.sum(diff * diff, axis=1, keepdims=True), axis=0, keepdims=True)
        dy = diff * (1.0 / D)
        dy_ref[...] = dy
        dyh_ref[...] = (0.5 * dy).astype(BF16)

        @pl.when(pl.program_id(0) == 0)
        def _():
            sq_ref[...] = sq

        @pl.when(pl.program_id(0) > 0)
        def _():
            sq_ref[...] += sq

    row = pl.BlockSpec((tm, D), lambda i: (i, 0))
    return pl.pallas_call(
        body, out_shape=(jax.ShapeDtypeStruct((T, D), F32), jax.ShapeDtypeStruct((T, D), BF16), jax.ShapeDtypeStruct((1, 1), F32)),
        grid=(T // tm,), in_specs=[row, row], out_specs=(row, row, pl.BlockSpec((1, 1), lambda i: (0, 0))),
        name=name, compiler_params=_params(1))(y, target)


def _ffn_up(name, n, wg, wu, tm, carry=None):
    T, D = n.shape
    J, Fs, _ = wg.shape

    def body(n_ref, wg_ref, wu_ref, a_ref, b_ref, h_ref):
        xv = n_ref[...]
        a = _dot(xv, wg_ref[...], NT)
        b = _dot(xv, wu_ref[...], NT)
        a_ref[...] = a.astype(BF16)
        b_ref[...] = b.astype(BF16)
        h_ref[...] = (a * jax.nn.sigmoid(a) * b).astype(BF16)

    act = jax.ShapeDtypeStruct((J, T, Fs), BF16)
    wspec = pl.BlockSpec((None, Fs, D), lambda j, i: (j, 0, 0))
    aspec = pl.BlockSpec((None, tm, Fs), lambda j, i: (j, i, 0))
    return _call(
        body, name=name, grid=(J, T // tm), out_shape=(act, act, act),
        in_specs=[pl.BlockSpec((tm, D), lambda j, i: (i, 0)), wspec, wspec], out_specs=(aspec, aspec, aspec),
        args=[n, wg, wu], carry=carry)


def _ffn_gate(name, n, wg, tm, carry=None):
    T, D = n.shape
    J, Fs, _ = wg.shape

    def body(n_ref, wg_ref, a_ref):
        a_ref[...] = _dot(n_ref[...], wg_ref[...], NT).astype(BF16)

    aspec = pl.BlockSpec((None, tm, Fs), lambda j, i: (j, i, 0))
    return _call(
        body, name=name, grid=(J, T // tm), out_shape=jax.ShapeDtypeStruct((J, T, Fs), BF16),
        in_specs=[pl.BlockSpec((tm, D), lambda j, i: (i, 0)), pl.BlockSpec((None, Fs, D), lambda j, i: (j, 0, 0))],
        out_specs=aspec, args=[n, wg], carry=carry)


def _ffn_up_only(name, n, wu, a, tm, carry=None):
    T, D = n.shape
    J, Fs, _ = wu.shape

    def body(n_ref, wu_ref, a_ref, b_ref, h_ref):
        b = _dot(n_ref[...], wu_ref[...], NT)
        a = a_ref[...].astype(F32)
        b_ref[...] = b.astype(BF16)
        h_ref[...] = (a * jax.nn.sigmoid(a) * b).astype(BF16)

    act = jax.ShapeDtypeStruct((J, T, Fs), BF16)
    aspec = pl.BlockSpec((None, tm, Fs), lambda j, i: (j, i, 0))
    return _call(
        body, name=name, grid=(J, T // tm), out_shape=(act, act),
        in_specs=[pl.BlockSpec((tm, D), lambda j, i: (i, 0)), pl.BlockSpec((None, Fs, D), lambda j, i: (j, 0, 0)), aspec],
        out_specs=(aspec, aspec), args=[n, wu, a], carry=carry)


def _ffn_down(name, hm, wd, resid, tm, tn, carry=None):
    J, T, Fs = hm.shape
    D = wd.shape[2]

    def body(h_ref, w_ref, r_ref, o_ref):
        @pl.when(pl.program_id(2) == 0)
        def _():
            o_ref[...] = r_ref[...]

        o_ref[...] += _dot(h_ref[...] * 0.5, w_ref[...], NN)

    tile = pl.BlockSpec((tm, tn), lambda i, n, j: (i, n))
    return _call(
        body, name=name, grid=(T // tm, D // tn, J), out_shape=jax.ShapeDtypeStruct((T, D), F32),
        in_specs=[pl.BlockSpec((None, tm, Fs), lambda i, n, j: (j, i, 0)), pl.BlockSpec((None, Fs, tn), lambda i, n, j: (j, 0, n)), tile],
        out_specs=tile, args=[hm, wd, resid], carry=carry)


def _ffn_bwd_mid(name, dfh, wd, a, b, tm, carry=None):
    T, D = dfh.shape
    J, Fs, _ = wd.shape

    def body(df_ref, w_ref, a_ref, b_ref, da_ref, db_ref):
        dhm = _dot(df_ref[...], w_ref[...], NT)
        av = a_ref[...].astype(F32)
        bv = b_ref[...].astype(F32)
        sg = jax.nn.sigmoid(av)
        da_ref[...] = (dhm * bv * (sg * (1.0 + av * (1.0 - sg)))).astype(BF16)
        db_ref[...] = (dhm * (av * sg)).astype(BF16)

    act = jax.ShapeDtypeStruct((J, T, Fs), BF16)
    aspec = pl.BlockSpec((None, tm, Fs), lambda j, i: (j, i, 0))
    return _call(
        body, name=name, grid=(J, T // tm), out_shape=(act, act),
        in_specs=[pl.BlockSpec((tm, D), lambda j, i: (i, 0)), pl.BlockSpec((None, Fs, D), lambda j, i: (j, 0, 0)), aspec, aspec],
        out_specs=(aspec, aspec), args=[dfh, wd, a, b], carry=carry)


def _rot_half(y, lane):
    first = (lane & (HEAD_DIM // 2)) == 0
    return jnp.where(first, pltpu.roll(y, LANES - HEAD_DIM // 2, 1), pltpu.roll(y, HEAD_DIM // 2, 1))


def _head_rstd(x, lo):
    sq = x * x
    ss_a = jnp.sum(jnp.where(lo, sq, 0.0), axis=-1, keepdims=True)
    ss_b = jnp.sum(jnp.where(lo, 0.0, sq), axis=-1, keepdims=True)
    return lax.rsqrt(jnp.where(lo, ss_a, ss_b) * (1.0 / HEAD_DIM) + EPS)


def _headnorm_fwd(name, proj, col_off, ncb, gains, tm, scale, rope=None, dup=False):
    T = proj.shape[0]
    with_rope = rope is not None
    width = 2 * LANES if dup else LANES

    def body(*refs):
        if with_rope:
            x_ref, g_ref, cos_ref, sin_ref, o_ref = refs
        else:
            x_ref, g_ref, o_ref = refs
        xv = x_ref[...]
        lane = lax.broadcasted_iota(jnp.int32, xv.shape, 1)
        lo = lane < HEAD_DIM
        y = xv * _head_rstd(xv, lo) * g_ref[...]
        if with_rope:
            y = y * cos_ref[...] + _rot_half(y, lane) * sin_ref[...]
        y = y * scale
        if dup:
            sw = pltpu.roll(y, HEAD_DIM, 1)
            o_ref[:, :LANES] = jnp.where(lo, y, sw).astype(BF16)
            o_ref[:, LANES:] = jnp.where(lo, sw, y).astype(BF16)
        else:
            o_ref[...] = y.astype(BF16)

    in_specs = [pl.BlockSpec((tm, LANES), lambda c, i: (i, col_off + c)), pl.BlockSpec((None, 1, LANES), lambda c, i: (c, 0, 0))]
    args = [proj, gains]
    if with_rope:
        tab = pl.BlockSpec((tm, LANES), lambda c, i: (i, 0))
        in_specs += [tab, tab]
        args += list(rope)
    return pl.pallas_call(
        body, out_shape=jax.ShapeDtypeStruct((T, ncb * width), BF16), grid=(ncb, T // tm),
        in_specs=in_specs, out_specs=pl.BlockSpec((tm, width), lambda c, i: (i, c)),
        name=name, compiler_params=_params(2))(*args)


def _headnorm_bwd(name, dy, proj, col_off, ncb, gains, group, tm, scale, rope=None, fold=False, norm=True, into=None):
    T = dy.shape[0]
    with_rope = rope is not None
    n_groups = ncb // group
    dy_width = 4 * LANES if fold else LANES

    def body(*refs):
        refs = list(refs)
        dy_ref = refs.pop(0)
        x_ref = refs.pop(0) if norm else None
        g_ref = refs.pop(0) if norm else None
        cos_ref = refs.pop(0) if with_rope else None
        sin_ref = refs.pop(0) if with_rope else None
        if into is not None:
            refs.pop(0)
        dx_ref = refs.pop(0)
        dg_ref = refs.pop(0) if norm else None
        c = pl.program_id(0)
        i = pl.program_id(1)
        d = dy_ref[...]
        lane = lax.broadcasted_iota(jnp.int32, (d.shape[0], LANES), 1)
        lo = lane < HEAD_DIM
        if fold:
            t0 = d[:, 0:LANES] + d[:, LANES:2 * LANES]
            t1 = d[:, 2 * LANES:3 * LANES] + d[:, 3 * LANES:4 * LANES]
            d = jnp.where(lo, t0 + pltpu.roll(t0, HEAD_DIM, 1), t1 + pltpu.roll(t1, HEAD_DIM, 1))
        d = d * scale
        if with_rope:
            d = d * cos_ref[...] + _rot_half(d * sin_ref[...], lane)
        if not norm:
            dx_ref[...] = d.astype(BF16)
            return
        xv = x_ref[...]
        gv = g_ref[...]
        r = _head_rstd(xv, lo)
        xh = xv * r
        dxh = d * gv
        pr = dxh * xh
        m_a = jnp.sum(jnp.where(lo, pr, 0.0), axis=-1, keepdims=True)
        m_b = jnp.sum(jnp.where(lo, 0.0, pr), axis=-1, keepdims=True)
        mean = jnp.where(lo, m_a, m_b) * (1.0 / HEAD_DIM)
        dx_ref[...] = (r * (dxh - xh * mean)).astype(BF16)
        dgp = jnp.sum(d * xh, axis=0, keepdims=True)
        dgp = dgp + pltpu.roll(dgp, HEAD_DIM, 1)
        first = jnp.logical_and(c % group == 0, i == 0)

        @pl.when(first)
        def _():
            dg_ref[...] = dgp

        @pl.when(jnp.logical_not(first))
        def _():
            dg_ref[...] += dgp

    in_specs = [pl.BlockSpec((tm, dy_width), lambda c, i: (i, c))]
    args = [dy]
    if norm:
        in_specs += [pl.BlockSpec((tm, LANES), lambda c, i: (i, col_off + c)), pl.BlockSpec((None, 1, LANES), lambda c, i: (c, 0, 0))]
        args += [proj, gains]
    if with_rope:
        tab = pl.BlockSpec((tm, LANES), lambda c, i: (i, 0))
        in_specs += [tab, tab]
        args += list(rope)
    out_shape = [jax.ShapeDtypeStruct((T, ncb * LANES), BF16)]
    out_off = 0
    aliases = {}
    if into is not None:
        buf, out_off = into
        assert buf.dtype == BF16 and buf.shape[0] == T and (out_off + ncb) * LANES <= buf.shape[1], buf.shape
        aliases = {len(args): 0}
        in_specs.append(_HBM)
        args.append(buf)
        out_shape = [jax.ShapeDtypeStruct(buf.shape, BF16)]
    out_specs = [pl.BlockSpec((tm, LANES), lambda c, i: (i, out_off + c))]
    if norm:
        out_shape.append(jax.ShapeDtypeStruct((n_groups, 1, LANES), F32))
        out_specs.append(pl.BlockSpec((None, 1, LANES), lambda c, i: (c // group, 0, 0)))
    res = pl.pallas_call(
        body, out_shape=tuple(out_shape), grid=(ncb, T // tm), in_specs=in_specs, out_specs=tuple(out_specs),
        input_output_aliases=aliases, name=name, compiler_params=_params(2))(*args)
    return res if norm else (res[0], None)


def _dot_exact(x, tri):
    hi = x.astype(BF16)
    r1 = x - hi.astype(F32)
    mid = r1.astype(BF16)
    lo = (r1 - mid.astype(F32)).astype(BF16)
    return _dot(hi, tri, NN) + _dot(mid, tri, NN) + _dot(lo, tri, NN)


def _forget_fwd(name, zt, bias):
    H, T = zt.shape
    blk = min(256, T)

    def body(z_ref, b_ref, c_ref, s_ref):
        z = z_ref[...] + b_ref[...]
        s_ref[...] = jax.nn.sigmoid(-z)
        lf = jnp.minimum(z, 0.0) - jnp.log(1.0 + jnp.exp(-jnp.abs(z)))
        tri = (lax.broadcasted_iota(jnp.int32, (blk, blk), 0) <= lax.broadcasted_iota(jnp.int32, (blk, blk), 1)).astype(BF16)
        carry = jnp.zeros((H, 1), F32)
        for bi in range(T // blk):
            xb = lf[:, bi * blk:(bi + 1) * blk]
            c_ref[:, bi * blk:(bi + 1) * blk] = _dot_exact(xb, tri) + carry
            carry = carry + jnp.sum(xb, axis=-1, keepdims=True)

    shape = jax.ShapeDtypeStruct((H, T), F32)
    full = pl.BlockSpec((H, T), lambda i: (0, 0))
    return pl.pallas_call(
        body, out_shape=(shape, shape), grid=(1,), in_specs=[full, pl.BlockSpec((H, 1), lambda i: (0, 0))],
        out_specs=(full, full), name=name, compiler_params=_params(1))(zt, bias)


def _forget_bwd(name, dct, drt, sgt):
    H, T = dct.shape
    blk = min(256, T)

    def body(dc_ref, dr_ref, s_ref, dz_ref, db_ref):
        dc = dc_ref[...] + dr_ref[...]
        tri = (lax.broadcasted_iota(jnp.int32, (blk, blk), 0) >= lax.broadcasted_iota(jnp.int32, (blk, blk), 1)).astype(BF16)
        carry = jnp.zeros((H, 1), F32)
        db = jnp.zeros((H, 1), F32)
        for bi in reversed(range(T // blk)):
            xb = dc[:, bi * blk:(bi + 1) * blk]
            dz = (_dot_exact(xb, tri) + carry) * s_ref[:, bi * blk:(bi + 1) * blk]
            dz_ref[:, bi * blk:(bi + 1) * blk] = dz
            db = db + jnp.sum(dz, axis=-1, keepdims=True)
            carry = carry + jnp.sum(xb, axis=-1, keepdims=True)
        db_ref[...] = db

    full = pl.BlockSpec((H, T), lambda i: (0, 0))
    return pl.pallas_call(
        body, out_shape=(jax.ShapeDtypeStruct((H, T), F32), jax.ShapeDtypeStruct((H, 1), F32)), grid=(1,),
        in_specs=[full, full, full], out_specs=(full, pl.BlockSpec((H, 1), lambda i: (0, 0))),
        name=name, compiler_params=_params(1))(dct, drt, sgt)


FOX_STRIP_FWD = 128
FOX_STRIP_BWD = 256


def _fox_fwd(name, qk, v, crow, tq, tk, strip, carry=None):
    T, Dh = v.shape
    HP = Dh // LANES
    nk = T // tk
    assert tk % tq == 0 and tq % strip == 0
    n_strips = tq // strip

    def body(q_ref, k_ref, v_ref, ra_ref, rb_ref, o_ref, la_ref, lb_ref, s_ref, p_ref, m_ref, l_ref, acc_ref):
        i = pl.program_id(1)
        q2 = q_ref[...]
        lo = _lane_lo((tq, LANES))
        q_st = jnp.concatenate([_keep(lo, q2), _keep(jnp.logical_not(lo), q2)], axis=0)
        r_refs = (ra_ref, rb_ref)
        m_ref[...] = jnp.full(m_ref.shape, NEG, F32)
        l_ref[...] = jnp.zeros(l_ref.shape, F32)
        acc_ref[...] = jnp.zeros(acc_ref.shape, F32)
        rel = lax.broadcasted_iota(jnp.int32, (strip, tk), 0) - lax.broadcasted_iota(jnp.int32, (strip, tk), 1)

        def chunk(kc, masked):
            start = pl.multiple_of(kc * tk, tk)
            kb = k_ref[pl.ds(start, tk), :]
            vb = v_ref[pl.ds(start, tk), :]
            s_ref[...] = _dot(q_st, kb, NT)
            for h in range(2):
                cs = r_refs[h][kc]
                for st in range(n_strips):
                    rows = pl.ds(h * tq + st * strip, strip)
                    s = s_ref[rows, :] - cs
                    if masked:
                        s = jnp.where(rel >= start - (i * tq + st * strip), s, NEG)
                    m_old = m_ref[rows, :]
                    mn = jnp.maximum(m_old, jnp.max(s, axis=-1, keepdims=True))
                    p = jnp.exp(s - mn)
                    alpha = jnp.exp(m_old - mn)
                    l_ref[rows, :] = alpha * l_ref[rows, :] + jnp.sum(p, axis=-1, keepdims=True)
                    m_ref[rows, :] = mn
                    p_ref[rows, :] = p.astype(BF16)
                    acc_ref[rows, :] = acc_ref[rows, :] * alpha
            acc_ref[...] += _dot(p_ref[...], vb, NN)

        n_full = (i * tq) // tk

        def full_chunk(kc, _):
            chunk(kc, False)
            return 0

        lax.fori_loop(0, n_full, full_chunk, 0)
        chunk(n_full, True)
        top, bot = pl.ds(0, tq), pl.ds(tq, tq)
        o_ref[...] = jnp.where(lo, acc_ref[top, :] / l_ref[top, :], acc_ref[bot, :] / l_ref[bot, :])
        la_ref[...] = m_ref[top, :] + jnp.log(l_ref[top, :])
        lb_ref[...] = m_ref[bot, :] + jnp.log(l_ref[bot, :])

    row = lambda off: pl.BlockSpec((None, nk, 1, tk), lambda h, i: (2 * h + off, 0, 0, 0))
    lse = jax.ShapeDtypeStruct((HP, T, 1), F32)
    lspec = pl.BlockSpec((None, tq, 1), lambda h, i: (h, i, 0))
    scratch = [pltpu.VMEM((2 * tq, tk), F32), pltpu.VMEM((2 * tq, tk), BF16), pltpu.VMEM((2 * tq, 1), F32),
               pltpu.VMEM((2 * tq, 1), F32), pltpu.VMEM((2 * tq, LANES), F32)]
    return _call(
        body, name=name, grid=(HP, T // tq), out_shape=(jax.ShapeDtypeStruct((T, Dh), F32), lse, lse),
        in_specs=[pl.BlockSpec((tq, LANES), lambda h, i: (i, h)), pl.BlockSpec((T, LANES), lambda h, i: (0, HP + h)),
                  pl.BlockSpec((T, LANES), lambda h, i: (0, h)), row(0), row(1)],
        out_specs=(pl.BlockSpec((tq, LANES), lambda h, i: (i, h)), lspec, lspec),
        args=[qk, qk, v, crow, crow], scratch_shapes=scratch, carry=carry)


def _fox_bwd(name, qk, v, o, do, crow, lse_a, lse_b, tq, tk, strip, dv_cols, dv_off, carry=None):
    T, Dh = v.shape
    HP = Dh // LANES
    nk = T // tk
    scale = HEAD_DIM ** -0.5
    assert tk % tq == 0 and tq % strip == 0
    n_strips = tq // strip

    def body(q_ref, k_ref, v_ref, o_ref, do_ref, ra_ref, rb_ref, la_ref, lb_ref,
             dq_ref, dk_ref, dv_ref, dca_ref, dcb_ref, dra_ref, drb_ref, s_ref, dp_ref, p_ref, ds_ref, dq_acc, dsum_ref, dv_acc):
        i = pl.program_id(1)

        @pl.when(i == 0)
        def _():
            dk_ref[...] = jnp.zeros_like(dk_ref)
            dv_acc[...] = jnp.zeros(dv_acc.shape, F32)
            dca_ref[...] = jnp.zeros_like(dca_ref)
            dcb_ref[...] = jnp.zeros_like(dcb_ref)

        q2 = q_ref[...]
        do2 = do_ref[...]
        lo = _lane_lo((tq, LANES))
        hi = jnp.logical_not(lo)
        q_st = jnp.concatenate([_keep(lo, q2), _keep(hi, q2)], axis=0)
        do_st = jnp.concatenate([_keep(lo, do2), _keep(hi, do2)], axis=0)
        prod = do2.astype(F32) * o_ref[...]
        dsum_ref[pl.ds(0, tq), :] = jnp.sum(jnp.where(lo, prod, 0.0), axis=-1, keepdims=True)
        dsum_ref[pl.ds(tq, tq), :] = jnp.sum(jnp.where(lo, 0.0, prod), axis=-1, keepdims=True)
        r_refs, l_refs, dc_refs, dr_refs = (ra_ref, rb_ref), (la_ref, lb_ref), (dca_ref, dcb_ref), (dra_ref, drb_ref)
        dq_acc[...] = jnp.zeros(dq_acc.shape, F32)
        dra_ref[...] = jnp.zeros(dra_ref.shape, F32)
        drb_ref[...] = jnp.zeros(drb_ref.shape, F32)
        rel = lax.broadcasted_iota(jnp.int32, (strip, tk), 0) - lax.broadcasted_iota(jnp.int32, (strip, tk), 1)

        def chunk(kc, masked):
            start = pl.multiple_of(kc * tk, tk)
            kb = k_ref[pl.ds(start, tk), :]
            vb = v_ref[pl.ds(start, tk), :]
            s_ref[...] = _dot(q_st, kb, NT)
            dp_ref[...] = _dot(do_st, vb, NT)
            for h in range(2):
                cs = r_refs[h][kc]
                col_sum = jnp.zeros((1, tk), F32)
                for st in range(n_strips):
                    rows = pl.ds(st * strip, strip)
                    both = pl.ds(h * tq + st * strip, strip)
                    s = s_ref[both, :] - cs
                    if masked:
                        s = jnp.where(rel >= start - (i * tq + st * strip), s, NEG)
                    p = jnp.exp(s - l_refs[h][rows, :])
                    ds = p * (dp_ref[both, :] - dsum_ref[both, :])
                    p_ref[both, :] = p.astype(BF16)
                    ds_ref[both, :] = ds.astype(BF16)
                    col_sum = col_sum + jnp.sum(ds, axis=0, keepdims=True)
                    dr_refs[h][rows, :] += jnp.sum(ds, axis=-1, keepdims=True)
                dc_refs[h][kc] = dc_refs[h][kc] - col_sum
            dk_ref[pl.ds(start, tk), :] += _dot(ds_ref[...], q_st, TN)
            dv_acc[pl.ds(start, tk), :] += _dot(p_ref[...], do_st, TN)
            dq_acc[...] += _dot(ds_ref[...], kb, NN)

        n_full = (i * tq) // tk

        def full_chunk(kc, _):
            chunk(kc, False)
            return 0

        lax.fori_loop(0, n_full, full_chunk, 0)
        chunk(n_full, True)
        dq_ref[...] = jnp.where(lo, dq_acc[pl.ds(0, tq), :], dq_acc[pl.ds(tq, tq), :]) * scale

        @pl.when(i == pl.num_programs(1) - 1)
        def _():
            dv_ref[...] = dv_acc[...].astype(BF16)

    row = lambda off: pl.BlockSpec((None, nk, 1, tk), lambda h, i: (2 * h + off, 0, 0, 0))
    lspec = pl.BlockSpec((None, tq, 1), lambda h, i: (h, i, 0))
    qspec = pl.BlockSpec((tq, LANES), lambda h, i: (i, h))
    full = pl.BlockSpec((T, LANES), lambda h, i: (0, h))
    dcspec = pl.BlockSpec((None, nk, 1, tk), lambda h, i: (h, 0, 0, 0))
    grad = jax.ShapeDtypeStruct((T, Dh), F32)
    dc = jax.ShapeDtypeStruct((HP, nk, 1, tk), F32)
    dr = jax.ShapeDtypeStruct((HP, T, 1), F32)
    scratch = [pltpu.VMEM((2 * tq, tk), F32), pltpu.VMEM((2 * tq, tk), F32), pltpu.VMEM((2 * tq, tk), BF16), pltpu.VMEM((2 * tq, tk), BF16),
               pltpu.VMEM((2 * tq, LANES), F32), pltpu.VMEM((2 * tq, 1), F32), pltpu.VMEM((T, LANES), F32)]
    assert (dv_off + HP) * LANES <= dv_cols
    dv = jax.ShapeDtypeStruct((T, dv_cols), BF16)
    dvspec = pl.BlockSpec((T, LANES), lambda h, i: (0, dv_off + h))
    return _call(
        body, name=name, grid=(HP, T // tq), out_shape=(grad, grad, dv, dc, dc, dr, dr),
        in_specs=[qspec, pl.BlockSpec((T, LANES), lambda h, i: (0, HP + h)), full, qspec, qspec, row(0), row(1), lspec, lspec],
        out_specs=(qspec, full, dvspec, dcspec, dcspec, lspec, lspec),
        args=[qk, qk, v, o, do, crow, crow, lse_a, lse_b], scratch_shapes=scratch, carry=carry)


SWA_GROUP = 2
SWA_GROUP_BWD = 4


def _swa_block(n, q_ref, k_ref):
    qs = pl.multiple_of(n * WINDOW, WINDOW)
    ks = pl.multiple_of(jnp.maximum(n - 1, 0) * WINDOW, WINDOW)
    rel = (qs + lax.broadcasted_iota(jnp.int32, (WINDOW, 2 * WINDOW), 0)) - (ks + lax.broadcasted_iota(jnp.int32, (WINDOW, 2 * WINDOW), 1))
    valid = jnp.logical_and(rel >= 0, rel < WINDOW)
    return qs, ks, valid


def _swa_fwd(name, q, kd, vd, sinks, carry=None):
    T, Dh = q.shape
    HP = Dh // LANES

    def body(q_ref, k_ref, v_ref, sa_ref, sb_ref, o_ref, la_ref, lb_ref):
        lo = _lane_lo((WINDOW, LANES))

        top = lax.broadcasted_iota(jnp.int32, (2 * WINDOW, 1), 0) < WINDOW
        sink = jnp.where(top, sa_ref[...], sb_ref[...])

        def block(n, _):
            qs, ks, valid = _swa_block(n, q_ref, k_ref)
            q2 = q_ref[pl.ds(qs, WINDOW), :]
            kb = k_ref[pl.ds(ks, 2 * WINDOW), :]
            vb = v_ref[pl.ds(ks, 2 * WINDOW), :]
            q_st = jnp.concatenate([_keep(lo, q2), _keep(jnp.logical_not(lo), q2)], axis=0)
            s = jnp.where(jnp.concatenate([valid, valid], axis=0), _dot(q_st, kb, NT), NEG)
            m = jnp.maximum(jnp.max(s, axis=-1, keepdims=True), sink)
            p = jnp.exp(s - m)
            l = jnp.sum(p, axis=-1, keepdims=True) + jnp.exp(sink - m)
            o2 = _dot(p.astype(BF16), vb, NN) / l
            lse = m + jnp.log(l)
            o_ref[pl.ds(qs, WINDOW), :] = jnp.where(lo, o2[:WINDOW], o2[WINDOW:])
            la_ref[pl.ds(qs, WINDOW), :] = lse[:WINDOW]
            lb_ref[pl.ds(qs, WINDOW), :] = lse[WINDOW:]
            return 0

        assert (T // WINDOW) % SWA_GROUP == 0

        def group(g, c):
            for b in range(SWA_GROUP):
                c = block(g * SWA_GROUP + b, c)
            return c

        lax.fori_loop(0, T // WINDOW // SWA_GROUP, group, 0)

    full = pl.BlockSpec((T, LANES), lambda h: (0, h))
    kv = pl.BlockSpec((T, LANES), lambda h: (0, h // 2))
    sink = lambda off: pl.BlockSpec((None, 1, 1), lambda h: (2 * h + off, 0, 0))
    lse = jax.ShapeDtypeStruct((HP, T, 1), F32)
    lspec = pl.BlockSpec((None, T, 1), lambda h: (h, 0, 0))
    return _call(
        body, name=name, grid=(HP,), out_shape=(jax.ShapeDtypeStruct((T, Dh), F32), lse, lse),
        in_specs=[full, kv, kv, sink(0), sink(1)], out_specs=(full, lspec, lspec),
        args=[q, kd, vd, sinks, sinks], carry=carry)


def _swa_bwd(name, q, kd, vd, sinks, o, do, lse_a, lse_b, carry=None):
    T, Dh = q.shape
    HP = Dh // LANES
    scale = HEAD_DIM ** -0.5

    def body(q_ref, k_ref, v_ref, sa_ref, sb_ref, o_ref, do_ref, la_ref, lb_ref, dq_ref, dk_ref, dv_ref, dsa_ref, dsb_ref):
        lo = _lane_lo((WINDOW, LANES))
        hi = jnp.logical_not(lo)
        dk_ref[...] = jnp.zeros_like(dk_ref)
        dv_ref[...] = jnp.zeros_like(dv_ref)

        top = lax.broadcasted_iota(jnp.int32, (2 * WINDOW, 1), 0) < WINDOW
        sink = jnp.where(top, sa_ref[...], sb_ref[...])

        def block(n, dsinks):
            qs, ks, valid = _swa_block(n, q_ref, k_ref)
            rows = pl.ds(qs, WINDOW)
            q2 = q_ref[rows, :]
            do2 = do_ref[rows, :]
            kb = k_ref[pl.ds(ks, 2 * WINDOW), :]
            vb = v_ref[pl.ds(ks, 2 * WINDOW), :]
            prod = do2.astype(F32) * o_ref[rows, :]
            q_st = jnp.concatenate([_keep(lo, q2), _keep(hi, q2)], axis=0)
            do_st = jnp.concatenate([_keep(lo, do2), _keep(hi, do2)], axis=0)
            dsum = jnp.concatenate([jnp.sum(jnp.where(lo, prod, 0.0), axis=-1, keepdims=True),
                                    jnp.sum(jnp.where(lo, 0.0, prod), axis=-1, keepdims=True)], axis=0)
            lse = jnp.concatenate([la_ref[rows, :], lb_ref[rows, :]], axis=0)
            s = jnp.where(jnp.concatenate([valid, valid], axis=0), _dot(q_st, kb, NT), NEG)
            p = jnp.exp(s - lse)
            ds = p * (_dot(do_st, vb, NT) - dsum)
            dsb = ds.astype(BF16)
            dq2 = _dot(dsb, kb, NN)
            dq_ref[rows, :] = jnp.where(lo, dq2[:WINDOW], dq2[WINDOW:]) * scale
            dk_ref[pl.ds(ks, 2 * WINDOW), :] += _dot(dsb, q_st, TN)
            dv_ref[pl.ds(ks, 2 * WINDOW), :] += _dot(p.astype(BF16), do_st, TN)
            gone = jnp.exp(sink - lse) * dsum
            return (dsinks[0] - jnp.sum(gone[:WINDOW], axis=0, keepdims=True),
                    dsinks[1] - jnp.sum(gone[WINDOW:], axis=0, keepdims=True))

        assert (T // WINDOW) % SWA_GROUP_BWD == 0

        def group(g, c):
            for b in range(SWA_GROUP_BWD):
                c = block(g * SWA_GROUP_BWD + b, c)
            return c

        dsa, dsb_ = lax.fori_loop(0, T // WINDOW // SWA_GROUP_BWD, group, (jnp.zeros((1, 1), F32), jnp.zeros((1, 1), F32)))
        dsa_ref[...] = dsa
        dsb_ref[...] = dsb_

    full = pl.BlockSpec((T, LANES), lambda h: (0, h))
    kv = pl.BlockSpec((T, LANES), lambda h: (0, h // 2))
    sink = lambda off: pl.BlockSpec((None, 1, 1), lambda h: (2 * h + off, 0, 0))
    lspec = pl.BlockSpec((None, T, 1), lambda h: (h, 0, 0))
    dsink = pl.BlockSpec((None, 1, 1), lambda h: (h, 0, 0))
    grad = jax.ShapeDtypeStruct((T, Dh), F32)
    ds_shape = jax.ShapeDtypeStruct((HP, 1, 1), F32)
    return _call(
        body, name=name, grid=(HP,), out_shape=(grad, grad, grad, ds_shape, ds_shape),
        in_specs=[full, kv, kv, sink(0), sink(1), full, full, lspec, lspec],
        out_specs=(full, full, full, dsink, dsink),
        args=[q, kd, vd, sinks, sinks, o, do, lse_a, lse_b], carry=carry)


def _place():
    return lax.axis_index("x"), lax.axis_index("y"), lax.axis_index("c")


def _run_carry(name, carry):
    c_in, c_out = len(carry.inputs), len(carry.out_shapes)

    def body(*refs):
        ins, outs, scr = refs[:c_in], refs[c_in:c_in + c_out], refs[c_in + c_out:]
        carry.start(ins, outs, scr)
        carry.middle(ins, outs, scr)
        carry.finish(ins, outs, scr)

    return pl.pallas_call(
        body, out_shape=tuple(carry.out_shapes), in_specs=[_HBM] * c_in, out_specs=tuple([_HBM] * c_out),
        scratch_shapes=carry.scratch, name=name)(*carry.inputs)


def _gather_carry(shards):
    n = len(shards)

    def plan(ins, outs, scr):
        send, recv, local = scr
        x, y, c = _place()
        me, sibling = (x, y, c), (x, y, 1 - c)
        partner, other, diag = (x ^ c, y ^ (1 - c)), (x ^ (1 - c), y ^ c), (1 - x, 1 - y)

        def copy(w, k, block, to, src=None):
            slot = 4 * block[0] + 2 * block[1] + block[2]
            return pltpu.make_async_remote_copy(
                src_ref=outs[w].at[slot] if src is None else src, dst_ref=outs[w].at[slot],
                send_sem=send.at[w, k], recv_sem=recv.at[w, k], device_id=to, device_id_type=MESH)

        def own():
            return [pltpu.make_async_copy(ins[w], outs[w].at[4 * x + 2 * y + c], local.at[w]) for w in range(n)]

        return copy, own, me, sibling, partner, other, diag, c

    def start(ins, outs, scr):
        copy, own, me, sibling, partner, other, _, c = plan(ins, outs, scr)
        for cp in own():
            cp.start()
        for w in range(n):
            copy(w, 1, me, (*partner, c), src=ins[w]).start()
            copy(w, 2, me, (*other, c), src=ins[w]).start()
            copy(w, 0, me, sibling, src=ins[w]).start()

    def middle(ins, outs, scr):
        copy, _, me, sibling, partner, other, _, c = plan(ins, outs, scr)
        for w in range(n):
            copy(w, 1, (*partner, c), me).wait_recv()
            copy(w, 3, (*partner, c), (*other, c)).start()
            copy(w, 4, (*partner, c), sibling).start()
        for w in range(n):
            copy(w, 2, (*other, c), me).wait_recv()
            copy(w, 5, (*other, c), sibling).start()

    def finish(ins, outs, scr):
        copy, own, me, sibling, partner, other, diag, c = plan(ins, outs, scr)
        for w in range(n):
            copy(w, 3, (*diag, c), me).wait_recv()
            copy(w, 6, (*diag, c), sibling).start()
        for w in range(n):
            copy(w, 0, sibling, me).wait_recv()
            copy(w, 4, (*other, 1 - c), me).wait_recv()
            copy(w, 5, (*partner, 1 - c), me).wait_recv()
            copy(w, 6, (*diag, 1 - c), me).wait_recv()
        for w in range(n):
            sent = [copy(w, 0, me, sibling, src=ins[w]), copy(w, 1, me, (*partner, c), src=ins[w]), copy(w, 2, me, (*other, c), src=ins[w]),
                    copy(w, 3, (*partner, c), (*other, c)), copy(w, 4, (*partner, c), sibling), copy(w, 5, (*other, c), sibling),
                    copy(w, 6, (*diag, c), sibling)]
            for cp in sent:
                cp.wait_send()
        for cp in own():
            cp.wait()

    return _Carry(shards, [jax.ShapeDtypeStruct((N_DEV,) + s.shape, s.dtype) for s in shards],
                  [pltpu.SemaphoreType.DMA((n, 7)), pltpu.SemaphoreType.DMA((n, 7)), pltpu.SemaphoreType.DMA((n,))], start, finish, middle)


def _sibling_carry(grads):
    n = len(grads)

    def copies(ins, outs, scr):
        send, recv = scr
        x, y, c = _place()
        return [pltpu.make_async_remote_copy(
            src_ref=ins[w].at[2 * q + (1 - c)], dst_ref=outs[w].at[q], send_sem=send.at[w, q], recv_sem=recv.at[w, q],
            device_id=(x, y, 1 - c), device_id_type=MESH) for w in range(n) for q in range(4)]

    def start(ins, outs, scr):
        for cp in copies(ins, outs, scr):
            cp.start()

    def finish(ins, outs, scr):
        for cp in copies(ins, outs, scr):
            cp.wait()

    return _Carry(grads, [jax.ShapeDtypeStruct((4,) + g.shape[1:], g.dtype) for g in grads],
                  [pltpu.SemaphoreType.DMA((n, 4)), pltpu.SemaphoreType.DMA((n, 4))], start, finish)


def _to_partner_carry(sums):
    n = len(sums)

    def copies(ins, outs, scr):
        send, recv = scr
        x, y, c = _place()
        return [pltpu.make_async_remote_copy(
            src_ref=ins[w].at[k], dst_ref=outs[2 * w + k], send_sem=send.at[w, k], recv_sem=recv.at[w, k],
            device_id=(x ^ c, y ^ (1 - c), c), device_id_type=MESH) for w in range(n) for k in range(2)]

    def start(ins, outs, scr):
        for cp in copies(ins, outs, scr):
            cp.start()

    def finish(ins, outs, scr):
        for cp in copies(ins, outs, scr):
            cp.wait()

    return _Carry(sums, [jax.ShapeDtypeStruct(s.shape[1:], s.dtype) for s in sums for _ in range(2)],
                  [pltpu.SemaphoreType.DMA((n, 2)), pltpu.SemaphoreType.DMA((n, 2))], start, finish)


def _to_other_carry(blocks):
    n = len(blocks)

    def copies(ins, outs, scr):
        send, recv = scr
        x, y, c = _place()
        return [pltpu.make_async_remote_copy(
            src_ref=ins[w], dst_ref=outs[w], send_sem=send.at[w], recv_sem=recv.at[w],
            device_id=(x ^ (1 - c), y ^ c, c), device_id_type=MESH) for w in range(n)]

    def start(ins, outs, scr):
        for cp in copies(ins, outs, scr):
            cp.start()

    def finish(ins, outs, scr):
        for cp in copies(ins, outs, scr):
            cp.wait()

    return _Carry(blocks, [jax.ShapeDtypeStruct(b.shape, b.dtype) for b in blocks],
                  [pltpu.SemaphoreType.DMA((n,)), pltpu.SemaphoreType.DMA((n,))], start, finish)


def _gather_small(packed):
    R, C = packed.shape

    def body(in_ref, out_ref, send, recv):
        x, y, c = _place()
        mine = 4 * x + 2 * y + c
        out_ref[mine] = in_ref[...]
        copies = []
        for k in range(1, N_DEV):
            peer = (x ^ (k >> 2), y ^ ((k >> 1) & 1), c ^ (k & 1))
            copies.append(pltpu.make_async_remote_copy(
                src_ref=in_ref, dst_ref=out_ref.at[mine], send_sem=send.at[k - 1], recv_sem=recv.at[k - 1],
                device_id=peer, device_id_type=MESH))
        for cp in copies:
            cp.start()
        for cp in copies:
            cp.wait()

    vmem = pl.BlockSpec(memory_space=pltpu.VMEM)
    return pl.pallas_call(
        body, out_shape=jax.ShapeDtypeStruct((N_DEV, R, C), F32), in_specs=[vmem], out_specs=vmem,
        scratch_shapes=[pltpu.SemaphoreType.DMA((N_DEV - 1,)), pltpu.SemaphoreType.DMA((N_DEV - 1,))],
        name="small_grads_all_gather")(packed)


def _adamw(w, g, m, v):
    m = ADAM_B1 * m + (1.0 - ADAM_B1) * g
    v = ADAM_B2 * v + (1.0 - ADAM_B2) * (g * g)
    m_hat = m / (1.0 - ADAM_B1 ** ADAM_STEP)
    v_hat = v / (1.0 - ADAM_B2 ** ADAM_STEP)
    delta = -ADAM_LR * (m_hat / (jnp.sqrt(v_hat) + ADAM_EPS) + ADAM_WD * w)
    return delta, m, v


def _pair_add(name, grads, received, slots):
    _, R, C = grads.shape
    tr = _row_tile(R, 2 * ROW_TILE_CAP)

    def body(s_ref, g_ref, r_ref, o_ref):
        o_ref[...] = (g_ref[...].astype(F32) + r_ref[...].astype(F32)).astype(BF16)

    return pl.pallas_call(
        body, out_shape=jax.ShapeDtypeStruct((3, R, C), BF16),
        grid_spec=pltpu.PrefetchScalarGridSpec(
            num_scalar_prefetch=1, grid=(3, R // tr),
            in_specs=[pl.BlockSpec((None, tr, C), lambda k, i, s: (s[k], i, 0)), pl.BlockSpec((None, tr, C), lambda k, i, s: (s[3 + k], i, 0))],
            out_specs=pl.BlockSpec((None, tr, C), lambda k, i, s: (k, i, 0))),
        name=name, compiler_params=_params(2))(slots, grads, received)


def _relay_add(name, sums, relayed):
    _, R, C = sums.shape
    tr = _row_tile(R, 2 * ROW_TILE_CAP)

    def body(s_ref, r_ref, o_ref):
        o_ref[...] = (s_ref[...].astype(F32) + r_ref[...].astype(F32)).astype(BF16)

    blk = pl.BlockSpec((tr, C), lambda i: (i, 0))
    return pl.pallas_call(
        body, out_shape=jax.ShapeDtypeStruct((R, C), BF16), grid=(R // tr,),
        in_specs=[pl.BlockSpec((None, tr, C), lambda i: (2, i, 0)), blk], out_specs=blk,
        name=name, compiler_params=_params(1))(sums, relayed)


def _adam_shard(name, grads, from_sibling, received, w, m, v, own):
    R, C = w.shape
    tr = _row_tile(R, 256)
    tc = C if tr < R or C % (4 * LANES) else 4 * LANES

    def body(o_ref, g_ref, s_ref, ra_ref, rb_ref, w_ref, m_ref, v_ref, g_out, d_out, m_out, v_out):
        g = (g_ref[...].astype(F32) + s_ref[...].astype(F32)) + ra_ref[...].astype(F32) + rb_ref[...].astype(F32)
        delta, mn, vn = _adamw(w_ref[...], g, m_ref[...], v_ref[...])
        g_out[...] = g
        d_out[...] = delta
        m_out[...] = mn
        v_out[...] = vn

    blk = pl.BlockSpec((tr, tc), lambda i, j, o: (i, j))
    shape = jax.ShapeDtypeStruct((R, C), F32)
    return pl.pallas_call(
        body, out_shape=(shape,) * 4,
        grid_spec=pltpu.PrefetchScalarGridSpec(
            num_scalar_prefetch=1, grid=(R // tr, C // tc),
            in_specs=[pl.BlockSpec((None, tr, tc), lambda i, j, o: (o[0], i, j)), pl.BlockSpec((None, tr, tc), lambda i, j, o: (o[1], i, j)),
                      blk, blk, blk, blk, blk],
            out_specs=(blk,) * 4),
        name=name, compiler_params=_params(2))(own, grads, from_sibling, received[0], received[1], w, m, v)


def _adam_small(name, gathered, w, m, v):
    R, C = w.shape

    def body(ga_ref, w_ref, m_ref, v_ref, g_out, d_out, m_out, v_out):
        g = ga_ref[0]
        for d in range(1, N_DEV):
            g = g + ga_ref[d]
        delta, mn, vn = _adamw(w_ref[...], g, m_ref[...], v_ref[...])
        g_out[...] = g
        d_out[...] = delta
        m_out[...] = mn
        v_out[...] = vn

    full = pl.BlockSpec((R, C), lambda i: (0, 0))
    shape = jax.ShapeDtypeStruct((R, C), F32)
    return pl.pallas_call(
        body, out_shape=(shape,) * 4, grid=(1,),
        in_specs=[pl.BlockSpec((N_DEV, R, C), lambda i: (0, 0, 0)), full, full, full], out_specs=(full,) * 4,
        name=name, compiler_params=_params(1))(gathered, w, m, v)


def _pack_small(parts, D, scalar=None):
    g1, gmix, g2, gof, gos, bf, gqf, gkf, gqs, gks, sinks = [p.reshape(-1).astype(F32) for p in parts]
    row3 = jnp.concatenate([gof, gos])
    row4 = jnp.zeros((D,), F32)
    for slot, vec in enumerate((bf, gqf, gkf, gqs, gks, sinks)):
        row4 = lax.dynamic_update_slice(row4, vec, (slot * LANES,))
    zero = jnp.zeros((D,), F32)
    row5 = zero if scalar is None else lax.dynamic_update_slice(zero, jnp.reshape(scalar, (1,)).astype(F32), (0,))
    return jnp.stack([g1, gmix, g2, row3, row4, row5, zero, zero])


def _unpack_small(packed, D, H):
    Dh = D // 2
    row4 = packed[4]
    short = [row4[s * LANES:s * LANES + n] for s, n in enumerate((H, HEAD_DIM, HEAD_DIM, HEAD_DIM, HEAD_DIM, H))]
    vecs = [packed[0], packed[1], packed[2], packed[3, :Dh], packed[3, Dh:]] + short
    return [v[None, :] for v in vecs]


def kernel(x, positions, norm_ffn1_g, ffn1_w_gate, ffn1_w_up, ffn1_w_down, norm_mix_g, w_in, b_forget, fox_q_norm_g, fox_k_norm_g, swa_q_norm_g, swa_k_norm_g, swa_sinks, out_norm_fox_g, out_norm_swa_g, w_out, norm_ffn2_g, ffn2_w_gate, ffn2_w_up, ffn2_w_down, loss_target, m_norm_ffn1_g, m_ffn1_w_gate, m_ffn1_w_up, m_ffn1_w_down, m_norm_mix_g, m_w_in, m_b_forget, m_fox_q_norm_g, m_fox_k_norm_g, m_swa_q_norm_g, m_swa_k_norm_g, m_swa_sinks, m_out_norm_fox_g, m_out_norm_swa_g, m_w_out, m_norm_ffn2_g, m_ffn2_w_gate, m_ffn2_w_up, m_ffn2_w_down, v_norm_ffn1_g, v_ffn1_w_gate, v_ffn1_w_up, v_ffn1_w_down, v_norm_mix_g, v_w_in, v_b_forget, v_fox_q_norm_g, v_fox_k_norm_g, v_swa_q_norm_g, v_swa_k_norm_g, v_swa_sinks, v_out_norm_fox_g, v_out_norm_swa_g, v_w_out, v_norm_ffn2_g, v_ffn2_w_gate, v_ffn2_w_up, v_ffn2_w_down):
    xs = x[0]
    target = loss_target[0]
    T, D = xs.shape
    Dh = D // 2
    H = Dh // HEAD_DIM
    HP = H // 2
    KVW = (H // GQA_GROUP) * HEAD_DIM
    KVB = KVW // LANES
    MAIN = 4 * Dh + 2 * KVW
    F_OFF = 3 * Dh
    tm = min(ROW_TILE_CAP, T)
    tm2 = min(2 * ROW_TILE_CAP, T)
    tq = min(512, T)
    tk = min(512, T)
    nk = T // tk
    cx, cy, cc = _place()
    near = [2 * (cx ^ cc) + (cy ^ (1 - cc)), 2 * (1 - cx) + (1 - cy), 2 * (cx ^ (1 - cc)) + (cy ^ cc)]
    slots = jnp.stack([2 * q + cc for q in near] + near).astype(jnp.int32)
    own = jnp.stack([4 * cx + 2 * cy + cc, 2 * cx + cy]).astype(jnp.int32)

    tr = jnp.transpose
    big_w = [tr(ffn1_w_gate[0]), tr(ffn1_w_up[0]), ffn1_w_down[0], tr(w_in[0]), w_out[0], tr(ffn2_w_gate[0]), tr(ffn2_w_up[0]),
             ffn2_w_down[0]]
    big_m = [tr(m_ffn1_w_gate[0]), tr(m_ffn1_w_up[0]), m_ffn1_w_down[0], tr(m_w_in[0]), m_w_out[0], tr(m_ffn2_w_gate[0]),
             tr(m_ffn2_w_up[0]), m_ffn2_w_down[0]]
    big_v = [tr(v_ffn1_w_gate[0]), tr(v_ffn1_w_up[0]), v_ffn1_w_down[0], tr(v_w_in[0]), v_w_out[0], tr(v_ffn2_w_gate[0]),
             tr(v_ffn2_w_up[0]), v_ffn2_w_down[0]]
    transposed = {"ffn1_w_gate", "ffn1_w_up", "w_in", "ffn2_w_gate", "ffn2_w_up"}
    names = ["ffn1_w_gate", "ffn1_w_up", "ffn1_w_down", "w_in", "w_out", "ffn2_w_gate", "ffn2_w_up", "ffn2_w_down"]
    sh = dict(zip(names, [w.astype(BF16) for w in big_w]))
    lane = jnp.arange(LANES)
    inv_freq = ROPE_THETA ** (-(2.0 * (lane % (HEAD_DIM // 2))).astype(F32) / HEAD_DIM)
    ang = positions[0].astype(F32)[:, None] * inv_freq[None, :]
    cos_t = jnp.cos(ang)
    sin_t = jnp.where((lane & (HEAD_DIM // 2)) == 0, -1.0, 1.0)[None, :] * jnp.sin(ang)
    rope = (cos_t, sin_t)

    def pair_gain(g, blocks):
        return jnp.tile(jnp.concatenate([g[0], g[0]])[None, None, :], (blocks, 1, 1))

    n1, (wg1,) = _rmsnorm_fwd("ffn1_norm", xs, norm_ffn1_g, tm, carry=_gather_carry([sh["ffn1_w_gate"]]))
    a1, (wu1,) = _ffn_gate("ffn1_gate", n1, wg1, tm, carry=_gather_carry([sh["ffn1_w_up"]]))
    (b1, hm1), (wd1,) = _ffn_up_only("ffn1_up", n1, wu1, a1, tm, carry=_gather_carry([sh["ffn1_w_down"]]))
    h1, (win_g,) = _ffn_down("ffn1_down", hm1, wd1, xs, tm, D, carry=_gather_carry([sh["w_in"]]))
    n_in = win_g.shape[1]
    win_t = win_g.reshape(N_DEV * n_in, D)
    win_f = jnp.pad(win_t[F_OFF:F_OFF + H], ((0, LANES - H), (0, 0)))
    tkb = MAIN // 9
    assert F_OFF % tkb == 0

    def main_row(r):
        return pl.multiple_of(r * tkb + H * (r >= F_OFF // tkb).astype(jnp.int32), min(H, tkb))

    u = _rmsnorm_fwd("mix_norm", h1, norm_mix_g, tm)
    (proj, proj_b), (wout_g,) = _mm("mix_proj", u, win_t, T, tkb, dims=NT, carry=_gather_carry([sh["w_out"]]),
                                    b_rows=(9, main_row), bf16_copy=True)
    wout = wout_g.reshape(D, D)
    proj_f = _mm("mix_proj_forget", u, win_f, T, LANES, dims=NT)
    scale = HEAD_DIM ** -0.5
    fox_gains = jnp.concatenate([pair_gain(fox_q_norm_g, HP), pair_gain(fox_k_norm_g, HP)])
    qk_f = _headnorm_fwd_scaled("fox_qk_norm", proj, 0, 2 * HP, fox_gains, T, scale, HP)
    v_f = proj_b[:, 2 * Dh:3 * Dh]
    c_t, sg_t = _forget_fwd("forget_gates", proj_f[:, :H].T, b_forget.reshape(H, 1))
    crow = c_t.reshape(H, nk, 1, tk)
    (o_fox, lse_fa, lse_fb), (wg2, wu2) = _fox_fwd("fox_attention", qk_f, v_f, crow, tq, tk, min(FOX_STRIP_FWD, tq),
                                                   carry=_gather_carry([sh["ffn2_w_gate"], sh["ffn2_w_up"]]))

    swa_q_gains = pair_gain(swa_q_norm_g, HP)
    swa_k_gains = pair_gain(swa_k_norm_g, KVB)
    q_s = _headnorm_fwd("swa_q_norm", proj, 3 * HP, HP, swa_q_gains, T, scale, rope=rope)
    k_d = _headnorm_fwd("swa_k_norm", proj, 4 * HP, KVB, swa_k_gains, T, 1.0, rope=rope, dup=True)
    v_s = proj_b[:, 4 * Dh + KVW:].reshape(T, H // GQA_GROUP, 1, HEAD_DIM)
    v_d = jnp.broadcast_to(v_s, (T, H // GQA_GROUP, 2, HEAD_DIM)).reshape(T, 2 * KVW)
    sinks3 = swa_sinks.reshape(H, 1, 1)
    o_swa, lse_sa, lse_sb = _swa_fwd("swa_attention", q_s, k_d, v_d, sinks3)

    on = _outnorm_fwd("out_norm", o_fox, o_swa, out_norm_fox_g, out_norm_swa_g, tm)
    h2 = _mm("mix_out", on, wout, tm2, min(512, D), resid=h1)

    n2 = _rmsnorm_fwd("ffn2_norm", h2, norm_ffn2_g, tm)
    (a2, b2, hm2), (wd2,) = _ffn_up("ffn2_up", n2, wg2, wu2, tm2, carry=_gather_carry([sh["ffn2_w_down"]]))
    y = _ffn_down("ffn2_down", hm2, wd2, h2, tm2, D)
    dy, dyh, sq = _loss_grad("loss_grad", y, target, tm)
    loss_part = 0.5 * sq[0, 0] / D

    J, Fs, _ = wg2.shape
    aspec = pl.BlockSpec((None, tm, Fs), lambda i, j: (j, i, 0))
    wspec = pl.BlockSpec((None, Fs, D), lambda i, j: (j, 0, 0))
    got = {}
    local = {}

    def pair_sums(keys, grads, received):
        for nm, g, r in zip(keys, grads, received):
            local[nm] = (g, r)
        return [_pair_add("sum_" + nm, g, r, slots) for nm, g, r in zip(keys, grads, received)]

    def relay_sums(keys, sums, hop1):
        out = []
        for i, (nm, s) in enumerate(zip(keys, sums)):
            got[nm] = [hop1[2 * i]]
            out.append(_relay_add("relay_" + nm, s, hop1[2 * i + 1]))
        return out

    def arrived(keys, hop2):
        for nm, blk in zip(keys, hop2):
            got[nm].append(blk)

    dwd2 = _wgrad_down("ffn2_wgrad_down", hm2, dyh, D)
    (da2, db2), (sib_d2,) = _ffn_bwd_mid("ffn2_bwd_mid", dyh, wd2, a2, b2, tm2, carry=_sibling_carry([dwd2]))
    (sum_wd2,) = pair_sums(names[7:8], [dwd2], [sib_d2])
    (dwg2, dwu2), hop1 = _wgrad_up("ffn2_wgrad_up", n2, da2, db2, min(1024, D), carry=_to_partner_carry([sum_wd2]))
    (t_wd2,) = relay_sums(names[7:8], [sum_wd2], hop1)
    aspec2 = pl.BlockSpec((None, tm2, Fs), lambda i, j: (j, i, 0))
    dn2, (via_wd2, *sib2) = _reduce_mm("ffn2_bwd_in", [(da2, aspec2, wg2, wspec), (db2, aspec2, wu2, wspec)], [], NN, T, D, tm2, J,
                                       carry=_join(_to_other_carry([t_wd2]), _sibling_carry([dwg2, dwu2])))
    arrived(names[7:8], [via_wd2])
    dh2, dg_ffn2, dh2b = _rmsnorm_bwd("ffn2_norm_bwd", dn2, h2, norm_ffn2_g, dy, min(256, T), 1.0)
    sum_wg2, sum_wu2 = pair_sums(names[5:7], [dwg2, dwu2], sib2)

    dwout = _wgrad_2d("mix_out_wgrad", on, dh2b, min(512, D), D)
    dwout_g = dwout.reshape(N_DEV, D // N_DEV, D)
    do_fox, dg_of = _outnorm_bwd("out_norm_bwd_fox", dh2b, wout, 0, o_fox, out_norm_fox_g, tm)
    do_swa, dg_os = _outnorm_bwd("out_norm_bwd_swa", dh2b, wout, 1, o_swa, out_norm_swa_g, tm)

    (dq_f, dk_f, dproj, dc_a, dc_b, dr_a, dr_b), (*hop1, sib_wout) = _fox_bwd(
        "fox_attention_bwd", qk_f, v_f, o_fox, do_fox, crow, lse_fa, lse_fb, tq, tk, min(FOX_STRIP_BWD, tq), MAIN, 2 * HP,
        carry=_join(_to_partner_carry([sum_wg2, sum_wu2]), _sibling_carry([dwout_g])))
    t_wg2, t_wu2 = relay_sums(names[5:7], [sum_wg2, sum_wu2], hop1)
    (sum_wout,) = pair_sums(names[4:5], [dwout_g], [sib_wout])
    dproj, dg_fq = _headnorm_bwd("fox_q_norm_bwd", dq_f, proj, 0, HP, fox_gains[:HP], HP, T, 1.0, into=(dproj, 0))
    dproj, dg_fk = _headnorm_bwd("fox_k_norm_bwd", dk_f, proj, HP, HP, fox_gains[HP:], HP, T, 1.0, into=(dproj, HP))
    dct = jnp.stack([dc_a.reshape(HP, T), dc_b.reshape(HP, T)], axis=1).reshape(H, T)
    drt = jnp.stack([dr_a.reshape(HP, T), dr_b.reshape(HP, T)], axis=1).reshape(H, T)
    dz_t, db_f = _forget_bwd("forget_gates_bwd", dct, drt, sg_t)

    (dq_s, dk_p, dv_p, dsink_a, dsink_b), hop2 = _swa_bwd(
        "swa_attention_bwd", q_s, k_d, v_d, sinks3, o_swa, do_swa, lse_sa, lse_sb, carry=_to_other_carry([t_wg2, t_wu2]))
    arrived(names[5:7], hop2)
    dproj, dg_sq = _headnorm_bwd("swa_q_norm_bwd", dq_s, proj, 3 * HP, HP, swa_q_gains, HP, T, 1.0, rope=rope, into=(dproj, 3 * HP))
    dproj, dg_sk = _headnorm_bwd("swa_k_norm_bwd", dk_p, proj, 4 * HP, KVB, swa_k_gains, KVB, T, 1.0, rope=rope, fold=True,
                                 into=(dproj, 4 * HP))
    dproj, _ = _headnorm_bwd("swa_v_fold", dv_p, None, 0, KVB, None, KVB, T, 1.0, fold=True, norm=False, into=(dproj, 4 * HP + KVB))
    dproj_f = jnp.pad(dz_t.T, ((0, 0), (0, LANES - H))).astype(BF16)
    dwin_t, hop1 = _wgrad_2d("mix_proj_wgrad", dproj, u, tkb, D, carry=_to_partner_carry([sum_wout]),
                             out_rows=(N_DEV * n_in, main_row))
    (t_wout,) = relay_sums(names[4:5], [sum_wout], hop1)
    dwin_f = _wgrad_2d("mix_proj_forget_wgrad", dproj_f, u, LANES, min(1024, D))
    dwin_t = lax.dynamic_update_slice(dwin_t, dwin_f[:H], (F_OFF, 0))
    dwin_g = dwin_t.reshape(N_DEV, n_in, D)
    du, (via_wout, sib_win) = _reduce_mm(
        "mix_bwd_in",
        [(dproj, pl.BlockSpec((tm2, tkb), lambda i, r: (i, r)), win_t, pl.BlockSpec((pl.Element(tkb), pl.Element(D)), lambda i, r: (main_row(r), 0)))],
        [(dproj_f, pl.BlockSpec((tm2, LANES), lambda i, r: (i, 0)), win_f, pl.BlockSpec((LANES, D), lambda i, r: (0, 0)))],
        NN, T, D, tm2, 9, carry=_join(_to_other_carry([t_wout]), _sibling_carry([dwin_g])))
    arrived(names[4:5], [via_wout])
    dh1, dg_mix, dh1h = _rmsnorm_bwd("mix_norm_bwd", du, h1, norm_mix_g, dh2, min(256, T), 0.5)
    (sum_win,) = pair_sums(names[3:4], [dwin_g], [sib_win])

    dwd1, hop1 = _wgrad_down("ffn1_wgrad_down", hm1, dh1h, D, carry=_to_partner_carry([sum_win]))
    (t_win,) = relay_sums(names[3:4], [sum_win], hop1)
    (da1, db1), (via_win, sib_d) = _ffn_bwd_mid("ffn1_bwd_mid", dh1h, wd1, a1, b1, tm2,
                                                carry=_join(_to_other_carry([t_win]), _sibling_carry([dwd1])))
    arrived(names[3:4], [via_win])
    (sum_wd1,) = pair_sums(names[2:3], [dwd1], [sib_d])
    dwg1, hop1 = _wgrad_down("ffn1_wgrad_gate", da1, n1, D, carry=_to_partner_carry([sum_wd1]))
    (t_wd1,) = relay_sums(names[2:3], [sum_wd1], hop1)
    dwu1, (via_wd1, sib_g) = _wgrad_down("ffn1_wgrad_up", db1, n1, D,
                                         carry=_join(_to_other_carry([t_wd1]), _sibling_carry([dwg1])))
    arrived(names[2:3], [via_wd1])
    (sum_wg1,) = pair_sums(names[0:1], [dwg1], [sib_g])
    dn1_gate, (*hop1, sib_u) = _reduce_mm(
        "ffn1_bwd_in_gate", [(da1, aspec2, wg1, wspec)], [], NN, T, D, tm2, J,
        carry=_join(_to_partner_carry([sum_wg1]), _sibling_carry([dwu1])))
    (t_wg1,) = relay_sums(names[0:1], [sum_wg1], hop1)
    (sum_wu1,) = pair_sums(names[1:2], [dwu1], [sib_u])
    dn1, (via_wg1, *hop1) = _reduce_mm(
        "ffn1_bwd_in_up", [(db1, aspec2, wu1, wspec)], [], NN, T, D, tm2, J, init=dn1_gate,
        carry=_join(_to_other_carry([t_wg1]), _to_partner_carry([sum_wu1])))
    arrived(names[0:1], [via_wg1])
    (t_wu1,) = relay_sums(names[1:2], [sum_wu1], hop1)
    arrived(names[1:2], _run_carry("grads_exchange", _to_other_carry([t_wu1])))
    dx, dg_ffn1 = _rmsnorm_bwd("ffn1_norm_bwd", dn1, xs, norm_ffn1_g, dh1, min(256, T), None)

    big_out = [_adam_shard("adam_" + nm, local[nm][0], local[nm][1], got[nm], w, m, v, own)
               for nm, w, m, v in zip(names, big_w, big_m, big_v)]

    dsinks = jnp.stack([dsink_a.reshape(HP), dsink_b.reshape(HP)], axis=1).reshape(H)
    small_g = [dg_ffn1, dg_mix, dg_ffn2, dg_of, dg_os, db_f, dg_fq[0, 0, :HEAD_DIM], dg_fk[0, 0, :HEAD_DIM],
               dg_sq[0, 0, :HEAD_DIM], dg_sk[0, 0, :HEAD_DIM], dsinks]
    small_w = [norm_ffn1_g, norm_mix_g, norm_ffn2_g, out_norm_fox_g, out_norm_swa_g, b_forget, fox_q_norm_g, fox_k_norm_g,
               swa_q_norm_g, swa_k_norm_g, swa_sinks]
    small_m = [m_norm_ffn1_g, m_norm_mix_g, m_norm_ffn2_g, m_out_norm_fox_g, m_out_norm_swa_g, m_b_forget, m_fox_q_norm_g,
               m_fox_k_norm_g, m_swa_q_norm_g, m_swa_k_norm_g, m_swa_sinks]
    small_v = [v_norm_ffn1_g, v_norm_mix_g, v_norm_ffn2_g, v_out_norm_fox_g, v_out_norm_swa_g, v_b_forget, v_fox_q_norm_g,
               v_fox_k_norm_g, v_swa_q_norm_g, v_swa_k_norm_g, v_swa_sinks]
    gathered = _gather_small(_pack_small(small_g, D, loss_part))
    small_out = _adam_small("adam_small", gathered, _pack_small(small_w, D), _pack_small(small_m, D), _pack_small(small_v, D))
    loss = small_out[0][5, 0]
    small_out = [_unpack_small(p, D, H) for p in small_out]

    order = ["norm_ffn1_g", "ffn1_w_gate", "ffn1_w_up", "ffn1_w_down", "norm_mix_g", "w_in", "b_forget", "fox_q_norm_g", "fox_k_norm_g",
             "swa_q_norm_g", "swa_k_norm_g", "swa_sinks", "out_norm_fox_g", "out_norm_swa_g", "w_out", "norm_ffn2_g",
             "ffn2_w_gate", "ffn2_w_up", "ffn2_w_down"]
    small_names = ["norm_ffn1_g", "norm_mix_g", "norm_ffn2_g", "out_norm_fox_g", "out_norm_swa_g", "b_forget", "fox_q_norm_g",
                   "fox_k_norm_g", "swa_q_norm_g", "swa_k_norm_g", "swa_sinks"]
    result = [loss, dx[None]]
    for kind in range(4):
        for nm in order:
            if nm in names:
                leaf = big_out[names.index(nm)][kind]
                result.append((tr(leaf) if nm in transposed else leaf)[None])
            else:
                result.append(small_out[kind][small_names.index(nm)])
    return tuple(result)


def _headnorm_fwd_scaled(name, proj, col_off, ncb, gains, tm, scale, n_scaled):
    T = proj.shape[0]

    def body(x_ref, g_ref, o_ref):
        xv = x_ref[...]
        lo = _lane_lo(xv.shape)
        y = xv * _head_rstd(xv, lo) * g_ref[...]
        y = y * jnp.where(pl.program_id(0) < n_scaled, scale, 1.0)
        o_ref[...] = y.astype(BF16)

    return pl.pallas_call(
        body, out_shape=jax.ShapeDtypeStruct((T, ncb * LANES), BF16), grid=(ncb, T // tm),
        in_specs=[pl.BlockSpec((tm, LANES), lambda c, i: (i, col_off + c)), pl.BlockSpec((None, 1, LANES), lambda c, i: (c, 0, 0))],
        out_specs=pl.BlockSpec((tm, LANES), lambda c, i: (i, c)), name=name, compiler_params=_params(2))(proj, gains)
```

```python
import jax
import jax.numpy as jnp
from jax import lax
from jax.experimental import pallas as pl
from jax.experimental.pallas import tpu as pltpu

F32 = jnp.float32
BF16 = jnp.bfloat16

HEAD_DIM = 64
LANES = 128
WINDOW = 128
GQA_GROUP = 4
EPS = 1e-6
ROPE_THETA = 10000.0
ADAM_LR = 0.001
ADAM_B1 = 0.9
ADAM_B2 = 0.999
ADAM_EPS = 1e-08
ADAM_WD = 0.01
ADAM_STEP = 10
N_DEV = 8
NEG = -1e30
VMEM_LIMIT_V7X = 48 * 1024 * 1024
ROW_TILE_CAP = 512
MESH = pl.DeviceIdType.MESH

NN = (((1,), (0,)), ((), ()))
NT = (((1,), (1,)), ((), ()))
TN = (((0,), (0,)), ((), ()))


def _dot(a, b, dims):
    return lax.dot_general(a, b, dims, preferred_element_type=F32)


def _params(n_axes):
    return pltpu.CompilerParams(dimension_semantics=("arbitrary",) * n_axes, vmem_limit_bytes=VMEM_LIMIT_V7X)


def _row_tile(rows, cap=ROW_TILE_CAP):
    best = None
    for t in range(16, min(rows, cap) + 1, 16):
        if rows % t == 0:
            best = t
    return best or rows


def _lane_lo(shape):
    return lax.broadcasted_iota(jnp.int32, shape, len(shape) - 1) < HEAD_DIM


def _keep(sel, x):
    return jnp.where(sel, x.astype(F32), 0.0).astype(BF16)


_HBM = pl.BlockSpec(memory_space=pltpu.HBM)


class _Carry:
    def __init__(self, inputs, out_shapes, scratch, start, finish, middle=None):
        self.inputs, self.out_shapes, self.scratch = list(inputs), list(out_shapes), list(scratch)
        self.start, self.finish, self.middle = start, finish, middle or (lambda ins, outs, scr: None)


def _join(*carries):
    def hook(which):
        def run(ins, outs, scr):
            i = o = s = 0
            for c in carries:
                ni, no, ns = len(c.inputs), len(c.out_shapes), len(c.scratch)
                getattr(c, which)(ins[i:i + ni], outs[o:o + no], scr[s:s + ns])
                i, o, s = i + ni, o + no, s + ns
        return run

    return _Carry([a for c in carries for a in c.inputs], [a for c in carries for a in c.out_shapes],
                  [a for c in carries for a in c.scratch], hook("start"), hook("finish"), hook("middle"))


def _call(body, *, name, grid, in_specs, out_specs, out_shape, args, scratch_shapes=(), carry=None):
    params = _params(len(grid))
    if carry is None:
        return pl.pallas_call(body, out_shape=out_shape, grid=grid, in_specs=list(in_specs), out_specs=out_specs,
                              scratch_shapes=list(scratch_shapes), name=name, compiler_params=params)(*args)
    single = not isinstance(out_shape, (tuple, list))
    shapes = (out_shape,) if single else tuple(out_shape)
    specs = (out_specs,) if single else tuple(out_specs)
    n_in, n_out, n_scr = len(args), len(shapes), len(scratch_shapes)
    c_in, c_out = len(carry.inputs), len(carry.out_shapes)

    def wrapped(*refs):
        ins, c_ins = refs[:n_in], refs[n_in:n_in + c_in]
        o0 = n_in + c_in
        outs, c_outs = refs[o0:o0 + n_out], refs[o0 + n_out:o0 + n_out + c_out]
        s0 = o0 + n_out + c_out
        scr, c_scr = refs[s0:s0 + n_scr], refs[s0 + n_scr:]
        step, total = pl.program_id(0), grid[0]
        for ax in range(1, len(grid)):
            step, total = step * grid[ax] + pl.program_id(ax), total * grid[ax]

        @pl.when(step == 0)
        def _():
            carry.start(c_ins, c_outs, c_scr)

        @pl.when(step == total // 2)
        def _():
            carry.middle(c_ins, c_outs, c_scr)

        body(*ins, *outs, *scr)

        @pl.when(step == total - 1)
        def _():
            carry.finish(c_ins, c_outs, c_scr)

    res = pl.pallas_call(
        wrapped, out_shape=shapes + tuple(carry.out_shapes), grid=grid, in_specs=list(in_specs) + [_HBM] * c_in,
        out_specs=specs + (_HBM,) * c_out, scratch_shapes=list(scratch_shapes) + carry.scratch, name=name,
        compiler_params=params)(*args, *carry.inputs)
    main = res[:n_out]
    return (main[0] if single else tuple(main)), tuple(res[n_out:])


def _rms_bwd(dn, x, g):
    r = lax.rsqrt(jnp.mean(x * x, axis=-1, keepdims=True) + EPS)
    xh = x * r
    dxh = dn * g
    dx = r * (dxh - xh * jnp.mean(dxh * xh, axis=-1, keepdims=True))
    return dx, jnp.sum(dn * xh, axis=0, keepdims=True)


def _rmsnorm_fwd(name, x, g, tm, carry=None):
    T, D = x.shape

    def body(x_ref, g_ref, o_ref):
        xf = x_ref[...]
        r = lax.rsqrt(jnp.mean(xf * xf, axis=-1, keepdims=True) + EPS)
        o_ref[...] = (xf * r * g_ref[...]).astype(BF16)

    return _call(
        body, name=name, grid=(T // tm,), out_shape=jax.ShapeDtypeStruct((T, D), BF16),
        in_specs=[pl.BlockSpec((tm, D), lambda i: (i, 0)), pl.BlockSpec((1, D), lambda i: (0, 0))],
        out_specs=pl.BlockSpec((tm, D), lambda i: (i, 0)), args=[x, g], carry=carry)


def _outnorm_fwd(name, o_fox, o_swa, g_fox, g_swa, tm):
    T, Dh = o_fox.shape

    def body(a_ref, b_ref, ga_ref, gb_ref, o_ref):
        for ref, g_ref, lo in ((a_ref, ga_ref, 0), (b_ref, gb_ref, Dh)):
            xf = ref[...]
            r = lax.rsqrt(jnp.mean(xf * xf, axis=-1, keepdims=True) + EPS)
            o_ref[:, lo:lo + Dh] = (xf * r * g_ref[...]).astype(BF16)

    row = pl.BlockSpec((tm, Dh), lambda i: (i, 0))
    gain = pl.BlockSpec((1, Dh), lambda i: (0, 0))
    return pl.pallas_call(
        body, out_shape=jax.ShapeDtypeStruct((T, 2 * Dh), BF16), grid=(T // tm,),
        in_specs=[row, row, gain, gain], out_specs=pl.BlockSpec((tm, 2 * Dh), lambda i: (i, 0)),
        name=name, compiler_params=_params(1))(o_fox, o_swa, g_fox, g_swa)


def _outnorm_bwd(name, dhb, wout, half, o, g, tm):
    T, D = dhb.shape
    Dh = o.shape[1]

    def body(a_ref, w_ref, o_ref, g_ref, do_ref, dg_ref):
        don = _dot(a_ref[...], w_ref[...], NT)
        dx, dg = _rms_bwd(don, o_ref[...], g_ref[...])
        do_ref[...] = dx.astype(BF16)

        @pl.when(pl.program_id(0) == 0)
        def _():
            dg_ref[...] = dg

        @pl.when(pl.program_id(0) > 0)
        def _():
            dg_ref[...] += dg

    return pl.pallas_call(
        body, out_shape=(jax.ShapeDtypeStruct((T, Dh), BF16), jax.ShapeDtypeStruct((1, Dh), F32)), grid=(T // tm,),
        in_specs=[pl.BlockSpec((tm, D), lambda i: (i, 0)), pl.BlockSpec((Dh, D), lambda i: (half, 0)),
                  pl.BlockSpec((tm, Dh), lambda i: (i, 0)), pl.BlockSpec((1, Dh), lambda i: (0, 0))],
        out_specs=(pl.BlockSpec((tm, Dh), lambda i: (i, 0)), pl.BlockSpec((1, Dh), lambda i: (0, 0))),
        name=name, compiler_params=_params(1))(dhb, wout, o, g)


def _mm(name, a, b, tm, tn, dims=NN, resid=None, carry=None, b_rows=None, bf16_copy=False):
    M, K = a.shape
    transposed = dims == NT
    N = b.shape[0] if transposed else b.shape[1]
    if b_rows is not None:
        N = b_rows[0] * tn

    def body(*refs):
        if bf16_copy:
            a_ref, b_ref, o_ref, ob_ref = refs
            res = _dot(a_ref[...], b_ref[...], dims)
            o_ref[...] = res
            ob_ref[...] = res.astype(BF16)
        elif resid is None:
            a_ref, b_ref, o_ref = refs
            o_ref[...] = _dot(a_ref[...], b_ref[...], dims)
        else:
            a_ref, b_ref, r_ref, o_ref = refs
            o_ref[...] = r_ref[...] + _dot(a_ref[...], b_ref[...], dims)

    ospec = pl.BlockSpec((tm, tn), lambda n, i: (i, n))
    bspec = pl.BlockSpec((tn, K), lambda n, i: (n, 0)) if transposed else pl.BlockSpec((K, tn), lambda n, i: (0, n))
    if b_rows is not None:
        bspec = pl.BlockSpec((pl.Element(tn), pl.Element(K)), lambda n, i: (b_rows[1](n), 0))
    in_specs = [pl.BlockSpec((tm, K), lambda n, i: (i, 0)), bspec]
    args = [a, b]
    if resid is not None:
        in_specs.append(ospec)
        args.append(resid)
    if bf16_copy:
        assert resid is None
        return _call(body, name=name, grid=(N // tn, M // tm), in_specs=in_specs, out_specs=(ospec, ospec),
                     out_shape=(jax.ShapeDtypeStruct((M, N), F32), jax.ShapeDtypeStruct((M, N), BF16)), args=args, carry=carry)
    return _call(body, name=name, grid=(N // tn, M // tm), in_specs=in_specs, out_specs=ospec,
                 out_shape=jax.ShapeDtypeStruct((M, N), F32), args=args, carry=carry)


def _wgrad_2d(name, a, b, tmm, tn, carry=None, out_rows=None):
    T, M = a.shape
    N = b.shape[1]

    def body(a_ref, b_ref, o_ref):
        o_ref[...] = _dot(a_ref[...], b_ref[...], TN).astype(BF16)

    out_spec = pl.BlockSpec((tmm, tn), lambda m, n: (m, n))
    if out_rows is not None:
        out_spec = pl.BlockSpec((pl.Element(tmm), pl.Element(tn)), lambda m, n: (out_rows[1](m), n * tn))
    return _call(
        body, name=name, grid=(M // tmm, N // tn), out_shape=jax.ShapeDtypeStruct((M if out_rows is None else out_rows[0], N), BF16),
        in_specs=[pl.BlockSpec((T, tmm), lambda m, n: (0, m)), pl.BlockSpec((T, tn), lambda m, n: (0, n))],
        out_specs=out_spec, args=[a, b], carry=carry)


def _wgrad_down(name, hm, df, tn, carry=None):
    J, T, Fs = hm.shape
    D = df.shape[1]

    def body(a_ref, b_ref, o_ref):
        o_ref[...] = _dot(a_ref[...], b_ref[...], TN).astype(BF16)

    return _call(
        body, name=name, grid=(J, D // tn), out_shape=jax.ShapeDtypeStruct((J, Fs, D), BF16),
        in_specs=[pl.BlockSpec((None, T, Fs), lambda j, n: (j, 0, 0)), pl.BlockSpec((T, tn), lambda j, n: (0, n))],
        out_specs=pl.BlockSpec((None, Fs, tn), lambda j, n: (j, 0, n)), args=[hm, df], carry=carry)


def _wgrad_up(name, n, da, db, tn, carry=None):
    T, D = n.shape
    J, _, Fs = da.shape

    def body(n_ref, da_ref, db_ref, og_ref, ou_ref):
        nv = n_ref[...]
        og_ref[...] = _dot(da_ref[...], nv, TN).astype(BF16)
        ou_ref[...] = _dot(db_ref[...], nv, TN).astype(BF16)

    act = pl.BlockSpec((None, T, Fs), lambda j, m: (j, 0, 0))
    out = pl.BlockSpec((None, Fs, tn), lambda j, m: (j, 0, m))
    shape = jax.ShapeDtypeStruct((J, Fs, D), BF16)
    return _call(
        body, name=name, grid=(J, D // tn), out_shape=(shape, shape),
        in_specs=[pl.BlockSpec((T, tn), lambda j, m: (0, m)), act, act], out_specs=(out, out),
        args=[n, da, db], carry=carry)


def _reduce_mm(name, pairs, once, dims, T, D, tm, steps, init=None, carry=None):
    n_pairs = len(pairs)
    n_once = len(once)
    n_mm = 2 * (n_pairs + n_once)

    def body(*refs):
        pr = refs[:2 * n_pairs]
        on = refs[2 * n_pairs:n_mm]
        o_ref = refs[-1]
        r = pl.program_id(1)

        @pl.when(r == 0)
        def _():
            o_ref[...] = jnp.zeros(o_ref.shape, F32) if init is None else refs[n_mm][...]

        for p in range(n_pairs):
            o_ref[...] += _dot(pr[2 * p][...], pr[2 * p + 1][...], dims)

        if n_once:
            @pl.when(r == steps - 1)
            def _():
                for p in range(n_once):
                    o_ref[...] += _dot(on[2 * p][...], on[2 * p + 1][...], dims)

    in_specs, args = [], []
    for a, a_spec, w, w_spec in list(pairs) + list(once):
        in_specs += [a_spec, w_spec]
        args += [a, w]
    row = pl.BlockSpec((tm, D), lambda i, r: (i, 0))
    if init is not None:
        in_specs.append(row)
        args.append(init)
    return _call(body, name=name, grid=(T // tm, steps), in_specs=in_specs, out_specs=row, out_shape=jax.ShapeDtypeStruct((T, D), F32),
                 args=args, carry=carry)


def _rmsnorm_bwd(name, dn, x, g, dh, tm, bf16_scale, carry=None):
    T, D = x.shape
    emit_bf16 = bf16_scale is not None

    def body(dn_ref, x_ref, g_ref, dh_ref, *outs):
        dxn, dg = _rms_bwd(dn_ref[...], x_ref[...], g_ref[...])
        dx = dh_ref[...] + dxn
        outs[0][...] = dx
        if emit_bf16:
            outs[2][...] = (bf16_scale * dx).astype(BF16)

        @pl.when(pl.program_id(0) == 0)
        def _():
            outs[1][...] = dg

        @pl.when(pl.program_id(0) > 0)
        def _():
            outs[1][...] += dg

    row = pl.BlockSpec((tm, D), lambda i: (i, 0))
    gain = pl.BlockSpec((1, D), lambda i: (0, 0))
    out_shape = [jax.ShapeDtypeStruct((T, D), F32), jax.ShapeDtypeStruct((1, D), F32)]
    out_specs = [row, gain]
    if emit_bf16:
        out_shape.append(jax.ShapeDtypeStruct((T, D), BF16))
        out_specs.append(row)
    return _call(body, name=name, grid=(T // tm,), in_specs=[row, row, gain, row], out_specs=tuple(out_specs),
                 out_shape=tuple(out_shape), args=[dn, x, g, dh], carry=carry)


def _loss_grad(name, y, target, tm):
    T, D = y.shape

    def body(y_ref, t_ref, dy_ref, dyh_ref, sq_ref):
        diff = y_ref[...] - t_ref[...]
        sq = jnp.sum(jnp.sum(diff * diff, axis=1, keepdims=True), axis=0, keepdims=True)
        dy = diff * (1.0 / D)
        dy_ref[...] = dy
        dyh_ref[...] = (0.5 * dy).astype(BF16)

        @pl.when(pl.program_id(0) == 0)
        def _():
            sq_ref[...] = sq

        @pl.when(pl.program_id(0) > 0)
        def _():
            sq_ref[...] += sq

    row = pl.BlockSpec((tm, D), lambda i: (i, 0))
    return pl.pallas_call(
        body, out_shape=(jax.ShapeDtypeStruct((T, D), F32), jax.ShapeDtypeStruct((T, D), BF16), jax.ShapeDtypeStruct((1, 1), F32)),
        grid=(T // tm,), in_specs=[row, row], out_specs=(row, row, pl.BlockSpec((1, 1), lambda i: (0, 0))),
        name=name, compiler_params=_params(1))(y, target)


def _ffn_up(name, n, wg, wu, tm, carry=None):
    T, D = n.shape
    J, Fs, _ = wg.shape

    def body(n_ref, wg_ref, wu_ref, a_ref, b_ref, h_ref):
        xv = n_ref[...]
        a = _dot(xv, wg_ref[...], NT)
        b = _dot(xv, wu_ref[...], NT)
        a_ref[...] = a.astype(BF16)
        b_ref[...] = b.astype(BF16)
        h_ref[...] = (a * jax.nn.sigmoid(a) * b).astype(BF16)

    act = jax.ShapeDtypeStruct((J, T, Fs), BF16)
    wspec = pl.BlockSpec((None, Fs, D), lambda j, i: (j, 0, 0))
    aspec = pl.BlockSpec((None, tm, Fs), lambda j, i: (j, i, 0))
    return _call(
        body, name=name, grid=(J, T // tm), out_shape=(act, act, act),
        in_specs=[pl.BlockSpec((tm, D), lambda j, i: (i, 0)), wspec, wspec], out_specs=(aspec, aspec, aspec),
        args=[n, wg, wu], carry=carry)


def _ffn_gate(name, n, wg, tm, carry=None):
    T, D = n.shape
    J, Fs, _ = wg.shape

    def body(n_ref, wg_ref, a_ref):
        a_ref[...] = _dot(n_ref[...], wg_ref[...], NT).astype(BF16)

    aspec = pl.BlockSpec((None, tm, Fs), lambda j, i: (j, i, 0))
    return _call(
        body, name=name, grid=(J, T // tm), out_shape=jax.ShapeDtypeStruct((J, T, Fs), BF16),
        in_specs=[pl.BlockSpec((tm, D), lambda j, i: (i, 0)), pl.BlockSpec((None, Fs, D), lambda j, i: (j, 0, 0))],
        out_specs=aspec, args=[n, wg], carry=carry)


def _ffn_up_only(name, n, wu, a, tm, carry=None):
    T, D = n.shape
    J, Fs, _ = wu.shape

    def body(n_ref, wu_ref, a_ref, b_ref, h_ref):
        b = _dot(n_ref[...], wu_ref[...], NT)
        a = a_ref[...].astype(F32)
        b_ref[...] = b.astype(BF16)
        h_ref[...] = (a * jax.nn.sigmoid(a) * b).astype(BF16)

    act = jax.ShapeDtypeStruct((J, T, Fs), BF16)
    aspec = pl.BlockSpec((None, tm, Fs), lambda j, i: (j, i, 0))
    return _call(
        body, name=name, grid=(J, T // tm), out_shape=(act, act),
        in_specs=[pl.BlockSpec((tm, D), lambda j, i: (i, 0)), pl.BlockSpec((None, Fs, D), lambda j, i: (j, 0, 0)), aspec],
        out_specs=(aspec, aspec), args=[n, wu, a], carry=carry)


def _ffn_down(name, hm, wd, resid, tm, tn, carry=None):
    J, T, Fs = hm.shape
    D = wd.shape[2]

    def body(h_ref, w_ref, r_ref, o_ref):
        @pl.when(pl.program_id(2) == 0)
        def _():
            o_ref[...] = r_ref[...]

        o_ref[...] += _dot(h_ref[...] * 0.5, w_ref[...], NN)

    tile = pl.BlockSpec((tm, tn), lambda i, n, j: (i, n))
    return _call(
        body, name=name, grid=(T // tm, D // tn, J), out_shape=jax.ShapeDtypeStruct((T, D), F32),
        in_specs=[pl.BlockSpec((None, tm, Fs), lambda i, n, j: (j, i, 0)), pl.BlockSpec((None, Fs, tn), lambda i, n, j: (j, 0, n)), tile],
        out_specs=tile, args=[hm, wd, resid], carry=carry)


def _ffn_bwd_mid(name, dfh, wd, a, b, tm, carry=None):
    T, D = dfh.shape
    J, Fs, _ = wd.shape

    def body(df_ref, w_ref, a_ref, b_ref, da_ref, db_ref):
        dhm = _dot(df_ref[...], w_ref[...], NT)
        av = a_ref[...].astype(F32)
        bv = b_ref[...].astype(F32)
        sg = jax.nn.sigmoid(av)
        da_ref[...] = (dhm * bv * (sg * (1.0 + av * (1.0 - sg)))).astype(BF16)
        db_ref[...] = (dhm * (av * sg)).astype(BF16)

    act = jax.ShapeDtypeStruct((J, T, Fs), BF16)
    aspec = pl.BlockSpec((None, tm, Fs), lambda j, i: (j, i, 0))
    return _call(
        body, name=name, grid=(J, T // tm), out_shape=(act, act),
        in_specs=[pl.BlockSpec((tm, D), lambda j, i: (i, 0)), pl.BlockSpec((None, Fs, D), lambda j, i: (j, 0, 0)), aspec, aspec],
        out_specs=(aspec, aspec), args=[dfh, wd, a, b], carry=carry)


def _rot_half(y, lane):
    first = (lane & (HEAD_DIM // 2)) == 0
    return jnp.where(first, pltpu.roll(y, LANES - HEAD_DIM // 2, 1), pltpu.roll(y, HEAD_DIM // 2, 1))


def _head_rstd(x, lo):
    sq = x * x
    ss_a = jnp.sum(jnp.where(lo, sq, 0.0), axis=-1, keepdims=True)
    ss_b = jnp.sum(jnp.where(lo, 0.0, sq), axis=-1, keepdims=True)
    return lax.rsqrt(jnp.where(lo, ss_a, ss_b) * (1.0 / HEAD_DIM) + EPS)


def _headnorm_fwd(name, proj, col_off, ncb, gains, tm, scale, rope=None, dup=False):
    T = proj.shape[0]
    with_rope = rope is not None
    width = 2 * LANES if dup else LANES

    def body(*refs):
        if with_rope:
            x_ref, g_ref, cos_ref, sin_ref, o_ref = refs
        else:
            x_ref, g_ref, o_ref = refs
        xv = x_ref[...]
        lane = lax.broadcasted_iota(jnp.int32, xv.shape, 1)
        lo = lane < HEAD_DIM
        y = xv * _head_rstd(xv, lo) * g_ref[...]
        if with_rope:
            y = y * cos_ref[...] + _rot_half(y, lane) * sin_ref[...]
        y = y * scale
        if dup:
            sw = pltpu.roll(y, HEAD_DIM, 1)
            o_ref[:, :LANES] = jnp.where(lo, y, sw).astype(BF16)
            o_ref[:, LANES:] = jnp.where(lo, sw, y).astype(BF16)
        else:
            o_ref[...] = y.astype(BF16)

    in_specs = [pl.BlockSpec((tm, LANES), lambda c, i: (i, col_off + c)), pl.BlockSpec((None, 1, LANES), lambda c, i: (c, 0, 0))]
    args = [proj, gains]
    if with_rope:
        tab = pl.BlockSpec((tm, LANES), lambda c, i: (i, 0))
        in_specs += [tab, tab]
        args += list(rope)
    return pl.pallas_call(
        body, out_shape=jax.ShapeDtypeStruct((T, ncb * width), BF16), grid=(ncb, T // tm),
        in_specs=in_specs, out_specs=pl.BlockSpec((tm, width), lambda c, i: (i, c)),
        name=name, compiler_params=_params(2))(*args)


def _headnorm_bwd(name, dy, proj, col_off, ncb, gains, group, tm, scale, rope=None, fold=False, norm=True, into=None):
    T = dy.shape[0]
    with_rope = rope is not None
    n_groups = ncb // group
    dy_width = 4 * LANES if fold else LANES

    def body(*refs):
        refs = list(refs)
        dy_ref = refs.pop(0)
        x_ref = refs.pop(0) if norm else None
        g_ref = refs.pop(0) if norm else None
        cos_ref = refs.pop(0) if with_rope else None
        sin_ref = refs.pop(0) if with_rope else None
        if into is not None:
            refs.pop(0)
        dx_ref = refs.pop(0)
        dg_ref = refs.pop(0) if norm else None
        c = pl.program_id(0)
        i = pl.program_id(1)
        d = dy_ref[...]
        lane = lax.broadcasted_iota(jnp.int32, (d.shape[0], LANES), 1)
        lo = lane < HEAD_DIM
        if fold:
            t0 = d[:, 0:LANES] + d[:, LANES:2 * LANES]
            t1 = d[:, 2 * LANES:3 * LANES] + d[:, 3 * LANES:4 * LANES]
            d = jnp.where(lo, t0 + pltpu.roll(t0, HEAD_DIM, 1), t1 + pltpu.roll(t1, HEAD_DIM, 1))
        d = d * scale
        if with_rope:
            d = d * cos_ref[...] + _rot_half(d * sin_ref[...], lane)
        if not norm:
            dx_ref[...] = d.astype(BF16)
            return
        xv = x_ref[...]
        gv = g_ref[...]
        r = _head_rstd(xv, lo)
        xh = xv * r
        dxh = d * gv
        pr = dxh * xh
        m_a = jnp.sum(jnp.where(lo, pr, 0.0), axis=-1, keepdims=True)
        m_b = jnp.sum(jnp.where(lo, 0.0, pr), axis=-1, keepdims=True)
        mean = jnp.where(lo, m_a, m_b) * (1.0 / HEAD_DIM)
        dx_ref[...] = (r * (dxh - xh * mean)).astype(BF16)
        dgp = jnp.sum(d * xh, axis=0, keepdims=True)
        dgp = dgp + pltpu.roll(dgp, HEAD_DIM, 1)
        first = jnp.logical_and(c % group == 0, i == 0)

        @pl.when(first)
        def _():
            dg_ref[...] = dgp

        @pl.when(jnp.logical_not(first))
        def _():
            dg_ref[...] += dgp

    in_specs = [pl.BlockSpec((tm, dy_width), lambda c, i: (i, c))]
    args = [dy]
    if norm:
        in_specs += [pl.BlockSpec((tm, LANES), lambda c, i: (i, col_off + c)), pl.BlockSpec((None, 1, LANES), lambda c, i: (c, 0, 0))]
        args += [proj, gains]
    if with_rope:
        tab = pl.BlockSpec((tm, LANES), lambda c, i: (i, 0))
        in_specs += [tab, tab]
        args += list(rope)
    out_shape = [jax.ShapeDtypeStruct((T, ncb * LANES), BF16)]
    out_off = 0
    aliases = {}
    if into is not None:
        buf, out_off = into
        assert buf.dtype == BF16 and buf.shape[0] == T and (out_off + ncb) * LANES <= buf.shape[1], buf.shape
        aliases = {len(args): 0}
        in_specs.append(_HBM)
        args.append(buf)
        out_shape = [jax.ShapeDtypeStruct(buf.shape, BF16)]
    out_specs = [pl.BlockSpec((tm, LANES), lambda c, i: (i, out_off + c))]
    if norm:
        out_shape.append(jax.ShapeDtypeStruct((n_groups, 1, LANES), F32))
        out_specs.append(pl.BlockSpec((None, 1, LANES), lambda c, i: (c // group, 0, 0)))
    res = pl.pallas_call(
        body, out_shape=tuple(out_shape), grid=(ncb, T // tm), in_specs=in_specs, out_specs=tuple(out_specs),
        input_output_aliases=aliases, name=name, compiler_params=_params(2))(*args)
    return res if norm else (res[0], None)


def _dot_exact(x, tri):
    hi = x.astype(BF16)
    r1 = x - hi.astype(F32)
    mid = r1.astype(BF16)
    lo = (r1 - mid.astype(F32)).astype(BF16)
    return _dot(hi, tri, NN) + _dot(mid, tri, NN) + _dot(lo, tri, NN)


def _forget_fwd(name, zt, bias):
    H, T = zt.shape
    blk = min(256, T)

    def body(z_ref, b_ref, c_ref, s_ref):
        z = z_ref[...] + b_ref[...]
        s_ref[...] = jax.nn.sigmoid(-z)
        lf = jnp.minimum(z, 0.0) - jnp.log(1.0 + jnp.exp(-jnp.abs(z)))
        tri = (lax.broadcasted_iota(jnp.int32, (blk, blk), 0) <= lax.broadcasted_iota(jnp.int32, (blk, blk), 1)).astype(BF16)
        carry = jnp.zeros((H, 1), F32)
        for bi in range(T // blk):
            xb = lf[:, bi * blk:(bi + 1) * blk]
            c_ref[:, bi * blk:(bi + 1) * blk] = _dot_exact(xb, tri) + carry
            carry = carry + jnp.sum(xb, axis=-1, keepdims=True)

    shape = jax.ShapeDtypeStruct((H, T), F32)
    full = pl.BlockSpec((H, T), lambda i: (0, 0))
    return pl.pallas_call(
        body, out_shape=(shape, shape), grid=(1,), in_specs=[full, pl.BlockSpec((H, 1), lambda i: (0, 0))],
        out_specs=(full, full), name=name, compiler_params=_params(1))(zt, bias)


def _forget_bwd(name, dct, drt, sgt):
    H, T = dct.shape
    blk = min(256, T)

    def body(dc_ref, dr_ref, s_ref, dz_ref, db_ref):
        dc = dc_ref[...] + dr_ref[...]
        tri = (lax.broadcasted_iota(jnp.int32, (blk, blk), 0) >= lax.broadcasted_iota(jnp.int32, (blk, blk), 1)).astype(BF16)
        carry = jnp.zeros((H, 1), F32)
        db = jnp.zeros((H, 1), F32)
        for bi in reversed(range(T // blk)):
            xb = dc[:, bi * blk:(bi + 1) * blk]
            dz = (_dot_exact(xb, tri) + carry) * s_ref[:, bi * blk:(bi + 1) * blk]
            dz_ref[:, bi * blk:(bi + 1) * blk] = dz
            db = db + jnp.sum(dz, axis=-1, keepdims=True)
            carry = carry + jnp.sum(xb, axis=-1, keepdims=True)
        db_ref[...] = db

    full = pl.BlockSpec((H, T), lambda i: (0, 0))
    return pl.pallas_call(
        body, out_shape=(jax.ShapeDtypeStruct((H, T), F32), jax.ShapeDtypeStruct((H, 1), F32)), grid=(1,),
        in_specs=[full, full, full], out_specs=(full, pl.BlockSpec((H, 1), lambda i: (0, 0))),
        name=name, compiler_params=_params(1))(dct, drt, sgt)


FOX_STRIP_FWD = 128
FOX_STRIP_BWD = 256


def _fox_fwd(name, qk, v, crow, tq, tk, strip, carry=None):
    T, Dh = v.shape
    HP = Dh // LANES
    nk = T // tk
    assert tk % tq == 0 and tq % strip == 0
    n_strips = tq // strip

    def body(q_ref, k_ref, v_ref, ra_ref, rb_ref, o_ref, la_ref, lb_ref, s_ref, p_ref, m_ref, l_ref, acc_ref):
        i = pl.program_id(1)
        q2 = q_ref[...]
        lo = _lane_lo((tq, LANES))
        q_st = jnp.concatenate([_keep(lo, q2), _keep(jnp.logical_not(lo), q2)], axis=0)
        r_refs = (ra_ref, rb_ref)
        m_ref[...] = jnp.full(m_ref.shape, NEG, F32)
        l_ref[...] = jnp.zeros(l_ref.shape, F32)
        acc_ref[...] = jnp.zeros(acc_ref.shape, F32)
        rel = lax.broadcasted_iota(jnp.int32, (strip, tk), 0) - lax.broadcasted_iota(jnp.int32, (strip, tk), 1)

        def chunk(kc, masked):
            start = pl.multiple_of(kc * tk, tk)
            kb = k_ref[pl.ds(start, tk), :]
            vb = v_ref[pl.ds(start, tk), :]
            s_ref[...] = _dot(q_st, kb, NT)
            for h in range(2):
                cs = r_refs[h][kc]
                for st in range(n_strips):
                    rows = pl.ds(h * tq + st * strip, strip)
                    s = s_ref[rows, :] - cs
                    if masked:
                        s = jnp.where(rel >= start - (i * tq + st * strip), s, NEG)
                    m_old = m_ref[rows, :]
                    mn = jnp.maximum(m_old, jnp.max(s, axis=-1, keepdims=True))
                    p = jnp.exp(s - mn)
                    alpha = jnp.exp(m_old - mn)
                    l_ref[rows, :] = alpha * l_ref[rows, :] + jnp.sum(p, axis=-1, keepdims=True)
                    m_ref[rows, :] = mn
                    p_ref[rows, :] = p.astype(BF16)
                    acc_ref[rows, :] = acc_ref[rows, :] * alpha
            acc_ref[...] += _dot(p_ref[...], vb, NN)

        n_full = (i * tq) // tk

        def full_chunk(kc, _):
            chunk(kc, False)
            return 0

        lax.fori_loop(0, n_full, full_chunk, 0)
        chunk(n_full, True)
        top, bot = pl.ds(0, tq), pl.ds(tq, tq)
        o_ref[...] = jnp.where(lo, acc_ref[top, :] / l_ref[top, :], acc_ref[bot, :] / l_ref[bot, :])
        la_ref[...] = m_ref[top, :] + jnp.log(l_ref[top, :])
        lb_ref[...] = m_ref[bot, :] + jnp.log(l_ref[bot, :])

    row = lambda off: pl.BlockSpec((None, nk, 1, tk), lambda h, i: (2 * h + off, 0, 0, 0))
    lse = jax.ShapeDtypeStruct((HP, T, 1), F32)
    lspec = pl.BlockSpec((None, tq, 1), lambda h, i: (h, i, 0))
    scratch = [pltpu.VMEM((2 * tq, tk), F32), pltpu.VMEM((2 * tq, tk), BF16), pltpu.VMEM((2 * tq, 1), F32),
               pltpu.VMEM((2 * tq, 1), F32), pltpu.VMEM((2 * tq, LANES), F32)]
    return _call(
        body, name=name, grid=(HP, T // tq), out_shape=(jax.ShapeDtypeStruct((T, Dh), F32), lse, lse),
        in_specs=[pl.BlockSpec((tq, LANES), lambda h, i: (i, h)), pl.BlockSpec((T, LANES), lambda h, i: (0, HP + h)),
                  pl.BlockSpec((T, LANES), lambda h, i: (0, h)), row(0), row(1)],
        out_specs=(pl.BlockSpec((tq, LANES), lambda h, i: (i, h)), lspec, lspec),
        args=[qk, qk, v, crow, crow], scratch_shapes=scratch, carry=carry)


def _fox_bwd(name, qk, v, o, do, crow, lse_a, lse_b, tq, tk, strip, dv_cols, dv_off, carry=None):
    T, Dh = v.shape
    HP = Dh // LANES
    nk = T // tk
    scale = HEAD_DIM ** -0.5
    assert tk % tq == 0 and tq % strip == 0
    n_strips = tq // strip

    def body(q_ref, k_ref, v_ref, o_ref, do_ref, ra_ref, rb_ref, la_ref, lb_ref,
             dq_ref, dk_ref, dv_ref, dca_ref, dcb_ref, dra_ref, drb_ref, s_ref, dp_ref, p_ref, ds_ref, dq_acc, dsum_ref, dv_acc):
        i = pl.program_id(1)

        @pl.when(i == 0)
        def _():
            dk_ref[...] = jnp.zeros_like(dk_ref)
            dv_acc[...] = jnp.zeros(dv_acc.shape, F32)
            dca_ref[...] = jnp.zeros_like(dca_ref)
            dcb_ref[...] = jnp.zeros_like(dcb_ref)

        q2 = q_ref[...]
        do2 = do_ref[...]
        lo = _lane_lo((tq, LANES))
        hi = jnp.logical_not(lo)
        q_st = jnp.concatenate([_keep(lo, q2), _keep(hi, q2)], axis=0)
        do_st = jnp.concatenate([_keep(lo, do2), _keep(hi, do2)], axis=0)
        prod = do2.astype(F32) * o_ref[...]
        dsum_ref[pl.ds(0, tq), :] = jnp.sum(jnp.where(lo, prod, 0.0), axis=-1, keepdims=True)
        dsum_ref[pl.ds(tq, tq), :] = jnp.sum(jnp.where(lo, 0.0, prod), axis=-1, keepdims=True)
        r_refs, l_refs, dc_refs, dr_refs = (ra_ref, rb_ref), (la_ref, lb_ref), (dca_ref, dcb_ref), (dra_ref, drb_ref)
        dq_acc[...] = jnp.zeros(dq_acc.shape, F32)
        dra_ref[...] = jnp.zeros(dra_ref.shape, F32)
        drb_ref[...] = jnp.zeros(drb_ref.shape, F32)
        rel = lax.broadcasted_iota(jnp.int32, (strip, tk), 0) - lax.broadcasted_iota(jnp.int32, (strip, tk), 1)

        def chunk(kc, masked):
            start = pl.multiple_of(kc * tk, tk)
            kb = k_ref[pl.ds(start, tk), :]
            vb = v_ref[pl.ds(start, tk), :]
            s_ref[...] = _dot(q_st, kb, NT)
            dp_ref[...] = _dot(do_st, vb, NT)
            for h in range(2):
                cs = r_refs[h][kc]
                col_sum = jnp.zeros((1, tk), F32)
                for st in range(n_strips):
                    rows = pl.ds(st * strip, strip)
                    both = pl.ds(h * tq + st * strip, strip)
                    s = s_ref[both, :] - cs
                    if masked:
                        s = jnp.where(rel >= start - (i * tq + st * strip), s, NEG)
                    p = jnp.exp(s - l_refs[h][rows, :])
                    ds = p * (dp_ref[both, :] - dsum_ref[both, :])
                    p_ref[both, :] = p.astype(BF16)
                    ds_ref[both, :] = ds.astype(BF16)
                    col_sum = col_sum + jnp.sum(ds, axis=0, keepdims=True)
                    dr_refs[h][rows, :] += jnp.sum(ds, axis=-1, keepdims=True)
                dc_refs[h][kc] = dc_refs[h][kc] - col_sum
            dk_ref[pl.ds(start, tk), :] += _dot(ds_ref[...], q_st, TN)
            dv_acc[pl.ds(start, tk), :] += _dot(p_ref[...], do_st, TN)
            dq_acc[...] += _dot(ds_ref[...], kb, NN)

        n_full = (i * tq) // tk

        def full_chunk(kc, _):
            chunk(kc, False)
            return 0

        lax.fori_loop(0, n_full, full_chunk, 0)
        chunk(n_full, True)
        dq_ref[...] = jnp.where(lo, dq_acc[pl.ds(0, tq), :], dq_acc[pl.ds(tq, tq), :]) * scale

        @pl.when(i == pl.num_programs(1) - 1)
        def _():
            dv_ref[...] = dv_acc[...].astype(BF16)

    row = lambda off: pl.BlockSpec((None, nk, 1, tk), lambda h, i: (2 * h + off, 0, 0, 0))
    lspec = pl.BlockSpec((None, tq, 1), lambda h, i: (h, i, 0))
    qspec = pl.BlockSpec((tq, LANES), lambda h, i: (i, h))
    full = pl.BlockSpec((T, LANES), lambda h, i: (0, h))
    dcspec = pl.BlockSpec((None, nk, 1, tk), lambda h, i: (h, 0, 0, 0))
    grad = jax.ShapeDtypeStruct((T, Dh), F32)
    dc = jax.ShapeDtypeStruct((HP, nk, 1, tk), F32)
    dr = jax.ShapeDtypeStruct((HP, T, 1), F32)
    scratch = [pltpu.VMEM((2 * tq, tk), F32), pltpu.VMEM((2 * tq, tk), F32), pltpu.VMEM((2 * tq, tk), BF16), pltpu.VMEM((2 * tq, tk), BF16),
               pltpu.VMEM((2 * tq, LANES), F32), pltpu.VMEM((2 * tq, 1), F32), pltpu.VMEM((T, LANES), F32)]
    assert (dv_off + HP) * LANES <= dv_cols
    dv = jax.ShapeDtypeStruct((T, dv_cols), BF16)
    dvspec = pl.BlockSpec((T, LANES), lambda h, i: (0, dv_off + h))
    return _call(
        body, name=name, grid=(HP, T // tq), out_shape=(grad, grad, dv, dc, dc, dr, dr),
        in_specs=[qspec, pl.BlockSpec((T, LANES), lambda h, i: (0, HP + h)), full, qspec, qspec, row(0), row(1), lspec, lspec],
        out_specs=(qspec, full, dvspec, dcspec, dcspec, lspec, lspec),
        args=[qk, qk, v, o, do, crow, crow, lse_a, lse_b], scratch_shapes=scratch, carry=carry)


SWA_GROUP = 2
SWA_GROUP_BWD = 4


def _swa_block(n, q_ref, k_ref):
    qs = pl.multiple_of(n * WINDOW, WINDOW)
    ks = pl.multiple_of(jnp.maximum(n - 1, 0) * WINDOW, WINDOW)
    rel = (qs + lax.broadcasted_iota(jnp.int32, (WINDOW, 2 * WINDOW), 0)) - (ks + lax.broadcasted_iota(jnp.int32, (WINDOW, 2 * WINDOW), 1))
    valid = jnp.logical_and(rel >= 0, rel < WINDOW)
    return qs, ks, valid


def _swa_fwd(name, q, kd, vd, sinks, carry=None):
    T, Dh = q.shape
    HP = Dh // LANES

    def body(q_ref, k_ref, v_ref, sa_ref, sb_ref, o_ref, la_ref, lb_ref):
        lo = _lane_lo((WINDOW, LANES))

        top = lax.broadcasted_iota(jnp.int32, (2 * WINDOW, 1), 0) < WINDOW
        sink = jnp.where(top, sa_ref[...], sb_ref[...])

        def block(n, _):
            qs, ks, valid = _swa_block(n, q_ref, k_ref)
            q2 = q_ref[pl.ds(qs, WINDOW), :]
            kb = k_ref[pl.ds(ks, 2 * WINDOW), :]
            vb = v_ref[pl.ds(ks, 2 * WINDOW), :]
            q_st = jnp.concatenate([_keep(lo, q2), _keep(jnp.logical_not(lo), q2)], axis=0)
            s = jnp.where(jnp.concatenate([valid, valid], axis=0), _dot(q_st, kb, NT), NEG)
            m = jnp.maximum(jnp.max(s, axis=-1, keepdims=True), sink)
            p = jnp.exp(s - m)
            l = jnp.sum(p, axis=-1, keepdims=True) + jnp.exp(sink - m)
            o2 = _dot(p.astype(BF16), vb, NN) / l
            lse = m + jnp.log(l)
            o_ref[pl.ds(qs, WINDOW), :] = jnp.where(lo, o2[:WINDOW], o2[WINDOW:])
            la_ref[pl.ds(qs, WINDOW), :] = lse[:WINDOW]
            lb_ref[pl.ds(qs, WINDOW), :] = lse[WINDOW:]
            return 0

        assert (T // WINDOW) % SWA_GROUP == 0

        def group(g, c):
            for b in range(SWA_GROUP):
                c = block(g * SWA_GROUP + b, c)
            return c

        lax.fori_loop(0, T // WINDOW // SWA_GROUP, group, 0)

    full = pl.BlockSpec((T, LANES), lambda h: (0, h))
    kv = pl.BlockSpec((T, LANES), lambda h: (0, h // 2))
    sink = lambda off: pl.BlockSpec((None, 1, 1), lambda h: (2 * h + off, 0, 0))
    lse = jax.ShapeDtypeStruct((HP, T, 1), F32)
    lspec = pl.BlockSpec((None, T, 1), lambda h: (h, 0, 0))
    return _call(
        body, name=name, grid=(HP,), out_shape=(jax.ShapeDtypeStruct((T, Dh), F32), lse, lse),
        in_specs=[full, kv, kv, sink(0), sink(1)], out_specs=(full, lspec, lspec),
        args=[q, kd, vd, sinks, sinks], carry=carry)


def _swa_bwd(name, q, kd, vd, sinks, o, do, lse_a, lse_b, carry=None):
    T, Dh = q.shape
    HP = Dh // LANES
    scale = HEAD_DIM ** -0.5

    def body(q_ref, k_ref, v_ref, sa_ref, sb_ref, o_ref, do_ref, la_ref, lb_ref, dq_ref, dk_ref, dv_ref, dsa_ref, dsb_ref):
        lo = _lane_lo((WINDOW, LANES))
        hi = jnp.logical_not(lo)
        dk_ref[...] = jnp.zeros_like(dk_ref)
        dv_ref[...] = jnp.zeros_like(dv_ref)

        top = lax.broadcasted_iota(jnp.int32, (2 * WINDOW, 1), 0) < WINDOW
        sink = jnp.where(top, sa_ref[...], sb_ref[...])

        def block(n, dsinks):
            qs, ks, valid = _swa_block(n, q_ref, k_ref)
            rows = pl.ds(qs, WINDOW)
            q2 = q_ref[rows, :]
            do2 = do_ref[rows, :]
            kb = k_ref[pl.ds(ks, 2 * WINDOW), :]
            vb = v_ref[pl.ds(ks, 2 * WINDOW), :]
            prod = do2.astype(F32) * o_ref[rows, :]
            q_st = jnp.concatenate([_keep(lo, q2), _keep(hi, q2)], axis=0)
            do_st = jnp.concatenate([_keep(lo, do2), _keep(hi, do2)], axis=0)
            dsum = jnp.concatenate([jnp.sum(jnp.where(lo, prod, 0.0), axis=-1, keepdims=True),
                                    jnp.sum(jnp.where(lo, 0.0, prod), axis=-1, keepdims=True)], axis=0)
            lse = jnp.concatenate([la_ref[rows, :], lb_ref[rows, :]], axis=0)
            s = jnp.where(jnp.concatenate([valid, valid], axis=0), _dot(q_st, kb, NT), NEG)
            p = jnp.exp(s - lse)
            ds = p * (_dot(do_st, vb, NT) - dsum)
            dsb = ds.astype(BF16)
            dq2 = _dot(dsb, kb, NN)
            dq_ref[rows, :] = jnp.where(lo, dq2[:WINDOW], dq2[WINDOW:]) * scale
            dk_ref[pl.ds(ks, 2 * WINDOW), :] += _dot(dsb, q_st, TN)
            dv_ref[pl.ds(ks, 2 * WINDOW), :] += _dot(p.astype(BF16), do_st, TN)
            gone = jnp.exp(sink - lse) * dsum
            return (dsinks[0] - jnp.sum(gone[:WINDOW], axis=0, keepdims=True),
                    dsinks[1] - jnp.sum(gone[WINDOW:], axis=0, keepdims=True))

        assert (T // WINDOW) % SWA_GROUP_BWD == 0

        def group(g, c):
            for b in range(SWA_GROUP_BWD):
                c = block(g * SWA_GROUP_BWD + b, c)
            return c

        dsa, dsb_ = lax.fori_loop(0, T // WINDOW // SWA_GROUP_BWD, group, (jnp.zeros((1, 1), F32), jnp.zeros((1, 1), F32)))
        dsa_ref[...] = dsa
        dsb_ref[...] = dsb_

    full = pl.BlockSpec((T, LANES), lambda h: (0, h))
    kv = pl.BlockSpec((T, LANES), lambda h: (0, h // 2))
    sink = lambda off: pl.BlockSpec((None, 1, 1), lambda h: (2 * h + off, 0, 0))
    lspec = pl.BlockSpec((None, T, 1), lambda h: (h, 0, 0))
    dsink = pl.BlockSpec((None, 1, 1), lambda h: (h, 0, 0))
    grad = jax.ShapeDtypeStruct((T, Dh), F32)
    ds_shape = jax.ShapeDtypeStruct((HP, 1, 1), F32)
    return _call(
        body, name=name, grid=(HP,), out_shape=(grad, grad, grad, ds_shape, ds_shape),
        in_specs=[full, kv, kv, sink(0), sink(1), full, full, lspec, lspec],
        out_specs=(full, full, full, dsink, dsink),
        args=[q, kd, vd, sinks, sinks, o, do, lse_a, lse_b], carry=carry)


def _place():
    return lax.axis_index("x"), lax.axis_index("y"), lax.axis_index("c")


def _run_carry(name, carry):
    c_in, c_out = len(carry.inputs), len(carry.out_shapes)

    def body(*refs):
        ins, outs, scr = refs[:c_in], refs[c_in:c_in + c_out], refs[c_in + c_out:]
        carry.start(ins, outs, scr)
        carry.middle(ins, outs, scr)
        carry.finish(ins, outs, scr)

    return pl.pallas_call(
        body, out_shape=tuple(carry.out_shapes), in_specs=[_HBM] * c_in, out_specs=tuple([_HBM] * c_out),
        scratch_shapes=carry.scratch, name=name)(*carry.inputs)


def _gather_carry(shards):
    n = len(shards)

    def plan(ins, outs, scr):
        send, recv, local = scr
        x, y, c = _place()
        me, sibling = (x, y, c), (x, y, 1 - c)
        partner, other, diag = (x ^ c, y ^ (1 - c)), (x ^ (1 - c), y ^ c), (1 - x, 1 - y)

        def copy(w, k, block, to, src=None):
            slot = 4 * block[0] + 2 * block[1] + block[2]
            return pltpu.make_async_remote_copy(
                src_ref=outs[w].at[slot] if src is None else src, dst_ref=outs[w].at[slot],
                send_sem=send.at[w, k], recv_sem=recv.at[w, k], device_id=to, device_id_type=MESH)

        def own():
            return [pltpu.make_async_copy(ins[w], outs[w].at[4 * x + 2 * y + c], local.at[w]) for w in range(n)]

        return copy, own, me, sibling, partner, other, diag, c

    def start(ins, outs, scr):
        copy, own, me, sibling, partner, other, _, c = plan(ins, outs, scr)
        for cp in own():
            cp.start()
        for w in range(n):
            copy(w, 1, me, (*partner, c), src=ins[w]).start()
            copy(w, 2, me, (*other, c), src=ins[w]).start()
            copy(w, 0, me, sibling, src=ins[w]).start()

    def middle(ins, outs, scr):
        copy, _, me, sibling, partner, other, _, c = plan(ins, outs, scr)
        for w in range(n):
            copy(w, 1, (*partner, c), me).wait_recv()
            copy(w, 3, (*partner, c), (*other, c)).start()
            copy(w, 4, (*partner, c), sibling).start()
        for w in range(n):
            copy(w, 2, (*other, c), me).wait_recv()
            copy(w, 5, (*other, c), sibling).start()

    def finish(ins, outs, scr):
        copy, own, me, sibling, partner, other, diag, c = plan(ins, outs, scr)
        for w in range(n):
            copy(w, 3, (*diag, c), me).wait_recv()
            copy(w, 6, (*diag, c), sibling).start()
        for w in range(n):
            copy(w, 0, sibling, me).wait_recv()
            copy(w, 4, (*other, 1 - c), me).wait_recv()
            copy(w, 5, (*partner, 1 - c), me).wait_recv()
            copy(w, 6, (*diag, 1 - c), me).wait_recv()
        for w in range(n):
            sent = [copy(w, 0, me, sibling, src=ins[w]), copy(w, 1, me, (*partner, c), src=ins[w]), copy(w, 2, me, (*other, c), src=ins[w]),
                    copy(w, 3, (*partner, c), (*other, c)), copy(w, 4, (*partner, c), sibling), copy(w, 5, (*other, c), sibling),
                    copy(w, 6, (*diag, c), sibling)]
            for cp in sent:
                cp.wait_send()
        for cp in own():
            cp.wait()

    return _Carry(shards, [jax.ShapeDtypeStruct((N_DEV,) + s.shape, s.dtype) for s in shards],
                  [pltpu.SemaphoreType.DMA((n, 7)), pltpu.SemaphoreType.DMA((n, 7)), pltpu.SemaphoreType.DMA((n,))], start, finish, middle)


def _sibling_carry(grads):
    n = len(grads)

    def copies(ins, outs, scr):
        send, recv = scr
        x, y, c = _place()
        return [pltpu.make_async_remote_copy(
            src_ref=ins[w].at[2 * q + (1 - c)], dst_ref=outs[w].at[q], send_sem=send.at[w, q], recv_sem=recv.at[w, q],
            device_id=(x, y, 1 - c), device_id_type=MESH) for w in range(n) for q in range(4)]

    def start(ins, outs, scr):
        for cp in copies(ins, outs, scr):
            cp.start()

    def finish(ins, outs, scr):
        for cp in copies(ins, outs, scr):
            cp.wait()

    return _Carry(grads, [jax.ShapeDtypeStruct((4,) + g.shape[1:], g.dtype) for g in grads],
                  [pltpu.SemaphoreType.DMA((n, 4)), pltpu.SemaphoreType.DMA((n, 4))], start, finish)


def _to_partner_carry(sums):
    n = len(sums)

    def copies(ins, outs, scr):
        send, recv = scr
        x, y, c = _place()
        return [pltpu.make_async_remote_copy(
            src_ref=ins[w].at[k], dst_ref=outs[2 * w + k], send_sem=send.at[w, k], recv_sem=recv.at[w, k],
            device_id=(x ^ c, y ^ (1 - c), c), device_id_type=MESH) for w in range(n) for k in range(2)]

    def start(ins, outs, scr):
        for cp in copies(ins, outs, scr):
            cp.start()

    def finish(ins, outs, scr):
        for cp in copies(ins, outs, scr):
            cp.wait()

    return _Carry(sums, [jax.ShapeDtypeStruct(s.shape[1:], s.dtype) for s in sums for _ in range(2)],
                  [pltpu.SemaphoreType.DMA((n, 2)), pltpu.SemaphoreType.DMA((n, 2))], start, finish)


def _to_other_carry(blocks):
    n = len(blocks)

    def copies(ins, outs, scr):
        send, recv = scr
        x, y, c = _place()
        return [pltpu.make_async_remote_copy(
            src_ref=ins[w], dst_ref=outs[w], send_sem=send.at[w], recv_sem=recv.at[w],
            device_id=(x ^ (1 - c), y ^ c, c), device_id_type=MESH) for w in range(n)]

    def start(ins, outs, scr):
        for cp in copies(ins, outs, scr):
            cp.start()

    def finish(ins, outs, scr):
        for cp in copies(ins, outs, scr):
            cp.wait()

    return _Carry(blocks, [jax.ShapeDtypeStruct(b.shape, b.dtype) for b in blocks],
                  [pltpu.SemaphoreType.DMA((n,)), pltpu.SemaphoreType.DMA((n,))], start, finish)


def _gather_small_carry(packed):
    R, C = packed.shape

    def copies(ins, outs, scr):
        send, recv, local = scr
        x, y, c = _place()
        mine = 4 * x + 2 * y + c
        cps = [pltpu.make_async_copy(ins[0], outs[0].at[mine], local.at[0])]
        for k in range(1, N_DEV):
            peer = (x ^ (k >> 2), y ^ ((k >> 1) & 1), c ^ (k & 1))
            cps.append(pltpu.make_async_remote_copy(
                src_ref=ins[0], dst_ref=outs[0].at[mine], send_sem=send.at[k - 1], recv_sem=recv.at[k - 1],
                device_id=peer, device_id_type=MESH))
        return cps

    def start(ins, outs, scr):
        for cp in copies(ins, outs, scr):
            cp.start()

    def finish(ins, outs, scr):
        for cp in copies(ins, outs, scr):
            cp.wait()

    return _Carry([packed], [jax.ShapeDtypeStruct((N_DEV, R, C), F32)],
                  [pltpu.SemaphoreType.DMA((N_DEV - 1,)), pltpu.SemaphoreType.DMA((N_DEV - 1,)), pltpu.SemaphoreType.DMA((1,))],
                  start, finish)


def _adamw(w, g, m, v):
    m = ADAM_B1 * m + (1.0 - ADAM_B1) * g
    v = ADAM_B2 * v + (1.0 - ADAM_B2) * (g * g)
    m_hat = m / (1.0 - ADAM_B1 ** ADAM_STEP)
    v_hat = v / (1.0 - ADAM_B2 ** ADAM_STEP)
    delta = -ADAM_LR * (m_hat / (jnp.sqrt(v_hat) + ADAM_EPS) + ADAM_WD * w)
    return delta, m, v


def _pair_add(name, grads, received, slots):
    _, R, C = grads.shape
    tr = _row_tile(R, 2 * ROW_TILE_CAP)

    def body(s_ref, g_ref, r_ref, o_ref):
        o_ref[...] = (g_ref[...].astype(F32) + r_ref[...].astype(F32)).astype(BF16)

    return pl.pallas_call(
        body, out_shape=jax.ShapeDtypeStruct((3, R, C), BF16),
        grid_spec=pltpu.PrefetchScalarGridSpec(
            num_scalar_prefetch=1, grid=(3, R // tr),
            in_specs=[pl.BlockSpec((None, tr, C), lambda k, i, s: (s[k], i, 0)), pl.BlockSpec((None, tr, C), lambda k, i, s: (s[3 + k], i, 0))],
            out_specs=pl.BlockSpec((None, tr, C), lambda k, i, s: (k, i, 0))),
        name=name, compiler_params=_params(2))(slots, grads, received)


def _relay_add(name, sums, relayed):
    _, R, C = sums.shape
    tr = _row_tile(R, 2 * ROW_TILE_CAP)

    def body(s_ref, r_ref, o_ref):
        o_ref[...] = (s_ref[...].astype(F32) + r_ref[...].astype(F32)).astype(BF16)

    blk = pl.BlockSpec((tr, C), lambda i: (i, 0))
    return pl.pallas_call(
        body, out_shape=jax.ShapeDtypeStruct((R, C), BF16), grid=(R // tr,),
        in_specs=[pl.BlockSpec((None, tr, C), lambda i: (2, i, 0)), blk], out_specs=blk,
        name=name, compiler_params=_params(1))(sums, relayed)


def _adam_shard(name, grads, from_sibling, received, w, m, v, own):
    R, C = w.shape
    tr = _row_tile(R, 256)
    tc = C if tr < R or C % (4 * LANES) else 4 * LANES

    def body(o_ref, g_ref, s_ref, ra_ref, rb_ref, w_ref, m_ref, v_ref, g_out, d_out, m_out, v_out):
        g = (g_ref[...].astype(F32) + s_ref[...].astype(F32)) + ra_ref[...].astype(F32) + rb_ref[...].astype(F32)
        delta, mn, vn = _adamw(w_ref[...], g, m_ref[...], v_ref[...])
        g_out[...] = g
        d_out[...] = delta
        m_out[...] = mn
        v_out[...] = vn

    blk = pl.BlockSpec((tr, tc), lambda i, j, o: (i, j))
    shape = jax.ShapeDtypeStruct((R, C), F32)
    return pl.pallas_call(
        body, out_shape=(shape,) * 4,
        grid_spec=pltpu.PrefetchScalarGridSpec(
            num_scalar_prefetch=1, grid=(R // tr, C // tc),
            in_specs=[pl.BlockSpec((None, tr, tc), lambda i, j, o: (o[0], i, j)), pl.BlockSpec((None, tr, tc), lambda i, j, o: (o[1], i, j)),
                      blk, blk, blk, blk, blk],
            out_specs=(blk,) * 4),
        name=name, compiler_params=_params(2))(own, grads, from_sibling, received[0], received[1], w, m, v)


def _adam_small(name, gathered, w, m, v):
    R, C = w.shape

    def body(ga_ref, w_ref, m_ref, v_ref, g_out, d_out, m_out, v_out):
        g = ga_ref[0]
        for d in range(1, N_DEV):
            g = g + ga_ref[d]
        delta, mn, vn = _adamw(w_ref[...], g, m_ref[...], v_ref[...])
        g_out[...] = g
        d_out[...] = delta
        m_out[...] = mn
        v_out[...] = vn

    full = pl.BlockSpec((R, C), lambda i: (0, 0))
    shape = jax.ShapeDtypeStruct((R, C), F32)
    return pl.pallas_call(
        body, out_shape=(shape,) * 4, grid=(1,),
        in_specs=[pl.BlockSpec((N_DEV, R, C), lambda i: (0, 0, 0)), full, full, full], out_specs=(full,) * 4,
        name=name, compiler_params=_params(1))(gathered, w, m, v)


def _pack_small(parts, D, scalar=None):
    g1, gmix, g2, gof, gos, bf, gqf, gkf, gqs, gks, sinks = [p.reshape(-1).astype(F32) for p in parts]
    row3 = jnp.concatenate([gof, gos])
    row4 = jnp.zeros((D,), F32)
    for slot, vec in enumerate((bf, gqf, gkf, gqs, gks, sinks)):
        row4 = lax.dynamic_update_slice(row4, vec, (slot * LANES,))
    zero = jnp.zeros((D,), F32)
    row5 = zero if scalar is None else lax.dynamic_update_slice(zero, jnp.reshape(scalar, (1,)).astype(F32), (0,))
    return jnp.stack([g1, gmix, g2, row3, row4, row5, zero, zero])


def _unpack_small(packed, D, H):
    Dh = D // 2
    row4 = packed[4]
    short = [row4[s * LANES:s * LANES + n] for s, n in enumerate((H, HEAD_DIM, HEAD_DIM, HEAD_DIM, HEAD_DIM, H))]
    vecs = [packed[0], packed[1], packed[2], packed[3, :Dh], packed[3, Dh:]] + short
    return [v[None, :] for v in vecs]


def kernel(x, positions, norm_ffn1_g, ffn1_w_gate, ffn1_w_up, ffn1_w_down, norm_mix_g, w_in, b_forget, fox_q_norm_g, fox_k_norm_g, swa_q_norm_g, swa_k_norm_g, swa_sinks, out_norm_fox_g, out_norm_swa_g, w_out, norm_ffn2_g, ffn2_w_gate, ffn2_w_up, ffn2_w_down, loss_target, m_norm_ffn1_g, m_ffn1_w_gate, m_ffn1_w_up, m_ffn1_w_down, m_norm_mix_g, m_w_in, m_b_forget, m_fox_q_norm_g, m_fox_k_norm_g, m_swa_q_norm_g, m_swa_k_norm_g, m_swa_sinks, m_out_norm_fox_g, m_out_norm_swa_g, m_w_out, m_norm_ffn2_g, m_ffn2_w_gate, m_ffn2_w_up, m_ffn2_w_down, v_norm_ffn1_g, v_ffn1_w_gate, v_ffn1_w_up, v_ffn1_w_down, v_norm_mix_g, v_w_in, v_b_forget, v_fox_q_norm_g, v_fox_k_norm_g, v_swa_q_norm_g, v_swa_k_norm_g, v_swa_sinks, v_out_norm_fox_g, v_out_norm_swa_g, v_w_out, v_norm_ffn2_g, v_ffn2_w_gate, v_ffn2_w_up, v_ffn2_w_down):
    xs = x[0]
    target = loss_target[0]
    T, D = xs.shape
    Dh = D // 2
    H = Dh // HEAD_DIM
    HP = H // 2
    KVW = (H // GQA_GROUP) * HEAD_DIM
    KVB = KVW // LANES
    MAIN = 4 * Dh + 2 * KVW
    F_OFF = 3 * Dh
    tm = min(ROW_TILE_CAP, T)
    tm2 = min(2 * ROW_TILE_CAP, T)
    tq = min(512, T)
    tk = min(512, T)
    nk = T // tk
    cx, cy, cc = _place()
    near = [2 * (cx ^ cc) + (cy ^ (1 - cc)), 2 * (1 - cx) + (1 - cy), 2 * (cx ^ (1 - cc)) + (cy ^ cc)]
    slots = jnp.stack([2 * q + cc for q in near] + near).astype(jnp.int32)
    own = jnp.stack([4 * cx + 2 * cy + cc, 2 * cx + cy]).astype(jnp.int32)

    tr = jnp.transpose
    big_w = [tr(ffn1_w_gate[0]), tr(ffn1_w_up[0]), ffn1_w_down[0], tr(w_in[0]), w_out[0], tr(ffn2_w_gate[0]), tr(ffn2_w_up[0]),
             ffn2_w_down[0]]
    big_m = [tr(m_ffn1_w_gate[0]), tr(m_ffn1_w_up[0]), m_ffn1_w_down[0], tr(m_w_in[0]), m_w_out[0], tr(m_ffn2_w_gate[0]),
             tr(m_ffn2_w_up[0]), m_ffn2_w_down[0]]
    big_v = [tr(v_ffn1_w_gate[0]), tr(v_ffn1_w_up[0]), v_ffn1_w_down[0], tr(v_w_in[0]), v_w_out[0], tr(v_ffn2_w_gate[0]),
             tr(v_ffn2_w_up[0]), v_ffn2_w_down[0]]
    transposed = {"ffn1_w_gate", "ffn1_w_up", "w_in", "ffn2_w_gate", "ffn2_w_up"}
    names = ["ffn1_w_gate", "ffn1_w_up", "ffn1_w_down", "w_in", "w_out", "ffn2_w_gate", "ffn2_w_up", "ffn2_w_down"]
    sh = dict(zip(names, [w.astype(BF16) for w in big_w]))
    lane = jnp.arange(LANES)
    inv_freq = ROPE_THETA ** (-(2.0 * (lane % (HEAD_DIM // 2))).astype(F32) / HEAD_DIM)
    ang = positions[0].astype(F32)[:, None] * inv_freq[None, :]
    cos_t = jnp.cos(ang)
    sin_t = jnp.where((lane & (HEAD_DIM // 2)) == 0, -1.0, 1.0)[None, :] * jnp.sin(ang)
    rope = (cos_t, sin_t)

    def pair_gain(g, blocks):
        return jnp.tile(jnp.concatenate([g[0], g[0]])[None, None, :], (blocks, 1, 1))

    n1, (wg1,) = _rmsnorm_fwd("ffn1_norm", xs, norm_ffn1_g, tm, carry=_gather_carry([sh["ffn1_w_gate"]]))
    a1, (wu1,) = _ffn_gate("ffn1_gate", n1, wg1, tm, carry=_gather_carry([sh["ffn1_w_up"]]))
    (b1, hm1), (wd1,) = _ffn_up_only("ffn1_up", n1, wu1, a1, tm, carry=_gather_carry([sh["ffn1_w_down"]]))
    h1, (win_g,) = _ffn_down("ffn1_down", hm1, wd1, xs, tm, D, carry=_gather_carry([sh["w_in"]]))
    n_in = win_g.shape[1]
    win_t = win_g.reshape(N_DEV * n_in, D)
    win_f = jnp.pad(win_t[F_OFF:F_OFF + H], ((0, LANES - H), (0, 0)))
    tkb = MAIN // 9
    assert F_OFF % tkb == 0

    def main_row(r):
        return pl.multiple_of(r * tkb + H * (r >= F_OFF // tkb).astype(jnp.int32), min(H, tkb))

    u = _rmsnorm_fwd("mix_norm", h1, norm_mix_g, tm)
    (proj, proj_b), (wout_g,) = _mm("mix_proj", u, win_t, T, tkb, dims=NT, carry=_gather_carry([sh["w_out"]]),
                                    b_rows=(9, main_row), bf16_copy=True)
    wout = wout_g.reshape(D, D)
    proj_f = _mm("mix_proj_forget", u, win_f, T, LANES, dims=NT)
    scale = HEAD_DIM ** -0.5
    fox_gains = jnp.concatenate([pair_gain(fox_q_norm_g, HP), pair_gain(fox_k_norm_g, HP)])
    qk_f = _headnorm_fwd_scaled("fox_qk_norm", proj, 0, 2 * HP, fox_gains, T, scale, HP)
    v_f = proj_b[:, 2 * Dh:3 * Dh]
    c_t, sg_t = _forget_fwd("forget_gates", proj_f[:, :H].T, b_forget.reshape(H, 1))
    crow = c_t.reshape(H, nk, 1, tk)
    (o_fox, lse_fa, lse_fb), (wg2, wu2) = _fox_fwd("fox_attention", qk_f, v_f, crow, tq, tk, min(FOX_STRIP_FWD, tq),
                                                   carry=_gather_carry([sh["ffn2_w_gate"], sh["ffn2_w_up"]]))

    swa_q_gains = pair_gain(swa_q_norm_g, HP)
    swa_k_gains = pair_gain(swa_k_norm_g, KVB)
    q_s = _headnorm_fwd("swa_q_norm", proj, 3 * HP, HP, swa_q_gains, T, scale, rope=rope)
    k_d = _headnorm_fwd("swa_k_norm", proj, 4 * HP, KVB, swa_k_gains, T, 1.0, rope=rope, dup=True)
    v_s = proj_b[:, 4 * Dh + KVW:].reshape(T, H // GQA_GROUP, 1, HEAD_DIM)
    v_d = jnp.broadcast_to(v_s, (T, H // GQA_GROUP, 2, HEAD_DIM)).reshape(T, 2 * KVW)
    sinks3 = swa_sinks.reshape(H, 1, 1)
    o_swa, lse_sa, lse_sb = _swa_fwd("swa_attention", q_s, k_d, v_d, sinks3)

    on = _outnorm_fwd("out_norm", o_fox, o_swa, out_norm_fox_g, out_norm_swa_g, tm)
    h2 = _mm("mix_out", on, wout, tm2, min(512, D), resid=h1)

    n2 = _rmsnorm_fwd("ffn2_norm", h2, norm_ffn2_g, tm)
    (a2, b2, hm2), (wd2,) = _ffn_up("ffn2_up", n2, wg2, wu2, tm2, carry=_gather_carry([sh["ffn2_w_down"]]))
    y = _ffn_down("ffn2_down", hm2, wd2, h2, tm2, D)
    dy, dyh, sq = _loss_grad("loss_grad", y, target, tm)
    loss_part = 0.5 * sq[0, 0] / D

    J, Fs, _ = wg2.shape
    aspec = pl.BlockSpec((None, tm, Fs), lambda i, j: (j, i, 0))
    wspec = pl.BlockSpec((None, Fs, D), lambda i, j: (j, 0, 0))
    got = {}
    local = {}

    def pair_sums(keys, grads, received):
        for nm, g, r in zip(keys, grads, received):
            local[nm] = (g, r)
        return [_pair_add("sum_" + nm, g, r, slots) for nm, g, r in zip(keys, grads, received)]

    def relay_sums(keys, sums, hop1):
        out = []
        for i, (nm, s) in enumerate(zip(keys, sums)):
            got[nm] = [hop1[2 * i]]
            out.append(_relay_add("relay_" + nm, s, hop1[2 * i + 1]))
        return out

    def arrived(keys, hop2):
        for nm, blk in zip(keys, hop2):
            got[nm].append(blk)

    dwd2 = _wgrad_down("ffn2_wgrad_down", hm2, dyh, D)
    (da2, db2), (sib_d2,) = _ffn_bwd_mid("ffn2_bwd_mid", dyh, wd2, a2, b2, tm2, carry=_sibling_carry([dwd2]))
    (sum_wd2,) = pair_sums(names[7:8], [dwd2], [sib_d2])
    (dwg2, dwu2), hop1 = _wgrad_up("ffn2_wgrad_up", n2, da2, db2, min(1024, D), carry=_to_partner_carry([sum_wd2]))
    (t_wd2,) = relay_sums(names[7:8], [sum_wd2], hop1)
    aspec2 = pl.BlockSpec((None, tm2, Fs), lambda i, j: (j, i, 0))
    dn2, (via_wd2, *sib2) = _reduce_mm("ffn2_bwd_in", [(da2, aspec2, wg2, wspec), (db2, aspec2, wu2, wspec)], [], NN, T, D, tm2, J,
                                       carry=_join(_to_other_carry([t_wd2]), _sibling_carry([dwg2, dwu2])))
    arrived(names[7:8], [via_wd2])
    dh2, dg_ffn2, dh2b = _rmsnorm_bwd("ffn2_norm_bwd", dn2, h2, norm_ffn2_g, dy, min(256, T), 1.0)
    sum_wg2, sum_wu2 = pair_sums(names[5:7], [dwg2, dwu2], sib2)

    dwout = _wgrad_2d("mix_out_wgrad", on, dh2b, min(512, D), D)
    dwout_g = dwout.reshape(N_DEV, D // N_DEV, D)
    do_fox, dg_of = _outnorm_bwd("out_norm_bwd_fox", dh2b, wout, 0, o_fox, out_norm_fox_g, tm)
    do_swa, dg_os = _outnorm_bwd("out_norm_bwd_swa", dh2b, wout, 1, o_swa, out_norm_swa_g, tm)

    (dq_f, dk_f, dproj, dc_a, dc_b, dr_a, dr_b), (*hop1, sib_wout) = _fox_bwd(
        "fox_attention_bwd", qk_f, v_f, o_fox, do_fox, crow, lse_fa, lse_fb, tq, tk, min(FOX_STRIP_BWD, tq), MAIN, 2 * HP,
        carry=_join(_to_partner_carry([sum_wg2, sum_wu2]), _sibling_carry([dwout_g])))
    t_wg2, t_wu2 = relay_sums(names[5:7], [sum_wg2, sum_wu2], hop1)
    (sum_wout,) = pair_sums(names[4:5], [dwout_g], [sib_wout])
    dproj, dg_fq = _headnorm_bwd("fox_q_norm_bwd", dq_f, proj, 0, HP, fox_gains[:HP], HP, T, 1.0, into=(dproj, 0))
    dproj, dg_fk = _headnorm_bwd("fox_k_norm_bwd", dk_f, proj, HP, HP, fox_gains[HP:], HP, T, 1.0, into=(dproj, HP))
    dct = jnp.stack([dc_a.reshape(HP, T), dc_b.reshape(HP, T)], axis=1).reshape(H, T)
    drt = jnp.stack([dr_a.reshape(HP, T), dr_b.reshape(HP, T)], axis=1).reshape(H, T)
    dz_t, db_f = _forget_bwd("forget_gates_bwd", dct, drt, sg_t)

    (dq_s, dk_p, dv_p, dsink_a, dsink_b), hop2 = _swa_bwd(
        "swa_attention_bwd", q_s, k_d, v_d, sinks3, o_swa, do_swa, lse_sa, lse_sb, carry=_to_other_carry([t_wg2, t_wu2]))
    arrived(names[5:7], hop2)
    dproj, dg_sq = _headnorm_bwd("swa_q_norm_bwd", dq_s, proj, 3 * HP, HP, swa_q_gains, HP, T, 1.0, rope=rope, into=(dproj, 3 * HP))
    dproj, dg_sk = _headnorm_bwd("swa_k_norm_bwd", dk_p, proj, 4 * HP, KVB, swa_k_gains, KVB, T, 1.0, rope=rope, fold=True,
                                 into=(dproj, 4 * HP))
    dproj, _ = _headnorm_bwd("swa_v_fold", dv_p, None, 0, KVB, None, KVB, T, 1.0, fold=True, norm=False, into=(dproj, 4 * HP + KVB))
    dproj_f = jnp.pad(dz_t.T, ((0, 0), (0, LANES - H))).astype(BF16)
    dwin_t, hop1 = _wgrad_2d("mix_proj_wgrad", dproj, u, tkb, D, carry=_to_partner_carry([sum_wout]),
                             out_rows=(N_DEV * n_in, main_row))
    (t_wout,) = relay_sums(names[4:5], [sum_wout], hop1)
    dwin_f = _wgrad_2d("mix_proj_forget_wgrad", dproj_f, u, LANES, min(1024, D))
    dwin_t = lax.dynamic_update_slice(dwin_t, dwin_f[:H], (F_OFF, 0))
    dwin_g = dwin_t.reshape(N_DEV, n_in, D)
    du, (via_wout, sib_win) = _reduce_mm(
        "mix_bwd_in",
        [(dproj, pl.BlockSpec((tm2, tkb), lambda i, r: (i, r)), win_t, pl.BlockSpec((pl.Element(tkb), pl.Element(D)), lambda i, r: (main_row(r), 0)))],
        [(dproj_f, pl.BlockSpec((tm2, LANES), lambda i, r: (i, 0)), win_f, pl.BlockSpec((LANES, D), lambda i, r: (0, 0)))],
        NN, T, D, tm2, 9, carry=_join(_to_other_carry([t_wout]), _sibling_carry([dwin_g])))
    arrived(names[4:5], [via_wout])
    dh1, dg_mix, dh1h = _rmsnorm_bwd("mix_norm_bwd", du, h1, norm_mix_g, dh2, min(256, T), 0.5)
    (sum_win,) = pair_sums(names[3:4], [dwin_g], [sib_win])

    dwd1, hop1 = _wgrad_down("ffn1_wgrad_down", hm1, dh1h, D, carry=_to_partner_carry([sum_win]))
    (t_win,) = relay_sums(names[3:4], [sum_win], hop1)
    (da1, db1), (via_win, sib_d) = _ffn_bwd_mid("ffn1_bwd_mid", dh1h, wd1, a1, b1, tm2,
                                                carry=_join(_to_other_carry([t_win]), _sibling_carry([dwd1])))
    arrived(names[3:4], [via_win])
    (sum_wd1,) = pair_sums(names[2:3], [dwd1], [sib_d])
    dwg1, hop1 = _wgrad_down("ffn1_wgrad_gate", da1, n1, D, carry=_to_partner_carry([sum_wd1]))
    (t_wd1,) = relay_sums(names[2:3], [sum_wd1], hop1)
    dwu1, (via_wd1, sib_g) = _wgrad_down("ffn1_wgrad_up", db1, n1, D,
                                         carry=_join(_to_other_carry([t_wd1]), _sibling_carry([dwg1])))
    arrived(names[2:3], [via_wd1])
    (sum_wg1,) = pair_sums(names[0:1], [dwg1], [sib_g])
    dn1_gate, (*hop1, sib_u) = _reduce_mm(
        "ffn1_bwd_in_gate", [(da1, aspec2, wg1, wspec)], [], NN, T, D, tm2, J,
        carry=_join(_to_partner_carry([sum_wg1]), _sibling_carry([dwu1])))
    (t_wg1,) = relay_sums(names[0:1], [sum_wg1], hop1)
    (sum_wu1,) = pair_sums(names[1:2], [dwu1], [sib_u])
    dn1, (via_wg1, *hop1) = _reduce_mm(
        "ffn1_bwd_in_up", [(db1, aspec2, wu1, wspec)], [], NN, T, D, tm2, J, init=dn1_gate,
        carry=_join(_to_other_carry([t_wg1]), _to_partner_carry([sum_wu1])))
    arrived(names[0:1], [via_wg1])
    (t_wu1,) = relay_sums(names[1:2], [sum_wu1], hop1)
    dx, dg_ffn1 = _rmsnorm_bwd("ffn1_norm_bwd", dn1, xs, norm_ffn1_g, dh1, min(256, T), None)

    dsinks = jnp.stack([dsink_a.reshape(HP), dsink_b.reshape(HP)], axis=1).reshape(H)
    small_g = [dg_ffn1, dg_mix, dg_ffn2, dg_of, dg_os, db_f, dg_fq[0, 0, :HEAD_DIM], dg_fk[0, 0, :HEAD_DIM],
               dg_sq[0, 0, :HEAD_DIM], dg_sk[0, 0, :HEAD_DIM], dsinks]
    via_wu1, gathered = _run_carry(
        "grads_exchange", _join(_to_other_carry([t_wu1]), _gather_small_carry(_pack_small(small_g, D, loss_part))))
    arrived(names[1:2], [via_wu1])

    big_out = [_adam_shard("adam_" + nm, local[nm][0], local[nm][1], got[nm], w, m, v, own)
               for nm, w, m, v in zip(names, big_w, big_m, big_v)]

    small_w = [norm_ffn1_g, norm_mix_g, norm_ffn2_g, out_norm_fox_g, out_norm_swa_g, b_forget, fox_q_norm_g, fox_k_norm_g,
               swa_q_norm_g, swa_k_norm_g, swa_sinks]
    small_m = [m_norm_ffn1_g, m_norm_mix_g, m_norm_ffn2_g, m_out_norm_fox_g, m_out_norm_swa_g, m_b_forget, m_fox_q_norm_g,
               m_fox_k_norm_g, m_swa_q_norm_g, m_swa_k_norm_g, m_swa_sinks]
    small_v = [v_norm_ffn1_g, v_norm_mix_g, v_norm_ffn2_g, v_out_norm_fox_g, v_out_norm_swa_g, v_b_forget, v_fox_q_norm_g,
               v_fox_k_norm_g, v_swa_q_norm_g, v_swa_k_norm_g, v_swa_sinks]
    small_out = _adam_small("adam_small", gathered, _pack_small(small_w, D), _pack_small(small_m, D), _pack_small(small_v, D))
    loss = small_out[0][5, 0]
    small_out = [_unpack_small(p, D, H) for p in small_out]

    order = ["norm_ffn1_g", "ffn1_w_gate", "ffn1_w_up", "ffn1_w_down", "norm_mix_g", "w_in", "b_forget", "fox_q_norm_g", "fox_k_norm_g",
             "swa_q_norm_g", "swa_k_norm_g", "swa_sinks", "out_norm_fox_g", "out_norm_swa_g", "w_out", "norm_ffn2_g",
             "ffn2_w_gate", "ffn2_w_up", "ffn2_w_down"]
    small_names = ["norm_ffn1_g", "norm_mix_g", "norm_ffn2_g", "out_norm_fox_g", "out_norm_swa_g", "b_forget", "fox_q_norm_g",
                   "fox_k_norm_g", "swa_q_norm_g", "swa_k_norm_g", "swa_sinks"]
    result = [loss, dx[None]]
    for kind in range(4):
        for nm in order:
            if nm in names:
                leaf = big_out[names.index(nm)][kind]
                result.append((tr(leaf) if nm in transposed else leaf)[None])
            else:
                result.append(small_out[kind][small_names.index(nm)])
    return tuple(result)


def _headnorm_fwd_scaled(name, proj, col_off, ncb, gains, tm, scale, n_scaled):
    T = proj.shape[0]

    def body(x_ref, g_ref, o_ref):
        xv = x_ref[...]
        lo = _lane_lo(xv.shape)
        y = xv * _head_rstd(xv, lo) * g_ref[...]
        y = y * jnp.where(pl.program_id(0) < n_scaled, scale, 1.0)
        o_ref[...] = y.astype(BF16)

    return pl.pallas_call(
        body, out_shape=jax.ShapeDtypeStruct((T, ncb * LANES), BF16), grid=(ncb, T // tm),
        in_specs=[pl.BlockSpec((tm, LANES), lambda c, i: (i, col_off + c)), pl.BlockSpec((None, 1, LANES), lambda c, i: (c, 0, 0))],
        out_specs=pl.BlockSpec((tm, LANES), lambda c, i: (i, c)), name=name, compiler_params=_params(2))(proj, gains)
```
